```python
import math
import jax, jax.numpy as jnp
from jax import lax
import numpy as np

D_MODEL = 1024
BATCH = 8
SEQ = 4096
DEPTH = 1

D_MIX = D_MODEL
D_SSM = D_MIX // 2
D_CONV = D_MIX - D_SSM
SSM_GROUP = 16
N_SSM_GROUPS = D_SSM // SSM_GROUP
SSM_STATE = 64
CONV_HEADS = 8
CONV_WIDTH = 3
D_FF = ((8 * D_MODEL // 3 + 127) // 128) * 128
FFN_CONV_WIDTH = 3
N_MOD = 6
D_IN_PROJ = D_SSM + 3 * D_CONV
EPS = 1e-6
DT_MIN = 1e-3
DT_MAX = 1e-1
LAMBDA_RE_MAX = -1e-4

kernel_name = 'hymba_s5_shortconv_convffn_adaln'


def rms_norm(x, g):
    xf = x.astype(jnp.float32)
    y = xf * lax.rsqrt(jnp.mean(xf * xf, axis=-1, keepdims=True) + EPS)
    return (y * g.astype(jnp.float32)).astype(x.dtype)


def head_rms_norm(y, g, n_heads):
    shp = y.shape
    yf = y.astype(jnp.float32).reshape(shp[:-1] + (n_heads, shp[-1] // n_heads))
    yf = yf * lax.rsqrt(jnp.mean(yf * yf, axis=-1, keepdims=True) + EPS)
    return (yf.reshape(shp) * g.astype(jnp.float32)).astype(y.dtype)


def causal_dwconv(x, w):
    k_w = w.shape[0]
    seq = x.shape[1]
    xp = jnp.pad(x, ((0, 0), (k_w - 1, 0), (0, 0)))
    y = xp[:, 0:seq, :] * w[0]
    for k in range(1, k_w):
        y = y + xp[:, k:k + seq, :] * w[k]
    return y


def _s5_binop(e1, e2):
    a1, b1 = e1
    a2, b2 = e2
    return a2 * a1, a2 * b1 + b2


def s5_mixer(u, lam_re, lam_im, log_step, b_re, b_im, c_re, c_im, d_skip, glu_w, glu_b):
    bsz, seq, _ = u.shape
    uf = u.astype(jnp.float32)
    ug = uf.reshape(bsz, seq, N_SSM_GROUPS, SSM_GROUP)
    lam = lax.complex(jnp.minimum(lam_re.astype(jnp.float32), LAMBDA_RE_MAX),
                      lam_im.astype(jnp.float32))
    step = jnp.exp(log_step.astype(jnp.float32))[:, None]
    lam_bar = jnp.exp(lam * step)
    b_c = lax.complex(b_re.astype(jnp.float32), b_im.astype(jnp.float32))
    b_bar = ((lam_bar - 1.0) / lam)[..., None] * b_c
    bu = jnp.einsum('blgh,gph->blgp', ug.astype(jnp.complex64), b_bar)
    a = jnp.broadcast_to(lam_bar, (1, seq) + lam_bar.shape)
    _, states = lax.associative_scan(_s5_binop, (a, bu), axis=1)
    c_c = lax.complex(c_re.astype(jnp.float32), c_im.astype(jnp.float32))
    y = jnp.einsum('blgp,ghp->blgh', states, c_c).real.reshape(bsz, seq, D_SSM)
    y = y + d_skip.astype(jnp.float32) * uf
    z = jax.nn.gelu(y)
    z = z * jax.nn.sigmoid(z @ glu_w.astype(jnp.float32) + glu_b.astype(jnp.float32))
    return z.astype(u.dtype)


def short_conv_mixer(bg, cg, v, conv_w):
    return bg * causal_dwconv(cg * v, conv_w)


def conv_ffn(h, w_up, ffn_conv_w, w_down):
    hid = causal_dwconv(h @ w_up, ffn_conv_w)
    a, v = jnp.split(hid, 2, axis=-1)
    return (jax.nn.silu(a) * v) @ w_down


def _fwd_setup_inputs(seed: int = 0) -> dict:
    key = jax.random.key(seed)
    ks = jax.random.split(key, 26)
    f32 = jnp.float32

    def nrm(k, shape, s):
        return jax.random.normal(k, shape, f32) * s

    nl = DEPTH
    g_, p_, h_ = N_SSM_GROUPS, SSM_STATE, SSM_GROUP
    n_idx = jnp.arange(SSM_STATE, dtype=f32)
    return {
        'x': nrm(ks[0], (BATCH, SEQ, D_MODEL), 1.0),
        'c': nrm(ks[1], (BATCH, D_MODEL), 1.0),
        'w_ada': nrm(ks[2], (nl, D_MODEL, N_MOD * D_MODEL), 0.5 * D_MODEL ** -0.5),
        'b_ada': nrm(ks[3], (nl, N_MOD * D_MODEL), 0.02),
        'g_pre_mix': 1.0 + nrm(ks[4], (nl, D_MODEL), 0.02),
        'g_post_mix': 1.0 + nrm(ks[5], (nl, D_MODEL), 0.02),
        'w_in': nrm(ks[6], (nl, D_MODEL, D_IN_PROJ), D_MODEL ** -0.5),
        'ssm_lam_re': -0.5 + nrm(ks[7], (nl, g_, p_), 0.01),
        'ssm_lam_im': math.pi * n_idx + nrm(ks[8], (nl, g_, p_), 0.01),
        'ssm_log_step': jax.random.uniform(ks[9], (nl, g_), f32, math.log(DT_MIN), math.log(DT_MAX)),
        'ssm_b_re': nrm(ks[10], (nl, g_, p_, h_), (2 * h_) ** -0.5),
        'ssm_b_im': nrm(ks[11], (nl, g_, p_, h_), (2 * h_) ** -0.5),
        'ssm_c_re': nrm(ks[12], (nl, g_, h_, p_), p_ ** -0.5),
        'ssm_c_im': nrm(ks[13], (nl, g_, h_, p_), p_ ** -0.5),
        'ssm_d': nrm(ks[14], (nl, D_SSM), 1.0),
        'glu_w': nrm(ks[15], (nl, D_SSM, D_SSM), D_SSM ** -0.5),
        'glu_b': nrm(ks[16], (nl, D_SSM), 0.02),
        'g_out_ssm': 1.0 + nrm(ks[17], (nl, D_SSM), 0.02),
        'conv_w': nrm(ks[18], (nl, CONV_WIDTH, D_CONV), CONV_WIDTH ** -0.5),
        'g_out_conv': 1.0 + nrm(ks[19], (nl, D_CONV), 0.02),
        'w_out': nrm(ks[20], (nl, D_MIX, D_MODEL), D_MIX ** -0.5),
        'g_pre_ffn': 1.0 + nrm(ks[21], (nl, D_MODEL), 0.02),
        'g_post_ffn': 1.0 + nrm(ks[22], (nl, D_MODEL), 0.02),
        'w_up': nrm(ks[23], (nl, D_MODEL, 2 * D_FF), D_MODEL ** -0.5),
        'ffn_conv_w': nrm(ks[24], (nl, FFN_CONV_WIDTH, 2 * D_FF), FFN_CONV_WIDTH ** -0.5),
        'w_down': nrm(ks[25], (nl, D_FF, D_MODEL), D_FF ** -0.5),
    }


def _fwd_reference(x, c, w_ada, b_ada, g_pre_mix, g_post_mix, w_in, ssm_lam_re, ssm_lam_im, ssm_log_step,
              ssm_b_re, ssm_b_im, ssm_c_re, ssm_c_im, ssm_d, glu_w, glu_b, g_out_ssm, conv_w, g_out_conv,
              w_out, g_pre_ffn, g_post_ffn, w_up, ffn_conv_w, w_down):
    c_act = jax.nn.silu(c)
    for i in range(DEPTH):
        mod = (c_act @ w_ada[i] + b_ada[i])[:, None, :]
        sh1, sc1, gt1, sh2, sc2, gt2 = jnp.split(mod, N_MOD, axis=-1)

        h = rms_norm(x, g_pre_mix[i]) * (1.0 + sc1) + sh1
        proj = h @ w_in[i]
        u = proj[..., :D_SSM]
        bg, cg, v = jnp.split(proj[..., D_SSM:], 3, axis=-1)
        y_a = s5_mixer(u, ssm_lam_re[i], ssm_lam_im[i], ssm_log_step[i], ssm_b_re[i], ssm_b_im[i],
                       ssm_c_re[i], ssm_c_im[i], ssm_d[i], glu_w[i], glu_b[i])
        y_b = short_conv_mixer(bg, cg, v, conv_w[i])
        y = jnp.concatenate([head_rms_norm(y_a, g_out_ssm[i], N_SSM_GROUPS),
                             head_rms_norm(y_b, g_out_conv[i], CONV_HEADS)], axis=-1)
        x = x + gt1 * rms_norm(y @ w_out[i], g_post_mix[i])

        h = rms_norm(x, g_pre_ffn[i]) * (1.0 + sc2) + sh2
        x = x + gt2 * rms_norm(conv_ffn(h, w_up[i], ffn_conv_w[i], w_down[i]), g_post_ffn[i])
    return x


import jax as _jax
import jax.numpy as _jnp

TWIN_FORMAT = 'train_step'
FWD_PARAMS = ['x', 'c', 'w_ada', 'b_ada', 'g_pre_mix', 'g_post_mix', 'w_in', 'ssm_lam_re', 'ssm_lam_im', 'ssm_log_step', 'ssm_b_re', 'ssm_b_im', 'ssm_c_re', 'ssm_c_im', 'ssm_d', 'glu_w', 'glu_b', 'g_out_ssm', 'conv_w', 'g_out_conv', 'w_out', 'g_pre_ffn', 'g_post_ffn', 'w_up', 'ffn_conv_w', 'w_down']
TWIN_WEIGHTS = ['w_ada', 'b_ada', 'g_pre_mix', 'g_post_mix', 'w_in', 'ssm_lam_re', 'ssm_lam_im', 'ssm_log_step', 'ssm_b_re', 'ssm_b_im', 'ssm_c_re', 'ssm_c_im', 'ssm_d', 'glu_w', 'glu_b', 'g_out_ssm', 'conv_w', 'g_out_conv', 'w_out', 'g_pre_ffn', 'g_post_ffn', 'w_up', 'ffn_conv_w', 'w_down']
TWIN_DIFF_INPUT = 'x'
TWIN_INPUTS = ['x', 'c', 'w_ada', 'b_ada', 'g_pre_mix', 'g_post_mix', 'w_in', 'ssm_lam_re', 'ssm_lam_im', 'ssm_log_step', 'ssm_b_re', 'ssm_b_im', 'ssm_c_re', 'ssm_c_im', 'ssm_d', 'glu_w', 'glu_b', 'g_out_ssm', 'conv_w', 'g_out_conv', 'w_out', 'g_pre_ffn', 'g_post_ffn', 'w_up', 'ffn_conv_w', 'w_down', 'loss_target', 'm_w_ada', 'm_b_ada', 'm_g_pre_mix', 'm_g_post_mix', 'm_w_in', 'm_ssm_lam_re', 'm_ssm_lam_im', 'm_ssm_log_step', 'm_ssm_b_re', 'm_ssm_b_im', 'm_ssm_c_re', 'm_ssm_c_im', 'm_ssm_d', 'm_glu_w', 'm_glu_b', 'm_g_out_ssm', 'm_conv_w', 'm_g_out_conv', 'm_w_out', 'm_g_pre_ffn', 'm_g_post_ffn', 'm_w_up', 'm_ffn_conv_w', 'm_w_down', 'v_w_ada', 'v_b_ada', 'v_g_pre_mix', 'v_g_post_mix', 'v_w_in', 'v_ssm_lam_re', 'v_ssm_lam_im', 'v_ssm_log_step', 'v_ssm_b_re', 'v_ssm_b_im', 'v_ssm_c_re', 'v_ssm_c_im', 'v_ssm_d', 'v_glu_w', 'v_glu_b', 'v_g_out_ssm', 'v_conv_w', 'v_g_out_conv', 'v_w_out', 'v_g_pre_ffn', 'v_g_post_ffn', 'v_w_up', 'v_ffn_conv_w', 'v_w_down']
TWIN_OUTPUTS = ['loss', 'grad_x', 'grad_w_ada', 'grad_b_ada', 'grad_g_pre_mix', 'grad_g_post_mix', 'grad_w_in', 'grad_ssm_lam_re', 'grad_ssm_lam_im', 'grad_ssm_log_step', 'grad_ssm_b_re', 'grad_ssm_b_im', 'grad_ssm_c_re', 'grad_ssm_c_im', 'grad_ssm_d', 'grad_glu_w', 'grad_glu_b', 'grad_g_out_ssm', 'grad_conv_w', 'grad_g_out_conv', 'grad_w_out', 'grad_g_pre_ffn', 'grad_g_post_ffn', 'grad_w_up', 'grad_ffn_conv_w', 'grad_w_down', 'delta_w_ada', 'delta_b_ada', 'delta_g_pre_mix', 'delta_g_post_mix', 'delta_w_in', 'delta_ssm_lam_re', 'delta_ssm_lam_im', 'delta_ssm_log_step', 'delta_ssm_b_re', 'delta_ssm_b_im', 'delta_ssm_c_re', 'delta_ssm_c_im', 'delta_ssm_d', 'delta_glu_w', 'delta_glu_b', 'delta_g_out_ssm', 'delta_conv_w', 'delta_g_out_conv', 'delta_w_out', 'delta_g_pre_ffn', 'delta_g_post_ffn', 'delta_w_up', 'delta_ffn_conv_w', 'delta_w_down', 'new_m_w_ada', 'new_m_b_ada', 'new_m_g_pre_mix', 'new_m_g_post_mix', 'new_m_w_in', 'new_m_ssm_lam_re', 'new_m_ssm_lam_im', 'new_m_ssm_log_step', 'new_m_ssm_b_re', 'new_m_ssm_b_im', 'new_m_ssm_c_re', 'new_m_ssm_c_im', 'new_m_ssm_d', 'new_m_glu_w', 'new_m_glu_b', 'new_m_g_out_ssm', 'new_m_conv_w', 'new_m_g_out_conv', 'new_m_w_out', 'new_m_g_pre_ffn', 'new_m_g_post_ffn', 'new_m_w_up', 'new_m_ffn_conv_w', 'new_m_w_down', 'new_v_w_ada', 'new_v_b_ada', 'new_v_g_pre_mix', 'new_v_g_post_mix', 'new_v_w_in', 'new_v_ssm_lam_re', 'new_v_ssm_lam_im', 'new_v_ssm_log_step', 'new_v_ssm_b_re', 'new_v_ssm_b_im', 'new_v_ssm_c_re', 'new_v_ssm_c_im', 'new_v_ssm_d', 'new_v_glu_w', 'new_v_glu_b', 'new_v_g_out_ssm', 'new_v_conv_w', 'new_v_g_out_conv', 'new_v_w_out', 'new_v_g_pre_ffn', 'new_v_g_post_ffn', 'new_v_w_up', 'new_v_ffn_conv_w', 'new_v_w_down']
TWIN_LEAF_KINDS = {'loss': 'loss', 'grad_x': 'grad_x', 'grad_w_ada': 'grad_w', 'grad_b_ada': 'grad_w', 'grad_g_pre_mix': 'grad_w', 'grad_g_post_mix': 'grad_w', 'grad_w_in': 'grad_w', 'grad_ssm_lam_re': 'grad_w', 'grad_ssm_lam_im': 'grad_w', 'grad_ssm_log_step': 'grad_w', 'grad_ssm_b_re': 'grad_w', 'grad_ssm_b_im': 'grad_w', 'grad_ssm_c_re': 'grad_w', 'grad_ssm_c_im': 'grad_w', 'grad_ssm_d': 'grad_w', 'grad_glu_w': 'grad_w', 'grad_glu_b': 'grad_w', 'grad_g_out_ssm': 'grad_w', 'grad_conv_w': 'grad_w', 'grad_g_out_conv': 'grad_w', 'grad_w_out': 'grad_w', 'grad_g_pre_ffn': 'grad_w', 'grad_g_post_ffn': 'grad_w', 'grad_w_up': 'grad_w', 'grad_ffn_conv_w': 'grad_w', 'grad_w_down': 'grad_w', 'delta_w_ada': 'delta_w', 'delta_b_ada': 'delta_w', 'delta_g_pre_mix': 'delta_w', 'delta_g_post_mix': 'delta_w', 'delta_w_in': 'delta_w', 'delta_ssm_lam_re': 'delta_w', 'delta_ssm_lam_im': 'delta_w', 'delta_ssm_log_step': 'delta_w', 'delta_ssm_b_re': 'delta_w', 'delta_ssm_b_im': 'delta_w', 'delta_ssm_c_re': 'delta_w', 'delta_ssm_c_im': 'delta_w', 'delta_ssm_d': 'delta_w', 'delta_glu_w': 'delta_w', 'delta_glu_b': 'delta_w', 'delta_g_out_ssm': 'delta_w', 'delta_conv_w': 'delta_w', 'delta_g_out_conv': 'delta_w', 'delta_w_out': 'delta_w', 'delta_g_pre_ffn': 'delta_w', 'delta_g_post_ffn': 'delta_w', 'delta_w_up': 'delta_w', 'delta_ffn_conv_w': 'delta_w', 'delta_w_down': 'delta_w', 'new_m_w_ada': 'new_m', 'new_m_b_ada': 'new_m', 'new_m_g_pre_mix': 'new_m', 'new_m_g_post_mix': 'new_m', 'new_m_w_in': 'new_m', 'new_m_ssm_lam_re': 'new_m', 'new_m_ssm_lam_im': 'new_m', 'new_m_ssm_log_step': 'new_m', 'new_m_ssm_b_re': 'new_m', 'new_m_ssm_b_im': 'new_m', 'new_m_ssm_c_re': 'new_m', 'new_m_ssm_c_im': 'new_m', 'new_m_ssm_d': 'new_m', 'new_m_glu_w': 'new_m', 'new_m_glu_b': 'new_m', 'new_m_g_out_ssm': 'new_m', 'new_m_conv_w': 'new_m', 'new_m_g_out_conv': 'new_m', 'new_m_w_out': 'new_m', 'new_m_g_pre_ffn': 'new_m', 'new_m_g_post_ffn': 'new_m', 'new_m_w_up': 'new_m', 'new_m_ffn_conv_w': 'new_m', 'new_m_w_down': 'new_m', 'new_v_w_ada': 'new_v', 'new_v_b_ada': 'new_v', 'new_v_g_pre_mix': 'new_v', 'new_v_g_post_mix': 'new_v', 'new_v_w_in': 'new_v', 'new_v_ssm_lam_re': 'new_v', 'new_v_ssm_lam_im': 'new_v', 'new_v_ssm_log_step': 'new_v', 'new_v_ssm_b_re': 'new_v', 'new_v_ssm_b_im': 'new_v', 'new_v_ssm_c_re': 'new_v', 'new_v_ssm_c_im': 'new_v', 'new_v_ssm_d': 'new_v', 'new_v_glu_w': 'new_v', 'new_v_glu_b': 'new_v', 'new_v_g_out_ssm': 'new_v', 'new_v_conv_w': 'new_v', 'new_v_g_out_conv': 'new_v', 'new_v_w_out': 'new_v', 'new_v_g_pre_ffn': 'new_v', 'new_v_g_post_ffn': 'new_v', 'new_v_w_up': 'new_v', 'new_v_ffn_conv_w': 'new_v', 'new_v_w_down': 'new_v'}


def _forward(args):
    return _fwd_reference(*[args[k] for k in FWD_PARAMS])


def _output_shape():
    out = _jax.eval_shape(lambda: _forward(_fwd_setup_inputs(0)))
    return out.shape, out.dtype

N_MICROBATCH = 1
ADAM_LR = 0.001
ADAM_B1 = 0.9
ADAM_B2 = 0.999
ADAM_EPS = 1e-08
ADAM_WD = 0.01
ADAM_STEP = 10
PER_EXAMPLE_BATCH_AXIS = {'x': 0, 'c': 0, 'loss_target': 0}
SHARED_INPUTS = []
_WEIGHT_DTYPES = {'w_ada': _jnp.float32, 'b_ada': _jnp.float32, 'g_pre_mix': _jnp.float32, 'g_post_mix': _jnp.float32, 'w_in': _jnp.float32, 'ssm_lam_re': _jnp.float32, 'ssm_lam_im': _jnp.float32, 'ssm_log_step': _jnp.float32, 'ssm_b_re': _jnp.float32, 'ssm_b_im': _jnp.float32, 'ssm_c_re': _jnp.float32, 'ssm_c_im': _jnp.float32, 'ssm_d': _jnp.float32, 'glu_w': _jnp.float32, 'glu_b': _jnp.float32, 'g_out_ssm': _jnp.float32, 'conv_w': _jnp.float32, 'g_out_conv': _jnp.float32, 'w_out': _jnp.float32, 'g_pre_ffn': _jnp.float32, 'g_post_ffn': _jnp.float32, 'w_up': _jnp.float32, 'ffn_conv_w': _jnp.float32, 'w_down': _jnp.float32}
MOMENT_SCALE = {'w_ada': 1.744233e+00, 'b_ada': 3.348643e+00, 'g_pre_mix': 1.374504e-01, 'g_post_mix': 3.794543e+00, 'w_in': 1.162076e-01, 'ssm_lam_re': 1.792419e-02, 'ssm_lam_im': 5.359919e-02, 'ssm_log_step': 1.340711e+01, 'ssm_b_re': 1.776972e-02, 'ssm_b_im': 1.935799e-02, 'ssm_c_re': 2.509263e-02, 'ssm_c_im': 2.421238e-02, 'ssm_d': 3.024270e-01, 'glu_w': 3.558996e-02, 'glu_b': 7.843168e-02, 'g_out_ssm': 2.927810e-01, 'conv_w': 1.038153e-01, 'g_out_conv': 1.261992e-01, 'w_out': 2.130608e-01, 'g_pre_ffn': 1.201274e-01, 'g_post_ffn': 3.800720e+00, 'w_up': 5.877939e-02, 'ffn_conv_w': 6.611685e-02, 'w_down': 1.113059e-01}


def _to_microbatches(a, axis):
    t = _jnp.moveaxis(a, axis, 0)
    t = t.reshape((N_MICROBATCH, t.shape[0] // N_MICROBATCH) + t.shape[1:])
    return _jnp.moveaxis(t, 1, axis + 1)


def setup_inputs(seed: int = 0) -> dict:
    inp = _fwd_setup_inputs(seed)
    key = _jax.random.fold_in(_jax.random.key(seed), 7919)
    shape, _ = _output_shape()
    out = dict(inp)
    out["loss_target"] = _jax.random.normal(_jax.random.fold_in(key, 0), shape, _jnp.float32)
    for i, name in enumerate(TWIN_WEIGHTS):
        w = inp[name].astype(_jnp.float32)
        if MOMENT_SCALE is None:
            s = _jnp.sqrt(_jnp.mean(_jnp.square(w)) + 1e-30)
        else:
            s = MOMENT_SCALE[name]
        km, kv = _jax.random.split(_jax.random.fold_in(key, i + 1))
        out[name] = w
        out["m_" + name] = s * _jax.random.normal(km, w.shape, _jnp.float32)
        out["v_" + name] = (s * s) * _jax.random.uniform(kv, w.shape, _jnp.float32, 0.5, 1.5)
    if N_MICROBATCH > 1:
        for name, axis in PER_EXAMPLE_BATCH_AXIS.items():
            out[name] = _to_microbatches(out[name], axis)
    return {'x': out['x'], 'c': out['c'], 'w_ada': out['w_ada'], 'b_ada': out['b_ada'], 'g_pre_mix': out['g_pre_mix'], 'g_post_mix': out['g_post_mix'], 'w_in': out['w_in'], 'ssm_lam_re': out['ssm_lam_re'], 'ssm_lam_im': out['ssm_lam_im'], 'ssm_log_step': out['ssm_log_step'], 'ssm_b_re': out['ssm_b_re'], 'ssm_b_im': out['ssm_b_im'], 'ssm_c_re': out['ssm_c_re'], 'ssm_c_im': out['ssm_c_im'], 'ssm_d': out['ssm_d'], 'glu_w': out['glu_w'], 'glu_b': out['glu_b'], 'g_out_ssm': out['g_out_ssm'], 'conv_w': out['conv_w'], 'g_out_conv': out['g_out_conv'], 'w_out': out['w_out'], 'g_pre_ffn': out['g_pre_ffn'], 'g_post_ffn': out['g_post_ffn'], 'w_up': out['w_up'], 'ffn_conv_w': out['ffn_conv_w'], 'w_down': out['w_down'], 'loss_target': out['loss_target'], 'm_w_ada': out['m_w_ada'], 'm_b_ada': out['m_b_ada'], 'm_g_pre_mix': out['m_g_pre_mix'], 'm_g_post_mix': out['m_g_post_mix'], 'm_w_in': out['m_w_in'], 'm_ssm_lam_re': out['m_ssm_lam_re'], 'm_ssm_lam_im': out['m_ssm_lam_im'], 'm_ssm_log_step': out['m_ssm_log_step'], 'm_ssm_b_re': out['m_ssm_b_re'], 'm_ssm_b_im': out['m_ssm_b_im'], 'm_ssm_c_re': out['m_ssm_c_re'], 'm_ssm_c_im': out['m_ssm_c_im'], 'm_ssm_d': out['m_ssm_d'], 'm_glu_w': out['m_glu_w'], 'm_glu_b': out['m_glu_b'], 'm_g_out_ssm': out['m_g_out_ssm'], 'm_conv_w': out['m_conv_w'], 'm_g_out_conv': out['m_g_out_conv'], 'm_w_out': out['m_w_out'], 'm_g_pre_ffn': out['m_g_pre_ffn'], 'm_g_post_ffn': out['m_g_post_ffn'], 'm_w_up': out['m_w_up'], 'm_ffn_conv_w': out['m_ffn_conv_w'], 'm_w_down': out['m_w_down'], 'v_w_ada': out['v_w_ada'], 'v_b_ada': out['v_b_ada'], 'v_g_pre_mix': out['v_g_pre_mix'], 'v_g_post_mix': out['v_g_post_mix'], 'v_w_in': out['v_w_in'], 'v_ssm_lam_re': out['v_ssm_lam_re'], 'v_ssm_lam_im': out['v_ssm_lam_im'], 'v_ssm_log_step': out['v_ssm_log_step'], 'v_ssm_b_re': out['v_ssm_b_re'], 'v_ssm_b_im': out['v_ssm_b_im'], 'v_ssm_c_re': out['v_ssm_c_re'], 'v_ssm_c_im': out['v_ssm_c_im'], 'v_ssm_d': out['v_ssm_d'], 'v_glu_w': out['v_glu_w'], 'v_glu_b': out['v_glu_b'], 'v_g_out_ssm': out['v_g_out_ssm'], 'v_conv_w': out['v_conv_w'], 'v_g_out_conv': out['v_g_out_conv'], 'v_w_out': out['v_w_out'], 'v_g_pre_ffn': out['v_g_pre_ffn'], 'v_g_post_ffn': out['v_g_post_ffn'], 'v_w_up': out['v_w_up'], 'v_ffn_conv_w': out['v_ffn_conv_w'], 'v_w_down': out['v_w_down']}


def _loss(weights, diff, rest, loss_target):
    with _jax.named_scope("forward"):
        args = {**rest, TWIN_DIFF_INPUT: diff, **{k: w.astype(_WEIGHT_DTYPES[k]) for k, w in weights.items()}}
        y = _forward(args)
    with _jax.named_scope("loss_head"):
        err = _jnp.square(y.astype(_jnp.float32) - loss_target)
        return 0.5 * _jnp.sum(_jnp.mean(err, axis=-1)) if err.ndim else 0.5 * err


def _adamw(w, g, m, v):
    m = ADAM_B1 * m + (1.0 - ADAM_B1) * g
    v = ADAM_B2 * v + (1.0 - ADAM_B2) * _jnp.square(g)
    m_hat = m / (1.0 - ADAM_B1 ** ADAM_STEP)
    v_hat = v / (1.0 - ADAM_B2 ** ADAM_STEP)
    delta = -ADAM_LR * (m_hat / (_jnp.sqrt(v_hat) + ADAM_EPS) + ADAM_WD * w)
    return delta, m, v


def reference(x, c, w_ada, b_ada, g_pre_mix, g_post_mix, w_in, ssm_lam_re, ssm_lam_im, ssm_log_step, ssm_b_re, ssm_b_im, ssm_c_re, ssm_c_im, ssm_d, glu_w, glu_b, g_out_ssm, conv_w, g_out_conv, w_out, g_pre_ffn, g_post_ffn, w_up, ffn_conv_w, w_down, loss_target, m_w_ada, m_b_ada, m_g_pre_mix, m_g_post_mix, m_w_in, m_ssm_lam_re, m_ssm_lam_im, m_ssm_log_step, m_ssm_b_re, m_ssm_b_im, m_ssm_c_re, m_ssm_c_im, m_ssm_d, m_glu_w, m_glu_b, m_g_out_ssm, m_conv_w, m_g_out_conv, m_w_out, m_g_pre_ffn, m_g_post_ffn, m_w_up, m_ffn_conv_w, m_w_down, v_w_ada, v_b_ada, v_g_pre_mix, v_g_post_mix, v_w_in, v_ssm_lam_re, v_ssm_lam_im, v_ssm_log_step, v_ssm_b_re, v_ssm_b_im, v_ssm_c_re, v_ssm_c_im, v_ssm_d, v_glu_w, v_glu_b, v_g_out_ssm, v_conv_w, v_g_out_conv, v_w_out, v_g_pre_ffn, v_g_post_ffn, v_w_up, v_ffn_conv_w, v_w_down):
    given = dict(x=x, c=c, w_ada=w_ada, b_ada=b_ada, g_pre_mix=g_pre_mix, g_post_mix=g_post_mix, w_in=w_in, ssm_lam_re=ssm_lam_re, ssm_lam_im=ssm_lam_im, ssm_log_step=ssm_log_step, ssm_b_re=ssm_b_re, ssm_b_im=ssm_b_im, ssm_c_re=ssm_c_re, ssm_c_im=ssm_c_im, ssm_d=ssm_d, glu_w=glu_w, glu_b=glu_b, g_out_ssm=g_out_ssm, conv_w=conv_w, g_out_conv=g_out_conv, w_out=w_out, g_pre_ffn=g_pre_ffn, g_post_ffn=g_post_ffn, w_up=w_up, ffn_conv_w=ffn_conv_w, w_down=w_down, loss_target=loss_target, m_w_ada=m_w_ada, m_b_ada=m_b_ada, m_g_pre_mix=m_g_pre_mix, m_g_post_mix=m_g_post_mix, m_w_in=m_w_in, m_ssm_lam_re=m_ssm_lam_re, m_ssm_lam_im=m_ssm_lam_im, m_ssm_log_step=m_ssm_log_step, m_ssm_b_re=m_ssm_b_re, m_ssm_b_im=m_ssm_b_im, m_ssm_c_re=m_ssm_c_re, m_ssm_c_im=m_ssm_c_im, m_ssm_d=m_ssm_d, m_glu_w=m_glu_w, m_glu_b=m_glu_b, m_g_out_ssm=m_g_out_ssm, m_conv_w=m_conv_w, m_g_out_conv=m_g_out_conv, m_w_out=m_w_out, m_g_pre_ffn=m_g_pre_ffn, m_g_post_ffn=m_g_post_ffn, m_w_up=m_w_up, m_ffn_conv_w=m_ffn_conv_w, m_w_down=m_w_down, v_w_ada=v_w_ada, v_b_ada=v_b_ada, v_g_pre_mix=v_g_pre_mix, v_g_post_mix=v_g_post_mix, v_w_in=v_w_in, v_ssm_lam_re=v_ssm_lam_re, v_ssm_lam_im=v_ssm_lam_im, v_ssm_log_step=v_ssm_log_step, v_ssm_b_re=v_ssm_b_re, v_ssm_b_im=v_ssm_b_im, v_ssm_c_re=v_ssm_c_re, v_ssm_c_im=v_ssm_c_im, v_ssm_d=v_ssm_d, v_glu_w=v_glu_w, v_glu_b=v_glu_b, v_g_out_ssm=v_g_out_ssm, v_conv_w=v_conv_w, v_g_out_conv=v_g_out_conv, v_w_out=v_w_out, v_g_pre_ffn=v_g_pre_ffn, v_g_post_ffn=v_g_post_ffn, v_w_up=v_w_up, v_ffn_conv_w=v_ffn_conv_w, v_w_down=v_w_down)
    weights = {n: given[n] for n in TWIN_WEIGHTS}
    shared = {n: given[n] for n in SHARED_INPUTS}
    per_example = {n: given[n] for n in ['x', 'c']}
    grad_fn = _jax.value_and_grad(_loss, argnums=(0, 1))

    def one_microbatch(ex, loss_target):
        ex = dict(ex)
        diff = ex.pop(TWIN_DIFF_INPUT)
        return grad_fn(weights, diff, {**shared, **ex}, loss_target)

    if N_MICROBATCH == 1:
        loss, (grad_w, grad_x) = one_microbatch(per_example, given["loss_target"])
    else:
        def body(carry, xs):
            loss_sum, grad_sum = carry
            l_k, (gw_k, gx_k) = one_microbatch(xs[0], xs[1])
            with _jax.named_scope("update"):
                return (loss_sum + l_k, _jax.tree.map(_jnp.add, grad_sum, gw_k)), gx_k

        init = (_jnp.zeros((), _jnp.float32), _jax.tree.map(_jnp.zeros_like, weights))
        (loss, grad_w), grad_x = _jax.lax.scan(body, init, (per_example, given["loss_target"]))
    with _jax.named_scope("update"):
        delta_w, new_m, new_v = {}, {}, {}
        for n in TWIN_WEIGHTS:
            delta_w[n], new_m[n], new_v[n] = _adamw(weights[n], grad_w[n], given["m_" + n], given["v_" + n])
    return (loss, grad_x, *[grad_w[n] for n in TWIN_WEIGHTS], *[delta_w[n] for n in TWIN_WEIGHTS],
            *[new_m[n] for n in TWIN_WEIGHTS], *[new_v[n] for n in TWIN_WEIGHTS])
```

```python
import functools
import math

import jax
import jax.numpy as jnp
from jax import lax
from jax.experimental import pallas as pl
from jax.experimental.pallas import tpu as pltpu

F32 = jnp.float32
BF16 = jnp.bfloat16
MESH = pl.DeviceIdType.MESH

EPS = 1e-6
LAMBDA_RE_MAX = -1e-4
ADAM_LR = 0.001
ADAM_B1 = 0.9
ADAM_B2 = 0.999
ADAM_EPS = 1e-08
ADAM_WD = 0.01
ADAM_STEP = 10

SUBLANES = 8
N_CHIPS = 4
N_DEV = 8
VMEM_BIG = 56 * 1024 * 1024
VMEM_MID = 40 * 1024 * 1024

TB_MIX = 256
TB_FFN = 128
TB_SCAN = 256
W_SCAN = 256
TB_TN = 512


def _cparams(sem=None, vmem=None):
    kw = {}
    if sem is not None:
        kw["dimension_semantics"] = sem
    if vmem is not None:
        kw["vmem_limit_bytes"] = vmem
    return pltpu.CompilerParams(**kw)


def _blk(t, pref):
    return pref if t % pref == 0 else t


def _dot(a, b):
    return jnp.dot(a.astype(BF16), b.astype(BF16), preferred_element_type=F32)


def _dot_nt(a, b):
    return lax.dot_general(a.astype(BF16), b.astype(BF16), (((1,), (1,)), ((), ())),
                           preferred_element_type=F32)


def _dot_tn(a, b):
    return lax.dot_general(a.astype(BF16), b.astype(BF16), (((0,), (0,)), ((), ())),
                           preferred_element_type=F32)


def _sigmoid(x):
    return 1.0 / (1.0 + jnp.exp(-x))


_GELU_K = math.sqrt(2.0 / math.pi)
_GELU_C = 0.044715


def _gelu(x):
    th = jnp.tanh(_GELU_K * (x + _GELU_C * x * x * x))
    return 0.5 * x * (1.0 + th)


def _gelu_grad(x):
    x2 = x * x
    th = jnp.tanh(_GELU_K * (x + _GELU_C * x2 * x))
    return 0.5 * (1.0 + th) + 0.5 * x * (1.0 - th * th) * _GELU_K * (1.0 + 3.0 * _GELU_C * x2)


def _rowmean(x):
    return jnp.mean(x, axis=-1, keepdims=True)


def _colsum(x):
    return jnp.sum(x, axis=0, keepdims=True)


def _split_dot(x, m):
    hi = x.astype(BF16)
    lo = (x - hi.astype(F32)).astype(BF16)
    return (jnp.dot(hi, m, preferred_element_type=F32) + jnp.dot(lo, m, preferred_element_type=F32))


def _split3_dot(x, m):
    hi = x.astype(BF16)
    r1 = x - hi.astype(F32)
    mid = r1.astype(BF16)
    lo = (r1 - mid.astype(F32)).astype(BF16)
    return (jnp.dot(hi, m, preferred_element_type=F32) + jnp.dot(mid, m, preferred_element_type=F32)
            + jnp.dot(lo, m, preferred_element_type=F32))


def _shift_down(x, halo, k):
    r = pltpu.roll(x, k, 0)
    row = lax.broadcasted_iota(jnp.int32, x.shape, 0)
    for j in range(k):
        r = jnp.where(row == j, halo[SUBLANES - k + j:SUBLANES - k + j + 1, :], r)
    return r


def _shift_up(x, halo, k):
    n = x.shape[0]
    r = pltpu.roll(x, n - k, 0)
    row = lax.broadcasted_iota(jnp.int32, x.shape, 0)
    for j in range(k):
        r = jnp.where(row == n - k + j, halo[j:j + 1, :], r)
    return r


def _acc_rows(ref, first, rows):
    @pl.when(first)
    def _():
        ref[...] = jnp.zeros(ref.shape, ref.dtype)
    for j, r in enumerate(rows):
        ref[j:j + 1, :] += r


def _rows(tb, c, col=0):
    return pl.BlockSpec((tb, c), lambda i, col=col: (i, col))


def _full(shape):
    nd = len(shape)
    return pl.BlockSpec(shape, lambda i, nd=nd: (0,) * nd)


def _halo_prev(tb, c, col=0):
    per = tb // SUBLANES
    return pl.BlockSpec((SUBLANES, c), lambda i, col=col: (jnp.maximum(i * per - 1, 0), col))


def _halo_next(tb, c, t, col=0):
    per = tb // SUBLANES
    last = t // SUBLANES - 1
    return pl.BlockSpec((SUBLANES, c), lambda i, col=col: (jnp.minimum((i + 1) * per, last), col))


def _mesh_pos():
    return lax.axis_index("x"), lax.axis_index("y"), lax.axis_index("c")


def _allgather8(x_pad, n_sum, name):
    m_per, n = x_pad.shape

    def body(x_ref, out_ref, sum_ref, send_sems, recv_sems, local_sem):
        x, y, c = _mesh_pos()
        me, sibling = (x, y, c), (x, y, 1 - c)
        chips = [(1 - x, y), (x, 1 - y), (1 - x, 1 - y)]

        def rows(px, py, pc):
            return out_ref.at[pl.ds((4 * px + 2 * py + pc) * m_per, m_per), :]

        def copy(k, block, to, src=None):
            return pltpu.make_async_remote_copy(
                src_ref=rows(*block) if src is None else src, dst_ref=rows(*block),
                send_sem=send_sems.at[k], recv_sem=recv_sems.at[k], device_id=to, device_id_type=MESH)

        mine = pltpu.make_async_copy(x_ref, rows(*me), local_sem)
        mine.start()
        first = [copy(0, me, sibling, src=x_ref)]
        first += [copy(1 + j, me, (*chip, c), src=x_ref) for j, chip in enumerate(chips)]
        for cp in first:
            cp.start()
        passed = [copy(4 + j, (*chip, c), sibling) for j, chip in enumerate(chips)]
        for j, chip in enumerate(chips):
            copy(1 + j, (*chip, c), me).wait_recv()
            passed[j].start()
        copy(0, sibling, me).wait_recv()
        for j, chip in enumerate(chips):
            copy(4 + j, (*chip, 1 - c), me).wait_recv()
        for cp in first + passed:
            cp.wait_send()
        mine.wait()
        acc = out_ref[0:n_sum, :]
        for k in range(1, N_DEV):
            acc = acc + out_ref[k * m_per:k * m_per + n_sum, :]
        sum_ref[...] = acc

    return pl.pallas_call(
        body, name=name,
        out_shape=(jax.ShapeDtypeStruct((N_DEV * m_per, n), F32), jax.ShapeDtypeStruct((n_sum, n), F32)),
        in_specs=[pl.BlockSpec(memory_space=pltpu.VMEM)],
        out_specs=(pl.BlockSpec(memory_space=pltpu.VMEM), pl.BlockSpec(memory_space=pltpu.VMEM)),
        scratch_shapes=[pltpu.SemaphoreType.DMA((7,)), pltpu.SemaphoreType.DMA((7,)), pltpu.SemaphoreType.DMA],
        compiler_params=_cparams(vmem=VMEM_MID),
    )(x_pad)


def _gather_chips(halves, name):
    n_arr = len(halves)

    def body(*refs):
        ins, outs = refs[:n_arr], refs[n_arr:2 * n_arr]
        send_sems, recv_sems, local_sems = refs[2 * n_arr:]
        x, y, c = _mesh_pos()
        sibling = (x, y, 1 - c)
        chips = [(1 - x, y), (x, 1 - y), (1 - x, 1 - y)]
        my_chip = 2 * x + y

        def slot(n, chip_idx, half):
            return outs[n].at[chip_idx, half]

        def send(n, j):
            return pltpu.make_async_remote_copy(
                src_ref=ins[n].at[c], dst_ref=slot(n, my_chip, c),
                send_sem=send_sems.at[n * 3 + j], recv_sem=recv_sems.at[n * 3 + j],
                device_id=(*chips[j], c), device_id_type=MESH)

        def landed(n, j):
            idx = 2 * chips[j][0] + chips[j][1]
            return pltpu.make_async_remote_copy(
                src_ref=ins[n].at[c], dst_ref=slot(n, idx, c),
                send_sem=send_sems.at[n * 3 + j], recv_sem=recv_sems.at[n * 3 + j],
                device_id=(*chips[j], c), device_id_type=MESH)

        def forward(n, j, half):
            idx = 2 * chips[j][0] + chips[j][1]
            k = 3 * n_arr + n * 3 + j
            return pltpu.make_async_remote_copy(
                src_ref=slot(n, idx, half), dst_ref=slot(n, idx, half),
                send_sem=send_sems.at[k], recv_sem=recv_sems.at[k],
                device_id=sibling, device_id_type=MESH)

        locals_ = [pltpu.make_async_copy(ins[n], outs[n].at[my_chip], local_sems.at[n]) for n in range(n_arr)]
        for cp in locals_:
            cp.start()
        sends = [send(n, j) for n in range(n_arr) for j in range(3)]
        for cp in sends:
            cp.start()
        fwds = []
        for n in range(n_arr):
            for j in range(3):
                landed(n, j).wait_recv()
                f = forward(n, j, c)
                f.start()
                fwds.append(f)
        for n in range(n_arr):
            for j in range(3):
                forward(n, j, 1 - c).wait_recv()
        for cp in sends + fwds:
            cp.wait_send()
        for cp in locals_:
            cp.wait()

    any_spec = pl.BlockSpec(memory_space=pl.ANY)
    return pl.pallas_call(
        body, name=name,
        out_shape=tuple(jax.ShapeDtypeStruct((N_CHIPS,) + h.shape, h.dtype) for h in halves),
        in_specs=[any_spec] * n_arr, out_specs=tuple([any_spec] * n_arr),
        scratch_shapes=[pltpu.SemaphoreType.DMA((6 * n_arr,)), pltpu.SemaphoreType.DMA((6 * n_arr,)),
                        pltpu.SemaphoreType.DMA((n_arr,))],
    )(*halves)


def _scatter_chips(stacks, name):
    n_arr = len(stacks)

    def body(*refs):
        ins, outs = refs[:n_arr], refs[n_arr:2 * n_arr]
        send_sems, recv_sems, local_sems = refs[2 * n_arr:]
        x, y, c = _mesh_pos()
        chips = [(1 - x, y), (x, 1 - y), (1 - x, 1 - y)]
        my_chip = 2 * x + y
        copies = []
        for n in range(n_arr):
            for j in range(3):
                idx = 2 * chips[j][0] + chips[j][1]
                copies.append(pltpu.make_async_remote_copy(
                    src_ref=ins[n].at[idx], dst_ref=outs[n].at[my_chip],
                    send_sem=send_sems.at[n * 3 + j], recv_sem=recv_sems.at[n * 3 + j],
                    device_id=(*chips[j], c), device_id_type=MESH))
        locals_ = [pltpu.make_async_copy(ins[n].at[my_chip], outs[n].at[my_chip], local_sems.at[n])
                   for n in range(n_arr)]
        for cp in locals_ + copies:
            cp.start()
        for n in range(n_arr):
            for j in range(3):
                idx = 2 * chips[j][0] + chips[j][1]
                pltpu.make_async_remote_copy(
                    src_ref=ins[n].at[idx], dst_ref=outs[n].at[idx],
                    send_sem=send_sems.at[n * 3 + j], recv_sem=recv_sems.at[n * 3 + j],
                    device_id=(*chips[j], c), device_id_type=MESH).wait_recv()
        for cp in copies:
            cp.wait_send()
        for cp in locals_:
            cp.wait()

    any_spec = pl.BlockSpec(memory_space=pl.ANY)
    return pl.pallas_call(
        body, name=name,
        out_shape=tuple(jax.ShapeDtypeStruct(s.shape, s.dtype) for s in stacks),
        in_specs=[any_spec] * n_arr, out_specs=tuple([any_spec] * n_arr),
        scratch_shapes=[pltpu.SemaphoreType.DMA((3 * n_arr,)), pltpu.SemaphoreType.DMA((3 * n_arr,)),
                        pltpu.SemaphoreType.DMA((n_arr,))],
    )(*stacks)


def _swap_sibling(arrs, name):
    n_arr = len(arrs)

    def body(*refs):
        ins, outs = refs[:n_arr], refs[n_arr:2 * n_arr]
        send_sems, recv_sems = refs[2 * n_arr:]
        x, y, c = _mesh_pos()
        copies = [pltpu.make_async_remote_copy(
            src_ref=ins[n], dst_ref=outs[n], send_sem=send_sems.at[n], recv_sem=recv_sems.at[n],
            device_id=(x, y, 1 - c), device_id_type=MESH) for n in range(n_arr)]
        for cp in copies:
            cp.start()
        for cp in copies:
            cp.wait()

    any_spec = pl.BlockSpec(memory_space=pl.ANY)
    return pl.pallas_call(
        body, name=name,
        out_shape=tuple(jax.ShapeDtypeStruct(a.shape, a.dtype) for a in arrs),
        in_specs=[any_spec] * n_arr, out_specs=tuple([any_spec] * n_arr),
        scratch_shapes=[pltpu.SemaphoreType.DMA((n_arr,)), pltpu.SemaphoreType.DMA((n_arr,))],
    )(*arrs)


def _mod_shard(c_all, w_ada_sh, b_sh):
    d, n = w_ada_sh.shape
    bn = 512

    def body(c_ref, w_ref, b_ref, o_ref):
        cc = c_ref[...]
        ca = cc * _sigmoid(cc)
        o_ref[...] = _dot(ca, w_ref[...]) + b_ref[...]

    return pl.pallas_call(
        body, name="mod_shard", grid=(n // bn,),
        out_shape=jax.ShapeDtypeStruct((N_DEV, n), F32),
        in_specs=[_full((N_DEV, d)), pl.BlockSpec((d, bn), lambda j: (0, j)), pl.BlockSpec((1, bn), lambda j: (0, j))],
        out_specs=pl.BlockSpec((N_DEV, bn), lambda j: (0, j)),
        compiler_params=_cparams(("parallel",)),
    )(c_all, w_ada_sh, b_sh)


def _ssm_prep(lam_re, lam_im, log_step):
    g, p = lam_re.shape

    def body(lr_ref, li_ref, ls_ref, ar_ref, ai_ref, cr_ref, ci_ref):
        lr = jnp.minimum(lr_ref[...], LAMBDA_RE_MAX)
        li = li_ref[...]
        st = jnp.exp(ls_ref[...])
        mag = jnp.exp(lr * st)
        ar = mag * jnp.cos(li * st)
        ai = mag * jnp.sin(li * st)
        den = lr * lr + li * li
        nr = ar - 1.0
        ar_ref[...] = ar
        ai_ref[...] = ai
        cr_ref[...] = (nr * lr + ai * li) / den
        ci_ref[...] = (ai * lr - nr * li) / den

    sds = jax.ShapeDtypeStruct((g, p), F32)
    return pl.pallas_call(body, name="ssm_prep", out_shape=(sds,) * 4)(lam_re, lam_im, log_step)


def _ssm_blocks(bt_re, bt_im, ct_re, ct_im, coef_rows):
    gh, gp = bt_re.shape
    nb = 4
    cb, rb = gp // nb, gp // nb

    def body(btr, bti, ctr, cti, cf, bre_o, bim_o, cre_o, cim_o):
        j = pl.program_id(0)
        row = lax.broadcasted_iota(jnp.int32, (gh, cb), 0)
        col = lax.broadcasted_iota(jnp.int32, (gh, cb), 1) + j * cb
        mask = (row >> 4) == (col >> 6)
        cr, ci = cf[0:1, :], cf[1:2, :]
        br, bi = btr[...], bti[...]
        bre_o[...] = jnp.where(mask, br * cr - bi * ci, 0.0).astype(BF16)
        bim_o[...] = jnp.where(mask, br * ci + bi * cr, 0.0).astype(BF16)
        row2 = lax.broadcasted_iota(jnp.int32, (rb, gh), 0) + j * rb
        col2 = lax.broadcasted_iota(jnp.int32, (rb, gh), 1)
        mask2 = (row2 >> 6) == (col2 >> 4)
        cre_o[...] = jnp.where(mask2, ctr[...], 0.0).astype(BF16)
        cim_o[...] = jnp.where(mask2, cti[...], 0.0).astype(BF16)

    bspec = pl.BlockSpec((gh, cb), lambda j: (0, j))
    cspec = pl.BlockSpec((rb, gh), lambda j: (j, 0))
    return pl.pallas_call(
        body, name="ssm_blocks", grid=(nb,),
        out_shape=(jax.ShapeDtypeStruct((gh, gp), BF16),) * 2 + (jax.ShapeDtypeStruct((gp, gh), BF16),) * 2,
        in_specs=[bspec, bspec, cspec, cspec, pl.BlockSpec((SUBLANES, cb), lambda j: (0, j))],
        out_specs=(bspec, bspec, cspec, cspec),
        compiler_params=_cparams(("parallel",)),
    )(bt_re, bt_im, ct_re, ct_im, coef_rows)


def _scan_consts(a_ref, reverse):
    w = a_ref.shape[1]
    ar1 = a_ref[0:1, :]
    ai1 = a_ref[1:2, :]
    if reverse:
        ai1 = -ai1
    pr, pi = [ar1], [ai1]
    for _ in range(1, SUBLANES):
        nr = pr[-1] * ar1 - pi[-1] * ai1
        ni = pr[-1] * ai1 + pi[-1] * ar1
        pr.append(nr)
        pi.append(ni)
    row = lax.broadcasted_iota(jnp.int32, (SUBLANES, w), 0)
    dist = (SUBLANES - 1 - row) if reverse else row

    def pick(vals):
        out = jnp.broadcast_to(vals[SUBLANES - 1], (SUBLANES, w))
        for r in range(SUBLANES - 1):
            out = jnp.where(dist == r, vals[r], out)
        return out

    p_r, p_i = pick(pr), pick(pi)
    steps = []
    for k in (1, 2, 4):
        steps.append((k, jnp.where(dist >= k, pr[k - 1], 0.0), jnp.where(dist >= k, pi[k - 1], 0.0)))
    a8 = (jnp.broadcast_to(pr[SUBLANES - 1], (SUBLANES, w)), jnp.broadcast_to(pi[SUBLANES - 1], (SUBLANES, w)))
    return row, p_r, p_i, steps, a8


def _scan_tile(xr, xi, cr, ci, consts, reverse):
    row, p_r, p_i, steps, (a8r, a8i) = consts
    for k, s_r, s_i in steps:
        sh = (SUBLANES - k) if reverse else k
        qr = pltpu.roll(xr, sh, 0)
        qi = pltpu.roll(xi, sh, 0)
        xr, xi = xr + s_r * qr - s_i * qi, xi + s_r * qi + s_i * qr
    outr = xr + p_r * cr - p_i * ci
    outi = xi + p_r * ci + p_i * cr
    e = 0 if reverse else SUBLANES - 1
    er = jnp.broadcast_to(xr[e:e + 1, :], xr.shape)
    ei = jnp.broadcast_to(xi[e:e + 1, :], xi.shape)
    return outr, outi, er + a8r * cr - a8i * ci, ei + a8r * ci + a8i * cr


def _scan_fwd(a_rows, bu_re, bu_im):
    t, n = bu_re.shape
    tb, w = _blk(t, TB_SCAN), W_SCAN
    ntile = tb // SUBLANES

    def body(a_ref, br_ref, bi_ref, sr_ref, si_ref, car, cai):
        @pl.when(pl.program_id(1) == 0)
        def _():
            car[...] = jnp.zeros(car.shape, F32)
            cai[...] = jnp.zeros(cai.shape, F32)
        consts = _scan_consts(a_ref, False)

        def tile(i, carry):
            o = pl.multiple_of(i * SUBLANES, SUBLANES)
            outr, outi, ncr, nci = _scan_tile(br_ref[pl.ds(o, SUBLANES), :], bi_ref[pl.ds(o, SUBLANES), :],
                                              carry[0], carry[1], consts, False)
            sr_ref[pl.ds(o, SUBLANES), :] = outr
            si_ref[pl.ds(o, SUBLANES), :] = outi
            return ncr, nci

        cr, ci = lax.fori_loop(0, ntile, tile, (car[...], cai[...]), unroll=4)
        car[...] = cr
        cai[...] = ci

    spec = pl.BlockSpec((tb, w), lambda s, k: (k, s))
    sds = jax.ShapeDtypeStruct((t, n), F32)
    return pl.pallas_call(
        body, name="scan_fwd", grid=(n // w, t // tb), out_shape=(sds, sds),
        in_specs=[pl.BlockSpec((SUBLANES, w), lambda s, k: (0, s)), spec, spec], out_specs=(spec, spec),
        scratch_shapes=[pltpu.VMEM((SUBLANES, w), F32), pltpu.VMEM((SUBLANES, w), F32)],
        compiler_params=_cparams(("parallel", "arbitrary")),
    )(a_rows, bu_re, bu_im)


def _scan_bwd(a_rows, g_re, g_im, s_re, s_im):
    t, n = g_re.shape
    tb, w = _blk(t, TB_SCAN), W_SCAN
    ntile = tb // SUBLANES
    nt = t // tb

    def body(a_ref, gr_ref, gi_ref, sr_ref, si_ref, or_ref, oi_ref, gar_ref, gai_ref, car, cai):
        @pl.when(pl.program_id(1) == 0)
        def _():
            car[...] = jnp.zeros(car.shape, F32)
            cai[...] = jnp.zeros(cai.shape, F32)
            gar_ref[...] = jnp.zeros(gar_ref.shape, F32)
            gai_ref[...] = jnp.zeros(gai_ref.shape, F32)
        consts = _scan_consts(a_ref, True)
        row = consts[0]

        def tile(i, carry):
            cr, ci, accr, acci = carry
            o = pl.multiple_of((ntile - 1 - i) * SUBLANES, SUBLANES)
            outr, outi, ncr, nci = _scan_tile(gr_ref[pl.ds(o, SUBLANES), :], gi_ref[pl.ds(o, SUBLANES), :],
                                              cr, ci, consts, True)
            or_ref[pl.ds(o, SUBLANES), :] = outr
            oi_ref[pl.ds(o, SUBLANES), :] = outi
            gnr = jnp.where(row == SUBLANES - 1, cr, pltpu.roll(outr, SUBLANES - 1, 0))
            gni = jnp.where(row == SUBLANES - 1, ci, pltpu.roll(outi, SUBLANES - 1, 0))
            sr = sr_ref[pl.ds(o, SUBLANES), :]
            si = si_ref[pl.ds(o, SUBLANES), :]
            return ncr, nci, accr + sr * gnr + si * gni, acci + sr * gni - si * gnr

        cr, ci, accr, acci = lax.fori_loop(0, ntile, tile, (car[...], cai[...], gar_ref[...], gai_ref[...]), unroll=4)
        car[...] = cr
        cai[...] = ci
        gar_ref[...] = accr
        gai_ref[...] = acci

    spec = pl.BlockSpec((tb, w), lambda s, k: (nt - 1 - k, s))
    aspec = pl.BlockSpec((SUBLANES, w), lambda s, k: (0, s))
    sds = jax.ShapeDtypeStruct((t, n), F32)
    asds = jax.ShapeDtypeStruct((SUBLANES, n), F32)
    return pl.pallas_call(
        body, name="scan_bwd", grid=(n // w, nt), out_shape=(sds, sds, asds, asds),
        in_specs=[aspec, spec, spec, spec, spec], out_specs=(spec, spec, aspec, aspec),
        scratch_shapes=[pltpu.VMEM((SUBLANES, w), F32), pltpu.VMEM((SUBLANES, w), F32)],
        compiler_params=_cparams(("parallel", "arbitrary")),
    )(a_rows, g_re, g_im, s_re, s_im)


def _mix_in(x, vec, w_in_st, b_re, b_im):
    t, d = x.shape
    ns, _, nc = w_in_st.shape
    dssm, nstate = b_re.shape
    tb = _blk(t, TB_MIX)

    def body(x_ref, vec_ref, w_ref, bre_ref, bim_ref, proj_ref, bur_ref, bui_ref, h1_ref):
        xv = x_ref[...]
        r = lax.rsqrt(_rowmean(xv * xv) + EPS)
        h = xv * r * vec_ref[0:1, :] * vec_ref[1:2, :] + vec_ref[2:3, :]
        hb = h.astype(BF16)
        h1_ref[...] = hb
        u = None
        for j in range(ns):
            pj = jnp.dot(hb, w_ref[j], preferred_element_type=F32)
            proj_ref[:, j * nc:(j + 1) * nc] = pj
            if j == 0:
                u = pj
        ub = u.astype(BF16)
        bur_ref[...] = jnp.dot(ub, bre_ref[...], preferred_element_type=F32)
        bui_ref[...] = jnp.dot(ub, bim_ref[...], preferred_element_type=F32)

    return pl.pallas_call(
        body, name="mix_in", grid=(t // tb,),
        out_shape=(jax.ShapeDtypeStruct((t, ns * nc), F32), jax.ShapeDtypeStruct((t, nstate), F32),
                   jax.ShapeDtypeStruct((t, nstate), F32), jax.ShapeDtypeStruct((t, d), BF16)),
        in_specs=[_rows(tb, d), _full((SUBLANES, d)), _full(w_in_st.shape), _full(b_re.shape), _full(b_im.shape)],
        out_specs=(_rows(tb, ns * nc), _rows(tb, nstate), _rows(tb, nstate), _rows(tb, d)),
        compiler_params=_cparams(("parallel",), VMEM_BIG),
    )(x, vec, w_in_st, b_re, b_im)


def _head_ms(y, h_ref):
    return _split_dot(y * y, h_ref[...])


def _conv3(x, halo, w_ref):
    return w_ref[0:1, :] * _shift_down(x, halo, 2) + w_ref[1:2, :] * _shift_down(x, halo, 1) + w_ref[2:3, :] * x


def _mix_out(x, proj, s_re, s_im, c_re, c_im, v512, convw, glu_w, h16, h64, w_out, vd):
    t, d = x.shape
    dh = c_re.shape[1]
    nstate = s_re.shape[1]
    tb = _blk(t, TB_MIX)

    def body(x_ref, u_ref, bg_ref, cg_ref, v_ref, cgh_ref, vh_ref, sr_ref, si_ref, cre_ref, cim_ref, p_ref,
             cw_ref, gw_ref, h16_ref, h64_ref, wo_ref, vd_ref, y1_ref, o_ref, x2_ref):
        i = pl.program_id(0)
        u = u_ref[...]
        ys = _dot(sr_ref[...], cre_ref[...]) - _dot(si_ref[...], cim_ref[...])
        y1 = ys + p_ref[0:1, :] * u
        y1_ref[...] = y1
        z = _gelu(y1)
        q = _dot(z, gw_ref[...]) + p_ref[1:2, :]
        ya = z * _sigmoid(q)
        na = ya * lax.rsqrt(_head_ms(ya, h16_ref) + EPS) * p_ref[2:3, :]
        cv = cg_ref[...] * v_ref[...]
        cvh = jnp.where(i > 0, cgh_ref[...] * vh_ref[...], 0.0)
        yb = bg_ref[...] * _conv3(cv, cvh, cw_ref)
        nb = yb * lax.rsqrt(_head_ms(yb, h64_ref) + EPS) * p_ref[3:4, :]
        o = _dot(na, wo_ref[0:dh, :]) + _dot(nb, wo_ref[dh:2 * dh, :])
        o_ref[...] = o
        on = o * lax.rsqrt(_rowmean(o * o) + EPS) * vd_ref[0:1, :]
        x2_ref[...] = x_ref[...] + vd_ref[1:2, :] * on

    return pl.pallas_call(
        body, name="mix_out", grid=(t // tb,),
        out_shape=(jax.ShapeDtypeStruct((t, dh), F32), jax.ShapeDtypeStruct((t, d), F32),
                   jax.ShapeDtypeStruct((t, d), F32)),
        in_specs=[_rows(tb, d), _rows(tb, dh, 0), _rows(tb, dh, 1), _rows(tb, dh, 2), _rows(tb, dh, 3),
                  _halo_prev(tb, dh, 2), _halo_prev(tb, dh, 3), _rows(tb, nstate), _rows(tb, nstate),
                  _full(c_re.shape), _full(c_im.shape), _full(v512.shape), _full(convw.shape), _full(glu_w.shape),
                  _full(h16.shape), _full(h64.shape), _full(w_out.shape), _full(vd.shape)],
        out_specs=(_rows(tb, dh), _rows(tb, d), _rows(tb, d)),
        compiler_params=_cparams(("parallel",), VMEM_BIG),
    )(x, proj, proj, proj, proj, proj, proj, s_re, s_im, c_re, c_im, v512, convw, glu_w, h16, h64, w_out, vd)


def _ffn_up(x2, vec, w_up_st):
    t, d = x2.shape
    ns, _, nc = w_up_st.shape
    tb = _blk(t, TB_FFN)

    def body(x_ref, vec_ref, w_ref, up_ref, h2_ref):
        xv = x_ref[...]
        r = lax.rsqrt(_rowmean(xv * xv) + EPS)
        h = xv * r * vec_ref[0:1, :] * vec_ref[1:2, :] + vec_ref[2:3, :]
        hb = h.astype(BF16)
        h2_ref[...] = hb
        for j in range(ns):
            up_ref[:, j * nc:(j + 1) * nc] = jnp.dot(hb, w_ref[j], preferred_element_type=F32)

    return pl.pallas_call(
        body, name="ffn_up", grid=(t // tb,),
        out_shape=(jax.ShapeDtypeStruct((t, ns * nc), F32), jax.ShapeDtypeStruct((t, d), BF16)),
        in_specs=[_rows(tb, d), _full((SUBLANES, d)), _full(w_up_st.shape)],
        out_specs=(_rows(tb, ns * nc), _rows(tb, d)),
        compiler_params=_cparams(("parallel",), VMEM_BIG),
    )(x2, vec, w_up_st)


def _ffn_down(up, fw, w_down, x2, tgt, vd):
    t, nh = up.shape
    dff, d = w_down.shape
    tb = _blk(t, TB_FFN)
    inv_d = 1.0 / d

    def body(up_ref, uph_ref, fw_ref, wd_ref, x2_ref, tgt_ref, vd_ref,
             act_ref, ddn_ref, dout_ref, dhid_ref, vec_ref, loss_ref):
        i = pl.program_id(0)
        up_v = up_ref[...]
        uph = jnp.where(i > 0, uph_ref[...], 0.0)
        hid = _conv3(up_v, uph, fw_ref)
        a = hid[:, :dff]
        vv = hid[:, dff:]
        sg = _sigmoid(a)
        si = a * sg
        actb = (si * vv).astype(BF16)
        act_ref[...] = actb
        dn = jnp.dot(actb, wd_ref[...], preferred_element_type=F32)
        r3 = lax.rsqrt(_rowmean(dn * dn) + EPS)
        xn = dn * r3
        g = vd_ref[0:1, :]
        gt2 = vd_ref[1:2, :]
        dnn = xn * g
        diff = x2_ref[...] + gt2 * dnn - tgt_ref[...]
        part = 0.5 * inv_d * jnp.sum(diff * diff)

        @pl.when(i == 0)
        def _():
            loss_ref[...] = jnp.zeros(loss_ref.shape, F32)
        loss_ref[...] += part
        dout = diff * inv_d
        dout_ref[...] = dout
        ddnn = dout * gt2
        _acc_rows(vec_ref, i == 0, [_colsum(dout * dnn), _colsum(ddnn * xn)])
        dxn = ddnn * g
        ddn = r3 * (dxn - xn * _rowmean(dxn * xn))
        ddnb = ddn.astype(BF16)
        ddn_ref[...] = ddnb
        dact = lax.dot_general(ddnb, wd_ref[...], (((1,), (1,)), ((), ())), preferred_element_type=F32)
        dhid_ref[:, :dff] = dact * vv * sg * (1.0 + a * (1.0 - sg))
        dhid_ref[:, dff:] = dact * si

    return pl.pallas_call(
        body, name="ffn_down", grid=(t // tb,),
        out_shape=(jax.ShapeDtypeStruct((t, dff), BF16), jax.ShapeDtypeStruct((t, d), BF16),
                   jax.ShapeDtypeStruct((t, d), F32), jax.ShapeDtypeStruct((t, nh), F32),
                   jax.ShapeDtypeStruct((SUBLANES, d), F32), jax.ShapeDtypeStruct((SUBLANES, 128), F32)),
        in_specs=[_rows(tb, nh), _halo_prev(tb, nh), _full(fw.shape), _full(w_down.shape), _rows(tb, d),
                  _rows(tb, d), _full(vd.shape)],
        out_specs=(_rows(tb, dff), _rows(tb, d), _rows(tb, d), _rows(tb, nh), _full((SUBLANES, d)),
                   _full((SUBLANES, 128))),
        compiler_params=_cparams(("arbitrary",), VMEM_BIG),
    )(up, up, fw, w_down, x2, tgt, vd)


def _ffn_up_bwd(dhid, up, fw, x2, dout, vec, w_up_st):
    t, nh = dhid.shape
    d = x2.shape[1]
    ns, _, nc = w_up_st.shape
    tb = _blk(t, TB_FFN)
    nblk = t // tb

    def body(dh_ref, dhn_ref, up_ref, uph_ref, fw_ref, x2_ref, dout_ref, vec_ref, w_ref,
             dx2_ref, dup_ref, vp_ref, df_ref):
        i = pl.program_id(0)
        dh = dh_ref[...]
        dhn = jnp.where(i < nblk - 1, dhn_ref[...], 0.0)
        dup = fw_ref[2:3, :] * dh + fw_ref[1:2, :] * _shift_up(dh, dhn, 1) + fw_ref[0:1, :] * _shift_up(dh, dhn, 2)
        up_v = up_ref[...]
        uph = jnp.where(i > 0, uph_ref[...], 0.0)
        _acc_rows(df_ref, i == 0, [_colsum(dh * _shift_down(up_v, uph, 2)), _colsum(dh * _shift_down(up_v, uph, 1)),
                                   _colsum(dh * up_v)])
        dupb = dup.astype(BF16)
        dup_ref[...] = dupb
        dh2 = None
        for j in range(ns):
            pj = lax.dot_general(dupb[:, j * nc:(j + 1) * nc], w_ref[j], (((1,), (1,)), ((), ())),
                                 preferred_element_type=F32)
            dh2 = pj if dh2 is None else dh2 + pj
        xv = x2_ref[...]
        r = lax.rsqrt(_rowmean(xv * xv) + EPS)
        xn = xv * r
        g = vec_ref[0:1, :]
        hg = xn * g
        dhg = dh2 * vec_ref[1:2, :]
        _acc_rows(vp_ref, i == 0, [_colsum(dh2), _colsum(dh2 * hg), _colsum(dhg * xn)])
        dxn = dhg * g
        dx2_ref[...] = dout_ref[...] + r * (dxn - xn * _rowmean(dxn * xn))

    return pl.pallas_call(
        body, name="ffn_up_bwd", grid=(nblk,),
        out_shape=(jax.ShapeDtypeStruct((t, d), F32), jax.ShapeDtypeStruct((t, nh), BF16),
                   jax.ShapeDtypeStruct((SUBLANES, d), F32), jax.ShapeDtypeStruct((SUBLANES, nh), F32)),
        in_specs=[_rows(tb, nh), _halo_next(tb, nh, t), _rows(tb, nh), _halo_prev(tb, nh), _full(fw.shape),
                  _rows(tb, d), _rows(tb, d), _full(vec.shape), _full(w_up_st.shape)],
        out_specs=(_rows(tb, d), _rows(tb, nh), _full((SUBLANES, d)), _full((SUBLANES, nh))),
        compiler_params=_cparams(("arbitrary",), VMEM_BIG),
    )(dhid, dhid, up, up, fw, x2, dout, vec, w_up_st)


def _mix_out_bwd(dx2, o, y1, proj, c_re, c_im, v512, convw, glu_w, h16, h64, w_out, vd):
    t, d = dx2.shape
    dh = y1.shape[1]
    nstate = c_re.shape[0]
    tb = _blk(t, TB_MIX)

    def body(dx2_ref, o_ref, y1_ref, u_ref, bg_ref, cg_ref, v_ref, cgh_ref, vh_ref, cre_ref, cim_ref, p_ref,
             cw_ref, gw_ref, h16_ref, h64_ref, wo_ref, vd_ref,
             do_ref, ycat_ref, z_ref, dq_ref, dy1_ref, gr_ref, gi_ref, dcc_ref, dbg_ref, vpd_ref, vp5_ref):
        i = pl.program_id(0)
        first = i == 0
        ov = o_ref[...]
        ro = lax.rsqrt(_rowmean(ov * ov) + EPS)
        on_ = ov * ro
        g = vd_ref[0:1, :]
        dx2v = dx2_ref[...]
        don = dx2v * vd_ref[1:2, :]
        _acc_rows(vpd_ref, first, [_colsum(dx2v * on_ * g), _colsum(don * on_)])
        dxn = don * g
        dob = (ro * (dxn - on_ * _rowmean(dxn * on_))).astype(BF16)
        do_ref[...] = dob
        dyc_a = lax.dot_general(dob, wo_ref[0:dh, :], (((1,), (1,)), ((), ())), preferred_element_type=F32)
        dyc_b = lax.dot_general(dob, wo_ref[dh:2 * dh, :], (((1,), (1,)), ((), ())), preferred_element_type=F32)
        y1v = y1_ref[...]
        u = u_ref[...]
        z = _gelu(y1v)
        zb = z.astype(BF16)
        z_ref[...] = zb
        sg = _sigmoid(jnp.dot(zb, gw_ref[...], preferred_element_type=F32) + p_ref[1:2, :])
        ya = z * sg
        ra = lax.rsqrt(_head_ms(ya, h16_ref) + EPS)
        yan = ya * ra
        ga = p_ref[2:3, :]
        ycat_ref[:, 0:dh] = (yan * ga).astype(BF16)
        dyn = dyc_a * ga
        dya = ra * (dyn - yan * _split_dot(dyn * yan, h16_ref[...]))
        dq = dya * z * sg * (1.0 - sg)
        dqb = dq.astype(BF16)
        dq_ref[...] = dqb
        dz = dya * sg + lax.dot_general(dqb, gw_ref[...], (((1,), (1,)), ((), ())), preferred_element_type=F32)
        dy1 = dz * _gelu_grad(y1v)
        dy1_ref[...] = dy1
        dy1b = dy1.astype(BF16)
        gr_ref[...] = lax.dot_general(dy1b, cre_ref[...], (((1,), (1,)), ((), ())), preferred_element_type=F32)
        gi_ref[...] = -lax.dot_general(dy1b, cim_ref[...], (((1,), (1,)), ((), ())), preferred_element_type=F32)
        bg = bg_ref[...]
        cv = cg_ref[...] * v_ref[...]
        cvh = jnp.where(i > 0, cgh_ref[...] * vh_ref[...], 0.0)
        cv1 = _shift_down(cv, cvh, 1)
        cv2 = _shift_down(cv, cvh, 2)
        cc = cw_ref[0:1, :] * cv2 + cw_ref[1:2, :] * cv1 + cw_ref[2:3, :] * cv
        yb = bg * cc
        rb = lax.rsqrt(_head_ms(yb, h64_ref) + EPS)
        ybn = yb * rb
        gb = p_ref[3:4, :]
        ycat_ref[:, dh:2 * dh] = (ybn * gb).astype(BF16)
        dynb = dyc_b * gb
        dyb = rb * (dynb - ybn * _split_dot(dynb * ybn, h64_ref[...]))
        dcc = dyb * bg
        dbg_ref[...] = dyb * cc
        dcc_ref[...] = dcc
        _acc_rows(vp5_ref, first, [_colsum(dyc_a * yan), _colsum(dyc_b * ybn), _colsum(dq), _colsum(dy1 * u),
                                   _colsum(dcc * cv2), _colsum(dcc * cv1), _colsum(dcc * cv)])

    return pl.pallas_call(
        body, name="mix_out_bwd", grid=(t // tb,),
        out_shape=(jax.ShapeDtypeStruct((t, d), BF16), jax.ShapeDtypeStruct((t, 2 * dh), BF16),
                   jax.ShapeDtypeStruct((t, dh), BF16), jax.ShapeDtypeStruct((t, dh), BF16),
                   jax.ShapeDtypeStruct((t, dh), F32), jax.ShapeDtypeStruct((t, nstate), F32),
                   jax.ShapeDtypeStruct((t, nstate), F32), jax.ShapeDtypeStruct((t, dh), F32),
                   jax.ShapeDtypeStruct((t, dh), F32), jax.ShapeDtypeStruct((SUBLANES, d), F32),
                   jax.ShapeDtypeStruct((SUBLANES, dh), F32)),
        in_specs=[_rows(tb, d), _rows(tb, d), _rows(tb, dh), _rows(tb, dh, 0), _rows(tb, dh, 1), _rows(tb, dh, 2),
                  _rows(tb, dh, 3), _halo_prev(tb, dh, 2), _halo_prev(tb, dh, 3), _full(c_re.shape), _full(c_im.shape),
                  _full(v512.shape), _full(convw.shape), _full(glu_w.shape), _full(h16.shape), _full(h64.shape),
                  _full(w_out.shape), _full(vd.shape)],
        out_specs=(_rows(tb, d), _rows(tb, 2 * dh), _rows(tb, dh), _rows(tb, dh), _rows(tb, dh), _rows(tb, nstate),
                   _rows(tb, nstate), _rows(tb, dh), _rows(tb, dh), _full((SUBLANES, d)), _full((SUBLANES, dh))),
        compiler_params=_cparams(("arbitrary",), VMEM_BIG),
    )(dx2, o, y1, proj, proj, proj, proj, proj, proj, c_re, c_im, v512, convw, glu_w, h16, h64, w_out, vd)


def _mix_in_bwd(gt_re, gt_im, b_re, b_im, dy1, dcc, dbg, proj, x, dx2, vec, v512, convw, w_in_st):
    t, d = x.shape
    dh = dy1.shape[1]
    nstate = gt_re.shape[1]
    ns, _, nc = w_in_st.shape
    tb = _blk(t, TB_MIX)
    nblk = t // tb

    def body(gr_ref, gi_ref, bre_ref, bim_ref, dy1_ref, dcc_ref, dccn_ref, dbg_ref, cg_ref, v_ref, x_ref, dx2_ref,
             vec_ref, p_ref, cw_ref, w_ref, gx_ref, dproj_ref, vp_ref):
        i = pl.program_id(0)
        du = (dy1_ref[...] * p_ref[0:1, :]
              + lax.dot_general(gr_ref[...].astype(BF16), bre_ref[...], (((1,), (1,)), ((), ())),
                                preferred_element_type=F32)
              + lax.dot_general(gi_ref[...].astype(BF16), bim_ref[...], (((1,), (1,)), ((), ())),
                                preferred_element_type=F32))
        dcc = dcc_ref[...]
        dccn = jnp.where(i < nblk - 1, dccn_ref[...], 0.0)
        dcv = (cw_ref[2:3, :] * dcc + cw_ref[1:2, :] * _shift_up(dcc, dccn, 1)
               + cw_ref[0:1, :] * _shift_up(dcc, dccn, 2))
        parts = [du, dbg_ref[...], dcv * v_ref[...], dcv * cg_ref[...]]
        dh1 = None
        for j in range(ns):
            pb = parts[j].astype(BF16)
            dproj_ref[:, j * nc:(j + 1) * nc] = pb
            pj = lax.dot_general(pb, w_ref[j], (((1,), (1,)), ((), ())), preferred_element_type=F32)
            dh1 = pj if dh1 is None else dh1 + pj
        xv = x_ref[...]
        r = lax.rsqrt(_rowmean(xv * xv) + EPS)
        xn = xv * r
        g = vec_ref[0:1, :]
        hg = xn * g
        dhg = dh1 * vec_ref[1:2, :]
        _acc_rows(vp_ref, i == 0, [_colsum(dh1), _colsum(dh1 * hg), _colsum(dhg * xn)])
        dxn = dhg * g
        gx_ref[...] = dx2_ref[...] + r * (dxn - xn * _rowmean(dxn * xn))

    assert nc == dh and ns == 4
    return pl.pallas_call(
        body, name="mix_in_bwd", grid=(nblk,),
        out_shape=(jax.ShapeDtypeStruct((t, d), F32), jax.ShapeDtypeStruct((t, ns * nc), BF16),
                   jax.ShapeDtypeStruct((SUBLANES, d), F32)),
        in_specs=[_rows(tb, nstate), _rows(tb, nstate), _full(b_re.shape), _full(b_im.shape), _rows(tb, dh),
                  _rows(tb, dh), _halo_next(tb, dh, t), _rows(tb, dh), _rows(tb, dh, 2), _rows(tb, dh, 3),
                  _rows(tb, d), _rows(tb, d), _full(vec.shape), _full(v512.shape), _full(convw.shape),
                  _full(w_in_st.shape)],
        out_specs=(_rows(tb, d), _rows(tb, ns * nc), _full((SUBLANES, d))),
        compiler_params=_cparams(("arbitrary",), VMEM_BIG),
    )(gt_re, gt_im, b_re, b_im, dy1, dcc, dcc, dbg, proj, proj, x, dx2, vec, v512, convw, w_in_st)


def _matmul_tn(a, b, m, bn, out_dtype, name, a_col=0):
    t = a.shape[0]
    n = b.shape[1]
    bt = _blk(t, TB_TN)
    nk = t // bt

    def body(a_ref, b_ref, o_ref, acc_ref):
        k = pl.program_id(1)

        @pl.when(k == 0)
        def _():
            acc_ref[...] = jnp.zeros(acc_ref.shape, F32)
        acc_ref[...] += _dot_tn(a_ref[...], b_ref[...])

        @pl.when(k == nk - 1)
        def _():
            o_ref[...] = acc_ref[...].astype(out_dtype)

    return pl.pallas_call(
        body, name=name, grid=(n // bn, nk),
        out_shape=jax.ShapeDtypeStruct((n // bn, m, bn), out_dtype),
        in_specs=[pl.BlockSpec((bt, m), lambda j, k: (k, a_col)), pl.BlockSpec((bt, bn), lambda j, k: (k, j))],
        out_specs=pl.BlockSpec((None, m, bn), lambda j, k: (j, 0, 0)),
        scratch_shapes=[pltpu.VMEM((m, bn), F32)],
        compiler_params=_cparams(("parallel", "arbitrary"), VMEM_BIG),
    )(a, b)


def _ssm_bgrad(d_bre, d_bim, bt_re, bt_im, rows_in, fold):
    gh, gp = d_bre.shape
    nb = 4
    cb = gp // nb
    p = fold.shape[1]

    def body(dr_ref, di_ref, br_ref, bi_ref, rin_ref, f_ref, dbr_ref, dbi_ref, rout_ref):
        j = pl.program_id(0)
        row = lax.broadcasted_iota(jnp.int32, (gh, cb), 0)
        col = lax.broadcasted_iota(jnp.int32, (gh, cb), 1) + j * cb
        mask = (row >> 4) == (col >> 6)
        gr = jnp.where(mask, dr_ref[...], 0.0)
        gi = jnp.where(mask, di_ref[...], 0.0)
        cr, ci = rin_ref[0:1, :], rin_ref[1:2, :]

        @pl.when(j == 0)
        def _():
            dbr_ref[...] = jnp.zeros(dbr_ref.shape, F32)
            dbi_ref[...] = jnp.zeros(dbi_ref.shape, F32)
        dbr_ref[...] += _split3_dot(cr * gr + ci * gi, f_ref[...])
        dbi_ref[...] += _split3_dot(cr * gi - ci * gr, f_ref[...])
        br, bi = br_ref[...], bi_ref[...]
        rout_ref[...] = jnp.zeros(rout_ref.shape, F32)
        rout_ref[0:1, :] = _colsum(br * gr + bi * gi)
        rout_ref[1:2, :] = _colsum(br * gi - bi * gr)

    bspec = pl.BlockSpec((gh, cb), lambda j: (0, j))
    rspec = pl.BlockSpec((SUBLANES, cb), lambda j: (0, j))
    return pl.pallas_call(
        body, name="ssm_bgrad", grid=(nb,),
        out_shape=(jax.ShapeDtypeStruct((gh, p), F32), jax.ShapeDtypeStruct((gh, p), F32),
                   jax.ShapeDtypeStruct((SUBLANES, gp), F32)),
        in_specs=[bspec, bspec, bspec, bspec, rspec, pl.BlockSpec((cb, p), lambda j: (j, 0))],
        out_specs=(_full((gh, p)), _full((gh, p)), rspec),
        compiler_params=_cparams(("arbitrary",)),
    )(d_bre, d_bim, bt_re, bt_im, rows_in, fold)


def _ssm_cgrad(d_cre, d_cim, fold):
    gp, gh = d_cre.shape
    nb = 4
    rb = gp // nb
    h = fold.shape[1]

    def body(dr_ref, di_ref, f_ref, cr_ref, ci_ref):
        j = pl.program_id(0)
        row = lax.broadcasted_iota(jnp.int32, (rb, gh), 0) + j * rb
        col = lax.broadcasted_iota(jnp.int32, (rb, gh), 1)
        mask = (row >> 6) == (col >> 4)
        cr_ref[...] = _split3_dot(jnp.where(mask, dr_ref[...], 0.0), f_ref[...])
        ci_ref[...] = -_split3_dot(jnp.where(mask, di_ref[...], 0.0), f_ref[...])

    cspec = pl.BlockSpec((rb, gh), lambda j: (j, 0))
    ospec = pl.BlockSpec((rb, h), lambda j: (j, 0))
    return pl.pallas_call(
        body, name="ssm_cgrad", grid=(nb,),
        out_shape=(jax.ShapeDtypeStruct((gp, h), F32),) * 2,
        in_specs=[cspec, cspec, _full(fold.shape)], out_specs=(ospec, ospec),
        compiler_params=_cparams(("parallel",)),
    )(d_cre, d_cim, fold)


def _ssm_lamgrad(lam_re, lam_im, log_step, abar_re, abar_im, coef_re, coef_im, gc_re, gc_im, ga_re, ga_im):
    g, p = lam_re.shape

    def body(lr_ref, li_ref, ls_ref, ar_ref, ai_ref, cr_ref, ci_ref, gcr_ref, gci_ref, gar_ref, gai_ref,
             dlr_ref, dli_ref, dls_ref):
        lam_raw = lr_ref[...]
        lr = jnp.minimum(lam_raw, LAMBDA_RE_MAX)
        li = li_ref[...]
        st = jnp.exp(ls_ref[...])
        den = lr * lr + li * li
        gcr, gci = gcr_ref[...], gci_ref[...]
        gab_r = gar_ref[...] + (lr * gcr - li * gci) / den
        gab_i = gai_ref[...] + (lr * gci + li * gcr) / den
        cr, ci = cr_ref[...], ci_ref[...]
        wr = -(cr * lr + ci * li) / den
        wi = -(ci * lr - cr * li) / den
        gl_r = wr * gcr + wi * gci
        gl_i = wr * gci - wi * gcr
        ar, ai = ar_ref[...], ai_ref[...]
        gw_r = ar * gab_r + ai * gab_i
        gw_i = ar * gab_i - ai * gab_r
        gl_r = gl_r + st * gw_r
        gl_i = gl_i + st * gw_i
        pass_through = jnp.where(lam_raw < LAMBDA_RE_MAX, 1.0, jnp.where(lam_raw == LAMBDA_RE_MAX, 0.5, 0.0))
        dlr_ref[...] = gl_r * pass_through
        dli_ref[...] = gl_i
        dls_ref[...] = st * jnp.sum(lr * gw_r + li * gw_i, axis=1, keepdims=True)

    sds = jax.ShapeDtypeStruct((g, p), F32)
    return pl.pallas_call(body, name="ssm_lamgrad", out_shape=(sds, sds, jax.ShapeDtypeStruct((g, 1), F32)))(
        lam_re, lam_im, log_step, abar_re, abar_im, coef_re, coef_im, gc_re, gc_im, ga_re, ga_im)


def _adamw_math(w, g, m, v):
    m = ADAM_B1 * m + (1.0 - ADAM_B1) * g
    v = ADAM_B2 * v + (1.0 - ADAM_B2) * (g * g)
    m_hat = m / (1.0 - ADAM_B1 ** ADAM_STEP)
    v_hat = v / (1.0 - ADAM_B2 ** ADAM_STEP)
    delta = -ADAM_LR * (m_hat / (jnp.sqrt(v_hat) + ADAM_EPS) + ADAM_WD * w)
    return delta, m, v


def _adamw_big(p_mine, p_sib, w, m, v, name):
    r, c = w.shape
    rb = 64 if r % 64 == 0 else r

    def body(a_ref, b_ref, w_ref, m_ref, v_ref, g_ref, d_ref, mo_ref, vo_ref):
        g = a_ref[...] + b_ref[...]
        g_ref[...] = g
        d_ref[...], mo_ref[...], vo_ref[...] = _adamw_math(w_ref[...], g, m_ref[...], v_ref[...])

    spec = pl.BlockSpec((rb, c), lambda i: (i, 0))
    sds = jax.ShapeDtypeStruct((r, c), F32)
    return pl.pallas_call(
        body, name=name, grid=(r // rb,), out_shape=(sds,) * 4, in_specs=[spec] * 5, out_specs=(spec,) * 4,
        compiler_params=_cparams(("parallel",)),
    )(p_mine, p_sib, w, m, v)


def _sum_blocks(stack, name):
    n, r, c = stack.shape
    rb = 64 if r % 64 == 0 else r

    def body(s_ref, o_ref):
        acc = s_ref[0].astype(F32)
        for k in range(1, n):
            acc = acc + s_ref[k].astype(F32)
        o_ref[...] = acc

    return pl.pallas_call(
        body, name=name, grid=(r // rb,), out_shape=jax.ShapeDtypeStruct((r, c), F32),
        in_specs=[pl.BlockSpec((n, rb, c), lambda i: (0, i, 0))], out_specs=pl.BlockSpec((rb, c), lambda i: (i, 0)),
        compiler_params=_cparams(("parallel",)),
    )(stack)


def _adamw_ada(c_all, dmod_cols, w, m, v):
    d, n = w.shape
    bn = 512

    def body(c_ref, dm_ref, w_ref, m_ref, v_ref, g_ref, d_ref, mo_ref, vo_ref):
        cc = c_ref[...]
        g = _dot_tn(cc * _sigmoid(cc), dm_ref[...])
        g_ref[...] = g
        d_ref[...], mo_ref[...], vo_ref[...] = _adamw_math(w_ref[...], g, m_ref[...], v_ref[...])

    spec = pl.BlockSpec((d, bn), lambda j: (0, j))
    sds = jax.ShapeDtypeStruct((d, n), F32)
    return pl.pallas_call(
        body, name="adamw_ada", grid=(n // bn,), out_shape=(sds,) * 4,
        in_specs=[_full((N_DEV, d)), pl.BlockSpec((N_DEV, bn), lambda j: (0, j)), spec, spec, spec],
        out_specs=(spec,) * 4, compiler_params=_cparams(("parallel",)),
    )(c_all, dmod_cols, w, m, v)


def _adamw_small(items):
    n = len(items)

    def body(*refs):
        ins, outs = refs[:4 * n], refs[4 * n:]
        for k in range(n):
            w_ref, g_ref, m_ref, v_ref = ins[4 * k:4 * k + 4]
            outs[3 * k][...], outs[3 * k + 1][...], outs[3 * k + 2][...] = _adamw_math(
                w_ref[...], g_ref[...], m_ref[...], v_ref[...])

    flat = [a for it in items for a in it]
    out_shape = tuple(jax.ShapeDtypeStruct(it[0].shape, F32) for it in items for _ in range(3))
    res = pl.pallas_call(body, name="adamw_small", out_shape=out_shape)(*flat)
    return [tuple(res[3 * k:3 * k + 3]) for k in range(n)]


def _rows8(*rows):
    c = rows[0].shape[-1]
    pad = jnp.zeros((SUBLANES - len(rows), c), F32)
    return jnp.concatenate([r.reshape(1, c) for r in rows] + [pad], axis=0)


def _to_rows(a, width):
    flat = a.reshape(-1)
    n = -(-flat.shape[0] // width)
    flat = jnp.pad(flat, (0, n * width - flat.shape[0]))
    return flat.reshape(n, width)


def kernel(x, c, w_ada, b_ada, g_pre_mix, g_post_mix, w_in, ssm_lam_re, ssm_lam_im, ssm_log_step, ssm_b_re, ssm_b_im, ssm_c_re, ssm_c_im, ssm_d, glu_w, glu_b, g_out_ssm, conv_w, g_out_conv, w_out, g_pre_ffn, g_post_ffn, w_up, ffn_conv_w, w_down, loss_target, m_w_ada, m_b_ada, m_g_pre_mix, m_g_post_mix, m_w_in, m_ssm_lam_re, m_ssm_lam_im, m_ssm_log_step, m_ssm_b_re, m_ssm_b_im, m_ssm_c_re, m_ssm_c_im, m_ssm_d, m_glu_w, m_glu_b, m_g_out_ssm, m_conv_w, m_g_out_conv, m_w_out, m_g_pre_ffn, m_g_post_ffn, m_w_up, m_ffn_conv_w, m_w_down, v_w_ada, v_b_ada, v_g_pre_mix, v_g_post_mix, v_w_in, v_ssm_lam_re, v_ssm_lam_im, v_ssm_log_step, v_ssm_b_re, v_ssm_b_im, v_ssm_c_re, v_ssm_c_im, v_ssm_d, v_glu_w, v_glu_b, v_g_out_ssm, v_conv_w, v_g_out_conv, v_w_out, v_g_pre_ffn, v_g_post_ffn, v_w_up, v_ffn_conv_w, v_w_down):
    xs = x[0]
    tgt = loss_target[0]
    t, d = xs.shape
    xi, yi, ci = lax.axis_index("x"), lax.axis_index("y"), lax.axis_index("c")
    chip = 2 * xi + yi
    dev = 2 * chip + ci

    n_groups, n_state = ssm_lam_re.shape[1:]
    n_gch = ssm_b_re.shape[3]
    d_ssm = n_groups * n_gch
    gp = n_groups * n_state
    n_ada = w_ada.shape[2]
    d_ff = w_down.shape[1] * N_CHIPS
    n_upc = w_up.shape[2]

    c_gath, _ = _allgather8(jnp.broadcast_to(c, (SUBLANES, d)), SUBLANES, "gather_c")
    c_all = c_gath.reshape(N_DEV, SUBLANES, d)[:, 0, :]
    b_sh = lax.dynamic_slice(b_ada, (0, chip * n_ada), (1, n_ada))
    mod_sh = _mod_shard(c_all, w_ada[0], b_sh)

    def halves(a):
        return a.reshape(2, a.shape[0] // 2, a.shape[1])

    def pad8(a):
        return jnp.concatenate([a, jnp.zeros((SUBLANES - a.shape[0], a.shape[1]), a.dtype)], axis=0)

    halves_in = [
        halves(w_in[0].astype(BF16)), halves(glu_w[0].astype(BF16)), halves(w_out[0].astype(BF16)),
        halves(w_up[0].astype(BF16)), halves(w_down[0].astype(BF16)),
        jnp.stack([mod_sh[:, :n_ada // 2], mod_sh[:, n_ada // 2:]]),
        jnp.stack([pad8(conv_w[0]), jnp.zeros((SUBLANES, conv_w.shape[2]), F32)]),
        jnp.stack([pad8(ffn_conv_w[0]), jnp.zeros((SUBLANES, ffn_conv_w.shape[2]), F32)]),
    ]
    g_win, g_glu, g_wout, g_wup, g_wdown, g_mod, g_cw, g_fw = _gather_chips(halves_in, "gather_weights")
    w_in_st = g_win.reshape(N_CHIPS, d, w_in.shape[2])
    glu_full = g_glu.reshape(d_ssm, d_ssm)
    w_out_full = g_wout.reshape(2 * d_ssm, d)
    w_up_st = g_wup.reshape(N_CHIPS, d, n_upc)
    w_down_full = g_wdown.reshape(d_ff, d)
    mod_all = g_mod.transpose(2, 0, 1, 3).reshape(N_DEV, N_CHIPS * n_ada)
    mod = lax.dynamic_slice(mod_all, (dev, 0), (1, N_CHIPS * n_ada))
    sh1, sc1, gt1, sh2, sc2, gt2 = [mod[:, k * d:(k + 1) * d] for k in range(6)]
    convw_full = pad8(g_cw[:, 0, :3, :].transpose(1, 0, 2).reshape(3, d_ssm))
    fw_full = pad8(g_fw[:, 0, :3, :].transpose(1, 0, 2).reshape(3, N_CHIPS * n_upc))

    lam_re, lam_im = ssm_lam_re[0], ssm_lam_im[0]
    log_step = ssm_log_step[0].reshape(n_groups, 1)
    abar_re, abar_im, coef_re, coef_im = _ssm_prep(lam_re, lam_im, log_step)
    a_rows = _rows8(abar_re.reshape(1, gp), abar_im.reshape(1, gp))
    coef_rows = _rows8(coef_re.reshape(1, gp), coef_im.reshape(1, gp))
    bt_re = jnp.tile(ssm_b_re[0].transpose(0, 2, 1).reshape(d_ssm, n_state), (1, n_groups))
    bt_im = jnp.tile(ssm_b_im[0].transpose(0, 2, 1).reshape(d_ssm, n_state), (1, n_groups))
    ct_re = jnp.tile(ssm_c_re[0].transpose(0, 2, 1).reshape(gp, n_gch), (1, n_groups))
    ct_im = jnp.tile(ssm_c_im[0].transpose(0, 2, 1).reshape(gp, n_gch), (1, n_groups))
    bblk_re, bblk_im, cblk_re, cblk_im = _ssm_blocks(bt_re, bt_im, ct_re, ct_im, coef_rows)

    head = lax.broadcasted_iota(jnp.int32, (d_ssm, d_ssm), 0), lax.broadcasted_iota(jnp.int32, (d_ssm, d_ssm), 1)
    h16 = jnp.where((head[0] // n_gch) == (head[1] // n_gch), 1.0 / n_gch, 0.0).astype(BF16)
    conv_hd = 64
    h64 = jnp.where((head[0] // conv_hd) == (head[1] // conv_hd), 1.0 / conv_hd, 0.0).astype(BF16)

    v512 = _rows8(ssm_d, glu_b, g_out_ssm, g_out_conv)
    vec1 = _rows8(g_pre_mix, 1.0 + sc1, sh1)
    vd1 = _rows8(g_post_mix, gt1)
    vec2 = _rows8(g_pre_ffn, 1.0 + sc2, sh2)
    vd2 = _rows8(g_post_ffn, gt2)

    proj, bu_re, bu_im, h1b = _mix_in(xs, vec1, w_in_st, bblk_re, bblk_im)
    s_re, s_im = _scan_fwd(a_rows, bu_re, bu_im)
    y1, o_mix, x2 = _mix_out(xs, proj, s_re, s_im, cblk_re, cblk_im, v512, convw_full, glu_full, h16, h64,
                             w_out_full, vd1)
    up, h2b = _ffn_up(x2, vec2, w_up_st)
    actb, ddnb, dout, dhid, vp_dn, loss_blk = _ffn_down(up, fw_full, w_down_full, x2, tgt, vd2)

    dx2, dupb, vp_up, df_rows = _ffn_up_bwd(dhid, up, fw_full, x2, dout, vec2, w_up_st)
    (dob, ycatb, zb, dqb, dy1, g_re, g_im, dcc, dbg, vp_mo, vp5) = _mix_out_bwd(
        dx2, o_mix, y1, proj, cblk_re, cblk_im, v512, convw_full, glu_full, h16, h64, w_out_full, vd1)
    gt_re, gt_im, ga_re8, ga_im8 = _scan_bwd(a_rows, g_re, g_im, s_re, s_im)
    grad_x, dprojb, vp_mi = _mix_in_bwd(gt_re, gt_im, bblk_re, bblk_im, dy1, dcc, dbg, proj, xs, dx2, vec1, v512,
                                        convw_full, w_in_st)

    gw_down = _matmul_tn(actb, ddnb, d_ff, d, BF16, "dw_down").reshape(N_CHIPS, d_ff // N_CHIPS, d)
    gw_up = _matmul_tn(h2b, dupb, d, n_upc, BF16, "dw_up")
    gw_out = _matmul_tn(ycatb, dob, 2 * d_ssm, d, BF16, "dw_out").reshape(N_CHIPS, 2 * d_ssm // N_CHIPS, d)
    gw_glu = _matmul_tn(zb, dqb, d_ssm, d_ssm, BF16, "dw_glu").reshape(N_CHIPS, d_ssm // N_CHIPS, d_ssm)
    gw_in = _matmul_tn(h1b, dprojb, d, w_in.shape[2], BF16, "dw_in")
    d_bre = _matmul_tn(proj, gt_re, d_ssm, gp, F32, "d_bre")[0]
    d_bim = _matmul_tn(proj, gt_im, d_ssm, gp, F32, "d_bim")[0]
    d_cre = _matmul_tn(s_re, dy1, gp, d_ssm, F32, "d_cre")[0]
    d_cim = _matmul_tn(s_im, dy1, gp, d_ssm, F32, "d_cim")[0]

    lane = lax.broadcasted_iota(jnp.int32, (gp, n_state), 0), lax.broadcasted_iota(jnp.int32, (gp, n_state), 1)
    fold_b = jnp.where((lane[0] % n_state) == lane[1], 1.0, 0.0).astype(BF16)
    lane_c = lax.broadcasted_iota(jnp.int32, (d_ssm, n_gch), 0), lax.broadcasted_iota(jnp.int32, (d_ssm, n_gch), 1)
    fold_c = jnp.where((lane_c[0] % n_gch) == lane_c[1], 1.0, 0.0).astype(BF16)
    db_re_f, db_im_f, gc_rows = _ssm_bgrad(d_bre, d_bim, bt_re, bt_im, coef_rows, fold_b)
    dc_re_f, dc_im_f = _ssm_cgrad(d_cre, d_cim, fold_c)
    ga_sum = _ga_rowsum(ga_re8, ga_im8)
    g_lam_re, g_lam_im, g_log_step = _ssm_lamgrad(
        lam_re, lam_im, log_step, abar_re, abar_im, coef_re, coef_im,
        gc_rows[0].reshape(n_groups, n_state), gc_rows[1].reshape(n_groups, n_state),
        ga_sum[0].reshape(n_groups, n_state), ga_sum[1].reshape(n_groups, n_state))
    g_b_re = db_re_f.reshape(n_groups, n_gch, n_state).transpose(0, 2, 1)
    g_b_im = db_im_f.reshape(n_groups, n_gch, n_state).transpose(0, 2, 1)
    g_c_re = dc_re_f.reshape(n_groups, n_state, n_gch).transpose(0, 2, 1)
    g_c_im = dc_im_f.reshape(n_groups, n_state, n_gch).transpose(0, 2, 1)

    dmod = jnp.concatenate([vp_mi[0:1], vp_mi[1:2], vp_mo[0:1], vp_up[0:1], vp_up[1:2], vp_dn[0:1]], axis=1)
    small = [
        ("g_pre_mix", vp_mi[2:3]), ("g_post_mix", vp_mo[1:2]), ("g_pre_ffn", vp_up[2:3]), ("g_post_ffn", vp_dn[1:2]),
        ("ssm_lam_re", g_lam_re), ("ssm_lam_im", g_lam_im), ("ssm_log_step", g_log_step),
        ("ssm_b_re", g_b_re), ("ssm_b_im", g_b_im), ("ssm_c_re", g_c_re), ("ssm_c_im", g_c_im),
        ("ssm_d", vp5[3:4]), ("glu_b", vp5[2:3]), ("g_out_ssm", vp5[0:1]), ("g_out_conv", vp5[1:2]),
        ("conv_w", vp5[4:7]), ("ffn_conv_w", df_rows[0:3]),
    ]
    packed, offsets, row = [], {}, 0
    for name, a in small:
        r = _to_rows(a, d)
        offsets[name] = (row, a.shape)
        packed.append(r)
        row += r.shape[0]
    n_small = -(-row // SUBLANES) * SUBLANES
    packed.append(jnp.zeros((n_small - row, d), F32))
    packed.append(pad8(dmod.reshape(6, d)))
    pack = jnp.concatenate(packed, axis=0)
    gath, sums = _allgather8(pack, n_small + SUBLANES, "reduce_small")
    dmod_all = gath.reshape(N_DEV, n_small + SUBLANES, d)[:, n_small:n_small + 6, :].reshape(N_DEV, 6 * d)
    g_b_ada = sums[n_small:n_small + 6].reshape(1, 6 * d)

    def unpack(name):
        r0, shape = offsets[name]
        size = math.prod(shape)
        nrow = -(-size // d)
        return sums[r0:r0 + nrow].reshape(-1)[:size].reshape(shape)

    dmod_cols = lax.dynamic_slice(dmod_all, (0, chip * n_ada), (N_DEV, n_ada))
    ada = _adamw_ada(c_all, dmod_cols, w_ada[0], m_w_ada[0], v_w_ada[0])

    stacks = [gw_in, gw_glu, gw_out, gw_up, gw_down]
    landed = _scatter_chips(stacks, "scatter_grads")
    partial = [_sum_blocks(s, "sum_" + nm) for s, nm in zip(landed, ("w_in", "glu_w", "w_out", "w_up", "w_down"))]
    theirs = _swap_sibling(partial, "swap_partials")
    big = {}
    for nm, pm, ps, w_, m_, v_ in zip(
            ("w_in", "glu_w", "w_out", "w_up", "w_down"), partial, theirs,
            (w_in, glu_w, w_out, w_up, w_down), (m_w_in, m_glu_w, m_w_out, m_w_up, m_w_down),
            (v_w_in, v_glu_w, v_w_out, v_w_up, v_w_down)):
        big[nm] = _adamw_big(pm, ps, w_[0], m_[0], v_[0], "adamw_" + nm)

    g_small = {name: unpack(name) for name, _ in small}
    g_small["b_ada"] = g_b_ada
    g_small["conv_w"] = lax.dynamic_slice(g_small["conv_w"], (0, chip * conv_w.shape[2]), (3, conv_w.shape[2]))
    g_small["ffn_conv_w"] = lax.dynamic_slice(g_small["ffn_conv_w"], (0, chip * n_upc), (3, n_upc))
    g_small["ssm_log_step"] = g_small["ssm_log_step"].reshape(1, n_groups)
    small_params = {
        "b_ada": (b_ada, m_b_ada, v_b_ada), "g_pre_mix": (g_pre_mix, m_g_pre_mix, v_g_pre_mix),
        "g_post_mix": (g_post_mix, m_g_post_mix, v_g_post_mix), "ssm_lam_re": (ssm_lam_re, m_ssm_lam_re, v_ssm_lam_re),
        "ssm_lam_im": (ssm_lam_im, m_ssm_lam_im, v_ssm_lam_im),
        "ssm_log_step": (ssm_log_step, m_ssm_log_step, v_ssm_log_step),
        "ssm_b_re": (ssm_b_re, m_ssm_b_re, v_ssm_b_re), "ssm_b_im": (ssm_b_im, m_ssm_b_im, v_ssm_b_im),
        "ssm_c_re": (ssm_c_re, m_ssm_c_re, v_ssm_c_re), "ssm_c_im": (ssm_c_im, m_ssm_c_im, v_ssm_c_im),
        "ssm_d": (ssm_d, m_ssm_d, v_ssm_d), "glu_b": (glu_b, m_glu_b, v_glu_b),
        "g_out_ssm": (g_out_ssm, m_g_out_ssm, v_g_out_ssm), "conv_w": (conv_w, m_conv_w, v_conv_w),
        "g_out_conv": (g_out_conv, m_g_out_conv, v_g_out_conv), "g_pre_ffn": (g_pre_ffn, m_g_pre_ffn, v_g_pre_ffn),
        "g_post_ffn": (g_post_ffn, m_g_post_ffn, v_g_post_ffn),
        "ffn_conv_w": (ffn_conv_w, m_ffn_conv_w, v_ffn_conv_w),
    }

    def flat2d(a):
        n = a.size
        return a.reshape(n // 1024, 1024) if n % 1024 == 0 and n > 1024 else a.reshape(-1, a.shape[-1])

    names = list(small_params)
    items = []
    for nm in names:
        w_, m_, v_ = small_params[nm]
        items.append((flat2d(w_[0]), flat2d(g_small[nm].reshape(w_[0].shape)), flat2d(m_[0]), flat2d(v_[0])))
    upd = _adamw_small(items)
    small_out = {}
    for nm, (dl, mo, vo) in zip(names, upd):
        shp = small_params[nm][0].shape
        small_out[nm] = (g_small[nm].reshape(shp), dl.reshape(shp), mo.reshape(shp), vo.reshape(shp))

    loss = lax.psum(loss_blk[0, 0], ("x", "y", "c"))

    order = ["w_ada", "b_ada", "g_pre_mix", "g_post_mix", "w_in", "ssm_lam_re", "ssm_lam_im", "ssm_log_step",
             "ssm_b_re", "ssm_b_im", "ssm_c_re", "ssm_c_im", "ssm_d", "glu_w", "glu_b", "g_out_ssm", "conv_w",
             "g_out_conv", "w_out", "g_pre_ffn", "g_post_ffn", "w_up", "ffn_conv_w", "w_down"]
    results = {"w_ada": tuple(a[None] for a in ada)}
    for nm in big:
        results[nm] = tuple(a[None] for a in big[nm])
    results.update(small_out)
    outs = [loss, grad_x[None]]
    for k in range(4):
        outs += [results[nm][k] for nm in order]
    return tuple(outs)


def _ga_rowsum(ga_re8, ga_im8):
    n = ga_re8.shape[1]

    def body(r_ref, i_ref, o_ref):
        o_ref[...] = jnp.zeros(o_ref.shape, F32)
        o_ref[0:1, :] = _colsum(r_ref[...])
        o_ref[1:2, :] = _colsum(i_ref[...])

    return pl.pallas_call(body, name="ga_rowsum", out_shape=jax.ShapeDtypeStruct((SUBLANES, n), F32))(ga_re8, ga_im8)
```

```python
import functools
import math

import jax
import jax.numpy as jnp
from jax import lax
from jax.experimental import pallas as pl
from jax.experimental.pallas import tpu as pltpu

F32 = jnp.float32
BF16 = jnp.bfloat16
MESH = pl.DeviceIdType.MESH

EPS = 1e-6
LAMBDA_RE_MAX = -1e-4
ADAM_LR = 0.001
ADAM_B1 = 0.9
ADAM_B2 = 0.999
ADAM_EPS = 1e-08
ADAM_WD = 0.01
ADAM_STEP = 10

SUBLANES = 8
N_CHIPS = 4
N_DEV = 8
VMEM_BIG = 56 * 1024 * 1024
VMEM_MID = 40 * 1024 * 1024

TB_MIX = 256
TB_FFN = 128
TB_SCAN = 256
W_SCAN = 256
SCAN_UNROLL = 4
TB_TN = 512


def _cparams(sem=None, vmem=None):
    kw = {}
    if sem is not None:
        kw["dimension_semantics"] = sem
    if vmem is not None:
        kw["vmem_limit_bytes"] = vmem
    return pltpu.CompilerParams(**kw)


def _blk(t, pref):
    return pref if t % pref == 0 else t


def _dot(a, b):
    return jnp.dot(a.astype(BF16), b.astype(BF16), preferred_element_type=F32)


def _dot_nt(a, b):
    return lax.dot_general(a.astype(BF16), b.astype(BF16), (((1,), (1,)), ((), ())),
                           preferred_element_type=F32)


def _dot_tn(a, b):
    return lax.dot_general(a.astype(BF16), b.astype(BF16), (((0,), (0,)), ((), ())),
                           preferred_element_type=F32)


def _sigmoid(x):
    return 1.0 / (1.0 + jnp.exp(-x))


_GELU_K = math.sqrt(2.0 / math.pi)
_GELU_C = 0.044715


def _gelu(x):
    th = jnp.tanh(_GELU_K * (x + _GELU_C * x * x * x))
    return 0.5 * x * (1.0 + th)


def _gelu_grad(x):
    x2 = x * x
    th = jnp.tanh(_GELU_K * (x + _GELU_C * x2 * x))
    return 0.5 * (1.0 + th) + 0.5 * x * (1.0 - th * th) * _GELU_K * (1.0 + 3.0 * _GELU_C * x2)


def _rowmean(x):
    return jnp.mean(x, axis=-1, keepdims=True)


def _colsum(x):
    return jnp.sum(x, axis=0, keepdims=True)


def _split_dot(x, m):
    hi = x.astype(BF16)
    lo = (x - hi.astype(F32)).astype(BF16)
    return (jnp.dot(hi, m, preferred_element_type=F32) + jnp.dot(lo, m, preferred_element_type=F32))


def _split3_dot(x, m):
    hi = x.astype(BF16)
    r1 = x - hi.astype(F32)
    mid = r1.astype(BF16)
    lo = (r1 - mid.astype(F32)).astype(BF16)
    return (jnp.dot(hi, m, preferred_element_type=F32) + jnp.dot(mid, m, preferred_element_type=F32)
            + jnp.dot(lo, m, preferred_element_type=F32))


def _shift_down(x, halo, k):
    r = pltpu.roll(x, k, 0)
    row = lax.broadcasted_iota(jnp.int32, x.shape, 0)
    for j in range(k):
        r = jnp.where(row == j, halo[SUBLANES - k + j:SUBLANES - k + j + 1, :], r)
    return r


def _shift_up(x, halo, k):
    n = x.shape[0]
    r = pltpu.roll(x, n - k, 0)
    row = lax.broadcasted_iota(jnp.int32, x.shape, 0)
    for j in range(k):
        r = jnp.where(row == n - k + j, halo[j:j + 1, :], r)
    return r


def _acc_rows(ref, first, rows):
    @pl.when(first)
    def _():
        ref[...] = jnp.zeros(ref.shape, ref.dtype)
    for j, r in enumerate(rows):
        ref[j:j + 1, :] += r


def _rows(tb, c, col=0):
    return pl.BlockSpec((tb, c), lambda i, col=col: (i, col))


def _full(shape):
    nd = len(shape)
    return pl.BlockSpec(shape, lambda i, nd=nd: (0,) * nd)


def _halo_prev(tb, c, col=0):
    per = tb // SUBLANES
    return pl.BlockSpec((SUBLANES, c), lambda i, col=col: (jnp.maximum(i * per - 1, 0), col))


def _halo_next(tb, c, t, col=0):
    per = tb // SUBLANES
    last = t // SUBLANES - 1
    return pl.BlockSpec((SUBLANES, c), lambda i, col=col: (jnp.minimum((i + 1) * per, last), col))


def _mesh_pos():
    return lax.axis_index("x"), lax.axis_index("y"), lax.axis_index("c")


def _allgather8(x_pad, n_sum, name):
    m_per, n = x_pad.shape

    def body(x_ref, out_ref, sum_ref, send_sems, recv_sems, local_sem):
        x, y, c = _mesh_pos()
        me, sibling = (x, y, c), (x, y, 1 - c)
        chips = [(1 - x, y), (x, 1 - y), (1 - x, 1 - y)]

        def rows(px, py, pc):
            return out_ref.at[pl.ds((4 * px + 2 * py + pc) * m_per, m_per), :]

        def copy(k, block, to, src=None):
            return pltpu.make_async_remote_copy(
                src_ref=rows(*block) if src is None else src, dst_ref=rows(*block),
                send_sem=send_sems.at[k], recv_sem=recv_sems.at[k], device_id=to, device_id_type=MESH)

        mine = pltpu.make_async_copy(x_ref, rows(*me), local_sem)
        mine.start()
        first = [copy(0, me, sibling, src=x_ref)]
        first += [copy(1 + j, me, (*chip, c), src=x_ref) for j, chip in enumerate(chips)]
        for cp in first:
            cp.start()
        passed = [copy(4 + j, (*chip, c), sibling) for j, chip in enumerate(chips)]
        for j, chip in enumerate(chips):
            copy(1 + j, (*chip, c), me).wait_recv()
            passed[j].start()
        copy(0, sibling, me).wait_recv()
        for j, chip in enumerate(chips):
            copy(4 + j, (*chip, 1 - c), me).wait_recv()
        for cp in first + passed:
            cp.wait_send()
        mine.wait()
        acc = out_ref[0:n_sum, :]
        for k in range(1, N_DEV):
            acc = acc + out_ref[k * m_per:k * m_per + n_sum, :]
        sum_ref[...] = acc

    return pl.pallas_call(
        body, name=name,
        out_shape=(jax.ShapeDtypeStruct((N_DEV * m_per, n), F32), jax.ShapeDtypeStruct((n_sum, n), F32)),
        in_specs=[pl.BlockSpec(memory_space=pltpu.VMEM)],
        out_specs=(pl.BlockSpec(memory_space=pltpu.VMEM), pl.BlockSpec(memory_space=pltpu.VMEM)),
        scratch_shapes=[pltpu.SemaphoreType.DMA((7,)), pltpu.SemaphoreType.DMA((7,)), pltpu.SemaphoreType.DMA],
        compiler_params=_cparams(vmem=VMEM_MID),
    )(x_pad)


def _gather_chips(halves, name):
    n_arr = len(halves)

    def body(*refs):
        ins, outs = refs[:n_arr], refs[n_arr:2 * n_arr]
        send_sems, recv_sems, local_sems = refs[2 * n_arr:]
        x, y, c = _mesh_pos()
        sibling = (x, y, 1 - c)
        chips = [(1 - x, y), (x, 1 - y), (1 - x, 1 - y)]
        my_chip = 2 * x + y

        def slot(n, chip_idx, half):
            return outs[n].at[chip_idx, half]

        def send(n, j):
            return pltpu.make_async_remote_copy(
                src_ref=ins[n].at[c], dst_ref=slot(n, my_chip, c),
                send_sem=send_sems.at[n * 3 + j], recv_sem=recv_sems.at[n * 3 + j],
                device_id=(*chips[j], c), device_id_type=MESH)

        def landed(n, j):
            idx = 2 * chips[j][0] + chips[j][1]
            return pltpu.make_async_remote_copy(
                src_ref=ins[n].at[c], dst_ref=slot(n, idx, c),
                send_sem=send_sems.at[n * 3 + j], recv_sem=recv_sems.at[n * 3 + j],
                device_id=(*chips[j], c), device_id_type=MESH)

        def forward(n, j, half):
            idx = 2 * chips[j][0] + chips[j][1]
            k = 3 * n_arr + n * 3 + j
            return pltpu.make_async_remote_copy(
                src_ref=slot(n, idx, half), dst_ref=slot(n, idx, half),
                send_sem=send_sems.at[k], recv_sem=recv_sems.at[k],
                device_id=sibling, device_id_type=MESH)

        locals_ = [pltpu.make_async_copy(ins[n], outs[n].at[my_chip], local_sems.at[n]) for n in range(n_arr)]
        for cp in locals_:
            cp.start()
        sends = [send(n, j) for n in range(n_arr) for j in range(3)]
        for cp in sends:
            cp.start()
        fwds = []
        for n in range(n_arr):
            for j in range(3):
                landed(n, j).wait_recv()
                f = forward(n, j, c)
                f.start()
                fwds.append(f)
        for n in range(n_arr):
            for j in range(3):
                forward(n, j, 1 - c).wait_recv()
        for cp in sends + fwds:
            cp.wait_send()
        for cp in locals_:
            cp.wait()

    any_spec = pl.BlockSpec(memory_space=pl.ANY)
    return pl.pallas_call(
        body, name=name,
        out_shape=tuple(jax.ShapeDtypeStruct((N_CHIPS,) + h.shape, h.dtype) for h in halves),
        in_specs=[any_spec] * n_arr, out_specs=tuple([any_spec] * n_arr),
        scratch_shapes=[pltpu.SemaphoreType.DMA((6 * n_arr,)), pltpu.SemaphoreType.DMA((6 * n_arr,)),
                        pltpu.SemaphoreType.DMA((n_arr,))],
    )(*halves)


_HBM = pl.BlockSpec(memory_space=pltpu.HBM)
_SEM = pl.BlockSpec(memory_space=pltpu.SEMAPHORE)
_EFFECT = pltpu.SideEffectType.DATAFLOW_SIDE_EFFECTING


def _chip_copy(gather, src_ref, land_ref, send, recv, j, arrival):
    x, y, c = _mesh_pos()
    peer = [(1 - x, y), (x, 1 - y), (1 - x, 1 - y)][j]
    peer_chip = 2 * peer[0] + peer[1]
    my_chip = 2 * x + y
    return pltpu.make_async_remote_copy(
        src_ref=src_ref if gather else src_ref.at[peer_chip],
        dst_ref=land_ref.at[peer_chip if arrival else my_chip],
        send_sem=send.at[j], recv_sem=recv.at[j], device_id=(*peer, c), device_id_type=MESH)


def _chips_start(name, gather, srcs, lands):
    n = len(srcs)

    def body(*refs):
        src_refs, land_refs = refs[:n], refs[n:2 * n]
        outs = refs[2 * n:]
        sends, recvs, token = outs[:n], outs[n:2 * n], outs[-1]
        for k in range(n):
            for j in range(3):
                _chip_copy(gather, src_refs[k], land_refs[k], sends[k], recvs[k], j, False).start()
        token[...] = jnp.zeros(token.shape, F32)

    sem = pltpu.SemaphoreType.DMA((3,))
    thru = tuple(pltpu.HBM(a.shape, a.dtype) for a in list(srcs) + list(lands))
    res = pl.pallas_call(
        body, name=name,
        out_shape=(sem,) * (2 * n) + thru + (jax.ShapeDtypeStruct((SUBLANES, 128), F32),),
        in_specs=[_HBM] * (2 * n),
        out_specs=(_SEM,) * (2 * n) + (_HBM,) * (2 * n) + (pl.BlockSpec(memory_space=pltpu.VMEM),),
        input_output_aliases={k: 2 * n + k for k in range(2 * n)},
        compiler_params=pltpu.CompilerParams(has_side_effects=_EFFECT),
    )(*[pltpu.with_memory_space_constraint(a, pltpu.HBM) for a in list(srcs) + list(lands)])
    return res[:n], res[n:2 * n], res[2 * n:3 * n], res[3 * n:4 * n], res[-1]


def _chips_wait(name, gather, sends, recvs, srcs, lands, after):
    n = len(srcs)

    def body(*refs):
        src_refs, land_refs = refs[:n], refs[n:2 * n]
        sends_, recvs_ = refs[2 * n:3 * n], refs[3 * n:4 * n]
        for k in range(n):
            for j in range(3):
                cp = _chip_copy(gather, src_refs[k], land_refs[k], sends_[k], recvs_[k], j, True)
                cp.wait_send()
                cp.wait_recv()

    thru = tuple(pltpu.HBM(a.shape, a.dtype) for a in list(srcs) + list(lands))
    res = pl.pallas_call(
        body, name=name, out_shape=thru,
        in_specs=[_HBM] * (2 * n) + [_SEM] * (2 * n) + [pl.BlockSpec(memory_space=pl.ANY)],
        out_specs=(_HBM,) * (2 * n),
        input_output_aliases={k: k for k in range(2 * n)},
        compiler_params=pltpu.CompilerParams(has_side_effects=_EFFECT),
    )(*srcs, *lands, *sends, *recvs, after)
    return res[n:]


def _landing(own, chip):
    zone = lax.empty((N_CHIPS,) + own.shape, own.dtype)
    return lax.dynamic_update_slice(zone, own[None], (chip,) + (0,) * own.ndim)


def _swap_sibling(arrs, name):
    n_arr = len(arrs)

    def body(*refs):
        ins, outs = refs[:n_arr], refs[n_arr:2 * n_arr]
        send_sems, recv_sems = refs[2 * n_arr:]
        x, y, c = _mesh_pos()
        copies = [pltpu.make_async_remote_copy(
            src_ref=ins[n], dst_ref=outs[n], send_sem=send_sems.at[n], recv_sem=recv_sems.at[n],
            device_id=(x, y, 1 - c), device_id_type=MESH) for n in range(n_arr)]
        for cp in copies:
            cp.start()
        for cp in copies:
            cp.wait()

    any_spec = pl.BlockSpec(memory_space=pl.ANY)
    return pl.pallas_call(
        body, name=name,
        out_shape=tuple(jax.ShapeDtypeStruct(a.shape, a.dtype) for a in arrs),
        in_specs=[any_spec] * n_arr, out_specs=tuple([any_spec] * n_arr),
        scratch_shapes=[pltpu.SemaphoreType.DMA((n_arr,)), pltpu.SemaphoreType.DMA((n_arr,))],
    )(*arrs)


def _mod_shard(c_all, w_ada_sh, b_sh):
    d, n = w_ada_sh.shape
    bn = 512

    def body(c_ref, w_ref, b_ref, o_ref):
        cc = c_ref[...]
        ca = cc * _sigmoid(cc)
        o_ref[...] = _dot(ca, w_ref[...]) + b_ref[...]

    return pl.pallas_call(
        body, name="mod_shard", grid=(n // bn,),
        out_shape=jax.ShapeDtypeStruct((N_DEV, n), F32),
        in_specs=[_full((N_DEV, d)), pl.BlockSpec((d, bn), lambda j: (0, j)), pl.BlockSpec((1, bn), lambda j: (0, j))],
        out_specs=pl.BlockSpec((N_DEV, bn), lambda j: (0, j)),
        compiler_params=_cparams(("parallel",)),
    )(c_all, w_ada_sh, b_sh)


def _ssm_prep(lam_re, lam_im, log_step):
    g, p = lam_re.shape

    def body(lr_ref, li_ref, ls_ref, ar_ref, ai_ref, cr_ref, ci_ref):
        lr = jnp.minimum(lr_ref[...], LAMBDA_RE_MAX)
        li = li_ref[...]
        st = jnp.exp(ls_ref[...])
        mag = jnp.exp(lr * st)
        ar = mag * jnp.cos(li * st)
        ai = mag * jnp.sin(li * st)
        den = lr * lr + li * li
        nr = ar - 1.0
        ar_ref[...] = ar
        ai_ref[...] = ai
        cr_ref[...] = (nr * lr + ai * li) / den
        ci_ref[...] = (ai * lr - nr * li) / den

    sds = jax.ShapeDtypeStruct((g, p), F32)
    return pl.pallas_call(body, name="ssm_prep", out_shape=(sds,) * 4)(lam_re, lam_im, log_step)


def _ssm_blocks(bt_re, bt_im, ct_re, ct_im, coef_rows):
    gh, gp = bt_re.shape
    nb = 4
    cb, rb = gp // nb, gp // nb

    def body(btr, bti, ctr, cti, cf, bre_o, bim_o, cre_o, cim_o):
        j = pl.program_id(0)
        row = lax.broadcasted_iota(jnp.int32, (gh, cb), 0)
        col = lax.broadcasted_iota(jnp.int32, (gh, cb), 1) + j * cb
        mask = (row >> 4) == (col >> 6)
        cr, ci = cf[0:1, :], cf[1:2, :]
        br, bi = btr[...], bti[...]
        bre_o[...] = jnp.where(mask, br * cr - bi * ci, 0.0).astype(BF16)
        bim_o[...] = jnp.where(mask, br * ci + bi * cr, 0.0).astype(BF16)
        row2 = lax.broadcasted_iota(jnp.int32, (rb, gh), 0) + j * rb
        col2 = lax.broadcasted_iota(jnp.int32, (rb, gh), 1)
        mask2 = (row2 >> 6) == (col2 >> 4)
        cre_o[...] = jnp.where(mask2, ctr[...], 0.0).astype(BF16)
        cim_o[...] = jnp.where(mask2, cti[...], 0.0).astype(BF16)

    bspec = pl.BlockSpec((gh, cb), lambda j: (0, j))
    cspec = pl.BlockSpec((rb, gh), lambda j: (j, 0))
    return pl.pallas_call(
        body, name="ssm_blocks", grid=(nb,),
        out_shape=(jax.ShapeDtypeStruct((gh, gp), BF16),) * 2 + (jax.ShapeDtypeStruct((gp, gh), BF16),) * 2,
        in_specs=[bspec, bspec, cspec, cspec, pl.BlockSpec((SUBLANES, cb), lambda j: (0, j))],
        out_specs=(bspec, bspec, cspec, cspec),
        compiler_params=_cparams(("parallel",)),
    )(bt_re, bt_im, ct_re, ct_im, coef_rows)


def _scan_consts(a_ref, reverse):
    w = a_ref.shape[1]
    ar1 = a_ref[0:1, :]
    ai1 = a_ref[1:2, :]
    if reverse:
        ai1 = -ai1
    pr, pi = [ar1], [ai1]
    for _ in range(1, SUBLANES):
        nr = pr[-1] * ar1 - pi[-1] * ai1
        ni = pr[-1] * ai1 + pi[-1] * ar1
        pr.append(nr)
        pi.append(ni)
    row = lax.broadcasted_iota(jnp.int32, (SUBLANES, w), 0)
    dist = (SUBLANES - 1 - row) if reverse else row

    def pick(vals):
        out = jnp.broadcast_to(vals[SUBLANES - 1], (SUBLANES, w))
        for r in range(SUBLANES - 1):
            out = jnp.where(dist == r, vals[r], out)
        return out

    p_r, p_i = pick(pr), pick(pi)
    steps = []
    for k in (1, 2, 4):
        steps.append((k, jnp.where(dist >= k, pr[k - 1], 0.0), jnp.where(dist >= k, pi[k - 1], 0.0)))
    a8 = (jnp.broadcast_to(pr[SUBLANES - 1], (SUBLANES, w)), jnp.broadcast_to(pi[SUBLANES - 1], (SUBLANES, w)))
    return row, p_r, p_i, steps, a8


def _scan_tile(xr, xi, cr, ci, consts, reverse):
    row, p_r, p_i, steps, (a8r, a8i) = consts
    for k, s_r, s_i in steps:
        sh = (SUBLANES - k) if reverse else k
        qr = pltpu.roll(xr, sh, 0)
        qi = pltpu.roll(xi, sh, 0)
        xr, xi = xr + s_r * qr - s_i * qi, xi + s_r * qi + s_i * qr
    outr = xr + p_r * cr - p_i * ci
    outi = xi + p_r * ci + p_i * cr
    e = 0 if reverse else SUBLANES - 1
    er = jnp.broadcast_to(xr[e:e + 1, :], xr.shape)
    ei = jnp.broadcast_to(xi[e:e + 1, :], xi.shape)
    return outr, outi, er + a8r * cr - a8i * ci, ei + a8r * ci + a8i * cr


def _scan_fwd(a_rows, bu_re, bu_im):
    t, n = bu_re.shape
    tb, w = _blk(t, TB_SCAN), W_SCAN
    ntile = tb // SUBLANES

    def body(a_ref, br_ref, bi_ref, sr_ref, si_ref, car, cai):
        @pl.when(pl.program_id(1) == 0)
        def _():
            car[...] = jnp.zeros(car.shape, F32)
            cai[...] = jnp.zeros(cai.shape, F32)
        consts = _scan_consts(a_ref, False)

        def tile(i, carry):
            o = pl.multiple_of(i * SUBLANES, SUBLANES)
            outr, outi, ncr, nci = _scan_tile(br_ref[pl.ds(o, SUBLANES), :], bi_ref[pl.ds(o, SUBLANES), :],
                                              carry[0], carry[1], consts, False)
            sr_ref[pl.ds(o, SUBLANES), :] = outr
            si_ref[pl.ds(o, SUBLANES), :] = outi
            return ncr, nci

        def tiles(i, carry):
            for s in range(SCAN_UNROLL):
                carry = tile(i * SCAN_UNROLL + s, carry)
            return carry

        cr, ci = lax.fori_loop(0, ntile // SCAN_UNROLL, tiles, (car[...], cai[...]))
        car[...] = cr
        cai[...] = ci

    spec = pl.BlockSpec((tb, w), lambda s, k: (k, s))
    sds = jax.ShapeDtypeStruct((t, n), F32)
    return pl.pallas_call(
        body, name="scan_fwd", grid=(n // w, t // tb), out_shape=(sds, sds),
        in_specs=[pl.BlockSpec((SUBLANES, w), lambda s, k: (0, s)), spec, spec], out_specs=(spec, spec),
        scratch_shapes=[pltpu.VMEM((SUBLANES, w), F32), pltpu.VMEM((SUBLANES, w), F32)],
        compiler_params=_cparams(("parallel", "arbitrary")),
    )(a_rows, bu_re, bu_im)


def _scan_bwd(a_rows, g_re, g_im, s_re, s_im):
    t, n = g_re.shape
    tb, w = _blk(t, TB_SCAN), W_SCAN
    ntile = tb // SUBLANES
    nt = t // tb

    def body(a_ref, gr_ref, gi_ref, sr_ref, si_ref, or_ref, oi_ref, gar_ref, gai_ref, car, cai):
        @pl.when(pl.program_id(1) == 0)
        def _():
            car[...] = jnp.zeros(car.shape, F32)
            cai[...] = jnp.zeros(cai.shape, F32)
            gar_ref[...] = jnp.zeros(gar_ref.shape, F32)
            gai_ref[...] = jnp.zeros(gai_ref.shape, F32)
        consts = _scan_consts(a_ref, True)
        row = consts[0]

        def tile(i, carry):
            cr, ci, accr, acci = carry
            o = pl.multiple_of((ntile - 1 - i) * SUBLANES, SUBLANES)
            outr, outi, ncr, nci = _scan_tile(gr_ref[pl.ds(o, SUBLANES), :], gi_ref[pl.ds(o, SUBLANES), :],
                                              cr, ci, consts, True)
            or_ref[pl.ds(o, SUBLANES), :] = outr
            oi_ref[pl.ds(o, SUBLANES), :] = outi
            gnr = jnp.where(row == SUBLANES - 1, cr, pltpu.roll(outr, SUBLANES - 1, 0))
            gni = jnp.where(row == SUBLANES - 1, ci, pltpu.roll(outi, SUBLANES - 1, 0))
            sr = sr_ref[pl.ds(o, SUBLANES), :]
            si = si_ref[pl.ds(o, SUBLANES), :]
            return ncr, nci, accr + sr * gnr + si * gni, acci + sr * gni - si * gnr

        def tiles(i, carry):
            for s in range(SCAN_UNROLL):
                carry = tile(i * SCAN_UNROLL + s, carry)
            return carry

        cr, ci, accr, acci = lax.fori_loop(0, ntile // SCAN_UNROLL, tiles,
                                           (car[...], cai[...], gar_ref[...], gai_ref[...]))
        car[...] = cr
        cai[...] = ci
        gar_ref[...] = accr
        gai_ref[...] = acci

    spec = pl.BlockSpec((tb, w), lambda s, k: (nt - 1 - k, s))
    aspec = pl.BlockSpec((SUBLANES, w), lambda s, k: (0, s))
    sds = jax.ShapeDtypeStruct((t, n), F32)
    asds = jax.ShapeDtypeStruct((SUBLANES, n), F32)
    return pl.pallas_call(
        body, name="scan_bwd", grid=(n // w, nt), out_shape=(sds, sds, asds, asds),
        in_specs=[aspec, spec, spec, spec, spec], out_specs=(spec, spec, aspec, aspec),
        scratch_shapes=[pltpu.VMEM((SUBLANES, w), F32), pltpu.VMEM((SUBLANES, w), F32)],
        compiler_params=_cparams(("parallel", "arbitrary")),
    )(a_rows, g_re, g_im, s_re, s_im)


def _mix_in(x, vec, w_in_st, b_re, b_im):
    t, d = x.shape
    ns, _, nc = w_in_st.shape
    dssm, nstate = b_re.shape
    tb = _blk(t, TB_MIX)

    def body(x_ref, vec_ref, w_ref, bre_ref, bim_ref, proj_ref, bur_ref, bui_ref, h1_ref):
        xv = x_ref[...]
        r = lax.rsqrt(_rowmean(xv * xv) + EPS)
        h = xv * r * vec_ref[0:1, :] * vec_ref[1:2, :] + vec_ref[2:3, :]
        hb = h.astype(BF16)
        h1_ref[...] = hb
        u = None
        for j in range(ns):
            pj = jnp.dot(hb, w_ref[j], preferred_element_type=F32)
            proj_ref[:, j * nc:(j + 1) * nc] = pj
            if j == 0:
                u = pj
        ub = u.astype(BF16)
        bur_ref[...] = jnp.dot(ub, bre_ref[...], preferred_element_type=F32)
        bui_ref[...] = jnp.dot(ub, bim_ref[...], preferred_element_type=F32)

    return pl.pallas_call(
        body, name="mix_in", grid=(t // tb,),
        out_shape=(jax.ShapeDtypeStruct((t, ns * nc), F32), jax.ShapeDtypeStruct((t, nstate), F32),
                   jax.ShapeDtypeStruct((t, nstate), F32), jax.ShapeDtypeStruct((t, d), BF16)),
        in_specs=[_rows(tb, d), _full((SUBLANES, d)), _full(w_in_st.shape), _full(b_re.shape), _full(b_im.shape)],
        out_specs=(_rows(tb, ns * nc), _rows(tb, nstate), _rows(tb, nstate), _rows(tb, d)),
        compiler_params=_cparams(("parallel",), VMEM_BIG),
    )(x, vec, w_in_st, b_re, b_im)


def _head_ms(y, h_ref):
    return _split_dot(y * y, h_ref[...])


def _conv3(x, halo, w_ref):
    return w_ref[0:1, :] * _shift_down(x, halo, 2) + w_ref[1:2, :] * _shift_down(x, halo, 1) + w_ref[2:3, :] * x


def _mix_out(x, proj, s_re, s_im, c_re, c_im, v512, convw, glu_w, h16, h64, w_out, vd):
    t, d = x.shape
    dh = c_re.shape[1]
    nstate = s_re.shape[1]
    tb = _blk(t, TB_MIX)

    def body(x_ref, u_ref, bg_ref, cg_ref, v_ref, cgh_ref, vh_ref, sr_ref, si_ref, cre_ref, cim_ref, p_ref,
             cw_ref, gw_ref, h16_ref, h64_ref, wo_ref, vd_ref, y1_ref, o_ref, x2_ref):
        i = pl.program_id(0)
        u = u_ref[...]
        ys = _dot(sr_ref[...], cre_ref[...]) - _dot(si_ref[...], cim_ref[...])
        y1 = ys + p_ref[0:1, :] * u
        y1_ref[...] = y1
        z = _gelu(y1)
        q = _dot(z, gw_ref[...]) + p_ref[1:2, :]
        ya = z * _sigmoid(q)
        na = ya * lax.rsqrt(_head_ms(ya, h16_ref) + EPS) * p_ref[2:3, :]
        cv = cg_ref[...] * v_ref[...]
        cvh = jnp.where(i > 0, cgh_ref[...] * vh_ref[...], 0.0)
        yb = bg_ref[...] * _conv3(cv, cvh, cw_ref)
        nb = yb * lax.rsqrt(_head_ms(yb, h64_ref) + EPS) * p_ref[3:4, :]
        o = _dot(na, wo_ref[0:dh, :]) + _dot(nb, wo_ref[dh:2 * dh, :])
        o_ref[...] = o
        on = o * lax.rsqrt(_rowmean(o * o) + EPS) * vd_ref[0:1, :]
        x2_ref[...] = x_ref[...] + vd_ref[1:2, :] * on

    return pl.pallas_call(
        body, name="mix_out", grid=(t // tb,),
        out_shape=(jax.ShapeDtypeStruct((t, dh), F32), jax.ShapeDtypeStruct((t, d), F32),
                   jax.ShapeDtypeStruct((t, d), F32)),
        in_specs=[_rows(tb, d), _rows(tb, dh, 0), _rows(tb, dh, 1), _rows(tb, dh, 2), _rows(tb, dh, 3),
                  _halo_prev(tb, dh, 2), _halo_prev(tb, dh, 3), _rows(tb, nstate), _rows(tb, nstate),
                  _full(c_re.shape), _full(c_im.shape), _full(v512.shape), _full(convw.shape), _full(glu_w.shape),
                  _full(h16.shape), _full(h64.shape), _full(w_out.shape), _full(vd.shape)],
        out_specs=(_rows(tb, dh), _rows(tb, d), _rows(tb, d)),
        compiler_params=_cparams(("parallel",), VMEM_BIG),
    )(x, proj, proj, proj, proj, proj, proj, s_re, s_im, c_re, c_im, v512, convw, glu_w, h16, h64, w_out, vd)


def _ffn_up(x2, vec, w_up_st):
    t, d = x2.shape
    ns, _, nc = w_up_st.shape
    tb = _blk(t, TB_FFN)

    def body(x_ref, vec_ref, w_ref, up_ref, h2_ref):
        xv = x_ref[...]
        r = lax.rsqrt(_rowmean(xv * xv) + EPS)
        h = xv * r * vec_ref[0:1, :] * vec_ref[1:2, :] + vec_ref[2:3, :]
        hb = h.astype(BF16)
        h2_ref[...] = hb
        for j in range(ns):
            up_ref[:, j * nc:(j + 1) * nc] = jnp.dot(hb, w_ref[j], preferred_element_type=F32)

    return pl.pallas_call(
        body, name="ffn_up", grid=(t // tb,),
        out_shape=(jax.ShapeDtypeStruct((t, ns * nc), F32), jax.ShapeDtypeStruct((t, d), BF16)),
        in_specs=[_rows(tb, d), _full((SUBLANES, d)), _full(w_up_st.shape)],
        out_specs=(_rows(tb, ns * nc), _rows(tb, d)),
        compiler_params=_cparams(("parallel",), VMEM_BIG),
    )(x2, vec, w_up_st)


def _ffn_down(up, fw, w_down, x2, tgt, vd):
    t, nh = up.shape
    dff, d = w_down.shape
    tb = _blk(t, TB_FFN)
    inv_d = 1.0 / d

    def body(up_ref, uph_ref, fw_ref, wd_ref, x2_ref, tgt_ref, vd_ref,
             act_ref, ddn_ref, dout_ref, dhid_ref, vec_ref, loss_ref):
        i = pl.program_id(0)
        up_v = up_ref[...]
        uph = jnp.where(i > 0, uph_ref[...], 0.0)
        hid = _conv3(up_v, uph, fw_ref)
        a = hid[:, :dff]
        vv = hid[:, dff:]
        sg = _sigmoid(a)
        si = a * sg
        actb = (si * vv).astype(BF16)
        act_ref[...] = actb
        dn = jnp.dot(actb, wd_ref[...], preferred_element_type=F32)
        r3 = lax.rsqrt(_rowmean(dn * dn) + EPS)
        xn = dn * r3
        g = vd_ref[0:1, :]
        gt2 = vd_ref[1:2, :]
        dnn = xn * g
        diff = x2_ref[...] + gt2 * dnn - tgt_ref[...]
        part = 0.5 * inv_d * jnp.sum(diff * diff)

        @pl.when(i == 0)
        def _():
            loss_ref[...] = jnp.zeros(loss_ref.shape, F32)
        loss_ref[...] += part
        dout = diff * inv_d
        dout_ref[...] = dout
        ddnn = dout * gt2
        _acc_rows(vec_ref, i == 0, [_colsum(dout * dnn), _colsum(ddnn * xn)])
        dxn = ddnn * g
        ddn = r3 * (dxn - xn * _rowmean(dxn * xn))
        ddnb = ddn.astype(BF16)
        ddn_ref[...] = ddnb
        dact = lax.dot_general(ddnb, wd_ref[...], (((1,), (1,)), ((), ())), preferred_element_type=F32)
        dhid_ref[:, :dff] = dact * vv * sg * (1.0 + a * (1.0 - sg))
        dhid_ref[:, dff:] = dact * si

    return pl.pallas_call(
        body, name="ffn_down", grid=(t // tb,),
        out_shape=(jax.ShapeDtypeStruct((t, dff), BF16), jax.ShapeDtypeStruct((t, d), BF16),
                   jax.ShapeDtypeStruct((t, d), F32), jax.ShapeDtypeStruct((t, nh), F32),
                   jax.ShapeDtypeStruct((SUBLANES, d), F32), jax.ShapeDtypeStruct((SUBLANES, 128), F32)),
        in_specs=[_rows(tb, nh), _halo_prev(tb, nh), _full(fw.shape), _full(w_down.shape), _rows(tb, d),
                  _rows(tb, d), _full(vd.shape)],
        out_specs=(_rows(tb, dff), _rows(tb, d), _rows(tb, d), _rows(tb, nh), _full((SUBLANES, d)),
                   _full((SUBLANES, 128))),
        compiler_params=_cparams(("arbitrary",), VMEM_BIG),
    )(up, up, fw, w_down, x2, tgt, vd)


def _ffn_up_bwd(dhid, up, fw, x2, dout, vec, w_up_st):
    t, nh = dhid.shape
    d = x2.shape[1]
    ns, _, nc = w_up_st.shape
    tb = _blk(t, TB_FFN)
    nblk = t // tb

    def body(dh_ref, dhn_ref, up_ref, fw_ref, x2_ref, dout_ref, vec_ref, w_ref,
             dx2_ref, dup_ref, vp_ref, df_ref):
        i = pl.program_id(0)
        dh = dh_ref[...]
        dhn = jnp.where(i < nblk - 1, dhn_ref[...], 0.0)
        dh1 = _shift_up(dh, dhn, 1)
        dh2 = _shift_up(dh, dhn, 2)
        dup = fw_ref[2:3, :] * dh + fw_ref[1:2, :] * dh1 + fw_ref[0:1, :] * dh2
        up_v = up_ref[...]
        _acc_rows(df_ref, i == 0, [_colsum(dh2 * up_v), _colsum(dh1 * up_v), _colsum(dh * up_v)])
        dupb = dup.astype(BF16)
        dup_ref[...] = dupb
        dh2 = None
        for j in range(ns):
            pj = lax.dot_general(dupb[:, j * nc:(j + 1) * nc], w_ref[j], (((1,), (1,)), ((), ())),
                                 preferred_element_type=F32)
            dh2 = pj if dh2 is None else dh2 + pj
        xv = x2_ref[...]
        r = lax.rsqrt(_rowmean(xv * xv) + EPS)
        xn = xv * r
        g = vec_ref[0:1, :]
        hg = xn * g
        dhg = dh2 * vec_ref[1:2, :]
        _acc_rows(vp_ref, i == 0, [_colsum(dh2), _colsum(dh2 * hg), _colsum(dhg * xn)])
        dxn = dhg * g
        dx2_ref[...] = dout_ref[...] + r * (dxn - xn * _rowmean(dxn * xn))

    return pl.pallas_call(
        body, name="ffn_up_bwd", grid=(nblk,),
        out_shape=(jax.ShapeDtypeStruct((t, d), F32), jax.ShapeDtypeStruct((t, nh), BF16),
                   jax.ShapeDtypeStruct((SUBLANES, d), F32), jax.ShapeDtypeStruct((SUBLANES, nh), F32)),
        in_specs=[_rows(tb, nh), _halo_next(tb, nh, t), _rows(tb, nh), _full(fw.shape),
                  _rows(tb, d), _rows(tb, d), _full(vec.shape), _full(w_up_st.shape)],
        out_specs=(_rows(tb, d), _rows(tb, nh), _full((SUBLANES, d)), _full((SUBLANES, nh))),
        compiler_params=_cparams(("arbitrary",), VMEM_BIG),
    )(dhid, dhid, up, fw, x2, dout, vec, w_up_st)


def _mix_out_bwd(dx2, o, y1, proj, c_re, c_im, v512, convw, glu_w, h16, h64, w_out, vd):
    t, d = dx2.shape
    dh = y1.shape[1]
    nstate = c_re.shape[0]
    tb = _blk(t, TB_MIX)

    def body(dx2_ref, o_ref, y1_ref, u_ref, bg_ref, cg_ref, v_ref, cgh_ref, vh_ref, cre_ref, cim_ref, p_ref,
             cw_ref, gw_ref, h16_ref, h64_ref, wo_ref, vd_ref,
             do_ref, ycat_ref, z_ref, dq_ref, dy1_ref, gr_ref, gi_ref, dcc_ref, dbg_ref, vpd_ref, vp5_ref):
        i = pl.program_id(0)
        first = i == 0
        ov = o_ref[...]
        ro = lax.rsqrt(_rowmean(ov * ov) + EPS)
        on_ = ov * ro
        g = vd_ref[0:1, :]
        dx2v = dx2_ref[...]
        don = dx2v * vd_ref[1:2, :]
        _acc_rows(vpd_ref, first, [_colsum(dx2v * on_ * g), _colsum(don * on_)])
        dxn = don * g
        dob = (ro * (dxn - on_ * _rowmean(dxn * on_))).astype(BF16)
        do_ref[...] = dob
        dyc_a = lax.dot_general(dob, wo_ref[0:dh, :], (((1,), (1,)), ((), ())), preferred_element_type=F32)
        dyc_b = lax.dot_general(dob, wo_ref[dh:2 * dh, :], (((1,), (1,)), ((), ())), preferred_element_type=F32)
        y1v = y1_ref[...]
        u = u_ref[...]
        z = _gelu(y1v)
        zb = z.astype(BF16)
        z_ref[...] = zb
        sg = _sigmoid(jnp.dot(zb, gw_ref[...], preferred_element_type=F32) + p_ref[1:2, :])
        ya = z * sg
        ra = lax.rsqrt(_head_ms(ya, h16_ref) + EPS)
        yan = ya * ra
        ga = p_ref[2:3, :]
        ycat_ref[:, 0:dh] = (yan * ga).astype(BF16)
        dyn = dyc_a * ga
        dya = ra * (dyn - yan * _split_dot(dyn * yan, h16_ref[...]))
        dq = dya * z * sg * (1.0 - sg)
        dqb = dq.astype(BF16)
        dq_ref[...] = dqb
        dz = dya * sg + lax.dot_general(dqb, gw_ref[...], (((1,), (1,)), ((), ())), preferred_element_type=F32)
        dy1 = dz * _gelu_grad(y1v)
        dy1_ref[...] = dy1
        dy1b = dy1.astype(BF16)
        gr_ref[...] = lax.dot_general(dy1b, cre_ref[...], (((1,), (1,)), ((), ())), preferred_element_type=F32)
        gi_ref[...] = -lax.dot_general(dy1b, cim_ref[...], (((1,), (1,)), ((), ())), preferred_element_type=F32)
        bg = bg_ref[...]
        cv = cg_ref[...] * v_ref[...]
        cvh = jnp.where(i > 0, cgh_ref[...] * vh_ref[...], 0.0)
        cv1 = _shift_down(cv, cvh, 1)
        cv2 = _shift_down(cv, cvh, 2)
        cc = cw_ref[0:1, :] * cv2 + cw_ref[1:2, :] * cv1 + cw_ref[2:3, :] * cv
        yb = bg * cc
        rb = lax.rsqrt(_head_ms(yb, h64_ref) + EPS)
        ybn = yb * rb
        gb = p_ref[3:4, :]
        ycat_ref[:, dh:2 * dh] = (ybn * gb).astype(BF16)
        dynb = dyc_b * gb
        dyb = rb * (dynb - ybn * _split_dot(dynb * ybn, h64_ref[...]))
        dcc = dyb * bg
        dbg_ref[...] = dyb * cc
        dcc_ref[...] = dcc
        _acc_rows(vp5_ref, first, [_colsum(dyc_a * yan), _colsum(dyc_b * ybn), _colsum(dq), _colsum(dy1 * u),
                                   _colsum(dcc * cv2), _colsum(dcc * cv1), _colsum(dcc * cv)])

    return pl.pallas_call(
        body, name="mix_out_bwd", grid=(t // tb,),
        out_shape=(jax.ShapeDtypeStruct((t, d), BF16), jax.ShapeDtypeStruct((t, 2 * dh), BF16),
                   jax.ShapeDtypeStruct((t, dh), BF16), jax.ShapeDtypeStruct((t, dh), BF16),
                   jax.ShapeDtypeStruct((t, dh), F32), jax.ShapeDtypeStruct((t, nstate), F32),
                   jax.ShapeDtypeStruct((t, nstate), F32), jax.ShapeDtypeStruct((t, dh), F32),
                   jax.ShapeDtypeStruct((t, dh), F32), jax.ShapeDtypeStruct((SUBLANES, d), F32),
                   jax.ShapeDtypeStruct((SUBLANES, dh), F32)),
        in_specs=[_rows(tb, d), _rows(tb, d), _rows(tb, dh), _rows(tb, dh, 0), _rows(tb, dh, 1), _rows(tb, dh, 2),
                  _rows(tb, dh, 3), _halo_prev(tb, dh, 2), _halo_prev(tb, dh, 3), _full(c_re.shape), _full(c_im.shape),
                  _full(v512.shape), _full(convw.shape), _full(glu_w.shape), _full(h16.shape), _full(h64.shape),
                  _full(w_out.shape), _full(vd.shape)],
        out_specs=(_rows(tb, d), _rows(tb, 2 * dh), _rows(tb, dh), _rows(tb, dh), _rows(tb, dh), _rows(tb, nstate),
                   _rows(tb, nstate), _rows(tb, dh), _rows(tb, dh), _full((SUBLANES, d)), _full((SUBLANES, dh))),
        compiler_params=_cparams(("arbitrary",), VMEM_BIG),
    )(dx2, o, y1, proj, proj, proj, proj, proj, proj, c_re, c_im, v512, convw, glu_w, h16, h64, w_out, vd)


def _mix_in_bwd(gt_re, gt_im, b_re, b_im, dy1, dcc, dbg, proj, x, dx2, vec, v512, convw, w_in_st):
    t, d = x.shape
    dh = dy1.shape[1]
    nstate = gt_re.shape[1]
    ns, _, nc = w_in_st.shape
    tb = _blk(t, TB_MIX)
    nblk = t // tb

    def body(gr_ref, gi_ref, bre_ref, bim_ref, dy1_ref, dcc_ref, dccn_ref, dbg_ref, cg_ref, v_ref, x_ref, dx2_ref,
             vec_ref, p_ref, cw_ref, w_ref, gx_ref, dproj_ref, vp_ref):
        i = pl.program_id(0)
        du = (dy1_ref[...] * p_ref[0:1, :]
              + lax.dot_general(gr_ref[...].astype(BF16), bre_ref[...], (((1,), (1,)), ((), ())),
                                preferred_element_type=F32)
              + lax.dot_general(gi_ref[...].astype(BF16), bim_ref[...], (((1,), (1,)), ((), ())),
                                preferred_element_type=F32))
        dcc = dcc_ref[...]
        dccn = jnp.where(i < nblk - 1, dccn_ref[...], 0.0)
        dcv = (cw_ref[2:3, :] * dcc + cw_ref[1:2, :] * _shift_up(dcc, dccn, 1)
               + cw_ref[0:1, :] * _shift_up(dcc, dccn, 2))
        parts = [du, dbg_ref[...], dcv * v_ref[...], dcv * cg_ref[...]]
        dh1 = None
        for j in range(ns):
            pb = parts[j].astype(BF16)
            dproj_ref[:, j * nc:(j + 1) * nc] = pb
            pj = lax.dot_general(pb, w_ref[j], (((1,), (1,)), ((), ())), preferred_element_type=F32)
            dh1 = pj if dh1 is None else dh1 + pj
        xv = x_ref[...]
        r = lax.rsqrt(_rowmean(xv * xv) + EPS)
        xn = xv * r
        g = vec_ref[0:1, :]
        hg = xn * g
        dhg = dh1 * vec_ref[1:2, :]
        _acc_rows(vp_ref, i == 0, [_colsum(dh1), _colsum(dh1 * hg), _colsum(dhg * xn)])
        dxn = dhg * g
        gx_ref[...] = dx2_ref[...] + r * (dxn - xn * _rowmean(dxn * xn))

    assert nc == dh and ns == 4
    return pl.pallas_call(
        body, name="mix_in_bwd", grid=(nblk,),
        out_shape=(jax.ShapeDtypeStruct((t, d), F32), jax.ShapeDtypeStruct((t, ns * nc), BF16),
                   jax.ShapeDtypeStruct((SUBLANES, d), F32)),
        in_specs=[_rows(tb, nstate), _rows(tb, nstate), _full(b_re.shape), _full(b_im.shape), _rows(tb, dh),
                  _rows(tb, dh), _halo_next(tb, dh, t), _rows(tb, dh), _rows(tb, dh, 2), _rows(tb, dh, 3),
                  _rows(tb, d), _rows(tb, d), _full(vec.shape), _full(v512.shape), _full(convw.shape),
                  _full(w_in_st.shape)],
        out_specs=(_rows(tb, d), _rows(tb, ns * nc), _full((SUBLANES, d))),
        compiler_params=_cparams(("arbitrary",), VMEM_BIG),
    )(gt_re, gt_im, b_re, b_im, dy1, dcc, dcc, dbg, proj, proj, x, dx2, vec, v512, convw, w_in_st)


def _matmul_tn(a, b, m, bn, out_dtype, name, a_col=0, after=None):
    t = a.shape[0]
    n = b.shape[1]
    bt = _blk(t, TB_TN)
    nk = t // bt
    extra = [] if after is None else [after]

    def body(a_ref, b_ref, *rest):
        o_ref, acc_ref = rest[-2:]
        k = pl.program_id(1)

        @pl.when(k == 0)
        def _():
            acc_ref[...] = jnp.zeros(acc_ref.shape, F32)
        acc_ref[...] += _dot_tn(a_ref[...], b_ref[...])

        @pl.when(k == nk - 1)
        def _():
            o_ref[...] = acc_ref[...].astype(out_dtype)

    return pl.pallas_call(
        body, name=name, grid=(n // bn, nk),
        out_shape=jax.ShapeDtypeStruct((n // bn, m, bn), out_dtype),
        in_specs=[pl.BlockSpec((bt, m), lambda j, k: (k, a_col)), pl.BlockSpec((bt, bn), lambda j, k: (k, j))]
        + [pl.BlockSpec(memory_space=pl.ANY)] * len(extra),
        out_specs=pl.BlockSpec((None, m, bn), lambda j, k: (j, 0, 0)),
        scratch_shapes=[pltpu.VMEM((m, bn), F32)],
        compiler_params=_cparams(("parallel", "arbitrary"), VMEM_BIG),
    )(a, b, *extra)


def _ssm_bgrad(d_bre, d_bim, bt_re, bt_im, rows_in, fold):
    gh, gp = d_bre.shape
    nb = 4
    cb = gp // nb
    p = fold.shape[1]

    def body(dr_ref, di_ref, br_ref, bi_ref, rin_ref, f_ref, dbr_ref, dbi_ref, rout_ref):
        j = pl.program_id(0)
        row = lax.broadcasted_iota(jnp.int32, (gh, cb), 0)
        col = lax.broadcasted_iota(jnp.int32, (gh, cb), 1) + j * cb
        mask = (row >> 4) == (col >> 6)
        gr = jnp.where(mask, dr_ref[...], 0.0)
        gi = jnp.where(mask, di_ref[...], 0.0)
        cr, ci = rin_ref[0:1, :], rin_ref[1:2, :]

        @pl.when(j == 0)
        def _():
            dbr_ref[...] = jnp.zeros(dbr_ref.shape, F32)
            dbi_ref[...] = jnp.zeros(dbi_ref.shape, F32)
        dbr_ref[...] += _split3_dot(cr * gr + ci * gi, f_ref[...])
        dbi_ref[...] += _split3_dot(cr * gi - ci * gr, f_ref[...])
        br, bi = br_ref[...], bi_ref[...]
        rout_ref[...] = jnp.zeros(rout_ref.shape, F32)
        rout_ref[0:1, :] = _colsum(br * gr + bi * gi)
        rout_ref[1:2, :] = _colsum(br * gi - bi * gr)

    bspec = pl.BlockSpec((gh, cb), lambda j: (0, j))
    rspec = pl.BlockSpec((SUBLANES, cb), lambda j: (0, j))
    return pl.pallas_call(
        body, name="ssm_bgrad", grid=(nb,),
        out_shape=(jax.ShapeDtypeStruct((gh, p), F32), jax.ShapeDtypeStruct((gh, p), F32),
                   jax.ShapeDtypeStruct((SUBLANES, gp), F32)),
        in_specs=[bspec, bspec, bspec, bspec, rspec, pl.BlockSpec((cb, p), lambda j: (j, 0))],
        out_specs=(_full((gh, p)), _full((gh, p)), rspec),
        compiler_params=_cparams(("arbitrary",)),
    )(d_bre, d_bim, bt_re, bt_im, rows_in, fold)


def _ssm_cgrad(d_cre, d_cim, fold):
    gp, gh = d_cre.shape
    nb = 4
    rb = gp // nb
    h = fold.shape[1]

    def body(dr_ref, di_ref, f_ref, cr_ref, ci_ref):
        j = pl.program_id(0)
        row = lax.broadcasted_iota(jnp.int32, (rb, gh), 0) + j * rb
        col = lax.broadcasted_iota(jnp.int32, (rb, gh), 1)
        mask = (row >> 6) == (col >> 4)
        cr_ref[...] = _split3_dot(jnp.where(mask, dr_ref[...], 0.0), f_ref[...])
        ci_ref[...] = -_split3_dot(jnp.where(mask, di_ref[...], 0.0), f_ref[...])

    cspec = pl.BlockSpec((rb, gh), lambda j: (j, 0))
    ospec = pl.BlockSpec((rb, h), lambda j: (j, 0))
    return pl.pallas_call(
        body, name="ssm_cgrad", grid=(nb,),
        out_shape=(jax.ShapeDtypeStruct((gp, h), F32),) * 2,
        in_specs=[cspec, cspec, _full(fold.shape)], out_specs=(ospec, ospec),
        compiler_params=_cparams(("parallel",)),
    )(d_cre, d_cim, fold)


def _ssm_lamgrad(lam_re, lam_im, log_step, abar_re, abar_im, coef_re, coef_im, gc_re, gc_im, ga_re, ga_im):
    g, p = lam_re.shape

    def body(lr_ref, li_ref, ls_ref, ar_ref, ai_ref, cr_ref, ci_ref, gcr_ref, gci_ref, gar_ref, gai_ref,
             dlr_ref, dli_ref, dls_ref):
        lam_raw = lr_ref[...]
        lr = jnp.minimum(lam_raw, LAMBDA_RE_MAX)
        li = li_ref[...]
        st = jnp.exp(ls_ref[...])
        den = lr * lr + li * li
        gcr, gci = gcr_ref[...], gci_ref[...]
        gab_r = gar_ref[...] + (lr * gcr - li * gci) / den
        gab_i = gai_ref[...] + (lr * gci + li * gcr) / den
        cr, ci = cr_ref[...], ci_ref[...]
        wr = -(cr * lr + ci * li) / den
        wi = -(ci * lr - cr * li) / den
        gl_r = wr * gcr + wi * gci
        gl_i = wr * gci - wi * gcr
        ar, ai = ar_ref[...], ai_ref[...]
        gw_r = ar * gab_r + ai * gab_i
        gw_i = ar * gab_i - ai * gab_r
        gl_r = gl_r + st * gw_r
        gl_i = gl_i + st * gw_i
        pass_through = jnp.where(lam_raw < LAMBDA_RE_MAX, 1.0, jnp.where(lam_raw == LAMBDA_RE_MAX, 0.5, 0.0))
        dlr_ref[...] = gl_r * pass_through
        dli_ref[...] = gl_i
        dls_ref[...] = st * jnp.sum(lr * gw_r + li * gw_i, axis=1, keepdims=True)

    sds = jax.ShapeDtypeStruct((g, p), F32)
    return pl.pallas_call(body, name="ssm_lamgrad", out_shape=(sds, sds, jax.ShapeDtypeStruct((g, 1), F32)))(
        lam_re, lam_im, log_step, abar_re, abar_im, coef_re, coef_im, gc_re, gc_im, ga_re, ga_im)


def _adamw_math(w, g, m, v):
    m = ADAM_B1 * m + (1.0 - ADAM_B1) * g
    v = ADAM_B2 * v + (1.0 - ADAM_B2) * (g * g)
    m_hat = m / (1.0 - ADAM_B1 ** ADAM_STEP)
    v_hat = v / (1.0 - ADAM_B2 ** ADAM_STEP)
    delta = -ADAM_LR * (m_hat / (jnp.sqrt(v_hat) + ADAM_EPS) + ADAM_WD * w)
    return delta, m, v


def _adamw_big(p_mine, p_sib, w, m, v, name):
    r, c = w.shape
    rb = 64 if r % 64 == 0 else r

    def body(a_ref, b_ref, w_ref, m_ref, v_ref, g_ref, d_ref, mo_ref, vo_ref):
        g = a_ref[...] + b_ref[...]
        g_ref[...] = g
        d_ref[...], mo_ref[...], vo_ref[...] = _adamw_math(w_ref[...], g, m_ref[...], v_ref[...])

    spec = pl.BlockSpec((rb, c), lambda i: (i, 0))
    sds = jax.ShapeDtypeStruct((r, c), F32)
    return pl.pallas_call(
        body, name=name, grid=(r // rb,), out_shape=(sds,) * 4, in_specs=[spec] * 5, out_specs=(spec,) * 4,
        compiler_params=_cparams(("parallel",)),
    )(p_mine, p_sib, w, m, v)


def _sum_blocks(stack, name):
    n, r, c = stack.shape
    rb = 64 if r % 64 == 0 else r

    def body(s_ref, o_ref):
        acc = s_ref[0].astype(F32)
        for k in range(1, n):
            acc = acc + s_ref[k].astype(F32)
        o_ref[...] = acc

    return pl.pallas_call(
        body, name=name, grid=(r // rb,), out_shape=jax.ShapeDtypeStruct((r, c), F32),
        in_specs=[pl.BlockSpec((n, rb, c), lambda i: (0, i, 0))], out_specs=pl.BlockSpec((rb, c), lambda i: (i, 0)),
        compiler_params=_cparams(("parallel",)),
    )(stack)


def _adamw_ada(c_all, dmod_cols, w, m, v):
    d, n = w.shape
    bn = 512

    def body(c_ref, dm_ref, w_ref, m_ref, v_ref, g_ref, d_ref, mo_ref, vo_ref):
        cc = c_ref[...]
        g = _dot_tn(cc * _sigmoid(cc), dm_ref[...])
        g_ref[...] = g
        d_ref[...], mo_ref[...], vo_ref[...] = _adamw_math(w_ref[...], g, m_ref[...], v_ref[...])

    spec = pl.BlockSpec((d, bn), lambda j: (0, j))
    sds = jax.ShapeDtypeStruct((d, n), F32)
    return pl.pallas_call(
        body, name="adamw_ada", grid=(n // bn,), out_shape=(sds,) * 4,
        in_specs=[_full((N_DEV, d)), pl.BlockSpec((N_DEV, bn), lambda j: (0, j)), spec, spec, spec],
        out_specs=(spec,) * 4, compiler_params=_cparams(("parallel",)),
    )(c_all, dmod_cols, w, m, v)


def _adamw_small(items):
    n = len(items)

    def body(*refs):
        ins, outs = refs[:4 * n], refs[4 * n:]
        for k in range(n):
            w_ref, g_ref, m_ref, v_ref = ins[4 * k:4 * k + 4]
            outs[3 * k][...], outs[3 * k + 1][...], outs[3 * k + 2][...] = _adamw_math(
                w_ref[...], g_ref[...], m_ref[...], v_ref[...])

    flat = [a for it in items for a in it]
    out_shape = tuple(jax.ShapeDtypeStruct(it[0].shape, F32) for it in items for _ in range(3))
    res = pl.pallas_call(body, name="adamw_small", out_shape=out_shape)(*flat)
    return [tuple(res[3 * k:3 * k + 3]) for k in range(n)]


def _rows8(*rows):
    c = rows[0].shape[-1]
    pad = jnp.zeros((SUBLANES - len(rows), c), F32)
    return jnp.concatenate([r.reshape(1, c) for r in rows] + [pad], axis=0)


def _to_rows(a, width):
    flat = a.reshape(-1)
    n = -(-flat.shape[0] // width)
    flat = jnp.pad(flat, (0, n * width - flat.shape[0]))
    return flat.reshape(n, width)


def kernel(x, c, w_ada, b_ada, g_pre_mix, g_post_mix, w_in, ssm_lam_re, ssm_lam_im, ssm_log_step, ssm_b_re, ssm_b_im, ssm_c_re, ssm_c_im, ssm_d, glu_w, glu_b, g_out_ssm, conv_w, g_out_conv, w_out, g_pre_ffn, g_post_ffn, w_up, ffn_conv_w, w_down, loss_target, m_w_ada, m_b_ada, m_g_pre_mix, m_g_post_mix, m_w_in, m_ssm_lam_re, m_ssm_lam_im, m_ssm_log_step, m_ssm_b_re, m_ssm_b_im, m_ssm_c_re, m_ssm_c_im, m_ssm_d, m_glu_w, m_glu_b, m_g_out_ssm, m_conv_w, m_g_out_conv, m_w_out, m_g_pre_ffn, m_g_post_ffn, m_w_up, m_ffn_conv_w, m_w_down, v_w_ada, v_b_ada, v_g_pre_mix, v_g_post_mix, v_w_in, v_ssm_lam_re, v_ssm_lam_im, v_ssm_log_step, v_ssm_b_re, v_ssm_b_im, v_ssm_c_re, v_ssm_c_im, v_ssm_d, v_glu_w, v_glu_b, v_g_out_ssm, v_conv_w, v_g_out_conv, v_w_out, v_g_pre_ffn, v_g_post_ffn, v_w_up, v_ffn_conv_w, v_w_down):
    xs = x[0]
    tgt = loss_target[0]
    t, d = xs.shape
    xi, yi, ci = lax.axis_index("x"), lax.axis_index("y"), lax.axis_index("c")
    chip = 2 * xi + yi
    dev = 2 * chip + ci

    n_groups, n_state = ssm_lam_re.shape[1:]
    n_gch = ssm_b_re.shape[3]
    d_ssm = n_groups * n_gch
    gp = n_groups * n_state
    n_ada = w_ada.shape[2]
    d_ff = w_down.shape[1] * N_CHIPS
    n_upc = w_up.shape[2]

    w_names = ("w_in", "glu_w", "w_out", "w_up", "w_down")
    w_own = [w[0].astype(BF16) for w in (w_in, glu_w, w_out, w_up, w_down)]
    w_send, w_recv, w_src, w_land, w_token = _chips_start(
        "weights_start", True, w_own, [_landing(a, chip) for a in w_own])

    def weights(names, after):
        ks = [w_names.index(nm) for nm in names]
        return _chips_wait("weights_wait_" + names[0], True, [w_send[k] for k in ks], [w_recv[k] for k in ks],
                           [w_src[k] for k in ks], [w_land[k] for k in ks], after)

    c = c + w_token[0:1, 0:1]
    c_gath, _ = _allgather8(jnp.broadcast_to(c, (SUBLANES, d)), SUBLANES, "gather_c")
    c_all = c_gath.reshape(N_DEV, SUBLANES, d)[:, 0, :]
    b_sh = lax.dynamic_slice(b_ada, (0, chip * n_ada), (1, n_ada))
    mod_sh = _mod_shard(c_all, w_ada[0], b_sh)

    def pad8(a):
        return jnp.concatenate([a, jnp.zeros((SUBLANES - a.shape[0], a.shape[1]), a.dtype)], axis=0)

    halves_in = [
        jnp.stack([mod_sh[:, :n_ada // 2], mod_sh[:, n_ada // 2:]]),
        jnp.stack([pad8(conv_w[0]), jnp.zeros((SUBLANES, conv_w.shape[2]), F32)]),
        jnp.stack([pad8(ffn_conv_w[0]), jnp.zeros((SUBLANES, ffn_conv_w.shape[2]), F32)]),
    ]
    g_mod, g_cw, g_fw = _gather_chips(halves_in, "gather_mod")
    mod_all = g_mod.transpose(2, 0, 1, 3).reshape(N_DEV, N_CHIPS * n_ada)
    mod = lax.dynamic_slice(mod_all, (dev, 0), (1, N_CHIPS * n_ada))
    sh1, sc1, gt1, sh2, sc2, gt2 = [mod[:, k * d:(k + 1) * d] for k in range(6)]
    convw_full = pad8(g_cw[:, 0, :3, :].transpose(1, 0, 2).reshape(3, d_ssm))
    fw_full = pad8(g_fw[:, 0, :3, :].transpose(1, 0, 2).reshape(3, N_CHIPS * n_upc))

    lam_re, lam_im = ssm_lam_re[0], ssm_lam_im[0]
    log_step = ssm_log_step[0].reshape(n_groups, 1)
    abar_re, abar_im, coef_re, coef_im = _ssm_prep(lam_re, lam_im, log_step)
    a_rows = _rows8(abar_re.reshape(1, gp), abar_im.reshape(1, gp))
    coef_rows = _rows8(coef_re.reshape(1, gp), coef_im.reshape(1, gp))
    bt_re = jnp.tile(ssm_b_re[0].transpose(0, 2, 1).reshape(d_ssm, n_state), (1, n_groups))
    bt_im = jnp.tile(ssm_b_im[0].transpose(0, 2, 1).reshape(d_ssm, n_state), (1, n_groups))
    ct_re = jnp.tile(ssm_c_re[0].transpose(0, 2, 1).reshape(gp, n_gch), (1, n_groups))
    ct_im = jnp.tile(ssm_c_im[0].transpose(0, 2, 1).reshape(gp, n_gch), (1, n_groups))
    bblk_re, bblk_im, cblk_re, cblk_im = _ssm_blocks(bt_re, bt_im, ct_re, ct_im, coef_rows)

    head = lax.broadcasted_iota(jnp.int32, (d_ssm, d_ssm), 0), lax.broadcasted_iota(jnp.int32, (d_ssm, d_ssm), 1)
    h16 = jnp.where((head[0] // n_gch) == (head[1] // n_gch), 1.0 / n_gch, 0.0).astype(BF16)
    conv_hd = 64
    h64 = jnp.where((head[0] // conv_hd) == (head[1] // conv_hd), 1.0 / conv_hd, 0.0).astype(BF16)

    v512 = _rows8(ssm_d, glu_b, g_out_ssm, g_out_conv)
    vec1 = _rows8(g_pre_mix, 1.0 + sc1, sh1)
    vd1 = _rows8(g_post_mix, gt1)
    vec2 = _rows8(g_pre_ffn, 1.0 + sc2, sh2)
    vd2 = _rows8(g_post_ffn, gt2)

    (g_win,) = weights(("w_in",), bblk_re)
    w_in_st = g_win
    proj, bu_re, bu_im, h1b = _mix_in(xs, vec1, w_in_st, bblk_re, bblk_im)
    s_re, s_im = _scan_fwd(a_rows, bu_re, bu_im)
    g_glu, g_wout = weights(("glu_w", "w_out"), s_re)
    glu_full = g_glu.reshape(d_ssm, d_ssm)
    w_out_full = g_wout.reshape(2 * d_ssm, d)
    y1, o_mix, x2 = _mix_out(xs, proj, s_re, s_im, cblk_re, cblk_im, v512, convw_full, glu_full, h16, h64,
                             w_out_full, vd1)
    (w_up_st,) = weights(("w_up",), x2)
    up, h2b = _ffn_up(x2, vec2, w_up_st)
    (g_wdown,) = weights(("w_down",), up)
    w_down_full = g_wdown.reshape(d_ff, d)
    actb, ddnb, dout, dhid, vp_dn, loss_blk = _ffn_down(up, fw_full, w_down_full, x2, tgt, vd2)

    g_names = ("w_down", "w_up", "w_out", "glu_w", "w_in")
    gw_down = _matmul_tn(actb, ddnb, d_ff, d, BF16, "dw_down").reshape(N_CHIPS, d_ff // N_CHIPS, d)
    dx2, dupb, vp_up, df_rows = _ffn_up_bwd(dhid, up, fw_full, x2, dout, vec2, w_up_st)
    gw_up = _matmul_tn(h2b, dupb, d, n_upc, BF16, "dw_up")
    ga_send, ga_recv, ga_src, ga_land, ga_token = _chips_start(
        "grads_start_ffn", False, [gw_down, gw_up],
        [_landing(lax.dynamic_index_in_dim(g, chip, 0, False), chip) for g in (gw_down, gw_up)])
    (dob, ycatb, zb, dqb, dy1, g_re, g_im, dcc, dbg, vp_mo, vp5) = _mix_out_bwd(
        dx2, o_mix, y1, proj, cblk_re, cblk_im, v512, convw_full, glu_full, h16, h64, w_out_full,
        vd1 + ga_token[0:1, 0:1])
    gt_re, gt_im, ga_re8, ga_im8 = _scan_bwd(a_rows, g_re, g_im, s_re, s_im)
    grad_x, dprojb, vp_mi = _mix_in_bwd(gt_re, gt_im, bblk_re, bblk_im, dy1, dcc, dbg, proj, xs, dx2, vec1, v512,
                                        convw_full, w_in_st)
    gw_out = _matmul_tn(ycatb, dob, 2 * d_ssm, d, BF16, "dw_out").reshape(N_CHIPS, 2 * d_ssm // N_CHIPS, d)
    gw_glu = _matmul_tn(zb, dqb, d_ssm, d_ssm, BF16, "dw_glu").reshape(N_CHIPS, d_ssm // N_CHIPS, d_ssm)
    gw_in = _matmul_tn(h1b, dprojb, d, w_in.shape[2], BF16, "dw_in")
    gb_send, gb_recv, gb_src, gb_land, gb_token = _chips_start(
        "grads_start_mix", False, [gw_out, gw_glu, gw_in],
        [_landing(lax.dynamic_index_in_dim(g, chip, 0, False), chip) for g in (gw_out, gw_glu, gw_in)])
    d_bre = _matmul_tn(proj, gt_re, d_ssm, gp, F32, "d_bre", after=gb_token)[0]
    d_bim = _matmul_tn(proj, gt_im, d_ssm, gp, F32, "d_bim")[0]
    d_cre = _matmul_tn(s_re, dy1, gp, d_ssm, F32, "d_cre")[0]
    d_cim = _matmul_tn(s_im, dy1, gp, d_ssm, F32, "d_cim")[0]

    lane = lax.broadcasted_iota(jnp.int32, (gp, n_state), 0), lax.broadcasted_iota(jnp.int32, (gp, n_state), 1)
    fold_b = jnp.where((lane[0] % n_state) == lane[1], 1.0, 0.0).astype(BF16)
    lane_c = lax.broadcasted_iota(jnp.int32, (d_ssm, n_gch), 0), lax.broadcasted_iota(jnp.int32, (d_ssm, n_gch), 1)
    fold_c = jnp.where((lane_c[0] % n_gch) == lane_c[1], 1.0, 0.0).astype(BF16)
    db_re_f, db_im_f, gc_rows = _ssm_bgrad(d_bre, d_bim, bt_re, bt_im, coef_rows, fold_b)
    dc_re_f, dc_im_f = _ssm_cgrad(d_cre, d_cim, fold_c)
    ga_sum = _ga_rowsum(ga_re8, ga_im8)
    g_lam_re, g_lam_im, g_log_step = _ssm_lamgrad(
        lam_re, lam_im, log_step, abar_re, abar_im, coef_re, coef_im,
        gc_rows[0].reshape(n_groups, n_state), gc_rows[1].reshape(n_groups, n_state),
        ga_sum[0].reshape(n_groups, n_state), ga_sum[1].reshape(n_groups, n_state))
    g_b_re = db_re_f.reshape(n_groups, n_gch, n_state).transpose(0, 2, 1)
    g_b_im = db_im_f.reshape(n_groups, n_gch, n_state).transpose(0, 2, 1)
    g_c_re = dc_re_f.reshape(n_groups, n_state, n_gch).transpose(0, 2, 1)
    g_c_im = dc_im_f.reshape(n_groups, n_state, n_gch).transpose(0, 2, 1)

    dmod = jnp.concatenate([vp_mi[0:1], vp_mi[1:2], vp_mo[0:1], vp_up[0:1], vp_up[1:2], vp_dn[0:1]], axis=1)
    small = [
        ("g_pre_mix", vp_mi[2:3]), ("g_post_mix", vp_mo[1:2]), ("g_pre_ffn", vp_up[2:3]), ("g_post_ffn", vp_dn[1:2]),
        ("ssm_lam_re", g_lam_re), ("ssm_lam_im", g_lam_im), ("ssm_log_step", g_log_step),
        ("ssm_b_re", g_b_re), ("ssm_b_im", g_b_im), ("ssm_c_re", g_c_re), ("ssm_c_im", g_c_im),
        ("ssm_d", vp5[3:4]), ("glu_b", vp5[2:3]), ("g_out_ssm", vp5[0:1]), ("g_out_conv", vp5[1:2]),
        ("conv_w", vp5[4:7]), ("ffn_conv_w", df_rows[0:3]),
    ]
    packed, offsets, row = [], {}, 0
    for name, a in small:
        r = _to_rows(a, d)
        offsets[name] = (row, a.shape)
        packed.append(r)
        row += r.shape[0]
    n_small = -(-row // SUBLANES) * SUBLANES
    packed.append(jnp.zeros((n_small - row, d), F32))
    packed.append(pad8(dmod.reshape(6, d)))
    pack = jnp.concatenate(packed, axis=0)
    gath, sums = _allgather8(pack, n_small + SUBLANES, "reduce_small")
    dmod_all = gath.reshape(N_DEV, n_small + SUBLANES, d)[:, n_small:n_small + 6, :].reshape(N_DEV, 6 * d)
    g_b_ada = sums[n_small:n_small + 6].reshape(1, 6 * d)

    def unpack(name):
        r0, shape = offsets[name]
        size = math.prod(shape)
        nrow = -(-size // d)
        return sums[r0:r0 + nrow].reshape(-1)[:size].reshape(shape)

    dmod_cols = lax.dynamic_slice(dmod_all, (0, chip * n_ada), (N_DEV, n_ada))
    ada = _adamw_ada(c_all, dmod_cols, w_ada[0], m_w_ada[0], v_w_ada[0])

    landed = _chips_wait("grads_wait", False, list(ga_send) + list(gb_send), list(ga_recv) + list(gb_recv),
                         list(ga_src) + list(gb_src), list(ga_land) + list(gb_land), ada[0])
    partial = [_sum_blocks(s, "sum_" + nm) for s, nm in zip(landed, g_names)]
    theirs = _swap_sibling(partial, "swap_partials")
    big = {}
    for nm, pm, ps, w_, m_, v_ in zip(
            g_names, partial, theirs,
            (w_down, w_up, w_out, glu_w, w_in), (m_w_down, m_w_up, m_w_out, m_glu_w, m_w_in),
            (v_w_down, v_w_up, v_w_out, v_glu_w, v_w_in)):
        big[nm] = _adamw_big(pm, ps, w_[0], m_[0], v_[0], "adamw_" + nm)

    g_small = {name: unpack(name) for name, _ in small}
    g_small["b_ada"] = g_b_ada
    g_small["conv_w"] = lax.dynamic_slice(g_small["conv_w"], (0, chip * conv_w.shape[2]), (3, conv_w.shape[2]))
    g_small["ffn_conv_w"] = lax.dynamic_slice(g_small["ffn_conv_w"], (0, chip * n_upc), (3, n_upc))
    g_small["ssm_log_step"] = g_small["ssm_log_step"].reshape(1, n_groups)
    small_params = {
        "b_ada": (b_ada, m_b_ada, v_b_ada), "g_pre_mix": (g_pre_mix, m_g_pre_mix, v_g_pre_mix),
        "g_post_mix": (g_post_mix, m_g_post_mix, v_g_post_mix), "ssm_lam_re": (ssm_lam_re, m_ssm_lam_re, v_ssm_lam_re),
        "ssm_lam_im": (ssm_lam_im, m_ssm_lam_im, v_ssm_lam_im),
        "ssm_log_step": (ssm_log_step, m_ssm_log_step, v_ssm_log_step),
        "ssm_b_re": (ssm_b_re, m_ssm_b_re, v_ssm_b_re), "ssm_b_im": (ssm_b_im, m_ssm_b_im, v_ssm_b_im),
        "ssm_c_re": (ssm_c_re, m_ssm_c_re, v_ssm_c_re), "ssm_c_im": (ssm_c_im, m_ssm_c_im, v_ssm_c_im),
        "ssm_d": (ssm_d, m_ssm_d, v_ssm_d), "glu_b": (glu_b, m_glu_b, v_glu_b),
        "g_out_ssm": (g_out_ssm, m_g_out_ssm, v_g_out_ssm), "conv_w": (conv_w, m_conv_w, v_conv_w),
        "g_out_conv": (g_out_conv, m_g_out_conv, v_g_out_conv), "g_pre_ffn": (g_pre_ffn, m_g_pre_ffn, v_g_pre_ffn),
        "g_post_ffn": (g_post_ffn, m_g_post_ffn, v_g_post_ffn),
        "ffn_conv_w": (ffn_conv_w, m_ffn_conv_w, v_ffn_conv_w),
    }

    def flat2d(a):
        n = a.size
        return a.reshape(n // 1024, 1024) if n % 1024 == 0 and n > 1024 else a.reshape(-1, a.shape[-1])

    names = list(small_params)
    items = []
    for nm in names:
        w_, m_, v_ = small_params[nm]
        items.append((flat2d(w_[0]), flat2d(g_small[nm].reshape(w_[0].shape)), flat2d(m_[0]), flat2d(v_[0])))
    upd = _adamw_small(items)
    small_out = {}
    for nm, (dl, mo, vo) in zip(names, upd):
        shp = small_params[nm][0].shape
        small_out[nm] = (g_small[nm].reshape(shp), dl.reshape(shp), mo.reshape(shp), vo.reshape(shp))

    loss = lax.psum(loss_blk[0, 0], ("x", "y", "c"))

    order = ["w_ada", "b_ada", "g_pre_mix", "g_post_mix", "w_in", "ssm_lam_re", "ssm_lam_im", "ssm_log_step",
             "ssm_b_re", "ssm_b_im", "ssm_c_re", "ssm_c_im", "ssm_d", "glu_w", "glu_b", "g_out_ssm", "conv_w",
             "g_out_conv", "w_out", "g_pre_ffn", "g_post_ffn", "w_up", "ffn_conv_w", "w_down"]
    results = {"w_ada": tuple(a[None] for a in ada)}
    for nm in big:
        results[nm] = tuple(a[None] for a in big[nm])
    results.update(small_out)
    outs = [loss, grad_x[None]]
    for k in range(4):
        outs += [results[nm][k] for nm in order]
    return tuple(outs)


def _ga_rowsum(ga_re8, ga_im8):
    n = ga_re8.shape[1]

    def body(r_ref, i_ref, o_ref):
        o_ref[...] = jnp.zeros(o_ref.shape, F32)
        o_ref[0:1, :] = _colsum(r_ref[...])
        o_ref[1:2, :] = _colsum(i_ref[...])

    return pl.pallas_call(body, name="ga_rowsum", out_shape=jax.ShapeDtypeStruct((SUBLANES, n), F32))(ga_re8, ga_im8)
```

```python
import functools
import math

import jax
import jax.numpy as jnp
from jax import lax
from jax.experimental import pallas as pl
from jax.experimental.pallas import tpu as pltpu

F32 = jnp.float32
BF16 = jnp.bfloat16
MESH = pl.DeviceIdType.MESH

EPS = 1e-6
LAMBDA_RE_MAX = -1e-4
ADAM_LR = 0.001
ADAM_B1 = 0.9
ADAM_B2 = 0.999
ADAM_EPS = 1e-08
ADAM_WD = 0.01
ADAM_STEP = 10

SUBLANES = 8
N_CHIPS = 4
N_DEV = 8
VMEM_BIG = 56 * 1024 * 1024
VMEM_MID = 40 * 1024 * 1024

TB_MIX = 256
TB_FFN = 128
TB_SCAN = 1024
W_SCAN = 256
CW_FFN = 256
SCAN_UNROLL = 4
TB_TN = 512


def _cparams(sem=None, vmem=None):
    kw = {}
    if sem is not None:
        kw["dimension_semantics"] = sem
    if vmem is not None:
        kw["vmem_limit_bytes"] = vmem
    return pltpu.CompilerParams(**kw)


def _blk(t, pref):
    return pref if t % pref == 0 else t


def _dot(a, b):
    return jnp.dot(a.astype(BF16), b.astype(BF16), preferred_element_type=F32)


def _dot_nt(a, b):
    return lax.dot_general(a.astype(BF16), b.astype(BF16), (((1,), (1,)), ((), ())),
                           preferred_element_type=F32)


def _dot_tn(a, b):
    return lax.dot_general(a.astype(BF16), b.astype(BF16), (((0,), (0,)), ((), ())),
                           preferred_element_type=F32)


def _sigmoid(x):
    return 1.0 / (1.0 + jnp.exp(-x))


_GELU_K = math.sqrt(2.0 / math.pi)
_GELU_C = 0.044715


def _gelu(x):
    th = jnp.tanh(_GELU_K * (x + _GELU_C * x * x * x))
    return 0.5 * x * (1.0 + th)


def _gelu_grad(x):
    x2 = x * x
    th = jnp.tanh(_GELU_K * (x + _GELU_C * x2 * x))
    return 0.5 * (1.0 + th) + 0.5 * x * (1.0 - th * th) * _GELU_K * (1.0 + 3.0 * _GELU_C * x2)


def _rowmean(x):
    return jnp.mean(x, axis=-1, keepdims=True)


def _colsum(x):
    return jnp.sum(x, axis=0, keepdims=True)


def _split_dot(x, m):
    hi = x.astype(BF16)
    lo = (x - hi.astype(F32)).astype(BF16)
    return (jnp.dot(hi, m, preferred_element_type=F32) + jnp.dot(lo, m, preferred_element_type=F32))


def _split3_dot(x, m):
    hi = x.astype(BF16)
    r1 = x - hi.astype(F32)
    mid = r1.astype(BF16)
    lo = (r1 - mid.astype(F32)).astype(BF16)
    return (jnp.dot(hi, m, preferred_element_type=F32) + jnp.dot(mid, m, preferred_element_type=F32)
            + jnp.dot(lo, m, preferred_element_type=F32))


def _shift_down(x, halo, k):
    r = pltpu.roll(x, k, 0)
    row = lax.broadcasted_iota(jnp.int32, x.shape, 0)
    for j in range(k):
        r = jnp.where(row == j, halo[SUBLANES - k + j:SUBLANES - k + j + 1, :], r)
    return r


def _shift_up(x, halo, k):
    n = x.shape[0]
    r = pltpu.roll(x, n - k, 0)
    row = lax.broadcasted_iota(jnp.int32, x.shape, 0)
    for j in range(k):
        r = jnp.where(row == n - k + j, halo[j:j + 1, :], r)
    return r


def _acc_rows(ref, first, rows):
    @pl.when(first)
    def _():
        ref[...] = jnp.zeros(ref.shape, ref.dtype)
    for j, r in enumerate(rows):
        ref[j:j + 1, :] += r


def _rows(tb, c, col=0):
    return pl.BlockSpec((tb, c), lambda i, col=col: (i, col))


def _full(shape):
    nd = len(shape)
    return pl.BlockSpec(shape, lambda i, nd=nd: (0,) * nd)


def _halo_prev(tb, c, col=0):
    per = tb // SUBLANES
    return pl.BlockSpec((SUBLANES, c), lambda i, col=col: (jnp.maximum(i * per - 1, 0), col))


def _halo_next(tb, c, t, col=0):
    per = tb // SUBLANES
    last = t // SUBLANES - 1
    return pl.BlockSpec((SUBLANES, c), lambda i, col=col: (jnp.minimum((i + 1) * per, last), col))


def _mesh_pos():
    return lax.axis_index("x"), lax.axis_index("y"), lax.axis_index("c")


def _allgather8(x_pad, n_sum, name):
    m_per, n = x_pad.shape

    def body(x_ref, out_ref, sum_ref, send_sems, recv_sems, local_sem):
        x, y, c = _mesh_pos()
        me, sibling = (x, y, c), (x, y, 1 - c)
        chips = [(1 - x, y), (x, 1 - y), (1 - x, 1 - y)]

        def rows(px, py, pc):
            return out_ref.at[pl.ds((4 * px + 2 * py + pc) * m_per, m_per), :]

        def copy(k, block, to, src=None):
            return pltpu.make_async_remote_copy(
                src_ref=rows(*block) if src is None else src, dst_ref=rows(*block),
                send_sem=send_sems.at[k], recv_sem=recv_sems.at[k], device_id=to, device_id_type=MESH)

        mine = pltpu.make_async_copy(x_ref, rows(*me), local_sem)
        mine.start()
        first = [copy(0, me, sibling, src=x_ref)]
        first += [copy(1 + j, me, (*chip, c), src=x_ref) for j, chip in enumerate(chips)]
        for cp in first:
            cp.start()
        passed = [copy(4 + j, (*chip, c), sibling) for j, chip in enumerate(chips)]
        for j, chip in enumerate(chips):
            copy(1 + j, (*chip, c), me).wait_recv()
            passed[j].start()
        copy(0, sibling, me).wait_recv()
        for j, chip in enumerate(chips):
            copy(4 + j, (*chip, 1 - c), me).wait_recv()
        for cp in first + passed:
            cp.wait_send()
        mine.wait()
        acc = out_ref[0:n_sum, :]
        for k in range(1, N_DEV):
            acc = acc + out_ref[k * m_per:k * m_per + n_sum, :]
        sum_ref[...] = acc

    return pl.pallas_call(
        body, name=name,
        out_shape=(jax.ShapeDtypeStruct((N_DEV * m_per, n), F32), jax.ShapeDtypeStruct((n_sum, n), F32)),
        in_specs=[pl.BlockSpec(memory_space=pltpu.VMEM)],
        out_specs=(pl.BlockSpec(memory_space=pltpu.VMEM), pl.BlockSpec(memory_space=pltpu.VMEM)),
        scratch_shapes=[pltpu.SemaphoreType.DMA((7,)), pltpu.SemaphoreType.DMA((7,)), pltpu.SemaphoreType.DMA],
        compiler_params=_cparams(vmem=VMEM_MID),
    )(x_pad)


def _gather_chips(halves, name):
    n_arr = len(halves)

    def body(*refs):
        ins, outs = refs[:n_arr], refs[n_arr:2 * n_arr]
        send_sems, recv_sems, local_sems = refs[2 * n_arr:]
        x, y, c = _mesh_pos()
        sibling = (x, y, 1 - c)
        chips = [(1 - x, y), (x, 1 - y), (1 - x, 1 - y)]
        my_chip = 2 * x + y

        def slot(n, chip_idx, half):
            return outs[n].at[chip_idx, half]

        def send(n, j):
            return pltpu.make_async_remote_copy(
                src_ref=ins[n].at[c], dst_ref=slot(n, my_chip, c),
                send_sem=send_sems.at[n * 3 + j], recv_sem=recv_sems.at[n * 3 + j],
                device_id=(*chips[j], c), device_id_type=MESH)

        def landed(n, j):
            idx = 2 * chips[j][0] + chips[j][1]
            return pltpu.make_async_remote_copy(
                src_ref=ins[n].at[c], dst_ref=slot(n, idx, c),
                send_sem=send_sems.at[n * 3 + j], recv_sem=recv_sems.at[n * 3 + j],
                device_id=(*chips[j], c), device_id_type=MESH)

        def forward(n, j, half):
            idx = 2 * chips[j][0] + chips[j][1]
            k = 3 * n_arr + n * 3 + j
            return pltpu.make_async_remote_copy(
                src_ref=slot(n, idx, half), dst_ref=slot(n, idx, half),
                send_sem=send_sems.at[k], recv_sem=recv_sems.at[k],
                device_id=sibling, device_id_type=MESH)

        locals_ = [pltpu.make_async_copy(ins[n], outs[n].at[my_chip], local_sems.at[n]) for n in range(n_arr)]
        for cp in locals_:
            cp.start()
        sends = [send(n, j) for n in range(n_arr) for j in range(3)]
        for cp in sends:
            cp.start()
        fwds = []
        for n in range(n_arr):
            for j in range(3):
                landed(n, j).wait_recv()
                f = forward(n, j, c)
                f.start()
                fwds.append(f)
        for n in range(n_arr):
            for j in range(3):
                forward(n, j, 1 - c).wait_recv()
        for cp in sends + fwds:
            cp.wait_send()
        for cp in locals_:
            cp.wait()

    any_spec = pl.BlockSpec(memory_space=pl.ANY)
    return pl.pallas_call(
        body, name=name,
        out_shape=tuple(jax.ShapeDtypeStruct((N_CHIPS,) + h.shape, h.dtype) for h in halves),
        in_specs=[any_spec] * n_arr, out_specs=tuple([any_spec] * n_arr),
        scratch_shapes=[pltpu.SemaphoreType.DMA((6 * n_arr,)), pltpu.SemaphoreType.DMA((6 * n_arr,)),
                        pltpu.SemaphoreType.DMA((n_arr,))],
    )(*halves)


_HBM = pl.BlockSpec(memory_space=pltpu.HBM)
_SEM = pl.BlockSpec(memory_space=pltpu.SEMAPHORE)
_EFFECT = pltpu.SideEffectType.DATAFLOW_SIDE_EFFECTING


def _chip_copy(gather, src_ref, land_ref, send, recv, j, arrival):
    x, y, c = _mesh_pos()
    peer = [(1 - x, y), (x, 1 - y), (1 - x, 1 - y)][j]
    peer_chip = 2 * peer[0] + peer[1]
    my_chip = 2 * x + y
    return pltpu.make_async_remote_copy(
        src_ref=src_ref if gather else src_ref.at[peer_chip],
        dst_ref=land_ref.at[peer_chip if arrival else my_chip],
        send_sem=send.at[j], recv_sem=recv.at[j], device_id=(*peer, c), device_id_type=MESH)


def _chips_start(name, gather, srcs, lands):
    n = len(srcs)

    def body(*refs):
        src_refs, land_refs = refs[:n], refs[n:2 * n]
        outs = refs[2 * n:]
        sends, recvs, token = outs[:n], outs[n:2 * n], outs[-1]
        for k in range(n):
            for j in range(3):
                _chip_copy(gather, src_refs[k], land_refs[k], sends[k], recvs[k], j, False).start()
        token[...] = jnp.zeros(token.shape, F32)

    sem = pltpu.SemaphoreType.DMA((3,))
    thru = tuple(pltpu.HBM(a.shape, a.dtype) for a in list(srcs) + list(lands))
    res = pl.pallas_call(
        body, name=name,
        out_shape=(sem,) * (2 * n) + thru + (jax.ShapeDtypeStruct((SUBLANES, 128), F32),),
        in_specs=[_HBM] * (2 * n),
        out_specs=(_SEM,) * (2 * n) + (_HBM,) * (2 * n) + (pl.BlockSpec(memory_space=pltpu.VMEM),),
        input_output_aliases={k: 2 * n + k for k in range(2 * n)},
        compiler_params=pltpu.CompilerParams(has_side_effects=_EFFECT),
    )(*[pltpu.with_memory_space_constraint(a, pltpu.HBM) for a in list(srcs) + list(lands)])
    return res[:n], res[n:2 * n], res[2 * n:3 * n], res[3 * n:4 * n], res[-1]


def _chips_wait(name, gather, sends, recvs, srcs, lands, after):
    n = len(srcs)

    def body(*refs):
        src_refs, land_refs = refs[:n], refs[n:2 * n]
        sends_, recvs_ = refs[2 * n:3 * n], refs[3 * n:4 * n]
        for k in range(n):
            for j in range(3):
                cp = _chip_copy(gather, src_refs[k], land_refs[k], sends_[k], recvs_[k], j, True)
                cp.wait_send()
                cp.wait_recv()

    thru = tuple(pltpu.HBM(a.shape, a.dtype) for a in list(srcs) + list(lands))
    res = pl.pallas_call(
        body, name=name, out_shape=thru,
        in_specs=[_HBM] * (2 * n) + [_SEM] * (2 * n) + [pl.BlockSpec(memory_space=pl.ANY)],
        out_specs=(_HBM,) * (2 * n),
        input_output_aliases={k: k for k in range(2 * n)},
        compiler_params=pltpu.CompilerParams(has_side_effects=_EFFECT),
    )(*srcs, *lands, *sends, *recvs, after)
    return res[n:]


def _landing(own, chip):
    zone = lax.empty((N_CHIPS,) + own.shape, own.dtype)
    return lax.dynamic_update_slice(zone, own[None], (chip,) + (0,) * own.ndim)


def _swap_sibling(arrs, name):
    n_arr = len(arrs)

    def body(*refs):
        ins, outs = refs[:n_arr], refs[n_arr:2 * n_arr]
        send_sems, recv_sems = refs[2 * n_arr:]
        x, y, c = _mesh_pos()
        copies = [pltpu.make_async_remote_copy(
            src_ref=ins[n], dst_ref=outs[n], send_sem=send_sems.at[n], recv_sem=recv_sems.at[n],
            device_id=(x, y, 1 - c), device_id_type=MESH) for n in range(n_arr)]
        for cp in copies:
            cp.start()
        for cp in copies:
            cp.wait()

    any_spec = pl.BlockSpec(memory_space=pl.ANY)
    return pl.pallas_call(
        body, name=name,
        out_shape=tuple(jax.ShapeDtypeStruct(a.shape, a.dtype) for a in arrs),
        in_specs=[any_spec] * n_arr, out_specs=tuple([any_spec] * n_arr),
        scratch_shapes=[pltpu.SemaphoreType.DMA((n_arr,)), pltpu.SemaphoreType.DMA((n_arr,))],
    )(*arrs)


def _mod_shard(c_all, w_ada_sh, b_sh):
    d, n = w_ada_sh.shape
    bn = 512

    def body(c_ref, w_ref, b_ref, o_ref):
        cc = c_ref[...]
        ca = cc * _sigmoid(cc)
        o_ref[...] = _dot(ca, w_ref[...]) + b_ref[...]

    return pl.pallas_call(
        body, name="mod_shard", grid=(n // bn,),
        out_shape=jax.ShapeDtypeStruct((N_DEV, n), F32),
        in_specs=[_full((N_DEV, d)), pl.BlockSpec((d, bn), lambda j: (0, j)), pl.BlockSpec((1, bn), lambda j: (0, j))],
        out_specs=pl.BlockSpec((N_DEV, bn), lambda j: (0, j)),
        compiler_params=_cparams(("parallel",)),
    )(c_all, w_ada_sh, b_sh)


def _ssm_prep(lam_re, lam_im, log_step):
    g, p = lam_re.shape

    def body(lr_ref, li_ref, ls_ref, ar_ref, ai_ref, cr_ref, ci_ref):
        lr = jnp.minimum(lr_ref[...], LAMBDA_RE_MAX)
        li = li_ref[...]
        st = jnp.exp(ls_ref[...])
        mag = jnp.exp(lr * st)
        ar = mag * jnp.cos(li * st)
        ai = mag * jnp.sin(li * st)
        den = lr * lr + li * li
        nr = ar - 1.0
        ar_ref[...] = ar
        ai_ref[...] = ai
        cr_ref[...] = (nr * lr + ai * li) / den
        ci_ref[...] = (ai * lr - nr * li) / den

    sds = jax.ShapeDtypeStruct((g, p), F32)
    return pl.pallas_call(body, name="ssm_prep", out_shape=(sds,) * 4)(lam_re, lam_im, log_step)


def _ssm_blocks(bt_re, bt_im, ct_re, ct_im, coef_rows):
    gh, gp = bt_re.shape
    nb = 4
    cb, rb = gp // nb, gp // nb

    def body(btr, bti, ctr, cti, cf, bre_o, bim_o, cre_o, cim_o):
        j = pl.program_id(0)
        row = lax.broadcasted_iota(jnp.int32, (gh, cb), 0)
        col = lax.broadcasted_iota(jnp.int32, (gh, cb), 1) + j * cb
        mask = (row >> 4) == (col >> 6)
        cr, ci = cf[0:1, :], cf[1:2, :]
        br, bi = btr[...], bti[...]
        bre_o[...] = jnp.where(mask, br * cr - bi * ci, 0.0).astype(BF16)
        bim_o[...] = jnp.where(mask, br * ci + bi * cr, 0.0).astype(BF16)
        row2 = lax.broadcasted_iota(jnp.int32, (rb, gh), 0) + j * rb
        col2 = lax.broadcasted_iota(jnp.int32, (rb, gh), 1)
        mask2 = (row2 >> 6) == (col2 >> 4)
        cre_o[...] = jnp.where(mask2, ctr[...], 0.0).astype(BF16)
        cim_o[...] = jnp.where(mask2, cti[...], 0.0).astype(BF16)

    bspec = pl.BlockSpec((gh, cb), lambda j: (0, j))
    cspec = pl.BlockSpec((rb, gh), lambda j: (j, 0))
    return pl.pallas_call(
        body, name="ssm_blocks", grid=(nb,),
        out_shape=(jax.ShapeDtypeStruct((gh, gp), BF16),) * 2 + (jax.ShapeDtypeStruct((gp, gh), BF16),) * 2,
        in_specs=[bspec, bspec, cspec, cspec, pl.BlockSpec((SUBLANES, cb), lambda j: (0, j))],
        out_specs=(bspec, bspec, cspec, cspec),
        compiler_params=_cparams(("parallel",)),
    )(bt_re, bt_im, ct_re, ct_im, coef_rows)


def _scan_consts(a_ref, reverse):
    w = a_ref.shape[1]
    ar1 = a_ref[0:1, :]
    ai1 = a_ref[1:2, :]
    if reverse:
        ai1 = -ai1
    pr, pi = [ar1], [ai1]
    for _ in range(1, SUBLANES):
        nr = pr[-1] * ar1 - pi[-1] * ai1
        ni = pr[-1] * ai1 + pi[-1] * ar1
        pr.append(nr)
        pi.append(ni)
    row = lax.broadcasted_iota(jnp.int32, (SUBLANES, w), 0)
    dist = (SUBLANES - 1 - row) if reverse else row

    def pick(vals):
        out = jnp.broadcast_to(vals[SUBLANES - 1], (SUBLANES, w))
        for r in range(SUBLANES - 1):
            out = jnp.where(dist == r, vals[r], out)
        return out

    p_r, p_i = pick(pr), pick(pi)
    steps = []
    for k in (1, 2, 4):
        steps.append((k, jnp.where(dist >= k, pr[k - 1], 0.0), jnp.where(dist >= k, pi[k - 1], 0.0)))
    a8 = (jnp.broadcast_to(pr[SUBLANES - 1], (SUBLANES, w)), jnp.broadcast_to(pi[SUBLANES - 1], (SUBLANES, w)))
    return row, p_r, p_i, steps, a8


def _scan_tile(xr, xi, cr, ci, consts, reverse):
    row, p_r, p_i, steps, (a8r, a8i) = consts
    for k, s_r, s_i in steps:
        sh = (SUBLANES - k) if reverse else k
        qr = pltpu.roll(xr, sh, 0)
        qi = pltpu.roll(xi, sh, 0)
        xr, xi = xr + s_r * qr - s_i * qi, xi + s_r * qi + s_i * qr
    outr = xr + p_r * cr - p_i * ci
    outi = xi + p_r * ci + p_i * cr
    e = 0 if reverse else SUBLANES - 1
    er = jnp.broadcast_to(xr[e:e + 1, :], xr.shape)
    ei = jnp.broadcast_to(xi[e:e + 1, :], xi.shape)
    return outr, outi, er + a8r * cr - a8i * ci, ei + a8r * ci + a8i * cr


def _scan_fwd(a_rows, bu_re, bu_im):
    t, n = bu_re.shape
    tb, w = _blk(t, TB_SCAN), W_SCAN
    ntile = tb // SUBLANES

    def body(a_ref, br_ref, bi_ref, sr_ref, si_ref, car, cai):
        @pl.when(pl.program_id(1) == 0)
        def _():
            car[...] = jnp.zeros(car.shape, F32)
            cai[...] = jnp.zeros(cai.shape, F32)
        consts = _scan_consts(a_ref, False)

        def tile(i, carry):
            o = pl.multiple_of(i * SUBLANES, SUBLANES)
            outr, outi, ncr, nci = _scan_tile(br_ref[pl.ds(o, SUBLANES), :], bi_ref[pl.ds(o, SUBLANES), :],
                                              carry[0], carry[1], consts, False)
            sr_ref[pl.ds(o, SUBLANES), :] = outr
            si_ref[pl.ds(o, SUBLANES), :] = outi
            return ncr, nci

        def tiles(i, carry):
            for s in range(SCAN_UNROLL):
                carry = tile(i * SCAN_UNROLL + s, carry)
            return carry

        cr, ci = lax.fori_loop(0, ntile // SCAN_UNROLL, tiles, (car[...], cai[...]))
        car[...] = cr
        cai[...] = ci

    spec = pl.BlockSpec((tb, w), lambda s, k: (k, s))
    sds = jax.ShapeDtypeStruct((t, n), F32)
    return pl.pallas_call(
        body, name="scan_fwd", grid=(n // w, t // tb), out_shape=(sds, sds),
        in_specs=[pl.BlockSpec((SUBLANES, w), lambda s, k: (0, s)), spec, spec], out_specs=(spec, spec),
        scratch_shapes=[pltpu.VMEM((SUBLANES, w), F32), pltpu.VMEM((SUBLANES, w), F32)],
        compiler_params=_cparams(("parallel", "arbitrary"), VMEM_MID),
    )(a_rows, bu_re, bu_im)


def _scan_bwd(a_rows, g_re, g_im, s_re, s_im):
    t, n = g_re.shape
    tb, w = _blk(t, TB_SCAN), W_SCAN
    ntile = tb // SUBLANES
    nt = t // tb

    def body(a_ref, gr_ref, gi_ref, sr_ref, si_ref, or_ref, oi_ref, gar_ref, gai_ref, car, cai):
        @pl.when(pl.program_id(1) == 0)
        def _():
            car[...] = jnp.zeros(car.shape, F32)
            cai[...] = jnp.zeros(cai.shape, F32)
            gar_ref[...] = jnp.zeros(gar_ref.shape, F32)
            gai_ref[...] = jnp.zeros(gai_ref.shape, F32)
        consts = _scan_consts(a_ref, True)
        row = consts[0]

        def tile(i, carry):
            cr, ci, accr, acci = carry
            o = pl.multiple_of((ntile - 1 - i) * SUBLANES, SUBLANES)
            outr, outi, ncr, nci = _scan_tile(gr_ref[pl.ds(o, SUBLANES), :], gi_ref[pl.ds(o, SUBLANES), :],
                                              cr, ci, consts, True)
            or_ref[pl.ds(o, SUBLANES), :] = outr
            oi_ref[pl.ds(o, SUBLANES), :] = outi
            gnr = jnp.where(row == SUBLANES - 1, cr, pltpu.roll(outr, SUBLANES - 1, 0))
            gni = jnp.where(row == SUBLANES - 1, ci, pltpu.roll(outi, SUBLANES - 1, 0))
            sr = sr_ref[pl.ds(o, SUBLANES), :]
            si = si_ref[pl.ds(o, SUBLANES), :]
            return ncr, nci, accr + sr * gnr + si * gni, acci + sr * gni - si * gnr

        def tiles(i, carry):
            for s in range(SCAN_UNROLL):
                carry = tile(i * SCAN_UNROLL + s, carry)
            return carry

        cr, ci, accr, acci = lax.fori_loop(0, ntile // SCAN_UNROLL, tiles,
                                           (car[...], cai[...], gar_ref[...], gai_ref[...]))
        car[...] = cr
        cai[...] = ci
        gar_ref[...] = accr
        gai_ref[...] = acci

    spec = pl.BlockSpec((tb, w), lambda s, k: (nt - 1 - k, s))
    aspec = pl.BlockSpec((SUBLANES, w), lambda s, k: (0, s))
    sds = jax.ShapeDtypeStruct((t, n), F32)
    asds = jax.ShapeDtypeStruct((SUBLANES, n), F32)
    return pl.pallas_call(
        body, name="scan_bwd", grid=(n // w, nt), out_shape=(sds, sds, asds, asds),
        in_specs=[aspec, spec, spec, spec, spec], out_specs=(spec, spec, aspec, aspec),
        scratch_shapes=[pltpu.VMEM((SUBLANES, w), F32), pltpu.VMEM((SUBLANES, w), F32)],
        compiler_params=_cparams(("parallel", "arbitrary"), VMEM_MID),
    )(a_rows, g_re, g_im, s_re, s_im)


def _mix_in(x, vec, w_in_st, b_re, b_im):
    t, d = x.shape
    ns, _, nc = w_in_st.shape
    dssm, nstate = b_re.shape
    tb = _blk(t, TB_MIX)

    def body(x_ref, vec_ref, w_ref, bre_ref, bim_ref, proj_ref, bur_ref, bui_ref, h1_ref):
        xv = x_ref[...]
        r = lax.rsqrt(_rowmean(xv * xv) + EPS)
        h = xv * r * vec_ref[0:1, :] * vec_ref[1:2, :] + vec_ref[2:3, :]
        hb = h.astype(BF16)
        h1_ref[...] = hb
        u = None
        for j in range(ns):
            pj = jnp.dot(hb, w_ref[j], preferred_element_type=F32)
            proj_ref[:, j * nc:(j + 1) * nc] = pj
            if j == 0:
                u = pj
        ub = u.astype(BF16)
        bur_ref[...] = jnp.dot(ub, bre_ref[...], preferred_element_type=F32)
        bui_ref[...] = jnp.dot(ub, bim_ref[...], preferred_element_type=F32)

    return pl.pallas_call(
        body, name="mix_in", grid=(t // tb,),
        out_shape=(jax.ShapeDtypeStruct((t, ns * nc), F32), jax.ShapeDtypeStruct((t, nstate), F32),
                   jax.ShapeDtypeStruct((t, nstate), F32), jax.ShapeDtypeStruct((t, d), BF16)),
        in_specs=[_rows(tb, d), _full((SUBLANES, d)), _full(w_in_st.shape), _full(b_re.shape), _full(b_im.shape)],
        out_specs=(_rows(tb, ns * nc), _rows(tb, nstate), _rows(tb, nstate), _rows(tb, d)),
        compiler_params=_cparams(("parallel",), VMEM_BIG),
    )(x, vec, w_in_st, b_re, b_im)


def _head_ms(y, h_ref):
    return _split_dot(y * y, h_ref[...])


def _conv3(x, halo, w_ref):
    return w_ref[0:1, :] * _shift_down(x, halo, 2) + w_ref[1:2, :] * _shift_down(x, halo, 1) + w_ref[2:3, :] * x


def _mix_out(x, proj, s_re, s_im, c_re, c_im, v512, convw, glu_w, h16, h64, w_out, vd):
    t, d = x.shape
    dh = c_re.shape[1]
    nstate = s_re.shape[1]
    tb = _blk(t, TB_MIX)

    def body(x_ref, u_ref, bg_ref, cg_ref, v_ref, cgh_ref, vh_ref, sr_ref, si_ref, cre_ref, cim_ref, p_ref,
             cw_ref, gw_ref, h16_ref, h64_ref, wo_ref, vd_ref, y1_ref, o_ref, x2_ref):
        i = pl.program_id(0)
        u = u_ref[...]
        ys = _dot(sr_ref[...], cre_ref[...]) - _dot(si_ref[...], cim_ref[...])
        y1 = ys + p_ref[0:1, :] * u
        y1_ref[...] = y1
        z = _gelu(y1)
        q = _dot(z, gw_ref[...]) + p_ref[1:2, :]
        ya = z * _sigmoid(q)
        na = ya * lax.rsqrt(_head_ms(ya, h16_ref) + EPS) * p_ref[2:3, :]
        cv = cg_ref[...] * v_ref[...]
        cvh = jnp.where(i > 0, cgh_ref[...] * vh_ref[...], 0.0)
        yb = bg_ref[...] * _conv3(cv, cvh, cw_ref)
        nb = yb * lax.rsqrt(_head_ms(yb, h64_ref) + EPS) * p_ref[3:4, :]
        o = _dot(na, wo_ref[0:dh, :]) + _dot(nb, wo_ref[dh:2 * dh, :])
        o_ref[...] = o
        on = o * lax.rsqrt(_rowmean(o * o) + EPS) * vd_ref[0:1, :]
        x2_ref[...] = x_ref[...] + vd_ref[1:2, :] * on

    return pl.pallas_call(
        body, name="mix_out", grid=(t // tb,),
        out_shape=(jax.ShapeDtypeStruct((t, dh), F32), jax.ShapeDtypeStruct((t, d), F32),
                   jax.ShapeDtypeStruct((t, d), F32)),
        in_specs=[_rows(tb, d), _rows(tb, dh, 0), _rows(tb, dh, 1), _rows(tb, dh, 2), _rows(tb, dh, 3),
                  _halo_prev(tb, dh, 2), _halo_prev(tb, dh, 3), _rows(tb, nstate), _rows(tb, nstate),
                  _full(c_re.shape), _full(c_im.shape), _full(v512.shape), _full(convw.shape), _full(glu_w.shape),
                  _full(h16.shape), _full(h64.shape), _full(w_out.shape), _full(vd.shape)],
        out_specs=(_rows(tb, dh), _rows(tb, d), _rows(tb, d)),
        compiler_params=_cparams(("parallel",), VMEM_BIG),
    )(x, proj, proj, proj, proj, proj, proj, s_re, s_im, c_re, c_im, v512, convw, glu_w, h16, h64, w_out, vd)


def _ffn_up(x2, vec, w_up_st):
    t, d = x2.shape
    ns, _, nc = w_up_st.shape
    tb = _blk(t, TB_FFN)

    def body(x_ref, vec_ref, w_ref, up_ref, h2_ref):
        xv = x_ref[...]
        r = lax.rsqrt(_rowmean(xv * xv) + EPS)
        h = xv * r * vec_ref[0:1, :] * vec_ref[1:2, :] + vec_ref[2:3, :]
        hb = h.astype(BF16)
        h2_ref[...] = hb
        for j in range(ns):
            up_ref[:, j * nc:(j + 1) * nc] = jnp.dot(hb, w_ref[j], preferred_element_type=F32)

    return pl.pallas_call(
        body, name="ffn_up", grid=(t // tb,),
        out_shape=(jax.ShapeDtypeStruct((t, ns * nc), F32), jax.ShapeDtypeStruct((t, d), BF16)),
        in_specs=[_rows(tb, d), _full((SUBLANES, d)), _full(w_up_st.shape)],
        out_specs=(_rows(tb, ns * nc), _rows(tb, d)),
        compiler_params=_cparams(("parallel",), VMEM_BIG),
    )(x2, vec, w_up_st)


def _ffn_down(up, fw, w_down, x2, tgt, vd):
    t, nh = up.shape
    dff, d = w_down.shape
    tb = _blk(t, TB_FFN)
    inv_d = 1.0 / d

    def body(up_ref, uph_ref, fw_ref, wd_ref, x2_ref, tgt_ref, vd_ref,
             act_ref, ddn_ref, dout_ref, dhid_ref, vec_ref, loss_ref, a_s, vv_s, sg_s, dact_s):
        i = pl.program_id(0)

        def conv_cols(sl):
            x = up_ref[:, sl]
            halo = jnp.where(i > 0, uph_ref[:, sl], 0.0)
            return (fw_ref[0:1, sl] * _shift_down(x, halo, 2) + fw_ref[1:2, sl] * _shift_down(x, halo, 1)
                    + fw_ref[2:3, sl] * x)

        for o in range(0, dff, CW_FFN):
            sl = slice(o, o + CW_FFN)
            a = conv_cols(sl)
            vv = conv_cols(slice(dff + o, dff + o + CW_FFN))
            sg = _sigmoid(a)
            a_s[:, sl] = a
            vv_s[:, sl] = vv
            sg_s[:, sl] = sg
            act_ref[:, sl] = (a * sg * vv).astype(BF16)
        dn = jnp.dot(act_ref[...], wd_ref[...], preferred_element_type=F32)
        r3 = lax.rsqrt(_rowmean(dn * dn) + EPS)
        xn = dn * r3
        g = vd_ref[0:1, :]
        gt2 = vd_ref[1:2, :]
        dnn = xn * g
        diff = x2_ref[...] + gt2 * dnn - tgt_ref[...]
        part = 0.5 * inv_d * jnp.sum(diff * diff)

        @pl.when(i == 0)
        def _():
            loss_ref[...] = jnp.zeros(loss_ref.shape, F32)
        loss_ref[...] += part
        dout = diff * inv_d
        dout_ref[...] = dout
        ddnn = dout * gt2
        _acc_rows(vec_ref, i == 0, [_colsum(dout * dnn), _colsum(ddnn * xn)])
        dxn = ddnn * g
        ddn = r3 * (dxn - xn * _rowmean(dxn * xn))
        ddnb = ddn.astype(BF16)
        ddn_ref[...] = ddnb
        dact_s[...] = lax.dot_general(ddnb, wd_ref[...], (((1,), (1,)), ((), ())), preferred_element_type=F32)
        for o in range(0, dff, CW_FFN):
            sl = slice(o, o + CW_FFN)
            a, vv, sg, dact = a_s[:, sl], vv_s[:, sl], sg_s[:, sl], dact_s[:, sl]
            dhid_ref[:, sl] = dact * vv * sg * (1.0 + a * (1.0 - sg))
            dhid_ref[:, dff + o:dff + o + CW_FFN] = dact * (a * sg)

    return pl.pallas_call(
        body, name="ffn_down", grid=(t // tb,),
        scratch_shapes=[pltpu.VMEM((tb, dff), F32)] * 4,
        out_shape=(jax.ShapeDtypeStruct((t, dff), BF16), jax.ShapeDtypeStruct((t, d), BF16),
                   jax.ShapeDtypeStruct((t, d), F32), jax.ShapeDtypeStruct((t, nh), F32),
                   jax.ShapeDtypeStruct((SUBLANES, d), F32), jax.ShapeDtypeStruct((SUBLANES, 128), F32)),
        in_specs=[_rows(tb, nh), _halo_prev(tb, nh), _full(fw.shape), _full(w_down.shape), _rows(tb, d),
                  _rows(tb, d), _full(vd.shape)],
        out_specs=(_rows(tb, dff), _rows(tb, d), _rows(tb, d), _rows(tb, nh), _full((SUBLANES, d)),
                   _full((SUBLANES, 128))),
        compiler_params=_cparams(("arbitrary",), VMEM_BIG),
    )(up, up, fw, w_down, x2, tgt, vd)


def _ffn_up_bwd(dhid, up, fw, x2, dout, vec, w_up_st):
    t, nh = dhid.shape
    d = x2.shape[1]
    ns, _, nc = w_up_st.shape
    tb = _blk(t, TB_FFN)
    nblk = t // tb

    def body(dh_ref, dhn_ref, up_ref, fw_ref, x2_ref, dout_ref, vec_ref, w_ref,
             dx2_ref, dup_ref, vp_ref, df_ref):
        i = pl.program_id(0)

        @pl.when(i == 0)
        def _():
            df_ref[...] = jnp.zeros(df_ref.shape, F32)
        for o in range(0, nh, CW_FFN):
            sl = slice(o, o + CW_FFN)
            dh = dh_ref[:, sl]
            dhn = jnp.where(i < nblk - 1, dhn_ref[:, sl], 0.0)
            dh1 = _shift_up(dh, dhn, 1)
            dh2 = _shift_up(dh, dhn, 2)
            dup_ref[:, sl] = (fw_ref[2:3, sl] * dh + fw_ref[1:2, sl] * dh1 + fw_ref[0:1, sl] * dh2).astype(BF16)
            up_v = up_ref[:, sl]
            df_ref[0:1, sl] += _colsum(dh2 * up_v)
            df_ref[1:2, sl] += _colsum(dh1 * up_v)
            df_ref[2:3, sl] += _colsum(dh * up_v)
        dh2 = None
        for j in range(ns):
            pj = lax.dot_general(dup_ref[:, j * nc:(j + 1) * nc], w_ref[j], (((1,), (1,)), ((), ())),
                                 preferred_element_type=F32)
            dh2 = pj if dh2 is None else dh2 + pj
        xv = x2_ref[...]
        r = lax.rsqrt(_rowmean(xv * xv) + EPS)
        xn = xv * r
        g = vec_ref[0:1, :]
        hg = xn * g
        dhg = dh2 * vec_ref[1:2, :]
        _acc_rows(vp_ref, i == 0, [_colsum(dh2), _colsum(dh2 * hg), _colsum(dhg * xn)])
        dxn = dhg * g
        dx2_ref[...] = dout_ref[...] + r * (dxn - xn * _rowmean(dxn * xn))

    return pl.pallas_call(
        body, name="ffn_up_bwd", grid=(nblk,),
        out_shape=(jax.ShapeDtypeStruct((t, d), F32), jax.ShapeDtypeStruct((t, nh), BF16),
                   jax.ShapeDtypeStruct((SUBLANES, d), F32), jax.ShapeDtypeStruct((SUBLANES, nh), F32)),
        in_specs=[_rows(tb, nh), _halo_next(tb, nh, t), _rows(tb, nh), _full(fw.shape),
                  _rows(tb, d), _rows(tb, d), _full(vec.shape), _full(w_up_st.shape)],
        out_specs=(_rows(tb, d), _rows(tb, nh), _full((SUBLANES, d)), _full((SUBLANES, nh))),
        compiler_params=_cparams(("arbitrary",), VMEM_BIG),
    )(dhid, dhid, up, fw, x2, dout, vec, w_up_st)


def _mix_out_bwd(dx2, o, y1, proj, c_re, c_im, v512, convw, glu_w, h16, h64, w_out, vd):
    t, d = dx2.shape
    dh = y1.shape[1]
    nstate = c_re.shape[0]
    tb = _blk(t, TB_MIX)

    def body(dx2_ref, o_ref, y1_ref, u_ref, bg_ref, cg_ref, v_ref, cgh_ref, vh_ref, cre_ref, cim_ref, p_ref,
             cw_ref, gw_ref, h16_ref, h64_ref, wo_ref, vd_ref,
             do_ref, ycat_ref, z_ref, dq_ref, dy1_ref, gr_ref, gi_ref, dcc_ref, dbg_ref, vpd_ref, vp5_ref):
        i = pl.program_id(0)
        first = i == 0
        ov = o_ref[...]
        ro = lax.rsqrt(_rowmean(ov * ov) + EPS)
        on_ = ov * ro
        g = vd_ref[0:1, :]
        dx2v = dx2_ref[...]
        don = dx2v * vd_ref[1:2, :]
        _acc_rows(vpd_ref, first, [_colsum(dx2v * on_ * g), _colsum(don * on_)])
        dxn = don * g
        dob = (ro * (dxn - on_ * _rowmean(dxn * on_))).astype(BF16)
        do_ref[...] = dob
        dyc_a = lax.dot_general(dob, wo_ref[0:dh, :], (((1,), (1,)), ((), ())), preferred_element_type=F32)
        dyc_b = lax.dot_general(dob, wo_ref[dh:2 * dh, :], (((1,), (1,)), ((), ())), preferred_element_type=F32)
        y1v = y1_ref[...]
        u = u_ref[...]
        z = _gelu(y1v)
        zb = z.astype(BF16)
        z_ref[...] = zb
        sg = _sigmoid(jnp.dot(zb, gw_ref[...], preferred_element_type=F32) + p_ref[1:2, :])
        ya = z * sg
        ra = lax.rsqrt(_head_ms(ya, h16_ref) + EPS)
        yan = ya * ra
        ga = p_ref[2:3, :]
        ycat_ref[:, 0:dh] = (yan * ga).astype(BF16)
        dyn = dyc_a * ga
        dya = ra * (dyn - yan * _split_dot(dyn * yan, h16_ref[...]))
        dq = dya * z * sg * (1.0 - sg)
        dqb = dq.astype(BF16)
        dq_ref[...] = dqb
        dz = dya * sg + lax.dot_general(dqb, gw_ref[...], (((1,), (1,)), ((), ())), preferred_element_type=F32)
        dy1 = dz * _gelu_grad(y1v)
        dy1_ref[...] = dy1
        dy1b = dy1.astype(BF16)
        gr_ref[...] = lax.dot_general(dy1b, cre_ref[...], (((1,), (1,)), ((), ())), preferred_element_type=F32)
        gi_ref[...] = -lax.dot_general(dy1b, cim_ref[...], (((1,), (1,)), ((), ())), preferred_element_type=F32)
        bg = bg_ref[...]
        cv = cg_ref[...] * v_ref[...]
        cvh = jnp.where(i > 0, cgh_ref[...] * vh_ref[...], 0.0)
        cv1 = _shift_down(cv, cvh, 1)
        cv2 = _shift_down(cv, cvh, 2)
        cc = cw_ref[0:1, :] * cv2 + cw_ref[1:2, :] * cv1 + cw_ref[2:3, :] * cv
        yb = bg * cc
        rb = lax.rsqrt(_head_ms(yb, h64_ref) + EPS)
        ybn = yb * rb
        gb = p_ref[3:4, :]
        ycat_ref[:, dh:2 * dh] = (ybn * gb).astype(BF16)
        dynb = dyc_b * gb
        dyb = rb * (dynb - ybn * _split_dot(dynb * ybn, h64_ref[...]))
        dcc = dyb * bg
        dbg_ref[...] = dyb * cc
        dcc_ref[...] = dcc
        _acc_rows(vp5_ref, first, [_colsum(dyc_a * yan), _colsum(dyc_b * ybn), _colsum(dq), _colsum(dy1 * u),
                                   _colsum(dcc * cv2), _colsum(dcc * cv1), _colsum(dcc * cv)])

    return pl.pallas_call(
        body, name="mix_out_bwd", grid=(t // tb,),
        out_shape=(jax.ShapeDtypeStruct((t, d), BF16), jax.ShapeDtypeStruct((t, 2 * dh), BF16),
                   jax.ShapeDtypeStruct((t, dh), BF16), jax.ShapeDtypeStruct((t, dh), BF16),
                   jax.ShapeDtypeStruct((t, dh), F32), jax.ShapeDtypeStruct((t, nstate), F32),
                   jax.ShapeDtypeStruct((t, nstate), F32), jax.ShapeDtypeStruct((t, dh), F32),
                   jax.ShapeDtypeStruct((t, dh), F32), jax.ShapeDtypeStruct((SUBLANES, d), F32),
                   jax.ShapeDtypeStruct((SUBLANES, dh), F32)),
        in_specs=[_rows(tb, d), _rows(tb, d), _rows(tb, dh), _rows(tb, dh, 0), _rows(tb, dh, 1), _rows(tb, dh, 2),
                  _rows(tb, dh, 3), _halo_prev(tb, dh, 2), _halo_prev(tb, dh, 3), _full(c_re.shape), _full(c_im.shape),
                  _full(v512.shape), _full(convw.shape), _full(glu_w.shape), _full(h16.shape), _full(h64.shape),
                  _full(w_out.shape), _full(vd.shape)],
        out_specs=(_rows(tb, d), _rows(tb, 2 * dh), _rows(tb, dh), _rows(tb, dh), _rows(tb, dh), _rows(tb, nstate),
                   _rows(tb, nstate), _rows(tb, dh), _rows(tb, dh), _full((SUBLANES, d)), _full((SUBLANES, dh))),
        compiler_params=_cparams(("arbitrary",), VMEM_BIG),
    )(dx2, o, y1, proj, proj, proj, proj, proj, proj, c_re, c_im, v512, convw, glu_w, h16, h64, w_out, vd)


def _mix_in_bwd(gt_re, gt_im, b_re, b_im, dy1, dcc, dbg, proj, x, dx2, vec, v512, convw, w_in_st):
    t, d = x.shape
    dh = dy1.shape[1]
    nstate = gt_re.shape[1]
    ns, _, nc = w_in_st.shape
    tb = _blk(t, TB_MIX)
    nblk = t // tb

    def body(gr_ref, gi_ref, bre_ref, bim_ref, dy1_ref, dcc_ref, dccn_ref, dbg_ref, cg_ref, v_ref, x_ref, dx2_ref,
             vec_ref, p_ref, cw_ref, w_ref, gx_ref, dproj_ref, vp_ref):
        i = pl.program_id(0)
        du = (dy1_ref[...] * p_ref[0:1, :]
              + lax.dot_general(gr_ref[...].astype(BF16), bre_ref[...], (((1,), (1,)), ((), ())),
                                preferred_element_type=F32)
              + lax.dot_general(gi_ref[...].astype(BF16), bim_ref[...], (((1,), (1,)), ((), ())),
                                preferred_element_type=F32))
        dcc = dcc_ref[...]
        dccn = jnp.where(i < nblk - 1, dccn_ref[...], 0.0)
        dcv = (cw_ref[2:3, :] * dcc + cw_ref[1:2, :] * _shift_up(dcc, dccn, 1)
               + cw_ref[0:1, :] * _shift_up(dcc, dccn, 2))
        parts = [du, dbg_ref[...], dcv * v_ref[...], dcv * cg_ref[...]]
        dh1 = None
        for j in range(ns):
            pb = parts[j].astype(BF16)
            dproj_ref[:, j * nc:(j + 1) * nc] = pb
            pj = lax.dot_general(pb, w_ref[j], (((1,), (1,)), ((), ())), preferred_element_type=F32)
            dh1 = pj if dh1 is None else dh1 + pj
        xv = x_ref[...]
        r = lax.rsqrt(_rowmean(xv * xv) + EPS)
        xn = xv * r
        g = vec_ref[0:1, :]
        hg = xn * g
        dhg = dh1 * vec_ref[1:2, :]
        _acc_rows(vp_ref, i == 0, [_colsum(dh1), _colsum(dh1 * hg), _colsum(dhg * xn)])
        dxn = dhg * g
        gx_ref[...] = dx2_ref[...] + r * (dxn - xn * _rowmean(dxn * xn))

    assert nc == dh and ns == 4
    return pl.pallas_call(
        body, name="mix_in_bwd", grid=(nblk,),
        out_shape=(jax.ShapeDtypeStruct((t, d), F32), jax.ShapeDtypeStruct((t, ns * nc), BF16),
                   jax.ShapeDtypeStruct((SUBLANES, d), F32)),
        in_specs=[_rows(tb, nstate), _rows(tb, nstate), _full(b_re.shape), _full(b_im.shape), _rows(tb, dh),
                  _rows(tb, dh), _halo_next(tb, dh, t), _rows(tb, dh), _rows(tb, dh, 2), _rows(tb, dh, 3),
                  _rows(tb, d), _rows(tb, d), _full(vec.shape), _full(v512.shape), _full(convw.shape),
                  _full(w_in_st.shape)],
        out_specs=(_rows(tb, d), _rows(tb, ns * nc), _full((SUBLANES, d))),
        compiler_params=_cparams(("arbitrary",), VMEM_BIG),
    )(gt_re, gt_im, b_re, b_im, dy1, dcc, dcc, dbg, proj, proj, x, dx2, vec, v512, convw, w_in_st)


def _matmul_tn(a, b, m, bn, out_dtype, name, a_col=0, after=None):
    t = a.shape[0]
    n = b.shape[1]
    bt = _blk(t, TB_TN)
    nk = t // bt
    extra = [] if after is None else [after]

    def body(a_ref, b_ref, *rest):
        o_ref, acc_ref = rest[-2:]
        k = pl.program_id(1)

        @pl.when(k == 0)
        def _():
            acc_ref[...] = jnp.zeros(acc_ref.shape, F32)
        acc_ref[...] += _dot_tn(a_ref[...], b_ref[...])

        @pl.when(k == nk - 1)
        def _():
            o_ref[...] = acc_ref[...].astype(out_dtype)

    return pl.pallas_call(
        body, name=name, grid=(n // bn, nk),
        out_shape=jax.ShapeDtypeStruct((n // bn, m, bn), out_dtype),
        in_specs=[pl.BlockSpec((bt, m), lambda j, k: (k, a_col)), pl.BlockSpec((bt, bn), lambda j, k: (k, j))]
        + [pl.BlockSpec(memory_space=pl.ANY)] * len(extra),
        out_specs=pl.BlockSpec((None, m, bn), lambda j, k: (j, 0, 0)),
        scratch_shapes=[pltpu.VMEM((m, bn), F32)],
        compiler_params=_cparams(("parallel", "arbitrary"), VMEM_BIG),
    )(a, b, *extra)


def _ssm_bgrad(d_bre, d_bim, bt_re, bt_im, rows_in, fold):
    gh, gp = d_bre.shape
    nb = 4
    cb = gp // nb
    p = fold.shape[1]

    def body(dr_ref, di_ref, br_ref, bi_ref, rin_ref, f_ref, dbr_ref, dbi_ref, rout_ref):
        j = pl.program_id(0)
        row = lax.broadcasted_iota(jnp.int32, (gh, cb), 0)
        col = lax.broadcasted_iota(jnp.int32, (gh, cb), 1) + j * cb
        mask = (row >> 4) == (col >> 6)
        gr = jnp.where(mask, dr_ref[...], 0.0)
        gi = jnp.where(mask, di_ref[...], 0.0)
        cr, ci = rin_ref[0:1, :], rin_ref[1:2, :]

        @pl.when(j == 0)
        def _():
            dbr_ref[...] = jnp.zeros(dbr_ref.shape, F32)
            dbi_ref[...] = jnp.zeros(dbi_ref.shape, F32)
        dbr_ref[...] += _split3_dot(cr * gr + ci * gi, f_ref[...])
        dbi_ref[...] += _split3_dot(cr * gi - ci * gr, f_ref[...])
        br, bi = br_ref[...], bi_ref[...]
        rout_ref[...] = jnp.zeros(rout_ref.shape, F32)
        rout_ref[0:1, :] = _colsum(br * gr + bi * gi)
        rout_ref[1:2, :] = _colsum(br * gi - bi * gr)

    bspec = pl.BlockSpec((gh, cb), lambda j: (0, j))
    rspec = pl.BlockSpec((SUBLANES, cb), lambda j: (0, j))
    return pl.pallas_call(
        body, name="ssm_bgrad", grid=(nb,),
        out_shape=(jax.ShapeDtypeStruct((gh, p), F32), jax.ShapeDtypeStruct((gh, p), F32),
                   jax.ShapeDtypeStruct((SUBLANES, gp), F32)),
        in_specs=[bspec, bspec, bspec, bspec, rspec, pl.BlockSpec((cb, p), lambda j: (j, 0))],
        out_specs=(_full((gh, p)), _full((gh, p)), rspec),
        compiler_params=_cparams(("arbitrary",)),
    )(d_bre, d_bim, bt_re, bt_im, rows_in, fold)


def _ssm_cgrad(d_cre, d_cim, fold):
    gp, gh = d_cre.shape
    nb = 4
    rb = gp // nb
    h = fold.shape[1]

    def body(dr_ref, di_ref, f_ref, cr_ref, ci_ref):
        j = pl.program_id(0)
        row = lax.broadcasted_iota(jnp.int32, (rb, gh), 0) + j * rb
        col = lax.broadcasted_iota(jnp.int32, (rb, gh), 1)
        mask = (row >> 6) == (col >> 4)
        cr_ref[...] = _split3_dot(jnp.where(mask, dr_ref[...], 0.0), f_ref[...])
        ci_ref[...] = -_split3_dot(jnp.where(mask, di_ref[...], 0.0), f_ref[...])

    cspec = pl.BlockSpec((rb, gh), lambda j: (j, 0))
    ospec = pl.BlockSpec((rb, h), lambda j: (j, 0))
    return pl.pallas_call(
        body, name="ssm_cgrad", grid=(nb,),
        out_shape=(jax.ShapeDtypeStruct((gp, h), F32),) * 2,
        in_specs=[cspec, cspec, _full(fold.shape)], out_specs=(ospec, ospec),
        compiler_params=_cparams(("parallel",)),
    )(d_cre, d_cim, fold)


def _ssm_lamgrad(lam_re, lam_im, log_step, abar_re, abar_im, coef_re, coef_im, gc_re, gc_im, ga_re, ga_im):
    g, p = lam_re.shape

    def body(lr_ref, li_ref, ls_ref, ar_ref, ai_ref, cr_ref, ci_ref, gcr_ref, gci_ref, gar_ref, gai_ref,
             dlr_ref, dli_ref, dls_ref):
        lam_raw = lr_ref[...]
        lr = jnp.minimum(lam_raw, LAMBDA_RE_MAX)
        li = li_ref[...]
        st = jnp.exp(ls_ref[...])
        den = lr * lr + li * li
        gcr, gci = gcr_ref[...], gci_ref[...]
        gab_r = gar_ref[...] + (lr * gcr - li * gci) / den
        gab_i = gai_ref[...] + (lr * gci + li * gcr) / den
        cr, ci = cr_ref[...], ci_ref[...]
        wr = -(cr * lr + ci * li) / den
        wi = -(ci * lr - cr * li) / den
        gl_r = wr * gcr + wi * gci
        gl_i = wr * gci - wi * gcr
        ar, ai = ar_ref[...], ai_ref[...]
        gw_r = ar * gab_r + ai * gab_i
        gw_i = ar * gab_i - ai * gab_r
        gl_r = gl_r + st * gw_r
        gl_i = gl_i + st * gw_i
        pass_through = jnp.where(lam_raw < LAMBDA_RE_MAX, 1.0, jnp.where(lam_raw == LAMBDA_RE_MAX, 0.5, 0.0))
        dlr_ref[...] = gl_r * pass_through
        dli_ref[...] = gl_i
        dls_ref[...] = st * jnp.sum(lr * gw_r + li * gw_i, axis=1, keepdims=True)

    sds = jax.ShapeDtypeStruct((g, p), F32)
    return pl.pallas_call(body, name="ssm_lamgrad", out_shape=(sds, sds, jax.ShapeDtypeStruct((g, 1), F32)))(
        lam_re, lam_im, log_step, abar_re, abar_im, coef_re, coef_im, gc_re, gc_im, ga_re, ga_im)


def _adamw_math(w, g, m, v):
    m = ADAM_B1 * m + (1.0 - ADAM_B1) * g
    v = ADAM_B2 * v + (1.0 - ADAM_B2) * (g * g)
    m_hat = m / (1.0 - ADAM_B1 ** ADAM_STEP)
    v_hat = v / (1.0 - ADAM_B2 ** ADAM_STEP)
    delta = -ADAM_LR * (m_hat / (jnp.sqrt(v_hat) + ADAM_EPS) + ADAM_WD * w)
    return delta, m, v


def _adamw_big(p_mine, p_sib, w, m, v, name):
    r, c = w.shape
    rb = 64 if r % 64 == 0 else r

    def body(a_ref, b_ref, w_ref, m_ref, v_ref, g_ref, d_ref, mo_ref, vo_ref):
        g = a_ref[...] + b_ref[...]
        g_ref[...] = g
        d_ref[...], mo_ref[...], vo_ref[...] = _adamw_math(w_ref[...], g, m_ref[...], v_ref[...])

    spec = pl.BlockSpec((rb, c), lambda i: (i, 0))
    sds = jax.ShapeDtypeStruct((r, c), F32)
    return pl.pallas_call(
        body, name=name, grid=(r // rb,), out_shape=(sds,) * 4, in_specs=[spec] * 5, out_specs=(spec,) * 4,
        compiler_params=_cparams(("parallel",)),
    )(p_mine, p_sib, w, m, v)


def _sum_blocks(stack, name):
    n, r, c = stack.shape
    rb = 64 if r % 64 == 0 else r

    def body(s_ref, o_ref):
        acc = s_ref[0].astype(F32)
        for k in range(1, n):
            acc = acc + s_ref[k].astype(F32)
        o_ref[...] = acc

    return pl.pallas_call(
        body, name=name, grid=(r // rb,), out_shape=jax.ShapeDtypeStruct((r, c), F32),
        in_specs=[pl.BlockSpec((n, rb, c), lambda i: (0, i, 0))], out_specs=pl.BlockSpec((rb, c), lambda i: (i, 0)),
        compiler_params=_cparams(("parallel",)),
    )(stack)


def _adamw_ada(c_all, dmod_cols, w, m, v):
    d, n = w.shape
    bn = 512

    def body(c_ref, dm_ref, w_ref, m_ref, v_ref, g_ref, d_ref, mo_ref, vo_ref):
        cc = c_ref[...]
        g = _dot_tn(cc * _sigmoid(cc), dm_ref[...])
        g_ref[...] = g
        d_ref[...], mo_ref[...], vo_ref[...] = _adamw_math(w_ref[...], g, m_ref[...], v_ref[...])

    spec = pl.BlockSpec((d, bn), lambda j: (0, j))
    sds = jax.ShapeDtypeStruct((d, n), F32)
    return pl.pallas_call(
        body, name="adamw_ada", grid=(n // bn,), out_shape=(sds,) * 4,
        in_specs=[_full((N_DEV, d)), pl.BlockSpec((N_DEV, bn), lambda j: (0, j)), spec, spec, spec],
        out_specs=(spec,) * 4, compiler_params=_cparams(("parallel",)),
    )(c_all, dmod_cols, w, m, v)


def _adamw_small(items):
    n = len(items)

    def body(*refs):
        ins, outs = refs[:4 * n], refs[4 * n:]
        for k in range(n):
            w_ref, g_ref, m_ref, v_ref = ins[4 * k:4 * k + 4]
            outs[3 * k][...], outs[3 * k + 1][...], outs[3 * k + 2][...] = _adamw_math(
                w_ref[...], g_ref[...], m_ref[...], v_ref[...])

    flat = [a for it in items for a in it]
    out_shape = tuple(jax.ShapeDtypeStruct(it[0].shape, F32) for it in items for _ in range(3))
    res = pl.pallas_call(body, name="adamw_small", out_shape=out_shape)(*flat)
    return [tuple(res[3 * k:3 * k + 3]) for k in range(n)]


def _rows8(*rows):
    c = rows[0].shape[-1]
    pad = jnp.zeros((SUBLANES - len(rows), c), F32)
    return jnp.concatenate([r.reshape(1, c) for r in rows] + [pad], axis=0)


def _to_rows(a, width):
    flat = a.reshape(-1)
    n = -(-flat.shape[0] // width)
    flat = jnp.pad(flat, (0, n * width - flat.shape[0]))
    return flat.reshape(n, width)


def kernel(x, c, w_ada, b_ada, g_pre_mix, g_post_mix, w_in, ssm_lam_re, ssm_lam_im, ssm_log_step, ssm_b_re, ssm_b_im, ssm_c_re, ssm_c_im, ssm_d, glu_w, glu_b, g_out_ssm, conv_w, g_out_conv, w_out, g_pre_ffn, g_post_ffn, w_up, ffn_conv_w, w_down, loss_target, m_w_ada, m_b_ada, m_g_pre_mix, m_g_post_mix, m_w_in, m_ssm_lam_re, m_ssm_lam_im, m_ssm_log_step, m_ssm_b_re, m_ssm_b_im, m_ssm_c_re, m_ssm_c_im, m_ssm_d, m_glu_w, m_glu_b, m_g_out_ssm, m_conv_w, m_g_out_conv, m_w_out, m_g_pre_ffn, m_g_post_ffn, m_w_up, m_ffn_conv_w, m_w_down, v_w_ada, v_b_ada, v_g_pre_mix, v_g_post_mix, v_w_in, v_ssm_lam_re, v_ssm_lam_im, v_ssm_log_step, v_ssm_b_re, v_ssm_b_im, v_ssm_c_re, v_ssm_c_im, v_ssm_d, v_glu_w, v_glu_b, v_g_out_ssm, v_conv_w, v_g_out_conv, v_w_out, v_g_pre_ffn, v_g_post_ffn, v_w_up, v_ffn_conv_w, v_w_down):
    xs = x[0]
    tgt = loss_target[0]
    t, d = xs.shape
    xi, yi, ci = lax.axis_index("x"), lax.axis_index("y"), lax.axis_index("c")
    chip = 2 * xi + yi
    dev = 2 * chip + ci

    n_groups, n_state = ssm_lam_re.shape[1:]
    n_gch = ssm_b_re.shape[3]
    d_ssm = n_groups * n_gch
    gp = n_groups * n_state
    n_ada = w_ada.shape[2]
    d_ff = w_down.shape[1] * N_CHIPS
    n_upc = w_up.shape[2]

    w_names = ("w_in", "glu_w", "w_out", "w_up", "w_down")
    c_gath, _ = _allgather8(jnp.broadcast_to(c, (SUBLANES, d)), SUBLANES, "gather_c")
    c_all = c_gath.reshape(N_DEV, SUBLANES, d)[:, 0, :]
    b_sh = lax.dynamic_slice(b_ada, (0, chip * n_ada), (1, n_ada))
    mod_sh = _mod_shard(c_all, w_ada[0], b_sh)

    def pad8(a):
        return jnp.concatenate([a, jnp.zeros((SUBLANES - a.shape[0], a.shape[1]), a.dtype)], axis=0)

    w_names = ("mod", "conv_w", "ffn_conv_w", "w_in", "glu_w", "w_out", "w_up", "w_down")
    w_own = [mod_sh, pad8(conv_w[0]), pad8(ffn_conv_w[0])]
    w_own += [w[0].astype(BF16) for w in (w_in, glu_w, w_out, w_up, w_down)]
    w_send, w_recv, w_src, w_land, _ = _chips_start("weights_start", True, w_own, [_landing(a, chip) for a in w_own])

    def weights(names, after):
        ks = [w_names.index(nm) for nm in names]
        return _chips_wait("weights_wait_" + names[-1], True, [w_send[k] for k in ks], [w_recv[k] for k in ks],
                           [w_src[k] for k in ks], [w_land[k] for k in ks], after)

    lam_re, lam_im = ssm_lam_re[0], ssm_lam_im[0]
    log_step = ssm_log_step[0].reshape(n_groups, 1)
    abar_re, abar_im, coef_re, coef_im = _ssm_prep(lam_re, lam_im, log_step)
    a_rows = _rows8(abar_re.reshape(1, gp), abar_im.reshape(1, gp))
    coef_rows = _rows8(coef_re.reshape(1, gp), coef_im.reshape(1, gp))
    bt_re = jnp.tile(ssm_b_re[0].transpose(0, 2, 1).reshape(d_ssm, n_state), (1, n_groups))
    bt_im = jnp.tile(ssm_b_im[0].transpose(0, 2, 1).reshape(d_ssm, n_state), (1, n_groups))
    ct_re = jnp.tile(ssm_c_re[0].transpose(0, 2, 1).reshape(gp, n_gch), (1, n_groups))
    ct_im = jnp.tile(ssm_c_im[0].transpose(0, 2, 1).reshape(gp, n_gch), (1, n_groups))
    bblk_re, bblk_im, cblk_re, cblk_im = _ssm_blocks(bt_re, bt_im, ct_re, ct_im, coef_rows)

    head = lax.broadcasted_iota(jnp.int32, (d_ssm, d_ssm), 0), lax.broadcasted_iota(jnp.int32, (d_ssm, d_ssm), 1)
    h16 = jnp.where((head[0] // n_gch) == (head[1] // n_gch), 1.0 / n_gch, 0.0).astype(BF16)
    conv_hd = 64
    h64 = jnp.where((head[0] // conv_hd) == (head[1] // conv_hd), 1.0 / conv_hd, 0.0).astype(BF16)

    g_mod, g_cw, g_fw, w_in_st = weights(("mod", "conv_w", "ffn_conv_w", "w_in"), bblk_re)
    mod_all = g_mod.transpose(1, 0, 2).reshape(N_DEV, N_CHIPS * n_ada)
    mod = lax.dynamic_slice(mod_all, (dev, 0), (1, N_CHIPS * n_ada))
    sh1, sc1, gt1, sh2, sc2, gt2 = [mod[:, k * d:(k + 1) * d] for k in range(6)]
    convw_full = pad8(g_cw[:, :3, :].transpose(1, 0, 2).reshape(3, d_ssm))
    fw_full = pad8(g_fw[:, :3, :].transpose(1, 0, 2).reshape(3, N_CHIPS * n_upc))

    v512 = _rows8(ssm_d, glu_b, g_out_ssm, g_out_conv)
    vec1 =_rows8(g_pre_mix, 1.0 + sc1, sh1)
    vd1 = _rows8(g_post_mix, gt1)
    vec2 = _rows8(g_pre_ffn, 1.0 + sc2, sh2)
    vd2 = _rows8(g_post_ffn, gt2)

    proj, bu_re, bu_im, h1b = _mix_in(xs, vec1, w_in_st, bblk_re, bblk_im)
    s_re, s_im = _scan_fwd(a_rows, bu_re, bu_im)
    g_glu, g_wout = weights(("glu_w", "w_out"), s_re)
    glu_full = g_glu.reshape(d_ssm, d_ssm)
    w_out_full = g_wout.reshape(2 * d_ssm, d)
    y1, o_mix, x2 = _mix_out(xs, proj, s_re, s_im, cblk_re, cblk_im, v512, convw_full, glu_full, h16, h64,
                             w_out_full, vd1)
    (w_up_st,) = weights(("w_up",), x2)
    up, h2b = _ffn_up(x2, vec2, w_up_st)
    (g_wdown,) = weights(("w_down",), up)
    w_down_full = g_wdown.reshape(d_ff, d)
    actb, ddnb, dout, dhid, vp_dn, loss_blk = _ffn_down(up, fw_full, w_down_full, x2, tgt, vd2)

    g_names = ("w_down", "w_up", "w_out", "glu_w", "w_in")
    gw_down = _matmul_tn(actb, ddnb, d_ff, d, BF16, "dw_down").reshape(N_CHIPS, d_ff // N_CHIPS, d)
    dx2, dupb, vp_up, df_rows = _ffn_up_bwd(dhid, up, fw_full, x2, dout, vec2, w_up_st)
    gw_up = _matmul_tn(h2b, dupb, d, n_upc, BF16, "dw_up")
    ga_send, ga_recv, ga_src, ga_land, ga_token = _chips_start(
        "grads_start_ffn", False, [gw_down, gw_up],
        [_landing(lax.dynamic_index_in_dim(g, chip, 0, False), chip) for g in (gw_down, gw_up)])
    (dob, ycatb, zb, dqb, dy1, g_re, g_im, dcc, dbg, vp_mo, vp5) = _mix_out_bwd(
        dx2, o_mix, y1, proj, cblk_re, cblk_im, v512, convw_full, glu_full, h16, h64, w_out_full,
        vd1 + ga_token[0:1, 0:1])
    gt_re, gt_im, ga_re8, ga_im8 = _scan_bwd(a_rows, g_re, g_im, s_re, s_im)
    grad_x, dprojb, vp_mi = _mix_in_bwd(gt_re, gt_im, bblk_re, bblk_im, dy1, dcc, dbg, proj, xs, dx2, vec1, v512,
                                        convw_full, w_in_st)
    gw_out = _matmul_tn(ycatb, dob, 2 * d_ssm, d, BF16, "dw_out").reshape(N_CHIPS, 2 * d_ssm // N_CHIPS, d)
    gw_glu = _matmul_tn(zb, dqb, d_ssm, d_ssm, BF16, "dw_glu").reshape(N_CHIPS, d_ssm // N_CHIPS, d_ssm)
    gw_in = _matmul_tn(h1b, dprojb, d, w_in.shape[2], BF16, "dw_in")
    gb_send, gb_recv, gb_src, gb_land, gb_token = _chips_start(
        "grads_start_mix", False, [gw_out, gw_glu, gw_in],
        [_landing(lax.dynamic_index_in_dim(g, chip, 0, False), chip) for g in (gw_out, gw_glu, gw_in)])
    d_bre = _matmul_tn(proj, gt_re, d_ssm, gp, F32, "d_bre", after=gb_token)[0]
    d_bim = _matmul_tn(proj, gt_im, d_ssm, gp, F32, "d_bim")[0]
    d_cre = _matmul_tn(s_re, dy1, gp, d_ssm, F32, "d_cre")[0]
    d_cim = _matmul_tn(s_im, dy1, gp, d_ssm, F32, "d_cim")[0]

    lane = lax.broadcasted_iota(jnp.int32, (gp, n_state), 0), lax.broadcasted_iota(jnp.int32, (gp, n_state), 1)
    fold_b = jnp.where((lane[0] % n_state) == lane[1], 1.0, 0.0).astype(BF16)
    lane_c = lax.broadcasted_iota(jnp.int32, (d_ssm, n_gch), 0), lax.broadcasted_iota(jnp.int32, (d_ssm, n_gch), 1)
    fold_c = jnp.where((lane_c[0] % n_gch) == lane_c[1], 1.0, 0.0).astype(BF16)
    db_re_f, db_im_f, gc_rows = _ssm_bgrad(d_bre, d_bim, bt_re, bt_im, coef_rows, fold_b)
    dc_re_f, dc_im_f = _ssm_cgrad(d_cre, d_cim, fold_c)
    ga_sum = _ga_rowsum(ga_re8, ga_im8)
    g_lam_re, g_lam_im, g_log_step = _ssm_lamgrad(
        lam_re, lam_im, log_step, abar_re, abar_im, coef_re, coef_im,
        gc_rows[0].reshape(n_groups, n_state), gc_rows[1].reshape(n_groups, n_state),
        ga_sum[0].reshape(n_groups, n_state), ga_sum[1].reshape(n_groups, n_state))
    g_b_re = db_re_f.reshape(n_groups, n_gch, n_state).transpose(0, 2, 1)
    g_b_im = db_im_f.reshape(n_groups, n_gch, n_state).transpose(0, 2, 1)
    g_c_re = dc_re_f.reshape(n_groups, n_state, n_gch).transpose(0, 2, 1)
    g_c_im = dc_im_f.reshape(n_groups, n_state, n_gch).transpose(0, 2, 1)

    dmod = jnp.concatenate([vp_mi[0:1], vp_mi[1:2], vp_mo[0:1], vp_up[0:1], vp_up[1:2], vp_dn[0:1]], axis=1)
    small = [
        ("g_pre_mix", vp_mi[2:3]), ("g_post_mix", vp_mo[1:2]), ("g_pre_ffn", vp_up[2:3]), ("g_post_ffn", vp_dn[1:2]),
        ("ssm_lam_re", g_lam_re), ("ssm_lam_im", g_lam_im), ("ssm_log_step", g_log_step),
        ("ssm_b_re", g_b_re), ("ssm_b_im", g_b_im), ("ssm_c_re", g_c_re), ("ssm_c_im", g_c_im),
        ("ssm_d", vp5[3:4]), ("glu_b", vp5[2:3]), ("g_out_ssm", vp5[0:1]), ("g_out_conv", vp5[1:2]),
        ("conv_w", vp5[4:7]), ("ffn_conv_w", df_rows[0:3]),
    ]
    packed, offsets, row = [], {}, 0
    for name, a in small:
        r = _to_rows(a, d)
        offsets[name] = (row, a.shape)
        packed.append(r)
        row += r.shape[0]
    n_small = -(-row // SUBLANES) * SUBLANES
    packed.append(jnp.zeros((n_small - row, d), F32))
    packed.append(pad8(dmod.reshape(6, d)))
    pack = jnp.concatenate(packed, axis=0)
    gath, sums = _allgather8(pack, n_small + SUBLANES, "reduce_small")
    dmod_all = gath.reshape(N_DEV, n_small + SUBLANES, d)[:, n_small:n_small + 6, :].reshape(N_DEV, 6 * d)
    g_b_ada = sums[n_small:n_small + 6].reshape(1, 6 * d)

    def unpack(name):
        r0, shape = offsets[name]
        size = math.prod(shape)
        nrow = -(-size // d)
        return sums[r0:r0 + nrow].reshape(-1)[:size].reshape(shape)

    dmod_cols = lax.dynamic_slice(dmod_all, (0, chip * n_ada), (N_DEV, n_ada))
    ada = _adamw_ada(c_all, dmod_cols, w_ada[0], m_w_ada[0], v_w_ada[0])

    landed = _chips_wait("grads_wait", False, list(ga_send) + list(gb_send), list(ga_recv) + list(gb_recv),
                         list(ga_src) + list(gb_src), list(ga_land) + list(gb_land), ada[0])
    partial = [_sum_blocks(s, "sum_" + nm) for s, nm in zip(landed, g_names)]
    theirs = _swap_sibling(partial, "swap_partials")
    big = {}
    for nm, pm, ps, w_, m_, v_ in zip(
            g_names, partial, theirs,
            (w_down, w_up, w_out, glu_w, w_in), (m_w_down, m_w_up, m_w_out, m_glu_w, m_w_in),
            (v_w_down, v_w_up, v_w_out, v_glu_w, v_w_in)):
        big[nm] = _adamw_big(pm, ps, w_[0], m_[0], v_[0], "adamw_" + nm)

    g_small = {name: unpack(name) for name, _ in small}
    g_small["b_ada"] = g_b_ada
    g_small["conv_w"] = lax.dynamic_slice(g_small["conv_w"], (0, chip * conv_w.shape[2]), (3, conv_w.shape[2]))
    g_small["ffn_conv_w"] = lax.dynamic_slice(g_small["ffn_conv_w"], (0, chip * n_upc), (3, n_upc))
    g_small["ssm_log_step"] = g_small["ssm_log_step"].reshape(1, n_groups)
    small_params = {
        "b_ada": (b_ada, m_b_ada, v_b_ada), "g_pre_mix": (g_pre_mix, m_g_pre_mix, v_g_pre_mix),
        "g_post_mix": (g_post_mix, m_g_post_mix, v_g_post_mix), "ssm_lam_re": (ssm_lam_re, m_ssm_lam_re, v_ssm_lam_re),
        "ssm_lam_im": (ssm_lam_im, m_ssm_lam_im, v_ssm_lam_im),
        "ssm_log_step": (ssm_log_step, m_ssm_log_step, v_ssm_log_step),
        "ssm_b_re": (ssm_b_re, m_ssm_b_re, v_ssm_b_re), "ssm_b_im": (ssm_b_im, m_ssm_b_im, v_ssm_b_im),
        "ssm_c_re": (ssm_c_re, m_ssm_c_re, v_ssm_c_re), "ssm_c_im": (ssm_c_im, m_ssm_c_im, v_ssm_c_im),
        "ssm_d": (ssm_d, m_ssm_d, v_ssm_d), "glu_b": (glu_b, m_glu_b, v_glu_b),
        "g_out_ssm": (g_out_ssm, m_g_out_ssm, v_g_out_ssm), "conv_w": (conv_w, m_conv_w, v_conv_w),
        "g_out_conv": (g_out_conv, m_g_out_conv, v_g_out_conv), "g_pre_ffn": (g_pre_ffn, m_g_pre_ffn, v_g_pre_ffn),
        "g_post_ffn": (g_post_ffn, m_g_post_ffn, v_g_post_ffn),
        "ffn_conv_w": (ffn_conv_w, m_ffn_conv_w, v_ffn_conv_w),
    }

    def flat2d(a):
        n = a.size
        return a.reshape(n // 1024, 1024) if n % 1024 == 0 and n > 1024 else a.reshape(-1, a.shape[-1])

    names = list(small_params)
    items = []
    for nm in names:
        w_, m_, v_ = small_params[nm]
        items.append((flat2d(w_[0]), flat2d(g_small[nm].reshape(w_[0].shape)), flat2d(m_[0]), flat2d(v_[0])))
    upd = _adamw_small(items)
    small_out = {}
    for nm, (dl, mo, vo) in zip(names, upd):
        shp = small_params[nm][0].shape
        small_out[nm] = (g_small[nm].reshape(shp), dl.reshape(shp), mo.reshape(shp), vo.reshape(shp))

    loss = lax.psum(loss_blk[0, 0], ("x", "y", "c"))

    order = ["w_ada", "b_ada", "g_pre_mix", "g_post_mix", "w_in", "ssm_lam_re", "ssm_lam_im", "ssm_log_step",
             "ssm_b_re", "ssm_b_im", "ssm_c_re", "ssm_c_im", "ssm_d", "glu_w", "glu_b", "g_out_ssm", "conv_w",
             "g_out_conv", "w_out", "g_pre_ffn", "g_post_ffn", "w_up", "ffn_conv_w", "w_down"]
    results = {"w_ada": tuple(a[None] for a in ada)}
    for nm in big:
        results[nm] = tuple(a[None] for a in big[nm])
    results.update(small_out)
    outs = [loss, grad_x[None]]
    for k in range(4):
        outs += [results[nm][k] for nm in order]
    return tuple(outs)


def _ga_rowsum(ga_re8, ga_im8):
    n = ga_re8.shape[1]

    def body(r_ref, i_ref, o_ref):
        o_ref[...] = jnp.zeros(o_ref.shape, F32)
        o_ref[0:1, :] = _colsum(r_ref[...])
        o_ref[1:2, :] = _colsum(i_ref[...])

    return pl.pallas_call(body, name="ga_rowsum", out_shape=jax.ShapeDtypeStruct((SUBLANES, n), F32))(ga_re8, ga_im8)
```

```python
import functools
import math

import jax
import jax.numpy as jnp
import numpy as np
from jax import lax
from jax.experimental import pallas as pl
from jax.experimental.pallas import tpu as pltpu

F32 = jnp.float32
BF16 = jnp.bfloat16
MESH = pl.DeviceIdType.MESH

EPS = 1e-6
LAMBDA_RE_MAX = -1e-4
ADAM_LR = 0.001
ADAM_B1 = 0.9
ADAM_B2 = 0.999
ADAM_EPS = 1e-08
ADAM_WD = 0.01
ADAM_STEP = 10

SUBLANES = 8
N_CHIPS = 4
N_DEV = 8
CONV_HEAD_DIM = 64
VMEM_BIG = 56 * 1024 * 1024
VMEM_MID = 40 * 1024 * 1024

TB_MIX = 256
TB_FFN = 128
TB_SCAN = 1024
W_SCAN = 256
SSM_SPLIT = 4
CW_FFN = 256
SCAN_UNROLL = 4
TB_TN = 512


def _cparams(sem=None, vmem=None):
    kw = {}
    if sem is not None:
        kw["dimension_semantics"] = sem
    if vmem is not None:
        kw["vmem_limit_bytes"] = vmem
    return pltpu.CompilerParams(**kw)


def _blk(t, pref):
    return pref if t % pref == 0 else t


def _dot(a, b):
    return jnp.dot(a.astype(BF16), b.astype(BF16), preferred_element_type=F32)


def _dot_nt(a, b):
    return lax.dot_general(a.astype(BF16), b.astype(BF16), (((1,), (1,)), ((), ())),
                           preferred_element_type=F32)


def _dot_tn(a, b):
    return lax.dot_general(a.astype(BF16), b.astype(BF16), (((0,), (0,)), ((), ())),
                           preferred_element_type=F32)


def _sigmoid(x):
    return 0.5 * jnp.tanh(0.5 * x) + 0.5


_GELU_K = math.sqrt(2.0 / math.pi)
_GELU_C = 0.044715


def _gelu(x):
    th = jnp.tanh(_GELU_K * (x + _GELU_C * x * x * x))
    return 0.5 * x * (1.0 + th)


def _gelu_grad(x):
    x2 = x * x
    th = jnp.tanh(_GELU_K * (x + _GELU_C * x2 * x))
    return 0.5 * (1.0 + th) + 0.5 * x * (1.0 - th * th) * _GELU_K * (1.0 + 3.0 * _GELU_C * x2)


def _rowmean(x):
    return jnp.mean(x, axis=-1, keepdims=True)


def _colsum(x):
    return jnp.sum(x, axis=0, keepdims=True)


def _split_dot(x, m):
    hi = x.astype(BF16)
    lo = (x - hi.astype(F32)).astype(BF16)
    return (jnp.dot(hi, m, preferred_element_type=F32) + jnp.dot(lo, m, preferred_element_type=F32))


def _split3_dot(x, m):
    hi = x.astype(BF16)
    r1 = x - hi.astype(F32)
    mid = r1.astype(BF16)
    lo = (r1 - mid.astype(F32)).astype(BF16)
    return (jnp.dot(hi, m, preferred_element_type=F32) + jnp.dot(mid, m, preferred_element_type=F32)
            + jnp.dot(lo, m, preferred_element_type=F32))


def _shift_down(x, halo, k):
    r = pltpu.roll(x, k, 0)
    row = lax.broadcasted_iota(jnp.int32, x.shape, 0)
    for j in range(k):
        r = jnp.where(row == j, halo[SUBLANES - k + j:SUBLANES - k + j + 1, :], r)
    return r


def _shift_up(x, halo, k):
    n = x.shape[0]
    r = pltpu.roll(x, n - k, 0)
    row = lax.broadcasted_iota(jnp.int32, x.shape, 0)
    for j in range(k):
        r = jnp.where(row == n - k + j, halo[j:j + 1, :], r)
    return r


def _acc_rows(ref, first, rows):
    @pl.when(first)
    def _():
        ref[...] = jnp.zeros(ref.shape, ref.dtype)
    for j, r in enumerate(rows):
        ref[j:j + 1, :] += r


def _rows(tb, c, col=0):
    return pl.BlockSpec((tb, c), lambda i, col=col: (i, col))


def _full(shape):
    nd = len(shape)
    return pl.BlockSpec(shape, lambda i, nd=nd: (0,) * nd)


def _halo_prev(tb, c, col=0):
    per = tb // SUBLANES
    return pl.BlockSpec((SUBLANES, c), lambda i, col=col: (jnp.maximum(i * per - 1, 0), col))


def _halo_next(tb, c, t, col=0):
    per = tb // SUBLANES
    last = t // SUBLANES - 1
    return pl.BlockSpec((SUBLANES, c), lambda i, col=col: (jnp.minimum((i + 1) * per, last), col))


def _mesh_pos():
    return lax.axis_index("x"), lax.axis_index("y"), lax.axis_index("c")


def _allgather8(x_pad, n_sum, name):
    m_per, n = x_pad.shape

    def body(x_ref, out_ref, sum_ref, send_sems, recv_sems, local_sem):
        x, y, c = _mesh_pos()
        me, sibling = (x, y, c), (x, y, 1 - c)
        chips = [(1 - x, y), (x, 1 - y), (1 - x, 1 - y)]

        def rows(px, py, pc):
            return out_ref.at[pl.ds((4 * px + 2 * py + pc) * m_per, m_per), :]

        def copy(k, block, to, src=None):
            return pltpu.make_async_remote_copy(
                src_ref=rows(*block) if src is None else src, dst_ref=rows(*block),
                send_sem=send_sems.at[k], recv_sem=recv_sems.at[k], device_id=to, device_id_type=MESH)

        mine = pltpu.make_async_copy(x_ref, rows(*me), local_sem)
        mine.start()
        first = [copy(0, me, sibling, src=x_ref)]
        first += [copy(1 + j, me, (*chip, c), src=x_ref) for j, chip in enumerate(chips)]
        for cp in first:
            cp.start()
        passed = [copy(4 + j, (*chip, c), sibling) for j, chip in enumerate(chips)]
        for j, chip in enumerate(chips):
            copy(1 + j, (*chip, c), me).wait_recv()
            passed[j].start()
        copy(0, sibling, me).wait_recv()
        for j, chip in enumerate(chips):
            copy(4 + j, (*chip, 1 - c), me).wait_recv()
        for cp in first + passed:
            cp.wait_send()
        mine.wait()
        acc = out_ref[0:n_sum, :]
        for k in range(1, N_DEV):
            acc = acc + out_ref[k * m_per:k * m_per + n_sum, :]
        sum_ref[...] = acc

    return pl.pallas_call(
        body, name=name,
        out_shape=(jax.ShapeDtypeStruct((N_DEV * m_per, n), F32), jax.ShapeDtypeStruct((n_sum, n), F32)),
        in_specs=[pl.BlockSpec(memory_space=pltpu.VMEM)],
        out_specs=(pl.BlockSpec(memory_space=pltpu.VMEM), pl.BlockSpec(memory_space=pltpu.VMEM)),
        scratch_shapes=[pltpu.SemaphoreType.DMA((7,)), pltpu.SemaphoreType.DMA((7,)), pltpu.SemaphoreType.DMA],
        compiler_params=_cparams(vmem=VMEM_MID),
    )(x_pad)


_HBM = pl.BlockSpec(memory_space=pltpu.HBM)
_SEM = pl.BlockSpec(memory_space=pltpu.SEMAPHORE)
_EFFECT = pltpu.SideEffectType.DATAFLOW_SIDE_EFFECTING


def _chip_copy(gather, src_ref, land_ref, send, recv, j, arrival):
    x, y, c = _mesh_pos()
    peer = [(1 - x, y), (x, 1 - y), (1 - x, 1 - y)][j]
    peer_chip = 2 * peer[0] + peer[1]
    my_chip = 2 * x + y
    return pltpu.make_async_remote_copy(
        src_ref=src_ref if gather else src_ref.at[peer_chip],
        dst_ref=land_ref.at[peer_chip if arrival else my_chip],
        send_sem=send.at[j], recv_sem=recv.at[j], device_id=(*peer, c), device_id_type=MESH)


def _chips_start(name, gather, srcs, lands):
    n = len(srcs)

    def body(*refs):
        src_refs, land_refs = refs[:n], refs[n:2 * n]
        outs = refs[2 * n:]
        sends, recvs, token = outs[:n], outs[n:2 * n], outs[-1]
        for k in range(n):
            for j in range(3):
                _chip_copy(gather, src_refs[k], land_refs[k], sends[k], recvs[k], j, False).start()
        token[...] = jnp.zeros(token.shape, F32)

    sem = pltpu.SemaphoreType.DMA((3,))
    thru = tuple(pltpu.HBM(a.shape, a.dtype) for a in list(srcs) + list(lands))
    res = pl.pallas_call(
        body, name=name,
        out_shape=(sem,) * (2 * n) + thru + (jax.ShapeDtypeStruct((SUBLANES, 128), F32),),
        in_specs=[_HBM] * (2 * n),
        out_specs=(_SEM,) * (2 * n) + (_HBM,) * (2 * n) + (pl.BlockSpec(memory_space=pltpu.VMEM),),
        input_output_aliases={k: 2 * n + k for k in range(2 * n)},
        compiler_params=pltpu.CompilerParams(has_side_effects=_EFFECT),
    )(*[pltpu.with_memory_space_constraint(a, pltpu.HBM) for a in list(srcs) + list(lands)])
    return res[:n], res[n:2 * n], res[2 * n:3 * n], res[3 * n:4 * n], res[-1]


def _chips_wait(name, gather, sends, recvs, srcs, lands, after):
    n = len(srcs)

    def body(*refs):
        src_refs, land_refs = refs[:n], refs[n:2 * n]
        sends_, recvs_ = refs[2 * n:3 * n], refs[3 * n:4 * n]
        for k in range(n):
            for j in range(3):
                cp = _chip_copy(gather, src_refs[k], land_refs[k], sends_[k], recvs_[k], j, True)
                cp.wait_send()
                cp.wait_recv()

    thru = tuple(pltpu.HBM(a.shape, a.dtype) for a in list(srcs) + list(lands))
    res = pl.pallas_call(
        body, name=name, out_shape=thru,
        in_specs=[_HBM] * (2 * n) + [_SEM] * (2 * n) + [pl.BlockSpec(memory_space=pl.ANY)],
        out_specs=(_HBM,) * (2 * n),
        input_output_aliases={k: k for k in range(2 * n)},
        compiler_params=pltpu.CompilerParams(has_side_effects=_EFFECT),
    )(*srcs, *lands, *sends, *recvs, after)
    return res[n:]


def _landing(own, chip):
    zone = lax.empty((N_CHIPS,) + own.shape, own.dtype)
    return lax.dynamic_update_slice(zone, own[None], (chip,) + (0,) * own.ndim)


def _swap_sibling(arrs, name):
    n_arr = len(arrs)

    def body(*refs):
        ins, outs = refs[:n_arr], refs[n_arr:2 * n_arr]
        send_sems, recv_sems = refs[2 * n_arr:]
        x, y, c = _mesh_pos()
        copies = [pltpu.make_async_remote_copy(
            src_ref=ins[n], dst_ref=outs[n], send_sem=send_sems.at[n], recv_sem=recv_sems.at[n],
            device_id=(x, y, 1 - c), device_id_type=MESH) for n in range(n_arr)]
        for cp in copies:
            cp.start()
        for cp in copies:
            cp.wait()

    any_spec = pl.BlockSpec(memory_space=pl.ANY)
    return pl.pallas_call(
        body, name=name,
        out_shape=tuple(jax.ShapeDtypeStruct(a.shape, a.dtype) for a in arrs),
        in_specs=[any_spec] * n_arr, out_specs=tuple([any_spec] * n_arr),
        scratch_shapes=[pltpu.SemaphoreType.DMA((n_arr,)), pltpu.SemaphoreType.DMA((n_arr,))],
    )(*arrs)


def _mod_shard(c_all, w_ada_sh, b_sh):
    d, n = w_ada_sh.shape
    bn = 512

    def body(c_ref, w_ref, b_ref, o_ref):
        cc = c_ref[...]
        ca = cc * _sigmoid(cc)
        o_ref[...] = _dot(ca, w_ref[...]) + b_ref[...]

    return pl.pallas_call(
        body, name="mod_shard", grid=(n // bn,),
        out_shape=jax.ShapeDtypeStruct((N_DEV, n), F32),
        in_specs=[_full((N_DEV, d)), pl.BlockSpec((d, bn), lambda j: (0, j)), pl.BlockSpec((1, bn), lambda j: (0, j))],
        out_specs=pl.BlockSpec((N_DEV, bn), lambda j: (0, j)),
        compiler_params=_cparams(("parallel",)),
    )(c_all, w_ada_sh, b_sh)


def _ssm_prep(lam_re, lam_im, log_step):
    g, p = lam_re.shape

    def body(lr_ref, li_ref, ls_ref, ar_ref, ai_ref, cr_ref, ci_ref):
        lr = jnp.minimum(lr_ref[...], LAMBDA_RE_MAX)
        li = li_ref[...]
        st = jnp.exp(ls_ref[...])
        mag = jnp.exp(lr * st)
        ar = mag * jnp.cos(li * st)
        ai = mag * jnp.sin(li * st)
        den = lr * lr + li * li
        nr = ar - 1.0
        ar_ref[...] = ar
        ai_ref[...] = ai
        cr_ref[...] = (nr * lr + ai * li) / den
        ci_ref[...] = (ai * lr - nr * li) / den

    sds = jax.ShapeDtypeStruct((g, p), F32)
    return pl.pallas_call(body, name="ssm_prep", out_shape=(sds,) * 4)(lam_re, lam_im, log_step)


def _ssm_blocks(bt_re, bt_im, ct_re, ct_im, coef_rows):
    gh, gp = bt_re.shape
    nb = 4
    cb, rb = gp // nb, gp // nb

    def body(btr, bti, ctr, cti, cf, bre_o, bim_o, cre_o, cim_o):
        j = pl.program_id(0)
        row = lax.broadcasted_iota(jnp.int32, (gh, cb), 0)
        col = lax.broadcasted_iota(jnp.int32, (gh, cb), 1) + j * cb
        mask = (row >> 4) == (col >> 6)
        cr, ci = cf[0:1, :], cf[1:2, :]
        br, bi = btr[...], bti[...]
        bre_o[...] = jnp.where(mask, br * cr - bi * ci, 0.0).astype(BF16)
        bim_o[...] = jnp.where(mask, br * ci + bi * cr, 0.0).astype(BF16)
        row2 = lax.broadcasted_iota(jnp.int32, (rb, gh), 0) + j * rb
        col2 = lax.broadcasted_iota(jnp.int32, (rb, gh), 1)
        mask2 = (row2 >> 6) == (col2 >> 4)
        cre_o[...] = jnp.where(mask2, ctr[...], 0.0).astype(BF16)
        cim_o[...] = jnp.where(mask2, cti[...], 0.0).astype(BF16)

    bspec = pl.BlockSpec((gh, cb), lambda j: (0, j))
    cspec = pl.BlockSpec((rb, gh), lambda j: (j, 0))
    return pl.pallas_call(
        body, name="ssm_blocks", grid=(nb,),
        out_shape=(jax.ShapeDtypeStruct((gh, gp), BF16),) * 2 + (jax.ShapeDtypeStruct((gp, gh), BF16),) * 2,
        in_specs=[bspec, bspec, cspec, cspec, pl.BlockSpec((SUBLANES, cb), lambda j: (0, j))],
        out_specs=(bspec, bspec, cspec, cspec),
        compiler_params=_cparams(("parallel",)),
    )(bt_re, bt_im, ct_re, ct_im, coef_rows)


def _scan_consts(a_ref, reverse):
    w = a_ref.shape[1]
    ar1 = a_ref[0:1, :]
    ai1 = a_ref[1:2, :]
    if reverse:
        ai1 = -ai1
    pr, pi = [ar1], [ai1]
    for _ in range(1, SUBLANES):
        nr = pr[-1] * ar1 - pi[-1] * ai1
        ni = pr[-1] * ai1 + pi[-1] * ar1
        pr.append(nr)
        pi.append(ni)
    row = lax.broadcasted_iota(jnp.int32, (SUBLANES, w), 0)
    dist = (SUBLANES - 1 - row) if reverse else row

    def pick(vals):
        out = jnp.broadcast_to(vals[SUBLANES - 1], (SUBLANES, w))
        for r in range(SUBLANES - 1):
            out = jnp.where(dist == r, vals[r], out)
        return out

    p_r, p_i = pick(pr), pick(pi)
    steps = []
    for k in (1, 2, 4):
        steps.append((k, jnp.where(dist >= k, pr[k - 1], 0.0), jnp.where(dist >= k, pi[k - 1], 0.0)))
    a8 = (jnp.broadcast_to(pr[SUBLANES - 1], (SUBLANES, w)), jnp.broadcast_to(pi[SUBLANES - 1], (SUBLANES, w)))
    return row, p_r, p_i, steps, a8


def _scan_tile(xr, xi, cr, ci, consts, reverse):
    row, p_r, p_i, steps, (a8r, a8i) = consts
    for k, s_r, s_i in steps:
        sh = (SUBLANES - k) if reverse else k
        qr = pltpu.roll(xr, sh, 0)
        qi = pltpu.roll(xi, sh, 0)
        xr, xi = xr + s_r * qr - s_i * qi, xi + s_r * qi + s_i * qr
    outr = xr + p_r * cr - p_i * ci
    outi = xi + p_r * ci + p_i * cr
    e = 0 if reverse else SUBLANES - 1
    er = jnp.broadcast_to(xr[e:e + 1, :], xr.shape)
    ei = jnp.broadcast_to(xi[e:e + 1, :], xi.shape)
    return outr, outi, er + a8r * cr - a8i * ci, ei + a8r * ci + a8i * cr


def _scan_fwd(a_rows, bu_re, bu_im):
    t, n = bu_re.shape
    tb, w = _blk(t, TB_SCAN), W_SCAN
    ntile = tb // SUBLANES

    def body(a_ref, br_ref, bi_ref, sr_ref, si_ref, car, cai):
        @pl.when(pl.program_id(1) == 0)
        def _():
            car[...] = jnp.zeros(car.shape, F32)
            cai[...] = jnp.zeros(cai.shape, F32)
        consts = _scan_consts(a_ref, False)

        def tile(i, carry):
            o = pl.multiple_of(i * SUBLANES, SUBLANES)
            outr, outi, ncr, nci = _scan_tile(br_ref[pl.ds(o, SUBLANES), :], bi_ref[pl.ds(o, SUBLANES), :],
                                              carry[0], carry[1], consts, False)
            sr_ref[pl.ds(o, SUBLANES), :] = outr
            si_ref[pl.ds(o, SUBLANES), :] = outi
            return ncr, nci

        def tiles(i, carry):
            for s in range(SCAN_UNROLL):
                carry = tile(i * SCAN_UNROLL + s, carry)
            return carry

        cr, ci = lax.fori_loop(0, ntile // SCAN_UNROLL, tiles, (car[...], cai[...]))
        car[...] = cr
        cai[...] = ci

    spec = pl.BlockSpec((tb, w), lambda s, k: (k, s))
    sds = jax.ShapeDtypeStruct((t, n), F32)
    return pl.pallas_call(
        body, name="scan_fwd", grid=(n // w, t // tb), out_shape=(sds, sds),
        in_specs=[pl.BlockSpec((SUBLANES, w), lambda s, k: (0, s)), spec, spec], out_specs=(spec, spec),
        scratch_shapes=[pltpu.VMEM((SUBLANES, w), F32), pltpu.VMEM((SUBLANES, w), F32)],
        compiler_params=_cparams(("parallel", "arbitrary"), VMEM_MID),
    )(a_rows, bu_re, bu_im)


def _scan_bwd(a_rows, g_re, g_im, s_re, s_im):
    t, n = g_re.shape
    tb, w = _blk(t, TB_SCAN), W_SCAN
    ntile = tb // SUBLANES
    nt = t // tb

    def body(a_ref, gr_ref, gi_ref, sr_ref, si_ref, or_ref, oi_ref, gar_ref, gai_ref, car, cai):
        @pl.when(pl.program_id(1) == 0)
        def _():
            car[...] = jnp.zeros(car.shape, F32)
            cai[...] = jnp.zeros(cai.shape, F32)
            gar_ref[...] = jnp.zeros(gar_ref.shape, F32)
            gai_ref[...] = jnp.zeros(gai_ref.shape, F32)
        consts = _scan_consts(a_ref, True)
        row = consts[0]

        def tile(i, carry):
            cr, ci, accr, acci = carry
            o = pl.multiple_of((ntile - 1 - i) * SUBLANES, SUBLANES)
            outr, outi, ncr, nci = _scan_tile(gr_ref[pl.ds(o, SUBLANES), :], gi_ref[pl.ds(o, SUBLANES), :],
                                              cr, ci, consts, True)
            or_ref[pl.ds(o, SUBLANES), :] = outr
            oi_ref[pl.ds(o, SUBLANES), :] = outi
            gnr = jnp.where(row == SUBLANES - 1, cr, pltpu.roll(outr, SUBLANES - 1, 0))
            gni = jnp.where(row == SUBLANES - 1, ci, pltpu.roll(outi, SUBLANES - 1, 0))
            sr = sr_ref[pl.ds(o, SUBLANES), :]
            si = si_ref[pl.ds(o, SUBLANES), :]
            return ncr, nci, accr + sr * gnr + si * gni, acci + sr * gni - si * gnr

        def tiles(i, carry):
            for s in range(SCAN_UNROLL):
                carry = tile(i * SCAN_UNROLL + s, carry)
            return carry

        cr, ci, accr, acci = lax.fori_loop(0, ntile // SCAN_UNROLL, tiles,
                                           (car[...], cai[...], gar_ref[...], gai_ref[...]))
        car[...] = cr
        cai[...] = ci
        gar_ref[...] = accr
        gai_ref[...] = acci

    spec = pl.BlockSpec((tb, w), lambda s, k: (nt - 1 - k, s))
    aspec = pl.BlockSpec((SUBLANES, w), lambda s, k: (0, s))
    sds = jax.ShapeDtypeStruct((t, n), F32)
    asds = jax.ShapeDtypeStruct((SUBLANES, n), F32)
    return pl.pallas_call(
        body, name="scan_bwd", grid=(n // w, nt), out_shape=(sds, sds, asds, asds),
        in_specs=[aspec, spec, spec, spec, spec], out_specs=(spec, spec, aspec, aspec),
        scratch_shapes=[pltpu.VMEM((SUBLANES, w), F32), pltpu.VMEM((SUBLANES, w), F32)],
        compiler_params=_cparams(("parallel", "arbitrary"), VMEM_MID),
    )(a_rows, g_re, g_im, s_re, s_im)


def _mix_in(x, vec, w_in_st, b_re, b_im):
    t, d = x.shape
    ns, _, nc = w_in_st.shape
    dssm, nstate = b_re.shape
    du, ds = dssm // SSM_SPLIT, nstate // SSM_SPLIT
    tb = _blk(t, TB_MIX)

    def body(x_ref, vec_ref, w_ref, bre_ref, bim_ref, proj_ref, bur_ref, bui_ref, h1_ref):
        xv = x_ref[...]
        r = lax.rsqrt(_rowmean(xv * xv) + EPS)
        h = xv * r * vec_ref[0:1, :] * vec_ref[1:2, :] + vec_ref[2:3, :]
        hb = h.astype(BF16)
        h1_ref[...] = hb
        u = None
        for j in range(ns):
            pj = jnp.dot(hb, w_ref[j], preferred_element_type=F32)
            proj_ref[:, j * nc:(j + 1) * nc] = pj
            if j == 0:
                u = pj
        ub = u.astype(BF16)
        for q in range(SSM_SPLIT):
            rq, cq = slice(q * du, (q + 1) * du), slice(q * ds, (q + 1) * ds)
            bur_ref[:, cq] = jnp.dot(ub[:, rq], bre_ref[rq, cq], preferred_element_type=F32)
            bui_ref[:, cq] = jnp.dot(ub[:, rq], bim_ref[rq, cq], preferred_element_type=F32)

    return pl.pallas_call(
        body, name="mix_in", grid=(t // tb,),
        out_shape=(jax.ShapeDtypeStruct((t, ns * nc), F32), jax.ShapeDtypeStruct((t, nstate), F32),
                   jax.ShapeDtypeStruct((t, nstate), F32), jax.ShapeDtypeStruct((t, d), BF16)),
        in_specs=[_rows(tb, d), _full((SUBLANES, d)), _full(w_in_st.shape), _full(b_re.shape), _full(b_im.shape)],
        out_specs=(_rows(tb, ns * nc), _rows(tb, nstate), _rows(tb, nstate), _rows(tb, d)),
        compiler_params=_cparams(("parallel",), VMEM_BIG),
    )(x, vec, w_in_st, b_re, b_im)


def _head_ms(y, h_ref):
    return _split_dot(y * y, h_ref[...])


def _conv3(x, halo, w_ref):
    return w_ref[0:1, :] * _shift_down(x, halo, 2) + w_ref[1:2, :] * _shift_down(x, halo, 1) + w_ref[2:3, :] * x


def _mix_out(x, proj, s_re, s_im, c_re, c_im, v512, convw, glu_w, h16, h64, w_out, vd):
    t, d = x.shape
    dh = c_re.shape[1]
    nstate = s_re.shape[1]
    du, ds = dh // SSM_SPLIT, nstate // SSM_SPLIT
    tb = _blk(t, TB_MIX)

    def body(x_ref, u_ref, bg_ref, cg_ref, v_ref, cgh_ref, vh_ref, sr_ref, si_ref, cre_ref, cim_ref, p_ref,
             cw_ref, gw_ref, h16_ref, h64_ref, wo_ref, vd_ref, y1_ref, o_ref, x2_ref):
        i = pl.program_id(0)
        u = u_ref[...]
        ys = []
        for q in range(SSM_SPLIT):
            rq, cq = slice(q * ds, (q + 1) * ds), slice(q * du, (q + 1) * du)
            ys.append(_dot(sr_ref[:, rq], cre_ref[rq, cq]) - _dot(si_ref[:, rq], cim_ref[rq, cq]))
        ys = jnp.concatenate(ys, axis=1)
        y1 = ys + p_ref[0:1, :] * u
        y1_ref[...] = y1
        z = _gelu(y1)
        q = _dot(z, gw_ref[...]) + p_ref[1:2, :]
        ya = z * _sigmoid(q)
        na = ya * lax.rsqrt(_head_ms(ya, h16_ref) + EPS) * p_ref[2:3, :]
        cv = cg_ref[...] * v_ref[...]
        cvh = jnp.where(i > 0, cgh_ref[...] * vh_ref[...], 0.0)
        yb = bg_ref[...] * _conv3(cv, cvh, cw_ref)
        nb = yb * lax.rsqrt(_head_ms(yb, h64_ref) + EPS) * p_ref[3:4, :]
        o = _dot(na, wo_ref[0:dh, :]) + _dot(nb, wo_ref[dh:2 * dh, :])
        o_ref[...] = o
        on = o * lax.rsqrt(_rowmean(o * o) + EPS) * vd_ref[0:1, :]
        x2_ref[...] = x_ref[...] + vd_ref[1:2, :] * on

    return pl.pallas_call(
        body, name="mix_out", grid=(t // tb,),
        out_shape=(jax.ShapeDtypeStruct((t, dh), F32), jax.ShapeDtypeStruct((t, d), F32),
                   jax.ShapeDtypeStruct((t, d), F32)),
        in_specs=[_rows(tb, d), _rows(tb, dh, 0), _rows(tb, dh, 1), _rows(tb, dh, 2), _rows(tb, dh, 3),
                  _halo_prev(tb, dh, 2), _halo_prev(tb, dh, 3), _rows(tb, nstate), _rows(tb, nstate),
                  _full(c_re.shape), _full(c_im.shape), _full(v512.shape), _full(convw.shape), _full(glu_w.shape),
                  _full(h16.shape), _full(h64.shape), _full(w_out.shape), _full(vd.shape)],
        out_specs=(_rows(tb, dh), _rows(tb, d), _rows(tb, d)),
        compiler_params=_cparams(("parallel",), VMEM_BIG),
    )(x, proj, proj, proj, proj, proj, proj, s_re, s_im, c_re, c_im, v512, convw, glu_w, h16, h64, w_out, vd)


def _ffn_up(x2, vec, w_up_st):
    t, d = x2.shape
    ns, _, nc = w_up_st.shape
    tb = _blk(t, TB_FFN)

    def body(x_ref, vec_ref, w_ref, up_ref, h2_ref):
        xv = x_ref[...]
        r = lax.rsqrt(_rowmean(xv * xv) + EPS)
        h = xv * r * vec_ref[0:1, :] * vec_ref[1:2, :] + vec_ref[2:3, :]
        hb = h.astype(BF16)
        h2_ref[...] = hb
        for j in range(ns):
            up_ref[:, j * nc:(j + 1) * nc] = jnp.dot(hb, w_ref[j], preferred_element_type=F32)

    return pl.pallas_call(
        body, name="ffn_up", grid=(t // tb,),
        out_shape=(jax.ShapeDtypeStruct((t, ns * nc), F32), jax.ShapeDtypeStruct((t, d), BF16)),
        in_specs=[_rows(tb, d), _full((SUBLANES, d)), _full(w_up_st.shape)],
        out_specs=(_rows(tb, ns * nc), _rows(tb, d)),
        compiler_params=_cparams(("parallel",), VMEM_BIG),
    )(x2, vec, w_up_st)


def _ffn_down(up, fw, w_down, x2, tgt, vd):
    t, nh = up.shape
    dff, d = w_down.shape
    tb = _blk(t, TB_FFN)
    inv_d = 1.0 / d

    def body(up_ref, uph_ref, fw_ref, wd_ref, x2_ref, tgt_ref, vd_ref,
             act_ref, ddn_ref, dout_ref, dhid_ref, vec_ref, loss_ref, a_s, vv_s, sg_s, dact_s):
        i = pl.program_id(0)

        def conv_cols(sl):
            x = up_ref[:, sl]
            halo = jnp.where(i > 0, uph_ref[:, sl], 0.0)
            return (fw_ref[0:1, sl] * _shift_down(x, halo, 2) + fw_ref[1:2, sl] * _shift_down(x, halo, 1)
                    + fw_ref[2:3, sl] * x)

        for o in range(0, dff, CW_FFN):
            sl = slice(o, o + CW_FFN)
            a = conv_cols(sl)
            vv = conv_cols(slice(dff + o, dff + o + CW_FFN))
            sg = _sigmoid(a)
            a_s[:, sl] = a
            vv_s[:, sl] = vv
            sg_s[:, sl] = sg
            act_ref[:, sl] = (a * sg * vv).astype(BF16)
        dn = jnp.dot(act_ref[...], wd_ref[...], preferred_element_type=F32)
        r3 = lax.rsqrt(_rowmean(dn * dn) + EPS)
        xn = dn * r3
        g = vd_ref[0:1, :]
        gt2 = vd_ref[1:2, :]
        dnn = xn * g
        diff = x2_ref[...] + gt2 * dnn - tgt_ref[...]
        part = 0.5 * inv_d * jnp.sum(diff * diff)

        @pl.when(i == 0)
        def _():
            loss_ref[...] = jnp.zeros(loss_ref.shape, F32)
        loss_ref[...] += part
        dout = diff * inv_d
        dout_ref[...] = dout
        ddnn = dout * gt2
        _acc_rows(vec_ref, i == 0, [_colsum(dout * dnn), _colsum(ddnn * xn)])
        dxn = ddnn * g
        ddn = r3 * (dxn - xn * _rowmean(dxn * xn))
        ddnb = ddn.astype(BF16)
        ddn_ref[...] = ddnb
        dact_s[...] = lax.dot_general(ddnb, wd_ref[...], (((1,), (1,)), ((), ())), preferred_element_type=F32)
        for o in range(0, dff, CW_FFN):
            sl = slice(o, o + CW_FFN)
            a, vv, sg, dact = a_s[:, sl], vv_s[:, sl], sg_s[:, sl], dact_s[:, sl]
            dhid_ref[:, sl] = dact * vv * sg * (1.0 + a * (1.0 - sg))
            dhid_ref[:, dff + o:dff + o + CW_FFN] = dact * (a * sg)

    return pl.pallas_call(
        body, name="ffn_down", grid=(t // tb,),
        scratch_shapes=[pltpu.VMEM((tb, dff), F32)] * 4,
        out_shape=(jax.ShapeDtypeStruct((t, dff), BF16), jax.ShapeDtypeStruct((t, d), BF16),
                   jax.ShapeDtypeStruct((t, d), F32), jax.ShapeDtypeStruct((t, nh), F32),
                   jax.ShapeDtypeStruct((SUBLANES, d), F32), jax.ShapeDtypeStruct((SUBLANES, 128), F32)),
        in_specs=[_rows(tb, nh), _halo_prev(tb, nh), _full(fw.shape), _full(w_down.shape), _rows(tb, d),
                  _rows(tb, d), _full(vd.shape)],
        out_specs=(_rows(tb, dff), _rows(tb, d), _rows(tb, d), _rows(tb, nh), _full((SUBLANES, d)),
                   _full((SUBLANES, 128))),
        compiler_params=_cparams(("arbitrary",), VMEM_BIG),
    )(up, up, fw, w_down, x2, tgt, vd)


def _ffn_up_bwd(dhid, up, fw, x2, dout, vec, w_up_st):
    t, nh = dhid.shape
    d = x2.shape[1]
    ns, _, nc = w_up_st.shape
    tb = _blk(t, TB_FFN)
    nblk = t // tb

    def body(dh_ref, dhn_ref, up_ref, fw_ref, x2_ref, dout_ref, vec_ref, w_ref,
             dx2_ref, dup_ref, vp_ref, df_ref):
        i = pl.program_id(0)

        @pl.when(i == 0)
        def _():
            df_ref[...] = jnp.zeros(df_ref.shape, F32)
        for o in range(0, nh, CW_FFN):
            sl = slice(o, o + CW_FFN)
            dh = dh_ref[:, sl]
            dhn = jnp.where(i < nblk - 1, dhn_ref[:, sl], 0.0)
            dh1 = _shift_up(dh, dhn, 1)
            dh2 = _shift_up(dh, dhn, 2)
            dup_ref[:, sl] = (fw_ref[2:3, sl] * dh + fw_ref[1:2, sl] * dh1 + fw_ref[0:1, sl] * dh2).astype(BF16)
            up_v = up_ref[:, sl]
            df_ref[0:1, sl] += _colsum(dh2 * up_v)
            df_ref[1:2, sl] += _colsum(dh1 * up_v)
            df_ref[2:3, sl] += _colsum(dh * up_v)
        dh2 = None
        for j in range(ns):
            pj = lax.dot_general(dup_ref[:, j * nc:(j + 1) * nc], w_ref[j], (((1,), (1,)), ((), ())),
                                 preferred_element_type=F32)
            dh2 = pj if dh2 is None else dh2 + pj
        xv = x2_ref[...]
        r = lax.rsqrt(_rowmean(xv * xv) + EPS)
        xn = xv * r
        g = vec_ref[0:1, :]
        hg = xn * g
        dhg = dh2 * vec_ref[1:2, :]
        _acc_rows(vp_ref, i == 0, [_colsum(dh2), _colsum(dh2 * hg), _colsum(dhg * xn)])
        dxn = dhg * g
        dx2_ref[...] = dout_ref[...] + r * (dxn - xn * _rowmean(dxn * xn))

    return pl.pallas_call(
        body, name="ffn_up_bwd", grid=(nblk,),
        out_shape=(jax.ShapeDtypeStruct((t, d), F32), jax.ShapeDtypeStruct((t, nh), BF16),
                   jax.ShapeDtypeStruct((SUBLANES, d), F32), jax.ShapeDtypeStruct((SUBLANES, nh), F32)),
        in_specs=[_rows(tb, nh), _halo_next(tb, nh, t), _rows(tb, nh), _full(fw.shape),
                  _rows(tb, d), _rows(tb, d), _full(vec.shape), _full(w_up_st.shape)],
        out_specs=(_rows(tb, d), _rows(tb, nh), _full((SUBLANES, d)), _full((SUBLANES, nh))),
        compiler_params=_cparams(("arbitrary",), VMEM_BIG),
    )(dhid, dhid, up, fw, x2, dout, vec, w_up_st)


def _mix_out_bwd(dx2, o, y1, proj, c_re, c_im, v512, convw, glu_w, h16, h64, w_out, vd):
    t, d = dx2.shape
    dh = y1.shape[1]
    nstate = c_re.shape[0]
    du, ds = dh // SSM_SPLIT, nstate // SSM_SPLIT
    tb = _blk(t, TB_MIX)

    def body(dx2_ref, o_ref, y1_ref, u_ref, bg_ref, cg_ref, v_ref, cgh_ref, vh_ref, cre_ref, cim_ref, p_ref,
             cw_ref, gw_ref, h16_ref, h64_ref, wo_ref, vd_ref,
             do_ref, ycat_ref, z_ref, dq_ref, dy1_ref, gr_ref, gi_ref, dcc_ref, dbg_ref, vpd_ref, vp5_ref):
        i = pl.program_id(0)
        first = i == 0
        ov = o_ref[...]
        ro = lax.rsqrt(_rowmean(ov * ov) + EPS)
        on_ = ov * ro
        g = vd_ref[0:1, :]
        dx2v = dx2_ref[...]
        don = dx2v * vd_ref[1:2, :]
        _acc_rows(vpd_ref, first, [_colsum(dx2v * on_ * g), _colsum(don * on_)])
        dxn = don * g
        dob = (ro * (dxn - on_ * _rowmean(dxn * on_))).astype(BF16)
        do_ref[...] = dob
        dyc_a = lax.dot_general(dob, wo_ref[0:dh, :], (((1,), (1,)), ((), ())), preferred_element_type=F32)
        dyc_b = lax.dot_general(dob, wo_ref[dh:2 * dh, :], (((1,), (1,)), ((), ())), preferred_element_type=F32)
        y1v = y1_ref[...]
        u = u_ref[...]
        z = _gelu(y1v)
        zb = z.astype(BF16)
        z_ref[...] = zb
        sg = _sigmoid(jnp.dot(zb, gw_ref[...], preferred_element_type=F32) + p_ref[1:2, :])
        ya = z * sg
        ra = lax.rsqrt(_head_ms(ya, h16_ref) + EPS)
        yan = ya * ra
        ga = p_ref[2:3, :]
        ycat_ref[:, 0:dh] = (yan * ga).astype(BF16)
        dyn = dyc_a * ga
        dya = ra * (dyn - yan * _split_dot(dyn * yan, h16_ref[...]))
        dq = dya * z * sg * (1.0 - sg)
        dqb = dq.astype(BF16)
        dq_ref[...] = dqb
        dz = dya * sg + lax.dot_general(dqb, gw_ref[...], (((1,), (1,)), ((), ())), preferred_element_type=F32)
        dy1 = dz * _gelu_grad(y1v)
        dy1_ref[...] = dy1
        dy1b = dy1.astype(BF16)
        for q in range(SSM_SPLIT):
            rq, cq = slice(q * ds, (q + 1) * ds), slice(q * du, (q + 1) * du)
            gr_ref[:, rq] = lax.dot_general(dy1b[:, cq], cre_ref[rq, cq], (((1,), (1,)), ((), ())),
                                            preferred_element_type=F32)
            gi_ref[:, rq] = -lax.dot_general(dy1b[:, cq], cim_ref[rq, cq], (((1,), (1,)), ((), ())),
                                             preferred_element_type=F32)
        bg = bg_ref[...]
        cv = cg_ref[...] * v_ref[...]
        cvh = jnp.where(i > 0, cgh_ref[...] * vh_ref[...], 0.0)
        cv1 = _shift_down(cv, cvh, 1)
        cv2 = _shift_down(cv, cvh, 2)
        cc = cw_ref[0:1, :] * cv2 + cw_ref[1:2, :] * cv1 + cw_ref[2:3, :] * cv
        yb = bg * cc
        rb = lax.rsqrt(_head_ms(yb, h64_ref) + EPS)
        ybn = yb * rb
        gb = p_ref[3:4, :]
        ycat_ref[:, dh:2 * dh] = (ybn * gb).astype(BF16)
        dynb = dyc_b * gb
        dyb = rb * (dynb - ybn * _split_dot(dynb * ybn, h64_ref[...]))
        dcc = dyb * bg
        dbg_ref[...] = dyb * cc
        dcc_ref[...] = dcc
        _acc_rows(vp5_ref, first, [_colsum(dyc_a * yan), _colsum(dyc_b * ybn), _colsum(dq), _colsum(dy1 * u),
                                   _colsum(dcc * cv2), _colsum(dcc * cv1), _colsum(dcc * cv)])

    return pl.pallas_call(
        body, name="mix_out_bwd", grid=(t // tb,),
        out_shape=(jax.ShapeDtypeStruct((t, d), BF16), jax.ShapeDtypeStruct((t, 2 * dh), BF16),
                   jax.ShapeDtypeStruct((t, dh), BF16), jax.ShapeDtypeStruct((t, dh), BF16),
                   jax.ShapeDtypeStruct((t, dh), F32), jax.ShapeDtypeStruct((t, nstate), F32),
                   jax.ShapeDtypeStruct((t, nstate), F32), jax.ShapeDtypeStruct((t, dh), F32),
                   jax.ShapeDtypeStruct((t, dh), F32), jax.ShapeDtypeStruct((SUBLANES, d), F32),
                   jax.ShapeDtypeStruct((SUBLANES, dh), F32)),
        in_specs=[_rows(tb, d), _rows(tb, d), _rows(tb, dh), _rows(tb, dh, 0), _rows(tb, dh, 1), _rows(tb, dh, 2),
                  _rows(tb, dh, 3), _halo_prev(tb, dh, 2), _halo_prev(tb, dh, 3), _full(c_re.shape), _full(c_im.shape),
                  _full(v512.shape), _full(convw.shape), _full(glu_w.shape), _full(h16.shape), _full(h64.shape),
                  _full(w_out.shape), _full(vd.shape)],
        out_specs=(_rows(tb, d), _rows(tb, 2 * dh), _rows(tb, dh), _rows(tb, dh), _rows(tb, dh), _rows(tb, nstate),
                   _rows(tb, nstate), _rows(tb, dh), _rows(tb, dh), _full((SUBLANES, d)), _full((SUBLANES, dh))),
        compiler_params=_cparams(("arbitrary",), VMEM_BIG),
    )(dx2, o, y1, proj, proj, proj, proj, proj, proj, c_re, c_im, v512, convw, glu_w, h16, h64, w_out, vd)


def _mix_in_bwd(gt_re, gt_im, b_re, b_im, dy1, dcc, dbg, proj, x, dx2, vec, v512, convw, w_in_st):
    t, d = x.shape
    dh = dy1.shape[1]
    nstate = gt_re.shape[1]
    du_w, ds = dh // SSM_SPLIT, nstate // SSM_SPLIT
    ns, _, nc = w_in_st.shape
    tb = _blk(t, TB_MIX)
    nblk = t // tb

    def body(gr_ref, gi_ref, bre_ref, bim_ref, dy1_ref, dcc_ref, dccn_ref, dbg_ref, cg_ref, v_ref, x_ref, dx2_ref,
             vec_ref, p_ref, cw_ref, w_ref, gx_ref, dproj_ref, vp_ref):
        i = pl.program_id(0)
        du = []
        for q in range(SSM_SPLIT):
            rq, cq = slice(q * du_w, (q + 1) * du_w), slice(q * ds, (q + 1) * ds)
            du.append(lax.dot_general(gr_ref[:, cq].astype(BF16), bre_ref[rq, cq], (((1,), (1,)), ((), ())),
                                      preferred_element_type=F32)
                      + lax.dot_general(gi_ref[:, cq].astype(BF16), bim_ref[rq, cq], (((1,), (1,)), ((), ())),
                                        preferred_element_type=F32))
        du = dy1_ref[...] * p_ref[0:1, :] + jnp.concatenate(du, axis=1)
        dcc = dcc_ref[...]
        dccn = jnp.where(i < nblk - 1, dccn_ref[...], 0.0)
        dcv = (cw_ref[2:3, :] * dcc + cw_ref[1:2, :] * _shift_up(dcc, dccn, 1)
               + cw_ref[0:1, :] * _shift_up(dcc, dccn, 2))
        parts = [du, dbg_ref[...], dcv * v_ref[...], dcv * cg_ref[...]]
        dh1 = None
        for j in range(ns):
            pb = parts[j].astype(BF16)
            dproj_ref[:, j * nc:(j + 1) * nc] = pb
            pj = lax.dot_general(pb, w_ref[j], (((1,), (1,)), ((), ())), preferred_element_type=F32)
            dh1 = pj if dh1 is None else dh1 + pj
        xv = x_ref[...]
        r = lax.rsqrt(_rowmean(xv * xv) + EPS)
        xn = xv * r
        g = vec_ref[0:1, :]
        hg = xn * g
        dhg = dh1 * vec_ref[1:2, :]
        _acc_rows(vp_ref, i == 0, [_colsum(dh1), _colsum(dh1 * hg), _colsum(dhg * xn)])
        dxn = dhg * g
        gx_ref[...] = dx2_ref[...] + r * (dxn - xn * _rowmean(dxn * xn))

    assert nc == dh and ns == 4
    return pl.pallas_call(
        body, name="mix_in_bwd", grid=(nblk,),
        out_shape=(jax.ShapeDtypeStruct((t, d), F32), jax.ShapeDtypeStruct((t, ns * nc), BF16),
                   jax.ShapeDtypeStruct((SUBLANES, d), F32)),
        in_specs=[_rows(tb, nstate), _rows(tb, nstate), _full(b_re.shape), _full(b_im.shape), _rows(tb, dh),
                  _rows(tb, dh), _halo_next(tb, dh, t), _rows(tb, dh), _rows(tb, dh, 2), _rows(tb, dh, 3),
                  _rows(tb, d), _rows(tb, d), _full(vec.shape), _full(v512.shape), _full(convw.shape),
                  _full(w_in_st.shape)],
        out_specs=(_rows(tb, d), _rows(tb, ns * nc), _full((SUBLANES, d))),
        compiler_params=_cparams(("arbitrary",), VMEM_BIG),
    )(gt_re, gt_im, b_re, b_im, dy1, dcc, dcc, dbg, proj, proj, x, dx2, vec, v512, convw, w_in_st)


def _matmul_tn(a, b, m, bn, out_dtype, name, diag=False, bt=TB_TN, after=None):
    t = a.shape[0]
    n = b.shape[1]
    bt = _blk(t, bt)
    nk = t // bt
    extra = [] if after is None else [after]
    a_map = (lambda j, k: (k, j)) if diag else (lambda j, k: (k, 0))

    def body(a_ref, b_ref, *rest):
        o_ref, acc_ref = rest[-2:]
        k = pl.program_id(1)

        @pl.when(k == 0)
        def _():
            acc_ref[...] = jnp.zeros(acc_ref.shape, F32)
        acc_ref[...] += _dot_tn(a_ref[...], b_ref[...])

        @pl.when(k == nk - 1)
        def _():
            o_ref[...] = acc_ref[...].astype(out_dtype)

    return pl.pallas_call(
        body, name=name, grid=(n // bn, nk),
        out_shape=jax.ShapeDtypeStruct((n // bn, m, bn), out_dtype),
        in_specs=[pl.BlockSpec((bt, m), a_map), pl.BlockSpec((bt, bn), lambda j, k: (k, j))]
        + [pl.BlockSpec(memory_space=pl.ANY)] * len(extra),
        out_specs=pl.BlockSpec((None, m, bn), lambda j, k: (j, 0, 0)),
        scratch_shapes=[pltpu.VMEM((m, bn), F32)],
        compiler_params=_cparams(("parallel", "arbitrary"), VMEM_BIG),
    )(a, b, *extra)


def _ssm_bgrad(d_bre, d_bim, bt_re, bt_im, rows_in, fold):
    gh, cb = d_bre.shape
    nb = SSM_SPLIT
    rb = gh // nb
    gp = nb * cb
    p = fold.shape[1]

    def body(dr_ref, di_ref, br_ref, bi_ref, rin_ref, f_ref, dbr_ref, dbi_ref, rout_ref):
        row = lax.broadcasted_iota(jnp.int32, (rb, cb), 0)
        col = lax.broadcasted_iota(jnp.int32, (rb, cb), 1)
        mask = (row >> 4) == (col >> 6)
        gr = jnp.where(mask, dr_ref[...], 0.0)
        gi = jnp.where(mask, di_ref[...], 0.0)
        cr, ci = rin_ref[0:1, :], rin_ref[1:2, :]
        dbr_ref[...] = _split3_dot(cr * gr + ci * gi, f_ref[...])
        dbi_ref[...] = _split3_dot(cr * gi - ci * gr, f_ref[...])
        br, bi = br_ref[...], bi_ref[...]
        rout_ref[...] = jnp.zeros(rout_ref.shape, F32)
        rout_ref[0:1, :] = _colsum(br * gr + bi * gi)
        rout_ref[1:2, :] = _colsum(br * gi - bi * gr)

    dspec = pl.BlockSpec((rb, cb), lambda j: (j, 0))
    bspec = pl.BlockSpec((rb, cb), lambda j: (j, j))
    rspec = pl.BlockSpec((SUBLANES, cb), lambda j: (0, j))
    ospec = pl.BlockSpec((rb, p), lambda j: (j, 0))
    return pl.pallas_call(
        body, name="ssm_bgrad", grid=(nb,),
        out_shape=(jax.ShapeDtypeStruct((gh, p), F32), jax.ShapeDtypeStruct((gh, p), F32),
                   jax.ShapeDtypeStruct((SUBLANES, gp), F32)),
        in_specs=[dspec, dspec, bspec, bspec, rspec, _full(fold.shape)],
        out_specs=(ospec, ospec, rspec),
        compiler_params=_cparams(("parallel",)),
    )(d_bre, d_bim, bt_re, bt_im, rows_in, fold)


def _ssm_cgrad(d_cre, d_cim, fold):
    gp, cb = d_cre.shape
    nb = SSM_SPLIT
    rb = gp // nb
    h = fold.shape[1]

    def body(dr_ref, di_ref, f_ref, cr_ref, ci_ref):
        row = lax.broadcasted_iota(jnp.int32, (rb, cb), 0)
        col = lax.broadcasted_iota(jnp.int32, (rb, cb), 1)
        mask = (row >> 6) == (col >> 4)
        cr_ref[...] = _split3_dot(jnp.where(mask, dr_ref[...], 0.0), f_ref[...])
        ci_ref[...] = -_split3_dot(jnp.where(mask, di_ref[...], 0.0), f_ref[...])

    cspec = pl.BlockSpec((rb, cb), lambda j: (j, 0))
    ospec = pl.BlockSpec((rb, h), lambda j: (j, 0))
    return pl.pallas_call(
        body, name="ssm_cgrad", grid=(nb,),
        out_shape=(jax.ShapeDtypeStruct((gp, h), F32),) * 2,
        in_specs=[cspec, cspec, _full(fold.shape)], out_specs=(ospec, ospec),
        compiler_params=_cparams(("parallel",)),
    )(d_cre, d_cim, fold)


def _ssm_lamgrad(lam_re, lam_im, log_step, abar_re, abar_im, coef_re, coef_im, gc_re, gc_im, ga_re, ga_im):
    g, p = lam_re.shape

    def body(lr_ref, li_ref, ls_ref, ar_ref, ai_ref, cr_ref, ci_ref, gcr_ref, gci_ref, gar_ref, gai_ref,
             dlr_ref, dli_ref, dls_ref):
        lam_raw = lr_ref[...]
        lr = jnp.minimum(lam_raw, LAMBDA_RE_MAX)
        li = li_ref[...]
        st = jnp.exp(ls_ref[...])
        den = lr * lr + li * li
        gcr, gci = gcr_ref[...], gci_ref[...]
        gab_r = gar_ref[...] + (lr * gcr - li * gci) / den
        gab_i = gai_ref[...] + (lr * gci + li * gcr) / den
        cr, ci = cr_ref[...], ci_ref[...]
        wr = -(cr * lr + ci * li) / den
        wi = -(ci * lr - cr * li) / den
        gl_r = wr * gcr + wi * gci
        gl_i = wr * gci - wi * gcr
        ar, ai = ar_ref[...], ai_ref[...]
        gw_r = ar * gab_r + ai * gab_i
        gw_i = ar * gab_i - ai * gab_r
        gl_r = gl_r + st * gw_r
        gl_i = gl_i + st * gw_i
        pass_through = jnp.where(lam_raw < LAMBDA_RE_MAX, 1.0, jnp.where(lam_raw == LAMBDA_RE_MAX, 0.5, 0.0))
        dlr_ref[...] = gl_r * pass_through
        dli_ref[...] = gl_i
        dls_ref[...] = st * jnp.sum(lr * gw_r + li * gw_i, axis=1, keepdims=True)

    sds = jax.ShapeDtypeStruct((g, p), F32)
    return pl.pallas_call(body, name="ssm_lamgrad", out_shape=(sds, sds, jax.ShapeDtypeStruct((g, 1), F32)))(
        lam_re, lam_im, log_step, abar_re, abar_im, coef_re, coef_im, gc_re, gc_im, ga_re, ga_im)


def _adamw_math(w, g, m, v):
    m = ADAM_B1 * m + (1.0 - ADAM_B1) * g
    v = ADAM_B2 * v + (1.0 - ADAM_B2) * (g * g)
    m_hat = m / (1.0 - ADAM_B1 ** ADAM_STEP)
    v_hat = v / (1.0 - ADAM_B2 ** ADAM_STEP)
    delta = -ADAM_LR * (m_hat / (jnp.sqrt(v_hat) + ADAM_EPS) + ADAM_WD * w)
    return delta, m, v


def _adamw_big(p_mine, p_sib, w, m, v, name):
    r, c = w.shape
    rb = 64 if r % 64 == 0 else r

    def body(a_ref, b_ref, w_ref, m_ref, v_ref, g_ref, d_ref, mo_ref, vo_ref):
        g = a_ref[...] + b_ref[...]
        g_ref[...] = g
        d_ref[...], mo_ref[...], vo_ref[...] = _adamw_math(w_ref[...], g, m_ref[...], v_ref[...])

    spec = pl.BlockSpec((rb, c), lambda i: (i, 0))
    sds = jax.ShapeDtypeStruct((r, c), F32)
    return pl.pallas_call(
        body, name=name, grid=(r // rb,), out_shape=(sds,) * 4, in_specs=[spec] * 5, out_specs=(spec,) * 4,
        compiler_params=_cparams(("parallel",)),
    )(p_mine, p_sib, w, m, v)


def _sum_blocks(stack, name):
    n, r, c = stack.shape
    rb = 64 if r % 64 == 0 else r

    def body(s_ref, o_ref):
        acc = s_ref[0].astype(F32)
        for k in range(1, n):
            acc = acc + s_ref[k].astype(F32)
        o_ref[...] = acc

    return pl.pallas_call(
        body, name=name, grid=(r // rb,), out_shape=jax.ShapeDtypeStruct((r, c), F32),
        in_specs=[pl.BlockSpec((n, rb, c), lambda i: (0, i, 0))], out_specs=pl.BlockSpec((rb, c), lambda i: (i, 0)),
        compiler_params=_cparams(("parallel",)),
    )(stack)


def _adamw_ada(c_all, dmod_cols, w, m, v):
    d, n = w.shape
    bn = 512

    def body(c_ref, dm_ref, w_ref, m_ref, v_ref, g_ref, d_ref, mo_ref, vo_ref):
        cc = c_ref[...]
        g = _dot_tn(cc * _sigmoid(cc), dm_ref[...])
        g_ref[...] = g
        d_ref[...], mo_ref[...], vo_ref[...] = _adamw_math(w_ref[...], g, m_ref[...], v_ref[...])

    spec = pl.BlockSpec((d, bn), lambda j: (0, j))
    sds = jax.ShapeDtypeStruct((d, n), F32)
    return pl.pallas_call(
        body, name="adamw_ada", grid=(n // bn,), out_shape=(sds,) * 4,
        in_specs=[_full((N_DEV, d)), pl.BlockSpec((N_DEV, bn), lambda j: (0, j)), spec, spec, spec],
        out_specs=(spec,) * 4, compiler_params=_cparams(("parallel",)),
    )(c_all, dmod_cols, w, m, v)


def _adamw_small(items):
    n = len(items)

    def body(*refs):
        ins, outs = refs[:4 * n], refs[4 * n:]
        for k in range(n):
            w_ref, g_ref, m_ref, v_ref = ins[4 * k:4 * k + 4]
            outs[3 * k][...], outs[3 * k + 1][...], outs[3 * k + 2][...] = _adamw_math(
                w_ref[...], g_ref[...], m_ref[...], v_ref[...])

    flat = [a for it in items for a in it]
    out_shape = tuple(jax.ShapeDtypeStruct(it[0].shape, F32) for it in items for _ in range(3))
    res = pl.pallas_call(body, name="adamw_small", out_shape=out_shape)(*flat)
    return [tuple(res[3 * k:3 * k + 3]) for k in range(n)]


def _group_mean_matrix(n, group):
    idx = np.arange(n) // group
    return (idx[:, None] == idx[None, :]).astype(np.float32) / group


def _fold_matrix(n, period):
    return (np.arange(n)[:, None] % period == np.arange(period)[None, :]).astype(np.float32)


def _rows8(*rows):
    c = rows[0].shape[-1]
    pad = jnp.zeros((SUBLANES - len(rows), c), F32)
    return jnp.concatenate([r.reshape(1, c) for r in rows] + [pad], axis=0)


def _to_rows(a, width):
    flat = a.reshape(-1)
    n = -(-flat.shape[0] // width)
    flat = jnp.pad(flat, (0, n * width - flat.shape[0]))
    return flat.reshape(n, width)


def kernel(x, c, w_ada, b_ada, g_pre_mix, g_post_mix, w_in, ssm_lam_re, ssm_lam_im, ssm_log_step, ssm_b_re, ssm_b_im, ssm_c_re, ssm_c_im, ssm_d, glu_w, glu_b, g_out_ssm, conv_w, g_out_conv, w_out, g_pre_ffn, g_post_ffn, w_up, ffn_conv_w, w_down, loss_target, m_w_ada, m_b_ada, m_g_pre_mix, m_g_post_mix, m_w_in, m_ssm_lam_re, m_ssm_lam_im, m_ssm_log_step, m_ssm_b_re, m_ssm_b_im, m_ssm_c_re, m_ssm_c_im, m_ssm_d, m_glu_w, m_glu_b, m_g_out_ssm, m_conv_w, m_g_out_conv, m_w_out, m_g_pre_ffn, m_g_post_ffn, m_w_up, m_ffn_conv_w, m_w_down, v_w_ada, v_b_ada, v_g_pre_mix, v_g_post_mix, v_w_in, v_ssm_lam_re, v_ssm_lam_im, v_ssm_log_step, v_ssm_b_re, v_ssm_b_im, v_ssm_c_re, v_ssm_c_im, v_ssm_d, v_glu_w, v_glu_b, v_g_out_ssm, v_conv_w, v_g_out_conv, v_w_out, v_g_pre_ffn, v_g_post_ffn, v_w_up, v_ffn_conv_w, v_w_down):
    xs = x[0]
    tgt = loss_target[0]
    t, d = xs.shape
    xi, yi, ci = lax.axis_index("x"), lax.axis_index("y"), lax.axis_index("c")
    chip = 2 * xi + yi
    dev = 2 * chip + ci

    n_groups, n_state = ssm_lam_re.shape[1:]
    n_gch = ssm_b_re.shape[3]
    d_ssm = n_groups * n_gch
    gp = n_groups * n_state
    n_ada = w_ada.shape[2]
    d_ff = w_down.shape[1] * N_CHIPS
    n_upc = w_up.shape[2]

    w_names = ("w_in", "glu_w", "w_out", "w_up", "w_down")
    c_gath, _ = _allgather8(jnp.broadcast_to(c, (SUBLANES, d)), SUBLANES, "gather_c")
    c_all = c_gath.reshape(N_DEV, SUBLANES, d)[:, 0, :]
    b_sh = lax.dynamic_slice(b_ada, (0, chip * n_ada), (1, n_ada))
    mod_sh = _mod_shard(c_all, w_ada[0], b_sh)

    def pad8(a):
        return jnp.concatenate([a, jnp.zeros((SUBLANES - a.shape[0], a.shape[1]), a.dtype)], axis=0)

    w_names = ("mod", "conv_w", "ffn_conv_w", "w_in", "glu_w", "w_out", "w_up", "w_down")
    w_own = [mod_sh, pad8(conv_w[0]), pad8(ffn_conv_w[0])]
    w_own += [w[0].astype(BF16) for w in (w_in, glu_w, w_out, w_up, w_down)]
    w_send, w_recv, w_src, w_land, _ = _chips_start("weights_start", True, w_own, [_landing(a, chip) for a in w_own])

    def weights(names, after):
        ks = [w_names.index(nm) for nm in names]
        return _chips_wait("weights_wait_" + names[-1], True, [w_send[k] for k in ks], [w_recv[k] for k in ks],
                           [w_src[k] for k in ks], [w_land[k] for k in ks], after)

    lam_re, lam_im = ssm_lam_re[0], ssm_lam_im[0]
    log_step = ssm_log_step[0].reshape(n_groups, 1)
    abar_re, abar_im, coef_re, coef_im = _ssm_prep(lam_re, lam_im, log_step)
    a_rows = _rows8(abar_re.reshape(1, gp), abar_im.reshape(1, gp))
    coef_rows = _rows8(coef_re.reshape(1, gp), coef_im.reshape(1, gp))
    bt_re = jnp.tile(ssm_b_re[0].transpose(0, 2, 1).reshape(d_ssm, n_state), (1, n_groups))
    bt_im = jnp.tile(ssm_b_im[0].transpose(0, 2, 1).reshape(d_ssm, n_state), (1, n_groups))
    ct_re = jnp.tile(ssm_c_re[0].transpose(0, 2, 1).reshape(gp, n_gch), (1, n_groups))
    ct_im = jnp.tile(ssm_c_im[0].transpose(0, 2, 1).reshape(gp, n_gch), (1, n_groups))
    bblk_re, bblk_im, cblk_re, cblk_im = _ssm_blocks(bt_re, bt_im, ct_re, ct_im, coef_rows)

    h16 = jnp.asarray(_group_mean_matrix(d_ssm, n_gch), BF16)
    h64 = jnp.asarray(_group_mean_matrix(d_ssm, CONV_HEAD_DIM), BF16)

    g_mod, g_cw, g_fw, w_in_st = weights(("mod", "conv_w", "ffn_conv_w", "w_in"), bblk_re)
    mod_all = g_mod.transpose(1, 0, 2).reshape(N_DEV, N_CHIPS * n_ada)
    mod = lax.dynamic_slice(mod_all, (dev, 0), (1, N_CHIPS * n_ada))
    sh1, sc1, gt1, sh2, sc2, gt2 = [mod[:, k * d:(k + 1) * d] for k in range(6)]
    convw_full = pad8(g_cw[:, :3, :].transpose(1, 0, 2).reshape(3, d_ssm))
    fw_full = pad8(g_fw[:, :3, :].transpose(1, 0, 2).reshape(3, N_CHIPS * n_upc))

    v512 = _rows8(ssm_d, glu_b, g_out_ssm, g_out_conv)
    vec1 =_rows8(g_pre_mix, 1.0 + sc1, sh1)
    vd1 = _rows8(g_post_mix, gt1)
    vec2 = _rows8(g_pre_ffn, 1.0 + sc2, sh2)
    vd2 = _rows8(g_post_ffn, gt2)

    proj, bu_re, bu_im, h1b = _mix_in(xs, vec1, w_in_st, bblk_re, bblk_im)
    s_re, s_im = _scan_fwd(a_rows, bu_re, bu_im)
    g_glu, g_wout = weights(("glu_w", "w_out"), s_re)
    glu_full = g_glu.reshape(d_ssm, d_ssm)
    w_out_full = g_wout.reshape(2 * d_ssm, d)
    y1, o_mix, x2 = _mix_out(xs, proj, s_re, s_im, cblk_re, cblk_im, v512, convw_full, glu_full, h16, h64,
                             w_out_full, vd1)
    (w_up_st,) = weights(("w_up",), x2)
    up, h2b = _ffn_up(x2, vec2, w_up_st)
    (g_wdown,) = weights(("w_down",), up)
    w_down_full = g_wdown.reshape(d_ff, d)
    actb, ddnb, dout, dhid, vp_dn, loss_blk = _ffn_down(up, fw_full, w_down_full, x2, tgt, vd2)

    g_names = ("w_down", "w_up", "w_out", "glu_w", "w_in")
    gw_down = _matmul_tn(actb, ddnb, d_ff, d, BF16, "dw_down", bt=1024).reshape(N_CHIPS, d_ff // N_CHIPS, d)
    dx2, dupb, vp_up, df_rows = _ffn_up_bwd(dhid, up, fw_full, x2, dout, vec2, w_up_st)
    gw_up = _matmul_tn(h2b, dupb, d, n_upc, BF16, "dw_up", bt=2048)
    ga_send, ga_recv, ga_src, ga_land, ga_token = _chips_start(
        "grads_start_ffn", False, [gw_down, gw_up],
        [_landing(lax.dynamic_index_in_dim(g, chip, 0, False), chip) for g in (gw_down, gw_up)])
    (dob, ycatb, zb, dqb, dy1, g_re, g_im, dcc, dbg, vp_mo, vp5) = _mix_out_bwd(
        dx2, o_mix, y1, proj, cblk_re, cblk_im, v512, convw_full, glu_full, h16, h64, w_out_full,
        vd1 + ga_token[0:1, 0:1])
    gt_re, gt_im, ga_re8, ga_im8 = _scan_bwd(a_rows, g_re, g_im, s_re, s_im)
    grad_x, dprojb, vp_mi = _mix_in_bwd(gt_re, gt_im, bblk_re, bblk_im, dy1, dcc, dbg, proj, xs, dx2, vec1, v512,
                                        convw_full, w_in_st)
    gw_out = _matmul_tn(ycatb, dob, 2 * d_ssm, d, BF16, "dw_out", bt=2048).reshape(N_CHIPS, 2 * d_ssm // N_CHIPS, d)
    gw_glu = _matmul_tn(zb, dqb, d_ssm, d_ssm, BF16, "dw_glu", bt=2048).reshape(N_CHIPS, d_ssm // N_CHIPS, d_ssm)
    gw_in = _matmul_tn(h1b, dprojb, d, w_in.shape[2], BF16, "dw_in", bt=2048)
    gb_send, gb_recv, gb_src, gb_land, gb_token = _chips_start(
        "grads_start_mix", False, [gw_out, gw_glu, gw_in],
        [_landing(lax.dynamic_index_in_dim(g, chip, 0, False), chip) for g in (gw_out, gw_glu, gw_in)])
    ssm_u, ssm_s = d_ssm // SSM_SPLIT, gp // SSM_SPLIT
    d_bre = _matmul_tn(proj, gt_re, ssm_u, ssm_s, F32, "d_bre", diag=True, bt=2048, after=gb_token)
    d_bim = _matmul_tn(proj, gt_im, ssm_u, ssm_s, F32, "d_bim", diag=True, bt=2048)
    d_cre = _matmul_tn(s_re, dy1, ssm_s, ssm_u, F32, "d_cre", diag=True, bt=2048)
    d_cim = _matmul_tn(s_im, dy1, ssm_s, ssm_u, F32, "d_cim", diag=True, bt=2048)
    d_bre, d_bim = d_bre.reshape(d_ssm, ssm_s), d_bim.reshape(d_ssm, ssm_s)
    d_cre, d_cim = d_cre.reshape(gp, ssm_u), d_cim.reshape(gp, ssm_u)

    fold_b = jnp.asarray(_fold_matrix(ssm_s, n_state), BF16)
    fold_c = jnp.asarray(_fold_matrix(ssm_u, n_gch), BF16)
    db_re_f, db_im_f, gc_rows = _ssm_bgrad(d_bre, d_bim, bt_re, bt_im, coef_rows, fold_b)
    dc_re_f, dc_im_f = _ssm_cgrad(d_cre, d_cim, fold_c)
    ga_sum = _ga_rowsum(ga_re8, ga_im8)
    g_lam_re, g_lam_im, g_log_step = _ssm_lamgrad(
        lam_re, lam_im, log_step, abar_re, abar_im, coef_re, coef_im,
        gc_rows[0].reshape(n_groups, n_state), gc_rows[1].reshape(n_groups, n_state),
        ga_sum[0].reshape(n_groups, n_state), ga_sum[1].reshape(n_groups, n_state))
    g_b_re = db_re_f.reshape(n_groups, n_gch, n_state).transpose(0, 2, 1)
    g_b_im = db_im_f.reshape(n_groups, n_gch, n_state).transpose(0, 2, 1)
    g_c_re = dc_re_f.reshape(n_groups, n_state, n_gch).transpose(0, 2, 1)
    g_c_im = dc_im_f.reshape(n_groups, n_state, n_gch).transpose(0, 2, 1)

    dmod = jnp.concatenate([vp_mi[0:1], vp_mi[1:2], vp_mo[0:1], vp_up[0:1], vp_up[1:2], vp_dn[0:1]], axis=1)
    small = [
        ("g_pre_mix", vp_mi[2:3]), ("g_post_mix", vp_mo[1:2]), ("g_pre_ffn", vp_up[2:3]), ("g_post_ffn", vp_dn[1:2]),
        ("ssm_lam_re", g_lam_re), ("ssm_lam_im", g_lam_im), ("ssm_log_step", g_log_step),
        ("ssm_b_re", g_b_re), ("ssm_b_im", g_b_im), ("ssm_c_re", g_c_re), ("ssm_c_im", g_c_im),
        ("ssm_d", vp5[3:4]), ("glu_b", vp5[2:3]), ("g_out_ssm", vp5[0:1]), ("g_out_conv", vp5[1:2]),
        ("conv_w", vp5[4:7]), ("ffn_conv_w", df_rows[0:3]), ("loss", loss_blk[0:1, 0:1]),
    ]
    packed, offsets, row = [], {}, 0
    for name, a in small:
        r = _to_rows(a, d)
        offsets[name] = (row, a.shape)
        packed.append(r)
        row += r.shape[0]
    n_small = -(-row // SUBLANES) * SUBLANES
    packed.append(jnp.zeros((n_small - row, d), F32))
    packed.append(pad8(dmod.reshape(6, d)))
    pack = jnp.concatenate(packed, axis=0)
    gath, sums = _allgather8(pack, n_small + SUBLANES, "reduce_small")
    dmod_all = gath.reshape(N_DEV, n_small + SUBLANES, d)[:, n_small:n_small + 6, :].reshape(N_DEV, 6 * d)
    g_b_ada = sums[n_small:n_small + 6].reshape(1, 6 * d)

    def unpack(name):
        r0, shape = offsets[name]
        size = math.prod(shape)
        nrow = -(-size // d)
        return sums[r0:r0 + nrow].reshape(-1)[:size].reshape(shape)

    dmod_cols = lax.dynamic_slice(dmod_all, (0, chip * n_ada), (N_DEV, n_ada))
    ada = _adamw_ada(c_all, dmod_cols, w_ada[0], m_w_ada[0], v_w_ada[0])

    landed = _chips_wait("grads_wait", False, list(ga_send) + list(gb_send), list(ga_recv) + list(gb_recv),
                         list(ga_src) + list(gb_src), list(ga_land) + list(gb_land), ada[0])
    partial = [_sum_blocks(s, "sum_" + nm) for s, nm in zip(landed, g_names)]
    theirs = _swap_sibling(partial, "swap_partials")
    big = {}
    for nm, pm, ps, w_, m_, v_ in zip(
            g_names, partial, theirs,
            (w_down, w_up, w_out, glu_w, w_in), (m_w_down, m_w_up, m_w_out, m_glu_w, m_w_in),
            (v_w_down, v_w_up, v_w_out, v_glu_w, v_w_in)):
        big[nm] = _adamw_big(pm, ps, w_[0], m_[0], v_[0], "adamw_" + nm)

    g_small = {name: unpack(name) for name, _ in small}
    g_small["b_ada"] = g_b_ada
    g_small["conv_w"] = lax.dynamic_slice(g_small["conv_w"], (0, chip * conv_w.shape[2]), (3, conv_w.shape[2]))
    g_small["ffn_conv_w"] = lax.dynamic_slice(g_small["ffn_conv_w"], (0, chip * n_upc), (3, n_upc))
    g_small["ssm_log_step"] = g_small["ssm_log_step"].reshape(1, n_groups)
    small_params = {
        "b_ada": (b_ada, m_b_ada, v_b_ada), "g_pre_mix": (g_pre_mix, m_g_pre_mix, v_g_pre_mix),
        "g_post_mix": (g_post_mix, m_g_post_mix, v_g_post_mix), "ssm_lam_re": (ssm_lam_re, m_ssm_lam_re, v_ssm_lam_re),
        "ssm_lam_im": (ssm_lam_im, m_ssm_lam_im, v_ssm_lam_im),
        "ssm_log_step": (ssm_log_step, m_ssm_log_step, v_ssm_log_step),
        "ssm_b_re": (ssm_b_re, m_ssm_b_re, v_ssm_b_re), "ssm_b_im": (ssm_b_im, m_ssm_b_im, v_ssm_b_im),
        "ssm_c_re": (ssm_c_re, m_ssm_c_re, v_ssm_c_re), "ssm_c_im": (ssm_c_im, m_ssm_c_im, v_ssm_c_im),
        "ssm_d": (ssm_d, m_ssm_d, v_ssm_d), "glu_b": (glu_b, m_glu_b, v_glu_b),
        "g_out_ssm": (g_out_ssm, m_g_out_ssm, v_g_out_ssm), "conv_w": (conv_w, m_conv_w, v_conv_w),
        "g_out_conv": (g_out_conv, m_g_out_conv, v_g_out_conv), "g_pre_ffn": (g_pre_ffn, m_g_pre_ffn, v_g_pre_ffn),
        "g_post_ffn": (g_post_ffn, m_g_post_ffn, v_g_post_ffn),
        "ffn_conv_w": (ffn_conv_w, m_ffn_conv_w, v_ffn_conv_w),
    }

    def flat2d(a):
        n = a.size
        return a.reshape(n // 1024, 1024) if n % 1024 == 0 and n > 1024 else a.reshape(-1, a.shape[-1])

    names = list(small_params)
    items = []
    for nm in names:
        w_, m_, v_ = small_params[nm]
        items.append((flat2d(w_[0]), flat2d(g_small[nm].reshape(w_[0].shape)), flat2d(m_[0]), flat2d(v_[0])))
    upd = _adamw_small(items)
    small_out = {}
    for nm, (dl, mo, vo) in zip(names, upd):
        shp = small_params[nm][0].shape
        small_out[nm] = (g_small[nm].reshape(shp), dl.reshape(shp), mo.reshape(shp), vo.reshape(shp))

    loss = g_small["loss"][0, 0]

    order = ["w_ada", "b_ada", "g_pre_mix", "g_post_mix", "w_in", "ssm_lam_re", "ssm_lam_im", "ssm_log_step",
             "ssm_b_re", "ssm_b_im", "ssm_c_re", "ssm_c_im", "ssm_d", "glu_w", "glu_b", "g_out_ssm", "conv_w",
             "g_out_conv", "w_out", "g_pre_ffn", "g_post_ffn", "w_up", "ffn_conv_w", "w_down"]
    results = {"w_ada": tuple(a[None] for a in ada)}
    for nm in big:
        results[nm] = tuple(a[None] for a in big[nm])
    results.update(small_out)
    outs = [loss, grad_x[None]]
    for k in range(4):
        outs += [results[nm][k] for nm in order]
    return tuple(outs)


def _ga_rowsum(ga_re8, ga_im8):
    n = ga_re8.shape[1]

    def body(r_ref, i_ref, o_ref):
        o_ref[...] = jnp.zeros(o_ref.shape, F32)
        o_ref[0:1, :] = _colsum(r_ref[...])
        o_ref[1:2, :] = _colsum(i_ref[...])

    return pl.pallas_call(body, name="ga_rowsum", out_shape=jax.ShapeDtypeStruct((SUBLANES, n), F32))(ga_re8, ga_im8)
```

```python
import functools
import math

import jax
import jax.numpy as jnp
import numpy as np
from jax import lax
from jax.experimental import pallas as pl
from jax.experimental.pallas import tpu as pltpu

F32 = jnp.float32
BF16 = jnp.bfloat16
MESH = pl.DeviceIdType.MESH

EPS = 1e-6
LAMBDA_RE_MAX = -1e-4
ADAM_LR = 0.001
ADAM_B1 = 0.9
ADAM_B2 = 0.999
ADAM_EPS = 1e-08
ADAM_WD = 0.01
ADAM_STEP = 10

SUBLANES = 8
BF16_ROWS = 16
N_CHIPS = 4
N_DEV = 8
CONV_HEAD_DIM = 64
VMEM_BIG = 56 * 1024 * 1024
VMEM_MID = 40 * 1024 * 1024

TB_MIX = 256
TB_FFN = 256
TB_SCAN = 1024
W_SCAN = 256
SSM_SPLIT = 4
CW_FFN = 256
SCAN_UNROLL = 4
TB_TN = 512


def _cparams(sem=None, vmem=None):
    kw = {}
    if sem is not None:
        kw["dimension_semantics"] = sem
    if vmem is not None:
        kw["vmem_limit_bytes"] = vmem
    return pltpu.CompilerParams(**kw)


def _blk(t, pref):
    return pref if t % pref == 0 else t


def _dot(a, b):
    return jnp.dot(a.astype(BF16), b.astype(BF16), preferred_element_type=F32)


def _dot_nt(a, b):
    return lax.dot_general(a.astype(BF16), b.astype(BF16), (((1,), (1,)), ((), ())),
                           preferred_element_type=F32)


def _dot_tn(a, b):
    return lax.dot_general(a.astype(BF16), b.astype(BF16), (((0,), (0,)), ((), ())),
                           preferred_element_type=F32)


def _sigmoid(x):
    return 0.5 * jnp.tanh(0.5 * x) + 0.5


_GELU_K = math.sqrt(2.0 / math.pi)
_GELU_C = 0.044715


def _gelu(x):
    th = jnp.tanh(_GELU_K * (x + _GELU_C * x * x * x))
    return 0.5 * x * (1.0 + th)


def _gelu_grad(x):
    x2 = x * x
    th = jnp.tanh(_GELU_K * (x + _GELU_C * x2 * x))
    return 0.5 * (1.0 + th) + 0.5 * x * (1.0 - th * th) * _GELU_K * (1.0 + 3.0 * _GELU_C * x2)


def _rowmean(x):
    return jnp.mean(x, axis=-1, keepdims=True)


def _colsum(x):
    return jnp.sum(x, axis=0, keepdims=True)


def _split_dot(x, m):
    hi = x.astype(BF16)
    lo = (x - hi.astype(F32)).astype(BF16)
    return (jnp.dot(hi, m, preferred_element_type=F32) + jnp.dot(lo, m, preferred_element_type=F32))


def _split3_dot(x, m):
    hi = x.astype(BF16)
    r1 = x - hi.astype(F32)
    mid = r1.astype(BF16)
    lo = (r1 - mid.astype(F32)).astype(BF16)
    return (jnp.dot(hi, m, preferred_element_type=F32) + jnp.dot(mid, m, preferred_element_type=F32)
            + jnp.dot(lo, m, preferred_element_type=F32))


def _shift_down(x, halo, k):
    r = pltpu.roll(x, k, 0)
    row = lax.broadcasted_iota(jnp.int32, x.shape, 0)
    for j in range(k):
        r = jnp.where(row == j, halo[SUBLANES - k + j:SUBLANES - k + j + 1, :], r)
    return r


def _shift_up(x, halo, k):
    n = x.shape[0]
    r = pltpu.roll(x, n - k, 0)
    row = lax.broadcasted_iota(jnp.int32, x.shape, 0)
    for j in range(k):
        r = jnp.where(row == n - k + j, halo[j:j + 1, :], r)
    return r


def _acc_rows(ref, first, rows):
    @pl.when(first)
    def _():
        ref[...] = jnp.zeros(ref.shape, ref.dtype)
    for j, r in enumerate(rows):
        ref[j:j + 1, :] += r


def _rows(tb, c, col=0):
    return pl.BlockSpec((tb, c), lambda i, col=col: (i, col))


def _full(shape):
    nd = len(shape)
    return pl.BlockSpec(shape, lambda i, nd=nd: (0,) * nd)


def _resident(shape):
    nd = len(shape)
    return pl.BlockSpec(shape, lambda i, nd=nd: (0,) * nd, pipeline_mode=pl.Buffered(1))


def _halo_prev(tb, c, col=0):
    per = tb // SUBLANES
    return pl.BlockSpec((SUBLANES, c), lambda i, col=col: (jnp.maximum(i * per - 1, 0), col))


def _halo_next(tb, c, t, col=0, rows=SUBLANES):
    per = tb // rows
    last = t // rows - 1
    return pl.BlockSpec((rows, c), lambda i, col=col: (jnp.minimum((i + 1) * per, last), col))


def _mesh_pos():
    return lax.axis_index("x"), lax.axis_index("y"), lax.axis_index("c")


def _allgather8(x_pad, n_sum, name):
    m_per, n = x_pad.shape

    def body(x_ref, out_ref, sum_ref, send_sems, recv_sems, local_sem):
        x, y, c = _mesh_pos()
        me, sibling = (x, y, c), (x, y, 1 - c)
        chips = [(1 - x, y), (x, 1 - y), (1 - x, 1 - y)]

        def rows(px, py, pc):
            return out_ref.at[pl.ds((4 * px + 2 * py + pc) * m_per, m_per), :]

        def copy(k, block, to, src=None):
            return pltpu.make_async_remote_copy(
                src_ref=rows(*block) if src is None else src, dst_ref=rows(*block),
                send_sem=send_sems.at[k], recv_sem=recv_sems.at[k], device_id=to, device_id_type=MESH)

        mine = pltpu.make_async_copy(x_ref, rows(*me), local_sem)
        mine.start()
        first = [copy(0, me, sibling, src=x_ref)]
        first += [copy(1 + j, me, (*chip, c), src=x_ref) for j, chip in enumerate(chips)]
        for cp in first:
            cp.start()
        passed = [copy(4 + j, (*chip, c), sibling) for j, chip in enumerate(chips)]
        for j, chip in enumerate(chips):
            copy(1 + j, (*chip, c), me).wait_recv()
            passed[j].start()
        copy(0, sibling, me).wait_recv()
        for j, chip in enumerate(chips):
            copy(4 + j, (*chip, 1 - c), me).wait_recv()
        for cp in first + passed:
            cp.wait_send()
        mine.wait()
        acc = out_ref[0:n_sum, :]
        for k in range(1, N_DEV):
            acc = acc + out_ref[k * m_per:k * m_per + n_sum, :]
        sum_ref[...] = acc

    return pl.pallas_call(
        body, name=name,
        out_shape=(jax.ShapeDtypeStruct((N_DEV * m_per, n), F32), jax.ShapeDtypeStruct((n_sum, n), F32)),
        in_specs=[pl.BlockSpec(memory_space=pltpu.VMEM)],
        out_specs=(pl.BlockSpec(memory_space=pltpu.VMEM), pl.BlockSpec(memory_space=pltpu.VMEM)),
        scratch_shapes=[pltpu.SemaphoreType.DMA((7,)), pltpu.SemaphoreType.DMA((7,)), pltpu.SemaphoreType.DMA],
        compiler_params=_cparams(vmem=VMEM_MID),
    )(x_pad)


_HBM = pl.BlockSpec(memory_space=pltpu.HBM)
_SEM = pl.BlockSpec(memory_space=pltpu.SEMAPHORE)
_EFFECT = pltpu.SideEffectType.DATAFLOW_SIDE_EFFECTING


def _chip_copy(gather, src_ref, land_ref, send, recv, j, arrival):
    x, y, c = _mesh_pos()
    peer = [(1 - x, y), (x, 1 - y), (1 - x, 1 - y)][j]
    peer_chip = 2 * peer[0] + peer[1]
    my_chip = 2 * x + y
    return pltpu.make_async_remote_copy(
        src_ref=src_ref if gather else src_ref.at[peer_chip],
        dst_ref=land_ref.at[peer_chip if arrival else my_chip],
        send_sem=send.at[j], recv_sem=recv.at[j], device_id=(*peer, c), device_id_type=MESH)


def _chips_start(name, gather, srcs, lands):
    n = len(srcs)

    def body(*refs):
        src_refs, land_refs = refs[:n], refs[n:2 * n]
        outs = refs[2 * n:]
        sends, recvs, token = outs[:n], outs[n:2 * n], outs[-1]
        for k in range(n):
            for j in range(3):
                _chip_copy(gather, src_refs[k], land_refs[k], sends[k], recvs[k], j, False).start()
        token[...] = jnp.zeros(token.shape, F32)

    sem = pltpu.SemaphoreType.DMA((3,))
    thru = tuple(pltpu.HBM(a.shape, a.dtype) for a in list(srcs) + list(lands))
    res = pl.pallas_call(
        body, name=name,
        out_shape=(sem,) * (2 * n) + thru + (jax.ShapeDtypeStruct((SUBLANES, 128), F32),),
        in_specs=[_HBM] * (2 * n),
        out_specs=(_SEM,) * (2 * n) + (_HBM,) * (2 * n) + (pl.BlockSpec(memory_space=pltpu.VMEM),),
        input_output_aliases={k: 2 * n + k for k in range(2 * n)},
        compiler_params=pltpu.CompilerParams(has_side_effects=_EFFECT),
    )(*[pltpu.with_memory_space_constraint(a, pltpu.HBM) for a in list(srcs) + list(lands)])
    return res[:n], res[n:2 * n], res[2 * n:3 * n], res[3 * n:4 * n], res[-1]


def _chips_wait(name, gather, sends, recvs, srcs, lands, after):
    n = len(srcs)

    def body(*refs):
        src_refs, land_refs = refs[:n], refs[n:2 * n]
        sends_, recvs_ = refs[2 * n:3 * n], refs[3 * n:4 * n]
        for k in range(n):
            for j in range(3):
                cp = _chip_copy(gather, src_refs[k], land_refs[k], sends_[k], recvs_[k], j, True)
                cp.wait_send()
                cp.wait_recv()

    thru = tuple(pltpu.HBM(a.shape, a.dtype) for a in list(srcs) + list(lands))
    res = pl.pallas_call(
        body, name=name, out_shape=thru,
        in_specs=[_HBM] * (2 * n) + [_SEM] * (2 * n) + [pl.BlockSpec(memory_space=pl.ANY)],
        out_specs=(_HBM,) * (2 * n),
        input_output_aliases={k: k for k in range(2 * n)},
        compiler_params=pltpu.CompilerParams(has_side_effects=_EFFECT),
    )(*srcs, *lands, *sends, *recvs, after)
    return res[n:]


def _landing(own, chip):
    zone = lax.empty((N_CHIPS,) + own.shape, own.dtype)
    return lax.dynamic_update_slice(zone, own[None], (chip,) + (0,) * own.ndim)


def _swap_sibling(arrs, name):
    n_arr = len(arrs)

    def body(*refs):
        ins, outs = refs[:n_arr], refs[n_arr:2 * n_arr]
        send_sems, recv_sems = refs[2 * n_arr:]
        x, y, c = _mesh_pos()
        copies = [pltpu.make_async_remote_copy(
            src_ref=ins[n], dst_ref=outs[n], send_sem=send_sems.at[n], recv_sem=recv_sems.at[n],
            device_id=(x, y, 1 - c), device_id_type=MESH) for n in range(n_arr)]
        for cp in copies:
            cp.start()
        for cp in copies:
            cp.wait()

    any_spec = pl.BlockSpec(memory_space=pl.ANY)
    return pl.pallas_call(
        body, name=name,
        out_shape=tuple(jax.ShapeDtypeStruct(a.shape, a.dtype) for a in arrs),
        in_specs=[any_spec] * n_arr, out_specs=tuple([any_spec] * n_arr),
        scratch_shapes=[pltpu.SemaphoreType.DMA((n_arr,)), pltpu.SemaphoreType.DMA((n_arr,))],
    )(*arrs)


def _mod_shard(c_all, w_ada_sh, b_sh):
    d, n = w_ada_sh.shape
    bn = 512

    def body(c_ref, w_ref, b_ref, o_ref):
        cc = c_ref[...]
        ca = cc * _sigmoid(cc)
        o_ref[...] = _dot(ca, w_ref[...]) + b_ref[...]

    return pl.pallas_call(
        body, name="mod_shard", grid=(n // bn,),
        out_shape=jax.ShapeDtypeStruct((N_DEV, n), F32),
        in_specs=[_full((N_DEV, d)), pl.BlockSpec((d, bn), lambda j: (0, j)), pl.BlockSpec((1, bn), lambda j: (0, j))],
        out_specs=pl.BlockSpec((N_DEV, bn), lambda j: (0, j)),
        compiler_params=_cparams(("parallel",)),
    )(c_all, w_ada_sh, b_sh)


def _ssm_prep(lam_re, lam_im, log_step):
    g, p = lam_re.shape

    def body(lr_ref, li_ref, ls_ref, ar_ref, ai_ref, cr_ref, ci_ref):
        lr = jnp.minimum(lr_ref[...], LAMBDA_RE_MAX)
        li = li_ref[...]
        st = jnp.exp(ls_ref[...])
        mag = jnp.exp(lr * st)
        ar = mag * jnp.cos(li * st)
        ai = mag * jnp.sin(li * st)
        den = lr * lr + li * li
        nr = ar - 1.0
        ar_ref[...] = ar
        ai_ref[...] = ai
        cr_ref[...] = (nr * lr + ai * li) / den
        ci_ref[...] = (ai * lr - nr * li) / den

    sds = jax.ShapeDtypeStruct((g, p), F32)
    return pl.pallas_call(body, name="ssm_prep", out_shape=(sds,) * 4)(lam_re, lam_im, log_step)


def _ssm_blocks(bt_re, bt_im, ct_re, ct_im, coef_rows):
    gh, gp = bt_re.shape
    nb = 4
    cb, rb = gp // nb, gp // nb

    def body(btr, bti, ctr, cti, cf, bre_o, bim_o, cre_o, cim_o):
        j = pl.program_id(0)
        row = lax.broadcasted_iota(jnp.int32, (gh, cb), 0)
        col = lax.broadcasted_iota(jnp.int32, (gh, cb), 1) + j * cb
        mask = (row >> 4) == (col >> 6)
        cr, ci = cf[0:1, :], cf[1:2, :]
        br, bi = btr[...], bti[...]
        bre_o[...] = jnp.where(mask, br * cr - bi * ci, 0.0).astype(BF16)
        bim_o[...] = jnp.where(mask, br * ci + bi * cr, 0.0).astype(BF16)
        row2 = lax.broadcasted_iota(jnp.int32, (rb, gh), 0) + j * rb
        col2 = lax.broadcasted_iota(jnp.int32, (rb, gh), 1)
        mask2 = (row2 >> 6) == (col2 >> 4)
        cre_o[...] = jnp.where(mask2, ctr[...], 0.0).astype(BF16)
        cim_o[...] = jnp.where(mask2, cti[...], 0.0).astype(BF16)

    bspec = pl.BlockSpec((gh, cb), lambda j: (0, j))
    cspec = pl.BlockSpec((rb, gh), lambda j: (j, 0))
    return pl.pallas_call(
        body, name="ssm_blocks", grid=(nb,),
        out_shape=(jax.ShapeDtypeStruct((gh, gp), BF16),) * 2 + (jax.ShapeDtypeStruct((gp, gh), BF16),) * 2,
        in_specs=[bspec, bspec, cspec, cspec, pl.BlockSpec((SUBLANES, cb), lambda j: (0, j))],
        out_specs=(bspec, bspec, cspec, cspec),
        compiler_params=_cparams(("parallel",)),
    )(bt_re, bt_im, ct_re, ct_im, coef_rows)


def _scan_consts(a_ref, reverse):
    w = a_ref.shape[1]
    ar1 = a_ref[0:1, :]
    ai1 = a_ref[1:2, :]
    if reverse:
        ai1 = -ai1
    pr, pi = [ar1], [ai1]
    for _ in range(1, SUBLANES):
        nr = pr[-1] * ar1 - pi[-1] * ai1
        ni = pr[-1] * ai1 + pi[-1] * ar1
        pr.append(nr)
        pi.append(ni)
    row = lax.broadcasted_iota(jnp.int32, (SUBLANES, w), 0)
    dist = (SUBLANES - 1 - row) if reverse else row

    def pick(vals):
        out = jnp.broadcast_to(vals[SUBLANES - 1], (SUBLANES, w))
        for r in range(SUBLANES - 1):
            out = jnp.where(dist == r, vals[r], out)
        return out

    p_r, p_i = pick(pr), pick(pi)
    steps = []
    for k in (1, 2, 4):
        steps.append((k, jnp.where(dist >= k, pr[k - 1], 0.0), jnp.where(dist >= k, pi[k - 1], 0.0)))
    a8 = (jnp.broadcast_to(pr[SUBLANES - 1], (SUBLANES, w)), jnp.broadcast_to(pi[SUBLANES - 1], (SUBLANES, w)))
    return row, p_r, p_i, steps, a8


def _scan_tile(xr, xi, cr, ci, consts, reverse):
    row, p_r, p_i, steps, (a8r, a8i) = consts
    for k, s_r, s_i in steps:
        sh = (SUBLANES - k) if reverse else k
        qr = pltpu.roll(xr, sh, 0)
        qi = pltpu.roll(xi, sh, 0)
        xr, xi = xr + s_r * qr - s_i * qi, xi + s_r * qi + s_i * qr
    outr = xr + p_r * cr - p_i * ci
    outi = xi + p_r * ci + p_i * cr
    e = 0 if reverse else SUBLANES - 1
    er = jnp.broadcast_to(xr[e:e + 1, :], xr.shape)
    ei = jnp.broadcast_to(xi[e:e + 1, :], xi.shape)
    return outr, outi, er + a8r * cr - a8i * ci, ei + a8r * ci + a8i * cr


def _scan_fwd(a_rows, bu_re, bu_im):
    t, n = bu_re.shape
    tb, w = _blk(t, TB_SCAN), W_SCAN
    ntile = tb // SUBLANES

    def body(a_ref, br_ref, bi_ref, sr_ref, si_ref, car, cai):
        @pl.when(pl.program_id(1) == 0)
        def _():
            car[...] = jnp.zeros(car.shape, F32)
            cai[...] = jnp.zeros(cai.shape, F32)
        consts = _scan_consts(a_ref, False)

        def tile(i, carry):
            o = pl.multiple_of(i * SUBLANES, SUBLANES)
            outr, outi, ncr, nci = _scan_tile(br_ref[pl.ds(o, SUBLANES), :], bi_ref[pl.ds(o, SUBLANES), :],
                                              carry[0], carry[1], consts, False)
            sr_ref[pl.ds(o, SUBLANES), :] = outr
            si_ref[pl.ds(o, SUBLANES), :] = outi
            return ncr, nci

        def tiles(i, carry):
            for s in range(SCAN_UNROLL):
                carry = tile(i * SCAN_UNROLL + s, carry)
            return carry

        cr, ci = lax.fori_loop(0, ntile // SCAN_UNROLL, tiles, (car[...], cai[...]))
        car[...] = cr
        cai[...] = ci

    spec = pl.BlockSpec((tb, w), lambda s, k: (k, s))
    sds = jax.ShapeDtypeStruct((t, n), F32)
    return pl.pallas_call(
        body, name="scan_fwd", grid=(n // w, t // tb), out_shape=(sds, sds),
        in_specs=[pl.BlockSpec((SUBLANES, w), lambda s, k: (0, s)), spec, spec], out_specs=(spec, spec),
        scratch_shapes=[pltpu.VMEM((SUBLANES, w), F32), pltpu.VMEM((SUBLANES, w), F32)],
        compiler_params=_cparams(("parallel", "arbitrary"), VMEM_MID),
    )(a_rows, bu_re, bu_im)


def _scan_bwd(a_rows, g_re, g_im, s_re, s_im):
    t, n = g_re.shape
    tb, w = _blk(t, TB_SCAN), W_SCAN
    ntile = tb // SUBLANES
    nt = t // tb

    def body(a_ref, gr_ref, gi_ref, sr_ref, si_ref, or_ref, oi_ref, gar_ref, gai_ref, car, cai):
        @pl.when(pl.program_id(1) == 0)
        def _():
            car[...] = jnp.zeros(car.shape, F32)
            cai[...] = jnp.zeros(cai.shape, F32)
            gar_ref[...] = jnp.zeros(gar_ref.shape, F32)
            gai_ref[...] = jnp.zeros(gai_ref.shape, F32)
        consts = _scan_consts(a_ref, True)
        row = consts[0]

        def tile(i, carry):
            cr, ci, accr, acci = carry
            o = pl.multiple_of((ntile - 1 - i) * SUBLANES, SUBLANES)
            outr, outi, ncr, nci = _scan_tile(gr_ref[pl.ds(o, SUBLANES), :], gi_ref[pl.ds(o, SUBLANES), :],
                                              cr, ci, consts, True)
            or_ref[pl.ds(o, SUBLANES), :] = outr
            oi_ref[pl.ds(o, SUBLANES), :] = outi
            gnr = jnp.where(row == SUBLANES - 1, cr, pltpu.roll(outr, SUBLANES - 1, 0))
            gni = jnp.where(row == SUBLANES - 1, ci, pltpu.roll(outi, SUBLANES - 1, 0))
            sr = sr_ref[pl.ds(o, SUBLANES), :]
            si = si_ref[pl.ds(o, SUBLANES), :]
            return ncr, nci, accr + sr * gnr + si * gni, acci + sr * gni - si * gnr

        def tiles(i, carry):
            for s in range(SCAN_UNROLL):
                carry = tile(i * SCAN_UNROLL + s, carry)
            return carry

        cr, ci, accr, acci = lax.fori_loop(0, ntile // SCAN_UNROLL, tiles,
                                           (car[...], cai[...], gar_ref[...], gai_ref[...]))
        car[...] = cr
        cai[...] = ci
        gar_ref[...] = accr
        gai_ref[...] = acci

    spec = pl.BlockSpec((tb, w), lambda s, k: (nt - 1 - k, s))
    aspec = pl.BlockSpec((SUBLANES, w), lambda s, k: (0, s))
    sds = jax.ShapeDtypeStruct((t, n), F32)
    asds = jax.ShapeDtypeStruct((SUBLANES, n), F32)
    return pl.pallas_call(
        body, name="scan_bwd", grid=(n // w, nt), out_shape=(sds, sds, asds, asds),
        in_specs=[aspec, spec, spec, spec, spec], out_specs=(spec, spec, aspec, aspec),
        scratch_shapes=[pltpu.VMEM((SUBLANES, w), F32), pltpu.VMEM((SUBLANES, w), F32)],
        compiler_params=_cparams(("parallel", "arbitrary"), VMEM_MID),
    )(a_rows, g_re, g_im, s_re, s_im)


def _mix_in(x, vec, w_in_st, b_re, b_im):
    t, d = x.shape
    ns, _, nc = w_in_st.shape
    dssm, nstate = b_re.shape
    du, ds = dssm // SSM_SPLIT, nstate // SSM_SPLIT
    tb = _blk(t, TB_MIX)

    def body(x_ref, vec_ref, w_ref, bre_ref, bim_ref, proj_ref, bur_ref, bui_ref, h1_ref):
        xv = x_ref[...]
        r = lax.rsqrt(_rowmean(xv * xv) + EPS)
        h = xv * r * vec_ref[0:1, :] * vec_ref[1:2, :] + vec_ref[2:3, :]
        hb = h.astype(BF16)
        h1_ref[...] = hb
        u = None
        for j in range(ns):
            pj = jnp.dot(hb, w_ref[j], preferred_element_type=F32)
            proj_ref[:, j * nc:(j + 1) * nc] = pj
            if j == 0:
                u = pj
        ub = u.astype(BF16)
        for q in range(SSM_SPLIT):
            rq, cq = slice(q * du, (q + 1) * du), slice(q * ds, (q + 1) * ds)
            bur_ref[:, cq] = jnp.dot(ub[:, rq], bre_ref[rq, cq], preferred_element_type=F32)
            bui_ref[:, cq] = jnp.dot(ub[:, rq], bim_ref[rq, cq], preferred_element_type=F32)

    return pl.pallas_call(
        body, name="mix_in", grid=(t // tb,),
        out_shape=(jax.ShapeDtypeStruct((t, ns * nc), F32), jax.ShapeDtypeStruct((t, nstate), F32),
                   jax.ShapeDtypeStruct((t, nstate), F32), jax.ShapeDtypeStruct((t, d), BF16)),
        in_specs=[_rows(tb, d), _full((SUBLANES, d)), _full(w_in_st.shape), _full(b_re.shape), _full(b_im.shape)],
        out_specs=(_rows(tb, ns * nc), _rows(tb, nstate), _rows(tb, nstate), _rows(tb, d)),
        compiler_params=_cparams(("parallel",), VMEM_BIG),
    )(x, vec, w_in_st, b_re, b_im)


def _head_ms(y, h_ref):
    return _split_dot(y * y, h_ref[...])


def _conv3(x, halo, w_ref):
    return w_ref[0:1, :] * _shift_down(x, halo, 2) + w_ref[1:2, :] * _shift_down(x, halo, 1) + w_ref[2:3, :] * x


def _mix_out(x, proj, s_re, s_im, c_re, c_im, v512, convw, glu_w, h16, h64, w_out, vd):
    t, d = x.shape
    dh = c_re.shape[1]
    nstate = s_re.shape[1]
    du, ds = dh // SSM_SPLIT, nstate // SSM_SPLIT
    tb = _blk(t, TB_MIX)

    def body(x_ref, u_ref, bg_ref, cg_ref, v_ref, cgh_ref, vh_ref, sr_ref, si_ref, cre_ref, cim_ref, p_ref,
             cw_ref, gw_ref, h16_ref, h64_ref, wo_ref, vd_ref, y1_ref, o_ref, x2_ref):
        i = pl.program_id(0)
        u = u_ref[...]
        ys = []
        for q in range(SSM_SPLIT):
            rq, cq = slice(q * ds, (q + 1) * ds), slice(q * du, (q + 1) * du)
            ys.append(_dot(sr_ref[:, rq], cre_ref[rq, cq]) - _dot(si_ref[:, rq], cim_ref[rq, cq]))
        ys = jnp.concatenate(ys, axis=1)
        y1 = ys + p_ref[0:1, :] * u
        y1_ref[...] = y1
        z = _gelu(y1)
        q = _dot(z, gw_ref[...]) + p_ref[1:2, :]
        ya = z * _sigmoid(q)
        na = ya * lax.rsqrt(_head_ms(ya, h16_ref) + EPS) * p_ref[2:3, :]
        cv = cg_ref[...] * v_ref[...]
        cvh = jnp.where(i > 0, cgh_ref[...] * vh_ref[...], 0.0)
        yb = bg_ref[...] * _conv3(cv, cvh, cw_ref)
        nb = yb * lax.rsqrt(_head_ms(yb, h64_ref) + EPS) * p_ref[3:4, :]
        o = _dot(na, wo_ref[0:dh, :]) + _dot(nb, wo_ref[dh:2 * dh, :])
        o_ref[...] = o
        on = o * lax.rsqrt(_rowmean(o * o) + EPS) * vd_ref[0:1, :]
        x2_ref[...] = x_ref[...] + vd_ref[1:2, :] * on

    return pl.pallas_call(
        body, name="mix_out", grid=(t // tb,),
        out_shape=(jax.ShapeDtypeStruct((t, dh), F32), jax.ShapeDtypeStruct((t, d), F32),
                   jax.ShapeDtypeStruct((t, d), F32)),
        in_specs=[_rows(tb, d), _rows(tb, dh, 0), _rows(tb, dh, 1), _rows(tb, dh, 2), _rows(tb, dh, 3),
                  _halo_prev(tb, dh, 2), _halo_prev(tb, dh, 3), _rows(tb, nstate), _rows(tb, nstate),
                  _full(c_re.shape), _full(c_im.shape), _full(v512.shape), _full(convw.shape), _full(glu_w.shape),
                  _full(h16.shape), _full(h64.shape), _full(w_out.shape), _full(vd.shape)],
        out_specs=(_rows(tb, dh), _rows(tb, d), _rows(tb, d)),
        compiler_params=_cparams(("parallel",), VMEM_BIG),
    )(x, proj, proj, proj, proj, proj, proj, s_re, s_im, c_re, c_im, v512, convw, glu_w, h16, h64, w_out, vd)


def _ffn_up(x2, vec, w_up_st):
    t, d = x2.shape
    ns, _, nc = w_up_st.shape
    tb = _blk(t, TB_FFN)

    def body(x_ref, vec_ref, w_ref, up_ref, h2_ref):
        xv = x_ref[...]
        r = lax.rsqrt(_rowmean(xv * xv) + EPS)
        h = xv * r * vec_ref[0:1, :] * vec_ref[1:2, :] + vec_ref[2:3, :]
        hb = h.astype(BF16)
        h2_ref[...] = hb
        for j in range(ns):
            up_ref[:, j * nc:(j + 1) * nc] = jnp.dot(hb, w_ref[j], preferred_element_type=F32)

    return pl.pallas_call(
        body, name="ffn_up", grid=(t // tb,),
        out_shape=(jax.ShapeDtypeStruct((t, ns * nc), F32), jax.ShapeDtypeStruct((t, d), BF16)),
        in_specs=[_rows(tb, d), _full((SUBLANES, d)), _resident(w_up_st.shape)],
        out_specs=(_rows(tb, ns * nc), _rows(tb, d)),
        compiler_params=_cparams(("parallel",), VMEM_BIG),
    )(x2, vec, w_up_st)


def _ffn_down(up, fw, w_down, w_down_t, x2, tgt, vd):
    t, nh = up.shape
    dff, d = w_down.shape
    tb = _blk(t, TB_FFN)
    inv_d = 1.0 / d

    def body(up_ref, uph_ref, fw_ref, wd_ref, wdt_ref, x2_ref, tgt_ref, vd_ref,
             act_ref, ddn_ref, dout_ref, dhid_ref, vec_ref, loss_ref, a_s, vv_s, sg_s):
        i = pl.program_id(0)

        def conv_cols(sl):
            x = up_ref[:, sl]
            halo = jnp.where(i > 0, uph_ref[:, sl], 0.0)
            return (fw_ref[0:1, sl] * _shift_down(x, halo, 2) + fw_ref[1:2, sl] * _shift_down(x, halo, 1)
                    + fw_ref[2:3, sl] * x)

        dn = None
        for o in range(0, dff, CW_FFN):
            sl = slice(o, o + CW_FFN)
            a = conv_cols(sl)
            vv = conv_cols(slice(dff + o, dff + o + CW_FFN))
            sg = _sigmoid(a)
            a_s[:, sl] = a
            vv_s[:, sl] = vv
            sg_s[:, sl] = sg
            actb = (a * sg * vv).astype(BF16)
            act_ref[:, sl] = actb
            pj = lax.dot_general(actb, wdt_ref[:, sl], (((1,), (1,)), ((), ())), preferred_element_type=F32)
            dn = pj if dn is None else dn + pj
        r3 = lax.rsqrt(_rowmean(dn * dn) + EPS)
        xn = dn * r3
        g = vd_ref[0:1, :]
        gt2 = vd_ref[1:2, :]
        dnn = xn * g
        diff = x2_ref[...] + gt2 * dnn - tgt_ref[...]
        part = 0.5 * inv_d * jnp.sum(diff * diff)

        @pl.when(i == 0)
        def _():
            loss_ref[...] = jnp.zeros(loss_ref.shape, F32)
        loss_ref[...] += part
        dout = diff * inv_d
        dout_ref[...] = dout
        ddnn = dout * gt2
        _acc_rows(vec_ref, i == 0, [_colsum(dout * dnn), _colsum(ddnn * xn)])
        dxn = ddnn * g
        ddn = r3 * (dxn - xn * _rowmean(dxn * xn))
        ddnb = ddn.astype(BF16)
        ddn_ref[...] = ddnb
        for o in range(0, dff, CW_FFN):
            sl = slice(o, o + CW_FFN)
            dact = lax.dot_general(ddnb, wd_ref[sl, :], (((1,), (1,)), ((), ())), preferred_element_type=F32)
            a, vv, sg = a_s[:, sl], vv_s[:, sl], sg_s[:, sl]
            dhid_ref[:, sl] = (dact * vv * sg * (1.0 + a * (1.0 - sg))).astype(BF16)
            dhid_ref[:, dff + o:dff + o + CW_FFN] = (dact * (a * sg)).astype(BF16)

    return pl.pallas_call(
        body, name="ffn_down", grid=(t // tb,),
        scratch_shapes=[pltpu.VMEM((tb, dff), F32)] * 3,
        out_shape=(jax.ShapeDtypeStruct((t, dff), BF16), jax.ShapeDtypeStruct((t, d), BF16),
                   jax.ShapeDtypeStruct((t, d), F32), jax.ShapeDtypeStruct((t, nh), BF16),
                   jax.ShapeDtypeStruct((SUBLANES, d), F32), jax.ShapeDtypeStruct((SUBLANES, 128), F32)),
        in_specs=[_rows(tb, nh), _halo_prev(tb, nh), _full(fw.shape), _resident(w_down.shape),
                  _resident(w_down_t.shape), _rows(tb, d),
                  _rows(tb, d), _full(vd.shape)],
        out_specs=(_rows(tb, dff), _rows(tb, d), _rows(tb, d), _rows(tb, nh), _full((SUBLANES, d)),
                   _full((SUBLANES, 128))),
        compiler_params=_cparams(("arbitrary",), VMEM_BIG),
    )(up, up, fw, w_down, w_down_t, x2, tgt, vd)


def _ffn_up_bwd(dhid, up, fw, x2, dout, vec, w_up_st):
    t, nh = dhid.shape
    d = x2.shape[1]
    ns, _, nc = w_up_st.shape
    tb = _blk(t, TB_FFN)
    nblk = t // tb

    def body(dh_ref, dhn_ref, up_ref, fw_ref, x2_ref, dout_ref, vec_ref, w_ref,
             dx2_ref, dup_ref, vp_ref, df_ref):
        i = pl.program_id(0)

        @pl.when(i == 0)
        def _():
            df_ref[...] = jnp.zeros(df_ref.shape, F32)
        for o in range(0, nh, CW_FFN):
            sl = slice(o, o + CW_FFN)
            dh = dh_ref[:, sl].astype(F32)
            dhn = jnp.where(i < nblk - 1, dhn_ref[:, sl].astype(F32), 0.0)
            dh1 = _shift_up(dh, dhn, 1)
            dh2 = _shift_up(dh, dhn, 2)
            dup_ref[:, sl] = (fw_ref[2:3, sl] * dh + fw_ref[1:2, sl] * dh1 + fw_ref[0:1, sl] * dh2).astype(BF16)
            up_v = up_ref[:, sl]
            df_ref[0:1, sl] += _colsum(dh2 * up_v)
            df_ref[1:2, sl] += _colsum(dh1 * up_v)
            df_ref[2:3, sl] += _colsum(dh * up_v)
        dh2 = None
        for j in range(ns):
            pj = lax.dot_general(dup_ref[:, j * nc:(j + 1) * nc], w_ref[j], (((1,), (1,)), ((), ())),
                                 preferred_element_type=F32)
            dh2 = pj if dh2 is None else dh2 + pj
        xv = x2_ref[...]
        r = lax.rsqrt(_rowmean(xv * xv) + EPS)
        xn = xv * r
        g = vec_ref[0:1, :]
        hg = xn * g
        dhg = dh2 * vec_ref[1:2, :]
        _acc_rows(vp_ref, i == 0, [_colsum(dh2), _colsum(dh2 * hg), _colsum(dhg * xn)])
        dxn = dhg * g
        dx2_ref[...] = dout_ref[...] + r * (dxn - xn * _rowmean(dxn * xn))

    return pl.pallas_call(
        body, name="ffn_up_bwd", grid=(nblk,),
        out_shape=(jax.ShapeDtypeStruct((t, d), F32), jax.ShapeDtypeStruct((t, nh), BF16),
                   jax.ShapeDtypeStruct((SUBLANES, d), F32), jax.ShapeDtypeStruct((SUBLANES, nh), F32)),
        in_specs=[_rows(tb, nh), _halo_next(tb, nh, t, rows=BF16_ROWS), _rows(tb, nh), _full(fw.shape),
                  _rows(tb, d), _rows(tb, d), _full(vec.shape), _resident(w_up_st.shape)],
        out_specs=(_rows(tb, d), _rows(tb, nh), _full((SUBLANES, d)), _full((SUBLANES, nh))),
        compiler_params=_cparams(("arbitrary",), VMEM_BIG),
    )(dhid, dhid, up, fw, x2, dout, vec, w_up_st)


def _mix_out_bwd(dx2, o, y1, proj, c_re, c_im, v512, convw, glu_w, h16, h64, w_out, vd):
    t, d = dx2.shape
    dh = y1.shape[1]
    nstate = c_re.shape[0]
    du, ds = dh // SSM_SPLIT, nstate // SSM_SPLIT
    tb = _blk(t, TB_MIX)

    def body(dx2_ref, o_ref, y1_ref, u_ref, bg_ref, cg_ref, v_ref, cgh_ref, vh_ref, cre_ref, cim_ref, p_ref,
             cw_ref, gw_ref, h16_ref, h64_ref, wo_ref, vd_ref,
             do_ref, ycat_ref, z_ref, dq_ref, dy1_ref, gr_ref, gi_ref, dcc_ref, dbg_ref, vpd_ref, vp5_ref):
        i = pl.program_id(0)
        first = i == 0
        ov = o_ref[...]
        ro = lax.rsqrt(_rowmean(ov * ov) + EPS)
        on_ = ov * ro
        g = vd_ref[0:1, :]
        dx2v = dx2_ref[...]
        don = dx2v * vd_ref[1:2, :]
        _acc_rows(vpd_ref, first, [_colsum(dx2v * on_ * g), _colsum(don * on_)])
        dxn = don * g
        dob = (ro * (dxn - on_ * _rowmean(dxn * on_))).astype(BF16)
        do_ref[...] = dob
        dyc_a = lax.dot_general(dob, wo_ref[0:dh, :], (((1,), (1,)), ((), ())), preferred_element_type=F32)
        dyc_b = lax.dot_general(dob, wo_ref[dh:2 * dh, :], (((1,), (1,)), ((), ())), preferred_element_type=F32)
        y1v = y1_ref[...]
        u = u_ref[...]
        z = _gelu(y1v)
        zb = z.astype(BF16)
        z_ref[...] = zb
        sg = _sigmoid(jnp.dot(zb, gw_ref[...], preferred_element_type=F32) + p_ref[1:2, :])
        ya = z * sg
        ra = lax.rsqrt(_head_ms(ya, h16_ref) + EPS)
        yan = ya * ra
        ga = p_ref[2:3, :]
        ycat_ref[:, 0:dh] = (yan * ga).astype(BF16)
        dyn = dyc_a * ga
        dya = ra * (dyn - yan * _split_dot(dyn * yan, h16_ref[...]))
        dq = dya * z * sg * (1.0 - sg)
        dqb = dq.astype(BF16)
        dq_ref[...] = dqb
        dz = dya * sg + lax.dot_general(dqb, gw_ref[...], (((1,), (1,)), ((), ())), preferred_element_type=F32)
        dy1 = dz * _gelu_grad(y1v)
        dy1_ref[...] = dy1
        dy1b = dy1.astype(BF16)
        for q in range(SSM_SPLIT):
            rq, cq = slice(q * ds, (q + 1) * ds), slice(q * du, (q + 1) * du)
            gr_ref[:, rq] = lax.dot_general(dy1b[:, cq], cre_ref[rq, cq], (((1,), (1,)), ((), ())),
                                            preferred_element_type=F32)
            gi_ref[:, rq] = -lax.dot_general(dy1b[:, cq], cim_ref[rq, cq], (((1,), (1,)), ((), ())),
                                             preferred_element_type=F32)
        bg = bg_ref[...]
        cv = cg_ref[...] * v_ref[...]
        cvh = jnp.where(i > 0, cgh_ref[...] * vh_ref[...], 0.0)
        cv1 = _shift_down(cv, cvh, 1)
        cv2 = _shift_down(cv, cvh, 2)
        cc = cw_ref[0:1, :] * cv2 + cw_ref[1:2, :] * cv1 + cw_ref[2:3, :] * cv
        yb = bg * cc
        rb = lax.rsqrt(_head_ms(yb, h64_ref) + EPS)
        ybn = yb * rb
        gb = p_ref[3:4, :]
        ycat_ref[:, dh:2 * dh] = (ybn * gb).astype(BF16)
        dynb = dyc_b * gb
        dyb = rb * (dynb - ybn * _split_dot(dynb * ybn, h64_ref[...]))
        dcc = dyb * bg
        dbg_ref[...] = dyb * cc
        dcc_ref[...] = dcc
        _acc_rows(vp5_ref, first, [_colsum(dyc_a * yan), _colsum(dyc_b * ybn), _colsum(dq), _colsum(dy1 * u),
                                   _colsum(dcc * cv2), _colsum(dcc * cv1), _colsum(dcc * cv)])

    return pl.pallas_call(
        body, name="mix_out_bwd", grid=(t // tb,),
        out_shape=(jax.ShapeDtypeStruct((t, d), BF16), jax.ShapeDtypeStruct((t, 2 * dh), BF16),
                   jax.ShapeDtypeStruct((t, dh), BF16), jax.ShapeDtypeStruct((t, dh), BF16),
                   jax.ShapeDtypeStruct((t, dh), F32), jax.ShapeDtypeStruct((t, nstate), F32),
                   jax.ShapeDtypeStruct((t, nstate), F32), jax.ShapeDtypeStruct((t, dh), F32),
                   jax.ShapeDtypeStruct((t, dh), F32), jax.ShapeDtypeStruct((SUBLANES, d), F32),
                   jax.ShapeDtypeStruct((SUBLANES, dh), F32)),
        in_specs=[_rows(tb, d), _rows(tb, d), _rows(tb, dh), _rows(tb, dh, 0), _rows(tb, dh, 1), _rows(tb, dh, 2),
                  _rows(tb, dh, 3), _halo_prev(tb, dh, 2), _halo_prev(tb, dh, 3), _full(c_re.shape), _full(c_im.shape),
                  _full(v512.shape), _full(convw.shape), _full(glu_w.shape), _full(h16.shape), _full(h64.shape),
                  _full(w_out.shape), _full(vd.shape)],
        out_specs=(_rows(tb, d), _rows(tb, 2 * dh), _rows(tb, dh), _rows(tb, dh), _rows(tb, dh), _rows(tb, nstate),
                   _rows(tb, nstate), _rows(tb, dh), _rows(tb, dh), _full((SUBLANES, d)), _full((SUBLANES, dh))),
        compiler_params=_cparams(("arbitrary",), VMEM_BIG),
    )(dx2, o, y1, proj, proj, proj, proj, proj, proj, c_re, c_im, v512, convw, glu_w, h16, h64, w_out, vd)


def _mix_in_bwd(gt_re, gt_im, b_re, b_im, dy1, dcc, dbg, proj, x, dx2, vec, v512, convw, w_in_st):
    t, d = x.shape
    dh = dy1.shape[1]
    nstate = gt_re.shape[1]
    du_w, ds = dh // SSM_SPLIT, nstate // SSM_SPLIT
    ns, _, nc = w_in_st.shape
    tb = _blk(t, TB_MIX)
    nblk = t // tb

    def body(gr_ref, gi_ref, bre_ref, bim_ref, dy1_ref, dcc_ref, dccn_ref, dbg_ref, cg_ref, v_ref, x_ref, dx2_ref,
             vec_ref, p_ref, cw_ref, w_ref, gx_ref, dproj_ref, vp_ref):
        i = pl.program_id(0)
        du = []
        for q in range(SSM_SPLIT):
            rq, cq = slice(q * du_w, (q + 1) * du_w), slice(q * ds, (q + 1) * ds)
            du.append(lax.dot_general(gr_ref[:, cq].astype(BF16), bre_ref[rq, cq], (((1,), (1,)), ((), ())),
                                      preferred_element_type=F32)
                      + lax.dot_general(gi_ref[:, cq].astype(BF16), bim_ref[rq, cq], (((1,), (1,)), ((), ())),
                                        preferred_element_type=F32))
        du = dy1_ref[...] * p_ref[0:1, :] + jnp.concatenate(du, axis=1)
        dcc = dcc_ref[...]
        dccn = jnp.where(i < nblk - 1, dccn_ref[...], 0.0)
        dcv = (cw_ref[2:3, :] * dcc + cw_ref[1:2, :] * _shift_up(dcc, dccn, 1)
               + cw_ref[0:1, :] * _shift_up(dcc, dccn, 2))
        parts = [du, dbg_ref[...], dcv * v_ref[...], dcv * cg_ref[...]]
        dh1 = None
        for j in range(ns):
            pb = parts[j].astype(BF16)
            dproj_ref[:, j * nc:(j + 1) * nc] = pb
            pj = lax.dot_general(pb, w_ref[j], (((1,), (1,)), ((), ())), preferred_element_type=F32)
            dh1 = pj if dh1 is None else dh1 + pj
        xv = x_ref[...]
        r = lax.rsqrt(_rowmean(xv * xv) + EPS)
        xn = xv * r
        g = vec_ref[0:1, :]
        hg = xn * g
        dhg = dh1 * vec_ref[1:2, :]
        _acc_rows(vp_ref, i == 0, [_colsum(dh1), _colsum(dh1 * hg), _colsum(dhg * xn)])
        dxn = dhg * g
        gx_ref[...] = dx2_ref[...] + r * (dxn - xn * _rowmean(dxn * xn))

    assert nc == dh and ns == 4
    return pl.pallas_call(
        body, name="mix_in_bwd", grid=(nblk,),
        out_shape=(jax.ShapeDtypeStruct((t, d), F32), jax.ShapeDtypeStruct((t, ns * nc), BF16),
                   jax.ShapeDtypeStruct((SUBLANES, d), F32)),
        in_specs=[_rows(tb, nstate), _rows(tb, nstate), _full(b_re.shape), _full(b_im.shape), _rows(tb, dh),
                  _rows(tb, dh), _halo_next(tb, dh, t), _rows(tb, dh), _rows(tb, dh, 2), _rows(tb, dh, 3),
                  _rows(tb, d), _rows(tb, d), _full(vec.shape), _full(v512.shape), _full(convw.shape),
                  _full(w_in_st.shape)],
        out_specs=(_rows(tb, d), _rows(tb, ns * nc), _full((SUBLANES, d))),
        compiler_params=_cparams(("arbitrary",), VMEM_BIG),
    )(gt_re, gt_im, b_re, b_im, dy1, dcc, dcc, dbg, proj, proj, x, dx2, vec, v512, convw, w_in_st)


def _matmul_tn(a, b, m, bn, out_dtype, name, diag=False, bt=TB_TN, after=None):
    t = a.shape[0]
    n = b.shape[1]
    bt = _blk(t, bt)
    nk = t // bt
    extra = [] if after is None else [after]
    a_map = (lambda j, k: (k, j)) if diag else (lambda j, k: (k, 0))

    def body(a_ref, b_ref, *rest):
        o_ref, acc_ref = rest[-2:]
        k = pl.program_id(1)

        @pl.when(k == 0)
        def _():
            acc_ref[...] = jnp.zeros(acc_ref.shape, F32)
        acc_ref[...] += _dot_tn(a_ref[...], b_ref[...])

        @pl.when(k == nk - 1)
        def _():
            o_ref[...] = acc_ref[...].astype(out_dtype)

    return pl.pallas_call(
        body, name=name, grid=(n // bn, nk),
        out_shape=jax.ShapeDtypeStruct((n // bn, m, bn), out_dtype),
        in_specs=[pl.BlockSpec((bt, m), a_map), pl.BlockSpec((bt, bn), lambda j, k: (k, j))]
        + [pl.BlockSpec(memory_space=pl.ANY)] * len(extra),
        out_specs=pl.BlockSpec((None, m, bn), lambda j, k: (j, 0, 0)),
        scratch_shapes=[pltpu.VMEM((m, bn), F32)],
        compiler_params=_cparams(("parallel", "arbitrary"), VMEM_BIG),
    )(a, b, *extra)


def _ssm_bgrad(d_bre, d_bim, bt_re, bt_im, rows_in, fold):
    gh, cb = d_bre.shape
    nb = SSM_SPLIT
    rb = gh // nb
    gp = nb * cb
    p = fold.shape[1]

    def body(dr_ref, di_ref, br_ref, bi_ref, rin_ref, f_ref, dbr_ref, dbi_ref, rout_ref):
        row = lax.broadcasted_iota(jnp.int32, (rb, cb), 0)
        col = lax.broadcasted_iota(jnp.int32, (rb, cb), 1)
        mask = (row >> 4) == (col >> 6)
        gr = jnp.where(mask, dr_ref[...], 0.0)
        gi = jnp.where(mask, di_ref[...], 0.0)
        cr, ci = rin_ref[0:1, :], rin_ref[1:2, :]
        dbr_ref[...] = _split3_dot(cr * gr + ci * gi, f_ref[...])
        dbi_ref[...] = _split3_dot(cr * gi - ci * gr, f_ref[...])
        br, bi = br_ref[...], bi_ref[...]
        rout_ref[...] = jnp.zeros(rout_ref.shape, F32)
        rout_ref[0:1, :] = _colsum(br * gr + bi * gi)
        rout_ref[1:2, :] = _colsum(br * gi - bi * gr)

    dspec = pl.BlockSpec((rb, cb), lambda j: (j, 0))
    bspec = pl.BlockSpec((rb, cb), lambda j: (j, j))
    rspec = pl.BlockSpec((SUBLANES, cb), lambda j: (0, j))
    ospec = pl.BlockSpec((rb, p), lambda j: (j, 0))
    return pl.pallas_call(
        body, name="ssm_bgrad", grid=(nb,),
        out_shape=(jax.ShapeDtypeStruct((gh, p), F32), jax.ShapeDtypeStruct((gh, p), F32),
                   jax.ShapeDtypeStruct((SUBLANES, gp), F32)),
        in_specs=[dspec, dspec, bspec, bspec, rspec, _full(fold.shape)],
        out_specs=(ospec, ospec, rspec),
        compiler_params=_cparams(("parallel",)),
    )(d_bre, d_bim, bt_re, bt_im, rows_in, fold)


def _ssm_cgrad(d_cre, d_cim, fold):
    gp, cb = d_cre.shape
    nb = SSM_SPLIT
    rb = gp // nb
    h = fold.shape[1]

    def body(dr_ref, di_ref, f_ref, cr_ref, ci_ref):
        row = lax.broadcasted_iota(jnp.int32, (rb, cb), 0)
        col = lax.broadcasted_iota(jnp.int32, (rb, cb), 1)
        mask = (row >> 6) == (col >> 4)
        cr_ref[...] = _split3_dot(jnp.where(mask, dr_ref[...], 0.0), f_ref[...])
        ci_ref[...] = -_split3_dot(jnp.where(mask, di_ref[...], 0.0), f_ref[...])

    cspec = pl.BlockSpec((rb, cb), lambda j: (j, 0))
    ospec = pl.BlockSpec((rb, h), lambda j: (j, 0))
    return pl.pallas_call(
        body, name="ssm_cgrad", grid=(nb,),
        out_shape=(jax.ShapeDtypeStruct((gp, h), F32),) * 2,
        in_specs=[cspec, cspec, _full(fold.shape)], out_specs=(ospec, ospec),
        compiler_params=_cparams(("parallel",)),
    )(d_cre, d_cim, fold)


def _ssm_lamgrad(lam_re, lam_im, log_step, abar_re, abar_im, coef_re, coef_im, gc_re, gc_im, ga_re, ga_im):
    g, p = lam_re.shape

    def body(lr_ref, li_ref, ls_ref, ar_ref, ai_ref, cr_ref, ci_ref, gcr_ref, gci_ref, gar_ref, gai_ref,
             dlr_ref, dli_ref, dls_ref):
        lam_raw = lr_ref[...]
        lr = jnp.minimum(lam_raw, LAMBDA_RE_MAX)
        li = li_ref[...]
        st = jnp.exp(ls_ref[...])
        den = lr * lr + li * li
        gcr, gci = gcr_ref[...], gci_ref[...]
        gab_r = gar_ref[...] + (lr * gcr - li * gci) / den
        gab_i = gai_ref[...] + (lr * gci + li * gcr) / den
        cr, ci = cr_ref[...], ci_ref[...]
        wr = -(cr * lr + ci * li) / den
        wi = -(ci * lr - cr * li) / den
        gl_r = wr * gcr + wi * gci
        gl_i = wr * gci - wi * gcr
        ar, ai = ar_ref[...], ai_ref[...]
        gw_r = ar * gab_r + ai * gab_i
        gw_i = ar * gab_i - ai * gab_r
        gl_r = gl_r + st * gw_r
        gl_i = gl_i + st * gw_i
        pass_through = jnp.where(lam_raw < LAMBDA_RE_MAX, 1.0, jnp.where(lam_raw == LAMBDA_RE_MAX, 0.5, 0.0))
        dlr_ref[...] = gl_r * pass_through
        dli_ref[...] = gl_i
        dls_ref[...] = st * jnp.sum(lr * gw_r + li * gw_i, axis=1, keepdims=True)

    sds = jax.ShapeDtypeStruct((g, p), F32)
    return pl.pallas_call(body, name="ssm_lamgrad", out_shape=(sds, sds, jax.ShapeDtypeStruct((g, 1), F32)))(
        lam_re, lam_im, log_step, abar_re, abar_im, coef_re, coef_im, gc_re, gc_im, ga_re, ga_im)


def _adamw_math(w, g, m, v):
    m = ADAM_B1 * m + (1.0 - ADAM_B1) * g
    v = ADAM_B2 * v + (1.0 - ADAM_B2) * (g * g)
    m_hat = m / (1.0 - ADAM_B1 ** ADAM_STEP)
    v_hat = v / (1.0 - ADAM_B2 ** ADAM_STEP)
    delta = -ADAM_LR * (m_hat / (jnp.sqrt(v_hat) + ADAM_EPS) + ADAM_WD * w)
    return delta, m, v


def _adamw_big(p_mine, p_sib, w, m, v, name):
    r, c = w.shape
    rb = 64 if r % 64 == 0 else r

    def body(a_ref, b_ref, w_ref, m_ref, v_ref, g_ref, d_ref, mo_ref, vo_ref):
        g = a_ref[...] + b_ref[...]
        g_ref[...] = g
        d_ref[...], mo_ref[...], vo_ref[...] = _adamw_math(w_ref[...], g, m_ref[...], v_ref[...])

    spec = pl.BlockSpec((rb, c), lambda i: (i, 0))
    sds = jax.ShapeDtypeStruct((r, c), F32)
    return pl.pallas_call(
        body, name=name, grid=(r // rb,), out_shape=(sds,) * 4, in_specs=[spec] * 5, out_specs=(spec,) * 4,
        compiler_params=_cparams(("parallel",)),
    )(p_mine, p_sib, w, m, v)


def _sum_blocks(stack, name):
    n, r, c = stack.shape
    rb = 64 if r % 64 == 0 else r

    def body(s_ref, o_ref):
        acc = s_ref[0].astype(F32)
        for k in range(1, n):
            acc = acc + s_ref[k].astype(F32)
        o_ref[...] = acc

    return pl.pallas_call(
        body, name=name, grid=(r // rb,), out_shape=jax.ShapeDtypeStruct((r, c), F32),
        in_specs=[pl.BlockSpec((n, rb, c), lambda i: (0, i, 0))], out_specs=pl.BlockSpec((rb, c), lambda i: (i, 0)),
        compiler_params=_cparams(("parallel",)),
    )(stack)


def _adamw_ada(c_all, dmod_cols, w, m, v):
    d, n = w.shape
    bn = 512

    def body(c_ref, dm_ref, w_ref, m_ref, v_ref, g_ref, d_ref, mo_ref, vo_ref):
        cc = c_ref[...]
        g = _dot_tn(cc * _sigmoid(cc), dm_ref[...])
        g_ref[...] = g
        d_ref[...], mo_ref[...], vo_ref[...] = _adamw_math(w_ref[...], g, m_ref[...], v_ref[...])

    spec = pl.BlockSpec((d, bn), lambda j: (0, j))
    sds = jax.ShapeDtypeStruct((d, n), F32)
    return pl.pallas_call(
        body, name="adamw_ada", grid=(n // bn,), out_shape=(sds,) * 4,
        in_specs=[_full((N_DEV, d)), pl.BlockSpec((N_DEV, bn), lambda j: (0, j)), spec, spec, spec],
        out_specs=(spec,) * 4, compiler_params=_cparams(("parallel",)),
    )(c_all, dmod_cols, w, m, v)


def _adamw_small(items):
    n = len(items)

    def body(*refs):
        ins, outs = refs[:4 * n], refs[4 * n:]
        for k in range(n):
            w_ref, g_ref, m_ref, v_ref = ins[4 * k:4 * k + 4]
            outs[3 * k][...], outs[3 * k + 1][...], outs[3 * k + 2][...] = _adamw_math(
                w_ref[...], g_ref[...], m_ref[...], v_ref[...])

    flat = [a for it in items for a in it]
    out_shape = tuple(jax.ShapeDtypeStruct(it[0].shape, F32) for it in items for _ in range(3))
    res = pl.pallas_call(body, name="adamw_small", out_shape=out_shape)(*flat)
    return [tuple(res[3 * k:3 * k + 3]) for k in range(n)]


def _group_mean_matrix(n, group):
    idx = np.arange(n) // group
    return (idx[:, None] == idx[None, :]).astype(np.float32) / group


def _fold_matrix(n, period):
    return (np.arange(n)[:, None] % period == np.arange(period)[None, :]).astype(np.float32)


def _rows8(*rows):
    c = rows[0].shape[-1]
    pad = jnp.zeros((SUBLANES - len(rows), c), F32)
    return jnp.concatenate([r.reshape(1, c) for r in rows] + [pad], axis=0)


def _to_rows(a, width):
    flat = a.reshape(-1)
    n = -(-flat.shape[0] // width)
    flat = jnp.pad(flat, (0, n * width - flat.shape[0]))
    return flat.reshape(n, width)


def kernel(x, c, w_ada, b_ada, g_pre_mix, g_post_mix, w_in, ssm_lam_re, ssm_lam_im, ssm_log_step, ssm_b_re, ssm_b_im, ssm_c_re, ssm_c_im, ssm_d, glu_w, glu_b, g_out_ssm, conv_w, g_out_conv, w_out, g_pre_ffn, g_post_ffn, w_up, ffn_conv_w, w_down, loss_target, m_w_ada, m_b_ada, m_g_pre_mix, m_g_post_mix, m_w_in, m_ssm_lam_re, m_ssm_lam_im, m_ssm_log_step, m_ssm_b_re, m_ssm_b_im, m_ssm_c_re, m_ssm_c_im, m_ssm_d, m_glu_w, m_glu_b, m_g_out_ssm, m_conv_w, m_g_out_conv, m_w_out, m_g_pre_ffn, m_g_post_ffn, m_w_up, m_ffn_conv_w, m_w_down, v_w_ada, v_b_ada, v_g_pre_mix, v_g_post_mix, v_w_in, v_ssm_lam_re, v_ssm_lam_im, v_ssm_log_step, v_ssm_b_re, v_ssm_b_im, v_ssm_c_re, v_ssm_c_im, v_ssm_d, v_glu_w, v_glu_b, v_g_out_ssm, v_conv_w, v_g_out_conv, v_w_out, v_g_pre_ffn, v_g_post_ffn, v_w_up, v_ffn_conv_w, v_w_down):
    xs = x[0]
    tgt = loss_target[0]
    t, d = xs.shape
    xi, yi, ci = lax.axis_index("x"), lax.axis_index("y"), lax.axis_index("c")
    chip = 2 * xi + yi
    dev = 2 * chip + ci

    n_groups, n_state = ssm_lam_re.shape[1:]
    n_gch = ssm_b_re.shape[3]
    d_ssm = n_groups * n_gch
    gp = n_groups * n_state
    n_ada = w_ada.shape[2]
    d_ff = w_down.shape[1] * N_CHIPS
    n_upc = w_up.shape[2]

    w_names = ("w_in", "glu_w", "w_out", "w_up", "w_down")
    c_gath, _ = _allgather8(jnp.broadcast_to(c, (SUBLANES, d)), SUBLANES, "gather_c")
    c_all = c_gath.reshape(N_DEV, SUBLANES, d)[:, 0, :]
    b_sh = lax.dynamic_slice(b_ada, (0, chip * n_ada), (1, n_ada))
    mod_sh = _mod_shard(c_all, w_ada[0], b_sh)

    def pad8(a):
        return jnp.concatenate([a, jnp.zeros((SUBLANES - a.shape[0], a.shape[1]), a.dtype)], axis=0)

    w_names = ("mod", "conv_w", "ffn_conv_w", "w_in", "glu_w", "w_out", "w_up", "w_down")
    w_own = [mod_sh, pad8(conv_w[0]), pad8(ffn_conv_w[0])]
    w_own += [w[0].astype(BF16) for w in (w_in, glu_w, w_out, w_up, w_down)]
    w_send, w_recv, w_src, w_land, _ = _chips_start("weights_start", True, w_own, [_landing(a, chip) for a in w_own])

    def weights(names, after):
        ks = [w_names.index(nm) for nm in names]
        return _chips_wait("weights_wait_" + names[-1], True, [w_send[k] for k in ks], [w_recv[k] for k in ks],
                           [w_src[k] for k in ks], [w_land[k] for k in ks], after)

    lam_re, lam_im = ssm_lam_re[0], ssm_lam_im[0]
    log_step = ssm_log_step[0].reshape(n_groups, 1)
    abar_re, abar_im, coef_re, coef_im = _ssm_prep(lam_re, lam_im, log_step)
    a_rows = _rows8(abar_re.reshape(1, gp), abar_im.reshape(1, gp))
    coef_rows = _rows8(coef_re.reshape(1, gp), coef_im.reshape(1, gp))
    bt_re = jnp.tile(ssm_b_re[0].transpose(0, 2, 1).reshape(d_ssm, n_state), (1, n_groups))
    bt_im = jnp.tile(ssm_b_im[0].transpose(0, 2, 1).reshape(d_ssm, n_state), (1, n_groups))
    ct_re = jnp.tile(ssm_c_re[0].transpose(0, 2, 1).reshape(gp, n_gch), (1, n_groups))
    ct_im = jnp.tile(ssm_c_im[0].transpose(0, 2, 1).reshape(gp, n_gch), (1, n_groups))
    bblk_re, bblk_im, cblk_re, cblk_im = _ssm_blocks(bt_re, bt_im, ct_re, ct_im, coef_rows)

    h16 = jnp.asarray(_group_mean_matrix(d_ssm, n_gch), BF16)
    h64 = jnp.asarray(_group_mean_matrix(d_ssm, CONV_HEAD_DIM), BF16)

    g_mod, g_cw, g_fw, w_in_st = weights(("mod", "conv_w", "ffn_conv_w", "w_in"), bblk_re)
    mod_all = g_mod.transpose(1, 0, 2).reshape(N_DEV, N_CHIPS * n_ada)
    mod = lax.dynamic_slice(mod_all, (dev, 0), (1, N_CHIPS * n_ada))
    sh1, sc1, gt1, sh2, sc2, gt2 = [mod[:, k * d:(k + 1) * d] for k in range(6)]
    convw_full = pad8(g_cw[:, :3, :].transpose(1, 0, 2).reshape(3, d_ssm))
    fw_full = pad8(g_fw[:, :3, :].transpose(1, 0, 2).reshape(3, N_CHIPS * n_upc))

    v512 = _rows8(ssm_d, glu_b, g_out_ssm, g_out_conv)
    vec1 =_rows8(g_pre_mix, 1.0 + sc1, sh1)
    vd1 = _rows8(g_post_mix, gt1)
    vec2 = _rows8(g_pre_ffn, 1.0 + sc2, sh2)
    vd2 = _rows8(g_post_ffn, gt2)

    proj, bu_re, bu_im, h1b = _mix_in(xs, vec1, w_in_st, bblk_re, bblk_im)
    s_re, s_im = _scan_fwd(a_rows, bu_re, bu_im)
    g_glu, g_wout = weights(("glu_w", "w_out"), s_re)
    glu_full = g_glu.reshape(d_ssm, d_ssm)
    w_out_full = g_wout.reshape(2 * d_ssm, d)
    y1, o_mix, x2 = _mix_out(xs, proj, s_re, s_im, cblk_re, cblk_im, v512, convw_full, glu_full, h16, h64,
                             w_out_full, vd1)
    (w_up_st,) = weights(("w_up",), x2)
    up, h2b = _ffn_up(x2, vec2, w_up_st)
    (g_wdown,) = weights(("w_down",), up)
    w_down_full = g_wdown.reshape(d_ff, d)
    actb, ddnb, dout, dhid, vp_dn, loss_blk = _ffn_down(up, fw_full, w_down_full, w_down_full.T, x2, tgt, vd2)

    g_names = ("w_down", "w_up", "w_out", "glu_w", "w_in")
    gw_down = _matmul_tn(actb, ddnb, d_ff, d, BF16, "dw_down", bt=1024).reshape(N_CHIPS, d_ff // N_CHIPS, d)
    dx2, dupb, vp_up, df_rows = _ffn_up_bwd(dhid, up, fw_full, x2, dout, vec2, w_up_st)
    gw_up = _matmul_tn(h2b, dupb, d, n_upc, BF16, "dw_up", bt=2048)
    ga_send, ga_recv, ga_src, ga_land, ga_token = _chips_start(
        "grads_start_ffn", False, [gw_down, gw_up],
        [_landing(lax.dynamic_index_in_dim(g, chip, 0, False), chip) for g in (gw_down, gw_up)])
    (dob, ycatb, zb, dqb, dy1, g_re, g_im, dcc, dbg, vp_mo, vp5) = _mix_out_bwd(
        dx2, o_mix, y1, proj, cblk_re, cblk_im, v512, convw_full, glu_full, h16, h64, w_out_full,
        vd1 + ga_token[0:1, 0:1])
    gt_re, gt_im, ga_re8, ga_im8 = _scan_bwd(a_rows, g_re, g_im, s_re, s_im)
    grad_x, dprojb, vp_mi = _mix_in_bwd(gt_re, gt_im, bblk_re, bblk_im, dy1, dcc, dbg, proj, xs, dx2, vec1, v512,
                                        convw_full, w_in_st)
    ssm_u, ssm_s = d_ssm // SSM_SPLIT, gp // SSM_SPLIT
    d_bre = _matmul_tn(proj, gt_re, ssm_u, ssm_s, F32, "d_bre", diag=True, bt=2048)
    d_bim = _matmul_tn(proj, gt_im, ssm_u, ssm_s, F32, "d_bim", diag=True, bt=2048)
    d_cre = _matmul_tn(s_re, dy1, ssm_s, ssm_u, F32, "d_cre", diag=True, bt=2048)
    d_cim = _matmul_tn(s_im, dy1, ssm_s, ssm_u, F32, "d_cim", diag=True, bt=2048)
    d_bre, d_bim = d_bre.reshape(d_ssm, ssm_s), d_bim.reshape(d_ssm, ssm_s)
    d_cre, d_cim = d_cre.reshape(gp, ssm_u), d_cim.reshape(gp, ssm_u)

    fold_b = jnp.asarray(_fold_matrix(ssm_s, n_state), BF16)
    fold_c = jnp.asarray(_fold_matrix(ssm_u, n_gch), BF16)
    db_re_f, db_im_f, gc_rows = _ssm_bgrad(d_bre, d_bim, bt_re, bt_im, coef_rows, fold_b)
    dc_re_f, dc_im_f = _ssm_cgrad(d_cre, d_cim, fold_c)
    ga_sum = _ga_rowsum(ga_re8, ga_im8)
    g_lam_re, g_lam_im, g_log_step = _ssm_lamgrad(
        lam_re, lam_im, log_step, abar_re, abar_im, coef_re, coef_im,
        gc_rows[0].reshape(n_groups, n_state), gc_rows[1].reshape(n_groups, n_state),
        ga_sum[0].reshape(n_groups, n_state), ga_sum[1].reshape(n_groups, n_state))
    g_b_re = db_re_f.reshape(n_groups, n_gch, n_state).transpose(0, 2, 1)
    g_b_im = db_im_f.reshape(n_groups, n_gch, n_state).transpose(0, 2, 1)
    g_c_re = dc_re_f.reshape(n_groups, n_state, n_gch).transpose(0, 2, 1)
    g_c_im = dc_im_f.reshape(n_groups, n_state, n_gch).transpose(0, 2, 1)

    dmod = jnp.concatenate([vp_mi[0:1], vp_mi[1:2], vp_mo[0:1], vp_up[0:1], vp_up[1:2], vp_dn[0:1]], axis=1)
    small = [
        ("g_pre_mix", vp_mi[2:3]), ("g_post_mix", vp_mo[1:2]), ("g_pre_ffn", vp_up[2:3]), ("g_post_ffn", vp_dn[1:2]),
        ("ssm_lam_re", g_lam_re), ("ssm_lam_im", g_lam_im), ("ssm_log_step", g_log_step),
        ("ssm_b_re", g_b_re), ("ssm_b_im", g_b_im), ("ssm_c_re", g_c_re), ("ssm_c_im", g_c_im),
        ("ssm_d", vp5[3:4]), ("glu_b", vp5[2:3]), ("g_out_ssm", vp5[0:1]), ("g_out_conv", vp5[1:2]),
        ("conv_w", vp5[4:7]), ("ffn_conv_w", df_rows[0:3]), ("loss", loss_blk[0:1, 0:1]),
    ]
    packed, offsets, row = [], {}, 0
    for name, a in small:
        r = _to_rows(a, d)
        offsets[name] = (row, a.shape)
        packed.append(r)
        row += r.shape[0]
    n_small = -(-row // SUBLANES) * SUBLANES
    packed.append(jnp.zeros((n_small - row, d), F32))
    packed.append(pad8(dmod.reshape(6, d)))
    pack = jnp.concatenate(packed, axis=0)
    gath, sums = _allgather8(pack, n_small + SUBLANES, "reduce_small")
    dmod_all = gath.reshape(N_DEV, n_small + SUBLANES, d)[:, n_small:n_small + 6, :].reshape(N_DEV, 6 * d)
    g_b_ada = sums[n_small:n_small + 6].reshape(1, 6 * d)

    def unpack(name):
        r0, shape = offsets[name]
        size = math.prod(shape)
        nrow = -(-size // d)
        return sums[r0:r0 + nrow].reshape(-1)[:size].reshape(shape)

    gw_out = _matmul_tn(ycatb, dob, 2 * d_ssm, d, BF16, "dw_out", bt=2048, after=sums)
    gw_out = gw_out.reshape(N_CHIPS, 2 * d_ssm // N_CHIPS, d)
    gw_glu = _matmul_tn(zb, dqb, d_ssm, d_ssm, BF16, "dw_glu", bt=2048).reshape(N_CHIPS, d_ssm // N_CHIPS, d_ssm)
    gw_in = _matmul_tn(h1b, dprojb, d, w_in.shape[2], BF16, "dw_in", bt=2048)
    gb_send, gb_recv, gb_src, gb_land, _ = _chips_start(
        "grads_start_mix", False, [gw_out, gw_glu, gw_in],
        [_landing(lax.dynamic_index_in_dim(g, chip, 0, False), chip) for g in (gw_out, gw_glu, gw_in)])

    dmod_cols = lax.dynamic_slice(dmod_all, (0, chip * n_ada), (N_DEV, n_ada))
    ada = _adamw_ada(c_all, dmod_cols, w_ada[0], m_w_ada[0], v_w_ada[0])

    def finish(names, landed):
        partial = [_sum_blocks(s, "sum_" + nm) for s, nm in zip(landed, names)]
        theirs = _swap_sibling(partial, "swap_" + names[0])
        done = {}
        for nm, pm, ps in zip(names, partial, theirs):
            w_, m_, v_ = big_params[nm]
            done[nm] = _adamw_big(pm, ps, w_[0], m_[0], v_[0], "adamw_" + nm)
        return done

    big_params = {"w_down": (w_down, m_w_down, v_w_down), "w_up": (w_up, m_w_up, v_w_up),
                  "w_out": (w_out, m_w_out, v_w_out), "glu_w": (glu_w, m_glu_w, v_glu_w),
                  "w_in": (w_in, m_w_in, v_w_in)}
    big = finish(("w_down", "w_up"), _chips_wait("grads_wait_ffn", False, ga_send, ga_recv, ga_src, ga_land, ada[0]))
    big.update(finish(("w_out", "glu_w", "w_in"),
                      _chips_wait("grads_wait_mix", False, gb_send, gb_recv, gb_src, gb_land, big["w_up"][0])))

    g_small = {name: unpack(name) for name, _ in small}
    g_small["b_ada"] = g_b_ada
    g_small["conv_w"] = lax.dynamic_slice(g_small["conv_w"], (0, chip * conv_w.shape[2]), (3, conv_w.shape[2]))
    g_small["ffn_conv_w"] = lax.dynamic_slice(g_small["ffn_conv_w"], (0, chip * n_upc), (3, n_upc))
    g_small["ssm_log_step"] = g_small["ssm_log_step"].reshape(1, n_groups)
    small_params = {
        "b_ada": (b_ada, m_b_ada, v_b_ada), "g_pre_mix": (g_pre_mix, m_g_pre_mix, v_g_pre_mix),
        "g_post_mix": (g_post_mix, m_g_post_mix, v_g_post_mix), "ssm_lam_re": (ssm_lam_re, m_ssm_lam_re, v_ssm_lam_re),
        "ssm_lam_im": (ssm_lam_im, m_ssm_lam_im, v_ssm_lam_im),
        "ssm_log_step": (ssm_log_step, m_ssm_log_step, v_ssm_log_step),
        "ssm_b_re": (ssm_b_re, m_ssm_b_re, v_ssm_b_re), "ssm_b_im": (ssm_b_im, m_ssm_b_im, v_ssm_b_im),
        "ssm_c_re": (ssm_c_re, m_ssm_c_re, v_ssm_c_re), "ssm_c_im": (ssm_c_im, m_ssm_c_im, v_ssm_c_im),
        "ssm_d": (ssm_d, m_ssm_d, v_ssm_d), "glu_b": (glu_b, m_glu_b, v_glu_b),
        "g_out_ssm": (g_out_ssm, m_g_out_ssm, v_g_out_ssm), "conv_w": (conv_w, m_conv_w, v_conv_w),
        "g_out_conv": (g_out_conv, m_g_out_conv, v_g_out_conv), "g_pre_ffn": (g_pre_ffn, m_g_pre_ffn, v_g_pre_ffn),
        "g_post_ffn": (g_post_ffn, m_g_post_ffn, v_g_post_ffn),
        "ffn_conv_w": (ffn_conv_w, m_ffn_conv_w, v_ffn_conv_w),
    }

    def flat2d(a):
        n = a.size
        return a.reshape(n // 1024, 1024) if n % 1024 == 0 and n > 1024 else a.reshape(-1, a.shape[-1])

    names = list(small_params)
    items = []
    for nm in names:
        w_, m_, v_ = small_params[nm]
        items.append((flat2d(w_[0]), flat2d(g_small[nm].reshape(w_[0].shape)), flat2d(m_[0]), flat2d(v_[0])))
    upd = _adamw_small(items)
    small_out = {}
    for nm, (dl, mo, vo) in zip(names, upd):
        shp = small_params[nm][0].shape
        small_out[nm] = (g_small[nm].reshape(shp), dl.reshape(shp), mo.reshape(shp), vo.reshape(shp))

    loss = g_small["loss"][0, 0]

    order = ["w_ada", "b_ada", "g_pre_mix", "g_post_mix", "w_in", "ssm_lam_re", "ssm_lam_im", "ssm_log_step",
             "ssm_b_re", "ssm_b_im", "ssm_c_re", "ssm_c_im", "ssm_d", "glu_w", "glu_b", "g_out_ssm", "conv_w",
             "g_out_conv", "w_out", "g_pre_ffn", "g_post_ffn", "w_up", "ffn_conv_w", "w_down"]
    results = {"w_ada": tuple(a[None] for a in ada)}
    for nm in big:
        results[nm] = tuple(a[None] for a in big[nm])
    results.update(small_out)
    outs = [loss, grad_x[None]]
    for k in range(4):
        outs += [results[nm][k] for nm in order]
    return tuple(outs)


def _ga_rowsum(ga_re8, ga_im8):
    n = ga_re8.shape[1]

    def body(r_ref, i_ref, o_ref):
        o_ref[...] = jnp.zeros(o_ref.shape, F32)
        o_ref[0:1, :] = _colsum(r_ref[...])
        o_ref[1:2, :] = _colsum(i_ref[...])

    return pl.pallas_call(body, name="ga_rowsum", out_shape=jax.ShapeDtypeStruct((SUBLANES, n), F32))(ga_re8, ga_im8)
```

```python
import functools
import math

import jax
import jax.numpy as jnp
import numpy as np
from jax import lax
from jax.experimental import pallas as pl
from jax.experimental.pallas import tpu as pltpu

F32 = jnp.float32
BF16 = jnp.bfloat16
MESH = pl.DeviceIdType.MESH

EPS = 1e-6
LAMBDA_RE_MAX = -1e-4
ADAM_LR = 0.001
ADAM_B1 = 0.9
ADAM_B2 = 0.999
ADAM_EPS = 1e-08
ADAM_WD = 0.01
ADAM_STEP = 10

SUBLANES = 8
BF16_ROWS = 16
N_CHIPS = 4
N_DEV = 8
CONV_HEAD_DIM = 64
VMEM_BIG = 56 * 1024 * 1024
VMEM_MID = 40 * 1024 * 1024

TB_MIX = 256
TB_FFN = 256
TB_SCAN = 1024
W_SCAN = 256
SSM_SPLIT = 4
CW_FFN = 256
SCAN_UNROLL = 4
TB_TN = 512


def _cparams(sem=None, vmem=None):
    kw = {}
    if sem is not None:
        kw["dimension_semantics"] = sem
    if vmem is not None:
        kw["vmem_limit_bytes"] = vmem
    return pltpu.CompilerParams(**kw)


def _blk(t, pref):
    return pref if t % pref == 0 else t


def _dot(a, b):
    return jnp.dot(a.astype(BF16), b.astype(BF16), preferred_element_type=F32)


def _dot_nt(a, b):
    return lax.dot_general(a.astype(BF16), b.astype(BF16), (((1,), (1,)), ((), ())),
                           preferred_element_type=F32)


def _dot_tn(a, b):
    return lax.dot_general(a.astype(BF16), b.astype(BF16), (((0,), (0,)), ((), ())),
                           preferred_element_type=F32)


def _sigmoid(x):
    return 0.5 * jnp.tanh(0.5 * x) + 0.5


_GELU_K = math.sqrt(2.0 / math.pi)
_GELU_C = 0.044715


def _gelu(x):
    th = jnp.tanh(_GELU_K * (x + _GELU_C * x * x * x))
    return 0.5 * x * (1.0 + th)


def _gelu_grad(x):
    x2 = x * x
    th = jnp.tanh(_GELU_K * (x + _GELU_C * x2 * x))
    return 0.5 * (1.0 + th) + 0.5 * x * (1.0 - th * th) * _GELU_K * (1.0 + 3.0 * _GELU_C * x2)


def _rowmean(x):
    return jnp.mean(x, axis=-1, keepdims=True)


def _colsum(x):
    return jnp.sum(x, axis=0, keepdims=True)


def _split_dot(x, m):
    hi = x.astype(BF16)
    lo = (x - hi.astype(F32)).astype(BF16)
    return (jnp.dot(hi, m, preferred_element_type=F32) + jnp.dot(lo, m, preferred_element_type=F32))


def _split3_dot(x, m):
    hi = x.astype(BF16)
    r1 = x - hi.astype(F32)
    mid = r1.astype(BF16)
    lo = (r1 - mid.astype(F32)).astype(BF16)
    return (jnp.dot(hi, m, preferred_element_type=F32) + jnp.dot(mid, m, preferred_element_type=F32)
            + jnp.dot(lo, m, preferred_element_type=F32))


def _shift_down(x, halo, k):
    r = pltpu.roll(x, k, 0)
    row = lax.broadcasted_iota(jnp.int32, x.shape, 0)
    for j in range(k):
        r = jnp.where(row == j, halo[SUBLANES - k + j:SUBLANES - k + j + 1, :], r)
    return r


def _shift_up(x, halo, k):
    n = x.shape[0]
    r = pltpu.roll(x, n - k, 0)
    row = lax.broadcasted_iota(jnp.int32, x.shape, 0)
    for j in range(k):
        r = jnp.where(row == n - k + j, halo[j:j + 1, :], r)
    return r


def _acc_rows(ref, first, rows):
    @pl.when(first)
    def _():
        ref[...] = jnp.zeros(ref.shape, ref.dtype)
    for j, r in enumerate(rows):
        ref[j:j + 1, :] += r


def _rows(tb, c, col=0):
    return pl.BlockSpec((tb, c), lambda i, col=col: (i, col))


def _full(shape):
    nd = len(shape)
    return pl.BlockSpec(shape, lambda i, nd=nd: (0,) * nd)


def _resident(shape):
    nd = len(shape)
    return pl.BlockSpec(shape, lambda i, nd=nd: (0,) * nd, pipeline_mode=pl.Buffered(1))


def _halo_prev(tb, c, col=0):
    per = tb // SUBLANES
    return pl.BlockSpec((SUBLANES, c), lambda i, col=col: (jnp.maximum(i * per - 1, 0), col))


def _halo_next(tb, c, t, col=0, rows=SUBLANES):
    per = tb // rows
    last = t // rows - 1
    return pl.BlockSpec((rows, c), lambda i, col=col: (jnp.minimum((i + 1) * per, last), col))


def _mesh_pos():
    return lax.axis_index("x"), lax.axis_index("y"), lax.axis_index("c")


def _allgather8(x_pad, n_sum, name):
    m_per, n = x_pad.shape

    def body(x_ref, out_ref, sum_ref, send_sems, recv_sems, local_sem):
        x, y, c = _mesh_pos()
        me, sibling = (x, y, c), (x, y, 1 - c)
        chips = [(1 - x, y), (x, 1 - y), (1 - x, 1 - y)]

        def rows(px, py, pc):
            return out_ref.at[pl.ds((4 * px + 2 * py + pc) * m_per, m_per), :]

        def copy(k, block, to, src=None):
            return pltpu.make_async_remote_copy(
                src_ref=rows(*block) if src is None else src, dst_ref=rows(*block),
                send_sem=send_sems.at[k], recv_sem=recv_sems.at[k], device_id=to, device_id_type=MESH)

        mine = pltpu.make_async_copy(x_ref, rows(*me), local_sem)
        mine.start()
        first = [copy(0, me, sibling, src=x_ref)]
        first += [copy(1 + j, me, (*chip, c), src=x_ref) for j, chip in enumerate(chips)]
        for cp in first:
            cp.start()
        passed = [copy(4 + j, (*chip, c), sibling) for j, chip in enumerate(chips)]
        for j, chip in enumerate(chips):
            copy(1 + j, (*chip, c), me).wait_recv()
            passed[j].start()
        copy(0, sibling, me).wait_recv()
        for j, chip in enumerate(chips):
            copy(4 + j, (*chip, 1 - c), me).wait_recv()
        for cp in first + passed:
            cp.wait_send()
        mine.wait()
        acc = out_ref[0:n_sum, :]
        for k in range(1, N_DEV):
            acc = acc + out_ref[k * m_per:k * m_per + n_sum, :]
        sum_ref[...] = acc

    return pl.pallas_call(
        body, name=name,
        out_shape=(jax.ShapeDtypeStruct((N_DEV * m_per, n), F32), jax.ShapeDtypeStruct((n_sum, n), F32)),
        in_specs=[pl.BlockSpec(memory_space=pltpu.VMEM)],
        out_specs=(pl.BlockSpec(memory_space=pltpu.VMEM), pl.BlockSpec(memory_space=pltpu.VMEM)),
        scratch_shapes=[pltpu.SemaphoreType.DMA((7,)), pltpu.SemaphoreType.DMA((7,)), pltpu.SemaphoreType.DMA],
        compiler_params=_cparams(vmem=VMEM_MID),
    )(x_pad)


_HBM = pl.BlockSpec(memory_space=pltpu.HBM)
_SEM = pl.BlockSpec(memory_space=pltpu.SEMAPHORE)
_EFFECT = pltpu.SideEffectType.DATAFLOW_SIDE_EFFECTING


def _chip_copy(gather, src_ref, land_ref, send, recv, j, arrival):
    x, y, c = _mesh_pos()
    peer = [(1 - x, y), (x, 1 - y), (1 - x, 1 - y)][j]
    peer_chip = 2 * peer[0] + peer[1]
    my_chip = 2 * x + y
    return pltpu.make_async_remote_copy(
        src_ref=src_ref if gather else src_ref.at[peer_chip],
        dst_ref=land_ref.at[peer_chip if arrival else my_chip],
        send_sem=send.at[j], recv_sem=recv.at[j], device_id=(*peer, c), device_id_type=MESH)


def _chips_start(name, gather, srcs, lands):
    n = len(srcs)

    def body(*refs):
        src_refs, land_refs = refs[:n], refs[n:2 * n]
        outs = refs[2 * n:]
        sends, recvs, token = outs[:n], outs[n:2 * n], outs[-1]
        for k in range(n):
            for j in range(3):
                _chip_copy(gather, src_refs[k], land_refs[k], sends[k], recvs[k], j, False).start()
        token[...] = jnp.zeros(token.shape, F32)

    sem = pltpu.SemaphoreType.DMA((3,))
    thru = tuple(pltpu.HBM(a.shape, a.dtype) for a in list(srcs) + list(lands))
    res = pl.pallas_call(
        body, name=name,
        out_shape=(sem,) * (2 * n) + thru + (jax.ShapeDtypeStruct((SUBLANES, 128), F32),),
        in_specs=[_HBM] * (2 * n),
        out_specs=(_SEM,) * (2 * n) + (_HBM,) * (2 * n) + (pl.BlockSpec(memory_space=pltpu.VMEM),),
        input_output_aliases={k: 2 * n + k for k in range(2 * n)},
        compiler_params=pltpu.CompilerParams(has_side_effects=_EFFECT),
    )(*[pltpu.with_memory_space_constraint(a, pltpu.HBM) for a in list(srcs) + list(lands)])
    return res[:n], res[n:2 * n], res[2 * n:3 * n], res[3 * n:4 * n], res[-1]


def _chips_wait(name, gather, sends, recvs, srcs, lands, after):
    n = len(srcs)

    def body(*refs):
        src_refs, land_refs = refs[:n], refs[n:2 * n]
        sends_, recvs_ = refs[2 * n:3 * n], refs[3 * n:4 * n]
        for k in range(n):
            for j in range(3):
                cp = _chip_copy(gather, src_refs[k], land_refs[k], sends_[k], recvs_[k], j, True)
                cp.wait_send()
                cp.wait_recv()

    thru = tuple(pltpu.HBM(a.shape, a.dtype) for a in list(srcs) + list(lands))
    res = pl.pallas_call(
        body, name=name, out_shape=thru,
        in_specs=[_HBM] * (2 * n) + [_SEM] * (2 * n) + [pl.BlockSpec(memory_space=pl.ANY)],
        out_specs=(_HBM,) * (2 * n),
        input_output_aliases={k: k for k in range(2 * n)},
        compiler_params=pltpu.CompilerParams(has_side_effects=_EFFECT),
    )(*srcs, *lands, *sends, *recvs, after)
    return res[n:]


def _landing(own, chip):
    zone = lax.empty((N_CHIPS,) + own.shape, own.dtype)
    return lax.dynamic_update_slice(zone, own[None], (chip,) + (0,) * own.ndim)


def _swap_sibling(arrs, name):
    n_arr = len(arrs)

    def body(*refs):
        ins, outs = refs[:n_arr], refs[n_arr:2 * n_arr]
        send_sems, recv_sems = refs[2 * n_arr:]
        x, y, c = _mesh_pos()
        copies = [pltpu.make_async_remote_copy(
            src_ref=ins[n], dst_ref=outs[n], send_sem=send_sems.at[n], recv_sem=recv_sems.at[n],
            device_id=(x, y, 1 - c), device_id_type=MESH) for n in range(n_arr)]
        for cp in copies:
            cp.start()
        for cp in copies:
            cp.wait()

    any_spec = pl.BlockSpec(memory_space=pl.ANY)
    return pl.pallas_call(
        body, name=name,
        out_shape=tuple(jax.ShapeDtypeStruct(a.shape, a.dtype) for a in arrs),
        in_specs=[any_spec] * n_arr, out_specs=tuple([any_spec] * n_arr),
        scratch_shapes=[pltpu.SemaphoreType.DMA((n_arr,)), pltpu.SemaphoreType.DMA((n_arr,))],
    )(*arrs)


def _mod_shard(c_all, w_ada_sh, b_sh):
    d, n = w_ada_sh.shape
    bn = 512

    def body(c_ref, w_ref, b_ref, o_ref):
        cc = c_ref[...]
        ca = cc * _sigmoid(cc)
        o_ref[...] = _dot(ca, w_ref[...]) + b_ref[...]

    return pl.pallas_call(
        body, name="mod_shard", grid=(n // bn,),
        out_shape=jax.ShapeDtypeStruct((N_DEV, n), F32),
        in_specs=[_full((N_DEV, d)), pl.BlockSpec((d, bn), lambda j: (0, j)), pl.BlockSpec((1, bn), lambda j: (0, j))],
        out_specs=pl.BlockSpec((N_DEV, bn), lambda j: (0, j)),
        compiler_params=_cparams(("parallel",)),
    )(c_all, w_ada_sh, b_sh)


def _ssm_prep(lam_re, lam_im, log_step):
    g, p = lam_re.shape

    def body(lr_ref, li_ref, ls_ref, ar_ref, ai_ref, cr_ref, ci_ref):
        lr = jnp.minimum(lr_ref[...], LAMBDA_RE_MAX)
        li = li_ref[...]
        st = jnp.exp(ls_ref[...])
        mag = jnp.exp(lr * st)
        ar = mag * jnp.cos(li * st)
        ai = mag * jnp.sin(li * st)
        den = lr * lr + li * li
        nr = ar - 1.0
        ar_ref[...] = ar
        ai_ref[...] = ai
        cr_ref[...] = (nr * lr + ai * li) / den
        ci_ref[...] = (ai * lr - nr * li) / den

    sds = jax.ShapeDtypeStruct((g, p), F32)
    return pl.pallas_call(body, name="ssm_prep", out_shape=(sds,) * 4)(lam_re, lam_im, log_step)


def _ssm_blocks(bt_re, bt_im, ct_re, ct_im, coef_rows):
    gh, gp = bt_re.shape
    nb = 4
    cb, rb = gp // nb, gp // nb

    def body(btr, bti, ctr, cti, cf, bre_o, bim_o, cre_o, cim_o):
        j = pl.program_id(0)
        row = lax.broadcasted_iota(jnp.int32, (gh, cb), 0)
        col = lax.broadcasted_iota(jnp.int32, (gh, cb), 1) + j * cb
        mask = (row >> 4) == (col >> 6)
        cr, ci = cf[0:1, :], cf[1:2, :]
        br, bi = btr[...], bti[...]
        bre_o[...] = jnp.where(mask, br * cr - bi * ci, 0.0).astype(BF16)
        bim_o[...] = jnp.where(mask, br * ci + bi * cr, 0.0).astype(BF16)
        row2 = lax.broadcasted_iota(jnp.int32, (rb, gh), 0) + j * rb
        col2 = lax.broadcasted_iota(jnp.int32, (rb, gh), 1)
        mask2 = (row2 >> 6) == (col2 >> 4)
        cre_o[...] = jnp.where(mask2, ctr[...], 0.0).astype(BF16)
        cim_o[...] = jnp.where(mask2, cti[...], 0.0).astype(BF16)

    bspec = pl.BlockSpec((gh, cb), lambda j: (0, j))
    cspec = pl.BlockSpec((rb, gh), lambda j: (j, 0))
    return pl.pallas_call(
        body, name="ssm_blocks", grid=(nb,),
        out_shape=(jax.ShapeDtypeStruct((gh, gp), BF16),) * 2 + (jax.ShapeDtypeStruct((gp, gh), BF16),) * 2,
        in_specs=[bspec, bspec, cspec, cspec, pl.BlockSpec((SUBLANES, cb), lambda j: (0, j))],
        out_specs=(bspec, bspec, cspec, cspec),
        compiler_params=_cparams(("parallel",)),
    )(bt_re, bt_im, ct_re, ct_im, coef_rows)


def _scan_consts(a_ref, reverse):
    w = a_ref.shape[1]
    ar1 = a_ref[0:1, :]
    ai1 = a_ref[1:2, :]
    if reverse:
        ai1 = -ai1
    pr, pi = [ar1], [ai1]
    for _ in range(1, SUBLANES):
        nr = pr[-1] * ar1 - pi[-1] * ai1
        ni = pr[-1] * ai1 + pi[-1] * ar1
        pr.append(nr)
        pi.append(ni)
    row = lax.broadcasted_iota(jnp.int32, (SUBLANES, w), 0)
    dist = (SUBLANES - 1 - row) if reverse else row

    def pick(vals):
        out = jnp.broadcast_to(vals[SUBLANES - 1], (SUBLANES, w))
        for r in range(SUBLANES - 1):
            out = jnp.where(dist == r, vals[r], out)
        return out

    p_r, p_i = pick(pr), pick(pi)
    steps = []
    for k in (1, 2, 4):
        steps.append((k, jnp.where(dist >= k, pr[k - 1], 0.0), jnp.where(dist >= k, pi[k - 1], 0.0)))
    a8 = (jnp.broadcast_to(pr[SUBLANES - 1], (SUBLANES, w)), jnp.broadcast_to(pi[SUBLANES - 1], (SUBLANES, w)))
    return row, p_r, p_i, steps, a8


def _scan_tile(xr, xi, cr, ci, consts, reverse):
    row, p_r, p_i, steps, (a8r, a8i) = consts
    for k, s_r, s_i in steps:
        sh = (SUBLANES - k) if reverse else k
        qr = pltpu.roll(xr, sh, 0)
        qi = pltpu.roll(xi, sh, 0)
        xr, xi = xr + s_r * qr - s_i * qi, xi + s_r * qi + s_i * qr
    outr = xr + p_r * cr - p_i * ci
    outi = xi + p_r * ci + p_i * cr
    e = 0 if reverse else SUBLANES - 1
    er = jnp.broadcast_to(xr[e:e + 1, :], xr.shape)
    ei = jnp.broadcast_to(xi[e:e + 1, :], xi.shape)
    return outr, outi, er + a8r * cr - a8i * ci, ei + a8r * ci + a8i * cr


def _scan_fwd(a_rows, bu_re, bu_im):
    t, n = bu_re.shape
    tb, w = _blk(t, TB_SCAN), W_SCAN
    ntile = tb // SUBLANES

    def body(a_ref, br_ref, bi_ref, sr_ref, si_ref, car, cai):
        @pl.when(pl.program_id(1) == 0)
        def _():
            car[...] = jnp.zeros(car.shape, F32)
            cai[...] = jnp.zeros(cai.shape, F32)
        consts = _scan_consts(a_ref, False)

        def tile(i, carry):
            o = pl.multiple_of(i * SUBLANES, SUBLANES)
            outr, outi, ncr, nci = _scan_tile(br_ref[pl.ds(o, SUBLANES), :], bi_ref[pl.ds(o, SUBLANES), :],
                                              carry[0], carry[1], consts, False)
            sr_ref[pl.ds(o, SUBLANES), :] = outr
            si_ref[pl.ds(o, SUBLANES), :] = outi
            return ncr, nci

        def tiles(i, carry):
            for s in range(SCAN_UNROLL):
                carry = tile(i * SCAN_UNROLL + s, carry)
            return carry

        cr, ci = lax.fori_loop(0, ntile // SCAN_UNROLL, tiles, (car[...], cai[...]))
        car[...] = cr
        cai[...] = ci

    spec = pl.BlockSpec((tb, w), lambda s, k: (k, s))
    sds = jax.ShapeDtypeStruct((t, n), F32)
    return pl.pallas_call(
        body, name="scan_fwd", grid=(n // w, t // tb), out_shape=(sds, sds),
        in_specs=[pl.BlockSpec((SUBLANES, w), lambda s, k: (0, s)), spec, spec], out_specs=(spec, spec),
        scratch_shapes=[pltpu.VMEM((SUBLANES, w), F32), pltpu.VMEM((SUBLANES, w), F32)],
        compiler_params=_cparams(("parallel", "arbitrary"), VMEM_MID),
    )(a_rows, bu_re, bu_im)


def _scan_bwd(a_rows, g_re, g_im, s_re, s_im):
    t, n = g_re.shape
    tb, w = _blk(t, TB_SCAN), W_SCAN
    ntile = tb // SUBLANES
    nt = t // tb

    def body(a_ref, gr_ref, gi_ref, sr_ref, si_ref, or_ref, oi_ref, gar_ref, gai_ref, car, cai):
        @pl.when(pl.program_id(1) == 0)
        def _():
            car[...] = jnp.zeros(car.shape, F32)
            cai[...] = jnp.zeros(cai.shape, F32)
            gar_ref[...] = jnp.zeros(gar_ref.shape, F32)
            gai_ref[...] = jnp.zeros(gai_ref.shape, F32)
        consts = _scan_consts(a_ref, True)
        row = consts[0]

        def tile(i, carry):
            cr, ci, accr, acci = carry
            o = pl.multiple_of((ntile - 1 - i) * SUBLANES, SUBLANES)
            outr, outi, ncr, nci = _scan_tile(gr_ref[pl.ds(o, SUBLANES), :], gi_ref[pl.ds(o, SUBLANES), :],
                                              cr, ci, consts, True)
            or_ref[pl.ds(o, SUBLANES), :] = outr
            oi_ref[pl.ds(o, SUBLANES), :] = outi
            gnr = jnp.where(row == SUBLANES - 1, cr, pltpu.roll(outr, SUBLANES - 1, 0))
            gni = jnp.where(row == SUBLANES - 1, ci, pltpu.roll(outi, SUBLANES - 1, 0))
            sr = sr_ref[pl.ds(o, SUBLANES), :]
            si = si_ref[pl.ds(o, SUBLANES), :]
            return ncr, nci, accr + sr * gnr + si * gni, acci + sr * gni - si * gnr

        def tiles(i, carry):
            for s in range(SCAN_UNROLL):
                carry = tile(i * SCAN_UNROLL + s, carry)
            return carry

        cr, ci, accr, acci = lax.fori_loop(0, ntile // SCAN_UNROLL, tiles,
                                           (car[...], cai[...], gar_ref[...], gai_ref[...]))
        car[...] = cr
        cai[...] = ci
        gar_ref[...] = accr
        gai_ref[...] = acci

    spec = pl.BlockSpec((tb, w), lambda s, k: (nt - 1 - k, s))
    aspec = pl.BlockSpec((SUBLANES, w), lambda s, k: (0, s))
    sds = jax.ShapeDtypeStruct((t, n), F32)
    asds = jax.ShapeDtypeStruct((SUBLANES, n), F32)
    return pl.pallas_call(
        body, name="scan_bwd", grid=(n // w, nt), out_shape=(sds, sds, asds, asds),
        in_specs=[aspec, spec, spec, spec, spec], out_specs=(spec, spec, aspec, aspec),
        scratch_shapes=[pltpu.VMEM((SUBLANES, w), F32), pltpu.VMEM((SUBLANES, w), F32)],
        compiler_params=_cparams(("parallel", "arbitrary"), VMEM_MID),
    )(a_rows, g_re, g_im, s_re, s_im)


def _mix_in(x, vec, w_in_st, b_re, b_im):
    t, d = x.shape
    ns, _, nc = w_in_st.shape
    dssm, nstate = b_re.shape
    du, ds = dssm // SSM_SPLIT, nstate // SSM_SPLIT
    tb = _blk(t, TB_MIX)

    def body(x_ref, vec_ref, w_ref, bre_ref, bim_ref, proj_ref, bur_ref, bui_ref, h1_ref):
        xv = x_ref[...]
        r = lax.rsqrt(_rowmean(xv * xv) + EPS)
        h = xv * r * vec_ref[0:1, :] * vec_ref[1:2, :] + vec_ref[2:3, :]
        hb = h.astype(BF16)
        h1_ref[...] = hb
        u = None
        for j in range(ns):
            pj = jnp.dot(hb, w_ref[j], preferred_element_type=F32)
            proj_ref[:, j * nc:(j + 1) * nc] = pj
            if j == 0:
                u = pj
        ub = u.astype(BF16)
        for q in range(SSM_SPLIT):
            rq, cq = slice(q * du, (q + 1) * du), slice(q * ds, (q + 1) * ds)
            bur_ref[:, cq] = jnp.dot(ub[:, rq], bre_ref[rq, cq], preferred_element_type=F32)
            bui_ref[:, cq] = jnp.dot(ub[:, rq], bim_ref[rq, cq], preferred_element_type=F32)

    return pl.pallas_call(
        body, name="mix_in", grid=(t // tb,),
        out_shape=(jax.ShapeDtypeStruct((t, ns * nc), F32), jax.ShapeDtypeStruct((t, nstate), F32),
                   jax.ShapeDtypeStruct((t, nstate), F32), jax.ShapeDtypeStruct((t, d), BF16)),
        in_specs=[_rows(tb, d), _full((SUBLANES, d)), _full(w_in_st.shape), _full(b_re.shape), _full(b_im.shape)],
        out_specs=(_rows(tb, ns * nc), _rows(tb, nstate), _rows(tb, nstate), _rows(tb, d)),
        compiler_params=_cparams(("parallel",), VMEM_BIG),
    )(x, vec, w_in_st, b_re, b_im)


def _head_ms(y, h_ref):
    return _split_dot(y * y, h_ref[...])


def _conv3(x, halo, w_ref):
    return w_ref[0:1, :] * _shift_down(x, halo, 2) + w_ref[1:2, :] * _shift_down(x, halo, 1) + w_ref[2:3, :] * x


def _mix_out(x, proj, s_re, s_im, c_re, c_im, v512, convw, glu_w, h16, h64, w_out, vd):
    t, d = x.shape
    dh = c_re.shape[1]
    nstate = s_re.shape[1]
    du, ds = dh // SSM_SPLIT, nstate // SSM_SPLIT
    tb = _blk(t, TB_MIX)

    def body(x_ref, u_ref, bg_ref, cg_ref, v_ref, cgh_ref, vh_ref, sr_ref, si_ref, cre_ref, cim_ref, p_ref,
             cw_ref, gw_ref, h16_ref, h64_ref, wo_ref, vd_ref, y1_ref, o_ref, x2_ref):
        i = pl.program_id(0)
        u = u_ref[...]
        ys = []
        for q in range(SSM_SPLIT):
            rq, cq = slice(q * ds, (q + 1) * ds), slice(q * du, (q + 1) * du)
            ys.append(_dot(sr_ref[:, rq], cre_ref[rq, cq]) - _dot(si_ref[:, rq], cim_ref[rq, cq]))
        ys = jnp.concatenate(ys, axis=1)
        y1 = ys + p_ref[0:1, :] * u
        y1_ref[...] = y1
        z = _gelu(y1)
        q = _dot(z, gw_ref[...]) + p_ref[1:2, :]
        ya = z * _sigmoid(q)
        na = ya * lax.rsqrt(_head_ms(ya, h16_ref) + EPS) * p_ref[2:3, :]
        cv = cg_ref[...] * v_ref[...]
        cvh = jnp.where(i > 0, cgh_ref[...] * vh_ref[...], 0.0)
        yb = bg_ref[...] * _conv3(cv, cvh, cw_ref)
        nb = yb * lax.rsqrt(_head_ms(yb, h64_ref) + EPS) * p_ref[3:4, :]
        o = _dot(na, wo_ref[0:dh, :]) + _dot(nb, wo_ref[dh:2 * dh, :])
        o_ref[...] = o
        on = o * lax.rsqrt(_rowmean(o * o) + EPS) * vd_ref[0:1, :]
        x2_ref[...] = x_ref[...] + vd_ref[1:2, :] * on

    return pl.pallas_call(
        body, name="mix_out", grid=(t // tb,),
        out_shape=(jax.ShapeDtypeStruct((t, dh), F32), jax.ShapeDtypeStruct((t, d), F32),
                   jax.ShapeDtypeStruct((t, d), F32)),
        in_specs=[_rows(tb, d), _rows(tb, dh, 0), _rows(tb, dh, 1), _rows(tb, dh, 2), _rows(tb, dh, 3),
                  _halo_prev(tb, dh, 2), _halo_prev(tb, dh, 3), _rows(tb, nstate), _rows(tb, nstate),
                  _full(c_re.shape), _full(c_im.shape), _full(v512.shape), _full(convw.shape), _full(glu_w.shape),
                  _full(h16.shape), _full(h64.shape), _full(w_out.shape), _full(vd.shape)],
        out_specs=(_rows(tb, dh), _rows(tb, d), _rows(tb, d)),
        compiler_params=_cparams(("parallel",), VMEM_BIG),
    )(x, proj, proj, proj, proj, proj, proj, s_re, s_im, c_re, c_im, v512, convw, glu_w, h16, h64, w_out, vd)


def _ffn_up(x2, vec, w_up_st):
    t, d = x2.shape
    ns, _, nc = w_up_st.shape
    tb = _blk(t, TB_FFN)

    def body(x_ref, vec_ref, w_ref, up_ref, h2_ref):
        xv = x_ref[...]
        r = lax.rsqrt(_rowmean(xv * xv) + EPS)
        h = xv * r * vec_ref[0:1, :] * vec_ref[1:2, :] + vec_ref[2:3, :]
        hb = h.astype(BF16)
        h2_ref[...] = hb
        for j in range(ns):
            up_ref[:, j * nc:(j + 1) * nc] = jnp.dot(hb, w_ref[j], preferred_element_type=F32)

    return pl.pallas_call(
        body, name="ffn_up", grid=(t // tb,),
        out_shape=(jax.ShapeDtypeStruct((t, ns * nc), F32), jax.ShapeDtypeStruct((t, d), BF16)),
        in_specs=[_rows(tb, d), _full((SUBLANES, d)), _resident(w_up_st.shape)],
        out_specs=(_rows(tb, ns * nc), _rows(tb, d)),
        compiler_params=_cparams(("parallel",), VMEM_BIG),
    )(x2, vec, w_up_st)


def _ffn_down(up, fw, w_down, w_down_t, x2, tgt, vd):
    t, nh = up.shape
    dff, d = w_down.shape
    tb = _blk(t, TB_FFN)
    inv_d = 1.0 / d

    def body(up_ref, uph_ref, fw_ref, wd_ref, wdt_ref, x2_ref, tgt_ref, vd_ref,
             act_ref, ddn_ref, dout_ref, dhid_ref, vec_ref, loss_ref, a_s, vv_s, sg_s):
        i = pl.program_id(0)

        def conv_cols(sl):
            x = up_ref[:, sl]
            halo = jnp.where(i > 0, uph_ref[:, sl], 0.0)
            return (fw_ref[0:1, sl] * _shift_down(x, halo, 2) + fw_ref[1:2, sl] * _shift_down(x, halo, 1)
                    + fw_ref[2:3, sl] * x)

        dn = None
        for o in range(0, dff, CW_FFN):
            sl = slice(o, o + CW_FFN)
            a = conv_cols(sl)
            vv = conv_cols(slice(dff + o, dff + o + CW_FFN))
            sg = _sigmoid(a)
            a_s[:, sl] = a
            vv_s[:, sl] = vv
            sg_s[:, sl] = sg
            actb = (a * sg * vv).astype(BF16)
            act_ref[:, sl] = actb
            pj = lax.dot_general(actb, wdt_ref[:, sl], (((1,), (1,)), ((), ())), preferred_element_type=F32)
            dn = pj if dn is None else dn + pj
        r3 = lax.rsqrt(_rowmean(dn * dn) + EPS)
        xn = dn * r3
        g = vd_ref[0:1, :]
        gt2 = vd_ref[1:2, :]
        dnn = xn * g
        diff = x2_ref[...] + gt2 * dnn - tgt_ref[...]
        part = 0.5 * inv_d * jnp.sum(diff * diff)

        @pl.when(i == 0)
        def _():
            loss_ref[...] = jnp.zeros(loss_ref.shape, F32)
        loss_ref[...] += part
        dout = diff * inv_d
        dout_ref[...] = dout
        ddnn = dout * gt2
        _acc_rows(vec_ref, i == 0, [_colsum(dout * dnn), _colsum(ddnn * xn)])
        dxn = ddnn * g
        ddn = r3 * (dxn - xn * _rowmean(dxn * xn))
        ddnb = ddn.astype(BF16)
        ddn_ref[...] = ddnb
        for o in range(0, dff, CW_FFN):
            sl = slice(o, o + CW_FFN)
            dact = lax.dot_general(ddnb, wd_ref[sl, :], (((1,), (1,)), ((), ())), preferred_element_type=F32)
            a, vv, sg = a_s[:, sl], vv_s[:, sl], sg_s[:, sl]
            dhid_ref[:, sl] = (dact * vv * sg * (1.0 + a * (1.0 - sg))).astype(BF16)
            dhid_ref[:, dff + o:dff + o + CW_FFN] = (dact * (a * sg)).astype(BF16)

    return pl.pallas_call(
        body, name="ffn_down", grid=(t // tb,),
        scratch_shapes=[pltpu.VMEM((tb, dff), F32)] * 3,
        out_shape=(jax.ShapeDtypeStruct((t, dff), BF16), jax.ShapeDtypeStruct((t, d), BF16),
                   jax.ShapeDtypeStruct((t, d), F32), jax.ShapeDtypeStruct((t, nh), BF16),
                   jax.ShapeDtypeStruct((SUBLANES, d), F32), jax.ShapeDtypeStruct((SUBLANES, 128), F32)),
        in_specs=[_rows(tb, nh), _halo_prev(tb, nh), _full(fw.shape), _resident(w_down.shape),
                  _resident(w_down_t.shape), _rows(tb, d),
                  _rows(tb, d), _full(vd.shape)],
        out_specs=(_rows(tb, dff), _rows(tb, d), _rows(tb, d), _rows(tb, nh), _full((SUBLANES, d)),
                   _full((SUBLANES, 128))),
        compiler_params=_cparams(("arbitrary",), VMEM_BIG),
    )(up, up, fw, w_down, w_down_t, x2, tgt, vd)


def _ffn_up_bwd(dhid, up, fw, x2, dout, vec, w_up_st):
    t, nh = dhid.shape
    d = x2.shape[1]
    ns, _, nc = w_up_st.shape
    tb = _blk(t, TB_FFN)
    nblk = t // tb

    def body(dh_ref, dhn_ref, up_ref, fw_ref, x2_ref, dout_ref, vec_ref, w_ref,
             dx2_ref, dup_ref, vp_ref, df_ref):
        i = pl.program_id(0)

        @pl.when(i == 0)
        def _():
            df_ref[...] = jnp.zeros(df_ref.shape, F32)
        for o in range(0, nh, CW_FFN):
            sl = slice(o, o + CW_FFN)
            dh = dh_ref[:, sl].astype(F32)
            dhn = jnp.where(i < nblk - 1, dhn_ref[:, sl].astype(F32), 0.0)
            dh1 = _shift_up(dh, dhn, 1)
            dh2 = _shift_up(dh, dhn, 2)
            dup_ref[:, sl] = (fw_ref[2:3, sl] * dh + fw_ref[1:2, sl] * dh1 + fw_ref[0:1, sl] * dh2).astype(BF16)
            up_v = up_ref[:, sl]
            df_ref[0:1, sl] += _colsum(dh2 * up_v)
            df_ref[1:2, sl] += _colsum(dh1 * up_v)
            df_ref[2:3, sl] += _colsum(dh * up_v)
        dh2 = None
        for j in range(ns):
            pj = lax.dot_general(dup_ref[:, j * nc:(j + 1) * nc], w_ref[j], (((1,), (1,)), ((), ())),
                                 preferred_element_type=F32)
            dh2 = pj if dh2 is None else dh2 + pj
        xv = x2_ref[...]
        r = lax.rsqrt(_rowmean(xv * xv) + EPS)
        xn = xv * r
        g = vec_ref[0:1, :]
        hg = xn * g
        dhg = dh2 * vec_ref[1:2, :]
        _acc_rows(vp_ref, i == 0, [_colsum(dh2), _colsum(dh2 * hg), _colsum(dhg * xn)])
        dxn = dhg * g
        dx2_ref[...] = dout_ref[...] + r * (dxn - xn * _rowmean(dxn * xn))

    return pl.pallas_call(
        body, name="ffn_up_bwd", grid=(nblk,),
        out_shape=(jax.ShapeDtypeStruct((t, d), F32), jax.ShapeDtypeStruct((t, nh), BF16),
                   jax.ShapeDtypeStruct((SUBLANES, d), F32), jax.ShapeDtypeStruct((SUBLANES, nh), F32)),
        in_specs=[_rows(tb, nh), _halo_next(tb, nh, t, rows=BF16_ROWS), _rows(tb, nh), _full(fw.shape),
                  _rows(tb, d), _rows(tb, d), _full(vec.shape), _resident(w_up_st.shape)],
        out_specs=(_rows(tb, d), _rows(tb, nh), _full((SUBLANES, d)), _full((SUBLANES, nh))),
        compiler_params=_cparams(("arbitrary",), VMEM_BIG),
    )(dhid, dhid, up, fw, x2, dout, vec, w_up_st)


def _mix_out_bwd(dx2, o, y1, proj, c_re, c_im, v512, convw, glu_w, h16, h64, w_out, vd):
    t, d = dx2.shape
    dh = y1.shape[1]
    nstate = c_re.shape[0]
    du, ds = dh // SSM_SPLIT, nstate // SSM_SPLIT
    tb = _blk(t, TB_MIX)

    def body(dx2_ref, o_ref, y1_ref, u_ref, bg_ref, cg_ref, v_ref, cgh_ref, vh_ref, cre_ref, cim_ref, p_ref,
             cw_ref, gw_ref, h16_ref, h64_ref, wo_ref, vd_ref,
             do_ref, ycat_ref, z_ref, dq_ref, dy1_ref, gr_ref, gi_ref, dcc_ref, dbg_ref, vpd_ref, vp5_ref):
        i = pl.program_id(0)
        first = i == 0
        ov = o_ref[...]
        ro = lax.rsqrt(_rowmean(ov * ov) + EPS)
        on_ = ov * ro
        g = vd_ref[0:1, :]
        dx2v = dx2_ref[...]
        don = dx2v * vd_ref[1:2, :]
        _acc_rows(vpd_ref, first, [_colsum(dx2v * on_ * g), _colsum(don * on_)])
        dxn = don * g
        dob = (ro * (dxn - on_ * _rowmean(dxn * on_))).astype(BF16)
        do_ref[...] = dob
        dyc_a = lax.dot_general(dob, wo_ref[0:dh, :], (((1,), (1,)), ((), ())), preferred_element_type=F32)
        dyc_b = lax.dot_general(dob, wo_ref[dh:2 * dh, :], (((1,), (1,)), ((), ())), preferred_element_type=F32)
        y1v = y1_ref[...]
        u = u_ref[...]
        z = _gelu(y1v)
        zb = z.astype(BF16)
        z_ref[...] = zb
        sg = _sigmoid(jnp.dot(zb, gw_ref[...], preferred_element_type=F32) + p_ref[1:2, :])
        ya = z * sg
        ra = lax.rsqrt(_head_ms(ya, h16_ref) + EPS)
        yan = ya * ra
        ga = p_ref[2:3, :]
        ycat_ref[:, 0:dh] = (yan * ga).astype(BF16)
        dyn = dyc_a * ga
        dya = ra * (dyn - yan * _split_dot(dyn * yan, h16_ref[...]))
        dq = dya * z * sg * (1.0 - sg)
        dqb = dq.astype(BF16)
        dq_ref[...] = dqb
        dz = dya * sg + lax.dot_general(dqb, gw_ref[...], (((1,), (1,)), ((), ())), preferred_element_type=F32)
        dy1 = dz * _gelu_grad(y1v)
        dy1_ref[...] = dy1
        dy1b = dy1.astype(BF16)
        for q in range(SSM_SPLIT):
            rq, cq = slice(q * ds, (q + 1) * ds), slice(q * du, (q + 1) * du)
            gr_ref[:, rq] = lax.dot_general(dy1b[:, cq], cre_ref[rq, cq], (((1,), (1,)), ((), ())),
                                            preferred_element_type=F32)
            gi_ref[:, rq] = -lax.dot_general(dy1b[:, cq], cim_ref[rq, cq], (((1,), (1,)), ((), ())),
                                             preferred_element_type=F32)
        bg = bg_ref[...]
        cv = cg_ref[...] * v_ref[...]
        cvh = jnp.where(i > 0, cgh_ref[...] * vh_ref[...], 0.0)
        cv1 = _shift_down(cv, cvh, 1)
        cv2 = _shift_down(cv, cvh, 2)
        cc = cw_ref[0:1, :] * cv2 + cw_ref[1:2, :] * cv1 + cw_ref[2:3, :] * cv
        yb = bg * cc
        rb = lax.rsqrt(_head_ms(yb, h64_ref) + EPS)
        ybn = yb * rb
        gb = p_ref[3:4, :]
        ycat_ref[:, dh:2 * dh] = (ybn * gb).astype(BF16)
        dynb = dyc_b * gb
        dyb = rb * (dynb - ybn * _split_dot(dynb * ybn, h64_ref[...]))
        dcc = dyb * bg
        dbg_ref[...] = dyb * cc
        dcc_ref[...] = dcc
        _acc_rows(vp5_ref, first, [_colsum(dyc_a * yan), _colsum(dyc_b * ybn), _colsum(dq), _colsum(dy1 * u),
                                   _colsum(dcc * cv2), _colsum(dcc * cv1), _colsum(dcc * cv)])

    return pl.pallas_call(
        body, name="mix_out_bwd", grid=(t // tb,),
        out_shape=(jax.ShapeDtypeStruct((t, d), BF16), jax.ShapeDtypeStruct((t, 2 * dh), BF16),
                   jax.ShapeDtypeStruct((t, dh), BF16), jax.ShapeDtypeStruct((t, dh), BF16),
                   jax.ShapeDtypeStruct((t, dh), F32), jax.ShapeDtypeStruct((t, nstate), F32),
                   jax.ShapeDtypeStruct((t, nstate), F32), jax.ShapeDtypeStruct((t, dh), F32),
                   jax.ShapeDtypeStruct((t, dh), F32), jax.ShapeDtypeStruct((SUBLANES, d), F32),
                   jax.ShapeDtypeStruct((SUBLANES, dh), F32)),
        in_specs=[_rows(tb, d), _rows(tb, d), _rows(tb, dh), _rows(tb, dh, 0), _rows(tb, dh, 1), _rows(tb, dh, 2),
                  _rows(tb, dh, 3), _halo_prev(tb, dh, 2), _halo_prev(tb, dh, 3), _full(c_re.shape), _full(c_im.shape),
                  _full(v512.shape), _full(convw.shape), _full(glu_w.shape), _full(h16.shape), _full(h64.shape),
                  _full(w_out.shape), _full(vd.shape)],
        out_specs=(_rows(tb, d), _rows(tb, 2 * dh), _rows(tb, dh), _rows(tb, dh), _rows(tb, dh), _rows(tb, nstate),
                   _rows(tb, nstate), _rows(tb, dh), _rows(tb, dh), _full((SUBLANES, d)), _full((SUBLANES, dh))),
        compiler_params=_cparams(("arbitrary",), VMEM_BIG),
    )(dx2, o, y1, proj, proj, proj, proj, proj, proj, c_re, c_im, v512, convw, glu_w, h16, h64, w_out, vd)


def _mix_in_bwd(gt_re, gt_im, b_re, b_im, dy1, dcc, dbg, proj, x, dx2, vec, v512, convw, w_in_st):
    t, d = x.shape
    dh = dy1.shape[1]
    nstate = gt_re.shape[1]
    du_w, ds = dh // SSM_SPLIT, nstate // SSM_SPLIT
    ns, _, nc = w_in_st.shape
    tb = _blk(t, TB_MIX)
    nblk = t // tb

    def body(gr_ref, gi_ref, bre_ref, bim_ref, dy1_ref, dcc_ref, dccn_ref, dbg_ref, cg_ref, v_ref, x_ref, dx2_ref,
             vec_ref, p_ref, cw_ref, w_ref, gx_ref, dproj_ref, vp_ref):
        i = pl.program_id(0)
        du = []
        for q in range(SSM_SPLIT):
            rq, cq = slice(q * du_w, (q + 1) * du_w), slice(q * ds, (q + 1) * ds)
            du.append(lax.dot_general(gr_ref[:, cq].astype(BF16), bre_ref[rq, cq], (((1,), (1,)), ((), ())),
                                      preferred_element_type=F32)
                      + lax.dot_general(gi_ref[:, cq].astype(BF16), bim_ref[rq, cq], (((1,), (1,)), ((), ())),
                                        preferred_element_type=F32))
        du = dy1_ref[...] * p_ref[0:1, :] + jnp.concatenate(du, axis=1)
        dcc = dcc_ref[...]
        dccn = jnp.where(i < nblk - 1, dccn_ref[...], 0.0)
        dcv = (cw_ref[2:3, :] * dcc + cw_ref[1:2, :] * _shift_up(dcc, dccn, 1)
               + cw_ref[0:1, :] * _shift_up(dcc, dccn, 2))
        parts = [du, dbg_ref[...], dcv * v_ref[...], dcv * cg_ref[...]]
        dh1 = None
        for j in range(ns):
            pb = parts[j].astype(BF16)
            dproj_ref[:, j * nc:(j + 1) * nc] = pb
            pj = lax.dot_general(pb, w_ref[j], (((1,), (1,)), ((), ())), preferred_element_type=F32)
            dh1 = pj if dh1 is None else dh1 + pj
        xv = x_ref[...]
        r = lax.rsqrt(_rowmean(xv * xv) + EPS)
        xn = xv * r
        g = vec_ref[0:1, :]
        hg = xn * g
        dhg = dh1 * vec_ref[1:2, :]
        _acc_rows(vp_ref, i == 0, [_colsum(dh1), _colsum(dh1 * hg), _colsum(dhg * xn)])
        dxn = dhg * g
        gx_ref[...] = dx2_ref[...] + r * (dxn - xn * _rowmean(dxn * xn))

    assert nc == dh and ns == 4
    return pl.pallas_call(
        body, name="mix_in_bwd", grid=(nblk,),
        out_shape=(jax.ShapeDtypeStruct((t, d), F32), jax.ShapeDtypeStruct((t, ns * nc), BF16),
                   jax.ShapeDtypeStruct((SUBLANES, d), F32)),
        in_specs=[_rows(tb, nstate), _rows(tb, nstate), _full(b_re.shape), _full(b_im.shape), _rows(tb, dh),
                  _rows(tb, dh), _halo_next(tb, dh, t), _rows(tb, dh), _rows(tb, dh, 2), _rows(tb, dh, 3),
                  _rows(tb, d), _rows(tb, d), _full(vec.shape), _full(v512.shape), _full(convw.shape),
                  _full(w_in_st.shape)],
        out_specs=(_rows(tb, d), _rows(tb, ns * nc), _full((SUBLANES, d))),
        compiler_params=_cparams(("arbitrary",), VMEM_BIG),
    )(gt_re, gt_im, b_re, b_im, dy1, dcc, dcc, dbg, proj, proj, x, dx2, vec, v512, convw, w_in_st)


def _matmul_tn(a, b, m, bn, out_dtype, name, diag=False, bt=TB_TN, after=None):
    t = a.shape[0]
    n = b.shape[1]
    bt = _blk(t, bt)
    nk = t // bt
    extra = [] if after is None else [after]
    a_map = (lambda j, k: (k, j)) if diag else (lambda j, k: (k, 0))

    def body(a_ref, b_ref, *rest):
        o_ref, acc_ref = rest[-2:]
        k = pl.program_id(1)

        @pl.when(k == 0)
        def _():
            acc_ref[...] = jnp.zeros(acc_ref.shape, F32)
        acc_ref[...] += _dot_tn(a_ref[...], b_ref[...])

        @pl.when(k == nk - 1)
        def _():
            o_ref[...] = acc_ref[...].astype(out_dtype)

    return pl.pallas_call(
        body, name=name, grid=(n // bn, nk),
        out_shape=jax.ShapeDtypeStruct((n // bn, m, bn), out_dtype),
        in_specs=[pl.BlockSpec((bt, m), a_map), pl.BlockSpec((bt, bn), lambda j, k: (k, j))]
        + [pl.BlockSpec(memory_space=pl.ANY)] * len(extra),
        out_specs=pl.BlockSpec((None, m, bn), lambda j, k: (j, 0, 0)),
        scratch_shapes=[pltpu.VMEM((m, bn), F32)],
        compiler_params=_cparams(("parallel", "arbitrary"), VMEM_BIG),
    )(a, b, *extra)


def _ssm_bgrad(d_bre, d_bim, bt_re, bt_im, rows_in, fold):
    gh, cb = d_bre.shape
    nb = SSM_SPLIT
    rb = gh // nb
    gp = nb * cb
    p = fold.shape[1]

    def body(dr_ref, di_ref, br_ref, bi_ref, rin_ref, f_ref, dbr_ref, dbi_ref, rout_ref):
        row = lax.broadcasted_iota(jnp.int32, (rb, cb), 0)
        col = lax.broadcasted_iota(jnp.int32, (rb, cb), 1)
        mask = (row >> 4) == (col >> 6)
        gr = jnp.where(mask, dr_ref[...], 0.0)
        gi = jnp.where(mask, di_ref[...], 0.0)
        cr, ci = rin_ref[0:1, :], rin_ref[1:2, :]
        dbr_ref[...] = _split3_dot(cr * gr + ci * gi, f_ref[...])
        dbi_ref[...] = _split3_dot(cr * gi - ci * gr, f_ref[...])
        br, bi = br_ref[...], bi_ref[...]
        rout_ref[...] = jnp.zeros(rout_ref.shape, F32)
        rout_ref[0:1, :] = _colsum(br * gr + bi * gi)
        rout_ref[1:2, :] = _colsum(br * gi - bi * gr)

    dspec = pl.BlockSpec((rb, cb), lambda j: (j, 0))
    bspec = pl.BlockSpec((rb, cb), lambda j: (j, j))
    rspec = pl.BlockSpec((SUBLANES, cb), lambda j: (0, j))
    ospec = pl.BlockSpec((rb, p), lambda j: (j, 0))
    return pl.pallas_call(
        body, name="ssm_bgrad", grid=(nb,),
        out_shape=(jax.ShapeDtypeStruct((gh, p), F32), jax.ShapeDtypeStruct((gh, p), F32),
                   jax.ShapeDtypeStruct((SUBLANES, gp), F32)),
        in_specs=[dspec, dspec, bspec, bspec, rspec, _full(fold.shape)],
        out_specs=(ospec, ospec, rspec),
        compiler_params=_cparams(("parallel",)),
    )(d_bre, d_bim, bt_re, bt_im, rows_in, fold)


def _ssm_cgrad(d_cre, d_cim, fold):
    gp, cb = d_cre.shape
    nb = SSM_SPLIT
    rb = gp // nb
    h = fold.shape[1]

    def body(dr_ref, di_ref, f_ref, cr_ref, ci_ref):
        row = lax.broadcasted_iota(jnp.int32, (rb, cb), 0)
        col = lax.broadcasted_iota(jnp.int32, (rb, cb), 1)
        mask = (row >> 6) == (col >> 4)
        cr_ref[...] = _split3_dot(jnp.where(mask, dr_ref[...], 0.0), f_ref[...])
        ci_ref[...] = -_split3_dot(jnp.where(mask, di_ref[...], 0.0), f_ref[...])

    cspec = pl.BlockSpec((rb, cb), lambda j: (j, 0))
    ospec = pl.BlockSpec((rb, h), lambda j: (j, 0))
    return pl.pallas_call(
        body, name="ssm_cgrad", grid=(nb,),
        out_shape=(jax.ShapeDtypeStruct((gp, h), F32),) * 2,
        in_specs=[cspec, cspec, _full(fold.shape)], out_specs=(ospec, ospec),
        compiler_params=_cparams(("parallel",)),
    )(d_cre, d_cim, fold)


def _ssm_lamgrad(lam_re, lam_im, log_step, abar_re, abar_im, coef_re, coef_im, gc_re, gc_im, ga_re, ga_im):
    g, p = lam_re.shape

    def body(lr_ref, li_ref, ls_ref, ar_ref, ai_ref, cr_ref, ci_ref, gcr_ref, gci_ref, gar_ref, gai_ref,
             dlr_ref, dli_ref, dls_ref):
        lam_raw = lr_ref[...]
        lr = jnp.minimum(lam_raw, LAMBDA_RE_MAX)
        li = li_ref[...]
        st = jnp.exp(ls_ref[...])
        den = lr * lr + li * li
        gcr, gci = gcr_ref[...], gci_ref[...]
        gab_r = gar_ref[...] + (lr * gcr - li * gci) / den
        gab_i = gai_ref[...] + (lr * gci + li * gcr) / den
        cr, ci = cr_ref[...], ci_ref[...]
        wr = -(cr * lr + ci * li) / den
        wi = -(ci * lr - cr * li) / den
        gl_r = wr * gcr + wi * gci
        gl_i = wr * gci - wi * gcr
        ar, ai = ar_ref[...], ai_ref[...]
        gw_r = ar * gab_r + ai * gab_i
        gw_i = ar * gab_i - ai * gab_r
        gl_r = gl_r + st * gw_r
        gl_i = gl_i + st * gw_i
        pass_through = jnp.where(lam_raw < LAMBDA_RE_MAX, 1.0, jnp.where(lam_raw == LAMBDA_RE_MAX, 0.5, 0.0))
        dlr_ref[...] = gl_r * pass_through
        dli_ref[...] = gl_i
        dls_ref[...] = st * jnp.sum(lr * gw_r + li * gw_i, axis=1, keepdims=True)

    sds = jax.ShapeDtypeStruct((g, p), F32)
    return pl.pallas_call(body, name="ssm_lamgrad", out_shape=(sds, sds, jax.ShapeDtypeStruct((g, 1), F32)))(
        lam_re, lam_im, log_step, abar_re, abar_im, coef_re, coef_im, gc_re, gc_im, ga_re, ga_im)


def _adamw_math(w, g, m, v):
    m = ADAM_B1 * m + (1.0 - ADAM_B1) * g
    v = ADAM_B2 * v + (1.0 - ADAM_B2) * (g * g)
    m_hat = m / (1.0 - ADAM_B1 ** ADAM_STEP)
    v_hat = v / (1.0 - ADAM_B2 ** ADAM_STEP)
    delta = -ADAM_LR * (m_hat / (jnp.sqrt(v_hat) + ADAM_EPS) + ADAM_WD * w)
    return delta, m, v


def _adamw_big(p_mine, p_sib, w, m, v, name):
    r, c = w.shape
    rb = 64 if r % 64 == 0 else r

    def body(a_ref, b_ref, w_ref, m_ref, v_ref, g_ref, d_ref, mo_ref, vo_ref):
        g = a_ref[...] + b_ref[...]
        g_ref[...] = g
        d_ref[...], mo_ref[...], vo_ref[...] = _adamw_math(w_ref[...], g, m_ref[...], v_ref[...])

    spec = pl.BlockSpec((rb, c), lambda i: (i, 0))
    sds = jax.ShapeDtypeStruct((r, c), F32)
    return pl.pallas_call(
        body, name=name, grid=(r // rb,), out_shape=(sds,) * 4, in_specs=[spec] * 5, out_specs=(spec,) * 4,
        compiler_params=_cparams(("parallel",)),
    )(p_mine, p_sib, w, m, v)


def _sum_blocks(stack, name):
    n, r, c = stack.shape
    rb = 64 if r % 64 == 0 else r

    def body(s_ref, o_ref):
        acc = s_ref[0].astype(F32)
        for k in range(1, n):
            acc = acc + s_ref[k].astype(F32)
        o_ref[...] = acc

    return pl.pallas_call(
        body, name=name, grid=(r // rb,), out_shape=jax.ShapeDtypeStruct((r, c), F32),
        in_specs=[pl.BlockSpec((n, rb, c), lambda i: (0, i, 0))], out_specs=pl.BlockSpec((rb, c), lambda i: (i, 0)),
        compiler_params=_cparams(("parallel",)),
    )(stack)


def _adamw_ada(c_all, dmod_cols, w, m, v):
    d, n = w.shape
    bn = 512

    def body(c_ref, dm_ref, w_ref, m_ref, v_ref, g_ref, d_ref, mo_ref, vo_ref):
        cc = c_ref[...]
        g = _dot_tn(cc * _sigmoid(cc), dm_ref[...])
        g_ref[...] = g
        d_ref[...], mo_ref[...], vo_ref[...] = _adamw_math(w_ref[...], g, m_ref[...], v_ref[...])

    spec = pl.BlockSpec((d, bn), lambda j: (0, j))
    sds = jax.ShapeDtypeStruct((d, n), F32)
    return pl.pallas_call(
        body, name="adamw_ada", grid=(n // bn,), out_shape=(sds,) * 4,
        in_specs=[_full((N_DEV, d)), pl.BlockSpec((N_DEV, bn), lambda j: (0, j)), spec, spec, spec],
        out_specs=(spec,) * 4, compiler_params=_cparams(("parallel",)),
    )(c_all, dmod_cols, w, m, v)


def _adamw_small(items):
    n = len(items)

    def body(*refs):
        ins, outs = refs[:4 * n], refs[4 * n:]
        for k in range(n):
            w_ref, g_ref, m_ref, v_ref = ins[4 * k:4 * k + 4]
            outs[3 * k][...], outs[3 * k + 1][...], outs[3 * k + 2][...] = _adamw_math(
                w_ref[...], g_ref[...], m_ref[...], v_ref[...])

    flat = [a for it in items for a in it]
    out_shape = tuple(jax.ShapeDtypeStruct(it[0].shape, F32) for it in items for _ in range(3))
    res = pl.pallas_call(body, name="adamw_small", out_shape=out_shape,
                         compiler_params=_cparams(vmem=VMEM_BIG))(*flat)
    return [tuple(res[3 * k:3 * k + 3]) for k in range(n)]


def _group_mean_matrix(n, group):
    idx = np.arange(n) // group
    return (idx[:, None] == idx[None, :]).astype(np.float32) / group


def _fold_matrix(n, period):
    return (np.arange(n)[:, None] % period == np.arange(period)[None, :]).astype(np.float32)


def _rows8(*rows):
    c = rows[0].shape[-1]
    pad = jnp.zeros((SUBLANES - len(rows), c), F32)
    return jnp.concatenate([r.reshape(1, c) for r in rows] + [pad], axis=0)


def _to_rows(a, width):
    flat = a.reshape(-1)
    n = -(-flat.shape[0] // width)
    flat = jnp.pad(flat, (0, n * width - flat.shape[0]))
    return flat.reshape(n, width)


def kernel(x, c, w_ada, b_ada, g_pre_mix, g_post_mix, w_in, ssm_lam_re, ssm_lam_im, ssm_log_step, ssm_b_re, ssm_b_im, ssm_c_re, ssm_c_im, ssm_d, glu_w, glu_b, g_out_ssm, conv_w, g_out_conv, w_out, g_pre_ffn, g_post_ffn, w_up, ffn_conv_w, w_down, loss_target, m_w_ada, m_b_ada, m_g_pre_mix, m_g_post_mix, m_w_in, m_ssm_lam_re, m_ssm_lam_im, m_ssm_log_step, m_ssm_b_re, m_ssm_b_im, m_ssm_c_re, m_ssm_c_im, m_ssm_d, m_glu_w, m_glu_b, m_g_out_ssm, m_conv_w, m_g_out_conv, m_w_out, m_g_pre_ffn, m_g_post_ffn, m_w_up, m_ffn_conv_w, m_w_down, v_w_ada, v_b_ada, v_g_pre_mix, v_g_post_mix, v_w_in, v_ssm_lam_re, v_ssm_lam_im, v_ssm_log_step, v_ssm_b_re, v_ssm_b_im, v_ssm_c_re, v_ssm_c_im, v_ssm_d, v_glu_w, v_glu_b, v_g_out_ssm, v_conv_w, v_g_out_conv, v_w_out, v_g_pre_ffn, v_g_post_ffn, v_w_up, v_ffn_conv_w, v_w_down):
    xs = x[0]
    tgt = loss_target[0]
    t, d = xs.shape
    xi, yi, ci = lax.axis_index("x"), lax.axis_index("y"), lax.axis_index("c")
    chip = 2 * xi + yi
    dev = 2 * chip + ci

    n_groups, n_state = ssm_lam_re.shape[1:]
    n_gch = ssm_b_re.shape[3]
    d_ssm = n_groups * n_gch
    gp = n_groups * n_state
    n_ada = w_ada.shape[2]
    d_ff = w_down.shape[1] * N_CHIPS
    n_upc = w_up.shape[2]

    w_names = ("w_in", "glu_w", "w_out", "w_up", "w_down")
    c_gath, _ = _allgather8(jnp.broadcast_to(c, (SUBLANES, d)), SUBLANES, "gather_c")
    c_all = c_gath.reshape(N_DEV, SUBLANES, d)[:, 0, :]
    b_sh = lax.dynamic_slice(b_ada, (0, chip * n_ada), (1, n_ada))
    mod_sh = _mod_shard(c_all, w_ada[0], b_sh)

    def pad8(a):
        return jnp.concatenate([a, jnp.zeros((SUBLANES - a.shape[0], a.shape[1]), a.dtype)], axis=0)

    w_names = ("mod", "conv_w", "ffn_conv_w", "w_in", "glu_w", "w_out", "w_up", "w_down")
    w_own = [mod_sh, pad8(conv_w[0]), pad8(ffn_conv_w[0])]
    w_own += [w[0].astype(BF16) for w in (w_in, glu_w, w_out, w_up, w_down)]
    w_send, w_recv, w_src, w_land, w_token = _chips_start(
        "weights_start", True, w_own, [_landing(a, chip) for a in w_own])

    def weights(names, after):
        ks = [w_names.index(nm) for nm in names]
        return _chips_wait("weights_wait_" + names[-1], True, [w_send[k] for k in ks], [w_recv[k] for k in ks],
                           [w_src[k] for k in ks], [w_land[k] for k in ks], after)

    lam_re, lam_im = ssm_lam_re[0], ssm_lam_im[0]
    log_step = ssm_log_step[0].reshape(n_groups, 1) + w_token[0:1, 0:1]
    abar_re, abar_im, coef_re, coef_im = _ssm_prep(lam_re, lam_im, log_step)
    a_rows = _rows8(abar_re.reshape(1, gp), abar_im.reshape(1, gp))
    coef_rows = _rows8(coef_re.reshape(1, gp), coef_im.reshape(1, gp))
    bt_re = jnp.tile(ssm_b_re[0].transpose(0, 2, 1).reshape(d_ssm, n_state), (1, n_groups))
    bt_im = jnp.tile(ssm_b_im[0].transpose(0, 2, 1).reshape(d_ssm, n_state), (1, n_groups))
    ct_re = jnp.tile(ssm_c_re[0].transpose(0, 2, 1).reshape(gp, n_gch), (1, n_groups))
    ct_im = jnp.tile(ssm_c_im[0].transpose(0, 2, 1).reshape(gp, n_gch), (1, n_groups))
    bblk_re, bblk_im, cblk_re, cblk_im = _ssm_blocks(bt_re, bt_im, ct_re, ct_im, coef_rows)

    h16 = jnp.asarray(_group_mean_matrix(d_ssm, n_gch), BF16)
    h64 = jnp.asarray(_group_mean_matrix(d_ssm, CONV_HEAD_DIM), BF16)

    g_mod, g_cw, g_fw, w_in_st = weights(("mod", "conv_w", "ffn_conv_w", "w_in"), bblk_re)
    mod_all = g_mod.transpose(1, 0, 2).reshape(N_DEV, N_CHIPS * n_ada)
    mod = lax.dynamic_slice(mod_all, (dev, 0), (1, N_CHIPS * n_ada))
    sh1, sc1, gt1, sh2, sc2, gt2 = [mod[:, k * d:(k + 1) * d] for k in range(6)]
    convw_full = pad8(g_cw[:, :3, :].transpose(1, 0, 2).reshape(3, d_ssm))
    fw_full = pad8(g_fw[:, :3, :].transpose(1, 0, 2).reshape(3, N_CHIPS * n_upc))

    v512 = _rows8(ssm_d, glu_b, g_out_ssm, g_out_conv)
    vec1 =_rows8(g_pre_mix, 1.0 + sc1, sh1)
    vd1 = _rows8(g_post_mix, gt1)
    vec2 = _rows8(g_pre_ffn, 1.0 + sc2, sh2)
    vd2 = _rows8(g_post_ffn, gt2)

    proj, bu_re, bu_im, h1b = _mix_in(xs, vec1, w_in_st, bblk_re, bblk_im)
    s_re, s_im = _scan_fwd(a_rows, bu_re, bu_im)
    g_glu, g_wout = weights(("glu_w", "w_out"), s_re)
    glu_full = g_glu.reshape(d_ssm, d_ssm)
    w_out_full = g_wout.reshape(2 * d_ssm, d)
    y1, o_mix, x2 = _mix_out(xs, proj, s_re, s_im, cblk_re, cblk_im, v512, convw_full, glu_full, h16, h64,
                             w_out_full, vd1)
    (w_up_st,) = weights(("w_up",), x2)
    up, h2b = _ffn_up(x2, vec2, w_up_st)
    (g_wdown,) = weights(("w_down",), up)
    w_down_full = g_wdown.reshape(d_ff, d)
    actb, ddnb, dout, dhid, vp_dn, loss_blk = _ffn_down(up, fw_full, w_down_full, w_down_full.T, x2, tgt, vd2)

    g_names = ("w_down", "w_up", "w_out", "glu_w", "w_in")
    gw_down = _matmul_tn(actb, ddnb, d_ff, d, BF16, "dw_down", bt=1024).reshape(N_CHIPS, d_ff // N_CHIPS, d)
    dx2, dupb, vp_up, df_rows = _ffn_up_bwd(dhid, up, fw_full, x2, dout, vec2, w_up_st)
    gw_up = _matmul_tn(h2b, dupb, d, n_upc, BF16, "dw_up", bt=2048)
    ga_send, ga_recv, ga_src, ga_land, ga_token = _chips_start(
        "grads_start_ffn", False, [gw_down, gw_up],
        [_landing(lax.dynamic_index_in_dim(g, chip, 0, False), chip) for g in (gw_down, gw_up)])
    (dob, ycatb, zb, dqb, dy1, g_re, g_im, dcc, dbg, vp_mo, vp5) = _mix_out_bwd(
        dx2, o_mix, y1, proj, cblk_re, cblk_im, v512, convw_full, glu_full, h16, h64, w_out_full,
        vd1 + ga_token[0:1, 0:1])
    gt_re, gt_im, ga_re8, ga_im8 = _scan_bwd(a_rows, g_re, g_im, s_re, s_im)
    grad_x, dprojb, vp_mi = _mix_in_bwd(gt_re, gt_im, bblk_re, bblk_im, dy1, dcc, dbg, proj, xs, dx2, vec1, v512,
                                        convw_full, w_in_st)
    ssm_u, ssm_s = d_ssm // SSM_SPLIT, gp // SSM_SPLIT
    d_bre = _matmul_tn(proj, gt_re, ssm_u, ssm_s, F32, "d_bre", diag=True, bt=2048)
    d_bim = _matmul_tn(proj, gt_im, ssm_u, ssm_s, F32, "d_bim", diag=True, bt=2048)
    d_cre = _matmul_tn(s_re, dy1, ssm_s, ssm_u, F32, "d_cre", diag=True, bt=2048)
    d_cim = _matmul_tn(s_im, dy1, ssm_s, ssm_u, F32, "d_cim", diag=True, bt=2048)
    d_bre, d_bim = d_bre.reshape(d_ssm, ssm_s), d_bim.reshape(d_ssm, ssm_s)
    d_cre, d_cim = d_cre.reshape(gp, ssm_u), d_cim.reshape(gp, ssm_u)

    fold_b = jnp.asarray(_fold_matrix(ssm_s, n_state), BF16)
    fold_c = jnp.asarray(_fold_matrix(ssm_u, n_gch), BF16)
    db_re_f, db_im_f, gc_rows = _ssm_bgrad(d_bre, d_bim, bt_re, bt_im, coef_rows, fold_b)
    dc_re_f, dc_im_f = _ssm_cgrad(d_cre, d_cim, fold_c)
    ga_sum = _ga_rowsum(ga_re8, ga_im8)
    g_lam_re, g_lam_im, g_log_step = _ssm_lamgrad(
        lam_re, lam_im, log_step, abar_re, abar_im, coef_re, coef_im,
        gc_rows[0].reshape(n_groups, n_state), gc_rows[1].reshape(n_groups, n_state),
        ga_sum[0].reshape(n_groups, n_state), ga_sum[1].reshape(n_groups, n_state))
    g_b_re = db_re_f.reshape(n_groups, n_gch, n_state).transpose(0, 2, 1)
    g_b_im = db_im_f.reshape(n_groups, n_gch, n_state).transpose(0, 2, 1)
    g_c_re = dc_re_f.reshape(n_groups, n_state, n_gch).transpose(0, 2, 1)
    g_c_im = dc_im_f.reshape(n_groups, n_state, n_gch).transpose(0, 2, 1)

    dmod = jnp.concatenate([vp_mi[0:1], vp_mi[1:2], vp_mo[0:1], vp_up[0:1], vp_up[1:2], vp_dn[0:1]], axis=1)
    small = [
        ("g_pre_mix", vp_mi[2:3]), ("g_post_mix", vp_mo[1:2]), ("g_pre_ffn", vp_up[2:3]), ("g_post_ffn", vp_dn[1:2]),
        ("ssm_lam_re", g_lam_re), ("ssm_lam_im", g_lam_im), ("ssm_log_step", g_log_step),
        ("ssm_b_re", g_b_re), ("ssm_b_im", g_b_im), ("ssm_c_re", g_c_re), ("ssm_c_im", g_c_im),
        ("ssm_d", vp5[3:4]), ("glu_b", vp5[2:3]), ("g_out_ssm", vp5[0:1]), ("g_out_conv", vp5[1:2]),
        ("conv_w", vp5[4:7]), ("ffn_conv_w", df_rows[0:3]), ("loss", loss_blk[0:1, 0:1]),
    ]
    packed, offsets, row = [], {}, 0
    for name, a in small:
        r = _to_rows(a, d)
        offsets[name] = (row, a.shape)
        packed.append(r)
        row += r.shape[0]
    n_small = -(-row // SUBLANES) * SUBLANES
    packed.append(jnp.zeros((n_small - row, d), F32))
    packed.append(pad8(dmod.reshape(6, d)))
    pack = jnp.concatenate(packed, axis=0)
    gath, sums = _allgather8(pack, n_small + SUBLANES, "reduce_small")
    dmod_all = gath.reshape(N_DEV, n_small + SUBLANES, d)[:, n_small:n_small + 6, :].reshape(N_DEV, 6 * d)
    g_b_ada = sums[n_small:n_small + 6].reshape(1, 6 * d)

    def unpack(name):
        r0, shape = offsets[name]
        size = math.prod(shape)
        nrow = -(-size // d)
        return sums[r0:r0 + nrow].reshape(-1)[:size].reshape(shape)

    gw_out = _matmul_tn(ycatb, dob, 2 * d_ssm, d, BF16, "dw_out", bt=2048, after=sums)
    gw_out = gw_out.reshape(N_CHIPS, 2 * d_ssm // N_CHIPS, d)
    gw_glu = _matmul_tn(zb, dqb, d_ssm, d_ssm, BF16, "dw_glu", bt=2048).reshape(N_CHIPS, d_ssm // N_CHIPS, d_ssm)
    gw_in = _matmul_tn(h1b, dprojb, d, w_in.shape[2], BF16, "dw_in", bt=2048)
    gb_send, gb_recv, gb_src, gb_land, gb_token = _chips_start(
        "grads_start_mix", False, [gw_out, gw_glu, gw_in],
        [_landing(lax.dynamic_index_in_dim(g, chip, 0, False), chip) for g in (gw_out, gw_glu, gw_in)])

    dmod_cols = lax.dynamic_slice(dmod_all, (0, chip * n_ada), (N_DEV, n_ada)) + gb_token[0:1, 0:1]
    ada = _adamw_ada(c_all, dmod_cols, w_ada[0], m_w_ada[0], v_w_ada[0])

    def finish(names, landed):
        partial = [_sum_blocks(s, "sum_" + nm) for s, nm in zip(landed, names)]
        theirs = _swap_sibling(partial, "swap_" + names[0])
        done = {}
        for nm, pm, ps in zip(names, partial, theirs):
            w_, m_, v_ = big_params[nm]
            done[nm] = _adamw_big(pm, ps, w_[0], m_[0], v_[0], "adamw_" + nm)
        return done

    big_params = {"w_down": (w_down, m_w_down, v_w_down), "w_up": (w_up, m_w_up, v_w_up),
                  "w_out": (w_out, m_w_out, v_w_out), "glu_w": (glu_w, m_glu_w, v_glu_w),
                  "w_in": (w_in, m_w_in, v_w_in)}
    big = finish(("w_down", "w_up"), _chips_wait("grads_wait_ffn", False, ga_send, ga_recv, ga_src, ga_land, ada[0]))
    big.update(finish(("w_out", "glu_w", "w_in"),
                      _chips_wait("grads_wait_mix", False, gb_send, gb_recv, gb_src, gb_land, big["w_up"][0])))

    g_small = {name: unpack(name) for name, _ in small}
    g_small["b_ada"] = g_b_ada
    g_small["conv_w"] = lax.dynamic_slice(g_small["conv_w"], (0, chip * conv_w.shape[2]), (3, conv_w.shape[2]))
    g_small["ffn_conv_w"] = lax.dynamic_slice(g_small["ffn_conv_w"], (0, chip * n_upc), (3, n_upc))
    g_small["ssm_log_step"] = g_small["ssm_log_step"].reshape(1, n_groups)
    small_params = {
        "b_ada": (b_ada, m_b_ada, v_b_ada), "g_pre_mix": (g_pre_mix, m_g_pre_mix, v_g_pre_mix),
        "g_post_mix": (g_post_mix, m_g_post_mix, v_g_post_mix), "ssm_lam_re": (ssm_lam_re, m_ssm_lam_re, v_ssm_lam_re),
        "ssm_lam_im": (ssm_lam_im, m_ssm_lam_im, v_ssm_lam_im),
        "ssm_log_step": (ssm_log_step, m_ssm_log_step, v_ssm_log_step),
        "ssm_b_re": (ssm_b_re, m_ssm_b_re, v_ssm_b_re), "ssm_b_im": (ssm_b_im, m_ssm_b_im, v_ssm_b_im),
        "ssm_c_re": (ssm_c_re, m_ssm_c_re, v_ssm_c_re), "ssm_c_im": (ssm_c_im, m_ssm_c_im, v_ssm_c_im),
        "ssm_d": (ssm_d, m_ssm_d, v_ssm_d), "glu_b": (glu_b, m_glu_b, v_glu_b),
        "g_out_ssm": (g_out_ssm, m_g_out_ssm, v_g_out_ssm), "conv_w": (conv_w, m_conv_w, v_conv_w),
        "g_out_conv": (g_out_conv, m_g_out_conv, v_g_out_conv), "g_pre_ffn": (g_pre_ffn, m_g_pre_ffn, v_g_pre_ffn),
        "g_post_ffn": (g_post_ffn, m_g_post_ffn, v_g_post_ffn),
        "ffn_conv_w": (ffn_conv_w, m_ffn_conv_w, v_ffn_conv_w),
    }

    def natural(a):
        return a[0] if a.ndim > 2 else a

    names = list(small_params)
    items = []
    for nm in names:
        w_, m_, v_ = small_params[nm]
        items.append((natural(w_), g_small[nm].reshape(natural(w_).shape), natural(m_), natural(v_)))
    upd = _adamw_small(items)
    small_out = {}
    for nm, (dl, mo, vo) in zip(names, upd):
        shp = small_params[nm][0].shape
        small_out[nm] = (g_small[nm].reshape(shp), dl.reshape(shp), mo.reshape(shp), vo.reshape(shp))

    loss = g_small["loss"][0, 0]

    order = ["w_ada", "b_ada", "g_pre_mix", "g_post_mix", "w_in", "ssm_lam_re", "ssm_lam_im", "ssm_log_step",
             "ssm_b_re", "ssm_b_im", "ssm_c_re", "ssm_c_im", "ssm_d", "glu_w", "glu_b", "g_out_ssm", "conv_w",
             "g_out_conv", "w_out", "g_pre_ffn", "g_post_ffn", "w_up", "ffn_conv_w", "w_down"]
    results = {"w_ada": tuple(a[None] for a in ada)}
    for nm in big:
        results[nm] = tuple(a[None] for a in big[nm])
    results.update(small_out)
    outs = [loss, grad_x[None]]
    for k in range(4):
        outs += [results[nm][k] for nm in order]
    return tuple(outs)


def _ga_rowsum(ga_re8, ga_im8):
    n = ga_re8.shape[1]

    def body(r_ref, i_ref, o_ref):
        o_ref[...] = jnp.zeros(o_ref.shape, F32)
        o_ref[0:1, :] = _colsum(r_ref[...])
        o_ref[1:2, :] = _colsum(i_ref[...])

    return pl.pallas_call(body, name="ga_rowsum", out_shape=jax.ShapeDtypeStruct((SUBLANES, n), F32))(ga_re8, ga_im8)
```

```python
import functools
import math

import jax
import jax.numpy as jnp
import numpy as np
from jax import lax
from jax.experimental import pallas as pl
from jax.experimental.pallas import tpu as pltpu

F32 = jnp.float32
BF16 = jnp.bfloat16
MESH = pl.DeviceIdType.MESH

EPS = 1e-6
LAMBDA_RE_MAX = -1e-4
ADAM_LR = 0.001
ADAM_B1 = 0.9
ADAM_B2 = 0.999
ADAM_EPS = 1e-08
ADAM_WD = 0.01
ADAM_STEP = 10

SUBLANES = 8
BF16_ROWS = 16
N_CHIPS = 4
N_DEV = 8
CONV_HEAD_DIM = 64
VMEM_BIG = 56 * 1024 * 1024
VMEM_MID = 40 * 1024 * 1024

TB_MIX = 256
TB_FFN = 256
TB_SCAN = 1024
W_SCAN = 256
SSM_SPLIT = 4
CW_FFN = 256
SCAN_UNROLL = 4
TB_TN = 512


def _cparams(sem=None, vmem=None):
    kw = {}
    if sem is not None:
        kw["dimension_semantics"] = sem
    if vmem is not None:
        kw["vmem_limit_bytes"] = vmem
    return pltpu.CompilerParams(**kw)


def _blk(t, pref):
    return pref if t % pref == 0 else t


def _dot(a, b):
    return jnp.dot(a.astype(BF16), b.astype(BF16), preferred_element_type=F32)


def _dot_nt(a, b):
    return lax.dot_general(a.astype(BF16), b.astype(BF16), (((1,), (1,)), ((), ())),
                           preferred_element_type=F32)


def _dot_tn(a, b):
    return lax.dot_general(a.astype(BF16), b.astype(BF16), (((0,), (0,)), ((), ())),
                           preferred_element_type=F32)


def _sigmoid(x):
    return 0.5 * jnp.tanh(0.5 * x) + 0.5


_GELU_K = math.sqrt(2.0 / math.pi)
_GELU_C = 0.044715


def _gelu(x):
    th = jnp.tanh(_GELU_K * (x + _GELU_C * x * x * x))
    return 0.5 * x * (1.0 + th)


def _gelu_grad(x):
    x2 = x * x
    th = jnp.tanh(_GELU_K * (x + _GELU_C * x2 * x))
    return 0.5 * (1.0 + th) + 0.5 * x * (1.0 - th * th) * _GELU_K * (1.0 + 3.0 * _GELU_C * x2)


def _rowmean(x):
    return jnp.mean(x, axis=-1, keepdims=True)


def _colsum(x):
    return jnp.sum(x, axis=0, keepdims=True)


def _split_dot(x, m):
    hi = x.astype(BF16)
    lo = (x - hi.astype(F32)).astype(BF16)
    return (jnp.dot(hi, m, preferred_element_type=F32) + jnp.dot(lo, m, preferred_element_type=F32))


def _split3_dot(x, m):
    hi = x.astype(BF16)
    r1 = x - hi.astype(F32)
    mid = r1.astype(BF16)
    lo = (r1 - mid.astype(F32)).astype(BF16)
    return (jnp.dot(hi, m, preferred_element_type=F32) + jnp.dot(mid, m, preferred_element_type=F32)
            + jnp.dot(lo, m, preferred_element_type=F32))


def _shift_down(x, halo, k):
    r = pltpu.roll(x, k, 0)
    row = lax.broadcasted_iota(jnp.int32, x.shape, 0)
    for j in range(k):
        r = jnp.where(row == j, halo[SUBLANES - k + j:SUBLANES - k + j + 1, :], r)
    return r


def _shift_up(x, halo, k):
    n = x.shape[0]
    r = pltpu.roll(x, n - k, 0)
    row = lax.broadcasted_iota(jnp.int32, x.shape, 0)
    for j in range(k):
        r = jnp.where(row == n - k + j, halo[j:j + 1, :], r)
    return r


def _acc_rows(ref, first, rows):
    @pl.when(first)
    def _():
        ref[...] = jnp.zeros(ref.shape, ref.dtype)
    for j, r in enumerate(rows):
        ref[j:j + 1, :] += r


def _rows(tb, c, col=0):
    return pl.BlockSpec((tb, c), lambda i, col=col: (i, col))


def _full(shape):
    nd = len(shape)
    return pl.BlockSpec(shape, lambda i, nd=nd: (0,) * nd)


def _resident(shape):
    nd = len(shape)
    return pl.BlockSpec(shape, lambda i, nd=nd: (0,) * nd, pipeline_mode=pl.Buffered(1))


def _halo_prev(tb, c, col=0):
    per = tb // SUBLANES
    return pl.BlockSpec((SUBLANES, c), lambda i, col=col: (jnp.maximum(i * per - 1, 0), col))


def _halo_next(tb, c, t, col=0, rows=SUBLANES):
    per = tb // rows
    last = t // rows - 1
    return pl.BlockSpec((rows, c), lambda i, col=col: (jnp.minimum((i + 1) * per, last), col))


def _mesh_pos():
    return lax.axis_index("x"), lax.axis_index("y"), lax.axis_index("c")


def _allgather8(x_pad, n_sum, name):
    m_per, n = x_pad.shape

    def body(x_ref, out_ref, sum_ref, send_sems, recv_sems, local_sem):
        x, y, c = _mesh_pos()
        me, sibling = (x, y, c), (x, y, 1 - c)
        chips = [(1 - x, y), (x, 1 - y), (1 - x, 1 - y)]

        def rows(px, py, pc):
            return out_ref.at[pl.ds((4 * px + 2 * py + pc) * m_per, m_per), :]

        def copy(k, block, to, src=None):
            return pltpu.make_async_remote_copy(
                src_ref=rows(*block) if src is None else src, dst_ref=rows(*block),
                send_sem=send_sems.at[k], recv_sem=recv_sems.at[k], device_id=to, device_id_type=MESH)

        mine = pltpu.make_async_copy(x_ref, rows(*me), local_sem)
        mine.start()
        first = [copy(0, me, sibling, src=x_ref)]
        first += [copy(1 + j, me, (*chip, c), src=x_ref) for j, chip in enumerate(chips)]
        for cp in first:
            cp.start()
        passed = [copy(4 + j, (*chip, c), sibling) for j, chip in enumerate(chips)]
        for j, chip in enumerate(chips):
            copy(1 + j, (*chip, c), me).wait_recv()
            passed[j].start()
        copy(0, sibling, me).wait_recv()
        for j, chip in enumerate(chips):
            copy(4 + j, (*chip, 1 - c), me).wait_recv()
        for cp in first + passed:
            cp.wait_send()
        mine.wait()
        acc = out_ref[0:n_sum, :]
        for k in range(1, N_DEV):
            acc = acc + out_ref[k * m_per:k * m_per + n_sum, :]
        sum_ref[...] = acc

    return pl.pallas_call(
        body, name=name,
        out_shape=(jax.ShapeDtypeStruct((N_DEV * m_per, n), F32), jax.ShapeDtypeStruct((n_sum, n), F32)),
        in_specs=[pl.BlockSpec(memory_space=pltpu.VMEM)],
        out_specs=(pl.BlockSpec(memory_space=pltpu.VMEM), pl.BlockSpec(memory_space=pltpu.VMEM)),
        scratch_shapes=[pltpu.SemaphoreType.DMA((7,)), pltpu.SemaphoreType.DMA((7,)), pltpu.SemaphoreType.DMA],
        compiler_params=_cparams(vmem=VMEM_MID),
    )(x_pad)


_HBM = pl.BlockSpec(memory_space=pltpu.HBM)
_SEM = pl.BlockSpec(memory_space=pltpu.SEMAPHORE)
_EFFECT = pltpu.SideEffectType.DATAFLOW_SIDE_EFFECTING


def _chip_copy(gather, src_ref, land_ref, send, recv, j, arrival):
    x, y, c = _mesh_pos()
    peer = [(1 - x, y), (x, 1 - y), (1 - x, 1 - y)][j]
    peer_chip = 2 * peer[0] + peer[1]
    my_chip = 2 * x + y
    return pltpu.make_async_remote_copy(
        src_ref=src_ref if gather else src_ref.at[peer_chip],
        dst_ref=land_ref.at[peer_chip if arrival else my_chip],
        send_sem=send.at[j], recv_sem=recv.at[j], device_id=(*peer, c), device_id_type=MESH)


def _chips_start(name, gather, srcs, lands):
    n = len(srcs)

    def body(*refs):
        src_refs, land_refs = refs[:n], refs[n:2 * n]
        outs = refs[2 * n:]
        sends, recvs, token = outs[:n], outs[n:2 * n], outs[-1]
        for k in range(n):
            for j in range(3):
                _chip_copy(gather, src_refs[k], land_refs[k], sends[k], recvs[k], j, False).start()
        token[...] = jnp.zeros(token.shape, F32)

    sem = pltpu.SemaphoreType.DMA((3,))
    thru = tuple(pltpu.HBM(a.shape, a.dtype) for a in list(srcs) + list(lands))
    res = pl.pallas_call(
        body, name=name,
        out_shape=(sem,) * (2 * n) + thru + (jax.ShapeDtypeStruct((SUBLANES, 128), F32),),
        in_specs=[_HBM] * (2 * n),
        out_specs=(_SEM,) * (2 * n) + (_HBM,) * (2 * n) + (pl.BlockSpec(memory_space=pltpu.VMEM),),
        input_output_aliases={k: 2 * n + k for k in range(2 * n)},
        compiler_params=pltpu.CompilerParams(has_side_effects=_EFFECT),
    )(*[pltpu.with_memory_space_constraint(a, pltpu.HBM) for a in list(srcs) + list(lands)])
    return res[:n], res[n:2 * n], res[2 * n:3 * n], res[3 * n:4 * n], res[-1]


def _chips_wait(name, gather, sends, recvs, srcs, lands, after):
    n = len(srcs)

    def body(*refs):
        src_refs, land_refs = refs[:n], refs[n:2 * n]
        sends_, recvs_ = refs[2 * n:3 * n], refs[3 * n:4 * n]
        for k in range(n):
            for j in range(3):
                cp = _chip_copy(gather, src_refs[k], land_refs[k], sends_[k], recvs_[k], j, True)
                cp.wait_send()
                cp.wait_recv()

    thru = tuple(pltpu.HBM(a.shape, a.dtype) for a in list(srcs) + list(lands))
    res = pl.pallas_call(
        body, name=name, out_shape=thru,
        in_specs=[_HBM] * (2 * n) + [_SEM] * (2 * n) + [pl.BlockSpec(memory_space=pl.ANY)],
        out_specs=(_HBM,) * (2 * n),
        input_output_aliases={k: k for k in range(2 * n)},
        compiler_params=pltpu.CompilerParams(has_side_effects=_EFFECT),
    )(*srcs, *lands, *sends, *recvs, after)
    return res[n:]


def _landing(own, chip):
    zone = lax.empty((N_CHIPS,) + own.shape, own.dtype)
    return lax.dynamic_update_slice(zone, own[None], (chip,) + (0,) * own.ndim)


def _swap_sibling(arrs, name):
    n_arr = len(arrs)

    def body(*refs):
        ins, outs = refs[:n_arr], refs[n_arr:2 * n_arr]
        send_sems, recv_sems = refs[2 * n_arr:]
        x, y, c = _mesh_pos()
        copies = [pltpu.make_async_remote_copy(
            src_ref=ins[n], dst_ref=outs[n], send_sem=send_sems.at[n], recv_sem=recv_sems.at[n],
            device_id=(x, y, 1 - c), device_id_type=MESH) for n in range(n_arr)]
        for cp in copies:
            cp.start()
        for cp in copies:
            cp.wait()

    any_spec = pl.BlockSpec(memory_space=pl.ANY)
    return pl.pallas_call(
        body, name=name,
        out_shape=tuple(jax.ShapeDtypeStruct(a.shape, a.dtype) for a in arrs),
        in_specs=[any_spec] * n_arr, out_specs=tuple([any_spec] * n_arr),
        scratch_shapes=[pltpu.SemaphoreType.DMA((n_arr,)), pltpu.SemaphoreType.DMA((n_arr,))],
    )(*arrs)


def _mod_shard(c_all, w_ada_sh, b_sh):
    d, n = w_ada_sh.shape
    bn = 512

    def body(c_ref, w_ref, b_ref, o_ref):
        cc = c_ref[...]
        ca = cc * _sigmoid(cc)
        o_ref[...] = _dot(ca, w_ref[...]) + b_ref[...]

    return pl.pallas_call(
        body, name="mod_shard", grid=(n // bn,),
        out_shape=jax.ShapeDtypeStruct((N_DEV, n), F32),
        in_specs=[_full((N_DEV, d)), pl.BlockSpec((d, bn), lambda j: (0, j)), pl.BlockSpec((1, bn), lambda j: (0, j))],
        out_specs=pl.BlockSpec((N_DEV, bn), lambda j: (0, j)),
        compiler_params=_cparams(("parallel",)),
    )(c_all, w_ada_sh, b_sh)


def _ssm_prep(lam_re, lam_im, log_step):
    g, p = lam_re.shape

    def body(lr_ref, li_ref, ls_ref, ar_ref, ai_ref, cr_ref, ci_ref):
        lr = jnp.minimum(lr_ref[...], LAMBDA_RE_MAX)
        li = li_ref[...]
        st = jnp.exp(ls_ref[...])
        mag = jnp.exp(lr * st)
        ar = mag * jnp.cos(li * st)
        ai = mag * jnp.sin(li * st)
        den = lr * lr + li * li
        nr = ar - 1.0
        ar_ref[...] = ar
        ai_ref[...] = ai
        cr_ref[...] = (nr * lr + ai * li) / den
        ci_ref[...] = (ai * lr - nr * li) / den

    sds = jax.ShapeDtypeStruct((g, p), F32)
    return pl.pallas_call(body, name="ssm_prep", out_shape=(sds,) * 4)(lam_re, lam_im, log_step)


def _ssm_blocks(bt_re, bt_im, ct_re, ct_im, coef_rows):
    gh, gp = bt_re.shape
    nb = 4
    cb, rb = gp // nb, gp // nb

    def body(btr, bti, ctr, cti, cf, bre_o, bim_o, cre_o, cim_o):
        j = pl.program_id(0)
        row = lax.broadcasted_iota(jnp.int32, (gh, cb), 0)
        col = lax.broadcasted_iota(jnp.int32, (gh, cb), 1) + j * cb
        mask = (row >> 4) == (col >> 6)
        cr, ci = cf[0:1, :], cf[1:2, :]
        br, bi = btr[...], bti[...]
        bre_o[...] = jnp.where(mask, br * cr - bi * ci, 0.0).astype(BF16)
        bim_o[...] = jnp.where(mask, br * ci + bi * cr, 0.0).astype(BF16)
        row2 = lax.broadcasted_iota(jnp.int32, (rb, gh), 0) + j * rb
        col2 = lax.broadcasted_iota(jnp.int32, (rb, gh), 1)
        mask2 = (row2 >> 6) == (col2 >> 4)
        cre_o[...] = jnp.where(mask2, ctr[...], 0.0).astype(BF16)
        cim_o[...] = jnp.where(mask2, cti[...], 0.0).astype(BF16)

    bspec = pl.BlockSpec((gh, cb), lambda j: (0, j))
    cspec = pl.BlockSpec((rb, gh), lambda j: (j, 0))
    return pl.pallas_call(
        body, name="ssm_blocks", grid=(nb,),
        out_shape=(jax.ShapeDtypeStruct((gh, gp), BF16),) * 2 + (jax.ShapeDtypeStruct((gp, gh), BF16),) * 2,
        in_specs=[bspec, bspec, cspec, cspec, pl.BlockSpec((SUBLANES, cb), lambda j: (0, j))],
        out_specs=(bspec, bspec, cspec, cspec),
        compiler_params=_cparams(("parallel",)),
    )(bt_re, bt_im, ct_re, ct_im, coef_rows)


def _scan_consts(a_ref, reverse):
    w = a_ref.shape[1]
    ar1 = a_ref[0:1, :]
    ai1 = a_ref[1:2, :]
    if reverse:
        ai1 = -ai1
    pr, pi = [ar1], [ai1]
    for _ in range(1, SUBLANES):
        nr = pr[-1] * ar1 - pi[-1] * ai1
        ni = pr[-1] * ai1 + pi[-1] * ar1
        pr.append(nr)
        pi.append(ni)
    row = lax.broadcasted_iota(jnp.int32, (SUBLANES, w), 0)
    dist = (SUBLANES - 1 - row) if reverse else row

    def pick(vals):
        out = jnp.broadcast_to(vals[SUBLANES - 1], (SUBLANES, w))
        for r in range(SUBLANES - 1):
            out = jnp.where(dist == r, vals[r], out)
        return out

    p_r, p_i = pick(pr), pick(pi)
    steps = []
    for k in (1, 2, 4):
        steps.append((k, jnp.where(dist >= k, pr[k - 1], 0.0), jnp.where(dist >= k, pi[k - 1], 0.0)))
    a8 = (jnp.broadcast_to(pr[SUBLANES - 1], (SUBLANES, w)), jnp.broadcast_to(pi[SUBLANES - 1], (SUBLANES, w)))
    return row, p_r, p_i, steps, a8


def _scan_tile(xr, xi, cr, ci, consts, reverse):
    row, p_r, p_i, steps, (a8r, a8i) = consts
    for k, s_r, s_i in steps:
        sh = (SUBLANES - k) if reverse else k
        qr = pltpu.roll(xr, sh, 0)
        qi = pltpu.roll(xi, sh, 0)
        xr, xi = xr + s_r * qr - s_i * qi, xi + s_r * qi + s_i * qr
    outr = xr + p_r * cr - p_i * ci
    outi = xi + p_r * ci + p_i * cr
    e = 0 if reverse else SUBLANES - 1
    er = jnp.broadcast_to(xr[e:e + 1, :], xr.shape)
    ei = jnp.broadcast_to(xi[e:e + 1, :], xi.shape)
    return outr, outi, er + a8r * cr - a8i * ci, ei + a8r * ci + a8i * cr


def _scan_fwd(a_rows, bu_re, bu_im):
    t, n = bu_re.shape
    tb, w = _blk(t, TB_SCAN), W_SCAN
    ntile = tb // SUBLANES

    def body(a_ref, br_ref, bi_ref, sr_ref, si_ref, car, cai):
        @pl.when(pl.program_id(1) == 0)
        def _():
            car[...] = jnp.zeros(car.shape, F32)
            cai[...] = jnp.zeros(cai.shape, F32)
        consts = _scan_consts(a_ref, False)

        def pair(i, carry):
            o = pl.multiple_of(i * BF16_ROWS, BF16_ROWS)
            outs = []
            for h in range(2):
                rows = pl.ds(o + h * SUBLANES, SUBLANES)
                outr, outi, ncr, nci = _scan_tile(br_ref[rows, :], bi_ref[rows, :], carry[0], carry[1], consts, False)
                outs.append((outr, outi))
                carry = (ncr, nci)
            sr_ref[pl.ds(o, BF16_ROWS), :] = jnp.concatenate([outs[0][0], outs[1][0]], axis=0).astype(BF16)
            si_ref[pl.ds(o, BF16_ROWS), :] = jnp.concatenate([outs[0][1], outs[1][1]], axis=0).astype(BF16)
            return carry

        def pairs(i, carry):
            for s in range(SCAN_UNROLL // 2):
                carry = pair(i * (SCAN_UNROLL // 2) + s, carry)
            return carry

        cr, ci = lax.fori_loop(0, ntile // SCAN_UNROLL, pairs, (car[...], cai[...]))
        car[...] = cr
        cai[...] = ci

    spec = pl.BlockSpec((tb, w), lambda s, k: (k, s))
    sds = jax.ShapeDtypeStruct((t, n), BF16)
    return pl.pallas_call(
        body, name="scan_fwd", grid=(n // w, t // tb), out_shape=(sds, sds),
        in_specs=[pl.BlockSpec((SUBLANES, w), lambda s, k: (0, s)), spec, spec], out_specs=(spec, spec),
        scratch_shapes=[pltpu.VMEM((SUBLANES, w), F32), pltpu.VMEM((SUBLANES, w), F32)],
        compiler_params=_cparams(("parallel", "arbitrary"), VMEM_MID),
    )(a_rows, bu_re, bu_im)


def _scan_bwd(a_rows, g_re, g_im, s_re, s_im):
    t, n = g_re.shape
    tb, w = _blk(t, TB_SCAN), W_SCAN
    ntile = tb // SUBLANES
    npair = tb // BF16_ROWS
    nt = t // tb

    def body(a_ref, gr_ref, gi_ref, sr_ref, si_ref, or_ref, oi_ref, gar_ref, gai_ref, car, cai):
        @pl.when(pl.program_id(1) == 0)
        def _():
            car[...] = jnp.zeros(car.shape, F32)
            cai[...] = jnp.zeros(cai.shape, F32)
            gar_ref[...] = jnp.zeros(gar_ref.shape, F32)
            gai_ref[...] = jnp.zeros(gai_ref.shape, F32)
        consts = _scan_consts(a_ref, True)
        row = consts[0]

        def pair(i, carry):
            cr, ci, accr, acci = carry
            o = pl.multiple_of((npair - 1 - i) * BF16_ROWS, BF16_ROWS)
            s_r = sr_ref[pl.ds(o, BF16_ROWS), :].astype(F32)
            s_i = si_ref[pl.ds(o, BF16_ROWS), :].astype(F32)
            outs = [None, None]
            for h in (1, 0):
                rows = pl.ds(o + h * SUBLANES, SUBLANES)
                outr, outi, ncr, nci = _scan_tile(gr_ref[rows, :], gi_ref[rows, :], cr, ci, consts, True)
                outs[h] = (outr, outi)
                gnr = jnp.where(row == SUBLANES - 1, cr, pltpu.roll(outr, SUBLANES - 1, 0))
                gni = jnp.where(row == SUBLANES - 1, ci, pltpu.roll(outi, SUBLANES - 1, 0))
                sr = s_r[h * SUBLANES:(h + 1) * SUBLANES, :]
                si = s_i[h * SUBLANES:(h + 1) * SUBLANES, :]
                accr, acci = accr + sr * gnr + si * gni, acci + sr * gni - si * gnr
                cr, ci = ncr, nci
            or_ref[pl.ds(o, BF16_ROWS), :] = jnp.concatenate([outs[0][0], outs[1][0]], axis=0).astype(BF16)
            oi_ref[pl.ds(o, BF16_ROWS), :] = jnp.concatenate([outs[0][1], outs[1][1]], axis=0).astype(BF16)
            return cr, ci, accr, acci

        def pairs(i, carry):
            for s in range(SCAN_UNROLL // 2):
                carry = pair(i * (SCAN_UNROLL // 2) + s, carry)
            return carry

        cr, ci, accr, acci = lax.fori_loop(0, ntile // SCAN_UNROLL, pairs,
                                           (car[...], cai[...], gar_ref[...], gai_ref[...]))
        car[...] = cr
        cai[...] = ci
        gar_ref[...] = accr
        gai_ref[...] = acci

    spec = pl.BlockSpec((tb, w), lambda s, k: (nt - 1 - k, s))
    aspec = pl.BlockSpec((SUBLANES, w), lambda s, k: (0, s))
    sds = jax.ShapeDtypeStruct((t, n), BF16)
    asds = jax.ShapeDtypeStruct((SUBLANES, n), F32)
    return pl.pallas_call(
        body, name="scan_bwd", grid=(n // w, nt), out_shape=(sds, sds, asds, asds),
        in_specs=[aspec, spec, spec, spec, spec], out_specs=(spec, spec, aspec, aspec),
        scratch_shapes=[pltpu.VMEM((SUBLANES, w), F32), pltpu.VMEM((SUBLANES, w), F32)],
        compiler_params=_cparams(("parallel", "arbitrary"), VMEM_MID),
    )(a_rows, g_re, g_im, s_re, s_im)


def _mix_in(x, vec, w_in_st, b_re, b_im):
    t, d = x.shape
    ns, _, nc = w_in_st.shape
    dssm, nstate = b_re.shape
    du, ds = dssm // SSM_SPLIT, nstate // SSM_SPLIT
    tb = _blk(t, TB_MIX)

    def body(x_ref, vec_ref, w_ref, bre_ref, bim_ref, proj_ref, bur_ref, bui_ref, h1_ref):
        xv = x_ref[...]
        r = lax.rsqrt(_rowmean(xv * xv) + EPS)
        h = xv * r * vec_ref[0:1, :] * vec_ref[1:2, :] + vec_ref[2:3, :]
        hb = h.astype(BF16)
        h1_ref[...] = hb
        u = None
        for j in range(ns):
            pj = jnp.dot(hb, w_ref[j], preferred_element_type=F32)
            proj_ref[:, j * nc:(j + 1) * nc] = pj
            if j == 0:
                u = pj
        ub = u.astype(BF16)
        for q in range(SSM_SPLIT):
            rq, cq = slice(q * du, (q + 1) * du), slice(q * ds, (q + 1) * ds)
            bur_ref[:, cq] = jnp.dot(ub[:, rq], bre_ref[rq, cq], preferred_element_type=F32)
            bui_ref[:, cq] = jnp.dot(ub[:, rq], bim_ref[rq, cq], preferred_element_type=F32)

    return pl.pallas_call(
        body, name="mix_in", grid=(t // tb,),
        out_shape=(jax.ShapeDtypeStruct((t, ns * nc), F32), jax.ShapeDtypeStruct((t, nstate), F32),
                   jax.ShapeDtypeStruct((t, nstate), F32), jax.ShapeDtypeStruct((t, d), BF16)),
        in_specs=[_rows(tb, d), _full((SUBLANES, d)), _full(w_in_st.shape), _full(b_re.shape), _full(b_im.shape)],
        out_specs=(_rows(tb, ns * nc), _rows(tb, nstate), _rows(tb, nstate), _rows(tb, d)),
        compiler_params=_cparams(("parallel",), VMEM_BIG),
    )(x, vec, w_in_st, b_re, b_im)


def _head_ms(y, h_ref):
    return _split_dot(y * y, h_ref[...])


def _conv3(x, halo, w_ref):
    return w_ref[0:1, :] * _shift_down(x, halo, 2) + w_ref[1:2, :] * _shift_down(x, halo, 1) + w_ref[2:3, :] * x


def _mix_out(x, proj, s_re, s_im, c_re, c_im, v512, convw, glu_w, h16, h64, w_out, vd):
    t, d = x.shape
    dh = c_re.shape[1]
    nstate = s_re.shape[1]
    du, ds = dh // SSM_SPLIT, nstate // SSM_SPLIT
    tb = _blk(t, TB_MIX)

    def body(x_ref, u_ref, bg_ref, cg_ref, v_ref, cgh_ref, vh_ref, sr_ref, si_ref, cre_ref, cim_ref, p_ref,
             cw_ref, gw_ref, h16_ref, h64_ref, wo_ref, vd_ref, y1_ref, o_ref, x2_ref):
        i = pl.program_id(0)
        u = u_ref[...]
        ys = []
        for q in range(SSM_SPLIT):
            rq, cq = slice(q * ds, (q + 1) * ds), slice(q * du, (q + 1) * du)
            ys.append(_dot(sr_ref[:, rq], cre_ref[rq, cq]) - _dot(si_ref[:, rq], cim_ref[rq, cq]))
        ys = jnp.concatenate(ys, axis=1)
        y1 = ys + p_ref[0:1, :] * u
        y1_ref[...] = y1
        z = _gelu(y1)
        q = _dot(z, gw_ref[...]) + p_ref[1:2, :]
        ya = z * _sigmoid(q)
        na = ya * lax.rsqrt(_head_ms(ya, h16_ref) + EPS) * p_ref[2:3, :]
        cv = cg_ref[...] * v_ref[...]
        cvh = jnp.where(i > 0, cgh_ref[...] * vh_ref[...], 0.0)
        yb = bg_ref[...] * _conv3(cv, cvh, cw_ref)
        nb = yb * lax.rsqrt(_head_ms(yb, h64_ref) + EPS) * p_ref[3:4, :]
        o = _dot(na, wo_ref[0:dh, :]) + _dot(nb, wo_ref[dh:2 * dh, :])
        o_ref[...] = o
        on = o * lax.rsqrt(_rowmean(o * o) + EPS) * vd_ref[0:1, :]
        x2_ref[...] = x_ref[...] + vd_ref[1:2, :] * on

    return pl.pallas_call(
        body, name="mix_out", grid=(t // tb,),
        out_shape=(jax.ShapeDtypeStruct((t, dh), F32), jax.ShapeDtypeStruct((t, d), F32),
                   jax.ShapeDtypeStruct((t, d), F32)),
        in_specs=[_rows(tb, d), _rows(tb, dh, 0), _rows(tb, dh, 1), _rows(tb, dh, 2), _rows(tb, dh, 3),
                  _halo_prev(tb, dh, 2), _halo_prev(tb, dh, 3), _rows(tb, nstate), _rows(tb, nstate),
                  _full(c_re.shape), _full(c_im.shape), _full(v512.shape), _full(convw.shape), _full(glu_w.shape),
                  _full(h16.shape), _full(h64.shape), _full(w_out.shape), _full(vd.shape)],
        out_specs=(_rows(tb, dh), _rows(tb, d), _rows(tb, d)),
        compiler_params=_cparams(("parallel",), VMEM_BIG),
    )(x, proj, proj, proj, proj, proj, proj, s_re, s_im, c_re, c_im, v512, convw, glu_w, h16, h64, w_out, vd)


def _ffn_up(x2, vec, w_up_st):
    t, d = x2.shape
    ns, _, nc = w_up_st.shape
    tb = _blk(t, TB_FFN)

    def body(x_ref, vec_ref, w_ref, up_ref, h2_ref):
        xv = x_ref[...]
        r = lax.rsqrt(_rowmean(xv * xv) + EPS)
        h = xv * r * vec_ref[0:1, :] * vec_ref[1:2, :] + vec_ref[2:3, :]
        hb = h.astype(BF16)
        h2_ref[...] = hb
        for j in range(ns):
            up_ref[:, j * nc:(j + 1) * nc] = jnp.dot(hb, w_ref[j], preferred_element_type=F32)

    return pl.pallas_call(
        body, name="ffn_up", grid=(t // tb,),
        out_shape=(jax.ShapeDtypeStruct((t, ns * nc), F32), jax.ShapeDtypeStruct((t, d), BF16)),
        in_specs=[_rows(tb, d), _full((SUBLANES, d)), _resident(w_up_st.shape)],
        out_specs=(_rows(tb, ns * nc), _rows(tb, d)),
        compiler_params=_cparams(("parallel",), VMEM_BIG),
    )(x2, vec, w_up_st)


def _ffn_down(up, fw, w_down, w_down_t, x2, tgt, vd):
    t, nh = up.shape
    dff, d = w_down.shape
    tb = _blk(t, TB_FFN)
    inv_d = 1.0 / d

    def body(up_ref, uph_ref, fw_ref, wd_ref, wdt_ref, x2_ref, tgt_ref, vd_ref,
             act_ref, ddn_ref, dout_ref, dhid_ref, vec_ref, loss_ref, a_s, vv_s, sg_s):
        i = pl.program_id(0)

        def conv_cols(sl):
            x = up_ref[:, sl]
            halo = jnp.where(i > 0, uph_ref[:, sl], 0.0)
            return (fw_ref[0:1, sl] * _shift_down(x, halo, 2) + fw_ref[1:2, sl] * _shift_down(x, halo, 1)
                    + fw_ref[2:3, sl] * x)

        dn = None
        for o in range(0, dff, CW_FFN):
            sl = slice(o, o + CW_FFN)
            a = conv_cols(sl)
            vv = conv_cols(slice(dff + o, dff + o + CW_FFN))
            sg = _sigmoid(a)
            a_s[:, sl] = a
            vv_s[:, sl] = vv
            sg_s[:, sl] = sg
            actb = (a * sg * vv).astype(BF16)
            act_ref[:, sl] = actb
            pj = lax.dot_general(actb, wdt_ref[:, sl], (((1,), (1,)), ((), ())), preferred_element_type=F32)
            dn = pj if dn is None else dn + pj
        r3 = lax.rsqrt(_rowmean(dn * dn) + EPS)
        xn = dn * r3
        g = vd_ref[0:1, :]
        gt2 = vd_ref[1:2, :]
        dnn = xn * g
        diff = x2_ref[...] + gt2 * dnn - tgt_ref[...]
        part = 0.5 * inv_d * jnp.sum(diff * diff)

        @pl.when(i == 0)
        def _():
            loss_ref[...] = jnp.zeros(loss_ref.shape, F32)
        loss_ref[...] += part
        dout = diff * inv_d
        dout_ref[...] = dout
        ddnn = dout * gt2
        _acc_rows(vec_ref, i == 0, [_colsum(dout * dnn), _colsum(ddnn * xn)])
        dxn = ddnn * g
        ddn = r3 * (dxn - xn * _rowmean(dxn * xn))
        ddnb = ddn.astype(BF16)
        ddn_ref[...] = ddnb
        for o in range(0, dff, CW_FFN):
            sl = slice(o, o + CW_FFN)
            dact = lax.dot_general(ddnb, wd_ref[sl, :], (((1,), (1,)), ((), ())), preferred_element_type=F32)
            a, vv, sg = a_s[:, sl], vv_s[:, sl], sg_s[:, sl]
            dhid_ref[:, sl] = (dact * vv * sg * (1.0 + a * (1.0 - sg))).astype(BF16)
            dhid_ref[:, dff + o:dff + o + CW_FFN] = (dact * (a * sg)).astype(BF16)

    return pl.pallas_call(
        body, name="ffn_down", grid=(t // tb,),
        scratch_shapes=[pltpu.VMEM((tb, dff), F32)] * 3,
        out_shape=(jax.ShapeDtypeStruct((t, dff), BF16), jax.ShapeDtypeStruct((t, d), BF16),
                   jax.ShapeDtypeStruct((t, d), F32), jax.ShapeDtypeStruct((t, nh), BF16),
                   jax.ShapeDtypeStruct((SUBLANES, d), F32), jax.ShapeDtypeStruct((SUBLANES, 128), F32)),
        in_specs=[_rows(tb, nh), _halo_prev(tb, nh), _full(fw.shape), _resident(w_down.shape),
                  _resident(w_down_t.shape), _rows(tb, d),
                  _rows(tb, d), _full(vd.shape)],
        out_specs=(_rows(tb, dff), _rows(tb, d), _rows(tb, d), _rows(tb, nh), _full((SUBLANES, d)),
                   _full((SUBLANES, 128))),
        compiler_params=_cparams(("arbitrary",), VMEM_BIG),
    )(up, up, fw, w_down, w_down_t, x2, tgt, vd)


def _ffn_up_bwd(dhid, up, fw, x2, dout, vec, w_up_st):
    t, nh = dhid.shape
    d = x2.shape[1]
    ns, _, nc = w_up_st.shape
    tb = _blk(t, TB_FFN)
    nblk = t // tb

    def body(dh_ref, dhn_ref, up_ref, fw_ref, x2_ref, dout_ref, vec_ref, w_ref,
             dx2_ref, dup_ref, vp_ref, df_ref):
        i = pl.program_id(0)

        @pl.when(i == 0)
        def _():
            df_ref[...] = jnp.zeros(df_ref.shape, F32)
        for o in range(0, nh, CW_FFN):
            sl = slice(o, o + CW_FFN)
            dh = dh_ref[:, sl].astype(F32)
            dhn = jnp.where(i < nblk - 1, dhn_ref[:, sl].astype(F32), 0.0)
            dh1 = _shift_up(dh, dhn, 1)
            dh2 = _shift_up(dh, dhn, 2)
            dup_ref[:, sl] = (fw_ref[2:3, sl] * dh + fw_ref[1:2, sl] * dh1 + fw_ref[0:1, sl] * dh2).astype(BF16)
            up_v = up_ref[:, sl]
            df_ref[0:1, sl] += _colsum(dh2 * up_v)
            df_ref[1:2, sl] += _colsum(dh1 * up_v)
            df_ref[2:3, sl] += _colsum(dh * up_v)
        dh2 = None
        for j in range(ns):
            pj = lax.dot_general(dup_ref[:, j * nc:(j + 1) * nc], w_ref[j], (((1,), (1,)), ((), ())),
                                 preferred_element_type=F32)
            dh2 = pj if dh2 is None else dh2 + pj
        xv = x2_ref[...]
        r = lax.rsqrt(_rowmean(xv * xv) + EPS)
        xn = xv * r
        g = vec_ref[0:1, :]
        hg = xn * g
        dhg = dh2 * vec_ref[1:2, :]
        _acc_rows(vp_ref, i == 0, [_colsum(dh2), _colsum(dh2 * hg), _colsum(dhg * xn)])
        dxn = dhg * g
        dx2_ref[...] = dout_ref[...] + r * (dxn - xn * _rowmean(dxn * xn))

    return pl.pallas_call(
        body, name="ffn_up_bwd", grid=(nblk,),
        out_shape=(jax.ShapeDtypeStruct((t, d), F32), jax.ShapeDtypeStruct((t, nh), BF16),
                   jax.ShapeDtypeStruct((SUBLANES, d), F32), jax.ShapeDtypeStruct((SUBLANES, nh), F32)),
        in_specs=[_rows(tb, nh), _halo_next(tb, nh, t, rows=BF16_ROWS), _rows(tb, nh), _full(fw.shape),
                  _rows(tb, d), _rows(tb, d), _full(vec.shape), _resident(w_up_st.shape)],
        out_specs=(_rows(tb, d), _rows(tb, nh), _full((SUBLANES, d)), _full((SUBLANES, nh))),
        compiler_params=_cparams(("arbitrary",), VMEM_BIG),
    )(dhid, dhid, up, fw, x2, dout, vec, w_up_st)


def _mix_out_bwd(dx2, o, y1, proj, c_re, c_im, v512, convw, glu_w, h16, h64, w_out, vd):
    t, d = dx2.shape
    dh = y1.shape[1]
    nstate = c_re.shape[0]
    du, ds = dh // SSM_SPLIT, nstate // SSM_SPLIT
    tb = _blk(t, TB_MIX)

    def body(dx2_ref, o_ref, y1_ref, u_ref, bg_ref, cg_ref, v_ref, cgh_ref, vh_ref, cre_ref, cim_ref, p_ref,
             cw_ref, gw_ref, h16_ref, h64_ref, wo_ref, vd_ref,
             do_ref, ycat_ref, z_ref, dq_ref, dy1_ref, gr_ref, gi_ref, dcc_ref, dbg_ref, vpd_ref, vp5_ref):
        i = pl.program_id(0)
        first = i == 0
        ov = o_ref[...]
        ro = lax.rsqrt(_rowmean(ov * ov) + EPS)
        on_ = ov * ro
        g = vd_ref[0:1, :]
        dx2v = dx2_ref[...]
        don = dx2v * vd_ref[1:2, :]
        _acc_rows(vpd_ref, first, [_colsum(dx2v * on_ * g), _colsum(don * on_)])
        dxn = don * g
        dob = (ro * (dxn - on_ * _rowmean(dxn * on_))).astype(BF16)
        do_ref[...] = dob
        dyc_a = lax.dot_general(dob, wo_ref[0:dh, :], (((1,), (1,)), ((), ())), preferred_element_type=F32)
        dyc_b = lax.dot_general(dob, wo_ref[dh:2 * dh, :], (((1,), (1,)), ((), ())), preferred_element_type=F32)
        y1v = y1_ref[...]
        u = u_ref[...]
        z = _gelu(y1v)
        zb = z.astype(BF16)
        z_ref[...] = zb
        sg = _sigmoid(jnp.dot(zb, gw_ref[...], preferred_element_type=F32) + p_ref[1:2, :])
        ya = z * sg
        ra = lax.rsqrt(_head_ms(ya, h16_ref) + EPS)
        yan = ya * ra
        ga = p_ref[2:3, :]
        ycat_ref[:, 0:dh] = (yan * ga).astype(BF16)
        dyn = dyc_a * ga
        dya = ra * (dyn - yan * _split_dot(dyn * yan, h16_ref[...]))
        dq = dya * z * sg * (1.0 - sg)
        dqb = dq.astype(BF16)
        dq_ref[...] = dqb
        dz = dya * sg + lax.dot_general(dqb, gw_ref[...], (((1,), (1,)), ((), ())), preferred_element_type=F32)
        dy1 = dz * _gelu_grad(y1v)
        dy1_ref[...] = dy1
        dy1b = dy1.astype(BF16)
        for q in range(SSM_SPLIT):
            rq, cq = slice(q * ds, (q + 1) * ds), slice(q * du, (q + 1) * du)
            gr_ref[:, rq] = lax.dot_general(dy1b[:, cq], cre_ref[rq, cq], (((1,), (1,)), ((), ())),
                                            preferred_element_type=F32)
            gi_ref[:, rq] = -lax.dot_general(dy1b[:, cq], cim_ref[rq, cq], (((1,), (1,)), ((), ())),
                                             preferred_element_type=F32)
        bg = bg_ref[...]
        cv = cg_ref[...] * v_ref[...]
        cvh = jnp.where(i > 0, cgh_ref[...] * vh_ref[...], 0.0)
        cv1 = _shift_down(cv, cvh, 1)
        cv2 = _shift_down(cv, cvh, 2)
        cc = cw_ref[0:1, :] * cv2 + cw_ref[1:2, :] * cv1 + cw_ref[2:3, :] * cv
        yb = bg * cc
        rb = lax.rsqrt(_head_ms(yb, h64_ref) + EPS)
        ybn = yb * rb
        gb = p_ref[3:4, :]
        ycat_ref[:, dh:2 * dh] = (ybn * gb).astype(BF16)
        dynb = dyc_b * gb
        dyb = rb * (dynb - ybn * _split_dot(dynb * ybn, h64_ref[...]))
        dcc = dyb * bg
        dbg_ref[...] = dyb * cc
        dcc_ref[...] = dcc
        _acc_rows(vp5_ref, first, [_colsum(dyc_a * yan), _colsum(dyc_b * ybn), _colsum(dq), _colsum(dy1 * u),
                                   _colsum(dcc * cv2), _colsum(dcc * cv1), _colsum(dcc * cv)])

    return pl.pallas_call(
        body, name="mix_out_bwd", grid=(t // tb,),
        out_shape=(jax.ShapeDtypeStruct((t, d), BF16), jax.ShapeDtypeStruct((t, 2 * dh), BF16),
                   jax.ShapeDtypeStruct((t, dh), BF16), jax.ShapeDtypeStruct((t, dh), BF16),
                   jax.ShapeDtypeStruct((t, dh), F32), jax.ShapeDtypeStruct((t, nstate), F32),
                   jax.ShapeDtypeStruct((t, nstate), F32), jax.ShapeDtypeStruct((t, dh), F32),
                   jax.ShapeDtypeStruct((t, dh), F32), jax.ShapeDtypeStruct((SUBLANES, d), F32),
                   jax.ShapeDtypeStruct((SUBLANES, dh), F32)),
        in_specs=[_rows(tb, d), _rows(tb, d), _rows(tb, dh), _rows(tb, dh, 0), _rows(tb, dh, 1), _rows(tb, dh, 2),
                  _rows(tb, dh, 3), _halo_prev(tb, dh, 2), _halo_prev(tb, dh, 3), _full(c_re.shape), _full(c_im.shape),
                  _full(v512.shape), _full(convw.shape), _full(glu_w.shape), _full(h16.shape), _full(h64.shape),
                  _full(w_out.shape), _full(vd.shape)],
        out_specs=(_rows(tb, d), _rows(tb, 2 * dh), _rows(tb, dh), _rows(tb, dh), _rows(tb, dh), _rows(tb, nstate),
                   _rows(tb, nstate), _rows(tb, dh), _rows(tb, dh), _full((SUBLANES, d)), _full((SUBLANES, dh))),
        compiler_params=_cparams(("arbitrary",), VMEM_BIG),
    )(dx2, o, y1, proj, proj, proj, proj, proj, proj, c_re, c_im, v512, convw, glu_w, h16, h64, w_out, vd)


def _mix_in_bwd(gt_re, gt_im, b_re, b_im, dy1, dcc, dbg, proj, x, dx2, vec, v512, convw, w_in_st):
    t, d = x.shape
    dh = dy1.shape[1]
    nstate = gt_re.shape[1]
    du_w, ds = dh // SSM_SPLIT, nstate // SSM_SPLIT
    ns, _, nc = w_in_st.shape
    tb = _blk(t, TB_MIX)
    nblk = t // tb

    def body(gr_ref, gi_ref, bre_ref, bim_ref, dy1_ref, dcc_ref, dccn_ref, dbg_ref, cg_ref, v_ref, x_ref, dx2_ref,
             vec_ref, p_ref, cw_ref, w_ref, gx_ref, dproj_ref, vp_ref):
        i = pl.program_id(0)
        du = []
        for q in range(SSM_SPLIT):
            rq, cq = slice(q * du_w, (q + 1) * du_w), slice(q * ds, (q + 1) * ds)
            du.append(lax.dot_general(gr_ref[:, cq].astype(BF16), bre_ref[rq, cq], (((1,), (1,)), ((), ())),
                                      preferred_element_type=F32)
                      + lax.dot_general(gi_ref[:, cq].astype(BF16), bim_ref[rq, cq], (((1,), (1,)), ((), ())),
                                        preferred_element_type=F32))
        du = dy1_ref[...] * p_ref[0:1, :] + jnp.concatenate(du, axis=1)
        dcc = dcc_ref[...]
        dccn = jnp.where(i < nblk - 1, dccn_ref[...], 0.0)
        dcv = (cw_ref[2:3, :] * dcc + cw_ref[1:2, :] * _shift_up(dcc, dccn, 1)
               + cw_ref[0:1, :] * _shift_up(dcc, dccn, 2))
        parts = [du, dbg_ref[...], dcv * v_ref[...], dcv * cg_ref[...]]
        dh1 = None
        for j in range(ns):
            pb = parts[j].astype(BF16)
            dproj_ref[:, j * nc:(j + 1) * nc] = pb
            pj = lax.dot_general(pb, w_ref[j], (((1,), (1,)), ((), ())), preferred_element_type=F32)
            dh1 = pj if dh1 is None else dh1 + pj
        xv = x_ref[...]
        r = lax.rsqrt(_rowmean(xv * xv) + EPS)
        xn = xv * r
        g = vec_ref[0:1, :]
        hg = xn * g
        dhg = dh1 * vec_ref[1:2, :]
        _acc_rows(vp_ref, i == 0, [_colsum(dh1), _colsum(dh1 * hg), _colsum(dhg * xn)])
        dxn = dhg * g
        gx_ref[...] = dx2_ref[...] + r * (dxn - xn * _rowmean(dxn * xn))

    assert nc == dh and ns == 4
    return pl.pallas_call(
        body, name="mix_in_bwd", grid=(nblk,),
        out_shape=(jax.ShapeDtypeStruct((t, d), F32), jax.ShapeDtypeStruct((t, ns * nc), BF16),
                   jax.ShapeDtypeStruct((SUBLANES, d), F32)),
        in_specs=[_rows(tb, nstate), _rows(tb, nstate), _full(b_re.shape), _full(b_im.shape), _rows(tb, dh),
                  _rows(tb, dh), _halo_next(tb, dh, t), _rows(tb, dh), _rows(tb, dh, 2), _rows(tb, dh, 3),
                  _rows(tb, d), _rows(tb, d), _full(vec.shape), _full(v512.shape), _full(convw.shape),
                  _full(w_in_st.shape)],
        out_specs=(_rows(tb, d), _rows(tb, ns * nc), _full((SUBLANES, d))),
        compiler_params=_cparams(("arbitrary",), VMEM_BIG),
    )(gt_re, gt_im, b_re, b_im, dy1, dcc, dcc, dbg, proj, proj, x, dx2, vec, v512, convw, w_in_st)


def _matmul_tn(a, b, m, bn, out_dtype, name, diag=False, bt=TB_TN, after=None):
    t = a.shape[0]
    n = b.shape[1]
    bt = _blk(t, bt)
    nk = t // bt
    extra = [] if after is None else [after]
    a_map = (lambda j, k: (k, j)) if diag else (lambda j, k: (k, 0))

    def body(a_ref, b_ref, *rest):
        o_ref, acc_ref = rest[-2:]
        k = pl.program_id(1)

        @pl.when(k == 0)
        def _():
            acc_ref[...] = jnp.zeros(acc_ref.shape, F32)
        acc_ref[...] += _dot_tn(a_ref[...], b_ref[...])

        @pl.when(k == nk - 1)
        def _():
            o_ref[...] = acc_ref[...].astype(out_dtype)

    return pl.pallas_call(
        body, name=name, grid=(n // bn, nk),
        out_shape=jax.ShapeDtypeStruct((n // bn, m, bn), out_dtype),
        in_specs=[pl.BlockSpec((bt, m), a_map), pl.BlockSpec((bt, bn), lambda j, k: (k, j))]
        + [pl.BlockSpec(memory_space=pl.ANY)] * len(extra),
        out_specs=pl.BlockSpec((None, m, bn), lambda j, k: (j, 0, 0)),
        scratch_shapes=[pltpu.VMEM((m, bn), F32)],
        compiler_params=_cparams(("parallel", "arbitrary"), VMEM_BIG),
    )(a, b, *extra)


def _ssm_bgrad(d_bre, d_bim, bt_re, bt_im, rows_in, fold):
    gh, cb = d_bre.shape
    nb = SSM_SPLIT
    rb = gh // nb
    gp = nb * cb
    p = fold.shape[1]

    def body(dr_ref, di_ref, br_ref, bi_ref, rin_ref, f_ref, dbr_ref, dbi_ref, rout_ref):
        row = lax.broadcasted_iota(jnp.int32, (rb, cb), 0)
        col = lax.broadcasted_iota(jnp.int32, (rb, cb), 1)
        mask = (row >> 4) == (col >> 6)
        gr = jnp.where(mask, dr_ref[...], 0.0)
        gi = jnp.where(mask, di_ref[...], 0.0)
        cr, ci = rin_ref[0:1, :], rin_ref[1:2, :]
        dbr_ref[...] = _split3_dot(cr * gr + ci * gi, f_ref[...])
        dbi_ref[...] = _split3_dot(cr * gi - ci * gr, f_ref[...])
        br, bi = br_ref[...], bi_ref[...]
        rout_ref[...] = jnp.zeros(rout_ref.shape, F32)
        rout_ref[0:1, :] = _colsum(br * gr + bi * gi)
        rout_ref[1:2, :] = _colsum(br * gi - bi * gr)

    dspec = pl.BlockSpec((rb, cb), lambda j: (j, 0))
    bspec = pl.BlockSpec((rb, cb), lambda j: (j, j))
    rspec = pl.BlockSpec((SUBLANES, cb), lambda j: (0, j))
    ospec = pl.BlockSpec((rb, p), lambda j: (j, 0))
    return pl.pallas_call(
        body, name="ssm_bgrad", grid=(nb,),
        out_shape=(jax.ShapeDtypeStruct((gh, p), F32), jax.ShapeDtypeStruct((gh, p), F32),
                   jax.ShapeDtypeStruct((SUBLANES, gp), F32)),
        in_specs=[dspec, dspec, bspec, bspec, rspec, _full(fold.shape)],
        out_specs=(ospec, ospec, rspec),
        compiler_params=_cparams(("parallel",)),
    )(d_bre, d_bim, bt_re, bt_im, rows_in, fold)


def _ssm_cgrad(d_cre, d_cim, fold):
    gp, cb = d_cre.shape
    nb = SSM_SPLIT
    rb = gp // nb
    h = fold.shape[1]

    def body(dr_ref, di_ref, f_ref, cr_ref, ci_ref):
        row = lax.broadcasted_iota(jnp.int32, (rb, cb), 0)
        col = lax.broadcasted_iota(jnp.int32, (rb, cb), 1)
        mask = (row >> 6) == (col >> 4)
        cr_ref[...] = _split3_dot(jnp.where(mask, dr_ref[...], 0.0), f_ref[...])
        ci_ref[...] = -_split3_dot(jnp.where(mask, di_ref[...], 0.0), f_ref[...])

    cspec = pl.BlockSpec((rb, cb), lambda j: (j, 0))
    ospec = pl.BlockSpec((rb, h), lambda j: (j, 0))
    return pl.pallas_call(
        body, name="ssm_cgrad", grid=(nb,),
        out_shape=(jax.ShapeDtypeStruct((gp, h), F32),) * 2,
        in_specs=[cspec, cspec, _full(fold.shape)], out_specs=(ospec, ospec),
        compiler_params=_cparams(("parallel",)),
    )(d_cre, d_cim, fold)


def _ssm_lamgrad(lam_re, lam_im, log_step, abar_re, abar_im, coef_re, coef_im, gc_re, gc_im, ga_re, ga_im):
    g, p = lam_re.shape

    def body(lr_ref, li_ref, ls_ref, ar_ref, ai_ref, cr_ref, ci_ref, gcr_ref, gci_ref, gar_ref, gai_ref,
             dlr_ref, dli_ref, dls_ref):
        lam_raw = lr_ref[...]
        lr = jnp.minimum(lam_raw, LAMBDA_RE_MAX)
        li = li_ref[...]
        st = jnp.exp(ls_ref[...])
        den = lr * lr + li * li
        gcr, gci = gcr_ref[...], gci_ref[...]
        gab_r = gar_ref[...] + (lr * gcr - li * gci) / den
        gab_i = gai_ref[...] + (lr * gci + li * gcr) / den
        cr, ci = cr_ref[...], ci_ref[...]
        wr = -(cr * lr + ci * li) / den
        wi = -(ci * lr - cr * li) / den
        gl_r = wr * gcr + wi * gci
        gl_i = wr * gci - wi * gcr
        ar, ai = ar_ref[...], ai_ref[...]
        gw_r = ar * gab_r + ai * gab_i
        gw_i = ar * gab_i - ai * gab_r
        gl_r = gl_r + st * gw_r
        gl_i = gl_i + st * gw_i
        pass_through = jnp.where(lam_raw < LAMBDA_RE_MAX, 1.0, jnp.where(lam_raw == LAMBDA_RE_MAX, 0.5, 0.0))
        dlr_ref[...] = gl_r * pass_through
        dli_ref[...] = gl_i
        dls_ref[...] = st * jnp.sum(lr * gw_r + li * gw_i, axis=1, keepdims=True)

    sds = jax.ShapeDtypeStruct((g, p), F32)
    return pl.pallas_call(body, name="ssm_lamgrad", out_shape=(sds, sds, jax.ShapeDtypeStruct((g, 1), F32)))(
        lam_re, lam_im, log_step, abar_re, abar_im, coef_re, coef_im, gc_re, gc_im, ga_re, ga_im)


def _row_block(r, most=256):
    for rb in range(min(r, most), BF16_ROWS - 1, -1):
        if r % rb == 0 and rb % BF16_ROWS == 0:
            return rb
    return r


def _adamw_math(w, g, m, v):
    m = ADAM_B1 * m + (1.0 - ADAM_B1) * g
    v = ADAM_B2 * v + (1.0 - ADAM_B2) * (g * g)
    m_hat = m / (1.0 - ADAM_B1 ** ADAM_STEP)
    v_hat = v / (1.0 - ADAM_B2 ** ADAM_STEP)
    delta = -ADAM_LR * (m_hat / (jnp.sqrt(v_hat) + ADAM_EPS) + ADAM_WD * w)
    return delta, m, v


def _adamw_big(p_mine, p_sib, w, m, v, name):
    r, c = w.shape
    rb = _row_block(r)

    def body(a_ref, b_ref, w_ref, m_ref, v_ref, g_ref, d_ref, mo_ref, vo_ref):
        g = a_ref[...] + b_ref[...]
        g_ref[...] = g
        d_ref[...], mo_ref[...], vo_ref[...] = _adamw_math(w_ref[...], g, m_ref[...], v_ref[...])

    spec = pl.BlockSpec((rb, c), lambda i: (i, 0))
    sds = jax.ShapeDtypeStruct((r, c), F32)
    return pl.pallas_call(
        body, name=name, grid=(r // rb,), out_shape=(sds,) * 4, in_specs=[spec] * 5, out_specs=(spec,) * 4,
        compiler_params=_cparams(("parallel",), VMEM_MID),
    )(p_mine, p_sib, w, m, v)


def _sum_blocks(stack, name):
    n, r, c = stack.shape
    rb = _row_block(r)

    def body(s_ref, o_ref):
        acc = s_ref[0].astype(F32)
        for k in range(1, n):
            acc = acc + s_ref[k].astype(F32)
        o_ref[...] = acc

    return pl.pallas_call(
        body, name=name, grid=(r // rb,), out_shape=jax.ShapeDtypeStruct((r, c), F32),
        in_specs=[pl.BlockSpec((n, rb, c), lambda i: (0, i, 0))], out_specs=pl.BlockSpec((rb, c), lambda i: (i, 0)),
        compiler_params=_cparams(("parallel",), VMEM_MID),
    )(stack)


def _adamw_ada(c_all, dmod_cols, w, m, v):
    d, n = w.shape
    bn = 512

    def body(c_ref, dm_ref, w_ref, m_ref, v_ref, g_ref, d_ref, mo_ref, vo_ref):
        cc = c_ref[...]
        g = _dot_tn(cc * _sigmoid(cc), dm_ref[...])
        g_ref[...] = g
        d_ref[...], mo_ref[...], vo_ref[...] = _adamw_math(w_ref[...], g, m_ref[...], v_ref[...])

    spec = pl.BlockSpec((d, bn), lambda j: (0, j))
    sds = jax.ShapeDtypeStruct((d, n), F32)
    return pl.pallas_call(
        body, name="adamw_ada", grid=(n // bn,), out_shape=(sds,) * 4,
        in_specs=[_full((N_DEV, d)), pl.BlockSpec((N_DEV, bn), lambda j: (0, j)), spec, spec, spec],
        out_specs=(spec,) * 4, compiler_params=_cparams(("parallel",)),
    )(c_all, dmod_cols, w, m, v)


def _adamw_small(items):
    n = len(items)

    def body(*refs):
        ins, outs = refs[:4 * n], refs[4 * n:]
        for k in range(n):
            w_ref, g_ref, m_ref, v_ref = ins[4 * k:4 * k + 4]
            outs[3 * k][...], outs[3 * k + 1][...], outs[3 * k + 2][...] = _adamw_math(
                w_ref[...], g_ref[...], m_ref[...], v_ref[...])

    flat = [a for it in items for a in it]
    out_shape = tuple(jax.ShapeDtypeStruct(it[0].shape, F32) for it in items for _ in range(3))
    res = pl.pallas_call(body, name="adamw_small", out_shape=out_shape,
                         compiler_params=_cparams(vmem=VMEM_BIG))(*flat)
    return [tuple(res[3 * k:3 * k + 3]) for k in range(n)]


def _group_mean_matrix(n, group):
    idx = np.arange(n) // group
    return (idx[:, None] == idx[None, :]).astype(np.float32) / group


def _fold_matrix(n, period):
    return (np.arange(n)[:, None] % period == np.arange(period)[None, :]).astype(np.float32)


def _rows8(*rows):
    c = rows[0].shape[-1]
    pad = jnp.zeros((SUBLANES - len(rows), c), F32)
    return jnp.concatenate([r.reshape(1, c) for r in rows] + [pad], axis=0)


def _to_rows(a, width):
    flat = a.reshape(-1)
    n = -(-flat.shape[0] // width)
    flat = jnp.pad(flat, (0, n * width - flat.shape[0]))
    return flat.reshape(n, width)


def kernel(x, c, w_ada, b_ada, g_pre_mix, g_post_mix, w_in, ssm_lam_re, ssm_lam_im, ssm_log_step, ssm_b_re, ssm_b_im, ssm_c_re, ssm_c_im, ssm_d, glu_w, glu_b, g_out_ssm, conv_w, g_out_conv, w_out, g_pre_ffn, g_post_ffn, w_up, ffn_conv_w, w_down, loss_target, m_w_ada, m_b_ada, m_g_pre_mix, m_g_post_mix, m_w_in, m_ssm_lam_re, m_ssm_lam_im, m_ssm_log_step, m_ssm_b_re, m_ssm_b_im, m_ssm_c_re, m_ssm_c_im, m_ssm_d, m_glu_w, m_glu_b, m_g_out_ssm, m_conv_w, m_g_out_conv, m_w_out, m_g_pre_ffn, m_g_post_ffn, m_w_up, m_ffn_conv_w, m_w_down, v_w_ada, v_b_ada, v_g_pre_mix, v_g_post_mix, v_w_in, v_ssm_lam_re, v_ssm_lam_im, v_ssm_log_step, v_ssm_b_re, v_ssm_b_im, v_ssm_c_re, v_ssm_c_im, v_ssm_d, v_glu_w, v_glu_b, v_g_out_ssm, v_conv_w, v_g_out_conv, v_w_out, v_g_pre_ffn, v_g_post_ffn, v_w_up, v_ffn_conv_w, v_w_down):
    xs = x[0]
    tgt = loss_target[0]
    t, d = xs.shape
    xi, yi, ci = lax.axis_index("x"), lax.axis_index("y"), lax.axis_index("c")
    chip = 2 * xi + yi
    dev = 2 * chip + ci

    n_groups, n_state = ssm_lam_re.shape[1:]
    n_gch = ssm_b_re.shape[3]
    d_ssm = n_groups * n_gch
    gp = n_groups * n_state
    n_ada = w_ada.shape[2]
    d_ff = w_down.shape[1] * N_CHIPS
    n_upc = w_up.shape[2]

    w_names = ("w_in", "glu_w", "w_out", "w_up", "w_down")
    c_gath, _ = _allgather8(jnp.broadcast_to(c, (SUBLANES, d)), SUBLANES, "gather_c")
    c_all = c_gath.reshape(N_DEV, SUBLANES, d)[:, 0, :]
    b_sh = lax.dynamic_slice(b_ada, (0, chip * n_ada), (1, n_ada))
    mod_sh = _mod_shard(c_all, w_ada[0], b_sh)

    def pad8(a):
        return jnp.concatenate([a, jnp.zeros((SUBLANES - a.shape[0], a.shape[1]), a.dtype)], axis=0)

    w_names = ("mod", "conv_w", "ffn_conv_w", "w_in", "glu_w", "w_out", "w_up", "w_down")
    w_own = [mod_sh, pad8(conv_w[0]), pad8(ffn_conv_w[0])]
    w_own += [w[0].astype(BF16) for w in (w_in, glu_w, w_out, w_up, w_down)]
    w_send, w_recv, w_src, w_land, w_token = _chips_start(
        "weights_start", True, w_own, [_landing(a, chip) for a in w_own])

    def weights(names, after):
        ks = [w_names.index(nm) for nm in names]
        return _chips_wait("weights_wait_" + names[-1], True, [w_send[k] for k in ks], [w_recv[k] for k in ks],
                           [w_src[k] for k in ks], [w_land[k] for k in ks], after)

    lam_re, lam_im = ssm_lam_re[0], ssm_lam_im[0]
    log_step = ssm_log_step[0].reshape(n_groups, 1) + w_token[0:1, 0:1]
    abar_re, abar_im, coef_re, coef_im = _ssm_prep(lam_re, lam_im, log_step)
    a_rows = _rows8(abar_re.reshape(1, gp), abar_im.reshape(1, gp))
    coef_rows = _rows8(coef_re.reshape(1, gp), coef_im.reshape(1, gp))
    bt_re = jnp.tile(ssm_b_re[0].transpose(0, 2, 1).reshape(d_ssm, n_state), (1, n_groups))
    bt_im = jnp.tile(ssm_b_im[0].transpose(0, 2, 1).reshape(d_ssm, n_state), (1, n_groups))
    ct_re = jnp.tile(ssm_c_re[0].transpose(0, 2, 1).reshape(gp, n_gch), (1, n_groups))
    ct_im = jnp.tile(ssm_c_im[0].transpose(0, 2, 1).reshape(gp, n_gch), (1, n_groups))
    bblk_re, bblk_im, cblk_re, cblk_im = _ssm_blocks(bt_re, bt_im, ct_re, ct_im, coef_rows)

    h16 = jnp.asarray(_group_mean_matrix(d_ssm, n_gch), BF16)
    h64 = jnp.asarray(_group_mean_matrix(d_ssm, CONV_HEAD_DIM), BF16)

    g_mod, g_cw, g_fw, w_in_st = weights(("mod", "conv_w", "ffn_conv_w", "w_in"), bblk_re)
    mod_all = g_mod.transpose(1, 0, 2).reshape(N_DEV, N_CHIPS * n_ada)
    mod = lax.dynamic_slice(mod_all, (dev, 0), (1, N_CHIPS * n_ada))
    sh1, sc1, gt1, sh2, sc2, gt2 = [mod[:, k * d:(k + 1) * d] for k in range(6)]
    convw_full = pad8(g_cw[:, :3, :].transpose(1, 0, 2).reshape(3, d_ssm))
    fw_full = pad8(g_fw[:, :3, :].transpose(1, 0, 2).reshape(3, N_CHIPS * n_upc))

    v512 = _rows8(ssm_d, glu_b, g_out_ssm, g_out_conv)
    vec1 =_rows8(g_pre_mix, 1.0 + sc1, sh1)
    vd1 = _rows8(g_post_mix, gt1)
    vec2 = _rows8(g_pre_ffn, 1.0 + sc2, sh2)
    vd2 = _rows8(g_post_ffn, gt2)

    proj, bu_re, bu_im, h1b = _mix_in(xs, vec1, w_in_st, bblk_re, bblk_im)
    s_re, s_im = _scan_fwd(a_rows, bu_re, bu_im)
    g_glu, g_wout = weights(("glu_w", "w_out"), s_re)
    glu_full = g_glu.reshape(d_ssm, d_ssm)
    w_out_full = g_wout.reshape(2 * d_ssm, d)
    y1, o_mix, x2 = _mix_out(xs, proj, s_re, s_im, cblk_re, cblk_im, v512, convw_full, glu_full, h16, h64,
                             w_out_full, vd1)
    (w_up_st,) = weights(("w_up",), x2)
    up, h2b = _ffn_up(x2, vec2, w_up_st)
    (g_wdown,) = weights(("w_down",), up)
    w_down_full = g_wdown.reshape(d_ff, d)
    actb, ddnb, dout, dhid, vp_dn, loss_blk = _ffn_down(up, fw_full, w_down_full, w_down_full.T, x2, tgt, vd2)

    g_names = ("w_down", "w_up", "w_out", "glu_w", "w_in")
    gw_down = _matmul_tn(actb, ddnb, d_ff, d, BF16, "dw_down", bt=1024).reshape(N_CHIPS, d_ff // N_CHIPS, d)
    dx2, dupb, vp_up, df_rows = _ffn_up_bwd(dhid, up, fw_full, x2, dout, vec2, w_up_st)
    gw_up = _matmul_tn(h2b, dupb, d, n_upc, BF16, "dw_up", bt=2048)
    ga_send, ga_recv, ga_src, ga_land, ga_token = _chips_start(
        "grads_start_ffn", False, [gw_down, gw_up],
        [_landing(lax.dynamic_index_in_dim(g, chip, 0, False), chip) for g in (gw_down, gw_up)])
    (dob, ycatb, zb, dqb, dy1, g_re, g_im, dcc, dbg, vp_mo, vp5) = _mix_out_bwd(
        dx2, o_mix, y1, proj, cblk_re, cblk_im, v512, convw_full, glu_full, h16, h64, w_out_full,
        vd1 + ga_token[0:1, 0:1])
    gt_re, gt_im, ga_re8, ga_im8 = _scan_bwd(a_rows, g_re, g_im, s_re, s_im)
    grad_x, dprojb, vp_mi = _mix_in_bwd(gt_re, gt_im, bblk_re, bblk_im, dy1, dcc, dbg, proj, xs, dx2, vec1, v512,
                                        convw_full, w_in_st)
    ssm_u, ssm_s = d_ssm // SSM_SPLIT, gp // SSM_SPLIT
    d_bre = _matmul_tn(proj, gt_re, ssm_u, ssm_s, F32, "d_bre", diag=True, bt=2048)
    d_bim = _matmul_tn(proj, gt_im, ssm_u, ssm_s, F32, "d_bim", diag=True, bt=2048)
    d_cre = _matmul_tn(s_re, dy1, ssm_s, ssm_u, F32, "d_cre", diag=True, bt=2048)
    d_cim = _matmul_tn(s_im, dy1, ssm_s, ssm_u, F32, "d_cim", diag=True, bt=2048)
    d_bre, d_bim = d_bre.reshape(d_ssm, ssm_s), d_bim.reshape(d_ssm, ssm_s)
    d_cre, d_cim = d_cre.reshape(gp, ssm_u), d_cim.reshape(gp, ssm_u)

    fold_b = jnp.asarray(_fold_matrix(ssm_s, n_state), BF16)
    fold_c = jnp.asarray(_fold_matrix(ssm_u, n_gch), BF16)
    db_re_f, db_im_f, gc_rows = _ssm_bgrad(d_bre, d_bim, bt_re, bt_im, coef_rows, fold_b)
    dc_re_f, dc_im_f = _ssm_cgrad(d_cre, d_cim, fold_c)
    ga_sum = _ga_rowsum(ga_re8, ga_im8)
    g_lam_re, g_lam_im, g_log_step = _ssm_lamgrad(
        lam_re, lam_im, log_step, abar_re, abar_im, coef_re, coef_im,
        gc_rows[0].reshape(n_groups, n_state), gc_rows[1].reshape(n_groups, n_state),
        ga_sum[0].reshape(n_groups, n_state), ga_sum[1].reshape(n_groups, n_state))
    g_b_re = db_re_f.reshape(n_groups, n_gch, n_state).transpose(0, 2, 1)
    g_b_im = db_im_f.reshape(n_groups, n_gch, n_state).transpose(0, 2, 1)
    g_c_re = dc_re_f.reshape(n_groups, n_state, n_gch).transpose(0, 2, 1)
    g_c_im = dc_im_f.reshape(n_groups, n_state, n_gch).transpose(0, 2, 1)

    dmod = jnp.concatenate([vp_mi[0:1], vp_mi[1:2], vp_mo[0:1], vp_up[0:1], vp_up[1:2], vp_dn[0:1]], axis=1)
    small = [
        ("g_pre_mix", vp_mi[2:3]), ("g_post_mix", vp_mo[1:2]), ("g_pre_ffn", vp_up[2:3]), ("g_post_ffn", vp_dn[1:2]),
        ("ssm_lam_re", g_lam_re), ("ssm_lam_im", g_lam_im), ("ssm_log_step", g_log_step),
        ("ssm_b_re", g_b_re), ("ssm_b_im", g_b_im), ("ssm_c_re", g_c_re), ("ssm_c_im", g_c_im),
        ("ssm_d", vp5[3:4]), ("glu_b", vp5[2:3]), ("g_out_ssm", vp5[0:1]), ("g_out_conv", vp5[1:2]),
        ("conv_w", vp5[4:7]), ("ffn_conv_w", df_rows[0:3]), ("loss", loss_blk[0:1, 0:1]),
    ]
    packed, offsets, row = [], {}, 0
    for name, a in small:
        r = _to_rows(a, d)
        offsets[name] = (row, a.shape)
        packed.append(r)
        row += r.shape[0]
    n_small = -(-row // SUBLANES) * SUBLANES
    packed.append(jnp.zeros((n_small - row, d), F32))
    packed.append(pad8(dmod.reshape(6, d)))
    pack = jnp.concatenate(packed, axis=0)
    gath, sums = _allgather8(pack, n_small + SUBLANES, "reduce_small")
    dmod_all = gath.reshape(N_DEV, n_small + SUBLANES, d)[:, n_small:n_small + 6, :].reshape(N_DEV, 6 * d)
    g_b_ada = sums[n_small:n_small + 6].reshape(1, 6 * d)

    def unpack(name):
        r0, shape = offsets[name]
        size = math.prod(shape)
        nrow = -(-size // d)
        return sums[r0:r0 + nrow].reshape(-1)[:size].reshape(shape)

    gw_out = _matmul_tn(ycatb, dob, 2 * d_ssm, d, BF16, "dw_out", bt=2048, after=sums)
    gw_out = gw_out.reshape(N_CHIPS, 2 * d_ssm // N_CHIPS, d)
    gw_glu = _matmul_tn(zb, dqb, d_ssm, d_ssm, BF16, "dw_glu", bt=2048).reshape(N_CHIPS, d_ssm // N_CHIPS, d_ssm)
    gw_in = _matmul_tn(h1b, dprojb, d, w_in.shape[2], BF16, "dw_in", bt=2048)
    gb_send, gb_recv, gb_src, gb_land, gb_token = _chips_start(
        "grads_start_mix", False, [gw_out, gw_glu, gw_in],
        [_landing(lax.dynamic_index_in_dim(g, chip, 0, False), chip) for g in (gw_out, gw_glu, gw_in)])

    dmod_cols = lax.dynamic_slice(dmod_all, (0, chip * n_ada), (N_DEV, n_ada)) + gb_token[0:1, 0:1]
    ada = _adamw_ada(c_all, dmod_cols, w_ada[0], m_w_ada[0], v_w_ada[0])

    def finish(names, landed):
        partial = [_sum_blocks(s, "sum_" + nm) for s, nm in zip(landed, names)]
        theirs = _swap_sibling(partial, "swap_" + names[0])
        done = {}
        for nm, pm, ps in zip(names, partial, theirs):
            w_, m_, v_ = big_params[nm]
            done[nm] = _adamw_big(pm, ps, w_[0], m_[0], v_[0], "adamw_" + nm)
        return done

    big_params = {"w_down": (w_down, m_w_down, v_w_down), "w_up": (w_up, m_w_up, v_w_up),
                  "w_out": (w_out, m_w_out, v_w_out), "glu_w": (glu_w, m_glu_w, v_glu_w),
                  "w_in": (w_in, m_w_in, v_w_in)}
    big = finish(("w_down", "w_up"), _chips_wait("grads_wait_ffn", False, ga_send, ga_recv, ga_src, ga_land, ada[0]))
    big.update(finish(("w_out", "glu_w", "w_in"),
                      _chips_wait("grads_wait_mix", False, gb_send, gb_recv, gb_src, gb_land, big["w_up"][0])))

    g_small = {name: unpack(name) for name, _ in small}
    g_small["b_ada"] = g_b_ada
    g_small["conv_w"] = lax.dynamic_slice(g_small["conv_w"], (0, chip * conv_w.shape[2]), (3, conv_w.shape[2]))
    g_small["ffn_conv_w"] = lax.dynamic_slice(g_small["ffn_conv_w"], (0, chip * n_upc), (3, n_upc))
    g_small["ssm_log_step"] = g_small["ssm_log_step"].reshape(1, n_groups)
    small_params = {
        "b_ada": (b_ada, m_b_ada, v_b_ada), "g_pre_mix": (g_pre_mix, m_g_pre_mix, v_g_pre_mix),
        "g_post_mix": (g_post_mix, m_g_post_mix, v_g_post_mix), "ssm_lam_re": (ssm_lam_re, m_ssm_lam_re, v_ssm_lam_re),
        "ssm_lam_im": (ssm_lam_im, m_ssm_lam_im, v_ssm_lam_im),
        "ssm_log_step": (ssm_log_step, m_ssm_log_step, v_ssm_log_step),
        "ssm_b_re": (ssm_b_re, m_ssm_b_re, v_ssm_b_re), "ssm_b_im": (ssm_b_im, m_ssm_b_im, v_ssm_b_im),
        "ssm_c_re": (ssm_c_re, m_ssm_c_re, v_ssm_c_re), "ssm_c_im": (ssm_c_im, m_ssm_c_im, v_ssm_c_im),
        "ssm_d": (ssm_d, m_ssm_d, v_ssm_d), "glu_b": (glu_b, m_glu_b, v_glu_b),
        "g_out_ssm": (g_out_ssm, m_g_out_ssm, v_g_out_ssm), "conv_w": (conv_w, m_conv_w, v_conv_w),
        "g_out_conv": (g_out_conv, m_g_out_conv, v_g_out_conv), "g_pre_ffn": (g_pre_ffn, m_g_pre_ffn, v_g_pre_ffn),
        "g_post_ffn": (g_post_ffn, m_g_post_ffn, v_g_post_ffn),
        "ffn_conv_w": (ffn_conv_w, m_ffn_conv_w, v_ffn_conv_w),
    }

    def natural(a):
        return a[0] if a.ndim > 2 else a

    names = list(small_params)
    items = []
    for nm in names:
        w_, m_, v_ = small_params[nm]
        items.append((natural(w_), g_small[nm].reshape(natural(w_).shape), natural(m_), natural(v_)))
    upd = _adamw_small(items)
    small_out = {}
    for nm, (dl, mo, vo) in zip(names, upd):
        shp = small_params[nm][0].shape
        small_out[nm] = (g_small[nm].reshape(shp), dl.reshape(shp), mo.reshape(shp), vo.reshape(shp))

    loss = g_small["loss"][0, 0]

    order = ["w_ada", "b_ada", "g_pre_mix", "g_post_mix", "w_in", "ssm_lam_re", "ssm_lam_im", "ssm_log_step",
             "ssm_b_re", "ssm_b_im", "ssm_c_re", "ssm_c_im", "ssm_d", "glu_w", "glu_b", "g_out_ssm", "conv_w",
             "g_out_conv", "w_out", "g_pre_ffn", "g_post_ffn", "w_up", "ffn_conv_w", "w_down"]
    results = {"w_ada": tuple(a[None] for a in ada)}
    for nm in big:
        results[nm] = tuple(a[None] for a in big[nm])
    results.update(small_out)
    outs = [loss, grad_x[None]]
    for k in range(4):
        outs += [results[nm][k] for nm in order]
    return tuple(outs)


def _ga_rowsum(ga_re8, ga_im8):
    n = ga_re8.shape[1]

    def body(r_ref, i_ref, o_ref):
        o_ref[...] = jnp.zeros(o_ref.shape, F32)
        o_ref[0:1, :] = _colsum(r_ref[...])
        o_ref[1:2, :] = _colsum(i_ref[...])

    return pl.pallas_call(body, name="ga_rowsum", out_shape=jax.ShapeDtypeStruct((SUBLANES, n), F32))(ga_re8, ga_im8)
```

```python
import functools
import math

import jax
import jax.numpy as jnp
import numpy as np
from jax import lax
from jax.experimental import pallas as pl
from jax.experimental.pallas import tpu as pltpu

F32 = jnp.float32
BF16 = jnp.bfloat16
MESH = pl.DeviceIdType.MESH

EPS = 1e-6
LAMBDA_RE_MAX = -1e-4
ADAM_LR = 0.001
ADAM_B1 = 0.9
ADAM_B2 = 0.999
ADAM_EPS = 1e-08
ADAM_WD = 0.01
ADAM_STEP = 10

SUBLANES = 8
BF16_ROWS = 16
N_CHIPS = 4
N_DEV = 8
CONV_HEAD_DIM = 64
VMEM_BIG = 56 * 1024 * 1024
VMEM_MID = 40 * 1024 * 1024

TB_MIX = 256
TB_FFN = 256
TB_SCAN = 1024
W_SCAN = 256
SSM_SPLIT = 4
CW_FFN = 256
SCAN_UNROLL = 4
TB_TN = 512


def _cparams(sem=None, vmem=None):
    kw = {}
    if sem is not None:
        kw["dimension_semantics"] = sem
    if vmem is not None:
        kw["vmem_limit_bytes"] = vmem
    return pltpu.CompilerParams(**kw)


def _blk(t, pref):
    return pref if t % pref == 0 else t


def _dot(a, b):
    return jnp.dot(a.astype(BF16), b.astype(BF16), preferred_element_type=F32)


def _dot_nt(a, b):
    return lax.dot_general(a.astype(BF16), b.astype(BF16), (((1,), (1,)), ((), ())),
                           preferred_element_type=F32)


def _dot_tn(a, b):
    return lax.dot_general(a.astype(BF16), b.astype(BF16), (((0,), (0,)), ((), ())),
                           preferred_element_type=F32)


def _sigmoid(x):
    return 0.5 * jnp.tanh(0.5 * x) + 0.5


_GELU_K = math.sqrt(2.0 / math.pi)
_GELU_C = 0.044715


def _gelu(x):
    th = jnp.tanh(_GELU_K * (x + _GELU_C * x * x * x))
    return 0.5 * x * (1.0 + th)


def _gelu_grad(x):
    x2 = x * x
    th = jnp.tanh(_GELU_K * (x + _GELU_C * x2 * x))
    return 0.5 * (1.0 + th) + 0.5 * x * (1.0 - th * th) * _GELU_K * (1.0 + 3.0 * _GELU_C * x2)


def _rowmean(x):
    return jnp.mean(x, axis=-1, keepdims=True)


def _colsum(x):
    return jnp.sum(x, axis=0, keepdims=True)


def _split_dot(x, m):
    hi = x.astype(BF16)
    lo = (x - hi.astype(F32)).astype(BF16)
    return (jnp.dot(hi, m, preferred_element_type=F32) + jnp.dot(lo, m, preferred_element_type=F32))


def _split3_dot(x, m):
    hi = x.astype(BF16)
    r1 = x - hi.astype(F32)
    mid = r1.astype(BF16)
    lo = (r1 - mid.astype(F32)).astype(BF16)
    return (jnp.dot(hi, m, preferred_element_type=F32) + jnp.dot(mid, m, preferred_element_type=F32)
            + jnp.dot(lo, m, preferred_element_type=F32))


def _shift_down(x, halo, k):
    r = pltpu.roll(x, k, 0)
    row = lax.broadcasted_iota(jnp.int32, x.shape, 0)
    for j in range(k):
        r = jnp.where(row == j, halo[SUBLANES - k + j:SUBLANES - k + j + 1, :], r)
    return r


def _shift_up(x, halo, k):
    n = x.shape[0]
    r = pltpu.roll(x, n - k, 0)
    row = lax.broadcasted_iota(jnp.int32, x.shape, 0)
    for j in range(k):
        r = jnp.where(row == n - k + j, halo[j:j + 1, :], r)
    return r


def _acc_rows(ref, first, rows):
    @pl.when(first)
    def _():
        ref[...] = jnp.zeros(ref.shape, ref.dtype)
    for j, r in enumerate(rows):
        ref[j:j + 1, :] += r


def _rows(tb, c, col=0):
    return pl.BlockSpec((tb, c), lambda i, col=col: (i, col))


def _full(shape):
    nd = len(shape)
    return pl.BlockSpec(shape, lambda i, nd=nd: (0,) * nd)


def _resident(shape):
    nd = len(shape)
    return pl.BlockSpec(shape, lambda i, nd=nd: (0,) * nd, pipeline_mode=pl.Buffered(1))


def _halo_prev(tb, c, col=0):
    per = tb // SUBLANES
    return pl.BlockSpec((SUBLANES, c), lambda i, col=col: (jnp.maximum(i * per - 1, 0), col))


def _halo_next(tb, c, t, col=0, rows=SUBLANES):
    per = tb // rows
    last = t // rows - 1
    return pl.BlockSpec((rows, c), lambda i, col=col: (jnp.minimum((i + 1) * per, last), col))


def _mesh_pos():
    return lax.axis_index("x"), lax.axis_index("y"), lax.axis_index("c")


def _allgather8(x_pad, n_sum, name):
    m_per, n = x_pad.shape

    def body(x_ref, out_ref, sum_ref, send_sems, recv_sems, local_sem):
        x, y, c = _mesh_pos()
        me, sibling = (x, y, c), (x, y, 1 - c)
        chips = [(1 - x, y), (x, 1 - y), (1 - x, 1 - y)]

        def rows(px, py, pc):
            return out_ref.at[pl.ds((4 * px + 2 * py + pc) * m_per, m_per), :]

        def copy(k, block, to, src=None):
            return pltpu.make_async_remote_copy(
                src_ref=rows(*block) if src is None else src, dst_ref=rows(*block),
                send_sem=send_sems.at[k], recv_sem=recv_sems.at[k], device_id=to, device_id_type=MESH)

        mine = pltpu.make_async_copy(x_ref, rows(*me), local_sem)
        mine.start()
        first = [copy(0, me, sibling, src=x_ref)]
        first += [copy(1 + j, me, (*chip, c), src=x_ref) for j, chip in enumerate(chips)]
        for cp in first:
            cp.start()
        passed = [copy(4 + j, (*chip, c), sibling) for j, chip in enumerate(chips)]
        for j, chip in enumerate(chips):
            copy(1 + j, (*chip, c), me).wait_recv()
            passed[j].start()
        copy(0, sibling, me).wait_recv()
        for j, chip in enumerate(chips):
            copy(4 + j, (*chip, 1 - c), me).wait_recv()
        for cp in first + passed:
            cp.wait_send()
        mine.wait()
        acc = out_ref[0:n_sum, :]
        for k in range(1, N_DEV):
            acc = acc + out_ref[k * m_per:k * m_per + n_sum, :]
        sum_ref[...] = acc

    return pl.pallas_call(
        body, name=name,
        out_shape=(jax.ShapeDtypeStruct((N_DEV * m_per, n), F32), jax.ShapeDtypeStruct((n_sum, n), F32)),
        in_specs=[pl.BlockSpec(memory_space=pltpu.VMEM)],
        out_specs=(pl.BlockSpec(memory_space=pltpu.VMEM), pl.BlockSpec(memory_space=pltpu.VMEM)),
        scratch_shapes=[pltpu.SemaphoreType.DMA((7,)), pltpu.SemaphoreType.DMA((7,)), pltpu.SemaphoreType.DMA],
        compiler_params=_cparams(vmem=VMEM_MID),
    )(x_pad)


_HBM = pl.BlockSpec(memory_space=pltpu.HBM)
_SEM = pl.BlockSpec(memory_space=pltpu.SEMAPHORE)
_EFFECT = pltpu.SideEffectType.DATAFLOW_SIDE_EFFECTING


def _chip_copy(gather, src_ref, land_ref, send, recv, j, arrival):
    x, y, c = _mesh_pos()
    peer = [(1 - x, y), (x, 1 - y), (1 - x, 1 - y)][j]
    peer_chip = 2 * peer[0] + peer[1]
    my_chip = 2 * x + y
    return pltpu.make_async_remote_copy(
        src_ref=src_ref if gather else src_ref.at[peer_chip],
        dst_ref=land_ref.at[peer_chip if arrival else my_chip],
        send_sem=send.at[j], recv_sem=recv.at[j], device_id=(*peer, c), device_id_type=MESH)


def _chips_start(name, gather, srcs, lands, after=None):
    n = len(srcs)
    extra = [] if after is None else [after]

    def body(*refs):
        src_refs, land_refs = refs[:n], refs[n:2 * n]
        outs = refs[2 * n + len(extra):]
        sends, recvs, token = outs[:n], outs[n:2 * n], outs[-1]
        for k in range(n):
            for j in range(3):
                _chip_copy(gather, src_refs[k], land_refs[k], sends[k], recvs[k], j, False).start()
        token[...] = jnp.zeros(token.shape, F32)

    sem = pltpu.SemaphoreType.DMA((3,))
    thru = tuple(pltpu.HBM(a.shape, a.dtype) for a in list(srcs) + list(lands))
    res = pl.pallas_call(
        body, name=name,
        out_shape=(sem,) * (2 * n) + thru + (jax.ShapeDtypeStruct((SUBLANES, 128), F32),),
        in_specs=[_HBM] * (2 * n) + [pl.BlockSpec(memory_space=pl.ANY)] * len(extra),
        out_specs=(_SEM,) * (2 * n) + (_HBM,) * (2 * n) + (pl.BlockSpec(memory_space=pltpu.VMEM),),
        input_output_aliases={k: 2 * n + k for k in range(2 * n)},
        compiler_params=pltpu.CompilerParams(has_side_effects=_EFFECT),
    )(*[pltpu.with_memory_space_constraint(a, pltpu.HBM) for a in list(srcs) + list(lands)], *extra)
    return res[:n], res[n:2 * n], res[2 * n:3 * n], res[3 * n:4 * n], res[-1]


def _chips_wait(name, gather, sends, recvs, srcs, lands, after):
    n = len(srcs)

    def body(*refs):
        src_refs, land_refs = refs[:n], refs[n:2 * n]
        sends_, recvs_ = refs[2 * n:3 * n], refs[3 * n:4 * n]
        for k in range(n):
            for j in range(3):
                cp = _chip_copy(gather, src_refs[k], land_refs[k], sends_[k], recvs_[k], j, True)
                cp.wait_send()
                cp.wait_recv()

    thru = tuple(pltpu.HBM(a.shape, a.dtype) for a in list(srcs) + list(lands))
    res = pl.pallas_call(
        body, name=name, out_shape=thru,
        in_specs=[_HBM] * (2 * n) + [_SEM] * (2 * n) + [pl.BlockSpec(memory_space=pl.ANY)],
        out_specs=(_HBM,) * (2 * n),
        input_output_aliases={k: k for k in range(2 * n)},
        compiler_params=pltpu.CompilerParams(has_side_effects=_EFFECT),
    )(*srcs, *lands, *sends, *recvs, after)
    return res[n:]


def _landing(own, chip):
    zone = lax.empty((N_CHIPS,) + own.shape, own.dtype)
    return lax.dynamic_update_slice(zone, own[None], (chip,) + (0,) * own.ndim)


def _swap_sibling(arrs, name):
    n_arr = len(arrs)

    def body(*refs):
        ins, outs = refs[:n_arr], refs[n_arr:2 * n_arr]
        send_sems, recv_sems = refs[2 * n_arr:]
        x, y, c = _mesh_pos()
        copies = [pltpu.make_async_remote_copy(
            src_ref=ins[n], dst_ref=outs[n], send_sem=send_sems.at[n], recv_sem=recv_sems.at[n],
            device_id=(x, y, 1 - c), device_id_type=MESH) for n in range(n_arr)]
        for cp in copies:
            cp.start()
        for cp in copies:
            cp.wait()

    any_spec = pl.BlockSpec(memory_space=pl.ANY)
    return pl.pallas_call(
        body, name=name,
        out_shape=tuple(jax.ShapeDtypeStruct(a.shape, a.dtype) for a in arrs),
        in_specs=[any_spec] * n_arr, out_specs=tuple([any_spec] * n_arr),
        scratch_shapes=[pltpu.SemaphoreType.DMA((n_arr,)), pltpu.SemaphoreType.DMA((n_arr,))],
    )(*arrs)


def _mod_shard(c_all, w_ada_sh, b_sh):
    d, n = w_ada_sh.shape
    bn = 512

    def body(c_ref, w_ref, b_ref, o_ref):
        cc = c_ref[...]
        ca = cc * _sigmoid(cc)
        o_ref[...] = _dot(ca, w_ref[...]) + b_ref[...]

    return pl.pallas_call(
        body, name="mod_shard", grid=(n // bn,),
        out_shape=jax.ShapeDtypeStruct((N_DEV, n), F32),
        in_specs=[_full((N_DEV, d)), pl.BlockSpec((d, bn), lambda j: (0, j)), pl.BlockSpec((1, bn), lambda j: (0, j))],
        out_specs=pl.BlockSpec((N_DEV, bn), lambda j: (0, j)),
        compiler_params=_cparams(("parallel",)),
    )(c_all, w_ada_sh, b_sh)


def _ssm_prep(lam_re, lam_im, log_step):
    g, p = lam_re.shape

    def body(lr_ref, li_ref, ls_ref, ar_ref, ai_ref, cr_ref, ci_ref):
        lr = jnp.minimum(lr_ref[...], LAMBDA_RE_MAX)
        li = li_ref[...]
        st = jnp.exp(ls_ref[...])
        mag = jnp.exp(lr * st)
        ar = mag * jnp.cos(li * st)
        ai = mag * jnp.sin(li * st)
        den = lr * lr + li * li
        nr = ar - 1.0
        ar_ref[...] = ar
        ai_ref[...] = ai
        cr_ref[...] = (nr * lr + ai * li) / den
        ci_ref[...] = (ai * lr - nr * li) / den

    sds = jax.ShapeDtypeStruct((g, p), F32)
    return pl.pallas_call(body, name="ssm_prep", out_shape=(sds,) * 4)(lam_re, lam_im, log_step)


def _ssm_blocks(bt_re, bt_im, ct_re, ct_im, coef_rows):
    gh, gp = bt_re.shape
    nb = 4
    cb, rb = gp // nb, gp // nb

    def body(btr, bti, ctr, cti, cf, bre_o, bim_o, cre_o, cim_o):
        j = pl.program_id(0)
        row = lax.broadcasted_iota(jnp.int32, (gh, cb), 0)
        col = lax.broadcasted_iota(jnp.int32, (gh, cb), 1) + j * cb
        mask = (row >> 4) == (col >> 6)
        cr, ci = cf[0:1, :], cf[1:2, :]
        br, bi = btr[...], bti[...]
        bre_o[...] = jnp.where(mask, br * cr - bi * ci, 0.0).astype(BF16)
        bim_o[...] = jnp.where(mask, br * ci + bi * cr, 0.0).astype(BF16)
        row2 = lax.broadcasted_iota(jnp.int32, (rb, gh), 0) + j * rb
        col2 = lax.broadcasted_iota(jnp.int32, (rb, gh), 1)
        mask2 = (row2 >> 6) == (col2 >> 4)
        cre_o[...] = jnp.where(mask2, ctr[...], 0.0).astype(BF16)
        cim_o[...] = jnp.where(mask2, cti[...], 0.0).astype(BF16)

    bspec = pl.BlockSpec((gh, cb), lambda j: (0, j))
    cspec = pl.BlockSpec((rb, gh), lambda j: (j, 0))
    return pl.pallas_call(
        body, name="ssm_blocks", grid=(nb,),
        out_shape=(jax.ShapeDtypeStruct((gh, gp), BF16),) * 2 + (jax.ShapeDtypeStruct((gp, gh), BF16),) * 2,
        in_specs=[bspec, bspec, cspec, cspec, pl.BlockSpec((SUBLANES, cb), lambda j: (0, j))],
        out_specs=(bspec, bspec, cspec, cspec),
        compiler_params=_cparams(("parallel",)),
    )(bt_re, bt_im, ct_re, ct_im, coef_rows)


def _scan_consts(a_ref, reverse):
    w = a_ref.shape[1]
    ar1 = a_ref[0:1, :]
    ai1 = a_ref[1:2, :]
    if reverse:
        ai1 = -ai1
    pr, pi = [ar1], [ai1]
    for _ in range(1, SUBLANES):
        nr = pr[-1] * ar1 - pi[-1] * ai1
        ni = pr[-1] * ai1 + pi[-1] * ar1
        pr.append(nr)
        pi.append(ni)
    row = lax.broadcasted_iota(jnp.int32, (SUBLANES, w), 0)
    dist = (SUBLANES - 1 - row) if reverse else row

    def pick(vals):
        out = jnp.broadcast_to(vals[SUBLANES - 1], (SUBLANES, w))
        for r in range(SUBLANES - 1):
            out = jnp.where(dist == r, vals[r], out)
        return out

    p_r, p_i = pick(pr), pick(pi)
    steps = []
    for k in (1, 2, 4):
        steps.append((k, jnp.where(dist >= k, pr[k - 1], 0.0), jnp.where(dist >= k, pi[k - 1], 0.0)))
    a8 = (jnp.broadcast_to(pr[SUBLANES - 1], (SUBLANES, w)), jnp.broadcast_to(pi[SUBLANES - 1], (SUBLANES, w)))
    return row, p_r, p_i, steps, a8


def _scan_tile(xr, xi, cr, ci, consts, reverse):
    row, p_r, p_i, steps, (a8r, a8i) = consts
    for k, s_r, s_i in steps:
        sh = (SUBLANES - k) if reverse else k
        qr = pltpu.roll(xr, sh, 0)
        qi = pltpu.roll(xi, sh, 0)
        xr, xi = xr + s_r * qr - s_i * qi, xi + s_r * qi + s_i * qr
    outr = xr + p_r * cr - p_i * ci
    outi = xi + p_r * ci + p_i * cr
    e = 0 if reverse else SUBLANES - 1
    er = jnp.broadcast_to(xr[e:e + 1, :], xr.shape)
    ei = jnp.broadcast_to(xi[e:e + 1, :], xi.shape)
    return outr, outi, er + a8r * cr - a8i * ci, ei + a8r * ci + a8i * cr


def _scan_fwd(a_rows, bu_re, bu_im):
    t, n = bu_re.shape
    tb, w = _blk(t, TB_SCAN), W_SCAN
    ntile = tb // SUBLANES

    def body(a_ref, br_ref, bi_ref, sr_ref, si_ref, car, cai):
        @pl.when(pl.program_id(1) == 0)
        def _():
            car[...] = jnp.zeros(car.shape, F32)
            cai[...] = jnp.zeros(cai.shape, F32)
        consts = _scan_consts(a_ref, False)

        def pair(i, carry):
            o = pl.multiple_of(i * BF16_ROWS, BF16_ROWS)
            outs = []
            for h in range(2):
                rows = pl.ds(o + h * SUBLANES, SUBLANES)
                outr, outi, ncr, nci = _scan_tile(br_ref[rows, :], bi_ref[rows, :], carry[0], carry[1], consts, False)
                outs.append((outr, outi))
                carry = (ncr, nci)
            sr_ref[pl.ds(o, BF16_ROWS), :] = jnp.concatenate([outs[0][0], outs[1][0]], axis=0).astype(BF16)
            si_ref[pl.ds(o, BF16_ROWS), :] = jnp.concatenate([outs[0][1], outs[1][1]], axis=0).astype(BF16)
            return carry

        def pairs(i, carry):
            for s in range(SCAN_UNROLL // 2):
                carry = pair(i * (SCAN_UNROLL // 2) + s, carry)
            return carry

        cr, ci = lax.fori_loop(0, ntile // SCAN_UNROLL, pairs, (car[...], cai[...]))
        car[...] = cr
        cai[...] = ci

    spec = pl.BlockSpec((tb, w), lambda s, k: (k, s))
    sds = jax.ShapeDtypeStruct((t, n), BF16)
    return pl.pallas_call(
        body, name="scan_fwd", grid=(n // w, t // tb), out_shape=(sds, sds),
        in_specs=[pl.BlockSpec((SUBLANES, w), lambda s, k: (0, s)), spec, spec], out_specs=(spec, spec),
        scratch_shapes=[pltpu.VMEM((SUBLANES, w), F32), pltpu.VMEM((SUBLANES, w), F32)],
        compiler_params=_cparams(("parallel", "arbitrary"), VMEM_MID),
    )(a_rows, bu_re, bu_im)


def _scan_bwd(a_rows, g_re, g_im, s_re, s_im):
    t, n = g_re.shape
    tb, w = _blk(t, TB_SCAN), W_SCAN
    ntile = tb // SUBLANES
    npair = tb // BF16_ROWS
    nt = t // tb

    def body(a_ref, gr_ref, gi_ref, sr_ref, si_ref, or_ref, oi_ref, gar_ref, gai_ref, car, cai):
        @pl.when(pl.program_id(1) == 0)
        def _():
            car[...] = jnp.zeros(car.shape, F32)
            cai[...] = jnp.zeros(cai.shape, F32)
            gar_ref[...] = jnp.zeros(gar_ref.shape, F32)
            gai_ref[...] = jnp.zeros(gai_ref.shape, F32)
        consts = _scan_consts(a_ref, True)
        row = consts[0]

        def pair(i, carry):
            cr, ci, accr, acci = carry
            o = pl.multiple_of((npair - 1 - i) * BF16_ROWS, BF16_ROWS)
            s_r = sr_ref[pl.ds(o, BF16_ROWS), :].astype(F32)
            s_i = si_ref[pl.ds(o, BF16_ROWS), :].astype(F32)
            outs = [None, None]
            for h in (1, 0):
                rows = pl.ds(o + h * SUBLANES, SUBLANES)
                outr, outi, ncr, nci = _scan_tile(gr_ref[rows, :], gi_ref[rows, :], cr, ci, consts, True)
                outs[h] = (outr, outi)
                gnr = jnp.where(row == SUBLANES - 1, cr, pltpu.roll(outr, SUBLANES - 1, 0))
                gni = jnp.where(row == SUBLANES - 1, ci, pltpu.roll(outi, SUBLANES - 1, 0))
                sr = s_r[h * SUBLANES:(h + 1) * SUBLANES, :]
                si = s_i[h * SUBLANES:(h + 1) * SUBLANES, :]
                accr, acci = accr + sr * gnr + si * gni, acci + sr * gni - si * gnr
                cr, ci = ncr, nci
            or_ref[pl.ds(o, BF16_ROWS), :] = jnp.concatenate([outs[0][0], outs[1][0]], axis=0).astype(BF16)
            oi_ref[pl.ds(o, BF16_ROWS), :] = jnp.concatenate([outs[0][1], outs[1][1]], axis=0).astype(BF16)
            return cr, ci, accr, acci

        def pairs(i, carry):
            for s in range(SCAN_UNROLL // 2):
                carry = pair(i * (SCAN_UNROLL // 2) + s, carry)
            return carry

        cr, ci, accr, acci = lax.fori_loop(0, ntile // SCAN_UNROLL, pairs,
                                           (car[...], cai[...], gar_ref[...], gai_ref[...]))
        car[...] = cr
        cai[...] = ci
        gar_ref[...] = accr
        gai_ref[...] = acci

    spec = pl.BlockSpec((tb, w), lambda s, k: (nt - 1 - k, s))
    aspec = pl.BlockSpec((SUBLANES, w), lambda s, k: (0, s))
    sds = jax.ShapeDtypeStruct((t, n), BF16)
    asds = jax.ShapeDtypeStruct((SUBLANES, n), F32)
    return pl.pallas_call(
        body, name="scan_bwd", grid=(n // w, nt), out_shape=(sds, sds, asds, asds),
        in_specs=[aspec, spec, spec, spec, spec], out_specs=(spec, spec, aspec, aspec),
        scratch_shapes=[pltpu.VMEM((SUBLANES, w), F32), pltpu.VMEM((SUBLANES, w), F32)],
        compiler_params=_cparams(("parallel", "arbitrary"), VMEM_MID),
    )(a_rows, g_re, g_im, s_re, s_im)


def _mix_in(x, vec, w_in_st, b_re, b_im):
    t, d = x.shape
    ns, _, nc = w_in_st.shape
    dssm, nstate = b_re.shape
    du, ds = dssm // SSM_SPLIT, nstate // SSM_SPLIT
    tb = _blk(t, TB_MIX)

    def body(x_ref, vec_ref, w_ref, bre_ref, bim_ref, proj_ref, bur_ref, bui_ref, h1_ref):
        xv = x_ref[...]
        r = lax.rsqrt(_rowmean(xv * xv) + EPS)
        h = xv * r * vec_ref[0:1, :] * vec_ref[1:2, :] + vec_ref[2:3, :]
        hb = h.astype(BF16)
        h1_ref[...] = hb
        u = None
        for j in range(ns):
            pj = jnp.dot(hb, w_ref[j], preferred_element_type=F32)
            proj_ref[:, j * nc:(j + 1) * nc] = pj
            if j == 0:
                u = pj
        ub = u.astype(BF16)
        for q in range(SSM_SPLIT):
            rq, cq = slice(q * du, (q + 1) * du), slice(q * ds, (q + 1) * ds)
            bur_ref[:, cq] = jnp.dot(ub[:, rq], bre_ref[rq, cq], preferred_element_type=F32)
            bui_ref[:, cq] = jnp.dot(ub[:, rq], bim_ref[rq, cq], preferred_element_type=F32)

    return pl.pallas_call(
        body, name="mix_in", grid=(t // tb,),
        out_shape=(jax.ShapeDtypeStruct((t, ns * nc), F32), jax.ShapeDtypeStruct((t, nstate), F32),
                   jax.ShapeDtypeStruct((t, nstate), F32), jax.ShapeDtypeStruct((t, d), BF16)),
        in_specs=[_rows(tb, d), _full((SUBLANES, d)), _full(w_in_st.shape), _full(b_re.shape), _full(b_im.shape)],
        out_specs=(_rows(tb, ns * nc), _rows(tb, nstate), _rows(tb, nstate), _rows(tb, d)),
        compiler_params=_cparams(("parallel",), VMEM_BIG),
    )(x, vec, w_in_st, b_re, b_im)


def _head_ms(y, h_ref):
    return _split_dot(y * y, h_ref[...])


def _conv3(x, halo, w_ref):
    return w_ref[0:1, :] * _shift_down(x, halo, 2) + w_ref[1:2, :] * _shift_down(x, halo, 1) + w_ref[2:3, :] * x


def _mix_out(x, proj, s_re, s_im, c_re, c_im, v512, convw, glu_w, h16, h64, w_out, vd):
    t, d = x.shape
    dh = c_re.shape[1]
    nstate = s_re.shape[1]
    du, ds = dh // SSM_SPLIT, nstate // SSM_SPLIT
    tb = _blk(t, TB_MIX)

    def body(x_ref, u_ref, bg_ref, cg_ref, v_ref, cgh_ref, vh_ref, sr_ref, si_ref, cre_ref, cim_ref, p_ref,
             cw_ref, gw_ref, h16_ref, h64_ref, wo_ref, vd_ref, y1_ref, o_ref, x2_ref):
        i = pl.program_id(0)
        u = u_ref[...]
        ys = []
        for q in range(SSM_SPLIT):
            rq, cq = slice(q * ds, (q + 1) * ds), slice(q * du, (q + 1) * du)
            ys.append(_dot(sr_ref[:, rq], cre_ref[rq, cq]) - _dot(si_ref[:, rq], cim_ref[rq, cq]))
        ys = jnp.concatenate(ys, axis=1)
        y1 = ys + p_ref[0:1, :] * u
        y1_ref[...] = y1
        z = _gelu(y1)
        q = _dot(z, gw_ref[...]) + p_ref[1:2, :]
        ya = z * _sigmoid(q)
        na = ya * lax.rsqrt(_head_ms(ya, h16_ref) + EPS) * p_ref[2:3, :]
        cv = cg_ref[...] * v_ref[...]
        cvh = jnp.where(i > 0, cgh_ref[...] * vh_ref[...], 0.0)
        yb = bg_ref[...] * _conv3(cv, cvh, cw_ref)
        nb = yb * lax.rsqrt(_head_ms(yb, h64_ref) + EPS) * p_ref[3:4, :]
        o = _dot(na, wo_ref[0:dh, :]) + _dot(nb, wo_ref[dh:2 * dh, :])
        o_ref[...] = o
        on = o * lax.rsqrt(_rowmean(o * o) + EPS) * vd_ref[0:1, :]
        x2_ref[...] = x_ref[...] + vd_ref[1:2, :] * on

    return pl.pallas_call(
        body, name="mix_out", grid=(t // tb,),
        out_shape=(jax.ShapeDtypeStruct((t, dh), F32), jax.ShapeDtypeStruct((t, d), F32),
                   jax.ShapeDtypeStruct((t, d), F32)),
        in_specs=[_rows(tb, d), _rows(tb, dh, 0), _rows(tb, dh, 1), _rows(tb, dh, 2), _rows(tb, dh, 3),
                  _halo_prev(tb, dh, 2), _halo_prev(tb, dh, 3), _rows(tb, nstate), _rows(tb, nstate),
                  _full(c_re.shape), _full(c_im.shape), _full(v512.shape), _full(convw.shape), _full(glu_w.shape),
                  _full(h16.shape), _full(h64.shape), _full(w_out.shape), _full(vd.shape)],
        out_specs=(_rows(tb, dh), _rows(tb, d), _rows(tb, d)),
        compiler_params=_cparams(("parallel",), VMEM_BIG),
    )(x, proj, proj, proj, proj, proj, proj, s_re, s_im, c_re, c_im, v512, convw, glu_w, h16, h64, w_out, vd)


def _ffn_up(x2, vec, w_up_st):
    t, d = x2.shape
    ns, _, nc = w_up_st.shape
    tb = _blk(t, TB_FFN)

    def body(x_ref, vec_ref, w_ref, up_ref, h2_ref):
        xv = x_ref[...]
        r = lax.rsqrt(_rowmean(xv * xv) + EPS)
        h = xv * r * vec_ref[0:1, :] * vec_ref[1:2, :] + vec_ref[2:3, :]
        hb = h.astype(BF16)
        h2_ref[...] = hb
        for j in range(ns):
            up_ref[:, j * nc:(j + 1) * nc] = jnp.dot(hb, w_ref[j], preferred_element_type=F32)

    return pl.pallas_call(
        body, name="ffn_up", grid=(t // tb,),
        out_shape=(jax.ShapeDtypeStruct((t, ns * nc), F32), jax.ShapeDtypeStruct((t, d), BF16)),
        in_specs=[_rows(tb, d), _full((SUBLANES, d)), _resident(w_up_st.shape)],
        out_specs=(_rows(tb, ns * nc), _rows(tb, d)),
        compiler_params=_cparams(("parallel",), VMEM_BIG),
    )(x2, vec, w_up_st)


def _ffn_down(up, fw, w_down, w_down_t, x2, tgt, vd):
    t, nh = up.shape
    dff, d = w_down.shape
    tb = _blk(t, TB_FFN)
    inv_d = 1.0 / d

    def body(up_ref, uph_ref, fw_ref, wd_ref, wdt_ref, x2_ref, tgt_ref, vd_ref,
             act_ref, ddn_ref, dout_ref, dhid_ref, vec_ref, loss_ref, a_s, vv_s, sg_s):
        i = pl.program_id(0)

        def conv_cols(sl):
            x = up_ref[:, sl]
            halo = jnp.where(i > 0, uph_ref[:, sl], 0.0)
            return (fw_ref[0:1, sl] * _shift_down(x, halo, 2) + fw_ref[1:2, sl] * _shift_down(x, halo, 1)
                    + fw_ref[2:3, sl] * x)

        dn = None
        for o in range(0, dff, CW_FFN):
            sl = slice(o, o + CW_FFN)
            a = conv_cols(sl)
            vv = conv_cols(slice(dff + o, dff + o + CW_FFN))
            sg = _sigmoid(a)
            a_s[:, sl] = a
            vv_s[:, sl] = vv
            sg_s[:, sl] = sg
            actb = (a * sg * vv).astype(BF16)
            act_ref[:, sl] = actb
            pj = lax.dot_general(actb, wdt_ref[:, sl], (((1,), (1,)), ((), ())), preferred_element_type=F32)
            dn = pj if dn is None else dn + pj
        r3 = lax.rsqrt(_rowmean(dn * dn) + EPS)
        xn = dn * r3
        g = vd_ref[0:1, :]
        gt2 = vd_ref[1:2, :]
        dnn = xn * g
        diff = x2_ref[...] + gt2 * dnn - tgt_ref[...]
        part = 0.5 * inv_d * jnp.sum(diff * diff)

        @pl.when(i == 0)
        def _():
            loss_ref[...] = jnp.zeros(loss_ref.shape, F32)
        loss_ref[...] += part
        dout = diff * inv_d
        dout_ref[...] = dout
        ddnn = dout * gt2
        _acc_rows(vec_ref, i == 0, [_colsum(dout * dnn), _colsum(ddnn * xn)])
        dxn = ddnn * g
        ddn = r3 * (dxn - xn * _rowmean(dxn * xn))
        ddnb = ddn.astype(BF16)
        ddn_ref[...] = ddnb
        for o in range(0, dff, CW_FFN):
            sl = slice(o, o + CW_FFN)
            dact = lax.dot_general(ddnb, wd_ref[sl, :], (((1,), (1,)), ((), ())), preferred_element_type=F32)
            a, vv, sg = a_s[:, sl], vv_s[:, sl], sg_s[:, sl]
            dhid_ref[:, sl] = (dact * vv * sg * (1.0 + a * (1.0 - sg))).astype(BF16)
            dhid_ref[:, dff + o:dff + o + CW_FFN] = (dact * (a * sg)).astype(BF16)

    return pl.pallas_call(
        body, name="ffn_down", grid=(t // tb,),
        scratch_shapes=[pltpu.VMEM((tb, dff), F32)] * 3,
        out_shape=(jax.ShapeDtypeStruct((t, dff), BF16), jax.ShapeDtypeStruct((t, d), BF16),
                   jax.ShapeDtypeStruct((t, d), F32), jax.ShapeDtypeStruct((t, nh), BF16),
                   jax.ShapeDtypeStruct((SUBLANES, d), F32), jax.ShapeDtypeStruct((SUBLANES, 128), F32)),
        in_specs=[_rows(tb, nh), _halo_prev(tb, nh), _full(fw.shape), _resident(w_down.shape),
                  _resident(w_down_t.shape), _rows(tb, d),
                  _rows(tb, d), _full(vd.shape)],
        out_specs=(_rows(tb, dff), _rows(tb, d), _rows(tb, d), _rows(tb, nh), _full((SUBLANES, d)),
                   _full((SUBLANES, 128))),
        compiler_params=_cparams(("arbitrary",), VMEM_BIG),
    )(up, up, fw, w_down, w_down_t, x2, tgt, vd)


def _ffn_up_bwd(dhid, up, fw, x2, dout, vec, w_up_st):
    t, nh = dhid.shape
    d = x2.shape[1]
    ns, _, nc = w_up_st.shape
    tb = _blk(t, TB_FFN)
    nblk = t // tb
    cw = 128

    def body(dh_ref, dhn_ref, up_ref, fw_ref, x2_ref, dout_ref, vec_ref, w_ref,
             dx2_ref, dup_ref, vp_ref, df_ref):
        i = pl.program_id(0)

        @pl.when(i == 0)
        def _():
            df_ref[...] = jnp.zeros(df_ref.shape, F32)
        dh2 = None
        for j in range(ns):
            for o in range(j * nc, (j + 1) * nc, cw):
                sl = slice(o, o + cw)
                dh = dh_ref[:, sl].astype(F32)
                dhn = jnp.where(i < nblk - 1, dhn_ref[:, sl].astype(F32), 0.0)
                dh1 = _shift_up(dh, dhn, 1)
                dh2s = _shift_up(dh, dhn, 2)
                dup_ref[:, sl] = (fw_ref[2:3, sl] * dh + fw_ref[1:2, sl] * dh1 + fw_ref[0:1, sl] * dh2s).astype(BF16)
                up_v = up_ref[:, sl]
                df_ref[0:1, sl] += _colsum(dh2s * up_v)
                df_ref[1:2, sl] += _colsum(dh1 * up_v)
                df_ref[2:3, sl] += _colsum(dh * up_v)
            pj = lax.dot_general(dup_ref[:, j * nc:(j + 1) * nc], w_ref[j], (((1,), (1,)), ((), ())),
                                 preferred_element_type=F32)
            dh2 = pj if dh2 is None else dh2 + pj
        xv = x2_ref[...]
        r = lax.rsqrt(_rowmean(xv * xv) + EPS)
        xn = xv * r
        g = vec_ref[0:1, :]
        hg = xn * g
        dhg = dh2 * vec_ref[1:2, :]
        _acc_rows(vp_ref, i == 0, [_colsum(dh2), _colsum(dh2 * hg), _colsum(dhg * xn)])
        dxn = dhg * g
        dx2_ref[...] = dout_ref[...] + r * (dxn - xn * _rowmean(dxn * xn))

    return pl.pallas_call(
        body, name="ffn_up_bwd", grid=(nblk,),
        out_shape=(jax.ShapeDtypeStruct((t, d), F32), jax.ShapeDtypeStruct((t, nh), BF16),
                   jax.ShapeDtypeStruct((SUBLANES, d), F32), jax.ShapeDtypeStruct((SUBLANES, nh), F32)),
        in_specs=[_rows(tb, nh), _halo_next(tb, nh, t, rows=BF16_ROWS), _rows(tb, nh), _full(fw.shape),
                  _rows(tb, d), _rows(tb, d), _full(vec.shape), _resident(w_up_st.shape)],
        out_specs=(_rows(tb, d), _rows(tb, nh), _full((SUBLANES, d)), _full((SUBLANES, nh))),
        compiler_params=_cparams(("arbitrary",), VMEM_BIG),
    )(dhid, dhid, up, fw, x2, dout, vec, w_up_st)


def _mix_out_bwd(dx2, o, y1, proj, c_re, c_im, v512, convw, glu_w, h16, h64, w_out, vd):
    t, d = dx2.shape
    dh = y1.shape[1]
    nstate = c_re.shape[0]
    du, ds = dh // SSM_SPLIT, nstate // SSM_SPLIT
    tb = _blk(t, TB_MIX)

    def body(dx2_ref, o_ref, y1_ref, u_ref, bg_ref, cg_ref, v_ref, cgh_ref, vh_ref, cre_ref, cim_ref, p_ref,
             cw_ref, gw_ref, h16_ref, h64_ref, wo_ref, vd_ref,
             do_ref, ycat_ref, z_ref, dq_ref, dy1_ref, gr_ref, gi_ref, dcc_ref, dbg_ref, vpd_ref, vp5_ref):
        i = pl.program_id(0)
        first = i == 0
        ov = o_ref[...]
        ro = lax.rsqrt(_rowmean(ov * ov) + EPS)
        on_ = ov * ro
        g = vd_ref[0:1, :]
        dx2v = dx2_ref[...]
        don = dx2v * vd_ref[1:2, :]
        _acc_rows(vpd_ref, first, [_colsum(dx2v * on_ * g), _colsum(don * on_)])
        dxn = don * g
        dob = (ro * (dxn - on_ * _rowmean(dxn * on_))).astype(BF16)
        do_ref[...] = dob
        dyc_a = lax.dot_general(dob, wo_ref[0:dh, :], (((1,), (1,)), ((), ())), preferred_element_type=F32)
        dyc_b = lax.dot_general(dob, wo_ref[dh:2 * dh, :], (((1,), (1,)), ((), ())), preferred_element_type=F32)
        y1v = y1_ref[...]
        u = u_ref[...]
        z = _gelu(y1v)
        zb = z.astype(BF16)
        z_ref[...] = zb
        sg = _sigmoid(jnp.dot(zb, gw_ref[...], preferred_element_type=F32) + p_ref[1:2, :])
        ya = z * sg
        ra = lax.rsqrt(_head_ms(ya, h16_ref) + EPS)
        yan = ya * ra
        ga = p_ref[2:3, :]
        ycat_ref[:, 0:dh] = (yan * ga).astype(BF16)
        dyn = dyc_a * ga
        dya = ra * (dyn - yan * _split_dot(dyn * yan, h16_ref[...]))
        dq = dya * z * sg * (1.0 - sg)
        dqb = dq.astype(BF16)
        dq_ref[...] = dqb
        dz = dya * sg + lax.dot_general(dqb, gw_ref[...], (((1,), (1,)), ((), ())), preferred_element_type=F32)
        dy1 = dz * _gelu_grad(y1v)
        dy1_ref[...] = dy1
        dy1b = dy1.astype(BF16)
        for q in range(SSM_SPLIT):
            rq, cq = slice(q * ds, (q + 1) * ds), slice(q * du, (q + 1) * du)
            gr_ref[:, rq] = lax.dot_general(dy1b[:, cq], cre_ref[rq, cq], (((1,), (1,)), ((), ())),
                                            preferred_element_type=F32)
            gi_ref[:, rq] = -lax.dot_general(dy1b[:, cq], cim_ref[rq, cq], (((1,), (1,)), ((), ())),
                                             preferred_element_type=F32)
        bg = bg_ref[...]
        cv = cg_ref[...] * v_ref[...]
        cvh = jnp.where(i > 0, cgh_ref[...] * vh_ref[...], 0.0)
        cv1 = _shift_down(cv, cvh, 1)
        cv2 = _shift_down(cv, cvh, 2)
        cc = cw_ref[0:1, :] * cv2 + cw_ref[1:2, :] * cv1 + cw_ref[2:3, :] * cv
        yb = bg * cc
        rb = lax.rsqrt(_head_ms(yb, h64_ref) + EPS)
        ybn = yb * rb
        gb = p_ref[3:4, :]
        ycat_ref[:, dh:2 * dh] = (ybn * gb).astype(BF16)
        dynb = dyc_b * gb
        dyb = rb * (dynb - ybn * _split_dot(dynb * ybn, h64_ref[...]))
        dcc = dyb * bg
        dbg_ref[...] = dyb * cc
        dcc_ref[...] = dcc
        _acc_rows(vp5_ref, first, [_colsum(dyc_a * yan), _colsum(dyc_b * ybn), _colsum(dq), _colsum(dy1 * u),
                                   _colsum(dcc * cv2), _colsum(dcc * cv1), _colsum(dcc * cv)])

    return pl.pallas_call(
        body, name="mix_out_bwd", grid=(t // tb,),
        out_shape=(jax.ShapeDtypeStruct((t, d), BF16), jax.ShapeDtypeStruct((t, 2 * dh), BF16),
                   jax.ShapeDtypeStruct((t, dh), BF16), jax.ShapeDtypeStruct((t, dh), BF16),
                   jax.ShapeDtypeStruct((t, dh), F32), jax.ShapeDtypeStruct((t, nstate), F32),
                   jax.ShapeDtypeStruct((t, nstate), F32), jax.ShapeDtypeStruct((t, dh), F32),
                   jax.ShapeDtypeStruct((t, dh), F32), jax.ShapeDtypeStruct((SUBLANES, d), F32),
                   jax.ShapeDtypeStruct((SUBLANES, dh), F32)),
        in_specs=[_rows(tb, d), _rows(tb, d), _rows(tb, dh), _rows(tb, dh, 0), _rows(tb, dh, 1), _rows(tb, dh, 2),
                  _rows(tb, dh, 3), _halo_prev(tb, dh, 2), _halo_prev(tb, dh, 3), _full(c_re.shape), _full(c_im.shape),
                  _full(v512.shape), _full(convw.shape), _full(glu_w.shape), _full(h16.shape), _full(h64.shape),
                  _full(w_out.shape), _full(vd.shape)],
        out_specs=(_rows(tb, d), _rows(tb, 2 * dh), _rows(tb, dh), _rows(tb, dh), _rows(tb, dh), _rows(tb, nstate),
                   _rows(tb, nstate), _rows(tb, dh), _rows(tb, dh), _full((SUBLANES, d)), _full((SUBLANES, dh))),
        compiler_params=_cparams(("arbitrary",), VMEM_BIG),
    )(dx2, o, y1, proj, proj, proj, proj, proj, proj, c_re, c_im, v512, convw, glu_w, h16, h64, w_out, vd)


def _mix_in_bwd(gt_re, gt_im, b_re, b_im, dy1, dcc, dbg, proj, x, dx2, vec, v512, convw, w_in_st):
    t, d = x.shape
    dh = dy1.shape[1]
    nstate = gt_re.shape[1]
    du_w, ds = dh // SSM_SPLIT, nstate // SSM_SPLIT
    ns, _, nc = w_in_st.shape
    tb = _blk(t, TB_MIX)
    nblk = t // tb

    def body(gr_ref, gi_ref, bre_ref, bim_ref, dy1_ref, dcc_ref, dccn_ref, dbg_ref, cg_ref, v_ref, x_ref, dx2_ref,
             vec_ref, p_ref, cw_ref, w_ref, gx_ref, dproj_ref, vp_ref):
        i = pl.program_id(0)
        du = []
        for q in range(SSM_SPLIT):
            rq, cq = slice(q * du_w, (q + 1) * du_w), slice(q * ds, (q + 1) * ds)
            du.append(lax.dot_general(gr_ref[:, cq].astype(BF16), bre_ref[rq, cq], (((1,), (1,)), ((), ())),
                                      preferred_element_type=F32)
                      + lax.dot_general(gi_ref[:, cq].astype(BF16), bim_ref[rq, cq], (((1,), (1,)), ((), ())),
                                        preferred_element_type=F32))
        du = dy1_ref[...] * p_ref[0:1, :] + jnp.concatenate(du, axis=1)
        dcc = dcc_ref[...]
        dccn = jnp.where(i < nblk - 1, dccn_ref[...], 0.0)
        dcv = (cw_ref[2:3, :] * dcc + cw_ref[1:2, :] * _shift_up(dcc, dccn, 1)
               + cw_ref[0:1, :] * _shift_up(dcc, dccn, 2))
        parts = [du, dbg_ref[...], dcv * v_ref[...], dcv * cg_ref[...]]
        dh1 = None
        for j in range(ns):
            pb = parts[j].astype(BF16)
            dproj_ref[:, j * nc:(j + 1) * nc] = pb
            pj = lax.dot_general(pb, w_ref[j], (((1,), (1,)), ((), ())), preferred_element_type=F32)
            dh1 = pj if dh1 is None else dh1 + pj
        xv = x_ref[...]
        r = lax.rsqrt(_rowmean(xv * xv) + EPS)
        xn = xv * r
        g = vec_ref[0:1, :]
        hg = xn * g
        dhg = dh1 * vec_ref[1:2, :]
        _acc_rows(vp_ref, i == 0, [_colsum(dh1), _colsum(dh1 * hg), _colsum(dhg * xn)])
        dxn = dhg * g
        gx_ref[...] = dx2_ref[...] + r * (dxn - xn * _rowmean(dxn * xn))

    assert nc == dh and ns == 4
    return pl.pallas_call(
        body, name="mix_in_bwd", grid=(nblk,),
        out_shape=(jax.ShapeDtypeStruct((t, d), F32), jax.ShapeDtypeStruct((t, ns * nc), BF16),
                   jax.ShapeDtypeStruct((SUBLANES, d), F32)),
        in_specs=[_rows(tb, nstate), _rows(tb, nstate), _full(b_re.shape), _full(b_im.shape), _rows(tb, dh),
                  _rows(tb, dh), _halo_next(tb, dh, t), _rows(tb, dh), _rows(tb, dh, 2), _rows(tb, dh, 3),
                  _rows(tb, d), _rows(tb, d), _full(vec.shape), _full(v512.shape), _full(convw.shape),
                  _full(w_in_st.shape)],
        out_specs=(_rows(tb, d), _rows(tb, ns * nc), _full((SUBLANES, d))),
        compiler_params=_cparams(("arbitrary",), VMEM_BIG),
    )(gt_re, gt_im, b_re, b_im, dy1, dcc, dcc, dbg, proj, proj, x, dx2, vec, v512, convw, w_in_st)


def _matmul_tn(a, b, m, bn, out_dtype, name, diag=False, bt=TB_TN, after=None):
    t = a.shape[0]
    n = b.shape[1]
    bt = _blk(t, bt)
    nk = t // bt
    extra = [] if after is None else [after]
    a_map = (lambda j, k: (k, j)) if diag else (lambda j, k: (k, 0))

    def body(a_ref, b_ref, *rest):
        o_ref, acc_ref = rest[-2:]
        k = pl.program_id(1)

        @pl.when(k == 0)
        def _():
            acc_ref[...] = jnp.zeros(acc_ref.shape, F32)
        acc_ref[...] += _dot_tn(a_ref[...], b_ref[...])

        @pl.when(k == nk - 1)
        def _():
            o_ref[...] = acc_ref[...].astype(out_dtype)

    return pl.pallas_call(
        body, name=name, grid=(n // bn, nk),
        out_shape=jax.ShapeDtypeStruct((n // bn, m, bn), out_dtype),
        in_specs=[pl.BlockSpec((bt, m), a_map), pl.BlockSpec((bt, bn), lambda j, k: (k, j))]
        + [pl.BlockSpec(memory_space=pl.ANY)] * len(extra),
        out_specs=pl.BlockSpec((None, m, bn), lambda j, k: (j, 0, 0)),
        scratch_shapes=[pltpu.VMEM((m, bn), F32)],
        compiler_params=_cparams(("parallel", "arbitrary"), VMEM_BIG),
    )(a, b, *extra)


def _ssm_bgrad(d_bre, d_bim, bt_re, bt_im, rows_in, fold):
    gh, cb = d_bre.shape
    nb = SSM_SPLIT
    rb = gh // nb
    gp = nb * cb
    p = fold.shape[1]

    def body(dr_ref, di_ref, br_ref, bi_ref, rin_ref, f_ref, dbr_ref, dbi_ref, rout_ref):
        row = lax.broadcasted_iota(jnp.int32, (rb, cb), 0)
        col = lax.broadcasted_iota(jnp.int32, (rb, cb), 1)
        mask = (row >> 4) == (col >> 6)
        gr = jnp.where(mask, dr_ref[...], 0.0)
        gi = jnp.where(mask, di_ref[...], 0.0)
        cr, ci = rin_ref[0:1, :], rin_ref[1:2, :]
        dbr_ref[...] = _split3_dot(cr * gr + ci * gi, f_ref[...])
        dbi_ref[...] = _split3_dot(cr * gi - ci * gr, f_ref[...])
        br, bi = br_ref[...], bi_ref[...]
        rout_ref[...] = jnp.zeros(rout_ref.shape, F32)
        rout_ref[0:1, :] = _colsum(br * gr + bi * gi)
        rout_ref[1:2, :] = _colsum(br * gi - bi * gr)

    dspec = pl.BlockSpec((rb, cb), lambda j: (j, 0))
    bspec = pl.BlockSpec((rb, cb), lambda j: (j, j))
    rspec = pl.BlockSpec((SUBLANES, cb), lambda j: (0, j))
    ospec = pl.BlockSpec((rb, p), lambda j: (j, 0))
    return pl.pallas_call(
        body, name="ssm_bgrad", grid=(nb,),
        out_shape=(jax.ShapeDtypeStruct((gh, p), F32), jax.ShapeDtypeStruct((gh, p), F32),
                   jax.ShapeDtypeStruct((SUBLANES, gp), F32)),
        in_specs=[dspec, dspec, bspec, bspec, rspec, _full(fold.shape)],
        out_specs=(ospec, ospec, rspec),
        compiler_params=_cparams(("parallel",)),
    )(d_bre, d_bim, bt_re, bt_im, rows_in, fold)


def _ssm_cgrad(d_cre, d_cim, fold):
    gp, cb = d_cre.shape
    nb = SSM_SPLIT
    rb = gp // nb
    h = fold.shape[1]

    def body(dr_ref, di_ref, f_ref, cr_ref, ci_ref):
        row = lax.broadcasted_iota(jnp.int32, (rb, cb), 0)
        col = lax.broadcasted_iota(jnp.int32, (rb, cb), 1)
        mask = (row >> 6) == (col >> 4)
        cr_ref[...] = _split3_dot(jnp.where(mask, dr_ref[...], 0.0), f_ref[...])
        ci_ref[...] = -_split3_dot(jnp.where(mask, di_ref[...], 0.0), f_ref[...])

    cspec = pl.BlockSpec((rb, cb), lambda j: (j, 0))
    ospec = pl.BlockSpec((rb, h), lambda j: (j, 0))
    return pl.pallas_call(
        body, name="ssm_cgrad", grid=(nb,),
        out_shape=(jax.ShapeDtypeStruct((gp, h), F32),) * 2,
        in_specs=[cspec, cspec, _full(fold.shape)], out_specs=(ospec, ospec),
        compiler_params=_cparams(("parallel",)),
    )(d_cre, d_cim, fold)


def _ssm_lamgrad(lam_re, lam_im, log_step, abar_re, abar_im, coef_re, coef_im, gc_re, gc_im, ga_re, ga_im):
    g, p = lam_re.shape

    def body(lr_ref, li_ref, ls_ref, ar_ref, ai_ref, cr_ref, ci_ref, gcr_ref, gci_ref, gar_ref, gai_ref,
             dlr_ref, dli_ref, dls_ref):
        lam_raw = lr_ref[...]
        lr = jnp.minimum(lam_raw, LAMBDA_RE_MAX)
        li = li_ref[...]
        st = jnp.exp(ls_ref[...])
        den = lr * lr + li * li
        gcr, gci = gcr_ref[...], gci_ref[...]
        gab_r = gar_ref[...] + (lr * gcr - li * gci) / den
        gab_i = gai_ref[...] + (lr * gci + li * gcr) / den
        cr, ci = cr_ref[...], ci_ref[...]
        wr = -(cr * lr + ci * li) / den
        wi = -(ci * lr - cr * li) / den
        gl_r = wr * gcr + wi * gci
        gl_i = wr * gci - wi * gcr
        ar, ai = ar_ref[...], ai_ref[...]
        gw_r = ar * gab_r + ai * gab_i
        gw_i = ar * gab_i - ai * gab_r
        gl_r = gl_r + st * gw_r
        gl_i = gl_i + st * gw_i
        pass_through = jnp.where(lam_raw < LAMBDA_RE_MAX, 1.0, jnp.where(lam_raw == LAMBDA_RE_MAX, 0.5, 0.0))
        dlr_ref[...] = gl_r * pass_through
        dli_ref[...] = gl_i
        dls_ref[...] = st * jnp.sum(lr * gw_r + li * gw_i, axis=1, keepdims=True)

    sds = jax.ShapeDtypeStruct((g, p), F32)
    return pl.pallas_call(body, name="ssm_lamgrad", out_shape=(sds, sds, jax.ShapeDtypeStruct((g, 1), F32)))(
        lam_re, lam_im, log_step, abar_re, abar_im, coef_re, coef_im, gc_re, gc_im, ga_re, ga_im)


def _row_block(r, most=256):
    for rb in range(min(r, most), BF16_ROWS - 1, -1):
        if r % rb == 0 and rb % BF16_ROWS == 0:
            return rb
    return r


def _adamw_math(w, g, m, v):
    m = ADAM_B1 * m + (1.0 - ADAM_B1) * g
    v = ADAM_B2 * v + (1.0 - ADAM_B2) * (g * g)
    m_hat = m / (1.0 - ADAM_B1 ** ADAM_STEP)
    v_hat = v / (1.0 - ADAM_B2 ** ADAM_STEP)
    delta = -ADAM_LR * (m_hat / (jnp.sqrt(v_hat) + ADAM_EPS) + ADAM_WD * w)
    return delta, m, v


def _adamw_big(p_mine, p_sib, w, m, v, name):
    r, c = w.shape
    rb = _row_block(r)

    def body(a_ref, b_ref, w_ref, m_ref, v_ref, g_ref, d_ref, mo_ref, vo_ref):
        g = a_ref[...] + b_ref[...]
        g_ref[...] = g
        d_ref[...], mo_ref[...], vo_ref[...] = _adamw_math(w_ref[...], g, m_ref[...], v_ref[...])

    spec = pl.BlockSpec((rb, c), lambda i: (i, 0))
    sds = jax.ShapeDtypeStruct((r, c), F32)
    return pl.pallas_call(
        body, name=name, grid=(r // rb,), out_shape=(sds,) * 4, in_specs=[spec] * 5, out_specs=(spec,) * 4,
        compiler_params=_cparams(("parallel",), VMEM_MID),
    )(p_mine, p_sib, w, m, v)


def _sum_blocks(stack, name):
    n, r, c = stack.shape
    rb = _row_block(r)

    def body(s_ref, o_ref):
        acc = s_ref[0].astype(F32)
        for k in range(1, n):
            acc = acc + s_ref[k].astype(F32)
        o_ref[...] = acc

    return pl.pallas_call(
        body, name=name, grid=(r // rb,), out_shape=jax.ShapeDtypeStruct((r, c), F32),
        in_specs=[pl.BlockSpec((n, rb, c), lambda i: (0, i, 0))], out_specs=pl.BlockSpec((rb, c), lambda i: (i, 0)),
        compiler_params=_cparams(("parallel",), VMEM_MID),
    )(stack)


def _adamw_ada(c_all, dmod_cols, w, m, v):
    d, n = w.shape
    bn = 512

    def body(c_ref, dm_ref, w_ref, m_ref, v_ref, g_ref, d_ref, mo_ref, vo_ref):
        cc = c_ref[...]
        g = _dot_tn(cc * _sigmoid(cc), dm_ref[...])
        g_ref[...] = g
        d_ref[...], mo_ref[...], vo_ref[...] = _adamw_math(w_ref[...], g, m_ref[...], v_ref[...])

    spec = pl.BlockSpec((d, bn), lambda j: (0, j))
    sds = jax.ShapeDtypeStruct((d, n), F32)
    return pl.pallas_call(
        body, name="adamw_ada", grid=(n // bn,), out_shape=(sds,) * 4,
        in_specs=[_full((N_DEV, d)), pl.BlockSpec((N_DEV, bn), lambda j: (0, j)), spec, spec, spec],
        out_specs=(spec,) * 4, compiler_params=_cparams(("parallel",)),
    )(c_all, dmod_cols, w, m, v)


def _adamw_small(items):
    n = len(items)

    def body(*refs):
        ins, outs = refs[:4 * n], refs[4 * n:]
        for k in range(n):
            w_ref, g_ref, m_ref, v_ref = ins[4 * k:4 * k + 4]
            outs[3 * k][...], outs[3 * k + 1][...], outs[3 * k + 2][...] = _adamw_math(
                w_ref[...], g_ref[...], m_ref[...], v_ref[...])

    flat = [a for it in items for a in it]
    out_shape = tuple(jax.ShapeDtypeStruct(it[0].shape, F32) for it in items for _ in range(3))
    res = pl.pallas_call(body, name="adamw_small", out_shape=out_shape,
                         compiler_params=_cparams(vmem=VMEM_BIG))(*flat)
    return [tuple(res[3 * k:3 * k + 3]) for k in range(n)]


def _group_mean_matrix(n, group):
    idx = np.arange(n) // group
    return (idx[:, None] == idx[None, :]).astype(np.float32) / group


def _fold_matrix(n, period):
    return (np.arange(n)[:, None] % period == np.arange(period)[None, :]).astype(np.float32)


def _rows8(*rows):
    c = rows[0].shape[-1]
    pad = jnp.zeros((SUBLANES - len(rows), c), F32)
    return jnp.concatenate([r.reshape(1, c) for r in rows] + [pad], axis=0)


def _to_rows(a, width):
    flat = a.reshape(-1)
    n = -(-flat.shape[0] // width)
    flat = jnp.pad(flat, (0, n * width - flat.shape[0]))
    return flat.reshape(n, width)


def kernel(x, c, w_ada, b_ada, g_pre_mix, g_post_mix, w_in, ssm_lam_re, ssm_lam_im, ssm_log_step, ssm_b_re, ssm_b_im, ssm_c_re, ssm_c_im, ssm_d, glu_w, glu_b, g_out_ssm, conv_w, g_out_conv, w_out, g_pre_ffn, g_post_ffn, w_up, ffn_conv_w, w_down, loss_target, m_w_ada, m_b_ada, m_g_pre_mix, m_g_post_mix, m_w_in, m_ssm_lam_re, m_ssm_lam_im, m_ssm_log_step, m_ssm_b_re, m_ssm_b_im, m_ssm_c_re, m_ssm_c_im, m_ssm_d, m_glu_w, m_glu_b, m_g_out_ssm, m_conv_w, m_g_out_conv, m_w_out, m_g_pre_ffn, m_g_post_ffn, m_w_up, m_ffn_conv_w, m_w_down, v_w_ada, v_b_ada, v_g_pre_mix, v_g_post_mix, v_w_in, v_ssm_lam_re, v_ssm_lam_im, v_ssm_log_step, v_ssm_b_re, v_ssm_b_im, v_ssm_c_re, v_ssm_c_im, v_ssm_d, v_glu_w, v_glu_b, v_g_out_ssm, v_conv_w, v_g_out_conv, v_w_out, v_g_pre_ffn, v_g_post_ffn, v_w_up, v_ffn_conv_w, v_w_down):
    xs = x[0]
    tgt = loss_target[0]
    t, d = xs.shape
    xi, yi, ci = lax.axis_index("x"), lax.axis_index("y"), lax.axis_index("c")
    chip = 2 * xi + yi
    dev = 2 * chip + ci

    n_groups, n_state = ssm_lam_re.shape[1:]
    n_gch = ssm_b_re.shape[3]
    d_ssm = n_groups * n_gch
    gp = n_groups * n_state
    n_ada = w_ada.shape[2]
    d_ff = w_down.shape[1] * N_CHIPS
    n_upc = w_up.shape[2]

    w_names = ("w_in", "glu_w", "w_out", "w_up", "w_down")
    c_gath, _ = _allgather8(jnp.broadcast_to(c, (SUBLANES, d)), SUBLANES, "gather_c")
    c_all = c_gath.reshape(N_DEV, SUBLANES, d)[:, 0, :]

    def pad8(a):
        return jnp.concatenate([a, jnp.zeros((SUBLANES - a.shape[0], a.shape[1]), a.dtype)], axis=0)

    def start(name, arrs, after):
        return _chips_start(name, True, arrs, [_landing(a, chip) for a in arrs], after)

    w_names = ("w_in", "mod", "conv_w", "ffn_conv_w", "glu_w", "w_out", "w_up", "w_down")
    first = start("weights_start_in", [w_in[0].astype(BF16)], c_gath)
    b_sh = lax.dynamic_slice(b_ada, (0, chip * n_ada), (1, n_ada))
    mod_sh = _mod_shard(c_all + first[4][0:1, 0:1], w_ada[0], b_sh)
    second = start("weights_start_mod", [mod_sh, pad8(conv_w[0]), pad8(ffn_conv_w[0])], None)
    third = start("weights_start_rest", [w[0].astype(BF16) for w in (glu_w, w_out, w_up, w_down)], second[4])
    w_send, w_recv, w_src, w_land = [list(first[k]) + list(second[k]) + list(third[k]) for k in range(4)]
    w_token = third[4]

    def weights(names, after):
        ks = [w_names.index(nm) for nm in names]
        return _chips_wait("weights_wait_" + names[-1], True, [w_send[k] for k in ks], [w_recv[k] for k in ks],
                           [w_src[k] for k in ks], [w_land[k] for k in ks], after)

    lam_re, lam_im = ssm_lam_re[0], ssm_lam_im[0]
    log_step = ssm_log_step[0].reshape(n_groups, 1) + w_token[0:1, 0:1]
    abar_re, abar_im, coef_re, coef_im = _ssm_prep(lam_re, lam_im, log_step)
    a_rows = _rows8(abar_re.reshape(1, gp), abar_im.reshape(1, gp))
    coef_rows = _rows8(coef_re.reshape(1, gp), coef_im.reshape(1, gp))
    bt_re = jnp.tile(ssm_b_re[0].transpose(0, 2, 1).reshape(d_ssm, n_state), (1, n_groups))
    bt_im = jnp.tile(ssm_b_im[0].transpose(0, 2, 1).reshape(d_ssm, n_state), (1, n_groups))
    ct_re = jnp.tile(ssm_c_re[0].transpose(0, 2, 1).reshape(gp, n_gch), (1, n_groups))
    ct_im = jnp.tile(ssm_c_im[0].transpose(0, 2, 1).reshape(gp, n_gch), (1, n_groups))
    bblk_re, bblk_im, cblk_re, cblk_im = _ssm_blocks(bt_re, bt_im, ct_re, ct_im, coef_rows)

    h16 = jnp.asarray(_group_mean_matrix(d_ssm, n_gch), BF16)
    h64 = jnp.asarray(_group_mean_matrix(d_ssm, CONV_HEAD_DIM), BF16)

    g_mod, g_cw, g_fw, w_in_st = weights(("mod", "conv_w", "ffn_conv_w", "w_in"), bblk_re)
    mod_all = g_mod.transpose(1, 0, 2).reshape(N_DEV, N_CHIPS * n_ada)
    mod = lax.dynamic_slice(mod_all, (dev, 0), (1, N_CHIPS * n_ada))
    sh1, sc1, gt1, sh2, sc2, gt2 = [mod[:, k * d:(k + 1) * d] for k in range(6)]
    convw_full = pad8(g_cw[:, :3, :].transpose(1, 0, 2).reshape(3, d_ssm))
    fw_full = pad8(g_fw[:, :3, :].transpose(1, 0, 2).reshape(3, N_CHIPS * n_upc))

    v512 = _rows8(ssm_d, glu_b, g_out_ssm, g_out_conv)
    vec1 =_rows8(g_pre_mix, 1.0 + sc1, sh1)
    vd1 = _rows8(g_post_mix, gt1)
    vec2 = _rows8(g_pre_ffn, 1.0 + sc2, sh2)
    vd2 = _rows8(g_post_ffn, gt2)

    proj, bu_re, bu_im, h1b = _mix_in(xs, vec1, w_in_st, bblk_re, bblk_im)
    s_re, s_im = _scan_fwd(a_rows, bu_re, bu_im)
    g_glu, g_wout = weights(("glu_w", "w_out"), s_re)
    glu_full = g_glu.reshape(d_ssm, d_ssm)
    w_out_full = g_wout.reshape(2 * d_ssm, d)
    y1, o_mix, x2 = _mix_out(xs, proj, s_re, s_im, cblk_re, cblk_im, v512, convw_full, glu_full, h16, h64,
                             w_out_full, vd1)
    (w_up_st,) = weights(("w_up",), x2)
    up, h2b = _ffn_up(x2, vec2, w_up_st)
    (g_wdown,) = weights(("w_down",), up)
    w_down_full = g_wdown.reshape(d_ff, d)
    actb, ddnb, dout, dhid, vp_dn, loss_blk = _ffn_down(up, fw_full, w_down_full, w_down_full.T, x2, tgt, vd2)

    g_names = ("w_down", "w_up", "w_out", "glu_w", "w_in")
    gw_down = _matmul_tn(actb, ddnb, d_ff, d, BF16, "dw_down", bt=1024).reshape(N_CHIPS, d_ff // N_CHIPS, d)
    dx2, dupb, vp_up, df_rows = _ffn_up_bwd(dhid, up, fw_full, x2, dout, vec2, w_up_st)
    gw_up = _matmul_tn(h2b, dupb, d, n_upc, BF16, "dw_up", bt=2048)
    ga_send, ga_recv, ga_src, ga_land, ga_token = _chips_start(
        "grads_start_ffn", False, [gw_down, gw_up],
        [_landing(lax.dynamic_index_in_dim(g, chip, 0, False), chip) for g in (gw_down, gw_up)])
    (dob, ycatb, zb, dqb, dy1, g_re, g_im, dcc, dbg, vp_mo, vp5) = _mix_out_bwd(
        dx2, o_mix, y1, proj, cblk_re, cblk_im, v512, convw_full, glu_full, h16, h64, w_out_full,
        vd1 + ga_token[0:1, 0:1])
    gt_re, gt_im, ga_re8, ga_im8 = _scan_bwd(a_rows, g_re, g_im, s_re, s_im)
    grad_x, dprojb, vp_mi = _mix_in_bwd(gt_re, gt_im, bblk_re, bblk_im, dy1, dcc, dbg, proj, xs, dx2, vec1, v512,
                                        convw_full, w_in_st)
    ssm_u, ssm_s = d_ssm // SSM_SPLIT, gp // SSM_SPLIT
    d_bre = _matmul_tn(proj, gt_re, ssm_u, ssm_s, F32, "d_bre", diag=True, bt=2048)
    d_bim = _matmul_tn(proj, gt_im, ssm_u, ssm_s, F32, "d_bim", diag=True, bt=2048)
    d_cre = _matmul_tn(s_re, dy1, ssm_s, ssm_u, F32, "d_cre", diag=True, bt=2048)
    d_cim = _matmul_tn(s_im, dy1, ssm_s, ssm_u, F32, "d_cim", diag=True, bt=2048)
    d_bre, d_bim = d_bre.reshape(d_ssm, ssm_s), d_bim.reshape(d_ssm, ssm_s)
    d_cre, d_cim = d_cre.reshape(gp, ssm_u), d_cim.reshape(gp, ssm_u)

    fold_b = jnp.asarray(_fold_matrix(ssm_s, n_state), BF16)
    fold_c = jnp.asarray(_fold_matrix(ssm_u, n_gch), BF16)
    db_re_f, db_im_f, gc_rows = _ssm_bgrad(d_bre, d_bim, bt_re, bt_im, coef_rows, fold_b)
    dc_re_f, dc_im_f = _ssm_cgrad(d_cre, d_cim, fold_c)
    ga_sum = _ga_rowsum(ga_re8, ga_im8)
    g_lam_re, g_lam_im, g_log_step = _ssm_lamgrad(
        lam_re, lam_im, log_step, abar_re, abar_im, coef_re, coef_im,
        gc_rows[0].reshape(n_groups, n_state), gc_rows[1].reshape(n_groups, n_state),
        ga_sum[0].reshape(n_groups, n_state), ga_sum[1].reshape(n_groups, n_state))
    g_b_re = db_re_f.reshape(n_groups, n_gch, n_state).transpose(0, 2, 1)
    g_b_im = db_im_f.reshape(n_groups, n_gch, n_state).transpose(0, 2, 1)
    g_c_re = dc_re_f.reshape(n_groups, n_state, n_gch).transpose(0, 2, 1)
    g_c_im = dc_im_f.reshape(n_groups, n_state, n_gch).transpose(0, 2, 1)

    dmod = jnp.concatenate([vp_mi[0:1], vp_mi[1:2], vp_mo[0:1], vp_up[0:1], vp_up[1:2], vp_dn[0:1]], axis=1)
    small = [
        ("g_pre_mix", vp_mi[2:3]), ("g_post_mix", vp_mo[1:2]), ("g_pre_ffn", vp_up[2:3]), ("g_post_ffn", vp_dn[1:2]),
        ("ssm_lam_re", g_lam_re), ("ssm_lam_im", g_lam_im), ("ssm_log_step", g_log_step),
        ("ssm_b_re", g_b_re), ("ssm_b_im", g_b_im), ("ssm_c_re", g_c_re), ("ssm_c_im", g_c_im),
        ("ssm_d", vp5[3:4]), ("glu_b", vp5[2:3]), ("g_out_ssm", vp5[0:1]), ("g_out_conv", vp5[1:2]),
        ("conv_w", vp5[4:7]), ("ffn_conv_w", df_rows[0:3]), ("loss", loss_blk[0:1, 0:1]),
    ]
    packed, offsets, row = [], {}, 0
    for name, a in small:
        r = _to_rows(a, d)
        offsets[name] = (row, a.shape)
        packed.append(r)
        row += r.shape[0]
    n_small = -(-row // SUBLANES) * SUBLANES
    packed.append(jnp.zeros((n_small - row, d), F32))
    packed.append(pad8(dmod.reshape(6, d)))
    pack = jnp.concatenate(packed, axis=0)
    gath, sums = _allgather8(pack, n_small + SUBLANES, "reduce_small")
    dmod_all = gath.reshape(N_DEV, n_small + SUBLANES, d)[:, n_small:n_small + 6, :].reshape(N_DEV, 6 * d)
    g_b_ada = sums[n_small:n_small + 6].reshape(1, 6 * d)

    def unpack(name):
        r0, shape = offsets[name]
        size = math.prod(shape)
        nrow = -(-size // d)
        return sums[r0:r0 + nrow].reshape(-1)[:size].reshape(shape)

    gw_out = _matmul_tn(ycatb, dob, 2 * d_ssm, d, BF16, "dw_out", bt=2048, after=sums)
    gw_out = gw_out.reshape(N_CHIPS, 2 * d_ssm // N_CHIPS, d)
    gw_glu = _matmul_tn(zb, dqb, d_ssm, d_ssm, BF16, "dw_glu", bt=2048).reshape(N_CHIPS, d_ssm // N_CHIPS, d_ssm)
    gw_in = _matmul_tn(h1b, dprojb, d, w_in.shape[2], BF16, "dw_in", bt=2048)
    gb_send, gb_recv, gb_src, gb_land, gb_token = _chips_start(
        "grads_start_mix", False, [gw_out, gw_glu, gw_in],
        [_landing(lax.dynamic_index_in_dim(g, chip, 0, False), chip) for g in (gw_out, gw_glu, gw_in)])

    dmod_cols = lax.dynamic_slice(dmod_all, (0, chip * n_ada), (N_DEV, n_ada)) + gb_token[0:1, 0:1]
    ada = _adamw_ada(c_all, dmod_cols, w_ada[0], m_w_ada[0], v_w_ada[0])

    def finish(names, landed):
        partial = [_sum_blocks(s, "sum_" + nm) for s, nm in zip(landed, names)]
        theirs = _swap_sibling(partial, "swap_" + names[0])
        done = {}
        for nm, pm, ps in zip(names, partial, theirs):
            w_, m_, v_ = big_params[nm]
            done[nm] = _adamw_big(pm, ps, w_[0], m_[0], v_[0], "adamw_" + nm)
        return done

    big_params = {"w_down": (w_down, m_w_down, v_w_down), "w_up": (w_up, m_w_up, v_w_up),
                  "w_out": (w_out, m_w_out, v_w_out), "glu_w": (glu_w, m_glu_w, v_glu_w),
                  "w_in": (w_in, m_w_in, v_w_in)}
    big = finish(("w_down", "w_up"), _chips_wait("grads_wait_ffn", False, ga_send, ga_recv, ga_src, ga_land, ada[0]))
    big.update(finish(("w_out", "glu_w", "w_in"),
                      _chips_wait("grads_wait_mix", False, gb_send, gb_recv, gb_src, gb_land, big["w_up"][0])))

    g_small = {name: unpack(name) for name, _ in small}
    g_small["b_ada"] = g_b_ada
    g_small["conv_w"] = lax.dynamic_slice(g_small["conv_w"], (0, chip * conv_w.shape[2]), (3, conv_w.shape[2]))
    g_small["ffn_conv_w"] = lax.dynamic_slice(g_small["ffn_conv_w"], (0, chip * n_upc), (3, n_upc))
    g_small["ssm_log_step"] = g_small["ssm_log_step"].reshape(1, n_groups)
    small_params = {
        "b_ada": (b_ada, m_b_ada, v_b_ada), "g_pre_mix": (g_pre_mix, m_g_pre_mix, v_g_pre_mix),
        "g_post_mix": (g_post_mix, m_g_post_mix, v_g_post_mix), "ssm_lam_re": (ssm_lam_re, m_ssm_lam_re, v_ssm_lam_re),
        "ssm_lam_im": (ssm_lam_im, m_ssm_lam_im, v_ssm_lam_im),
        "ssm_log_step": (ssm_log_step, m_ssm_log_step, v_ssm_log_step),
        "ssm_b_re": (ssm_b_re, m_ssm_b_re, v_ssm_b_re), "ssm_b_im": (ssm_b_im, m_ssm_b_im, v_ssm_b_im),
        "ssm_c_re": (ssm_c_re, m_ssm_c_re, v_ssm_c_re), "ssm_c_im": (ssm_c_im, m_ssm_c_im, v_ssm_c_im),
        "ssm_d": (ssm_d, m_ssm_d, v_ssm_d), "glu_b": (glu_b, m_glu_b, v_glu_b),
        "g_out_ssm": (g_out_ssm, m_g_out_ssm, v_g_out_ssm), "conv_w": (conv_w, m_conv_w, v_conv_w),
        "g_out_conv": (g_out_conv, m_g_out_conv, v_g_out_conv), "g_pre_ffn": (g_pre_ffn, m_g_pre_ffn, v_g_pre_ffn),
        "g_post_ffn": (g_post_ffn, m_g_post_ffn, v_g_post_ffn),
        "ffn_conv_w": (ffn_conv_w, m_ffn_conv_w, v_ffn_conv_w),
    }

    def natural(a):
        return a[0] if a.ndim > 2 else a

    names = list(small_params)
    items = []
    for nm in names:
        w_, m_, v_ = small_params[nm]
        items.append((natural(w_), g_small[nm].reshape(natural(w_).shape), natural(m_), natural(v_)))
    upd = _adamw_small(items)
    small_out = {}
    for nm, (dl, mo, vo) in zip(names, upd):
        shp = small_params[nm][0].shape
        small_out[nm] = (g_small[nm].reshape(shp), dl.reshape(shp), mo.reshape(shp), vo.reshape(shp))

    loss = g_small["loss"][0, 0]

    order = ["w_ada", "b_ada", "g_pre_mix", "g_post_mix", "w_in", "ssm_lam_re", "ssm_lam_im", "ssm_log_step",
             "ssm_b_re", "ssm_b_im", "ssm_c_re", "ssm_c_im", "ssm_d", "glu_w", "glu_b", "g_out_ssm", "conv_w",
             "g_out_conv", "w_out", "g_pre_ffn", "g_post_ffn", "w_up", "ffn_conv_w", "w_down"]
    results = {"w_ada": tuple(a[None] for a in ada)}
    for nm in big:
        results[nm] = tuple(a[None] for a in big[nm])
    results.update(small_out)
    outs = [loss, grad_x[None]]
    for k in range(4):
        outs += [results[nm][k] for nm in order]
    return tuple(outs)


def _ga_rowsum(ga_re8, ga_im8):
    n = ga_re8.shape[1]

    def body(r_ref, i_ref, o_ref):
        o_ref[...] = jnp.zeros(o_ref.shape, F32)
        o_ref[0:1, :] = _colsum(r_ref[...])
        o_ref[1:2, :] = _colsum(i_ref[...])

    return pl.pallas_call(body, name="ga_rowsum", out_shape=jax.ShapeDtypeStruct((SUBLANES, n), F32))(ga_re8, ga_im8)
```

```python
import functools
import math

import jax
import jax.numpy as jnp
import numpy as np
from jax import lax
from jax.experimental import pallas as pl
from jax.experimental.pallas import tpu as pltpu

F32 = jnp.float32
BF16 = jnp.bfloat16
MESH = pl.DeviceIdType.MESH

EPS = 1e-6
LAMBDA_RE_MAX = -1e-4
ADAM_LR = 0.001
ADAM_B1 = 0.9
ADAM_B2 = 0.999
ADAM_EPS = 1e-08
ADAM_WD = 0.01
ADAM_STEP = 10

SUBLANES = 8
BF16_ROWS = 16
N_CHIPS = 4
N_DEV = 8
CONV_HEAD_DIM = 64
VMEM_BIG = 56 * 1024 * 1024
VMEM_MID = 40 * 1024 * 1024

TB_MIX = 256
TB_FFN = 256
TB_SCAN = 1024
W_SCAN = 256
SSM_SPLIT = 4
CW_FFN = 256
SCAN_UNROLL = 4
TB_TN = 512


def _cparams(sem=None, vmem=None):
    kw = {}
    if sem is not None:
        kw["dimension_semantics"] = sem
    if vmem is not None:
        kw["vmem_limit_bytes"] = vmem
    return pltpu.CompilerParams(**kw)


def _blk(t, pref):
    return pref if t % pref == 0 else t


def _dot(a, b):
    return jnp.dot(a.astype(BF16), b.astype(BF16), preferred_element_type=F32)


def _dot_nt(a, b):
    return lax.dot_general(a.astype(BF16), b.astype(BF16), (((1,), (1,)), ((), ())),
                           preferred_element_type=F32)


def _dot_tn(a, b):
    return lax.dot_general(a.astype(BF16), b.astype(BF16), (((0,), (0,)), ((), ())),
                           preferred_element_type=F32)


def _sigmoid(x):
    return 0.5 * jnp.tanh(0.5 * x) + 0.5


_GELU_K = math.sqrt(2.0 / math.pi)
_GELU_C = 0.044715


def _gelu(x):
    th = jnp.tanh(_GELU_K * (x + _GELU_C * x * x * x))
    return 0.5 * x * (1.0 + th)


def _gelu_grad(x):
    x2 = x * x
    th = jnp.tanh(_GELU_K * (x + _GELU_C * x2 * x))
    return 0.5 * (1.0 + th) + 0.5 * x * (1.0 - th * th) * _GELU_K * (1.0 + 3.0 * _GELU_C * x2)


def _rowmean(x):
    return jnp.mean(x, axis=-1, keepdims=True)


def _colsum(x):
    return jnp.sum(x, axis=0, keepdims=True)


def _split_dot(x, m):
    hi = x.astype(BF16)
    lo = (x - hi.astype(F32)).astype(BF16)
    return (jnp.dot(hi, m, preferred_element_type=F32) + jnp.dot(lo, m, preferred_element_type=F32))


def _split3_dot(x, m):
    hi = x.astype(BF16)
    r1 = x - hi.astype(F32)
    mid = r1.astype(BF16)
    lo = (r1 - mid.astype(F32)).astype(BF16)
    return (jnp.dot(hi, m, preferred_element_type=F32) + jnp.dot(mid, m, preferred_element_type=F32)
            + jnp.dot(lo, m, preferred_element_type=F32))


def _shift_down(x, halo, k):
    r = pltpu.roll(x, k, 0)
    row = lax.broadcasted_iota(jnp.int32, x.shape, 0)
    for j in range(k):
        r = jnp.where(row == j, halo[SUBLANES - k + j:SUBLANES - k + j + 1, :], r)
    return r


def _shift_up(x, halo, k):
    n = x.shape[0]
    r = pltpu.roll(x, n - k, 0)
    row = lax.broadcasted_iota(jnp.int32, x.shape, 0)
    for j in range(k):
        r = jnp.where(row == n - k + j, halo[j:j + 1, :], r)
    return r


def _acc_rows(ref, first, rows):
    @pl.when(first)
    def _():
        ref[...] = jnp.zeros(ref.shape, ref.dtype)
    for j, r in enumerate(rows):
        ref[j:j + 1, :] += r


def _rows(tb, c, col=0):
    return pl.BlockSpec((tb, c), lambda i, col=col: (i, col))


def _full(shape):
    nd = len(shape)
    return pl.BlockSpec(shape, lambda i, nd=nd: (0,) * nd)


def _resident(shape):
    nd = len(shape)
    return pl.BlockSpec(shape, lambda i, nd=nd: (0,) * nd, pipeline_mode=pl.Buffered(1))


def _halo_prev(tb, c, col=0):
    per = tb // SUBLANES
    return pl.BlockSpec((SUBLANES, c), lambda i, col=col: (jnp.maximum(i * per - 1, 0), col))


def _halo_next(tb, c, t, col=0, rows=SUBLANES):
    per = tb // rows
    last = t // rows - 1
    return pl.BlockSpec((rows, c), lambda i, col=col: (jnp.minimum((i + 1) * per, last), col))


def _mesh_pos():
    return lax.axis_index("x"), lax.axis_index("y"), lax.axis_index("c")


def _allgather8(x_pad, n_sum, name):
    m_per, n = x_pad.shape

    def body(x_ref, out_ref, sum_ref, send_sems, recv_sems, local_sem):
        x, y, c = _mesh_pos()
        me, sibling = (x, y, c), (x, y, 1 - c)
        chips = [(1 - x, y), (x, 1 - y), (1 - x, 1 - y)]

        def rows(px, py, pc):
            return out_ref.at[pl.ds((4 * px + 2 * py + pc) * m_per, m_per), :]

        def copy(k, block, to, src=None):
            return pltpu.make_async_remote_copy(
                src_ref=rows(*block) if src is None else src, dst_ref=rows(*block),
                send_sem=send_sems.at[k], recv_sem=recv_sems.at[k], device_id=to, device_id_type=MESH)

        mine = pltpu.make_async_copy(x_ref, rows(*me), local_sem)
        mine.start()
        first = [copy(0, me, sibling, src=x_ref)]
        first += [copy(1 + j, me, (*chip, c), src=x_ref) for j, chip in enumerate(chips)]
        for cp in first:
            cp.start()
        passed = [copy(4 + j, (*chip, c), sibling) for j, chip in enumerate(chips)]
        for j, chip in enumerate(chips):
            copy(1 + j, (*chip, c), me).wait_recv()
            passed[j].start()
        copy(0, sibling, me).wait_recv()
        for j, chip in enumerate(chips):
            copy(4 + j, (*chip, 1 - c), me).wait_recv()
        for cp in first + passed:
            cp.wait_send()
        mine.wait()
        acc = out_ref[0:n_sum, :]
        for k in range(1, N_DEV):
            acc = acc + out_ref[k * m_per:k * m_per + n_sum, :]
        sum_ref[...] = acc

    return pl.pallas_call(
        body, name=name,
        out_shape=(jax.ShapeDtypeStruct((N_DEV * m_per, n), F32), jax.ShapeDtypeStruct((n_sum, n), F32)),
        in_specs=[pl.BlockSpec(memory_space=pltpu.VMEM)],
        out_specs=(pl.BlockSpec(memory_space=pltpu.VMEM), pl.BlockSpec(memory_space=pltpu.VMEM)),
        scratch_shapes=[pltpu.SemaphoreType.DMA((7,)), pltpu.SemaphoreType.DMA((7,)), pltpu.SemaphoreType.DMA],
        compiler_params=_cparams(vmem=VMEM_MID),
    )(x_pad)


_HBM = pl.BlockSpec(memory_space=pltpu.HBM)
_SEM = pl.BlockSpec(memory_space=pltpu.SEMAPHORE)
_EFFECT = pltpu.SideEffectType.DATAFLOW_SIDE_EFFECTING


def _chip_copy(gather, src_ref, land_ref, send, recv, j, arrival):
    x, y, c = _mesh_pos()
    peer = [(1 - x, y), (x, 1 - y), (1 - x, 1 - y)][j]
    peer_chip = 2 * peer[0] + peer[1]
    my_chip = 2 * x + y
    return pltpu.make_async_remote_copy(
        src_ref=src_ref if gather else src_ref.at[peer_chip],
        dst_ref=land_ref.at[peer_chip if arrival else my_chip],
        send_sem=send.at[j], recv_sem=recv.at[j], device_id=(*peer, c), device_id_type=MESH)


def _chips_start(name, gather, srcs, lands, after=None):
    n = len(srcs)
    extra = [] if after is None else [after]

    def body(*refs):
        src_refs, land_refs = refs[:n], refs[n:2 * n]
        outs = refs[2 * n + len(extra):]
        sends, recvs, token = outs[:n], outs[n:2 * n], outs[-1]
        for k in range(n):
            for j in range(3):
                _chip_copy(gather, src_refs[k], land_refs[k], sends[k], recvs[k], j, False).start()
        token[...] = jnp.zeros(token.shape, F32)

    sem = pltpu.SemaphoreType.DMA((3,))
    thru = tuple(pltpu.HBM(a.shape, a.dtype) for a in list(srcs) + list(lands))
    res = pl.pallas_call(
        body, name=name,
        out_shape=(sem,) * (2 * n) + thru + (jax.ShapeDtypeStruct((SUBLANES, 128), F32),),
        in_specs=[_HBM] * (2 * n) + [pl.BlockSpec(memory_space=pl.ANY)] * len(extra),
        out_specs=(_SEM,) * (2 * n) + (_HBM,) * (2 * n) + (pl.BlockSpec(memory_space=pltpu.VMEM),),
        input_output_aliases={k: 2 * n + k for k in range(2 * n)},
        compiler_params=pltpu.CompilerParams(has_side_effects=_EFFECT),
    )(*[pltpu.with_memory_space_constraint(a, pltpu.HBM) for a in list(srcs) + list(lands)], *extra)
    return res[:n], res[n:2 * n], res[2 * n:3 * n], res[3 * n:4 * n], res[-1]


def _chips_wait(name, gather, sends, recvs, srcs, lands, after):
    n = len(srcs)

    def body(*refs):
        src_refs, land_refs = refs[:n], refs[n:2 * n]
        sends_, recvs_ = refs[2 * n:3 * n], refs[3 * n:4 * n]
        for k in range(n):
            for j in range(3):
                cp = _chip_copy(gather, src_refs[k], land_refs[k], sends_[k], recvs_[k], j, True)
                cp.wait_send()
                cp.wait_recv()

    thru = tuple(pltpu.HBM(a.shape, a.dtype) for a in list(srcs) + list(lands))
    res = pl.pallas_call(
        body, name=name, out_shape=thru,
        in_specs=[_HBM] * (2 * n) + [_SEM] * (2 * n) + [pl.BlockSpec(memory_space=pl.ANY)],
        out_specs=(_HBM,) * (2 * n),
        input_output_aliases={k: k for k in range(2 * n)},
        compiler_params=pltpu.CompilerParams(has_side_effects=_EFFECT),
    )(*srcs, *lands, *sends, *recvs, after)
    return res[n:]


def _landing(own, chip):
    zone = lax.empty((N_CHIPS,) + own.shape, own.dtype)
    return lax.dynamic_update_slice(zone, own[None], (chip,) + (0,) * own.ndim)


def _swap_sibling(arrs, name):
    n_arr = len(arrs)

    def body(*refs):
        ins, outs = refs[:n_arr], refs[n_arr:2 * n_arr]
        send_sems, recv_sems = refs[2 * n_arr:]
        x, y, c = _mesh_pos()
        copies = [pltpu.make_async_remote_copy(
            src_ref=ins[n], dst_ref=outs[n], send_sem=send_sems.at[n], recv_sem=recv_sems.at[n],
            device_id=(x, y, 1 - c), device_id_type=MESH) for n in range(n_arr)]
        for cp in copies:
            cp.start()
        for cp in copies:
            cp.wait()

    any_spec = pl.BlockSpec(memory_space=pl.ANY)
    return pl.pallas_call(
        body, name=name,
        out_shape=tuple(jax.ShapeDtypeStruct(a.shape, a.dtype) for a in arrs),
        in_specs=[any_spec] * n_arr, out_specs=tuple([any_spec] * n_arr),
        scratch_shapes=[pltpu.SemaphoreType.DMA((n_arr,)), pltpu.SemaphoreType.DMA((n_arr,))],
    )(*arrs)


def _mod_shard(c_all, w_ada_sh, b_sh):
    d, n = w_ada_sh.shape
    bn = 512

    def body(c_ref, w_ref, b_ref, o_ref):
        cc = c_ref[...]
        ca = cc * _sigmoid(cc)
        o_ref[...] = _dot(ca, w_ref[...]) + b_ref[...]

    return pl.pallas_call(
        body, name="mod_shard", grid=(n // bn,),
        out_shape=jax.ShapeDtypeStruct((N_DEV, n), F32),
        in_specs=[_full((N_DEV, d)), pl.BlockSpec((d, bn), lambda j: (0, j)), pl.BlockSpec((1, bn), lambda j: (0, j))],
        out_specs=pl.BlockSpec((N_DEV, bn), lambda j: (0, j)),
        compiler_params=_cparams(("parallel",)),
    )(c_all, w_ada_sh, b_sh)


def _ssm_prep(lam_re, lam_im, log_step):
    g, p = lam_re.shape

    def body(lr_ref, li_ref, ls_ref, ar_ref, ai_ref, cr_ref, ci_ref):
        lr = jnp.minimum(lr_ref[...], LAMBDA_RE_MAX)
        li = li_ref[...]
        st = jnp.exp(ls_ref[...])
        mag = jnp.exp(lr * st)
        ar = mag * jnp.cos(li * st)
        ai = mag * jnp.sin(li * st)
        den = lr * lr + li * li
        nr = ar - 1.0
        ar_ref[...] = ar
        ai_ref[...] = ai
        cr_ref[...] = (nr * lr + ai * li) / den
        ci_ref[...] = (ai * lr - nr * li) / den

    sds = jax.ShapeDtypeStruct((g, p), F32)
    return pl.pallas_call(body, name="ssm_prep", out_shape=(sds,) * 4)(lam_re, lam_im, log_step)


def _ssm_blocks(bt_re, bt_im, ct_re, ct_im, coef_rows):
    gh, gp = bt_re.shape
    nb = 4
    cb, rb = gp // nb, gp // nb

    def body(btr, bti, ctr, cti, cf, bre_o, bim_o, cre_o, cim_o):
        j = pl.program_id(0)
        row = lax.broadcasted_iota(jnp.int32, (gh, cb), 0)
        col = lax.broadcasted_iota(jnp.int32, (gh, cb), 1) + j * cb
        mask = (row >> 4) == (col >> 6)
        cr, ci = cf[0:1, :], cf[1:2, :]
        br, bi = btr[...], bti[...]
        bre_o[...] = jnp.where(mask, br * cr - bi * ci, 0.0).astype(BF16)
        bim_o[...] = jnp.where(mask, br * ci + bi * cr, 0.0).astype(BF16)
        row2 = lax.broadcasted_iota(jnp.int32, (rb, gh), 0) + j * rb
        col2 = lax.broadcasted_iota(jnp.int32, (rb, gh), 1)
        mask2 = (row2 >> 6) == (col2 >> 4)
        cre_o[...] = jnp.where(mask2, ctr[...], 0.0).astype(BF16)
        cim_o[...] = jnp.where(mask2, cti[...], 0.0).astype(BF16)

    bspec = pl.BlockSpec((gh, cb), lambda j: (0, j))
    cspec = pl.BlockSpec((rb, gh), lambda j: (j, 0))
    return pl.pallas_call(
        body, name="ssm_blocks", grid=(nb,),
        out_shape=(jax.ShapeDtypeStruct((gh, gp), BF16),) * 2 + (jax.ShapeDtypeStruct((gp, gh), BF16),) * 2,
        in_specs=[bspec, bspec, cspec, cspec, pl.BlockSpec((SUBLANES, cb), lambda j: (0, j))],
        out_specs=(bspec, bspec, cspec, cspec),
        compiler_params=_cparams(("parallel",)),
    )(bt_re, bt_im, ct_re, ct_im, coef_rows)


def _scan_consts(a_ref, reverse):
    w = a_ref.shape[1]
    ar1 = a_ref[0:1, :]
    ai1 = a_ref[1:2, :]
    if reverse:
        ai1 = -ai1
    pr, pi = [ar1], [ai1]
    for _ in range(1, SUBLANES):
        nr = pr[-1] * ar1 - pi[-1] * ai1
        ni = pr[-1] * ai1 + pi[-1] * ar1
        pr.append(nr)
        pi.append(ni)
    row = lax.broadcasted_iota(jnp.int32, (SUBLANES, w), 0)
    dist = (SUBLANES - 1 - row) if reverse else row

    def pick(vals):
        out = jnp.broadcast_to(vals[SUBLANES - 1], (SUBLANES, w))
        for r in range(SUBLANES - 1):
            out = jnp.where(dist == r, vals[r], out)
        return out

    p_r, p_i = pick(pr), pick(pi)
    steps = []
    for k in (1, 2, 4):
        steps.append((k, jnp.where(dist >= k, pr[k - 1], 0.0), jnp.where(dist >= k, pi[k - 1], 0.0)))
    a8 = (jnp.broadcast_to(pr[SUBLANES - 1], (SUBLANES, w)), jnp.broadcast_to(pi[SUBLANES - 1], (SUBLANES, w)))
    return row, p_r, p_i, steps, a8


def _scan_tile(xr, xi, cr, ci, consts, reverse):
    row, p_r, p_i, steps, (a8r, a8i) = consts
    for k, s_r, s_i in steps:
        sh = (SUBLANES - k) if reverse else k
        qr = pltpu.roll(xr, sh, 0)
        qi = pltpu.roll(xi, sh, 0)
        xr, xi = xr + s_r * qr - s_i * qi, xi + s_r * qi + s_i * qr
    outr = xr + p_r * cr - p_i * ci
    outi = xi + p_r * ci + p_i * cr
    e = 0 if reverse else SUBLANES - 1
    er = jnp.broadcast_to(xr[e:e + 1, :], xr.shape)
    ei = jnp.broadcast_to(xi[e:e + 1, :], xi.shape)
    return outr, outi, er + a8r * cr - a8i * ci, ei + a8r * ci + a8i * cr


def _scan_fwd(a_rows, bu_re, bu_im):
    t, n = bu_re.shape
    tb, w = _blk(t, TB_SCAN), W_SCAN
    ntile = tb // SUBLANES

    def body(a_ref, br_ref, bi_ref, sr_ref, si_ref, car, cai):
        @pl.when(pl.program_id(1) == 0)
        def _():
            car[...] = jnp.zeros(car.shape, F32)
            cai[...] = jnp.zeros(cai.shape, F32)
        consts = _scan_consts(a_ref, False)

        def pair(i, carry):
            o = pl.multiple_of(i * BF16_ROWS, BF16_ROWS)
            outs = []
            for h in range(2):
                rows = pl.ds(o + h * SUBLANES, SUBLANES)
                outr, outi, ncr, nci = _scan_tile(br_ref[rows, :], bi_ref[rows, :], carry[0], carry[1], consts, False)
                outs.append((outr, outi))
                carry = (ncr, nci)
            sr_ref[pl.ds(o, BF16_ROWS), :] = jnp.concatenate([outs[0][0], outs[1][0]], axis=0).astype(BF16)
            si_ref[pl.ds(o, BF16_ROWS), :] = jnp.concatenate([outs[0][1], outs[1][1]], axis=0).astype(BF16)
            return carry

        def pairs(i, carry):
            for s in range(SCAN_UNROLL // 2):
                carry = pair(i * (SCAN_UNROLL // 2) + s, carry)
            return carry

        cr, ci = lax.fori_loop(0, ntile // SCAN_UNROLL, pairs, (car[...], cai[...]))
        car[...] = cr
        cai[...] = ci

    spec = pl.BlockSpec((tb, w), lambda s, k: (k, s))
    sds = jax.ShapeDtypeStruct((t, n), BF16)
    return pl.pallas_call(
        body, name="scan_fwd", grid=(n // w, t // tb), out_shape=(sds, sds),
        in_specs=[pl.BlockSpec((SUBLANES, w), lambda s, k: (0, s)), spec, spec], out_specs=(spec, spec),
        scratch_shapes=[pltpu.VMEM((SUBLANES, w), F32), pltpu.VMEM((SUBLANES, w), F32)],
        compiler_params=_cparams(("parallel", "arbitrary"), VMEM_MID),
    )(a_rows, bu_re, bu_im)


def _scan_bwd(a_rows, g_re, g_im, s_re, s_im):
    t, n = g_re.shape
    tb, w = _blk(t, TB_SCAN), W_SCAN
    ntile = tb // SUBLANES
    npair = tb // BF16_ROWS
    nt = t // tb

    def body(a_ref, gr_ref, gi_ref, sr_ref, si_ref, or_ref, oi_ref, gar_ref, gai_ref, car, cai):
        @pl.when(pl.program_id(1) == 0)
        def _():
            car[...] = jnp.zeros(car.shape, F32)
            cai[...] = jnp.zeros(cai.shape, F32)
            gar_ref[...] = jnp.zeros(gar_ref.shape, F32)
            gai_ref[...] = jnp.zeros(gai_ref.shape, F32)
        consts = _scan_consts(a_ref, True)
        row = consts[0]

        def pair(i, carry):
            cr, ci, accr, acci = carry
            o = pl.multiple_of((npair - 1 - i) * BF16_ROWS, BF16_ROWS)
            s_r = sr_ref[pl.ds(o, BF16_ROWS), :].astype(F32)
            s_i = si_ref[pl.ds(o, BF16_ROWS), :].astype(F32)
            outs = [None, None]
            for h in (1, 0):
                rows = pl.ds(o + h * SUBLANES, SUBLANES)
                outr, outi, ncr, nci = _scan_tile(gr_ref[rows, :], gi_ref[rows, :], cr, ci, consts, True)
                outs[h] = (outr, outi)
                gnr = jnp.where(row == SUBLANES - 1, cr, pltpu.roll(outr, SUBLANES - 1, 0))
                gni = jnp.where(row == SUBLANES - 1, ci, pltpu.roll(outi, SUBLANES - 1, 0))
                sr = s_r[h * SUBLANES:(h + 1) * SUBLANES, :]
                si = s_i[h * SUBLANES:(h + 1) * SUBLANES, :]
                accr, acci = accr + sr * gnr + si * gni, acci + sr * gni - si * gnr
                cr, ci = ncr, nci
            or_ref[pl.ds(o, BF16_ROWS), :] = jnp.concatenate([outs[0][0], outs[1][0]], axis=0).astype(BF16)
            oi_ref[pl.ds(o, BF16_ROWS), :] = jnp.concatenate([outs[0][1], outs[1][1]], axis=0).astype(BF16)
            return cr, ci, accr, acci

        def pairs(i, carry):
            for s in range(SCAN_UNROLL // 2):
                carry = pair(i * (SCAN_UNROLL // 2) + s, carry)
            return carry

        cr, ci, accr, acci = lax.fori_loop(0, ntile // SCAN_UNROLL, pairs,
                                           (car[...], cai[...], gar_ref[...], gai_ref[...]))
        car[...] = cr
        cai[...] = ci
        gar_ref[...] = accr
        gai_ref[...] = acci

    spec = pl.BlockSpec((tb, w), lambda s, k: (nt - 1 - k, s))
    aspec = pl.BlockSpec((SUBLANES, w), lambda s, k: (0, s))
    sds = jax.ShapeDtypeStruct((t, n), BF16)
    asds = jax.ShapeDtypeStruct((SUBLANES, n), F32)
    return pl.pallas_call(
        body, name="scan_bwd", grid=(n // w, nt), out_shape=(sds, sds, asds, asds),
        in_specs=[aspec, spec, spec, spec, spec], out_specs=(spec, spec, aspec, aspec),
        scratch_shapes=[pltpu.VMEM((SUBLANES, w), F32), pltpu.VMEM((SUBLANES, w), F32)],
        compiler_params=_cparams(("parallel", "arbitrary"), VMEM_MID),
    )(a_rows, g_re, g_im, s_re, s_im)


def _mix_in(x, vec, w_in_st, b_re, b_im):
    t, d = x.shape
    ns, _, nc = w_in_st.shape
    dssm, nstate = b_re.shape
    du, ds = dssm // SSM_SPLIT, nstate // SSM_SPLIT
    tb = _blk(t, TB_MIX)

    def body(x_ref, vec_ref, w_ref, bre_ref, bim_ref, proj_ref, bur_ref, bui_ref, h1_ref):
        xv = x_ref[...]
        r = lax.rsqrt(_rowmean(xv * xv) + EPS)
        h = xv * r * vec_ref[0:1, :] * vec_ref[1:2, :] + vec_ref[2:3, :]
        hb = h.astype(BF16)
        h1_ref[...] = hb
        u = None
        for j in range(ns):
            pj = jnp.dot(hb, w_ref[j], preferred_element_type=F32)
            proj_ref[:, j * nc:(j + 1) * nc] = pj
            if j == 0:
                u = pj
        ub = u.astype(BF16)
        for q in range(SSM_SPLIT):
            rq, cq = slice(q * du, (q + 1) * du), slice(q * ds, (q + 1) * ds)
            bur_ref[:, cq] = jnp.dot(ub[:, rq], bre_ref[rq, cq], preferred_element_type=F32)
            bui_ref[:, cq] = jnp.dot(ub[:, rq], bim_ref[rq, cq], preferred_element_type=F32)

    return pl.pallas_call(
        body, name="mix_in", grid=(t // tb,),
        out_shape=(jax.ShapeDtypeStruct((t, ns * nc), F32), jax.ShapeDtypeStruct((t, nstate), F32),
                   jax.ShapeDtypeStruct((t, nstate), F32), jax.ShapeDtypeStruct((t, d), BF16)),
        in_specs=[_rows(tb, d), _full((SUBLANES, d)), _full(w_in_st.shape), _full(b_re.shape), _full(b_im.shape)],
        out_specs=(_rows(tb, ns * nc), _rows(tb, nstate), _rows(tb, nstate), _rows(tb, d)),
        compiler_params=_cparams(("parallel",), VMEM_BIG),
    )(x, vec, w_in_st, b_re, b_im)


def _head_ms(y, h_ref):
    return _split_dot(y * y, h_ref[...])


def _conv3(x, halo, w_ref):
    return w_ref[0:1, :] * _shift_down(x, halo, 2) + w_ref[1:2, :] * _shift_down(x, halo, 1) + w_ref[2:3, :] * x


def _mix_out(x, proj, s_re, s_im, c_re, c_im, v512, convw, glu_w, h16, h64, w_out, vd):
    t, d = x.shape
    dh = c_re.shape[1]
    nstate = s_re.shape[1]
    du, ds = dh // SSM_SPLIT, nstate // SSM_SPLIT
    tb = _blk(t, TB_MIX)

    def body(x_ref, u_ref, bg_ref, cg_ref, v_ref, cgh_ref, vh_ref, sr_ref, si_ref, cre_ref, cim_ref, p_ref,
             cw_ref, gw_ref, h16_ref, h64_ref, wo_ref, vd_ref, y1_ref, o_ref, x2_ref):
        i = pl.program_id(0)
        u = u_ref[...]
        ys = []
        for q in range(SSM_SPLIT):
            rq, cq = slice(q * ds, (q + 1) * ds), slice(q * du, (q + 1) * du)
            ys.append(_dot(sr_ref[:, rq], cre_ref[rq, cq]) - _dot(si_ref[:, rq], cim_ref[rq, cq]))
        ys = jnp.concatenate(ys, axis=1)
        y1 = ys + p_ref[0:1, :] * u
        y1_ref[...] = y1
        z = _gelu(y1)
        q = _dot(z, gw_ref[...]) + p_ref[1:2, :]
        ya = z * _sigmoid(q)
        na = ya * lax.rsqrt(_head_ms(ya, h16_ref) + EPS) * p_ref[2:3, :]
        cv = cg_ref[...] * v_ref[...]
        cvh = jnp.where(i > 0, cgh_ref[...] * vh_ref[...], 0.0)
        yb = bg_ref[...] * _conv3(cv, cvh, cw_ref)
        nb = yb * lax.rsqrt(_head_ms(yb, h64_ref) + EPS) * p_ref[3:4, :]
        o = _dot(na, wo_ref[0:dh, :]) + _dot(nb, wo_ref[dh:2 * dh, :])
        o_ref[...] = o
        on = o * lax.rsqrt(_rowmean(o * o) + EPS) * vd_ref[0:1, :]
        x2_ref[...] = x_ref[...] + vd_ref[1:2, :] * on

    return pl.pallas_call(
        body, name="mix_out", grid=(t // tb,),
        out_shape=(jax.ShapeDtypeStruct((t, dh), F32), jax.ShapeDtypeStruct((t, d), F32),
                   jax.ShapeDtypeStruct((t, d), F32)),
        in_specs=[_rows(tb, d), _rows(tb, dh, 0), _rows(tb, dh, 1), _rows(tb, dh, 2), _rows(tb, dh, 3),
                  _halo_prev(tb, dh, 2), _halo_prev(tb, dh, 3), _rows(tb, nstate), _rows(tb, nstate),
                  _full(c_re.shape), _full(c_im.shape), _full(v512.shape), _full(convw.shape), _full(glu_w.shape),
                  _full(h16.shape), _full(h64.shape), _full(w_out.shape), _full(vd.shape)],
        out_specs=(_rows(tb, dh), _rows(tb, d), _rows(tb, d)),
        compiler_params=_cparams(("parallel",), VMEM_BIG),
    )(x, proj, proj, proj, proj, proj, proj, s_re, s_im, c_re, c_im, v512, convw, glu_w, h16, h64, w_out, vd)


def _ffn_up(x2, vec, w_up_st):
    t, d = x2.shape
    ns, _, nc = w_up_st.shape
    tb = _blk(t, TB_FFN)

    def body(x_ref, vec_ref, w_ref, up_ref, h2_ref):
        xv = x_ref[...]
        r = lax.rsqrt(_rowmean(xv * xv) + EPS)
        h = xv * r * vec_ref[0:1, :] * vec_ref[1:2, :] + vec_ref[2:3, :]
        hb = h.astype(BF16)
        h2_ref[...] = hb
        for j in range(ns):
            up_ref[:, j * nc:(j + 1) * nc] = jnp.dot(hb, w_ref[j], preferred_element_type=F32)

    return pl.pallas_call(
        body, name="ffn_up", grid=(t // tb,),
        out_shape=(jax.ShapeDtypeStruct((t, ns * nc), F32), jax.ShapeDtypeStruct((t, d), BF16)),
        in_specs=[_rows(tb, d), _full((SUBLANES, d)), _resident(w_up_st.shape)],
        out_specs=(_rows(tb, ns * nc), _rows(tb, d)),
        compiler_params=_cparams(("parallel",), VMEM_BIG),
    )(x2, vec, w_up_st)


def _ffn_down(up, fw, w_down, w_down_t, x2, tgt, vd):
    t, nh = up.shape
    dff, d = w_down.shape
    tb = _blk(t, TB_FFN)
    inv_d = 1.0 / d

    def body(up_ref, uph_ref, fw_ref, wd_ref, wdt_ref, x2_ref, tgt_ref, vd_ref,
             act_ref, ddn_ref, dout_ref, dhid_ref, vec_ref, loss_ref, a_s, vv_s, sg_s):
        i = pl.program_id(0)

        def conv_cols(sl):
            x = up_ref[:, sl]
            halo = jnp.where(i > 0, uph_ref[:, sl], 0.0)
            return (fw_ref[0:1, sl] * _shift_down(x, halo, 2) + fw_ref[1:2, sl] * _shift_down(x, halo, 1)
                    + fw_ref[2:3, sl] * x)

        dn = None
        for o in range(0, dff, CW_FFN):
            sl = slice(o, o + CW_FFN)
            a = conv_cols(sl)
            vv = conv_cols(slice(dff + o, dff + o + CW_FFN))
            sg = _sigmoid(a)
            a_s[:, sl] = a
            vv_s[:, sl] = vv
            sg_s[:, sl] = sg
            actb = (a * sg * vv).astype(BF16)
            act_ref[:, sl] = actb
            pj = lax.dot_general(actb, wdt_ref[:, sl], (((1,), (1,)), ((), ())), preferred_element_type=F32)
            dn = pj if dn is None else dn + pj
        r3 = lax.rsqrt(_rowmean(dn * dn) + EPS)
        xn = dn * r3
        g = vd_ref[0:1, :]
        gt2 = vd_ref[1:2, :]
        dnn = xn * g
        diff = x2_ref[...] + gt2 * dnn - tgt_ref[...]
        part = 0.5 * inv_d * jnp.sum(diff * diff)

        @pl.when(i == 0)
        def _():
            loss_ref[...] = jnp.zeros(loss_ref.shape, F32)
        loss_ref[...] += part
        dout = diff * inv_d
        dout_ref[...] = dout
        ddnn = dout * gt2
        _acc_rows(vec_ref, i == 0, [_colsum(dout * dnn), _colsum(ddnn * xn)])
        dxn = ddnn * g
        ddn = r3 * (dxn - xn * _rowmean(dxn * xn))
        ddnb = ddn.astype(BF16)
        ddn_ref[...] = ddnb
        for o in range(0, dff, CW_FFN):
            sl = slice(o, o + CW_FFN)
            dact = lax.dot_general(ddnb, wd_ref[sl, :], (((1,), (1,)), ((), ())), preferred_element_type=F32)
            a, vv, sg = a_s[:, sl], vv_s[:, sl], sg_s[:, sl]
            dhid_ref[:, sl] = (dact * vv * sg * (1.0 + a * (1.0 - sg))).astype(BF16)
            dhid_ref[:, dff + o:dff + o + CW_FFN] = (dact * (a * sg)).astype(BF16)

    return pl.pallas_call(
        body, name="ffn_down", grid=(t // tb,),
        scratch_shapes=[pltpu.VMEM((tb, dff), F32)] * 3,
        out_shape=(jax.ShapeDtypeStruct((t, dff), BF16), jax.ShapeDtypeStruct((t, d), BF16),
                   jax.ShapeDtypeStruct((t, d), F32), jax.ShapeDtypeStruct((t, nh), BF16),
                   jax.ShapeDtypeStruct((SUBLANES, d), F32), jax.ShapeDtypeStruct((SUBLANES, 128), F32)),
        in_specs=[_rows(tb, nh), _halo_prev(tb, nh), _full(fw.shape), _resident(w_down.shape),
                  _resident(w_down_t.shape), _rows(tb, d),
                  _rows(tb, d), _full(vd.shape)],
        out_specs=(_rows(tb, dff), _rows(tb, d), _rows(tb, d), _rows(tb, nh), _full((SUBLANES, d)),
                   _full((SUBLANES, 128))),
        compiler_params=_cparams(("arbitrary",), VMEM_BIG),
    )(up, up, fw, w_down, w_down_t, x2, tgt, vd)


def _ffn_up_bwd(dhid, up, fw, x2, dout, vec, w_up_st):
    t, nh = dhid.shape
    d = x2.shape[1]
    ns, _, nc = w_up_st.shape
    tb = _blk(t, TB_FFN)
    nblk = t // tb
    cw = 128

    def body(dh_ref, dhn_ref, up_ref, fw_ref, x2_ref, dout_ref, vec_ref, w_ref,
             dx2_ref, dup_ref, vp_ref, df_ref):
        i = pl.program_id(0)

        @pl.when(i == 0)
        def _():
            df_ref[...] = jnp.zeros(df_ref.shape, F32)
        dh2 = None
        for j in range(ns):
            for o in range(j * nc, (j + 1) * nc, cw):
                sl = slice(o, o + cw)
                dh = dh_ref[:, sl].astype(F32)
                dhn = jnp.where(i < nblk - 1, dhn_ref[:, sl].astype(F32), 0.0)
                dh1 = _shift_up(dh, dhn, 1)
                dh2s = _shift_up(dh, dhn, 2)
                dup_ref[:, sl] = (fw_ref[2:3, sl] * dh + fw_ref[1:2, sl] * dh1 + fw_ref[0:1, sl] * dh2s).astype(BF16)
                up_v = up_ref[:, sl]
                df_ref[0:1, sl] += _colsum(dh2s * up_v)
                df_ref[1:2, sl] += _colsum(dh1 * up_v)
                df_ref[2:3, sl] += _colsum(dh * up_v)
            pj = lax.dot_general(dup_ref[:, j * nc:(j + 1) * nc], w_ref[j], (((1,), (1,)), ((), ())),
                                 preferred_element_type=F32)
            dh2 = pj if dh2 is None else dh2 + pj
        xv = x2_ref[...]
        r = lax.rsqrt(_rowmean(xv * xv) + EPS)
        xn = xv * r
        g = vec_ref[0:1, :]
        hg = xn * g
        dhg = dh2 * vec_ref[1:2, :]
        _acc_rows(vp_ref, i == 0, [_colsum(dh2), _colsum(dh2 * hg), _colsum(dhg * xn)])
        dxn = dhg * g
        dx2_ref[...] = dout_ref[...] + r * (dxn - xn * _rowmean(dxn * xn))

    return pl.pallas_call(
        body, name="ffn_up_bwd", grid=(nblk,),
        out_shape=(jax.ShapeDtypeStruct((t, d), F32), jax.ShapeDtypeStruct((t, nh), BF16),
                   jax.ShapeDtypeStruct((SUBLANES, d), F32), jax.ShapeDtypeStruct((SUBLANES, nh), F32)),
        in_specs=[_rows(tb, nh), _halo_next(tb, nh, t, rows=BF16_ROWS), _rows(tb, nh), _full(fw.shape),
                  _rows(tb, d), _rows(tb, d), _full(vec.shape), _resident(w_up_st.shape)],
        out_specs=(_rows(tb, d), _rows(tb, nh), _full((SUBLANES, d)), _full((SUBLANES, nh))),
        compiler_params=_cparams(("arbitrary",), VMEM_BIG),
    )(dhid, dhid, up, fw, x2, dout, vec, w_up_st)


def _mix_out_bwd(dx2, o, y1, proj, c_re, c_im, v512, convw, glu_w, h16, h64, w_out, vd):
    t, d = dx2.shape
    dh = y1.shape[1]
    nstate = c_re.shape[0]
    du, ds = dh // SSM_SPLIT, nstate // SSM_SPLIT
    tb = _blk(t, TB_MIX)

    def body(dx2_ref, o_ref, y1_ref, u_ref, bg_ref, cg_ref, v_ref, cgh_ref, vh_ref, cre_ref, cim_ref, p_ref,
             cw_ref, gw_ref, h16_ref, h64_ref, wo_ref, vd_ref,
             do_ref, ycat_ref, z_ref, dq_ref, dy1_ref, gr_ref, gi_ref, dcc_ref, dbg_ref, vpd_ref, vp5_ref):
        i = pl.program_id(0)
        first = i == 0
        ov = o_ref[...]
        ro = lax.rsqrt(_rowmean(ov * ov) + EPS)
        on_ = ov * ro
        g = vd_ref[0:1, :]
        dx2v = dx2_ref[...]
        don = dx2v * vd_ref[1:2, :]
        _acc_rows(vpd_ref, first, [_colsum(dx2v * on_ * g), _colsum(don * on_)])
        dxn = don * g
        dob = (ro * (dxn - on_ * _rowmean(dxn * on_))).astype(BF16)
        do_ref[...] = dob
        dyc_a = lax.dot_general(dob, wo_ref[0:dh, :], (((1,), (1,)), ((), ())), preferred_element_type=F32)
        dyc_b = lax.dot_general(dob, wo_ref[dh:2 * dh, :], (((1,), (1,)), ((), ())), preferred_element_type=F32)
        y1v = y1_ref[...]
        u = u_ref[...]
        z = _gelu(y1v)
        zb = z.astype(BF16)
        z_ref[...] = zb
        sg = _sigmoid(jnp.dot(zb, gw_ref[...], preferred_element_type=F32) + p_ref[1:2, :])
        ya = z * sg
        ra = lax.rsqrt(_head_ms(ya, h16_ref) + EPS)
        yan = ya * ra
        ga = p_ref[2:3, :]
        ycat_ref[:, 0:dh] = (yan * ga).astype(BF16)
        dyn = dyc_a * ga
        dya = ra * (dyn - yan * _split_dot(dyn * yan, h16_ref[...]))
        dq = dya * z * sg * (1.0 - sg)
        dqb = dq.astype(BF16)
        dq_ref[...] = dqb
        dz = dya * sg + lax.dot_general(dqb, gw_ref[...], (((1,), (1,)), ((), ())), preferred_element_type=F32)
        dy1 = dz * _gelu_grad(y1v)
        dy1_ref[...] = dy1
        dy1b = dy1.astype(BF16)
        for q in range(SSM_SPLIT):
            rq, cq = slice(q * ds, (q + 1) * ds), slice(q * du, (q + 1) * du)
            gr_ref[:, rq] = lax.dot_general(dy1b[:, cq], cre_ref[rq, cq], (((1,), (1,)), ((), ())),
                                            preferred_element_type=F32)
            gi_ref[:, rq] = -lax.dot_general(dy1b[:, cq], cim_ref[rq, cq], (((1,), (1,)), ((), ())),
                                             preferred_element_type=F32)
        bg = bg_ref[...]
        cv = cg_ref[...] * v_ref[...]
        cvh = jnp.where(i > 0, cgh_ref[...] * vh_ref[...], 0.0)
        cv1 = _shift_down(cv, cvh, 1)
        cv2 = _shift_down(cv, cvh, 2)
        cc = cw_ref[0:1, :] * cv2 + cw_ref[1:2, :] * cv1 + cw_ref[2:3, :] * cv
        yb = bg * cc
        rb = lax.rsqrt(_head_ms(yb, h64_ref) + EPS)
        ybn = yb * rb
        gb = p_ref[3:4, :]
        ycat_ref[:, dh:2 * dh] = (ybn * gb).astype(BF16)
        dynb = dyc_b * gb
        dyb = rb * (dynb - ybn * _split_dot(dynb * ybn, h64_ref[...]))
        dcc = dyb * bg
        dbg_ref[...] = dyb * cc
        dcc_ref[...] = dcc
        _acc_rows(vp5_ref, first, [_colsum(dyc_a * yan), _colsum(dyc_b * ybn), _colsum(dq), _colsum(dy1 * u),
                                   _colsum(dcc * cv2), _colsum(dcc * cv1), _colsum(dcc * cv)])

    return pl.pallas_call(
        body, name="mix_out_bwd", grid=(t // tb,),
        out_shape=(jax.ShapeDtypeStruct((t, d), BF16), jax.ShapeDtypeStruct((t, 2 * dh), BF16),
                   jax.ShapeDtypeStruct((t, dh), BF16), jax.ShapeDtypeStruct((t, dh), BF16),
                   jax.ShapeDtypeStruct((t, dh), F32), jax.ShapeDtypeStruct((t, nstate), F32),
                   jax.ShapeDtypeStruct((t, nstate), F32), jax.ShapeDtypeStruct((t, dh), F32),
                   jax.ShapeDtypeStruct((t, dh), F32), jax.ShapeDtypeStruct((SUBLANES, d), F32),
                   jax.ShapeDtypeStruct((SUBLANES, dh), F32)),
        in_specs=[_rows(tb, d), _rows(tb, d), _rows(tb, dh), _rows(tb, dh, 0), _rows(tb, dh, 1), _rows(tb, dh, 2),
                  _rows(tb, dh, 3), _halo_prev(tb, dh, 2), _halo_prev(tb, dh, 3), _full(c_re.shape), _full(c_im.shape),
                  _full(v512.shape), _full(convw.shape), _full(glu_w.shape), _full(h16.shape), _full(h64.shape),
                  _full(w_out.shape), _full(vd.shape)],
        out_specs=(_rows(tb, d), _rows(tb, 2 * dh), _rows(tb, dh), _rows(tb, dh), _rows(tb, dh), _rows(tb, nstate),
                   _rows(tb, nstate), _rows(tb, dh), _rows(tb, dh), _full((SUBLANES, d)), _full((SUBLANES, dh))),
        compiler_params=_cparams(("arbitrary",), VMEM_BIG),
    )(dx2, o, y1, proj, proj, proj, proj, proj, proj, c_re, c_im, v512, convw, glu_w, h16, h64, w_out, vd)


def _mix_in_bwd(gt_re, gt_im, b_re, b_im, dy1, dcc, dbg, proj, x, dx2, vec, v512, convw, w_in_st):
    t, d = x.shape
    dh = dy1.shape[1]
    nstate = gt_re.shape[1]
    du_w, ds = dh // SSM_SPLIT, nstate // SSM_SPLIT
    ns, _, nc = w_in_st.shape
    tb = _blk(t, TB_MIX)
    nblk = t // tb

    def body(gr_ref, gi_ref, bre_ref, bim_ref, dy1_ref, dcc_ref, dccn_ref, dbg_ref, cg_ref, v_ref, x_ref, dx2_ref,
             vec_ref, p_ref, cw_ref, w_ref, gx_ref, dproj_ref, vp_ref):
        i = pl.program_id(0)
        du = []
        for q in range(SSM_SPLIT):
            rq, cq = slice(q * du_w, (q + 1) * du_w), slice(q * ds, (q + 1) * ds)
            du.append(lax.dot_general(gr_ref[:, cq].astype(BF16), bre_ref[rq, cq], (((1,), (1,)), ((), ())),
                                      preferred_element_type=F32)
                      + lax.dot_general(gi_ref[:, cq].astype(BF16), bim_ref[rq, cq], (((1,), (1,)), ((), ())),
                                        preferred_element_type=F32))
        du = dy1_ref[...] * p_ref[0:1, :] + jnp.concatenate(du, axis=1)
        dcc = dcc_ref[...]
        dccn = jnp.where(i < nblk - 1, dccn_ref[...], 0.0)
        dcv = (cw_ref[2:3, :] * dcc + cw_ref[1:2, :] * _shift_up(dcc, dccn, 1)
               + cw_ref[0:1, :] * _shift_up(dcc, dccn, 2))
        parts = [du, dbg_ref[...], dcv * v_ref[...], dcv * cg_ref[...]]
        dh1 = None
        for j in range(ns):
            pb = parts[j].astype(BF16)
            dproj_ref[:, j * nc:(j + 1) * nc] = pb
            pj = lax.dot_general(pb, w_ref[j], (((1,), (1,)), ((), ())), preferred_element_type=F32)
            dh1 = pj if dh1 is None else dh1 + pj
        xv = x_ref[...]
        r = lax.rsqrt(_rowmean(xv * xv) + EPS)
        xn = xv * r
        g = vec_ref[0:1, :]
        hg = xn * g
        dhg = dh1 * vec_ref[1:2, :]
        _acc_rows(vp_ref, i == 0, [_colsum(dh1), _colsum(dh1 * hg), _colsum(dhg * xn)])
        dxn = dhg * g
        gx_ref[...] = dx2_ref[...] + r * (dxn - xn * _rowmean(dxn * xn))

    assert nc == dh and ns == 4
    return pl.pallas_call(
        body, name="mix_in_bwd", grid=(nblk,),
        out_shape=(jax.ShapeDtypeStruct((t, d), F32), jax.ShapeDtypeStruct((t, ns * nc), BF16),
                   jax.ShapeDtypeStruct((SUBLANES, d), F32)),
        in_specs=[_rows(tb, nstate), _rows(tb, nstate), _full(b_re.shape), _full(b_im.shape), _rows(tb, dh),
                  _rows(tb, dh), _halo_next(tb, dh, t), _rows(tb, dh), _rows(tb, dh, 2), _rows(tb, dh, 3),
                  _rows(tb, d), _rows(tb, d), _full(vec.shape), _full(v512.shape), _full(convw.shape),
                  _full(w_in_st.shape)],
        out_specs=(_rows(tb, d), _rows(tb, ns * nc), _full((SUBLANES, d))),
        compiler_params=_cparams(("arbitrary",), VMEM_BIG),
    )(gt_re, gt_im, b_re, b_im, dy1, dcc, dcc, dbg, proj, proj, x, dx2, vec, v512, convw, w_in_st)


def _matmul_tn(a, b, m, bn, out_dtype, name, diag=False, bt=TB_TN, after=None):
    t = a.shape[0]
    n = b.shape[1]
    bt = _blk(t, bt)
    nk = t // bt
    extra = [] if after is None else [after]
    a_map = (lambda j, k: (k, j)) if diag else (lambda j, k: (k, 0))

    def body(a_ref, b_ref, *rest):
        o_ref, acc_ref = rest[-2:]
        k = pl.program_id(1)

        @pl.when(k == 0)
        def _():
            acc_ref[...] = jnp.zeros(acc_ref.shape, F32)
        acc_ref[...] += _dot_tn(a_ref[...], b_ref[...])

        @pl.when(k == nk - 1)
        def _():
            o_ref[...] = acc_ref[...].astype(out_dtype)

    return pl.pallas_call(
        body, name=name, grid=(n // bn, nk),
        out_shape=jax.ShapeDtypeStruct((n // bn, m, bn), out_dtype),
        in_specs=[pl.BlockSpec((bt, m), a_map), pl.BlockSpec((bt, bn), lambda j, k: (k, j))]
        + [pl.BlockSpec(memory_space=pl.ANY)] * len(extra),
        out_specs=pl.BlockSpec((None, m, bn), lambda j, k: (j, 0, 0)),
        scratch_shapes=[pltpu.VMEM((m, bn), F32)],
        compiler_params=_cparams(("parallel", "arbitrary"), VMEM_BIG),
    )(a, b, *extra)


def _ssm_bgrad(d_bre, d_bim, bt_re, bt_im, rows_in, fold):
    gh, cb = d_bre.shape
    nb = SSM_SPLIT
    rb = gh // nb
    gp = nb * cb
    p = fold.shape[1]

    def body(dr_ref, di_ref, br_ref, bi_ref, rin_ref, f_ref, dbr_ref, dbi_ref, rout_ref):
        row = lax.broadcasted_iota(jnp.int32, (rb, cb), 0)
        col = lax.broadcasted_iota(jnp.int32, (rb, cb), 1)
        mask = (row >> 4) == (col >> 6)
        gr = jnp.where(mask, dr_ref[...], 0.0)
        gi = jnp.where(mask, di_ref[...], 0.0)
        cr, ci = rin_ref[0:1, :], rin_ref[1:2, :]
        dbr_ref[...] = _split3_dot(cr * gr + ci * gi, f_ref[...])
        dbi_ref[...] = _split3_dot(cr * gi - ci * gr, f_ref[...])
        br, bi = br_ref[...], bi_ref[...]
        rout_ref[...] = jnp.zeros(rout_ref.shape, F32)
        rout_ref[0:1, :] = _colsum(br * gr + bi * gi)
        rout_ref[1:2, :] = _colsum(br * gi - bi * gr)

    dspec = pl.BlockSpec((rb, cb), lambda j: (j, 0))
    bspec = pl.BlockSpec((rb, cb), lambda j: (j, j))
    rspec = pl.BlockSpec((SUBLANES, cb), lambda j: (0, j))
    ospec = pl.BlockSpec((rb, p), lambda j: (j, 0))
    return pl.pallas_call(
        body, name="ssm_bgrad", grid=(nb,),
        out_shape=(jax.ShapeDtypeStruct((gh, p), F32), jax.ShapeDtypeStruct((gh, p), F32),
                   jax.ShapeDtypeStruct((SUBLANES, gp), F32)),
        in_specs=[dspec, dspec, bspec, bspec, rspec, _full(fold.shape)],
        out_specs=(ospec, ospec, rspec),
        compiler_params=_cparams(("parallel",)),
    )(d_bre, d_bim, bt_re, bt_im, rows_in, fold)


def _ssm_cgrad(d_cre, d_cim, fold):
    gp, cb = d_cre.shape
    nb = SSM_SPLIT
    rb = gp // nb
    h = fold.shape[1]

    def body(dr_ref, di_ref, f_ref, cr_ref, ci_ref):
        row = lax.broadcasted_iota(jnp.int32, (rb, cb), 0)
        col = lax.broadcasted_iota(jnp.int32, (rb, cb), 1)
        mask = (row >> 6) == (col >> 4)
        cr_ref[...] = _split3_dot(jnp.where(mask, dr_ref[...], 0.0), f_ref[...])
        ci_ref[...] = -_split3_dot(jnp.where(mask, di_ref[...], 0.0), f_ref[...])

    cspec = pl.BlockSpec((rb, cb), lambda j: (j, 0))
    ospec = pl.BlockSpec((rb, h), lambda j: (j, 0))
    return pl.pallas_call(
        body, name="ssm_cgrad", grid=(nb,),
        out_shape=(jax.ShapeDtypeStruct((gp, h), F32),) * 2,
        in_specs=[cspec, cspec, _full(fold.shape)], out_specs=(ospec, ospec),
        compiler_params=_cparams(("parallel",)),
    )(d_cre, d_cim, fold)


def _ssm_lamgrad(lam_re, lam_im, log_step, abar_re, abar_im, coef_re, coef_im, gc_re, gc_im, ga_re, ga_im):
    g, p = lam_re.shape

    def body(lr_ref, li_ref, ls_ref, ar_ref, ai_ref, cr_ref, ci_ref, gcr_ref, gci_ref, gar_ref, gai_ref,
             dlr_ref, dli_ref, dls_ref):
        lam_raw = lr_ref[...]
        lr = jnp.minimum(lam_raw, LAMBDA_RE_MAX)
        li = li_ref[...]
        st = jnp.exp(ls_ref[...])
        den = lr * lr + li * li
        gcr, gci = gcr_ref[...], gci_ref[...]
        gab_r = gar_ref[...] + (lr * gcr - li * gci) / den
        gab_i = gai_ref[...] + (lr * gci + li * gcr) / den
        cr, ci = cr_ref[...], ci_ref[...]
        wr = -(cr * lr + ci * li) / den
        wi = -(ci * lr - cr * li) / den
        gl_r = wr * gcr + wi * gci
        gl_i = wr * gci - wi * gcr
        ar, ai = ar_ref[...], ai_ref[...]
        gw_r = ar * gab_r + ai * gab_i
        gw_i = ar * gab_i - ai * gab_r
        gl_r = gl_r + st * gw_r
        gl_i = gl_i + st * gw_i
        pass_through = jnp.where(lam_raw < LAMBDA_RE_MAX, 1.0, jnp.where(lam_raw == LAMBDA_RE_MAX, 0.5, 0.0))
        dlr_ref[...] = gl_r * pass_through
        dli_ref[...] = gl_i
        dls_ref[...] = st * jnp.sum(lr * gw_r + li * gw_i, axis=1, keepdims=True)

    sds = jax.ShapeDtypeStruct((g, p), F32)
    return pl.pallas_call(body, name="ssm_lamgrad", out_shape=(sds, sds, jax.ShapeDtypeStruct((g, 1), F32)))(
        lam_re, lam_im, log_step, abar_re, abar_im, coef_re, coef_im, gc_re, gc_im, ga_re, ga_im)


def _row_block(r, most=256):
    for rb in range(min(r, most), BF16_ROWS - 1, -1):
        if r % rb == 0 and rb % BF16_ROWS == 0:
            return rb
    return r


def _adamw_math(w, g, m, v):
    m = ADAM_B1 * m + (1.0 - ADAM_B1) * g
    v = ADAM_B2 * v + (1.0 - ADAM_B2) * (g * g)
    m_hat = m / (1.0 - ADAM_B1 ** ADAM_STEP)
    v_hat = v / (1.0 - ADAM_B2 ** ADAM_STEP)
    delta = -ADAM_LR * (m_hat / (jnp.sqrt(v_hat) + ADAM_EPS) + ADAM_WD * w)
    return delta, m, v


def _adamw_big(p_mine, p_sib, w, m, v, name):
    r, c = w.shape
    rb = _row_block(r)

    def body(a_ref, b_ref, w_ref, m_ref, v_ref, g_ref, d_ref, mo_ref, vo_ref):
        g = a_ref[...] + b_ref[...]
        g_ref[...] = g
        d_ref[...], mo_ref[...], vo_ref[...] = _adamw_math(w_ref[...], g, m_ref[...], v_ref[...])

    spec = pl.BlockSpec((rb, c), lambda i: (i, 0))
    sds = jax.ShapeDtypeStruct((r, c), F32)
    return pl.pallas_call(
        body, name=name, grid=(r // rb,), out_shape=(sds,) * 4, in_specs=[spec] * 5, out_specs=(spec,) * 4,
        compiler_params=_cparams(("parallel",), VMEM_MID),
    )(p_mine, p_sib, w, m, v)


def _sum_blocks(stack, name):
    n, r, c = stack.shape
    rb = _row_block(r)

    def body(s_ref, o_ref):
        acc = s_ref[0].astype(F32)
        for k in range(1, n):
            acc = acc + s_ref[k].astype(F32)
        o_ref[...] = acc

    return pl.pallas_call(
        body, name=name, grid=(r // rb,), out_shape=jax.ShapeDtypeStruct((r, c), F32),
        in_specs=[pl.BlockSpec((n, rb, c), lambda i: (0, i, 0))], out_specs=pl.BlockSpec((rb, c), lambda i: (i, 0)),
        compiler_params=_cparams(("parallel",), VMEM_MID),
    )(stack)


def _add2(a, b):
    def body(a_ref, b_ref, o_ref):
        o_ref[...] = a_ref[...] + b_ref[...]

    return pl.pallas_call(body, name="add_small", out_shape=jax.ShapeDtypeStruct(a.shape, F32))(a, b)


def _adamw_ada(c_all, dmod_cols, w, m, v):
    d, n = w.shape
    bn = 512

    def body(c_ref, dm_ref, w_ref, m_ref, v_ref, g_ref, d_ref, mo_ref, vo_ref):
        cc = c_ref[...]
        g = _dot_tn(cc * _sigmoid(cc), dm_ref[...])
        g_ref[...] = g
        d_ref[...], mo_ref[...], vo_ref[...] = _adamw_math(w_ref[...], g, m_ref[...], v_ref[...])

    spec = pl.BlockSpec((d, bn), lambda j: (0, j))
    sds = jax.ShapeDtypeStruct((d, n), F32)
    return pl.pallas_call(
        body, name="adamw_ada", grid=(n // bn,), out_shape=(sds,) * 4,
        in_specs=[_full((N_DEV, d)), pl.BlockSpec((N_DEV, bn), lambda j: (0, j)), spec, spec, spec],
        out_specs=(spec,) * 4, compiler_params=_cparams(("parallel",)),
    )(c_all, dmod_cols, w, m, v)


def _adamw_small(items):
    n = len(items)

    def body(*refs):
        ins, outs = refs[:4 * n], refs[4 * n:]
        for k in range(n):
            w_ref, g_ref, m_ref, v_ref = ins[4 * k:4 * k + 4]
            outs[3 * k][...], outs[3 * k + 1][...], outs[3 * k + 2][...] = _adamw_math(
                w_ref[...], g_ref[...], m_ref[...], v_ref[...])

    flat = [a for it in items for a in it]
    out_shape = tuple(jax.ShapeDtypeStruct(it[0].shape, F32) for it in items for _ in range(3))
    res = pl.pallas_call(body, name="adamw_small", out_shape=out_shape,
                         compiler_params=_cparams(vmem=VMEM_BIG))(*flat)
    return [tuple(res[3 * k:3 * k + 3]) for k in range(n)]


def _group_mean_matrix(n, group):
    idx = np.arange(n) // group
    return (idx[:, None] == idx[None, :]).astype(np.float32) / group


def _fold_matrix(n, period):
    return (np.arange(n)[:, None] % period == np.arange(period)[None, :]).astype(np.float32)


def _rows8(*rows):
    c = rows[0].shape[-1]
    pad = jnp.zeros((SUBLANES - len(rows), c), F32)
    return jnp.concatenate([r.reshape(1, c) for r in rows] + [pad], axis=0)


def _to_rows(a, width):
    flat = a.reshape(-1)
    n = -(-flat.shape[0] // width)
    flat = jnp.pad(flat, (0, n * width - flat.shape[0]))
    return flat.reshape(n, width)


def kernel(x, c, w_ada, b_ada, g_pre_mix, g_post_mix, w_in, ssm_lam_re, ssm_lam_im, ssm_log_step, ssm_b_re, ssm_b_im, ssm_c_re, ssm_c_im, ssm_d, glu_w, glu_b, g_out_ssm, conv_w, g_out_conv, w_out, g_pre_ffn, g_post_ffn, w_up, ffn_conv_w, w_down, loss_target, m_w_ada, m_b_ada, m_g_pre_mix, m_g_post_mix, m_w_in, m_ssm_lam_re, m_ssm_lam_im, m_ssm_log_step, m_ssm_b_re, m_ssm_b_im, m_ssm_c_re, m_ssm_c_im, m_ssm_d, m_glu_w, m_glu_b, m_g_out_ssm, m_conv_w, m_g_out_conv, m_w_out, m_g_pre_ffn, m_g_post_ffn, m_w_up, m_ffn_conv_w, m_w_down, v_w_ada, v_b_ada, v_g_pre_mix, v_g_post_mix, v_w_in, v_ssm_lam_re, v_ssm_lam_im, v_ssm_log_step, v_ssm_b_re, v_ssm_b_im, v_ssm_c_re, v_ssm_c_im, v_ssm_d, v_glu_w, v_glu_b, v_g_out_ssm, v_conv_w, v_g_out_conv, v_w_out, v_g_pre_ffn, v_g_post_ffn, v_w_up, v_ffn_conv_w, v_w_down):
    xs = x[0]
    tgt = loss_target[0]
    t, d = xs.shape
    xi, yi, ci = lax.axis_index("x"), lax.axis_index("y"), lax.axis_index("c")
    chip = 2 * xi + yi
    dev = 2 * chip + ci

    n_groups, n_state = ssm_lam_re.shape[1:]
    n_gch = ssm_b_re.shape[3]
    d_ssm = n_groups * n_gch
    gp = n_groups * n_state
    n_ada = w_ada.shape[2]
    d_ff = w_down.shape[1] * N_CHIPS
    n_upc = w_up.shape[2]

    w_names = ("w_in", "glu_w", "w_out", "w_up", "w_down")
    c_gath, _ = _allgather8(jnp.broadcast_to(c, (SUBLANES, d)), SUBLANES, "gather_c")
    c_all = c_gath.reshape(N_DEV, SUBLANES, d)[:, 0, :]

    def pad8(a):
        return jnp.concatenate([a, jnp.zeros((SUBLANES - a.shape[0], a.shape[1]), a.dtype)], axis=0)

    def start(name, arrs, after):
        return _chips_start(name, True, arrs, [_landing(a, chip) for a in arrs], after)

    w_names = ("w_in", "mod", "conv_w", "ffn_conv_w", "glu_w", "w_out", "w_up", "w_down")
    first = start("weights_start_in", [w_in[0].astype(BF16)], c_gath)
    b_sh = lax.dynamic_slice(b_ada, (0, chip * n_ada), (1, n_ada))
    mod_sh = _mod_shard(c_all + first[4][0:1, 0:1], w_ada[0], b_sh)
    second = start("weights_start_mod", [mod_sh, pad8(conv_w[0]), pad8(ffn_conv_w[0])], None)
    third = start("weights_start_rest", [w[0].astype(BF16) for w in (glu_w, w_out, w_up, w_down)], second[4])
    w_send, w_recv, w_src, w_land = [list(first[k]) + list(second[k]) + list(third[k]) for k in range(4)]
    w_token = third[4]

    def weights(names, after):
        ks = [w_names.index(nm) for nm in names]
        return _chips_wait("weights_wait_" + names[-1], True, [w_send[k] for k in ks], [w_recv[k] for k in ks],
                           [w_src[k] for k in ks], [w_land[k] for k in ks], after)

    lam_re, lam_im = ssm_lam_re[0], ssm_lam_im[0]
    log_step = ssm_log_step[0].reshape(n_groups, 1) + w_token[0:1, 0:1]
    abar_re, abar_im, coef_re, coef_im = _ssm_prep(lam_re, lam_im, log_step)
    a_rows = _rows8(abar_re.reshape(1, gp), abar_im.reshape(1, gp))
    coef_rows = _rows8(coef_re.reshape(1, gp), coef_im.reshape(1, gp))
    bt_re = jnp.tile(ssm_b_re[0].transpose(0, 2, 1).reshape(d_ssm, n_state), (1, n_groups))
    bt_im = jnp.tile(ssm_b_im[0].transpose(0, 2, 1).reshape(d_ssm, n_state), (1, n_groups))
    ct_re = jnp.tile(ssm_c_re[0].transpose(0, 2, 1).reshape(gp, n_gch), (1, n_groups))
    ct_im = jnp.tile(ssm_c_im[0].transpose(0, 2, 1).reshape(gp, n_gch), (1, n_groups))
    bblk_re, bblk_im, cblk_re, cblk_im = _ssm_blocks(bt_re, bt_im, ct_re, ct_im, coef_rows)

    h16 = jnp.asarray(_group_mean_matrix(d_ssm, n_gch), BF16)
    h64 = jnp.asarray(_group_mean_matrix(d_ssm, CONV_HEAD_DIM), BF16)

    g_mod, g_cw, g_fw, w_in_st = weights(("mod", "conv_w", "ffn_conv_w", "w_in"), bblk_re)
    mod_all = g_mod.transpose(1, 0, 2).reshape(N_DEV, N_CHIPS * n_ada)
    mod = lax.dynamic_slice(mod_all, (dev, 0), (1, N_CHIPS * n_ada))
    sh1, sc1, gt1, sh2, sc2, gt2 = [mod[:, k * d:(k + 1) * d] for k in range(6)]
    convw_full = pad8(g_cw[:, :3, :].transpose(1, 0, 2).reshape(3, d_ssm))
    fw_full = pad8(g_fw[:, :3, :].transpose(1, 0, 2).reshape(3, N_CHIPS * n_upc))

    v512 = _rows8(ssm_d, glu_b, g_out_ssm, g_out_conv)
    vec1 =_rows8(g_pre_mix, 1.0 + sc1, sh1)
    vd1 = _rows8(g_post_mix, gt1)
    vec2 = _rows8(g_pre_ffn, 1.0 + sc2, sh2)
    vd2 = _rows8(g_post_ffn, gt2)

    proj, bu_re, bu_im, h1b = _mix_in(xs, vec1, w_in_st, bblk_re, bblk_im)
    s_re, s_im = _scan_fwd(a_rows, bu_re, bu_im)
    g_glu, g_wout = weights(("glu_w", "w_out"), s_re)
    glu_full = g_glu.reshape(d_ssm, d_ssm)
    w_out_full = g_wout.reshape(2 * d_ssm, d)
    y1, o_mix, x2 = _mix_out(xs, proj, s_re, s_im, cblk_re, cblk_im, v512, convw_full, glu_full, h16, h64,
                             w_out_full, vd1)
    (w_up_st,) = weights(("w_up",), x2)
    up, h2b = _ffn_up(x2, vec2, w_up_st)
    (g_wdown,) = weights(("w_down",), up)
    w_down_full = g_wdown.reshape(d_ff, d)
    actb, ddnb, dout, dhid, vp_dn, loss_blk = _ffn_down(up, fw_full, w_down_full, w_down_full.T, x2, tgt, vd2)

    g_names = ("w_down", "w_up", "w_out", "glu_w", "w_in")
    gw_down = _matmul_tn(actb, ddnb, d_ff, d, BF16, "dw_down", bt=1024).reshape(N_CHIPS, d_ff // N_CHIPS, d)
    dx2, dupb, vp_up, df_rows = _ffn_up_bwd(dhid, up, fw_full, x2, dout, vec2, w_up_st)
    gw_up = _matmul_tn(h2b, dupb, d, n_upc, BF16, "dw_up", bt=2048)
    ga_send, ga_recv, ga_src, ga_land, ga_token = _chips_start(
        "grads_start_ffn", False, [gw_down, gw_up],
        [_landing(lax.dynamic_index_in_dim(g, chip, 0, False), chip) for g in (gw_down, gw_up)])
    (dob, ycatb, zb, dqb, dy1, g_re, g_im, dcc, dbg, vp_mo, vp5) = _mix_out_bwd(
        dx2, o_mix, y1, proj, cblk_re, cblk_im, v512, convw_full, glu_full, h16, h64, w_out_full,
        vd1 + ga_token[0:1, 0:1])
    gt_re, gt_im, ga_re8, ga_im8 = _scan_bwd(a_rows, g_re, g_im, s_re, s_im)
    grad_x, dprojb, vp_mi = _mix_in_bwd(gt_re, gt_im, bblk_re, bblk_im, dy1, dcc, dbg, proj, xs, dx2, vec1, v512,
                                        convw_full, w_in_st)
    ssm_u, ssm_s = d_ssm // SSM_SPLIT, gp // SSM_SPLIT
    d_bre = _matmul_tn(proj, gt_re, ssm_u, ssm_s, F32, "d_bre", diag=True, bt=2048)
    d_bim = _matmul_tn(proj, gt_im, ssm_u, ssm_s, F32, "d_bim", diag=True, bt=2048)
    d_cre = _matmul_tn(s_re, dy1, ssm_s, ssm_u, F32, "d_cre", diag=True, bt=2048)
    d_cim = _matmul_tn(s_im, dy1, ssm_s, ssm_u, F32, "d_cim", diag=True, bt=2048)
    d_bre, d_bim = d_bre.reshape(d_ssm, ssm_s), d_bim.reshape(d_ssm, ssm_s)
    d_cre, d_cim = d_cre.reshape(gp, ssm_u), d_cim.reshape(gp, ssm_u)

    fold_b = jnp.asarray(_fold_matrix(ssm_s, n_state), BF16)
    fold_c = jnp.asarray(_fold_matrix(ssm_u, n_gch), BF16)
    db_re_f, db_im_f, gc_rows = _ssm_bgrad(d_bre, d_bim, bt_re, bt_im, coef_rows, fold_b)
    dc_re_f, dc_im_f = _ssm_cgrad(d_cre, d_cim, fold_c)
    ga_sum = _ga_rowsum(ga_re8, ga_im8)
    g_lam_re, g_lam_im, g_log_step = _ssm_lamgrad(
        lam_re, lam_im, log_step, abar_re, abar_im, coef_re, coef_im,
        gc_rows[0].reshape(n_groups, n_state), gc_rows[1].reshape(n_groups, n_state),
        ga_sum[0].reshape(n_groups, n_state), ga_sum[1].reshape(n_groups, n_state))
    g_b_re = db_re_f.reshape(n_groups, n_gch, n_state).transpose(0, 2, 1)
    g_b_im = db_im_f.reshape(n_groups, n_gch, n_state).transpose(0, 2, 1)
    g_c_re = dc_re_f.reshape(n_groups, n_state, n_gch).transpose(0, 2, 1)
    g_c_im = dc_im_f.reshape(n_groups, n_state, n_gch).transpose(0, 2, 1)

    dmod = jnp.concatenate([vp_mi[0:1], vp_mi[1:2], vp_mo[0:1], vp_up[0:1], vp_up[1:2], vp_dn[0:1]], axis=1)
    small = [
        ("g_pre_mix", vp_mi[2:3]), ("g_post_mix", vp_mo[1:2]), ("g_pre_ffn", vp_up[2:3]), ("g_post_ffn", vp_dn[1:2]),
        ("ssm_lam_re", g_lam_re), ("ssm_lam_im", g_lam_im), ("ssm_log_step", g_log_step),
        ("ssm_b_re", g_b_re), ("ssm_b_im", g_b_im), ("ssm_c_re", g_c_re), ("ssm_c_im", g_c_im),
        ("ssm_d", vp5[3:4]), ("glu_b", vp5[2:3]), ("g_out_ssm", vp5[0:1]), ("g_out_conv", vp5[1:2]),
        ("conv_w", vp5[4:7]), ("ffn_conv_w", df_rows[0:3]), ("loss", loss_blk[0:1, 0:1]),
    ]
    packed, offsets, row = [], {}, 0
    for name, a in small:
        r = _to_rows(a, d)
        offsets[name] = (row, a.shape)
        packed.append(r)
        row += r.shape[0]
    n_small = -(-row // SUBLANES) * SUBLANES
    packed.append(jnp.zeros((n_small - row, d), F32))
    packed.append(pad8(dmod.reshape(6, d)))
    pack = jnp.concatenate(packed, axis=0)
    sm_send, sm_recv, sm_src, sm_land, sm_token = _chips_start("small_start", True, [pack], [_landing(pack, chip)])

    gw_out = _matmul_tn(ycatb, dob, 2 * d_ssm, d, BF16, "dw_out", bt=2048, after=sm_token)
    gw_out = gw_out.reshape(N_CHIPS, 2 * d_ssm // N_CHIPS, d)
    gw_glu = _matmul_tn(zb, dqb, d_ssm, d_ssm, BF16, "dw_glu", bt=2048).reshape(N_CHIPS, d_ssm // N_CHIPS, d_ssm)
    gw_in = _matmul_tn(h1b, dprojb, d, w_in.shape[2], BF16, "dw_in", bt=2048)
    gb_send, gb_recv, gb_src, gb_land, gb_token = _chips_start(
        "grads_start_mix", False, [gw_out, gw_glu, gw_in],
        [_landing(lax.dynamic_index_in_dim(g, chip, 0, False), chip) for g in (gw_out, gw_glu, gw_in)])

    (sm_landed,) = _chips_wait("small_wait", True, sm_send, sm_recv, sm_src, sm_land, gb_token)
    sm_part = _sum_blocks(sm_landed, "sum_small")
    dmod_mine = sm_landed[:, n_small:n_small + SUBLANES, :]
    sm_sib, dmod_sib = _swap_sibling([sm_part, dmod_mine], "swap_small")
    sums = _add2(sm_part, sm_sib)
    dmod_by_core = jnp.stack([dmod_mine, dmod_sib], axis=1)
    dmod_by_core = jnp.where(ci == 0, dmod_by_core, dmod_by_core[:, ::-1])
    dmod_all = dmod_by_core[:, :, :6, :].reshape(N_DEV, 6 * d)
    g_b_ada = sums[n_small:n_small + 6].reshape(1, 6 * d)

    def unpack(name):
        r0, shape = offsets[name]
        size = math.prod(shape)
        nrow = -(-size // d)
        return sums[r0:r0 + nrow].reshape(-1)[:size].reshape(shape)

    dmod_cols = lax.dynamic_slice(dmod_all, (0, chip * n_ada), (N_DEV, n_ada))
    ada = _adamw_ada(c_all, dmod_cols, w_ada[0], m_w_ada[0], v_w_ada[0])

    def finish(names, landed):
        partial = [_sum_blocks(s, "sum_" + nm) for s, nm in zip(landed, names)]
        theirs = _swap_sibling(partial, "swap_" + names[0])
        done = {}
        for nm, pm, ps in zip(names, partial, theirs):
            w_, m_, v_ = big_params[nm]
            done[nm] = _adamw_big(pm, ps, w_[0], m_[0], v_[0], "adamw_" + nm)
        return done

    big_params = {"w_down": (w_down, m_w_down, v_w_down), "w_up": (w_up, m_w_up, v_w_up),
                  "w_out": (w_out, m_w_out, v_w_out), "glu_w": (glu_w, m_glu_w, v_glu_w),
                  "w_in": (w_in, m_w_in, v_w_in)}
    big = finish(("w_down", "w_up"), _chips_wait("grads_wait_ffn", False, ga_send, ga_recv, ga_src, ga_land, ada[0]))
    big.update(finish(("w_out", "glu_w", "w_in"),
                      _chips_wait("grads_wait_mix", False, gb_send, gb_recv, gb_src, gb_land, big["w_up"][0])))

    g_small = {name: unpack(name) for name, _ in small}
    g_small["b_ada"] = g_b_ada
    g_small["conv_w"] = lax.dynamic_slice(g_small["conv_w"], (0, chip * conv_w.shape[2]), (3, conv_w.shape[2]))
    g_small["ffn_conv_w"] = lax.dynamic_slice(g_small["ffn_conv_w"], (0, chip * n_upc), (3, n_upc))
    g_small["ssm_log_step"] = g_small["ssm_log_step"].reshape(1, n_groups)
    small_params = {
        "b_ada": (b_ada, m_b_ada, v_b_ada), "g_pre_mix": (g_pre_mix, m_g_pre_mix, v_g_pre_mix),
        "g_post_mix": (g_post_mix, m_g_post_mix, v_g_post_mix), "ssm_lam_re": (ssm_lam_re, m_ssm_lam_re, v_ssm_lam_re),
        "ssm_lam_im": (ssm_lam_im, m_ssm_lam_im, v_ssm_lam_im),
        "ssm_log_step": (ssm_log_step, m_ssm_log_step, v_ssm_log_step),
        "ssm_b_re": (ssm_b_re, m_ssm_b_re, v_ssm_b_re), "ssm_b_im": (ssm_b_im, m_ssm_b_im, v_ssm_b_im),
        "ssm_c_re": (ssm_c_re, m_ssm_c_re, v_ssm_c_re), "ssm_c_im": (ssm_c_im, m_ssm_c_im, v_ssm_c_im),
        "ssm_d": (ssm_d, m_ssm_d, v_ssm_d), "glu_b": (glu_b, m_glu_b, v_glu_b),
        "g_out_ssm": (g_out_ssm, m_g_out_ssm, v_g_out_ssm), "conv_w": (conv_w, m_conv_w, v_conv_w),
        "g_out_conv": (g_out_conv, m_g_out_conv, v_g_out_conv), "g_pre_ffn": (g_pre_ffn, m_g_pre_ffn, v_g_pre_ffn),
        "g_post_ffn": (g_post_ffn, m_g_post_ffn, v_g_post_ffn),
        "ffn_conv_w": (ffn_conv_w, m_ffn_conv_w, v_ffn_conv_w),
    }

    def natural(a):
        return a[0] if a.ndim > 2 else a

    names = list(small_params)
    items = []
    for nm in names:
        w_, m_, v_ = small_params[nm]
        items.append((natural(w_), g_small[nm].reshape(natural(w_).shape), natural(m_), natural(v_)))
    upd = _adamw_small(items)
    small_out = {}
    for nm, (dl, mo, vo) in zip(names, upd):
        shp = small_params[nm][0].shape
        small_out[nm] = (g_small[nm].reshape(shp), dl.reshape(shp), mo.reshape(shp), vo.reshape(shp))

    loss = g_small["loss"][0, 0]

    order = ["w_ada", "b_ada", "g_pre_mix", "g_post_mix", "w_in", "ssm_lam_re", "ssm_lam_im", "ssm_log_step",
             "ssm_b_re", "ssm_b_im", "ssm_c_re", "ssm_c_im", "ssm_d", "glu_w", "glu_b", "g_out_ssm", "conv_w",
             "g_out_conv", "w_out", "g_pre_ffn", "g_post_ffn", "w_up", "ffn_conv_w", "w_down"]
    results = {"w_ada": tuple(a[None] for a in ada)}
    for nm in big:
        results[nm] = tuple(a[None] for a in big[nm])
    results.update(small_out)
    outs = [loss, grad_x[None]]
    for k in range(4):
        outs += [results[nm][k] for nm in order]
    return tuple(outs)


def _ga_rowsum(ga_re8, ga_im8):
    n = ga_re8.shape[1]

    def body(r_ref, i_ref, o_ref):
        o_ref[...] = jnp.zeros(o_ref.shape, F32)
        o_ref[0:1, :] = _colsum(r_ref[...])
        o_ref[1:2, :] = _colsum(i_ref[...])

    return pl.pallas_call(body, name="ga_rowsum", out_shape=jax.ShapeDtypeStruct((SUBLANES, n), F32))(ga_re8, ga_im8)
```

```python
import functools
import math

import jax
import jax.numpy as jnp
import numpy as np
from jax import lax
from jax.experimental import pallas as pl
from jax.experimental.pallas import tpu as pltpu

F32 = jnp.float32
BF16 = jnp.bfloat16
MESH = pl.DeviceIdType.MESH

EPS = 1e-6
LAMBDA_RE_MAX = -1e-4
ADAM_LR = 0.001
ADAM_B1 = 0.9
ADAM_B2 = 0.999
ADAM_EPS = 1e-08
ADAM_WD = 0.01
ADAM_STEP = 10

SUBLANES = 8
BF16_ROWS = 16
N_CHIPS = 4
N_DEV = 8
CONV_HEAD_DIM = 64
VMEM_BIG = 56 * 1024 * 1024
VMEM_MID = 40 * 1024 * 1024

TB_MIX = 256
TB_FFN = 256
TB_SCAN = 1024
W_SCAN = 256
SSM_SPLIT = 4
CW_FFN = 256
SCAN_UNROLL = 4
TB_TN = 512


def _cparams(sem=None, vmem=None):
    kw = {}
    if sem is not None:
        kw["dimension_semantics"] = sem
    if vmem is not None:
        kw["vmem_limit_bytes"] = vmem
    return pltpu.CompilerParams(**kw)


def _blk(t, pref):
    return pref if t % pref == 0 else t


def _dot(a, b):
    return jnp.dot(a.astype(BF16), b.astype(BF16), preferred_element_type=F32)


def _dot_nt(a, b):
    return lax.dot_general(a.astype(BF16), b.astype(BF16), (((1,), (1,)), ((), ())),
                           preferred_element_type=F32)


def _dot_tn(a, b):
    return lax.dot_general(a.astype(BF16), b.astype(BF16), (((0,), (0,)), ((), ())),
                           preferred_element_type=F32)


def _sigmoid(x):
    return 0.5 * jnp.tanh(0.5 * x) + 0.5


_GELU_K = math.sqrt(2.0 / math.pi)
_GELU_C = 0.044715


def _gelu(x):
    th = jnp.tanh(_GELU_K * (x + _GELU_C * x * x * x))
    return 0.5 * x * (1.0 + th)


def _gelu_grad(x):
    x2 = x * x
    th = jnp.tanh(_GELU_K * (x + _GELU_C * x2 * x))
    return 0.5 * (1.0 + th) + 0.5 * x * (1.0 - th * th) * _GELU_K * (1.0 + 3.0 * _GELU_C * x2)


def _rowmean(x):
    return jnp.mean(x, axis=-1, keepdims=True)


def _colsum(x):
    return jnp.sum(x, axis=0, keepdims=True)


def _split_dot(x, m):
    hi = x.astype(BF16)
    lo = (x - hi.astype(F32)).astype(BF16)
    return (jnp.dot(hi, m, preferred_element_type=F32) + jnp.dot(lo, m, preferred_element_type=F32))


def _split3_dot(x, m):
    hi = x.astype(BF16)
    r1 = x - hi.astype(F32)
    mid = r1.astype(BF16)
    lo = (r1 - mid.astype(F32)).astype(BF16)
    return (jnp.dot(hi, m, preferred_element_type=F32) + jnp.dot(mid, m, preferred_element_type=F32)
            + jnp.dot(lo, m, preferred_element_type=F32))


def _shift_down(x, halo, k):
    r = pltpu.roll(x, k, 0)
    row = lax.broadcasted_iota(jnp.int32, x.shape, 0)
    for j in range(k):
        r = jnp.where(row == j, halo[SUBLANES - k + j:SUBLANES - k + j + 1, :], r)
    return r


def _shift_up(x, halo, k):
    n = x.shape[0]
    r = pltpu.roll(x, n - k, 0)
    row = lax.broadcasted_iota(jnp.int32, x.shape, 0)
    for j in range(k):
        r = jnp.where(row == n - k + j, halo[j:j + 1, :], r)
    return r


def _acc_rows(ref, first, rows):
    @pl.when(first)
    def _():
        ref[...] = jnp.zeros(ref.shape, ref.dtype)
    for j, r in enumerate(rows):
        ref[j:j + 1, :] += r


def _rows(tb, c, col=0):
    return pl.BlockSpec((tb, c), lambda i, col=col: (i, col))


def _full(shape):
    nd = len(shape)
    return pl.BlockSpec(shape, lambda i, nd=nd: (0,) * nd)


def _resident(shape):
    nd = len(shape)
    return pl.BlockSpec(shape, lambda i, nd=nd: (0,) * nd, pipeline_mode=pl.Buffered(1))


def _halo_prev(tb, c, col=0):
    per = tb // SUBLANES
    return pl.BlockSpec((SUBLANES, c), lambda i, col=col: (jnp.maximum(i * per - 1, 0), col))


def _halo_next(tb, c, t, col=0, rows=SUBLANES):
    per = tb // rows
    last = t // rows - 1
    return pl.BlockSpec((rows, c), lambda i, col=col: (jnp.minimum((i + 1) * per, last), col))


def _mesh_pos():
    return lax.axis_index("x"), lax.axis_index("y"), lax.axis_index("c")


def _allgather8(x_pad, n_sum, name):
    m_per, n = x_pad.shape

    def body(x_ref, out_ref, sum_ref, send_sems, recv_sems, local_sem):
        x, y, c = _mesh_pos()
        me, sibling = (x, y, c), (x, y, 1 - c)
        chips = [(1 - x, y), (x, 1 - y), (1 - x, 1 - y)]

        def rows(px, py, pc):
            return out_ref.at[pl.ds((4 * px + 2 * py + pc) * m_per, m_per), :]

        def copy(k, block, to, src=None):
            return pltpu.make_async_remote_copy(
                src_ref=rows(*block) if src is None else src, dst_ref=rows(*block),
                send_sem=send_sems.at[k], recv_sem=recv_sems.at[k], device_id=to, device_id_type=MESH)

        mine = pltpu.make_async_copy(x_ref, rows(*me), local_sem)
        mine.start()
        first = [copy(0, me, sibling, src=x_ref)]
        first += [copy(1 + j, me, (*chip, c), src=x_ref) for j, chip in enumerate(chips)]
        for cp in first:
            cp.start()
        passed = [copy(4 + j, (*chip, c), sibling) for j, chip in enumerate(chips)]
        for j, chip in enumerate(chips):
            copy(1 + j, (*chip, c), me).wait_recv()
            passed[j].start()
        copy(0, sibling, me).wait_recv()
        for j, chip in enumerate(chips):
            copy(4 + j, (*chip, 1 - c), me).wait_recv()
        for cp in first + passed:
            cp.wait_send()
        mine.wait()
        acc = out_ref[0:n_sum, :]
        for k in range(1, N_DEV):
            acc = acc + out_ref[k * m_per:k * m_per + n_sum, :]
        sum_ref[...] = acc

    return pl.pallas_call(
        body, name=name,
        out_shape=(jax.ShapeDtypeStruct((N_DEV * m_per, n), F32), jax.ShapeDtypeStruct((n_sum, n), F32)),
        in_specs=[pl.BlockSpec(memory_space=pltpu.VMEM)],
        out_specs=(pl.BlockSpec(memory_space=pltpu.VMEM), pl.BlockSpec(memory_space=pltpu.VMEM)),
        scratch_shapes=[pltpu.SemaphoreType.DMA((7,)), pltpu.SemaphoreType.DMA((7,)), pltpu.SemaphoreType.DMA],
        compiler_params=_cparams(vmem=VMEM_MID),
    )(x_pad)


_HBM = pl.BlockSpec(memory_space=pltpu.HBM)
_SEM = pl.BlockSpec(memory_space=pltpu.SEMAPHORE)
_EFFECT = pltpu.SideEffectType.DATAFLOW_SIDE_EFFECTING


def _chip_copy(gather, src_ref, land_ref, send, recv, j, arrival):
    x, y, c = _mesh_pos()
    peer = [(1 - x, y), (x, 1 - y), (1 - x, 1 - y)][j]
    peer_chip = 2 * peer[0] + peer[1]
    my_chip = 2 * x + y
    return pltpu.make_async_remote_copy(
        src_ref=src_ref if gather else src_ref.at[peer_chip],
        dst_ref=land_ref.at[peer_chip if arrival else my_chip],
        send_sem=send.at[j], recv_sem=recv.at[j], device_id=(*peer, c), device_id_type=MESH)


def _chips_start(name, gather, srcs, lands, after=None):
    n = len(srcs)
    extra = [] if after is None else [after]

    def body(*refs):
        src_refs, land_refs = refs[:n], refs[n:2 * n]
        outs = refs[2 * n + len(extra):]
        sends, recvs, token = outs[:n], outs[n:2 * n], outs[-1]
        for k in range(n):
            for j in range(3):
                _chip_copy(gather, src_refs[k], land_refs[k], sends[k], recvs[k], j, False).start()
        token[...] = jnp.zeros(token.shape, F32)

    sem = pltpu.SemaphoreType.DMA((3,))
    thru = tuple(pltpu.HBM(a.shape, a.dtype) for a in list(srcs) + list(lands))
    res = pl.pallas_call(
        body, name=name,
        out_shape=(sem,) * (2 * n) + thru + (jax.ShapeDtypeStruct((SUBLANES, 128), F32),),
        in_specs=[_HBM] * (2 * n) + [pl.BlockSpec(memory_space=pl.ANY)] * len(extra),
        out_specs=(_SEM,) * (2 * n) + (_HBM,) * (2 * n) + (pl.BlockSpec(memory_space=pltpu.VMEM),),
        input_output_aliases={k: 2 * n + k for k in range(2 * n)},
        compiler_params=pltpu.CompilerParams(has_side_effects=_EFFECT),
    )(*[pltpu.with_memory_space_constraint(a, pltpu.HBM) for a in list(srcs) + list(lands)], *extra)
    return res[:n], res[n:2 * n], res[2 * n:3 * n], res[3 * n:4 * n], res[-1]


def _chips_wait(name, gather, sends, recvs, srcs, lands, after):
    n = len(srcs)

    def body(*refs):
        src_refs, land_refs = refs[:n], refs[n:2 * n]
        sends_, recvs_ = refs[2 * n:3 * n], refs[3 * n:4 * n]
        for k in range(n):
            for j in range(3):
                cp = _chip_copy(gather, src_refs[k], land_refs[k], sends_[k], recvs_[k], j, True)
                cp.wait_send()
                cp.wait_recv()

    thru = tuple(pltpu.HBM(a.shape, a.dtype) for a in list(srcs) + list(lands))
    res = pl.pallas_call(
        body, name=name, out_shape=thru,
        in_specs=[_HBM] * (2 * n) + [_SEM] * (2 * n) + [pl.BlockSpec(memory_space=pl.ANY)],
        out_specs=(_HBM,) * (2 * n),
        input_output_aliases={k: k for k in range(2 * n)},
        compiler_params=pltpu.CompilerParams(has_side_effects=_EFFECT),
    )(*srcs, *lands, *sends, *recvs, after)
    return res[n:]


def _landing(own, chip):
    zone = lax.empty((N_CHIPS,) + own.shape, own.dtype)
    return lax.dynamic_update_slice(zone, own[None], (chip,) + (0,) * own.ndim)


def _swap_sibling(arrs, name):
    n_arr = len(arrs)

    def body(*refs):
        ins, outs = refs[:n_arr], refs[n_arr:2 * n_arr]
        send_sems, recv_sems = refs[2 * n_arr:]
        x, y, c = _mesh_pos()
        copies = [pltpu.make_async_remote_copy(
            src_ref=ins[n], dst_ref=outs[n], send_sem=send_sems.at[n], recv_sem=recv_sems.at[n],
            device_id=(x, y, 1 - c), device_id_type=MESH) for n in range(n_arr)]
        for cp in copies:
            cp.start()
        for cp in copies:
            cp.wait()

    any_spec = pl.BlockSpec(memory_space=pl.ANY)
    return pl.pallas_call(
        body, name=name,
        out_shape=tuple(jax.ShapeDtypeStruct(a.shape, a.dtype) for a in arrs),
        in_specs=[any_spec] * n_arr, out_specs=tuple([any_spec] * n_arr),
        scratch_shapes=[pltpu.SemaphoreType.DMA((n_arr,)), pltpu.SemaphoreType.DMA((n_arr,))],
    )(*arrs)


def _mod_shard(c_all, w_ada_sh, b_sh):
    d, n = w_ada_sh.shape
    bn = 512

    def body(c_ref, w_ref, b_ref, o_ref):
        cc = c_ref[...]
        ca = cc * _sigmoid(cc)
        o_ref[...] = _dot(ca, w_ref[...]) + b_ref[...]

    return pl.pallas_call(
        body, name="mod_shard", grid=(n // bn,),
        out_shape=jax.ShapeDtypeStruct((N_DEV, n), F32),
        in_specs=[_full((N_DEV, d)), pl.BlockSpec((d, bn), lambda j: (0, j)), pl.BlockSpec((1, bn), lambda j: (0, j))],
        out_specs=pl.BlockSpec((N_DEV, bn), lambda j: (0, j)),
        compiler_params=_cparams(("parallel",)),
    )(c_all, w_ada_sh, b_sh)


def _ssm_prep(lam_re, lam_im, log_step):
    g, p = lam_re.shape

    def body(lr_ref, li_ref, ls_ref, ar_ref, ai_ref, cr_ref, ci_ref):
        lr = jnp.minimum(lr_ref[...], LAMBDA_RE_MAX)
        li = li_ref[...]
        st = jnp.exp(ls_ref[...])
        mag = jnp.exp(lr * st)
        ar = mag * jnp.cos(li * st)
        ai = mag * jnp.sin(li * st)
        den = lr * lr + li * li
        nr = ar - 1.0
        ar_ref[...] = ar
        ai_ref[...] = ai
        cr_ref[...] = (nr * lr + ai * li) / den
        ci_ref[...] = (ai * lr - nr * li) / den

    sds = jax.ShapeDtypeStruct((g, p), F32)
    return pl.pallas_call(body, name="ssm_prep", out_shape=(sds,) * 4)(lam_re, lam_im, log_step)


def _ssm_blocks(bt_re, bt_im, ct_re, ct_im, coef_rows):
    gh, gp = bt_re.shape
    nb = 4
    cb, rb = gp // nb, gp // nb

    def body(btr, bti, ctr, cti, cf, bre_o, bim_o, cre_o, cim_o):
        j = pl.program_id(0)
        row = lax.broadcasted_iota(jnp.int32, (gh, cb), 0)
        col = lax.broadcasted_iota(jnp.int32, (gh, cb), 1) + j * cb
        mask = (row >> 4) == (col >> 6)
        cr, ci = cf[0:1, :], cf[1:2, :]
        br, bi = btr[...], bti[...]
        bre_o[...] = jnp.where(mask, br * cr - bi * ci, 0.0).astype(BF16)
        bim_o[...] = jnp.where(mask, br * ci + bi * cr, 0.0).astype(BF16)
        row2 = lax.broadcasted_iota(jnp.int32, (rb, gh), 0) + j * rb
        col2 = lax.broadcasted_iota(jnp.int32, (rb, gh), 1)
        mask2 = (row2 >> 6) == (col2 >> 4)
        cre_o[...] = jnp.where(mask2, ctr[...], 0.0).astype(BF16)
        cim_o[...] = jnp.where(mask2, cti[...], 0.0).astype(BF16)

    bspec = pl.BlockSpec((gh, cb), lambda j: (0, j))
    cspec = pl.BlockSpec((rb, gh), lambda j: (j, 0))
    return pl.pallas_call(
        body, name="ssm_blocks", grid=(nb,),
        out_shape=(jax.ShapeDtypeStruct((gh, gp), BF16),) * 2 + (jax.ShapeDtypeStruct((gp, gh), BF16),) * 2,
        in_specs=[bspec, bspec, cspec, cspec, pl.BlockSpec((SUBLANES, cb), lambda j: (0, j))],
        out_specs=(bspec, bspec, cspec, cspec),
        compiler_params=_cparams(("parallel",)),
    )(bt_re, bt_im, ct_re, ct_im, coef_rows)


def _scan_consts(a_ref, reverse):
    w = a_ref.shape[1]
    ar1 = a_ref[0:1, :]
    ai1 = a_ref[1:2, :]
    if reverse:
        ai1 = -ai1
    pr, pi = [ar1], [ai1]
    for _ in range(1, SUBLANES):
        nr = pr[-1] * ar1 - pi[-1] * ai1
        ni = pr[-1] * ai1 + pi[-1] * ar1
        pr.append(nr)
        pi.append(ni)
    row = lax.broadcasted_iota(jnp.int32, (SUBLANES, w), 0)
    dist = (SUBLANES - 1 - row) if reverse else row

    def pick(vals):
        out = jnp.broadcast_to(vals[SUBLANES - 1], (SUBLANES, w))
        for r in range(SUBLANES - 1):
            out = jnp.where(dist == r, vals[r], out)
        return out

    p_r, p_i = pick(pr), pick(pi)
    steps = []
    for k in (1, 2, 4):
        steps.append((k, jnp.where(dist >= k, pr[k - 1], 0.0), jnp.where(dist >= k, pi[k - 1], 0.0)))
    a8 = (jnp.broadcast_to(pr[SUBLANES - 1], (SUBLANES, w)), jnp.broadcast_to(pi[SUBLANES - 1], (SUBLANES, w)))
    return row, p_r, p_i, steps, a8


def _scan_tile(xr, xi, cr, ci, consts, reverse):
    row, p_r, p_i, steps, (a8r, a8i) = consts
    for k, s_r, s_i in steps:
        sh = (SUBLANES - k) if reverse else k
        qr = pltpu.roll(xr, sh, 0)
        qi = pltpu.roll(xi, sh, 0)
        xr, xi = xr + s_r * qr - s_i * qi, xi + s_r * qi + s_i * qr
    outr = xr + p_r * cr - p_i * ci
    outi = xi + p_r * ci + p_i * cr
    e = 0 if reverse else SUBLANES - 1
    er = jnp.broadcast_to(xr[e:e + 1, :], xr.shape)
    ei = jnp.broadcast_to(xi[e:e + 1, :], xi.shape)
    return outr, outi, er + a8r * cr - a8i * ci, ei + a8r * ci + a8i * cr


def _scan_fwd(a_rows, bu_re, bu_im):
    t, n = bu_re.shape
    tb, w = _blk(t, TB_SCAN), W_SCAN
    ntile = tb // SUBLANES

    def body(a_ref, br_ref, bi_ref, sr_ref, si_ref, car, cai):
        @pl.when(pl.program_id(1) == 0)
        def _():
            car[...] = jnp.zeros(car.shape, F32)
            cai[...] = jnp.zeros(cai.shape, F32)
        consts = _scan_consts(a_ref, False)

        def pair(i, carry):
            o = pl.multiple_of(i * BF16_ROWS, BF16_ROWS)
            outs = []
            for h in range(2):
                rows = pl.ds(o + h * SUBLANES, SUBLANES)
                outr, outi, ncr, nci = _scan_tile(br_ref[rows, :], bi_ref[rows, :], carry[0], carry[1], consts, False)
                outs.append((outr, outi))
                carry = (ncr, nci)
            sr_ref[pl.ds(o, BF16_ROWS), :] = jnp.concatenate([outs[0][0], outs[1][0]], axis=0).astype(BF16)
            si_ref[pl.ds(o, BF16_ROWS), :] = jnp.concatenate([outs[0][1], outs[1][1]], axis=0).astype(BF16)
            return carry

        def pairs(i, carry):
            for s in range(SCAN_UNROLL // 2):
                carry = pair(i * (SCAN_UNROLL // 2) + s, carry)
            return carry

        cr, ci = lax.fori_loop(0, ntile // SCAN_UNROLL, pairs, (car[...], cai[...]))
        car[...] = cr
        cai[...] = ci

    spec = pl.BlockSpec((tb, w), lambda s, k: (k, s))
    sds = jax.ShapeDtypeStruct((t, n), BF16)
    return pl.pallas_call(
        body, name="scan_fwd", grid=(n // w, t // tb), out_shape=(sds, sds),
        in_specs=[pl.BlockSpec((SUBLANES, w), lambda s, k: (0, s)), spec, spec], out_specs=(spec, spec),
        scratch_shapes=[pltpu.VMEM((SUBLANES, w), F32), pltpu.VMEM((SUBLANES, w), F32)],
        compiler_params=_cparams(("parallel", "arbitrary"), VMEM_MID),
    )(a_rows, bu_re, bu_im)


def _scan_bwd(a_rows, g_re, g_im, s_re, s_im):
    t, n = g_re.shape
    tb, w = _blk(t, TB_SCAN), W_SCAN
    ntile = tb // SUBLANES
    npair = tb // BF16_ROWS
    nt = t // tb

    def body(a_ref, gr_ref, gi_ref, sr_ref, si_ref, or_ref, oi_ref, gar_ref, gai_ref, car, cai):
        @pl.when(pl.program_id(1) == 0)
        def _():
            car[...] = jnp.zeros(car.shape, F32)
            cai[...] = jnp.zeros(cai.shape, F32)
            gar_ref[...] = jnp.zeros(gar_ref.shape, F32)
            gai_ref[...] = jnp.zeros(gai_ref.shape, F32)
        consts = _scan_consts(a_ref, True)
        row = consts[0]

        def pair(i, carry):
            cr, ci, accr, acci = carry
            o = pl.multiple_of((npair - 1 - i) * BF16_ROWS, BF16_ROWS)
            s_r = sr_ref[pl.ds(o, BF16_ROWS), :].astype(F32)
            s_i = si_ref[pl.ds(o, BF16_ROWS), :].astype(F32)
            outs = [None, None]
            for h in (1, 0):
                rows = pl.ds(o + h * SUBLANES, SUBLANES)
                outr, outi, ncr, nci = _scan_tile(gr_ref[rows, :], gi_ref[rows, :], cr, ci, consts, True)
                outs[h] = (outr, outi)
                gnr = jnp.where(row == SUBLANES - 1, cr, pltpu.roll(outr, SUBLANES - 1, 0))
                gni = jnp.where(row == SUBLANES - 1, ci, pltpu.roll(outi, SUBLANES - 1, 0))
                sr = s_r[h * SUBLANES:(h + 1) * SUBLANES, :]
                si = s_i[h * SUBLANES:(h + 1) * SUBLANES, :]
                accr, acci = accr + sr * gnr + si * gni, acci + sr * gni - si * gnr
                cr, ci = ncr, nci
            or_ref[pl.ds(o, BF16_ROWS), :] = jnp.concatenate([outs[0][0], outs[1][0]], axis=0).astype(BF16)
            oi_ref[pl.ds(o, BF16_ROWS), :] = jnp.concatenate([outs[0][1], outs[1][1]], axis=0).astype(BF16)
            return cr, ci, accr, acci

        def pairs(i, carry):
            for s in range(SCAN_UNROLL // 2):
                carry = pair(i * (SCAN_UNROLL // 2) + s, carry)
            return carry

        cr, ci, accr, acci = lax.fori_loop(0, ntile // SCAN_UNROLL, pairs,
                                           (car[...], cai[...], gar_ref[...], gai_ref[...]))
        car[...] = cr
        cai[...] = ci
        gar_ref[...] = accr
        gai_ref[...] = acci

    spec = pl.BlockSpec((tb, w), lambda s, k: (nt - 1 - k, s))
    aspec = pl.BlockSpec((SUBLANES, w), lambda s, k: (0, s))
    sds = jax.ShapeDtypeStruct((t, n), BF16)
    asds = jax.ShapeDtypeStruct((SUBLANES, n), F32)
    return pl.pallas_call(
        body, name="scan_bwd", grid=(n // w, nt), out_shape=(sds, sds, asds, asds),
        in_specs=[aspec, spec, spec, spec, spec], out_specs=(spec, spec, aspec, aspec),
        scratch_shapes=[pltpu.VMEM((SUBLANES, w), F32), pltpu.VMEM((SUBLANES, w), F32)],
        compiler_params=_cparams(("parallel", "arbitrary"), VMEM_MID),
    )(a_rows, g_re, g_im, s_re, s_im)


def _mix_in(x, vec, w_in_st, b_re, b_im):
    t, d = x.shape
    ns, _, nc = w_in_st.shape
    dssm, nstate = b_re.shape
    du, ds = dssm // SSM_SPLIT, nstate // SSM_SPLIT
    tb = _blk(t, TB_MIX)

    def body(x_ref, vec_ref, w_ref, bre_ref, bim_ref, proj_ref, bur_ref, bui_ref, h1_ref):
        xv = x_ref[...]
        r = lax.rsqrt(_rowmean(xv * xv) + EPS)
        h = xv * r * vec_ref[0:1, :] * vec_ref[1:2, :] + vec_ref[2:3, :]
        hb = h.astype(BF16)
        h1_ref[...] = hb
        u = None
        for j in range(ns):
            pj = jnp.dot(hb, w_ref[j], preferred_element_type=F32)
            proj_ref[:, j * nc:(j + 1) * nc] = pj
            if j == 0:
                u = pj
        ub = u.astype(BF16)
        for q in range(SSM_SPLIT):
            rq, cq = slice(q * du, (q + 1) * du), slice(q * ds, (q + 1) * ds)
            bur_ref[:, cq] = jnp.dot(ub[:, rq], bre_ref[rq, cq], preferred_element_type=F32)
            bui_ref[:, cq] = jnp.dot(ub[:, rq], bim_ref[rq, cq], preferred_element_type=F32)

    return pl.pallas_call(
        body, name="mix_in", grid=(t // tb,),
        out_shape=(jax.ShapeDtypeStruct((t, ns * nc), F32), jax.ShapeDtypeStruct((t, nstate), F32),
                   jax.ShapeDtypeStruct((t, nstate), F32), jax.ShapeDtypeStruct((t, d), BF16)),
        in_specs=[_rows(tb, d), _full((SUBLANES, d)), _resident(w_in_st.shape), _resident(b_re.shape),
                  _resident(b_im.shape)],
        out_specs=(_rows(tb, ns * nc), _rows(tb, nstate), _rows(tb, nstate), _rows(tb, d)),
        compiler_params=_cparams(("parallel",), VMEM_BIG),
    )(x, vec, w_in_st, b_re, b_im)


def _head_ms(y, h_ref):
    return _split_dot(y * y, h_ref[...])


def _conv3(x, halo, w_ref):
    return w_ref[0:1, :] * _shift_down(x, halo, 2) + w_ref[1:2, :] * _shift_down(x, halo, 1) + w_ref[2:3, :] * x


def _mix_out(x, proj, s_re, s_im, c_re, c_im, v512, convw, glu_w, h16, h64, w_out, vd):
    t, d = x.shape
    dh = c_re.shape[1]
    nstate = s_re.shape[1]
    du, ds = dh // SSM_SPLIT, nstate // SSM_SPLIT
    tb = _blk(t, TB_MIX)

    def body(x_ref, u_ref, bg_ref, cg_ref, v_ref, cgh_ref, vh_ref, sr_ref, si_ref, cre_ref, cim_ref, p_ref,
             cw_ref, gw_ref, h16_ref, h64_ref, wo_ref, vd_ref, y1_ref, o_ref, x2_ref):
        i = pl.program_id(0)
        u = u_ref[...]
        ys = []
        for q in range(SSM_SPLIT):
            rq, cq = slice(q * ds, (q + 1) * ds), slice(q * du, (q + 1) * du)
            ys.append(_dot(sr_ref[:, rq], cre_ref[rq, cq]) - _dot(si_ref[:, rq], cim_ref[rq, cq]))
        ys = jnp.concatenate(ys, axis=1)
        y1 = ys + p_ref[0:1, :] * u
        y1_ref[...] = y1
        z = _gelu(y1)
        q = _dot(z, gw_ref[...]) + p_ref[1:2, :]
        ya = z * _sigmoid(q)
        na = ya * lax.rsqrt(_head_ms(ya, h16_ref) + EPS) * p_ref[2:3, :]
        cv = cg_ref[...] * v_ref[...]
        cvh = jnp.where(i > 0, cgh_ref[...] * vh_ref[...], 0.0)
        yb = bg_ref[...] * _conv3(cv, cvh, cw_ref)
        nb = yb * lax.rsqrt(_head_ms(yb, h64_ref) + EPS) * p_ref[3:4, :]
        o = _dot(na, wo_ref[0:dh, :]) + _dot(nb, wo_ref[dh:2 * dh, :])
        o_ref[...] = o
        on = o * lax.rsqrt(_rowmean(o * o) + EPS) * vd_ref[0:1, :]
        x2_ref[...] = x_ref[...] + vd_ref[1:2, :] * on

    return pl.pallas_call(
        body, name="mix_out", grid=(t // tb,),
        out_shape=(jax.ShapeDtypeStruct((t, dh), F32), jax.ShapeDtypeStruct((t, d), F32),
                   jax.ShapeDtypeStruct((t, d), F32)),
        in_specs=[_rows(tb, d), _rows(tb, dh, 0), _rows(tb, dh, 1), _rows(tb, dh, 2), _rows(tb, dh, 3),
                  _halo_prev(tb, dh, 2), _halo_prev(tb, dh, 3), _rows(tb, nstate), _rows(tb, nstate),
                  _full(c_re.shape), _full(c_im.shape), _full(v512.shape), _full(convw.shape), _full(glu_w.shape),
                  _full(h16.shape), _full(h64.shape), _full(w_out.shape), _full(vd.shape)],
        out_specs=(_rows(tb, dh), _rows(tb, d), _rows(tb, d)),
        compiler_params=_cparams(("parallel",), VMEM_BIG),
    )(x, proj, proj, proj, proj, proj, proj, s_re, s_im, c_re, c_im, v512, convw, glu_w, h16, h64, w_out, vd)


def _ffn_up(x2, vec, w_up_st):
    t, d = x2.shape
    ns, _, nc = w_up_st.shape
    tb = _blk(t, TB_FFN)

    def body(x_ref, vec_ref, w_ref, up_ref, h2_ref):
        xv = x_ref[...]
        r = lax.rsqrt(_rowmean(xv * xv) + EPS)
        h = xv * r * vec_ref[0:1, :] * vec_ref[1:2, :] + vec_ref[2:3, :]
        hb = h.astype(BF16)
        h2_ref[...] = hb
        for j in range(ns):
            up_ref[:, j * nc:(j + 1) * nc] = jnp.dot(hb, w_ref[j], preferred_element_type=F32)

    return pl.pallas_call(
        body, name="ffn_up", grid=(t // tb,),
        out_shape=(jax.ShapeDtypeStruct((t, ns * nc), F32), jax.ShapeDtypeStruct((t, d), BF16)),
        in_specs=[_rows(tb, d), _full((SUBLANES, d)), _resident(w_up_st.shape)],
        out_specs=(_rows(tb, ns * nc), _rows(tb, d)),
        compiler_params=_cparams(("parallel",), VMEM_BIG),
    )(x2, vec, w_up_st)


def _ffn_down(up, fw, w_down, w_down_t, x2, tgt, vd):
    t, nh = up.shape
    dff, d = w_down.shape
    tb = _blk(t, TB_FFN)
    inv_d = 1.0 / d

    def body(up_ref, uph_ref, fw_ref, wd_ref, wdt_ref, x2_ref, tgt_ref, vd_ref,
             act_ref, ddn_ref, dout_ref, dhid_ref, vec_ref, loss_ref, a_s, vv_s, sg_s):
        i = pl.program_id(0)

        def conv_cols(sl):
            x = up_ref[:, sl]
            halo = jnp.where(i > 0, uph_ref[:, sl], 0.0)
            return (fw_ref[0:1, sl] * _shift_down(x, halo, 2) + fw_ref[1:2, sl] * _shift_down(x, halo, 1)
                    + fw_ref[2:3, sl] * x)

        dn = None
        for o in range(0, dff, CW_FFN):
            sl = slice(o, o + CW_FFN)
            a = conv_cols(sl)
            vv = conv_cols(slice(dff + o, dff + o + CW_FFN))
            sg = _sigmoid(a)
            a_s[:, sl] = a
            vv_s[:, sl] = vv
            sg_s[:, sl] = sg
            actb = (a * sg * vv).astype(BF16)
            act_ref[:, sl] = actb
            pj = lax.dot_general(actb, wdt_ref[:, sl], (((1,), (1,)), ((), ())), preferred_element_type=F32)
            dn = pj if dn is None else dn + pj
        r3 = lax.rsqrt(_rowmean(dn * dn) + EPS)
        xn = dn * r3
        g = vd_ref[0:1, :]
        gt2 = vd_ref[1:2, :]
        dnn = xn * g
        diff = x2_ref[...] + gt2 * dnn - tgt_ref[...]
        part = 0.5 * inv_d * jnp.sum(diff * diff)

        @pl.when(i == 0)
        def _():
            loss_ref[...] = jnp.zeros(loss_ref.shape, F32)
        loss_ref[...] += part
        dout = diff * inv_d
        dout_ref[...] = dout
        ddnn = dout * gt2
        _acc_rows(vec_ref, i == 0, [_colsum(dout * dnn), _colsum(ddnn * xn)])
        dxn = ddnn * g
        ddn = r3 * (dxn - xn * _rowmean(dxn * xn))
        ddnb = ddn.astype(BF16)
        ddn_ref[...] = ddnb
        for o in range(0, dff, CW_FFN):
            sl = slice(o, o + CW_FFN)
            dact = lax.dot_general(ddnb, wd_ref[sl, :], (((1,), (1,)), ((), ())), preferred_element_type=F32)
            a, vv, sg = a_s[:, sl], vv_s[:, sl], sg_s[:, sl]
            dhid_ref[:, sl] = (dact * vv * sg * (1.0 + a * (1.0 - sg))).astype(BF16)
            dhid_ref[:, dff + o:dff + o + CW_FFN] = (dact * (a * sg)).astype(BF16)

    return pl.pallas_call(
        body, name="ffn_down", grid=(t // tb,),
        scratch_shapes=[pltpu.VMEM((tb, dff), F32)] * 3,
        out_shape=(jax.ShapeDtypeStruct((t, dff), BF16), jax.ShapeDtypeStruct((t, d), BF16),
                   jax.ShapeDtypeStruct((t, d), F32), jax.ShapeDtypeStruct((t, nh), BF16),
                   jax.ShapeDtypeStruct((SUBLANES, d), F32), jax.ShapeDtypeStruct((SUBLANES, 128), F32)),
        in_specs=[_rows(tb, nh), _halo_prev(tb, nh), _full(fw.shape), _resident(w_down.shape),
                  _resident(w_down_t.shape), _rows(tb, d),
                  _rows(tb, d), _full(vd.shape)],
        out_specs=(_rows(tb, dff), _rows(tb, d), _rows(tb, d), _rows(tb, nh), _full((SUBLANES, d)),
                   _full((SUBLANES, 128))),
        compiler_params=_cparams(("arbitrary",), VMEM_BIG),
    )(up, up, fw, w_down, w_down_t, x2, tgt, vd)


def _ffn_up_bwd(dhid, up, fw, x2, dout, vec, w_up_st):
    t, nh = dhid.shape
    d = x2.shape[1]
    ns, _, nc = w_up_st.shape
    tb = _blk(t, TB_FFN)
    nblk = t // tb
    cw = 128

    def body(dh_ref, dhn_ref, up_ref, fw_ref, x2_ref, dout_ref, vec_ref, w_ref,
             dx2_ref, dup_ref, vp_ref, df_ref):
        i = pl.program_id(0)

        @pl.when(i == 0)
        def _():
            df_ref[...] = jnp.zeros(df_ref.shape, F32)
        dh2 = None
        for j in range(ns):
            for o in range(j * nc, (j + 1) * nc, cw):
                sl = slice(o, o + cw)
                dh = dh_ref[:, sl].astype(F32)
                dhn = jnp.where(i < nblk - 1, dhn_ref[:, sl].astype(F32), 0.0)
                dh1 = _shift_up(dh, dhn, 1)
                dh2s = _shift_up(dh, dhn, 2)
                dup_ref[:, sl] = (fw_ref[2:3, sl] * dh + fw_ref[1:2, sl] * dh1 + fw_ref[0:1, sl] * dh2s).astype(BF16)
                up_v = up_ref[:, sl]
                df_ref[0:1, sl] += _colsum(dh2s * up_v)
                df_ref[1:2, sl] += _colsum(dh1 * up_v)
                df_ref[2:3, sl] += _colsum(dh * up_v)
            pj = lax.dot_general(dup_ref[:, j * nc:(j + 1) * nc], w_ref[j], (((1,), (1,)), ((), ())),
                                 preferred_element_type=F32)
            dh2 = pj if dh2 is None else dh2 + pj
        xv = x2_ref[...]
        r = lax.rsqrt(_rowmean(xv * xv) + EPS)
        xn = xv * r
        g = vec_ref[0:1, :]
        hg = xn * g
        dhg = dh2 * vec_ref[1:2, :]
        _acc_rows(vp_ref, i == 0, [_colsum(dh2), _colsum(dh2 * hg), _colsum(dhg * xn)])
        dxn = dhg * g
        dx2_ref[...] = dout_ref[...] + r * (dxn - xn * _rowmean(dxn * xn))

    return pl.pallas_call(
        body, name="ffn_up_bwd", grid=(nblk,),
        out_shape=(jax.ShapeDtypeStruct((t, d), F32), jax.ShapeDtypeStruct((t, nh), BF16),
                   jax.ShapeDtypeStruct((SUBLANES, d), F32), jax.ShapeDtypeStruct((SUBLANES, nh), F32)),
        in_specs=[_rows(tb, nh), _halo_next(tb, nh, t, rows=BF16_ROWS), _rows(tb, nh), _full(fw.shape),
                  _rows(tb, d), _rows(tb, d), _full(vec.shape), _resident(w_up_st.shape)],
        out_specs=(_rows(tb, d), _rows(tb, nh), _full((SUBLANES, d)), _full((SUBLANES, nh))),
        compiler_params=_cparams(("arbitrary",), VMEM_BIG),
    )(dhid, dhid, up, fw, x2, dout, vec, w_up_st)


def _mix_out_bwd(dx2, o, y1, proj, s_re, s_im, c_re, c_im, v512, convw, glu_w, h16, h64, w_out, vd):
    t, d = dx2.shape
    dh = y1.shape[1]
    nstate = c_re.shape[0]
    du, ds = dh // SSM_SPLIT, nstate // SSM_SPLIT
    tb = _blk(t, TB_MIX)

    def body(dx2_ref, o_ref, y1_ref, u_ref, bg_ref, cg_ref, v_ref, cgh_ref, vh_ref, cre_ref, cim_ref, p_ref,
             cw_ref, gw_ref, h16_ref, h64_ref, wo_ref, vd_ref, sr_ref, si_ref,
             do_ref, ycat_ref, z_ref, dq_ref, dy1_ref, gr_ref, gi_ref, dcc_ref, dbg_ref, vpd_ref, vp5_ref,
             dcr_ref, dci_ref):
        i = pl.program_id(0)
        first = i == 0

        @pl.when(first)
        def _():
            dcr_ref[...] = jnp.zeros(dcr_ref.shape, F32)
            dci_ref[...] = jnp.zeros(dci_ref.shape, F32)
        ov = o_ref[...]
        ro = lax.rsqrt(_rowmean(ov * ov) + EPS)
        on_ = ov * ro
        g = vd_ref[0:1, :]
        dx2v = dx2_ref[...]
        don = dx2v * vd_ref[1:2, :]
        _acc_rows(vpd_ref, first, [_colsum(dx2v * on_ * g), _colsum(don * on_)])
        dxn = don * g
        dob = (ro * (dxn - on_ * _rowmean(dxn * on_))).astype(BF16)
        do_ref[...] = dob
        dyc_a =lax.dot_general(dob, wo_ref[0:dh, :], (((1,), (1,)), ((), ())), preferred_element_type=F32)
        dyc_b = lax.dot_general(dob, wo_ref[dh:2 * dh, :], (((1,), (1,)), ((), ())), preferred_element_type=F32)
        y1v = y1_ref[...]
        u = u_ref[...]
        z = _gelu(y1v)
        zb = z.astype(BF16)
        sg = _sigmoid(jnp.dot(zb, gw_ref[...], preferred_element_type=F32) + p_ref[1:2, :])
        ya = z * sg
        ra = lax.rsqrt(_head_ms(ya, h16_ref) + EPS)
        yan = ya * ra
        ga = p_ref[2:3, :]
        ycat_ref[:, 0:dh] = (yan * ga).astype(BF16)
        dyn = dyc_a * ga
        dya = ra * (dyn - yan * _split_dot(dyn * yan, h16_ref[...]))
        dq = dya * z * sg * (1.0 - sg)
        dqb = dq.astype(BF16)
        z_ref[...] = zb
        dq_ref[...] = dqb
        dz = dya * sg + lax.dot_general(dqb, gw_ref[...], (((1,), (1,)), ((), ())), preferred_element_type=F32)
        dy1 = dz * _gelu_grad(y1v)
        dy1_ref[...] = dy1
        dy1b = dy1.astype(BF16)
        for q in range(SSM_SPLIT):
            rq, cq = slice(q * ds, (q + 1) * ds), slice(q * du, (q + 1) * du)
            gr_ref[:, rq] = lax.dot_general(dy1b[:, cq], cre_ref[rq, cq], (((1,), (1,)), ((), ())),
                                            preferred_element_type=F32)
            gi_ref[:, rq] = -lax.dot_general(dy1b[:, cq], cim_ref[rq, cq], (((1,), (1,)), ((), ())),
                                             preferred_element_type=F32)
            dcr_ref[rq, :] += _dot_tn(sr_ref[:, rq], dy1b[:, cq])
            dci_ref[rq, :] += _dot_tn(si_ref[:, rq], dy1b[:, cq])
        bg = bg_ref[...]
        cv = cg_ref[...] * v_ref[...]
        cvh = jnp.where(i > 0, cgh_ref[...] * vh_ref[...], 0.0)
        cv1 = _shift_down(cv, cvh, 1)
        cv2 = _shift_down(cv, cvh, 2)
        cc = cw_ref[0:1, :] * cv2 + cw_ref[1:2, :] * cv1 + cw_ref[2:3, :] * cv
        yb = bg * cc
        rb = lax.rsqrt(_head_ms(yb, h64_ref) + EPS)
        ybn = yb * rb
        gb = p_ref[3:4, :]
        ycat_ref[:, dh:2 * dh] = (ybn * gb).astype(BF16)
        dynb = dyc_b * gb
        dyb = rb * (dynb - ybn * _split_dot(dynb * ybn, h64_ref[...]))
        dcc = dyb * bg
        dbg_ref[...] = dyb * cc
        dcc_ref[...] = dcc
        _acc_rows(vp5_ref, first, [_colsum(dyc_a * yan), _colsum(dyc_b * ybn), _colsum(dq), _colsum(dy1 * u),
                                   _colsum(dcc * cv2), _colsum(dcc * cv1), _colsum(dcc * cv)])

    return pl.pallas_call(
        body, name="mix_out_bwd", grid=(t // tb,),
        out_shape=(jax.ShapeDtypeStruct((t, d), BF16), jax.ShapeDtypeStruct((t, 2 * dh), BF16),
                   jax.ShapeDtypeStruct((t, dh), BF16), jax.ShapeDtypeStruct((t, dh), BF16),
                   jax.ShapeDtypeStruct((t, dh), F32), jax.ShapeDtypeStruct((t, nstate), F32),
                   jax.ShapeDtypeStruct((t, nstate), F32), jax.ShapeDtypeStruct((t, dh), F32),
                   jax.ShapeDtypeStruct((t, dh), F32), jax.ShapeDtypeStruct((SUBLANES, d), F32),
                   jax.ShapeDtypeStruct((SUBLANES, dh), F32), jax.ShapeDtypeStruct((nstate, du), F32),
                   jax.ShapeDtypeStruct((nstate, du), F32)),
        in_specs=[_rows(tb, d), _rows(tb, d), _rows(tb, dh), _rows(tb, dh, 0), _rows(tb, dh, 1), _rows(tb, dh, 2),
                  _rows(tb, dh, 3), _halo_prev(tb, dh, 2), _halo_prev(tb, dh, 3), _resident(c_re.shape),
                  _resident(c_im.shape), _full(v512.shape), _full(convw.shape), _resident(glu_w.shape),
                  _resident(h16.shape), _resident(h64.shape), _resident(w_out.shape), _full(vd.shape),
                  _rows(tb, nstate), _rows(tb, nstate)],
        out_specs=(_rows(tb, d), _rows(tb, 2 * dh), _rows(tb, dh), _rows(tb, dh), _rows(tb, dh), _rows(tb, nstate),
                   _rows(tb, nstate), _rows(tb, dh), _rows(tb, dh), _full((SUBLANES, d)), _full((SUBLANES, dh)),
                   _full((nstate, du)), _full((nstate, du))),
        compiler_params=_cparams(("arbitrary",), VMEM_BIG),
    )(dx2, o, y1, proj, proj, proj, proj, proj, proj, c_re, c_im, v512, convw, glu_w, h16, h64, w_out, vd,
      s_re, s_im)


def _mix_in_bwd(gt_re, gt_im, b_re, b_im, dy1, dcc, dbg, proj, x, dx2, vec, v512, convw, w_in_st):
    t, d = x.shape
    dh = dy1.shape[1]
    nstate = gt_re.shape[1]
    du_w, ds = dh // SSM_SPLIT, nstate // SSM_SPLIT
    ns, _, nc = w_in_st.shape
    tb = _blk(t, TB_MIX)
    nblk = t // tb

    def body(gr_ref, gi_ref, bre_ref, bim_ref, dy1_ref, dcc_ref, dccn_ref, dbg_ref, u_ref, cg_ref, v_ref, x_ref,
             dx2_ref, vec_ref, p_ref, cw_ref, w_ref, gx_ref, dproj_ref, vp_ref, dbr_ref, dbi_ref):
        i = pl.program_id(0)

        @pl.when(i == 0)
        def _():
            dbr_ref[...] = jnp.zeros(dbr_ref.shape, F32)
            dbi_ref[...] = jnp.zeros(dbi_ref.shape, F32)
        ub = u_ref[...].astype(BF16)
        du = []
        for q in range(SSM_SPLIT):
            rq, cq = slice(q * du_w, (q + 1) * du_w), slice(q * ds, (q + 1) * ds)
            du.append(lax.dot_general(gr_ref[:, cq].astype(BF16), bre_ref[rq, cq], (((1,), (1,)), ((), ())),
                                      preferred_element_type=F32)
                      + lax.dot_general(gi_ref[:, cq].astype(BF16), bim_ref[rq, cq], (((1,), (1,)), ((), ())),
                                        preferred_element_type=F32))
            dbr_ref[rq, :] += _dot_tn(ub[:, rq], gr_ref[:, cq])
            dbi_ref[rq, :] += _dot_tn(ub[:, rq], gi_ref[:, cq])
        du = dy1_ref[...] * p_ref[0:1, :] + jnp.concatenate(du, axis=1)
        dcc = dcc_ref[...]
        dccn = jnp.where(i < nblk - 1, dccn_ref[...], 0.0)
        dcv = (cw_ref[2:3, :] * dcc + cw_ref[1:2, :] * _shift_up(dcc, dccn, 1)
               + cw_ref[0:1, :] * _shift_up(dcc, dccn, 2))
        parts = [du, dbg_ref[...], dcv * v_ref[...], dcv * cg_ref[...]]
        xv = x_ref[...]
        r = lax.rsqrt(_rowmean(xv * xv) + EPS)
        xn = xv * r
        g = vec_ref[0:1, :]
        hg = xn * g
        dh1 = None
        for j in range(ns):
            pb = parts[j].astype(BF16)
            dproj_ref[:, j * nc:(j + 1) * nc] = pb
            pj =lax.dot_general(pb, w_ref[j], (((1,), (1,)), ((), ())), preferred_element_type=F32)
            dh1 = pj if dh1 is None else dh1 + pj
        dhg = dh1 * vec_ref[1:2, :]
        _acc_rows(vp_ref, i == 0, [_colsum(dh1), _colsum(dh1 * hg), _colsum(dhg * xn)])
        dxn = dhg * g
        gx_ref[...] = dx2_ref[...] + r * (dxn - xn * _rowmean(dxn * xn))

    assert nc == dh and ns == 4
    return pl.pallas_call(
        body, name="mix_in_bwd", grid=(nblk,),
        out_shape=(jax.ShapeDtypeStruct((t, d), F32), jax.ShapeDtypeStruct((t, ns * nc), BF16),
                   jax.ShapeDtypeStruct((SUBLANES, d), F32), jax.ShapeDtypeStruct((dh, ds), F32),
                   jax.ShapeDtypeStruct((dh, ds), F32)),
        in_specs=[_rows(tb, nstate), _rows(tb, nstate), _resident(b_re.shape), _resident(b_im.shape), _rows(tb, dh),
                  _rows(tb, dh), _halo_next(tb, dh, t), _rows(tb, dh), _rows(tb, dh, 0), _rows(tb, dh, 2),
                  _rows(tb, dh, 3), _rows(tb, d), _rows(tb, d), _full(vec.shape), _full(v512.shape),
                  _full(convw.shape), _resident(w_in_st.shape)],
        out_specs=(_rows(tb, d), _rows(tb, ns * nc), _full((SUBLANES, d)), _full((dh, ds)), _full((dh, ds))),
        compiler_params=_cparams(("arbitrary",), VMEM_BIG),
    )(gt_re, gt_im, b_re, b_im, dy1, dcc, dcc, dbg, proj, proj, proj, x, dx2, vec, v512, convw, w_in_st)


def _matmul_tn(a, b, m, bn, out_dtype, name, diag=False, bt=TB_TN, after=None):
    t = a.shape[0]
    n = b.shape[1]
    bt = _blk(t, bt)
    nk = t // bt
    extra = [] if after is None else [after]
    a_map = (lambda j, k: (k, j)) if diag else (lambda j, k: (k, 0))

    def body(a_ref, b_ref, *rest):
        o_ref, acc_ref = rest[-2:]
        k = pl.program_id(1)

        @pl.when(k == 0)
        def _():
            acc_ref[...] = jnp.zeros(acc_ref.shape, F32)
        acc_ref[...] += _dot_tn(a_ref[...], b_ref[...])

        @pl.when(k == nk - 1)
        def _():
            o_ref[...] = acc_ref[...].astype(out_dtype)

    return pl.pallas_call(
        body, name=name, grid=(n // bn, nk),
        out_shape=jax.ShapeDtypeStruct((n // bn, m, bn), out_dtype),
        in_specs=[pl.BlockSpec((bt, m), a_map), pl.BlockSpec((bt, bn), lambda j, k: (k, j))]
        + [pl.BlockSpec(memory_space=pl.ANY)] * len(extra),
        out_specs=pl.BlockSpec((None, m, bn), lambda j, k: (j, 0, 0)),
        scratch_shapes=[pltpu.VMEM((m, bn), F32)],
        compiler_params=_cparams(("parallel", "arbitrary"), VMEM_BIG),
    )(a, b, *extra)


def _ssm_bgrad(d_bre, d_bim, bt_re, bt_im, rows_in, fold):
    gh, cb = d_bre.shape
    nb = SSM_SPLIT
    rb = gh // nb
    gp = nb * cb
    p = fold.shape[1]

    def body(dr_ref, di_ref, br_ref, bi_ref, rin_ref, f_ref, dbr_ref, dbi_ref, rout_ref):
        row = lax.broadcasted_iota(jnp.int32, (rb, cb), 0)
        col = lax.broadcasted_iota(jnp.int32, (rb, cb), 1)
        mask = (row >> 4) == (col >> 6)
        gr = jnp.where(mask, dr_ref[...], 0.0)
        gi = jnp.where(mask, di_ref[...], 0.0)
        cr, ci = rin_ref[0:1, :], rin_ref[1:2, :]
        dbr_ref[...] = _split3_dot(cr * gr + ci * gi, f_ref[...])
        dbi_ref[...] = _split3_dot(cr * gi - ci * gr, f_ref[...])
        br, bi = br_ref[...], bi_ref[...]
        rout_ref[...] = jnp.zeros(rout_ref.shape, F32)
        rout_ref[0:1, :] = _colsum(br * gr + bi * gi)
        rout_ref[1:2, :] = _colsum(br * gi - bi * gr)

    dspec = pl.BlockSpec((rb, cb), lambda j: (j, 0))
    bspec = pl.BlockSpec((rb, cb), lambda j: (j, j))
    rspec = pl.BlockSpec((SUBLANES, cb), lambda j: (0, j))
    ospec = pl.BlockSpec((rb, p), lambda j: (j, 0))
    return pl.pallas_call(
        body, name="ssm_bgrad", grid=(nb,),
        out_shape=(jax.ShapeDtypeStruct((gh, p), F32), jax.ShapeDtypeStruct((gh, p), F32),
                   jax.ShapeDtypeStruct((SUBLANES, gp), F32)),
        in_specs=[dspec, dspec, bspec, bspec, rspec, _full(fold.shape)],
        out_specs=(ospec, ospec, rspec),
        compiler_params=_cparams(("parallel",)),
    )(d_bre, d_bim, bt_re, bt_im, rows_in, fold)


def _ssm_cgrad(d_cre, d_cim, fold):
    gp, cb = d_cre.shape
    nb = SSM_SPLIT
    rb = gp // nb
    h = fold.shape[1]

    def body(dr_ref, di_ref, f_ref, cr_ref, ci_ref):
        row = lax.broadcasted_iota(jnp.int32, (rb, cb), 0)
        col = lax.broadcasted_iota(jnp.int32, (rb, cb), 1)
        mask = (row >> 6) == (col >> 4)
        cr_ref[...] = _split3_dot(jnp.where(mask, dr_ref[...], 0.0), f_ref[...])
        ci_ref[...] = -_split3_dot(jnp.where(mask, di_ref[...], 0.0), f_ref[...])

    cspec = pl.BlockSpec((rb, cb), lambda j: (j, 0))
    ospec = pl.BlockSpec((rb, h), lambda j: (j, 0))
    return pl.pallas_call(
        body, name="ssm_cgrad", grid=(nb,),
        out_shape=(jax.ShapeDtypeStruct((gp, h), F32),) * 2,
        in_specs=[cspec, cspec, _full(fold.shape)], out_specs=(ospec, ospec),
        compiler_params=_cparams(("parallel",)),
    )(d_cre, d_cim, fold)


def _ssm_lamgrad(lam_re, lam_im, log_step, abar_re, abar_im, coef_re, coef_im, gc_re, gc_im, ga_re, ga_im):
    g, p = lam_re.shape

    def body(lr_ref, li_ref, ls_ref, ar_ref, ai_ref, cr_ref, ci_ref, gcr_ref, gci_ref, gar_ref, gai_ref,
             dlr_ref, dli_ref, dls_ref):
        lam_raw = lr_ref[...]
        lr = jnp.minimum(lam_raw, LAMBDA_RE_MAX)
        li = li_ref[...]
        st = jnp.exp(ls_ref[...])
        den = lr * lr + li * li
        gcr, gci = gcr_ref[...], gci_ref[...]
        gab_r = gar_ref[...] + (lr * gcr - li * gci) / den
        gab_i = gai_ref[...] + (lr * gci + li * gcr) / den
        cr, ci = cr_ref[...], ci_ref[...]
        wr = -(cr * lr + ci * li) / den
        wi = -(ci * lr - cr * li) / den
        gl_r = wr * gcr + wi * gci
        gl_i = wr * gci - wi * gcr
        ar, ai = ar_ref[...], ai_ref[...]
        gw_r = ar * gab_r + ai * gab_i
        gw_i = ar * gab_i - ai * gab_r
        gl_r = gl_r + st * gw_r
        gl_i = gl_i + st * gw_i
        pass_through = jnp.where(lam_raw < LAMBDA_RE_MAX, 1.0, jnp.where(lam_raw == LAMBDA_RE_MAX, 0.5, 0.0))
        dlr_ref[...] = gl_r * pass_through
        dli_ref[...] = gl_i
        dls_ref[...] = st * jnp.sum(lr * gw_r + li * gw_i, axis=1, keepdims=True)

    sds = jax.ShapeDtypeStruct((g, p), F32)
    return pl.pallas_call(body, name="ssm_lamgrad", out_shape=(sds, sds, jax.ShapeDtypeStruct((g, 1), F32)))(
        lam_re, lam_im, log_step, abar_re, abar_im, coef_re, coef_im, gc_re, gc_im, ga_re, ga_im)


def _row_block(r, most=256):
    for rb in range(min(r, most), BF16_ROWS - 1, -1):
        if r % rb == 0 and rb % BF16_ROWS == 0:
            return rb
    return r


def _adamw_math(w, g, m, v):
    m = ADAM_B1 * m + (1.0 - ADAM_B1) * g
    v = ADAM_B2 * v + (1.0 - ADAM_B2) * (g * g)
    m_hat = m / (1.0 - ADAM_B1 ** ADAM_STEP)
    v_hat = v / (1.0 - ADAM_B2 ** ADAM_STEP)
    delta = -ADAM_LR * (m_hat / (jnp.sqrt(v_hat) + ADAM_EPS) + ADAM_WD * w)
    return delta, m, v


def _adamw_big(p_mine, p_sib, w, m, v, name):
    r, c = w.shape
    rb = _row_block(r)

    def body(a_ref, b_ref, w_ref, m_ref, v_ref, g_ref, d_ref, mo_ref, vo_ref):
        g = a_ref[...] + b_ref[...]
        g_ref[...] = g
        d_ref[...], mo_ref[...], vo_ref[...] = _adamw_math(w_ref[...], g, m_ref[...], v_ref[...])

    spec = pl.BlockSpec((rb, c), lambda i: (i, 0))
    sds = jax.ShapeDtypeStruct((r, c), F32)
    return pl.pallas_call(
        body, name=name, grid=(r // rb,), out_shape=(sds,) * 4, in_specs=[spec] * 5, out_specs=(spec,) * 4,
        compiler_params=_cparams(("parallel",), VMEM_MID),
    )(p_mine, p_sib, w, m, v)


def _sum_blocks(stack, name):
    n, r, c = stack.shape
    rb = _row_block(r)

    def body(s_ref, o_ref):
        acc = s_ref[0].astype(F32)
        for k in range(1, n):
            acc = acc + s_ref[k].astype(F32)
        o_ref[...] = acc

    return pl.pallas_call(
        body, name=name, grid=(r // rb,), out_shape=jax.ShapeDtypeStruct((r, c), F32),
        in_specs=[pl.BlockSpec((n, rb, c), lambda i: (0, i, 0))], out_specs=pl.BlockSpec((rb, c), lambda i: (i, 0)),
        compiler_params=_cparams(("parallel",), VMEM_MID),
    )(stack)


def _add2(a, b):
    def body(a_ref, b_ref, o_ref):
        o_ref[...] = a_ref[...] + b_ref[...]

    return pl.pallas_call(body, name="add_small", out_shape=jax.ShapeDtypeStruct(a.shape, F32))(a, b)


def _adamw_ada(c_all, dmod_cols, w, m, v):
    d, n = w.shape
    bn = 512

    def body(c_ref, dm_ref, w_ref, m_ref, v_ref, g_ref, d_ref, mo_ref, vo_ref):
        cc = c_ref[...]
        g = _dot_tn(cc * _sigmoid(cc), dm_ref[...])
        g_ref[...] = g
        d_ref[...], mo_ref[...], vo_ref[...] = _adamw_math(w_ref[...], g, m_ref[...], v_ref[...])

    spec = pl.BlockSpec((d, bn), lambda j: (0, j))
    sds = jax.ShapeDtypeStruct((d, n), F32)
    return pl.pallas_call(
        body, name="adamw_ada", grid=(n // bn,), out_shape=(sds,) * 4,
        in_specs=[_full((N_DEV, d)), pl.BlockSpec((N_DEV, bn), lambda j: (0, j)), spec, spec, spec],
        out_specs=(spec,) * 4, compiler_params=_cparams(("parallel",)),
    )(c_all, dmod_cols, w, m, v)


def _adamw_small(items):
    n = len(items)

    def body(*refs):
        ins, outs = refs[:4 * n], refs[4 * n:]
        for k in range(n):
            w_ref, g_ref, m_ref, v_ref = ins[4 * k:4 * k + 4]
            outs[3 * k][...], outs[3 * k + 1][...], outs[3 * k + 2][...] = _adamw_math(
                w_ref[...], g_ref[...], m_ref[...], v_ref[...])

    flat = [a for it in items for a in it]
    out_shape = tuple(jax.ShapeDtypeStruct(it[0].shape, F32) for it in items for _ in range(3))
    res = pl.pallas_call(body, name="adamw_small", out_shape=out_shape,
                         compiler_params=_cparams(vmem=VMEM_BIG))(*flat)
    return [tuple(res[3 * k:3 * k + 3]) for k in range(n)]


def _group_mean_matrix(n, group):
    idx = np.arange(n) // group
    return (idx[:, None] == idx[None, :]).astype(np.float32) / group


def _fold_matrix(n, period):
    return (np.arange(n)[:, None] % period == np.arange(period)[None, :]).astype(np.float32)


def _rows8(*rows):
    c = rows[0].shape[-1]
    pad = jnp.zeros((SUBLANES - len(rows), c), F32)
    return jnp.concatenate([r.reshape(1, c) for r in rows] + [pad], axis=0)


def _to_rows(a, width):
    flat = a.reshape(-1)
    n = -(-flat.shape[0] // width)
    flat = jnp.pad(flat, (0, n * width - flat.shape[0]))
    return flat.reshape(n, width)


def kernel(x, c, w_ada, b_ada, g_pre_mix, g_post_mix, w_in, ssm_lam_re, ssm_lam_im, ssm_log_step, ssm_b_re, ssm_b_im, ssm_c_re, ssm_c_im, ssm_d, glu_w, glu_b, g_out_ssm, conv_w, g_out_conv, w_out, g_pre_ffn, g_post_ffn, w_up, ffn_conv_w, w_down, loss_target, m_w_ada, m_b_ada, m_g_pre_mix, m_g_post_mix, m_w_in, m_ssm_lam_re, m_ssm_lam_im, m_ssm_log_step, m_ssm_b_re, m_ssm_b_im, m_ssm_c_re, m_ssm_c_im, m_ssm_d, m_glu_w, m_glu_b, m_g_out_ssm, m_conv_w, m_g_out_conv, m_w_out, m_g_pre_ffn, m_g_post_ffn, m_w_up, m_ffn_conv_w, m_w_down, v_w_ada, v_b_ada, v_g_pre_mix, v_g_post_mix, v_w_in, v_ssm_lam_re, v_ssm_lam_im, v_ssm_log_step, v_ssm_b_re, v_ssm_b_im, v_ssm_c_re, v_ssm_c_im, v_ssm_d, v_glu_w, v_glu_b, v_g_out_ssm, v_conv_w, v_g_out_conv, v_w_out, v_g_pre_ffn, v_g_post_ffn, v_w_up, v_ffn_conv_w, v_w_down):
    xs = x[0]
    tgt = loss_target[0]
    t, d = xs.shape
    xi, yi, ci = lax.axis_index("x"), lax.axis_index("y"), lax.axis_index("c")
    chip = 2 * xi + yi
    dev = 2 * chip + ci

    n_groups, n_state = ssm_lam_re.shape[1:]
    n_gch = ssm_b_re.shape[3]
    d_ssm = n_groups * n_gch
    gp = n_groups * n_state
    n_ada = w_ada.shape[2]
    d_ff = w_down.shape[1] * N_CHIPS
    n_upc = w_up.shape[2]

    w_names = ("w_in", "glu_w", "w_out", "w_up", "w_down")
    c_gath, _ = _allgather8(jnp.broadcast_to(c, (SUBLANES, d)), SUBLANES, "gather_c")
    c_all = c_gath.reshape(N_DEV, SUBLANES, d)[:, 0, :]

    def pad8(a):
        return jnp.concatenate([a, jnp.zeros((SUBLANES - a.shape[0], a.shape[1]), a.dtype)], axis=0)

    def start(name, arrs, after):
        return _chips_start(name, True, arrs, [_landing(a, chip) for a in arrs], after)

    w_names = ("w_in", "mod", "conv_w", "ffn_conv_w", "glu_w", "w_out", "w_up", "w_down")
    first = start("weights_start_in", [w_in[0].astype(BF16)], c_gath)
    b_sh = lax.dynamic_slice(b_ada, (0, chip * n_ada), (1, n_ada))
    mod_sh = _mod_shard(c_all + first[4][0:1, 0:1], w_ada[0], b_sh)
    second = start("weights_start_mod", [mod_sh, pad8(conv_w[0]), pad8(ffn_conv_w[0])], None)
    third = start("weights_start_rest", [w[0].astype(BF16) for w in (glu_w, w_out, w_up, w_down)], second[4])
    w_send, w_recv, w_src, w_land = [list(first[k]) + list(second[k]) + list(third[k]) for k in range(4)]
    w_token = third[4]

    def weights(names, after):
        ks = [w_names.index(nm) for nm in names]
        return _chips_wait("weights_wait_" + names[-1], True, [w_send[k] for k in ks], [w_recv[k] for k in ks],
                           [w_src[k] for k in ks], [w_land[k] for k in ks], after)

    lam_re, lam_im = ssm_lam_re[0], ssm_lam_im[0]
    log_step = ssm_log_step[0].reshape(n_groups, 1) + w_token[0:1, 0:1]
    abar_re, abar_im, coef_re, coef_im = _ssm_prep(lam_re, lam_im, log_step)
    a_rows = _rows8(abar_re.reshape(1, gp), abar_im.reshape(1, gp))
    coef_rows = _rows8(coef_re.reshape(1, gp), coef_im.reshape(1, gp))
    bt_re = jnp.tile(ssm_b_re[0].transpose(0, 2, 1).reshape(d_ssm, n_state), (1, n_groups))
    bt_im = jnp.tile(ssm_b_im[0].transpose(0, 2, 1).reshape(d_ssm, n_state), (1, n_groups))
    ct_re = jnp.tile(ssm_c_re[0].transpose(0, 2, 1).reshape(gp, n_gch), (1, n_groups))
    ct_im = jnp.tile(ssm_c_im[0].transpose(0, 2, 1).reshape(gp, n_gch), (1, n_groups))
    bblk_re, bblk_im, cblk_re, cblk_im = _ssm_blocks(bt_re, bt_im, ct_re, ct_im, coef_rows)

    h16 = jnp.asarray(_group_mean_matrix(d_ssm, n_gch), BF16)
    h64 = jnp.asarray(_group_mean_matrix(d_ssm, CONV_HEAD_DIM), BF16)

    g_mod, g_cw, g_fw, w_in_st = weights(("mod", "conv_w", "ffn_conv_w", "w_in"), bblk_re)
    mod_all = g_mod.transpose(1, 0, 2).reshape(N_DEV, N_CHIPS * n_ada)
    mod = lax.dynamic_slice(mod_all, (dev, 0), (1, N_CHIPS * n_ada))
    sh1, sc1, gt1, sh2, sc2, gt2 = [mod[:, k * d:(k + 1) * d] for k in range(6)]
    convw_full = pad8(g_cw[:, :3, :].transpose(1, 0, 2).reshape(3, d_ssm))
    fw_full = pad8(g_fw[:, :3, :].transpose(1, 0, 2).reshape(3, N_CHIPS * n_upc))

    v512 = _rows8(ssm_d, glu_b, g_out_ssm, g_out_conv)
    vec1 =_rows8(g_pre_mix, 1.0 + sc1, sh1)
    vd1 = _rows8(g_post_mix, gt1)
    vec2 = _rows8(g_pre_ffn, 1.0 + sc2, sh2)
    vd2 = _rows8(g_post_ffn, gt2)

    proj, bu_re, bu_im, h1b = _mix_in(xs, vec1, w_in_st, bblk_re, bblk_im)
    s_re, s_im = _scan_fwd(a_rows, bu_re, bu_im)
    g_glu, g_wout = weights(("glu_w", "w_out"), s_re)
    glu_full = g_glu.reshape(d_ssm, d_ssm)
    w_out_full = g_wout.reshape(2 * d_ssm, d)
    y1, o_mix, x2 = _mix_out(xs, proj, s_re, s_im, cblk_re, cblk_im, v512, convw_full, glu_full, h16, h64,
                             w_out_full, vd1)
    (w_up_st,) = weights(("w_up",), x2)
    up, h2b = _ffn_up(x2, vec2, w_up_st)
    (g_wdown,) = weights(("w_down",), up)
    w_down_full = g_wdown.reshape(d_ff, d)
    actb, ddnb, dout, dhid, vp_dn, loss_blk = _ffn_down(up, fw_full, w_down_full, w_down_full.T, x2, tgt, vd2)

    g_names = ("w_down", "w_up", "w_out", "glu_w", "w_in")
    gw_down = _matmul_tn(actb, ddnb, d_ff, d, BF16, "dw_down", bt=1024).reshape(N_CHIPS, d_ff // N_CHIPS, d)
    dx2, dupb, vp_up, df_rows = _ffn_up_bwd(dhid, up, fw_full, x2, dout, vec2, w_up_st)
    gw_up = _matmul_tn(h2b, dupb, d, n_upc, BF16, "dw_up", bt=2048)
    ga_send, ga_recv, ga_src, ga_land, ga_token = _chips_start(
        "grads_start_ffn", False, [gw_down, gw_up],
        [_landing(lax.dynamic_index_in_dim(g, chip, 0, False), chip) for g in (gw_down, gw_up)])
    (dob, ycatb, zb, dqb, dy1, g_re, g_im, dcc, dbg, vp_mo, vp5, d_cre, d_cim) = _mix_out_bwd(
        dx2, o_mix, y1, proj, s_re, s_im, cblk_re, cblk_im, v512, convw_full, glu_full, h16, h64, w_out_full,
        vd1 + ga_token[0:1, 0:1])
    gw_out = _matmul_tn(ycatb, dob, 2 * d_ssm, d, BF16, "dw_out", bt=2048)
    gw_out = gw_out.reshape(N_CHIPS, 2 * d_ssm // N_CHIPS, d)
    gw_glu = _matmul_tn(zb, dqb, d_ssm, d_ssm, BF16, "dw_glu", bt=2048).reshape(N_CHIPS, d_ssm // N_CHIPS, d_ssm)
    gb_send, gb_recv, gb_src, gb_land, gb_token = _chips_start(
        "grads_start_mix", False, [gw_out, gw_glu],
        [_landing(lax.dynamic_index_in_dim(g, chip, 0, False), chip) for g in (gw_out, gw_glu)])
    gt_re, gt_im, ga_re8, ga_im8 = _scan_bwd(a_rows + gb_token[0:1, 0:1], g_re, g_im, s_re, s_im)
    grad_x, dprojb, vp_mi, d_bre, d_bim = _mix_in_bwd(gt_re, gt_im, bblk_re, bblk_im, dy1, dcc, dbg, proj, xs, dx2,
                                                      vec1, v512, convw_full, w_in_st)
    ssm_u, ssm_s = d_ssm // SSM_SPLIT, gp // SSM_SPLIT

    fold_b = jnp.asarray(_fold_matrix(ssm_s, n_state), BF16)
    fold_c = jnp.asarray(_fold_matrix(ssm_u, n_gch), BF16)
    db_re_f, db_im_f, gc_rows = _ssm_bgrad(d_bre, d_bim, bt_re, bt_im, coef_rows, fold_b)
    dc_re_f, dc_im_f = _ssm_cgrad(d_cre, d_cim, fold_c)
    ga_sum = _ga_rowsum(ga_re8, ga_im8)
    g_lam_re, g_lam_im, g_log_step = _ssm_lamgrad(
        lam_re, lam_im, log_step, abar_re, abar_im, coef_re, coef_im,
        gc_rows[0].reshape(n_groups, n_state), gc_rows[1].reshape(n_groups, n_state),
        ga_sum[0].reshape(n_groups, n_state), ga_sum[1].reshape(n_groups, n_state))
    g_b_re = db_re_f.reshape(n_groups, n_gch, n_state).transpose(0, 2, 1)
    g_b_im = db_im_f.reshape(n_groups, n_gch, n_state).transpose(0, 2, 1)
    g_c_re = dc_re_f.reshape(n_groups, n_state, n_gch).transpose(0, 2, 1)
    g_c_im = dc_im_f.reshape(n_groups, n_state, n_gch).transpose(0, 2, 1)

    dmod = jnp.concatenate([vp_mi[0:1], vp_mi[1:2], vp_mo[0:1], vp_up[0:1], vp_up[1:2], vp_dn[0:1]], axis=1)
    small = [
        ("g_pre_mix", vp_mi[2:3]), ("g_post_mix", vp_mo[1:2]), ("g_pre_ffn", vp_up[2:3]), ("g_post_ffn", vp_dn[1:2]),
        ("ssm_lam_re", g_lam_re), ("ssm_lam_im", g_lam_im), ("ssm_log_step", g_log_step),
        ("ssm_b_re", g_b_re), ("ssm_b_im", g_b_im), ("ssm_c_re", g_c_re), ("ssm_c_im", g_c_im),
        ("ssm_d", vp5[3:4]), ("glu_b", vp5[2:3]), ("g_out_ssm", vp5[0:1]), ("g_out_conv", vp5[1:2]),
        ("conv_w", vp5[4:7]), ("ffn_conv_w", df_rows[0:3]), ("loss", loss_blk[0:1, 0:1]),
    ]
    packed, offsets, row = [], {}, 0
    for name, a in small:
        r = _to_rows(a, d)
        offsets[name] = (row, a.shape)
        packed.append(r)
        row += r.shape[0]
    n_small = -(-row // SUBLANES) * SUBLANES
    packed.append(jnp.zeros((n_small - row, d), F32))
    packed.append(pad8(dmod.reshape(6, d)))
    pack = jnp.concatenate(packed, axis=0)
    sm_send, sm_recv, sm_src, sm_land, sm_token = _chips_start("small_start", True, [pack], [_landing(pack, chip)])

    gw_in = _matmul_tn(h1b, dprojb, d, w_in.shape[2], BF16, "dw_in", bt=2048, after=sm_token)
    gc_send, gc_recv, gc_src, gc_land, gc_token = _chips_start(
        "grads_start_in", False, [gw_in], [_landing(lax.dynamic_index_in_dim(gw_in, chip, 0, False), chip)])

    def finish(names, landed):
        partial = [_sum_blocks(s, "sum_" + nm) for s, nm in zip(landed, names)]
        theirs = _swap_sibling(partial, "swap_" + names[0])
        done = {}
        for nm, pm, ps in zip(names, partial, theirs):
            w_, m_, v_ = big_params[nm]
            done[nm] = _adamw_big(pm, ps, w_[0], m_[0], v_[0], "adamw_" + nm)
        return done

    big_params = {"w_down": (w_down, m_w_down, v_w_down), "w_up": (w_up, m_w_up, v_w_up),
                  "w_out": (w_out, m_w_out, v_w_out), "glu_w": (glu_w, m_glu_w, v_glu_w),
                  "w_in": (w_in, m_w_in, v_w_in)}
    big = finish(("w_down", "w_up"), _chips_wait("grads_wait_ffn", False, ga_send, ga_recv, ga_src, ga_land, gc_token))

    (sm_landed,) = _chips_wait("small_wait", True, sm_send, sm_recv, sm_src, sm_land, big["w_up"][0])
    sm_part = _sum_blocks(sm_landed, "sum_small")
    dmod_mine = sm_landed[:, n_small:n_small + SUBLANES, :]
    sm_sib, dmod_sib = _swap_sibling([sm_part, dmod_mine], "swap_small")
    sums = _add2(sm_part, sm_sib)
    dmod_by_core = jnp.stack([dmod_mine, dmod_sib], axis=1)
    dmod_by_core = jnp.where(ci == 0, dmod_by_core, dmod_by_core[:, ::-1])
    dmod_all = dmod_by_core[:, :, :6, :].reshape(N_DEV, 6 * d)
    g_b_ada = sums[n_small:n_small + 6].reshape(1, 6 * d)

    def unpack(name):
        r0, shape = offsets[name]
        size = math.prod(shape)
        nrow = -(-size // d)
        return sums[r0:r0 + nrow].reshape(-1)[:size].reshape(shape)

    dmod_cols = lax.dynamic_slice(dmod_all, (0, chip * n_ada), (N_DEV, n_ada))
    ada = _adamw_ada(c_all, dmod_cols, w_ada[0], m_w_ada[0], v_w_ada[0])

    big.update(finish(("w_out", "glu_w", "w_in"), _chips_wait(
        "grads_wait_mix", False, list(gb_send) + list(gc_send), list(gb_recv) + list(gc_recv),
        list(gb_src) + list(gc_src), list(gb_land) + list(gc_land), ada[0])))

    g_small = {name: unpack(name) for name, _ in small}
    g_small["b_ada"] = g_b_ada
    g_small["conv_w"] = lax.dynamic_slice(g_small["conv_w"], (0, chip * conv_w.shape[2]), (3, conv_w.shape[2]))
    g_small["ffn_conv_w"] = lax.dynamic_slice(g_small["ffn_conv_w"], (0, chip * n_upc), (3, n_upc))
    g_small["ssm_log_step"] = g_small["ssm_log_step"].reshape(1, n_groups)
    small_params = {
        "b_ada": (b_ada, m_b_ada, v_b_ada), "g_pre_mix": (g_pre_mix, m_g_pre_mix, v_g_pre_mix),
        "g_post_mix": (g_post_mix, m_g_post_mix, v_g_post_mix), "ssm_lam_re": (ssm_lam_re, m_ssm_lam_re, v_ssm_lam_re),
        "ssm_lam_im": (ssm_lam_im, m_ssm_lam_im, v_ssm_lam_im),
        "ssm_log_step": (ssm_log_step, m_ssm_log_step, v_ssm_log_step),
        "ssm_b_re": (ssm_b_re, m_ssm_b_re, v_ssm_b_re), "ssm_b_im": (ssm_b_im, m_ssm_b_im, v_ssm_b_im),
        "ssm_c_re": (ssm_c_re, m_ssm_c_re, v_ssm_c_re), "ssm_c_im": (ssm_c_im, m_ssm_c_im, v_ssm_c_im),
        "ssm_d": (ssm_d, m_ssm_d, v_ssm_d), "glu_b": (glu_b, m_glu_b, v_glu_b),
        "g_out_ssm": (g_out_ssm, m_g_out_ssm, v_g_out_ssm), "conv_w": (conv_w, m_conv_w, v_conv_w),
        "g_out_conv": (g_out_conv, m_g_out_conv, v_g_out_conv), "g_pre_ffn": (g_pre_ffn, m_g_pre_ffn, v_g_pre_ffn),
        "g_post_ffn": (g_post_ffn, m_g_post_ffn, v_g_post_ffn),
        "ffn_conv_w": (ffn_conv_w, m_ffn_conv_w, v_ffn_conv_w),
    }

    def natural(a):
        return a[0] if a.ndim > 2 else a

    names = list(small_params)
    items = []
    for nm in names:
        w_, m_, v_ = small_params[nm]
        items.append((natural(w_), g_small[nm].reshape(natural(w_).shape), natural(m_), natural(v_)))
    upd = _adamw_small(items)
    small_out = {}
    for nm, (dl, mo, vo) in zip(names, upd):
        shp = small_params[nm][0].shape
        small_out[nm] = (g_small[nm].reshape(shp), dl.reshape(shp), mo.reshape(shp), vo.reshape(shp))

    loss = g_small["loss"][0, 0]

    order = ["w_ada", "b_ada", "g_pre_mix", "g_post_mix", "w_in", "ssm_lam_re", "ssm_lam_im", "ssm_log_step",
             "ssm_b_re", "ssm_b_im", "ssm_c_re", "ssm_c_im", "ssm_d", "glu_w", "glu_b", "g_out_ssm", "conv_w",
             "g_out_conv", "w_out", "g_pre_ffn", "g_post_ffn", "w_up", "ffn_conv_w", "w_down"]
    results = {"w_ada": tuple(a[None] for a in ada)}
    for nm in big:
        results[nm] = tuple(a[None] for a in big[nm])
    results.update(small_out)
    outs = [loss, grad_x[None]]
    for k in range(4):
        outs += [results[nm][k] for nm in order]
    return tuple(outs)


def _ga_rowsum(ga_re8, ga_im8):
    n = ga_re8.shape[1]

    def body(r_ref, i_ref, o_ref):
        o_ref[...] = jnp.zeros(o_ref.shape, F32)
        o_ref[0:1, :] = _colsum(r_ref[...])
        o_ref[1:2, :] = _colsum(i_ref[...])

    return pl.pallas_call(body, name="ga_rowsum", out_shape=jax.ShapeDtypeStruct((SUBLANES, n), F32))(ga_re8, ga_im8)
```

```python
import functools
import math

import jax
import jax.numpy as jnp
import numpy as np
from jax import lax
from jax.experimental import pallas as pl
from jax.experimental.pallas import tpu as pltpu

F32 = jnp.float32
BF16 = jnp.bfloat16
MESH = pl.DeviceIdType.MESH

EPS = 1e-6
LAMBDA_RE_MAX = -1e-4
ADAM_LR = 0.001
ADAM_B1 = 0.9
ADAM_B2 = 0.999
ADAM_EPS = 1e-08
ADAM_WD = 0.01
ADAM_STEP = 10

SUBLANES = 8
BF16_ROWS = 16
N_CHIPS = 4
N_DEV = 8
CONV_HEAD_DIM = 64
VMEM_BIG = 56 * 1024 * 1024
VMEM_MID = 40 * 1024 * 1024

TB_MIX = 256
TB_FFN = 256
TB_SCAN = 1024
W_SCAN = 256
SSM_SPLIT = 4
CW_FFN = 256
SCAN_UNROLL = 4
TB_TN = 512


def _cparams(sem=None, vmem=None):
    kw = {}
    if sem is not None:
        kw["dimension_semantics"] = sem
    if vmem is not None:
        kw["vmem_limit_bytes"] = vmem
    return pltpu.CompilerParams(**kw)


def _blk(t, pref):
    return pref if t % pref == 0 else t


def _dot(a, b):
    return jnp.dot(a.astype(BF16), b.astype(BF16), preferred_element_type=F32)


def _dot_nt(a, b):
    return lax.dot_general(a.astype(BF16), b.astype(BF16), (((1,), (1,)), ((), ())),
                           preferred_element_type=F32)


def _dot_tn(a, b):
    return lax.dot_general(a.astype(BF16), b.astype(BF16), (((0,), (0,)), ((), ())),
                           preferred_element_type=F32)


def _sigmoid(x):
    return 0.5 * jnp.tanh(0.5 * x) + 0.5


_GELU_K = math.sqrt(2.0 / math.pi)
_GELU_C = 0.044715


def _gelu(x):
    th = jnp.tanh(_GELU_K * (x + _GELU_C * x * x * x))
    return 0.5 * x * (1.0 + th)


def _gelu_grad(x):
    x2 = x * x
    th = jnp.tanh(_GELU_K * (x + _GELU_C * x2 * x))
    return 0.5 * (1.0 + th) + 0.5 * x * (1.0 - th * th) * _GELU_K * (1.0 + 3.0 * _GELU_C * x2)


def _rowmean(x):
    return jnp.mean(x, axis=-1, keepdims=True)


def _colsum(x):
    return jnp.sum(x, axis=0, keepdims=True)


def _split_dot(x, m):
    hi = x.astype(BF16)
    lo = (x - hi.astype(F32)).astype(BF16)
    return (jnp.dot(hi, m, preferred_element_type=F32) + jnp.dot(lo, m, preferred_element_type=F32))


def _split3_dot(x, m):
    hi = x.astype(BF16)
    r1 = x - hi.astype(F32)
    mid = r1.astype(BF16)
    lo = (r1 - mid.astype(F32)).astype(BF16)
    return (jnp.dot(hi, m, preferred_element_type=F32) + jnp.dot(mid, m, preferred_element_type=F32)
            + jnp.dot(lo, m, preferred_element_type=F32))


def _shift_down(x, halo, k):
    r = pltpu.roll(x, k, 0)
    row = lax.broadcasted_iota(jnp.int32, x.shape, 0)
    for j in range(k):
        r = jnp.where(row == j, halo[SUBLANES - k + j:SUBLANES - k + j + 1, :], r)
    return r


def _shift_up(x, halo, k):
    n = x.shape[0]
    r = pltpu.roll(x, n - k, 0)
    row = lax.broadcasted_iota(jnp.int32, x.shape, 0)
    for j in range(k):
        r = jnp.where(row == n - k + j, halo[j:j + 1, :], r)
    return r


def _acc_rows(ref, first, rows):
    @pl.when(first)
    def _():
        ref[...] = jnp.zeros(ref.shape, ref.dtype)
    for j, r in enumerate(rows):
        ref[j:j + 1, :] += r


def _rows(tb, c, col=0):
    return pl.BlockSpec((tb, c), lambda i, col=col: (i, col))


def _full(shape):
    nd = len(shape)
    return pl.BlockSpec(shape, lambda i, nd=nd: (0,) * nd)


def _resident(shape):
    nd = len(shape)
    return pl.BlockSpec(shape, lambda i, nd=nd: (0,) * nd, pipeline_mode=pl.Buffered(1))


def _halo_prev(tb, c, col=0):
    per = tb // SUBLANES
    return pl.BlockSpec((SUBLANES, c), lambda i, col=col: (jnp.maximum(i * per - 1, 0), col))


def _halo_next(tb, c, t, col=0, rows=SUBLANES):
    per = tb // rows
    last = t // rows - 1
    return pl.BlockSpec((rows, c), lambda i, col=col: (jnp.minimum((i + 1) * per, last), col))


def _mesh_pos():
    return lax.axis_index("x"), lax.axis_index("y"), lax.axis_index("c")


def _allgather8(x_pad, n_sum, name):
    m_per, n = x_pad.shape

    def body(x_ref, out_ref, sum_ref, send_sems, recv_sems, local_sem):
        x, y, c = _mesh_pos()
        me, sibling = (x, y, c), (x, y, 1 - c)
        chips = [(1 - x, y), (x, 1 - y), (1 - x, 1 - y)]

        def rows(px, py, pc):
            return out_ref.at[pl.ds((4 * px + 2 * py + pc) * m_per, m_per), :]

        def copy(k, block, to, src=None):
            return pltpu.make_async_remote_copy(
                src_ref=rows(*block) if src is None else src, dst_ref=rows(*block),
                send_sem=send_sems.at[k], recv_sem=recv_sems.at[k], device_id=to, device_id_type=MESH)

        mine = pltpu.make_async_copy(x_ref, rows(*me), local_sem)
        mine.start()
        first = [copy(0, me, sibling, src=x_ref)]
        first += [copy(1 + j, me, (*chip, c), src=x_ref) for j, chip in enumerate(chips)]
        for cp in first:
            cp.start()
        passed = [copy(4 + j, (*chip, c), sibling) for j, chip in enumerate(chips)]
        for j, chip in enumerate(chips):
            copy(1 + j, (*chip, c), me).wait_recv()
            passed[j].start()
        copy(0, sibling, me).wait_recv()
        for j, chip in enumerate(chips):
            copy(4 + j, (*chip, 1 - c), me).wait_recv()
        for cp in first + passed:
            cp.wait_send()
        mine.wait()
        acc = out_ref[0:n_sum, :]
        for k in range(1, N_DEV):
            acc = acc + out_ref[k * m_per:k * m_per + n_sum, :]
        sum_ref[...] = acc

    return pl.pallas_call(
        body, name=name,
        out_shape=(jax.ShapeDtypeStruct((N_DEV * m_per, n), F32), jax.ShapeDtypeStruct((n_sum, n), F32)),
        in_specs=[pl.BlockSpec(memory_space=pltpu.VMEM)],
        out_specs=(pl.BlockSpec(memory_space=pltpu.VMEM), pl.BlockSpec(memory_space=pltpu.VMEM)),
        scratch_shapes=[pltpu.SemaphoreType.DMA((7,)), pltpu.SemaphoreType.DMA((7,)), pltpu.SemaphoreType.DMA],
        compiler_params=_cparams(vmem=VMEM_MID),
    )(x_pad)


_HBM = pl.BlockSpec(memory_space=pltpu.HBM)
_SEM = pl.BlockSpec(memory_space=pltpu.SEMAPHORE)
_EFFECT = pltpu.SideEffectType.DATAFLOW_SIDE_EFFECTING


def _chip_copy(gather, src_ref, land_ref, send, recv, j, arrival):
    x, y, c = _mesh_pos()
    peer = [(1 - x, y), (x, 1 - y), (1 - x, 1 - y)][j]
    peer_chip = 2 * peer[0] + peer[1]
    my_chip = 2 * x + y
    return pltpu.make_async_remote_copy(
        src_ref=land_ref.at[my_chip] if gather else src_ref.at[peer_chip],
        dst_ref=land_ref.at[peer_chip if arrival else my_chip],
        send_sem=send.at[j], recv_sem=recv.at[j], device_id=(*peer, c), device_id_type=MESH)


def _chips_start(name, gather, srcs, lands, after=None):
    n, ns = len(lands), len(srcs)
    extra = [] if after is None else [after]

    def body(*refs):
        src_refs, land_refs = refs[:ns], refs[ns:ns + n]
        outs = refs[ns + n + len(extra):]
        sends, recvs, token = outs[:n], outs[n:2 * n], outs[-1]
        for k in range(n):
            for j in range(3):
                _chip_copy(gather, src_refs[k] if ns else None, land_refs[k], sends[k], recvs[k], j, False).start()
        token[...] = jnp.zeros(token.shape, F32)

    sem = pltpu.SemaphoreType.DMA((3,))
    thru = tuple(pltpu.HBM(a.shape, a.dtype) for a in list(srcs) + list(lands))
    res = pl.pallas_call(
        body, name=name,
        out_shape=(sem,) * (2 * n) + thru + (jax.ShapeDtypeStruct((SUBLANES, 128), F32),),
        in_specs=[_HBM] * (ns + n) + [pl.BlockSpec(memory_space=pl.ANY)] * len(extra),
        out_specs=(_SEM,) * (2 * n) + (_HBM,) * (ns + n) + (pl.BlockSpec(memory_space=pltpu.VMEM),),
        input_output_aliases={k: 2 * n + k for k in range(ns + n)},
        compiler_params=pltpu.CompilerParams(has_side_effects=_EFFECT),
    )(*[pltpu.with_memory_space_constraint(a, pltpu.HBM) for a in list(srcs) + list(lands)], *extra)
    return res[:n], res[n:2 * n], res[2 * n:2 * n + ns], res[2 * n + ns:2 * n + ns + n], res[-1]


def _chips_wait(name, gather, sends, recvs, srcs, lands, after):
    n, ns = len(lands), len(srcs)

    def body(*refs):
        src_refs, land_refs = refs[:ns], refs[ns:ns + n]
        sends_, recvs_ = refs[ns + n:ns + 2 * n], refs[ns + 2 * n:ns + 3 * n]
        for k in range(n):
            for j in range(3):
                cp = _chip_copy(gather, src_refs[k] if ns else None, land_refs[k], sends_[k], recvs_[k], j, True)
                cp.wait_send()
                cp.wait_recv()

    thru = tuple(pltpu.HBM(a.shape, a.dtype) for a in list(srcs) + list(lands))
    res = pl.pallas_call(
        body, name=name, out_shape=thru,
        in_specs=[_HBM] * (ns + n) + [_SEM] * (2 * n) + [pl.BlockSpec(memory_space=pl.ANY)],
        out_specs=(_HBM,) * (ns + n),
        input_output_aliases={k: k for k in range(ns + n)},
        compiler_params=pltpu.CompilerParams(has_side_effects=_EFFECT),
    )(*srcs, *lands, *sends, *recvs, after)
    return res[ns:]


def _landing(own, chip):
    zone = lax.empty((N_CHIPS,) + own.shape, own.dtype)
    return lax.dynamic_update_slice(zone, own[None], (chip,) + (0,) * own.ndim)


def _swap_sibling(arrs, name):
    n_arr = len(arrs)

    def body(*refs):
        ins, outs = refs[:n_arr], refs[n_arr:2 * n_arr]
        send_sems, recv_sems = refs[2 * n_arr:]
        x, y, c = _mesh_pos()
        copies = [pltpu.make_async_remote_copy(
            src_ref=ins[n], dst_ref=outs[n], send_sem=send_sems.at[n], recv_sem=recv_sems.at[n],
            device_id=(x, y, 1 - c), device_id_type=MESH) for n in range(n_arr)]
        for cp in copies:
            cp.start()
        for cp in copies:
            cp.wait()

    any_spec = pl.BlockSpec(memory_space=pl.ANY)
    return pl.pallas_call(
        body, name=name,
        out_shape=tuple(jax.ShapeDtypeStruct(a.shape, a.dtype) for a in arrs),
        in_specs=[any_spec] * n_arr, out_specs=tuple([any_spec] * n_arr),
        scratch_shapes=[pltpu.SemaphoreType.DMA((n_arr,)), pltpu.SemaphoreType.DMA((n_arr,))],
    )(*arrs)


def _mod_shard(c_all, w_ada_sh, b_sh):
    d, n = w_ada_sh.shape
    bn = 512

    def body(c_ref, w_ref, b_ref, o_ref):
        cc = c_ref[...]
        ca = cc * _sigmoid(cc)
        o_ref[...] = _dot(ca, w_ref[...]) + b_ref[...]

    return pl.pallas_call(
        body, name="mod_shard", grid=(n // bn,),
        out_shape=jax.ShapeDtypeStruct((N_DEV, n), F32),
        in_specs=[_full((N_DEV, d)), pl.BlockSpec((d, bn), lambda j: (0, j)), pl.BlockSpec((1, bn), lambda j: (0, j))],
        out_specs=pl.BlockSpec((N_DEV, bn), lambda j: (0, j)),
        compiler_params=_cparams(("parallel",)),
    )(c_all, w_ada_sh, b_sh)


def _ssm_prep(lam_re, lam_im, log_step):
    g, p = lam_re.shape

    def body(lr_ref, li_ref, ls_ref, ar_ref, ai_ref, cr_ref, ci_ref):
        lr = jnp.minimum(lr_ref[...], LAMBDA_RE_MAX)
        li = li_ref[...]
        st = jnp.exp(ls_ref[...])
        mag = jnp.exp(lr * st)
        ar = mag * jnp.cos(li * st)
        ai = mag * jnp.sin(li * st)
        den = lr * lr + li * li
        nr = ar - 1.0
        ar_ref[...] = ar
        ai_ref[...] = ai
        cr_ref[...] = (nr * lr + ai * li) / den
        ci_ref[...] = (ai * lr - nr * li) / den

    sds = jax.ShapeDtypeStruct((g, p), F32)
    return pl.pallas_call(body, name="ssm_prep", out_shape=(sds,) * 4)(lam_re, lam_im, log_step)


def _ssm_blocks(bt_re, bt_im, ct_re, ct_im, coef_rows, tile_b, tile_c):
    gh, p = bt_re.shape
    gp, h = ct_re.shape
    nb = SSM_SPLIT
    cb, rb = gp // nb, gp // nb

    def body(btr, bti, ctr, cti, cf, tb_ref, tc_ref, bre_o, bim_o, cre_o, cim_o):
        j = pl.program_id(0)
        row = lax.broadcasted_iota(jnp.int32, (gh, cb), 0)
        col = lax.broadcasted_iota(jnp.int32, (gh, cb), 1) + j * cb
        mask = (row >> 4) == (col >> 6)
        cr, ci = cf[0:1, :], cf[1:2, :]
        br = _split3_dot(btr[...], tb_ref[...])
        bi = _split3_dot(bti[...], tb_ref[...])
        bre_o[...] = jnp.where(mask, br * cr - bi * ci, 0.0).astype(BF16)
        bim_o[...] = jnp.where(mask, br * ci + bi * cr, 0.0).astype(BF16)
        row2 = lax.broadcasted_iota(jnp.int32, (rb, gh), 0) + j * rb
        col2 = lax.broadcasted_iota(jnp.int32, (rb, gh), 1)
        mask2 = (row2 >> 6) == (col2 >> 4)
        cre_o[...] = jnp.where(mask2, _split3_dot(ctr[...], tc_ref[...]), 0.0).astype(BF16)
        cim_o[...] = jnp.where(mask2, _split3_dot(cti[...], tc_ref[...]), 0.0).astype(BF16)

    bspec = pl.BlockSpec((gh, cb), lambda j: (0, j))
    cspec = pl.BlockSpec((rb, gh), lambda j: (j, 0))
    cin = pl.BlockSpec((rb, h), lambda j: (j, 0))
    return pl.pallas_call(
        body, name="ssm_blocks", grid=(nb,),
        out_shape=(jax.ShapeDtypeStruct((gh, gp), BF16),) * 2 + (jax.ShapeDtypeStruct((gp, gh), BF16),) * 2,
        in_specs=[_full((gh, p)), _full((gh, p)), cin, cin, pl.BlockSpec((SUBLANES, cb), lambda j: (0, j)),
                  _full(tile_b.shape), _full(tile_c.shape)],
        out_specs=(bspec, bspec, cspec, cspec),
        compiler_params=_cparams(("parallel",)),
    )(bt_re, bt_im, ct_re, ct_im, coef_rows, tile_b, tile_c)


def _scan_consts(a_ref, reverse):
    w = a_ref.shape[1]
    ar1 = a_ref[0:1, :]
    ai1 = a_ref[1:2, :]
    if reverse:
        ai1 = -ai1
    pr, pi = [ar1], [ai1]
    for _ in range(1, SUBLANES):
        nr = pr[-1] * ar1 - pi[-1] * ai1
        ni = pr[-1] * ai1 + pi[-1] * ar1
        pr.append(nr)
        pi.append(ni)
    row = lax.broadcasted_iota(jnp.int32, (SUBLANES, w), 0)
    dist = (SUBLANES - 1 - row) if reverse else row

    def pick(vals):
        out = jnp.broadcast_to(vals[SUBLANES - 1], (SUBLANES, w))
        for r in range(SUBLANES - 1):
            out = jnp.where(dist == r, vals[r], out)
        return out

    p_r, p_i = pick(pr), pick(pi)
    steps = []
    for k in (1, 2, 4):
        steps.append((k, jnp.where(dist >= k, pr[k - 1], 0.0), jnp.where(dist >= k, pi[k - 1], 0.0)))
    a8 = (jnp.broadcast_to(pr[SUBLANES - 1], (SUBLANES, w)), jnp.broadcast_to(pi[SUBLANES - 1], (SUBLANES, w)))
    return row, p_r, p_i, steps, a8


def _scan_tile(xr, xi, cr, ci, consts, reverse):
    row, p_r, p_i, steps, (a8r, a8i) = consts
    for k, s_r, s_i in steps:
        sh = (SUBLANES - k) if reverse else k
        qr = pltpu.roll(xr, sh, 0)
        qi = pltpu.roll(xi, sh, 0)
        xr, xi = xr + s_r * qr - s_i * qi, xi + s_r * qi + s_i * qr
    outr = xr + p_r * cr - p_i * ci
    outi = xi + p_r * ci + p_i * cr
    e = 0 if reverse else SUBLANES - 1
    er = jnp.broadcast_to(xr[e:e + 1, :], xr.shape)
    ei = jnp.broadcast_to(xi[e:e + 1, :], xi.shape)
    return outr, outi, er + a8r * cr - a8i * ci, ei + a8r * ci + a8i * cr


def _scan_fwd(a_rows, bu_re, bu_im):
    t, n = bu_re.shape
    tb, w = _blk(t, TB_SCAN), W_SCAN
    ntile = tb // SUBLANES

    def body(a_ref, br_ref, bi_ref, sr_ref, si_ref, car, cai):
        @pl.when(pl.program_id(1) == 0)
        def _():
            car[...] = jnp.zeros(car.shape, F32)
            cai[...] = jnp.zeros(cai.shape, F32)
        consts = _scan_consts(a_ref, False)

        def pair(i, carry):
            o = pl.multiple_of(i * BF16_ROWS, BF16_ROWS)
            outs = []
            for h in range(2):
                rows = pl.ds(o + h * SUBLANES, SUBLANES)
                outr, outi, ncr, nci = _scan_tile(br_ref[rows, :], bi_ref[rows, :], carry[0], carry[1], consts, False)
                outs.append((outr, outi))
                carry = (ncr, nci)
            sr_ref[pl.ds(o, BF16_ROWS), :] = jnp.concatenate([outs[0][0], outs[1][0]], axis=0).astype(BF16)
            si_ref[pl.ds(o, BF16_ROWS), :] = jnp.concatenate([outs[0][1], outs[1][1]], axis=0).astype(BF16)
            return carry

        def pairs(i, carry):
            for s in range(SCAN_UNROLL // 2):
                carry = pair(i * (SCAN_UNROLL // 2) + s, carry)
            return carry

        cr, ci = lax.fori_loop(0, ntile // SCAN_UNROLL, pairs, (car[...], cai[...]))
        car[...] = cr
        cai[...] = ci

    spec = pl.BlockSpec((tb, w), lambda s, k: (k, s))
    sds = jax.ShapeDtypeStruct((t, n), BF16)
    return pl.pallas_call(
        body, name="scan_fwd", grid=(n // w, t // tb), out_shape=(sds, sds),
        in_specs=[pl.BlockSpec((SUBLANES, w), lambda s, k: (0, s)), spec, spec], out_specs=(spec, spec),
        scratch_shapes=[pltpu.VMEM((SUBLANES, w), F32), pltpu.VMEM((SUBLANES, w), F32)],
        compiler_params=_cparams(("parallel", "arbitrary"), VMEM_MID),
    )(a_rows, bu_re, bu_im)


def _scan_bwd(a_rows, g_re, g_im, s_re, s_im):
    t, n = g_re.shape
    tb, w = _blk(t, TB_SCAN), W_SCAN
    ntile = tb // SUBLANES
    npair = tb // BF16_ROWS
    nt = t // tb

    def body(a_ref, gr_ref, gi_ref, sr_ref, si_ref, or_ref, oi_ref, gar_ref, gai_ref, car, cai):
        @pl.when(pl.program_id(1) == 0)
        def _():
            car[...] = jnp.zeros(car.shape, F32)
            cai[...] = jnp.zeros(cai.shape, F32)
            gar_ref[...] = jnp.zeros(gar_ref.shape, F32)
            gai_ref[...] = jnp.zeros(gai_ref.shape, F32)
        consts = _scan_consts(a_ref, True)
        row = consts[0]

        def pair(i, carry):
            cr, ci, accr, acci = carry
            o = pl.multiple_of((npair - 1 - i) * BF16_ROWS, BF16_ROWS)
            s_r = sr_ref[pl.ds(o, BF16_ROWS), :].astype(F32)
            s_i = si_ref[pl.ds(o, BF16_ROWS), :].astype(F32)
            outs = [None, None]
            for h in (1, 0):
                rows = pl.ds(o + h * SUBLANES, SUBLANES)
                outr, outi, ncr, nci = _scan_tile(gr_ref[rows, :], gi_ref[rows, :], cr, ci, consts, True)
                outs[h] = (outr, outi)
                gnr = jnp.where(row == SUBLANES - 1, cr, pltpu.roll(outr, SUBLANES - 1, 0))
                gni = jnp.where(row == SUBLANES - 1, ci, pltpu.roll(outi, SUBLANES - 1, 0))
                sr = s_r[h * SUBLANES:(h + 1) * SUBLANES, :]
                si = s_i[h * SUBLANES:(h + 1) * SUBLANES, :]
                accr, acci = accr + sr * gnr + si * gni, acci + sr * gni - si * gnr
                cr, ci = ncr, nci
            or_ref[pl.ds(o, BF16_ROWS), :] = jnp.concatenate([outs[0][0], outs[1][0]], axis=0).astype(BF16)
            oi_ref[pl.ds(o, BF16_ROWS), :] = jnp.concatenate([outs[0][1], outs[1][1]], axis=0).astype(BF16)
            return cr, ci, accr, acci

        def pairs(i, carry):
            for s in range(SCAN_UNROLL // 2):
                carry = pair(i * (SCAN_UNROLL // 2) + s, carry)
            return carry

        cr, ci, accr, acci = lax.fori_loop(0, ntile // SCAN_UNROLL, pairs,
                                           (car[...], cai[...], gar_ref[...], gai_ref[...]))
        car[...] = cr
        cai[...] = ci
        gar_ref[...] = accr
        gai_ref[...] = acci

    spec = pl.BlockSpec((tb, w), lambda s, k: (nt - 1 - k, s))
    aspec = pl.BlockSpec((SUBLANES, w), lambda s, k: (0, s))
    sds = jax.ShapeDtypeStruct((t, n), BF16)
    asds = jax.ShapeDtypeStruct((SUBLANES, n), F32)
    return pl.pallas_call(
        body, name="scan_bwd", grid=(n // w, nt), out_shape=(sds, sds, asds, asds),
        in_specs=[aspec, spec, spec, spec, spec], out_specs=(spec, spec, aspec, aspec),
        scratch_shapes=[pltpu.VMEM((SUBLANES, w), F32), pltpu.VMEM((SUBLANES, w), F32)],
        compiler_params=_cparams(("parallel", "arbitrary"), VMEM_MID),
    )(a_rows, g_re, g_im, s_re, s_im)


def _mix_in(x, vec, w_in_st, b_re, b_im):
    t, d = x.shape
    ns, _, nc = w_in_st.shape
    dssm, nstate = b_re.shape
    du, ds = dssm // SSM_SPLIT, nstate // SSM_SPLIT
    tb = _blk(t, TB_MIX)

    def body(x_ref, vec_ref, w_ref, bre_ref, bim_ref, proj_ref, bur_ref, bui_ref, h1_ref):
        xv = x_ref[...]
        r = lax.rsqrt(_rowmean(xv * xv) + EPS)
        h = xv * r * vec_ref[0:1, :] * vec_ref[1:2, :] + vec_ref[2:3, :]
        hb = h.astype(BF16)
        h1_ref[...] = hb
        u = None
        for j in range(ns):
            pj = jnp.dot(hb, w_ref[j], preferred_element_type=F32)
            proj_ref[:, j * nc:(j + 1) * nc] = pj
            if j == 0:
                u = pj
        ub = u.astype(BF16)
        for q in range(SSM_SPLIT):
            rq, cq = slice(q * du, (q + 1) * du), slice(q * ds, (q + 1) * ds)
            bur_ref[:, cq] = jnp.dot(ub[:, rq], bre_ref[rq, cq], preferred_element_type=F32)
            bui_ref[:, cq] = jnp.dot(ub[:, rq], bim_ref[rq, cq], preferred_element_type=F32)

    return pl.pallas_call(
        body, name="mix_in", grid=(t // tb,),
        out_shape=(jax.ShapeDtypeStruct((t, ns * nc), F32), jax.ShapeDtypeStruct((t, nstate), F32),
                   jax.ShapeDtypeStruct((t, nstate), F32), jax.ShapeDtypeStruct((t, d), BF16)),
        in_specs=[_rows(tb, d), _full((SUBLANES, d)), _resident(w_in_st.shape), _resident(b_re.shape),
                  _resident(b_im.shape)],
        out_specs=(_rows(tb, ns * nc), _rows(tb, nstate), _rows(tb, nstate), _rows(tb, d)),
        compiler_params=_cparams(("parallel",), VMEM_BIG),
    )(x, vec, w_in_st, b_re, b_im)


def _head_ms(y, h_ref):
    return _split_dot(y * y, h_ref[...])


def _conv3(x, halo, w_ref):
    return w_ref[0:1, :] * _shift_down(x, halo, 2) + w_ref[1:2, :] * _shift_down(x, halo, 1) + w_ref[2:3, :] * x


def _mix_out(x, proj, s_re, s_im, c_re, c_im, v512, convw, glu_w, h16, h64, w_out, vd):
    t, d = x.shape
    dh = c_re.shape[1]
    nstate = s_re.shape[1]
    du, ds = dh // SSM_SPLIT, nstate // SSM_SPLIT
    tb = _blk(t, TB_MIX)

    def body(x_ref, u_ref, bg_ref, cg_ref, v_ref, cgh_ref, vh_ref, sr_ref, si_ref, cre_ref, cim_ref, p_ref,
             cw_ref, gw_ref, h16_ref, h64_ref, wo_ref, vd_ref, y1_ref, o_ref, x2_ref):
        i = pl.program_id(0)
        u = u_ref[...]
        ys = []
        for q in range(SSM_SPLIT):
            rq, cq = slice(q * ds, (q + 1) * ds), slice(q * du, (q + 1) * du)
            ys.append(_dot(sr_ref[:, rq], cre_ref[rq, cq]) - _dot(si_ref[:, rq], cim_ref[rq, cq]))
        ys = jnp.concatenate(ys, axis=1)
        y1 = ys + p_ref[0:1, :] * u
        y1_ref[...] = y1
        z = _gelu(y1)
        q = _dot(z, gw_ref[...]) + p_ref[1:2, :]
        ya = z * _sigmoid(q)
        na = ya * lax.rsqrt(_head_ms(ya, h16_ref) + EPS) * p_ref[2:3, :]
        cv = cg_ref[...] * v_ref[...]
        cvh = jnp.where(i > 0, cgh_ref[...] * vh_ref[...], 0.0)
        yb = bg_ref[...] * _conv3(cv, cvh, cw_ref)
        nb = yb * lax.rsqrt(_head_ms(yb, h64_ref) + EPS) * p_ref[3:4, :]
        o = _dot(na, wo_ref[0:dh, :]) + _dot(nb, wo_ref[dh:2 * dh, :])
        o_ref[...] = o
        on = o * lax.rsqrt(_rowmean(o * o) + EPS) * vd_ref[0:1, :]
        x2_ref[...] = x_ref[...] + vd_ref[1:2, :] * on

    return pl.pallas_call(
        body, name="mix_out", grid=(t // tb,),
        out_shape=(jax.ShapeDtypeStruct((t, dh), F32), jax.ShapeDtypeStruct((t, d), F32),
                   jax.ShapeDtypeStruct((t, d), F32)),
        in_specs=[_rows(tb, d), _rows(tb, dh, 0), _rows(tb, dh, 1), _rows(tb, dh, 2), _rows(tb, dh, 3),
                  _halo_prev(tb, dh, 2), _halo_prev(tb, dh, 3), _rows(tb, nstate), _rows(tb, nstate),
                  _full(c_re.shape), _full(c_im.shape), _full(v512.shape), _full(convw.shape), _full(glu_w.shape),
                  _full(h16.shape), _full(h64.shape), _full(w_out.shape), _full(vd.shape)],
        out_specs=(_rows(tb, dh), _rows(tb, d), _rows(tb, d)),
        compiler_params=_cparams(("parallel",), VMEM_BIG),
    )(x, proj, proj, proj, proj, proj, proj, s_re, s_im, c_re, c_im, v512, convw, glu_w, h16, h64, w_out, vd)


def _ffn_up(x2, vec, w_up_st):
    t, d = x2.shape
    ns, _, nc = w_up_st.shape
    tb = _blk(t, TB_FFN)

    def body(x_ref, vec_ref, w_ref, up_ref, h2_ref):
        xv = x_ref[...]
        r = lax.rsqrt(_rowmean(xv * xv) + EPS)
        h = xv * r * vec_ref[0:1, :] * vec_ref[1:2, :] + vec_ref[2:3, :]
        hb = h.astype(BF16)
        h2_ref[...] = hb
        for j in range(ns):
            up_ref[:, j * nc:(j + 1) * nc] = jnp.dot(hb, w_ref[j], preferred_element_type=F32)

    return pl.pallas_call(
        body, name="ffn_up", grid=(t // tb,),
        out_shape=(jax.ShapeDtypeStruct((t, ns * nc), F32), jax.ShapeDtypeStruct((t, d), BF16)),
        in_specs=[_rows(tb, d), _full((SUBLANES, d)), _resident(w_up_st.shape)],
        out_specs=(_rows(tb, ns * nc), _rows(tb, d)),
        compiler_params=_cparams(("parallel",), VMEM_BIG),
    )(x2, vec, w_up_st)


def _ffn_down(up, fw, w_down, w_down_t, x2, tgt, vd):
    t, nh = up.shape
    dff, d = w_down.shape
    tb = _blk(t, TB_FFN)
    inv_d = 1.0 / d

    def body(up_ref, uph_ref, fw_ref, wd_ref, wdt_ref, x2_ref, tgt_ref, vd_ref,
             act_ref, ddn_ref, dout_ref, dhid_ref, vec_ref, loss_ref, a_s, vv_s, sg_s):
        i = pl.program_id(0)

        def conv_cols(sl):
            x = up_ref[:, sl]
            halo = jnp.where(i > 0, uph_ref[:, sl], 0.0)
            return (fw_ref[0:1, sl] * _shift_down(x, halo, 2) + fw_ref[1:2, sl] * _shift_down(x, halo, 1)
                    + fw_ref[2:3, sl] * x)

        dn = None
        for o in range(0, dff, CW_FFN):
            sl = slice(o, o + CW_FFN)
            a = conv_cols(sl)
            vv = conv_cols(slice(dff + o, dff + o + CW_FFN))
            sg = _sigmoid(a)
            a_s[:, sl] = a
            vv_s[:, sl] = vv
            sg_s[:, sl] = sg
            actb = (a * sg * vv).astype(BF16)
            act_ref[:, sl] = actb
            pj = lax.dot_general(actb, wdt_ref[:, sl], (((1,), (1,)), ((), ())), preferred_element_type=F32)
            dn = pj if dn is None else dn + pj
        r3 = lax.rsqrt(_rowmean(dn * dn) + EPS)
        xn = dn * r3
        g = vd_ref[0:1, :]
        gt2 = vd_ref[1:2, :]
        dnn = xn * g
        diff = x2_ref[...] + gt2 * dnn - tgt_ref[...]
        part = 0.5 * inv_d * jnp.sum(diff * diff)

        @pl.when(i == 0)
        def _():
            loss_ref[...] = jnp.zeros(loss_ref.shape, F32)
        loss_ref[...] += part
        dout = diff * inv_d
        dout_ref[...] = dout
        ddnn = dout * gt2
        _acc_rows(vec_ref, i == 0, [_colsum(dout * dnn), _colsum(ddnn * xn)])
        dxn = ddnn * g
        ddn = r3 * (dxn - xn * _rowmean(dxn * xn))
        ddnb = ddn.astype(BF16)
        ddn_ref[...] = ddnb
        for o in range(0, dff, CW_FFN):
            sl = slice(o, o + CW_FFN)
            dact = lax.dot_general(ddnb, wd_ref[sl, :], (((1,), (1,)), ((), ())), preferred_element_type=F32)
            a, vv, sg = a_s[:, sl], vv_s[:, sl], sg_s[:, sl]
            dhid_ref[:, sl] = (dact * vv * sg * (1.0 + a * (1.0 - sg))).astype(BF16)
            dhid_ref[:, dff + o:dff + o + CW_FFN] = (dact * (a * sg)).astype(BF16)

    return pl.pallas_call(
        body, name="ffn_down", grid=(t // tb,),
        scratch_shapes=[pltpu.VMEM((tb, dff), F32)] * 3,
        out_shape=(jax.ShapeDtypeStruct((t, dff), BF16), jax.ShapeDtypeStruct((t, d), BF16),
                   jax.ShapeDtypeStruct((t, d), F32), jax.ShapeDtypeStruct((t, nh), BF16),
                   jax.ShapeDtypeStruct((SUBLANES, d), F32), jax.ShapeDtypeStruct((SUBLANES, 128), F32)),
        in_specs=[_rows(tb, nh), _halo_prev(tb, nh), _full(fw.shape), _resident(w_down.shape),
                  _resident(w_down_t.shape), _rows(tb, d),
                  _rows(tb, d), _full(vd.shape)],
        out_specs=(_rows(tb, dff), _rows(tb, d), _rows(tb, d), _rows(tb, nh), _full((SUBLANES, d)),
                   _full((SUBLANES, 128))),
        compiler_params=_cparams(("arbitrary",), VMEM_BIG),
    )(up, up, fw, w_down, w_down_t, x2, tgt, vd)


def _ffn_up_bwd(dhid, up, fw, x2, dout, vec, w_up_st):
    t, nh = dhid.shape
    d = x2.shape[1]
    ns, _, nc = w_up_st.shape
    tb = _blk(t, TB_FFN)
    nblk = t // tb
    cw = 128

    def body(dh_ref, dhn_ref, up_ref, fw_ref, x2_ref, dout_ref, vec_ref, w_ref,
             dx2_ref, dup_ref, vp_ref, df_ref):
        i = pl.program_id(0)

        @pl.when(i == 0)
        def _():
            df_ref[...] = jnp.zeros(df_ref.shape, F32)
        dh2 = None
        for j in range(ns):
            for o in range(j * nc, (j + 1) * nc, cw):
                sl = slice(o, o + cw)
                dh = dh_ref[:, sl].astype(F32)
                dhn = jnp.where(i < nblk - 1, dhn_ref[:, sl].astype(F32), 0.0)
                dh1 = _shift_up(dh, dhn, 1)
                dh2s = _shift_up(dh, dhn, 2)
                dup_ref[:, sl] = (fw_ref[2:3, sl] * dh + fw_ref[1:2, sl] * dh1 + fw_ref[0:1, sl] * dh2s).astype(BF16)
                up_v = up_ref[:, sl]
                df_ref[0:1, sl] += _colsum(dh2s * up_v)
                df_ref[1:2, sl] += _colsum(dh1 * up_v)
                df_ref[2:3, sl] += _colsum(dh * up_v)
            pj = lax.dot_general(dup_ref[:, j * nc:(j + 1) * nc], w_ref[j], (((1,), (1,)), ((), ())),
                                 preferred_element_type=F32)
            dh2 = pj if dh2 is None else dh2 + pj
        xv = x2_ref[...]
        r = lax.rsqrt(_rowmean(xv * xv) + EPS)
        xn = xv * r
        g = vec_ref[0:1, :]
        hg = xn * g
        dhg = dh2 * vec_ref[1:2, :]
        _acc_rows(vp_ref, i == 0, [_colsum(dh2), _colsum(dh2 * hg), _colsum(dhg * xn)])
        dxn = dhg * g
        dx2_ref[...] = dout_ref[...] + r * (dxn - xn * _rowmean(dxn * xn))

    return pl.pallas_call(
        body, name="ffn_up_bwd", grid=(nblk,),
        out_shape=(jax.ShapeDtypeStruct((t, d), F32), jax.ShapeDtypeStruct((t, nh), BF16),
                   jax.ShapeDtypeStruct((SUBLANES, d), F32), jax.ShapeDtypeStruct((SUBLANES, nh), F32)),
        in_specs=[_rows(tb, nh), _halo_next(tb, nh, t, rows=BF16_ROWS), _rows(tb, nh), _full(fw.shape),
                  _rows(tb, d), _rows(tb, d), _full(vec.shape), _resident(w_up_st.shape)],
        out_specs=(_rows(tb, d), _rows(tb, nh), _full((SUBLANES, d)), _full((SUBLANES, nh))),
        compiler_params=_cparams(("arbitrary",), VMEM_BIG),
    )(dhid, dhid, up, fw, x2, dout, vec, w_up_st)


def _mix_out_bwd(dx2, o, y1, proj, s_re, s_im, c_re, c_im, v512, convw, glu_w, h16, h64, w_out, vd):
    t, d = dx2.shape
    dh = y1.shape[1]
    nstate = c_re.shape[0]
    du, ds = dh // SSM_SPLIT, nstate // SSM_SPLIT
    tb = _blk(t, TB_MIX)

    def body(dx2_ref, o_ref, y1_ref, u_ref, bg_ref, cg_ref, v_ref, cgh_ref, vh_ref, cre_ref, cim_ref, p_ref,
             cw_ref, gw_ref, h16_ref, h64_ref, wo_ref, vd_ref, sr_ref, si_ref,
             do_ref, ycat_ref, z_ref, dq_ref, dy1_ref, gr_ref, gi_ref, dcc_ref, dbg_ref, vpd_ref, vp5_ref,
             dcr_ref, dci_ref):
        i = pl.program_id(0)
        first = i == 0

        @pl.when(first)
        def _():
            dcr_ref[...] = jnp.zeros(dcr_ref.shape, F32)
            dci_ref[...] = jnp.zeros(dci_ref.shape, F32)
        ov = o_ref[...]
        ro = lax.rsqrt(_rowmean(ov * ov) + EPS)
        on_ = ov * ro
        g = vd_ref[0:1, :]
        dx2v = dx2_ref[...]
        don = dx2v * vd_ref[1:2, :]
        _acc_rows(vpd_ref, first, [_colsum(dx2v * on_ * g), _colsum(don * on_)])
        dxn = don * g
        dob = (ro * (dxn - on_ * _rowmean(dxn * on_))).astype(BF16)
        do_ref[...] = dob
        dyc_a =lax.dot_general(dob, wo_ref[0:dh, :], (((1,), (1,)), ((), ())), preferred_element_type=F32)
        dyc_b = lax.dot_general(dob, wo_ref[dh:2 * dh, :], (((1,), (1,)), ((), ())), preferred_element_type=F32)
        y1v = y1_ref[...]
        u = u_ref[...]
        z = _gelu(y1v)
        zb = z.astype(BF16)
        sg = _sigmoid(jnp.dot(zb, gw_ref[...], preferred_element_type=F32) + p_ref[1:2, :])
        ya = z * sg
        ra = lax.rsqrt(_head_ms(ya, h16_ref) + EPS)
        yan = ya * ra
        ga = p_ref[2:3, :]
        ycat_ref[:, 0:dh] = (yan * ga).astype(BF16)
        dyn = dyc_a * ga
        dya = ra * (dyn - yan * _split_dot(dyn * yan, h16_ref[...]))
        dq = dya * z * sg * (1.0 - sg)
        dqb = dq.astype(BF16)
        z_ref[...] = zb
        dq_ref[...] = dqb
        dz = dya * sg + lax.dot_general(dqb, gw_ref[...], (((1,), (1,)), ((), ())), preferred_element_type=F32)
        dy1 = dz * _gelu_grad(y1v)
        dy1_ref[...] = dy1
        dy1b = dy1.astype(BF16)
        for q in range(SSM_SPLIT):
            rq, cq = slice(q * ds, (q + 1) * ds), slice(q * du, (q + 1) * du)
            gr_ref[:, rq] = lax.dot_general(dy1b[:, cq], cre_ref[rq, cq], (((1,), (1,)), ((), ())),
                                            preferred_element_type=F32)
            gi_ref[:, rq] = -lax.dot_general(dy1b[:, cq], cim_ref[rq, cq], (((1,), (1,)), ((), ())),
                                             preferred_element_type=F32)
            dcr_ref[rq, :] += _dot_tn(sr_ref[:, rq], dy1b[:, cq])
            dci_ref[rq, :] += _dot_tn(si_ref[:, rq], dy1b[:, cq])
        bg = bg_ref[...]
        cv = cg_ref[...] * v_ref[...]
        cvh = jnp.where(i > 0, cgh_ref[...] * vh_ref[...], 0.0)
        cv1 = _shift_down(cv, cvh, 1)
        cv2 = _shift_down(cv, cvh, 2)
        cc = cw_ref[0:1, :] * cv2 + cw_ref[1:2, :] * cv1 + cw_ref[2:3, :] * cv
        yb = bg * cc
        rb = lax.rsqrt(_head_ms(yb, h64_ref) + EPS)
        ybn = yb * rb
        gb = p_ref[3:4, :]
        ycat_ref[:, dh:2 * dh] = (ybn * gb).astype(BF16)
        dynb = dyc_b * gb
        dyb = rb * (dynb - ybn * _split_dot(dynb * ybn, h64_ref[...]))
        dcc = dyb * bg
        dbg_ref[...] = dyb * cc
        dcc_ref[...] = dcc
        _acc_rows(vp5_ref, first, [_colsum(dyc_a * yan), _colsum(dyc_b * ybn), _colsum(dq), _colsum(dy1 * u),
                                   _colsum(dcc * cv2), _colsum(dcc * cv1), _colsum(dcc * cv)])

    return pl.pallas_call(
        body, name="mix_out_bwd", grid=(t // tb,),
        out_shape=(jax.ShapeDtypeStruct((t, d), BF16), jax.ShapeDtypeStruct((t, 2 * dh), BF16),
                   jax.ShapeDtypeStruct((t, dh), BF16), jax.ShapeDtypeStruct((t, dh), BF16),
                   jax.ShapeDtypeStruct((t, dh), F32), jax.ShapeDtypeStruct((t, nstate), F32),
                   jax.ShapeDtypeStruct((t, nstate), F32), jax.ShapeDtypeStruct((t, dh), F32),
                   jax.ShapeDtypeStruct((t, dh), F32), jax.ShapeDtypeStruct((SUBLANES, d), F32),
                   jax.ShapeDtypeStruct((SUBLANES, dh), F32), jax.ShapeDtypeStruct((nstate, du), F32),
                   jax.ShapeDtypeStruct((nstate, du), F32)),
        in_specs=[_rows(tb, d), _rows(tb, d), _rows(tb, dh), _rows(tb, dh, 0), _rows(tb, dh, 1), _rows(tb, dh, 2),
                  _rows(tb, dh, 3), _halo_prev(tb, dh, 2), _halo_prev(tb, dh, 3), _resident(c_re.shape),
                  _resident(c_im.shape), _full(v512.shape), _full(convw.shape), _resident(glu_w.shape),
                  _resident(h16.shape), _resident(h64.shape), _resident(w_out.shape), _full(vd.shape),
                  _rows(tb, nstate), _rows(tb, nstate)],
        out_specs=(_rows(tb, d), _rows(tb, 2 * dh), _rows(tb, dh), _rows(tb, dh), _rows(tb, dh), _rows(tb, nstate),
                   _rows(tb, nstate), _rows(tb, dh), _rows(tb, dh), _full((SUBLANES, d)), _full((SUBLANES, dh)),
                   _full((nstate, du)), _full((nstate, du))),
        compiler_params=_cparams(("arbitrary",), VMEM_BIG),
    )(dx2, o, y1, proj, proj, proj, proj, proj, proj, c_re, c_im, v512, convw, glu_w, h16, h64, w_out, vd,
      s_re, s_im)


def _mix_in_bwd(gt_re, gt_im, b_re, b_im, dy1, dcc, dbg, proj, x, dx2, vec, v512, convw, w_in_st):
    t, d = x.shape
    dh = dy1.shape[1]
    nstate = gt_re.shape[1]
    du_w, ds = dh // SSM_SPLIT, nstate // SSM_SPLIT
    ns, _, nc = w_in_st.shape
    tb = _blk(t, TB_MIX)
    nblk = t // tb

    def body(gr_ref, gi_ref, bre_ref, bim_ref, dy1_ref, dcc_ref, dccn_ref, dbg_ref, u_ref, cg_ref, v_ref, x_ref,
             dx2_ref, vec_ref, p_ref, cw_ref, w_ref, gx_ref, dproj_ref, vp_ref, dbr_ref, dbi_ref):
        i = pl.program_id(0)

        @pl.when(i == 0)
        def _():
            dbr_ref[...] = jnp.zeros(dbr_ref.shape, F32)
            dbi_ref[...] = jnp.zeros(dbi_ref.shape, F32)
        ub = u_ref[...].astype(BF16)
        du = []
        for q in range(SSM_SPLIT):
            rq, cq = slice(q * du_w, (q + 1) * du_w), slice(q * ds, (q + 1) * ds)
            du.append(lax.dot_general(gr_ref[:, cq].astype(BF16), bre_ref[rq, cq], (((1,), (1,)), ((), ())),
                                      preferred_element_type=F32)
                      + lax.dot_general(gi_ref[:, cq].astype(BF16), bim_ref[rq, cq], (((1,), (1,)), ((), ())),
                                        preferred_element_type=F32))
            dbr_ref[rq, :] += _dot_tn(ub[:, rq], gr_ref[:, cq])
            dbi_ref[rq, :] += _dot_tn(ub[:, rq], gi_ref[:, cq])
        du = dy1_ref[...] * p_ref[0:1, :] + jnp.concatenate(du, axis=1)
        dcc = dcc_ref[...]
        dccn = jnp.where(i < nblk - 1, dccn_ref[...], 0.0)
        dcv = (cw_ref[2:3, :] * dcc + cw_ref[1:2, :] * _shift_up(dcc, dccn, 1)
               + cw_ref[0:1, :] * _shift_up(dcc, dccn, 2))
        parts = [du, dbg_ref[...], dcv * v_ref[...], dcv * cg_ref[...]]
        xv = x_ref[...]
        r = lax.rsqrt(_rowmean(xv * xv) + EPS)
        xn = xv * r
        g = vec_ref[0:1, :]
        hg = xn * g
        dh1 = None
        for j in range(ns):
            pb = parts[j].astype(BF16)
            dproj_ref[:, j * nc:(j + 1) * nc] = pb
            pj =lax.dot_general(pb, w_ref[j], (((1,), (1,)), ((), ())), preferred_element_type=F32)
            dh1 = pj if dh1 is None else dh1 + pj
        dhg = dh1 * vec_ref[1:2, :]
        _acc_rows(vp_ref, i == 0, [_colsum(dh1), _colsum(dh1 * hg), _colsum(dhg * xn)])
        dxn = dhg * g
        gx_ref[...] = dx2_ref[...] + r * (dxn - xn * _rowmean(dxn * xn))

    assert nc == dh and ns == 4
    return pl.pallas_call(
        body, name="mix_in_bwd", grid=(nblk,),
        out_shape=(jax.ShapeDtypeStruct((t, d), F32), jax.ShapeDtypeStruct((t, ns * nc), BF16),
                   jax.ShapeDtypeStruct((SUBLANES, d), F32), jax.ShapeDtypeStruct((dh, ds), F32),
                   jax.ShapeDtypeStruct((dh, ds), F32)),
        in_specs=[_rows(tb, nstate), _rows(tb, nstate), _resident(b_re.shape), _resident(b_im.shape), _rows(tb, dh),
                  _rows(tb, dh), _halo_next(tb, dh, t), _rows(tb, dh), _rows(tb, dh, 0), _rows(tb, dh, 2),
                  _rows(tb, dh, 3), _rows(tb, d), _rows(tb, d), _full(vec.shape), _full(v512.shape),
                  _full(convw.shape), _resident(w_in_st.shape)],
        out_specs=(_rows(tb, d), _rows(tb, ns * nc), _full((SUBLANES, d)), _full((dh, ds)), _full((dh, ds))),
        compiler_params=_cparams(("arbitrary",), VMEM_BIG),
    )(gt_re, gt_im, b_re, b_im, dy1, dcc, dcc, dbg, proj, proj, proj, x, dx2, vec, v512, convw, w_in_st)


def _matmul_tn(a, b, m, bn, out_dtype, name, diag=False, bt=TB_TN, after=None):
    t = a.shape[0]
    n = b.shape[1]
    bt = _blk(t, bt)
    nk = t // bt
    extra = [] if after is None else [after]
    a_map = (lambda j, k: (k, j)) if diag else (lambda j, k: (k, 0))

    def body(a_ref, b_ref, *rest):
        o_ref, acc_ref = rest[-2:]
        k = pl.program_id(1)

        @pl.when(k == 0)
        def _():
            acc_ref[...] = jnp.zeros(acc_ref.shape, F32)
        acc_ref[...] += _dot_tn(a_ref[...], b_ref[...])

        @pl.when(k == nk - 1)
        def _():
            o_ref[...] = acc_ref[...].astype(out_dtype)

    return pl.pallas_call(
        body, name=name, grid=(n // bn, nk),
        out_shape=jax.ShapeDtypeStruct((n // bn, m, bn), out_dtype),
        in_specs=[pl.BlockSpec((bt, m), a_map), pl.BlockSpec((bt, bn), lambda j, k: (k, j))]
        + [pl.BlockSpec(memory_space=pl.ANY)] * len(extra),
        out_specs=pl.BlockSpec((None, m, bn), lambda j, k: (j, 0, 0)),
        scratch_shapes=[pltpu.VMEM((m, bn), F32)],
        compiler_params=_cparams(("parallel", "arbitrary"), VMEM_BIG),
    )(a, b, *extra)


def _ssm_bgrad(d_bre, d_bim, bt_re, bt_im, rows_in, fold, tile_b):
    gh, cb = d_bre.shape
    nb = SSM_SPLIT
    rb = gh // nb
    gp = nb * cb
    p = fold.shape[1]

    def body(dr_ref, di_ref, br_ref, bi_ref, rin_ref, f_ref, tb_ref, dbr_ref, dbi_ref, rout_ref):
        row = lax.broadcasted_iota(jnp.int32, (rb, cb), 0)
        col = lax.broadcasted_iota(jnp.int32, (rb, cb), 1)
        mask = (row >> 4) == (col >> 6)
        gr = jnp.where(mask, dr_ref[...], 0.0)
        gi = jnp.where(mask, di_ref[...], 0.0)
        cr, ci = rin_ref[0:1, :], rin_ref[1:2, :]
        dbr_ref[...] = _split3_dot(cr * gr + ci * gi, f_ref[...])
        dbi_ref[...] = _split3_dot(cr * gi - ci * gr, f_ref[...])
        br = _split3_dot(br_ref[...], tb_ref[...])
        bi = _split3_dot(bi_ref[...], tb_ref[...])
        rout_ref[...] = jnp.zeros(rout_ref.shape, F32)
        rout_ref[0:1, :] = _colsum(br * gr + bi * gi)
        rout_ref[1:2, :] = _colsum(br * gi - bi * gr)

    dspec = pl.BlockSpec((rb, cb), lambda j: (j, 0))
    rspec = pl.BlockSpec((SUBLANES, cb), lambda j: (0, j))
    ospec = pl.BlockSpec((rb, p), lambda j: (j, 0))
    return pl.pallas_call(
        body, name="ssm_bgrad", grid=(nb,),
        out_shape=(jax.ShapeDtypeStruct((gh, p), F32), jax.ShapeDtypeStruct((gh, p), F32),
                   jax.ShapeDtypeStruct((SUBLANES, gp), F32)),
        in_specs=[dspec, dspec, ospec, ospec, rspec, _full(fold.shape), _full(tile_b.shape)],
        out_specs=(ospec, ospec, rspec),
        compiler_params=_cparams(("parallel",)),
    )(d_bre, d_bim, bt_re, bt_im, rows_in, fold, tile_b)


def _ssm_cgrad(d_cre, d_cim, fold):
    gp, cb = d_cre.shape
    nb = SSM_SPLIT
    rb = gp // nb
    h = fold.shape[1]

    def body(dr_ref, di_ref, f_ref, cr_ref, ci_ref):
        row = lax.broadcasted_iota(jnp.int32, (rb, cb), 0)
        col = lax.broadcasted_iota(jnp.int32, (rb, cb), 1)
        mask = (row >> 6) == (col >> 4)
        cr_ref[...] = _split3_dot(jnp.where(mask, dr_ref[...], 0.0), f_ref[...])
        ci_ref[...] = -_split3_dot(jnp.where(mask, di_ref[...], 0.0), f_ref[...])

    cspec = pl.BlockSpec((rb, cb), lambda j: (j, 0))
    ospec = pl.BlockSpec((rb, h), lambda j: (j, 0))
    return pl.pallas_call(
        body, name="ssm_cgrad", grid=(nb,),
        out_shape=(jax.ShapeDtypeStruct((gp, h), F32),) * 2,
        in_specs=[cspec, cspec, _full(fold.shape)], out_specs=(ospec, ospec),
        compiler_params=_cparams(("parallel",)),
    )(d_cre, d_cim, fold)


def _ssm_lamgrad(lam_re, lam_im, log_step, abar_re, abar_im, coef_re, coef_im, gc_re, gc_im, ga_re, ga_im):
    g, p = lam_re.shape

    def body(lr_ref, li_ref, ls_ref, ar_ref, ai_ref, cr_ref, ci_ref, gcr_ref, gci_ref, gar_ref, gai_ref,
             dlr_ref, dli_ref, dls_ref):
        lam_raw = lr_ref[...]
        lr = jnp.minimum(lam_raw, LAMBDA_RE_MAX)
        li = li_ref[...]
        st = jnp.exp(ls_ref[...])
        den = lr * lr + li * li
        gcr, gci = gcr_ref[...], gci_ref[...]
        gab_r = gar_ref[...] + (lr * gcr - li * gci) / den
        gab_i = gai_ref[...] + (lr * gci + li * gcr) / den
        cr, ci = cr_ref[...], ci_ref[...]
        wr = -(cr * lr + ci * li) / den
        wi = -(ci * lr - cr * li) / den
        gl_r = wr * gcr + wi * gci
        gl_i = wr * gci - wi * gcr
        ar, ai = ar_ref[...], ai_ref[...]
        gw_r = ar * gab_r + ai * gab_i
        gw_i = ar * gab_i - ai * gab_r
        gl_r = gl_r + st * gw_r
        gl_i = gl_i + st * gw_i
        pass_through = jnp.where(lam_raw < LAMBDA_RE_MAX, 1.0, jnp.where(lam_raw == LAMBDA_RE_MAX, 0.5, 0.0))
        dlr_ref[...] = gl_r * pass_through
        dli_ref[...] = gl_i
        dls_ref[...] = st * jnp.sum(lr * gw_r + li * gw_i, axis=1, keepdims=True)

    sds = jax.ShapeDtypeStruct((g, p), F32)
    return pl.pallas_call(body, name="ssm_lamgrad", out_shape=(sds, sds, jax.ShapeDtypeStruct((g, 1), F32)))(
        lam_re, lam_im, log_step, abar_re, abar_im, coef_re, coef_im, gc_re, gc_im, ga_re, ga_im)


def _row_block(r, most=256):
    for rb in range(min(r, most), BF16_ROWS - 1, -1):
        if r % rb == 0 and rb % BF16_ROWS == 0:
            return rb
    return r


def _adamw_math(w, g, m, v):
    m = ADAM_B1 * m + (1.0 - ADAM_B1) * g
    v = ADAM_B2 * v + (1.0 - ADAM_B2) * (g * g)
    m_hat = m / (1.0 - ADAM_B1 ** ADAM_STEP)
    v_hat = v / (1.0 - ADAM_B2 ** ADAM_STEP)
    delta = -ADAM_LR * (m_hat / (jnp.sqrt(v_hat) + ADAM_EPS) + ADAM_WD * w)
    return delta, m, v


def _adamw_big(p_mine, p_sib, w, m, v, name):
    r, c = w.shape
    rb = _row_block(r)

    def body(a_ref, b_ref, w_ref, m_ref, v_ref, g_ref, d_ref, mo_ref, vo_ref):
        g = a_ref[...] + b_ref[...]
        g_ref[...] = g
        d_ref[...], mo_ref[...], vo_ref[...] = _adamw_math(w_ref[...], g, m_ref[...], v_ref[...])

    spec = pl.BlockSpec((rb, c), lambda i: (i, 0))
    sds = jax.ShapeDtypeStruct((r, c), F32)
    return pl.pallas_call(
        body, name=name, grid=(r // rb,), out_shape=(sds,) * 4, in_specs=[spec] * 5, out_specs=(spec,) * 4,
        compiler_params=_cparams(("parallel",), VMEM_MID),
    )(p_mine, p_sib, w, m, v)


def _sum_blocks(stack, name):
    n, r, c = stack.shape
    rb = _row_block(r)

    def body(s_ref, o_ref):
        acc = s_ref[0].astype(F32)
        for k in range(1, n):
            acc = acc + s_ref[k].astype(F32)
        o_ref[...] = acc

    return pl.pallas_call(
        body, name=name, grid=(r // rb,), out_shape=jax.ShapeDtypeStruct((r, c), F32),
        in_specs=[pl.BlockSpec((n, rb, c), lambda i: (0, i, 0))], out_specs=pl.BlockSpec((rb, c), lambda i: (i, 0)),
        compiler_params=_cparams(("parallel",), VMEM_MID),
    )(stack)


def _add2(a, b):
    def body(a_ref, b_ref, o_ref):
        o_ref[...] = a_ref[...] + b_ref[...]

    return pl.pallas_call(body, name="add_small", out_shape=jax.ShapeDtypeStruct(a.shape, F32))(a, b)


def _adamw_ada(c_all, dmod_cols, w, m, v):
    d, n = w.shape
    bn = 512

    def body(c_ref, dm_ref, w_ref, m_ref, v_ref, g_ref, d_ref, mo_ref, vo_ref):
        cc = c_ref[...]
        g = _dot_tn(cc * _sigmoid(cc), dm_ref[...])
        g_ref[...] = g
        d_ref[...], mo_ref[...], vo_ref[...] = _adamw_math(w_ref[...], g, m_ref[...], v_ref[...])

    spec = pl.BlockSpec((d, bn), lambda j: (0, j))
    sds = jax.ShapeDtypeStruct((d, n), F32)
    return pl.pallas_call(
        body, name="adamw_ada", grid=(n // bn,), out_shape=(sds,) * 4,
        in_specs=[_full((N_DEV, d)), pl.BlockSpec((N_DEV, bn), lambda j: (0, j)), spec, spec, spec],
        out_specs=(spec,) * 4, compiler_params=_cparams(("parallel",)),
    )(c_all, dmod_cols, w, m, v)


def _adamw_small(items):
    n = len(items)

    def body(*refs):
        ins, outs = refs[:4 * n], refs[4 * n:]
        for k in range(n):
            w_ref, g_ref, m_ref, v_ref = ins[4 * k:4 * k + 4]
            outs[3 * k][...], outs[3 * k + 1][...], outs[3 * k + 2][...] = _adamw_math(
                w_ref[...], g_ref[...], m_ref[...], v_ref[...])

    flat = [a for it in items for a in it]
    out_shape = tuple(jax.ShapeDtypeStruct(it[0].shape, F32) for it in items for _ in range(3))
    res = pl.pallas_call(body, name="adamw_small", out_shape=out_shape,
                         compiler_params=_cparams(vmem=VMEM_BIG))(*flat)
    return [tuple(res[3 * k:3 * k + 3]) for k in range(n)]


def _group_mean_matrix(n, group):
    idx = np.arange(n) // group
    return (idx[:, None] == idx[None, :]).astype(np.float32) / group


def _fold_matrix(n, period):
    return (np.arange(n)[:, None] % period == np.arange(period)[None, :]).astype(np.float32)


def _rows8(*rows):
    c = rows[0].shape[-1]
    pad = jnp.zeros((SUBLANES - len(rows), c), F32)
    return jnp.concatenate([r.reshape(1, c) for r in rows] + [pad], axis=0)


def _to_rows(a, width):
    flat = a.reshape(-1)
    n = -(-flat.shape[0] // width)
    flat = jnp.pad(flat, (0, n * width - flat.shape[0]))
    return flat.reshape(n, width)


def kernel(x, c, w_ada, b_ada, g_pre_mix, g_post_mix, w_in, ssm_lam_re, ssm_lam_im, ssm_log_step, ssm_b_re, ssm_b_im, ssm_c_re, ssm_c_im, ssm_d, glu_w, glu_b, g_out_ssm, conv_w, g_out_conv, w_out, g_pre_ffn, g_post_ffn, w_up, ffn_conv_w, w_down, loss_target, m_w_ada, m_b_ada, m_g_pre_mix, m_g_post_mix, m_w_in, m_ssm_lam_re, m_ssm_lam_im, m_ssm_log_step, m_ssm_b_re, m_ssm_b_im, m_ssm_c_re, m_ssm_c_im, m_ssm_d, m_glu_w, m_glu_b, m_g_out_ssm, m_conv_w, m_g_out_conv, m_w_out, m_g_pre_ffn, m_g_post_ffn, m_w_up, m_ffn_conv_w, m_w_down, v_w_ada, v_b_ada, v_g_pre_mix, v_g_post_mix, v_w_in, v_ssm_lam_re, v_ssm_lam_im, v_ssm_log_step, v_ssm_b_re, v_ssm_b_im, v_ssm_c_re, v_ssm_c_im, v_ssm_d, v_glu_w, v_glu_b, v_g_out_ssm, v_conv_w, v_g_out_conv, v_w_out, v_g_pre_ffn, v_g_post_ffn, v_w_up, v_ffn_conv_w, v_w_down):
    xs = x[0]
    tgt = loss_target[0]
    t, d = xs.shape
    xi, yi, ci = lax.axis_index("x"), lax.axis_index("y"), lax.axis_index("c")
    chip = 2 * xi + yi
    dev = 2 * chip + ci

    n_groups, n_state = ssm_lam_re.shape[1:]
    n_gch = ssm_b_re.shape[3]
    d_ssm = n_groups * n_gch
    gp = n_groups * n_state
    n_ada = w_ada.shape[2]
    d_ff = w_down.shape[1] * N_CHIPS
    n_upc = w_up.shape[2]

    w_names = ("w_in", "glu_w", "w_out", "w_up", "w_down")
    c_gath, _ = _allgather8(jnp.broadcast_to(c, (SUBLANES, d)), SUBLANES, "gather_c")
    c_all = c_gath.reshape(N_DEV, SUBLANES, d)[:, 0, :]

    def pad8(a):
        return jnp.concatenate([a, jnp.zeros((SUBLANES - a.shape[0], a.shape[1]), a.dtype)], axis=0)

    def start(name, arrs, after):
        return _chips_start(name, True, [], [_landing(a, chip) for a in arrs], after)

    w_names = ("w_in", "mod", "conv_w", "ffn_conv_w", "glu_w", "w_out", "w_up", "w_down")
    first = start("weights_start_in", [w_in[0].astype(BF16)], c_gath)
    b_sh = lax.dynamic_slice(b_ada, (0, chip * n_ada), (1, n_ada))
    mod_sh = _mod_shard(c_all + first[4][0:1, 0:1], w_ada[0], b_sh)
    second = start("weights_start_mod", [mod_sh, pad8(conv_w[0]), pad8(ffn_conv_w[0])], None)
    third = start("weights_start_rest", [w[0].astype(BF16) for w in (glu_w, w_out, w_up, w_down)], second[4])
    w_send, w_recv, w_land = [list(first[k]) + list(second[k]) + list(third[k]) for k in (0, 1, 3)]
    w_token = third[4]

    def weights(names, after):
        ks = [w_names.index(nm) for nm in names]
        return _chips_wait("weights_wait_" + names[-1], True, [w_send[k] for k in ks], [w_recv[k] for k in ks],
                           [], [w_land[k] for k in ks], after)

    lam_re, lam_im = ssm_lam_re[0], ssm_lam_im[0]
    log_step = ssm_log_step[0].reshape(n_groups, 1) + w_token[0:1, 0:1]
    abar_re, abar_im, coef_re, coef_im = _ssm_prep(lam_re, lam_im, log_step)
    a_rows = _rows8(abar_re.reshape(1, gp), abar_im.reshape(1, gp))
    coef_rows = _rows8(coef_re.reshape(1, gp), coef_im.reshape(1, gp))
    bt_re = ssm_b_re[0].transpose(0, 2, 1).reshape(d_ssm, n_state)
    bt_im = ssm_b_im[0].transpose(0, 2, 1).reshape(d_ssm, n_state)
    ct_re = ssm_c_re[0].transpose(0, 2, 1).reshape(gp, n_gch)
    ct_im = ssm_c_im[0].transpose(0, 2, 1).reshape(gp, n_gch)
    tile_b = jnp.asarray(np.tile(np.eye(n_state), (1, n_groups // SSM_SPLIT)), BF16)
    tile_c = jnp.asarray(np.tile(np.eye(n_gch), (1, n_groups)), BF16)
    bblk_re, bblk_im, cblk_re, cblk_im = _ssm_blocks(bt_re, bt_im, ct_re, ct_im, coef_rows, tile_b, tile_c)

    h16 = jnp.asarray(_group_mean_matrix(d_ssm, n_gch), BF16)
    h64 = jnp.asarray(_group_mean_matrix(d_ssm, CONV_HEAD_DIM), BF16)

    g_mod, g_cw, g_fw, w_in_st = weights(("mod", "conv_w", "ffn_conv_w", "w_in"), bblk_re)
    mod_all = g_mod.transpose(1, 0, 2).reshape(N_DEV, N_CHIPS * n_ada)
    mod = lax.dynamic_slice(mod_all, (dev, 0), (1, N_CHIPS * n_ada))
    sh1, sc1, gt1, sh2, sc2, gt2 = [mod[:, k * d:(k + 1) * d] for k in range(6)]
    convw_full = pad8(g_cw[:, :3, :].transpose(1, 0, 2).reshape(3, d_ssm))
    fw_full = pad8(g_fw[:, :3, :].transpose(1, 0, 2).reshape(3, N_CHIPS * n_upc))

    v512 = _rows8(ssm_d, glu_b, g_out_ssm, g_out_conv)
    vec1 =_rows8(g_pre_mix, 1.0 + sc1, sh1)
    vd1 = _rows8(g_post_mix, gt1)
    vec2 = _rows8(g_pre_ffn, 1.0 + sc2, sh2)
    vd2 = _rows8(g_post_ffn, gt2)

    proj, bu_re, bu_im, h1b = _mix_in(xs, vec1, w_in_st, bblk_re, bblk_im)
    s_re, s_im = _scan_fwd(a_rows, bu_re, bu_im)
    g_glu, g_wout = weights(("glu_w", "w_out"), s_re)
    glu_full = g_glu.reshape(d_ssm, d_ssm)
    w_out_full = g_wout.reshape(2 * d_ssm, d)
    y1, o_mix, x2 = _mix_out(xs, proj, s_re, s_im, cblk_re, cblk_im, v512, convw_full, glu_full, h16, h64,
                             w_out_full, vd1)
    (w_up_st,) = weights(("w_up",), x2)
    up, h2b = _ffn_up(x2, vec2, w_up_st)
    (g_wdown,) = weights(("w_down",), up)
    w_down_full = g_wdown.reshape(d_ff, d)
    actb, ddnb, dout, dhid, vp_dn, loss_blk = _ffn_down(up, fw_full, w_down_full, w_down_full.T, x2, tgt, vd2)

    g_names = ("w_down", "w_up", "w_out", "glu_w", "w_in")
    gw_down = _matmul_tn(actb, ddnb, d_ff, d, BF16, "dw_down", bt=1024).reshape(N_CHIPS, d_ff // N_CHIPS, d)
    dx2, dupb, vp_up, df_rows = _ffn_up_bwd(dhid, up, fw_full, x2, dout, vec2, w_up_st)
    gw_up = _matmul_tn(h2b, dupb, d, n_upc, BF16, "dw_up", bt=2048)
    ga_send, ga_recv, ga_src, ga_land, ga_token = _chips_start(
        "grads_start_ffn", False, [gw_down, gw_up],
        [_landing(lax.dynamic_index_in_dim(g, chip, 0, False), chip) for g in (gw_down, gw_up)])
    (dob, ycatb, zb, dqb, dy1, g_re, g_im, dcc, dbg, vp_mo, vp5, d_cre, d_cim) = _mix_out_bwd(
        dx2, o_mix, y1, proj, s_re, s_im, cblk_re, cblk_im, v512, convw_full, glu_full, h16, h64, w_out_full,
        vd1 + ga_token[0:1, 0:1])
    gw_out = _matmul_tn(ycatb, dob, 2 * d_ssm, d, BF16, "dw_out", bt=2048)
    gw_out = gw_out.reshape(N_CHIPS, 2 * d_ssm // N_CHIPS, d)
    gw_glu = _matmul_tn(zb, dqb, d_ssm, d_ssm, BF16, "dw_glu", bt=2048).reshape(N_CHIPS, d_ssm // N_CHIPS, d_ssm)
    gb_send, gb_recv, gb_src, gb_land, gb_token = _chips_start(
        "grads_start_mix", False, [gw_out, gw_glu],
        [_landing(lax.dynamic_index_in_dim(g, chip, 0, False), chip) for g in (gw_out, gw_glu)])
    gt_re, gt_im, ga_re8, ga_im8 = _scan_bwd(a_rows + gb_token[0:1, 0:1], g_re, g_im, s_re, s_im)
    grad_x, dprojb, vp_mi, d_bre, d_bim = _mix_in_bwd(gt_re, gt_im, bblk_re, bblk_im, dy1, dcc, dbg, proj, xs, dx2,
                                                      vec1, v512, convw_full, w_in_st)
    ssm_u, ssm_s = d_ssm // SSM_SPLIT, gp // SSM_SPLIT

    fold_b = jnp.asarray(_fold_matrix(ssm_s, n_state), BF16)
    fold_c = jnp.asarray(_fold_matrix(ssm_u, n_gch), BF16)
    db_re_f, db_im_f, gc_rows = _ssm_bgrad(d_bre, d_bim, bt_re, bt_im, coef_rows, fold_b, tile_b)
    dc_re_f, dc_im_f = _ssm_cgrad(d_cre, d_cim, fold_c)
    ga_sum = _ga_rowsum(ga_re8, ga_im8)
    g_lam_re, g_lam_im, g_log_step = _ssm_lamgrad(
        lam_re, lam_im, log_step, abar_re, abar_im, coef_re, coef_im,
        gc_rows[0].reshape(n_groups, n_state), gc_rows[1].reshape(n_groups, n_state),
        ga_sum[0].reshape(n_groups, n_state), ga_sum[1].reshape(n_groups, n_state))
    g_b_re = db_re_f.reshape(n_groups, n_gch, n_state).transpose(0, 2, 1)
    g_b_im = db_im_f.reshape(n_groups, n_gch, n_state).transpose(0, 2, 1)
    g_c_re = dc_re_f.reshape(n_groups, n_state, n_gch).transpose(0, 2, 1)
    g_c_im = dc_im_f.reshape(n_groups, n_state, n_gch).transpose(0, 2, 1)

    dmod = jnp.concatenate([vp_mi[0:1], vp_mi[1:2], vp_mo[0:1], vp_up[0:1], vp_up[1:2], vp_dn[0:1]], axis=1)
    small = [
        ("g_pre_mix", vp_mi[2:3]), ("g_post_mix", vp_mo[1:2]), ("g_pre_ffn", vp_up[2:3]), ("g_post_ffn", vp_dn[1:2]),
        ("ssm_lam_re", g_lam_re), ("ssm_lam_im", g_lam_im), ("ssm_log_step", g_log_step),
        ("ssm_b_re", g_b_re), ("ssm_b_im", g_b_im), ("ssm_c_re", g_c_re), ("ssm_c_im", g_c_im),
        ("ssm_d", vp5[3:4]), ("glu_b", vp5[2:3]), ("g_out_ssm", vp5[0:1]), ("g_out_conv", vp5[1:2]),
        ("conv_w", vp5[4:7]), ("ffn_conv_w", df_rows[0:3]), ("loss", loss_blk[0:1, 0:1]),
    ]
    packed, offsets, row = [], {}, 0
    for name, a in small:
        r = _to_rows(a, d)
        offsets[name] = (row, a.shape)
        packed.append(r)
        row += r.shape[0]
    n_small = -(-row // SUBLANES) * SUBLANES
    packed.append(jnp.zeros((n_small - row, d), F32))
    packed.append(pad8(dmod.reshape(6, d)))
    pack = jnp.concatenate(packed, axis=0)
    sm_send, sm_recv, _, sm_land, sm_token = _chips_start("small_start", True, [], [_landing(pack, chip)])

    gw_in = _matmul_tn(h1b, dprojb, d, w_in.shape[2], BF16, "dw_in", bt=2048, after=sm_token)
    gc_send, gc_recv, gc_src, gc_land, gc_token = _chips_start(
        "grads_start_in", False, [gw_in], [_landing(lax.dynamic_index_in_dim(gw_in, chip, 0, False), chip)])

    def finish(names, landed):
        partial = [_sum_blocks(s, "sum_" + nm) for s, nm in zip(landed, names)]
        theirs = _swap_sibling(partial, "swap_" + names[0])
        done = {}
        for nm, pm, ps in zip(names, partial, theirs):
            w_, m_, v_ = big_params[nm]
            done[nm] = _adamw_big(pm, ps, w_[0], m_[0], v_[0], "adamw_" + nm)
        return done

    big_params = {"w_down": (w_down, m_w_down, v_w_down), "w_up": (w_up, m_w_up, v_w_up),
                  "w_out": (w_out, m_w_out, v_w_out), "glu_w": (glu_w, m_glu_w, v_glu_w),
                  "w_in": (w_in, m_w_in, v_w_in)}
    big = finish(("w_down", "w_up"), _chips_wait("grads_wait_ffn", False, ga_send, ga_recv, ga_src, ga_land, gc_token))

    (sm_landed,) = _chips_wait("small_wait", True, sm_send, sm_recv, [], sm_land, big["w_up"][0])
    sm_part = _sum_blocks(sm_landed, "sum_small")
    dmod_mine = sm_landed[:, n_small:n_small + SUBLANES, :]
    sm_sib, dmod_sib = _swap_sibling([sm_part, dmod_mine], "swap_small")
    sums = _add2(sm_part, sm_sib)
    dmod_by_core = jnp.stack([dmod_mine, dmod_sib], axis=1)
    dmod_by_core = jnp.where(ci == 0, dmod_by_core, dmod_by_core[:, ::-1])
    dmod_all = dmod_by_core[:, :, :6, :].reshape(N_DEV, 6 * d)
    g_b_ada = sums[n_small:n_small + 6].reshape(1, 6 * d)

    def unpack(name):
        r0, shape = offsets[name]
        size = math.prod(shape)
        nrow = -(-size // d)
        return sums[r0:r0 + nrow].reshape(-1)[:size].reshape(shape)

    dmod_cols = lax.dynamic_slice(dmod_all, (0, chip * n_ada), (N_DEV, n_ada))
    ada = _adamw_ada(c_all, dmod_cols, w_ada[0], m_w_ada[0], v_w_ada[0])

    big.update(finish(("w_out", "glu_w", "w_in"), _chips_wait(
        "grads_wait_mix", False, list(gb_send) + list(gc_send), list(gb_recv) + list(gc_recv),
        list(gb_src) + list(gc_src), list(gb_land) + list(gc_land), ada[0])))

    g_small = {name: unpack(name) for name, _ in small}
    g_small["b_ada"] = g_b_ada
    g_small["conv_w"] = lax.dynamic_slice(g_small["conv_w"], (0, chip * conv_w.shape[2]), (3, conv_w.shape[2]))
    g_small["ffn_conv_w"] = lax.dynamic_slice(g_small["ffn_conv_w"], (0, chip * n_upc), (3, n_upc))
    g_small["ssm_log_step"] = g_small["ssm_log_step"].reshape(1, n_groups)
    small_params = {
        "b_ada": (b_ada, m_b_ada, v_b_ada), "g_pre_mix": (g_pre_mix, m_g_pre_mix, v_g_pre_mix),
        "g_post_mix": (g_post_mix, m_g_post_mix, v_g_post_mix), "ssm_lam_re": (ssm_lam_re, m_ssm_lam_re, v_ssm_lam_re),
        "ssm_lam_im": (ssm_lam_im, m_ssm_lam_im, v_ssm_lam_im),
        "ssm_log_step": (ssm_log_step, m_ssm_log_step, v_ssm_log_step),
        "ssm_b_re": (ssm_b_re, m_ssm_b_re, v_ssm_b_re), "ssm_b_im": (ssm_b_im, m_ssm_b_im, v_ssm_b_im),
        "ssm_c_re": (ssm_c_re, m_ssm_c_re, v_ssm_c_re), "ssm_c_im": (ssm_c_im, m_ssm_c_im, v_ssm_c_im),
        "ssm_d": (ssm_d, m_ssm_d, v_ssm_d), "glu_b": (glu_b, m_glu_b, v_glu_b),
        "g_out_ssm": (g_out_ssm, m_g_out_ssm, v_g_out_ssm), "conv_w": (conv_w, m_conv_w, v_conv_w),
        "g_out_conv": (g_out_conv, m_g_out_conv, v_g_out_conv), "g_pre_ffn": (g_pre_ffn, m_g_pre_ffn, v_g_pre_ffn),
        "g_post_ffn": (g_post_ffn, m_g_post_ffn, v_g_post_ffn),
        "ffn_conv_w": (ffn_conv_w, m_ffn_conv_w, v_ffn_conv_w),
    }

    def natural(a):
        return a[0] if a.ndim > 2 else a

    names = list(small_params)
    items = []
    for nm in names:
        w_, m_, v_ = small_params[nm]
        items.append((natural(w_), g_small[nm].reshape(natural(w_).shape), natural(m_), natural(v_)))
    upd = _adamw_small(items)
    small_out = {}
    for nm, (dl, mo, vo) in zip(names, upd):
        shp = small_params[nm][0].shape
        small_out[nm] = (g_small[nm].reshape(shp), dl.reshape(shp), mo.reshape(shp), vo.reshape(shp))

    loss = g_small["loss"][0, 0]

    order = ["w_ada", "b_ada", "g_pre_mix", "g_post_mix", "w_in", "ssm_lam_re", "ssm_lam_im", "ssm_log_step",
             "ssm_b_re", "ssm_b_im", "ssm_c_re", "ssm_c_im", "ssm_d", "glu_w", "glu_b", "g_out_ssm", "conv_w",
             "g_out_conv", "w_out", "g_pre_ffn", "g_post_ffn", "w_up", "ffn_conv_w", "w_down"]
    results = {"w_ada": tuple(a[None] for a in ada)}
    for nm in big:
        results[nm] = tuple(a[None] for a in big[nm])
    results.update(small_out)
    outs = [loss, grad_x[None]]
    for k in range(4):
        outs += [results[nm][k] for nm in order]
    return tuple(outs)


def _ga_rowsum(ga_re8, ga_im8):
    n = ga_re8.shape[1]

    def body(r_ref, i_ref, o_ref):
        o_ref[...] = jnp.zeros(o_ref.shape, F32)
        o_ref[0:1, :] = _colsum(r_ref[...])
        o_ref[1:2, :] = _colsum(i_ref[...])

    return pl.pallas_call(body, name="ga_rowsum", out_shape=jax.ShapeDtypeStruct((SUBLANES, n), F32))(ga_re8, ga_im8)
```

```python
import functools
import math

import jax
import jax.numpy as jnp
import numpy as np
from jax import lax
from jax.experimental import pallas as pl
from jax.experimental.pallas import tpu as pltpu

F32 = jnp.float32
BF16 = jnp.bfloat16
MESH = pl.DeviceIdType.MESH

EPS = 1e-6
LAMBDA_RE_MAX = -1e-4
ADAM_LR = 0.001
ADAM_B1 = 0.9
ADAM_B2 = 0.999
ADAM_EPS = 1e-08
ADAM_WD = 0.01
ADAM_STEP = 10

SUBLANES = 8
BF16_ROWS = 16
N_CHIPS = 4
N_DEV = 8
CONV_HEAD_DIM = 64
VMEM_BIG = 56 * 1024 * 1024
VMEM_MID = 40 * 1024 * 1024

TB_MIX = 256
TB_FFN = 256
TB_SCAN = 1024
W_SCAN = 256
SSM_SPLIT = 4
CW_FFN = 256
SCAN_UNROLL = 4
TB_TN = 512


def _cparams(sem=None, vmem=None):
    kw = {}
    if sem is not None:
        kw["dimension_semantics"] = sem
    if vmem is not None:
        kw["vmem_limit_bytes"] = vmem
    return pltpu.CompilerParams(**kw)


def _blk(t, pref):
    return pref if t % pref == 0 else t


def _dot(a, b):
    return jnp.dot(a.astype(BF16), b.astype(BF16), preferred_element_type=F32)


def _dot_nt(a, b):
    return lax.dot_general(a.astype(BF16), b.astype(BF16), (((1,), (1,)), ((), ())),
                           preferred_element_type=F32)


def _dot_tn(a, b):
    return lax.dot_general(a.astype(BF16), b.astype(BF16), (((0,), (0,)), ((), ())),
                           preferred_element_type=F32)


def _sigmoid(x):
    return 0.5 * jnp.tanh(0.5 * x) + 0.5


_GELU_K = math.sqrt(2.0 / math.pi)
_GELU_C = 0.044715


def _gelu(x):
    th = jnp.tanh(_GELU_K * (x + _GELU_C * x * x * x))
    return x * (0.5 * (1.0 + th))


def _gelu_and_grad(x):
    x2 = x * x
    th = jnp.tanh(_GELU_K * (x + _GELU_C * x2 * x))
    half = 0.5 * (1.0 + th)
    return x * half, half + 0.5 * x * (1.0 - th * th) * _GELU_K * (1.0 + 3.0 * _GELU_C * x2)


def _rowmean(x):
    return jnp.mean(x, axis=-1, keepdims=True)


def _colsum(x):
    return jnp.sum(x, axis=0, keepdims=True)


def _split_dot(x, m):
    hi = x.astype(BF16)
    lo = (x - hi.astype(F32)).astype(BF16)
    return (jnp.dot(hi, m, preferred_element_type=F32) + jnp.dot(lo, m, preferred_element_type=F32))


def _split3_dot(x, m):
    hi = x.astype(BF16)
    r1 = x - hi.astype(F32)
    mid = r1.astype(BF16)
    lo = (r1 - mid.astype(F32)).astype(BF16)
    return (jnp.dot(hi, m, preferred_element_type=F32) + jnp.dot(mid, m, preferred_element_type=F32)
            + jnp.dot(lo, m, preferred_element_type=F32))


def _shift_down(x, halo, k):
    r = pltpu.roll(x, k, 0)
    row = lax.broadcasted_iota(jnp.int32, x.shape, 0)
    for j in range(k):
        r = jnp.where(row == j, halo[SUBLANES - k + j:SUBLANES - k + j + 1, :], r)
    return r


def _shift_up(x, halo, k):
    n = x.shape[0]
    r = pltpu.roll(x, n - k, 0)
    row = lax.broadcasted_iota(jnp.int32, x.shape, 0)
    for j in range(k):
        r = jnp.where(row == n - k + j, halo[j:j + 1, :], r)
    return r


def _acc_rows(ref, first, rows):
    @pl.when(first)
    def _():
        ref[...] = jnp.zeros(ref.shape, ref.dtype)
    for j, r in enumerate(rows):
        ref[j:j + 1, :] += r


def _rows(tb, c, col=0):
    return pl.BlockSpec((tb, c), lambda i, col=col: (i, col))


def _full(shape):
    nd = len(shape)
    return pl.BlockSpec(shape, lambda i, nd=nd: (0,) * nd)


def _resident(shape):
    nd = len(shape)
    return pl.BlockSpec(shape, lambda i, nd=nd: (0,) * nd, pipeline_mode=pl.Buffered(1))


def _halo_prev(tb, c, col=0):
    per = tb // SUBLANES
    return pl.BlockSpec((SUBLANES, c), lambda i, col=col: (jnp.maximum(i * per - 1, 0), col))


def _halo_next(tb, c, t, col=0, rows=SUBLANES):
    per = tb // rows
    last = t // rows - 1
    return pl.BlockSpec((rows, c), lambda i, col=col: (jnp.minimum((i + 1) * per, last), col))


def _mesh_pos():
    return lax.axis_index("x"), lax.axis_index("y"), lax.axis_index("c")


def _allgather8(x_pad, n_sum, name):
    m_per, n = x_pad.shape

    def body(x_ref, out_ref, sum_ref, send_sems, recv_sems, local_sem):
        x, y, c = _mesh_pos()
        me, sibling = (x, y, c), (x, y, 1 - c)
        chips = [(1 - x, y), (x, 1 - y), (1 - x, 1 - y)]

        def rows(px, py, pc):
            return out_ref.at[pl.ds((4 * px + 2 * py + pc) * m_per, m_per), :]

        def copy(k, block, to, src=None):
            return pltpu.make_async_remote_copy(
                src_ref=rows(*block) if src is None else src, dst_ref=rows(*block),
                send_sem=send_sems.at[k], recv_sem=recv_sems.at[k], device_id=to, device_id_type=MESH)

        mine = pltpu.make_async_copy(x_ref, rows(*me), local_sem)
        mine.start()
        first = [copy(0, me, sibling, src=x_ref)]
        first += [copy(1 + j, me, (*chip, c), src=x_ref) for j, chip in enumerate(chips)]
        for cp in first:
            cp.start()
        passed = [copy(4 + j, (*chip, c), sibling) for j, chip in enumerate(chips)]
        for j, chip in enumerate(chips):
            copy(1 + j, (*chip, c), me).wait_recv()
            passed[j].start()
        copy(0, sibling, me).wait_recv()
        for j, chip in enumerate(chips):
            copy(4 + j, (*chip, 1 - c), me).wait_recv()
        for cp in first + passed:
            cp.wait_send()
        mine.wait()
        acc = out_ref[0:n_sum, :]
        for k in range(1, N_DEV):
            acc = acc + out_ref[k * m_per:k * m_per + n_sum, :]
        sum_ref[...] = acc

    return pl.pallas_call(
        body, name=name,
        out_shape=(jax.ShapeDtypeStruct((N_DEV * m_per, n), F32), jax.ShapeDtypeStruct((n_sum, n), F32)),
        in_specs=[pl.BlockSpec(memory_space=pltpu.VMEM)],
        out_specs=(pl.BlockSpec(memory_space=pltpu.VMEM), pl.BlockSpec(memory_space=pltpu.VMEM)),
        scratch_shapes=[pltpu.SemaphoreType.DMA((7,)), pltpu.SemaphoreType.DMA((7,)), pltpu.SemaphoreType.DMA],
        compiler_params=_cparams(vmem=VMEM_MID),
    )(x_pad)


_HBM = pl.BlockSpec(memory_space=pltpu.HBM)
_SEM = pl.BlockSpec(memory_space=pltpu.SEMAPHORE)
_EFFECT = pltpu.SideEffectType.DATAFLOW_SIDE_EFFECTING


def _chip_copy(gather, src_ref, land_ref, send, recv, j, arrival):
    x, y, c = _mesh_pos()
    peer = [(1 - x, y), (x, 1 - y), (1 - x, 1 - y)][j]
    peer_chip = 2 * peer[0] + peer[1]
    my_chip = 2 * x + y
    return pltpu.make_async_remote_copy(
        src_ref=land_ref.at[my_chip] if gather else src_ref.at[peer_chip],
        dst_ref=land_ref.at[peer_chip if arrival else my_chip],
        send_sem=send.at[j], recv_sem=recv.at[j], device_id=(*peer, c), device_id_type=MESH)


def _chips_start(name, gather, srcs, lands, after=None):
    n, ns = len(lands), len(srcs)
    extra = [] if after is None else [after]

    def body(*refs):
        src_refs, land_refs = refs[:ns], refs[ns:ns + n]
        outs = refs[ns + n + len(extra):]
        sends, recvs, token = outs[:n], outs[n:2 * n], outs[-1]
        for k in range(n):
            for j in range(3):
                _chip_copy(gather, src_refs[k] if ns else None, land_refs[k], sends[k], recvs[k], j, False).start()
        token[...] = jnp.zeros(token.shape, F32)

    sem = pltpu.SemaphoreType.DMA((3,))
    thru = tuple(pltpu.HBM(a.shape, a.dtype) for a in list(srcs) + list(lands))
    res = pl.pallas_call(
        body, name=name,
        out_shape=(sem,) * (2 * n) + thru + (jax.ShapeDtypeStruct((SUBLANES, 128), F32),),
        in_specs=[_HBM] * (ns + n) + [pl.BlockSpec(memory_space=pl.ANY)] * len(extra),
        out_specs=(_SEM,) * (2 * n) + (_HBM,) * (ns + n) + (pl.BlockSpec(memory_space=pltpu.VMEM),),
        input_output_aliases={k: 2 * n + k for k in range(ns + n)},
        compiler_params=pltpu.CompilerParams(has_side_effects=_EFFECT),
    )(*[pltpu.with_memory_space_constraint(a, pltpu.HBM) for a in list(srcs) + list(lands)], *extra)
    return res[:n], res[n:2 * n], res[2 * n:2 * n + ns], res[2 * n + ns:2 * n + ns + n], res[-1]


def _chips_wait(name, gather, sends, recvs, srcs, lands, after):
    n, ns = len(lands), len(srcs)

    def body(*refs):
        src_refs, land_refs = refs[:ns], refs[ns:ns + n]
        sends_, recvs_ = refs[ns + n:ns + 2 * n], refs[ns + 2 * n:ns + 3 * n]
        for k in range(n):
            for j in range(3):
                cp = _chip_copy(gather, src_refs[k] if ns else None, land_refs[k], sends_[k], recvs_[k], j, True)
                cp.wait_send()
                cp.wait_recv()

    thru = tuple(pltpu.HBM(a.shape, a.dtype) for a in list(srcs) + list(lands))
    res = pl.pallas_call(
        body, name=name, out_shape=thru,
        in_specs=[_HBM] * (ns + n) + [_SEM] * (2 * n) + [pl.BlockSpec(memory_space=pl.ANY)],
        out_specs=(_HBM,) * (ns + n),
        input_output_aliases={k: k for k in range(ns + n)},
        compiler_params=pltpu.CompilerParams(has_side_effects=_EFFECT),
    )(*srcs, *lands, *sends, *recvs, after)
    return res[ns:]


def _landing(own, chip):
    zone = lax.empty((N_CHIPS,) + own.shape, own.dtype)
    return lax.dynamic_update_slice(zone, own[None], (chip,) + (0,) * own.ndim)


def _swap_sibling(arrs, name):
    n_arr = len(arrs)

    def body(*refs):
        ins, outs = refs[:n_arr], refs[n_arr:2 * n_arr]
        send_sems, recv_sems = refs[2 * n_arr:]
        x, y, c = _mesh_pos()
        copies = [pltpu.make_async_remote_copy(
            src_ref=ins[n], dst_ref=outs[n], send_sem=send_sems.at[n], recv_sem=recv_sems.at[n],
            device_id=(x, y, 1 - c), device_id_type=MESH) for n in range(n_arr)]
        for cp in copies:
            cp.start()
        for cp in copies:
            cp.wait()

    any_spec = pl.BlockSpec(memory_space=pl.ANY)
    return pl.pallas_call(
        body, name=name,
        out_shape=tuple(jax.ShapeDtypeStruct(a.shape, a.dtype) for a in arrs),
        in_specs=[any_spec] * n_arr, out_specs=tuple([any_spec] * n_arr),
        scratch_shapes=[pltpu.SemaphoreType.DMA((n_arr,)), pltpu.SemaphoreType.DMA((n_arr,))],
    )(*arrs)


def _mod_shard(c_all, w_ada_sh, b_sh):
    d, n = w_ada_sh.shape
    bn = 512

    def body(c_ref, w_ref, b_ref, o_ref):
        cc = c_ref[...]
        ca = cc * _sigmoid(cc)
        o_ref[...] = _dot(ca, w_ref[...]) + b_ref[...]

    return pl.pallas_call(
        body, name="mod_shard", grid=(n // bn,),
        out_shape=jax.ShapeDtypeStruct((N_DEV, n), F32),
        in_specs=[_full((N_DEV, d)), pl.BlockSpec((d, bn), lambda j: (0, j)), pl.BlockSpec((1, bn), lambda j: (0, j))],
        out_specs=pl.BlockSpec((N_DEV, bn), lambda j: (0, j)),
        compiler_params=_cparams(("parallel",)),
    )(c_all, w_ada_sh, b_sh)


def _ssm_prep(lam_re, lam_im, log_step):
    g, p = lam_re.shape

    def body(lr_ref, li_ref, ls_ref, ar_ref, ai_ref, cr_ref, ci_ref):
        lr = jnp.minimum(lr_ref[...], LAMBDA_RE_MAX)
        li = li_ref[...]
        st = jnp.exp(ls_ref[...])
        mag = jnp.exp(lr * st)
        ar = mag * jnp.cos(li * st)
        ai = mag * jnp.sin(li * st)
        den = lr * lr + li * li
        nr = ar - 1.0
        ar_ref[...] = ar
        ai_ref[...] = ai
        cr_ref[...] = (nr * lr + ai * li) / den
        ci_ref[...] = (ai * lr - nr * li) / den

    sds = jax.ShapeDtypeStruct((g, p), F32)
    return pl.pallas_call(body, name="ssm_prep", out_shape=(sds,) * 4)(lam_re, lam_im, log_step)


def _ssm_blocks(bt_re, bt_im, ct_re, ct_im, coef_rows, tile_b, tile_c):
    gh, p = bt_re.shape
    gp, h = ct_re.shape
    nb = SSM_SPLIT
    cb, rb = gp // nb, gp // nb

    def body(btr, bti, ctr, cti, cf, tb_ref, tc_ref, bre_o, bim_o, cre_o, cim_o):
        j = pl.program_id(0)
        row = lax.broadcasted_iota(jnp.int32, (gh, cb), 0)
        col = lax.broadcasted_iota(jnp.int32, (gh, cb), 1) + j * cb
        mask = (row >> 4) == (col >> 6)
        cr, ci = cf[0:1, :], cf[1:2, :]
        br = _split3_dot(btr[...], tb_ref[...])
        bi = _split3_dot(bti[...], tb_ref[...])
        bre_o[...] = jnp.where(mask, br * cr - bi * ci, 0.0).astype(BF16)
        bim_o[...] = jnp.where(mask, br * ci + bi * cr, 0.0).astype(BF16)
        row2 = lax.broadcasted_iota(jnp.int32, (rb, gh), 0) + j * rb
        col2 = lax.broadcasted_iota(jnp.int32, (rb, gh), 1)
        mask2 = (row2 >> 6) == (col2 >> 4)
        cre_o[...] = jnp.where(mask2, _split3_dot(ctr[...], tc_ref[...]), 0.0).astype(BF16)
        cim_o[...] = jnp.where(mask2, _split3_dot(cti[...], tc_ref[...]), 0.0).astype(BF16)

    bspec = pl.BlockSpec((gh, cb), lambda j: (0, j))
    cspec = pl.BlockSpec((rb, gh), lambda j: (j, 0))
    cin = pl.BlockSpec((rb, h), lambda j: (j, 0))
    return pl.pallas_call(
        body, name="ssm_blocks", grid=(nb,),
        out_shape=(jax.ShapeDtypeStruct((gh, gp), BF16),) * 2 + (jax.ShapeDtypeStruct((gp, gh), BF16),) * 2,
        in_specs=[_full((gh, p)), _full((gh, p)), cin, cin, pl.BlockSpec((SUBLANES, cb), lambda j: (0, j)),
                  _full(tile_b.shape), _full(tile_c.shape)],
        out_specs=(bspec, bspec, cspec, cspec),
        compiler_params=_cparams(("parallel",)),
    )(bt_re, bt_im, ct_re, ct_im, coef_rows, tile_b, tile_c)


def _scan_consts(a_ref, reverse):
    w = a_ref.shape[1]
    ar1 = a_ref[0:1, :]
    ai1 = a_ref[1:2, :]
    if reverse:
        ai1 = -ai1
    pr, pi = [ar1], [ai1]
    for _ in range(1, SUBLANES):
        nr = pr[-1] * ar1 - pi[-1] * ai1
        ni = pr[-1] * ai1 + pi[-1] * ar1
        pr.append(nr)
        pi.append(ni)
    row = lax.broadcasted_iota(jnp.int32, (SUBLANES, w), 0)
    dist = (SUBLANES - 1 - row) if reverse else row

    def pick(vals):
        out = jnp.broadcast_to(vals[SUBLANES - 1], (SUBLANES, w))
        for r in range(SUBLANES - 1):
            out = jnp.where(dist == r, vals[r], out)
        return out

    p_r, p_i = pick(pr), pick(pi)
    steps = []
    for k in (1, 2, 4):
        steps.append((k, jnp.where(dist >= k, pr[k - 1], 0.0), jnp.where(dist >= k, pi[k - 1], 0.0)))
    a8 = (jnp.broadcast_to(pr[SUBLANES - 1], (SUBLANES, w)), jnp.broadcast_to(pi[SUBLANES - 1], (SUBLANES, w)))
    return row, p_r, p_i, steps, a8


def _scan_tile(xr, xi, cr, ci, consts, reverse):
    row, p_r, p_i, steps, (a8r, a8i) = consts
    for k, s_r, s_i in steps:
        sh = (SUBLANES - k) if reverse else k
        qr = pltpu.roll(xr, sh, 0)
        qi = pltpu.roll(xi, sh, 0)
        xr, xi = xr + s_r * qr - s_i * qi, xi + s_r * qi + s_i * qr
    outr = xr + p_r * cr - p_i * ci
    outi = xi + p_r * ci + p_i * cr
    e = 0 if reverse else SUBLANES - 1
    er = jnp.broadcast_to(xr[e:e + 1, :], xr.shape)
    ei = jnp.broadcast_to(xi[e:e + 1, :], xi.shape)
    return outr, outi, er + a8r * cr - a8i * ci, ei + a8r * ci + a8i * cr


def _scan_fwd(a_rows, bu_re, bu_im):
    t, n = bu_re.shape
    tb, w = _blk(t, TB_SCAN), W_SCAN
    ntile = tb // SUBLANES

    def body(a_ref, br_ref, bi_ref, sr_ref, si_ref, car, cai):
        @pl.when(pl.program_id(1) == 0)
        def _():
            car[...] = jnp.zeros(car.shape, F32)
            cai[...] = jnp.zeros(cai.shape, F32)
        consts = _scan_consts(a_ref, False)

        def pair(i, carry):
            o = pl.multiple_of(i * BF16_ROWS, BF16_ROWS)
            outs = []
            for h in range(2):
                rows = pl.ds(o + h * SUBLANES, SUBLANES)
                outr, outi, ncr, nci = _scan_tile(br_ref[rows, :], bi_ref[rows, :], carry[0], carry[1], consts, False)
                outs.append((outr, outi))
                carry = (ncr, nci)
            sr_ref[pl.ds(o, BF16_ROWS), :] = jnp.concatenate([outs[0][0], outs[1][0]], axis=0).astype(BF16)
            si_ref[pl.ds(o, BF16_ROWS), :] = jnp.concatenate([outs[0][1], outs[1][1]], axis=0).astype(BF16)
            return carry

        def pairs(i, carry):
            for s in range(SCAN_UNROLL // 2):
                carry = pair(i * (SCAN_UNROLL // 2) + s, carry)
            return carry

        cr, ci = lax.fori_loop(0, ntile // SCAN_UNROLL, pairs, (car[...], cai[...]))
        car[...] = cr
        cai[...] = ci

    spec = pl.BlockSpec((tb, w), lambda s, k: (k, s))
    sds = jax.ShapeDtypeStruct((t, n), BF16)
    return pl.pallas_call(
        body, name="scan_fwd", grid=(n // w, t // tb), out_shape=(sds, sds),
        in_specs=[pl.BlockSpec((SUBLANES, w), lambda s, k: (0, s)), spec, spec], out_specs=(spec, spec),
        scratch_shapes=[pltpu.VMEM((SUBLANES, w), F32), pltpu.VMEM((SUBLANES, w), F32)],
        compiler_params=_cparams(("parallel", "arbitrary"), VMEM_MID),
    )(a_rows, bu_re, bu_im)


def _scan_bwd(a_rows, g_re, g_im, s_re, s_im):
    t, n = g_re.shape
    tb, w = _blk(t, TB_SCAN), W_SCAN
    ntile = tb // SUBLANES
    npair = tb // BF16_ROWS
    nt = t // tb

    def body(a_ref, gr_ref, gi_ref, sr_ref, si_ref, or_ref, oi_ref, gar_ref, gai_ref, car, cai):
        @pl.when(pl.program_id(1) == 0)
        def _():
            car[...] = jnp.zeros(car.shape, F32)
            cai[...] = jnp.zeros(cai.shape, F32)
            gar_ref[...] = jnp.zeros(gar_ref.shape, F32)
            gai_ref[...] = jnp.zeros(gai_ref.shape, F32)
        consts = _scan_consts(a_ref, True)
        row = consts[0]

        def pair(i, carry):
            cr, ci, accr, acci = carry
            o = pl.multiple_of((npair - 1 - i) * BF16_ROWS, BF16_ROWS)
            s_r = sr_ref[pl.ds(o, BF16_ROWS), :].astype(F32)
            s_i = si_ref[pl.ds(o, BF16_ROWS), :].astype(F32)
            outs = [None, None]
            for h in (1, 0):
                rows = pl.ds(o + h * SUBLANES, SUBLANES)
                outr, outi, ncr, nci = _scan_tile(gr_ref[rows, :], gi_ref[rows, :], cr, ci, consts, True)
                outs[h] = (outr, outi)
                gnr = jnp.where(row == SUBLANES - 1, cr, pltpu.roll(outr, SUBLANES - 1, 0))
                gni = jnp.where(row == SUBLANES - 1, ci, pltpu.roll(outi, SUBLANES - 1, 0))
                sr = s_r[h * SUBLANES:(h + 1) * SUBLANES, :]
                si = s_i[h * SUBLANES:(h + 1) * SUBLANES, :]
                accr, acci = accr + sr * gnr + si * gni, acci + sr * gni - si * gnr
                cr, ci = ncr, nci
            or_ref[pl.ds(o, BF16_ROWS), :] = jnp.concatenate([outs[0][0], outs[1][0]], axis=0).astype(BF16)
            oi_ref[pl.ds(o, BF16_ROWS), :] = jnp.concatenate([outs[0][1], outs[1][1]], axis=0).astype(BF16)
            return cr, ci, accr, acci

        def pairs(i, carry):
            for s in range(SCAN_UNROLL // 2):
                carry = pair(i * (SCAN_UNROLL // 2) + s, carry)
            return carry

        cr, ci, accr, acci = lax.fori_loop(0, ntile // SCAN_UNROLL, pairs,
                                           (car[...], cai[...], gar_ref[...], gai_ref[...]))
        car[...] = cr
        cai[...] = ci
        gar_ref[...] = accr
        gai_ref[...] = acci

    spec = pl.BlockSpec((tb, w), lambda s, k: (nt - 1 - k, s))
    aspec = pl.BlockSpec((SUBLANES, w), lambda s, k: (0, s))
    sds = jax.ShapeDtypeStruct((t, n), BF16)
    asds = jax.ShapeDtypeStruct((SUBLANES, n), F32)
    return pl.pallas_call(
        body, name="scan_bwd", grid=(n // w, nt), out_shape=(sds, sds, asds, asds),
        in_specs=[aspec, spec, spec, spec, spec], out_specs=(spec, spec, aspec, aspec),
        scratch_shapes=[pltpu.VMEM((SUBLANES, w), F32), pltpu.VMEM((SUBLANES, w), F32)],
        compiler_params=_cparams(("parallel", "arbitrary"), VMEM_MID),
    )(a_rows, g_re, g_im, s_re, s_im)


def _mix_in(x, vec, w_in_st, b_re, b_im):
    t, d = x.shape
    ns, _, nc = w_in_st.shape
    dssm, nstate = b_re.shape
    du, ds = dssm // SSM_SPLIT, nstate // SSM_SPLIT
    tb = _blk(t, TB_MIX)

    def body(x_ref, vec_ref, w_ref, bre_ref, bim_ref, proj_ref, bur_ref, bui_ref, h1_ref):
        xv = x_ref[...]
        r = lax.rsqrt(_rowmean(xv * xv) + EPS)
        h = xv * r * vec_ref[0:1, :] * vec_ref[1:2, :] + vec_ref[2:3, :]
        hb = h.astype(BF16)
        h1_ref[...] = hb
        u = None
        for j in range(ns):
            pj = jnp.dot(hb, w_ref[j], preferred_element_type=F32)
            proj_ref[:, j * nc:(j + 1) * nc] = pj
            if j == 0:
                u = pj
        ub = u.astype(BF16)
        for q in range(SSM_SPLIT):
            rq, cq = slice(q * du, (q + 1) * du), slice(q * ds, (q + 1) * ds)
            bur_ref[:, cq] = jnp.dot(ub[:, rq], bre_ref[rq, cq], preferred_element_type=F32)
            bui_ref[:, cq] = jnp.dot(ub[:, rq], bim_ref[rq, cq], preferred_element_type=F32)

    return pl.pallas_call(
        body, name="mix_in", grid=(t // tb,),
        out_shape=(jax.ShapeDtypeStruct((t, ns * nc), F32), jax.ShapeDtypeStruct((t, nstate), F32),
                   jax.ShapeDtypeStruct((t, nstate), F32), jax.ShapeDtypeStruct((t, d), BF16)),
        in_specs=[_rows(tb, d), _full((SUBLANES, d)), _resident(w_in_st.shape), _resident(b_re.shape),
                  _resident(b_im.shape)],
        out_specs=(_rows(tb, ns * nc), _rows(tb, nstate), _rows(tb, nstate), _rows(tb, d)),
        compiler_params=_cparams(("parallel",), VMEM_BIG),
    )(x, vec, w_in_st, b_re, b_im)


def _head_ms(y, h_ref):
    return _split_dot(y * y, h_ref[...])


def _conv3(x, halo, w_ref):
    return w_ref[0:1, :] * _shift_down(x, halo, 2) + w_ref[1:2, :] * _shift_down(x, halo, 1) + w_ref[2:3, :] * x


def _mix_out(x, proj, s_re, s_im, c_re, c_im, v512, convw, glu_w, h16, h64, w_out, vd):
    t, d = x.shape
    dh = c_re.shape[1]
    nstate = s_re.shape[1]
    du, ds = dh // SSM_SPLIT, nstate // SSM_SPLIT
    tb = _blk(t, TB_MIX)

    def body(x_ref, u_ref, bg_ref, cg_ref, v_ref, cgh_ref, vh_ref, sr_ref, si_ref, cre_ref, cim_ref, p_ref,
             cw_ref, gw_ref, h16_ref, h64_ref, wo_ref, vd_ref, y1_ref, o_ref, x2_ref):
        i = pl.program_id(0)
        u = u_ref[...]
        ys = []
        for q in range(SSM_SPLIT):
            rq, cq = slice(q * ds, (q + 1) * ds), slice(q * du, (q + 1) * du)
            ys.append(_dot(sr_ref[:, rq], cre_ref[rq, cq]) - _dot(si_ref[:, rq], cim_ref[rq, cq]))
        ys = jnp.concatenate(ys, axis=1)
        y1 = ys + p_ref[0:1, :] * u
        y1_ref[...] = y1
        z = _gelu(y1)
        q = _dot(z, gw_ref[...]) + p_ref[1:2, :]
        ya = z * _sigmoid(q)
        na = ya * lax.rsqrt(_head_ms(ya, h16_ref) + EPS) * p_ref[2:3, :]
        cv = cg_ref[...] * v_ref[...]
        cvh = jnp.where(i > 0, cgh_ref[...] * vh_ref[...], 0.0)
        yb = bg_ref[...] * _conv3(cv, cvh, cw_ref)
        nb = yb * lax.rsqrt(_head_ms(yb, h64_ref) + EPS) * p_ref[3:4, :]
        o = _dot(na, wo_ref[0:dh, :]) + _dot(nb, wo_ref[dh:2 * dh, :])
        o_ref[...] = o
        on = o * lax.rsqrt(_rowmean(o * o) + EPS) * vd_ref[0:1, :]
        x2_ref[...] = x_ref[...] + vd_ref[1:2, :] * on

    return pl.pallas_call(
        body, name="mix_out", grid=(t // tb,),
        out_shape=(jax.ShapeDtypeStruct((t, dh), F32), jax.ShapeDtypeStruct((t, d), F32),
                   jax.ShapeDtypeStruct((t, d), F32)),
        in_specs=[_rows(tb, d), _rows(tb, dh, 0), _rows(tb, dh, 1), _rows(tb, dh, 2), _rows(tb, dh, 3),
                  _halo_prev(tb, dh, 2), _halo_prev(tb, dh, 3), _rows(tb, nstate), _rows(tb, nstate),
                  _full(c_re.shape), _full(c_im.shape), _full(v512.shape), _full(convw.shape), _full(glu_w.shape),
                  _full(h16.shape), _full(h64.shape), _full(w_out.shape), _full(vd.shape)],
        out_specs=(_rows(tb, dh), _rows(tb, d), _rows(tb, d)),
        compiler_params=_cparams(("parallel",), VMEM_BIG),
    )(x, proj, proj, proj, proj, proj, proj, s_re, s_im, c_re, c_im, v512, convw, glu_w, h16, h64, w_out, vd)


def _ffn_up(x2, vec, w_up_st):
    t, d = x2.shape
    ns, _, nc = w_up_st.shape
    tb = _blk(t, TB_FFN)

    def body(x_ref, vec_ref, w_ref, up_ref, h2_ref):
        xv = x_ref[...]
        r = lax.rsqrt(_rowmean(xv * xv) + EPS)
        h = xv * r * vec_ref[0:1, :] * vec_ref[1:2, :] + vec_ref[2:3, :]
        hb = h.astype(BF16)
        h2_ref[...] = hb
        for j in range(ns):
            up_ref[:, j * nc:(j + 1) * nc] = jnp.dot(hb, w_ref[j], preferred_element_type=F32)

    return pl.pallas_call(
        body, name="ffn_up", grid=(t // tb,),
        out_shape=(jax.ShapeDtypeStruct((t, ns * nc), F32), jax.ShapeDtypeStruct((t, d), BF16)),
        in_specs=[_rows(tb, d), _full((SUBLANES, d)), _resident(w_up_st.shape)],
        out_specs=(_rows(tb, ns * nc), _rows(tb, d)),
        compiler_params=_cparams(("parallel",), VMEM_BIG),
    )(x2, vec, w_up_st)


def _ffn_down(up, fw, w_down, w_down_t, x2, tgt, vd):
    t, nh = up.shape
    dff, d = w_down.shape
    tb = _blk(t, TB_FFN)
    inv_d = 1.0 / d

    def body(up_ref, uph_ref, fw_ref, wd_ref, wdt_ref, x2_ref, tgt_ref, vd_ref,
             act_ref, ddn_ref, dout_ref, dhid_ref, vec_ref, loss_ref, a_s, vv_s, sg_s):
        i = pl.program_id(0)

        def conv_cols(sl):
            x = up_ref[:, sl]
            halo = jnp.where(i > 0, uph_ref[:, sl], 0.0)
            return (fw_ref[0:1, sl] * _shift_down(x, halo, 2) + fw_ref[1:2, sl] * _shift_down(x, halo, 1)
                    + fw_ref[2:3, sl] * x)

        dn = None
        for o in range(0, dff, CW_FFN):
            sl = slice(o, o + CW_FFN)
            a = conv_cols(sl)
            vv = conv_cols(slice(dff + o, dff + o + CW_FFN))
            sg = _sigmoid(a)
            si = a * sg
            a_s[:, sl] = si
            vv_s[:, sl] = vv
            sg_s[:, sl] = sg
            actb = (si * vv).astype(BF16)
            act_ref[:, sl] = actb
            pj = lax.dot_general(actb, wdt_ref[:, sl], (((1,), (1,)), ((), ())), preferred_element_type=F32)
            dn = pj if dn is None else dn + pj
        r3 = lax.rsqrt(_rowmean(dn * dn) + EPS)
        xn = dn * r3
        g = vd_ref[0:1, :]
        gt2 = vd_ref[1:2, :]
        dnn = xn * g
        diff = x2_ref[...] + gt2 * dnn - tgt_ref[...]
        part = 0.5 * inv_d * jnp.sum(diff * diff)

        @pl.when(i == 0)
        def _():
            loss_ref[...] = jnp.zeros(loss_ref.shape, F32)
        loss_ref[...] += part
        dout = diff * inv_d
        dout_ref[...] = dout
        ddnn = dout * gt2
        _acc_rows(vec_ref, i == 0, [_colsum(dout * dnn), _colsum(ddnn * xn)])
        dxn = ddnn * g
        ddn = r3 * (dxn - xn * _rowmean(dxn * xn))
        ddnb = ddn.astype(BF16)
        ddn_ref[...] = ddnb
        for o in range(0, dff, CW_FFN):
            sl = slice(o, o + CW_FFN)
            dact = lax.dot_general(ddnb, wd_ref[sl, :], (((1,), (1,)), ((), ())), preferred_element_type=F32)
            si, vv, sg = a_s[:, sl], vv_s[:, sl], sg_s[:, sl]
            dhid_ref[:, sl] = (dact * vv * (sg + si * (1.0 - sg))).astype(BF16)
            dhid_ref[:, dff + o:dff + o + CW_FFN] = (dact * si).astype(BF16)

    return pl.pallas_call(
        body, name="ffn_down", grid=(t // tb,),
        scratch_shapes=[pltpu.VMEM((tb, dff), F32)] * 3,
        out_shape=(jax.ShapeDtypeStruct((t, dff), BF16), jax.ShapeDtypeStruct((t, d), BF16),
                   jax.ShapeDtypeStruct((t, d), F32), jax.ShapeDtypeStruct((t, nh), BF16),
                   jax.ShapeDtypeStruct((SUBLANES, d), F32), jax.ShapeDtypeStruct((SUBLANES, 128), F32)),
        in_specs=[_rows(tb, nh), _halo_prev(tb, nh), _full(fw.shape), _resident(w_down.shape),
                  _resident(w_down_t.shape), _rows(tb, d),
                  _rows(tb, d), _full(vd.shape)],
        out_specs=(_rows(tb, dff), _rows(tb, d), _rows(tb, d), _rows(tb, nh), _full((SUBLANES, d)),
                   _full((SUBLANES, 128))),
        compiler_params=_cparams(("arbitrary",), VMEM_BIG),
    )(up, up, fw, w_down, w_down_t, x2, tgt, vd)


def _ffn_up_bwd(dhid, up, fw, x2, dout, vec, w_up_st):
    t, nh = dhid.shape
    d = x2.shape[1]
    ns, _, nc = w_up_st.shape
    tb = _blk(t, TB_FFN)
    nblk = t // tb
    cw = 128

    def body(dh_ref, dhn_ref, up_ref, fw_ref, x2_ref, dout_ref, vec_ref, w_ref,
             dx2_ref, dup_ref, vp_ref, df_ref):
        i = pl.program_id(0)

        @pl.when(i == 0)
        def _():
            df_ref[...] = jnp.zeros(df_ref.shape, F32)
        dh2 = None
        for j in range(ns):
            for o in range(j * nc, (j + 1) * nc, cw):
                sl = slice(o, o + cw)
                dh = dh_ref[:, sl].astype(F32)
                dhn = jnp.where(i < nblk - 1, dhn_ref[:, sl].astype(F32), 0.0)
                dh1 = _shift_up(dh, dhn, 1)
                dh2s = _shift_up(dh, dhn, 2)
                dup_ref[:, sl] = (fw_ref[2:3, sl] * dh + fw_ref[1:2, sl] * dh1 + fw_ref[0:1, sl] * dh2s).astype(BF16)
                up_v = up_ref[:, sl]
                df_ref[0:1, sl] += _colsum(dh2s * up_v)
                df_ref[1:2, sl] += _colsum(dh1 * up_v)
                df_ref[2:3, sl] += _colsum(dh * up_v)
            pj = lax.dot_general(dup_ref[:, j * nc:(j + 1) * nc], w_ref[j], (((1,), (1,)), ((), ())),
                                 preferred_element_type=F32)
            dh2 = pj if dh2 is None else dh2 + pj
        xv = x2_ref[...]
        r = lax.rsqrt(_rowmean(xv * xv) + EPS)
        xn = xv * r
        g = vec_ref[0:1, :]
        hg = xn * g
        dhg = dh2 * vec_ref[1:2, :]
        _acc_rows(vp_ref, i == 0, [_colsum(dh2), _colsum(dh2 * hg), _colsum(dhg * xn)])
        dxn = dhg * g
        dx2_ref[...] = dout_ref[...] + r * (dxn - xn * _rowmean(dxn * xn))

    return pl.pallas_call(
        body, name="ffn_up_bwd", grid=(nblk,),
        out_shape=(jax.ShapeDtypeStruct((t, d), F32), jax.ShapeDtypeStruct((t, nh), BF16),
                   jax.ShapeDtypeStruct((SUBLANES, d), F32), jax.ShapeDtypeStruct((SUBLANES, nh), F32)),
        in_specs=[_rows(tb, nh), _halo_next(tb, nh, t, rows=BF16_ROWS), _rows(tb, nh), _full(fw.shape),
                  _rows(tb, d), _rows(tb, d), _full(vec.shape), _resident(w_up_st.shape)],
        out_specs=(_rows(tb, d), _rows(tb, nh), _full((SUBLANES, d)), _full((SUBLANES, nh))),
        compiler_params=_cparams(("arbitrary",), VMEM_BIG),
    )(dhid, dhid, up, fw, x2, dout, vec, w_up_st)


def _mix_out_bwd(dx2, o, y1, proj, s_re, s_im, c_re, c_im, v512, convw, glu_w, h16, h64, w_out, vd):
    t, d = dx2.shape
    dh = y1.shape[1]
    nstate = c_re.shape[0]
    du, ds = dh // SSM_SPLIT, nstate // SSM_SPLIT
    tb = _blk(t, TB_MIX)

    def body(dx2_ref, o_ref, y1_ref, u_ref, bg_ref, cg_ref, v_ref, cgh_ref, vh_ref, cre_ref, cim_ref, p_ref,
             cw_ref, gw_ref, h16_ref, h64_ref, wo_ref, vd_ref, sr_ref, si_ref,
             do_ref, ycat_ref, z_ref, dq_ref, dy1_ref, gr_ref, gi_ref, dcc_ref, dbg_ref, vpd_ref, vp5_ref,
             dcr_ref, dci_ref):
        i = pl.program_id(0)
        first = i == 0

        @pl.when(first)
        def _():
            dcr_ref[...] = jnp.zeros(dcr_ref.shape, F32)
            dci_ref[...] = jnp.zeros(dci_ref.shape, F32)
        ov = o_ref[...]
        ro = lax.rsqrt(_rowmean(ov * ov) + EPS)
        on_ = ov * ro
        g = vd_ref[0:1, :]
        dx2v = dx2_ref[...]
        don = dx2v * vd_ref[1:2, :]
        _acc_rows(vpd_ref, first, [_colsum(dx2v * on_ * g), _colsum(don * on_)])
        dxn = don * g
        dob = (ro * (dxn - on_ * _rowmean(dxn * on_))).astype(BF16)
        do_ref[...] = dob
        dyc_a =lax.dot_general(dob, wo_ref[0:dh, :], (((1,), (1,)), ((), ())), preferred_element_type=F32)
        dyc_b = lax.dot_general(dob, wo_ref[dh:2 * dh, :], (((1,), (1,)), ((), ())), preferred_element_type=F32)
        y1v = y1_ref[...]
        u = u_ref[...]
        z, dz_dy1 = _gelu_and_grad(y1v)
        zb = z.astype(BF16)
        sg = _sigmoid(jnp.dot(zb, gw_ref[...], preferred_element_type=F32) + p_ref[1:2, :])
        ya = z * sg
        ra = lax.rsqrt(_head_ms(ya, h16_ref) + EPS)
        yan = ya * ra
        ga = p_ref[2:3, :]
        ycat_ref[:, 0:dh] = (yan * ga).astype(BF16)
        dyn = dyc_a * ga
        dya = ra * (dyn - yan * _split_dot(dyn * yan, h16_ref[...]))
        dq = dya * z * sg * (1.0 - sg)
        dqb = dq.astype(BF16)
        z_ref[...] = zb
        dq_ref[...] = dqb
        dz = dya * sg + lax.dot_general(dqb, gw_ref[...], (((1,), (1,)), ((), ())), preferred_element_type=F32)
        dy1 = dz * dz_dy1
        dy1_ref[...] = dy1
        dy1b = dy1.astype(BF16)
        for q in range(SSM_SPLIT):
            rq, cq = slice(q * ds, (q + 1) * ds), slice(q * du, (q + 1) * du)
            gr_ref[:, rq] = lax.dot_general(dy1b[:, cq], cre_ref[rq, cq], (((1,), (1,)), ((), ())),
                                            preferred_element_type=F32)
            gi_ref[:, rq] = -lax.dot_general(dy1b[:, cq], cim_ref[rq, cq], (((1,), (1,)), ((), ())),
                                             preferred_element_type=F32)
            dcr_ref[rq, :] += _dot_tn(sr_ref[:, rq], dy1b[:, cq])
            dci_ref[rq, :] += _dot_tn(si_ref[:, rq], dy1b[:, cq])
        bg = bg_ref[...]
        cv = cg_ref[...] * v_ref[...]
        cvh = jnp.where(i > 0, cgh_ref[...] * vh_ref[...], 0.0)
        cv1 = _shift_down(cv, cvh, 1)
        cv2 = _shift_down(cv, cvh, 2)
        cc = cw_ref[0:1, :] * cv2 + cw_ref[1:2, :] * cv1 + cw_ref[2:3, :] * cv
        yb = bg * cc
        rb = lax.rsqrt(_head_ms(yb, h64_ref) + EPS)
        ybn = yb * rb
        gb = p_ref[3:4, :]
        ycat_ref[:, dh:2 * dh] = (ybn * gb).astype(BF16)
        dynb = dyc_b * gb
        dyb = rb * (dynb - ybn * _split_dot(dynb * ybn, h64_ref[...]))
        dcc = dyb * bg
        dbg_ref[...] = dyb * cc
        dcc_ref[...] = dcc
        _acc_rows(vp5_ref, first, [_colsum(dyc_a * yan), _colsum(dyc_b * ybn), _colsum(dq), _colsum(dy1 * u),
                                   _colsum(dcc * cv2), _colsum(dcc * cv1), _colsum(dcc * cv)])

    return pl.pallas_call(
        body, name="mix_out_bwd", grid=(t // tb,),
        out_shape=(jax.ShapeDtypeStruct((t, d), BF16), jax.ShapeDtypeStruct((t, 2 * dh), BF16),
                   jax.ShapeDtypeStruct((t, dh), BF16), jax.ShapeDtypeStruct((t, dh), BF16),
                   jax.ShapeDtypeStruct((t, dh), F32), jax.ShapeDtypeStruct((t, nstate), F32),
                   jax.ShapeDtypeStruct((t, nstate), F32), jax.ShapeDtypeStruct((t, dh), F32),
                   jax.ShapeDtypeStruct((t, dh), F32), jax.ShapeDtypeStruct((SUBLANES, d), F32),
                   jax.ShapeDtypeStruct((SUBLANES, dh), F32), jax.ShapeDtypeStruct((nstate, du), F32),
                   jax.ShapeDtypeStruct((nstate, du), F32)),
        in_specs=[_rows(tb, d), _rows(tb, d), _rows(tb, dh), _rows(tb, dh, 0), _rows(tb, dh, 1), _rows(tb, dh, 2),
                  _rows(tb, dh, 3), _halo_prev(tb, dh, 2), _halo_prev(tb, dh, 3), _resident(c_re.shape),
                  _resident(c_im.shape), _full(v512.shape), _full(convw.shape), _resident(glu_w.shape),
                  _resident(h16.shape), _resident(h64.shape), _resident(w_out.shape), _full(vd.shape),
                  _rows(tb, nstate), _rows(tb, nstate)],
        out_specs=(_rows(tb, d), _rows(tb, 2 * dh), _rows(tb, dh), _rows(tb, dh), _rows(tb, dh), _rows(tb, nstate),
                   _rows(tb, nstate), _rows(tb, dh), _rows(tb, dh), _full((SUBLANES, d)), _full((SUBLANES, dh)),
                   _full((nstate, du)), _full((nstate, du))),
        compiler_params=_cparams(("arbitrary",), VMEM_BIG),
    )(dx2, o, y1, proj, proj, proj, proj, proj, proj, c_re, c_im, v512, convw, glu_w, h16, h64, w_out, vd,
      s_re, s_im)


def _mix_in_bwd(gt_re, gt_im, b_re, b_im, dy1, dcc, dbg, proj, x, dx2, vec, v512, convw, w_in_st):
    t, d = x.shape
    dh = dy1.shape[1]
    nstate = gt_re.shape[1]
    du_w, ds = dh // SSM_SPLIT, nstate // SSM_SPLIT
    ns, _, nc = w_in_st.shape
    tb = _blk(t, TB_MIX)
    nblk = t // tb

    def body(gr_ref, gi_ref, bre_ref, bim_ref, dy1_ref, dcc_ref, dccn_ref, dbg_ref, u_ref, cg_ref, v_ref, x_ref,
             dx2_ref, vec_ref, p_ref, cw_ref, w_ref, gx_ref, dproj_ref, vp_ref, dbr_ref, dbi_ref):
        i = pl.program_id(0)

        @pl.when(i == 0)
        def _():
            dbr_ref[...] = jnp.zeros(dbr_ref.shape, F32)
            dbi_ref[...] = jnp.zeros(dbi_ref.shape, F32)
        ub = u_ref[...].astype(BF16)
        du = []
        for q in range(SSM_SPLIT):
            rq, cq = slice(q * du_w, (q + 1) * du_w), slice(q * ds, (q + 1) * ds)
            du.append(lax.dot_general(gr_ref[:, cq].astype(BF16), bre_ref[rq, cq], (((1,), (1,)), ((), ())),
                                      preferred_element_type=F32)
                      + lax.dot_general(gi_ref[:, cq].astype(BF16), bim_ref[rq, cq], (((1,), (1,)), ((), ())),
                                        preferred_element_type=F32))
            dbr_ref[rq, :] += _dot_tn(ub[:, rq], gr_ref[:, cq])
            dbi_ref[rq, :] += _dot_tn(ub[:, rq], gi_ref[:, cq])
        du = dy1_ref[...] * p_ref[0:1, :] + jnp.concatenate(du, axis=1)
        dcc = dcc_ref[...]
        dccn = jnp.where(i < nblk - 1, dccn_ref[...], 0.0)
        dcv = (cw_ref[2:3, :] * dcc + cw_ref[1:2, :] * _shift_up(dcc, dccn, 1)
               + cw_ref[0:1, :] * _shift_up(dcc, dccn, 2))
        parts = [du, dbg_ref[...], dcv * v_ref[...], dcv * cg_ref[...]]
        xv = x_ref[...]
        r = lax.rsqrt(_rowmean(xv * xv) + EPS)
        xn = xv * r
        g = vec_ref[0:1, :]
        hg = xn * g
        dh1 = None
        for j in range(ns):
            pb = parts[j].astype(BF16)
            dproj_ref[:, j * nc:(j + 1) * nc] = pb
            pj =lax.dot_general(pb, w_ref[j], (((1,), (1,)), ((), ())), preferred_element_type=F32)
            dh1 = pj if dh1 is None else dh1 + pj
        dhg = dh1 * vec_ref[1:2, :]
        _acc_rows(vp_ref, i == 0, [_colsum(dh1), _colsum(dh1 * hg), _colsum(dhg * xn)])
        dxn = dhg * g
        gx_ref[...] = dx2_ref[...] + r * (dxn - xn * _rowmean(dxn * xn))

    assert nc == dh and ns == 4
    return pl.pallas_call(
        body, name="mix_in_bwd", grid=(nblk,),
        out_shape=(jax.ShapeDtypeStruct((t, d), F32), jax.ShapeDtypeStruct((t, ns * nc), BF16),
                   jax.ShapeDtypeStruct((SUBLANES, d), F32), jax.ShapeDtypeStruct((dh, ds), F32),
                   jax.ShapeDtypeStruct((dh, ds), F32)),
        in_specs=[_rows(tb, nstate), _rows(tb, nstate), _resident(b_re.shape), _resident(b_im.shape), _rows(tb, dh),
                  _rows(tb, dh), _halo_next(tb, dh, t), _rows(tb, dh), _rows(tb, dh, 0), _rows(tb, dh, 2),
                  _rows(tb, dh, 3), _rows(tb, d), _rows(tb, d), _full(vec.shape), _full(v512.shape),
                  _full(convw.shape), _resident(w_in_st.shape)],
        out_specs=(_rows(tb, d), _rows(tb, ns * nc), _full((SUBLANES, d)), _full((dh, ds)), _full((dh, ds))),
        compiler_params=_cparams(("arbitrary",), VMEM_BIG),
    )(gt_re, gt_im, b_re, b_im, dy1, dcc, dcc, dbg, proj, proj, proj, x, dx2, vec, v512, convw, w_in_st)


def _matmul_tn(a, b, m, bn, out_dtype, name, diag=False, bt=TB_TN, after=None):
    t = a.shape[0]
    n = b.shape[1]
    bt = _blk(t, bt)
    nk = t // bt
    extra = [] if after is None else [after]
    a_map = (lambda j, k: (k, j)) if diag else (lambda j, k: (k, 0))

    def body(a_ref, b_ref, *rest):
        o_ref, acc_ref = rest[-2:]
        k = pl.program_id(1)

        @pl.when(k == 0)
        def _():
            acc_ref[...] = jnp.zeros(acc_ref.shape, F32)
        acc_ref[...] += _dot_tn(a_ref[...], b_ref[...])

        @pl.when(k == nk - 1)
        def _():
            o_ref[...] = acc_ref[...].astype(out_dtype)

    return pl.pallas_call(
        body, name=name, grid=(n // bn, nk),
        out_shape=jax.ShapeDtypeStruct((n // bn, m, bn), out_dtype),
        in_specs=[pl.BlockSpec((bt, m), a_map), pl.BlockSpec((bt, bn), lambda j, k: (k, j))]
        + [pl.BlockSpec(memory_space=pl.ANY)] * len(extra),
        out_specs=pl.BlockSpec((None, m, bn), lambda j, k: (j, 0, 0)),
        scratch_shapes=[pltpu.VMEM((m, bn), F32)],
        compiler_params=_cparams(("parallel", "arbitrary"), VMEM_BIG),
    )(a, b, *extra)


def _ssm_bgrad(d_bre, d_bim, bt_re, bt_im, rows_in, fold, tile_b):
    gh, cb = d_bre.shape
    nb = SSM_SPLIT
    rb = gh // nb
    gp = nb * cb
    p = fold.shape[1]

    def body(dr_ref, di_ref, br_ref, bi_ref, rin_ref, f_ref, tb_ref, dbr_ref, dbi_ref, rout_ref):
        row = lax.broadcasted_iota(jnp.int32, (rb, cb), 0)
        col = lax.broadcasted_iota(jnp.int32, (rb, cb), 1)
        mask = (row >> 4) == (col >> 6)
        gr = jnp.where(mask, dr_ref[...], 0.0)
        gi = jnp.where(mask, di_ref[...], 0.0)
        cr, ci = rin_ref[0:1, :], rin_ref[1:2, :]
        dbr_ref[...] = _split3_dot(cr * gr + ci * gi, f_ref[...])
        dbi_ref[...] = _split3_dot(cr * gi - ci * gr, f_ref[...])
        br = _split3_dot(br_ref[...], tb_ref[...])
        bi = _split3_dot(bi_ref[...], tb_ref[...])
        rout_ref[...] = jnp.zeros(rout_ref.shape, F32)
        rout_ref[0:1, :] = _colsum(br * gr + bi * gi)
        rout_ref[1:2, :] = _colsum(br * gi - bi * gr)

    dspec = pl.BlockSpec((rb, cb), lambda j: (j, 0))
    rspec = pl.BlockSpec((SUBLANES, cb), lambda j: (0, j))
    ospec = pl.BlockSpec((rb, p), lambda j: (j, 0))
    return pl.pallas_call(
        body, name="ssm_bgrad", grid=(nb,),
        out_shape=(jax.ShapeDtypeStruct((gh, p), F32), jax.ShapeDtypeStruct((gh, p), F32),
                   jax.ShapeDtypeStruct((SUBLANES, gp), F32)),
        in_specs=[dspec, dspec, ospec, ospec, rspec, _full(fold.shape), _full(tile_b.shape)],
        out_specs=(ospec, ospec, rspec),
        compiler_params=_cparams(("parallel",)),
    )(d_bre, d_bim, bt_re, bt_im, rows_in, fold, tile_b)


def _ssm_cgrad(d_cre, d_cim, fold):
    gp, cb = d_cre.shape
    nb = SSM_SPLIT
    rb = gp // nb
    h = fold.shape[1]

    def body(dr_ref, di_ref, f_ref, cr_ref, ci_ref):
        row = lax.broadcasted_iota(jnp.int32, (rb, cb), 0)
        col = lax.broadcasted_iota(jnp.int32, (rb, cb), 1)
        mask = (row >> 6) == (col >> 4)
        cr_ref[...] = _split3_dot(jnp.where(mask, dr_ref[...], 0.0), f_ref[...])
        ci_ref[...] = -_split3_dot(jnp.where(mask, di_ref[...], 0.0), f_ref[...])

    cspec = pl.BlockSpec((rb, cb), lambda j: (j, 0))
    ospec = pl.BlockSpec((rb, h), lambda j: (j, 0))
    return pl.pallas_call(
        body, name="ssm_cgrad", grid=(nb,),
        out_shape=(jax.ShapeDtypeStruct((gp, h), F32),) * 2,
        in_specs=[cspec, cspec, _full(fold.shape)], out_specs=(ospec, ospec),
        compiler_params=_cparams(("parallel",)),
    )(d_cre, d_cim, fold)


def _ssm_lamgrad(lam_re, lam_im, log_step, abar_re, abar_im, coef_re, coef_im, gc_re, gc_im, ga_re, ga_im):
    g, p = lam_re.shape

    def body(lr_ref, li_ref, ls_ref, ar_ref, ai_ref, cr_ref, ci_ref, gcr_ref, gci_ref, gar_ref, gai_ref,
             dlr_ref, dli_ref, dls_ref):
        lam_raw = lr_ref[...]
        lr = jnp.minimum(lam_raw, LAMBDA_RE_MAX)
        li = li_ref[...]
        st = jnp.exp(ls_ref[...])
        den = lr * lr + li * li
        gcr, gci = gcr_ref[...], gci_ref[...]
        gab_r = gar_ref[...] + (lr * gcr - li * gci) / den
        gab_i = gai_ref[...] + (lr * gci + li * gcr) / den
        cr, ci = cr_ref[...], ci_ref[...]
        wr = -(cr * lr + ci * li) / den
        wi = -(ci * lr - cr * li) / den
        gl_r = wr * gcr + wi * gci
        gl_i = wr * gci - wi * gcr
        ar, ai = ar_ref[...], ai_ref[...]
        gw_r = ar * gab_r + ai * gab_i
        gw_i = ar * gab_i - ai * gab_r
        gl_r = gl_r + st * gw_r
        gl_i = gl_i + st * gw_i
        pass_through = jnp.where(lam_raw < LAMBDA_RE_MAX, 1.0, jnp.where(lam_raw == LAMBDA_RE_MAX, 0.5, 0.0))
        dlr_ref[...] = gl_r * pass_through
        dli_ref[...] = gl_i
        dls_ref[...] = st * jnp.sum(lr * gw_r + li * gw_i, axis=1, keepdims=True)

    sds = jax.ShapeDtypeStruct((g, p), F32)
    return pl.pallas_call(body, name="ssm_lamgrad", out_shape=(sds, sds, jax.ShapeDtypeStruct((g, 1), F32)))(
        lam_re, lam_im, log_step, abar_re, abar_im, coef_re, coef_im, gc_re, gc_im, ga_re, ga_im)


def _row_block(r, most=256):
    for rb in range(min(r, most), BF16_ROWS - 1, -1):
        if r % rb == 0 and rb % BF16_ROWS == 0:
            return rb
    return r


def _adamw_math(w, g, m, v):
    m = ADAM_B1 * m + (1.0 - ADAM_B1) * g
    v = ADAM_B2 * v + (1.0 - ADAM_B2) * (g * g)
    m_hat = m / (1.0 - ADAM_B1 ** ADAM_STEP)
    v_hat = v / (1.0 - ADAM_B2 ** ADAM_STEP)
    delta = -ADAM_LR * (m_hat / (jnp.sqrt(v_hat) + ADAM_EPS) + ADAM_WD * w)
    return delta, m, v


def _adamw_big(p_mine, p_sib, w, m, v, name):
    r, c = w.shape
    rb = _row_block(r)

    def body(a_ref, b_ref, w_ref, m_ref, v_ref, g_ref, d_ref, mo_ref, vo_ref):
        g = a_ref[...] + b_ref[...]
        g_ref[...] = g
        d_ref[...], mo_ref[...], vo_ref[...] = _adamw_math(w_ref[...], g, m_ref[...], v_ref[...])

    spec = pl.BlockSpec((rb, c), lambda i: (i, 0))
    sds = jax.ShapeDtypeStruct((r, c), F32)
    return pl.pallas_call(
        body, name=name, grid=(r // rb,), out_shape=(sds,) * 4, in_specs=[spec] * 5, out_specs=(spec,) * 4,
        compiler_params=_cparams(("parallel",), VMEM_MID),
    )(p_mine, p_sib, w, m, v)


def _sum_blocks(stack, name):
    n, r, c = stack.shape
    rb = _row_block(r)

    def body(s_ref, o_ref):
        acc = s_ref[0].astype(F32)
        for k in range(1, n):
            acc = acc + s_ref[k].astype(F32)
        o_ref[...] = acc

    return pl.pallas_call(
        body, name=name, grid=(r // rb,), out_shape=jax.ShapeDtypeStruct((r, c), F32),
        in_specs=[pl.BlockSpec((n, rb, c), lambda i: (0, i, 0))], out_specs=pl.BlockSpec((rb, c), lambda i: (i, 0)),
        compiler_params=_cparams(("parallel",), VMEM_MID),
    )(stack)


def _add2(a, b):
    def body(a_ref, b_ref, o_ref):
        o_ref[...] = a_ref[...] + b_ref[...]

    return pl.pallas_call(body, name="add_small", out_shape=jax.ShapeDtypeStruct(a.shape, F32))(a, b)


def _adamw_ada(c_all, dmod_cols, w, m, v):
    d, n = w.shape
    bn = 512

    def body(c_ref, dm_ref, w_ref, m_ref, v_ref, g_ref, d_ref, mo_ref, vo_ref):
        cc = c_ref[...]
        g = _dot_tn(cc * _sigmoid(cc), dm_ref[...])
        g_ref[...] = g
        d_ref[...], mo_ref[...], vo_ref[...] = _adamw_math(w_ref[...], g, m_ref[...], v_ref[...])

    spec = pl.BlockSpec((d, bn), lambda j: (0, j))
    sds = jax.ShapeDtypeStruct((d, n), F32)
    return pl.pallas_call(
        body, name="adamw_ada", grid=(n // bn,), out_shape=(sds,) * 4,
        in_specs=[_full((N_DEV, d)), pl.BlockSpec((N_DEV, bn), lambda j: (0, j)), spec, spec, spec],
        out_specs=(spec,) * 4, compiler_params=_cparams(("parallel",)),
    )(c_all, dmod_cols, w, m, v)


def _adamw_small(items):
    n = len(items)

    def body(*refs):
        ins, outs = refs[:4 * n], refs[4 * n:]
        for k in range(n):
            w_ref, g_ref, m_ref, v_ref = ins[4 * k:4 * k + 4]
            outs[3 * k][...], outs[3 * k + 1][...], outs[3 * k + 2][...] = _adamw_math(
                w_ref[...], g_ref[...], m_ref[...], v_ref[...])

    flat = [a for it in items for a in it]
    out_shape = tuple(jax.ShapeDtypeStruct(it[0].shape, F32) for it in items for _ in range(3))
    res = pl.pallas_call(body, name="adamw_small", out_shape=out_shape,
                         compiler_params=_cparams(vmem=VMEM_BIG))(*flat)
    return [tuple(res[3 * k:3 * k + 3]) for k in range(n)]


def _group_mean_matrix(n, group):
    idx = np.arange(n) // group
    return (idx[:, None] == idx[None, :]).astype(np.float32) / group


def _fold_matrix(n, period):
    return (np.arange(n)[:, None] % period == np.arange(period)[None, :]).astype(np.float32)


def _rows8(*rows):
    c = rows[0].shape[-1]
    pad = jnp.zeros((SUBLANES - len(rows), c), F32)
    return jnp.concatenate([r.reshape(1, c) for r in rows] + [pad], axis=0)


def _to_rows(a, width):
    flat = a.reshape(-1)
    n = -(-flat.shape[0] // width)
    flat = jnp.pad(flat, (0, n * width - flat.shape[0]))
    return flat.reshape(n, width)


def kernel(x, c, w_ada, b_ada, g_pre_mix, g_post_mix, w_in, ssm_lam_re, ssm_lam_im, ssm_log_step, ssm_b_re, ssm_b_im, ssm_c_re, ssm_c_im, ssm_d, glu_w, glu_b, g_out_ssm, conv_w, g_out_conv, w_out, g_pre_ffn, g_post_ffn, w_up, ffn_conv_w, w_down, loss_target, m_w_ada, m_b_ada, m_g_pre_mix, m_g_post_mix, m_w_in, m_ssm_lam_re, m_ssm_lam_im, m_ssm_log_step, m_ssm_b_re, m_ssm_b_im, m_ssm_c_re, m_ssm_c_im, m_ssm_d, m_glu_w, m_glu_b, m_g_out_ssm, m_conv_w, m_g_out_conv, m_w_out, m_g_pre_ffn, m_g_post_ffn, m_w_up, m_ffn_conv_w, m_w_down, v_w_ada, v_b_ada, v_g_pre_mix, v_g_post_mix, v_w_in, v_ssm_lam_re, v_ssm_lam_im, v_ssm_log_step, v_ssm_b_re, v_ssm_b_im, v_ssm_c_re, v_ssm_c_im, v_ssm_d, v_glu_w, v_glu_b, v_g_out_ssm, v_conv_w, v_g_out_conv, v_w_out, v_g_pre_ffn, v_g_post_ffn, v_w_up, v_ffn_conv_w, v_w_down):
    xs = x[0]
    tgt = loss_target[0]
    t, d = xs.shape
    xi, yi, ci = lax.axis_index("x"), lax.axis_index("y"), lax.axis_index("c")
    chip = 2 * xi + yi
    dev = 2 * chip + ci

    n_groups, n_state = ssm_lam_re.shape[1:]
    n_gch = ssm_b_re.shape[3]
    d_ssm = n_groups * n_gch
    gp = n_groups * n_state
    n_ada = w_ada.shape[2]
    d_ff = w_down.shape[1] * N_CHIPS
    n_upc = w_up.shape[2]

    w_names = ("w_in", "glu_w", "w_out", "w_up", "w_down")
    c_gath, _ = _allgather8(jnp.broadcast_to(c, (SUBLANES, d)), SUBLANES, "gather_c")
    c_all = c_gath.reshape(N_DEV, SUBLANES, d)[:, 0, :]

    def pad8(a):
        return jnp.concatenate([a, jnp.zeros((SUBLANES - a.shape[0], a.shape[1]), a.dtype)], axis=0)

    def start(name, arrs, after):
        return _chips_start(name, True, [], [_landing(a, chip) for a in arrs], after)

    w_names = ("w_in", "mod", "conv_w", "ffn_conv_w", "glu_w", "w_out", "w_up", "w_down")
    first = start("weights_start_in", [w_in[0].astype(BF16)], c_gath)
    b_sh = lax.dynamic_slice(b_ada, (0, chip * n_ada), (1, n_ada))
    mod_sh = _mod_shard(c_all + first[4][0:1, 0:1], w_ada[0], b_sh)
    second = start("weights_start_mod", [mod_sh, pad8(conv_w[0]), pad8(ffn_conv_w[0])], None)
    third = start("weights_start_rest", [w[0].astype(BF16) for w in (glu_w, w_out, w_up, w_down)], second[4])
    w_send, w_recv, w_land = [list(first[k]) + list(second[k]) + list(third[k]) for k in (0, 1, 3)]
    w_token = third[4]

    def weights(names, after):
        ks = [w_names.index(nm) for nm in names]
        return _chips_wait("weights_wait_" + names[-1], True, [w_send[k] for k in ks], [w_recv[k] for k in ks],
                           [], [w_land[k] for k in ks], after)

    lam_re, lam_im = ssm_lam_re[0], ssm_lam_im[0]
    log_step = ssm_log_step[0].reshape(n_groups, 1) + w_token[0:1, 0:1]
    abar_re, abar_im, coef_re, coef_im = _ssm_prep(lam_re, lam_im, log_step)
    a_rows = _rows8(abar_re.reshape(1, gp), abar_im.reshape(1, gp))
    coef_rows = _rows8(coef_re.reshape(1, gp), coef_im.reshape(1, gp))
    bt_re = ssm_b_re[0].transpose(0, 2, 1).reshape(d_ssm, n_state)
    bt_im = ssm_b_im[0].transpose(0, 2, 1).reshape(d_ssm, n_state)
    ct_re = ssm_c_re[0].transpose(0, 2, 1).reshape(gp, n_gch)
    ct_im = ssm_c_im[0].transpose(0, 2, 1).reshape(gp, n_gch)
    tile_b = jnp.asarray(np.tile(np.eye(n_state), (1, n_groups // SSM_SPLIT)), BF16)
    tile_c = jnp.asarray(np.tile(np.eye(n_gch), (1, n_groups)), BF16)
    bblk_re, bblk_im, cblk_re, cblk_im = _ssm_blocks(bt_re, bt_im, ct_re, ct_im, coef_rows, tile_b, tile_c)

    h16 = jnp.asarray(_group_mean_matrix(d_ssm, n_gch), BF16)
    h64 = jnp.asarray(_group_mean_matrix(d_ssm, CONV_HEAD_DIM), BF16)

    g_mod, g_cw, g_fw, w_in_st = weights(("mod", "conv_w", "ffn_conv_w", "w_in"), bblk_re)
    mod_all = g_mod.transpose(1, 0, 2).reshape(N_DEV, N_CHIPS * n_ada)
    mod = lax.dynamic_slice(mod_all, (dev, 0), (1, N_CHIPS * n_ada))
    sh1, sc1, gt1, sh2, sc2, gt2 = [mod[:, k * d:(k + 1) * d] for k in range(6)]
    convw_full = pad8(g_cw[:, :3, :].transpose(1, 0, 2).reshape(3, d_ssm))
    fw_full = pad8(g_fw[:, :3, :].transpose(1, 0, 2).reshape(3, N_CHIPS * n_upc))

    v512 = _rows8(ssm_d, glu_b, g_out_ssm, g_out_conv)
    vec1 =_rows8(g_pre_mix, 1.0 + sc1, sh1)
    vd1 = _rows8(g_post_mix, gt1)
    vec2 = _rows8(g_pre_ffn, 1.0 + sc2, sh2)
    vd2 = _rows8(g_post_ffn, gt2)

    proj, bu_re, bu_im, h1b = _mix_in(xs, vec1, w_in_st, bblk_re, bblk_im)
    s_re, s_im = _scan_fwd(a_rows, bu_re, bu_im)
    g_glu, g_wout = weights(("glu_w", "w_out"), s_re)
    glu_full = g_glu.reshape(d_ssm, d_ssm)
    w_out_full = g_wout.reshape(2 * d_ssm, d)
    y1, o_mix, x2 = _mix_out(xs, proj, s_re, s_im, cblk_re, cblk_im, v512, convw_full, glu_full, h16, h64,
                             w_out_full, vd1)
    (w_up_st,) = weights(("w_up",), x2)
    up, h2b = _ffn_up(x2, vec2, w_up_st)
    (g_wdown,) = weights(("w_down",), up)
    w_down_full = g_wdown.reshape(d_ff, d)
    actb, ddnb, dout, dhid, vp_dn, loss_blk = _ffn_down(up, fw_full, w_down_full, w_down_full.T, x2, tgt, vd2)

    g_names = ("w_down", "w_up", "w_out", "glu_w", "w_in")
    gw_down = _matmul_tn(actb, ddnb, d_ff, d, BF16, "dw_down", bt=1024).reshape(N_CHIPS, d_ff // N_CHIPS, d)
    dx2, dupb, vp_up, df_rows = _ffn_up_bwd(dhid, up, fw_full, x2, dout, vec2, w_up_st)
    gw_up = _matmul_tn(h2b, dupb, d, n_upc, BF16, "dw_up", bt=2048)
    ga_send, ga_recv, ga_src, ga_land, ga_token = _chips_start(
        "grads_start_ffn", False, [gw_down, gw_up],
        [_landing(lax.dynamic_index_in_dim(g, chip, 0, False), chip) for g in (gw_down, gw_up)])
    (dob, ycatb, zb, dqb, dy1, g_re, g_im, dcc, dbg, vp_mo, vp5, d_cre, d_cim) = _mix_out_bwd(
        dx2, o_mix, y1, proj, s_re, s_im, cblk_re, cblk_im, v512, convw_full, glu_full, h16, h64, w_out_full,
        vd1 + ga_token[0:1, 0:1])
    gw_out = _matmul_tn(ycatb, dob, 2 * d_ssm, d, BF16, "dw_out", bt=2048)
    gw_out = gw_out.reshape(N_CHIPS, 2 * d_ssm // N_CHIPS, d)
    gw_glu = _matmul_tn(zb, dqb, d_ssm, d_ssm, BF16, "dw_glu", bt=2048).reshape(N_CHIPS, d_ssm // N_CHIPS, d_ssm)
    gb_send, gb_recv, gb_src, gb_land, gb_token = _chips_start(
        "grads_start_mix", False, [gw_out, gw_glu],
        [_landing(lax.dynamic_index_in_dim(g, chip, 0, False), chip) for g in (gw_out, gw_glu)])
    gt_re, gt_im, ga_re8, ga_im8 = _scan_bwd(a_rows + gb_token[0:1, 0:1], g_re, g_im, s_re, s_im)
    grad_x, dprojb, vp_mi, d_bre, d_bim = _mix_in_bwd(gt_re, gt_im, bblk_re, bblk_im, dy1, dcc, dbg, proj, xs, dx2,
                                                      vec1, v512, convw_full, w_in_st)
    ssm_u, ssm_s = d_ssm // SSM_SPLIT, gp // SSM_SPLIT

    fold_b = jnp.asarray(_fold_matrix(ssm_s, n_state), BF16)
    fold_c = jnp.asarray(_fold_matrix(ssm_u, n_gch), BF16)
    db_re_f, db_im_f, gc_rows = _ssm_bgrad(d_bre, d_bim, bt_re, bt_im, coef_rows, fold_b, tile_b)
    dc_re_f, dc_im_f = _ssm_cgrad(d_cre, d_cim, fold_c)
    ga_sum = _ga_rowsum(ga_re8, ga_im8)
    g_lam_re, g_lam_im, g_log_step = _ssm_lamgrad(
        lam_re, lam_im, log_step, abar_re, abar_im, coef_re, coef_im,
        gc_rows[0].reshape(n_groups, n_state), gc_rows[1].reshape(n_groups, n_state),
        ga_sum[0].reshape(n_groups, n_state), ga_sum[1].reshape(n_groups, n_state))
    g_b_re = db_re_f.reshape(n_groups, n_gch, n_state).transpose(0, 2, 1)
    g_b_im = db_im_f.reshape(n_groups, n_gch, n_state).transpose(0, 2, 1)
    g_c_re = dc_re_f.reshape(n_groups, n_state, n_gch).transpose(0, 2, 1)
    g_c_im = dc_im_f.reshape(n_groups, n_state, n_gch).transpose(0, 2, 1)

    dmod = jnp.concatenate([vp_mi[0:1], vp_mi[1:2], vp_mo[0:1], vp_up[0:1], vp_up[1:2], vp_dn[0:1]], axis=1)
    small = [
        ("g_pre_mix", vp_mi[2:3]), ("g_post_mix", vp_mo[1:2]), ("g_pre_ffn", vp_up[2:3]), ("g_post_ffn", vp_dn[1:2]),
        ("ssm_lam_re", g_lam_re), ("ssm_lam_im", g_lam_im), ("ssm_log_step", g_log_step),
        ("ssm_b_re", g_b_re), ("ssm_b_im", g_b_im), ("ssm_c_re", g_c_re), ("ssm_c_im", g_c_im),
        ("ssm_d", vp5[3:4]), ("glu_b", vp5[2:3]), ("g_out_ssm", vp5[0:1]), ("g_out_conv", vp5[1:2]),
        ("conv_w", vp5[4:7]), ("ffn_conv_w", df_rows[0:3]), ("loss", loss_blk[0:1, 0:1]),
    ]
    packed, offsets, row = [], {}, 0
    for name, a in small:
        r = _to_rows(a, d)
        offsets[name] = (row, a.shape)
        packed.append(r)
        row += r.shape[0]
    n_small = -(-row // SUBLANES) * SUBLANES
    packed.append(jnp.zeros((n_small - row, d), F32))
    packed.append(pad8(dmod.reshape(6, d)))
    pack = jnp.concatenate(packed, axis=0)
    sm_send, sm_recv, _, sm_land, sm_token = _chips_start("small_start", True, [], [_landing(pack, chip)])

    gw_in = _matmul_tn(h1b, dprojb, d, w_in.shape[2], BF16, "dw_in", bt=2048, after=sm_token)
    gc_send, gc_recv, gc_src, gc_land, gc_token = _chips_start(
        "grads_start_in", False, [gw_in], [_landing(lax.dynamic_index_in_dim(gw_in, chip, 0, False), chip)])

    def finish(names, landed):
        partial = [_sum_blocks(s, "sum_" + nm) for s, nm in zip(landed, names)]
        theirs = _swap_sibling(partial, "swap_" + names[0])
        done = {}
        for nm, pm, ps in zip(names, partial, theirs):
            w_, m_, v_ = big_params[nm]
            done[nm] = _adamw_big(pm, ps, w_[0], m_[0], v_[0], "adamw_" + nm)
        return done

    big_params = {"w_down": (w_down, m_w_down, v_w_down), "w_up": (w_up, m_w_up, v_w_up),
                  "w_out": (w_out, m_w_out, v_w_out), "glu_w": (glu_w, m_glu_w, v_glu_w),
                  "w_in": (w_in, m_w_in, v_w_in)}
    big = finish(("w_down", "w_up"), _chips_wait("grads_wait_ffn", False, ga_send, ga_recv, ga_src, ga_land, gc_token))

    (sm_landed,) = _chips_wait("small_wait", True, sm_send, sm_recv, [], sm_land, big["w_up"][0])
    sm_part = _sum_blocks(sm_landed, "sum_small")
    dmod_mine = sm_landed[:, n_small:n_small + SUBLANES, :]
    sm_sib, dmod_sib = _swap_sibling([sm_part, dmod_mine], "swap_small")
    sums = _add2(sm_part, sm_sib)
    dmod_by_core = jnp.stack([dmod_mine, dmod_sib], axis=1)
    dmod_by_core = jnp.where(ci == 0, dmod_by_core, dmod_by_core[:, ::-1])
    dmod_all = dmod_by_core[:, :, :6, :].reshape(N_DEV, 6 * d)
    g_b_ada = sums[n_small:n_small + 6].reshape(1, 6 * d)

    def unpack(name):
        r0, shape = offsets[name]
        size = math.prod(shape)
        nrow = -(-size // d)
        return sums[r0:r0 + nrow].reshape(-1)[:size].reshape(shape)

    dmod_cols = lax.dynamic_slice(dmod_all, (0, chip * n_ada), (N_DEV, n_ada))
    ada = _adamw_ada(c_all, dmod_cols, w_ada[0], m_w_ada[0], v_w_ada[0])

    big.update(finish(("w_out", "glu_w", "w_in"), _chips_wait(
        "grads_wait_mix", False, list(gb_send) + list(gc_send), list(gb_recv) + list(gc_recv),
        list(gb_src) + list(gc_src), list(gb_land) + list(gc_land), ada[0])))

    g_small = {name: unpack(name) for name, _ in small}
    g_small["b_ada"] = g_b_ada
    g_small["conv_w"] = lax.dynamic_slice(g_small["conv_w"], (0, chip * conv_w.shape[2]), (3, conv_w.shape[2]))
    g_small["ffn_conv_w"] = lax.dynamic_slice(g_small["ffn_conv_w"], (0, chip * n_upc), (3, n_upc))
    g_small["ssm_log_step"] = g_small["ssm_log_step"].reshape(1, n_groups)
    small_params = {
        "b_ada": (b_ada, m_b_ada, v_b_ada), "g_pre_mix": (g_pre_mix, m_g_pre_mix, v_g_pre_mix),
        "g_post_mix": (g_post_mix, m_g_post_mix, v_g_post_mix), "ssm_lam_re": (ssm_lam_re, m_ssm_lam_re, v_ssm_lam_re),
        "ssm_lam_im": (ssm_lam_im, m_ssm_lam_im, v_ssm_lam_im),
        "ssm_log_step": (ssm_log_step, m_ssm_log_step, v_ssm_log_step),
        "ssm_b_re": (ssm_b_re, m_ssm_b_re, v_ssm_b_re), "ssm_b_im": (ssm_b_im, m_ssm_b_im, v_ssm_b_im),
        "ssm_c_re": (ssm_c_re, m_ssm_c_re, v_ssm_c_re), "ssm_c_im": (ssm_c_im, m_ssm_c_im, v_ssm_c_im),
        "ssm_d": (ssm_d, m_ssm_d, v_ssm_d), "glu_b": (glu_b, m_glu_b, v_glu_b),
        "g_out_ssm": (g_out_ssm, m_g_out_ssm, v_g_out_ssm), "conv_w": (conv_w, m_conv_w, v_conv_w),
        "g_out_conv": (g_out_conv, m_g_out_conv, v_g_out_conv), "g_pre_ffn": (g_pre_ffn, m_g_pre_ffn, v_g_pre_ffn),
        "g_post_ffn": (g_post_ffn, m_g_post_ffn, v_g_post_ffn),
        "ffn_conv_w": (ffn_conv_w, m_ffn_conv_w, v_ffn_conv_w),
    }

    def natural(a):
        return a[0] if a.ndim > 2 else a

    names = list(small_params)
    items = []
    for nm in names:
        w_, m_, v_ = small_params[nm]
        items.append((natural(w_), g_small[nm].reshape(natural(w_).shape), natural(m_), natural(v_)))
    upd = _adamw_small(items)
    small_out = {}
    for nm, (dl, mo, vo) in zip(names, upd):
        shp = small_params[nm][0].shape
        small_out[nm] = (g_small[nm].reshape(shp), dl.reshape(shp), mo.reshape(shp), vo.reshape(shp))

    loss = g_small["loss"][0, 0]

    order = ["w_ada", "b_ada", "g_pre_mix", "g_post_mix", "w_in", "ssm_lam_re", "ssm_lam_im", "ssm_log_step",
             "ssm_b_re", "ssm_b_im", "ssm_c_re", "ssm_c_im", "ssm_d", "glu_w", "glu_b", "g_out_ssm", "conv_w",
             "g_out_conv", "w_out", "g_pre_ffn", "g_post_ffn", "w_up", "ffn_conv_w", "w_down"]
    results = {"w_ada": tuple(a[None] for a in ada)}
    for nm in big:
        results[nm] = tuple(a[None] for a in big[nm])
    results.update(small_out)
    outs = [loss, grad_x[None]]
    for k in range(4):
        outs += [results[nm][k] for nm in order]
    return tuple(outs)


def _ga_rowsum(ga_re8, ga_im8):
    n = ga_re8.shape[1]

    def body(r_ref, i_ref, o_ref):
        o_ref[...] = jnp.zeros(o_ref.shape, F32)
        o_ref[0:1, :] = _colsum(r_ref[...])
        o_ref[1:2, :] = _colsum(i_ref[...])

    return pl.pallas_call(body, name="ga_rowsum", out_shape=jax.ShapeDtypeStruct((SUBLANES, n), F32))(ga_re8, ga_im8)
```

```python
import functools
import math

import jax
import jax.numpy as jnp
import numpy as np
from jax import lax
from jax.experimental import pallas as pl
from jax.experimental.pallas import tpu as pltpu

F32 = jnp.float32
BF16 = jnp.bfloat16
MESH = pl.DeviceIdType.MESH

EPS = 1e-6
LAMBDA_RE_MAX = -1e-4
ADAM_LR = 0.001
ADAM_B1 = 0.9
ADAM_B2 = 0.999
ADAM_EPS = 1e-08
ADAM_WD = 0.01
ADAM_STEP = 10

SUBLANES = 8
BF16_ROWS = 16
N_CHIPS = 4
N_DEV = 8
CONV_HEAD_DIM = 64
VMEM_BIG = 56 * 1024 * 1024
VMEM_MID = 40 * 1024 * 1024

TB_MIX = 256
TB_FFN = 256
TB_FFN_UP = 512
TB_SCAN = 1024
W_SCAN = 256
SSM_SPLIT = 4
CW_FFN = 256
SCAN_UNROLL = 4
TB_TN = 512


def _cparams(sem=None, vmem=None):
    kw = {}
    if sem is not None:
        kw["dimension_semantics"] = sem
    if vmem is not None:
        kw["vmem_limit_bytes"] = vmem
    return pltpu.CompilerParams(**kw)


def _blk(t, pref):
    return pref if t % pref == 0 else t


def _dot(a, b):
    return jnp.dot(a.astype(BF16), b.astype(BF16), preferred_element_type=F32)


def _dot_nt(a, b):
    return lax.dot_general(a.astype(BF16), b.astype(BF16), (((1,), (1,)), ((), ())),
                           preferred_element_type=F32)


def _dot_tn(a, b):
    return lax.dot_general(a.astype(BF16), b.astype(BF16), (((0,), (0,)), ((), ())),
                           preferred_element_type=F32)


def _sigmoid(x):
    return 0.5 * jnp.tanh(0.5 * x) + 0.5


_GELU_K = math.sqrt(2.0 / math.pi)
_GELU_C = 0.044715


def _gelu(x):
    th = jnp.tanh(_GELU_K * (x + _GELU_C * x * x * x))
    return x * (0.5 * (1.0 + th))


def _gelu_and_grad(x):
    x2 = x * x
    th = jnp.tanh(_GELU_K * (x + _GELU_C * x2 * x))
    half = 0.5 * (1.0 + th)
    return x * half, half + 0.5 * x * (1.0 - th * th) * _GELU_K * (1.0 + 3.0 * _GELU_C * x2)


def _rowmean(x):
    return jnp.mean(x, axis=-1, keepdims=True)


def _colsum(x):
    return jnp.sum(x, axis=0, keepdims=True)


def _split_dot(x, m):
    hi = x.astype(BF16)
    lo = (x - hi.astype(F32)).astype(BF16)
    return (jnp.dot(hi, m, preferred_element_type=F32) + jnp.dot(lo, m, preferred_element_type=F32))


def _split3_dot(x, m):
    hi = x.astype(BF16)
    r1 = x - hi.astype(F32)
    mid = r1.astype(BF16)
    lo = (r1 - mid.astype(F32)).astype(BF16)
    return (jnp.dot(hi, m, preferred_element_type=F32) + jnp.dot(mid, m, preferred_element_type=F32)
            + jnp.dot(lo, m, preferred_element_type=F32))


def _shift_down(x, halo, k):
    r = pltpu.roll(x, k, 0)
    row = lax.broadcasted_iota(jnp.int32, x.shape, 0)
    for j in range(k):
        r = jnp.where(row == j, halo[SUBLANES - k + j:SUBLANES - k + j + 1, :], r)
    return r


def _shift_up(x, halo, k):
    n = x.shape[0]
    r = pltpu.roll(x, n - k, 0)
    row = lax.broadcasted_iota(jnp.int32, x.shape, 0)
    for j in range(k):
        r = jnp.where(row == n - k + j, halo[j:j + 1, :], r)
    return r


def _acc_rows(ref, first, rows):
    @pl.when(first)
    def _():
        ref[...] = jnp.zeros(ref.shape, ref.dtype)
    for j, r in enumerate(rows):
        ref[j:j + 1, :] += r


def _rows(tb, c, col=0):
    return pl.BlockSpec((tb, c), lambda i, col=col: (i, col))


def _full(shape):
    nd = len(shape)
    return pl.BlockSpec(shape, lambda i, nd=nd: (0,) * nd)


def _resident(shape):
    nd = len(shape)
    return pl.BlockSpec(shape, lambda i, nd=nd: (0,) * nd, pipeline_mode=pl.Buffered(1))


def _halo_prev(tb, c, col=0):
    per = tb // SUBLANES
    return pl.BlockSpec((SUBLANES, c), lambda i, col=col: (jnp.maximum(i * per - 1, 0), col))


def _halo_next(tb, c, t, col=0, rows=SUBLANES):
    per = tb // rows
    last = t // rows - 1
    return pl.BlockSpec((rows, c), lambda i, col=col: (jnp.minimum((i + 1) * per, last), col))


def _mesh_pos():
    return lax.axis_index("x"), lax.axis_index("y"), lax.axis_index("c")


def _allgather8(x_pad, n_sum, name):
    m_per, n = x_pad.shape

    def body(x_ref, out_ref, sum_ref, send_sems, recv_sems, local_sem):
        x, y, c = _mesh_pos()
        me, sibling = (x, y, c), (x, y, 1 - c)
        chips = [(1 - x, y), (x, 1 - y), (1 - x, 1 - y)]

        def rows(px, py, pc):
            return out_ref.at[pl.ds((4 * px + 2 * py + pc) * m_per, m_per), :]

        def copy(k, block, to, src=None):
            return pltpu.make_async_remote_copy(
                src_ref=rows(*block) if src is None else src, dst_ref=rows(*block),
                send_sem=send_sems.at[k], recv_sem=recv_sems.at[k], device_id=to, device_id_type=MESH)

        mine = pltpu.make_async_copy(x_ref, rows(*me), local_sem)
        mine.start()
        first = [copy(0, me, sibling, src=x_ref)]
        first += [copy(1 + j, me, (*chip, c), src=x_ref) for j, chip in enumerate(chips)]
        for cp in first:
            cp.start()
        passed = [copy(4 + j, (*chip, c), sibling) for j, chip in enumerate(chips)]
        for j, chip in enumerate(chips):
            copy(1 + j, (*chip, c), me).wait_recv()
            passed[j].start()
        copy(0, sibling, me).wait_recv()
        for j, chip in enumerate(chips):
            copy(4 + j, (*chip, 1 - c), me).wait_recv()
        for cp in first + passed:
            cp.wait_send()
        mine.wait()
        acc = out_ref[0:n_sum, :]
        for k in range(1, N_DEV):
            acc = acc + out_ref[k * m_per:k * m_per + n_sum, :]
        sum_ref[...] = acc

    return pl.pallas_call(
        body, name=name,
        out_shape=(jax.ShapeDtypeStruct((N_DEV * m_per, n), F32), jax.ShapeDtypeStruct((n_sum, n), F32)),
        in_specs=[pl.BlockSpec(memory_space=pltpu.VMEM)],
        out_specs=(pl.BlockSpec(memory_space=pltpu.VMEM), pl.BlockSpec(memory_space=pltpu.VMEM)),
        scratch_shapes=[pltpu.SemaphoreType.DMA((7,)), pltpu.SemaphoreType.DMA((7,)), pltpu.SemaphoreType.DMA],
        compiler_params=_cparams(vmem=VMEM_MID),
    )(x_pad)


_HBM = pl.BlockSpec(memory_space=pltpu.HBM)
_SEM = pl.BlockSpec(memory_space=pltpu.SEMAPHORE)
_EFFECT = pltpu.SideEffectType.DATAFLOW_SIDE_EFFECTING


def _chip_copy(gather, src_ref, land_ref, send, recv, j, arrival):
    x, y, c = _mesh_pos()
    peer = [(1 - x, y), (x, 1 - y), (1 - x, 1 - y)][j]
    peer_chip = 2 * peer[0] + peer[1]
    my_chip = 2 * x + y
    return pltpu.make_async_remote_copy(
        src_ref=land_ref.at[my_chip] if gather else src_ref.at[peer_chip],
        dst_ref=land_ref.at[peer_chip if arrival else my_chip],
        send_sem=send.at[j], recv_sem=recv.at[j], device_id=(*peer, c), device_id_type=MESH)


def _chips_start(name, gather, srcs, lands, after=None):
    n, ns = len(lands), len(srcs)
    extra = [] if after is None else [after]

    def body(*refs):
        src_refs, land_refs = refs[:ns], refs[ns:ns + n]
        outs = refs[ns + n + len(extra):]
        sends, recvs, token = outs[:n], outs[n:2 * n], outs[-1]
        for k in range(n):
            for j in range(3):
                _chip_copy(gather, src_refs[k] if ns else None, land_refs[k], sends[k], recvs[k], j, False).start()
        token[...] = jnp.zeros(token.shape, F32)

    sem = pltpu.SemaphoreType.DMA((3,))
    thru = tuple(pltpu.HBM(a.shape, a.dtype) for a in list(srcs) + list(lands))
    res = pl.pallas_call(
        body, name=name,
        out_shape=(sem,) * (2 * n) + thru + (jax.ShapeDtypeStruct((SUBLANES, 128), F32),),
        in_specs=[_HBM] * (ns + n) + [pl.BlockSpec(memory_space=pl.ANY)] * len(extra),
        out_specs=(_SEM,) * (2 * n) + (_HBM,) * (ns + n) + (pl.BlockSpec(memory_space=pltpu.VMEM),),
        input_output_aliases={k: 2 * n + k for k in range(ns + n)},
        compiler_params=pltpu.CompilerParams(has_side_effects=_EFFECT),
    )(*[pltpu.with_memory_space_constraint(a, pltpu.HBM) for a in list(srcs) + list(lands)], *extra)
    return res[:n], res[n:2 * n], res[2 * n:2 * n + ns], res[2 * n + ns:2 * n + ns + n], res[-1]


def _chips_wait(name, gather, sends, recvs, srcs, lands, after):
    n, ns = len(lands), len(srcs)

    def body(*refs):
        src_refs, land_refs = refs[:ns], refs[ns:ns + n]
        sends_, recvs_ = refs[ns + n:ns + 2 * n], refs[ns + 2 * n:ns + 3 * n]
        for k in range(n):
            for j in range(3):
                cp = _chip_copy(gather, src_refs[k] if ns else None, land_refs[k], sends_[k], recvs_[k], j, True)
                cp.wait_send()
                cp.wait_recv()

    thru = tuple(pltpu.HBM(a.shape, a.dtype) for a in list(srcs) + list(lands))
    res = pl.pallas_call(
        body, name=name, out_shape=thru,
        in_specs=[_HBM] * (ns + n) + [_SEM] * (2 * n) + [pl.BlockSpec(memory_space=pl.ANY)],
        out_specs=(_HBM,) * (ns + n),
        input_output_aliases={k: k for k in range(ns + n)},
        compiler_params=pltpu.CompilerParams(has_side_effects=_EFFECT),
    )(*srcs, *lands, *sends, *recvs, after)
    return res[ns:]


def _sibling_copy(src_ref, land_ref, send, recv):
    x, y, c = _mesh_pos()
    return pltpu.make_async_remote_copy(src_ref=src_ref, dst_ref=land_ref, send_sem=send.at[0], recv_sem=recv.at[0],
                                        device_id=(x, y, 1 - c), device_id_type=MESH)


def _sibling_start(name, arrs, after=None):
    n = len(arrs)
    extra = [] if after is None else [after]
    lands = [lax.empty(a.shape, a.dtype) for a in arrs]

    def body(*refs):
        src_refs, land_refs = refs[:n], refs[n:2 * n]
        outs = refs[2 * n + len(extra):]
        sends, recvs, token = outs[:n], outs[n:2 * n], outs[-1]
        for k in range(n):
            _sibling_copy(src_refs[k], land_refs[k], sends[k], recvs[k]).start()
        token[...] = jnp.zeros(token.shape, F32)

    sem = pltpu.SemaphoreType.DMA((1,))
    thru = tuple(pltpu.HBM(a.shape, a.dtype) for a in list(arrs) + lands)
    res = pl.pallas_call(
        body, name=name,
        out_shape=(sem,) * (2 * n) + thru + (jax.ShapeDtypeStruct((SUBLANES, 128), F32),),
        in_specs=[_HBM] * (2 * n) + [pl.BlockSpec(memory_space=pl.ANY)] * len(extra),
        out_specs=(_SEM,) * (2 * n) + (_HBM,) * (2 * n) + (pl.BlockSpec(memory_space=pltpu.VMEM),),
        input_output_aliases={k: 2 * n + k for k in range(2 * n)},
        compiler_params=pltpu.CompilerParams(has_side_effects=_EFFECT),
    )(*[pltpu.with_memory_space_constraint(a, pltpu.HBM) for a in list(arrs) + lands], *extra)
    return res[:n], res[n:2 * n], res[2 * n:3 * n], res[3 * n:4 * n], res[-1]


def _sibling_wait(name, sends, recvs, srcs, lands, after):
    n = len(srcs)

    def body(*refs):
        src_refs, land_refs = refs[:n], refs[n:2 * n]
        sends_, recvs_ = refs[2 * n:3 * n], refs[3 * n:4 * n]
        for k in range(n):
            cp = _sibling_copy(src_refs[k], land_refs[k], sends_[k], recvs_[k])
            cp.wait_send()
            cp.wait_recv()

    thru = tuple(pltpu.HBM(a.shape, a.dtype) for a in list(srcs) + list(lands))
    res = pl.pallas_call(
        body, name=name, out_shape=thru,
        in_specs=[_HBM] * (2 * n) + [_SEM] * (2 * n) + [pl.BlockSpec(memory_space=pl.ANY)],
        out_specs=(_HBM,) * (2 * n),
        input_output_aliases={k: k for k in range(2 * n)},
        compiler_params=pltpu.CompilerParams(has_side_effects=_EFFECT),
    )(*srcs, *lands, *sends, *recvs, after)
    return res[:n], res[n:]


def _landing(own, chip):
    zone = lax.empty((N_CHIPS,) + own.shape, own.dtype)
    return lax.dynamic_update_slice(zone, own[None], (chip,) + (0,) * own.ndim)


def _swap_sibling(arrs, name):
    n_arr = len(arrs)

    def body(*refs):
        ins, outs = refs[:n_arr], refs[n_arr:2 * n_arr]
        send_sems, recv_sems = refs[2 * n_arr:]
        x, y, c = _mesh_pos()
        copies = [pltpu.make_async_remote_copy(
            src_ref=ins[n], dst_ref=outs[n], send_sem=send_sems.at[n], recv_sem=recv_sems.at[n],
            device_id=(x, y, 1 - c), device_id_type=MESH) for n in range(n_arr)]
        for cp in copies:
            cp.start()
        for cp in copies:
            cp.wait()

    any_spec = pl.BlockSpec(memory_space=pl.ANY)
    return pl.pallas_call(
        body, name=name,
        out_shape=tuple(jax.ShapeDtypeStruct(a.shape, a.dtype) for a in arrs),
        in_specs=[any_spec] * n_arr, out_specs=tuple([any_spec] * n_arr),
        scratch_shapes=[pltpu.SemaphoreType.DMA((n_arr,)), pltpu.SemaphoreType.DMA((n_arr,))],
    )(*arrs)


def _mod_shard(c_all, w_ada_sh, b_sh):
    d, n = w_ada_sh.shape
    bn = 512

    def body(c_ref, w_ref, b_ref, o_ref):
        cc = c_ref[...]
        ca = cc * _sigmoid(cc)
        o_ref[...] = _dot(ca, w_ref[...]) + b_ref[...]

    return pl.pallas_call(
        body, name="mod_shard", grid=(n // bn,),
        out_shape=jax.ShapeDtypeStruct((N_DEV, n), F32),
        in_specs=[_full((N_DEV, d)), pl.BlockSpec((d, bn), lambda j: (0, j)), pl.BlockSpec((1, bn), lambda j: (0, j))],
        out_specs=pl.BlockSpec((N_DEV, bn), lambda j: (0, j)),
        compiler_params=_cparams(("parallel",)),
    )(c_all, w_ada_sh, b_sh)


def _ssm_prep(lam_re, lam_im, log_step):
    g, p = lam_re.shape

    def body(lr_ref, li_ref, ls_ref, ar_ref, ai_ref, cr_ref, ci_ref):
        lr = jnp.minimum(lr_ref[...], LAMBDA_RE_MAX)
        li = li_ref[...]
        st = jnp.exp(ls_ref[...])
        mag = jnp.exp(lr * st)
        ar = mag * jnp.cos(li * st)
        ai = mag * jnp.sin(li * st)
        den = lr * lr + li * li
        nr = ar - 1.0
        ar_ref[...] = ar
        ai_ref[...] = ai
        cr_ref[...] = (nr * lr + ai * li) / den
        ci_ref[...] = (ai * lr - nr * li) / den

    sds = jax.ShapeDtypeStruct((g, p), F32)
    return pl.pallas_call(body, name="ssm_prep", out_shape=(sds,) * 4)(lam_re, lam_im, log_step)


def _ssm_blocks(bt_re, bt_im, ct_re, ct_im, coef_rows, tile_b, tile_c):
    gh, p = bt_re.shape
    gp, h = ct_re.shape
    nb = SSM_SPLIT
    cb, rb = gp // nb, gp // nb

    def body(btr, bti, ctr, cti, cf, tb_ref, tc_ref, bre_o, bim_o, cre_o, cim_o):
        j = pl.program_id(0)
        row = lax.broadcasted_iota(jnp.int32, (gh, cb), 0)
        col = lax.broadcasted_iota(jnp.int32, (gh, cb), 1) + j * cb
        mask = (row >> 4) == (col >> 6)
        cr, ci = cf[0:1, :], cf[1:2, :]
        br = _split3_dot(btr[...], tb_ref[...])
        bi = _split3_dot(bti[...], tb_ref[...])
        bre_o[...] = jnp.where(mask, br * cr - bi * ci, 0.0).astype(BF16)
        bim_o[...] = jnp.where(mask, br * ci + bi * cr, 0.0).astype(BF16)
        row2 = lax.broadcasted_iota(jnp.int32, (rb, gh), 0) + j * rb
        col2 = lax.broadcasted_iota(jnp.int32, (rb, gh), 1)
        mask2 = (row2 >> 6) == (col2 >> 4)
        cre_o[...] = jnp.where(mask2, _split3_dot(ctr[...], tc_ref[...]), 0.0).astype(BF16)
        cim_o[...] = jnp.where(mask2, _split3_dot(cti[...], tc_ref[...]), 0.0).astype(BF16)

    bspec = pl.BlockSpec((gh, cb), lambda j: (0, j))
    cspec = pl.BlockSpec((rb, gh), lambda j: (j, 0))
    cin = pl.BlockSpec((rb, h), lambda j: (j, 0))
    return pl.pallas_call(
        body, name="ssm_blocks", grid=(nb,),
        out_shape=(jax.ShapeDtypeStruct((gh, gp), BF16),) * 2 + (jax.ShapeDtypeStruct((gp, gh), BF16),) * 2,
        in_specs=[_full((gh, p)), _full((gh, p)), cin, cin, pl.BlockSpec((SUBLANES, cb), lambda j: (0, j)),
                  _full(tile_b.shape), _full(tile_c.shape)],
        out_specs=(bspec, bspec, cspec, cspec),
        compiler_params=_cparams(("parallel",)),
    )(bt_re, bt_im, ct_re, ct_im, coef_rows, tile_b, tile_c)


def _scan_consts(a_ref, reverse):
    w = a_ref.shape[1]
    ar1 = a_ref[0:1, :]
    ai1 = a_ref[1:2, :]
    if reverse:
        ai1 = -ai1
    pr, pi = [ar1], [ai1]
    for _ in range(1, SUBLANES):
        nr = pr[-1] * ar1 - pi[-1] * ai1
        ni = pr[-1] * ai1 + pi[-1] * ar1
        pr.append(nr)
        pi.append(ni)
    row = lax.broadcasted_iota(jnp.int32, (SUBLANES, w), 0)
    dist = (SUBLANES - 1 - row) if reverse else row

    def pick(vals):
        out = jnp.broadcast_to(vals[SUBLANES - 1], (SUBLANES, w))
        for r in range(SUBLANES - 1):
            out = jnp.where(dist == r, vals[r], out)
        return out

    p_r, p_i = pick(pr), pick(pi)
    steps = []
    for k in (1, 2, 4):
        steps.append((k, jnp.where(dist >= k, pr[k - 1], 0.0), jnp.where(dist >= k, pi[k - 1], 0.0)))
    a8 = (jnp.broadcast_to(pr[SUBLANES - 1], (SUBLANES, w)), jnp.broadcast_to(pi[SUBLANES - 1], (SUBLANES, w)))
    return row, p_r, p_i, steps, a8


def _scan_tile(xr, xi, cr, ci, consts, reverse):
    row, p_r, p_i, steps, (a8r, a8i) = consts
    for k, s_r, s_i in steps:
        sh = (SUBLANES - k) if reverse else k
        qr = pltpu.roll(xr, sh, 0)
        qi = pltpu.roll(xi, sh, 0)
        xr, xi = xr + s_r * qr - s_i * qi, xi + s_r * qi + s_i * qr
    outr = xr + p_r * cr - p_i * ci
    outi = xi + p_r * ci + p_i * cr
    e = 0 if reverse else SUBLANES - 1
    er = jnp.broadcast_to(xr[e:e + 1, :], xr.shape)
    ei = jnp.broadcast_to(xi[e:e + 1, :], xi.shape)
    return outr, outi, er + a8r * cr - a8i * ci, ei + a8r * ci + a8i * cr


def _scan_fwd(a_rows, bu_re, bu_im):
    t, n = bu_re.shape
    tb, w = _blk(t, TB_SCAN), W_SCAN
    ntile = tb // SUBLANES

    def body(a_ref, br_ref, bi_ref, sr_ref, si_ref, car, cai):
        @pl.when(pl.program_id(1) == 0)
        def _():
            car[...] = jnp.zeros(car.shape, F32)
            cai[...] = jnp.zeros(cai.shape, F32)
        consts = _scan_consts(a_ref, False)

        def pair(i, carry):
            o = pl.multiple_of(i * BF16_ROWS, BF16_ROWS)
            outs = []
            for h in range(2):
                rows = pl.ds(o + h * SUBLANES, SUBLANES)
                outr, outi, ncr, nci = _scan_tile(br_ref[rows, :], bi_ref[rows, :], carry[0], carry[1], consts, False)
                outs.append((outr, outi))
                carry = (ncr, nci)
            sr_ref[pl.ds(o, BF16_ROWS), :] = jnp.concatenate([outs[0][0], outs[1][0]], axis=0).astype(BF16)
            si_ref[pl.ds(o, BF16_ROWS), :] = jnp.concatenate([outs[0][1], outs[1][1]], axis=0).astype(BF16)
            return carry

        def pairs(i, carry):
            for s in range(SCAN_UNROLL // 2):
                carry = pair(i * (SCAN_UNROLL // 2) + s, carry)
            return carry

        cr, ci = lax.fori_loop(0, ntile // SCAN_UNROLL, pairs, (car[...], cai[...]))
        car[...] = cr
        cai[...] = ci

    spec = pl.BlockSpec((tb, w), lambda s, k: (k, s))
    sds = jax.ShapeDtypeStruct((t, n), BF16)
    return pl.pallas_call(
        body, name="scan_fwd", grid=(n // w, t // tb), out_shape=(sds, sds),
        in_specs=[pl.BlockSpec((SUBLANES, w), lambda s, k: (0, s)), spec, spec], out_specs=(spec, spec),
        scratch_shapes=[pltpu.VMEM((SUBLANES, w), F32), pltpu.VMEM((SUBLANES, w), F32)],
        compiler_params=_cparams(("parallel", "arbitrary"), VMEM_MID),
    )(a_rows, bu_re, bu_im)


def _scan_bwd(a_rows, g_re, g_im, s_re, s_im):
    t, n = g_re.shape
    tb, w = _blk(t, TB_SCAN), W_SCAN
    ntile = tb // SUBLANES
    npair = tb // BF16_ROWS
    nt = t // tb

    def body(a_ref, gr_ref, gi_ref, sr_ref, si_ref, or_ref, oi_ref, gar_ref, gai_ref, car, cai):
        @pl.when(pl.program_id(1) == 0)
        def _():
            car[...] = jnp.zeros(car.shape, F32)
            cai[...] = jnp.zeros(cai.shape, F32)
            gar_ref[...] = jnp.zeros(gar_ref.shape, F32)
            gai_ref[...] = jnp.zeros(gai_ref.shape, F32)
        consts = _scan_consts(a_ref, True)
        row = consts[0]

        def pair(i, carry):
            cr, ci, accr, acci = carry
            o = pl.multiple_of((npair - 1 - i) * BF16_ROWS, BF16_ROWS)
            s_r = sr_ref[pl.ds(o, BF16_ROWS), :].astype(F32)
            s_i = si_ref[pl.ds(o, BF16_ROWS), :].astype(F32)
            outs = [None, None]
            for h in (1, 0):
                rows = pl.ds(o + h * SUBLANES, SUBLANES)
                outr, outi, ncr, nci = _scan_tile(gr_ref[rows, :], gi_ref[rows, :], cr, ci, consts, True)
                outs[h] = (outr, outi)
                gnr = jnp.where(row == SUBLANES - 1, cr, pltpu.roll(outr, SUBLANES - 1, 0))
                gni = jnp.where(row == SUBLANES - 1, ci, pltpu.roll(outi, SUBLANES - 1, 0))
                sr = s_r[h * SUBLANES:(h + 1) * SUBLANES, :]
                si = s_i[h * SUBLANES:(h + 1) * SUBLANES, :]
                accr, acci = accr + sr * gnr + si * gni, acci + sr * gni - si * gnr
                cr, ci = ncr, nci
            or_ref[pl.ds(o, BF16_ROWS), :] = jnp.concatenate([outs[0][0], outs[1][0]], axis=0).astype(BF16)
            oi_ref[pl.ds(o, BF16_ROWS), :] = jnp.concatenate([outs[0][1], outs[1][1]], axis=0).astype(BF16)
            return cr, ci, accr, acci

        def pairs(i, carry):
            for s in range(SCAN_UNROLL // 2):
                carry = pair(i * (SCAN_UNROLL // 2) + s, carry)
            return carry

        cr, ci, accr, acci = lax.fori_loop(0, ntile // SCAN_UNROLL, pairs,
                                           (car[...], cai[...], gar_ref[...], gai_ref[...]))
        car[...] = cr
        cai[...] = ci
        gar_ref[...] = accr
        gai_ref[...] = acci

    spec = pl.BlockSpec((tb, w), lambda s, k: (nt - 1 - k, s))
    aspec = pl.BlockSpec((SUBLANES, w), lambda s, k: (0, s))
    sds = jax.ShapeDtypeStruct((t, n), BF16)
    asds = jax.ShapeDtypeStruct((SUBLANES, n), F32)
    return pl.pallas_call(
        body, name="scan_bwd", grid=(n // w, nt), out_shape=(sds, sds, asds, asds),
        in_specs=[aspec, spec, spec, spec, spec], out_specs=(spec, spec, aspec, aspec),
        scratch_shapes=[pltpu.VMEM((SUBLANES, w), F32), pltpu.VMEM((SUBLANES, w), F32)],
        compiler_params=_cparams(("parallel", "arbitrary"), VMEM_MID),
    )(a_rows, g_re, g_im, s_re, s_im)


def _mix_in(x, vec, w_in_st, b_re, b_im):
    t, d = x.shape
    ns, _, nc = w_in_st.shape
    dssm, nstate = b_re.shape
    du, ds = dssm // SSM_SPLIT, nstate // SSM_SPLIT
    tb = _blk(t, TB_MIX)

    def body(x_ref, vec_ref, w_ref, bre_ref, bim_ref, proj_ref, bur_ref, bui_ref, h1_ref):
        xv = x_ref[...]
        r = lax.rsqrt(_rowmean(xv * xv) + EPS)
        h = xv * r * vec_ref[0:1, :] * vec_ref[1:2, :] + vec_ref[2:3, :]
        hb = h.astype(BF16)
        h1_ref[...] = hb
        u = None
        for j in range(ns):
            pj = jnp.dot(hb, w_ref[j], preferred_element_type=F32)
            proj_ref[:, j * nc:(j + 1) * nc] = pj
            if j == 0:
                u = pj
        ub = u.astype(BF16)
        for q in range(SSM_SPLIT):
            rq, cq = slice(q * du, (q + 1) * du), slice(q * ds, (q + 1) * ds)
            bur_ref[:, cq] = jnp.dot(ub[:, rq], bre_ref[rq, cq], preferred_element_type=F32)
            bui_ref[:, cq] = jnp.dot(ub[:, rq], bim_ref[rq, cq], preferred_element_type=F32)

    return pl.pallas_call(
        body, name="mix_in", grid=(t // tb,),
        out_shape=(jax.ShapeDtypeStruct((t, ns * nc), F32), jax.ShapeDtypeStruct((t, nstate), F32),
                   jax.ShapeDtypeStruct((t, nstate), F32), jax.ShapeDtypeStruct((t, d), BF16)),
        in_specs=[_rows(tb, d), _full((SUBLANES, d)), _resident(w_in_st.shape), _resident(b_re.shape),
                  _resident(b_im.shape)],
        out_specs=(_rows(tb, ns * nc), _rows(tb, nstate), _rows(tb, nstate), _rows(tb, d)),
        compiler_params=_cparams(("parallel",), VMEM_BIG),
    )(x, vec, w_in_st, b_re, b_im)


def _head_ms(y, h_ref):
    return _split_dot(y * y, h_ref[...])


def _conv3(x, halo, w_ref):
    return w_ref[0:1, :] * _shift_down(x, halo, 2) + w_ref[1:2, :] * _shift_down(x, halo, 1) + w_ref[2:3, :] * x


def _mix_out(x, proj, s_re, s_im, c_re, c_im, v512, convw, glu_w, h16, h64, w_out, vd):
    t, d = x.shape
    dh = c_re.shape[1]
    nstate = s_re.shape[1]
    du, ds = dh // SSM_SPLIT, nstate // SSM_SPLIT
    tb = _blk(t, TB_MIX)

    def body(x_ref, u_ref, bg_ref, cg_ref, v_ref, cgh_ref, vh_ref, sr_ref, si_ref, cre_ref, cim_ref, p_ref,
             cw_ref, gw_ref, h16_ref, h64_ref, wo_ref, vd_ref, y1_ref, o_ref, x2_ref):
        i = pl.program_id(0)
        u = u_ref[...]
        ys = []
        for q in range(SSM_SPLIT):
            rq, cq = slice(q * ds, (q + 1) * ds), slice(q * du, (q + 1) * du)
            ys.append(_dot(sr_ref[:, rq], cre_ref[rq, cq]) - _dot(si_ref[:, rq], cim_ref[rq, cq]))
        ys = jnp.concatenate(ys, axis=1)
        y1 = ys + p_ref[0:1, :] * u
        y1_ref[...] = y1
        z = _gelu(y1)
        q = _dot(z, gw_ref[...]) + p_ref[1:2, :]
        ya = z * _sigmoid(q)
        na = ya * lax.rsqrt(_head_ms(ya, h16_ref) + EPS) * p_ref[2:3, :]
        cv = cg_ref[...] * v_ref[...]
        cvh = jnp.where(i > 0, cgh_ref[...] * vh_ref[...], 0.0)
        yb = bg_ref[...] * _conv3(cv, cvh, cw_ref)
        nb = yb * lax.rsqrt(_head_ms(yb, h64_ref) + EPS) * p_ref[3:4, :]
        o = _dot(na, wo_ref[0:dh, :]) + _dot(nb, wo_ref[dh:2 * dh, :])
        o_ref[...] = o
        on = o * lax.rsqrt(_rowmean(o * o) + EPS) * vd_ref[0:1, :]
        x2_ref[...] = x_ref[...] + vd_ref[1:2, :] * on

    return pl.pallas_call(
        body, name="mix_out", grid=(t // tb,),
        out_shape=(jax.ShapeDtypeStruct((t, dh), F32), jax.ShapeDtypeStruct((t, d), F32),
                   jax.ShapeDtypeStruct((t, d), F32)),
        in_specs=[_rows(tb, d), _rows(tb, dh, 0), _rows(tb, dh, 1), _rows(tb, dh, 2), _rows(tb, dh, 3),
                  _halo_prev(tb, dh, 2), _halo_prev(tb, dh, 3), _rows(tb, nstate), _rows(tb, nstate),
                  _full(c_re.shape), _full(c_im.shape), _full(v512.shape), _full(convw.shape), _full(glu_w.shape),
                  _full(h16.shape), _full(h64.shape), _full(w_out.shape), _full(vd.shape)],
        out_specs=(_rows(tb, dh), _rows(tb, d), _rows(tb, d)),
        compiler_params=_cparams(("parallel",), VMEM_BIG),
    )(x, proj, proj, proj, proj, proj, proj, s_re, s_im, c_re, c_im, v512, convw, glu_w, h16, h64, w_out, vd)


def _ffn_up(x2, vec, w_up_st):
    t, d = x2.shape
    ns, _, nc = w_up_st.shape
    tb = _blk(t, TB_FFN_UP)

    def body(x_ref, vec_ref, w_ref, up_ref, h2_ref):
        xv = x_ref[...]
        r = lax.rsqrt(_rowmean(xv * xv) + EPS)
        h = xv * r * vec_ref[0:1, :] * vec_ref[1:2, :] + vec_ref[2:3, :]
        hb = h.astype(BF16)
        h2_ref[...] = hb
        for j in range(ns):
            up_ref[:, j * nc:(j + 1) * nc] = jnp.dot(hb, w_ref[j], preferred_element_type=F32)

    return pl.pallas_call(
        body, name="ffn_up", grid=(t // tb,),
        out_shape=(jax.ShapeDtypeStruct((t, ns * nc), F32), jax.ShapeDtypeStruct((t, d), BF16)),
        in_specs=[_rows(tb, d), _full((SUBLANES, d)), _resident(w_up_st.shape)],
        out_specs=(_rows(tb, ns * nc), _rows(tb, d)),
        compiler_params=_cparams(("parallel",), VMEM_BIG),
    )(x2, vec, w_up_st)


def _ffn_down(up, fw, w_down, w_down_t, x2, tgt, vd):
    t, nh = up.shape
    dff, d = w_down.shape
    tb = _blk(t, TB_FFN)
    inv_d = 1.0 / d

    def body(up_ref, uph_ref, fw_ref, wd_ref, wdt_ref, x2_ref, tgt_ref, vd_ref,
             act_ref, ddn_ref, dout_ref, dhid_ref, vec_ref, loss_ref, a_s, vv_s, sg_s):
        i = pl.program_id(0)

        def conv_cols(sl):
            x = up_ref[:, sl]
            halo = jnp.where(i > 0, uph_ref[:, sl], 0.0)
            return (fw_ref[0:1, sl] * _shift_down(x, halo, 2) + fw_ref[1:2, sl] * _shift_down(x, halo, 1)
                    + fw_ref[2:3, sl] * x)

        dn = None
        for o in range(0, dff, CW_FFN):
            sl = slice(o, o + CW_FFN)
            a = conv_cols(sl)
            vv = conv_cols(slice(dff + o, dff + o + CW_FFN))
            sg = _sigmoid(a)
            si = a * sg
            a_s[:, sl] = si
            vv_s[:, sl] = vv
            sg_s[:, sl] = sg
            actb = (si * vv).astype(BF16)
            act_ref[:, sl] = actb
            pj = lax.dot_general(actb, wdt_ref[:, sl], (((1,), (1,)), ((), ())), preferred_element_type=F32)
            dn = pj if dn is None else dn + pj
        r3 = lax.rsqrt(_rowmean(dn * dn) + EPS)
        xn = dn * r3
        g = vd_ref[0:1, :]
        gt2 = vd_ref[1:2, :]
        dnn = xn * g
        diff = x2_ref[...] + gt2 * dnn - tgt_ref[...]
        part = 0.5 * inv_d * jnp.sum(diff * diff)

        @pl.when(i == 0)
        def _():
            loss_ref[...] = jnp.zeros(loss_ref.shape, F32)
        loss_ref[...] += part
        dout = diff * inv_d
        dout_ref[...] = dout
        ddnn = dout * gt2
        _acc_rows(vec_ref, i == 0, [_colsum(dout * dnn), _colsum(ddnn * xn)])
        dxn = ddnn * g
        ddn = r3 * (dxn - xn * _rowmean(dxn * xn))
        ddnb = ddn.astype(BF16)
        ddn_ref[...] = ddnb
        for o in range(0, dff, CW_FFN):
            sl = slice(o, o + CW_FFN)
            dact = lax.dot_general(ddnb, wd_ref[sl, :], (((1,), (1,)), ((), ())), preferred_element_type=F32)
            si, vv, sg = a_s[:, sl], vv_s[:, sl], sg_s[:, sl]
            dhid_ref[:, sl] = (dact * vv * (sg + si * (1.0 - sg))).astype(BF16)
            dhid_ref[:, dff + o:dff + o + CW_FFN] = (dact * si).astype(BF16)

    return pl.pallas_call(
        body, name="ffn_down", grid=(t // tb,),
        scratch_shapes=[pltpu.VMEM((tb, dff), F32)] * 3,
        out_shape=(jax.ShapeDtypeStruct((t, dff), BF16), jax.ShapeDtypeStruct((t, d), BF16),
                   jax.ShapeDtypeStruct((t, d), F32), jax.ShapeDtypeStruct((t, nh), BF16),
                   jax.ShapeDtypeStruct((SUBLANES, d), F32), jax.ShapeDtypeStruct((SUBLANES, 128), F32)),
        in_specs=[_rows(tb, nh), _halo_prev(tb, nh), _full(fw.shape), _resident(w_down.shape),
                  _resident(w_down_t.shape), _rows(tb, d),
                  _rows(tb, d), _full(vd.shape)],
        out_specs=(_rows(tb, dff), _rows(tb, d), _rows(tb, d), _rows(tb, nh), _full((SUBLANES, d)),
                   _full((SUBLANES, 128))),
        compiler_params=_cparams(("arbitrary",), VMEM_BIG),
    )(up, up, fw, w_down, w_down_t, x2, tgt, vd)


def _ffn_up_bwd(dhid, up, fw, x2, dout, vec, w_up_st):
    t, nh = dhid.shape
    d = x2.shape[1]
    ns, _, nc = w_up_st.shape
    tb = _blk(t, TB_FFN)
    nblk = t // tb
    cw = 128

    def body(dh_ref, dhn_ref, up_ref, fw_ref, x2_ref, dout_ref, vec_ref, w_ref,
             dx2_ref, dup_ref, vp_ref, df_ref):
        i = pl.program_id(0)

        @pl.when(i == 0)
        def _():
            df_ref[...] = jnp.zeros(df_ref.shape, F32)
        dh2 = None
        for j in range(ns):
            for o in range(j * nc, (j + 1) * nc, cw):
                sl = slice(o, o + cw)
                dh = dh_ref[:, sl].astype(F32)
                dhn = jnp.where(i < nblk - 1, dhn_ref[:, sl].astype(F32), 0.0)
                dh1 = _shift_up(dh, dhn, 1)
                dh2s = _shift_up(dh, dhn, 2)
                dup_ref[:, sl] = (fw_ref[2:3, sl] * dh + fw_ref[1:2, sl] * dh1 + fw_ref[0:1, sl] * dh2s).astype(BF16)
                up_v = up_ref[:, sl]
                df_ref[0:1, sl] += _colsum(dh2s * up_v)
                df_ref[1:2, sl] += _colsum(dh1 * up_v)
                df_ref[2:3, sl] += _colsum(dh * up_v)
            pj = lax.dot_general(dup_ref[:, j * nc:(j + 1) * nc], w_ref[j], (((1,), (1,)), ((), ())),
                                 preferred_element_type=F32)
            dh2 = pj if dh2 is None else dh2 + pj
        xv = x2_ref[...]
        r = lax.rsqrt(_rowmean(xv * xv) + EPS)
        xn = xv * r
        g = vec_ref[0:1, :]
        hg = xn * g
        dhg = dh2 * vec_ref[1:2, :]
        _acc_rows(vp_ref, i == 0, [_colsum(dh2), _colsum(dh2 * hg), _colsum(dhg * xn)])
        dxn = dhg * g
        dx2_ref[...] = dout_ref[...] + r * (dxn - xn * _rowmean(dxn * xn))

    return pl.pallas_call(
        body, name="ffn_up_bwd", grid=(nblk,),
        out_shape=(jax.ShapeDtypeStruct((t, d), F32), jax.ShapeDtypeStruct((t, nh), BF16),
                   jax.ShapeDtypeStruct((SUBLANES, d), F32), jax.ShapeDtypeStruct((SUBLANES, nh), F32)),
        in_specs=[_rows(tb, nh), _halo_next(tb, nh, t, rows=BF16_ROWS), _rows(tb, nh), _full(fw.shape),
                  _rows(tb, d), _rows(tb, d), _full(vec.shape), _resident(w_up_st.shape)],
        out_specs=(_rows(tb, d), _rows(tb, nh), _full((SUBLANES, d)), _full((SUBLANES, nh))),
        compiler_params=_cparams(("arbitrary",), VMEM_BIG),
    )(dhid, dhid, up, fw, x2, dout, vec, w_up_st)


def _mix_out_bwd(dx2, o, y1, proj, s_re, s_im, c_re, c_im, v512, convw, glu_w, h16, h64, w_out, vd):
    t, d = dx2.shape
    dh = y1.shape[1]
    nstate = c_re.shape[0]
    du, ds = dh // SSM_SPLIT, nstate // SSM_SPLIT
    tb = _blk(t, TB_MIX)

    def body(dx2_ref, o_ref, y1_ref, u_ref, bg_ref, cg_ref, v_ref, cgh_ref, vh_ref, cre_ref, cim_ref, p_ref,
             cw_ref, gw_ref, h16_ref, h64_ref, wo_ref, vd_ref, sr_ref, si_ref,
             do_ref, ycat_ref, z_ref, dq_ref, dy1_ref, gr_ref, gi_ref, dcc_ref, dbg_ref, vpd_ref, vp5_ref,
             dcr_ref, dci_ref):
        i = pl.program_id(0)
        first = i == 0

        @pl.when(first)
        def _():
            dcr_ref[...] = jnp.zeros(dcr_ref.shape, F32)
            dci_ref[...] = jnp.zeros(dci_ref.shape, F32)
        ov = o_ref[...]
        ro = lax.rsqrt(_rowmean(ov * ov) + EPS)
        on_ = ov * ro
        g = vd_ref[0:1, :]
        dx2v = dx2_ref[...]
        don = dx2v * vd_ref[1:2, :]
        _acc_rows(vpd_ref, first, [_colsum(dx2v * on_ * g), _colsum(don * on_)])
        dxn = don * g
        dob = (ro * (dxn - on_ * _rowmean(dxn * on_))).astype(BF16)
        do_ref[...] = dob
        dyc_a =lax.dot_general(dob, wo_ref[0:dh, :], (((1,), (1,)), ((), ())), preferred_element_type=F32)
        dyc_b = lax.dot_general(dob, wo_ref[dh:2 * dh, :], (((1,), (1,)), ((), ())), preferred_element_type=F32)
        y1v = y1_ref[...]
        u = u_ref[...]
        z, dz_dy1 = _gelu_and_grad(y1v)
        zb = z.astype(BF16)
        sg = _sigmoid(jnp.dot(zb, gw_ref[...], preferred_element_type=F32) + p_ref[1:2, :])
        ya = z * sg
        ra = lax.rsqrt(_head_ms(ya, h16_ref) + EPS)
        yan = ya * ra
        ga = p_ref[2:3, :]
        ycat_ref[:, 0:dh] = (yan * ga).astype(BF16)
        dyn = dyc_a * ga
        dya = ra * (dyn - yan * _split_dot(dyn * yan, h16_ref[...]))
        dq = dya * z * sg * (1.0 - sg)
        dqb = dq.astype(BF16)
        z_ref[...] = zb
        dq_ref[...] = dqb
        dz = dya * sg + lax.dot_general(dqb, gw_ref[...], (((1,), (1,)), ((), ())), preferred_element_type=F32)
        dy1 = dz * dz_dy1
        dy1_ref[...] = dy1
        dy1b = dy1.astype(BF16)
        for q in range(SSM_SPLIT):
            rq, cq = slice(q * ds, (q + 1) * ds), slice(q * du, (q + 1) * du)
            gr_ref[:, rq] = lax.dot_general(dy1b[:, cq], cre_ref[rq, cq], (((1,), (1,)), ((), ())),
                                            preferred_element_type=F32)
            gi_ref[:, rq] = -lax.dot_general(dy1b[:, cq], cim_ref[rq, cq], (((1,), (1,)), ((), ())),
                                             preferred_element_type=F32)
            dcr_ref[rq, :] += _dot_tn(sr_ref[:, rq], dy1b[:, cq])
            dci_ref[rq, :] += _dot_tn(si_ref[:, rq], dy1b[:, cq])
        bg = bg_ref[...]
        cv = cg_ref[...] * v_ref[...]
        cvh = jnp.where(i > 0, cgh_ref[...] * vh_ref[...], 0.0)
        cv1 = _shift_down(cv, cvh, 1)
        cv2 = _shift_down(cv, cvh, 2)
        cc = cw_ref[0:1, :] * cv2 + cw_ref[1:2, :] * cv1 + cw_ref[2:3, :] * cv
        yb = bg * cc
        rb = lax.rsqrt(_head_ms(yb, h64_ref) + EPS)
        ybn = yb * rb
        gb = p_ref[3:4, :]
        ycat_ref[:, dh:2 * dh] = (ybn * gb).astype(BF16)
        dynb = dyc_b * gb
        dyb = rb * (dynb - ybn * _split_dot(dynb * ybn, h64_ref[...]))
        dcc = dyb * bg
        dbg_ref[...] = dyb * cc
        dcc_ref[...] = dcc
        _acc_rows(vp5_ref, first, [_colsum(dyc_a * yan), _colsum(dyc_b * ybn), _colsum(dq), _colsum(dy1 * u),
                                   _colsum(dcc * cv2), _colsum(dcc * cv1), _colsum(dcc * cv)])

    return pl.pallas_call(
        body, name="mix_out_bwd", grid=(t // tb,),
        out_shape=(jax.ShapeDtypeStruct((t, d), BF16), jax.ShapeDtypeStruct((t, 2 * dh), BF16),
                   jax.ShapeDtypeStruct((t, dh), BF16), jax.ShapeDtypeStruct((t, dh), BF16),
                   jax.ShapeDtypeStruct((t, dh), F32), jax.ShapeDtypeStruct((t, nstate), F32),
                   jax.ShapeDtypeStruct((t, nstate), F32), jax.ShapeDtypeStruct((t, dh), F32),
                   jax.ShapeDtypeStruct((t, dh), F32), jax.ShapeDtypeStruct((SUBLANES, d), F32),
                   jax.ShapeDtypeStruct((SUBLANES, dh), F32), jax.ShapeDtypeStruct((nstate, du), F32),
                   jax.ShapeDtypeStruct((nstate, du), F32)),
        in_specs=[_rows(tb, d), _rows(tb, d), _rows(tb, dh), _rows(tb, dh, 0), _rows(tb, dh, 1), _rows(tb, dh, 2),
                  _rows(tb, dh, 3), _halo_prev(tb, dh, 2), _halo_prev(tb, dh, 3), _resident(c_re.shape),
                  _resident(c_im.shape), _full(v512.shape), _full(convw.shape), _resident(glu_w.shape),
                  _resident(h16.shape), _resident(h64.shape), _resident(w_out.shape), _full(vd.shape),
                  _rows(tb, nstate), _rows(tb, nstate)],
        out_specs=(_rows(tb, d), _rows(tb, 2 * dh), _rows(tb, dh), _rows(tb, dh), _rows(tb, dh), _rows(tb, nstate),
                   _rows(tb, nstate), _rows(tb, dh), _rows(tb, dh), _full((SUBLANES, d)), _full((SUBLANES, dh)),
                   _full((nstate, du)), _full((nstate, du))),
        compiler_params=_cparams(("arbitrary",), VMEM_BIG),
    )(dx2, o, y1, proj, proj, proj, proj, proj, proj, c_re, c_im, v512, convw, glu_w, h16, h64, w_out, vd,
      s_re, s_im)


def _mix_in_bwd(gt_re, gt_im, b_re, b_im, dy1, dcc, dbg, proj, x, dx2, vec, v512, convw, w_in_st):
    t, d = x.shape
    dh = dy1.shape[1]
    nstate = gt_re.shape[1]
    du_w, ds = dh // SSM_SPLIT, nstate // SSM_SPLIT
    ns, _, nc = w_in_st.shape
    tb = _blk(t, TB_MIX)
    nblk = t // tb

    def body(gr_ref, gi_ref, bre_ref, bim_ref, dy1_ref, dcc_ref, dccn_ref, dbg_ref, u_ref, cg_ref, v_ref, x_ref,
             dx2_ref, vec_ref, p_ref, cw_ref, w_ref, gx_ref, dproj_ref, vp_ref, dbr_ref, dbi_ref):
        i = pl.program_id(0)

        @pl.when(i == 0)
        def _():
            dbr_ref[...] = jnp.zeros(dbr_ref.shape, F32)
            dbi_ref[...] = jnp.zeros(dbi_ref.shape, F32)
        ub = u_ref[...].astype(BF16)
        du = []
        for q in range(SSM_SPLIT):
            rq, cq = slice(q * du_w, (q + 1) * du_w), slice(q * ds, (q + 1) * ds)
            du.append(lax.dot_general(gr_ref[:, cq].astype(BF16), bre_ref[rq, cq], (((1,), (1,)), ((), ())),
                                      preferred_element_type=F32)
                      + lax.dot_general(gi_ref[:, cq].astype(BF16), bim_ref[rq, cq], (((1,), (1,)), ((), ())),
                                        preferred_element_type=F32))
            dbr_ref[rq, :] += _dot_tn(ub[:, rq], gr_ref[:, cq])
            dbi_ref[rq, :] += _dot_tn(ub[:, rq], gi_ref[:, cq])
        du = dy1_ref[...] * p_ref[0:1, :] + jnp.concatenate(du, axis=1)
        dcc = dcc_ref[...]
        dccn = jnp.where(i < nblk - 1, dccn_ref[...], 0.0)
        dcv = (cw_ref[2:3, :] * dcc + cw_ref[1:2, :] * _shift_up(dcc, dccn, 1)
               + cw_ref[0:1, :] * _shift_up(dcc, dccn, 2))
        parts = [du, dbg_ref[...], dcv * v_ref[...], dcv * cg_ref[...]]
        xv = x_ref[...]
        r = lax.rsqrt(_rowmean(xv * xv) + EPS)
        xn = xv * r
        g = vec_ref[0:1, :]
        hg = xn * g
        dh1 = None
        for j in range(ns):
            pb = parts[j].astype(BF16)
            dproj_ref[:, j * nc:(j + 1) * nc] = pb
            pj =lax.dot_general(pb, w_ref[j], (((1,), (1,)), ((), ())), preferred_element_type=F32)
            dh1 = pj if dh1 is None else dh1 + pj
        dhg = dh1 * vec_ref[1:2, :]
        _acc_rows(vp_ref, i == 0, [_colsum(dh1), _colsum(dh1 * hg), _colsum(dhg * xn)])
        dxn = dhg * g
        gx_ref[...] = dx2_ref[...] + r * (dxn - xn * _rowmean(dxn * xn))

    assert nc == dh and ns == 4
    return pl.pallas_call(
        body, name="mix_in_bwd", grid=(nblk,),
        out_shape=(jax.ShapeDtypeStruct((t, d), F32), jax.ShapeDtypeStruct((t, ns * nc), BF16),
                   jax.ShapeDtypeStruct((SUBLANES, d), F32), jax.ShapeDtypeStruct((dh, ds), F32),
                   jax.ShapeDtypeStruct((dh, ds), F32)),
        in_specs=[_rows(tb, nstate), _rows(tb, nstate), _resident(b_re.shape), _resident(b_im.shape), _rows(tb, dh),
                  _rows(tb, dh), _halo_next(tb, dh, t), _rows(tb, dh), _rows(tb, dh, 0), _rows(tb, dh, 2),
                  _rows(tb, dh, 3), _rows(tb, d), _rows(tb, d), _full(vec.shape), _full(v512.shape),
                  _full(convw.shape), _resident(w_in_st.shape)],
        out_specs=(_rows(tb, d), _rows(tb, ns * nc), _full((SUBLANES, d)), _full((dh, ds)), _full((dh, ds))),
        compiler_params=_cparams(("arbitrary",), VMEM_BIG),
    )(gt_re, gt_im, b_re, b_im, dy1, dcc, dcc, dbg, proj, proj, proj, x, dx2, vec, v512, convw, w_in_st)


def _matmul_tn(a, b, m, bn, out_dtype, name, diag=False, bt=TB_TN, after=None):
    t = a.shape[0]
    n = b.shape[1]
    bt = _blk(t, bt)
    nk = t // bt
    extra = [] if after is None else [after]
    a_map = (lambda j, k: (k, j)) if diag else (lambda j, k: (k, 0))

    def body(a_ref, b_ref, *rest):
        o_ref, acc_ref = rest[-2:]
        k = pl.program_id(1)

        @pl.when(k == 0)
        def _():
            acc_ref[...] = jnp.zeros(acc_ref.shape, F32)
        acc_ref[...] += _dot_tn(a_ref[...], b_ref[...])

        @pl.when(k == nk - 1)
        def _():
            o_ref[...] = acc_ref[...].astype(out_dtype)

    return pl.pallas_call(
        body, name=name, grid=(n // bn, nk),
        out_shape=jax.ShapeDtypeStruct((n // bn, m, bn), out_dtype),
        in_specs=[pl.BlockSpec((bt, m), a_map), pl.BlockSpec((bt, bn), lambda j, k: (k, j))]
        + [pl.BlockSpec(memory_space=pl.ANY)] * len(extra),
        out_specs=pl.BlockSpec((None, m, bn), lambda j, k: (j, 0, 0)),
        scratch_shapes=[pltpu.VMEM((m, bn), F32)],
        compiler_params=_cparams(("parallel", "arbitrary"), VMEM_BIG),
    )(a, b, *extra)


def _ssm_bgrad(d_bre, d_bim, bt_re, bt_im, rows_in, fold, tile_b):
    gh, cb = d_bre.shape
    nb = SSM_SPLIT
    rb = gh // nb
    gp = nb * cb
    p = fold.shape[1]

    def body(dr_ref, di_ref, br_ref, bi_ref, rin_ref, f_ref, tb_ref, dbr_ref, dbi_ref, rout_ref):
        row = lax.broadcasted_iota(jnp.int32, (rb, cb), 0)
        col = lax.broadcasted_iota(jnp.int32, (rb, cb), 1)
        mask = (row >> 4) == (col >> 6)
        gr = jnp.where(mask, dr_ref[...], 0.0)
        gi = jnp.where(mask, di_ref[...], 0.0)
        cr, ci = rin_ref[0:1, :], rin_ref[1:2, :]
        dbr_ref[...] = _split3_dot(cr * gr + ci * gi, f_ref[...])
        dbi_ref[...] = _split3_dot(cr * gi - ci * gr, f_ref[...])
        br = _split3_dot(br_ref[...], tb_ref[...])
        bi = _split3_dot(bi_ref[...], tb_ref[...])
        rout_ref[...] = jnp.zeros(rout_ref.shape, F32)
        rout_ref[0:1, :] = _colsum(br * gr + bi * gi)
        rout_ref[1:2, :] = _colsum(br * gi - bi * gr)

    dspec = pl.BlockSpec((rb, cb), lambda j: (j, 0))
    rspec = pl.BlockSpec((SUBLANES, cb), lambda j: (0, j))
    ospec = pl.BlockSpec((rb, p), lambda j: (j, 0))
    return pl.pallas_call(
        body, name="ssm_bgrad", grid=(nb,),
        out_shape=(jax.ShapeDtypeStruct((gh, p), F32), jax.ShapeDtypeStruct((gh, p), F32),
                   jax.ShapeDtypeStruct((SUBLANES, gp), F32)),
        in_specs=[dspec, dspec, ospec, ospec, rspec, _full(fold.shape), _full(tile_b.shape)],
        out_specs=(ospec, ospec, rspec),
        compiler_params=_cparams(("parallel",)),
    )(d_bre, d_bim, bt_re, bt_im, rows_in, fold, tile_b)


def _ssm_cgrad(d_cre, d_cim, fold):
    gp, cb = d_cre.shape
    nb = SSM_SPLIT
    rb = gp // nb
    h = fold.shape[1]

    def body(dr_ref, di_ref, f_ref, cr_ref, ci_ref):
        row = lax.broadcasted_iota(jnp.int32, (rb, cb), 0)
        col = lax.broadcasted_iota(jnp.int32, (rb, cb), 1)
        mask = (row >> 6) == (col >> 4)
        cr_ref[...] = _split3_dot(jnp.where(mask, dr_ref[...], 0.0), f_ref[...])
        ci_ref[...] = -_split3_dot(jnp.where(mask, di_ref[...], 0.0), f_ref[...])

    cspec = pl.BlockSpec((rb, cb), lambda j: (j, 0))
    ospec = pl.BlockSpec((rb, h), lambda j: (j, 0))
    return pl.pallas_call(
        body, name="ssm_cgrad", grid=(nb,),
        out_shape=(jax.ShapeDtypeStruct((gp, h), F32),) * 2,
        in_specs=[cspec, cspec, _full(fold.shape)], out_specs=(ospec, ospec),
        compiler_params=_cparams(("parallel",)),
    )(d_cre, d_cim, fold)


def _ssm_lamgrad(lam_re, lam_im, log_step, abar_re, abar_im, coef_re, coef_im, gc_re, gc_im, ga_re, ga_im):
    g, p = lam_re.shape

    def body(lr_ref, li_ref, ls_ref, ar_ref, ai_ref, cr_ref, ci_ref, gcr_ref, gci_ref, gar_ref, gai_ref,
             dlr_ref, dli_ref, dls_ref):
        lam_raw = lr_ref[...]
        lr = jnp.minimum(lam_raw, LAMBDA_RE_MAX)
        li = li_ref[...]
        st = jnp.exp(ls_ref[...])
        den = lr * lr + li * li
        gcr, gci = gcr_ref[...], gci_ref[...]
        gab_r = gar_ref[...] + (lr * gcr - li * gci) / den
        gab_i = gai_ref[...] + (lr * gci + li * gcr) / den
        cr, ci = cr_ref[...], ci_ref[...]
        wr = -(cr * lr + ci * li) / den
        wi = -(ci * lr - cr * li) / den
        gl_r = wr * gcr + wi * gci
        gl_i = wr * gci - wi * gcr
        ar, ai = ar_ref[...], ai_ref[...]
        gw_r = ar * gab_r + ai * gab_i
        gw_i = ar * gab_i - ai * gab_r
        gl_r = gl_r + st * gw_r
        gl_i = gl_i + st * gw_i
        pass_through = jnp.where(lam_raw < LAMBDA_RE_MAX, 1.0, jnp.where(lam_raw == LAMBDA_RE_MAX, 0.5, 0.0))
        dlr_ref[...] = gl_r * pass_through
        dli_ref[...] = gl_i
        dls_ref[...] = st * jnp.sum(lr * gw_r + li * gw_i, axis=1, keepdims=True)

    sds = jax.ShapeDtypeStruct((g, p), F32)
    return pl.pallas_call(body, name="ssm_lamgrad", out_shape=(sds, sds, jax.ShapeDtypeStruct((g, 1), F32)))(
        lam_re, lam_im, log_step, abar_re, abar_im, coef_re, coef_im, gc_re, gc_im, ga_re, ga_im)


def _row_block(r, most=256):
    for rb in range(min(r, most), BF16_ROWS - 1, -1):
        if r % rb == 0 and rb % BF16_ROWS == 0:
            return rb
    return r


def _adamw_math(w, g, m, v):
    m = ADAM_B1 * m + (1.0 - ADAM_B1) * g
    v = ADAM_B2 * v + (1.0 - ADAM_B2) * (g * g)
    m_hat = m / (1.0 - ADAM_B1 ** ADAM_STEP)
    v_hat = v / (1.0 - ADAM_B2 ** ADAM_STEP)
    delta = -ADAM_LR * (m_hat / (jnp.sqrt(v_hat) + ADAM_EPS) + ADAM_WD * w)
    return delta, m, v


def _adamw_big(p_mine, p_sib, w, m, v, name):
    r, c = w.shape
    rb = _row_block(r)

    def body(a_ref, b_ref, w_ref, m_ref, v_ref, g_ref, d_ref, mo_ref, vo_ref):
        g = a_ref[...] + b_ref[...]
        g_ref[...] = g
        d_ref[...], mo_ref[...], vo_ref[...] = _adamw_math(w_ref[...], g, m_ref[...], v_ref[...])

    spec = pl.BlockSpec((rb, c), lambda i: (i, 0))
    sds = jax.ShapeDtypeStruct((r, c), F32)
    return pl.pallas_call(
        body, name=name, grid=(r // rb,), out_shape=(sds,) * 4, in_specs=[spec] * 5, out_specs=(spec,) * 4,
        compiler_params=_cparams(("parallel",), VMEM_MID),
    )(p_mine, p_sib, w, m, v)


def _sum_blocks(stack, name):
    n, r, c = stack.shape
    rb = _row_block(r)

    def body(s_ref, o_ref):
        acc = s_ref[0].astype(F32)
        for k in range(1, n):
            acc = acc + s_ref[k].astype(F32)
        o_ref[...] = acc

    return pl.pallas_call(
        body, name=name, grid=(r // rb,), out_shape=jax.ShapeDtypeStruct((r, c), F32),
        in_specs=[pl.BlockSpec((n, rb, c), lambda i: (0, i, 0))], out_specs=pl.BlockSpec((rb, c), lambda i: (i, 0)),
        compiler_params=_cparams(("parallel",), VMEM_MID),
    )(stack)


def _add2(a, b):
    def body(a_ref, b_ref, o_ref):
        o_ref[...] = a_ref[...] + b_ref[...]

    return pl.pallas_call(body, name="add_small", out_shape=jax.ShapeDtypeStruct(a.shape, F32))(a, b)


def _adamw_ada(c_all, dmod_cols, w, m, v):
    d, n = w.shape
    bn = 512

    def body(c_ref, dm_ref, w_ref, m_ref, v_ref, g_ref, d_ref, mo_ref, vo_ref):
        cc = c_ref[...]
        g = _dot_tn(cc * _sigmoid(cc), dm_ref[...])
        g_ref[...] = g
        d_ref[...], mo_ref[...], vo_ref[...] = _adamw_math(w_ref[...], g, m_ref[...], v_ref[...])

    spec = pl.BlockSpec((d, bn), lambda j: (0, j))
    sds = jax.ShapeDtypeStruct((d, n), F32)
    return pl.pallas_call(
        body, name="adamw_ada", grid=(n // bn,), out_shape=(sds,) * 4,
        in_specs=[_full((N_DEV, d)), pl.BlockSpec((N_DEV, bn), lambda j: (0, j)), spec, spec, spec],
        out_specs=(spec,) * 4, compiler_params=_cparams(("parallel",)),
    )(c_all, dmod_cols, w, m, v)


def _adamw_small(items):
    n = len(items)

    def body(*refs):
        ins, outs = refs[:4 * n], refs[4 * n:]
        for k in range(n):
            w_ref, g_ref, m_ref, v_ref = ins[4 * k:4 * k + 4]
            outs[3 * k][...], outs[3 * k + 1][...], outs[3 * k + 2][...] = _adamw_math(
                w_ref[...], g_ref[...], m_ref[...], v_ref[...])

    flat = [a for it in items for a in it]
    out_shape = tuple(jax.ShapeDtypeStruct(it[0].shape, F32) for it in items for _ in range(3))
    res = pl.pallas_call(body, name="adamw_small", out_shape=out_shape,
                         compiler_params=_cparams(vmem=VMEM_BIG))(*flat)
    return [tuple(res[3 * k:3 * k + 3]) for k in range(n)]


def _group_mean_matrix(n, group):
    idx = np.arange(n) // group
    return (idx[:, None] == idx[None, :]).astype(np.float32) / group


def _fold_matrix(n, period):
    return (np.arange(n)[:, None] % period == np.arange(period)[None, :]).astype(np.float32)


def _rows8(*rows):
    c = rows[0].shape[-1]
    pad = jnp.zeros((SUBLANES - len(rows), c), F32)
    return jnp.concatenate([r.reshape(1, c) for r in rows] + [pad], axis=0)


def _to_rows(a, width):
    flat = a.reshape(-1)
    n = -(-flat.shape[0] // width)
    flat = jnp.pad(flat, (0, n * width - flat.shape[0]))
    return flat.reshape(n, width)


def kernel(x, c, w_ada, b_ada, g_pre_mix, g_post_mix, w_in, ssm_lam_re, ssm_lam_im, ssm_log_step, ssm_b_re, ssm_b_im, ssm_c_re, ssm_c_im, ssm_d, glu_w, glu_b, g_out_ssm, conv_w, g_out_conv, w_out, g_pre_ffn, g_post_ffn, w_up, ffn_conv_w, w_down, loss_target, m_w_ada, m_b_ada, m_g_pre_mix, m_g_post_mix, m_w_in, m_ssm_lam_re, m_ssm_lam_im, m_ssm_log_step, m_ssm_b_re, m_ssm_b_im, m_ssm_c_re, m_ssm_c_im, m_ssm_d, m_glu_w, m_glu_b, m_g_out_ssm, m_conv_w, m_g_out_conv, m_w_out, m_g_pre_ffn, m_g_post_ffn, m_w_up, m_ffn_conv_w, m_w_down, v_w_ada, v_b_ada, v_g_pre_mix, v_g_post_mix, v_w_in, v_ssm_lam_re, v_ssm_lam_im, v_ssm_log_step, v_ssm_b_re, v_ssm_b_im, v_ssm_c_re, v_ssm_c_im, v_ssm_d, v_glu_w, v_glu_b, v_g_out_ssm, v_conv_w, v_g_out_conv, v_w_out, v_g_pre_ffn, v_g_post_ffn, v_w_up, v_ffn_conv_w, v_w_down):
    xs = x[0]
    tgt = loss_target[0]
    t, d = xs.shape
    xi, yi, ci = lax.axis_index("x"), lax.axis_index("y"), lax.axis_index("c")
    chip = 2 * xi + yi
    dev = 2 * chip + ci

    n_groups, n_state = ssm_lam_re.shape[1:]
    n_gch = ssm_b_re.shape[3]
    d_ssm = n_groups * n_gch
    gp = n_groups * n_state
    n_ada = w_ada.shape[2]
    d_ff = w_down.shape[1] * N_CHIPS
    n_upc = w_up.shape[2]

    w_names = ("w_in", "glu_w", "w_out", "w_up", "w_down")
    c_gath, _ = _allgather8(jnp.broadcast_to(c, (SUBLANES, d)), SUBLANES, "gather_c")
    c_all = c_gath.reshape(N_DEV, SUBLANES, d)[:, 0, :]

    def pad8(a):
        return jnp.concatenate([a, jnp.zeros((SUBLANES - a.shape[0], a.shape[1]), a.dtype)], axis=0)

    def start(name, arrs, after):
        return _chips_start(name, True, [], [_landing(a, chip) for a in arrs], after)

    w_names = ("w_in", "mod", "conv_w", "ffn_conv_w", "glu_w", "w_out", "w_up", "w_down")
    first = start("weights_start_in", [w_in[0].astype(BF16)], c_gath)
    b_sh = lax.dynamic_slice(b_ada, (0, chip * n_ada), (1, n_ada))
    mod_sh = _mod_shard(c_all + first[4][0:1, 0:1], w_ada[0], b_sh)
    second = start("weights_start_rest", [mod_sh, pad8(conv_w[0]), pad8(ffn_conv_w[0])]
                   + [w[0].astype(BF16) for w in (glu_w, w_out, w_up, w_down)], None)
    w_send, w_recv, w_land = [list(first[k]) + list(second[k]) for k in (0, 1, 3)]
    w_token = second[4]

    def weights(names, after):
        ks = [w_names.index(nm) for nm in names]
        return _chips_wait("weights_wait_" + names[-1], True, [w_send[k] for k in ks], [w_recv[k] for k in ks],
                           [], [w_land[k] for k in ks], after)

    lam_re, lam_im = ssm_lam_re[0], ssm_lam_im[0]
    log_step = ssm_log_step[0].reshape(n_groups, 1) + w_token[0:1, 0:1]
    abar_re, abar_im, coef_re, coef_im = _ssm_prep(lam_re, lam_im, log_step)
    a_rows = _rows8(abar_re.reshape(1, gp), abar_im.reshape(1, gp))
    coef_rows = _rows8(coef_re.reshape(1, gp), coef_im.reshape(1, gp))
    bt_re = ssm_b_re[0].transpose(0, 2, 1).reshape(d_ssm, n_state)
    bt_im = ssm_b_im[0].transpose(0, 2, 1).reshape(d_ssm, n_state)
    ct_re = ssm_c_re[0].transpose(0, 2, 1).reshape(gp, n_gch)
    ct_im = ssm_c_im[0].transpose(0, 2, 1).reshape(gp, n_gch)
    tile_b = jnp.asarray(np.tile(np.eye(n_state), (1, n_groups // SSM_SPLIT)), BF16)
    tile_c = jnp.asarray(np.tile(np.eye(n_gch), (1, n_groups)), BF16)
    bblk_re, bblk_im, cblk_re, cblk_im = _ssm_blocks(bt_re, bt_im, ct_re, ct_im, coef_rows, tile_b, tile_c)

    h16 = jnp.asarray(_group_mean_matrix(d_ssm, n_gch), BF16)
    h64 = jnp.asarray(_group_mean_matrix(d_ssm, CONV_HEAD_DIM), BF16)

    g_mod, g_cw, g_fw, w_in_st = weights(("mod", "conv_w", "ffn_conv_w", "w_in"), bblk_re)
    mod_all = g_mod.transpose(1, 0, 2).reshape(N_DEV, N_CHIPS * n_ada)
    mod = lax.dynamic_slice(mod_all, (dev, 0), (1, N_CHIPS * n_ada))
    sh1, sc1, gt1, sh2, sc2, gt2 = [mod[:, k * d:(k + 1) * d] for k in range(6)]
    convw_full = pad8(g_cw[:, :3, :].transpose(1, 0, 2).reshape(3, d_ssm))
    fw_full = pad8(g_fw[:, :3, :].transpose(1, 0, 2).reshape(3, N_CHIPS * n_upc))

    v512 = _rows8(ssm_d, glu_b, g_out_ssm, g_out_conv)
    vec1 =_rows8(g_pre_mix, 1.0 + sc1, sh1)
    vd1 = _rows8(g_post_mix, gt1)
    vec2 = _rows8(g_pre_ffn, 1.0 + sc2, sh2)
    vd2 = _rows8(g_post_ffn, gt2)

    proj, bu_re, bu_im, h1b = _mix_in(xs, vec1, w_in_st, bblk_re, bblk_im)
    s_re, s_im = _scan_fwd(a_rows, bu_re, bu_im)
    g_glu, g_wout = weights(("glu_w", "w_out"), s_re)
    glu_full = g_glu.reshape(d_ssm, d_ssm)
    w_out_full = g_wout.reshape(2 * d_ssm, d)
    y1, o_mix, x2 = _mix_out(xs, proj, s_re, s_im, cblk_re, cblk_im, v512, convw_full, glu_full, h16, h64,
                             w_out_full, vd1)
    (w_up_st,) = weights(("w_up",), x2)
    up, h2b = _ffn_up(x2, vec2, w_up_st)
    (g_wdown,) = weights(("w_down",), up)
    w_down_full = g_wdown.reshape(d_ff, d)
    actb, ddnb, dout, dhid, vp_dn, loss_blk = _ffn_down(up, fw_full, w_down_full, w_down_full.T, x2, tgt, vd2)

    g_names = ("w_down", "w_up", "w_out", "glu_w", "w_in")
    gw_down = _matmul_tn(actb, ddnb, d_ff, d, BF16, "dw_down", bt=1024).reshape(N_CHIPS, d_ff // N_CHIPS, d)
    dx2, dupb, vp_up, df_rows = _ffn_up_bwd(dhid, up, fw_full, x2, dout, vec2, w_up_st)
    gw_up = _matmul_tn(h2b, dupb, d, n_upc, BF16, "dw_up", bt=2048)
    ga_send, ga_recv, ga_src, ga_land, ga_token = _chips_start(
        "grads_start_ffn", False, [gw_down, gw_up],
        [_landing(lax.dynamic_index_in_dim(g, chip, 0, False), chip) for g in (gw_down, gw_up)])
    (dob, ycatb, zb, dqb, dy1, g_re, g_im, dcc, dbg, vp_mo, vp5, d_cre, d_cim) = _mix_out_bwd(
        dx2, o_mix, y1, proj, s_re, s_im, cblk_re, cblk_im, v512, convw_full, glu_full, h16, h64, w_out_full,
        vd1 + ga_token[0:1, 0:1])
    gw_out = _matmul_tn(ycatb, dob, 2 * d_ssm, d, BF16, "dw_out", bt=2048)
    gw_out = gw_out.reshape(N_CHIPS, 2 * d_ssm // N_CHIPS, d)
    gw_glu = _matmul_tn(zb, dqb, d_ssm, d_ssm, BF16, "dw_glu", bt=2048).reshape(N_CHIPS, d_ssm // N_CHIPS, d_ssm)
    gb_send, gb_recv, gb_src, gb_land, gb_token = _chips_start(
        "grads_start_mix", False, [gw_out, gw_glu],
        [_landing(lax.dynamic_index_in_dim(g, chip, 0, False), chip) for g in (gw_out, gw_glu)])
    gt_re, gt_im, ga_re8, ga_im8 = _scan_bwd(a_rows + gb_token[0:1, 0:1], g_re, g_im, s_re, s_im)
    grad_x, dprojb, vp_mi, d_bre, d_bim = _mix_in_bwd(gt_re, gt_im, bblk_re, bblk_im, dy1, dcc, dbg, proj, xs, dx2,
                                                      vec1, v512, convw_full, w_in_st)
    ssm_u, ssm_s = d_ssm // SSM_SPLIT, gp // SSM_SPLIT

    fold_b = jnp.asarray(_fold_matrix(ssm_s, n_state), BF16)
    fold_c = jnp.asarray(_fold_matrix(ssm_u, n_gch), BF16)
    db_re_f, db_im_f, gc_rows = _ssm_bgrad(d_bre, d_bim, bt_re, bt_im, coef_rows, fold_b, tile_b)
    dc_re_f, dc_im_f = _ssm_cgrad(d_cre, d_cim, fold_c)
    ga_sum = _ga_rowsum(ga_re8, ga_im8)
    g_lam_re, g_lam_im, g_log_step = _ssm_lamgrad(
        lam_re, lam_im, log_step, abar_re, abar_im, coef_re, coef_im,
        gc_rows[0].reshape(n_groups, n_state), gc_rows[1].reshape(n_groups, n_state),
        ga_sum[0].reshape(n_groups, n_state), ga_sum[1].reshape(n_groups, n_state))
    g_b_re = db_re_f.reshape(n_groups, n_gch, n_state).transpose(0, 2, 1)
    g_b_im = db_im_f.reshape(n_groups, n_gch, n_state).transpose(0, 2, 1)
    g_c_re = dc_re_f.reshape(n_groups, n_state, n_gch).transpose(0, 2, 1)
    g_c_im = dc_im_f.reshape(n_groups, n_state, n_gch).transpose(0, 2, 1)

    dmod = jnp.concatenate([vp_mi[0:1], vp_mi[1:2], vp_mo[0:1], vp_up[0:1], vp_up[1:2], vp_dn[0:1]], axis=1)
    small = [
        ("g_pre_mix", vp_mi[2:3]), ("g_post_mix", vp_mo[1:2]), ("g_pre_ffn", vp_up[2:3]), ("g_post_ffn", vp_dn[1:2]),
        ("ssm_lam_re", g_lam_re), ("ssm_lam_im", g_lam_im), ("ssm_log_step", g_log_step),
        ("ssm_b_re", g_b_re), ("ssm_b_im", g_b_im), ("ssm_c_re", g_c_re), ("ssm_c_im", g_c_im),
        ("ssm_d", vp5[3:4]), ("glu_b", vp5[2:3]), ("g_out_ssm", vp5[0:1]), ("g_out_conv", vp5[1:2]),
        ("conv_w", vp5[4:7]), ("ffn_conv_w", df_rows[0:3]), ("loss", loss_blk[0:1, 0:1]),
    ]
    packed, offsets, row = [], {}, 0
    for name, a in small:
        r = _to_rows(a, d)
        offsets[name] = (row, a.shape)
        packed.append(r)
        row += r.shape[0]
    n_small = -(-row // SUBLANES) * SUBLANES
    packed.append(jnp.zeros((n_small - row, d), F32))
    packed.append(pad8(dmod.reshape(6, d)))
    pack = jnp.concatenate(packed, axis=0)
    sm_send, sm_recv, _, sm_land, sm_token = _chips_start("small_start", True, [], [_landing(pack, chip)])

    gw_in = _matmul_tn(h1b, dprojb, d, w_in.shape[2], BF16, "dw_in", bt=2048, after=sm_token)
    gc_send, gc_recv, gc_src, gc_land, gc_token = _chips_start(
        "grads_start_in", False, [gw_in], [_landing(lax.dynamic_index_in_dim(gw_in, chip, 0, False), chip)])

    def partials(names, landed):
        return [_sum_blocks(s, "sum_" + nm) for s, nm in zip(landed, names)]

    def update(names, mine, theirs):
        done = {}
        for nm, pm, ps in zip(names, mine, theirs):
            w_, m_, v_ = big_params[nm]
            done[nm] = _adamw_big(pm, ps, w_[0], m_[0], v_[0], "adamw_" + nm)
        return done

    big_params = {"w_down": (w_down, m_w_down, v_w_down), "w_up": (w_up, m_w_up, v_w_up),
                  "w_out": (w_out, m_w_out, v_w_out), "glu_w": (glu_w, m_glu_w, v_glu_w),
                  "w_in": (w_in, m_w_in, v_w_in)}
    ffn_names, mix_names = ("w_down", "w_up"), ("w_out", "glu_w", "w_in")
    p_ffn = partials(ffn_names, _chips_wait("grads_wait_ffn", False, ga_send, ga_recv, ga_src, ga_land, gc_token))
    sa_send, sa_recv, sa_src, sa_land, sa_token = _sibling_start("swap_start_ffn", p_ffn)

    (sm_landed,) = _chips_wait("small_wait", True, sm_send, sm_recv, [], sm_land, sa_token)
    sm_part = _sum_blocks(sm_landed, "sum_small")
    dmod_mine = sm_landed[:, n_small:n_small + SUBLANES, :]
    ss_send, ss_recv, ss_src, ss_land, ss_token = _sibling_start("swap_start_small", [sm_part, dmod_mine])
    p_ffn, t_ffn = _sibling_wait("swap_wait_ffn", sa_send, sa_recv, sa_src, sa_land, ss_token)
    big = update(ffn_names, p_ffn, t_ffn)
    (sm_part, dmod_mine), (sm_sib, dmod_sib) = _sibling_wait("swap_wait_small", ss_send, ss_recv, ss_src, ss_land,
                                                              big["w_up"][0])
    sums = _add2(sm_part, sm_sib)
    dmod_by_core = jnp.stack([dmod_mine, dmod_sib], axis=1)
    dmod_by_core = jnp.where(ci == 0, dmod_by_core, dmod_by_core[:, ::-1])
    dmod_all = dmod_by_core[:, :, :6, :].reshape(N_DEV, 6 * d)
    g_b_ada = sums[n_small:n_small + 6].reshape(1, 6 * d)

    def unpack(name):
        r0, shape = offsets[name]
        size = math.prod(shape)
        nrow = -(-size // d)
        return sums[r0:r0 + nrow].reshape(-1)[:size].reshape(shape)

    p_mix = partials(mix_names, _chips_wait(
        "grads_wait_mix", False, list(gb_send) + list(gc_send), list(gb_recv) + list(gc_recv),
        list(gb_src) + list(gc_src), list(gb_land) + list(gc_land), sums))
    sb_send, sb_recv, sb_src, sb_land, sb_token = _sibling_start("swap_start_mix", p_mix)

    dmod_cols = lax.dynamic_slice(dmod_all, (0, chip * n_ada), (N_DEV, n_ada)) + sb_token[0:1, 0:1]
    ada = _adamw_ada(c_all, dmod_cols, w_ada[0], m_w_ada[0], v_w_ada[0])
    p_mix, t_mix = _sibling_wait("swap_wait_mix", sb_send, sb_recv, sb_src, sb_land, ada[0])
    big.update(update(mix_names, p_mix, t_mix))

    g_small = {name: unpack(name) for name, _ in small}
    g_small["b_ada"] = g_b_ada
    g_small["conv_w"] = lax.dynamic_slice(g_small["conv_w"], (0, chip * conv_w.shape[2]), (3, conv_w.shape[2]))
    g_small["ffn_conv_w"] = lax.dynamic_slice(g_small["ffn_conv_w"], (0, chip * n_upc), (3, n_upc))
    g_small["ssm_log_step"] = g_small["ssm_log_step"].reshape(1, n_groups)
    small_params = {
        "b_ada": (b_ada, m_b_ada, v_b_ada), "g_pre_mix": (g_pre_mix, m_g_pre_mix, v_g_pre_mix),
        "g_post_mix": (g_post_mix, m_g_post_mix, v_g_post_mix), "ssm_lam_re": (ssm_lam_re, m_ssm_lam_re, v_ssm_lam_re),
        "ssm_lam_im": (ssm_lam_im, m_ssm_lam_im, v_ssm_lam_im),
        "ssm_log_step": (ssm_log_step, m_ssm_log_step, v_ssm_log_step),
        "ssm_b_re": (ssm_b_re, m_ssm_b_re, v_ssm_b_re), "ssm_b_im": (ssm_b_im, m_ssm_b_im, v_ssm_b_im),
        "ssm_c_re": (ssm_c_re, m_ssm_c_re, v_ssm_c_re), "ssm_c_im": (ssm_c_im, m_ssm_c_im, v_ssm_c_im),
        "ssm_d": (ssm_d, m_ssm_d, v_ssm_d), "glu_b": (glu_b, m_glu_b, v_glu_b),
        "g_out_ssm": (g_out_ssm, m_g_out_ssm, v_g_out_ssm), "conv_w": (conv_w, m_conv_w, v_conv_w),
        "g_out_conv": (g_out_conv, m_g_out_conv, v_g_out_conv), "g_pre_ffn": (g_pre_ffn, m_g_pre_ffn, v_g_pre_ffn),
        "g_post_ffn": (g_post_ffn, m_g_post_ffn, v_g_post_ffn),
        "ffn_conv_w": (ffn_conv_w, m_ffn_conv_w, v_ffn_conv_w),
    }

    def natural(a):
        return a[0] if a.ndim > 2 else a

    names = list(small_params)
    items = []
    for nm in names:
        w_, m_, v_ = small_params[nm]
        items.append((natural(w_), g_small[nm].reshape(natural(w_).shape), natural(m_), natural(v_)))
    upd = _adamw_small(items)
    small_out = {}
    for nm, (dl, mo, vo) in zip(names, upd):
        shp = small_params[nm][0].shape
        small_out[nm] = (g_small[nm].reshape(shp), dl.reshape(shp), mo.reshape(shp), vo.reshape(shp))

    loss = g_small["loss"][0, 0]

    order = ["w_ada", "b_ada", "g_pre_mix", "g_post_mix", "w_in", "ssm_lam_re", "ssm_lam_im", "ssm_log_step",
             "ssm_b_re", "ssm_b_im", "ssm_c_re", "ssm_c_im", "ssm_d", "glu_w", "glu_b", "g_out_ssm", "conv_w",
             "g_out_conv", "w_out", "g_pre_ffn", "g_post_ffn", "w_up", "ffn_conv_w", "w_down"]
    results = {"w_ada": tuple(a[None] for a in ada)}
    for nm in big:
        results[nm] = tuple(a[None] for a in big[nm])
    results.update(small_out)
    outs = [loss, grad_x[None]]
    for k in range(4):
        outs += [results[nm][k] for nm in order]
    return tuple(outs)


def _ga_rowsum(ga_re8, ga_im8):
    n = ga_re8.shape[1]

    def body(r_ref, i_ref, o_ref):
        o_ref[...] = jnp.zeros(o_ref.shape, F32)
        o_ref[0:1, :] = _colsum(r_ref[...])
        o_ref[1:2, :] = _colsum(i_ref[...])

    return pl.pallas_call(body, name="ga_rowsum", out_shape=jax.ShapeDtypeStruct((SUBLANES, n), F32))(ga_re8, ga_im8)
```

```python
import functools
import math

import jax
import jax.numpy as jnp
import numpy as np
from jax import lax
from jax.experimental import pallas as pl
from jax.experimental.pallas import tpu as pltpu

F32 = jnp.float32
BF16 = jnp.bfloat16
MESH = pl.DeviceIdType.MESH

EPS = 1e-6
LAMBDA_RE_MAX = -1e-4
ADAM_LR = 0.001
ADAM_B1 = 0.9
ADAM_B2 = 0.999
ADAM_EPS = 1e-08
ADAM_WD = 0.01
ADAM_STEP = 10

SUBLANES = 8
BF16_ROWS = 16
N_CHIPS = 4
N_DEV = 8
CONV_HEAD_DIM = 64
VMEM_BIG = 56 * 1024 * 1024
VMEM_MID = 40 * 1024 * 1024
VMEM_KEEP_OPERANDS_IN_HBM = 62 * 1024 * 1024

TB_MIX = 256
TB_FFN = 256
TB_FFN_UP = 512
TB_SCAN = 1024
W_SCAN = 256
SSM_SPLIT = 4
CW_FFN = 256
SCAN_UNROLL = 4
TB_TN = 512


def _cparams(sem=None, vmem=None):
    kw = {}
    if sem is not None:
        kw["dimension_semantics"] = sem
    if vmem is not None:
        kw["vmem_limit_bytes"] = vmem
    return pltpu.CompilerParams(**kw)


def _blk(t, pref):
    return pref if t % pref == 0 else t


def _dot(a, b):
    return jnp.dot(a.astype(BF16), b.astype(BF16), preferred_element_type=F32)


def _dot_nt(a, b):
    return lax.dot_general(a.astype(BF16), b.astype(BF16), (((1,), (1,)), ((), ())),
                           preferred_element_type=F32)


def _dot_tn(a, b):
    return lax.dot_general(a.astype(BF16), b.astype(BF16), (((0,), (0,)), ((), ())),
                           preferred_element_type=F32)


def _sigmoid(x):
    return 0.5 * jnp.tanh(0.5 * x) + 0.5


_GELU_K = math.sqrt(2.0 / math.pi)
_GELU_C = 0.044715


def _gelu(x):
    th = jnp.tanh(_GELU_K * (x + _GELU_C * x * x * x))
    return x * (0.5 * (1.0 + th))


def _gelu_and_grad(x):
    x2 = x * x
    th = jnp.tanh(_GELU_K * (x + _GELU_C * x2 * x))
    half = 0.5 * (1.0 + th)
    return x * half, half + 0.5 * x * (1.0 - th * th) * _GELU_K * (1.0 + 3.0 * _GELU_C * x2)


def _rowmean(x):
    return jnp.mean(x, axis=-1, keepdims=True)


def _colsum(x):
    return jnp.sum(x, axis=0, keepdims=True)


def _split_dot(x, m):
    hi = x.astype(BF16)
    lo = (x - hi.astype(F32)).astype(BF16)
    return (jnp.dot(hi, m, preferred_element_type=F32) + jnp.dot(lo, m, preferred_element_type=F32))


def _split3_dot(x, m):
    hi = x.astype(BF16)
    r1 = x - hi.astype(F32)
    mid = r1.astype(BF16)
    lo = (r1 - mid.astype(F32)).astype(BF16)
    return (jnp.dot(hi, m, preferred_element_type=F32) + jnp.dot(mid, m, preferred_element_type=F32)
            + jnp.dot(lo, m, preferred_element_type=F32))


def _shift_down(x, halo, k):
    r = pltpu.roll(x, k, 0)
    row = lax.broadcasted_iota(jnp.int32, x.shape, 0)
    for j in range(k):
        r = jnp.where(row == j, halo[SUBLANES - k + j:SUBLANES - k + j + 1, :], r)
    return r


def _shift_up(x, halo, k):
    n = x.shape[0]
    r = pltpu.roll(x, n - k, 0)
    row = lax.broadcasted_iota(jnp.int32, x.shape, 0)
    for j in range(k):
        r = jnp.where(row == n - k + j, halo[j:j + 1, :], r)
    return r


def _acc_rows(ref, first, rows):
    @pl.when(first)
    def _():
        ref[...] = jnp.zeros(ref.shape, ref.dtype)
    for j, r in enumerate(rows):
        ref[j:j + 1, :] += r


def _rows(tb, c, col=0):
    return pl.BlockSpec((tb, c), lambda i, col=col: (i, col))


def _full(shape):
    nd = len(shape)
    return pl.BlockSpec(shape, lambda i, nd=nd: (0,) * nd)


def _resident(shape):
    nd = len(shape)
    return pl.BlockSpec(shape, lambda i, nd=nd: (0,) * nd, pipeline_mode=pl.Buffered(1))


def _halo_prev(tb, c, col=0):
    per = tb // SUBLANES
    return pl.BlockSpec((SUBLANES, c), lambda i, col=col: (jnp.maximum(i * per - 1, 0), col))


def _halo_next(tb, c, t, col=0, rows=SUBLANES):
    per = tb // rows
    last = t // rows - 1
    return pl.BlockSpec((rows, c), lambda i, col=col: (jnp.minimum((i + 1) * per, last), col))


def _mesh_pos():
    return lax.axis_index("x"), lax.axis_index("y"), lax.axis_index("c")


def _allgather8(x_pad, n_sum, name):
    m_per, n = x_pad.shape

    def body(x_ref, out_ref, sum_ref, send_sems, recv_sems, local_sem):
        x, y, c = _mesh_pos()
        me, sibling = (x, y, c), (x, y, 1 - c)
        chips = [(1 - x, y), (x, 1 - y), (1 - x, 1 - y)]

        def rows(px, py, pc):
            return out_ref.at[pl.ds((4 * px + 2 * py + pc) * m_per, m_per), :]

        def copy(k, block, to, src=None):
            return pltpu.make_async_remote_copy(
                src_ref=rows(*block) if src is None else src, dst_ref=rows(*block),
                send_sem=send_sems.at[k], recv_sem=recv_sems.at[k], device_id=to, device_id_type=MESH)

        mine = pltpu.make_async_copy(x_ref, rows(*me), local_sem)
        mine.start()
        first = [copy(0, me, sibling, src=x_ref)]
        first += [copy(1 + j, me, (*chip, c), src=x_ref) for j, chip in enumerate(chips)]
        for cp in first:
            cp.start()
        passed = [copy(4 + j, (*chip, c), sibling) for j, chip in enumerate(chips)]
        for j, chip in enumerate(chips):
            copy(1 + j, (*chip, c), me).wait_recv()
            passed[j].start()
        copy(0, sibling, me).wait_recv()
        for j, chip in enumerate(chips):
            copy(4 + j, (*chip, 1 - c), me).wait_recv()
        for cp in first + passed:
            cp.wait_send()
        mine.wait()
        acc = out_ref[0:n_sum, :]
        for k in range(1, N_DEV):
            acc = acc + out_ref[k * m_per:k * m_per + n_sum, :]
        sum_ref[...] = acc

    return pl.pallas_call(
        body, name=name,
        out_shape=(jax.ShapeDtypeStruct((N_DEV * m_per, n), F32), jax.ShapeDtypeStruct((n_sum, n), F32)),
        in_specs=[pl.BlockSpec(memory_space=pltpu.VMEM)],
        out_specs=(pl.BlockSpec(memory_space=pltpu.VMEM), pl.BlockSpec(memory_space=pltpu.VMEM)),
        scratch_shapes=[pltpu.SemaphoreType.DMA((7,)), pltpu.SemaphoreType.DMA((7,)), pltpu.SemaphoreType.DMA],
        compiler_params=_cparams(vmem=VMEM_MID),
    )(x_pad)


_HBM = pl.BlockSpec(memory_space=pltpu.HBM)
_SEM = pl.BlockSpec(memory_space=pltpu.SEMAPHORE)
_EFFECT = pltpu.SideEffectType.DATAFLOW_SIDE_EFFECTING


def _chip_copy(gather, src_ref, land_ref, send, recv, j, arrival):
    x, y, c = _mesh_pos()
    peer = [(1 - x, y), (x, 1 - y), (1 - x, 1 - y)][j]
    peer_chip = 2 * peer[0] + peer[1]
    my_chip = 2 * x + y
    return pltpu.make_async_remote_copy(
        src_ref=land_ref.at[my_chip] if gather else src_ref.at[peer_chip],
        dst_ref=land_ref.at[peer_chip if arrival else my_chip],
        send_sem=send.at[j], recv_sem=recv.at[j], device_id=(*peer, c), device_id_type=MESH)


def _chips_start(name, gather, srcs, lands, after=None):
    n, ns = len(lands), len(srcs)
    extra = [] if after is None else [after]

    def body(*refs):
        src_refs, land_refs = refs[:ns], refs[ns:ns + n]
        outs = refs[ns + n + len(extra):]
        sends, recvs, token = outs[:n], outs[n:2 * n], outs[-1]
        for k in range(n):
            for j in range(3):
                _chip_copy(gather, src_refs[k] if ns else None, land_refs[k], sends[k], recvs[k], j, False).start()
        token[...] = jnp.zeros(token.shape, F32)

    sem = pltpu.SemaphoreType.DMA((3,))
    thru = tuple(pltpu.HBM(a.shape, a.dtype) for a in list(srcs) + list(lands))
    res = pl.pallas_call(
        body, name=name,
        out_shape=(sem,) * (2 * n) + thru + (jax.ShapeDtypeStruct((SUBLANES, 128), F32),),
        in_specs=[_HBM] * (ns + n) + [pl.BlockSpec(memory_space=pl.ANY)] * len(extra),
        out_specs=(_SEM,) * (2 * n) + (_HBM,) * (ns + n) + (pl.BlockSpec(memory_space=pltpu.VMEM),),
        input_output_aliases={k: 2 * n + k for k in range(ns + n)},
        compiler_params=pltpu.CompilerParams(has_side_effects=_EFFECT),
    )(*[pltpu.with_memory_space_constraint(a, pltpu.HBM) for a in list(srcs) + list(lands)], *extra)
    return res[:n], res[n:2 * n], res[2 * n:2 * n + ns], res[2 * n + ns:2 * n + ns + n], res[-1]


def _chips_wait(name, gather, sends, recvs, srcs, lands, after):
    n, ns = len(lands), len(srcs)

    def body(*refs):
        src_refs, land_refs = refs[:ns], refs[ns:ns + n]
        sends_, recvs_ = refs[ns + n:ns + 2 * n], refs[ns + 2 * n:ns + 3 * n]
        for k in range(n):
            for j in range(3):
                cp = _chip_copy(gather, src_refs[k] if ns else None, land_refs[k], sends_[k], recvs_[k], j, True)
                cp.wait_send()
                cp.wait_recv()

    thru = tuple(pltpu.HBM(a.shape, a.dtype) for a in list(srcs) + list(lands))
    res = pl.pallas_call(
        body, name=name, out_shape=thru,
        in_specs=[_HBM] * (ns + n) + [_SEM] * (2 * n) + [pl.BlockSpec(memory_space=pl.ANY)],
        out_specs=(_HBM,) * (ns + n),
        input_output_aliases={k: k for k in range(ns + n)},
        compiler_params=pltpu.CompilerParams(has_side_effects=_EFFECT),
    )(*srcs, *lands, *sends, *recvs, after)
    return res[ns:]


def _sibling_copy(src_ref, land_ref, send, recv):
    x, y, c = _mesh_pos()
    return pltpu.make_async_remote_copy(src_ref=src_ref, dst_ref=land_ref, send_sem=send.at[0], recv_sem=recv.at[0],
                                        device_id=(x, y, 1 - c), device_id_type=MESH)


def _sibling_start(name, arrs, after=None):
    n = len(arrs)
    extra = [] if after is None else [after]
    lands = [lax.empty(a.shape, a.dtype) for a in arrs]

    def body(*refs):
        src_refs, land_refs = refs[:n], refs[n:2 * n]
        outs = refs[2 * n + len(extra):]
        sends, recvs, token = outs[:n], outs[n:2 * n], outs[-1]
        for k in range(n):
            _sibling_copy(src_refs[k], land_refs[k], sends[k], recvs[k]).start()
        token[...] = jnp.zeros(token.shape, F32)

    sem = pltpu.SemaphoreType.DMA((1,))
    thru = tuple(pltpu.HBM(a.shape, a.dtype) for a in list(arrs) + lands)
    res = pl.pallas_call(
        body, name=name,
        out_shape=(sem,) * (2 * n) + thru + (jax.ShapeDtypeStruct((SUBLANES, 128), F32),),
        in_specs=[_HBM] * (2 * n) + [pl.BlockSpec(memory_space=pl.ANY)] * len(extra),
        out_specs=(_SEM,) * (2 * n) + (_HBM,) * (2 * n) + (pl.BlockSpec(memory_space=pltpu.VMEM),),
        input_output_aliases={k: 2 * n + k for k in range(2 * n)},
        compiler_params=pltpu.CompilerParams(has_side_effects=_EFFECT),
    )(*[pltpu.with_memory_space_constraint(a, pltpu.HBM) for a in list(arrs) + lands], *extra)
    return res[:n], res[n:2 * n], res[2 * n:3 * n], res[3 * n:4 * n], res[-1]


def _sibling_wait(name, sends, recvs, srcs, lands, after):
    n = len(srcs)

    def body(*refs):
        src_refs, land_refs = refs[:n], refs[n:2 * n]
        sends_, recvs_ = refs[2 * n:3 * n], refs[3 * n:4 * n]
        for k in range(n):
            cp = _sibling_copy(src_refs[k], land_refs[k], sends_[k], recvs_[k])
            cp.wait_send()
            cp.wait_recv()

    thru = tuple(pltpu.HBM(a.shape, a.dtype) for a in list(srcs) + list(lands))
    res = pl.pallas_call(
        body, name=name, out_shape=thru,
        in_specs=[_HBM] * (2 * n) + [_SEM] * (2 * n) + [pl.BlockSpec(memory_space=pl.ANY)],
        out_specs=(_HBM,) * (2 * n),
        input_output_aliases={k: k for k in range(2 * n)},
        compiler_params=pltpu.CompilerParams(has_side_effects=_EFFECT),
    )(*srcs, *lands, *sends, *recvs, after)
    return res[:n], res[n:]


def _landing(own, chip):
    zone = lax.empty((N_CHIPS,) + own.shape, own.dtype)
    return lax.dynamic_update_slice(zone, own[None], (chip,) + (0,) * own.ndim)


def _mod_shard(c_all, w_ada_sh, b_sh):
    d, n = w_ada_sh.shape
    bn = 512

    def body(c_ref, w_ref, b_ref, o_ref):
        cc = c_ref[...]
        ca = cc * _sigmoid(cc)
        o_ref[...] = _dot(ca, w_ref[...]) + b_ref[...]

    return pl.pallas_call(
        body, name="mod_shard", grid=(n // bn,),
        out_shape=jax.ShapeDtypeStruct((N_DEV, n), F32),
        in_specs=[_full((N_DEV, d)), pl.BlockSpec((d, bn), lambda j: (0, j)), pl.BlockSpec((1, bn), lambda j: (0, j))],
        out_specs=pl.BlockSpec((N_DEV, bn), lambda j: (0, j)),
        compiler_params=_cparams(("parallel",)),
    )(c_all, w_ada_sh, b_sh)


def _ssm_prep(lam_re, lam_im, log_step):
    g, p = lam_re.shape

    def body(lr_ref, li_ref, ls_ref, ar_ref, ai_ref, cr_ref, ci_ref):
        lr = jnp.minimum(lr_ref[...], LAMBDA_RE_MAX)
        li = li_ref[...]
        st = jnp.exp(ls_ref[...])
        mag = jnp.exp(lr * st)
        ar = mag * jnp.cos(li * st)
        ai = mag * jnp.sin(li * st)
        den = lr * lr + li * li
        nr = ar - 1.0
        ar_ref[...] = ar
        ai_ref[...] = ai
        cr_ref[...] = (nr * lr + ai * li) / den
        ci_ref[...] = (ai * lr - nr * li) / den

    sds = jax.ShapeDtypeStruct((g, p), F32)
    return pl.pallas_call(body, name="ssm_prep", out_shape=(sds,) * 4)(lam_re, lam_im, log_step)


def _ssm_blocks(bt_re, bt_im, ct_re, ct_im, coef_rows, tile_b, tile_c):
    gh, p = bt_re.shape
    gp, h = ct_re.shape
    nb = SSM_SPLIT
    cb, rb = gp // nb, gp // nb

    def body(btr, bti, ctr, cti, cf, tb_ref, tc_ref, bre_o, bim_o, cre_o, cim_o):
        j = pl.program_id(0)
        row = lax.broadcasted_iota(jnp.int32, (gh, cb), 0)
        col = lax.broadcasted_iota(jnp.int32, (gh, cb), 1) + j * cb
        mask = (row >> 4) == (col >> 6)
        cr, ci = cf[0:1, :], cf[1:2, :]
        br = _split3_dot(btr[...], tb_ref[...])
        bi = _split3_dot(bti[...], tb_ref[...])
        bre_o[...] = jnp.where(mask, br * cr - bi * ci, 0.0).astype(BF16)
        bim_o[...] = jnp.where(mask, br * ci + bi * cr, 0.0).astype(BF16)
        row2 = lax.broadcasted_iota(jnp.int32, (rb, gh), 0) + j * rb
        col2 = lax.broadcasted_iota(jnp.int32, (rb, gh), 1)
        mask2 = (row2 >> 6) == (col2 >> 4)
        cre_o[...] = jnp.where(mask2, _split3_dot(ctr[...], tc_ref[...]), 0.0).astype(BF16)
        cim_o[...] = jnp.where(mask2, _split3_dot(cti[...], tc_ref[...]), 0.0).astype(BF16)

    bspec = pl.BlockSpec((gh, cb), lambda j: (0, j))
    cspec = pl.BlockSpec((rb, gh), lambda j: (j, 0))
    cin = pl.BlockSpec((rb, h), lambda j: (j, 0))
    return pl.pallas_call(
        body, name="ssm_blocks", grid=(nb,),
        out_shape=(jax.ShapeDtypeStruct((gh, gp), BF16),) * 2 + (jax.ShapeDtypeStruct((gp, gh), BF16),) * 2,
        in_specs=[_full((gh, p)), _full((gh, p)), cin, cin, pl.BlockSpec((SUBLANES, cb), lambda j: (0, j)),
                  _full(tile_b.shape), _full(tile_c.shape)],
        out_specs=(bspec, bspec, cspec, cspec),
        compiler_params=_cparams(("parallel",)),
    )(bt_re, bt_im, ct_re, ct_im, coef_rows, tile_b, tile_c)


def _scan_consts(a_ref, reverse):
    w = a_ref.shape[1]
    ar1 = a_ref[0:1, :]
    ai1 = a_ref[1:2, :]
    if reverse:
        ai1 = -ai1
    pr, pi = [ar1], [ai1]
    for _ in range(1, SUBLANES):
        nr = pr[-1] * ar1 - pi[-1] * ai1
        ni = pr[-1] * ai1 + pi[-1] * ar1
        pr.append(nr)
        pi.append(ni)
    row = lax.broadcasted_iota(jnp.int32, (SUBLANES, w), 0)
    dist = (SUBLANES - 1 - row) if reverse else row

    def pick(vals):
        out = jnp.broadcast_to(vals[SUBLANES - 1], (SUBLANES, w))
        for r in range(SUBLANES - 1):
            out = jnp.where(dist == r, vals[r], out)
        return out

    p_r, p_i = pick(pr), pick(pi)
    steps = []
    for k in (1, 2, 4):
        steps.append((k, jnp.where(dist >= k, pr[k - 1], 0.0), jnp.where(dist >= k, pi[k - 1], 0.0)))
    a8 = (jnp.broadcast_to(pr[SUBLANES - 1], (SUBLANES, w)), jnp.broadcast_to(pi[SUBLANES - 1], (SUBLANES, w)))
    return row, p_r, p_i, steps, a8


def _scan_tile(xr, xi, cr, ci, consts, reverse):
    row, p_r, p_i, steps, (a8r, a8i) = consts
    for k, s_r, s_i in steps:
        sh = (SUBLANES - k) if reverse else k
        qr = pltpu.roll(xr, sh, 0)
        qi = pltpu.roll(xi, sh, 0)
        xr, xi = xr + s_r * qr - s_i * qi, xi + s_r * qi + s_i * qr
    outr = xr + p_r * cr - p_i * ci
    outi = xi + p_r * ci + p_i * cr
    e = 0 if reverse else SUBLANES - 1
    er = jnp.broadcast_to(xr[e:e + 1, :], xr.shape)
    ei = jnp.broadcast_to(xi[e:e + 1, :], xi.shape)
    return outr, outi, er + a8r * cr - a8i * ci, ei + a8r * ci + a8i * cr


def _scan_fwd(a_rows, bu_re, bu_im):
    t, n = bu_re.shape
    tb, w = _blk(t, TB_SCAN), W_SCAN
    ntile = tb // SUBLANES

    def body(a_ref, br_ref, bi_ref, sr_ref, si_ref, car, cai):
        @pl.when(pl.program_id(1) == 0)
        def _():
            car[...] = jnp.zeros(car.shape, F32)
            cai[...] = jnp.zeros(cai.shape, F32)
        consts = _scan_consts(a_ref, False)

        def pair(i, carry):
            o = pl.multiple_of(i * BF16_ROWS, BF16_ROWS)
            outs = []
            for h in range(2):
                rows = pl.ds(o + h * SUBLANES, SUBLANES)
                outr, outi, ncr, nci = _scan_tile(br_ref[rows, :], bi_ref[rows, :], carry[0], carry[1], consts, False)
                outs.append((outr, outi))
                carry = (ncr, nci)
            sr_ref[pl.ds(o, BF16_ROWS), :] = jnp.concatenate([outs[0][0], outs[1][0]], axis=0).astype(BF16)
            si_ref[pl.ds(o, BF16_ROWS), :] = jnp.concatenate([outs[0][1], outs[1][1]], axis=0).astype(BF16)
            return carry

        def pairs(i, carry):
            for s in range(SCAN_UNROLL // 2):
                carry = pair(i * (SCAN_UNROLL // 2) + s, carry)
            return carry

        cr, ci = lax.fori_loop(0, ntile // SCAN_UNROLL, pairs, (car[...], cai[...]))
        car[...] = cr
        cai[...] = ci

    spec = pl.BlockSpec((tb, w), lambda s, k: (k, s))
    sds = jax.ShapeDtypeStruct((t, n), BF16)
    return pl.pallas_call(
        body, name="scan_fwd", grid=(n // w, t // tb), out_shape=(sds, sds),
        in_specs=[pl.BlockSpec((SUBLANES, w), lambda s, k: (0, s)), spec, spec], out_specs=(spec, spec),
        scratch_shapes=[pltpu.VMEM((SUBLANES, w), F32), pltpu.VMEM((SUBLANES, w), F32)],
        compiler_params=_cparams(("parallel", "arbitrary"), VMEM_MID),
    )(a_rows, bu_re, bu_im)


def _scan_bwd(a_rows, g_re, g_im, s_re, s_im):
    t, n = g_re.shape
    tb, w = _blk(t, TB_SCAN), W_SCAN
    ntile = tb // SUBLANES
    npair = tb // BF16_ROWS
    nt = t // tb

    def body(a_ref, gr_ref, gi_ref, sr_ref, si_ref, or_ref, oi_ref, gar_ref, gai_ref, car, cai):
        @pl.when(pl.program_id(1) == 0)
        def _():
            car[...] = jnp.zeros(car.shape, F32)
            cai[...] = jnp.zeros(cai.shape, F32)
            gar_ref[...] = jnp.zeros(gar_ref.shape, F32)
            gai_ref[...] = jnp.zeros(gai_ref.shape, F32)
        consts = _scan_consts(a_ref, True)
        row = consts[0]

        def pair(i, carry):
            cr, ci, accr, acci = carry
            o = pl.multiple_of((npair - 1 - i) * BF16_ROWS, BF16_ROWS)
            s_r = sr_ref[pl.ds(o, BF16_ROWS), :].astype(F32)
            s_i = si_ref[pl.ds(o, BF16_ROWS), :].astype(F32)
            outs = [None, None]
            for h in (1, 0):
                rows = pl.ds(o + h * SUBLANES, SUBLANES)
                outr, outi, ncr, nci = _scan_tile(gr_ref[rows, :], gi_ref[rows, :], cr, ci, consts, True)
                outs[h] = (outr, outi)
                gnr = jnp.where(row == SUBLANES - 1, cr, pltpu.roll(outr, SUBLANES - 1, 0))
                gni = jnp.where(row == SUBLANES - 1, ci, pltpu.roll(outi, SUBLANES - 1, 0))
                sr = s_r[h * SUBLANES:(h + 1) * SUBLANES, :]
                si = s_i[h * SUBLANES:(h + 1) * SUBLANES, :]
                accr, acci = accr + sr * gnr + si * gni, acci + sr * gni - si * gnr
                cr, ci = ncr, nci
            or_ref[pl.ds(o, BF16_ROWS), :] = jnp.concatenate([outs[0][0], outs[1][0]], axis=0).astype(BF16)
            oi_ref[pl.ds(o, BF16_ROWS), :] = jnp.concatenate([outs[0][1], outs[1][1]], axis=0).astype(BF16)
            return cr, ci, accr, acci

        def pairs(i, carry):
            for s in range(SCAN_UNROLL // 2):
                carry = pair(i * (SCAN_UNROLL // 2) + s, carry)
            return carry

        cr, ci, accr, acci = lax.fori_loop(0, ntile // SCAN_UNROLL, pairs,
                                           (car[...], cai[...], gar_ref[...], gai_ref[...]))
        car[...] = cr
        cai[...] = ci
        gar_ref[...] = accr
        gai_ref[...] = acci

    spec = pl.BlockSpec((tb, w), lambda s, k: (nt - 1 - k, s))
    aspec = pl.BlockSpec((SUBLANES, w), lambda s, k: (0, s))
    sds = jax.ShapeDtypeStruct((t, n), BF16)
    asds = jax.ShapeDtypeStruct((SUBLANES, n), F32)
    return pl.pallas_call(
        body, name="scan_bwd", grid=(n // w, nt), out_shape=(sds, sds, asds, asds),
        in_specs=[aspec, spec, spec, spec, spec], out_specs=(spec, spec, aspec, aspec),
        scratch_shapes=[pltpu.VMEM((SUBLANES, w), F32), pltpu.VMEM((SUBLANES, w), F32)],
        compiler_params=_cparams(("parallel", "arbitrary"), VMEM_MID),
    )(a_rows, g_re, g_im, s_re, s_im)


def _mix_in(x, vec, w_in_st, b_re, b_im):
    t, d = x.shape
    ns, _, nc = w_in_st.shape
    dssm, nstate = b_re.shape
    du, ds = dssm // SSM_SPLIT, nstate // SSM_SPLIT
    tb = _blk(t, TB_MIX)

    def body(x_ref, vec_ref, w_ref, bre_ref, bim_ref, proj_ref, bur_ref, bui_ref, h1_ref):
        xv = x_ref[...]
        r = lax.rsqrt(_rowmean(xv * xv) + EPS)
        h = xv * r * vec_ref[0:1, :] * vec_ref[1:2, :] + vec_ref[2:3, :]
        hb = h.astype(BF16)
        h1_ref[...] = hb
        u = None
        for j in range(ns):
            pj = jnp.dot(hb, w_ref[j], preferred_element_type=F32)
            proj_ref[:, j * nc:(j + 1) * nc] = pj
            if j == 0:
                u = pj
        ub = u.astype(BF16)
        for q in range(SSM_SPLIT):
            rq, cq = slice(q * du, (q + 1) * du), slice(q * ds, (q + 1) * ds)
            bur_ref[:, cq] = jnp.dot(ub[:, rq], bre_ref[rq, cq], preferred_element_type=F32)
            bui_ref[:, cq] = jnp.dot(ub[:, rq], bim_ref[rq, cq], preferred_element_type=F32)

    return pl.pallas_call(
        body, name="mix_in", grid=(t // tb,),
        out_shape=(jax.ShapeDtypeStruct((t, ns * nc), F32), jax.ShapeDtypeStruct((t, nstate), F32),
                   jax.ShapeDtypeStruct((t, nstate), F32), jax.ShapeDtypeStruct((t, d), BF16)),
        in_specs=[_rows(tb, d), _full((SUBLANES, d)), _resident(w_in_st.shape), _resident(b_re.shape),
                  _resident(b_im.shape)],
        out_specs=(_rows(tb, ns * nc), _rows(tb, nstate), _rows(tb, nstate), _rows(tb, d)),
        compiler_params=_cparams(("parallel",), VMEM_BIG),
    )(x, vec, w_in_st, b_re, b_im)


def _head_ms(y, h_ref):
    return _split_dot(y * y, h_ref[...])


def _conv3(x, halo, w_ref):
    return w_ref[0:1, :] * _shift_down(x, halo, 2) + w_ref[1:2, :] * _shift_down(x, halo, 1) + w_ref[2:3, :] * x


def _mix_out(x, proj, s_re, s_im, c_re, c_im, v512, convw, glu_w, h16, h64, w_out, vd):
    t, d = x.shape
    dh = c_re.shape[1]
    nstate = s_re.shape[1]
    du, ds = dh // SSM_SPLIT, nstate // SSM_SPLIT
    tb = _blk(t, TB_MIX)

    def body(x_ref, u_ref, bg_ref, cg_ref, v_ref, cgh_ref, vh_ref, sr_ref, si_ref, cre_ref, cim_ref, p_ref,
             cw_ref, gw_ref, h16_ref, h64_ref, wo_ref, vd_ref, y1_ref, o_ref, x2_ref):
        i = pl.program_id(0)
        u = u_ref[...]
        ys = []
        for q in range(SSM_SPLIT):
            rq, cq = slice(q * ds, (q + 1) * ds), slice(q * du, (q + 1) * du)
            ys.append(_dot(sr_ref[:, rq], cre_ref[rq, cq]) - _dot(si_ref[:, rq], cim_ref[rq, cq]))
        ys = jnp.concatenate(ys, axis=1)
        y1 = ys + p_ref[0:1, :] * u
        y1_ref[...] = y1
        z = _gelu(y1)
        q = _dot(z, gw_ref[...]) + p_ref[1:2, :]
        ya = z * _sigmoid(q)
        na = ya * lax.rsqrt(_head_ms(ya, h16_ref) + EPS) * p_ref[2:3, :]
        cv = cg_ref[...] * v_ref[...]
        cvh = jnp.where(i > 0, cgh_ref[...] * vh_ref[...], 0.0)
        yb = bg_ref[...] * _conv3(cv, cvh, cw_ref)
        nb = yb * lax.rsqrt(_head_ms(yb, h64_ref) + EPS) * p_ref[3:4, :]
        o = _dot(na, wo_ref[0:dh, :]) + _dot(nb, wo_ref[dh:2 * dh, :])
        o_ref[...] = o
        on = o * lax.rsqrt(_rowmean(o * o) + EPS) * vd_ref[0:1, :]
        x2_ref[...] = x_ref[...] + vd_ref[1:2, :] * on

    return pl.pallas_call(
        body, name="mix_out", grid=(t // tb,),
        out_shape=(jax.ShapeDtypeStruct((t, dh), F32), jax.ShapeDtypeStruct((t, d), F32),
                   jax.ShapeDtypeStruct((t, d), F32)),
        in_specs=[_rows(tb, d), _rows(tb, dh, 0), _rows(tb, dh, 1), _rows(tb, dh, 2), _rows(tb, dh, 3),
                  _halo_prev(tb, dh, 2), _halo_prev(tb, dh, 3), _rows(tb, nstate), _rows(tb, nstate),
                  _full(c_re.shape), _full(c_im.shape), _full(v512.shape), _full(convw.shape), _full(glu_w.shape),
                  _full(h16.shape), _full(h64.shape), _full(w_out.shape), _full(vd.shape)],
        out_specs=(_rows(tb, dh), _rows(tb, d), _rows(tb, d)),
        compiler_params=_cparams(("parallel",), VMEM_BIG),
    )(x, proj, proj, proj, proj, proj, proj, s_re, s_im, c_re, c_im, v512, convw, glu_w, h16, h64, w_out, vd)


def _ffn_up(x2, vec, w_up_st):
    t, d = x2.shape
    ns, _, nc = w_up_st.shape
    tb = _blk(t, TB_FFN_UP)

    def body(x_ref, vec_ref, w_ref, up_ref, h2_ref):
        xv = x_ref[...]
        r = lax.rsqrt(_rowmean(xv * xv) + EPS)
        h = xv * r * vec_ref[0:1, :] * vec_ref[1:2, :] + vec_ref[2:3, :]
        hb = h.astype(BF16)
        h2_ref[...] = hb
        for j in range(ns):
            up_ref[:, j * nc:(j + 1) * nc] = jnp.dot(hb, w_ref[j], preferred_element_type=F32)

    return pl.pallas_call(
        body, name="ffn_up", grid=(t // tb,),
        out_shape=(jax.ShapeDtypeStruct((t, ns * nc), F32), jax.ShapeDtypeStruct((t, d), BF16)),
        in_specs=[_rows(tb, d), _full((SUBLANES, d)), _resident(w_up_st.shape)],
        out_specs=(_rows(tb, ns * nc), _rows(tb, d)),
        compiler_params=_cparams(("parallel",), VMEM_BIG),
    )(x2, vec, w_up_st)


def _ffn_down(up, fw, w_down, w_down_t, x2, tgt, vd):
    t, nh = up.shape
    dff, d = w_down.shape
    tb = _blk(t, TB_FFN)
    inv_d = 1.0 / d

    def body(up_ref, uph_ref, fw_ref, wd_ref, wdt_ref, x2_ref, tgt_ref, vd_ref,
             act_ref, ddn_ref, dout_ref, dhid_ref, vec_ref, loss_ref, a_s, vv_s, sg_s):
        i = pl.program_id(0)

        def conv_cols(sl):
            x = up_ref[:, sl]
            halo = jnp.where(i > 0, uph_ref[:, sl], 0.0)
            return (fw_ref[0:1, sl] * _shift_down(x, halo, 2) + fw_ref[1:2, sl] * _shift_down(x, halo, 1)
                    + fw_ref[2:3, sl] * x)

        dn = None
        for o in range(0, dff, CW_FFN):
            sl = slice(o, o + CW_FFN)
            a = conv_cols(sl)
            vv = conv_cols(slice(dff + o, dff + o + CW_FFN))
            sg = _sigmoid(a)
            si = a * sg
            a_s[:, sl] = si
            vv_s[:, sl] = vv
            sg_s[:, sl] = sg
            actb = (si * vv).astype(BF16)
            act_ref[:, sl] = actb
            pj = lax.dot_general(actb, wdt_ref[:, sl], (((1,), (1,)), ((), ())), preferred_element_type=F32)
            dn = pj if dn is None else dn + pj
        r3 = lax.rsqrt(_rowmean(dn * dn) + EPS)
        xn = dn * r3
        g = vd_ref[0:1, :]
        gt2 = vd_ref[1:2, :]
        dnn = xn * g
        diff = x2_ref[...] + gt2 * dnn - tgt_ref[...]
        part = 0.5 * inv_d * jnp.sum(diff * diff)

        @pl.when(i == 0)
        def _():
            loss_ref[...] = jnp.zeros(loss_ref.shape, F32)
        loss_ref[...] += part
        dout = diff * inv_d
        dout_ref[...] = dout
        ddnn = dout * gt2
        _acc_rows(vec_ref, i == 0, [_colsum(dout * dnn), _colsum(ddnn * xn)])
        dxn = ddnn * g
        ddn = r3 * (dxn - xn * _rowmean(dxn * xn))
        ddnb = ddn.astype(BF16)
        ddn_ref[...] = ddnb
        for o in range(0, dff, CW_FFN):
            sl = slice(o, o + CW_FFN)
            dact = lax.dot_general(ddnb, wd_ref[sl, :], (((1,), (1,)), ((), ())), preferred_element_type=F32)
            si, vv, sg = a_s[:, sl], vv_s[:, sl], sg_s[:, sl]
            dhid_ref[:, sl] = (dact * vv * (sg + si * (1.0 - sg))).astype(BF16)
            dhid_ref[:, dff + o:dff + o + CW_FFN] = (dact * si).astype(BF16)

    return pl.pallas_call(
        body, name="ffn_down", grid=(t // tb,),
        scratch_shapes=[pltpu.VMEM((tb, dff), F32)] * 3,
        out_shape=(jax.ShapeDtypeStruct((t, dff), BF16), jax.ShapeDtypeStruct((t, d), BF16),
                   jax.ShapeDtypeStruct((t, d), F32), jax.ShapeDtypeStruct((t, nh), BF16),
                   jax.ShapeDtypeStruct((SUBLANES, d), F32), jax.ShapeDtypeStruct((SUBLANES, 128), F32)),
        in_specs=[_rows(tb, nh), _halo_prev(tb, nh), _full(fw.shape), _resident(w_down.shape),
                  _resident(w_down_t.shape), _rows(tb, d),
                  _rows(tb, d), _full(vd.shape)],
        out_specs=(_rows(tb, dff), _rows(tb, d), _rows(tb, d), _rows(tb, nh), _full((SUBLANES, d)),
                   _full((SUBLANES, 128))),
        compiler_params=_cparams(("arbitrary",), VMEM_BIG),
    )(up, up, fw, w_down, w_down_t, x2, tgt, vd)


def _ffn_up_bwd(dhid, up, fw, x2, dout, vec, w_up_st):
    t, nh = dhid.shape
    d = x2.shape[1]
    ns, _, nc = w_up_st.shape
    tb = _blk(t, TB_FFN)
    nblk = t // tb
    cw = 128

    def body(dh_ref, dhn_ref, up_ref, fw_ref, x2_ref, dout_ref, vec_ref, w_ref,
             dx2_ref, dup_ref, vp_ref, df_ref):
        i = pl.program_id(0)

        @pl.when(i == 0)
        def _():
            df_ref[...] = jnp.zeros(df_ref.shape, F32)
        dh2 = None
        for j in range(ns):
            for o in range(j * nc, (j + 1) * nc, cw):
                sl = slice(o, o + cw)
                dh = dh_ref[:, sl].astype(F32)
                dhn = jnp.where(i < nblk - 1, dhn_ref[:, sl].astype(F32), 0.0)
                dh1 = _shift_up(dh, dhn, 1)
                dh2s = _shift_up(dh, dhn, 2)
                dup_ref[:, sl] = (fw_ref[2:3, sl] * dh + fw_ref[1:2, sl] * dh1 + fw_ref[0:1, sl] * dh2s).astype(BF16)
                up_v = up_ref[:, sl]
                df_ref[0:1, sl] += _colsum(dh2s * up_v)
                df_ref[1:2, sl] += _colsum(dh1 * up_v)
                df_ref[2:3, sl] += _colsum(dh * up_v)
            pj = lax.dot_general(dup_ref[:, j * nc:(j + 1) * nc], w_ref[j], (((1,), (1,)), ((), ())),
                                 preferred_element_type=F32)
            dh2 = pj if dh2 is None else dh2 + pj
        xv = x2_ref[...]
        r = lax.rsqrt(_rowmean(xv * xv) + EPS)
        xn = xv * r
        g = vec_ref[0:1, :]
        hg = xn * g
        dhg = dh2 * vec_ref[1:2, :]
        _acc_rows(vp_ref, i == 0, [_colsum(dh2), _colsum(dh2 * hg), _colsum(dhg * xn)])
        dxn = dhg * g
        dx2_ref[...] = dout_ref[...] + r * (dxn - xn * _rowmean(dxn * xn))

    return pl.pallas_call(
        body, name="ffn_up_bwd", grid=(nblk,),
        out_shape=(jax.ShapeDtypeStruct((t, d), F32), jax.ShapeDtypeStruct((t, nh), BF16),
                   jax.ShapeDtypeStruct((SUBLANES, d), F32), jax.ShapeDtypeStruct((SUBLANES, nh), F32)),
        in_specs=[_rows(tb, nh), _halo_next(tb, nh, t, rows=BF16_ROWS), _rows(tb, nh), _full(fw.shape),
                  _rows(tb, d), _rows(tb, d), _full(vec.shape), _resident(w_up_st.shape)],
        out_specs=(_rows(tb, d), _rows(tb, nh), _full((SUBLANES, d)), _full((SUBLANES, nh))),
        compiler_params=_cparams(("arbitrary",), VMEM_BIG),
    )(dhid, dhid, up, fw, x2, dout, vec, w_up_st)


def _mix_out_bwd(dx2, o, y1, proj, s_re, s_im, c_re, c_im, v512, convw, glu_w, h16, h64, w_out, vd):
    t, d = dx2.shape
    dh = y1.shape[1]
    nstate = c_re.shape[0]
    du, ds = dh // SSM_SPLIT, nstate // SSM_SPLIT
    tb = _blk(t, TB_MIX)

    def body(dx2_ref, o_ref, y1_ref, u_ref, bg_ref, cg_ref, v_ref, cgh_ref, vh_ref, cre_ref, cim_ref, p_ref,
             cw_ref, gw_ref, h16_ref, h64_ref, wo_ref, vd_ref, sr_ref, si_ref,
             do_ref, ycat_ref, z_ref, dq_ref, dy1_ref, gr_ref, gi_ref, dcc_ref, dbg_ref, vpd_ref, vp5_ref,
             dcr_ref, dci_ref):
        i = pl.program_id(0)
        first = i == 0

        @pl.when(first)
        def _():
            dcr_ref[...] = jnp.zeros(dcr_ref.shape, F32)
            dci_ref[...] = jnp.zeros(dci_ref.shape, F32)
        ov = o_ref[...]
        ro = lax.rsqrt(_rowmean(ov * ov) + EPS)
        on_ = ov * ro
        g = vd_ref[0:1, :]
        dx2v = dx2_ref[...]
        don = dx2v * vd_ref[1:2, :]
        _acc_rows(vpd_ref, first, [_colsum(dx2v * on_ * g), _colsum(don * on_)])
        dxn = don * g
        dob = (ro * (dxn - on_ * _rowmean(dxn * on_))).astype(BF16)
        do_ref[...] = dob
        dyc_a =lax.dot_general(dob, wo_ref[0:dh, :], (((1,), (1,)), ((), ())), preferred_element_type=F32)
        dyc_b = lax.dot_general(dob, wo_ref[dh:2 * dh, :], (((1,), (1,)), ((), ())), preferred_element_type=F32)
        y1v = y1_ref[...]
        u = u_ref[...]
        z, dz_dy1 = _gelu_and_grad(y1v)
        zb = z.astype(BF16)
        sg = _sigmoid(jnp.dot(zb, gw_ref[...], preferred_element_type=F32) + p_ref[1:2, :])
        ya = z * sg
        ra = lax.rsqrt(_head_ms(ya, h16_ref) + EPS)
        yan = ya * ra
        ga = p_ref[2:3, :]
        ycat_ref[:, 0:dh] = (yan * ga).astype(BF16)
        dyn = dyc_a * ga
        dya = ra * (dyn - yan * _split_dot(dyn * yan, h16_ref[...]))
        dq = dya * z * sg * (1.0 - sg)
        dqb = dq.astype(BF16)
        z_ref[...] = zb
        dq_ref[...] = dqb
        dz = dya * sg + lax.dot_general(dqb, gw_ref[...], (((1,), (1,)), ((), ())), preferred_element_type=F32)
        dy1 = dz * dz_dy1
        dy1_ref[...] = dy1
        dy1b = dy1.astype(BF16)
        for q in range(SSM_SPLIT):
            rq, cq = slice(q * ds, (q + 1) * ds), slice(q * du, (q + 1) * du)
            gr_ref[:, rq] = lax.dot_general(dy1b[:, cq], cre_ref[rq, cq], (((1,), (1,)), ((), ())),
                                            preferred_element_type=F32)
            gi_ref[:, rq] = -lax.dot_general(dy1b[:, cq], cim_ref[rq, cq], (((1,), (1,)), ((), ())),
                                             preferred_element_type=F32)
            dcr_ref[rq, :] += _dot_tn(sr_ref[:, rq], dy1b[:, cq])
            dci_ref[rq, :] += _dot_tn(si_ref[:, rq], dy1b[:, cq])
        bg = bg_ref[...]
        cv = cg_ref[...] * v_ref[...]
        cvh = jnp.where(i > 0, cgh_ref[...] * vh_ref[...], 0.0)
        cv1 = _shift_down(cv, cvh, 1)
        cv2 = _shift_down(cv, cvh, 2)
        cc = cw_ref[0:1, :] * cv2 + cw_ref[1:2, :] * cv1 + cw_ref[2:3, :] * cv
        yb = bg * cc
        rb = lax.rsqrt(_head_ms(yb, h64_ref) + EPS)
        ybn = yb * rb
        gb = p_ref[3:4, :]
        ycat_ref[:, dh:2 * dh] = (ybn * gb).astype(BF16)
        dynb = dyc_b * gb
        dyb = rb * (dynb - ybn * _split_dot(dynb * ybn, h64_ref[...]))
        dcc = dyb * bg
        dbg_ref[...] = dyb * cc
        dcc_ref[...] = dcc
        _acc_rows(vp5_ref, first, [_colsum(dyc_a * yan), _colsum(dyc_b * ybn), _colsum(dq), _colsum(dy1 * u),
                                   _colsum(dcc * cv2), _colsum(dcc * cv1), _colsum(dcc * cv)])

    return pl.pallas_call(
        body, name="mix_out_bwd", grid=(t // tb,),
        out_shape=(jax.ShapeDtypeStruct((t, d), BF16), jax.ShapeDtypeStruct((t, 2 * dh), BF16),
                   jax.ShapeDtypeStruct((t, dh), BF16), jax.ShapeDtypeStruct((t, dh), BF16),
                   jax.ShapeDtypeStruct((t, dh), F32), jax.ShapeDtypeStruct((t, nstate), F32),
                   jax.ShapeDtypeStruct((t, nstate), F32), jax.ShapeDtypeStruct((t, dh), F32),
                   jax.ShapeDtypeStruct((t, dh), F32), jax.ShapeDtypeStruct((SUBLANES, d), F32),
                   jax.ShapeDtypeStruct((SUBLANES, dh), F32), jax.ShapeDtypeStruct((nstate, du), F32),
                   jax.ShapeDtypeStruct((nstate, du), F32)),
        in_specs=[_rows(tb, d), _rows(tb, d), _rows(tb, dh), _rows(tb, dh, 0), _rows(tb, dh, 1), _rows(tb, dh, 2),
                  _rows(tb, dh, 3), _halo_prev(tb, dh, 2), _halo_prev(tb, dh, 3), _resident(c_re.shape),
                  _resident(c_im.shape), _full(v512.shape), _full(convw.shape), _resident(glu_w.shape),
                  _resident(h16.shape), _resident(h64.shape), _resident(w_out.shape), _full(vd.shape),
                  _rows(tb, nstate), _rows(tb, nstate)],
        out_specs=(_rows(tb, d), _rows(tb, 2 * dh), _rows(tb, dh), _rows(tb, dh), _rows(tb, dh), _rows(tb, nstate),
                   _rows(tb, nstate), _rows(tb, dh), _rows(tb, dh), _full((SUBLANES, d)), _full((SUBLANES, dh)),
                   _full((nstate, du)), _full((nstate, du))),
        compiler_params=_cparams(("arbitrary",), VMEM_BIG),
    )(dx2, o, y1, proj, proj, proj, proj, proj, proj, c_re, c_im, v512, convw, glu_w, h16, h64, w_out, vd,
      s_re, s_im)


def _mix_in_bwd(gt_re, gt_im, b_re, b_im, dy1, dcc, dbg, proj, x, dx2, vec, v512, convw, w_in_st):
    t, d = x.shape
    dh = dy1.shape[1]
    nstate = gt_re.shape[1]
    du_w, ds = dh // SSM_SPLIT, nstate // SSM_SPLIT
    ns, _, nc = w_in_st.shape
    tb = _blk(t, TB_MIX)
    nblk = t // tb

    def body(gr_ref, gi_ref, bre_ref, bim_ref, dy1_ref, dcc_ref, dccn_ref, dbg_ref, u_ref, cg_ref, v_ref, x_ref,
             dx2_ref, vec_ref, p_ref, cw_ref, w_ref, gx_ref, dproj_ref, vp_ref, dbr_ref, dbi_ref):
        i = pl.program_id(0)

        @pl.when(i == 0)
        def _():
            dbr_ref[...] = jnp.zeros(dbr_ref.shape, F32)
            dbi_ref[...] = jnp.zeros(dbi_ref.shape, F32)
        ub = u_ref[...].astype(BF16)
        du = []
        for q in range(SSM_SPLIT):
            rq, cq = slice(q * du_w, (q + 1) * du_w), slice(q * ds, (q + 1) * ds)
            du.append(lax.dot_general(gr_ref[:, cq].astype(BF16), bre_ref[rq, cq], (((1,), (1,)), ((), ())),
                                      preferred_element_type=F32)
                      + lax.dot_general(gi_ref[:, cq].astype(BF16), bim_ref[rq, cq], (((1,), (1,)), ((), ())),
                                        preferred_element_type=F32))
            dbr_ref[rq, :] += _dot_tn(ub[:, rq], gr_ref[:, cq])
            dbi_ref[rq, :] += _dot_tn(ub[:, rq], gi_ref[:, cq])
        du = dy1_ref[...] * p_ref[0:1, :] + jnp.concatenate(du, axis=1)
        dcc = dcc_ref[...]
        dccn = jnp.where(i < nblk - 1, dccn_ref[...], 0.0)
        dcv = (cw_ref[2:3, :] * dcc + cw_ref[1:2, :] * _shift_up(dcc, dccn, 1)
               + cw_ref[0:1, :] * _shift_up(dcc, dccn, 2))
        parts = [du, dbg_ref[...], dcv * v_ref[...], dcv * cg_ref[...]]
        xv = x_ref[...]
        r = lax.rsqrt(_rowmean(xv * xv) + EPS)
        xn = xv * r
        g = vec_ref[0:1, :]
        hg = xn * g
        dh1 = None
        for j in range(ns):
            pb = parts[j].astype(BF16)
            dproj_ref[:, j * nc:(j + 1) * nc] = pb
            pj =lax.dot_general(pb, w_ref[j], (((1,), (1,)), ((), ())), preferred_element_type=F32)
            dh1 = pj if dh1 is None else dh1 + pj
        dhg = dh1 * vec_ref[1:2, :]
        _acc_rows(vp_ref, i == 0, [_colsum(dh1), _colsum(dh1 * hg), _colsum(dhg * xn)])
        dxn = dhg * g
        gx_ref[...] = dx2_ref[...] + r * (dxn - xn * _rowmean(dxn * xn))

    assert nc == dh and ns == 4
    return pl.pallas_call(
        body, name="mix_in_bwd", grid=(nblk,),
        out_shape=(jax.ShapeDtypeStruct((t, d), F32), jax.ShapeDtypeStruct((t, ns * nc), BF16),
                   jax.ShapeDtypeStruct((SUBLANES, d), F32), jax.ShapeDtypeStruct((dh, ds), F32),
                   jax.ShapeDtypeStruct((dh, ds), F32)),
        in_specs=[_rows(tb, nstate), _rows(tb, nstate), _resident(b_re.shape), _resident(b_im.shape), _rows(tb, dh),
                  _rows(tb, dh), _halo_next(tb, dh, t), _rows(tb, dh), _rows(tb, dh, 0), _rows(tb, dh, 2),
                  _rows(tb, dh, 3), _rows(tb, d), _rows(tb, d), _full(vec.shape), _full(v512.shape),
                  _full(convw.shape), _resident(w_in_st.shape)],
        out_specs=(_rows(tb, d), _rows(tb, ns * nc), _full((SUBLANES, d)), _full((dh, ds)), _full((dh, ds))),
        compiler_params=_cparams(("arbitrary",), VMEM_BIG),
    )(gt_re, gt_im, b_re, b_im, dy1, dcc, dcc, dbg, proj, proj, proj, x, dx2, vec, v512, convw, w_in_st)


def _matmul_tn(a, b, m, bn, out_dtype, name, diag=False, bt=TB_TN, after=None):
    t = a.shape[0]
    n = b.shape[1]
    bt = _blk(t, bt)
    nk = t // bt
    extra = [] if after is None else [after]
    a_map = (lambda j, k: (k, j)) if diag else (lambda j, k: (k, 0))

    def body(a_ref, b_ref, *rest):
        o_ref, acc_ref = rest[-2:]
        k = pl.program_id(1)

        @pl.when(k == 0)
        def _():
            acc_ref[...] = jnp.zeros(acc_ref.shape, F32)
        acc_ref[...] += _dot_tn(a_ref[...], b_ref[...])

        @pl.when(k == nk - 1)
        def _():
            o_ref[...] = acc_ref[...].astype(out_dtype)

    return pl.pallas_call(
        body, name=name, grid=(n // bn, nk),
        out_shape=jax.ShapeDtypeStruct((n // bn, m, bn), out_dtype),
        in_specs=[pl.BlockSpec((bt, m), a_map), pl.BlockSpec((bt, bn), lambda j, k: (k, j))]
        + [pl.BlockSpec(memory_space=pl.ANY)] * len(extra),
        out_specs=pl.BlockSpec((None, m, bn), lambda j, k: (j, 0, 0)),
        scratch_shapes=[pltpu.VMEM((m, bn), F32)],
        compiler_params=_cparams(("parallel", "arbitrary"), VMEM_BIG),
    )(a, b, *extra)


def _ssm_bgrad(d_bre, d_bim, bt_re, bt_im, rows_in, fold, tile_b):
    gh, cb = d_bre.shape
    nb = SSM_SPLIT
    rb = gh // nb
    gp = nb * cb
    p = fold.shape[1]

    def body(dr_ref, di_ref, br_ref, bi_ref, rin_ref, f_ref, tb_ref, dbr_ref, dbi_ref, rout_ref):
        row = lax.broadcasted_iota(jnp.int32, (rb, cb), 0)
        col = lax.broadcasted_iota(jnp.int32, (rb, cb), 1)
        mask = (row >> 4) == (col >> 6)
        gr = jnp.where(mask, dr_ref[...], 0.0)
        gi = jnp.where(mask, di_ref[...], 0.0)
        cr, ci = rin_ref[0:1, :], rin_ref[1:2, :]
        dbr_ref[...] = _split3_dot(cr * gr + ci * gi, f_ref[...])
        dbi_ref[...] = _split3_dot(cr * gi - ci * gr, f_ref[...])
        br = _split3_dot(br_ref[...], tb_ref[...])
        bi = _split3_dot(bi_ref[...], tb_ref[...])
        rout_ref[...] = jnp.zeros(rout_ref.shape, F32)
        rout_ref[0:1, :] = _colsum(br * gr + bi * gi)
        rout_ref[1:2, :] = _colsum(br * gi - bi * gr)

    dspec = pl.BlockSpec((rb, cb), lambda j: (j, 0))
    rspec = pl.BlockSpec((SUBLANES, cb), lambda j: (0, j))
    ospec = pl.BlockSpec((rb, p), lambda j: (j, 0))
    return pl.pallas_call(
        body, name="ssm_bgrad", grid=(nb,),
        out_shape=(jax.ShapeDtypeStruct((gh, p), F32), jax.ShapeDtypeStruct((gh, p), F32),
                   jax.ShapeDtypeStruct((SUBLANES, gp), F32)),
        in_specs=[dspec, dspec, ospec, ospec, rspec, _full(fold.shape), _full(tile_b.shape)],
        out_specs=(ospec, ospec, rspec),
        compiler_params=_cparams(("parallel",)),
    )(d_bre, d_bim, bt_re, bt_im, rows_in, fold, tile_b)


def _ssm_cgrad(d_cre, d_cim, fold):
    gp, cb = d_cre.shape
    nb = SSM_SPLIT
    rb = gp // nb
    h = fold.shape[1]

    def body(dr_ref, di_ref, f_ref, cr_ref, ci_ref):
        row = lax.broadcasted_iota(jnp.int32, (rb, cb), 0)
        col = lax.broadcasted_iota(jnp.int32, (rb, cb), 1)
        mask = (row >> 6) == (col >> 4)
        cr_ref[...] = _split3_dot(jnp.where(mask, dr_ref[...], 0.0), f_ref[...])
        ci_ref[...] = -_split3_dot(jnp.where(mask, di_ref[...], 0.0), f_ref[...])

    cspec = pl.BlockSpec((rb, cb), lambda j: (j, 0))
    ospec = pl.BlockSpec((rb, h), lambda j: (j, 0))
    return pl.pallas_call(
        body, name="ssm_cgrad", grid=(nb,),
        out_shape=(jax.ShapeDtypeStruct((gp, h), F32),) * 2,
        in_specs=[cspec, cspec, _full(fold.shape)], out_specs=(ospec, ospec),
        compiler_params=_cparams(("parallel",)),
    )(d_cre, d_cim, fold)


def _ssm_lamgrad(lam_re, lam_im, log_step, abar_re, abar_im, coef_re, coef_im, gc_re, gc_im, ga_re, ga_im):
    g, p = lam_re.shape

    def body(lr_ref, li_ref, ls_ref, ar_ref, ai_ref, cr_ref, ci_ref, gcr_ref, gci_ref, gar_ref, gai_ref,
             dlr_ref, dli_ref, dls_ref):
        lam_raw = lr_ref[...]
        lr = jnp.minimum(lam_raw, LAMBDA_RE_MAX)
        li = li_ref[...]
        st = jnp.exp(ls_ref[...])
        den = lr * lr + li * li
        gcr, gci = gcr_ref[...], gci_ref[...]
        gab_r = gar_ref[...] + (lr * gcr - li * gci) / den
        gab_i = gai_ref[...] + (lr * gci + li * gcr) / den
        cr, ci = cr_ref[...], ci_ref[...]
        wr = -(cr * lr + ci * li) / den
        wi = -(ci * lr - cr * li) / den
        gl_r = wr * gcr + wi * gci
        gl_i = wr * gci - wi * gcr
        ar, ai = ar_ref[...], ai_ref[...]
        gw_r = ar * gab_r + ai * gab_i
        gw_i = ar * gab_i - ai * gab_r
        gl_r = gl_r + st * gw_r
        gl_i = gl_i + st * gw_i
        pass_through = jnp.where(lam_raw < LAMBDA_RE_MAX, 1.0, jnp.where(lam_raw == LAMBDA_RE_MAX, 0.5, 0.0))
        dlr_ref[...] = gl_r * pass_through
        dli_ref[...] = gl_i
        dls_ref[...] = st * jnp.sum(lr * gw_r + li * gw_i, axis=1, keepdims=True)

    sds = jax.ShapeDtypeStruct((g, p), F32)
    return pl.pallas_call(body, name="ssm_lamgrad", out_shape=(sds, sds, jax.ShapeDtypeStruct((g, 1), F32)))(
        lam_re, lam_im, log_step, abar_re, abar_im, coef_re, coef_im, gc_re, gc_im, ga_re, ga_im)


def _row_block(r, most=256):
    for rb in range(min(r, most), BF16_ROWS - 1, -1):
        if r % rb == 0 and rb % BF16_ROWS == 0:
            return rb
    return r


def _adamw_math(w, g, m, v):
    m = ADAM_B1 * m + (1.0 - ADAM_B1) * g
    v = ADAM_B2 * v + (1.0 - ADAM_B2) * (g * g)
    m_hat = m / (1.0 - ADAM_B1 ** ADAM_STEP)
    v_hat = v / (1.0 - ADAM_B2 ** ADAM_STEP)
    delta = -ADAM_LR * (m_hat / (jnp.sqrt(v_hat) + ADAM_EPS) + ADAM_WD * w)
    return delta, m, v


def _adamw_big(p_mine, p_sib, w, m, v, name):
    r, c = w.shape
    rb = _row_block(r)

    def body(a_ref, b_ref, w_ref, m_ref, v_ref, g_ref, d_ref, mo_ref, vo_ref):
        g = a_ref[...] + b_ref[...]
        g_ref[...] = g
        d_ref[...], mo_ref[...], vo_ref[...] = _adamw_math(w_ref[...], g, m_ref[...], v_ref[...])

    spec = pl.BlockSpec((rb, c), lambda i: (i, 0))
    sds = jax.ShapeDtypeStruct((r, c), F32)
    return pl.pallas_call(
        body, name=name, grid=(r // rb,), out_shape=(sds,) * 4, in_specs=[spec] * 5, out_specs=(spec,) * 4,
        compiler_params=_cparams(("parallel",), VMEM_KEEP_OPERANDS_IN_HBM),
    )(p_mine, p_sib, w, m, v)


def _sum_blocks(stack, name):
    n, r, c = stack.shape
    rb = _row_block(r)

    def body(s_ref, o_ref):
        acc = s_ref[0].astype(F32)
        for k in range(1, n):
            acc = acc + s_ref[k].astype(F32)
        o_ref[...] = acc

    return pl.pallas_call(
        body, name=name, grid=(r // rb,), out_shape=jax.ShapeDtypeStruct((r, c), F32),
        in_specs=[pl.BlockSpec((n, rb, c), lambda i: (0, i, 0))], out_specs=pl.BlockSpec((rb, c), lambda i: (i, 0)),
        compiler_params=_cparams(("parallel",), VMEM_KEEP_OPERANDS_IN_HBM),
    )(stack)


def _add2(a, b):
    def body(a_ref, b_ref, o_ref):
        o_ref[...] = a_ref[...] + b_ref[...]

    return pl.pallas_call(body, name="add_small", out_shape=jax.ShapeDtypeStruct(a.shape, F32))(a, b)


def _adamw_ada(c_all, dmod_cols, w, m, v):
    d, n = w.shape
    bn = 512

    def body(c_ref, dm_ref, w_ref, m_ref, v_ref, g_ref, d_ref, mo_ref, vo_ref):
        cc = c_ref[...]
        g = _dot_tn(cc * _sigmoid(cc), dm_ref[...])
        g_ref[...] = g
        d_ref[...], mo_ref[...], vo_ref[...] = _adamw_math(w_ref[...], g, m_ref[...], v_ref[...])

    spec = pl.BlockSpec((d, bn), lambda j: (0, j))
    sds = jax.ShapeDtypeStruct((d, n), F32)
    return pl.pallas_call(
        body, name="adamw_ada", grid=(n // bn,), out_shape=(sds,) * 4,
        in_specs=[_full((N_DEV, d)), pl.BlockSpec((N_DEV, bn), lambda j: (0, j)), spec, spec, spec],
        out_specs=(spec,) * 4, compiler_params=_cparams(("parallel",), VMEM_KEEP_OPERANDS_IN_HBM),
    )(c_all, dmod_cols, w, m, v)


def _adamw_small(items):
    n = len(items)

    def body(*refs):
        ins, outs = refs[:4 * n], refs[4 * n:]
        for k in range(n):
            w_ref, g_ref, m_ref, v_ref = ins[4 * k:4 * k + 4]
            outs[3 * k][...], outs[3 * k + 1][...], outs[3 * k + 2][...] = _adamw_math(
                w_ref[...], g_ref[...], m_ref[...], v_ref[...])

    flat = [a for it in items for a in it]
    out_shape = tuple(jax.ShapeDtypeStruct(it[0].shape, F32) for it in items for _ in range(3))
    res = pl.pallas_call(body, name="adamw_small", out_shape=out_shape,
                         compiler_params=_cparams(vmem=VMEM_KEEP_OPERANDS_IN_HBM))(*flat)
    return [tuple(res[3 * k:3 * k + 3]) for k in range(n)]


def _group_mean_matrix(n, group):
    idx = np.arange(n) // group
    return (idx[:, None] == idx[None, :]).astype(np.float32) / group


def _fold_matrix(n, period):
    return (np.arange(n)[:, None] % period == np.arange(period)[None, :]).astype(np.float32)


def _rows8(*rows):
    c = rows[0].shape[-1]
    pad = jnp.zeros((SUBLANES - len(rows), c), F32)
    return jnp.concatenate([r.reshape(1, c) for r in rows] + [pad], axis=0)


def _to_rows(a, width):
    flat = a.reshape(-1)
    n = -(-flat.shape[0] // width)
    flat = jnp.pad(flat, (0, n * width - flat.shape[0]))
    return flat.reshape(n, width)


def kernel(x, c, w_ada, b_ada, g_pre_mix, g_post_mix, w_in, ssm_lam_re, ssm_lam_im, ssm_log_step, ssm_b_re, ssm_b_im, ssm_c_re, ssm_c_im, ssm_d, glu_w, glu_b, g_out_ssm, conv_w, g_out_conv, w_out, g_pre_ffn, g_post_ffn, w_up, ffn_conv_w, w_down, loss_target, m_w_ada, m_b_ada, m_g_pre_mix, m_g_post_mix, m_w_in, m_ssm_lam_re, m_ssm_lam_im, m_ssm_log_step, m_ssm_b_re, m_ssm_b_im, m_ssm_c_re, m_ssm_c_im, m_ssm_d, m_glu_w, m_glu_b, m_g_out_ssm, m_conv_w, m_g_out_conv, m_w_out, m_g_pre_ffn, m_g_post_ffn, m_w_up, m_ffn_conv_w, m_w_down, v_w_ada, v_b_ada, v_g_pre_mix, v_g_post_mix, v_w_in, v_ssm_lam_re, v_ssm_lam_im, v_ssm_log_step, v_ssm_b_re, v_ssm_b_im, v_ssm_c_re, v_ssm_c_im, v_ssm_d, v_glu_w, v_glu_b, v_g_out_ssm, v_conv_w, v_g_out_conv, v_w_out, v_g_pre_ffn, v_g_post_ffn, v_w_up, v_ffn_conv_w, v_w_down):
    xs = x[0]
    tgt = loss_target[0]
    t, d = xs.shape
    xi, yi, ci = lax.axis_index("x"), lax.axis_index("y"), lax.axis_index("c")
    chip = 2 * xi + yi
    dev = 2 * chip + ci

    n_groups, n_state = ssm_lam_re.shape[1:]
    n_gch = ssm_b_re.shape[3]
    d_ssm = n_groups * n_gch
    gp = n_groups * n_state
    n_ada = w_ada.shape[2]
    d_ff = w_down.shape[1] * N_CHIPS
    n_upc = w_up.shape[2]

    w_names = ("w_in", "glu_w", "w_out", "w_up", "w_down")
    c_gath, _ = _allgather8(jnp.broadcast_to(c, (SUBLANES, d)), SUBLANES, "gather_c")
    c_all = c_gath.reshape(N_DEV, SUBLANES, d)[:, 0, :]

    def pad8(a):
        return jnp.concatenate([a, jnp.zeros((SUBLANES - a.shape[0], a.shape[1]), a.dtype)], axis=0)

    def start(name, arrs, after):
        return _chips_start(name, True, [], [_landing(a, chip) for a in arrs], after)

    w_names = ("w_in", "mod", "conv_w", "ffn_conv_w", "glu_w", "w_out", "w_up", "w_down")
    first = start("weights_start_in", [w_in[0].astype(BF16)], c_gath)
    b_sh = lax.dynamic_slice(b_ada, (0, chip * n_ada), (1, n_ada))
    mod_sh = _mod_shard(c_all + first[4][0:1, 0:1], w_ada[0], b_sh)
    second = start("weights_start_rest", [mod_sh, pad8(conv_w[0]), pad8(ffn_conv_w[0])]
                   + [w[0].astype(BF16) for w in (glu_w, w_out, w_up, w_down)], None)
    w_send, w_recv, w_land = [list(first[k]) + list(second[k]) for k in (0, 1, 3)]
    w_token = second[4]

    def weights(names, after):
        ks = [w_names.index(nm) for nm in names]
        return _chips_wait("weights_wait_" + names[-1], True, [w_send[k] for k in ks], [w_recv[k] for k in ks],
                           [], [w_land[k] for k in ks], after)

    lam_re, lam_im = ssm_lam_re[0], ssm_lam_im[0]
    log_step = ssm_log_step[0].reshape(n_groups, 1) + w_token[0:1, 0:1]
    abar_re, abar_im, coef_re, coef_im = _ssm_prep(lam_re, lam_im, log_step)
    a_rows = _rows8(abar_re.reshape(1, gp), abar_im.reshape(1, gp))
    coef_rows = _rows8(coef_re.reshape(1, gp), coef_im.reshape(1, gp))
    bt_re = ssm_b_re[0].transpose(0, 2, 1).reshape(d_ssm, n_state)
    bt_im = ssm_b_im[0].transpose(0, 2, 1).reshape(d_ssm, n_state)
    ct_re = ssm_c_re[0].transpose(0, 2, 1).reshape(gp, n_gch)
    ct_im = ssm_c_im[0].transpose(0, 2, 1).reshape(gp, n_gch)
    tile_b = jnp.asarray(np.tile(np.eye(n_state), (1, n_groups // SSM_SPLIT)), BF16)
    tile_c = jnp.asarray(np.tile(np.eye(n_gch), (1, n_groups)), BF16)
    bblk_re, bblk_im, cblk_re, cblk_im = _ssm_blocks(bt_re, bt_im, ct_re, ct_im, coef_rows, tile_b, tile_c)

    h16 = jnp.asarray(_group_mean_matrix(d_ssm, n_gch), BF16)
    h64 = jnp.asarray(_group_mean_matrix(d_ssm, CONV_HEAD_DIM), BF16)

    g_mod, g_cw, g_fw, w_in_st = weights(("mod", "conv_w", "ffn_conv_w", "w_in"), bblk_re)
    mod_all = g_mod.transpose(1, 0, 2).reshape(N_DEV, N_CHIPS * n_ada)
    mod = lax.dynamic_slice(mod_all, (dev, 0), (1, N_CHIPS * n_ada))
    sh1, sc1, gt1, sh2, sc2, gt2 = [mod[:, k * d:(k + 1) * d] for k in range(6)]
    convw_full = pad8(g_cw[:, :3, :].transpose(1, 0, 2).reshape(3, d_ssm))
    fw_full = pad8(g_fw[:, :3, :].transpose(1, 0, 2).reshape(3, N_CHIPS * n_upc))

    v512 = _rows8(ssm_d, glu_b, g_out_ssm, g_out_conv)
    vec1 =_rows8(g_pre_mix, 1.0 + sc1, sh1)
    vd1 = _rows8(g_post_mix, gt1)
    vec2 = _rows8(g_pre_ffn, 1.0 + sc2, sh2)
    vd2 = _rows8(g_post_ffn, gt2)

    proj, bu_re, bu_im, h1b = _mix_in(xs, vec1, w_in_st, bblk_re, bblk_im)
    s_re, s_im = _scan_fwd(a_rows, bu_re, bu_im)
    g_glu, g_wout = weights(("glu_w", "w_out"), s_re)
    glu_full = g_glu.reshape(d_ssm, d_ssm)
    w_out_full = g_wout.reshape(2 * d_ssm, d)
    y1, o_mix, x2 = _mix_out(xs, proj, s_re, s_im, cblk_re, cblk_im, v512, convw_full, glu_full, h16, h64,
                             w_out_full, vd1)
    (w_up_st,) = weights(("w_up",), x2)
    up, h2b = _ffn_up(x2, vec2, w_up_st)
    (g_wdown,) = weights(("w_down",), up)
    w_down_full = g_wdown.reshape(d_ff, d)
    actb, ddnb, dout, dhid, vp_dn, loss_blk = _ffn_down(up, fw_full, w_down_full, w_down_full.T, x2, tgt, vd2)

    gw_down = _matmul_tn(actb, ddnb, d_ff, d, BF16, "dw_down", bt=1024).reshape(N_CHIPS, d_ff // N_CHIPS, d)
    dx2, dupb, vp_up, df_rows = _ffn_up_bwd(dhid, up, fw_full, x2, dout, vec2, w_up_st)
    gw_up = _matmul_tn(h2b, dupb, d, n_upc, BF16, "dw_up", bt=2048)
    ga_send, ga_recv, ga_src, ga_land, ga_token = _chips_start(
        "grads_start_ffn", False, [gw_down, gw_up],
        [_landing(lax.dynamic_index_in_dim(g, chip, 0, False), chip) for g in (gw_down, gw_up)])
    (dob, ycatb, zb, dqb, dy1, g_re, g_im, dcc, dbg, vp_mo, vp5, d_cre, d_cim) = _mix_out_bwd(
        dx2, o_mix, y1, proj, s_re, s_im, cblk_re, cblk_im, v512, convw_full, glu_full, h16, h64, w_out_full,
        vd1 + ga_token[0:1, 0:1])
    gw_out = _matmul_tn(ycatb, dob, 2 * d_ssm, d, BF16, "dw_out", bt=2048)
    gw_out = gw_out.reshape(N_CHIPS, 2 * d_ssm // N_CHIPS, d)
    gw_glu = _matmul_tn(zb, dqb, d_ssm, d_ssm, BF16, "dw_glu", bt=2048).reshape(N_CHIPS, d_ssm // N_CHIPS, d_ssm)
    gb_send, gb_recv, gb_src, gb_land, gb_token = _chips_start(
        "grads_start_mix", False, [gw_out, gw_glu],
        [_landing(lax.dynamic_index_in_dim(g, chip, 0, False), chip) for g in (gw_out, gw_glu)])
    gt_re, gt_im, ga_re8, ga_im8 = _scan_bwd(a_rows + gb_token[0:1, 0:1], g_re, g_im, s_re, s_im)
    grad_x, dprojb, vp_mi, d_bre, d_bim = _mix_in_bwd(gt_re, gt_im, bblk_re, bblk_im, dy1, dcc, dbg, proj, xs, dx2,
                                                      vec1, v512, convw_full, w_in_st)
    ssm_u, ssm_s = d_ssm // SSM_SPLIT, gp // SSM_SPLIT

    fold_b = jnp.asarray(_fold_matrix(ssm_s, n_state), BF16)
    fold_c = jnp.asarray(_fold_matrix(ssm_u, n_gch), BF16)
    db_re_f, db_im_f, gc_rows = _ssm_bgrad(d_bre, d_bim, bt_re, bt_im, coef_rows, fold_b, tile_b)
    dc_re_f, dc_im_f = _ssm_cgrad(d_cre, d_cim, fold_c)
    ga_sum = _ga_rowsum(ga_re8, ga_im8)
    g_lam_re, g_lam_im, g_log_step = _ssm_lamgrad(
        lam_re, lam_im, log_step, abar_re, abar_im, coef_re, coef_im,
        gc_rows[0].reshape(n_groups, n_state), gc_rows[1].reshape(n_groups, n_state),
        ga_sum[0].reshape(n_groups, n_state), ga_sum[1].reshape(n_groups, n_state))
    g_b_re = db_re_f.reshape(n_groups, n_gch, n_state).transpose(0, 2, 1)
    g_b_im = db_im_f.reshape(n_groups, n_gch, n_state).transpose(0, 2, 1)
    g_c_re = dc_re_f.reshape(n_groups, n_state, n_gch).transpose(0, 2, 1)
    g_c_im = dc_im_f.reshape(n_groups, n_state, n_gch).transpose(0, 2, 1)

    dmod = jnp.concatenate([vp_mi[0:1], vp_mi[1:2], vp_mo[0:1], vp_up[0:1], vp_up[1:2], vp_dn[0:1]], axis=1)
    small = [
        ("g_pre_mix", vp_mi[2:3]), ("g_post_mix", vp_mo[1:2]), ("g_pre_ffn", vp_up[2:3]), ("g_post_ffn", vp_dn[1:2]),
        ("ssm_lam_re", g_lam_re), ("ssm_lam_im", g_lam_im), ("ssm_log_step", g_log_step),
        ("ssm_b_re", g_b_re), ("ssm_b_im", g_b_im), ("ssm_c_re", g_c_re), ("ssm_c_im", g_c_im),
        ("ssm_d", vp5[3:4]), ("glu_b", vp5[2:3]), ("g_out_ssm", vp5[0:1]), ("g_out_conv", vp5[1:2]),
        ("conv_w", vp5[4:7]), ("ffn_conv_w", df_rows[0:3]), ("loss", loss_blk[0:1, 0:1]),
    ]
    packed, offsets, row = [], {}, 0
    for name, a in small:
        r = _to_rows(a, d)
        offsets[name] = (row, a.shape)
        packed.append(r)
        row += r.shape[0]
    n_small = -(-row // SUBLANES) * SUBLANES
    packed.append(jnp.zeros((n_small - row, d), F32))
    packed.append(pad8(dmod.reshape(6, d)))
    pack = jnp.concatenate(packed, axis=0)
    sm_send, sm_recv, _, sm_land, sm_token = _chips_start("small_start", True, [], [_landing(pack, chip)])

    gw_in = _matmul_tn(h1b, dprojb, d, w_in.shape[2], BF16, "dw_in", bt=2048, after=sm_token)
    gc_send, gc_recv, gc_src, gc_land, gc_token = _chips_start(
        "grads_start_in", False, [gw_in], [_landing(lax.dynamic_index_in_dim(gw_in, chip, 0, False), chip)])

    def partials(names, landed):
        return [_sum_blocks(s, "sum_" + nm) for s, nm in zip(landed, names)]

    def update(names, mine, theirs):
        done = {}
        for nm, pm, ps in zip(names, mine, theirs):
            w_, m_, v_ = big_params[nm]
            done[nm] = _adamw_big(pm, ps, w_[0], m_[0], v_[0], "adamw_" + nm)
        return done

    big_params = {"w_down": (w_down, m_w_down, v_w_down), "w_up": (w_up, m_w_up, v_w_up),
                  "w_out": (w_out, m_w_out, v_w_out), "glu_w": (glu_w, m_glu_w, v_glu_w),
                  "w_in": (w_in, m_w_in, v_w_in)}
    ffn_names, mix_names = ("w_down", "w_up"), ("w_out", "glu_w", "w_in")
    p_ffn = partials(ffn_names, _chips_wait("grads_wait_ffn", False, ga_send, ga_recv, ga_src, ga_land, gc_token))
    sa_send, sa_recv, sa_src, sa_land, sa_token = _sibling_start("swap_start_ffn", p_ffn)

    (sm_landed,) = _chips_wait("small_wait", True, sm_send, sm_recv, [], sm_land, sa_token)
    sm_part = _sum_blocks(sm_landed, "sum_small")
    dmod_mine = sm_landed[:, n_small:n_small + SUBLANES, :]
    ss_send, ss_recv, ss_src, ss_land, ss_token = _sibling_start("swap_start_small", [sm_part, dmod_mine])
    p_ffn, t_ffn = _sibling_wait("swap_wait_ffn", sa_send, sa_recv, sa_src, sa_land, ss_token)
    big = update(ffn_names, p_ffn, t_ffn)
    (sm_part, dmod_mine), (sm_sib, dmod_sib) = _sibling_wait("swap_wait_small", ss_send, ss_recv, ss_src, ss_land,
                                                              big["w_up"][0])
    sums = _add2(sm_part, sm_sib)
    dmod_by_core = jnp.stack([dmod_mine, dmod_sib], axis=1)
    dmod_by_core = jnp.where(ci == 0, dmod_by_core, dmod_by_core[:, ::-1])
    dmod_all = dmod_by_core[:, :, :6, :].reshape(N_DEV, 6 * d)
    g_b_ada = sums[n_small:n_small + 6].reshape(1, 6 * d)

    def unpack(name):
        r0, shape = offsets[name]
        size = math.prod(shape)
        nrow = -(-size // d)
        return sums[r0:r0 + nrow].reshape(-1)[:size].reshape(shape)

    p_mix = partials(mix_names, _chips_wait(
        "grads_wait_mix", False, list(gb_send) + list(gc_send), list(gb_recv) + list(gc_recv),
        list(gb_src) + list(gc_src), list(gb_land) + list(gc_land), sums))
    sb_send, sb_recv, sb_src, sb_land, sb_token = _sibling_start("swap_start_mix", p_mix)

    dmod_cols = lax.dynamic_slice(dmod_all, (0, chip * n_ada), (N_DEV, n_ada)) + sb_token[0:1, 0:1]
    ada = _adamw_ada(c_all, dmod_cols, w_ada[0], m_w_ada[0], v_w_ada[0])
    p_mix, t_mix = _sibling_wait("swap_wait_mix", sb_send, sb_recv, sb_src, sb_land, ada[0])
    big.update(update(mix_names, p_mix, t_mix))

    g_small = {name: unpack(name) for name, _ in small}
    g_small["b_ada"] = g_b_ada
    g_small["conv_w"] = lax.dynamic_slice(g_small["conv_w"], (0, chip * conv_w.shape[2]), (3, conv_w.shape[2]))
    g_small["ffn_conv_w"] = lax.dynamic_slice(g_small["ffn_conv_w"], (0, chip * n_upc), (3, n_upc))
    g_small["ssm_log_step"] = g_small["ssm_log_step"].reshape(1, n_groups)
    small_params = {
        "b_ada": (b_ada, m_b_ada, v_b_ada), "g_pre_mix": (g_pre_mix, m_g_pre_mix, v_g_pre_mix),
        "g_post_mix": (g_post_mix, m_g_post_mix, v_g_post_mix), "ssm_lam_re": (ssm_lam_re, m_ssm_lam_re, v_ssm_lam_re),
        "ssm_lam_im": (ssm_lam_im, m_ssm_lam_im, v_ssm_lam_im),
        "ssm_log_step": (ssm_log_step, m_ssm_log_step, v_ssm_log_step),
        "ssm_b_re": (ssm_b_re, m_ssm_b_re, v_ssm_b_re), "ssm_b_im": (ssm_b_im, m_ssm_b_im, v_ssm_b_im),
        "ssm_c_re": (ssm_c_re, m_ssm_c_re, v_ssm_c_re), "ssm_c_im": (ssm_c_im, m_ssm_c_im, v_ssm_c_im),
        "ssm_d": (ssm_d, m_ssm_d, v_ssm_d), "glu_b": (glu_b, m_glu_b, v_glu_b),
        "g_out_ssm": (g_out_ssm, m_g_out_ssm, v_g_out_ssm), "conv_w": (conv_w, m_conv_w, v_conv_w),
        "g_out_conv": (g_out_conv, m_g_out_conv, v_g_out_conv), "g_pre_ffn": (g_pre_ffn, m_g_pre_ffn, v_g_pre_ffn),
        "g_post_ffn": (g_post_ffn, m_g_post_ffn, v_g_post_ffn),
        "ffn_conv_w": (ffn_conv_w, m_ffn_conv_w, v_ffn_conv_w),
    }

    def natural(a):
        return a[0] if a.ndim > 2 else a

    names = list(small_params)
    items = []
    for nm in names:
        w_, m_, v_ = small_params[nm]
        items.append((natural(w_), g_small[nm].reshape(natural(w_).shape), natural(m_), natural(v_)))
    upd = _adamw_small(items)
    small_out = {}
    for nm, (dl, mo, vo) in zip(names, upd):
        shp = small_params[nm][0].shape
        small_out[nm] = (g_small[nm].reshape(shp), dl.reshape(shp), mo.reshape(shp), vo.reshape(shp))

    loss = g_small["loss"][0, 0]

    order = ["w_ada", "b_ada", "g_pre_mix", "g_post_mix", "w_in", "ssm_lam_re", "ssm_lam_im", "ssm_log_step",
             "ssm_b_re", "ssm_b_im", "ssm_c_re", "ssm_c_im", "ssm_d", "glu_w", "glu_b", "g_out_ssm", "conv_w",
             "g_out_conv", "w_out", "g_pre_ffn", "g_post_ffn", "w_up", "ffn_conv_w", "w_down"]
    results = {"w_ada": tuple(a[None] for a in ada)}
    for nm in big:
        results[nm] = tuple(a[None] for a in big[nm])
    results.update(small_out)
    outs = [loss, grad_x[None]]
    for k in range(4):
        outs += [results[nm][k] for nm in order]
    return tuple(outs)


def _ga_rowsum(ga_re8, ga_im8):
    n = ga_re8.shape[1]

    def body(r_ref, i_ref, o_ref):
        o_ref[...] = jnp.zeros(o_ref.shape, F32)
        o_ref[0:1, :] = _colsum(r_ref[...])
        o_ref[1:2, :] = _colsum(i_ref[...])

    return pl.pallas_call(body, name="ga_rowsum", out_shape=jax.ShapeDtypeStruct((SUBLANES, n), F32))(ga_re8, ga_im8)
```

```python
import functools
import math

import jax
import jax.numpy as jnp
import numpy as np
from jax import lax
from jax.experimental import pallas as pl
from jax.experimental.pallas import tpu as pltpu

F32 = jnp.float32
BF16 = jnp.bfloat16
MESH = pl.DeviceIdType.MESH

EPS = 1e-6
LAMBDA_RE_MAX = -1e-4
ADAM_LR = 0.001
ADAM_B1 = 0.9
ADAM_B2 = 0.999
ADAM_EPS = 1e-08
ADAM_WD = 0.01
ADAM_STEP = 10

SUBLANES = 8
BF16_ROWS = 16
N_CHIPS = 4
N_DEV = 8
CONV_HEAD_DIM = 64
VMEM_BIG = 56 * 1024 * 1024
VMEM_MID = 40 * 1024 * 1024
VMEM_KEEP_OPERANDS_IN_HBM = 62 * 1024 * 1024

TB_MIX = 256
TB_FFN = 256
TB_FFN_UP = 512
TB_SCAN = 1024
W_SCAN = 256
SSM_SPLIT = 4
CW_FFN = 256
SCAN_UNROLL = 4
TB_TN = 512


def _cparams(sem=None, vmem=None):
    kw = {}
    if sem is not None:
        kw["dimension_semantics"] = sem
    if vmem is not None:
        kw["vmem_limit_bytes"] = vmem
    return pltpu.CompilerParams(**kw)


def _blk(t, pref):
    return pref if t % pref == 0 else t


def _dot(a, b):
    return jnp.dot(a.astype(BF16), b.astype(BF16), preferred_element_type=F32)


def _dot_nt(a, b):
    return lax.dot_general(a.astype(BF16), b.astype(BF16), (((1,), (1,)), ((), ())),
                           preferred_element_type=F32)


def _dot_tn(a, b):
    return lax.dot_general(a.astype(BF16), b.astype(BF16), (((0,), (0,)), ((), ())),
                           preferred_element_type=F32)


def _sigmoid(x):
    return 0.5 * jnp.tanh(0.5 * x) + 0.5


_GELU_K = math.sqrt(2.0 / math.pi)
_GELU_C = 0.044715


def _gelu(x):
    th = jnp.tanh(_GELU_K * (x + _GELU_C * x * x * x))
    return x * (0.5 * (1.0 + th))


def _gelu_and_grad(x):
    x2 = x * x
    th = jnp.tanh(_GELU_K * (x + _GELU_C * x2 * x))
    half = 0.5 * (1.0 + th)
    return x * half, half + 0.5 * x * (1.0 - th * th) * _GELU_K * (1.0 + 3.0 * _GELU_C * x2)


def _rowmean(x):
    return jnp.mean(x, axis=-1, keepdims=True)


def _colsum(x):
    return jnp.sum(x, axis=0, keepdims=True)


def _split_dot(x, m):
    hi = x.astype(BF16)
    lo = (x - hi.astype(F32)).astype(BF16)
    return (jnp.dot(hi, m, preferred_element_type=F32) + jnp.dot(lo, m, preferred_element_type=F32))


def _split3_dot(x, m):
    hi = x.astype(BF16)
    r1 = x - hi.astype(F32)
    mid = r1.astype(BF16)
    lo = (r1 - mid.astype(F32)).astype(BF16)
    return (jnp.dot(hi, m, preferred_element_type=F32) + jnp.dot(mid, m, preferred_element_type=F32)
            + jnp.dot(lo, m, preferred_element_type=F32))


def _shift_down(x, halo, k):
    r = pltpu.roll(x, k, 0)
    row = lax.broadcasted_iota(jnp.int32, x.shape, 0)
    for j in range(k):
        r = jnp.where(row == j, halo[SUBLANES - k + j:SUBLANES - k + j + 1, :], r)
    return r


def _shift_up(x, halo, k):
    n = x.shape[0]
    r = pltpu.roll(x, n - k, 0)
    row = lax.broadcasted_iota(jnp.int32, x.shape, 0)
    for j in range(k):
        r = jnp.where(row == n - k + j, halo[j:j + 1, :], r)
    return r


def _acc_rows(ref, first, rows):
    @pl.when(first)
    def _():
        ref[...] = jnp.zeros(ref.shape, ref.dtype)
    for j, r in enumerate(rows):
        ref[j:j + 1, :] += r


def _rows(tb, c, col=0):
    return pl.BlockSpec((tb, c), lambda i, col=col: (i, col))


def _full(shape):
    nd = len(shape)
    return pl.BlockSpec(shape, lambda i, nd=nd: (0,) * nd)


def _resident(shape):
    nd = len(shape)
    return pl.BlockSpec(shape, lambda i, nd=nd: (0,) * nd, pipeline_mode=pl.Buffered(1))


def _halo_prev(tb, c, col=0):
    per = tb // SUBLANES
    return pl.BlockSpec((SUBLANES, c), lambda i, col=col: (jnp.maximum(i * per - 1, 0), col))


def _halo_next(tb, c, t, col=0, rows=SUBLANES):
    per = tb // rows
    last = t // rows - 1
    return pl.BlockSpec((rows, c), lambda i, col=col: (jnp.minimum((i + 1) * per, last), col))


def _mesh_pos():
    return lax.axis_index("x"), lax.axis_index("y"), lax.axis_index("c")


def _allgather8(x_pad, n_sum, name):
    m_per, n = x_pad.shape

    def body(x_ref, out_ref, sum_ref, send_sems, recv_sems, local_sem):
        x, y, c = _mesh_pos()
        me, sibling = (x, y, c), (x, y, 1 - c)
        chips = [(1 - x, y), (x, 1 - y), (1 - x, 1 - y)]

        def rows(px, py, pc):
            return out_ref.at[pl.ds((4 * px + 2 * py + pc) * m_per, m_per), :]

        def copy(k, block, to, src=None):
            return pltpu.make_async_remote_copy(
                src_ref=rows(*block) if src is None else src, dst_ref=rows(*block),
                send_sem=send_sems.at[k], recv_sem=recv_sems.at[k], device_id=to, device_id_type=MESH)

        mine = pltpu.make_async_copy(x_ref, rows(*me), local_sem)
        mine.start()
        first = [copy(0, me, sibling, src=x_ref)]
        first += [copy(1 + j, me, (*chip, c), src=x_ref) for j, chip in enumerate(chips)]
        for cp in first:
            cp.start()
        passed = [copy(4 + j, (*chip, c), sibling) for j, chip in enumerate(chips)]
        for j, chip in enumerate(chips):
            copy(1 + j, (*chip, c), me).wait_recv()
            passed[j].start()
        copy(0, sibling, me).wait_recv()
        for j, chip in enumerate(chips):
            copy(4 + j, (*chip, 1 - c), me).wait_recv()
        for cp in first + passed:
            cp.wait_send()
        mine.wait()
        acc = out_ref[0:n_sum, :]
        for k in range(1, N_DEV):
            acc = acc + out_ref[k * m_per:k * m_per + n_sum, :]
        sum_ref[...] = acc

    return pl.pallas_call(
        body, name=name,
        out_shape=(jax.ShapeDtypeStruct((N_DEV * m_per, n), F32), jax.ShapeDtypeStruct((n_sum, n), F32)),
        in_specs=[pl.BlockSpec(memory_space=pltpu.VMEM)],
        out_specs=(pl.BlockSpec(memory_space=pltpu.VMEM), pl.BlockSpec(memory_space=pltpu.VMEM)),
        scratch_shapes=[pltpu.SemaphoreType.DMA((7,)), pltpu.SemaphoreType.DMA((7,)), pltpu.SemaphoreType.DMA],
        compiler_params=_cparams(vmem=VMEM_MID),
    )(x_pad)


_HBM = pl.BlockSpec(memory_space=pltpu.HBM)
_SEM = pl.BlockSpec(memory_space=pltpu.SEMAPHORE)
_EFFECT = pltpu.SideEffectType.DATAFLOW_SIDE_EFFECTING


def _chip_copy(gather, src_ref, land_ref, send, recv, j, arrival):
    x, y, c = _mesh_pos()
    peer = [(1 - x, y), (x, 1 - y), (1 - x, 1 - y)][j]
    peer_chip = 2 * peer[0] + peer[1]
    my_chip = 2 * x + y
    return pltpu.make_async_remote_copy(
        src_ref=land_ref.at[my_chip] if gather else src_ref.at[peer_chip],
        dst_ref=land_ref.at[peer_chip if arrival else my_chip],
        send_sem=send.at[j], recv_sem=recv.at[j], device_id=(*peer, c), device_id_type=MESH)


def _chips_start(name, gather, srcs, lands, after=None):
    n, ns = len(lands), len(srcs)
    extra = [] if after is None else [after]

    def body(*refs):
        src_refs, land_refs = refs[:ns], refs[ns:ns + n]
        outs = refs[ns + n + len(extra):]
        sends, recvs, token = outs[:n], outs[n:2 * n], outs[-1]
        for k in range(n):
            for j in range(3):
                _chip_copy(gather, src_refs[k] if ns else None, land_refs[k], sends[k], recvs[k], j, False).start()
        token[...] = jnp.zeros(token.shape, F32)

    sem = pltpu.SemaphoreType.DMA((3,))
    thru = tuple(pltpu.HBM(a.shape, a.dtype) for a in list(srcs) + list(lands))
    res = pl.pallas_call(
        body, name=name,
        out_shape=(sem,) * (2 * n) + thru + (jax.ShapeDtypeStruct((SUBLANES, 128), F32),),
        in_specs=[_HBM] * (ns + n) + [pl.BlockSpec(memory_space=pl.ANY)] * len(extra),
        out_specs=(_SEM,) * (2 * n) + (_HBM,) * (ns + n) + (pl.BlockSpec(memory_space=pltpu.VMEM),),
        input_output_aliases={k: 2 * n + k for k in range(ns + n)},
        compiler_params=pltpu.CompilerParams(has_side_effects=_EFFECT),
    )(*[pltpu.with_memory_space_constraint(a, pltpu.HBM) for a in list(srcs) + list(lands)], *extra)
    return res[:n], res[n:2 * n], res[2 * n:2 * n + ns], res[2 * n + ns:2 * n + ns + n], res[-1]


def _chips_wait(name, gather, sends, recvs, srcs, lands, after):
    n, ns = len(lands), len(srcs)

    def body(*refs):
        src_refs, land_refs = refs[:ns], refs[ns:ns + n]
        sends_, recvs_ = refs[ns + n:ns + 2 * n], refs[ns + 2 * n:ns + 3 * n]
        for k in range(n):
            for j in range(3):
                cp = _chip_copy(gather, src_refs[k] if ns else None, land_refs[k], sends_[k], recvs_[k], j, True)
                cp.wait_send()
                cp.wait_recv()

    thru = tuple(pltpu.HBM(a.shape, a.dtype) for a in list(srcs) + list(lands))
    res = pl.pallas_call(
        body, name=name, out_shape=thru,
        in_specs=[_HBM] * (ns + n) + [_SEM] * (2 * n) + [pl.BlockSpec(memory_space=pl.ANY)],
        out_specs=(_HBM,) * (ns + n),
        input_output_aliases={k: k for k in range(ns + n)},
        compiler_params=pltpu.CompilerParams(has_side_effects=_EFFECT),
    )(*srcs, *lands, *sends, *recvs, after)
    return res[:ns], res[ns:]


def _sibling_copy(src_ref, land_ref, send, recv):
    x, y, c = _mesh_pos()
    return pltpu.make_async_remote_copy(src_ref=src_ref, dst_ref=land_ref, send_sem=send.at[0], recv_sem=recv.at[0],
                                        device_id=(x, y, 1 - c), device_id_type=MESH)


def _sibling_start(name, arrs, after=None):
    n = len(arrs)
    extra = [] if after is None else [after]
    lands = [lax.empty(a.shape, a.dtype) for a in arrs]

    def body(*refs):
        src_refs, land_refs = refs[:n], refs[n:2 * n]
        outs = refs[2 * n + len(extra):]
        sends, recvs, token = outs[:n], outs[n:2 * n], outs[-1]
        for k in range(n):
            _sibling_copy(src_refs[k], land_refs[k], sends[k], recvs[k]).start()
        token[...] = jnp.zeros(token.shape, F32)

    sem = pltpu.SemaphoreType.DMA((1,))
    thru = tuple(pltpu.HBM(a.shape, a.dtype) for a in list(arrs) + lands)
    res = pl.pallas_call(
        body, name=name,
        out_shape=(sem,) * (2 * n) + thru + (jax.ShapeDtypeStruct((SUBLANES, 128), F32),),
        in_specs=[_HBM] * (2 * n) + [pl.BlockSpec(memory_space=pl.ANY)] * len(extra),
        out_specs=(_SEM,) * (2 * n) + (_HBM,) * (2 * n) + (pl.BlockSpec(memory_space=pltpu.VMEM),),
        input_output_aliases={k: 2 * n + k for k in range(2 * n)},
        compiler_params=pltpu.CompilerParams(has_side_effects=_EFFECT),
    )(*[pltpu.with_memory_space_constraint(a, pltpu.HBM) for a in list(arrs) + lands], *extra)
    return res[:n], res[n:2 * n], res[2 * n:3 * n], res[3 * n:4 * n], res[-1]


def _sibling_wait(name, sends, recvs, srcs, lands, after):
    n = len(srcs)

    def body(*refs):
        src_refs, land_refs = refs[:n], refs[n:2 * n]
        sends_, recvs_ = refs[2 * n:3 * n], refs[3 * n:4 * n]
        for k in range(n):
            cp = _sibling_copy(src_refs[k], land_refs[k], sends_[k], recvs_[k])
            cp.wait_send()
            cp.wait_recv()

    thru = tuple(pltpu.HBM(a.shape, a.dtype) for a in list(srcs) + list(lands))
    res = pl.pallas_call(
        body, name=name, out_shape=thru,
        in_specs=[_HBM] * (2 * n) + [_SEM] * (2 * n) + [pl.BlockSpec(memory_space=pl.ANY)],
        out_specs=(_HBM,) * (2 * n),
        input_output_aliases={k: k for k in range(2 * n)},
        compiler_params=pltpu.CompilerParams(has_side_effects=_EFFECT),
    )(*srcs, *lands, *sends, *recvs, after)
    return res[:n], res[n:]


def _landing(own, chip):
    zone = lax.empty((N_CHIPS,) + own.shape, own.dtype)
    return lax.dynamic_update_slice(zone, own[None], (chip,) + (0,) * own.ndim)


def _mod_shard(c_all, w_ada_sh, b_sh):
    d, n = w_ada_sh.shape
    bn = 512

    def body(c_ref, w_ref, b_ref, o_ref):
        cc = c_ref[...]
        ca = cc * _sigmoid(cc)
        o_ref[...] = _dot(ca, w_ref[...]) + b_ref[...]

    return pl.pallas_call(
        body, name="mod_shard", grid=(n // bn,),
        out_shape=jax.ShapeDtypeStruct((N_DEV, n), F32),
        in_specs=[_full((N_DEV, d)), pl.BlockSpec((d, bn), lambda j: (0, j)), pl.BlockSpec((1, bn), lambda j: (0, j))],
        out_specs=pl.BlockSpec((N_DEV, bn), lambda j: (0, j)),
        compiler_params=_cparams(("parallel",)),
    )(c_all, w_ada_sh, b_sh)


def _ssm_prep(lam_re, lam_im, log_step):
    g, p = lam_re.shape

    def body(lr_ref, li_ref, ls_ref, ar_ref, ai_ref, cr_ref, ci_ref):
        lr = jnp.minimum(lr_ref[...], LAMBDA_RE_MAX)
        li = li_ref[...]
        st = jnp.exp(ls_ref[...])
        mag = jnp.exp(lr * st)
        ar = mag * jnp.cos(li * st)
        ai = mag * jnp.sin(li * st)
        den = lr * lr + li * li
        nr = ar - 1.0
        ar_ref[...] = ar
        ai_ref[...] = ai
        cr_ref[...] = (nr * lr + ai * li) / den
        ci_ref[...] = (ai * lr - nr * li) / den

    sds = jax.ShapeDtypeStruct((g, p), F32)
    return pl.pallas_call(body, name="ssm_prep", out_shape=(sds,) * 4)(lam_re, lam_im, log_step)


def _ssm_blocks(bt_re, bt_im, ct_re, ct_im, coef_rows, tile_b, tile_c):
    gh, p = bt_re.shape
    gp, h = ct_re.shape
    nb = SSM_SPLIT
    cb, rb = gp // nb, gp // nb

    def body(btr, bti, ctr, cti, cf, tb_ref, tc_ref, bre_o, bim_o, cre_o, cim_o):
        j = pl.program_id(0)
        row = lax.broadcasted_iota(jnp.int32, (gh, cb), 0)
        col = lax.broadcasted_iota(jnp.int32, (gh, cb), 1) + j * cb
        mask = (row >> 4) == (col >> 6)
        cr, ci = cf[0:1, :], cf[1:2, :]
        br = _split3_dot(btr[...], tb_ref[...])
        bi = _split3_dot(bti[...], tb_ref[...])
        bre_o[...] = jnp.where(mask, br * cr - bi * ci, 0.0).astype(BF16)
        bim_o[...] = jnp.where(mask, br * ci + bi * cr, 0.0).astype(BF16)
        row2 = lax.broadcasted_iota(jnp.int32, (rb, gh), 0) + j * rb
        col2 = lax.broadcasted_iota(jnp.int32, (rb, gh), 1)
        mask2 = (row2 >> 6) == (col2 >> 4)
        cre_o[...] = jnp.where(mask2, _split3_dot(ctr[...], tc_ref[...]), 0.0).astype(BF16)
        cim_o[...] = jnp.where(mask2, _split3_dot(cti[...], tc_ref[...]), 0.0).astype(BF16)

    bspec = pl.BlockSpec((gh, cb), lambda j: (0, j))
    cspec = pl.BlockSpec((rb, gh), lambda j: (j, 0))
    cin = pl.BlockSpec((rb, h), lambda j: (j, 0))
    return pl.pallas_call(
        body, name="ssm_blocks", grid=(nb,),
        out_shape=(jax.ShapeDtypeStruct((gh, gp), BF16),) * 2 + (jax.ShapeDtypeStruct((gp, gh), BF16),) * 2,
        in_specs=[_full((gh, p)), _full((gh, p)), cin, cin, pl.BlockSpec((SUBLANES, cb), lambda j: (0, j)),
                  _full(tile_b.shape), _full(tile_c.shape)],
        out_specs=(bspec, bspec, cspec, cspec),
        compiler_params=_cparams(("parallel",)),
    )(bt_re, bt_im, ct_re, ct_im, coef_rows, tile_b, tile_c)


def _scan_consts(a_ref, reverse):
    w = a_ref.shape[1]
    ar1 = a_ref[0:1, :]
    ai1 = a_ref[1:2, :]
    if reverse:
        ai1 = -ai1
    pr, pi = [ar1], [ai1]
    for _ in range(1, SUBLANES):
        nr = pr[-1] * ar1 - pi[-1] * ai1
        ni = pr[-1] * ai1 + pi[-1] * ar1
        pr.append(nr)
        pi.append(ni)
    row = lax.broadcasted_iota(jnp.int32, (SUBLANES, w), 0)
    dist = (SUBLANES - 1 - row) if reverse else row

    def pick(vals):
        out = jnp.broadcast_to(vals[SUBLANES - 1], (SUBLANES, w))
        for r in range(SUBLANES - 1):
            out = jnp.where(dist == r, vals[r], out)
        return out

    p_r, p_i = pick(pr), pick(pi)
    steps = []
    for k in (1, 2, 4):
        steps.append((k, jnp.where(dist >= k, pr[k - 1], 0.0), jnp.where(dist >= k, pi[k - 1], 0.0)))
    a8 = (jnp.broadcast_to(pr[SUBLANES - 1], (SUBLANES, w)), jnp.broadcast_to(pi[SUBLANES - 1], (SUBLANES, w)))
    return row, p_r, p_i, steps, a8


def _scan_tile(xr, xi, cr, ci, consts, reverse):
    row, p_r, p_i, steps, (a8r, a8i) = consts
    for k, s_r, s_i in steps:
        sh = (SUBLANES - k) if reverse else k
        qr = pltpu.roll(xr, sh, 0)
        qi = pltpu.roll(xi, sh, 0)
        xr, xi = xr + s_r * qr - s_i * qi, xi + s_r * qi + s_i * qr
    outr = xr + p_r * cr - p_i * ci
    outi = xi + p_r * ci + p_i * cr
    e = 0 if reverse else SUBLANES - 1
    er = jnp.broadcast_to(xr[e:e + 1, :], xr.shape)
    ei = jnp.broadcast_to(xi[e:e + 1, :], xi.shape)
    return outr, outi, er + a8r * cr - a8i * ci, ei + a8r * ci + a8i * cr


def _scan_fwd(a_rows, bu_re, bu_im):
    t, n = bu_re.shape
    tb, w = _blk(t, TB_SCAN), W_SCAN
    ntile = tb // SUBLANES

    def body(a_ref, br_ref, bi_ref, sr_ref, si_ref, car, cai):
        @pl.when(pl.program_id(1) == 0)
        def _():
            car[...] = jnp.zeros(car.shape, F32)
            cai[...] = jnp.zeros(cai.shape, F32)
        consts = _scan_consts(a_ref, False)

        def pair(i, carry):
            o = pl.multiple_of(i * BF16_ROWS, BF16_ROWS)
            outs = []
            for h in range(2):
                rows = pl.ds(o + h * SUBLANES, SUBLANES)
                outr, outi, ncr, nci = _scan_tile(br_ref[rows, :], bi_ref[rows, :], carry[0], carry[1], consts, False)
                outs.append((outr, outi))
                carry = (ncr, nci)
            sr_ref[pl.ds(o, BF16_ROWS), :] = jnp.concatenate([outs[0][0], outs[1][0]], axis=0).astype(BF16)
            si_ref[pl.ds(o, BF16_ROWS), :] = jnp.concatenate([outs[0][1], outs[1][1]], axis=0).astype(BF16)
            return carry

        def pairs(i, carry):
            for s in range(SCAN_UNROLL // 2):
                carry = pair(i * (SCAN_UNROLL // 2) + s, carry)
            return carry

        cr, ci = lax.fori_loop(0, ntile // SCAN_UNROLL, pairs, (car[...], cai[...]))
        car[...] = cr
        cai[...] = ci

    spec = pl.BlockSpec((tb, w), lambda s, k: (k, s))
    sds = jax.ShapeDtypeStruct((t, n), BF16)
    return pl.pallas_call(
        body, name="scan_fwd", grid=(n // w, t // tb), out_shape=(sds, sds),
        in_specs=[pl.BlockSpec((SUBLANES, w), lambda s, k: (0, s)), spec, spec], out_specs=(spec, spec),
        scratch_shapes=[pltpu.VMEM((SUBLANES, w), F32), pltpu.VMEM((SUBLANES, w), F32)],
        compiler_params=_cparams(("parallel", "arbitrary"), VMEM_MID),
    )(a_rows, bu_re, bu_im)


def _scan_bwd(a_rows, g_re, g_im, s_re, s_im):
    t, n = g_re.shape
    tb, w = _blk(t, TB_SCAN), W_SCAN
    ntile = tb // SUBLANES
    npair = tb // BF16_ROWS
    nt = t // tb

    def body(a_ref, gr_ref, gi_ref, sr_ref, si_ref, or_ref, oi_ref, gar_ref, gai_ref, car, cai):
        @pl.when(pl.program_id(1) == 0)
        def _():
            car[...] = jnp.zeros(car.shape, F32)
            cai[...] = jnp.zeros(cai.shape, F32)
            gar_ref[...] = jnp.zeros(gar_ref.shape, F32)
            gai_ref[...] = jnp.zeros(gai_ref.shape, F32)
        consts = _scan_consts(a_ref, True)
        row = consts[0]

        def pair(i, carry):
            cr, ci, accr, acci = carry
            o = pl.multiple_of((npair - 1 - i) * BF16_ROWS, BF16_ROWS)
            s_r = sr_ref[pl.ds(o, BF16_ROWS), :].astype(F32)
            s_i = si_ref[pl.ds(o, BF16_ROWS), :].astype(F32)
            outs = [None, None]
            for h in (1, 0):
                rows = pl.ds(o + h * SUBLANES, SUBLANES)
                outr, outi, ncr, nci = _scan_tile(gr_ref[rows, :], gi_ref[rows, :], cr, ci, consts, True)
                outs[h] = (outr, outi)
                gnr = jnp.where(row == SUBLANES - 1, cr, pltpu.roll(outr, SUBLANES - 1, 0))
                gni = jnp.where(row == SUBLANES - 1, ci, pltpu.roll(outi, SUBLANES - 1, 0))
                sr = s_r[h * SUBLANES:(h + 1) * SUBLANES, :]
                si = s_i[h * SUBLANES:(h + 1) * SUBLANES, :]
                accr, acci = accr + sr * gnr + si * gni, acci + sr * gni - si * gnr
                cr, ci = ncr, nci
            or_ref[pl.ds(o, BF16_ROWS), :] = jnp.concatenate([outs[0][0], outs[1][0]], axis=0).astype(BF16)
            oi_ref[pl.ds(o, BF16_ROWS), :] = jnp.concatenate([outs[0][1], outs[1][1]], axis=0).astype(BF16)
            return cr, ci, accr, acci

        def pairs(i, carry):
            for s in range(SCAN_UNROLL // 2):
                carry = pair(i * (SCAN_UNROLL // 2) + s, carry)
            return carry

        cr, ci, accr, acci = lax.fori_loop(0, ntile // SCAN_UNROLL, pairs,
                                           (car[...], cai[...], gar_ref[...], gai_ref[...]))
        car[...] = cr
        cai[...] = ci
        gar_ref[...] = accr
        gai_ref[...] = acci

    spec = pl.BlockSpec((tb, w), lambda s, k: (nt - 1 - k, s))
    aspec = pl.BlockSpec((SUBLANES, w), lambda s, k: (0, s))
    sds = jax.ShapeDtypeStruct((t, n), BF16)
    asds = jax.ShapeDtypeStruct((SUBLANES, n), F32)
    return pl.pallas_call(
        body, name="scan_bwd", grid=(n // w, nt), out_shape=(sds, sds, asds, asds),
        in_specs=[aspec, spec, spec, spec, spec], out_specs=(spec, spec, aspec, aspec),
        scratch_shapes=[pltpu.VMEM((SUBLANES, w), F32), pltpu.VMEM((SUBLANES, w), F32)],
        compiler_params=_cparams(("parallel", "arbitrary"), VMEM_MID),
    )(a_rows, g_re, g_im, s_re, s_im)


def _mix_in(x, vec, w_in_st, b_re, b_im):
    t, d = x.shape
    ns, _, nc = w_in_st.shape
    dssm, nstate = b_re.shape
    du, ds = dssm // SSM_SPLIT, nstate // SSM_SPLIT
    tb = _blk(t, TB_MIX)

    def body(x_ref, vec_ref, w_ref, bre_ref, bim_ref, proj_ref, bur_ref, bui_ref, h1_ref):
        xv = x_ref[...]
        r = lax.rsqrt(_rowmean(xv * xv) + EPS)
        h = xv * r * vec_ref[0:1, :] * vec_ref[1:2, :] + vec_ref[2:3, :]
        hb = h.astype(BF16)
        h1_ref[...] = hb
        u = None
        for j in range(ns):
            pj = jnp.dot(hb, w_ref[j], preferred_element_type=F32)
            proj_ref[:, j * nc:(j + 1) * nc] = pj
            if j == 0:
                u = pj
        ub = u.astype(BF16)
        for q in range(SSM_SPLIT):
            rq, cq = slice(q * du, (q + 1) * du), slice(q * ds, (q + 1) * ds)
            bur_ref[:, cq] = jnp.dot(ub[:, rq], bre_ref[rq, cq], preferred_element_type=F32)
            bui_ref[:, cq] = jnp.dot(ub[:, rq], bim_ref[rq, cq], preferred_element_type=F32)

    return pl.pallas_call(
        body, name="mix_in", grid=(t // tb,),
        out_shape=(jax.ShapeDtypeStruct((t, ns * nc), F32), jax.ShapeDtypeStruct((t, nstate), F32),
                   jax.ShapeDtypeStruct((t, nstate), F32), jax.ShapeDtypeStruct((t, d), BF16)),
        in_specs=[_rows(tb, d), _full((SUBLANES, d)), _resident(w_in_st.shape), _resident(b_re.shape),
                  _resident(b_im.shape)],
        out_specs=(_rows(tb, ns * nc), _rows(tb, nstate), _rows(tb, nstate), _rows(tb, d)),
        compiler_params=_cparams(("parallel",), VMEM_BIG),
    )(x, vec, w_in_st, b_re, b_im)


def _head_ms(y, h_ref):
    return _split_dot(y * y, h_ref[...])


def _conv3(x, halo, w_ref):
    return w_ref[0:1, :] * _shift_down(x, halo, 2) + w_ref[1:2, :] * _shift_down(x, halo, 1) + w_ref[2:3, :] * x


def _mix_out(x, proj, s_re, s_im, c_re, c_im, v512, convw, glu_w, h16, h64, w_out, vd):
    t, d = x.shape
    dh = c_re.shape[1]
    nstate = s_re.shape[1]
    du, ds = dh // SSM_SPLIT, nstate // SSM_SPLIT
    tb = _blk(t, TB_MIX)

    def body(x_ref, u_ref, bg_ref, cg_ref, v_ref, cgh_ref, vh_ref, sr_ref, si_ref, cre_ref, cim_ref, p_ref,
             cw_ref, gw_ref, h16_ref, h64_ref, wo_ref, vd_ref, y1_ref, o_ref, x2_ref):
        i = pl.program_id(0)
        u = u_ref[...]
        ys = []
        for q in range(SSM_SPLIT):
            rq, cq = slice(q * ds, (q + 1) * ds), slice(q * du, (q + 1) * du)
            ys.append(_dot(sr_ref[:, rq], cre_ref[rq, cq]) - _dot(si_ref[:, rq], cim_ref[rq, cq]))
        ys = jnp.concatenate(ys, axis=1)
        y1 = ys + p_ref[0:1, :] * u
        y1_ref[...] = y1
        z = _gelu(y1)
        q = _dot(z, gw_ref[...]) + p_ref[1:2, :]
        ya = z * _sigmoid(q)
        na = ya * lax.rsqrt(_head_ms(ya, h16_ref) + EPS) * p_ref[2:3, :]
        cv = cg_ref[...] * v_ref[...]
        cvh = jnp.where(i > 0, cgh_ref[...] * vh_ref[...], 0.0)
        yb = bg_ref[...] * _conv3(cv, cvh, cw_ref)
        nb = yb * lax.rsqrt(_head_ms(yb, h64_ref) + EPS) * p_ref[3:4, :]
        o = _dot(na, wo_ref[0:dh, :]) + _dot(nb, wo_ref[dh:2 * dh, :])
        o_ref[...] = o
        on = o * lax.rsqrt(_rowmean(o * o) + EPS) * vd_ref[0:1, :]
        x2_ref[...] = x_ref[...] + vd_ref[1:2, :] * on

    return pl.pallas_call(
        body, name="mix_out", grid=(t // tb,),
        out_shape=(jax.ShapeDtypeStruct((t, dh), F32), jax.ShapeDtypeStruct((t, d), F32),
                   jax.ShapeDtypeStruct((t, d), F32)),
        in_specs=[_rows(tb, d), _rows(tb, dh, 0), _rows(tb, dh, 1), _rows(tb, dh, 2), _rows(tb, dh, 3),
                  _halo_prev(tb, dh, 2), _halo_prev(tb, dh, 3), _rows(tb, nstate), _rows(tb, nstate),
                  _full(c_re.shape), _full(c_im.shape), _full(v512.shape), _full(convw.shape), _full(glu_w.shape),
                  _full(h16.shape), _full(h64.shape), _full(w_out.shape), _full(vd.shape)],
        out_specs=(_rows(tb, dh), _rows(tb, d), _rows(tb, d)),
        compiler_params=_cparams(("parallel",), VMEM_BIG),
    )(x, proj, proj, proj, proj, proj, proj, s_re, s_im, c_re, c_im, v512, convw, glu_w, h16, h64, w_out, vd)


def _ffn_up(x2, vec, w_up_st):
    t, d = x2.shape
    ns, _, nc = w_up_st.shape
    tb = _blk(t, TB_FFN_UP)

    def body(x_ref, vec_ref, w_ref, up_ref, h2_ref):
        xv = x_ref[...]
        r = lax.rsqrt(_rowmean(xv * xv) + EPS)
        h = xv * r * vec_ref[0:1, :] * vec_ref[1:2, :] + vec_ref[2:3, :]
        hb = h.astype(BF16)
        h2_ref[...] = hb
        for j in range(ns):
            up_ref[:, j * nc:(j + 1) * nc] = jnp.dot(hb, w_ref[j], preferred_element_type=F32)

    return pl.pallas_call(
        body, name="ffn_up", grid=(t // tb,),
        out_shape=(jax.ShapeDtypeStruct((t, ns * nc), F32), jax.ShapeDtypeStruct((t, d), BF16)),
        in_specs=[_rows(tb, d), _full((SUBLANES, d)), _resident(w_up_st.shape)],
        out_specs=(_rows(tb, ns * nc), _rows(tb, d)),
        compiler_params=_cparams(("parallel",), VMEM_BIG),
    )(x2, vec, w_up_st)


def _ffn_down(up, fw, w_down, w_down_t, x2, tgt, vd):
    t, nh = up.shape
    dff, d = w_down.shape
    tb = _blk(t, TB_FFN)
    inv_d = 1.0 / d

    def body(up_ref, uph_ref, fw_ref, wd_ref, wdt_ref, x2_ref, tgt_ref, vd_ref,
             act_ref, ddn_ref, dout_ref, dhid_ref, vec_ref, loss_ref, a_s, vv_s, sg_s):
        i = pl.program_id(0)

        def conv_cols(sl):
            x = up_ref[:, sl]
            halo = jnp.where(i > 0, uph_ref[:, sl], 0.0)
            return (fw_ref[0:1, sl] * _shift_down(x, halo, 2) + fw_ref[1:2, sl] * _shift_down(x, halo, 1)
                    + fw_ref[2:3, sl] * x)

        dn = None
        for o in range(0, dff, CW_FFN):
            sl = slice(o, o + CW_FFN)
            a = conv_cols(sl)
            vv = conv_cols(slice(dff + o, dff + o + CW_FFN))
            sg = _sigmoid(a)
            si = a * sg
            a_s[:, sl] = si
            vv_s[:, sl] = vv
            sg_s[:, sl] = sg
            actb = (si * vv).astype(BF16)
            act_ref[:, sl] = actb
            pj = lax.dot_general(actb, wdt_ref[:, sl], (((1,), (1,)), ((), ())), preferred_element_type=F32)
            dn = pj if dn is None else dn + pj
        r3 = lax.rsqrt(_rowmean(dn * dn) + EPS)
        xn = dn * r3
        g = vd_ref[0:1, :]
        gt2 = vd_ref[1:2, :]
        dnn = xn * g
        diff = x2_ref[...] + gt2 * dnn - tgt_ref[...]
        part = 0.5 * inv_d * jnp.sum(diff * diff)

        @pl.when(i == 0)
        def _():
            loss_ref[...] = jnp.zeros(loss_ref.shape, F32)
        loss_ref[...] += part
        dout = diff * inv_d
        dout_ref[...] = dout
        ddnn = dout * gt2
        _acc_rows(vec_ref, i == 0, [_colsum(dout * dnn), _colsum(ddnn * xn)])
        dxn = ddnn * g
        ddn = r3 * (dxn - xn * _rowmean(dxn * xn))
        ddnb = ddn.astype(BF16)
        ddn_ref[...] = ddnb
        for o in range(0, dff, CW_FFN):
            sl = slice(o, o + CW_FFN)
            dact = lax.dot_general(ddnb, wd_ref[sl, :], (((1,), (1,)), ((), ())), preferred_element_type=F32)
            si, vv, sg = a_s[:, sl], vv_s[:, sl], sg_s[:, sl]
            dhid_ref[:, sl] = (dact * vv * (sg + si * (1.0 - sg))).astype(BF16)
            dhid_ref[:, dff + o:dff + o + CW_FFN] = (dact * si).astype(BF16)

    return pl.pallas_call(
        body, name="ffn_down", grid=(t // tb,),
        scratch_shapes=[pltpu.VMEM((tb, dff), F32)] * 3,
        out_shape=(jax.ShapeDtypeStruct((t, dff), BF16), jax.ShapeDtypeStruct((t, d), BF16),
                   jax.ShapeDtypeStruct((t, d), F32), jax.ShapeDtypeStruct((t, nh), BF16),
                   jax.ShapeDtypeStruct((SUBLANES, d), F32), jax.ShapeDtypeStruct((SUBLANES, 128), F32)),
        in_specs=[_rows(tb, nh), _halo_prev(tb, nh), _full(fw.shape), _resident(w_down.shape),
                  _resident(w_down_t.shape), _rows(tb, d),
                  _rows(tb, d), _full(vd.shape)],
        out_specs=(_rows(tb, dff), _rows(tb, d), _rows(tb, d), _rows(tb, nh), _full((SUBLANES, d)),
                   _full((SUBLANES, 128))),
        compiler_params=_cparams(("arbitrary",), VMEM_BIG),
    )(up, up, fw, w_down, w_down_t, x2, tgt, vd)


def _ffn_up_bwd(dhid, up, fw, x2, dout, vec, w_up_st):
    t, nh = dhid.shape
    d = x2.shape[1]
    ns, _, nc = w_up_st.shape
    tb = _blk(t, TB_FFN)
    nblk = t // tb
    cw = 128

    def body(dh_ref, dhn_ref, up_ref, fw_ref, x2_ref, dout_ref, vec_ref, w_ref,
             dx2_ref, dup_ref, vp_ref, df_ref):
        i = pl.program_id(0)

        @pl.when(i == 0)
        def _():
            df_ref[...] = jnp.zeros(df_ref.shape, F32)
        dh2 = None
        for j in range(ns):
            for o in range(j * nc, (j + 1) * nc, cw):
                sl = slice(o, o + cw)
                dh = dh_ref[:, sl].astype(F32)
                dhn = jnp.where(i < nblk - 1, dhn_ref[:, sl].astype(F32), 0.0)
                dh1 = _shift_up(dh, dhn, 1)
                dh2s = _shift_up(dh, dhn, 2)
                dup_ref[:, sl] = (fw_ref[2:3, sl] * dh + fw_ref[1:2, sl] * dh1 + fw_ref[0:1, sl] * dh2s).astype(BF16)
                up_v = up_ref[:, sl]
                df_ref[0:1, sl] += _colsum(dh2s * up_v)
                df_ref[1:2, sl] += _colsum(dh1 * up_v)
                df_ref[2:3, sl] += _colsum(dh * up_v)
            pj = lax.dot_general(dup_ref[:, j * nc:(j + 1) * nc], w_ref[j], (((1,), (1,)), ((), ())),
                                 preferred_element_type=F32)
            dh2 = pj if dh2 is None else dh2 + pj
        xv = x2_ref[...]
        r = lax.rsqrt(_rowmean(xv * xv) + EPS)
        xn = xv * r
        g = vec_ref[0:1, :]
        hg = xn * g
        dhg = dh2 * vec_ref[1:2, :]
        _acc_rows(vp_ref, i == 0, [_colsum(dh2), _colsum(dh2 * hg), _colsum(dhg * xn)])
        dxn = dhg * g
        dx2_ref[...] = dout_ref[...] + r * (dxn - xn * _rowmean(dxn * xn))

    return pl.pallas_call(
        body, name="ffn_up_bwd", grid=(nblk,),
        out_shape=(jax.ShapeDtypeStruct((t, d), F32), jax.ShapeDtypeStruct((t, nh), BF16),
                   jax.ShapeDtypeStruct((SUBLANES, d), F32), jax.ShapeDtypeStruct((SUBLANES, nh), F32)),
        in_specs=[_rows(tb, nh), _halo_next(tb, nh, t, rows=BF16_ROWS), _rows(tb, nh), _full(fw.shape),
                  _rows(tb, d), _rows(tb, d), _full(vec.shape), _resident(w_up_st.shape)],
        out_specs=(_rows(tb, d), _rows(tb, nh), _full((SUBLANES, d)), _full((SUBLANES, nh))),
        compiler_params=_cparams(("arbitrary",), VMEM_BIG),
    )(dhid, dhid, up, fw, x2, dout, vec, w_up_st)


def _mix_out_bwd(dx2, o, y1, proj, s_re, s_im, c_re, c_im, v512, convw, glu_w, h16, h64, w_out, vd):
    t, d = dx2.shape
    dh = y1.shape[1]
    nstate = c_re.shape[0]
    du, ds = dh // SSM_SPLIT, nstate // SSM_SPLIT
    tb = _blk(t, TB_MIX)

    def body(dx2_ref, o_ref, y1_ref, u_ref, bg_ref, cg_ref, v_ref, cgh_ref, vh_ref, cre_ref, cim_ref, p_ref,
             cw_ref, gw_ref, h16_ref, h64_ref, wo_ref, vd_ref, sr_ref, si_ref,
             do_ref, ycat_ref, z_ref, dq_ref, dy1_ref, gr_ref, gi_ref, dcc_ref, dbg_ref, vpd_ref, vp5_ref,
             dcr_ref, dci_ref):
        i = pl.program_id(0)
        first = i == 0

        @pl.when(first)
        def _():
            dcr_ref[...] = jnp.zeros(dcr_ref.shape, F32)
            dci_ref[...] = jnp.zeros(dci_ref.shape, F32)
        ov = o_ref[...]
        ro = lax.rsqrt(_rowmean(ov * ov) + EPS)
        on_ = ov * ro
        g = vd_ref[0:1, :]
        dx2v = dx2_ref[...]
        don = dx2v * vd_ref[1:2, :]
        _acc_rows(vpd_ref, first, [_colsum(dx2v * on_ * g), _colsum(don * on_)])
        dxn = don * g
        dob = (ro * (dxn - on_ * _rowmean(dxn * on_))).astype(BF16)
        do_ref[...] = dob
        dyc_a =lax.dot_general(dob, wo_ref[0:dh, :], (((1,), (1,)), ((), ())), preferred_element_type=F32)
        dyc_b = lax.dot_general(dob, wo_ref[dh:2 * dh, :], (((1,), (1,)), ((), ())), preferred_element_type=F32)
        y1v = y1_ref[...]
        u = u_ref[...]
        z, dz_dy1 = _gelu_and_grad(y1v)
        zb = z.astype(BF16)
        sg = _sigmoid(jnp.dot(zb, gw_ref[...], preferred_element_type=F32) + p_ref[1:2, :])
        ya = z * sg
        ra = lax.rsqrt(_head_ms(ya, h16_ref) + EPS)
        yan = ya * ra
        ga = p_ref[2:3, :]
        ycat_ref[:, 0:dh] = (yan * ga).astype(BF16)
        dyn = dyc_a * ga
        dya = ra * (dyn - yan * _split_dot(dyn * yan, h16_ref[...]))
        dq = dya * z * sg * (1.0 - sg)
        dqb = dq.astype(BF16)
        z_ref[...] = zb
        dq_ref[...] = dqb
        dz = dya * sg + lax.dot_general(dqb, gw_ref[...], (((1,), (1,)), ((), ())), preferred_element_type=F32)
        dy1 = dz * dz_dy1
        dy1_ref[...] = dy1
        dy1b = dy1.astype(BF16)
        for q in range(SSM_SPLIT):
            rq, cq = slice(q * ds, (q + 1) * ds), slice(q * du, (q + 1) * du)
            gr_ref[:, rq] = lax.dot_general(dy1b[:, cq], cre_ref[rq, cq], (((1,), (1,)), ((), ())),
                                            preferred_element_type=F32)
            gi_ref[:, rq] = -lax.dot_general(dy1b[:, cq], cim_ref[rq, cq], (((1,), (1,)), ((), ())),
                                             preferred_element_type=F32)
            dcr_ref[rq, :] += _dot_tn(sr_ref[:, rq], dy1b[:, cq])
            dci_ref[rq, :] += _dot_tn(si_ref[:, rq], dy1b[:, cq])
        bg = bg_ref[...]
        cv = cg_ref[...] * v_ref[...]
        cvh = jnp.where(i > 0, cgh_ref[...] * vh_ref[...], 0.0)
        cv1 = _shift_down(cv, cvh, 1)
        cv2 = _shift_down(cv, cvh, 2)
        cc = cw_ref[0:1, :] * cv2 + cw_ref[1:2, :] * cv1 + cw_ref[2:3, :] * cv
        yb = bg * cc
        rb = lax.rsqrt(_head_ms(yb, h64_ref) + EPS)
        ybn = yb * rb
        gb = p_ref[3:4, :]
        ycat_ref[:, dh:2 * dh] = (ybn * gb).astype(BF16)
        dynb = dyc_b * gb
        dyb = rb * (dynb - ybn * _split_dot(dynb * ybn, h64_ref[...]))
        dcc = dyb * bg
        dbg_ref[...] = dyb * cc
        dcc_ref[...] = dcc
        _acc_rows(vp5_ref, first, [_colsum(dyc_a * yan), _colsum(dyc_b * ybn), _colsum(dq), _colsum(dy1 * u),
                                   _colsum(dcc * cv2), _colsum(dcc * cv1), _colsum(dcc * cv)])

    return pl.pallas_call(
        body, name="mix_out_bwd", grid=(t // tb,),
        out_shape=(jax.ShapeDtypeStruct((t, d), BF16), jax.ShapeDtypeStruct((t, 2 * dh), BF16),
                   jax.ShapeDtypeStruct((t, dh), BF16), jax.ShapeDtypeStruct((t, dh), BF16),
                   jax.ShapeDtypeStruct((t, dh), F32), jax.ShapeDtypeStruct((t, nstate), F32),
                   jax.ShapeDtypeStruct((t, nstate), F32), jax.ShapeDtypeStruct((t, dh), F32),
                   jax.ShapeDtypeStruct((t, dh), F32), jax.ShapeDtypeStruct((SUBLANES, d), F32),
                   jax.ShapeDtypeStruct((SUBLANES, dh), F32), jax.ShapeDtypeStruct((nstate, du), F32),
                   jax.ShapeDtypeStruct((nstate, du), F32)),
        in_specs=[_rows(tb, d), _rows(tb, d), _rows(tb, dh), _rows(tb, dh, 0), _rows(tb, dh, 1), _rows(tb, dh, 2),
                  _rows(tb, dh, 3), _halo_prev(tb, dh, 2), _halo_prev(tb, dh, 3), _resident(c_re.shape),
                  _resident(c_im.shape), _full(v512.shape), _full(convw.shape), _resident(glu_w.shape),
                  _resident(h16.shape), _resident(h64.shape), _resident(w_out.shape), _full(vd.shape),
                  _rows(tb, nstate), _rows(tb, nstate)],
        out_specs=(_rows(tb, d), _rows(tb, 2 * dh), _rows(tb, dh), _rows(tb, dh), _rows(tb, dh), _rows(tb, nstate),
                   _rows(tb, nstate), _rows(tb, dh), _rows(tb, dh), _full((SUBLANES, d)), _full((SUBLANES, dh)),
                   _full((nstate, du)), _full((nstate, du))),
        compiler_params=_cparams(("arbitrary",), VMEM_BIG),
    )(dx2, o, y1, proj, proj, proj, proj, proj, proj, c_re, c_im, v512, convw, glu_w, h16, h64, w_out, vd,
      s_re, s_im)


def _mix_in_bwd(gt_re, gt_im, b_re, b_im, dy1, dcc, dbg, proj, x, dx2, vec, v512, convw, w_in_st):
    t, d = x.shape
    dh = dy1.shape[1]
    nstate = gt_re.shape[1]
    du_w, ds = dh // SSM_SPLIT, nstate // SSM_SPLIT
    ns, _, nc = w_in_st.shape
    tb = _blk(t, TB_MIX)
    nblk = t // tb

    def body(gr_ref, gi_ref, bre_ref, bim_ref, dy1_ref, dcc_ref, dccn_ref, dbg_ref, u_ref, cg_ref, v_ref, x_ref,
             dx2_ref, vec_ref, p_ref, cw_ref, w_ref, gx_ref, dproj_ref, vp_ref, dbr_ref, dbi_ref):
        i = pl.program_id(0)

        @pl.when(i == 0)
        def _():
            dbr_ref[...] = jnp.zeros(dbr_ref.shape, F32)
            dbi_ref[...] = jnp.zeros(dbi_ref.shape, F32)
        ub = u_ref[...].astype(BF16)
        du = []
        for q in range(SSM_SPLIT):
            rq, cq = slice(q * du_w, (q + 1) * du_w), slice(q * ds, (q + 1) * ds)
            du.append(lax.dot_general(gr_ref[:, cq].astype(BF16), bre_ref[rq, cq], (((1,), (1,)), ((), ())),
                                      preferred_element_type=F32)
                      + lax.dot_general(gi_ref[:, cq].astype(BF16), bim_ref[rq, cq], (((1,), (1,)), ((), ())),
                                        preferred_element_type=F32))
            dbr_ref[rq, :] += _dot_tn(ub[:, rq], gr_ref[:, cq])
            dbi_ref[rq, :] += _dot_tn(ub[:, rq], gi_ref[:, cq])
        du = dy1_ref[...] * p_ref[0:1, :] + jnp.concatenate(du, axis=1)
        dcc = dcc_ref[...]
        dccn = jnp.where(i < nblk - 1, dccn_ref[...], 0.0)
        dcv = (cw_ref[2:3, :] * dcc + cw_ref[1:2, :] * _shift_up(dcc, dccn, 1)
               + cw_ref[0:1, :] * _shift_up(dcc, dccn, 2))
        parts = [du, dbg_ref[...], dcv * v_ref[...], dcv * cg_ref[...]]
        xv = x_ref[...]
        r = lax.rsqrt(_rowmean(xv * xv) + EPS)
        xn = xv * r
        g = vec_ref[0:1, :]
        hg = xn * g
        dh1 = None
        for j in range(ns):
            pb = parts[j].astype(BF16)
            dproj_ref[:, j * nc:(j + 1) * nc] = pb
            pj =lax.dot_general(pb, w_ref[j], (((1,), (1,)), ((), ())), preferred_element_type=F32)
            dh1 = pj if dh1 is None else dh1 + pj
        dhg = dh1 * vec_ref[1:2, :]
        _acc_rows(vp_ref, i == 0, [_colsum(dh1), _colsum(dh1 * hg), _colsum(dhg * xn)])
        dxn = dhg * g
        gx_ref[...] = dx2_ref[...] + r * (dxn - xn * _rowmean(dxn * xn))

    assert nc == dh and ns == 4
    return pl.pallas_call(
        body, name="mix_in_bwd", grid=(nblk,),
        out_shape=(jax.ShapeDtypeStruct((t, d), F32), jax.ShapeDtypeStruct((t, ns * nc), BF16),
                   jax.ShapeDtypeStruct((SUBLANES, d), F32), jax.ShapeDtypeStruct((dh, ds), F32),
                   jax.ShapeDtypeStruct((dh, ds), F32)),
        in_specs=[_rows(tb, nstate), _rows(tb, nstate), _resident(b_re.shape), _resident(b_im.shape), _rows(tb, dh),
                  _rows(tb, dh), _halo_next(tb, dh, t), _rows(tb, dh), _rows(tb, dh, 0), _rows(tb, dh, 2),
                  _rows(tb, dh, 3), _rows(tb, d), _rows(tb, d), _full(vec.shape), _full(v512.shape),
                  _full(convw.shape), _resident(w_in_st.shape)],
        out_specs=(_rows(tb, d), _rows(tb, ns * nc), _full((SUBLANES, d)), _full((dh, ds)), _full((dh, ds))),
        compiler_params=_cparams(("arbitrary",), VMEM_BIG),
    )(gt_re, gt_im, b_re, b_im, dy1, dcc, dcc, dbg, proj, proj, proj, x, dx2, vec, v512, convw, w_in_st)


def _matmul_tn(a, b, m, bn, out_dtype, name, diag=False, bt=TB_TN, after=None):
    t = a.shape[0]
    n = b.shape[1]
    bt = _blk(t, bt)
    nk = t // bt
    extra = [] if after is None else [after]
    a_map = (lambda j, k: (k, j)) if diag else (lambda j, k: (k, 0))

    def body(a_ref, b_ref, *rest):
        o_ref, acc_ref = rest[-2:]
        k = pl.program_id(1)

        @pl.when(k == 0)
        def _():
            acc_ref[...] = jnp.zeros(acc_ref.shape, F32)
        acc_ref[...] += _dot_tn(a_ref[...], b_ref[...])

        @pl.when(k == nk - 1)
        def _():
            o_ref[...] = acc_ref[...].astype(out_dtype)

    return pl.pallas_call(
        body, name=name, grid=(n // bn, nk),
        out_shape=jax.ShapeDtypeStruct((n // bn, m, bn), out_dtype),
        in_specs=[pl.BlockSpec((bt, m), a_map), pl.BlockSpec((bt, bn), lambda j, k: (k, j))]
        + [pl.BlockSpec(memory_space=pl.ANY)] * len(extra),
        out_specs=pl.BlockSpec((None, m, bn), lambda j, k: (j, 0, 0)),
        scratch_shapes=[pltpu.VMEM((m, bn), F32)],
        compiler_params=_cparams(("parallel", "arbitrary"), VMEM_BIG),
    )(a, b, *extra)


def _ssm_bgrad(d_bre, d_bim, bt_re, bt_im, rows_in, fold, tile_b):
    gh, cb = d_bre.shape
    nb = SSM_SPLIT
    rb = gh // nb
    gp = nb * cb
    p = fold.shape[1]

    def body(dr_ref, di_ref, br_ref, bi_ref, rin_ref, f_ref, tb_ref, dbr_ref, dbi_ref, rout_ref):
        row = lax.broadcasted_iota(jnp.int32, (rb, cb), 0)
        col = lax.broadcasted_iota(jnp.int32, (rb, cb), 1)
        mask = (row >> 4) == (col >> 6)
        gr = jnp.where(mask, dr_ref[...], 0.0)
        gi = jnp.where(mask, di_ref[...], 0.0)
        cr, ci = rin_ref[0:1, :], rin_ref[1:2, :]
        dbr_ref[...] = _split3_dot(cr * gr + ci * gi, f_ref[...])
        dbi_ref[...] = _split3_dot(cr * gi - ci * gr, f_ref[...])
        br = _split3_dot(br_ref[...], tb_ref[...])
        bi = _split3_dot(bi_ref[...], tb_ref[...])
        rout_ref[...] = jnp.zeros(rout_ref.shape, F32)
        rout_ref[0:1, :] = _colsum(br * gr + bi * gi)
        rout_ref[1:2, :] = _colsum(br * gi - bi * gr)

    dspec = pl.BlockSpec((rb, cb), lambda j: (j, 0))
    rspec = pl.BlockSpec((SUBLANES, cb), lambda j: (0, j))
    ospec = pl.BlockSpec((rb, p), lambda j: (j, 0))
    return pl.pallas_call(
        body, name="ssm_bgrad", grid=(nb,),
        out_shape=(jax.ShapeDtypeStruct((gh, p), F32), jax.ShapeDtypeStruct((gh, p), F32),
                   jax.ShapeDtypeStruct((SUBLANES, gp), F32)),
        in_specs=[dspec, dspec, ospec, ospec, rspec, _full(fold.shape), _full(tile_b.shape)],
        out_specs=(ospec, ospec, rspec),
        compiler_params=_cparams(("parallel",)),
    )(d_bre, d_bim, bt_re, bt_im, rows_in, fold, tile_b)


def _ssm_cgrad(d_cre, d_cim, fold):
    gp, cb = d_cre.shape
    nb = SSM_SPLIT
    rb = gp // nb
    h = fold.shape[1]

    def body(dr_ref, di_ref, f_ref, cr_ref, ci_ref):
        row = lax.broadcasted_iota(jnp.int32, (rb, cb), 0)
        col = lax.broadcasted_iota(jnp.int32, (rb, cb), 1)
        mask = (row >> 6) == (col >> 4)
        cr_ref[...] = _split3_dot(jnp.where(mask, dr_ref[...], 0.0), f_ref[...])
        ci_ref[...] = -_split3_dot(jnp.where(mask, di_ref[...], 0.0), f_ref[...])

    cspec = pl.BlockSpec((rb, cb), lambda j: (j, 0))
    ospec = pl.BlockSpec((rb, h), lambda j: (j, 0))
    return pl.pallas_call(
        body, name="ssm_cgrad", grid=(nb,),
        out_shape=(jax.ShapeDtypeStruct((gp, h), F32),) * 2,
        in_specs=[cspec, cspec, _full(fold.shape)], out_specs=(ospec, ospec),
        compiler_params=_cparams(("parallel",)),
    )(d_cre, d_cim, fold)


def _ssm_lamgrad(lam_re, lam_im, log_step, abar_re, abar_im, coef_re, coef_im, gc_re, gc_im, ga_re, ga_im):
    g, p = lam_re.shape

    def body(lr_ref, li_ref, ls_ref, ar_ref, ai_ref, cr_ref, ci_ref, gcr_ref, gci_ref, gar_ref, gai_ref,
             dlr_ref, dli_ref, dls_ref):
        lam_raw = lr_ref[...]
        lr = jnp.minimum(lam_raw, LAMBDA_RE_MAX)
        li = li_ref[...]
        st = jnp.exp(ls_ref[...])
        den = lr * lr + li * li
        gcr, gci = gcr_ref[...], gci_ref[...]
        gab_r = gar_ref[...] + (lr * gcr - li * gci) / den
        gab_i = gai_ref[...] + (lr * gci + li * gcr) / den
        cr, ci = cr_ref[...], ci_ref[...]
        wr = -(cr * lr + ci * li) / den
        wi = -(ci * lr - cr * li) / den
        gl_r = wr * gcr + wi * gci
        gl_i = wr * gci - wi * gcr
        ar, ai = ar_ref[...], ai_ref[...]
        gw_r = ar * gab_r + ai * gab_i
        gw_i = ar * gab_i - ai * gab_r
        gl_r = gl_r + st * gw_r
        gl_i = gl_i + st * gw_i
        pass_through = jnp.where(lam_raw < LAMBDA_RE_MAX, 1.0, jnp.where(lam_raw == LAMBDA_RE_MAX, 0.5, 0.0))
        dlr_ref[...] = gl_r * pass_through
        dli_ref[...] = gl_i
        dls_ref[...] = st * jnp.sum(lr * gw_r + li * gw_i, axis=1, keepdims=True)

    sds = jax.ShapeDtypeStruct((g, p), F32)
    return pl.pallas_call(body, name="ssm_lamgrad", out_shape=(sds, sds, jax.ShapeDtypeStruct((g, 1), F32)))(
        lam_re, lam_im, log_step, abar_re, abar_im, coef_re, coef_im, gc_re, gc_im, ga_re, ga_im)


def _row_block(r, most=256):
    for rb in range(min(r, most), BF16_ROWS - 1, -1):
        if r % rb == 0 and rb % BF16_ROWS == 0:
            return rb
    return r


def _adamw_math(w, g, m, v):
    m = ADAM_B1 * m + (1.0 - ADAM_B1) * g
    v = ADAM_B2 * v + (1.0 - ADAM_B2) * (g * g)
    m_hat = m / (1.0 - ADAM_B1 ** ADAM_STEP)
    v_hat = v / (1.0 - ADAM_B2 ** ADAM_STEP)
    delta = -ADAM_LR * (m_hat / (jnp.sqrt(v_hat) + ADAM_EPS) + ADAM_WD * w)
    return delta, m, v


def _adamw_big(p_mine, p_sib, w, m, v, name):
    r, c = w.shape
    rb = _row_block(r)

    def body(a_ref, b_ref, w_ref, m_ref, v_ref, g_ref, d_ref, mo_ref, vo_ref):
        g = a_ref[...] + b_ref[...]
        g_ref[...] = g
        d_ref[...], mo_ref[...], vo_ref[...] = _adamw_math(w_ref[...], g, m_ref[...], v_ref[...])

    spec = pl.BlockSpec((rb, c), lambda i: (i, 0))
    sds = jax.ShapeDtypeStruct((r, c), F32)
    return pl.pallas_call(
        body, name=name, grid=(r // rb,), out_shape=(sds,) * 4, in_specs=[spec] * 5, out_specs=(spec,) * 4,
        compiler_params=_cparams(("parallel",), VMEM_KEEP_OPERANDS_IN_HBM),
    )(p_mine, p_sib, w, m, v)


def _sum_blocks(stack, name):
    n, r, c = stack.shape
    rb = _row_block(r)

    def body(s_ref, o_ref):
        acc = s_ref[0].astype(F32)
        for k in range(1, n):
            acc = acc + s_ref[k].astype(F32)
        o_ref[...] = acc

    return pl.pallas_call(
        body, name=name, grid=(r // rb,), out_shape=jax.ShapeDtypeStruct((r, c), F32),
        in_specs=[pl.BlockSpec((n, rb, c), lambda i: (0, i, 0))], out_specs=pl.BlockSpec((rb, c), lambda i: (i, 0)),
        compiler_params=_cparams(("parallel",), VMEM_KEEP_OPERANDS_IN_HBM),
    )(stack)


def _sum_landed(landed, own, chip, name):
    n, r, c = landed.shape
    rb = _row_block(r)

    def body(chip_ref, own_ref, l1_ref, l2_ref, l3_ref, o_ref):
        acc = own_ref[0].astype(F32)
        for ref in (l1_ref, l2_ref, l3_ref):
            acc = acc + ref[0].astype(F32)
        o_ref[...] = acc

    def slot(k):
        return pl.BlockSpec((1, rb, c), lambda i, ch: ((ch[0] + k) % n, i, 0))

    return pl.pallas_call(
        body, name=name, out_shape=jax.ShapeDtypeStruct((r, c), F32),
        grid_spec=pltpu.PrefetchScalarGridSpec(
            num_scalar_prefetch=1, grid=(r // rb,), in_specs=[slot(0), slot(1), slot(2), slot(3)],
            out_specs=pl.BlockSpec((rb, c), lambda i, ch: (i, 0))),
        compiler_params=_cparams(("parallel",), VMEM_KEEP_OPERANDS_IN_HBM),
    )(jnp.reshape(chip, (1,)).astype(jnp.int32), own, landed, landed, landed)


def _add2(a, b):
    def body(a_ref, b_ref, o_ref):
        o_ref[...] = a_ref[...] + b_ref[...]

    return pl.pallas_call(body, name="add_small", out_shape=jax.ShapeDtypeStruct(a.shape, F32))(a, b)


def _adamw_ada(c_all, dmod_cols, w, m, v):
    d, n = w.shape
    bn = 512

    def body(c_ref, dm_ref, w_ref, m_ref, v_ref, g_ref, d_ref, mo_ref, vo_ref):
        cc = c_ref[...]
        g = _dot_tn(cc * _sigmoid(cc), dm_ref[...])
        g_ref[...] = g
        d_ref[...], mo_ref[...], vo_ref[...] = _adamw_math(w_ref[...], g, m_ref[...], v_ref[...])

    spec = pl.BlockSpec((d, bn), lambda j: (0, j))
    sds = jax.ShapeDtypeStruct((d, n), F32)
    return pl.pallas_call(
        body, name="adamw_ada", grid=(n // bn,), out_shape=(sds,) * 4,
        in_specs=[_full((N_DEV, d)), pl.BlockSpec((N_DEV, bn), lambda j: (0, j)), spec, spec, spec],
        out_specs=(spec,) * 4, compiler_params=_cparams(("parallel",), VMEM_KEEP_OPERANDS_IN_HBM),
    )(c_all, dmod_cols, w, m, v)


def _adamw_small(items):
    n = len(items)

    def body(*refs):
        ins, outs = refs[:4 * n], refs[4 * n:]
        for k in range(n):
            w_ref, g_ref, m_ref, v_ref = ins[4 * k:4 * k + 4]
            outs[3 * k][...], outs[3 * k + 1][...], outs[3 * k + 2][...] = _adamw_math(
                w_ref[...], g_ref[...], m_ref[...], v_ref[...])

    flat = [a for it in items for a in it]
    out_shape = tuple(jax.ShapeDtypeStruct(it[0].shape, F32) for it in items for _ in range(3))
    res = pl.pallas_call(body, name="adamw_small", out_shape=out_shape,
                         compiler_params=_cparams(vmem=VMEM_KEEP_OPERANDS_IN_HBM))(*flat)
    return [tuple(res[3 * k:3 * k + 3]) for k in range(n)]


def _group_mean_matrix(n, group):
    idx = np.arange(n) // group
    return (idx[:, None] == idx[None, :]).astype(np.float32) / group


def _fold_matrix(n, period):
    return (np.arange(n)[:, None] % period == np.arange(period)[None, :]).astype(np.float32)


def _rows8(*rows):
    c = rows[0].shape[-1]
    pad = jnp.zeros((SUBLANES - len(rows), c), F32)
    return jnp.concatenate([r.reshape(1, c) for r in rows] + [pad], axis=0)


def _to_rows(a, width):
    flat = a.reshape(-1)
    n = -(-flat.shape[0] // width)
    flat = jnp.pad(flat, (0, n * width - flat.shape[0]))
    return flat.reshape(n, width)


def kernel(x, c, w_ada, b_ada, g_pre_mix, g_post_mix, w_in, ssm_lam_re, ssm_lam_im, ssm_log_step, ssm_b_re, ssm_b_im, ssm_c_re, ssm_c_im, ssm_d, glu_w, glu_b, g_out_ssm, conv_w, g_out_conv, w_out, g_pre_ffn, g_post_ffn, w_up, ffn_conv_w, w_down, loss_target, m_w_ada, m_b_ada, m_g_pre_mix, m_g_post_mix, m_w_in, m_ssm_lam_re, m_ssm_lam_im, m_ssm_log_step, m_ssm_b_re, m_ssm_b_im, m_ssm_c_re, m_ssm_c_im, m_ssm_d, m_glu_w, m_glu_b, m_g_out_ssm, m_conv_w, m_g_out_conv, m_w_out, m_g_pre_ffn, m_g_post_ffn, m_w_up, m_ffn_conv_w, m_w_down, v_w_ada, v_b_ada, v_g_pre_mix, v_g_post_mix, v_w_in, v_ssm_lam_re, v_ssm_lam_im, v_ssm_log_step, v_ssm_b_re, v_ssm_b_im, v_ssm_c_re, v_ssm_c_im, v_ssm_d, v_glu_w, v_glu_b, v_g_out_ssm, v_conv_w, v_g_out_conv, v_w_out, v_g_pre_ffn, v_g_post_ffn, v_w_up, v_ffn_conv_w, v_w_down):
    xs = x[0]
    tgt = loss_target[0]
    t, d = xs.shape
    xi, yi, ci = lax.axis_index("x"), lax.axis_index("y"), lax.axis_index("c")
    chip = 2 * xi + yi
    dev = 2 * chip + ci

    n_groups, n_state = ssm_lam_re.shape[1:]
    n_gch = ssm_b_re.shape[3]
    d_ssm = n_groups * n_gch
    gp = n_groups * n_state
    n_ada = w_ada.shape[2]
    d_ff = w_down.shape[1] * N_CHIPS
    n_upc = w_up.shape[2]

    w_names = ("w_in", "glu_w", "w_out", "w_up", "w_down")
    c_gath, _ = _allgather8(jnp.broadcast_to(c, (SUBLANES, d)), SUBLANES, "gather_c")
    c_all = c_gath.reshape(N_DEV, SUBLANES, d)[:, 0, :]

    def pad8(a):
        return jnp.concatenate([a, jnp.zeros((SUBLANES - a.shape[0], a.shape[1]), a.dtype)], axis=0)

    def start(name, arrs, after):
        return _chips_start(name, True, [], [_landing(a, chip) for a in arrs], after)

    w_names = ("w_in", "mod", "conv_w", "ffn_conv_w", "glu_w", "w_out", "w_up", "w_down")
    first = start("weights_start_in", [w_in[0].astype(BF16)], c_gath)
    b_sh = lax.dynamic_slice(b_ada, (0, chip * n_ada), (1, n_ada))
    mod_sh = _mod_shard(c_all + first[4][0:1, 0:1], w_ada[0], b_sh)
    second = start("weights_start_rest", [mod_sh, pad8(conv_w[0]), pad8(ffn_conv_w[0])]
                   + [w[0].astype(BF16) for w in (glu_w, w_out, w_up, w_down)], None)
    w_send, w_recv, w_land = [list(first[k]) + list(second[k]) for k in (0, 1, 3)]
    w_token = second[4]

    def weights(names, after):
        ks = [w_names.index(nm) for nm in names]
        return _chips_wait("weights_wait_" + names[-1], True, [w_send[k] for k in ks], [w_recv[k] for k in ks],
                           [], [w_land[k] for k in ks], after)[1]

    lam_re, lam_im = ssm_lam_re[0], ssm_lam_im[0]
    log_step = ssm_log_step[0].reshape(n_groups, 1) + w_token[0:1, 0:1]
    abar_re, abar_im, coef_re, coef_im = _ssm_prep(lam_re, lam_im, log_step)
    a_rows = _rows8(abar_re.reshape(1, gp), abar_im.reshape(1, gp))
    coef_rows = _rows8(coef_re.reshape(1, gp), coef_im.reshape(1, gp))
    bt_re = ssm_b_re[0].transpose(0, 2, 1).reshape(d_ssm, n_state)
    bt_im = ssm_b_im[0].transpose(0, 2, 1).reshape(d_ssm, n_state)
    ct_re = ssm_c_re[0].transpose(0, 2, 1).reshape(gp, n_gch)
    ct_im = ssm_c_im[0].transpose(0, 2, 1).reshape(gp, n_gch)
    tile_b = jnp.asarray(np.tile(np.eye(n_state), (1, n_groups // SSM_SPLIT)), BF16)
    tile_c = jnp.asarray(np.tile(np.eye(n_gch), (1, n_groups)), BF16)
    bblk_re, bblk_im, cblk_re, cblk_im = _ssm_blocks(bt_re, bt_im, ct_re, ct_im, coef_rows, tile_b, tile_c)

    h16 = jnp.asarray(_group_mean_matrix(d_ssm, n_gch), BF16)
    h64 = jnp.asarray(_group_mean_matrix(d_ssm, CONV_HEAD_DIM), BF16)

    g_mod, g_cw, g_fw, w_in_st = weights(("mod", "conv_w", "ffn_conv_w", "w_in"), bblk_re)
    mod_all = g_mod.transpose(1, 0, 2).reshape(N_DEV, N_CHIPS * n_ada)
    mod = lax.dynamic_slice(mod_all, (dev, 0), (1, N_CHIPS * n_ada))
    sh1, sc1, gt1, sh2, sc2, gt2 = [mod[:, k * d:(k + 1) * d] for k in range(6)]
    convw_full = pad8(g_cw[:, :3, :].transpose(1, 0, 2).reshape(3, d_ssm))
    fw_full = pad8(g_fw[:, :3, :].transpose(1, 0, 2).reshape(3, N_CHIPS * n_upc))

    v512 = _rows8(ssm_d, glu_b, g_out_ssm, g_out_conv)
    vec1 =_rows8(g_pre_mix, 1.0 + sc1, sh1)
    vd1 = _rows8(g_post_mix, gt1)
    vec2 = _rows8(g_pre_ffn, 1.0 + sc2, sh2)
    vd2 = _rows8(g_post_ffn, gt2)

    proj, bu_re, bu_im, h1b = _mix_in(xs, vec1, w_in_st, bblk_re, bblk_im)
    s_re, s_im = _scan_fwd(a_rows, bu_re, bu_im)
    g_glu, g_wout = weights(("glu_w", "w_out"), s_re)
    glu_full = g_glu.reshape(d_ssm, d_ssm)
    w_out_full = g_wout.reshape(2 * d_ssm, d)
    y1, o_mix, x2 = _mix_out(xs, proj, s_re, s_im, cblk_re, cblk_im, v512, convw_full, glu_full, h16, h64,
                             w_out_full, vd1)
    (w_up_st,) = weights(("w_up",), x2)
    up, h2b = _ffn_up(x2, vec2, w_up_st)
    (g_wdown,) = weights(("w_down",), up)
    w_down_full = g_wdown.reshape(d_ff, d)
    actb, ddnb, dout, dhid, vp_dn, loss_blk = _ffn_down(up, fw_full, w_down_full, w_down_full.T, x2, tgt, vd2)

    gw_down = _matmul_tn(actb, ddnb, d_ff, d, BF16, "dw_down", bt=1024).reshape(N_CHIPS, d_ff // N_CHIPS, d)
    dx2, dupb, vp_up, df_rows = _ffn_up_bwd(dhid, up, fw_full, x2, dout, vec2, w_up_st)
    gw_up = _matmul_tn(h2b, dupb, d, n_upc, BF16, "dw_up", bt=2048)
    ga_send, ga_recv, ga_src, ga_land, ga_token = _chips_start(
        "grads_start_ffn", False, [gw_down, gw_up], [lax.empty(g.shape, g.dtype) for g in (gw_down, gw_up)])
    (dob, ycatb, zb, dqb, dy1, g_re, g_im, dcc, dbg, vp_mo, vp5, d_cre, d_cim) = _mix_out_bwd(
        dx2, o_mix, y1, proj, s_re, s_im, cblk_re, cblk_im, v512, convw_full, glu_full, h16, h64, w_out_full,
        vd1 + ga_token[0:1, 0:1])
    gw_out = _matmul_tn(ycatb, dob, 2 * d_ssm, d, BF16, "dw_out", bt=2048)
    gw_out = gw_out.reshape(N_CHIPS, 2 * d_ssm // N_CHIPS, d)
    gw_glu = _matmul_tn(zb, dqb, d_ssm, d_ssm, BF16, "dw_glu", bt=2048).reshape(N_CHIPS, d_ssm // N_CHIPS, d_ssm)
    gb_send, gb_recv, gb_src, gb_land, gb_token = _chips_start(
        "grads_start_mix", False, [gw_out, gw_glu], [lax.empty(g.shape, g.dtype) for g in (gw_out, gw_glu)])
    gt_re, gt_im, ga_re8, ga_im8 = _scan_bwd(a_rows + gb_token[0:1, 0:1], g_re, g_im, s_re, s_im)
    grad_x, dprojb, vp_mi, d_bre, d_bim = _mix_in_bwd(gt_re, gt_im, bblk_re, bblk_im, dy1, dcc, dbg, proj, xs, dx2,
                                                      vec1, v512, convw_full, w_in_st)
    ssm_u, ssm_s = d_ssm // SSM_SPLIT, gp // SSM_SPLIT

    fold_b = jnp.asarray(_fold_matrix(ssm_s, n_state), BF16)
    fold_c = jnp.asarray(_fold_matrix(ssm_u, n_gch), BF16)
    db_re_f, db_im_f, gc_rows = _ssm_bgrad(d_bre, d_bim, bt_re, bt_im, coef_rows, fold_b, tile_b)
    dc_re_f, dc_im_f = _ssm_cgrad(d_cre, d_cim, fold_c)
    ga_sum = _ga_rowsum(ga_re8, ga_im8)
    g_lam_re, g_lam_im, g_log_step = _ssm_lamgrad(
        lam_re, lam_im, log_step, abar_re, abar_im, coef_re, coef_im,
        gc_rows[0].reshape(n_groups, n_state), gc_rows[1].reshape(n_groups, n_state),
        ga_sum[0].reshape(n_groups, n_state), ga_sum[1].reshape(n_groups, n_state))
    g_b_re = db_re_f.reshape(n_groups, n_gch, n_state).transpose(0, 2, 1)
    g_b_im = db_im_f.reshape(n_groups, n_gch, n_state).transpose(0, 2, 1)
    g_c_re = dc_re_f.reshape(n_groups, n_state, n_gch).transpose(0, 2, 1)
    g_c_im = dc_im_f.reshape(n_groups, n_state, n_gch).transpose(0, 2, 1)

    dmod = jnp.concatenate([vp_mi[0:1], vp_mi[1:2], vp_mo[0:1], vp_up[0:1], vp_up[1:2], vp_dn[0:1]], axis=1)
    small = [
        ("g_pre_mix", vp_mi[2:3]), ("g_post_mix", vp_mo[1:2]), ("g_pre_ffn", vp_up[2:3]), ("g_post_ffn", vp_dn[1:2]),
        ("ssm_lam_re", g_lam_re), ("ssm_lam_im", g_lam_im), ("ssm_log_step", g_log_step),
        ("ssm_b_re", g_b_re), ("ssm_b_im", g_b_im), ("ssm_c_re", g_c_re), ("ssm_c_im", g_c_im),
        ("ssm_d", vp5[3:4]), ("glu_b", vp5[2:3]), ("g_out_ssm", vp5[0:1]), ("g_out_conv", vp5[1:2]),
        ("conv_w", vp5[4:7]), ("ffn_conv_w", df_rows[0:3]), ("loss", loss_blk[0:1, 0:1]),
    ]
    packed, offsets, row = [], {}, 0
    for name, a in small:
        r = _to_rows(a, d)
        offsets[name] = (row, a.shape)
        packed.append(r)
        row += r.shape[0]
    n_small = -(-row // SUBLANES) * SUBLANES
    packed.append(jnp.zeros((n_small - row, d), F32))
    packed.append(pad8(dmod.reshape(6, d)))
    pack = jnp.concatenate(packed, axis=0)
    sm_send, sm_recv, _, sm_land, sm_token = _chips_start("small_start", True, [], [_landing(pack, chip)])

    gw_in = _matmul_tn(h1b, dprojb, d, w_in.shape[2], BF16, "dw_in", bt=2048, after=sm_token)
    gc_send, gc_recv, gc_src, gc_land, gc_token = _chips_start(
        "grads_start_in", False, [gw_in], [lax.empty(gw_in.shape, gw_in.dtype)])

    def partials(names, own, landed):
        return [_sum_landed(l, o, chip, "sum_" + nm) for l, o, nm in zip(landed, own, names)]

    def update(names, mine, theirs):
        done = {}
        for nm, pm, ps in zip(names, mine, theirs):
            w_, m_, v_ = big_params[nm]
            done[nm] = _adamw_big(pm, ps, w_[0], m_[0], v_[0], "adamw_" + nm)
        return done

    big_params = {"w_down": (w_down, m_w_down, v_w_down), "w_up": (w_up, m_w_up, v_w_up),
                  "w_out": (w_out, m_w_out, v_w_out), "glu_w": (glu_w, m_glu_w, v_glu_w),
                  "w_in": (w_in, m_w_in, v_w_in)}
    ffn_names, mix_names = ("w_down", "w_up"), ("w_out", "glu_w", "w_in")
    p_ffn = partials(ffn_names, *_chips_wait("grads_wait_ffn", False, ga_send, ga_recv, ga_src, ga_land, gc_token))
    sa_send, sa_recv, sa_src, sa_land, sa_token = _sibling_start("swap_start_ffn", p_ffn)

    (sm_landed,) = _chips_wait("small_wait", True, sm_send, sm_recv, [], sm_land, sa_token)[1]
    sm_part = _sum_blocks(sm_landed, "sum_small")
    dmod_mine = sm_landed[:, n_small:n_small + SUBLANES, :]
    ss_send, ss_recv, ss_src, ss_land, ss_token = _sibling_start("swap_start_small", [sm_part, dmod_mine])
    p_ffn, t_ffn = _sibling_wait("swap_wait_ffn", sa_send, sa_recv, sa_src, sa_land, ss_token)
    big = update(ffn_names, p_ffn, t_ffn)
    (sm_part, dmod_mine), (sm_sib, dmod_sib) = _sibling_wait("swap_wait_small", ss_send, ss_recv, ss_src, ss_land,
                                                              big["w_up"][0])
    sums = _add2(sm_part, sm_sib)
    dmod_by_core = jnp.stack([dmod_mine, dmod_sib], axis=1)
    dmod_by_core = jnp.where(ci == 0, dmod_by_core, dmod_by_core[:, ::-1])
    dmod_all = dmod_by_core[:, :, :6, :].reshape(N_DEV, 6 * d)
    g_b_ada = sums[n_small:n_small + 6].reshape(1, 6 * d)

    def unpack(name):
        r0, shape = offsets[name]
        size = math.prod(shape)
        nrow = -(-size // d)
        return sums[r0:r0 + nrow].reshape(-1)[:size].reshape(shape)

    p_mix = partials(mix_names, *_chips_wait(
        "grads_wait_mix", False, list(gb_send) + list(gc_send), list(gb_recv) + list(gc_recv),
        list(gb_src) + list(gc_src), list(gb_land) + list(gc_land), sums))
    sb_send, sb_recv, sb_src, sb_land, sb_token = _sibling_start("swap_start_mix", p_mix)

    dmod_cols = lax.dynamic_slice(dmod_all, (0, chip * n_ada), (N_DEV, n_ada)) + sb_token[0:1, 0:1]
    ada = _adamw_ada(c_all, dmod_cols, w_ada[0], m_w_ada[0], v_w_ada[0])
    p_mix, t_mix = _sibling_wait("swap_wait_mix", sb_send, sb_recv, sb_src, sb_land, ada[0])
    big.update(update(mix_names, p_mix, t_mix))

    g_small = {name: unpack(name) for name, _ in small}
    g_small["b_ada"] = g_b_ada
    g_small["conv_w"] = lax.dynamic_slice(g_small["conv_w"], (0, chip * conv_w.shape[2]), (3, conv_w.shape[2]))
    g_small["ffn_conv_w"] = lax.dynamic_slice(g_small["ffn_conv_w"], (0, chip * n_upc), (3, n_upc))
    g_small["ssm_log_step"] = g_small["ssm_log_step"].reshape(1, n_groups)
    small_params = {
        "b_ada": (b_ada, m_b_ada, v_b_ada), "g_pre_mix": (g_pre_mix, m_g_pre_mix, v_g_pre_mix),
        "g_post_mix": (g_post_mix, m_g_post_mix, v_g_post_mix), "ssm_lam_re": (ssm_lam_re, m_ssm_lam_re, v_ssm_lam_re),
        "ssm_lam_im": (ssm_lam_im, m_ssm_lam_im, v_ssm_lam_im),
        "ssm_log_step": (ssm_log_step, m_ssm_log_step, v_ssm_log_step),
        "ssm_b_re": (ssm_b_re, m_ssm_b_re, v_ssm_b_re), "ssm_b_im": (ssm_b_im, m_ssm_b_im, v_ssm_b_im),
        "ssm_c_re": (ssm_c_re, m_ssm_c_re, v_ssm_c_re), "ssm_c_im": (ssm_c_im, m_ssm_c_im, v_ssm_c_im),
        "ssm_d": (ssm_d, m_ssm_d, v_ssm_d), "glu_b": (glu_b, m_glu_b, v_glu_b),
        "g_out_ssm": (g_out_ssm, m_g_out_ssm, v_g_out_ssm), "conv_w": (conv_w, m_conv_w, v_conv_w),
        "g_out_conv": (g_out_conv, m_g_out_conv, v_g_out_conv), "g_pre_ffn": (g_pre_ffn, m_g_pre_ffn, v_g_pre_ffn),
        "g_post_ffn": (g_post_ffn, m_g_post_ffn, v_g_post_ffn),
        "ffn_conv_w": (ffn_conv_w, m_ffn_conv_w, v_ffn_conv_w),
    }

    def natural(a):
        return a[0] if a.ndim > 2 else a

    names = list(small_params)
    items = []
    for nm in names:
        w_, m_, v_ = small_params[nm]
        items.append((natural(w_), g_small[nm].reshape(natural(w_).shape), natural(m_), natural(v_)))
    upd = _adamw_small(items)
    small_out = {}
    for nm, (dl, mo, vo) in zip(names, upd):
        shp = small_params[nm][0].shape
        small_out[nm] = (g_small[nm].reshape(shp), dl.reshape(shp), mo.reshape(shp), vo.reshape(shp))

    loss = g_small["loss"][0, 0]

    order = ["w_ada", "b_ada", "g_pre_mix", "g_post_mix", "w_in", "ssm_lam_re", "ssm_lam_im", "ssm_log_step",
             "ssm_b_re", "ssm_b_im", "ssm_c_re", "ssm_c_im", "ssm_d", "glu_w", "glu_b", "g_out_ssm", "conv_w",
             "g_out_conv", "w_out", "g_pre_ffn", "g_post_ffn", "w_up", "ffn_conv_w", "w_down"]
    results = {"w_ada": tuple(a[None] for a in ada)}
    for nm in big:
        results[nm] = tuple(a[None] for a in big[nm])
    results.update(small_out)
    outs = [loss, grad_x[None]]
    for k in range(4):
        outs += [results[nm][k] for nm in order]
    return tuple(outs)


def _ga_rowsum(ga_re8, ga_im8):
    n = ga_re8.shape[1]

    def body(r_ref, i_ref, o_ref):
        o_ref[...] = jnp.zeros(o_ref.shape, F32)
        o_ref[0:1, :] = _colsum(r_ref[...])
        o_ref[1:2, :] = _colsum(i_ref[...])

    return pl.pallas_call(body, name="ga_rowsum", out_shape=jax.ShapeDtypeStruct((SUBLANES, n), F32))(ga_re8, ga_im8)
```

```python
import functools
import math

import jax
import jax.numpy as jnp
import numpy as np
from jax import lax
from jax.experimental import pallas as pl
from jax.experimental.pallas import tpu as pltpu

F32 = jnp.float32
BF16 = jnp.bfloat16
MESH = pl.DeviceIdType.MESH

EPS = 1e-6
LAMBDA_RE_MAX = -1e-4
ADAM_LR = 0.001
ADAM_B1 = 0.9
ADAM_B2 = 0.999
ADAM_EPS = 1e-08
ADAM_WD = 0.01
ADAM_STEP = 10

SUBLANES = 8
BF16_ROWS = 16
N_CHIPS = 4
N_DEV = 8
CONV_HEAD_DIM = 64
VMEM_BIG = 56 * 1024 * 1024
VMEM_MID = 40 * 1024 * 1024
VMEM_KEEP_OPERANDS_IN_HBM = 62 * 1024 * 1024

TB_MIX = 256
TB_FFN = 256
TB_FFN_UP = 512
TB_SCAN = 1024
W_SCAN = 256
SSM_SPLIT = 4
CW_FFN = 256
SCAN_UNROLL = 4
TB_TN = 512


def _cparams(sem=None, vmem=None):
    kw = {}
    if sem is not None:
        kw["dimension_semantics"] = sem
    if vmem is not None:
        kw["vmem_limit_bytes"] = vmem
    return pltpu.CompilerParams(**kw)


def _blk(t, pref):
    return pref if t % pref == 0 else t


def _dot(a, b):
    return jnp.dot(a.astype(BF16), b.astype(BF16), preferred_element_type=F32)


def _dot_nt(a, b):
    return lax.dot_general(a.astype(BF16), b.astype(BF16), (((1,), (1,)), ((), ())),
                           preferred_element_type=F32)


def _dot_tn(a, b):
    return lax.dot_general(a.astype(BF16), b.astype(BF16), (((0,), (0,)), ((), ())),
                           preferred_element_type=F32)


def _sigmoid(x):
    return 0.5 * jnp.tanh(0.5 * x) + 0.5


_GELU_K = math.sqrt(2.0 / math.pi)
_GELU_C = 0.044715


def _gelu(x):
    th = jnp.tanh(_GELU_K * (x + _GELU_C * x * x * x))
    return x * (0.5 * (1.0 + th))


def _gelu_and_grad(x):
    x2 = x * x
    th = jnp.tanh(_GELU_K * (x + _GELU_C * x2 * x))
    half = 0.5 * (1.0 + th)
    return x * half, half + 0.5 * x * (1.0 - th * th) * _GELU_K * (1.0 + 3.0 * _GELU_C * x2)


def _rowmean(x):
    return jnp.mean(x, axis=-1, keepdims=True)


def _colsum(x):
    return jnp.sum(x, axis=0, keepdims=True)


def _split_dot(x, m):
    hi = x.astype(BF16)
    lo = (x - hi.astype(F32)).astype(BF16)
    return (jnp.dot(hi, m, preferred_element_type=F32) + jnp.dot(lo, m, preferred_element_type=F32))


def _split3_dot(x, m):
    hi = x.astype(BF16)
    r1 = x - hi.astype(F32)
    mid = r1.astype(BF16)
    lo = (r1 - mid.astype(F32)).astype(BF16)
    return (jnp.dot(hi, m, preferred_element_type=F32) + jnp.dot(mid, m, preferred_element_type=F32)
            + jnp.dot(lo, m, preferred_element_type=F32))


def _shift_down(x, halo, k):
    r = pltpu.roll(x, k, 0)
    row = lax.broadcasted_iota(jnp.int32, x.shape, 0)
    for j in range(k):
        r = jnp.where(row == j, halo[SUBLANES - k + j:SUBLANES - k + j + 1, :], r)
    return r


def _shift_up(x, halo, k):
    n = x.shape[0]
    r = pltpu.roll(x, n - k, 0)
    row = lax.broadcasted_iota(jnp.int32, x.shape, 0)
    for j in range(k):
        r = jnp.where(row == n - k + j, halo[j:j + 1, :], r)
    return r


def _acc_rows(ref, first, rows):
    @pl.when(first)
    def _():
        ref[...] = jnp.zeros(ref.shape, ref.dtype)
    for j, r in enumerate(rows):
        ref[j:j + 1, :] += r


def _rows(tb, c, col=0):
    return pl.BlockSpec((tb, c), lambda i, col=col: (i, col))


def _full(shape):
    nd = len(shape)
    return pl.BlockSpec(shape, lambda i, nd=nd: (0,) * nd)


def _resident(shape):
    nd = len(shape)
    return pl.BlockSpec(shape, lambda i, nd=nd: (0,) * nd, pipeline_mode=pl.Buffered(1))


def _halo_prev(tb, c, col=0):
    per = tb // SUBLANES
    return pl.BlockSpec((SUBLANES, c), lambda i, col=col: (jnp.maximum(i * per - 1, 0), col))


def _halo_next(tb, c, t, col=0, rows=SUBLANES):
    per = tb // rows
    last = t // rows - 1
    return pl.BlockSpec((rows, c), lambda i, col=col: (jnp.minimum((i + 1) * per, last), col))


def _mesh_pos():
    return lax.axis_index("x"), lax.axis_index("y"), lax.axis_index("c")


def _allgather8(x_pad, name):
    m_per, n = x_pad.shape

    def body(x_ref, out_ref, send_sems, recv_sems, local_sem):
        x, y, c = _mesh_pos()
        me, sibling = (x, y, c), (x, y, 1 - c)
        chips = [(1 - x, y), (x, 1 - y), (1 - x, 1 - y)]

        def rows(px, py, pc):
            return out_ref.at[pl.ds((4 * px + 2 * py + pc) * m_per, m_per), :]

        def copy(k, block, to, src=None):
            return pltpu.make_async_remote_copy(
                src_ref=rows(*block) if src is None else src, dst_ref=rows(*block),
                send_sem=send_sems.at[k], recv_sem=recv_sems.at[k], device_id=to, device_id_type=MESH)

        mine = pltpu.make_async_copy(x_ref, rows(*me), local_sem)
        mine.start()
        first = [copy(0, me, sibling, src=x_ref)]
        first += [copy(1 + j, me, (*chip, c), src=x_ref) for j, chip in enumerate(chips)]
        for cp in first:
            cp.start()
        passed = [copy(4 + j, (*chip, c), sibling) for j, chip in enumerate(chips)]
        for j, chip in enumerate(chips):
            copy(1 + j, (*chip, c), me).wait_recv()
            passed[j].start()
        copy(0, sibling, me).wait_recv()
        for j, chip in enumerate(chips):
            copy(4 + j, (*chip, 1 - c), me).wait_recv()
        for cp in first + passed:
            cp.wait_send()
        mine.wait()

    return pl.pallas_call(
        body, name=name,
        out_shape=jax.ShapeDtypeStruct((N_DEV * m_per, n), F32),
        in_specs=[pl.BlockSpec(memory_space=pltpu.VMEM)],
        out_specs=pl.BlockSpec(memory_space=pltpu.VMEM),
        scratch_shapes=[pltpu.SemaphoreType.DMA((7,)), pltpu.SemaphoreType.DMA((7,)), pltpu.SemaphoreType.DMA],
    )(x_pad)


_HBM = pl.BlockSpec(memory_space=pltpu.HBM)
_SEM = pl.BlockSpec(memory_space=pltpu.SEMAPHORE)
_EFFECT = pltpu.SideEffectType.DATAFLOW_SIDE_EFFECTING


def _chip_copy(gather, src_ref, land_ref, send, recv, j, arrival):
    x, y, c = _mesh_pos()
    peer = [(1 - x, y), (x, 1 - y), (1 - x, 1 - y)][j]
    peer_chip = 2 * peer[0] + peer[1]
    my_chip = 2 * x + y
    return pltpu.make_async_remote_copy(
        src_ref=land_ref.at[my_chip] if gather else src_ref.at[peer_chip],
        dst_ref=land_ref.at[peer_chip if arrival else my_chip],
        send_sem=send.at[j], recv_sem=recv.at[j], device_id=(*peer, c), device_id_type=MESH)


def _chips_start(name, gather, srcs, lands, after=None):
    n, ns = len(lands), len(srcs)
    extra = [] if after is None else [after]

    def body(*refs):
        src_refs, land_refs = refs[:ns], refs[ns:ns + n]
        outs = refs[ns + n + len(extra):]
        sends, recvs, token = outs[:n], outs[n:2 * n], outs[-1]
        for k in range(n):
            for j in range(3):
                _chip_copy(gather, src_refs[k] if ns else None, land_refs[k], sends[k], recvs[k], j, False).start()
        token[...] = jnp.zeros(token.shape, F32)

    sem = pltpu.SemaphoreType.DMA((3,))
    thru = tuple(pltpu.HBM(a.shape, a.dtype) for a in list(srcs) + list(lands))
    res = pl.pallas_call(
        body, name=name,
        out_shape=(sem,) * (2 * n) + thru + (jax.ShapeDtypeStruct((SUBLANES, 128), F32),),
        in_specs=[_HBM] * (ns + n) + [pl.BlockSpec(memory_space=pl.ANY)] * len(extra),
        out_specs=(_SEM,) * (2 * n) + (_HBM,) * (ns + n) + (pl.BlockSpec(memory_space=pltpu.VMEM),),
        input_output_aliases={k: 2 * n + k for k in range(ns + n)},
        compiler_params=pltpu.CompilerParams(has_side_effects=_EFFECT),
    )(*[pltpu.with_memory_space_constraint(a, pltpu.HBM) for a in list(srcs) + list(lands)], *extra)
    return res[:n], res[n:2 * n], res[2 * n:2 * n + ns], res[2 * n + ns:2 * n + ns + n], res[-1]


def _chips_wait(name, gather, sends, recvs, srcs, lands, after):
    n, ns = len(lands), len(srcs)

    def body(*refs):
        src_refs, land_refs = refs[:ns], refs[ns:ns + n]
        sends_, recvs_ = refs[ns + n:ns + 2 * n], refs[ns + 2 * n:ns + 3 * n]
        for k in range(n):
            for j in range(3):
                cp = _chip_copy(gather, src_refs[k] if ns else None, land_refs[k], sends_[k], recvs_[k], j, True)
                cp.wait_send()
                cp.wait_recv()

    thru = tuple(pltpu.HBM(a.shape, a.dtype) for a in list(srcs) + list(lands))
    res = pl.pallas_call(
        body, name=name, out_shape=thru,
        in_specs=[_HBM] * (ns + n) + [_SEM] * (2 * n) + [pl.BlockSpec(memory_space=pl.ANY)],
        out_specs=(_HBM,) * (ns + n),
        input_output_aliases={k: k for k in range(ns + n)},
        compiler_params=pltpu.CompilerParams(has_side_effects=_EFFECT),
    )(*srcs, *lands, *sends, *recvs, after)
    return res[:ns], res[ns:]


def _sibling_copy(src_ref, land_ref, send, recv):
    x, y, c = _mesh_pos()
    return pltpu.make_async_remote_copy(src_ref=src_ref, dst_ref=land_ref, send_sem=send.at[0], recv_sem=recv.at[0],
                                        device_id=(x, y, 1 - c), device_id_type=MESH)


def _sibling_start(name, arrs, after=None):
    n = len(arrs)
    extra = [] if after is None else [after]
    lands = [lax.empty(a.shape, a.dtype) for a in arrs]

    def body(*refs):
        src_refs, land_refs = refs[:n], refs[n:2 * n]
        outs = refs[2 * n + len(extra):]
        sends, recvs, token = outs[:n], outs[n:2 * n], outs[-1]
        for k in range(n):
            _sibling_copy(src_refs[k], land_refs[k], sends[k], recvs[k]).start()
        token[...] = jnp.zeros(token.shape, F32)

    sem = pltpu.SemaphoreType.DMA((1,))
    thru = tuple(pltpu.HBM(a.shape, a.dtype) for a in list(arrs) + lands)
    res = pl.pallas_call(
        body, name=name,
        out_shape=(sem,) * (2 * n) + thru + (jax.ShapeDtypeStruct((SUBLANES, 128), F32),),
        in_specs=[_HBM] * (2 * n) + [pl.BlockSpec(memory_space=pl.ANY)] * len(extra),
        out_specs=(_SEM,) * (2 * n) + (_HBM,) * (2 * n) + (pl.BlockSpec(memory_space=pltpu.VMEM),),
        input_output_aliases={k: 2 * n + k for k in range(2 * n)},
        compiler_params=pltpu.CompilerParams(has_side_effects=_EFFECT),
    )(*[pltpu.with_memory_space_constraint(a, pltpu.HBM) for a in list(arrs) + lands], *extra)
    return res[:n], res[n:2 * n], res[2 * n:3 * n], res[3 * n:4 * n], res[-1]


def _sibling_wait(name, sends, recvs, srcs, lands, after):
    n = len(srcs)

    def body(*refs):
        src_refs, land_refs = refs[:n], refs[n:2 * n]
        sends_, recvs_ = refs[2 * n:3 * n], refs[3 * n:4 * n]
        for k in range(n):
            cp = _sibling_copy(src_refs[k], land_refs[k], sends_[k], recvs_[k])
            cp.wait_send()
            cp.wait_recv()

    thru = tuple(pltpu.HBM(a.shape, a.dtype) for a in list(srcs) + list(lands))
    res = pl.pallas_call(
        body, name=name, out_shape=thru,
        in_specs=[_HBM] * (2 * n) + [_SEM] * (2 * n) + [pl.BlockSpec(memory_space=pl.ANY)],
        out_specs=(_HBM,) * (2 * n),
        input_output_aliases={k: k for k in range(2 * n)},
        compiler_params=pltpu.CompilerParams(has_side_effects=_EFFECT),
    )(*srcs, *lands, *sends, *recvs, after)
    return res[:n], res[n:]


def _landing(own, chip):
    zone = lax.empty((N_CHIPS,) + own.shape, own.dtype)
    return lax.dynamic_update_slice(zone, own[None], (chip,) + (0,) * own.ndim)


def _mod_shard(c_all, w_ada_sh, b_sh):
    d, n = w_ada_sh.shape
    bn = 512

    def body(c_ref, w_ref, b_ref, o_ref):
        cc = c_ref[...]
        ca = cc * _sigmoid(cc)
        o_ref[...] = _dot(ca, w_ref[...]) + b_ref[...]

    return pl.pallas_call(
        body, name="mod_shard", grid=(n // bn,),
        out_shape=jax.ShapeDtypeStruct((N_DEV, n), F32),
        in_specs=[_full((N_DEV, d)), pl.BlockSpec((d, bn), lambda j: (0, j)), pl.BlockSpec((1, bn), lambda j: (0, j))],
        out_specs=pl.BlockSpec((N_DEV, bn), lambda j: (0, j)),
        compiler_params=_cparams(("parallel",)),
    )(c_all, w_ada_sh, b_sh)


def _ssm_prep(lam_re, lam_im, log_step):
    g, p = lam_re.shape

    def body(lr_ref, li_ref, ls_ref, ar_ref, ai_ref, cr_ref, ci_ref):
        lr = jnp.minimum(lr_ref[...], LAMBDA_RE_MAX)
        li = li_ref[...]
        st = jnp.exp(ls_ref[...])
        mag = jnp.exp(lr * st)
        ar = mag * jnp.cos(li * st)
        ai = mag * jnp.sin(li * st)
        den = lr * lr + li * li
        nr = ar - 1.0
        ar_ref[...] = ar
        ai_ref[...] = ai
        cr_ref[...] = (nr * lr + ai * li) / den
        ci_ref[...] = (ai * lr - nr * li) / den

    sds = jax.ShapeDtypeStruct((g, p), F32)
    return pl.pallas_call(body, name="ssm_prep", out_shape=(sds,) * 4)(lam_re, lam_im, log_step)


def _ssm_blocks(bt_re, bt_im, ct_re, ct_im, coef_rows, tile_b, tile_c):
    gh, p = bt_re.shape
    gp, h = ct_re.shape
    nb = SSM_SPLIT
    cb, rb = gp // nb, gp // nb

    def body(btr, bti, ctr, cti, cf, tb_ref, tc_ref, bre_o, bim_o, cre_o, cim_o):
        j = pl.program_id(0)
        row = lax.broadcasted_iota(jnp.int32, (gh, cb), 0)
        col = lax.broadcasted_iota(jnp.int32, (gh, cb), 1) + j * cb
        mask = (row >> 4) == (col >> 6)
        cr, ci = cf[0:1, :], cf[1:2, :]
        br = _split3_dot(btr[...], tb_ref[...])
        bi = _split3_dot(bti[...], tb_ref[...])
        bre_o[...] = jnp.where(mask, br * cr - bi * ci, 0.0).astype(BF16)
        bim_o[...] = jnp.where(mask, br * ci + bi * cr, 0.0).astype(BF16)
        row2 = lax.broadcasted_iota(jnp.int32, (rb, gh), 0) + j * rb
        col2 = lax.broadcasted_iota(jnp.int32, (rb, gh), 1)
        mask2 = (row2 >> 6) == (col2 >> 4)
        cre_o[...] = jnp.where(mask2, _split3_dot(ctr[...], tc_ref[...]), 0.0).astype(BF16)
        cim_o[...] = jnp.where(mask2, _split3_dot(cti[...], tc_ref[...]), 0.0).astype(BF16)

    bspec = pl.BlockSpec((gh, cb), lambda j: (0, j))
    cspec = pl.BlockSpec((rb, gh), lambda j: (j, 0))
    cin = pl.BlockSpec((rb, h), lambda j: (j, 0))
    return pl.pallas_call(
        body, name="ssm_blocks", grid=(nb,),
        out_shape=(jax.ShapeDtypeStruct((gh, gp), BF16),) * 2 + (jax.ShapeDtypeStruct((gp, gh), BF16),) * 2,
        in_specs=[_full((gh, p)), _full((gh, p)), cin, cin, pl.BlockSpec((SUBLANES, cb), lambda j: (0, j)),
                  _full(tile_b.shape), _full(tile_c.shape)],
        out_specs=(bspec, bspec, cspec, cspec),
        compiler_params=_cparams(("parallel",)),
    )(bt_re, bt_im, ct_re, ct_im, coef_rows, tile_b, tile_c)


def _scan_consts(a_ref, reverse):
    w = a_ref.shape[1]
    ar1 = a_ref[0:1, :]
    ai1 = a_ref[1:2, :]
    if reverse:
        ai1 = -ai1
    pr, pi = [ar1], [ai1]
    for _ in range(1, SUBLANES):
        nr = pr[-1] * ar1 - pi[-1] * ai1
        ni = pr[-1] * ai1 + pi[-1] * ar1
        pr.append(nr)
        pi.append(ni)
    row = lax.broadcasted_iota(jnp.int32, (SUBLANES, w), 0)
    dist = (SUBLANES - 1 - row) if reverse else row

    def pick(vals):
        out = jnp.broadcast_to(vals[SUBLANES - 1], (SUBLANES, w))
        for r in range(SUBLANES - 1):
            out = jnp.where(dist == r, vals[r], out)
        return out

    p_r, p_i = pick(pr), pick(pi)
    steps = []
    for k in (1, 2, 4):
        steps.append((k, jnp.where(dist >= k, pr[k - 1], 0.0), jnp.where(dist >= k, pi[k - 1], 0.0)))
    a8 = (jnp.broadcast_to(pr[SUBLANES - 1], (SUBLANES, w)), jnp.broadcast_to(pi[SUBLANES - 1], (SUBLANES, w)))
    return row, p_r, p_i, steps, a8


def _scan_tile(xr, xi, cr, ci, consts, reverse):
    row, p_r, p_i, steps, (a8r, a8i) = consts
    for k, s_r, s_i in steps:
        sh = (SUBLANES - k) if reverse else k
        qr = pltpu.roll(xr, sh, 0)
        qi = pltpu.roll(xi, sh, 0)
        xr, xi = xr + s_r * qr - s_i * qi, xi + s_r * qi + s_i * qr
    outr = xr + p_r * cr - p_i * ci
    outi = xi + p_r * ci + p_i * cr
    e = 0 if reverse else SUBLANES - 1
    er = jnp.broadcast_to(xr[e:e + 1, :], xr.shape)
    ei = jnp.broadcast_to(xi[e:e + 1, :], xi.shape)
    return outr, outi, er + a8r * cr - a8i * ci, ei + a8r * ci + a8i * cr


def _scan_fwd(a_rows, bu_re, bu_im):
    t, n = bu_re.shape
    tb, w = _blk(t, TB_SCAN), W_SCAN
    ntile = tb // SUBLANES

    def body(a_ref, br_ref, bi_ref, sr_ref, si_ref, car, cai):
        @pl.when(pl.program_id(1) == 0)
        def _():
            car[...] = jnp.zeros(car.shape, F32)
            cai[...] = jnp.zeros(cai.shape, F32)
        consts = _scan_consts(a_ref, False)

        def pair(i, carry):
            o = pl.multiple_of(i * BF16_ROWS, BF16_ROWS)
            outs = []
            for h in range(2):
                rows = pl.ds(o + h * SUBLANES, SUBLANES)
                outr, outi, ncr, nci = _scan_tile(br_ref[rows, :], bi_ref[rows, :], carry[0], carry[1], consts, False)
                outs.append((outr, outi))
                carry = (ncr, nci)
            sr_ref[pl.ds(o, BF16_ROWS), :] = jnp.concatenate([outs[0][0], outs[1][0]], axis=0).astype(BF16)
            si_ref[pl.ds(o, BF16_ROWS), :] = jnp.concatenate([outs[0][1], outs[1][1]], axis=0).astype(BF16)
            return carry

        def pairs(i, carry):
            for s in range(SCAN_UNROLL // 2):
                carry = pair(i * (SCAN_UNROLL // 2) + s, carry)
            return carry

        cr, ci = lax.fori_loop(0, ntile // SCAN_UNROLL, pairs, (car[...], cai[...]))
        car[...] = cr
        cai[...] = ci

    spec = pl.BlockSpec((tb, w), lambda s, k: (k, s))
    sds = jax.ShapeDtypeStruct((t, n), BF16)
    return pl.pallas_call(
        body, name="scan_fwd", grid=(n // w, t // tb), out_shape=(sds, sds),
        in_specs=[pl.BlockSpec((SUBLANES, w), lambda s, k: (0, s)), spec, spec], out_specs=(spec, spec),
        scratch_shapes=[pltpu.VMEM((SUBLANES, w), F32), pltpu.VMEM((SUBLANES, w), F32)],
        compiler_params=_cparams(("parallel", "arbitrary"), VMEM_MID),
    )(a_rows, bu_re, bu_im)


def _scan_bwd(a_rows, g_re, g_im, s_re, s_im):
    t, n = g_re.shape
    tb, w = _blk(t, TB_SCAN), W_SCAN
    ntile = tb // SUBLANES
    npair = tb // BF16_ROWS
    nt = t // tb

    def body(a_ref, gr_ref, gi_ref, sr_ref, si_ref, or_ref, oi_ref, gar_ref, gai_ref, car, cai):
        @pl.when(pl.program_id(1) == 0)
        def _():
            car[...] = jnp.zeros(car.shape, F32)
            cai[...] = jnp.zeros(cai.shape, F32)
            gar_ref[...] = jnp.zeros(gar_ref.shape, F32)
            gai_ref[...] = jnp.zeros(gai_ref.shape, F32)
        consts = _scan_consts(a_ref, True)
        row = consts[0]

        def pair(i, carry):
            cr, ci, accr, acci = carry
            o = pl.multiple_of((npair - 1 - i) * BF16_ROWS, BF16_ROWS)
            s_r = sr_ref[pl.ds(o, BF16_ROWS), :].astype(F32)
            s_i = si_ref[pl.ds(o, BF16_ROWS), :].astype(F32)
            outs = [None, None]
            for h in (1, 0):
                rows = pl.ds(o + h * SUBLANES, SUBLANES)
                outr, outi, ncr, nci = _scan_tile(gr_ref[rows, :], gi_ref[rows, :], cr, ci, consts, True)
                outs[h] = (outr, outi)
                gnr = jnp.where(row == SUBLANES - 1, cr, pltpu.roll(outr, SUBLANES - 1, 0))
                gni = jnp.where(row == SUBLANES - 1, ci, pltpu.roll(outi, SUBLANES - 1, 0))
                sr = s_r[h * SUBLANES:(h + 1) * SUBLANES, :]
                si = s_i[h * SUBLANES:(h + 1) * SUBLANES, :]
                accr, acci = accr + sr * gnr + si * gni, acci + sr * gni - si * gnr
                cr, ci = ncr, nci
            or_ref[pl.ds(o, BF16_ROWS), :] = jnp.concatenate([outs[0][0], outs[1][0]], axis=0).astype(BF16)
            oi_ref[pl.ds(o, BF16_ROWS), :] = jnp.concatenate([outs[0][1], outs[1][1]], axis=0).astype(BF16)
            return cr, ci, accr, acci

        def pairs(i, carry):
            for s in range(SCAN_UNROLL // 2):
                carry = pair(i * (SCAN_UNROLL // 2) + s, carry)
            return carry

        cr, ci, accr, acci = lax.fori_loop(0, ntile // SCAN_UNROLL, pairs,
                                           (car[...], cai[...], gar_ref[...], gai_ref[...]))
        car[...] = cr
        cai[...] = ci
        gar_ref[...] = accr
        gai_ref[...] = acci

    spec = pl.BlockSpec((tb, w), lambda s, k: (nt - 1 - k, s))
    aspec = pl.BlockSpec((SUBLANES, w), lambda s, k: (0, s))
    sds = jax.ShapeDtypeStruct((t, n), BF16)
    asds = jax.ShapeDtypeStruct((SUBLANES, n), F32)
    return pl.pallas_call(
        body, name="scan_bwd", grid=(n // w, nt), out_shape=(sds, sds, asds, asds),
        in_specs=[aspec, spec, spec, spec, spec], out_specs=(spec, spec, aspec, aspec),
        scratch_shapes=[pltpu.VMEM((SUBLANES, w), F32), pltpu.VMEM((SUBLANES, w), F32)],
        compiler_params=_cparams(("parallel", "arbitrary"), VMEM_MID),
    )(a_rows, g_re, g_im, s_re, s_im)


def _mix_in(x, vec, w_in_st, b_re, b_im):
    t, d = x.shape
    ns, _, nc = w_in_st.shape
    dssm, nstate = b_re.shape
    du, ds = dssm // SSM_SPLIT, nstate // SSM_SPLIT
    tb = _blk(t, TB_MIX)

    def body(x_ref, vec_ref, w_ref, bre_ref, bim_ref, proj_ref, bur_ref, bui_ref, h1_ref):
        xv = x_ref[...]
        r = lax.rsqrt(_rowmean(xv * xv) + EPS)
        h = xv * r * vec_ref[0:1, :] * vec_ref[1:2, :] + vec_ref[2:3, :]
        hb = h.astype(BF16)
        h1_ref[...] = hb
        u = None
        for j in range(ns):
            pj = jnp.dot(hb, w_ref[j], preferred_element_type=F32)
            proj_ref[:, j * nc:(j + 1) * nc] = pj
            if j == 0:
                u = pj
        ub = u.astype(BF16)
        for q in range(SSM_SPLIT):
            rq, cq = slice(q * du, (q + 1) * du), slice(q * ds, (q + 1) * ds)
            bur_ref[:, cq] = jnp.dot(ub[:, rq], bre_ref[rq, cq], preferred_element_type=F32)
            bui_ref[:, cq] = jnp.dot(ub[:, rq], bim_ref[rq, cq], preferred_element_type=F32)

    return pl.pallas_call(
        body, name="mix_in", grid=(t // tb,),
        out_shape=(jax.ShapeDtypeStruct((t, ns * nc), F32), jax.ShapeDtypeStruct((t, nstate), F32),
                   jax.ShapeDtypeStruct((t, nstate), F32), jax.ShapeDtypeStruct((t, d), BF16)),
        in_specs=[_rows(tb, d), _full((SUBLANES, d)), _resident(w_in_st.shape), _resident(b_re.shape),
                  _resident(b_im.shape)],
        out_specs=(_rows(tb, ns * nc), _rows(tb, nstate), _rows(tb, nstate), _rows(tb, d)),
        compiler_params=_cparams(("parallel",), VMEM_BIG),
    )(x, vec, w_in_st, b_re, b_im)


def _head_ms(y, h_ref):
    return _split_dot(y * y, h_ref[...])


def _conv3(x, halo, w_ref):
    return w_ref[0:1, :] * _shift_down(x, halo, 2) + w_ref[1:2, :] * _shift_down(x, halo, 1) + w_ref[2:3, :] * x


def _mix_out(x, proj, s_re, s_im, c_re, c_im, v512, convw, glu_w, h16, h64, w_out, vd):
    t, d = x.shape
    dh = c_re.shape[1]
    nstate = s_re.shape[1]
    du, ds = dh // SSM_SPLIT, nstate // SSM_SPLIT
    tb = _blk(t, TB_MIX)

    def body(x_ref, u_ref, bg_ref, cg_ref, v_ref, cgh_ref, vh_ref, sr_ref, si_ref, cre_ref, cim_ref, p_ref,
             cw_ref, gw_ref, h16_ref, h64_ref, wo_ref, vd_ref, y1_ref, o_ref, x2_ref):
        i = pl.program_id(0)
        u = u_ref[...]
        ys = []
        for q in range(SSM_SPLIT):
            rq, cq = slice(q * ds, (q + 1) * ds), slice(q * du, (q + 1) * du)
            ys.append(_dot(sr_ref[:, rq], cre_ref[rq, cq]) - _dot(si_ref[:, rq], cim_ref[rq, cq]))
        ys = jnp.concatenate(ys, axis=1)
        y1 = ys + p_ref[0:1, :] * u
        y1_ref[...] = y1
        z = _gelu(y1)
        q = _dot(z, gw_ref[...]) + p_ref[1:2, :]
        ya = z * _sigmoid(q)
        na = ya * lax.rsqrt(_head_ms(ya, h16_ref) + EPS) * p_ref[2:3, :]
        cv = cg_ref[...] * v_ref[...]
        cvh = jnp.where(i > 0, cgh_ref[...] * vh_ref[...], 0.0)
        yb = bg_ref[...] * _conv3(cv, cvh, cw_ref)
        nb = yb * lax.rsqrt(_head_ms(yb, h64_ref) + EPS) * p_ref[3:4, :]
        o = _dot(na, wo_ref[0:dh, :]) + _dot(nb, wo_ref[dh:2 * dh, :])
        o_ref[...] = o
        on = o * lax.rsqrt(_rowmean(o * o) + EPS) * vd_ref[0:1, :]
        x2_ref[...] = x_ref[...] + vd_ref[1:2, :] * on

    return pl.pallas_call(
        body, name="mix_out", grid=(t // tb,),
        out_shape=(jax.ShapeDtypeStruct((t, dh), F32), jax.ShapeDtypeStruct((t, d), F32),
                   jax.ShapeDtypeStruct((t, d), F32)),
        in_specs=[_rows(tb, d), _rows(tb, dh, 0), _rows(tb, dh, 1), _rows(tb, dh, 2), _rows(tb, dh, 3),
                  _halo_prev(tb, dh, 2), _halo_prev(tb, dh, 3), _rows(tb, nstate), _rows(tb, nstate),
                  _full(c_re.shape), _full(c_im.shape), _full(v512.shape), _full(convw.shape), _full(glu_w.shape),
                  _full(h16.shape), _full(h64.shape), _full(w_out.shape), _full(vd.shape)],
        out_specs=(_rows(tb, dh), _rows(tb, d), _rows(tb, d)),
        compiler_params=_cparams(("parallel",), VMEM_BIG),
    )(x, proj, proj, proj, proj, proj, proj, s_re, s_im, c_re, c_im, v512, convw, glu_w, h16, h64, w_out, vd)


def _ffn_up(x2, vec, w_up_st):
    t, d = x2.shape
    ns, _, nc = w_up_st.shape
    tb = _blk(t, TB_FFN_UP)

    def body(x_ref, vec_ref, w_ref, up_ref, h2_ref):
        xv = x_ref[...]
        r = lax.rsqrt(_rowmean(xv * xv) + EPS)
        h = xv * r * vec_ref[0:1, :] * vec_ref[1:2, :] + vec_ref[2:3, :]
        hb = h.astype(BF16)
        h2_ref[...] = hb
        for j in range(ns):
            up_ref[:, j * nc:(j + 1) * nc] = jnp.dot(hb, w_ref[j], preferred_element_type=F32)

    return pl.pallas_call(
        body, name="ffn_up", grid=(t // tb,),
        out_shape=(jax.ShapeDtypeStruct((t, ns * nc), F32), jax.ShapeDtypeStruct((t, d), BF16)),
        in_specs=[_rows(tb, d), _full((SUBLANES, d)), _resident(w_up_st.shape)],
        out_specs=(_rows(tb, ns * nc), _rows(tb, d)),
        compiler_params=_cparams(("parallel",), VMEM_BIG),
    )(x2, vec, w_up_st)


def _ffn_down(up, fw, w_down, w_down_t, x2, tgt, vd):
    t, nh = up.shape
    dff, d = w_down.shape
    tb = _blk(t, TB_FFN)
    inv_d = 1.0 / d

    def body(up_ref, uph_ref, fw_ref, wd_ref, wdt_ref, x2_ref, tgt_ref, vd_ref,
             act_ref, ddn_ref, dout_ref, dhid_ref, vec_ref, loss_ref, a_s, vv_s, sg_s):
        i = pl.program_id(0)

        def conv_cols(sl):
            x = up_ref[:, sl]
            halo = jnp.where(i > 0, uph_ref[:, sl], 0.0)
            return (fw_ref[0:1, sl] * _shift_down(x, halo, 2) + fw_ref[1:2, sl] * _shift_down(x, halo, 1)
                    + fw_ref[2:3, sl] * x)

        dn = None
        for o in range(0, dff, CW_FFN):
            sl = slice(o, o + CW_FFN)
            a = conv_cols(sl)
            vv = conv_cols(slice(dff + o, dff + o + CW_FFN))
            sg = _sigmoid(a)
            si = a * sg
            a_s[:, sl] = si
            vv_s[:, sl] = vv
            sg_s[:, sl] = sg
            actb = (si * vv).astype(BF16)
            act_ref[:, sl] = actb
            pj = lax.dot_general(actb, wdt_ref[:, sl], (((1,), (1,)), ((), ())), preferred_element_type=F32)
            dn = pj if dn is None else dn + pj
        r3 = lax.rsqrt(_rowmean(dn * dn) + EPS)
        xn = dn * r3
        g = vd_ref[0:1, :]
        gt2 = vd_ref[1:2, :]
        dnn = xn * g
        diff = x2_ref[...] + gt2 * dnn - tgt_ref[...]
        part = 0.5 * inv_d * jnp.sum(diff * diff)

        @pl.when(i == 0)
        def _():
            loss_ref[...] = jnp.zeros(loss_ref.shape, F32)
        loss_ref[...] += part
        dout = diff * inv_d
        dout_ref[...] = dout
        ddnn = dout * gt2
        _acc_rows(vec_ref, i == 0, [_colsum(dout * dnn), _colsum(ddnn * xn)])
        dxn = ddnn * g
        ddn = r3 * (dxn - xn * _rowmean(dxn * xn))
        ddnb = ddn.astype(BF16)
        ddn_ref[...] = ddnb
        for o in range(0, dff, CW_FFN):
            sl = slice(o, o + CW_FFN)
            dact = lax.dot_general(ddnb, wd_ref[sl, :], (((1,), (1,)), ((), ())), preferred_element_type=F32)
            si, vv, sg = a_s[:, sl], vv_s[:, sl], sg_s[:, sl]
            dhid_ref[:, sl] = (dact * vv * (sg + si * (1.0 - sg))).astype(BF16)
            dhid_ref[:, dff + o:dff + o + CW_FFN] = (dact * si).astype(BF16)

    return pl.pallas_call(
        body, name="ffn_down", grid=(t // tb,),
        scratch_shapes=[pltpu.VMEM((tb, dff), F32)] * 3,
        out_shape=(jax.ShapeDtypeStruct((t, dff), BF16), jax.ShapeDtypeStruct((t, d), BF16),
                   jax.ShapeDtypeStruct((t, d), F32), jax.ShapeDtypeStruct((t, nh), BF16),
                   jax.ShapeDtypeStruct((SUBLANES, d), F32), jax.ShapeDtypeStruct((SUBLANES, 128), F32)),
        in_specs=[_rows(tb, nh), _halo_prev(tb, nh), _full(fw.shape), _resident(w_down.shape),
                  _resident(w_down_t.shape), _rows(tb, d),
                  _rows(tb, d), _full(vd.shape)],
        out_specs=(_rows(tb, dff), _rows(tb, d), _rows(tb, d), _rows(tb, nh), _full((SUBLANES, d)),
                   _full((SUBLANES, 128))),
        compiler_params=_cparams(("arbitrary",), VMEM_BIG),
    )(up, up, fw, w_down, w_down_t, x2, tgt, vd)


def _ffn_up_bwd(dhid, up, fw, x2, dout, vec, w_up_st):
    t, nh = dhid.shape
    d = x2.shape[1]
    ns, _, nc = w_up_st.shape
    tb = _blk(t, TB_FFN)
    nblk = t // tb
    cw = 128

    def body(dh_ref, dhn_ref, up_ref, fw_ref, x2_ref, dout_ref, vec_ref, w_ref,
             dx2_ref, dup_ref, vp_ref, df_ref):
        i = pl.program_id(0)

        @pl.when(i == 0)
        def _():
            df_ref[...] = jnp.zeros(df_ref.shape, F32)
        dh2 = None
        for j in range(ns):
            for o in range(j * nc, (j + 1) * nc, cw):
                sl = slice(o, o + cw)
                dh = dh_ref[:, sl].astype(F32)
                dhn = jnp.where(i < nblk - 1, dhn_ref[:, sl].astype(F32), 0.0)
                dh1 = _shift_up(dh, dhn, 1)
                dh2s = _shift_up(dh, dhn, 2)
                dup_ref[:, sl] = (fw_ref[2:3, sl] * dh + fw_ref[1:2, sl] * dh1 + fw_ref[0:1, sl] * dh2s).astype(BF16)
                up_v = up_ref[:, sl]
                df_ref[0:1, sl] += _colsum(dh2s * up_v)
                df_ref[1:2, sl] += _colsum(dh1 * up_v)
                df_ref[2:3, sl] += _colsum(dh * up_v)
            pj = lax.dot_general(dup_ref[:, j * nc:(j + 1) * nc], w_ref[j], (((1,), (1,)), ((), ())),
                                 preferred_element_type=F32)
            dh2 = pj if dh2 is None else dh2 + pj
        xv = x2_ref[...]
        r = lax.rsqrt(_rowmean(xv * xv) + EPS)
        xn = xv * r
        g = vec_ref[0:1, :]
        hg = xn * g
        dhg = dh2 * vec_ref[1:2, :]
        _acc_rows(vp_ref, i == 0, [_colsum(dh2), _colsum(dh2 * hg), _colsum(dhg * xn)])
        dxn = dhg * g
        dx2_ref[...] = dout_ref[...] + r * (dxn - xn * _rowmean(dxn * xn))

    return pl.pallas_call(
        body, name="ffn_up_bwd", grid=(nblk,),
        out_shape=(jax.ShapeDtypeStruct((t, d), F32), jax.ShapeDtypeStruct((t, nh), BF16),
                   jax.ShapeDtypeStruct((SUBLANES, d), F32), jax.ShapeDtypeStruct((SUBLANES, nh), F32)),
        in_specs=[_rows(tb, nh), _halo_next(tb, nh, t, rows=BF16_ROWS), _rows(tb, nh), _full(fw.shape),
                  _rows(tb, d), _rows(tb, d), _full(vec.shape), _resident(w_up_st.shape)],
        out_specs=(_rows(tb, d), _rows(tb, nh), _full((SUBLANES, d)), _full((SUBLANES, nh))),
        compiler_params=_cparams(("arbitrary",), VMEM_BIG),
    )(dhid, dhid, up, fw, x2, dout, vec, w_up_st)


def _mix_out_bwd(dx2, o, y1, proj, s_re, s_im, c_re, c_im, v512, convw, glu_w, h16, h64, w_out, vd):
    t, d = dx2.shape
    dh = y1.shape[1]
    nstate = c_re.shape[0]
    du, ds = dh // SSM_SPLIT, nstate // SSM_SPLIT
    tb = _blk(t, TB_MIX)

    def body(dx2_ref, o_ref, y1_ref, u_ref, bg_ref, cg_ref, v_ref, cgh_ref, vh_ref, cre_ref, cim_ref, p_ref,
             cw_ref, gw_ref, h16_ref, h64_ref, wo_ref, vd_ref, sr_ref, si_ref,
             do_ref, ycat_ref, z_ref, dq_ref, dy1_ref, gr_ref, gi_ref, dcc_ref, dbg_ref, vpd_ref, vp5_ref,
             dcr_ref, dci_ref):
        i = pl.program_id(0)
        first = i == 0

        @pl.when(first)
        def _():
            dcr_ref[...] = jnp.zeros(dcr_ref.shape, F32)
            dci_ref[...] = jnp.zeros(dci_ref.shape, F32)
        ov = o_ref[...]
        ro = lax.rsqrt(_rowmean(ov * ov) + EPS)
        on_ = ov * ro
        g = vd_ref[0:1, :]
        dx2v = dx2_ref[...]
        don = dx2v * vd_ref[1:2, :]
        _acc_rows(vpd_ref, first, [_colsum(dx2v * on_ * g), _colsum(don * on_)])
        dxn = don * g
        dob = (ro * (dxn - on_ * _rowmean(dxn * on_))).astype(BF16)
        do_ref[...] = dob
        dyc_a =lax.dot_general(dob, wo_ref[0:dh, :], (((1,), (1,)), ((), ())), preferred_element_type=F32)
        dyc_b = lax.dot_general(dob, wo_ref[dh:2 * dh, :], (((1,), (1,)), ((), ())), preferred_element_type=F32)
        y1v = y1_ref[...]
        u = u_ref[...]
        z, dz_dy1 = _gelu_and_grad(y1v)
        zb = z.astype(BF16)
        sg = _sigmoid(jnp.dot(zb, gw_ref[...], preferred_element_type=F32) + p_ref[1:2, :])
        ya = z * sg
        ra = lax.rsqrt(_head_ms(ya, h16_ref) + EPS)
        yan = ya * ra
        ga = p_ref[2:3, :]
        ycat_ref[:, 0:dh] = (yan * ga).astype(BF16)
        dyn = dyc_a * ga
        dya = ra * (dyn - yan * _split_dot(dyn * yan, h16_ref[...]))
        dq = dya * z * sg * (1.0 - sg)
        dqb = dq.astype(BF16)
        z_ref[...] = zb
        dq_ref[...] = dqb
        dz = dya * sg + lax.dot_general(dqb, gw_ref[...], (((1,), (1,)), ((), ())), preferred_element_type=F32)
        dy1 = dz * dz_dy1
        dy1_ref[...] = dy1
        dy1b = dy1.astype(BF16)
        for q in range(SSM_SPLIT):
            rq, cq = slice(q * ds, (q + 1) * ds), slice(q * du, (q + 1) * du)
            gr_ref[:, rq] = lax.dot_general(dy1b[:, cq], cre_ref[rq, cq], (((1,), (1,)), ((), ())),
                                            preferred_element_type=F32)
            gi_ref[:, rq] = -lax.dot_general(dy1b[:, cq], cim_ref[rq, cq], (((1,), (1,)), ((), ())),
                                             preferred_element_type=F32)
            dcr_ref[rq, :] += _dot_tn(sr_ref[:, rq], dy1b[:, cq])
            dci_ref[rq, :] += _dot_tn(si_ref[:, rq], dy1b[:, cq])
        bg = bg_ref[...]
        cv = cg_ref[...] * v_ref[...]
        cvh = jnp.where(i > 0, cgh_ref[...] * vh_ref[...], 0.0)
        cv1 = _shift_down(cv, cvh, 1)
        cv2 = _shift_down(cv, cvh, 2)
        cc = cw_ref[0:1, :] * cv2 + cw_ref[1:2, :] * cv1 + cw_ref[2:3, :] * cv
        yb = bg * cc
        rb = lax.rsqrt(_head_ms(yb, h64_ref) + EPS)
        ybn = yb * rb
        gb = p_ref[3:4, :]
        ycat_ref[:, dh:2 * dh] = (ybn * gb).astype(BF16)
        dynb = dyc_b * gb
        dyb = rb * (dynb - ybn * _split_dot(dynb * ybn, h64_ref[...]))
        dcc = dyb * bg
        dbg_ref[...] = dyb * cc
        dcc_ref[...] = dcc
        _acc_rows(vp5_ref, first, [_colsum(dyc_a * yan), _colsum(dyc_b * ybn), _colsum(dq), _colsum(dy1 * u),
                                   _colsum(dcc * cv2), _colsum(dcc * cv1), _colsum(dcc * cv)])

    return pl.pallas_call(
        body, name="mix_out_bwd", grid=(t // tb,),
        out_shape=(jax.ShapeDtypeStruct((t, d), BF16), jax.ShapeDtypeStruct((t, 2 * dh), BF16),
                   jax.ShapeDtypeStruct((t, dh), BF16), jax.ShapeDtypeStruct((t, dh), BF16),
                   jax.ShapeDtypeStruct((t, dh), F32), jax.ShapeDtypeStruct((t, nstate), F32),
                   jax.ShapeDtypeStruct((t, nstate), F32), jax.ShapeDtypeStruct((t, dh), F32),
                   jax.ShapeDtypeStruct((t, dh), F32), jax.ShapeDtypeStruct((SUBLANES, d), F32),
                   jax.ShapeDtypeStruct((SUBLANES, dh), F32), jax.ShapeDtypeStruct((nstate, du), F32),
                   jax.ShapeDtypeStruct((nstate, du), F32)),
        in_specs=[_rows(tb, d), _rows(tb, d), _rows(tb, dh), _rows(tb, dh, 0), _rows(tb, dh, 1), _rows(tb, dh, 2),
                  _rows(tb, dh, 3), _halo_prev(tb, dh, 2), _halo_prev(tb, dh, 3), _resident(c_re.shape),
                  _resident(c_im.shape), _full(v512.shape), _full(convw.shape), _resident(glu_w.shape),
                  _resident(h16.shape), _resident(h64.shape), _resident(w_out.shape), _full(vd.shape),
                  _rows(tb, nstate), _rows(tb, nstate)],
        out_specs=(_rows(tb, d), _rows(tb, 2 * dh), _rows(tb, dh), _rows(tb, dh), _rows(tb, dh), _rows(tb, nstate),
                   _rows(tb, nstate), _rows(tb, dh), _rows(tb, dh), _full((SUBLANES, d)), _full((SUBLANES, dh)),
                   _full((nstate, du)), _full((nstate, du))),
        compiler_params=_cparams(("arbitrary",), VMEM_BIG),
    )(dx2, o, y1, proj, proj, proj, proj, proj, proj, c_re, c_im, v512, convw, glu_w, h16, h64, w_out, vd,
      s_re, s_im)


def _mix_in_bwd(gt_re, gt_im, b_re, b_im, dy1, dcc, dbg, proj, x, dx2, vec, v512, convw, w_in_st):
    t, d = x.shape
    dh = dy1.shape[1]
    nstate = gt_re.shape[1]
    du_w, ds = dh // SSM_SPLIT, nstate // SSM_SPLIT
    ns, _, nc = w_in_st.shape
    tb = _blk(t, TB_MIX)
    nblk = t // tb

    def body(gr_ref, gi_ref, bre_ref, bim_ref, dy1_ref, dcc_ref, dccn_ref, dbg_ref, u_ref, cg_ref, v_ref, x_ref,
             dx2_ref, vec_ref, p_ref, cw_ref, w_ref, gx_ref, dproj_ref, vp_ref, dbr_ref, dbi_ref):
        i = pl.program_id(0)

        @pl.when(i == 0)
        def _():
            dbr_ref[...] = jnp.zeros(dbr_ref.shape, F32)
            dbi_ref[...] = jnp.zeros(dbi_ref.shape, F32)
        ub = u_ref[...].astype(BF16)
        du = []
        for q in range(SSM_SPLIT):
            rq, cq = slice(q * du_w, (q + 1) * du_w), slice(q * ds, (q + 1) * ds)
            du.append(lax.dot_general(gr_ref[:, cq].astype(BF16), bre_ref[rq, cq], (((1,), (1,)), ((), ())),
                                      preferred_element_type=F32)
                      + lax.dot_general(gi_ref[:, cq].astype(BF16), bim_ref[rq, cq], (((1,), (1,)), ((), ())),
                                        preferred_element_type=F32))
            dbr_ref[rq, :] += _dot_tn(ub[:, rq], gr_ref[:, cq])
            dbi_ref[rq, :] += _dot_tn(ub[:, rq], gi_ref[:, cq])
        du = dy1_ref[...] * p_ref[0:1, :] + jnp.concatenate(du, axis=1)
        dcc = dcc_ref[...]
        dccn = jnp.where(i < nblk - 1, dccn_ref[...], 0.0)
        dcv = (cw_ref[2:3, :] * dcc + cw_ref[1:2, :] * _shift_up(dcc, dccn, 1)
               + cw_ref[0:1, :] * _shift_up(dcc, dccn, 2))
        parts = [du, dbg_ref[...], dcv * v_ref[...], dcv * cg_ref[...]]
        xv = x_ref[...]
        r = lax.rsqrt(_rowmean(xv * xv) + EPS)
        xn = xv * r
        g = vec_ref[0:1, :]
        hg = xn * g
        dh1 = None
        for j in range(ns):
            pb = parts[j].astype(BF16)
            dproj_ref[:, j * nc:(j + 1) * nc] = pb
            pj =lax.dot_general(pb, w_ref[j], (((1,), (1,)), ((), ())), preferred_element_type=F32)
            dh1 = pj if dh1 is None else dh1 + pj
        dhg = dh1 * vec_ref[1:2, :]
        _acc_rows(vp_ref, i == 0, [_colsum(dh1), _colsum(dh1 * hg), _colsum(dhg * xn)])
        dxn = dhg * g
        gx_ref[...] = dx2_ref[...] + r * (dxn - xn * _rowmean(dxn * xn))

    assert nc == dh and ns == 4
    return pl.pallas_call(
        body, name="mix_in_bwd", grid=(nblk,),
        out_shape=(jax.ShapeDtypeStruct((t, d), F32), jax.ShapeDtypeStruct((t, ns * nc), BF16),
                   jax.ShapeDtypeStruct((SUBLANES, d), F32), jax.ShapeDtypeStruct((dh, ds), F32),
                   jax.ShapeDtypeStruct((dh, ds), F32)),
        in_specs=[_rows(tb, nstate), _rows(tb, nstate), _resident(b_re.shape), _resident(b_im.shape), _rows(tb, dh),
                  _rows(tb, dh), _halo_next(tb, dh, t), _rows(tb, dh), _rows(tb, dh, 0), _rows(tb, dh, 2),
                  _rows(tb, dh, 3), _rows(tb, d), _rows(tb, d), _full(vec.shape), _full(v512.shape),
                  _full(convw.shape), _resident(w_in_st.shape)],
        out_specs=(_rows(tb, d), _rows(tb, ns * nc), _full((SUBLANES, d)), _full((dh, ds)), _full((dh, ds))),
        compiler_params=_cparams(("arbitrary",), VMEM_BIG),
    )(gt_re, gt_im, b_re, b_im, dy1, dcc, dcc, dbg, proj, proj, proj, x, dx2, vec, v512, convw, w_in_st)


def _matmul_tn(a, b, m, bn, out_dtype, name, diag=False, bt=TB_TN, after=None):
    t = a.shape[0]
    n = b.shape[1]
    bt = _blk(t, bt)
    nk = t // bt
    extra = [] if after is None else [after]
    a_map = (lambda j, k: (k, j)) if diag else (lambda j, k: (k, 0))

    def body(a_ref, b_ref, *rest):
        o_ref, acc_ref = rest[-2:]
        k = pl.program_id(1)

        @pl.when(k == 0)
        def _():
            acc_ref[...] = jnp.zeros(acc_ref.shape, F32)
        acc_ref[...] += _dot_tn(a_ref[...], b_ref[...])

        @pl.when(k == nk - 1)
        def _():
            o_ref[...] = acc_ref[...].astype(out_dtype)

    return pl.pallas_call(
        body, name=name, grid=(n // bn, nk),
        out_shape=jax.ShapeDtypeStruct((n // bn, m, bn), out_dtype),
        in_specs=[pl.BlockSpec((bt, m), a_map), pl.BlockSpec((bt, bn), lambda j, k: (k, j))]
        + [pl.BlockSpec(memory_space=pl.ANY)] * len(extra),
        out_specs=pl.BlockSpec((None, m, bn), lambda j, k: (j, 0, 0)),
        scratch_shapes=[pltpu.VMEM((m, bn), F32)],
        compiler_params=_cparams(("parallel", "arbitrary"), VMEM_BIG),
    )(a, b, *extra)


def _ssm_bgrad(d_bre, d_bim, bt_re, bt_im, rows_in, fold, tile_b):
    gh, cb = d_bre.shape
    nb = SSM_SPLIT
    rb = gh // nb
    gp = nb * cb
    p = fold.shape[1]

    def body(dr_ref, di_ref, br_ref, bi_ref, rin_ref, f_ref, tb_ref, dbr_ref, dbi_ref, rout_ref):
        row = lax.broadcasted_iota(jnp.int32, (rb, cb), 0)
        col = lax.broadcasted_iota(jnp.int32, (rb, cb), 1)
        mask = (row >> 4) == (col >> 6)
        gr = jnp.where(mask, dr_ref[...], 0.0)
        gi = jnp.where(mask, di_ref[...], 0.0)
        cr, ci = rin_ref[0:1, :], rin_ref[1:2, :]
        dbr_ref[...] = _split3_dot(cr * gr + ci * gi, f_ref[...])
        dbi_ref[...] = _split3_dot(cr * gi - ci * gr, f_ref[...])
        br = _split3_dot(br_ref[...], tb_ref[...])
        bi = _split3_dot(bi_ref[...], tb_ref[...])
        rout_ref[...] = jnp.zeros(rout_ref.shape, F32)
        rout_ref[0:1, :] = _colsum(br * gr + bi * gi)
        rout_ref[1:2, :] = _colsum(br * gi - bi * gr)

    dspec = pl.BlockSpec((rb, cb), lambda j: (j, 0))
    rspec = pl.BlockSpec((SUBLANES, cb), lambda j: (0, j))
    ospec = pl.BlockSpec((rb, p), lambda j: (j, 0))
    return pl.pallas_call(
        body, name="ssm_bgrad", grid=(nb,),
        out_shape=(jax.ShapeDtypeStruct((gh, p), F32), jax.ShapeDtypeStruct((gh, p), F32),
                   jax.ShapeDtypeStruct((SUBLANES, gp), F32)),
        in_specs=[dspec, dspec, ospec, ospec, rspec, _full(fold.shape), _full(tile_b.shape)],
        out_specs=(ospec, ospec, rspec),
        compiler_params=_cparams(("parallel",)),
    )(d_bre, d_bim, bt_re, bt_im, rows_in, fold, tile_b)


def _ssm_cgrad(d_cre, d_cim, fold):
    gp, cb = d_cre.shape
    nb = SSM_SPLIT
    rb = gp // nb
    h = fold.shape[1]

    def body(dr_ref, di_ref, f_ref, cr_ref, ci_ref):
        row = lax.broadcasted_iota(jnp.int32, (rb, cb), 0)
        col = lax.broadcasted_iota(jnp.int32, (rb, cb), 1)
        mask = (row >> 6) == (col >> 4)
        cr_ref[...] = _split3_dot(jnp.where(mask, dr_ref[...], 0.0), f_ref[...])
        ci_ref[...] = -_split3_dot(jnp.where(mask, di_ref[...], 0.0), f_ref[...])

    cspec = pl.BlockSpec((rb, cb), lambda j: (j, 0))
    ospec = pl.BlockSpec((rb, h), lambda j: (j, 0))
    return pl.pallas_call(
        body, name="ssm_cgrad", grid=(nb,),
        out_shape=(jax.ShapeDtypeStruct((gp, h), F32),) * 2,
        in_specs=[cspec, cspec, _full(fold.shape)], out_specs=(ospec, ospec),
        compiler_params=_cparams(("parallel",)),
    )(d_cre, d_cim, fold)


def _ssm_lamgrad(lam_re, lam_im, log_step, abar_re, abar_im, coef_re, coef_im, gc_re, gc_im, ga_re, ga_im):
    g, p = lam_re.shape

    def body(lr_ref, li_ref, ls_ref, ar_ref, ai_ref, cr_ref, ci_ref, gcr_ref, gci_ref, gar_ref, gai_ref,
             dlr_ref, dli_ref, dls_ref):
        lam_raw = lr_ref[...]
        lr = jnp.minimum(lam_raw, LAMBDA_RE_MAX)
        li = li_ref[...]
        st = jnp.exp(ls_ref[...])
        den = lr * lr + li * li
        gcr, gci = gcr_ref[...], gci_ref[...]
        gab_r = gar_ref[...] + (lr * gcr - li * gci) / den
        gab_i = gai_ref[...] + (lr * gci + li * gcr) / den
        cr, ci = cr_ref[...], ci_ref[...]
        wr = -(cr * lr + ci * li) / den
        wi = -(ci * lr - cr * li) / den
        gl_r = wr * gcr + wi * gci
        gl_i = wr * gci - wi * gcr
        ar, ai = ar_ref[...], ai_ref[...]
        gw_r = ar * gab_r + ai * gab_i
        gw_i = ar * gab_i - ai * gab_r
        gl_r = gl_r + st * gw_r
        gl_i = gl_i + st * gw_i
        pass_through = jnp.where(lam_raw < LAMBDA_RE_MAX, 1.0, jnp.where(lam_raw == LAMBDA_RE_MAX, 0.5, 0.0))
        dlr_ref[...] = gl_r * pass_through
        dli_ref[...] = gl_i
        dls_ref[...] = st * jnp.sum(lr * gw_r + li * gw_i, axis=1, keepdims=True)

    sds = jax.ShapeDtypeStruct((g, p), F32)
    return pl.pallas_call(body, name="ssm_lamgrad", out_shape=(sds, sds, jax.ShapeDtypeStruct((g, 1), F32)))(
        lam_re, lam_im, log_step, abar_re, abar_im, coef_re, coef_im, gc_re, gc_im, ga_re, ga_im)


def _row_block(r, most=256):
    for rb in range(min(r, most), BF16_ROWS - 1, -1):
        if r % rb == 0 and rb % BF16_ROWS == 0:
            return rb
    return r


def _adamw_math(w, g, m, v):
    m = ADAM_B1 * m + (1.0 - ADAM_B1) * g
    v = ADAM_B2 * v + (1.0 - ADAM_B2) * (g * g)
    m_hat = m / (1.0 - ADAM_B1 ** ADAM_STEP)
    v_hat = v / (1.0 - ADAM_B2 ** ADAM_STEP)
    delta = -ADAM_LR * (m_hat / (jnp.sqrt(v_hat) + ADAM_EPS) + ADAM_WD * w)
    return delta, m, v


def _adamw_big(p_mine, p_sib, w, m, v, name):
    r, c = w.shape
    rb = _row_block(r)

    def body(a_ref, b_ref, w_ref, m_ref, v_ref, g_ref, d_ref, mo_ref, vo_ref):
        g = a_ref[...].astype(F32) + b_ref[...].astype(F32)
        g_ref[...] = g
        d_ref[...], mo_ref[...], vo_ref[...] = _adamw_math(w_ref[...], g, m_ref[...], v_ref[...])

    spec = pl.BlockSpec((rb, c), lambda i: (i, 0))
    sds = jax.ShapeDtypeStruct((r, c), F32)
    return pl.pallas_call(
        body, name=name, grid=(r // rb,), out_shape=(sds,) * 4, in_specs=[spec] * 5, out_specs=(spec,) * 4,
        compiler_params=_cparams(("parallel",), VMEM_KEEP_OPERANDS_IN_HBM),
    )(p_mine, p_sib, w, m, v)


def _sum_blocks(stack, name):
    n, r, c = stack.shape
    rb = _row_block(r)

    def body(s_ref, o_ref):
        acc = s_ref[0].astype(F32)
        for k in range(1, n):
            acc = acc + s_ref[k].astype(F32)
        o_ref[...] = acc

    return pl.pallas_call(
        body, name=name, grid=(r // rb,), out_shape=jax.ShapeDtypeStruct((r, c), F32),
        in_specs=[pl.BlockSpec((n, rb, c), lambda i: (0, i, 0))], out_specs=pl.BlockSpec((rb, c), lambda i: (i, 0)),
        compiler_params=_cparams(("parallel",), VMEM_KEEP_OPERANDS_IN_HBM),
    )(stack)


def _sum_landed(landed, own, chip, name):
    n, r, c = landed.shape
    rb = _row_block(r)

    def body(chip_ref, own_ref, l1_ref, l2_ref, l3_ref, o_ref):
        acc = own_ref[0].astype(F32)
        for ref in (l1_ref, l2_ref, l3_ref):
            acc = acc + ref[0].astype(F32)
        o_ref[...] = acc.astype(BF16)

    def slot(k):
        return pl.BlockSpec((1, rb, c), lambda i, ch: ((ch[0] + k) % n, i, 0))

    return pl.pallas_call(
        body, name=name, out_shape=jax.ShapeDtypeStruct((r, c), BF16),
        grid_spec=pltpu.PrefetchScalarGridSpec(
            num_scalar_prefetch=1, grid=(r // rb,), in_specs=[slot(0), slot(1), slot(2), slot(3)],
            out_specs=pl.BlockSpec((rb, c), lambda i, ch: (i, 0))),
        compiler_params=_cparams(("parallel",), VMEM_KEEP_OPERANDS_IN_HBM),
    )(jnp.reshape(chip, (1,)).astype(jnp.int32), own, landed, landed, landed)


def _add2(a, b):
    def body(a_ref, b_ref, o_ref):
        o_ref[...] = a_ref[...] + b_ref[...]

    return pl.pallas_call(body, name="add_small", out_shape=jax.ShapeDtypeStruct(a.shape, F32))(a, b)


def _adamw_ada(c_all, dmod_cols, w, m, v):
    d, n = w.shape
    bn = 512

    def body(c_ref, dm_ref, w_ref, m_ref, v_ref, g_ref, d_ref, mo_ref, vo_ref):
        cc = c_ref[...]
        g = _dot_tn(cc * _sigmoid(cc), dm_ref[...])
        g_ref[...] = g
        d_ref[...], mo_ref[...], vo_ref[...] = _adamw_math(w_ref[...], g, m_ref[...], v_ref[...])

    spec = pl.BlockSpec((d, bn), lambda j: (0, j))
    sds = jax.ShapeDtypeStruct((d, n), F32)
    return pl.pallas_call(
        body, name="adamw_ada", grid=(n // bn,), out_shape=(sds,) * 4,
        in_specs=[_full((N_DEV, d)), pl.BlockSpec((N_DEV, bn), lambda j: (0, j)), spec, spec, spec],
        out_specs=(spec,) * 4, compiler_params=_cparams(("parallel",), VMEM_KEEP_OPERANDS_IN_HBM),
    )(c_all, dmod_cols, w, m, v)


def _adamw_small(items):
    n = len(items)

    def body(*refs):
        ins, outs = refs[:4 * n], refs[4 * n:]
        for k in range(n):
            w_ref, g_ref, m_ref, v_ref = ins[4 * k:4 * k + 4]
            outs[3 * k][...], outs[3 * k + 1][...], outs[3 * k + 2][...] = _adamw_math(
                w_ref[...], g_ref[...], m_ref[...], v_ref[...])

    flat = [a for it in items for a in it]
    out_shape = tuple(jax.ShapeDtypeStruct(it[0].shape, F32) for it in items for _ in range(3))
    res = pl.pallas_call(body, name="adamw_small", out_shape=out_shape,
                         compiler_params=_cparams(vmem=VMEM_KEEP_OPERANDS_IN_HBM))(*flat)
    return [tuple(res[3 * k:3 * k + 3]) for k in range(n)]


def _group_mean_matrix(n, group):
    idx = np.arange(n) // group
    return (idx[:, None] == idx[None, :]).astype(np.float32) / group


def _fold_matrix(n, period):
    return (np.arange(n)[:, None] % period == np.arange(period)[None, :]).astype(np.float32)


def _rows8(*rows):
    c = rows[0].shape[-1]
    pad = jnp.zeros((SUBLANES - len(rows), c), F32)
    return jnp.concatenate([r.reshape(1, c) for r in rows] + [pad], axis=0)


def _to_rows(a, width):
    flat = a.reshape(-1)
    n = -(-flat.shape[0] // width)
    flat = jnp.pad(flat, (0, n * width - flat.shape[0]))
    return flat.reshape(n, width)


def kernel(x, c, w_ada, b_ada, g_pre_mix, g_post_mix, w_in, ssm_lam_re, ssm_lam_im, ssm_log_step, ssm_b_re, ssm_b_im, ssm_c_re, ssm_c_im, ssm_d, glu_w, glu_b, g_out_ssm, conv_w, g_out_conv, w_out, g_pre_ffn, g_post_ffn, w_up, ffn_conv_w, w_down, loss_target, m_w_ada, m_b_ada, m_g_pre_mix, m_g_post_mix, m_w_in, m_ssm_lam_re, m_ssm_lam_im, m_ssm_log_step, m_ssm_b_re, m_ssm_b_im, m_ssm_c_re, m_ssm_c_im, m_ssm_d, m_glu_w, m_glu_b, m_g_out_ssm, m_conv_w, m_g_out_conv, m_w_out, m_g_pre_ffn, m_g_post_ffn, m_w_up, m_ffn_conv_w, m_w_down, v_w_ada, v_b_ada, v_g_pre_mix, v_g_post_mix, v_w_in, v_ssm_lam_re, v_ssm_lam_im, v_ssm_log_step, v_ssm_b_re, v_ssm_b_im, v_ssm_c_re, v_ssm_c_im, v_ssm_d, v_glu_w, v_glu_b, v_g_out_ssm, v_conv_w, v_g_out_conv, v_w_out, v_g_pre_ffn, v_g_post_ffn, v_w_up, v_ffn_conv_w, v_w_down):
    xs = x[0]
    tgt = loss_target[0]
    t, d = xs.shape
    xi, yi, ci = lax.axis_index("x"), lax.axis_index("y"), lax.axis_index("c")
    chip = 2 * xi + yi
    dev = 2 * chip + ci

    n_groups, n_state = ssm_lam_re.shape[1:]
    n_gch = ssm_b_re.shape[3]
    d_ssm = n_groups * n_gch
    gp = n_groups * n_state
    n_ada = w_ada.shape[2]
    d_ff = w_down.shape[1] * N_CHIPS
    n_upc = w_up.shape[2]

    w_names = ("w_in", "glu_w", "w_out", "w_up", "w_down")
    c_gath = _allgather8(jnp.broadcast_to(c, (SUBLANES, d)), "gather_c")
    c_all = c_gath.reshape(N_DEV, SUBLANES, d)[:, 0, :]

    def pad8(a):
        return jnp.concatenate([a, jnp.zeros((SUBLANES - a.shape[0], a.shape[1]), a.dtype)], axis=0)

    def start(name, arrs, after):
        return _chips_start(name, True, [], [_landing(a, chip) for a in arrs], after)

    w_names = ("w_in", "mod", "conv_w", "ffn_conv_w", "glu_w", "w_out", "w_up", "w_down")
    first = start("weights_start_in", [w_in[0].astype(BF16)], c_gath)
    b_sh = lax.dynamic_slice(b_ada, (0, chip * n_ada), (1, n_ada))
    mod_sh = _mod_shard(c_all + first[4][0:1, 0:1], w_ada[0], b_sh)
    second = start("weights_start_rest", [mod_sh, pad8(conv_w[0]), pad8(ffn_conv_w[0])]
                   + [w[0].astype(BF16) for w in (glu_w, w_out, w_up, w_down)], None)
    w_send, w_recv, w_land = [list(first[k]) + list(second[k]) for k in (0, 1, 3)]
    w_token = second[4]

    def weights(names, after):
        ks = [w_names.index(nm) for nm in names]
        return _chips_wait("weights_wait_" + names[-1], True, [w_send[k] for k in ks], [w_recv[k] for k in ks],
                           [], [w_land[k] for k in ks], after)[1]

    lam_re, lam_im = ssm_lam_re[0], ssm_lam_im[0]
    log_step = ssm_log_step[0].reshape(n_groups, 1) + w_token[0:1, 0:1]
    abar_re, abar_im, coef_re, coef_im = _ssm_prep(lam_re, lam_im, log_step)
    a_rows = _rows8(abar_re.reshape(1, gp), abar_im.reshape(1, gp))
    coef_rows = _rows8(coef_re.reshape(1, gp), coef_im.reshape(1, gp))
    bt_re = ssm_b_re[0].transpose(0, 2, 1).reshape(d_ssm, n_state)
    bt_im = ssm_b_im[0].transpose(0, 2, 1).reshape(d_ssm, n_state)
    ct_re = ssm_c_re[0].transpose(0, 2, 1).reshape(gp, n_gch)
    ct_im = ssm_c_im[0].transpose(0, 2, 1).reshape(gp, n_gch)
    tile_b = jnp.asarray(np.tile(np.eye(n_state), (1, n_groups // SSM_SPLIT)), BF16)
    tile_c = jnp.asarray(np.tile(np.eye(n_gch), (1, n_groups)), BF16)
    bblk_re, bblk_im, cblk_re, cblk_im = _ssm_blocks(bt_re, bt_im, ct_re, ct_im, coef_rows, tile_b, tile_c)

    h16 = jnp.asarray(_group_mean_matrix(d_ssm, n_gch), BF16)
    h64 = jnp.asarray(_group_mean_matrix(d_ssm, CONV_HEAD_DIM), BF16)

    g_mod, g_cw, g_fw, w_in_st = weights(("mod", "conv_w", "ffn_conv_w", "w_in"), bblk_re)
    mod_all = g_mod.transpose(1, 0, 2).reshape(N_DEV, N_CHIPS * n_ada)
    mod = lax.dynamic_slice(mod_all, (dev, 0), (1, N_CHIPS * n_ada))
    sh1, sc1, gt1, sh2, sc2, gt2 = [mod[:, k * d:(k + 1) * d] for k in range(6)]
    convw_full = pad8(g_cw[:, :3, :].transpose(1, 0, 2).reshape(3, d_ssm))
    fw_full = pad8(g_fw[:, :3, :].transpose(1, 0, 2).reshape(3, N_CHIPS * n_upc))

    v512 = _rows8(ssm_d, glu_b, g_out_ssm, g_out_conv)
    vec1 =_rows8(g_pre_mix, 1.0 + sc1, sh1)
    vd1 = _rows8(g_post_mix, gt1)
    vec2 = _rows8(g_pre_ffn, 1.0 + sc2, sh2)
    vd2 = _rows8(g_post_ffn, gt2)

    proj, bu_re, bu_im, h1b = _mix_in(xs, vec1, w_in_st, bblk_re, bblk_im)
    s_re, s_im = _scan_fwd(a_rows, bu_re, bu_im)
    g_glu, g_wout = weights(("glu_w", "w_out"), s_re)
    glu_full = g_glu.reshape(d_ssm, d_ssm)
    w_out_full = g_wout.reshape(2 * d_ssm, d)
    y1, o_mix, x2 = _mix_out(xs, proj, s_re, s_im, cblk_re, cblk_im, v512, convw_full, glu_full, h16, h64,
                             w_out_full, vd1)
    (w_up_st,) = weights(("w_up",), x2)
    up, h2b = _ffn_up(x2, vec2, w_up_st)
    (g_wdown,) = weights(("w_down",), up)
    w_down_full = g_wdown.reshape(d_ff, d)
    actb, ddnb, dout, dhid, vp_dn, loss_blk = _ffn_down(up, fw_full, w_down_full, w_down_full.T, x2, tgt, vd2)

    gw_down = _matmul_tn(actb, ddnb, d_ff, d, BF16, "dw_down", bt=1024).reshape(N_CHIPS, d_ff // N_CHIPS, d)
    dx2, dupb, vp_up, df_rows = _ffn_up_bwd(dhid, up, fw_full, x2, dout, vec2, w_up_st)
    gw_up = _matmul_tn(h2b, dupb, d, n_upc, BF16, "dw_up", bt=2048)
    ga_send, ga_recv, ga_src, ga_land, ga_token = _chips_start(
        "grads_start_ffn", False, [gw_down, gw_up], [lax.empty(g.shape, g.dtype) for g in (gw_down, gw_up)])
    (dob, ycatb, zb, dqb, dy1, g_re, g_im, dcc, dbg, vp_mo, vp5, d_cre, d_cim) = _mix_out_bwd(
        dx2, o_mix, y1, proj, s_re, s_im, cblk_re, cblk_im, v512, convw_full, glu_full, h16, h64, w_out_full,
        vd1 + ga_token[0:1, 0:1])
    gw_out = _matmul_tn(ycatb, dob, 2 * d_ssm, d, BF16, "dw_out", bt=2048)
    gw_out = gw_out.reshape(N_CHIPS, 2 * d_ssm // N_CHIPS, d)
    gw_glu = _matmul_tn(zb, dqb, d_ssm, d_ssm, BF16, "dw_glu", bt=2048).reshape(N_CHIPS, d_ssm // N_CHIPS, d_ssm)
    gb_send, gb_recv, gb_src, gb_land, gb_token = _chips_start(
        "grads_start_mix", False, [gw_out, gw_glu], [lax.empty(g.shape, g.dtype) for g in (gw_out, gw_glu)])
    gt_re, gt_im, ga_re8, ga_im8 = _scan_bwd(a_rows + gb_token[0:1, 0:1], g_re, g_im, s_re, s_im)
    grad_x, dprojb, vp_mi, d_bre, d_bim = _mix_in_bwd(gt_re, gt_im, bblk_re, bblk_im, dy1, dcc, dbg, proj, xs, dx2,
                                                      vec1, v512, convw_full, w_in_st)
    ssm_u, ssm_s = d_ssm // SSM_SPLIT, gp // SSM_SPLIT

    fold_b = jnp.asarray(_fold_matrix(ssm_s, n_state), BF16)
    fold_c = jnp.asarray(_fold_matrix(ssm_u, n_gch), BF16)
    db_re_f, db_im_f, gc_rows = _ssm_bgrad(d_bre, d_bim, bt_re, bt_im, coef_rows, fold_b, tile_b)
    dc_re_f, dc_im_f = _ssm_cgrad(d_cre, d_cim, fold_c)
    ga_sum = _ga_rowsum(ga_re8, ga_im8)
    g_lam_re, g_lam_im, g_log_step = _ssm_lamgrad(
        lam_re, lam_im, log_step, abar_re, abar_im, coef_re, coef_im,
        gc_rows[0].reshape(n_groups, n_state), gc_rows[1].reshape(n_groups, n_state),
        ga_sum[0].reshape(n_groups, n_state), ga_sum[1].reshape(n_groups, n_state))
    g_b_re = db_re_f.reshape(n_groups, n_gch, n_state).transpose(0, 2, 1)
    g_b_im = db_im_f.reshape(n_groups, n_gch, n_state).transpose(0, 2, 1)
    g_c_re = dc_re_f.reshape(n_groups, n_state, n_gch).transpose(0, 2, 1)
    g_c_im = dc_im_f.reshape(n_groups, n_state, n_gch).transpose(0, 2, 1)

    dmod = jnp.concatenate([vp_mi[0:1], vp_mi[1:2], vp_mo[0:1], vp_up[0:1], vp_up[1:2], vp_dn[0:1]], axis=1)
    small = [
        ("g_pre_mix", vp_mi[2:3]), ("g_post_mix", vp_mo[1:2]), ("g_pre_ffn", vp_up[2:3]), ("g_post_ffn", vp_dn[1:2]),
        ("ssm_lam_re", g_lam_re), ("ssm_lam_im", g_lam_im), ("ssm_log_step", g_log_step),
        ("ssm_b_re", g_b_re), ("ssm_b_im", g_b_im), ("ssm_c_re", g_c_re), ("ssm_c_im", g_c_im),
        ("ssm_d", vp5[3:4]), ("glu_b", vp5[2:3]), ("g_out_ssm", vp5[0:1]), ("g_out_conv", vp5[1:2]),
        ("conv_w", vp5[4:7]), ("ffn_conv_w", df_rows[0:3]), ("loss", loss_blk[0:1, 0:1]),
    ]
    packed, offsets, row = [], {}, 0
    for name, a in small:
        r = _to_rows(a, d)
        offsets[name] = (row, a.shape)
        packed.append(r)
        row += r.shape[0]
    n_small = -(-row // SUBLANES) * SUBLANES
    packed.append(jnp.zeros((n_small - row, d), F32))
    packed.append(pad8(dmod.reshape(6, d)))
    pack = jnp.concatenate(packed, axis=0)
    sm_send, sm_recv, _, sm_land, sm_token = _chips_start("small_start", True, [], [_landing(pack, chip)])

    gw_in = _matmul_tn(h1b, dprojb, d, w_in.shape[2], BF16, "dw_in", bt=2048, after=sm_token)
    gc_send, gc_recv, gc_src, gc_land, gc_token = _chips_start(
        "grads_start_in", False, [gw_in], [lax.empty(gw_in.shape, gw_in.dtype)])

    def partials(names, own, landed):
        return [_sum_landed(l, o, chip, "sum_" + nm) for l, o, nm in zip(landed, own, names)]

    def update(names, mine, theirs):
        done = {}
        for nm, pm, ps in zip(names, mine, theirs):
            w_, m_, v_ = big_params[nm]
            done[nm] = _adamw_big(pm, ps, w_[0], m_[0], v_[0], "adamw_" + nm)
        return done

    big_params = {"w_down": (w_down, m_w_down, v_w_down), "w_up": (w_up, m_w_up, v_w_up),
                  "w_out": (w_out, m_w_out, v_w_out), "glu_w": (glu_w, m_glu_w, v_glu_w),
                  "w_in": (w_in, m_w_in, v_w_in)}
    ffn_names, mix_names = ("w_down", "w_up"), ("w_out", "glu_w", "w_in")
    p_ffn = partials(ffn_names, *_chips_wait("grads_wait_ffn", False, ga_send, ga_recv, ga_src, ga_land, gc_token))
    sa_send, sa_recv, sa_src, sa_land, sa_token = _sibling_start("swap_start_ffn", p_ffn)

    (sm_landed,) = _chips_wait("small_wait", True, sm_send, sm_recv, [], sm_land, sa_token)[1]
    sm_part = _sum_blocks(sm_landed, "sum_small")
    dmod_mine = sm_landed[:, n_small:n_small + SUBLANES, :]
    ss_send, ss_recv, ss_src, ss_land, ss_token = _sibling_start("swap_start_small", [sm_part, dmod_mine])
    p_ffn, t_ffn = _sibling_wait("swap_wait_ffn", sa_send, sa_recv, sa_src, sa_land, ss_token)
    big = update(ffn_names, p_ffn, t_ffn)
    (sm_part, dmod_mine), (sm_sib, dmod_sib) = _sibling_wait("swap_wait_small", ss_send, ss_recv, ss_src, ss_land,
                                                              big["w_up"][0])
    sums = _add2(sm_part, sm_sib)
    dmod_by_core = jnp.stack([dmod_mine, dmod_sib], axis=1)
    dmod_by_core = jnp.where(ci == 0, dmod_by_core, dmod_by_core[:, ::-1])
    dmod_all = dmod_by_core[:, :, :6, :].reshape(N_DEV, 6 * d)
    g_b_ada = sums[n_small:n_small + 6].reshape(1, 6 * d)

    def unpack(name):
        r0, shape = offsets[name]
        size = math.prod(shape)
        nrow = -(-size // d)
        return sums[r0:r0 + nrow].reshape(-1)[:size].reshape(shape)

    p_mix = partials(mix_names, *_chips_wait(
        "grads_wait_mix", False, list(gb_send) + list(gc_send), list(gb_recv) + list(gc_recv),
        list(gb_src) + list(gc_src), list(gb_land) + list(gc_land), sums))
    sb_send, sb_recv, sb_src, sb_land, sb_token = _sibling_start("swap_start_mix", p_mix)

    dmod_cols = lax.dynamic_slice(dmod_all, (0, chip * n_ada), (N_DEV, n_ada)) + sb_token[0:1, 0:1]
    ada = _adamw_ada(c_all, dmod_cols, w_ada[0], m_w_ada[0], v_w_ada[0])
    p_mix, t_mix = _sibling_wait("swap_wait_mix", sb_send, sb_recv, sb_src, sb_land, ada[0])
    big.update(update(mix_names, p_mix, t_mix))

    g_small = {name: unpack(name) for name, _ in small}
    g_small["b_ada"] = g_b_ada
    g_small["conv_w"] = lax.dynamic_slice(g_small["conv_w"], (0, chip * conv_w.shape[2]), (3, conv_w.shape[2]))
    g_small["ffn_conv_w"] = lax.dynamic_slice(g_small["ffn_conv_w"], (0, chip * n_upc), (3, n_upc))
    g_small["ssm_log_step"] = g_small["ssm_log_step"].reshape(1, n_groups)
    small_params = {
        "b_ada": (b_ada, m_b_ada, v_b_ada), "g_pre_mix": (g_pre_mix, m_g_pre_mix, v_g_pre_mix),
        "g_post_mix": (g_post_mix, m_g_post_mix, v_g_post_mix), "ssm_lam_re": (ssm_lam_re, m_ssm_lam_re, v_ssm_lam_re),
        "ssm_lam_im": (ssm_lam_im, m_ssm_lam_im, v_ssm_lam_im),
        "ssm_log_step": (ssm_log_step, m_ssm_log_step, v_ssm_log_step),
        "ssm_b_re": (ssm_b_re, m_ssm_b_re, v_ssm_b_re), "ssm_b_im": (ssm_b_im, m_ssm_b_im, v_ssm_b_im),
        "ssm_c_re": (ssm_c_re, m_ssm_c_re, v_ssm_c_re), "ssm_c_im": (ssm_c_im, m_ssm_c_im, v_ssm_c_im),
        "ssm_d": (ssm_d, m_ssm_d, v_ssm_d), "glu_b": (glu_b, m_glu_b, v_glu_b),
        "g_out_ssm": (g_out_ssm, m_g_out_ssm, v_g_out_ssm), "conv_w": (conv_w, m_conv_w, v_conv_w),
        "g_out_conv": (g_out_conv, m_g_out_conv, v_g_out_conv), "g_pre_ffn": (g_pre_ffn, m_g_pre_ffn, v_g_pre_ffn),
        "g_post_ffn": (g_post_ffn, m_g_post_ffn, v_g_post_ffn),
        "ffn_conv_w": (ffn_conv_w, m_ffn_conv_w, v_ffn_conv_w),
    }

    def natural(a):
        return a[0] if a.ndim > 2 else a

    names = list(small_params)
    items = []
    for nm in names:
        w_, m_, v_ = small_params[nm]
        items.append((natural(w_), g_small[nm].reshape(natural(w_).shape), natural(m_), natural(v_)))
    upd = _adamw_small(items)
    small_out = {}
    for nm, (dl, mo, vo) in zip(names, upd):
        shp = small_params[nm][0].shape
        small_out[nm] = (g_small[nm].reshape(shp), dl.reshape(shp), mo.reshape(shp), vo.reshape(shp))

    loss = g_small["loss"][0, 0]

    order = ["w_ada", "b_ada", "g_pre_mix", "g_post_mix", "w_in", "ssm_lam_re", "ssm_lam_im", "ssm_log_step",
             "ssm_b_re", "ssm_b_im", "ssm_c_re", "ssm_c_im", "ssm_d", "glu_w", "glu_b", "g_out_ssm", "conv_w",
             "g_out_conv", "w_out", "g_pre_ffn", "g_post_ffn", "w_up", "ffn_conv_w", "w_down"]
    results = {"w_ada": tuple(a[None] for a in ada)}
    for nm in big:
        results[nm] = tuple(a[None] for a in big[nm])
    results.update(small_out)
    outs = [loss, grad_x[None]]
    for k in range(4):
        outs += [results[nm][k] for nm in order]
    return tuple(outs)


def _ga_rowsum(ga_re8, ga_im8):
    n = ga_re8.shape[1]

    def body(r_ref, i_ref, o_ref):
        o_ref[...] = jnp.zeros(o_ref.shape, F32)
        o_ref[0:1, :] = _colsum(r_ref[...])
        o_ref[1:2, :] = _colsum(i_ref[...])

    return pl.pallas_call(body, name="ga_rowsum", out_shape=jax.ShapeDtypeStruct((SUBLANES, n), F32))(ga_re8, ga_im8)
```

```python
import functools
import math

import jax
import jax.numpy as jnp
import numpy as np
from jax import lax
from jax.experimental import pallas as pl
from jax.experimental.pallas import tpu as pltpu

F32 = jnp.float32
BF16 = jnp.bfloat16
MESH = pl.DeviceIdType.MESH

EPS = 1e-6
LAMBDA_RE_MAX = -1e-4
ADAM_LR = 0.001
ADAM_B1 = 0.9
ADAM_B2 = 0.999
ADAM_EPS = 1e-08
ADAM_WD = 0.01
ADAM_STEP = 10

SUBLANES = 8
BF16_ROWS = 16
N_CHIPS = 4
N_DEV = 8
CONV_HEAD_DIM = 64
VMEM_BIG = 56 * 1024 * 1024
VMEM_MID = 40 * 1024 * 1024
VMEM_KEEP_OPERANDS_IN_HBM = 62 * 1024 * 1024

TB_MIX = 256
TB_FFN = 256
TB_FFN_UP = 512
TB_SCAN = 1024
W_SCAN = 256
SSM_SPLIT = 4
CW_FFN = 256
SCAN_UNROLL = 4
TB_TN = 512


def _cparams(sem=None, vmem=None):
    kw = {}
    if sem is not None:
        kw["dimension_semantics"] = sem
    if vmem is not None:
        kw["vmem_limit_bytes"] = vmem
    return pltpu.CompilerParams(**kw)


def _blk(t, pref):
    return pref if t % pref == 0 else t


def _dot(a, b):
    return jnp.dot(a.astype(BF16), b.astype(BF16), preferred_element_type=F32)


def _dot_nt(a, b):
    return lax.dot_general(a.astype(BF16), b.astype(BF16), (((1,), (1,)), ((), ())),
                           preferred_element_type=F32)


def _dot_tn(a, b):
    return lax.dot_general(a.astype(BF16), b.astype(BF16), (((0,), (0,)), ((), ())),
                           preferred_element_type=F32)


def _sigmoid(x):
    return 0.5 * jnp.tanh(0.5 * x) + 0.5


_GELU_K = math.sqrt(2.0 / math.pi)
_GELU_C = 0.044715


def _gelu(x):
    th = jnp.tanh(_GELU_K * (x + _GELU_C * x * x * x))
    return x * (0.5 * (1.0 + th))


def _gelu_and_grad(x):
    x2 = x * x
    th = jnp.tanh(_GELU_K * (x + _GELU_C * x2 * x))
    half = 0.5 * (1.0 + th)
    return x * half, half + 0.5 * x * (1.0 - th * th) * _GELU_K * (1.0 + 3.0 * _GELU_C * x2)


def _rowmean(x):
    return jnp.mean(x, axis=-1, keepdims=True)


def _colsum(x):
    return jnp.sum(x, axis=0, keepdims=True)


def _split_dot(x, m):
    hi = x.astype(BF16)
    lo = (x - hi.astype(F32)).astype(BF16)
    return (jnp.dot(hi, m, preferred_element_type=F32) + jnp.dot(lo, m, preferred_element_type=F32))


def _split3_dot(x, m):
    hi = x.astype(BF16)
    r1 = x - hi.astype(F32)
    mid = r1.astype(BF16)
    lo = (r1 - mid.astype(F32)).astype(BF16)
    return (jnp.dot(hi, m, preferred_element_type=F32) + jnp.dot(mid, m, preferred_element_type=F32)
            + jnp.dot(lo, m, preferred_element_type=F32))


def _shift_down(x, halo, k):
    r = pltpu.roll(x, k, 0)
    row = lax.broadcasted_iota(jnp.int32, x.shape, 0)
    for j in range(k):
        r = jnp.where(row == j, halo[SUBLANES - k + j:SUBLANES - k + j + 1, :], r)
    return r


def _shift_up(x, halo, k):
    n = x.shape[0]
    r = pltpu.roll(x, n - k, 0)
    row = lax.broadcasted_iota(jnp.int32, x.shape, 0)
    for j in range(k):
        r = jnp.where(row == n - k + j, halo[j:j + 1, :], r)
    return r


def _acc_rows(ref, first, rows):
    @pl.when(first)
    def _():
        ref[...] = jnp.zeros(ref.shape, ref.dtype)
    for j, r in enumerate(rows):
        ref[j:j + 1, :] += r


def _rows(tb, c, col=0):
    return pl.BlockSpec((tb, c), lambda i, col=col: (i, col))


def _full(shape):
    nd = len(shape)
    return pl.BlockSpec(shape, lambda i, nd=nd: (0,) * nd)


def _resident(shape):
    nd = len(shape)
    return pl.BlockSpec(shape, lambda i, nd=nd: (0,) * nd, pipeline_mode=pl.Buffered(1))


def _halo_prev(tb, c, col=0):
    per = tb // SUBLANES
    return pl.BlockSpec((SUBLANES, c), lambda i, col=col: (jnp.maximum(i * per - 1, 0), col))


def _halo_next(tb, c, t, col=0, rows=SUBLANES):
    per = tb // rows
    last = t // rows - 1
    return pl.BlockSpec((rows, c), lambda i, col=col: (jnp.minimum((i + 1) * per, last), col))


def _mesh_pos():
    return lax.axis_index("x"), lax.axis_index("y"), lax.axis_index("c")


def _allgather8(x_pad, name):
    m_per, n = x_pad.shape

    def body(x_ref, out_ref, send_sems, recv_sems, local_sem):
        x, y, c = _mesh_pos()
        me, sibling = (x, y, c), (x, y, 1 - c)
        chips = [(1 - x, y), (x, 1 - y), (1 - x, 1 - y)]

        def rows(px, py, pc):
            return out_ref.at[pl.ds((4 * px + 2 * py + pc) * m_per, m_per), :]

        def copy(k, block, to, src=None):
            return pltpu.make_async_remote_copy(
                src_ref=rows(*block) if src is None else src, dst_ref=rows(*block),
                send_sem=send_sems.at[k], recv_sem=recv_sems.at[k], device_id=to, device_id_type=MESH)

        mine = pltpu.make_async_copy(x_ref, rows(*me), local_sem)
        mine.start()
        first = [copy(0, me, sibling, src=x_ref)]
        first += [copy(1 + j, me, (*chip, c), src=x_ref) for j, chip in enumerate(chips)]
        for cp in first:
            cp.start()
        passed = [copy(4 + j, (*chip, c), sibling) for j, chip in enumerate(chips)]
        for j, chip in enumerate(chips):
            copy(1 + j, (*chip, c), me).wait_recv()
            passed[j].start()
        copy(0, sibling, me).wait_recv()
        for j, chip in enumerate(chips):
            copy(4 + j, (*chip, 1 - c), me).wait_recv()
        for cp in first + passed:
            cp.wait_send()
        mine.wait()

    return pl.pallas_call(
        body, name=name,
        out_shape=jax.ShapeDtypeStruct((N_DEV * m_per, n), F32),
        in_specs=[pl.BlockSpec(memory_space=pltpu.VMEM)],
        out_specs=pl.BlockSpec(memory_space=pltpu.VMEM),
        scratch_shapes=[pltpu.SemaphoreType.DMA((7,)), pltpu.SemaphoreType.DMA((7,)), pltpu.SemaphoreType.DMA],
    )(x_pad)


_HBM = pl.BlockSpec(memory_space=pltpu.HBM)
_SEM = pl.BlockSpec(memory_space=pltpu.SEMAPHORE)
_EFFECT = pltpu.SideEffectType.DATAFLOW_SIDE_EFFECTING


def _chip_copy(gather, src_ref, land_ref, send, recv, j, arrival):
    x, y, c = _mesh_pos()
    peer = [(1 - x, y), (x, 1 - y), (1 - x, 1 - y)][j]
    peer_chip = 2 * peer[0] + peer[1]
    my_chip = 2 * x + y
    return pltpu.make_async_remote_copy(
        src_ref=land_ref.at[my_chip] if gather else src_ref.at[peer_chip],
        dst_ref=land_ref.at[peer_chip if arrival else my_chip],
        send_sem=send.at[j], recv_sem=recv.at[j], device_id=(*peer, c), device_id_type=MESH)


def _chips_start(name, gather, srcs, lands, after=None):
    n, ns = len(lands), len(srcs)
    extra = [] if after is None else [after]

    def body(*refs):
        src_refs, land_refs = refs[:ns], refs[ns:ns + n]
        outs = refs[ns + n + len(extra):]
        sends, recvs, token = outs[:n], outs[n:2 * n], outs[-1]
        for k in range(n):
            for j in range(3):
                _chip_copy(gather, src_refs[k] if ns else None, land_refs[k], sends[k], recvs[k], j, False).start()
        token[...] = jnp.zeros(token.shape, F32)

    sem = pltpu.SemaphoreType.DMA((3,))
    thru = tuple(pltpu.HBM(a.shape, a.dtype) for a in list(srcs) + list(lands))
    res = pl.pallas_call(
        body, name=name,
        out_shape=(sem,) * (2 * n) + thru + (jax.ShapeDtypeStruct((SUBLANES, 128), F32),),
        in_specs=[_HBM] * (ns + n) + [pl.BlockSpec(memory_space=pl.ANY)] * len(extra),
        out_specs=(_SEM,) * (2 * n) + (_HBM,) * (ns + n) + (pl.BlockSpec(memory_space=pltpu.VMEM),),
        input_output_aliases={k: 2 * n + k for k in range(ns + n)},
        compiler_params=pltpu.CompilerParams(has_side_effects=_EFFECT),
    )(*[pltpu.with_memory_space_constraint(a, pltpu.HBM) for a in list(srcs) + list(lands)], *extra)
    return res[:n], res[n:2 * n], res[2 * n:2 * n + ns], res[2 * n + ns:2 * n + ns + n], res[-1]


def _chips_wait(name, gather, sends, recvs, srcs, lands, after):
    n, ns = len(lands), len(srcs)

    def body(*refs):
        src_refs, land_refs = refs[:ns], refs[ns:ns + n]
        sends_, recvs_ = refs[ns + n:ns + 2 * n], refs[ns + 2 * n:ns + 3 * n]
        for k in range(n):
            for j in range(3):
                cp = _chip_copy(gather, src_refs[k] if ns else None, land_refs[k], sends_[k], recvs_[k], j, True)
                cp.wait_send()
                cp.wait_recv()

    thru = tuple(pltpu.HBM(a.shape, a.dtype) for a in list(srcs) + list(lands))
    res = pl.pallas_call(
        body, name=name, out_shape=thru,
        in_specs=[_HBM] * (ns + n) + [_SEM] * (2 * n) + [pl.BlockSpec(memory_space=pl.ANY)],
        out_specs=(_HBM,) * (ns + n),
        input_output_aliases={k: k for k in range(ns + n)},
        compiler_params=pltpu.CompilerParams(has_side_effects=_EFFECT),
    )(*srcs, *lands, *sends, *recvs, after)
    return res[:ns], res[ns:]


def _sibling_copy(src_ref, land_ref, send, recv):
    x, y, c = _mesh_pos()
    return pltpu.make_async_remote_copy(src_ref=src_ref, dst_ref=land_ref, send_sem=send.at[0], recv_sem=recv.at[0],
                                        device_id=(x, y, 1 - c), device_id_type=MESH)


def _sibling_start(name, arrs, after=None):
    n = len(arrs)
    extra = [] if after is None else [after]
    lands = [lax.empty(a.shape, a.dtype) for a in arrs]

    def body(*refs):
        src_refs, land_refs = refs[:n], refs[n:2 * n]
        outs = refs[2 * n + len(extra):]
        sends, recvs, token = outs[:n], outs[n:2 * n], outs[-1]
        for k in range(n):
            _sibling_copy(src_refs[k], land_refs[k], sends[k], recvs[k]).start()
        token[...] = jnp.zeros(token.shape, F32)

    sem = pltpu.SemaphoreType.DMA((1,))
    thru = tuple(pltpu.HBM(a.shape, a.dtype) for a in list(arrs) + lands)
    res = pl.pallas_call(
        body, name=name,
        out_shape=(sem,) * (2 * n) + thru + (jax.ShapeDtypeStruct((SUBLANES, 128), F32),),
        in_specs=[_HBM] * (2 * n) + [pl.BlockSpec(memory_space=pl.ANY)] * len(extra),
        out_specs=(_SEM,) * (2 * n) + (_HBM,) * (2 * n) + (pl.BlockSpec(memory_space=pltpu.VMEM),),
        input_output_aliases={k: 2 * n + k for k in range(2 * n)},
        compiler_params=pltpu.CompilerParams(has_side_effects=_EFFECT),
    )(*[pltpu.with_memory_space_constraint(a, pltpu.HBM) for a in list(arrs) + lands], *extra)
    return res[:n], res[n:2 * n], res[2 * n:3 * n], res[3 * n:4 * n], res[-1]


def _sibling_wait(name, sends, recvs, srcs, lands, after):
    n = len(srcs)

    def body(*refs):
        src_refs, land_refs = refs[:n], refs[n:2 * n]
        sends_, recvs_ = refs[2 * n:3 * n], refs[3 * n:4 * n]
        for k in range(n):
            cp = _sibling_copy(src_refs[k], land_refs[k], sends_[k], recvs_[k])
            cp.wait_send()
            cp.wait_recv()

    thru = tuple(pltpu.HBM(a.shape, a.dtype) for a in list(srcs) + list(lands))
    res = pl.pallas_call(
        body, name=name, out_shape=thru,
        in_specs=[_HBM] * (2 * n) + [_SEM] * (2 * n) + [pl.BlockSpec(memory_space=pl.ANY)],
        out_specs=(_HBM,) * (2 * n),
        input_output_aliases={k: k for k in range(2 * n)},
        compiler_params=pltpu.CompilerParams(has_side_effects=_EFFECT),
    )(*srcs, *lands, *sends, *recvs, after)
    return res[:n], res[n:]


def _landing(own, chip):
    zone = lax.empty((N_CHIPS,) + own.shape, own.dtype)
    return lax.dynamic_update_slice(zone, own[None], (chip,) + (0,) * own.ndim)


def _mod_shard(c_all, w_ada_sh, b_sh):
    d, n = w_ada_sh.shape
    bn = 512

    def body(c_ref, w_ref, b_ref, o_ref):
        cc = c_ref[...]
        ca = cc * _sigmoid(cc)
        o_ref[...] = _dot(ca, w_ref[...]) + b_ref[...]

    return pl.pallas_call(
        body, name="mod_shard", grid=(n // bn,),
        out_shape=jax.ShapeDtypeStruct((N_DEV, n), F32),
        in_specs=[_full((N_DEV, d)), pl.BlockSpec((d, bn), lambda j: (0, j)), pl.BlockSpec((1, bn), lambda j: (0, j))],
        out_specs=pl.BlockSpec((N_DEV, bn), lambda j: (0, j)),
        compiler_params=_cparams(("parallel",)),
    )(c_all, w_ada_sh, b_sh)


def _ssm_prep(lam_re, lam_im, log_step):
    g, p = lam_re.shape

    def body(lr_ref, li_ref, ls_ref, ar_ref, ai_ref, cr_ref, ci_ref):
        lr = jnp.minimum(lr_ref[...], LAMBDA_RE_MAX)
        li = li_ref[...]
        st = jnp.exp(ls_ref[...])
        mag = jnp.exp(lr * st)
        ar = mag * jnp.cos(li * st)
        ai = mag * jnp.sin(li * st)
        den = lr * lr + li * li
        nr = ar - 1.0
        ar_ref[...] = ar
        ai_ref[...] = ai
        cr_ref[...] = (nr * lr + ai * li) / den
        ci_ref[...] = (ai * lr - nr * li) / den

    sds = jax.ShapeDtypeStruct((g, p), F32)
    return pl.pallas_call(body, name="ssm_prep", out_shape=(sds,) * 4)(lam_re, lam_im, log_step)


def _ssm_blocks(bt_re, bt_im, ct_re, ct_im, coef_rows, tile_b, tile_c):
    gh, p = bt_re.shape
    gp, h = ct_re.shape
    nb = SSM_SPLIT
    cb, rb = gp // nb, gp // nb

    def body(btr, bti, ctr, cti, cf, tb_ref, tc_ref, bre_o, bim_o, cre_o, cim_o):
        j = pl.program_id(0)
        row = lax.broadcasted_iota(jnp.int32, (gh, cb), 0)
        col = lax.broadcasted_iota(jnp.int32, (gh, cb), 1) + j * cb
        mask = (row >> 4) == (col >> 6)
        cr, ci = cf[0:1, :], cf[1:2, :]
        br = _split3_dot(btr[...], tb_ref[...])
        bi = _split3_dot(bti[...], tb_ref[...])
        bre_o[...] = jnp.where(mask, br * cr - bi * ci, 0.0).astype(BF16)
        bim_o[...] = jnp.where(mask, br * ci + bi * cr, 0.0).astype(BF16)
        row2 = lax.broadcasted_iota(jnp.int32, (rb, gh), 0) + j * rb
        col2 = lax.broadcasted_iota(jnp.int32, (rb, gh), 1)
        mask2 = (row2 >> 6) == (col2 >> 4)
        cre_o[...] = jnp.where(mask2, _split3_dot(ctr[...], tc_ref[...]), 0.0).astype(BF16)
        cim_o[...] = jnp.where(mask2, _split3_dot(cti[...], tc_ref[...]), 0.0).astype(BF16)

    bspec = pl.BlockSpec((gh, cb), lambda j: (0, j))
    cspec = pl.BlockSpec((rb, gh), lambda j: (j, 0))
    cin = pl.BlockSpec((rb, h), lambda j: (j, 0))
    return pl.pallas_call(
        body, name="ssm_blocks", grid=(nb,),
        out_shape=(jax.ShapeDtypeStruct((gh, gp), BF16),) * 2 + (jax.ShapeDtypeStruct((gp, gh), BF16),) * 2,
        in_specs=[_full((gh, p)), _full((gh, p)), cin, cin, pl.BlockSpec((SUBLANES, cb), lambda j: (0, j)),
                  _full(tile_b.shape), _full(tile_c.shape)],
        out_specs=(bspec, bspec, cspec, cspec),
        compiler_params=_cparams(("parallel",)),
    )(bt_re, bt_im, ct_re, ct_im, coef_rows, tile_b, tile_c)


def _scan_consts(a_ref, reverse):
    w = a_ref.shape[1]
    ar1 = a_ref[0:1, :]
    ai1 = a_ref[1:2, :]
    if reverse:
        ai1 = -ai1
    pr, pi = [ar1], [ai1]
    for _ in range(1, SUBLANES):
        nr = pr[-1] * ar1 - pi[-1] * ai1
        ni = pr[-1] * ai1 + pi[-1] * ar1
        pr.append(nr)
        pi.append(ni)
    row = lax.broadcasted_iota(jnp.int32, (SUBLANES, w), 0)
    dist = (SUBLANES - 1 - row) if reverse else row

    def pick(vals):
        out = jnp.broadcast_to(vals[SUBLANES - 1], (SUBLANES, w))
        for r in range(SUBLANES - 1):
            out = jnp.where(dist == r, vals[r], out)
        return out

    p_r, p_i = pick(pr), pick(pi)
    steps = []
    for k in (1, 2, 4):
        steps.append((k, jnp.where(dist >= k, pr[k - 1], 0.0), jnp.where(dist >= k, pi[k - 1], 0.0)))
    a8 = (jnp.broadcast_to(pr[SUBLANES - 1], (SUBLANES, w)), jnp.broadcast_to(pi[SUBLANES - 1], (SUBLANES, w)))
    return row, p_r, p_i, steps, a8


def _scan_tile(xr, xi, cr, ci, consts, reverse):
    row, p_r, p_i, steps, (a8r, a8i) = consts
    for k, s_r, s_i in steps:
        sh = (SUBLANES - k) if reverse else k
        qr = pltpu.roll(xr, sh, 0)
        qi = pltpu.roll(xi, sh, 0)
        xr, xi = xr + s_r * qr - s_i * qi, xi + s_r * qi + s_i * qr
    outr = xr + p_r * cr - p_i * ci
    outi = xi + p_r * ci + p_i * cr
    e = 0 if reverse else SUBLANES - 1
    er = jnp.broadcast_to(xr[e:e + 1, :], xr.shape)
    ei = jnp.broadcast_to(xi[e:e + 1, :], xi.shape)
    return outr, outi, er + a8r * cr - a8i * ci, ei + a8r * ci + a8i * cr


def _scan_fwd(a_rows, bu_re, bu_im):
    t, n = bu_re.shape
    tb, w = _blk(t, TB_SCAN), W_SCAN
    ntile = tb // SUBLANES

    def body(a_ref, br_ref, bi_ref, sr_ref, si_ref, car, cai):
        @pl.when(pl.program_id(1) == 0)
        def _():
            car[...] = jnp.zeros(car.shape, F32)
            cai[...] = jnp.zeros(cai.shape, F32)
        consts = _scan_consts(a_ref, False)

        def pair(i, carry):
            o = pl.multiple_of(i * BF16_ROWS, BF16_ROWS)
            b_r = br_ref[pl.ds(o, BF16_ROWS), :].astype(F32)
            b_i = bi_ref[pl.ds(o, BF16_ROWS), :].astype(F32)
            outs = []
            for h in range(2):
                rows = slice(h * SUBLANES, (h + 1) * SUBLANES)
                outr, outi, ncr, nci = _scan_tile(b_r[rows, :], b_i[rows, :], carry[0], carry[1], consts, False)
                outs.append((outr, outi))
                carry = (ncr, nci)
            sr_ref[pl.ds(o, BF16_ROWS), :] = jnp.concatenate([outs[0][0], outs[1][0]], axis=0).astype(BF16)
            si_ref[pl.ds(o, BF16_ROWS), :] = jnp.concatenate([outs[0][1], outs[1][1]], axis=0).astype(BF16)
            return carry

        def pairs(i, carry):
            for s in range(SCAN_UNROLL // 2):
                carry = pair(i * (SCAN_UNROLL // 2) + s, carry)
            return carry

        cr, ci = lax.fori_loop(0, ntile // SCAN_UNROLL, pairs, (car[...], cai[...]))
        car[...] = cr
        cai[...] = ci

    spec = pl.BlockSpec((tb, w), lambda s, k: (k, s))
    sds = jax.ShapeDtypeStruct((t, n), BF16)
    return pl.pallas_call(
        body, name="scan_fwd", grid=(n // w, t // tb), out_shape=(sds, sds),
        in_specs=[pl.BlockSpec((SUBLANES, w), lambda s, k: (0, s)), spec, spec], out_specs=(spec, spec),
        scratch_shapes=[pltpu.VMEM((SUBLANES, w), F32), pltpu.VMEM((SUBLANES, w), F32)],
        compiler_params=_cparams(("parallel", "arbitrary"), VMEM_MID),
    )(a_rows, bu_re, bu_im)


def _scan_bwd(a_rows, g_re, g_im, s_re, s_im):
    t, n = g_re.shape
    tb, w = _blk(t, TB_SCAN), W_SCAN
    ntile = tb // SUBLANES
    npair = tb // BF16_ROWS
    nt = t // tb

    def body(a_ref, gr_ref, gi_ref, sr_ref, si_ref, or_ref, oi_ref, gar_ref, gai_ref, car, cai):
        @pl.when(pl.program_id(1) == 0)
        def _():
            car[...] = jnp.zeros(car.shape, F32)
            cai[...] = jnp.zeros(cai.shape, F32)
            gar_ref[...] = jnp.zeros(gar_ref.shape, F32)
            gai_ref[...] = jnp.zeros(gai_ref.shape, F32)
        consts = _scan_consts(a_ref, True)
        row = consts[0]

        def pair(i, carry):
            cr, ci, accr, acci = carry
            o = pl.multiple_of((npair - 1 - i) * BF16_ROWS, BF16_ROWS)
            s_r = sr_ref[pl.ds(o, BF16_ROWS), :].astype(F32)
            s_i = si_ref[pl.ds(o, BF16_ROWS), :].astype(F32)
            outs = [None, None]
            for h in (1, 0):
                rows = pl.ds(o + h * SUBLANES, SUBLANES)
                outr, outi, ncr, nci = _scan_tile(gr_ref[rows, :], gi_ref[rows, :], cr, ci, consts, True)
                outs[h] = (outr, outi)
                gnr = jnp.where(row == SUBLANES - 1, cr, pltpu.roll(outr, SUBLANES - 1, 0))
                gni = jnp.where(row == SUBLANES - 1, ci, pltpu.roll(outi, SUBLANES - 1, 0))
                sr = s_r[h * SUBLANES:(h + 1) * SUBLANES, :]
                si = s_i[h * SUBLANES:(h + 1) * SUBLANES, :]
                accr, acci = accr + sr * gnr + si * gni, acci + sr * gni - si * gnr
                cr, ci = ncr, nci
            or_ref[pl.ds(o, BF16_ROWS), :] = jnp.concatenate([outs[0][0], outs[1][0]], axis=0).astype(BF16)
            oi_ref[pl.ds(o, BF16_ROWS), :] = jnp.concatenate([outs[0][1], outs[1][1]], axis=0).astype(BF16)
            return cr, ci, accr, acci

        def pairs(i, carry):
            for s in range(SCAN_UNROLL // 2):
                carry = pair(i * (SCAN_UNROLL // 2) + s, carry)
            return carry

        cr, ci, accr, acci = lax.fori_loop(0, ntile // SCAN_UNROLL, pairs,
                                           (car[...], cai[...], gar_ref[...], gai_ref[...]))
        car[...] = cr
        cai[...] = ci
        gar_ref[...] = accr
        gai_ref[...] = acci

    spec = pl.BlockSpec((tb, w), lambda s, k: (nt - 1 - k, s))
    aspec = pl.BlockSpec((SUBLANES, w), lambda s, k: (0, s))
    sds = jax.ShapeDtypeStruct((t, n), BF16)
    asds = jax.ShapeDtypeStruct((SUBLANES, n), F32)
    return pl.pallas_call(
        body, name="scan_bwd", grid=(n // w, nt), out_shape=(sds, sds, asds, asds),
        in_specs=[aspec, spec, spec, spec, spec], out_specs=(spec, spec, aspec, aspec),
        scratch_shapes=[pltpu.VMEM((SUBLANES, w), F32), pltpu.VMEM((SUBLANES, w), F32)],
        compiler_params=_cparams(("parallel", "arbitrary"), VMEM_MID),
    )(a_rows, g_re, g_im, s_re, s_im)


def _mix_in(x, vec, w_in_st, b_re, b_im):
    t, d = x.shape
    ns, _, nc = w_in_st.shape
    dssm, nstate = b_re.shape
    du, ds = dssm // SSM_SPLIT, nstate // SSM_SPLIT
    tb = _blk(t, TB_MIX)

    def body(x_ref, vec_ref, w_ref, bre_ref, bim_ref, proj_ref, bur_ref, bui_ref, h1_ref):
        xv = x_ref[...]
        r = lax.rsqrt(_rowmean(xv * xv) + EPS)
        h = xv * r * vec_ref[0:1, :] * vec_ref[1:2, :] + vec_ref[2:3, :]
        hb = h.astype(BF16)
        h1_ref[...] = hb
        u = None
        for j in range(ns):
            pj = jnp.dot(hb, w_ref[j], preferred_element_type=F32)
            proj_ref[:, j * nc:(j + 1) * nc] = pj
            if j == 0:
                u = pj
        ub = u.astype(BF16)
        for q in range(SSM_SPLIT):
            rq, cq = slice(q * du, (q + 1) * du), slice(q * ds, (q + 1) * ds)
            bur_ref[:, cq] = jnp.dot(ub[:, rq], bre_ref[rq, cq], preferred_element_type=F32).astype(BF16)
            bui_ref[:, cq] = jnp.dot(ub[:, rq], bim_ref[rq, cq], preferred_element_type=F32).astype(BF16)

    return pl.pallas_call(
        body, name="mix_in", grid=(t // tb,),
        out_shape=(jax.ShapeDtypeStruct((t, ns * nc), F32), jax.ShapeDtypeStruct((t, nstate), BF16),
                   jax.ShapeDtypeStruct((t, nstate), BF16), jax.ShapeDtypeStruct((t, d), BF16)),
        in_specs=[_rows(tb, d), _full((SUBLANES, d)), _resident(w_in_st.shape), _resident(b_re.shape),
                  _resident(b_im.shape)],
        out_specs=(_rows(tb, ns * nc), _rows(tb, nstate), _rows(tb, nstate), _rows(tb, d)),
        compiler_params=_cparams(("parallel",), VMEM_BIG),
    )(x, vec, w_in_st, b_re, b_im)


def _head_ms(y, h_ref):
    return _split_dot(y * y, h_ref[...])


def _conv3(x, halo, w_ref):
    return w_ref[0:1, :] * _shift_down(x, halo, 2) + w_ref[1:2, :] * _shift_down(x, halo, 1) + w_ref[2:3, :] * x


def _mix_out(x, proj, s_re, s_im, c_re, c_im, v512, convw, glu_w, h16, h64, w_out, vd):
    t, d = x.shape
    dh = c_re.shape[1]
    nstate = s_re.shape[1]
    du, ds = dh // SSM_SPLIT, nstate // SSM_SPLIT
    tb = _blk(t, TB_MIX)

    def body(x_ref, u_ref, bg_ref, cg_ref, v_ref, cgh_ref, vh_ref, sr_ref, si_ref, cre_ref, cim_ref, p_ref,
             cw_ref, gw_ref, h16_ref, h64_ref, wo_ref, vd_ref, y1_ref, o_ref, x2_ref):
        i = pl.program_id(0)
        u = u_ref[...]
        ys = []
        for q in range(SSM_SPLIT):
            rq, cq = slice(q * ds, (q + 1) * ds), slice(q * du, (q + 1) * du)
            ys.append(_dot(sr_ref[:, rq], cre_ref[rq, cq]) - _dot(si_ref[:, rq], cim_ref[rq, cq]))
        ys = jnp.concatenate(ys, axis=1)
        y1 = ys + p_ref[0:1, :] * u
        y1_ref[...] = y1
        z = _gelu(y1)
        q = _dot(z, gw_ref[...]) + p_ref[1:2, :]
        ya = z * _sigmoid(q)
        na = ya * lax.rsqrt(_head_ms(ya, h16_ref) + EPS) * p_ref[2:3, :]
        cv = cg_ref[...] * v_ref[...]
        cvh = jnp.where(i > 0, cgh_ref[...] * vh_ref[...], 0.0)
        yb = bg_ref[...] * _conv3(cv, cvh, cw_ref)
        nb = yb * lax.rsqrt(_head_ms(yb, h64_ref) + EPS) * p_ref[3:4, :]
        o = _dot(na, wo_ref[0:dh, :]) + _dot(nb, wo_ref[dh:2 * dh, :])
        o_ref[...] = o
        on = o * lax.rsqrt(_rowmean(o * o) + EPS) * vd_ref[0:1, :]
        x2_ref[...] = x_ref[...] + vd_ref[1:2, :] * on

    return pl.pallas_call(
        body, name="mix_out", grid=(t // tb,),
        out_shape=(jax.ShapeDtypeStruct((t, dh), F32), jax.ShapeDtypeStruct((t, d), F32),
                   jax.ShapeDtypeStruct((t, d), F32)),
        in_specs=[_rows(tb, d), _rows(tb, dh, 0), _rows(tb, dh, 1), _rows(tb, dh, 2), _rows(tb, dh, 3),
                  _halo_prev(tb, dh, 2), _halo_prev(tb, dh, 3), _rows(tb, nstate), _rows(tb, nstate),
                  _full(c_re.shape), _full(c_im.shape), _full(v512.shape), _full(convw.shape), _full(glu_w.shape),
                  _full(h16.shape), _full(h64.shape), _full(w_out.shape), _full(vd.shape)],
        out_specs=(_rows(tb, dh), _rows(tb, d), _rows(tb, d)),
        compiler_params=_cparams(("parallel",), VMEM_BIG),
    )(x, proj, proj, proj, proj, proj, proj, s_re, s_im, c_re, c_im, v512, convw, glu_w, h16, h64, w_out, vd)


def _ffn_up(x2, vec, w_up_st):
    t, d = x2.shape
    ns, _, nc = w_up_st.shape
    tb = _blk(t, TB_FFN_UP)

    def body(x_ref, vec_ref, w_ref, up_ref, h2_ref):
        xv = x_ref[...]
        r = lax.rsqrt(_rowmean(xv * xv) + EPS)
        h = xv * r * vec_ref[0:1, :] * vec_ref[1:2, :] + vec_ref[2:3, :]
        hb = h.astype(BF16)
        h2_ref[...] = hb
        for j in range(ns):
            up_ref[:, j * nc:(j + 1) * nc] = jnp.dot(hb, w_ref[j], preferred_element_type=F32)

    return pl.pallas_call(
        body, name="ffn_up", grid=(t // tb,),
        out_shape=(jax.ShapeDtypeStruct((t, ns * nc), F32), jax.ShapeDtypeStruct((t, d), BF16)),
        in_specs=[_rows(tb, d), _full((SUBLANES, d)), _resident(w_up_st.shape)],
        out_specs=(_rows(tb, ns * nc), _rows(tb, d)),
        compiler_params=_cparams(("parallel",), VMEM_BIG),
    )(x2, vec, w_up_st)


def _ffn_down(up, fw, w_down, w_down_t, x2, tgt, vd):
    t, nh = up.shape
    dff, d = w_down.shape
    tb = _blk(t, TB_FFN)
    inv_d = 1.0 / d

    def body(up_ref, uph_ref, fw_ref, wd_ref, wdt_ref, x2_ref, tgt_ref, vd_ref,
             act_ref, ddn_ref, dout_ref, dhid_ref, vec_ref, loss_ref, a_s, vv_s, sg_s):
        i = pl.program_id(0)

        def conv_cols(sl):
            x = up_ref[:, sl]
            halo = jnp.where(i > 0, uph_ref[:, sl], 0.0)
            return (fw_ref[0:1, sl] * _shift_down(x, halo, 2) + fw_ref[1:2, sl] * _shift_down(x, halo, 1)
                    + fw_ref[2:3, sl] * x)

        dn = None
        for o in range(0, dff, CW_FFN):
            sl = slice(o, o + CW_FFN)
            a = conv_cols(sl)
            vv = conv_cols(slice(dff + o, dff + o + CW_FFN))
            sg = _sigmoid(a)
            si = a * sg
            a_s[:, sl] = si
            vv_s[:, sl] = vv
            sg_s[:, sl] = sg
            actb = (si * vv).astype(BF16)
            act_ref[:, sl] = actb
            pj = lax.dot_general(actb, wdt_ref[:, sl], (((1,), (1,)), ((), ())), preferred_element_type=F32)
            dn = pj if dn is None else dn + pj
        r3 = lax.rsqrt(_rowmean(dn * dn) + EPS)
        xn = dn * r3
        g = vd_ref[0:1, :]
        gt2 = vd_ref[1:2, :]
        dnn = xn * g
        diff = x2_ref[...] + gt2 * dnn - tgt_ref[...]
        part = 0.5 * inv_d * jnp.sum(diff * diff)

        @pl.when(i == 0)
        def _():
            loss_ref[...] = jnp.zeros(loss_ref.shape, F32)
        loss_ref[...] += part
        dout = diff * inv_d
        dout_ref[...] = dout
        ddnn = dout * gt2
        _acc_rows(vec_ref, i == 0, [_colsum(dout * dnn), _colsum(ddnn * xn)])
        dxn = ddnn * g
        ddn = r3 * (dxn - xn * _rowmean(dxn * xn))
        ddnb = ddn.astype(BF16)
        ddn_ref[...] = ddnb
        for o in range(0, dff, CW_FFN):
            sl = slice(o, o + CW_FFN)
            dact = lax.dot_general(ddnb, wd_ref[sl, :], (((1,), (1,)), ((), ())), preferred_element_type=F32)
            si, vv, sg = a_s[:, sl], vv_s[:, sl], sg_s[:, sl]
            dhid_ref[:, sl] = (dact * vv * (sg + si * (1.0 - sg))).astype(BF16)
            dhid_ref[:, dff + o:dff + o + CW_FFN] = (dact * si).astype(BF16)

    return pl.pallas_call(
        body, name="ffn_down", grid=(t // tb,),
        scratch_shapes=[pltpu.VMEM((tb, dff), F32)] * 3,
        out_shape=(jax.ShapeDtypeStruct((t, dff), BF16), jax.ShapeDtypeStruct((t, d), BF16),
                   jax.ShapeDtypeStruct((t, d), F32), jax.ShapeDtypeStruct((t, nh), BF16),
                   jax.ShapeDtypeStruct((SUBLANES, d), F32), jax.ShapeDtypeStruct((SUBLANES, 128), F32)),
        in_specs=[_rows(tb, nh), _halo_prev(tb, nh), _full(fw.shape), _resident(w_down.shape),
                  _resident(w_down_t.shape), _rows(tb, d),
                  _rows(tb, d), _full(vd.shape)],
        out_specs=(_rows(tb, dff), _rows(tb, d), _rows(tb, d), _rows(tb, nh), _full((SUBLANES, d)),
                   _full((SUBLANES, 128))),
        compiler_params=_cparams(("arbitrary",), VMEM_BIG),
    )(up, up, fw, w_down, w_down_t, x2, tgt, vd)


def _ffn_up_bwd(dhid, up, fw, x2, dout, vec, w_up_st):
    t, nh = dhid.shape
    d = x2.shape[1]
    ns, _, nc = w_up_st.shape
    tb = _blk(t, TB_FFN)
    nblk = t // tb
    cw = 128

    def body(dh_ref, dhn_ref, up_ref, fw_ref, x2_ref, dout_ref, vec_ref, w_ref,
             dx2_ref, dup_ref, vp_ref, df_ref):
        i = pl.program_id(0)

        @pl.when(i == 0)
        def _():
            df_ref[...] = jnp.zeros(df_ref.shape, F32)
        dh2 = None
        for j in range(ns):
            for o in range(j * nc, (j + 1) * nc, cw):
                sl = slice(o, o + cw)
                dh = dh_ref[:, sl].astype(F32)
                dhn = jnp.where(i < nblk - 1, dhn_ref[:, sl].astype(F32), 0.0)
                dh1 = _shift_up(dh, dhn, 1)
                dh2s = _shift_up(dh, dhn, 2)
                dup_ref[:, sl] = (fw_ref[2:3, sl] * dh + fw_ref[1:2, sl] * dh1 + fw_ref[0:1, sl] * dh2s).astype(BF16)
                up_v = up_ref[:, sl]
                df_ref[0:1, sl] += _colsum(dh2s * up_v)
                df_ref[1:2, sl] += _colsum(dh1 * up_v)
                df_ref[2:3, sl] += _colsum(dh * up_v)
            pj = lax.dot_general(dup_ref[:, j * nc:(j + 1) * nc], w_ref[j], (((1,), (1,)), ((), ())),
                                 preferred_element_type=F32)
            dh2 = pj if dh2 is None else dh2 + pj
        xv = x2_ref[...]
        r = lax.rsqrt(_rowmean(xv * xv) + EPS)
        xn = xv * r
        g = vec_ref[0:1, :]
        hg = xn * g
        dhg = dh2 * vec_ref[1:2, :]
        _acc_rows(vp_ref, i == 0, [_colsum(dh2), _colsum(dh2 * hg), _colsum(dhg * xn)])
        dxn = dhg * g
        dx2_ref[...] = dout_ref[...] + r * (dxn - xn * _rowmean(dxn * xn))

    return pl.pallas_call(
        body, name="ffn_up_bwd", grid=(nblk,),
        out_shape=(jax.ShapeDtypeStruct((t, d), F32), jax.ShapeDtypeStruct((t, nh), BF16),
                   jax.ShapeDtypeStruct((SUBLANES, d), F32), jax.ShapeDtypeStruct((SUBLANES, nh), F32)),
        in_specs=[_rows(tb, nh), _halo_next(tb, nh, t, rows=BF16_ROWS), _rows(tb, nh), _full(fw.shape),
                  _rows(tb, d), _rows(tb, d), _full(vec.shape), _resident(w_up_st.shape)],
        out_specs=(_rows(tb, d), _rows(tb, nh), _full((SUBLANES, d)), _full((SUBLANES, nh))),
        compiler_params=_cparams(("arbitrary",), VMEM_BIG),
    )(dhid, dhid, up, fw, x2, dout, vec, w_up_st)


def _mix_out_bwd(dx2, o, y1, proj, s_re, s_im, c_re, c_im, v512, convw, glu_w, h16, h64, w_out, vd):
    t, d = dx2.shape
    dh = y1.shape[1]
    nstate = c_re.shape[0]
    du, ds = dh // SSM_SPLIT, nstate // SSM_SPLIT
    tb = _blk(t, TB_MIX)

    def body(dx2_ref, o_ref, y1_ref, u_ref, bg_ref, cg_ref, v_ref, cgh_ref, vh_ref, cre_ref, cim_ref, p_ref,
             cw_ref, gw_ref, h16_ref, h64_ref, wo_ref, vd_ref, sr_ref, si_ref,
             do_ref, ycat_ref, z_ref, dq_ref, dy1_ref, gr_ref, gi_ref, dcc_ref, dbg_ref, vpd_ref, vp5_ref,
             dcr_ref, dci_ref):
        i = pl.program_id(0)
        first = i == 0

        @pl.when(first)
        def _():
            dcr_ref[...] = jnp.zeros(dcr_ref.shape, F32)
            dci_ref[...] = jnp.zeros(dci_ref.shape, F32)
        ov = o_ref[...]
        ro = lax.rsqrt(_rowmean(ov * ov) + EPS)
        on_ = ov * ro
        g = vd_ref[0:1, :]
        dx2v = dx2_ref[...]
        don = dx2v * vd_ref[1:2, :]
        _acc_rows(vpd_ref, first, [_colsum(dx2v * on_ * g), _colsum(don * on_)])
        dxn = don * g
        dob = (ro * (dxn - on_ * _rowmean(dxn * on_))).astype(BF16)
        do_ref[...] = dob
        dyc_a =lax.dot_general(dob, wo_ref[0:dh, :], (((1,), (1,)), ((), ())), preferred_element_type=F32)
        dyc_b = lax.dot_general(dob, wo_ref[dh:2 * dh, :], (((1,), (1,)), ((), ())), preferred_element_type=F32)
        y1v = y1_ref[...]
        u = u_ref[...]
        z, dz_dy1 = _gelu_and_grad(y1v)
        zb = z.astype(BF16)
        sg = _sigmoid(jnp.dot(zb, gw_ref[...], preferred_element_type=F32) + p_ref[1:2, :])
        ya = z * sg
        ra = lax.rsqrt(_head_ms(ya, h16_ref) + EPS)
        yan = ya * ra
        ga = p_ref[2:3, :]
        ycat_ref[:, 0:dh] = (yan * ga).astype(BF16)
        dyn = dyc_a * ga
        dya = ra * (dyn - yan * _split_dot(dyn * yan, h16_ref[...]))
        dq = dya * z * sg * (1.0 - sg)
        dqb = dq.astype(BF16)
        z_ref[...] = zb
        dq_ref[...] = dqb
        dz = dya * sg + lax.dot_general(dqb, gw_ref[...], (((1,), (1,)), ((), ())), preferred_element_type=F32)
        dy1 = dz * dz_dy1
        dy1_ref[...] = dy1
        dy1b = dy1.astype(BF16)
        for q in range(SSM_SPLIT):
            rq, cq = slice(q * ds, (q + 1) * ds), slice(q * du, (q + 1) * du)
            gr_ref[:, rq] = lax.dot_general(dy1b[:, cq], cre_ref[rq, cq], (((1,), (1,)), ((), ())),
                                            preferred_element_type=F32)
            gi_ref[:, rq] = -lax.dot_general(dy1b[:, cq], cim_ref[rq, cq], (((1,), (1,)), ((), ())),
                                             preferred_element_type=F32)
            dcr_ref[rq, :] += _dot_tn(sr_ref[:, rq], dy1b[:, cq])
            dci_ref[rq, :] += _dot_tn(si_ref[:, rq], dy1b[:, cq])
        bg = bg_ref[...]
        cv = cg_ref[...] * v_ref[...]
        cvh = jnp.where(i > 0, cgh_ref[...] * vh_ref[...], 0.0)
        cv1 = _shift_down(cv, cvh, 1)
        cv2 = _shift_down(cv, cvh, 2)
        cc = cw_ref[0:1, :] * cv2 + cw_ref[1:2, :] * cv1 + cw_ref[2:3, :] * cv
        yb = bg * cc
        rb = lax.rsqrt(_head_ms(yb, h64_ref) + EPS)
        ybn = yb * rb
        gb = p_ref[3:4, :]
        ycat_ref[:, dh:2 * dh] = (ybn * gb).astype(BF16)
        dynb = dyc_b * gb
        dyb = rb * (dynb - ybn * _split_dot(dynb * ybn, h64_ref[...]))
        dcc = dyb * bg
        dbg_ref[...] = dyb * cc
        dcc_ref[...] = dcc
        _acc_rows(vp5_ref, first, [_colsum(dyc_a * yan), _colsum(dyc_b * ybn), _colsum(dq), _colsum(dy1 * u),
                                   _colsum(dcc * cv2), _colsum(dcc * cv1), _colsum(dcc * cv)])

    return pl.pallas_call(
        body, name="mix_out_bwd", grid=(t // tb,),
        out_shape=(jax.ShapeDtypeStruct((t, d), BF16), jax.ShapeDtypeStruct((t, 2 * dh), BF16),
                   jax.ShapeDtypeStruct((t, dh), BF16), jax.ShapeDtypeStruct((t, dh), BF16),
                   jax.ShapeDtypeStruct((t, dh), F32), jax.ShapeDtypeStruct((t, nstate), F32),
                   jax.ShapeDtypeStruct((t, nstate), F32), jax.ShapeDtypeStruct((t, dh), F32),
                   jax.ShapeDtypeStruct((t, dh), F32), jax.ShapeDtypeStruct((SUBLANES, d), F32),
                   jax.ShapeDtypeStruct((SUBLANES, dh), F32), jax.ShapeDtypeStruct((nstate, du), F32),
                   jax.ShapeDtypeStruct((nstate, du), F32)),
        in_specs=[_rows(tb, d), _rows(tb, d), _rows(tb, dh), _rows(tb, dh, 0), _rows(tb, dh, 1), _rows(tb, dh, 2),
                  _rows(tb, dh, 3), _halo_prev(tb, dh, 2), _halo_prev(tb, dh, 3), _resident(c_re.shape),
                  _resident(c_im.shape), _full(v512.shape), _full(convw.shape), _resident(glu_w.shape),
                  _resident(h16.shape), _resident(h64.shape), _resident(w_out.shape), _full(vd.shape),
                  _rows(tb, nstate), _rows(tb, nstate)],
        out_specs=(_rows(tb, d), _rows(tb, 2 * dh), _rows(tb, dh), _rows(tb, dh), _rows(tb, dh), _rows(tb, nstate),
                   _rows(tb, nstate), _rows(tb, dh), _rows(tb, dh), _full((SUBLANES, d)), _full((SUBLANES, dh)),
                   _full((nstate, du)), _full((nstate, du))),
        compiler_params=_cparams(("arbitrary",), VMEM_BIG),
    )(dx2, o, y1, proj, proj, proj, proj, proj, proj, c_re, c_im, v512, convw, glu_w, h16, h64, w_out, vd,
      s_re, s_im)


def _mix_in_bwd(gt_re, gt_im, b_re, b_im, dy1, dcc, dbg, proj, x, dx2, vec, v512, convw, w_in_st):
    t, d = x.shape
    dh = dy1.shape[1]
    nstate = gt_re.shape[1]
    du_w, ds = dh // SSM_SPLIT, nstate // SSM_SPLIT
    ns, _, nc = w_in_st.shape
    tb = _blk(t, TB_MIX)
    nblk = t // tb

    def body(gr_ref, gi_ref, bre_ref, bim_ref, dy1_ref, dcc_ref, dccn_ref, dbg_ref, u_ref, cg_ref, v_ref, x_ref,
             dx2_ref, vec_ref, p_ref, cw_ref, w_ref, gx_ref, dproj_ref, vp_ref, dbr_ref, dbi_ref):
        i = pl.program_id(0)

        @pl.when(i == 0)
        def _():
            dbr_ref[...] = jnp.zeros(dbr_ref.shape, F32)
            dbi_ref[...] = jnp.zeros(dbi_ref.shape, F32)
        ub = u_ref[...].astype(BF16)
        du = []
        for q in range(SSM_SPLIT):
            rq, cq = slice(q * du_w, (q + 1) * du_w), slice(q * ds, (q + 1) * ds)
            du.append(lax.dot_general(gr_ref[:, cq].astype(BF16), bre_ref[rq, cq], (((1,), (1,)), ((), ())),
                                      preferred_element_type=F32)
                      + lax.dot_general(gi_ref[:, cq].astype(BF16), bim_ref[rq, cq], (((1,), (1,)), ((), ())),
                                        preferred_element_type=F32))
            dbr_ref[rq, :] += _dot_tn(ub[:, rq], gr_ref[:, cq])
            dbi_ref[rq, :] += _dot_tn(ub[:, rq], gi_ref[:, cq])
        du = dy1_ref[...] * p_ref[0:1, :] + jnp.concatenate(du, axis=1)
        dcc = dcc_ref[...]
        dccn = jnp.where(i < nblk - 1, dccn_ref[...], 0.0)
        dcv = (cw_ref[2:3, :] * dcc + cw_ref[1:2, :] * _shift_up(dcc, dccn, 1)
               + cw_ref[0:1, :] * _shift_up(dcc, dccn, 2))
        parts = [du, dbg_ref[...], dcv * v_ref[...], dcv * cg_ref[...]]
        xv = x_ref[...]
        r = lax.rsqrt(_rowmean(xv * xv) + EPS)
        xn = xv * r
        g = vec_ref[0:1, :]
        hg = xn * g
        dh1 = None
        for j in range(ns):
            pb = parts[j].astype(BF16)
            dproj_ref[:, j * nc:(j + 1) * nc] = pb
            pj =lax.dot_general(pb, w_ref[j], (((1,), (1,)), ((), ())), preferred_element_type=F32)
            dh1 = pj if dh1 is None else dh1 + pj
        dhg = dh1 * vec_ref[1:2, :]
        _acc_rows(vp_ref, i == 0, [_colsum(dh1), _colsum(dh1 * hg), _colsum(dhg * xn)])
        dxn = dhg * g
        gx_ref[...] = dx2_ref[...] + r * (dxn - xn * _rowmean(dxn * xn))

    assert nc == dh and ns == 4
    return pl.pallas_call(
        body, name="mix_in_bwd", grid=(nblk,),
        out_shape=(jax.ShapeDtypeStruct((t, d), F32), jax.ShapeDtypeStruct((t, ns * nc), BF16),
                   jax.ShapeDtypeStruct((SUBLANES, d), F32), jax.ShapeDtypeStruct((dh, ds), F32),
                   jax.ShapeDtypeStruct((dh, ds), F32)),
        in_specs=[_rows(tb, nstate), _rows(tb, nstate), _resident(b_re.shape), _resident(b_im.shape), _rows(tb, dh),
                  _rows(tb, dh), _halo_next(tb, dh, t), _rows(tb, dh), _rows(tb, dh, 0), _rows(tb, dh, 2),
                  _rows(tb, dh, 3), _rows(tb, d), _rows(tb, d), _full(vec.shape), _full(v512.shape),
                  _full(convw.shape), _resident(w_in_st.shape)],
        out_specs=(_rows(tb, d), _rows(tb, ns * nc), _full((SUBLANES, d)), _full((dh, ds)), _full((dh, ds))),
        compiler_params=_cparams(("arbitrary",), VMEM_BIG),
    )(gt_re, gt_im, b_re, b_im, dy1, dcc, dcc, dbg, proj, proj, proj, x, dx2, vec, v512, convw, w_in_st)


def _matmul_tn(a, b, m, bn, out_dtype, name, diag=False, bt=TB_TN, after=None):
    t = a.shape[0]
    n = b.shape[1]
    bt = _blk(t, bt)
    nk = t // bt
    extra = [] if after is None else [after]
    a_map = (lambda j, k: (k, j)) if diag else (lambda j, k: (k, 0))

    def body(a_ref, b_ref, *rest):
        o_ref, acc_ref = rest[-2:]
        k = pl.program_id(1)

        @pl.when(k == 0)
        def _():
            acc_ref[...] = jnp.zeros(acc_ref.shape, F32)
        acc_ref[...] += _dot_tn(a_ref[...], b_ref[...])

        @pl.when(k == nk - 1)
        def _():
            o_ref[...] = acc_ref[...].astype(out_dtype)

    return pl.pallas_call(
        body, name=name, grid=(n // bn, nk),
        out_shape=jax.ShapeDtypeStruct((n // bn, m, bn), out_dtype),
        in_specs=[pl.BlockSpec((bt, m), a_map), pl.BlockSpec((bt, bn), lambda j, k: (k, j))]
        + [pl.BlockSpec(memory_space=pl.ANY)] * len(extra),
        out_specs=pl.BlockSpec((None, m, bn), lambda j, k: (j, 0, 0)),
        scratch_shapes=[pltpu.VMEM((m, bn), F32)],
        compiler_params=_cparams(("parallel", "arbitrary"), VMEM_BIG),
    )(a, b, *extra)


def _ssm_bgrad(d_bre, d_bim, bt_re, bt_im, rows_in, fold, tile_b):
    gh, cb = d_bre.shape
    nb = SSM_SPLIT
    rb = gh // nb
    gp = nb * cb
    p = fold.shape[1]

    def body(dr_ref, di_ref, br_ref, bi_ref, rin_ref, f_ref, tb_ref, dbr_ref, dbi_ref, rout_ref):
        row = lax.broadcasted_iota(jnp.int32, (rb, cb), 0)
        col = lax.broadcasted_iota(jnp.int32, (rb, cb), 1)
        mask = (row >> 4) == (col >> 6)
        gr = jnp.where(mask, dr_ref[...], 0.0)
        gi = jnp.where(mask, di_ref[...], 0.0)
        cr, ci = rin_ref[0:1, :], rin_ref[1:2, :]
        dbr_ref[...] = _split3_dot(cr * gr + ci * gi, f_ref[...])
        dbi_ref[...] = _split3_dot(cr * gi - ci * gr, f_ref[...])
        br = _split3_dot(br_ref[...], tb_ref[...])
        bi = _split3_dot(bi_ref[...], tb_ref[...])
        rout_ref[...] = jnp.zeros(rout_ref.shape, F32)
        rout_ref[0:1, :] = _colsum(br * gr + bi * gi)
        rout_ref[1:2, :] = _colsum(br * gi - bi * gr)

    dspec = pl.BlockSpec((rb, cb), lambda j: (j, 0))
    rspec = pl.BlockSpec((SUBLANES, cb), lambda j: (0, j))
    ospec = pl.BlockSpec((rb, p), lambda j: (j, 0))
    return pl.pallas_call(
        body, name="ssm_bgrad", grid=(nb,),
        out_shape=(jax.ShapeDtypeStruct((gh, p), F32), jax.ShapeDtypeStruct((gh, p), F32),
                   jax.ShapeDtypeStruct((SUBLANES, gp), F32)),
        in_specs=[dspec, dspec, ospec, ospec, rspec, _full(fold.shape), _full(tile_b.shape)],
        out_specs=(ospec, ospec, rspec),
        compiler_params=_cparams(("parallel",)),
    )(d_bre, d_bim, bt_re, bt_im, rows_in, fold, tile_b)


def _ssm_cgrad(d_cre, d_cim, fold):
    gp, cb = d_cre.shape
    nb = SSM_SPLIT
    rb = gp // nb
    h = fold.shape[1]

    def body(dr_ref, di_ref, f_ref, cr_ref, ci_ref):
        row = lax.broadcasted_iota(jnp.int32, (rb, cb), 0)
        col = lax.broadcasted_iota(jnp.int32, (rb, cb), 1)
        mask = (row >> 6) == (col >> 4)
        cr_ref[...] = _split3_dot(jnp.where(mask, dr_ref[...], 0.0), f_ref[...])
        ci_ref[...] = -_split3_dot(jnp.where(mask, di_ref[...], 0.0), f_ref[...])

    cspec = pl.BlockSpec((rb, cb), lambda j: (j, 0))
    ospec = pl.BlockSpec((rb, h), lambda j: (j, 0))
    return pl.pallas_call(
        body, name="ssm_cgrad", grid=(nb,),
        out_shape=(jax.ShapeDtypeStruct((gp, h), F32),) * 2,
        in_specs=[cspec, cspec, _full(fold.shape)], out_specs=(ospec, ospec),
        compiler_params=_cparams(("parallel",)),
    )(d_cre, d_cim, fold)


def _ssm_lamgrad(lam_re, lam_im, log_step, abar_re, abar_im, coef_re, coef_im, gc_re, gc_im, ga_re, ga_im):
    g, p = lam_re.shape

    def body(lr_ref, li_ref, ls_ref, ar_ref, ai_ref, cr_ref, ci_ref, gcr_ref, gci_ref, gar_ref, gai_ref,
             dlr_ref, dli_ref, dls_ref):
        lam_raw = lr_ref[...]
        lr = jnp.minimum(lam_raw, LAMBDA_RE_MAX)
        li = li_ref[...]
        st = jnp.exp(ls_ref[...])
        den = lr * lr + li * li
        gcr, gci = gcr_ref[...], gci_ref[...]
        gab_r = gar_ref[...] + (lr * gcr - li * gci) / den
        gab_i = gai_ref[...] + (lr * gci + li * gcr) / den
        cr, ci = cr_ref[...], ci_ref[...]
        wr = -(cr * lr + ci * li) / den
        wi = -(ci * lr - cr * li) / den
        gl_r = wr * gcr + wi * gci
        gl_i = wr * gci - wi * gcr
        ar, ai = ar_ref[...], ai_ref[...]
        gw_r = ar * gab_r + ai * gab_i
        gw_i = ar * gab_i - ai * gab_r
        gl_r = gl_r + st * gw_r
        gl_i = gl_i + st * gw_i
        pass_through = jnp.where(lam_raw < LAMBDA_RE_MAX, 1.0, jnp.where(lam_raw == LAMBDA_RE_MAX, 0.5, 0.0))
        dlr_ref[...] = gl_r * pass_through
        dli_ref[...] = gl_i
        dls_ref[...] = st * jnp.sum(lr * gw_r + li * gw_i, axis=1, keepdims=True)

    sds = jax.ShapeDtypeStruct((g, p), F32)
    return pl.pallas_call(body, name="ssm_lamgrad", out_shape=(sds, sds, jax.ShapeDtypeStruct((g, 1), F32)))(
        lam_re, lam_im, log_step, abar_re, abar_im, coef_re, coef_im, gc_re, gc_im, ga_re, ga_im)


def _row_block(r, most=256):
    for rb in range(min(r, most), BF16_ROWS - 1, -1):
        if r % rb == 0 and rb % BF16_ROWS == 0:
            return rb
    return r


def _adamw_math(w, g, m, v):
    m = ADAM_B1 * m + (1.0 - ADAM_B1) * g
    v = ADAM_B2 * v + (1.0 - ADAM_B2) * (g * g)
    m_hat = m / (1.0 - ADAM_B1 ** ADAM_STEP)
    v_hat = v / (1.0 - ADAM_B2 ** ADAM_STEP)
    delta = -ADAM_LR * (m_hat / (jnp.sqrt(v_hat) + ADAM_EPS) + ADAM_WD * w)
    return delta, m, v


def _adamw_big(p_mine, p_sib, w, m, v, name):
    r, c = w.shape
    rb = _row_block(r)

    def body(a_ref, b_ref, w_ref, m_ref, v_ref, g_ref, d_ref, mo_ref, vo_ref):
        g = a_ref[...].astype(F32) + b_ref[...].astype(F32)
        g_ref[...] = g
        d_ref[...], mo_ref[...], vo_ref[...] = _adamw_math(w_ref[...], g, m_ref[...], v_ref[...])

    spec = pl.BlockSpec((rb, c), lambda i: (i, 0))
    sds = jax.ShapeDtypeStruct((r, c), F32)
    return pl.pallas_call(
        body, name=name, grid=(r // rb,), out_shape=(sds,) * 4, in_specs=[spec] * 5, out_specs=(spec,) * 4,
        compiler_params=_cparams(("parallel",), VMEM_KEEP_OPERANDS_IN_HBM),
    )(p_mine, p_sib, w, m, v)


def _sum_blocks(stack, name):
    n, r, c = stack.shape
    rb = _row_block(r)

    def body(s_ref, o_ref):
        acc = s_ref[0].astype(F32)
        for k in range(1, n):
            acc = acc + s_ref[k].astype(F32)
        o_ref[...] = acc

    return pl.pallas_call(
        body, name=name, grid=(r // rb,), out_shape=jax.ShapeDtypeStruct((r, c), F32),
        in_specs=[pl.BlockSpec((n, rb, c), lambda i: (0, i, 0))], out_specs=pl.BlockSpec((rb, c), lambda i: (i, 0)),
        compiler_params=_cparams(("parallel",), VMEM_KEEP_OPERANDS_IN_HBM),
    )(stack)


def _sum_landed(landed, own, chip, name):
    n, r, c = landed.shape
    rb = _row_block(r)

    def body(chip_ref, own_ref, l1_ref, l2_ref, l3_ref, o_ref):
        acc = own_ref[0].astype(F32)
        for ref in (l1_ref, l2_ref, l3_ref):
            acc = acc + ref[0].astype(F32)
        o_ref[...] = acc.astype(BF16)

    def slot(k):
        return pl.BlockSpec((1, rb, c), lambda i, ch: ((ch[0] + k) % n, i, 0))

    return pl.pallas_call(
        body, name=name, out_shape=jax.ShapeDtypeStruct((r, c), BF16),
        grid_spec=pltpu.PrefetchScalarGridSpec(
            num_scalar_prefetch=1, grid=(r // rb,), in_specs=[slot(0), slot(1), slot(2), slot(3)],
            out_specs=pl.BlockSpec((rb, c), lambda i, ch: (i, 0))),
        compiler_params=_cparams(("parallel",), VMEM_KEEP_OPERANDS_IN_HBM),
    )(jnp.reshape(chip, (1,)).astype(jnp.int32), own, landed, landed, landed)


def _add2(a, b):
    def body(a_ref, b_ref, o_ref):
        o_ref[...] = a_ref[...] + b_ref[...]

    return pl.pallas_call(body, name="add_small", out_shape=jax.ShapeDtypeStruct(a.shape, F32))(a, b)


def _adamw_ada(c_all, dmod_cols, w, m, v):
    d, n = w.shape
    bn = 512

    def body(c_ref, dm_ref, w_ref, m_ref, v_ref, g_ref, d_ref, mo_ref, vo_ref):
        cc = c_ref[...]
        g = _dot_tn(cc * _sigmoid(cc), dm_ref[...])
        g_ref[...] = g
        d_ref[...], mo_ref[...], vo_ref[...] = _adamw_math(w_ref[...], g, m_ref[...], v_ref[...])

    spec = pl.BlockSpec((d, bn), lambda j: (0, j))
    sds = jax.ShapeDtypeStruct((d, n), F32)
    return pl.pallas_call(
        body, name="adamw_ada", grid=(n // bn,), out_shape=(sds,) * 4,
        in_specs=[_full((N_DEV, d)), pl.BlockSpec((N_DEV, bn), lambda j: (0, j)), spec, spec, spec],
        out_specs=(spec,) * 4, compiler_params=_cparams(("parallel",), VMEM_KEEP_OPERANDS_IN_HBM),
    )(c_all, dmod_cols, w, m, v)


def _adamw_small(items):
    n = len(items)

    def body(*refs):
        ins, outs = refs[:4 * n], refs[4 * n:]
        for k in range(n):
            w_ref, g_ref, m_ref, v_ref = ins[4 * k:4 * k + 4]
            outs[3 * k][...], outs[3 * k + 1][...], outs[3 * k + 2][...] = _adamw_math(
                w_ref[...], g_ref[...], m_ref[...], v_ref[...])

    flat = [a for it in items for a in it]
    out_shape = tuple(jax.ShapeDtypeStruct(it[0].shape, F32) for it in items for _ in range(3))
    res = pl.pallas_call(body, name="adamw_small", out_shape=out_shape,
                         compiler_params=_cparams(vmem=VMEM_KEEP_OPERANDS_IN_HBM))(*flat)
    return [tuple(res[3 * k:3 * k + 3]) for k in range(n)]


def _group_mean_matrix(n, group):
    idx = np.arange(n) // group
    return (idx[:, None] == idx[None, :]).astype(np.float32) / group


def _fold_matrix(n, period):
    return (np.arange(n)[:, None] % period == np.arange(period)[None, :]).astype(np.float32)


def _rows8(*rows):
    c = rows[0].shape[-1]
    pad = jnp.zeros((SUBLANES - len(rows), c), F32)
    return jnp.concatenate([r.reshape(1, c) for r in rows] + [pad], axis=0)


def _to_rows(a, width):
    flat = a.reshape(-1)
    n = -(-flat.shape[0] // width)
    flat = jnp.pad(flat, (0, n * width - flat.shape[0]))
    return flat.reshape(n, width)


def kernel(x, c, w_ada, b_ada, g_pre_mix, g_post_mix, w_in, ssm_lam_re, ssm_lam_im, ssm_log_step, ssm_b_re, ssm_b_im, ssm_c_re, ssm_c_im, ssm_d, glu_w, glu_b, g_out_ssm, conv_w, g_out_conv, w_out, g_pre_ffn, g_post_ffn, w_up, ffn_conv_w, w_down, loss_target, m_w_ada, m_b_ada, m_g_pre_mix, m_g_post_mix, m_w_in, m_ssm_lam_re, m_ssm_lam_im, m_ssm_log_step, m_ssm_b_re, m_ssm_b_im, m_ssm_c_re, m_ssm_c_im, m_ssm_d, m_glu_w, m_glu_b, m_g_out_ssm, m_conv_w, m_g_out_conv, m_w_out, m_g_pre_ffn, m_g_post_ffn, m_w_up, m_ffn_conv_w, m_w_down, v_w_ada, v_b_ada, v_g_pre_mix, v_g_post_mix, v_w_in, v_ssm_lam_re, v_ssm_lam_im, v_ssm_log_step, v_ssm_b_re, v_ssm_b_im, v_ssm_c_re, v_ssm_c_im, v_ssm_d, v_glu_w, v_glu_b, v_g_out_ssm, v_conv_w, v_g_out_conv, v_w_out, v_g_pre_ffn, v_g_post_ffn, v_w_up, v_ffn_conv_w, v_w_down):
    xs = x[0]
    tgt = loss_target[0]
    t, d = xs.shape
    xi, yi, ci = lax.axis_index("x"), lax.axis_index("y"), lax.axis_index("c")
    chip = 2 * xi + yi
    dev = 2 * chip + ci

    n_groups, n_state = ssm_lam_re.shape[1:]
    n_gch = ssm_b_re.shape[3]
    d_ssm = n_groups * n_gch
    gp = n_groups * n_state
    n_ada = w_ada.shape[2]
    d_ff = w_down.shape[1] * N_CHIPS
    n_upc = w_up.shape[2]

    w_names = ("w_in", "glu_w", "w_out", "w_up", "w_down")
    c_gath = _allgather8(jnp.broadcast_to(c, (SUBLANES, d)), "gather_c")
    c_all = c_gath.reshape(N_DEV, SUBLANES, d)[:, 0, :]

    def pad8(a):
        return jnp.concatenate([a, jnp.zeros((SUBLANES - a.shape[0], a.shape[1]), a.dtype)], axis=0)

    def start(name, arrs, after):
        return _chips_start(name, True, [], [_landing(a, chip) for a in arrs], after)

    w_names = ("w_in", "mod", "conv_w", "ffn_conv_w", "glu_w", "w_out", "w_up", "w_down")
    first = start("weights_start_in", [w_in[0].astype(BF16)], c_gath)
    b_sh = lax.dynamic_slice(b_ada, (0, chip * n_ada), (1, n_ada))
    mod_sh = _mod_shard(c_all + first[4][0:1, 0:1], w_ada[0], b_sh)
    second = start("weights_start_rest", [mod_sh, pad8(conv_w[0]), pad8(ffn_conv_w[0])]
                   + [w[0].astype(BF16) for w in (glu_w, w_out, w_up, w_down)], None)
    w_send, w_recv, w_land = [list(first[k]) + list(second[k]) for k in (0, 1, 3)]
    w_token = second[4]

    def weights(names, after):
        ks = [w_names.index(nm) for nm in names]
        return _chips_wait("weights_wait_" + names[-1], True, [w_send[k] for k in ks], [w_recv[k] for k in ks],
                           [], [w_land[k] for k in ks], after)[1]

    lam_re, lam_im = ssm_lam_re[0], ssm_lam_im[0]
    log_step = ssm_log_step[0].reshape(n_groups, 1) + w_token[0:1, 0:1]
    abar_re, abar_im, coef_re, coef_im = _ssm_prep(lam_re, lam_im, log_step)
    a_rows = _rows8(abar_re.reshape(1, gp), abar_im.reshape(1, gp))
    coef_rows = _rows8(coef_re.reshape(1, gp), coef_im.reshape(1, gp))
    bt_re = ssm_b_re[0].transpose(0, 2, 1).reshape(d_ssm, n_state)
    bt_im = ssm_b_im[0].transpose(0, 2, 1).reshape(d_ssm, n_state)
    ct_re = ssm_c_re[0].transpose(0, 2, 1).reshape(gp, n_gch)
    ct_im = ssm_c_im[0].transpose(0, 2, 1).reshape(gp, n_gch)
    tile_b = jnp.asarray(np.tile(np.eye(n_state), (1, n_groups // SSM_SPLIT)), BF16)
    tile_c = jnp.asarray(np.tile(np.eye(n_gch), (1, n_groups)), BF16)
    bblk_re, bblk_im, cblk_re, cblk_im = _ssm_blocks(bt_re, bt_im, ct_re, ct_im, coef_rows, tile_b, tile_c)

    h16 = jnp.asarray(_group_mean_matrix(d_ssm, n_gch), BF16)
    h64 = jnp.asarray(_group_mean_matrix(d_ssm, CONV_HEAD_DIM), BF16)

    g_mod, g_cw, g_fw, w_in_st = weights(("mod", "conv_w", "ffn_conv_w", "w_in"), bblk_re)
    mod_all = g_mod.transpose(1, 0, 2).reshape(N_DEV, N_CHIPS * n_ada)
    mod = lax.dynamic_slice(mod_all, (dev, 0), (1, N_CHIPS * n_ada))
    sh1, sc1, gt1, sh2, sc2, gt2 = [mod[:, k * d:(k + 1) * d] for k in range(6)]
    convw_full = pad8(g_cw[:, :3, :].transpose(1, 0, 2).reshape(3, d_ssm))
    fw_full = pad8(g_fw[:, :3, :].transpose(1, 0, 2).reshape(3, N_CHIPS * n_upc))

    v512 = _rows8(ssm_d, glu_b, g_out_ssm, g_out_conv)
    vec1 =_rows8(g_pre_mix, 1.0 + sc1, sh1)
    vd1 = _rows8(g_post_mix, gt1)
    vec2 = _rows8(g_pre_ffn, 1.0 + sc2, sh2)
    vd2 = _rows8(g_post_ffn, gt2)

    proj, bu_re, bu_im, h1b = _mix_in(xs, vec1, w_in_st, bblk_re, bblk_im)
    s_re, s_im = _scan_fwd(a_rows, bu_re, bu_im)
    g_glu, g_wout = weights(("glu_w", "w_out"), s_re)
    glu_full = g_glu.reshape(d_ssm, d_ssm)
    w_out_full = g_wout.reshape(2 * d_ssm, d)
    y1, o_mix, x2 = _mix_out(xs, proj, s_re, s_im, cblk_re, cblk_im, v512, convw_full, glu_full, h16, h64,
                             w_out_full, vd1)
    (w_up_st,) = weights(("w_up",), x2)
    up, h2b = _ffn_up(x2, vec2, w_up_st)
    (g_wdown,) = weights(("w_down",), up)
    w_down_full = g_wdown.reshape(d_ff, d)
    actb, ddnb, dout, dhid, vp_dn, loss_blk = _ffn_down(up, fw_full, w_down_full, w_down_full.T, x2, tgt, vd2)

    gw_down = _matmul_tn(actb, ddnb, d_ff, d, BF16, "dw_down", bt=1024).reshape(N_CHIPS, d_ff // N_CHIPS, d)
    dx2, dupb, vp_up, df_rows = _ffn_up_bwd(dhid, up, fw_full, x2, dout, vec2, w_up_st)
    gw_up = _matmul_tn(h2b, dupb, d, n_upc, BF16, "dw_up", bt=2048)
    ga_send, ga_recv, ga_src, ga_land, ga_token = _chips_start(
        "grads_start_ffn", False, [gw_down, gw_up], [lax.empty(g.shape, g.dtype) for g in (gw_down, gw_up)])
    (dob, ycatb, zb, dqb, dy1, g_re, g_im, dcc, dbg, vp_mo, vp5, d_cre, d_cim) = _mix_out_bwd(
        dx2, o_mix, y1, proj, s_re, s_im, cblk_re, cblk_im, v512, convw_full, glu_full, h16, h64, w_out_full,
        vd1 + ga_token[0:1, 0:1])
    gw_out = _matmul_tn(ycatb, dob, 2 * d_ssm, d, BF16, "dw_out", bt=2048)
    gw_out = gw_out.reshape(N_CHIPS, 2 * d_ssm // N_CHIPS, d)
    gw_glu = _matmul_tn(zb, dqb, d_ssm, d_ssm, BF16, "dw_glu", bt=2048).reshape(N_CHIPS, d_ssm // N_CHIPS, d_ssm)
    gb_send, gb_recv, gb_src, gb_land, gb_token = _chips_start(
        "grads_start_mix", False, [gw_out, gw_glu], [lax.empty(g.shape, g.dtype) for g in (gw_out, gw_glu)])
    gt_re, gt_im, ga_re8, ga_im8 = _scan_bwd(a_rows + gb_token[0:1, 0:1], g_re, g_im, s_re, s_im)
    grad_x, dprojb, vp_mi, d_bre, d_bim = _mix_in_bwd(gt_re, gt_im, bblk_re, bblk_im, dy1, dcc, dbg, proj, xs, dx2,
                                                      vec1, v512, convw_full, w_in_st)
    ssm_u, ssm_s = d_ssm // SSM_SPLIT, gp // SSM_SPLIT

    fold_b = jnp.asarray(_fold_matrix(ssm_s, n_state), BF16)
    fold_c = jnp.asarray(_fold_matrix(ssm_u, n_gch), BF16)
    db_re_f, db_im_f, gc_rows = _ssm_bgrad(d_bre, d_bim, bt_re, bt_im, coef_rows, fold_b, tile_b)
    dc_re_f, dc_im_f = _ssm_cgrad(d_cre, d_cim, fold_c)
    ga_sum = _ga_rowsum(ga_re8, ga_im8)
    g_lam_re, g_lam_im, g_log_step = _ssm_lamgrad(
        lam_re, lam_im, log_step, abar_re, abar_im, coef_re, coef_im,
        gc_rows[0].reshape(n_groups, n_state), gc_rows[1].reshape(n_groups, n_state),
        ga_sum[0].reshape(n_groups, n_state), ga_sum[1].reshape(n_groups, n_state))
    g_b_re = db_re_f.reshape(n_groups, n_gch, n_state).transpose(0, 2, 1)
    g_b_im = db_im_f.reshape(n_groups, n_gch, n_state).transpose(0, 2, 1)
    g_c_re = dc_re_f.reshape(n_groups, n_state, n_gch).transpose(0, 2, 1)
    g_c_im = dc_im_f.reshape(n_groups, n_state, n_gch).transpose(0, 2, 1)

    dmod = jnp.concatenate([vp_mi[0:1], vp_mi[1:2], vp_mo[0:1], vp_up[0:1], vp_up[1:2], vp_dn[0:1]], axis=1)
    small = [
        ("g_pre_mix", vp_mi[2:3]), ("g_post_mix", vp_mo[1:2]), ("g_pre_ffn", vp_up[2:3]), ("g_post_ffn", vp_dn[1:2]),
        ("ssm_lam_re", g_lam_re), ("ssm_lam_im", g_lam_im), ("ssm_log_step", g_log_step),
        ("ssm_b_re", g_b_re), ("ssm_b_im", g_b_im), ("ssm_c_re", g_c_re), ("ssm_c_im", g_c_im),
        ("ssm_d", vp5[3:4]), ("glu_b", vp5[2:3]), ("g_out_ssm", vp5[0:1]), ("g_out_conv", vp5[1:2]),
        ("conv_w", vp5[4:7]), ("ffn_conv_w", df_rows[0:3]), ("loss", loss_blk[0:1, 0:1]),
    ]
    packed, offsets, row = [], {}, 0
    for name, a in small:
        r = _to_rows(a, d)
        offsets[name] = (row, a.shape)
        packed.append(r)
        row += r.shape[0]
    n_small = -(-row // SUBLANES) * SUBLANES
    packed.append(jnp.zeros((n_small - row, d), F32))
    packed.append(pad8(dmod.reshape(6, d)))
    pack = jnp.concatenate(packed, axis=0)
    sm_send, sm_recv, _, sm_land, sm_token = _chips_start("small_start", True, [], [_landing(pack, chip)])

    gw_in = _matmul_tn(h1b, dprojb, d, w_in.shape[2], BF16, "dw_in", bt=2048, after=sm_token)
    gc_send, gc_recv, gc_src, gc_land, gc_token = _chips_start(
        "grads_start_in", False, [gw_in], [lax.empty(gw_in.shape, gw_in.dtype)])

    def partials(names, own, landed):
        return [_sum_landed(l, o, chip, "sum_" + nm) for l, o, nm in zip(landed, own, names)]

    def update(names, mine, theirs):
        done = {}
        for nm, pm, ps in zip(names, mine, theirs):
            w_, m_, v_ = big_params[nm]
            done[nm] = _adamw_big(pm, ps, w_[0], m_[0], v_[0], "adamw_" + nm)
        return done

    big_params = {"w_down": (w_down, m_w_down, v_w_down), "w_up": (w_up, m_w_up, v_w_up),
                  "w_out": (w_out, m_w_out, v_w_out), "glu_w": (glu_w, m_glu_w, v_glu_w),
                  "w_in": (w_in, m_w_in, v_w_in)}
    ffn_names, mix_names = ("w_down", "w_up"), ("w_out", "glu_w", "w_in")
    p_ffn = partials(ffn_names, *_chips_wait("grads_wait_ffn", False, ga_send, ga_recv, ga_src, ga_land, gc_token))
    sa_send, sa_recv, sa_src, sa_land, sa_token = _sibling_start("swap_start_ffn", p_ffn)

    (sm_landed,) = _chips_wait("small_wait", True, sm_send, sm_recv, [], sm_land, sa_token)[1]
    sm_part = _sum_blocks(sm_landed, "sum_small")
    dmod_mine = sm_landed[:, n_small:n_small + SUBLANES, :]
    ss_send, ss_recv, ss_src, ss_land, ss_token = _sibling_start("swap_start_small", [sm_part, dmod_mine])
    p_ffn, t_ffn = _sibling_wait("swap_wait_ffn", sa_send, sa_recv, sa_src, sa_land, ss_token)
    big = update(ffn_names, p_ffn, t_ffn)
    (sm_part, dmod_mine), (sm_sib, dmod_sib) = _sibling_wait("swap_wait_small", ss_send, ss_recv, ss_src, ss_land,
                                                              big["w_up"][0])
    sums = _add2(sm_part, sm_sib)
    dmod_by_core = jnp.stack([dmod_mine, dmod_sib], axis=1)
    dmod_by_core = jnp.where(ci == 0, dmod_by_core, dmod_by_core[:, ::-1])
    dmod_all = dmod_by_core[:, :, :6, :].reshape(N_DEV, 6 * d)
    g_b_ada = sums[n_small:n_small + 6].reshape(1, 6 * d)

    def unpack(name):
        r0, shape = offsets[name]
        size = math.prod(shape)
        nrow = -(-size // d)
        return sums[r0:r0 + nrow].reshape(-1)[:size].reshape(shape)

    p_mix = partials(mix_names, *_chips_wait(
        "grads_wait_mix", False, list(gb_send) + list(gc_send), list(gb_recv) + list(gc_recv),
        list(gb_src) + list(gc_src), list(gb_land) + list(gc_land), sums))
    sb_send, sb_recv, sb_src, sb_land, sb_token = _sibling_start("swap_start_mix", p_mix)

    dmod_cols = lax.dynamic_slice(dmod_all, (0, chip * n_ada), (N_DEV, n_ada)) + sb_token[0:1, 0:1]
    ada = _adamw_ada(c_all, dmod_cols, w_ada[0], m_w_ada[0], v_w_ada[0])
    p_mix, t_mix = _sibling_wait("swap_wait_mix", sb_send, sb_recv, sb_src, sb_land, ada[0])
    big.update(update(mix_names, p_mix, t_mix))

    g_small = {name: unpack(name) for name, _ in small}
    g_small["b_ada"] = g_b_ada
    g_small["conv_w"] = lax.dynamic_slice(g_small["conv_w"], (0, chip * conv_w.shape[2]), (3, conv_w.shape[2]))
    g_small["ffn_conv_w"] = lax.dynamic_slice(g_small["ffn_conv_w"], (0, chip * n_upc), (3, n_upc))
    g_small["ssm_log_step"] = g_small["ssm_log_step"].reshape(1, n_groups)
    small_params = {
        "b_ada": (b_ada, m_b_ada, v_b_ada), "g_pre_mix": (g_pre_mix, m_g_pre_mix, v_g_pre_mix),
        "g_post_mix": (g_post_mix, m_g_post_mix, v_g_post_mix), "ssm_lam_re": (ssm_lam_re, m_ssm_lam_re, v_ssm_lam_re),
        "ssm_lam_im": (ssm_lam_im, m_ssm_lam_im, v_ssm_lam_im),
        "ssm_log_step": (ssm_log_step, m_ssm_log_step, v_ssm_log_step),
        "ssm_b_re": (ssm_b_re, m_ssm_b_re, v_ssm_b_re), "ssm_b_im": (ssm_b_im, m_ssm_b_im, v_ssm_b_im),
        "ssm_c_re": (ssm_c_re, m_ssm_c_re, v_ssm_c_re), "ssm_c_im": (ssm_c_im, m_ssm_c_im, v_ssm_c_im),
        "ssm_d": (ssm_d, m_ssm_d, v_ssm_d), "glu_b": (glu_b, m_glu_b, v_glu_b),
        "g_out_ssm": (g_out_ssm, m_g_out_ssm, v_g_out_ssm), "conv_w": (conv_w, m_conv_w, v_conv_w),
        "g_out_conv": (g_out_conv, m_g_out_conv, v_g_out_conv), "g_pre_ffn": (g_pre_ffn, m_g_pre_ffn, v_g_pre_ffn),
        "g_post_ffn": (g_post_ffn, m_g_post_ffn, v_g_post_ffn),
        "ffn_conv_w": (ffn_conv_w, m_ffn_conv_w, v_ffn_conv_w),
    }

    def natural(a):
        return a[0] if a.ndim > 2 else a

    names = list(small_params)
    items = []
    for nm in names:
        w_, m_, v_ = small_params[nm]
        items.append((natural(w_), g_small[nm].reshape(natural(w_).shape), natural(m_), natural(v_)))
    upd = _adamw_small(items)
    small_out = {}
    for nm, (dl, mo, vo) in zip(names, upd):
        shp = small_params[nm][0].shape
        small_out[nm] = (g_small[nm].reshape(shp), dl.reshape(shp), mo.reshape(shp), vo.reshape(shp))

    loss = g_small["loss"][0, 0]

    order = ["w_ada", "b_ada", "g_pre_mix", "g_post_mix", "w_in", "ssm_lam_re", "ssm_lam_im", "ssm_log_step",
             "ssm_b_re", "ssm_b_im", "ssm_c_re", "ssm_c_im", "ssm_d", "glu_w", "glu_b", "g_out_ssm", "conv_w",
             "g_out_conv", "w_out", "g_pre_ffn", "g_post_ffn", "w_up", "ffn_conv_w", "w_down"]
    results = {"w_ada": tuple(a[None] for a in ada)}
    for nm in big:
        results[nm] = tuple(a[None] for a in big[nm])
    results.update(small_out)
    outs = [loss, grad_x[None]]
    for k in range(4):
        outs += [results[nm][k] for nm in order]
    return tuple(outs)


def _ga_rowsum(ga_re8, ga_im8):
    n = ga_re8.shape[1]

    def body(r_ref, i_ref, o_ref):
        o_ref[...] = jnp.zeros(o_ref.shape, F32)
        o_ref[0:1, :] = _colsum(r_ref[...])
        o_ref[1:2, :] = _colsum(i_ref[...])

    return pl.pallas_call(body, name="ga_rowsum", out_shape=jax.ShapeDtypeStruct((SUBLANES, n), F32))(ga_re8, ga_im8)
```

```python
import functools
import math

import jax
import jax.numpy as jnp
import numpy as np
from jax import lax
from jax.experimental import pallas as pl
from jax.experimental.pallas import tpu as pltpu

F32 = jnp.float32
BF16 = jnp.bfloat16
MESH = pl.DeviceIdType.MESH

EPS = 1e-6
LAMBDA_RE_MAX = -1e-4
ADAM_LR = 0.001
ADAM_B1 = 0.9
ADAM_B2 = 0.999
ADAM_EPS = 1e-08
ADAM_WD = 0.01
ADAM_STEP = 10

SUBLANES = 8
BF16_ROWS = 16
N_CHIPS = 4
N_DEV = 8
CONV_HEAD_DIM = 64
VMEM_BIG = 56 * 1024 * 1024
VMEM_MID = 40 * 1024 * 1024
VMEM_KEEP_OPERANDS_IN_HBM = 62 * 1024 * 1024

TB_MIX = 256
TB_FFN = 256
TB_FFN_UP = 512
TB_SCAN = 1024
W_SCAN = 256
SSM_SPLIT = 4
CW_FFN = 256
SCAN_UNROLL = 4
TB_TN = 512


def _cparams(sem=None, vmem=None):
    kw = {}
    if sem is not None:
        kw["dimension_semantics"] = sem
    if vmem is not None:
        kw["vmem_limit_bytes"] = vmem
    return pltpu.CompilerParams(**kw)


def _blk(t, pref):
    return pref if t % pref == 0 else t


def _dot(a, b):
    return jnp.dot(a.astype(BF16), b.astype(BF16), preferred_element_type=F32)


def _dot_nt(a, b):
    return lax.dot_general(a.astype(BF16), b.astype(BF16), (((1,), (1,)), ((), ())),
                           preferred_element_type=F32)


def _dot_tn(a, b):
    return lax.dot_general(a.astype(BF16), b.astype(BF16), (((0,), (0,)), ((), ())),
                           preferred_element_type=F32)


def _sigmoid(x):
    return 0.5 * jnp.tanh(0.5 * x) + 0.5


_GELU_K = math.sqrt(2.0 / math.pi)
_GELU_C = 0.044715


def _gelu(x):
    th = jnp.tanh(_GELU_K * (x + _GELU_C * x * x * x))
    return x * (0.5 * (1.0 + th))


def _gelu_and_grad(x):
    x2 = x * x
    th = jnp.tanh(_GELU_K * (x + _GELU_C * x2 * x))
    half = 0.5 * (1.0 + th)
    return x * half, half + 0.5 * x * (1.0 - th * th) * _GELU_K * (1.0 + 3.0 * _GELU_C * x2)


def _rowmean(x):
    return jnp.mean(x, axis=-1, keepdims=True)


def _colsum(x):
    return jnp.sum(x, axis=0, keepdims=True)


def _split_dot(x, m):
    hi = x.astype(BF16)
    lo = (x - hi.astype(F32)).astype(BF16)
    return (jnp.dot(hi, m, preferred_element_type=F32) + jnp.dot(lo, m, preferred_element_type=F32))


def _split3_dot(x, m):
    hi = x.astype(BF16)
    r1 = x - hi.astype(F32)
    mid = r1.astype(BF16)
    lo = (r1 - mid.astype(F32)).astype(BF16)
    return (jnp.dot(hi, m, preferred_element_type=F32) + jnp.dot(mid, m, preferred_element_type=F32)
            + jnp.dot(lo, m, preferred_element_type=F32))


def _shift_down(x, halo, k):
    r = pltpu.roll(x, k, 0)
    row = lax.broadcasted_iota(jnp.int32, x.shape, 0)
    last = halo.shape[0]
    for j in range(k):
        r = jnp.where(row == j, halo[last - k + j:last - k + j + 1, :], r)
    return r


def _shift_up(x, halo, k):
    n = x.shape[0]
    r = pltpu.roll(x, n - k, 0)
    row = lax.broadcasted_iota(jnp.int32, x.shape, 0)
    for j in range(k):
        r = jnp.where(row == n - k + j, halo[j:j + 1, :], r)
    return r


def _acc_rows(ref, first, rows):
    @pl.when(first)
    def _():
        ref[...] = jnp.zeros(ref.shape, ref.dtype)
    for j, r in enumerate(rows):
        ref[j:j + 1, :] += r


def _rows(tb, c, col=0):
    return pl.BlockSpec((tb, c), lambda i, col=col: (i, col))


def _full(shape):
    nd = len(shape)
    return pl.BlockSpec(shape, lambda i, nd=nd: (0,) * nd)


def _resident(shape):
    nd = len(shape)
    return pl.BlockSpec(shape, lambda i, nd=nd: (0,) * nd, pipeline_mode=pl.Buffered(1))


def _halo_prev(tb, c, col=0, rows=SUBLANES):
    per = tb // rows
    return pl.BlockSpec((rows, c), lambda i, col=col: (jnp.maximum(i * per - 1, 0), col))


def _halo_next(tb, c, t, col=0, rows=SUBLANES):
    per = tb // rows
    last = t // rows - 1
    return pl.BlockSpec((rows, c), lambda i, col=col: (jnp.minimum((i + 1) * per, last), col))


def _mesh_pos():
    return lax.axis_index("x"), lax.axis_index("y"), lax.axis_index("c")


def _allgather8(x_pad, name):
    m_per, n = x_pad.shape

    def body(x_ref, out_ref, send_sems, recv_sems, local_sem):
        x, y, c = _mesh_pos()
        me, sibling = (x, y, c), (x, y, 1 - c)
        chips = [(1 - x, y), (x, 1 - y), (1 - x, 1 - y)]

        def rows(px, py, pc):
            return out_ref.at[pl.ds((4 * px + 2 * py + pc) * m_per, m_per), :]

        def copy(k, block, to, src=None):
            return pltpu.make_async_remote_copy(
                src_ref=rows(*block) if src is None else src, dst_ref=rows(*block),
                send_sem=send_sems.at[k], recv_sem=recv_sems.at[k], device_id=to, device_id_type=MESH)

        mine = pltpu.make_async_copy(x_ref, rows(*me), local_sem)
        mine.start()
        first = [copy(0, me, sibling, src=x_ref)]
        first += [copy(1 + j, me, (*chip, c), src=x_ref) for j, chip in enumerate(chips)]
        for cp in first:
            cp.start()
        passed = [copy(4 + j, (*chip, c), sibling) for j, chip in enumerate(chips)]
        for j, chip in enumerate(chips):
            copy(1 + j, (*chip, c), me).wait_recv()
            passed[j].start()
        copy(0, sibling, me).wait_recv()
        for j, chip in enumerate(chips):
            copy(4 + j, (*chip, 1 - c), me).wait_recv()
        for cp in first + passed:
            cp.wait_send()
        mine.wait()

    return pl.pallas_call(
        body, name=name,
        out_shape=jax.ShapeDtypeStruct((N_DEV * m_per, n), F32),
        in_specs=[pl.BlockSpec(memory_space=pltpu.VMEM)],
        out_specs=pl.BlockSpec(memory_space=pltpu.VMEM),
        scratch_shapes=[pltpu.SemaphoreType.DMA((7,)), pltpu.SemaphoreType.DMA((7,)), pltpu.SemaphoreType.DMA],
    )(x_pad)


_HBM = pl.BlockSpec(memory_space=pltpu.HBM)
_SEM = pl.BlockSpec(memory_space=pltpu.SEMAPHORE)
_EFFECT = pltpu.SideEffectType.DATAFLOW_SIDE_EFFECTING


def _chip_copy(gather, src_ref, land_ref, send, recv, j, arrival):
    x, y, c = _mesh_pos()
    peer = [(1 - x, y), (x, 1 - y), (1 - x, 1 - y)][j]
    peer_chip = 2 * peer[0] + peer[1]
    my_chip = 2 * x + y
    return pltpu.make_async_remote_copy(
        src_ref=land_ref.at[my_chip] if gather else src_ref.at[peer_chip],
        dst_ref=land_ref.at[peer_chip if arrival else my_chip],
        send_sem=send.at[j], recv_sem=recv.at[j], device_id=(*peer, c), device_id_type=MESH)


def _chips_start(name, gather, srcs, lands, after=None):
    n, ns = len(lands), len(srcs)
    extra = [] if after is None else [after]

    def body(*refs):
        src_refs, land_refs = refs[:ns], refs[ns:ns + n]
        outs = refs[ns + n + len(extra):]
        sends, recvs, token = outs[:n], outs[n:2 * n], outs[-1]
        for k in range(n):
            for j in range(3):
                _chip_copy(gather, src_refs[k] if ns else None, land_refs[k], sends[k], recvs[k], j, False).start()
        token[...] = jnp.zeros(token.shape, F32)

    sem = pltpu.SemaphoreType.DMA((3,))
    thru = tuple(pltpu.HBM(a.shape, a.dtype) for a in list(srcs) + list(lands))
    res = pl.pallas_call(
        body, name=name,
        out_shape=(sem,) * (2 * n) + thru + (jax.ShapeDtypeStruct((SUBLANES, 128), F32),),
        in_specs=[_HBM] * (ns + n) + [pl.BlockSpec(memory_space=pl.ANY)] * len(extra),
        out_specs=(_SEM,) * (2 * n) + (_HBM,) * (ns + n) + (pl.BlockSpec(memory_space=pltpu.VMEM),),
        input_output_aliases={k: 2 * n + k for k in range(ns + n)},
        compiler_params=pltpu.CompilerParams(has_side_effects=_EFFECT),
    )(*[pltpu.with_memory_space_constraint(a, pltpu.HBM) for a in list(srcs) + list(lands)], *extra)
    return res[:n], res[n:2 * n], res[2 * n:2 * n + ns], res[2 * n + ns:2 * n + ns + n], res[-1]


def _chips_wait(name, gather, sends, recvs, srcs, lands, after):
    n, ns = len(lands), len(srcs)

    def body(*refs):
        src_refs, land_refs = refs[:ns], refs[ns:ns + n]
        sends_, recvs_ = refs[ns + n:ns + 2 * n], refs[ns + 2 * n:ns + 3 * n]
        for k in range(n):
            for j in range(3):
                cp = _chip_copy(gather, src_refs[k] if ns else None, land_refs[k], sends_[k], recvs_[k], j, True)
                cp.wait_send()
                cp.wait_recv()

    thru = tuple(pltpu.HBM(a.shape, a.dtype) for a in list(srcs) + list(lands))
    res = pl.pallas_call(
        body, name=name, out_shape=thru,
        in_specs=[_HBM] * (ns + n) + [_SEM] * (2 * n) + [pl.BlockSpec(memory_space=pl.ANY)],
        out_specs=(_HBM,) * (ns + n),
        input_output_aliases={k: k for k in range(ns + n)},
        compiler_params=pltpu.CompilerParams(has_side_effects=_EFFECT),
    )(*srcs, *lands, *sends, *recvs, after)
    return res[:ns], res[ns:]


def _sibling_copy(src_ref, land_ref, send, recv):
    x, y, c = _mesh_pos()
    return pltpu.make_async_remote_copy(src_ref=src_ref, dst_ref=land_ref, send_sem=send.at[0], recv_sem=recv.at[0],
                                        device_id=(x, y, 1 - c), device_id_type=MESH)


def _sibling_start(name, arrs, after=None):
    n = len(arrs)
    extra = [] if after is None else [after]
    lands = [lax.empty(a.shape, a.dtype) for a in arrs]

    def body(*refs):
        src_refs, land_refs = refs[:n], refs[n:2 * n]
        outs = refs[2 * n + len(extra):]
        sends, recvs, token = outs[:n], outs[n:2 * n], outs[-1]
        for k in range(n):
            _sibling_copy(src_refs[k], land_refs[k], sends[k], recvs[k]).start()
        token[...] = jnp.zeros(token.shape, F32)

    sem = pltpu.SemaphoreType.DMA((1,))
    thru = tuple(pltpu.HBM(a.shape, a.dtype) for a in list(arrs) + lands)
    res = pl.pallas_call(
        body, name=name,
        out_shape=(sem,) * (2 * n) + thru + (jax.ShapeDtypeStruct((SUBLANES, 128), F32),),
        in_specs=[_HBM] * (2 * n) + [pl.BlockSpec(memory_space=pl.ANY)] * len(extra),
        out_specs=(_SEM,) * (2 * n) + (_HBM,) * (2 * n) + (pl.BlockSpec(memory_space=pltpu.VMEM),),
        input_output_aliases={k: 2 * n + k for k in range(2 * n)},
        compiler_params=pltpu.CompilerParams(has_side_effects=_EFFECT),
    )(*[pltpu.with_memory_space_constraint(a, pltpu.HBM) for a in list(arrs) + lands], *extra)
    return res[:n], res[n:2 * n], res[2 * n:3 * n], res[3 * n:4 * n], res[-1]


def _sibling_wait(name, sends, recvs, srcs, lands, after):
    n = len(srcs)

    def body(*refs):
        src_refs, land_refs = refs[:n], refs[n:2 * n]
        sends_, recvs_ = refs[2 * n:3 * n], refs[3 * n:4 * n]
        for k in range(n):
            cp = _sibling_copy(src_refs[k], land_refs[k], sends_[k], recvs_[k])
            cp.wait_send()
            cp.wait_recv()

    thru = tuple(pltpu.HBM(a.shape, a.dtype) for a in list(srcs) + list(lands))
    res = pl.pallas_call(
        body, name=name, out_shape=thru,
        in_specs=[_HBM] * (2 * n) + [_SEM] * (2 * n) + [pl.BlockSpec(memory_space=pl.ANY)],
        out_specs=(_HBM,) * (2 * n),
        input_output_aliases={k: k for k in range(2 * n)},
        compiler_params=pltpu.CompilerParams(has_side_effects=_EFFECT),
    )(*srcs, *lands, *sends, *recvs, after)
    return res[:n], res[n:]


def _landing(own, chip):
    zone = lax.empty((N_CHIPS,) + own.shape, own.dtype)
    return lax.dynamic_update_slice(zone, own[None], (chip,) + (0,) * own.ndim)


def _mod_shard(c_all, w_ada_sh, b_sh):
    d, n = w_ada_sh.shape
    bn = 512

    def body(c_ref, w_ref, b_ref, o_ref):
        cc = c_ref[...]
        ca = cc * _sigmoid(cc)
        o_ref[...] = _dot(ca, w_ref[...]) + b_ref[...]

    return pl.pallas_call(
        body, name="mod_shard", grid=(n // bn,),
        out_shape=jax.ShapeDtypeStruct((N_DEV, n), F32),
        in_specs=[_full((N_DEV, d)), pl.BlockSpec((d, bn), lambda j: (0, j)), pl.BlockSpec((1, bn), lambda j: (0, j))],
        out_specs=pl.BlockSpec((N_DEV, bn), lambda j: (0, j)),
        compiler_params=_cparams(("parallel",)),
    )(c_all, w_ada_sh, b_sh)


def _ssm_prep(lam_re, lam_im, log_step):
    g, p = lam_re.shape

    def body(lr_ref, li_ref, ls_ref, ar_ref, ai_ref, cr_ref, ci_ref):
        lr = jnp.minimum(lr_ref[...], LAMBDA_RE_MAX)
        li = li_ref[...]
        st = jnp.exp(ls_ref[...])
        mag = jnp.exp(lr * st)
        ar = mag * jnp.cos(li * st)
        ai = mag * jnp.sin(li * st)
        den = lr * lr + li * li
        nr = ar - 1.0
        ar_ref[...] = ar
        ai_ref[...] = ai
        cr_ref[...] = (nr * lr + ai * li) / den
        ci_ref[...] = (ai * lr - nr * li) / den

    sds = jax.ShapeDtypeStruct((g, p), F32)
    return pl.pallas_call(body, name="ssm_prep", out_shape=(sds,) * 4)(lam_re, lam_im, log_step)


def _ssm_blocks(bt_re, bt_im, ct_re, ct_im, coef_rows, tile_b, tile_c):
    gh, p = bt_re.shape
    gp, h = ct_re.shape
    nb = SSM_SPLIT
    cb, rb = gp // nb, gp // nb

    def body(btr, bti, ctr, cti, cf, tb_ref, tc_ref, bre_o, bim_o, cre_o, cim_o):
        j = pl.program_id(0)
        row = lax.broadcasted_iota(jnp.int32, (gh, cb), 0)
        col = lax.broadcasted_iota(jnp.int32, (gh, cb), 1) + j * cb
        mask = (row >> 4) == (col >> 6)
        cr, ci = cf[0:1, :], cf[1:2, :]
        br = _split3_dot(btr[...], tb_ref[...])
        bi = _split3_dot(bti[...], tb_ref[...])
        bre_o[...] = jnp.where(mask, br * cr - bi * ci, 0.0).astype(BF16)
        bim_o[...] = jnp.where(mask, br * ci + bi * cr, 0.0).astype(BF16)
        row2 = lax.broadcasted_iota(jnp.int32, (rb, gh), 0) + j * rb
        col2 = lax.broadcasted_iota(jnp.int32, (rb, gh), 1)
        mask2 = (row2 >> 6) == (col2 >> 4)
        cre_o[...] = jnp.where(mask2, _split3_dot(ctr[...], tc_ref[...]), 0.0).astype(BF16)
        cim_o[...] = jnp.where(mask2, _split3_dot(cti[...], tc_ref[...]), 0.0).astype(BF16)

    bspec = pl.BlockSpec((gh, cb), lambda j: (0, j))
    cspec = pl.BlockSpec((rb, gh), lambda j: (j, 0))
    cin = pl.BlockSpec((rb, h), lambda j: (j, 0))
    return pl.pallas_call(
        body, name="ssm_blocks", grid=(nb,),
        out_shape=(jax.ShapeDtypeStruct((gh, gp), BF16),) * 2 + (jax.ShapeDtypeStruct((gp, gh), BF16),) * 2,
        in_specs=[_full((gh, p)), _full((gh, p)), cin, cin, pl.BlockSpec((SUBLANES, cb), lambda j: (0, j)),
                  _full(tile_b.shape), _full(tile_c.shape)],
        out_specs=(bspec, bspec, cspec, cspec),
        compiler_params=_cparams(("parallel",)),
    )(bt_re, bt_im, ct_re, ct_im, coef_rows, tile_b, tile_c)


def _scan_consts(a_ref, reverse):
    w = a_ref.shape[1]
    ar1 = a_ref[0:1, :]
    ai1 = a_ref[1:2, :]
    if reverse:
        ai1 = -ai1
    pr, pi = [ar1], [ai1]
    for _ in range(1, SUBLANES):
        nr = pr[-1] * ar1 - pi[-1] * ai1
        ni = pr[-1] * ai1 + pi[-1] * ar1
        pr.append(nr)
        pi.append(ni)
    row = lax.broadcasted_iota(jnp.int32, (SUBLANES, w), 0)
    dist = (SUBLANES - 1 - row) if reverse else row

    def pick(vals):
        out = jnp.broadcast_to(vals[SUBLANES - 1], (SUBLANES, w))
        for r in range(SUBLANES - 1):
            out = jnp.where(dist == r, vals[r], out)
        return out

    p_r, p_i = pick(pr), pick(pi)
    steps = []
    for k in (1, 2, 4):
        steps.append((k, jnp.where(dist >= k, pr[k - 1], 0.0), jnp.where(dist >= k, pi[k - 1], 0.0)))
    a8 = (jnp.broadcast_to(pr[SUBLANES - 1], (SUBLANES, w)), jnp.broadcast_to(pi[SUBLANES - 1], (SUBLANES, w)))
    return row, p_r, p_i, steps, a8


def _scan_tile(xr, xi, cr, ci, consts, reverse):
    row, p_r, p_i, steps, (a8r, a8i) = consts
    for k, s_r, s_i in steps:
        sh = (SUBLANES - k) if reverse else k
        qr = pltpu.roll(xr, sh, 0)
        qi = pltpu.roll(xi, sh, 0)
        xr, xi = xr + s_r * qr - s_i * qi, xi + s_r * qi + s_i * qr
    outr = xr + p_r * cr - p_i * ci
    outi = xi + p_r * ci + p_i * cr
    e = 0 if reverse else SUBLANES - 1
    er = jnp.broadcast_to(xr[e:e + 1, :], xr.shape)
    ei = jnp.broadcast_to(xi[e:e + 1, :], xi.shape)
    return outr, outi, er + a8r * cr - a8i * ci, ei + a8r * ci + a8i * cr


def _scan_fwd(a_rows, bu_re, bu_im):
    t, n = bu_re.shape
    tb, w = _blk(t, TB_SCAN), W_SCAN
    ntile = tb // SUBLANES

    def body(a_ref, br_ref, bi_ref, sr_ref, si_ref, car, cai):
        @pl.when(pl.program_id(1) == 0)
        def _():
            car[...] = jnp.zeros(car.shape, F32)
            cai[...] = jnp.zeros(cai.shape, F32)
        consts = _scan_consts(a_ref, False)

        def pair(i, carry):
            o = pl.multiple_of(i * BF16_ROWS, BF16_ROWS)
            b_r = br_ref[pl.ds(o, BF16_ROWS), :].astype(F32)
            b_i = bi_ref[pl.ds(o, BF16_ROWS), :].astype(F32)
            outs = []
            for h in range(2):
                rows = slice(h * SUBLANES, (h + 1) * SUBLANES)
                outr, outi, ncr, nci = _scan_tile(b_r[rows, :], b_i[rows, :], carry[0], carry[1], consts, False)
                outs.append((outr, outi))
                carry = (ncr, nci)
            sr_ref[pl.ds(o, BF16_ROWS), :] = jnp.concatenate([outs[0][0], outs[1][0]], axis=0).astype(BF16)
            si_ref[pl.ds(o, BF16_ROWS), :] = jnp.concatenate([outs[0][1], outs[1][1]], axis=0).astype(BF16)
            return carry

        def pairs(i, carry):
            for s in range(SCAN_UNROLL // 2):
                carry = pair(i * (SCAN_UNROLL // 2) + s, carry)
            return carry

        cr, ci = lax.fori_loop(0, ntile // SCAN_UNROLL, pairs, (car[...], cai[...]))
        car[...] = cr
        cai[...] = ci

    spec = pl.BlockSpec((tb, w), lambda s, k: (k, s))
    sds = jax.ShapeDtypeStruct((t, n), BF16)
    return pl.pallas_call(
        body, name="scan_fwd", grid=(n // w, t // tb), out_shape=(sds, sds),
        in_specs=[pl.BlockSpec((SUBLANES, w), lambda s, k: (0, s)), spec, spec], out_specs=(spec, spec),
        scratch_shapes=[pltpu.VMEM((SUBLANES, w), F32), pltpu.VMEM((SUBLANES, w), F32)],
        compiler_params=_cparams(("parallel", "arbitrary"), VMEM_MID),
    )(a_rows, bu_re, bu_im)


def _scan_bwd(a_rows, g_re, g_im, s_re, s_im):
    t, n = g_re.shape
    tb, w = _blk(t, TB_SCAN), W_SCAN
    ntile = tb // SUBLANES
    npair = tb // BF16_ROWS
    nt = t // tb

    def body(a_ref, gr_ref, gi_ref, sr_ref, si_ref, or_ref, oi_ref, gar_ref, gai_ref, car, cai):
        @pl.when(pl.program_id(1) == 0)
        def _():
            car[...] = jnp.zeros(car.shape, F32)
            cai[...] = jnp.zeros(cai.shape, F32)
            gar_ref[...] = jnp.zeros(gar_ref.shape, F32)
            gai_ref[...] = jnp.zeros(gai_ref.shape, F32)
        consts = _scan_consts(a_ref, True)
        row = consts[0]

        def pair(i, carry):
            cr, ci, accr, acci = carry
            o = pl.multiple_of((npair - 1 - i) * BF16_ROWS, BF16_ROWS)
            s_r = sr_ref[pl.ds(o, BF16_ROWS), :].astype(F32)
            s_i = si_ref[pl.ds(o, BF16_ROWS), :].astype(F32)
            g_r = gr_ref[pl.ds(o, BF16_ROWS), :].astype(F32)
            g_i = gi_ref[pl.ds(o, BF16_ROWS), :].astype(F32)
            outs = [None, None]
            for h in (1, 0):
                rows = slice(h * SUBLANES, (h + 1) * SUBLANES)
                outr, outi, ncr, nci = _scan_tile(g_r[rows, :], g_i[rows, :], cr, ci, consts, True)
                outs[h] = (outr, outi)
                gnr = jnp.where(row == SUBLANES - 1, cr, pltpu.roll(outr, SUBLANES - 1, 0))
                gni = jnp.where(row == SUBLANES - 1, ci, pltpu.roll(outi, SUBLANES - 1, 0))
                sr = s_r[h * SUBLANES:(h + 1) * SUBLANES, :]
                si = s_i[h * SUBLANES:(h + 1) * SUBLANES, :]
                accr, acci = accr + sr * gnr + si * gni, acci + sr * gni - si * gnr
                cr, ci = ncr, nci
            or_ref[pl.ds(o, BF16_ROWS), :] = jnp.concatenate([outs[0][0], outs[1][0]], axis=0).astype(BF16)
            oi_ref[pl.ds(o, BF16_ROWS), :] = jnp.concatenate([outs[0][1], outs[1][1]], axis=0).astype(BF16)
            return cr, ci, accr, acci

        def pairs(i, carry):
            for s in range(SCAN_UNROLL // 2):
                carry = pair(i * (SCAN_UNROLL // 2) + s, carry)
            return carry

        cr, ci, accr, acci = lax.fori_loop(0, ntile // SCAN_UNROLL, pairs,
                                           (car[...], cai[...], gar_ref[...], gai_ref[...]))
        car[...] = cr
        cai[...] = ci
        gar_ref[...] = accr
        gai_ref[...] = acci

    spec = pl.BlockSpec((tb, w), lambda s, k: (nt - 1 - k, s))
    aspec = pl.BlockSpec((SUBLANES, w), lambda s, k: (0, s))
    sds = jax.ShapeDtypeStruct((t, n), BF16)
    asds = jax.ShapeDtypeStruct((SUBLANES, n), F32)
    return pl.pallas_call(
        body, name="scan_bwd", grid=(n // w, nt), out_shape=(sds, sds, asds, asds),
        in_specs=[aspec, spec, spec, spec, spec], out_specs=(spec, spec, aspec, aspec),
        scratch_shapes=[pltpu.VMEM((SUBLANES, w), F32), pltpu.VMEM((SUBLANES, w), F32)],
        compiler_params=_cparams(("parallel", "arbitrary"), VMEM_MID),
    )(a_rows, g_re, g_im, s_re, s_im)


def _mix_in(x, vec, w_in_st, b_re, b_im):
    t, d = x.shape
    ns, _, nc = w_in_st.shape
    dssm, nstate = b_re.shape
    du, ds = dssm // SSM_SPLIT, nstate // SSM_SPLIT
    tb = _blk(t, TB_MIX)

    def body(x_ref, vec_ref, w_ref, bre_ref, bim_ref, proj_ref, bur_ref, bui_ref, h1_ref):
        xv = x_ref[...]
        r = lax.rsqrt(_rowmean(xv * xv) + EPS)
        h = xv * r * vec_ref[0:1, :] * vec_ref[1:2, :] + vec_ref[2:3, :]
        hb = h.astype(BF16)
        h1_ref[...] = hb
        u = None
        for j in range(ns):
            pj = jnp.dot(hb, w_ref[j], preferred_element_type=F32)
            proj_ref[:, j * nc:(j + 1) * nc] = pj.astype(BF16)
            if j == 0:
                u = pj
        ub = u.astype(BF16)
        for q in range(SSM_SPLIT):
            rq, cq = slice(q * du, (q + 1) * du), slice(q * ds, (q + 1) * ds)
            bur_ref[:, cq] = jnp.dot(ub[:, rq], bre_ref[rq, cq], preferred_element_type=F32).astype(BF16)
            bui_ref[:, cq] = jnp.dot(ub[:, rq], bim_ref[rq, cq], preferred_element_type=F32).astype(BF16)

    return pl.pallas_call(
        body, name="mix_in", grid=(t // tb,),
        out_shape=(jax.ShapeDtypeStruct((t, ns * nc), BF16), jax.ShapeDtypeStruct((t, nstate), BF16),
                   jax.ShapeDtypeStruct((t, nstate), BF16), jax.ShapeDtypeStruct((t, d), BF16)),
        in_specs=[_rows(tb, d), _full((SUBLANES, d)), _resident(w_in_st.shape), _resident(b_re.shape),
                  _resident(b_im.shape)],
        out_specs=(_rows(tb, ns * nc), _rows(tb, nstate), _rows(tb, nstate), _rows(tb, d)),
        compiler_params=_cparams(("parallel",), VMEM_BIG),
    )(x, vec, w_in_st, b_re, b_im)


def _head_ms(y, h_ref):
    return _split_dot(y * y, h_ref[...])


def _conv3(x, halo, w_ref):
    return w_ref[0:1, :] * _shift_down(x, halo, 2) + w_ref[1:2, :] * _shift_down(x, halo, 1) + w_ref[2:3, :] * x


def _mix_out(x, proj, s_re, s_im, c_re, c_im, v512, convw, glu_w, h16, h64, w_out, vd):
    t, d = x.shape
    dh = c_re.shape[1]
    nstate = s_re.shape[1]
    du, ds = dh // SSM_SPLIT, nstate // SSM_SPLIT
    tb = _blk(t, TB_MIX)

    def body(x_ref, u_ref, bg_ref, cg_ref, v_ref, cgh_ref, vh_ref, sr_ref, si_ref, cre_ref, cim_ref, p_ref,
             cw_ref, gw_ref, h16_ref, h64_ref, wo_ref, vd_ref, y1_ref, o_ref, x2_ref):
        i = pl.program_id(0)
        u = u_ref[...].astype(F32)
        ys = []
        for q in range(SSM_SPLIT):
            rq, cq = slice(q * ds, (q + 1) * ds), slice(q * du, (q + 1) * du)
            ys.append(_dot(sr_ref[:, rq], cre_ref[rq, cq]) - _dot(si_ref[:, rq], cim_ref[rq, cq]))
        ys = jnp.concatenate(ys, axis=1)
        y1 = ys + p_ref[0:1, :] * u
        y1_ref[...] = y1
        z = _gelu(y1)
        q = _dot(z, gw_ref[...]) + p_ref[1:2, :]
        ya = z * _sigmoid(q)
        na = ya * lax.rsqrt(_head_ms(ya, h16_ref) + EPS) * p_ref[2:3, :]
        cv = cg_ref[...].astype(F32) * v_ref[...].astype(F32)
        cvh = jnp.where(i > 0, cgh_ref[...].astype(F32) * vh_ref[...].astype(F32), 0.0)
        yb = bg_ref[...].astype(F32) * _conv3(cv, cvh, cw_ref)
        nb = yb * lax.rsqrt(_head_ms(yb, h64_ref) + EPS) * p_ref[3:4, :]
        o = _dot(na, wo_ref[0:dh, :]) + _dot(nb, wo_ref[dh:2 * dh, :])
        o_ref[...] = o
        on = o * lax.rsqrt(_rowmean(o * o) + EPS) * vd_ref[0:1, :]
        x2_ref[...] = x_ref[...] + vd_ref[1:2, :] * on

    return pl.pallas_call(
        body, name="mix_out", grid=(t // tb,),
        out_shape=(jax.ShapeDtypeStruct((t, dh), F32), jax.ShapeDtypeStruct((t, d), F32),
                   jax.ShapeDtypeStruct((t, d), F32)),
        in_specs=[_rows(tb, d), _rows(tb, dh, 0), _rows(tb, dh, 1), _rows(tb, dh, 2), _rows(tb, dh, 3),
                  _halo_prev(tb, dh, 2, BF16_ROWS), _halo_prev(tb, dh, 3, BF16_ROWS), _rows(tb, nstate), _rows(tb, nstate),
                  _full(c_re.shape), _full(c_im.shape), _full(v512.shape), _full(convw.shape), _full(glu_w.shape),
                  _full(h16.shape), _full(h64.shape), _full(w_out.shape), _full(vd.shape)],
        out_specs=(_rows(tb, dh), _rows(tb, d), _rows(tb, d)),
        compiler_params=_cparams(("parallel",), VMEM_BIG),
    )(x, proj, proj, proj, proj, proj, proj, s_re, s_im, c_re, c_im, v512, convw, glu_w, h16, h64, w_out, vd)


def _ffn_up(x2, vec, w_up_st):
    t, d = x2.shape
    ns, _, nc = w_up_st.shape
    tb = _blk(t, TB_FFN_UP)

    def body(x_ref, vec_ref, w_ref, up_ref, h2_ref):
        xv = x_ref[...]
        r = lax.rsqrt(_rowmean(xv * xv) + EPS)
        h = xv * r * vec_ref[0:1, :] * vec_ref[1:2, :] + vec_ref[2:3, :]
        hb = h.astype(BF16)
        h2_ref[...] = hb
        for j in range(ns):
            up_ref[:, j * nc:(j + 1) * nc] = jnp.dot(hb, w_ref[j], preferred_element_type=F32)

    return pl.pallas_call(
        body, name="ffn_up", grid=(t // tb,),
        out_shape=(jax.ShapeDtypeStruct((t, ns * nc), F32), jax.ShapeDtypeStruct((t, d), BF16)),
        in_specs=[_rows(tb, d), _full((SUBLANES, d)), _resident(w_up_st.shape)],
        out_specs=(_rows(tb, ns * nc), _rows(tb, d)),
        compiler_params=_cparams(("parallel",), VMEM_BIG),
    )(x2, vec, w_up_st)


def _ffn_down(up, fw, w_down, w_down_t, x2, tgt, vd):
    t, nh = up.shape
    dff, d = w_down.shape
    tb = _blk(t, TB_FFN)
    inv_d = 1.0 / d

    def body(up_ref, uph_ref, fw_ref, wd_ref, wdt_ref, x2_ref, tgt_ref, vd_ref,
             act_ref, ddn_ref, dout_ref, dhid_ref, vec_ref, loss_ref, a_s, vv_s, sg_s):
        i = pl.program_id(0)

        def conv_cols(sl):
            x = up_ref[:, sl]
            halo = jnp.where(i > 0, uph_ref[:, sl], 0.0)
            return (fw_ref[0:1, sl] * _shift_down(x, halo, 2) + fw_ref[1:2, sl] * _shift_down(x, halo, 1)
                    + fw_ref[2:3, sl] * x)

        dn = None
        for o in range(0, dff, CW_FFN):
            sl = slice(o, o + CW_FFN)
            a = conv_cols(sl)
            vv = conv_cols(slice(dff + o, dff + o + CW_FFN))
            sg = _sigmoid(a)
            si = a * sg
            a_s[:, sl] = si
            vv_s[:, sl] = vv
            sg_s[:, sl] = sg
            actb = (si * vv).astype(BF16)
            act_ref[:, sl] = actb
            pj = lax.dot_general(actb, wdt_ref[:, sl], (((1,), (1,)), ((), ())), preferred_element_type=F32)
            dn = pj if dn is None else dn + pj
        r3 = lax.rsqrt(_rowmean(dn * dn) + EPS)
        xn = dn * r3
        g = vd_ref[0:1, :]
        gt2 = vd_ref[1:2, :]
        dnn = xn * g
        diff = x2_ref[...] + gt2 * dnn - tgt_ref[...]
        part = 0.5 * inv_d * jnp.sum(diff * diff)

        @pl.when(i == 0)
        def _():
            loss_ref[...] = jnp.zeros(loss_ref.shape, F32)
        loss_ref[...] += part
        dout = diff * inv_d
        dout_ref[...] = dout
        ddnn = dout * gt2
        _acc_rows(vec_ref, i == 0, [_colsum(dout * dnn), _colsum(ddnn * xn)])
        dxn = ddnn * g
        ddn = r3 * (dxn - xn * _rowmean(dxn * xn))
        ddnb = ddn.astype(BF16)
        ddn_ref[...] = ddnb
        for o in range(0, dff, CW_FFN):
            sl = slice(o, o + CW_FFN)
            dact = lax.dot_general(ddnb, wd_ref[sl, :], (((1,), (1,)), ((), ())), preferred_element_type=F32)
            si, vv, sg = a_s[:, sl], vv_s[:, sl], sg_s[:, sl]
            dhid_ref[:, sl] = (dact * vv * (sg + si * (1.0 - sg))).astype(BF16)
            dhid_ref[:, dff + o:dff + o + CW_FFN] = (dact * si).astype(BF16)

    return pl.pallas_call(
        body, name="ffn_down", grid=(t // tb,),
        scratch_shapes=[pltpu.VMEM((tb, dff), F32)] * 3,
        out_shape=(jax.ShapeDtypeStruct((t, dff), BF16), jax.ShapeDtypeStruct((t, d), BF16),
                   jax.ShapeDtypeStruct((t, d), F32), jax.ShapeDtypeStruct((t, nh), BF16),
                   jax.ShapeDtypeStruct((SUBLANES, d), F32), jax.ShapeDtypeStruct((SUBLANES, 128), F32)),
        in_specs=[_rows(tb, nh), _halo_prev(tb, nh), _full(fw.shape), _resident(w_down.shape),
                  _resident(w_down_t.shape), _rows(tb, d),
                  _rows(tb, d), _full(vd.shape)],
        out_specs=(_rows(tb, dff), _rows(tb, d), _rows(tb, d), _rows(tb, nh), _full((SUBLANES, d)),
                   _full((SUBLANES, 128))),
        compiler_params=_cparams(("arbitrary",), VMEM_BIG),
    )(up, up, fw, w_down, w_down_t, x2, tgt, vd)


def _ffn_up_bwd(dhid, up, fw, x2, dout, vec, w_up_st):
    t, nh = dhid.shape
    d = x2.shape[1]
    ns, _, nc = w_up_st.shape
    tb = _blk(t, TB_FFN)
    nblk = t // tb
    cw = 128

    def body(dh_ref, dhn_ref, up_ref, fw_ref, x2_ref, dout_ref, vec_ref, w_ref,
             dx2_ref, dup_ref, vp_ref, df_ref):
        i = pl.program_id(0)

        @pl.when(i == 0)
        def _():
            df_ref[...] = jnp.zeros(df_ref.shape, F32)
        dh2 = None
        for j in range(ns):
            for o in range(j * nc, (j + 1) * nc, cw):
                sl = slice(o, o + cw)
                dh = dh_ref[:, sl].astype(F32)
                dhn = jnp.where(i < nblk - 1, dhn_ref[:, sl].astype(F32), 0.0)
                dh1 = _shift_up(dh, dhn, 1)
                dh2s = _shift_up(dh, dhn, 2)
                dup_ref[:, sl] = (fw_ref[2:3, sl] * dh + fw_ref[1:2, sl] * dh1 + fw_ref[0:1, sl] * dh2s).astype(BF16)
                up_v = up_ref[:, sl]
                df_ref[0:1, sl] += _colsum(dh2s * up_v)
                df_ref[1:2, sl] += _colsum(dh1 * up_v)
                df_ref[2:3, sl] += _colsum(dh * up_v)
            pj = lax.dot_general(dup_ref[:, j * nc:(j + 1) * nc], w_ref[j], (((1,), (1,)), ((), ())),
                                 preferred_element_type=F32)
            dh2 = pj if dh2 is None else dh2 + pj
        xv = x2_ref[...]
        r = lax.rsqrt(_rowmean(xv * xv) + EPS)
        xn = xv * r
        g = vec_ref[0:1, :]
        hg = xn * g
        dhg = dh2 * vec_ref[1:2, :]
        _acc_rows(vp_ref, i == 0, [_colsum(dh2), _colsum(dh2 * hg), _colsum(dhg * xn)])
        dxn = dhg * g
        dx2_ref[...] = dout_ref[...] + r * (dxn - xn * _rowmean(dxn * xn))

    return pl.pallas_call(
        body, name="ffn_up_bwd", grid=(nblk,),
        out_shape=(jax.ShapeDtypeStruct((t, d), F32), jax.ShapeDtypeStruct((t, nh), BF16),
                   jax.ShapeDtypeStruct((SUBLANES, d), F32), jax.ShapeDtypeStruct((SUBLANES, nh), F32)),
        in_specs=[_rows(tb, nh), _halo_next(tb, nh, t, rows=BF16_ROWS), _rows(tb, nh), _full(fw.shape),
                  _rows(tb, d), _rows(tb, d), _full(vec.shape), _resident(w_up_st.shape)],
        out_specs=(_rows(tb, d), _rows(tb, nh), _full((SUBLANES, d)), _full((SUBLANES, nh))),
        compiler_params=_cparams(("arbitrary",), VMEM_BIG),
    )(dhid, dhid, up, fw, x2, dout, vec, w_up_st)


def _mix_out_bwd(dx2, o, y1, proj, s_re, s_im, c_re, c_im, v512, convw, glu_w, h16, h64, w_out, vd):
    t, d = dx2.shape
    dh = y1.shape[1]
    nstate = c_re.shape[0]
    du, ds = dh // SSM_SPLIT, nstate // SSM_SPLIT
    tb = _blk(t, TB_MIX)

    def body(dx2_ref, o_ref, y1_ref, u_ref, bg_ref, cg_ref, v_ref, cgh_ref, vh_ref, cre_ref, cim_ref, p_ref,
             cw_ref, gw_ref, h16_ref, h64_ref, wo_ref, vd_ref, sr_ref, si_ref,
             do_ref, ycat_ref, z_ref, dq_ref, dy1_ref, gr_ref, gi_ref, dcc_ref, dbg_ref, vpd_ref, vp5_ref,
             dcr_ref, dci_ref):
        i = pl.program_id(0)
        first = i == 0

        @pl.when(first)
        def _():
            dcr_ref[...] = jnp.zeros(dcr_ref.shape, F32)
            dci_ref[...] = jnp.zeros(dci_ref.shape, F32)
        ov = o_ref[...]
        ro = lax.rsqrt(_rowmean(ov * ov) + EPS)
        on_ = ov * ro
        g = vd_ref[0:1, :]
        dx2v = dx2_ref[...]
        don = dx2v * vd_ref[1:2, :]
        _acc_rows(vpd_ref, first, [_colsum(dx2v * on_ * g), _colsum(don * on_)])
        dxn = don * g
        dob = (ro * (dxn - on_ * _rowmean(dxn * on_))).astype(BF16)
        do_ref[...] = dob
        dyc_a =lax.dot_general(dob, wo_ref[0:dh, :], (((1,), (1,)), ((), ())), preferred_element_type=F32)
        dyc_b = lax.dot_general(dob, wo_ref[dh:2 * dh, :], (((1,), (1,)), ((), ())), preferred_element_type=F32)
        y1v = y1_ref[...]
        u = u_ref[...].astype(F32)
        z, dz_dy1 = _gelu_and_grad(y1v)
        zb = z.astype(BF16)
        sg = _sigmoid(jnp.dot(zb, gw_ref[...], preferred_element_type=F32) + p_ref[1:2, :])
        ya = z * sg
        ra = lax.rsqrt(_head_ms(ya, h16_ref) + EPS)
        yan = ya * ra
        ga = p_ref[2:3, :]
        ycat_ref[:, 0:dh] = (yan * ga).astype(BF16)
        dyn = dyc_a * ga
        dya = ra * (dyn - yan * _split_dot(dyn * yan, h16_ref[...]))
        dq = dya * z * sg * (1.0 - sg)
        dqb = dq.astype(BF16)
        z_ref[...] = zb
        dq_ref[...] = dqb
        dz = dya * sg + lax.dot_general(dqb, gw_ref[...], (((1,), (1,)), ((), ())), preferred_element_type=F32)
        dy1 = dz * dz_dy1
        dy1_ref[...] = dy1
        dy1b = dy1.astype(BF16)
        for q in range(SSM_SPLIT):
            rq, cq = slice(q * ds, (q + 1) * ds), slice(q * du, (q + 1) * du)
            gr_ref[:, rq] = lax.dot_general(dy1b[:, cq], cre_ref[rq, cq], (((1,), (1,)), ((), ())),
                                            preferred_element_type=F32).astype(BF16)
            gi_ref[:, rq] = (-lax.dot_general(dy1b[:, cq], cim_ref[rq, cq], (((1,), (1,)), ((), ())),
                                              preferred_element_type=F32)).astype(BF16)
            dcr_ref[rq, :] += _dot_tn(sr_ref[:, rq], dy1b[:, cq])
            dci_ref[rq, :] += _dot_tn(si_ref[:, rq], dy1b[:, cq])
        bg = bg_ref[...].astype(F32)
        cv = cg_ref[...].astype(F32) * v_ref[...].astype(F32)
        cvh = jnp.where(i > 0, cgh_ref[...].astype(F32) * vh_ref[...].astype(F32), 0.0)
        cv1 = _shift_down(cv, cvh, 1)
        cv2 = _shift_down(cv, cvh, 2)
        cc = cw_ref[0:1, :] * cv2 + cw_ref[1:2, :] * cv1 + cw_ref[2:3, :] * cv
        yb = bg * cc
        rb = lax.rsqrt(_head_ms(yb, h64_ref) + EPS)
        ybn = yb * rb
        gb = p_ref[3:4, :]
        ycat_ref[:, dh:2 * dh] = (ybn * gb).astype(BF16)
        dynb = dyc_b * gb
        dyb = rb * (dynb - ybn * _split_dot(dynb * ybn, h64_ref[...]))
        dcc = dyb * bg
        dbg_ref[...] = dyb * cc
        dcc_ref[...] = dcc
        _acc_rows(vp5_ref, first, [_colsum(dyc_a * yan), _colsum(dyc_b * ybn), _colsum(dq), _colsum(dy1 * u),
                                   _colsum(dcc * cv2), _colsum(dcc * cv1), _colsum(dcc * cv)])

    return pl.pallas_call(
        body, name="mix_out_bwd", grid=(t // tb,),
        out_shape=(jax.ShapeDtypeStruct((t, d), BF16), jax.ShapeDtypeStruct((t, 2 * dh), BF16),
                   jax.ShapeDtypeStruct((t, dh), BF16), jax.ShapeDtypeStruct((t, dh), BF16),
                   jax.ShapeDtypeStruct((t, dh), F32), jax.ShapeDtypeStruct((t, nstate), BF16),
                   jax.ShapeDtypeStruct((t, nstate), BF16), jax.ShapeDtypeStruct((t, dh), F32),
                   jax.ShapeDtypeStruct((t, dh), F32), jax.ShapeDtypeStruct((SUBLANES, d), F32),
                   jax.ShapeDtypeStruct((SUBLANES, dh), F32), jax.ShapeDtypeStruct((nstate, du), F32),
                   jax.ShapeDtypeStruct((nstate, du), F32)),
        in_specs=[_rows(tb, d), _rows(tb, d), _rows(tb, dh), _rows(tb, dh, 0), _rows(tb, dh, 1), _rows(tb, dh, 2),
                  _rows(tb, dh, 3), _halo_prev(tb, dh, 2, BF16_ROWS), _halo_prev(tb, dh, 3, BF16_ROWS), _resident(c_re.shape),
                  _resident(c_im.shape), _full(v512.shape), _full(convw.shape), _resident(glu_w.shape),
                  _resident(h16.shape), _resident(h64.shape), _resident(w_out.shape), _full(vd.shape),
                  _rows(tb, nstate), _rows(tb, nstate)],
        out_specs=(_rows(tb, d), _rows(tb, 2 * dh), _rows(tb, dh), _rows(tb, dh), _rows(tb, dh), _rows(tb, nstate),
                   _rows(tb, nstate), _rows(tb, dh), _rows(tb, dh), _full((SUBLANES, d)), _full((SUBLANES, dh)),
                   _full((nstate, du)), _full((nstate, du))),
        compiler_params=_cparams(("arbitrary",), VMEM_BIG),
    )(dx2, o, y1, proj, proj, proj, proj, proj, proj, c_re, c_im, v512, convw, glu_w, h16, h64, w_out, vd,
      s_re, s_im)


def _mix_in_bwd(gt_re, gt_im, b_re, b_im, dy1, dcc, dbg, proj, x, dx2, vec, v512, convw, w_in_st):
    t, d = x.shape
    dh = dy1.shape[1]
    nstate = gt_re.shape[1]
    du_w, ds = dh // SSM_SPLIT, nstate // SSM_SPLIT
    ns, _, nc = w_in_st.shape
    tb = _blk(t, TB_MIX)
    nblk = t // tb

    def body(gr_ref, gi_ref, bre_ref, bim_ref, dy1_ref, dcc_ref, dccn_ref, dbg_ref, u_ref, cg_ref, v_ref, x_ref,
             dx2_ref, vec_ref, p_ref, cw_ref, w_ref, gx_ref, dproj_ref, vp_ref, dbr_ref, dbi_ref):
        i = pl.program_id(0)

        @pl.when(i == 0)
        def _():
            dbr_ref[...] = jnp.zeros(dbr_ref.shape, F32)
            dbi_ref[...] = jnp.zeros(dbi_ref.shape, F32)
        ub = u_ref[...].astype(BF16)
        du = []
        for q in range(SSM_SPLIT):
            rq, cq = slice(q * du_w, (q + 1) * du_w), slice(q * ds, (q + 1) * ds)
            du.append(lax.dot_general(gr_ref[:, cq].astype(BF16), bre_ref[rq, cq], (((1,), (1,)), ((), ())),
                                      preferred_element_type=F32)
                      + lax.dot_general(gi_ref[:, cq].astype(BF16), bim_ref[rq, cq], (((1,), (1,)), ((), ())),
                                        preferred_element_type=F32))
            dbr_ref[rq, :] += _dot_tn(ub[:, rq], gr_ref[:, cq])
            dbi_ref[rq, :] += _dot_tn(ub[:, rq], gi_ref[:, cq])
        du = dy1_ref[...] * p_ref[0:1, :] + jnp.concatenate(du, axis=1)
        dcc = dcc_ref[...]
        dccn = jnp.where(i < nblk - 1, dccn_ref[...], 0.0)
        dcv = (cw_ref[2:3, :] * dcc + cw_ref[1:2, :] * _shift_up(dcc, dccn, 1)
               + cw_ref[0:1, :] * _shift_up(dcc, dccn, 2))
        parts = [du, dbg_ref[...], dcv * v_ref[...].astype(F32), dcv * cg_ref[...].astype(F32)]
        xv = x_ref[...]
        r = lax.rsqrt(_rowmean(xv * xv) + EPS)
        xn = xv * r
        g = vec_ref[0:1, :]
        hg = xn * g
        dh1 = None
        for j in range(ns):
            pb = parts[j].astype(BF16)
            dproj_ref[:, j * nc:(j + 1) * nc] = pb
            pj =lax.dot_general(pb, w_ref[j], (((1,), (1,)), ((), ())), preferred_element_type=F32)
            dh1 = pj if dh1 is None else dh1 + pj
        dhg = dh1 * vec_ref[1:2, :]
        _acc_rows(vp_ref, i == 0, [_colsum(dh1), _colsum(dh1 * hg), _colsum(dhg * xn)])
        dxn = dhg * g
        gx_ref[...] = dx2_ref[...] + r * (dxn - xn * _rowmean(dxn * xn))

    assert nc == dh and ns == 4
    return pl.pallas_call(
        body, name="mix_in_bwd", grid=(nblk,),
        out_shape=(jax.ShapeDtypeStruct((t, d), F32), jax.ShapeDtypeStruct((t, ns * nc), BF16),
                   jax.ShapeDtypeStruct((SUBLANES, d), F32), jax.ShapeDtypeStruct((dh, ds), F32),
                   jax.ShapeDtypeStruct((dh, ds), F32)),
        in_specs=[_rows(tb, nstate), _rows(tb, nstate), _resident(b_re.shape), _resident(b_im.shape), _rows(tb, dh),
                  _rows(tb, dh), _halo_next(tb, dh, t), _rows(tb, dh), _rows(tb, dh, 0), _rows(tb, dh, 2),
                  _rows(tb, dh, 3), _rows(tb, d), _rows(tb, d), _full(vec.shape), _full(v512.shape),
                  _full(convw.shape), _resident(w_in_st.shape)],
        out_specs=(_rows(tb, d), _rows(tb, ns * nc), _full((SUBLANES, d)), _full((dh, ds)), _full((dh, ds))),
        compiler_params=_cparams(("arbitrary",), VMEM_BIG),
    )(gt_re, gt_im, b_re, b_im, dy1, dcc, dcc, dbg, proj, proj, proj, x, dx2, vec, v512, convw, w_in_st)


def _matmul_tn(a, b, m, bn, out_dtype, name, diag=False, bt=TB_TN, after=None):
    t = a.shape[0]
    n = b.shape[1]
    bt = _blk(t, bt)
    nk = t // bt
    extra = [] if after is None else [after]
    a_map = (lambda j, k: (k, j)) if diag else (lambda j, k: (k, 0))

    def body(a_ref, b_ref, *rest):
        o_ref, acc_ref = rest[-2:]
        k = pl.program_id(1)

        @pl.when(k == 0)
        def _():
            acc_ref[...] = jnp.zeros(acc_ref.shape, F32)
        acc_ref[...] += _dot_tn(a_ref[...], b_ref[...])

        @pl.when(k == nk - 1)
        def _():
            o_ref[...] = acc_ref[...].astype(out_dtype)

    return pl.pallas_call(
        body, name=name, grid=(n // bn, nk),
        out_shape=jax.ShapeDtypeStruct((n // bn, m, bn), out_dtype),
        in_specs=[pl.BlockSpec((bt, m), a_map), pl.BlockSpec((bt, bn), lambda j, k: (k, j))]
        + [pl.BlockSpec(memory_space=pl.ANY)] * len(extra),
        out_specs=pl.BlockSpec((None, m, bn), lambda j, k: (j, 0, 0)),
        scratch_shapes=[pltpu.VMEM((m, bn), F32)],
        compiler_params=_cparams(("parallel", "arbitrary"), VMEM_BIG),
    )(a, b, *extra)


def _ssm_bgrad(d_bre, d_bim, bt_re, bt_im, rows_in, fold, tile_b):
    gh, cb = d_bre.shape
    nb = SSM_SPLIT
    rb = gh // nb
    gp = nb * cb
    p = fold.shape[1]

    def body(dr_ref, di_ref, br_ref, bi_ref, rin_ref, f_ref, tb_ref, dbr_ref, dbi_ref, rout_ref):
        row = lax.broadcasted_iota(jnp.int32, (rb, cb), 0)
        col = lax.broadcasted_iota(jnp.int32, (rb, cb), 1)
        mask = (row >> 4) == (col >> 6)
        gr = jnp.where(mask, dr_ref[...], 0.0)
        gi = jnp.where(mask, di_ref[...], 0.0)
        cr, ci = rin_ref[0:1, :], rin_ref[1:2, :]
        dbr_ref[...] = _split3_dot(cr * gr + ci * gi, f_ref[...])
        dbi_ref[...] = _split3_dot(cr * gi - ci * gr, f_ref[...])
        br = _split3_dot(br_ref[...], tb_ref[...])
        bi = _split3_dot(bi_ref[...], tb_ref[...])
        rout_ref[...] = jnp.zeros(rout_ref.shape, F32)
        rout_ref[0:1, :] = _colsum(br * gr + bi * gi)
        rout_ref[1:2, :] = _colsum(br * gi - bi * gr)

    dspec = pl.BlockSpec((rb, cb), lambda j: (j, 0))
    rspec = pl.BlockSpec((SUBLANES, cb), lambda j: (0, j))
    ospec = pl.BlockSpec((rb, p), lambda j: (j, 0))
    return pl.pallas_call(
        body, name="ssm_bgrad", grid=(nb,),
        out_shape=(jax.ShapeDtypeStruct((gh, p), F32), jax.ShapeDtypeStruct((gh, p), F32),
                   jax.ShapeDtypeStruct((SUBLANES, gp), F32)),
        in_specs=[dspec, dspec, ospec, ospec, rspec, _full(fold.shape), _full(tile_b.shape)],
        out_specs=(ospec, ospec, rspec),
        compiler_params=_cparams(("parallel",)),
    )(d_bre, d_bim, bt_re, bt_im, rows_in, fold, tile_b)


def _ssm_cgrad(d_cre, d_cim, fold):
    gp, cb = d_cre.shape
    nb = SSM_SPLIT
    rb = gp // nb
    h = fold.shape[1]

    def body(dr_ref, di_ref, f_ref, cr_ref, ci_ref):
        row = lax.broadcasted_iota(jnp.int32, (rb, cb), 0)
        col = lax.broadcasted_iota(jnp.int32, (rb, cb), 1)
        mask = (row >> 6) == (col >> 4)
        cr_ref[...] = _split3_dot(jnp.where(mask, dr_ref[...], 0.0), f_ref[...])
        ci_ref[...] = -_split3_dot(jnp.where(mask, di_ref[...], 0.0), f_ref[...])

    cspec = pl.BlockSpec((rb, cb), lambda j: (j, 0))
    ospec = pl.BlockSpec((rb, h), lambda j: (j, 0))
    return pl.pallas_call(
        body, name="ssm_cgrad", grid=(nb,),
        out_shape=(jax.ShapeDtypeStruct((gp, h), F32),) * 2,
        in_specs=[cspec, cspec, _full(fold.shape)], out_specs=(ospec, ospec),
        compiler_params=_cparams(("parallel",)),
    )(d_cre, d_cim, fold)


def _ssm_lamgrad(lam_re, lam_im, log_step, abar_re, abar_im, coef_re, coef_im, gc_re, gc_im, ga_re, ga_im):
    g, p = lam_re.shape

    def body(lr_ref, li_ref, ls_ref, ar_ref, ai_ref, cr_ref, ci_ref, gcr_ref, gci_ref, gar_ref, gai_ref,
             dlr_ref, dli_ref, dls_ref):
        lam_raw = lr_ref[...]
        lr = jnp.minimum(lam_raw, LAMBDA_RE_MAX)
        li = li_ref[...]
        st = jnp.exp(ls_ref[...])
        den = lr * lr + li * li
        gcr, gci = gcr_ref[...], gci_ref[...]
        gab_r = gar_ref[...] + (lr * gcr - li * gci) / den
        gab_i = gai_ref[...] + (lr * gci + li * gcr) / den
        cr, ci = cr_ref[...], ci_ref[...]
        wr = -(cr * lr + ci * li) / den
        wi = -(ci * lr - cr * li) / den
        gl_r = wr * gcr + wi * gci
        gl_i = wr * gci - wi * gcr
        ar, ai = ar_ref[...], ai_ref[...]
        gw_r = ar * gab_r + ai * gab_i
        gw_i = ar * gab_i - ai * gab_r
        gl_r = gl_r + st * gw_r
        gl_i = gl_i + st * gw_i
        pass_through = jnp.where(lam_raw < LAMBDA_RE_MAX, 1.0, jnp.where(lam_raw == LAMBDA_RE_MAX, 0.5, 0.0))
        dlr_ref[...] = gl_r * pass_through
        dli_ref[...] = gl_i
        dls_ref[...] = st * jnp.sum(lr * gw_r + li * gw_i, axis=1, keepdims=True)

    sds = jax.ShapeDtypeStruct((g, p), F32)
    return pl.pallas_call(body, name="ssm_lamgrad", out_shape=(sds, sds, jax.ShapeDtypeStruct((g, 1), F32)))(
        lam_re, lam_im, log_step, abar_re, abar_im, coef_re, coef_im, gc_re, gc_im, ga_re, ga_im)


def _row_block(r, most=256):
    for rb in range(min(r, most), BF16_ROWS - 1, -1):
        if r % rb == 0 and rb % BF16_ROWS == 0:
            return rb
    return r


def _adamw_math(w, g, m, v):
    m = ADAM_B1 * m + (1.0 - ADAM_B1) * g
    v = ADAM_B2 * v + (1.0 - ADAM_B2) * (g * g)
    m_hat = m / (1.0 - ADAM_B1 ** ADAM_STEP)
    v_hat = v / (1.0 - ADAM_B2 ** ADAM_STEP)
    delta = -ADAM_LR * (m_hat / (jnp.sqrt(v_hat) + ADAM_EPS) + ADAM_WD * w)
    return delta, m, v


def _adamw_big(p_mine, p_sib, w, m, v, name):
    r, c = w.shape
    rb = _row_block(r)

    def body(a_ref, b_ref, w_ref, m_ref, v_ref, g_ref, d_ref, mo_ref, vo_ref):
        g = a_ref[...].astype(F32) + b_ref[...].astype(F32)
        g_ref[...] = g
        d_ref[...], mo_ref[...], vo_ref[...] = _adamw_math(w_ref[...], g, m_ref[...], v_ref[...])

    spec = pl.BlockSpec((rb, c), lambda i: (i, 0))
    sds = jax.ShapeDtypeStruct((r, c), F32)
    return pl.pallas_call(
        body, name=name, grid=(r // rb,), out_shape=(sds,) * 4, in_specs=[spec] * 5, out_specs=(spec,) * 4,
        compiler_params=_cparams(("parallel",), VMEM_KEEP_OPERANDS_IN_HBM),
    )(p_mine, p_sib, w, m, v)


def _sum_blocks(stack, name):
    n, r, c = stack.shape
    rb = _row_block(r)

    def body(s_ref, o_ref):
        acc = s_ref[0].astype(F32)
        for k in range(1, n):
            acc = acc + s_ref[k].astype(F32)
        o_ref[...] = acc

    return pl.pallas_call(
        body, name=name, grid=(r // rb,), out_shape=jax.ShapeDtypeStruct((r, c), F32),
        in_specs=[pl.BlockSpec((n, rb, c), lambda i: (0, i, 0))], out_specs=pl.BlockSpec((rb, c), lambda i: (i, 0)),
        compiler_params=_cparams(("parallel",), VMEM_KEEP_OPERANDS_IN_HBM),
    )(stack)


def _sum_landed(landed, own, chip, name):
    n, r, c = landed.shape
    rb = _row_block(r)

    def body(chip_ref, own_ref, l1_ref, l2_ref, l3_ref, o_ref):
        acc = own_ref[0].astype(F32)
        for ref in (l1_ref, l2_ref, l3_ref):
            acc = acc + ref[0].astype(F32)
        o_ref[...] = acc.astype(BF16)

    def slot(k):
        return pl.BlockSpec((1, rb, c), lambda i, ch: ((ch[0] + k) % n, i, 0))

    return pl.pallas_call(
        body, name=name, out_shape=jax.ShapeDtypeStruct((r, c), BF16),
        grid_spec=pltpu.PrefetchScalarGridSpec(
            num_scalar_prefetch=1, grid=(r // rb,), in_specs=[slot(0), slot(1), slot(2), slot(3)],
            out_specs=pl.BlockSpec((rb, c), lambda i, ch: (i, 0))),
        compiler_params=_cparams(("parallel",), VMEM_KEEP_OPERANDS_IN_HBM),
    )(jnp.reshape(chip, (1,)).astype(jnp.int32), own, landed, landed, landed)


def _add2(a, b):
    def body(a_ref, b_ref, o_ref):
        o_ref[...] = a_ref[...] + b_ref[...]

    return pl.pallas_call(body, name="add_small", out_shape=jax.ShapeDtypeStruct(a.shape, F32))(a, b)


def _adamw_ada(c_all, dmod_cols, w, m, v):
    d, n = w.shape
    bn = 512

    def body(c_ref, dm_ref, w_ref, m_ref, v_ref, g_ref, d_ref, mo_ref, vo_ref):
        cc = c_ref[...]
        g = _dot_tn(cc * _sigmoid(cc), dm_ref[...])
        g_ref[...] = g
        d_ref[...], mo_ref[...], vo_ref[...] = _adamw_math(w_ref[...], g, m_ref[...], v_ref[...])

    spec = pl.BlockSpec((d, bn), lambda j: (0, j))
    sds = jax.ShapeDtypeStruct((d, n), F32)
    return pl.pallas_call(
        body, name="adamw_ada", grid=(n // bn,), out_shape=(sds,) * 4,
        in_specs=[_full((N_DEV, d)), pl.BlockSpec((N_DEV, bn), lambda j: (0, j)), spec, spec, spec],
        out_specs=(spec,) * 4, compiler_params=_cparams(("parallel",), VMEM_KEEP_OPERANDS_IN_HBM),
    )(c_all, dmod_cols, w, m, v)


def _adamw_small(items):
    n = len(items)

    def body(*refs):
        ins, outs = refs[:4 * n], refs[4 * n:]
        for k in range(n):
            w_ref, g_ref, m_ref, v_ref = ins[4 * k:4 * k + 4]
            outs[3 * k][...], outs[3 * k + 1][...], outs[3 * k + 2][...] = _adamw_math(
                w_ref[...], g_ref[...], m_ref[...], v_ref[...])

    flat = [a for it in items for a in it]
    out_shape = tuple(jax.ShapeDtypeStruct(it[0].shape, F32) for it in items for _ in range(3))
    res = pl.pallas_call(body, name="adamw_small", out_shape=out_shape,
                         compiler_params=_cparams(vmem=VMEM_KEEP_OPERANDS_IN_HBM))(*flat)
    return [tuple(res[3 * k:3 * k + 3]) for k in range(n)]


def _group_mean_matrix(n, group):
    idx = np.arange(n) // group
    return (idx[:, None] == idx[None, :]).astype(np.float32) / group


def _fold_matrix(n, period):
    return (np.arange(n)[:, None] % period == np.arange(period)[None, :]).astype(np.float32)


def _rows8(*rows):
    c = rows[0].shape[-1]
    pad = jnp.zeros((SUBLANES - len(rows), c), F32)
    return jnp.concatenate([r.reshape(1, c) for r in rows] + [pad], axis=0)


def _to_rows(a, width):
    flat = a.reshape(-1)
    n = -(-flat.shape[0] // width)
    flat = jnp.pad(flat, (0, n * width - flat.shape[0]))
    return flat.reshape(n, width)


def kernel(x, c, w_ada, b_ada, g_pre_mix, g_post_mix, w_in, ssm_lam_re, ssm_lam_im, ssm_log_step, ssm_b_re, ssm_b_im, ssm_c_re, ssm_c_im, ssm_d, glu_w, glu_b, g_out_ssm, conv_w, g_out_conv, w_out, g_pre_ffn, g_post_ffn, w_up, ffn_conv_w, w_down, loss_target, m_w_ada, m_b_ada, m_g_pre_mix, m_g_post_mix, m_w_in, m_ssm_lam_re, m_ssm_lam_im, m_ssm_log_step, m_ssm_b_re, m_ssm_b_im, m_ssm_c_re, m_ssm_c_im, m_ssm_d, m_glu_w, m_glu_b, m_g_out_ssm, m_conv_w, m_g_out_conv, m_w_out, m_g_pre_ffn, m_g_post_ffn, m_w_up, m_ffn_conv_w, m_w_down, v_w_ada, v_b_ada, v_g_pre_mix, v_g_post_mix, v_w_in, v_ssm_lam_re, v_ssm_lam_im, v_ssm_log_step, v_ssm_b_re, v_ssm_b_im, v_ssm_c_re, v_ssm_c_im, v_ssm_d, v_glu_w, v_glu_b, v_g_out_ssm, v_conv_w, v_g_out_conv, v_w_out, v_g_pre_ffn, v_g_post_ffn, v_w_up, v_ffn_conv_w, v_w_down):
    xs = x[0]
    tgt = loss_target[0]
    t, d = xs.shape
    xi, yi, ci = lax.axis_index("x"), lax.axis_index("y"), lax.axis_index("c")
    chip = 2 * xi + yi
    dev = 2 * chip + ci

    n_groups, n_state = ssm_lam_re.shape[1:]
    n_gch = ssm_b_re.shape[3]
    d_ssm = n_groups * n_gch
    gp = n_groups * n_state
    n_ada = w_ada.shape[2]
    d_ff = w_down.shape[1] * N_CHIPS
    n_upc = w_up.shape[2]

    w_names = ("w_in", "glu_w", "w_out", "w_up", "w_down")
    c_gath = _allgather8(jnp.broadcast_to(c, (SUBLANES, d)), "gather_c")
    c_all = c_gath.reshape(N_DEV, SUBLANES, d)[:, 0, :]

    def pad8(a):
        return jnp.concatenate([a, jnp.zeros((SUBLANES - a.shape[0], a.shape[1]), a.dtype)], axis=0)

    def start(name, arrs, after):
        return _chips_start(name, True, [], [_landing(a, chip) for a in arrs], after)

    w_names = ("w_in", "mod", "conv_w", "ffn_conv_w", "glu_w", "w_out", "w_up", "w_down")
    first = start("weights_start_in", [w_in[0].astype(BF16)], c_gath)
    b_sh = lax.dynamic_slice(b_ada, (0, chip * n_ada), (1, n_ada))
    mod_sh = _mod_shard(c_all + first[4][0:1, 0:1], w_ada[0], b_sh)
    second = start("weights_start_rest", [mod_sh, pad8(conv_w[0]), pad8(ffn_conv_w[0])]
                   + [w[0].astype(BF16) for w in (glu_w, w_out, w_up, w_down)], None)
    w_send, w_recv, w_land = [list(first[k]) + list(second[k]) for k in (0, 1, 3)]
    w_token = second[4]

    def weights(names, after):
        ks = [w_names.index(nm) for nm in names]
        return _chips_wait("weights_wait_" + names[-1], True, [w_send[k] for k in ks], [w_recv[k] for k in ks],
                           [], [w_land[k] for k in ks], after)[1]

    lam_re, lam_im = ssm_lam_re[0], ssm_lam_im[0]
    log_step = ssm_log_step[0].reshape(n_groups, 1) + w_token[0:1, 0:1]
    abar_re, abar_im, coef_re, coef_im = _ssm_prep(lam_re, lam_im, log_step)
    a_rows = _rows8(abar_re.reshape(1, gp), abar_im.reshape(1, gp))
    coef_rows = _rows8(coef_re.reshape(1, gp), coef_im.reshape(1, gp))
    bt_re = ssm_b_re[0].transpose(0, 2, 1).reshape(d_ssm, n_state)
    bt_im = ssm_b_im[0].transpose(0, 2, 1).reshape(d_ssm, n_state)
    ct_re = ssm_c_re[0].transpose(0, 2, 1).reshape(gp, n_gch)
    ct_im = ssm_c_im[0].transpose(0, 2, 1).reshape(gp, n_gch)
    tile_b = jnp.asarray(np.tile(np.eye(n_state), (1, n_groups // SSM_SPLIT)), BF16)
    tile_c = jnp.asarray(np.tile(np.eye(n_gch), (1, n_groups)), BF16)
    bblk_re, bblk_im, cblk_re, cblk_im = _ssm_blocks(bt_re, bt_im, ct_re, ct_im, coef_rows, tile_b, tile_c)

    h16 = jnp.asarray(_group_mean_matrix(d_ssm, n_gch), BF16)
    h64 = jnp.asarray(_group_mean_matrix(d_ssm, CONV_HEAD_DIM), BF16)

    g_mod, g_cw, g_fw, w_in_st = weights(("mod", "conv_w", "ffn_conv_w", "w_in"), bblk_re)
    mod_all = g_mod.transpose(1, 0, 2).reshape(N_DEV, N_CHIPS * n_ada)
    mod = lax.dynamic_slice(mod_all, (dev, 0), (1, N_CHIPS * n_ada))
    sh1, sc1, gt1, sh2, sc2, gt2 = [mod[:, k * d:(k + 1) * d] for k in range(6)]
    convw_full = pad8(g_cw[:, :3, :].transpose(1, 0, 2).reshape(3, d_ssm))
    fw_full = pad8(g_fw[:, :3, :].transpose(1, 0, 2).reshape(3, N_CHIPS * n_upc))

    v512 = _rows8(ssm_d, glu_b, g_out_ssm, g_out_conv)
    vec1 =_rows8(g_pre_mix, 1.0 + sc1, sh1)
    vd1 = _rows8(g_post_mix, gt1)
    vec2 = _rows8(g_pre_ffn, 1.0 + sc2, sh2)
    vd2 = _rows8(g_post_ffn, gt2)

    proj, bu_re, bu_im, h1b = _mix_in(xs, vec1, w_in_st, bblk_re, bblk_im)
    s_re, s_im = _scan_fwd(a_rows, bu_re, bu_im)
    g_glu, g_wout = weights(("glu_w", "w_out"), s_re)
    glu_full = g_glu.reshape(d_ssm, d_ssm)
    w_out_full = g_wout.reshape(2 * d_ssm, d)
    y1, o_mix, x2 = _mix_out(xs, proj, s_re, s_im, cblk_re, cblk_im, v512, convw_full, glu_full, h16, h64,
                             w_out_full, vd1)
    (w_up_st,) = weights(("w_up",), x2)
    up, h2b = _ffn_up(x2, vec2, w_up_st)
    (g_wdown,) = weights(("w_down",), up)
    w_down_full = g_wdown.reshape(d_ff, d)
    actb, ddnb, dout, dhid, vp_dn, loss_blk = _ffn_down(up, fw_full, w_down_full, w_down_full.T, x2, tgt, vd2)

    gw_down = _matmul_tn(actb, ddnb, d_ff, d, BF16, "dw_down", bt=1024).reshape(N_CHIPS, d_ff // N_CHIPS, d)
    dx2, dupb, vp_up, df_rows = _ffn_up_bwd(dhid, up, fw_full, x2, dout, vec2, w_up_st)
    gw_up = _matmul_tn(h2b, dupb, d, n_upc, BF16, "dw_up", bt=2048)
    ga_send, ga_recv, ga_src, ga_land, ga_token = _chips_start(
        "grads_start_ffn", False, [gw_down, gw_up], [lax.empty(g.shape, g.dtype) for g in (gw_down, gw_up)])
    (dob, ycatb, zb, dqb, dy1, g_re, g_im, dcc, dbg, vp_mo, vp5, d_cre, d_cim) = _mix_out_bwd(
        dx2, o_mix, y1, proj, s_re, s_im, cblk_re, cblk_im, v512, convw_full, glu_full, h16, h64, w_out_full,
        vd1 + ga_token[0:1, 0:1])
    gw_out = _matmul_tn(ycatb, dob, 2 * d_ssm, d, BF16, "dw_out", bt=2048)
    gw_out = gw_out.reshape(N_CHIPS, 2 * d_ssm // N_CHIPS, d)
    gw_glu = _matmul_tn(zb, dqb, d_ssm, d_ssm, BF16, "dw_glu", bt=2048).reshape(N_CHIPS, d_ssm // N_CHIPS, d_ssm)
    gb_send, gb_recv, gb_src, gb_land, gb_token = _chips_start(
        "grads_start_mix", False, [gw_out, gw_glu], [lax.empty(g.shape, g.dtype) for g in (gw_out, gw_glu)])
    gt_re, gt_im, ga_re8, ga_im8 = _scan_bwd(a_rows + gb_token[0:1, 0:1], g_re, g_im, s_re, s_im)
    grad_x, dprojb, vp_mi, d_bre, d_bim = _mix_in_bwd(gt_re, gt_im, bblk_re, bblk_im, dy1, dcc, dbg, proj, xs, dx2,
                                                      vec1, v512, convw_full, w_in_st)
    ssm_u, ssm_s = d_ssm // SSM_SPLIT, gp // SSM_SPLIT

    fold_b = jnp.asarray(_fold_matrix(ssm_s, n_state), BF16)
    fold_c = jnp.asarray(_fold_matrix(ssm_u, n_gch), BF16)
    db_re_f, db_im_f, gc_rows = _ssm_bgrad(d_bre, d_bim, bt_re, bt_im, coef_rows, fold_b, tile_b)
    dc_re_f, dc_im_f = _ssm_cgrad(d_cre, d_cim, fold_c)
    ga_sum = _ga_rowsum(ga_re8, ga_im8)
    g_lam_re, g_lam_im, g_log_step = _ssm_lamgrad(
        lam_re, lam_im, log_step, abar_re, abar_im, coef_re, coef_im,
        gc_rows[0].reshape(n_groups, n_state), gc_rows[1].reshape(n_groups, n_state),
        ga_sum[0].reshape(n_groups, n_state), ga_sum[1].reshape(n_groups, n_state))
    g_b_re = db_re_f.reshape(n_groups, n_gch, n_state).transpose(0, 2, 1)
    g_b_im = db_im_f.reshape(n_groups, n_gch, n_state).transpose(0, 2, 1)
    g_c_re = dc_re_f.reshape(n_groups, n_state, n_gch).transpose(0, 2, 1)
    g_c_im = dc_im_f.reshape(n_groups, n_state, n_gch).transpose(0, 2, 1)

    dmod = jnp.concatenate([vp_mi[0:1], vp_mi[1:2], vp_mo[0:1], vp_up[0:1], vp_up[1:2], vp_dn[0:1]], axis=1)
    small = [
        ("g_pre_mix", vp_mi[2:3]), ("g_post_mix", vp_mo[1:2]), ("g_pre_ffn", vp_up[2:3]), ("g_post_ffn", vp_dn[1:2]),
        ("ssm_lam_re", g_lam_re), ("ssm_lam_im", g_lam_im), ("ssm_log_step", g_log_step),
        ("ssm_b_re", g_b_re), ("ssm_b_im", g_b_im), ("ssm_c_re", g_c_re), ("ssm_c_im", g_c_im),
        ("ssm_d", vp5[3:4]), ("glu_b", vp5[2:3]), ("g_out_ssm", vp5[0:1]), ("g_out_conv", vp5[1:2]),
        ("conv_w", vp5[4:7]), ("ffn_conv_w", df_rows[0:3]), ("loss", loss_blk[0:1, 0:1]),
    ]
    packed, offsets, row = [], {}, 0
    for name, a in small:
        r = _to_rows(a, d)
        offsets[name] = (row, a.shape)
        packed.append(r)
        row += r.shape[0]
    n_small = -(-row // SUBLANES) * SUBLANES
    packed.append(jnp.zeros((n_small - row, d), F32))
    packed.append(pad8(dmod.reshape(6, d)))
    pack = jnp.concatenate(packed, axis=0)
    sm_send, sm_recv, _, sm_land, sm_token = _chips_start("small_start", True, [], [_landing(pack, chip)])

    gw_in = _matmul_tn(h1b, dprojb, d, w_in.shape[2], BF16, "dw_in", bt=2048, after=sm_token)
    gc_send, gc_recv, gc_src, gc_land, gc_token = _chips_start(
        "grads_start_in", False, [gw_in], [lax.empty(gw_in.shape, gw_in.dtype)])

    def partials(names, own, landed):
        return [_sum_landed(l, o, chip, "sum_" + nm) for l, o, nm in zip(landed, own, names)]

    def update(names, mine, theirs):
        done = {}
        for nm, pm, ps in zip(names, mine, theirs):
            w_, m_, v_ = big_params[nm]
            done[nm] = _adamw_big(pm, ps, w_[0], m_[0], v_[0], "adamw_" + nm)
        return done

    big_params = {"w_down": (w_down, m_w_down, v_w_down), "w_up": (w_up, m_w_up, v_w_up),
                  "w_out": (w_out, m_w_out, v_w_out), "glu_w": (glu_w, m_glu_w, v_glu_w),
                  "w_in": (w_in, m_w_in, v_w_in)}
    ffn_names, mix_names = ("w_down", "w_up"), ("w_out", "glu_w", "w_in")
    p_ffn = partials(ffn_names, *_chips_wait("grads_wait_ffn", False, ga_send, ga_recv, ga_src, ga_land, gc_token))
    sa_send, sa_recv, sa_src, sa_land, sa_token = _sibling_start("swap_start_ffn", p_ffn)

    (sm_landed,) = _chips_wait("small_wait", True, sm_send, sm_recv, [], sm_land, sa_token)[1]
    sm_part = _sum_blocks(sm_landed, "sum_small")
    dmod_mine = sm_landed[:, n_small:n_small + SUBLANES, :]
    ss_send, ss_recv, ss_src, ss_land, ss_token = _sibling_start("swap_start_small", [sm_part, dmod_mine])
    p_ffn, t_ffn = _sibling_wait("swap_wait_ffn", sa_send, sa_recv, sa_src, sa_land, ss_token)
    big = update(ffn_names, p_ffn, t_ffn)
    (sm_part, dmod_mine), (sm_sib, dmod_sib) = _sibling_wait("swap_wait_small", ss_send, ss_recv, ss_src, ss_land,
                                                              big["w_up"][0])
    sums = _add2(sm_part, sm_sib)
    dmod_by_core = jnp.stack([dmod_mine, dmod_sib], axis=1)
    dmod_by_core = jnp.where(ci == 0, dmod_by_core, dmod_by_core[:, ::-1])
    dmod_all = dmod_by_core[:, :, :6, :].reshape(N_DEV, 6 * d)
    g_b_ada = sums[n_small:n_small + 6].reshape(1, 6 * d)

    def unpack(name):
        r0, shape = offsets[name]
        size = math.prod(shape)
        nrow = -(-size // d)
        return sums[r0:r0 + nrow].reshape(-1)[:size].reshape(shape)

    p_mix = partials(mix_names, *_chips_wait(
        "grads_wait_mix", False, list(gb_send) + list(gc_send), list(gb_recv) + list(gc_recv),
        list(gb_src) + list(gc_src), list(gb_land) + list(gc_land), sums))
    sb_send, sb_recv, sb_src, sb_land, sb_token = _sibling_start("swap_start_mix", p_mix)

    dmod_cols = lax.dynamic_slice(dmod_all, (0, chip * n_ada), (N_DEV, n_ada)) + sb_token[0:1, 0:1]
    ada = _adamw_ada(c_all, dmod_cols, w_ada[0], m_w_ada[0], v_w_ada[0])
    p_mix, t_mix = _sibling_wait("swap_wait_mix", sb_send, sb_recv, sb_src, sb_land, ada[0])
    big.update(update(mix_names, p_mix, t_mix))

    g_small = {name: unpack(name) for name, _ in small}
    g_small["b_ada"] = g_b_ada
    g_small["conv_w"] = lax.dynamic_slice(g_small["conv_w"], (0, chip * conv_w.shape[2]), (3, conv_w.shape[2]))
    g_small["ffn_conv_w"] = lax.dynamic_slice(g_small["ffn_conv_w"], (0, chip * n_upc), (3, n_upc))
    g_small["ssm_log_step"] = g_small["ssm_log_step"].reshape(1, n_groups)
    small_params = {
        "b_ada": (b_ada, m_b_ada, v_b_ada), "g_pre_mix": (g_pre_mix, m_g_pre_mix, v_g_pre_mix),
        "g_post_mix": (g_post_mix, m_g_post_mix, v_g_post_mix), "ssm_lam_re": (ssm_lam_re, m_ssm_lam_re, v_ssm_lam_re),
        "ssm_lam_im": (ssm_lam_im, m_ssm_lam_im, v_ssm_lam_im),
        "ssm_log_step": (ssm_log_step, m_ssm_log_step, v_ssm_log_step),
        "ssm_b_re": (ssm_b_re, m_ssm_b_re, v_ssm_b_re), "ssm_b_im": (ssm_b_im, m_ssm_b_im, v_ssm_b_im),
        "ssm_c_re": (ssm_c_re, m_ssm_c_re, v_ssm_c_re), "ssm_c_im": (ssm_c_im, m_ssm_c_im, v_ssm_c_im),
        "ssm_d": (ssm_d, m_ssm_d, v_ssm_d), "glu_b": (glu_b, m_glu_b, v_glu_b),
        "g_out_ssm": (g_out_ssm, m_g_out_ssm, v_g_out_ssm), "conv_w": (conv_w, m_conv_w, v_conv_w),
        "g_out_conv": (g_out_conv, m_g_out_conv, v_g_out_conv), "g_pre_ffn": (g_pre_ffn, m_g_pre_ffn, v_g_pre_ffn),
        "g_post_ffn": (g_post_ffn, m_g_post_ffn, v_g_post_ffn),
        "ffn_conv_w": (ffn_conv_w, m_ffn_conv_w, v_ffn_conv_w),
    }

    def natural(a):
        return a[0] if a.ndim > 2 else a

    names = list(small_params)
    items = []
    for nm in names:
        w_, m_, v_ = small_params[nm]
        items.append((natural(w_), g_small[nm].reshape(natural(w_).shape), natural(m_), natural(v_)))
    upd = _adamw_small(items)
    small_out = {}
    for nm, (dl, mo, vo) in zip(names, upd):
        shp = small_params[nm][0].shape
        small_out[nm] = (g_small[nm].reshape(shp), dl.reshape(shp), mo.reshape(shp), vo.reshape(shp))

    loss = g_small["loss"][0, 0]

    order = ["w_ada", "b_ada", "g_pre_mix", "g_post_mix", "w_in", "ssm_lam_re", "ssm_lam_im", "ssm_log_step",
             "ssm_b_re", "ssm_b_im", "ssm_c_re", "ssm_c_im", "ssm_d", "glu_w", "glu_b", "g_out_ssm", "conv_w",
             "g_out_conv", "w_out", "g_pre_ffn", "g_post_ffn", "w_up", "ffn_conv_w", "w_down"]
    results = {"w_ada": tuple(a[None] for a in ada)}
    for nm in big:
        results[nm] = tuple(a[None] for a in big[nm])
    results.update(small_out)
    outs = [loss, grad_x[None]]
    for k in range(4):
        outs += [results[nm][k] for nm in order]
    return tuple(outs)


def _ga_rowsum(ga_re8, ga_im8):
    n = ga_re8.shape[1]

    def body(r_ref, i_ref, o_ref):
        o_ref[...] = jnp.zeros(o_ref.shape, F32)
        o_ref[0:1, :] = _colsum(r_ref[...])
        o_ref[1:2, :] = _colsum(i_ref[...])

    return pl.pallas_call(body, name="ga_rowsum", out_shape=jax.ShapeDtypeStruct((SUBLANES, n), F32))(ga_re8, ga_im8)
```

```python
import functools
import math

import jax
import jax.numpy as jnp
import numpy as np
from jax import lax
from jax.experimental import pallas as pl
from jax.experimental.pallas import tpu as pltpu

F32 = jnp.float32
BF16 = jnp.bfloat16
MESH = pl.DeviceIdType.MESH

EPS = 1e-6
LAMBDA_RE_MAX = -1e-4
ADAM_LR = 0.001
ADAM_B1 = 0.9
ADAM_B2 = 0.999
ADAM_EPS = 1e-08
ADAM_WD = 0.01
ADAM_STEP = 10

SUBLANES = 8
BF16_ROWS = 16
N_CHIPS = 4
N_DEV = 8
CONV_HEAD_DIM = 64
VMEM_BIG = 56 * 1024 * 1024
VMEM_MID = 40 * 1024 * 1024
VMEM_KEEP_OPERANDS_IN_HBM = 62 * 1024 * 1024

TB_MIX = 256
TB_FFN = 256
TB_FFN_UP = 512
TB_SCAN = 2048
W_SCAN = 256
SSM_SPLIT = 4
CW_FFN = 256
SCAN_UNROLL = 4
TB_TN = 512


def _cparams(sem=None, vmem=None):
    kw = {}
    if sem is not None:
        kw["dimension_semantics"] = sem
    if vmem is not None:
        kw["vmem_limit_bytes"] = vmem
    return pltpu.CompilerParams(**kw)


def _blk(t, pref):
    return pref if t % pref == 0 else t


def _dot(a, b):
    return jnp.dot(a.astype(BF16), b.astype(BF16), preferred_element_type=F32)


def _dot_nt(a, b):
    return lax.dot_general(a.astype(BF16), b.astype(BF16), (((1,), (1,)), ((), ())),
                           preferred_element_type=F32)


def _dot_tn(a, b):
    return lax.dot_general(a.astype(BF16), b.astype(BF16), (((0,), (0,)), ((), ())),
                           preferred_element_type=F32)


def _sigmoid(x):
    return 0.5 * jnp.tanh(0.5 * x) + 0.5


_GELU_K = math.sqrt(2.0 / math.pi)
_GELU_C = 0.044715


def _gelu(x):
    th = jnp.tanh(_GELU_K * (x + _GELU_C * x * x * x))
    return x * (0.5 * (1.0 + th))


def _gelu_and_grad(x):
    x2 = x * x
    th = jnp.tanh(_GELU_K * (x + _GELU_C * x2 * x))
    half = 0.5 * (1.0 + th)
    return x * half, half + 0.5 * x * (1.0 - th * th) * _GELU_K * (1.0 + 3.0 * _GELU_C * x2)


def _rowmean(x):
    return jnp.mean(x, axis=-1, keepdims=True)


def _colsum(x):
    return jnp.sum(x, axis=0, keepdims=True)


def _split_dot(x, m):
    hi = x.astype(BF16)
    lo = (x - hi.astype(F32)).astype(BF16)
    return (jnp.dot(hi, m, preferred_element_type=F32) + jnp.dot(lo, m, preferred_element_type=F32))


def _split3_dot(x, m):
    hi = x.astype(BF16)
    r1 = x - hi.astype(F32)
    mid = r1.astype(BF16)
    lo = (r1 - mid.astype(F32)).astype(BF16)
    return (jnp.dot(hi, m, preferred_element_type=F32) + jnp.dot(mid, m, preferred_element_type=F32)
            + jnp.dot(lo, m, preferred_element_type=F32))


def _shift_down(x, halo, k):
    r = pltpu.roll(x, k, 0)
    row = lax.broadcasted_iota(jnp.int32, x.shape, 0)
    last = halo.shape[0]
    for j in range(k):
        r = jnp.where(row == j, halo[last - k + j:last - k + j + 1, :], r)
    return r


def _shift_up(x, halo, k):
    n = x.shape[0]
    r = pltpu.roll(x, n - k, 0)
    row = lax.broadcasted_iota(jnp.int32, x.shape, 0)
    for j in range(k):
        r = jnp.where(row == n - k + j, halo[j:j + 1, :], r)
    return r


def _acc_rows(ref, first, rows):
    @pl.when(first)
    def _():
        ref[...] = jnp.zeros(ref.shape, ref.dtype)
    for j, r in enumerate(rows):
        ref[j:j + 1, :] += r


def _rows(tb, c, col=0):
    return pl.BlockSpec((tb, c), lambda i, col=col: (i, col))


def _full(shape):
    nd = len(shape)
    return pl.BlockSpec(shape, lambda i, nd=nd: (0,) * nd)


def _resident(shape):
    nd = len(shape)
    return pl.BlockSpec(shape, lambda i, nd=nd: (0,) * nd, pipeline_mode=pl.Buffered(1))


def _halo_prev(tb, c, col=0, rows=SUBLANES):
    per = tb // rows
    return pl.BlockSpec((rows, c), lambda i, col=col: (jnp.maximum(i * per - 1, 0), col))


def _halo_next(tb, c, t, col=0, rows=SUBLANES):
    per = tb // rows
    last = t // rows - 1
    return pl.BlockSpec((rows, c), lambda i, col=col: (jnp.minimum((i + 1) * per, last), col))


def _mesh_pos():
    return lax.axis_index("x"), lax.axis_index("y"), lax.axis_index("c")


def _allgather8(x_pad, name):
    m_per, n = x_pad.shape

    def body(x_ref, out_ref, send_sems, recv_sems, local_sem):
        x, y, c = _mesh_pos()
        me, sibling = (x, y, c), (x, y, 1 - c)
        chips = [(1 - x, y), (x, 1 - y), (1 - x, 1 - y)]

        def rows(px, py, pc):
            return out_ref.at[pl.ds((4 * px + 2 * py + pc) * m_per, m_per), :]

        def copy(k, block, to, src=None):
            return pltpu.make_async_remote_copy(
                src_ref=rows(*block) if src is None else src, dst_ref=rows(*block),
                send_sem=send_sems.at[k], recv_sem=recv_sems.at[k], device_id=to, device_id_type=MESH)

        mine = pltpu.make_async_copy(x_ref, rows(*me), local_sem)
        mine.start()
        first = [copy(0, me, sibling, src=x_ref)]
        first += [copy(1 + j, me, (*chip, c), src=x_ref) for j, chip in enumerate(chips)]
        for cp in first:
            cp.start()
        passed = [copy(4 + j, (*chip, c), sibling) for j, chip in enumerate(chips)]
        for j, chip in enumerate(chips):
            copy(1 + j, (*chip, c), me).wait_recv()
            passed[j].start()
        copy(0, sibling, me).wait_recv()
        for j, chip in enumerate(chips):
            copy(4 + j, (*chip, 1 - c), me).wait_recv()
        for cp in first + passed:
            cp.wait_send()
        mine.wait()

    return pl.pallas_call(
        body, name=name,
        out_shape=jax.ShapeDtypeStruct((N_DEV * m_per, n), F32),
        in_specs=[pl.BlockSpec(memory_space=pltpu.VMEM)],
        out_specs=pl.BlockSpec(memory_space=pltpu.VMEM),
        scratch_shapes=[pltpu.SemaphoreType.DMA((7,)), pltpu.SemaphoreType.DMA((7,)), pltpu.SemaphoreType.DMA],
    )(x_pad)


_HBM = pl.BlockSpec(memory_space=pltpu.HBM)
_SEM = pl.BlockSpec(memory_space=pltpu.SEMAPHORE)
_EFFECT = pltpu.SideEffectType.DATAFLOW_SIDE_EFFECTING


def _chip_copy(gather, src_ref, land_ref, send, recv, j, arrival):
    x, y, c = _mesh_pos()
    peer = [(1 - x, y), (x, 1 - y), (1 - x, 1 - y)][j]
    peer_chip = 2 * peer[0] + peer[1]
    my_chip = 2 * x + y
    return pltpu.make_async_remote_copy(
        src_ref=land_ref.at[my_chip] if gather else src_ref.at[peer_chip],
        dst_ref=land_ref.at[peer_chip if arrival else my_chip],
        send_sem=send.at[j], recv_sem=recv.at[j], device_id=(*peer, c), device_id_type=MESH)


def _chips_start(name, gather, srcs, lands, after=None):
    n, ns = len(lands), len(srcs)
    extra = [] if after is None else [after]

    def body(*refs):
        src_refs, land_refs = refs[:ns], refs[ns:ns + n]
        outs = refs[ns + n + len(extra):]
        sends, recvs, token = outs[:n], outs[n:2 * n], outs[-1]
        for k in range(n):
            for j in range(3):
                _chip_copy(gather, src_refs[k] if ns else None, land_refs[k], sends[k], recvs[k], j, False).start()
        token[...] = jnp.zeros(token.shape, F32)

    sem = pltpu.SemaphoreType.DMA((3,))
    thru = tuple(pltpu.HBM(a.shape, a.dtype) for a in list(srcs) + list(lands))
    res = pl.pallas_call(
        body, name=name,
        out_shape=(sem,) * (2 * n) + thru + (jax.ShapeDtypeStruct((SUBLANES, 128), F32),),
        in_specs=[_HBM] * (ns + n) + [pl.BlockSpec(memory_space=pl.ANY)] * len(extra),
        out_specs=(_SEM,) * (2 * n) + (_HBM,) * (ns + n) + (pl.BlockSpec(memory_space=pltpu.VMEM),),
        input_output_aliases={k: 2 * n + k for k in range(ns + n)},
        compiler_params=pltpu.CompilerParams(has_side_effects=_EFFECT),
    )(*[pltpu.with_memory_space_constraint(a, pltpu.HBM) for a in list(srcs) + list(lands)], *extra)
    return res[:n], res[n:2 * n], res[2 * n:2 * n + ns], res[2 * n + ns:2 * n + ns + n], res[-1]


def _chips_wait(name, gather, sends, recvs, srcs, lands, after):
    n, ns = len(lands), len(srcs)

    def body(*refs):
        src_refs, land_refs = refs[:ns], refs[ns:ns + n]
        sends_, recvs_ = refs[ns + n:ns + 2 * n], refs[ns + 2 * n:ns + 3 * n]
        for k in range(n):
            for j in range(3):
                cp = _chip_copy(gather, src_refs[k] if ns else None, land_refs[k], sends_[k], recvs_[k], j, True)
                cp.wait_send()
                cp.wait_recv()

    thru = tuple(pltpu.HBM(a.shape, a.dtype) for a in list(srcs) + list(lands))
    res = pl.pallas_call(
        body, name=name, out_shape=thru,
        in_specs=[_HBM] * (ns + n) + [_SEM] * (2 * n) + [pl.BlockSpec(memory_space=pl.ANY)],
        out_specs=(_HBM,) * (ns + n),
        input_output_aliases={k: k for k in range(ns + n)},
        compiler_params=pltpu.CompilerParams(has_side_effects=_EFFECT),
    )(*srcs, *lands, *sends, *recvs, after)
    return res[:ns], res[ns:]


def _sibling_copy(src_ref, land_ref, send, recv):
    x, y, c = _mesh_pos()
    return pltpu.make_async_remote_copy(src_ref=src_ref, dst_ref=land_ref, send_sem=send.at[0], recv_sem=recv.at[0],
                                        device_id=(x, y, 1 - c), device_id_type=MESH)


def _sibling_start(name, arrs, after=None):
    n = len(arrs)
    extra = [] if after is None else [after]
    lands = [lax.empty(a.shape, a.dtype) for a in arrs]

    def body(*refs):
        src_refs, land_refs = refs[:n], refs[n:2 * n]
        outs = refs[2 * n + len(extra):]
        sends, recvs, token = outs[:n], outs[n:2 * n], outs[-1]
        for k in range(n):
            _sibling_copy(src_refs[k], land_refs[k], sends[k], recvs[k]).start()
        token[...] = jnp.zeros(token.shape, F32)

    sem = pltpu.SemaphoreType.DMA((1,))
    thru = tuple(pltpu.HBM(a.shape, a.dtype) for a in list(arrs) + lands)
    res = pl.pallas_call(
        body, name=name,
        out_shape=(sem,) * (2 * n) + thru + (jax.ShapeDtypeStruct((SUBLANES, 128), F32),),
        in_specs=[_HBM] * (2 * n) + [pl.BlockSpec(memory_space=pl.ANY)] * len(extra),
        out_specs=(_SEM,) * (2 * n) + (_HBM,) * (2 * n) + (pl.BlockSpec(memory_space=pltpu.VMEM),),
        input_output_aliases={k: 2 * n + k for k in range(2 * n)},
        compiler_params=pltpu.CompilerParams(has_side_effects=_EFFECT),
    )(*[pltpu.with_memory_space_constraint(a, pltpu.HBM) for a in list(arrs) + lands], *extra)
    return res[:n], res[n:2 * n], res[2 * n:3 * n], res[3 * n:4 * n], res[-1]


def _sibling_wait(name, sends, recvs, srcs, lands, after):
    n = len(srcs)

    def body(*refs):
        src_refs, land_refs = refs[:n], refs[n:2 * n]
        sends_, recvs_ = refs[2 * n:3 * n], refs[3 * n:4 * n]
        for k in range(n):
            cp = _sibling_copy(src_refs[k], land_refs[k], sends_[k], recvs_[k])
            cp.wait_send()
            cp.wait_recv()

    thru = tuple(pltpu.HBM(a.shape, a.dtype) for a in list(srcs) + list(lands))
    res = pl.pallas_call(
        body, name=name, out_shape=thru,
        in_specs=[_HBM] * (2 * n) + [_SEM] * (2 * n) + [pl.BlockSpec(memory_space=pl.ANY)],
        out_specs=(_HBM,) * (2 * n),
        input_output_aliases={k: k for k in range(2 * n)},
        compiler_params=pltpu.CompilerParams(has_side_effects=_EFFECT),
    )(*srcs, *lands, *sends, *recvs, after)
    return res[:n], res[n:]


def _landing(own, chip):
    zone = lax.empty((N_CHIPS,) + own.shape, own.dtype)
    return lax.dynamic_update_slice(zone, own[None], (chip,) + (0,) * own.ndim)


def _mod_shard(c_all, w_ada_sh, b_sh):
    d, n = w_ada_sh.shape
    bn = 512

    def body(c_ref, w_ref, b_ref, o_ref):
        cc = c_ref[...]
        ca = cc * _sigmoid(cc)
        o_ref[...] = _dot(ca, w_ref[...]) + b_ref[...]

    return pl.pallas_call(
        body, name="mod_shard", grid=(n // bn,),
        out_shape=jax.ShapeDtypeStruct((N_DEV, n), F32),
        in_specs=[_full((N_DEV, d)), pl.BlockSpec((d, bn), lambda j: (0, j)), pl.BlockSpec((1, bn), lambda j: (0, j))],
        out_specs=pl.BlockSpec((N_DEV, bn), lambda j: (0, j)),
        compiler_params=_cparams(("parallel",)),
    )(c_all, w_ada_sh, b_sh)


def _ssm_prep(lam_re, lam_im, log_step):
    g, p = lam_re.shape

    def body(lr_ref, li_ref, ls_ref, ar_ref, ai_ref, cr_ref, ci_ref):
        lr = jnp.minimum(lr_ref[...], LAMBDA_RE_MAX)
        li = li_ref[...]
        st = jnp.exp(ls_ref[...])
        mag = jnp.exp(lr * st)
        ar = mag * jnp.cos(li * st)
        ai = mag * jnp.sin(li * st)
        den = lr * lr + li * li
        nr = ar - 1.0
        ar_ref[...] = ar
        ai_ref[...] = ai
        cr_ref[...] = (nr * lr + ai * li) / den
        ci_ref[...] = (ai * lr - nr * li) / den

    sds = jax.ShapeDtypeStruct((g, p), F32)
    return pl.pallas_call(body, name="ssm_prep", out_shape=(sds,) * 4)(lam_re, lam_im, log_step)


def _ssm_blocks(bt_re, bt_im, ct_re, ct_im, coef_rows, tile_b, tile_c):
    gh, p = bt_re.shape
    gp, h = ct_re.shape
    nb = SSM_SPLIT
    cb, rb = gp // nb, gp // nb

    def body(btr, bti, ctr, cti, cf, tb_ref, tc_ref, bre_o, bim_o, cre_o, cim_o):
        j = pl.program_id(0)
        row = lax.broadcasted_iota(jnp.int32, (gh, cb), 0)
        col = lax.broadcasted_iota(jnp.int32, (gh, cb), 1) + j * cb
        mask = (row >> 4) == (col >> 6)
        cr, ci = cf[0:1, :], cf[1:2, :]
        br = _split3_dot(btr[...], tb_ref[...])
        bi = _split3_dot(bti[...], tb_ref[...])
        bre_o[...] = jnp.where(mask, br * cr - bi * ci, 0.0).astype(BF16)
        bim_o[...] = jnp.where(mask, br * ci + bi * cr, 0.0).astype(BF16)
        row2 = lax.broadcasted_iota(jnp.int32, (rb, gh), 0) + j * rb
        col2 = lax.broadcasted_iota(jnp.int32, (rb, gh), 1)
        mask2 = (row2 >> 6) == (col2 >> 4)
        cre_o[...] = jnp.where(mask2, _split3_dot(ctr[...], tc_ref[...]), 0.0).astype(BF16)
        cim_o[...] = jnp.where(mask2, _split3_dot(cti[...], tc_ref[...]), 0.0).astype(BF16)

    bspec = pl.BlockSpec((gh, cb), lambda j: (0, j))
    cspec = pl.BlockSpec((rb, gh), lambda j: (j, 0))
    cin = pl.BlockSpec((rb, h), lambda j: (j, 0))
    return pl.pallas_call(
        body, name="ssm_blocks", grid=(nb,),
        out_shape=(jax.ShapeDtypeStruct((gh, gp), BF16),) * 2 + (jax.ShapeDtypeStruct((gp, gh), BF16),) * 2,
        in_specs=[_full((gh, p)), _full((gh, p)), cin, cin, pl.BlockSpec((SUBLANES, cb), lambda j: (0, j)),
                  _full(tile_b.shape), _full(tile_c.shape)],
        out_specs=(bspec, bspec, cspec, cspec),
        compiler_params=_cparams(("parallel",)),
    )(bt_re, bt_im, ct_re, ct_im, coef_rows, tile_b, tile_c)


def _scan_consts(a_ref, reverse):
    w = a_ref.shape[1]
    ar1 = a_ref[0:1, :]
    ai1 = a_ref[1:2, :]
    if reverse:
        ai1 = -ai1
    pr, pi = [ar1], [ai1]
    for _ in range(1, SUBLANES):
        nr = pr[-1] * ar1 - pi[-1] * ai1
        ni = pr[-1] * ai1 + pi[-1] * ar1
        pr.append(nr)
        pi.append(ni)
    row = lax.broadcasted_iota(jnp.int32, (SUBLANES, w), 0)
    dist = (SUBLANES - 1 - row) if reverse else row

    def pick(vals):
        out = jnp.broadcast_to(vals[SUBLANES - 1], (SUBLANES, w))
        for r in range(SUBLANES - 1):
            out = jnp.where(dist == r, vals[r], out)
        return out

    p_r, p_i = pick(pr), pick(pi)
    steps = []
    for k in (1, 2, 4):
        steps.append((k, jnp.where(dist >= k, pr[k - 1], 0.0), jnp.where(dist >= k, pi[k - 1], 0.0)))
    a8 = (jnp.broadcast_to(pr[SUBLANES - 1], (SUBLANES, w)), jnp.broadcast_to(pi[SUBLANES - 1], (SUBLANES, w)))
    return row, p_r, p_i, steps, a8


def _scan_tile(xr, xi, cr, ci, consts, reverse):
    row, p_r, p_i, steps, (a8r, a8i) = consts
    for k, s_r, s_i in steps:
        sh = (SUBLANES - k) if reverse else k
        qr = pltpu.roll(xr, sh, 0)
        qi = pltpu.roll(xi, sh, 0)
        xr, xi = xr + s_r * qr - s_i * qi, xi + s_r * qi + s_i * qr
    outr = xr + p_r * cr - p_i * ci
    outi = xi + p_r * ci + p_i * cr
    e = 0 if reverse else SUBLANES - 1
    er = jnp.broadcast_to(xr[e:e + 1, :], xr.shape)
    ei = jnp.broadcast_to(xi[e:e + 1, :], xi.shape)
    return outr, outi, er + a8r * cr - a8i * ci, ei + a8r * ci + a8i * cr


def _scan_fwd(a_rows, bu_re, bu_im):
    t, n = bu_re.shape
    tb, w = _blk(t, TB_SCAN), W_SCAN
    ntile = tb // SUBLANES

    def body(a_ref, br_ref, bi_ref, sr_ref, si_ref, car, cai):
        @pl.when(pl.program_id(1) == 0)
        def _():
            car[...] = jnp.zeros(car.shape, F32)
            cai[...] = jnp.zeros(cai.shape, F32)
        consts = _scan_consts(a_ref, False)

        def pair(i, carry):
            o = pl.multiple_of(i * BF16_ROWS, BF16_ROWS)
            b_r = br_ref[pl.ds(o, BF16_ROWS), :].astype(F32)
            b_i = bi_ref[pl.ds(o, BF16_ROWS), :].astype(F32)
            outs = []
            for h in range(2):
                rows = slice(h * SUBLANES, (h + 1) * SUBLANES)
                outr, outi, ncr, nci = _scan_tile(b_r[rows, :], b_i[rows, :], carry[0], carry[1], consts, False)
                outs.append((outr, outi))
                carry = (ncr, nci)
            sr_ref[pl.ds(o, BF16_ROWS), :] = jnp.concatenate([outs[0][0], outs[1][0]], axis=0).astype(BF16)
            si_ref[pl.ds(o, BF16_ROWS), :] = jnp.concatenate([outs[0][1], outs[1][1]], axis=0).astype(BF16)
            return carry

        def pairs(i, carry):
            for s in range(SCAN_UNROLL // 2):
                carry = pair(i * (SCAN_UNROLL // 2) + s, carry)
            return carry

        cr, ci = lax.fori_loop(0, ntile // SCAN_UNROLL, pairs, (car[...], cai[...]))
        car[...] = cr
        cai[...] = ci

    spec = pl.BlockSpec((tb, w), lambda s, k: (k, s))
    sds = jax.ShapeDtypeStruct((t, n), BF16)
    return pl.pallas_call(
        body, name="scan_fwd", grid=(n // w, t // tb), out_shape=(sds, sds),
        in_specs=[pl.BlockSpec((SUBLANES, w), lambda s, k: (0, s)), spec, spec], out_specs=(spec, spec),
        scratch_shapes=[pltpu.VMEM((SUBLANES, w), F32), pltpu.VMEM((SUBLANES, w), F32)],
        compiler_params=_cparams(("parallel", "arbitrary"), VMEM_MID),
    )(a_rows, bu_re, bu_im)


def _scan_bwd(a_rows, g_re, g_im, s_re, s_im):
    t, n = g_re.shape
    tb, w = _blk(t, TB_SCAN), W_SCAN
    ntile = tb // SUBLANES
    npair = tb // BF16_ROWS
    nt = t // tb

    def body(a_ref, gr_ref, gi_ref, sr_ref, si_ref, or_ref, oi_ref, gar_ref, gai_ref, car, cai):
        @pl.when(pl.program_id(1) == 0)
        def _():
            car[...] = jnp.zeros(car.shape, F32)
            cai[...] = jnp.zeros(cai.shape, F32)
            gar_ref[...] = jnp.zeros(gar_ref.shape, F32)
            gai_ref[...] = jnp.zeros(gai_ref.shape, F32)
        consts = _scan_consts(a_ref, True)
        row = consts[0]

        def pair(i, carry):
            cr, ci, accr, acci = carry
            o = pl.multiple_of((npair - 1 - i) * BF16_ROWS, BF16_ROWS)
            s_r = sr_ref[pl.ds(o, BF16_ROWS), :].astype(F32)
            s_i = si_ref[pl.ds(o, BF16_ROWS), :].astype(F32)
            g_r = gr_ref[pl.ds(o, BF16_ROWS), :].astype(F32)
            g_i = gi_ref[pl.ds(o, BF16_ROWS), :].astype(F32)
            outs = [None, None]
            for h in (1, 0):
                rows = slice(h * SUBLANES, (h + 1) * SUBLANES)
                outr, outi, ncr, nci = _scan_tile(g_r[rows, :], g_i[rows, :], cr, ci, consts, True)
                outs[h] = (outr, outi)
                gnr = jnp.where(row == SUBLANES - 1, cr, pltpu.roll(outr, SUBLANES - 1, 0))
                gni = jnp.where(row == SUBLANES - 1, ci, pltpu.roll(outi, SUBLANES - 1, 0))
                sr = s_r[h * SUBLANES:(h + 1) * SUBLANES, :]
                si = s_i[h * SUBLANES:(h + 1) * SUBLANES, :]
                accr, acci = accr + sr * gnr + si * gni, acci + sr * gni - si * gnr
                cr, ci = ncr, nci
            or_ref[pl.ds(o, BF16_ROWS), :] = jnp.concatenate([outs[0][0], outs[1][0]], axis=0).astype(BF16)
            oi_ref[pl.ds(o, BF16_ROWS), :] = jnp.concatenate([outs[0][1], outs[1][1]], axis=0).astype(BF16)
            return cr, ci, accr, acci

        def pairs(i, carry):
            for s in range(SCAN_UNROLL // 2):
                carry = pair(i * (SCAN_UNROLL // 2) + s, carry)
            return carry

        cr, ci, accr, acci = lax.fori_loop(0, ntile // SCAN_UNROLL, pairs,
                                           (car[...], cai[...], gar_ref[...], gai_ref[...]))
        car[...] = cr
        cai[...] = ci
        gar_ref[...] = accr
        gai_ref[...] = acci

    spec = pl.BlockSpec((tb, w), lambda s, k: (nt - 1 - k, s))
    aspec = pl.BlockSpec((SUBLANES, w), lambda s, k: (0, s))
    sds = jax.ShapeDtypeStruct((t, n), BF16)
    asds = jax.ShapeDtypeStruct((SUBLANES, n), F32)
    return pl.pallas_call(
        body, name="scan_bwd", grid=(n // w, nt), out_shape=(sds, sds, asds, asds),
        in_specs=[aspec, spec, spec, spec, spec], out_specs=(spec, spec, aspec, aspec),
        scratch_shapes=[pltpu.VMEM((SUBLANES, w), F32), pltpu.VMEM((SUBLANES, w), F32)],
        compiler_params=_cparams(("parallel", "arbitrary"), VMEM_MID),
    )(a_rows, g_re, g_im, s_re, s_im)


def _mix_in(x, vec, w_in_st, b_re, b_im):
    t, d = x.shape
    ns, _, nc = w_in_st.shape
    dssm, nstate = b_re.shape
    du, ds = dssm // SSM_SPLIT, nstate // SSM_SPLIT
    tb = _blk(t, TB_MIX)

    def body(x_ref, vec_ref, w_ref, bre_ref, bim_ref, proj_ref, bur_ref, bui_ref, h1_ref):
        xv = x_ref[...]
        r = lax.rsqrt(_rowmean(xv * xv) + EPS)
        h = xv * r * vec_ref[0:1, :] * vec_ref[1:2, :] + vec_ref[2:3, :]
        hb = h.astype(BF16)
        h1_ref[...] = hb
        u = None
        for j in range(ns):
            pj = jnp.dot(hb, w_ref[j], preferred_element_type=F32)
            proj_ref[:, j * nc:(j + 1) * nc] = pj.astype(BF16)
            if j == 0:
                u = pj
        ub = u.astype(BF16)
        for q in range(SSM_SPLIT):
            rq, cq = slice(q * du, (q + 1) * du), slice(q * ds, (q + 1) * ds)
            bur_ref[:, cq] = jnp.dot(ub[:, rq], bre_ref[rq, cq], preferred_element_type=F32).astype(BF16)
            bui_ref[:, cq] = jnp.dot(ub[:, rq], bim_ref[rq, cq], preferred_element_type=F32).astype(BF16)

    return pl.pallas_call(
        body, name="mix_in", grid=(t // tb,),
        out_shape=(jax.ShapeDtypeStruct((t, ns * nc), BF16), jax.ShapeDtypeStruct((t, nstate), BF16),
                   jax.ShapeDtypeStruct((t, nstate), BF16), jax.ShapeDtypeStruct((t, d), BF16)),
        in_specs=[_rows(tb, d), _full((SUBLANES, d)), _resident(w_in_st.shape), _resident(b_re.shape),
                  _resident(b_im.shape)],
        out_specs=(_rows(tb, ns * nc), _rows(tb, nstate), _rows(tb, nstate), _rows(tb, d)),
        compiler_params=_cparams(("parallel",), VMEM_BIG),
    )(x, vec, w_in_st, b_re, b_im)


def _head_ms(y, h_ref):
    return _split_dot(y * y, h_ref[...])


def _conv3(x, halo, w_ref):
    return w_ref[0:1, :] * _shift_down(x, halo, 2) + w_ref[1:2, :] * _shift_down(x, halo, 1) + w_ref[2:3, :] * x


def _mix_out(x, proj, s_re, s_im, c_re, c_im, v512, convw, glu_w, h16, h64, w_out, vd):
    t, d = x.shape
    dh = c_re.shape[1]
    nstate = s_re.shape[1]
    du, ds = dh // SSM_SPLIT, nstate // SSM_SPLIT
    tb = _blk(t, TB_MIX)

    def body(x_ref, u_ref, bg_ref, cg_ref, v_ref, cgh_ref, vh_ref, sr_ref, si_ref, cre_ref, cim_ref, p_ref,
             cw_ref, gw_ref, h16_ref, h64_ref, wo_ref, vd_ref, y1_ref, o_ref, x2_ref):
        i = pl.program_id(0)
        u = u_ref[...].astype(F32)
        ys = []
        for q in range(SSM_SPLIT):
            rq, cq = slice(q * ds, (q + 1) * ds), slice(q * du, (q + 1) * du)
            ys.append(_dot(sr_ref[:, rq], cre_ref[rq, cq]) - _dot(si_ref[:, rq], cim_ref[rq, cq]))
        ys = jnp.concatenate(ys, axis=1)
        y1 = ys + p_ref[0:1, :] * u
        y1_ref[...] = y1
        z = _gelu(y1)
        q = _dot(z, gw_ref[...]) + p_ref[1:2, :]
        ya = z * _sigmoid(q)
        na = ya * lax.rsqrt(_head_ms(ya, h16_ref) + EPS) * p_ref[2:3, :]
        cv = cg_ref[...].astype(F32) * v_ref[...].astype(F32)
        cvh = jnp.where(i > 0, cgh_ref[...].astype(F32) * vh_ref[...].astype(F32), 0.0)
        yb = bg_ref[...].astype(F32) * _conv3(cv, cvh, cw_ref)
        nb = yb * lax.rsqrt(_head_ms(yb, h64_ref) + EPS) * p_ref[3:4, :]
        o = _dot(na, wo_ref[0:dh, :]) + _dot(nb, wo_ref[dh:2 * dh, :])
        o_ref[...] = o
        on = o * lax.rsqrt(_rowmean(o * o) + EPS) * vd_ref[0:1, :]
        x2_ref[...] = x_ref[...] + vd_ref[1:2, :] * on

    return pl.pallas_call(
        body, name="mix_out", grid=(t // tb,),
        out_shape=(jax.ShapeDtypeStruct((t, dh), F32), jax.ShapeDtypeStruct((t, d), F32),
                   jax.ShapeDtypeStruct((t, d), F32)),
        in_specs=[_rows(tb, d), _rows(tb, dh, 0), _rows(tb, dh, 1), _rows(tb, dh, 2), _rows(tb, dh, 3),
                  _halo_prev(tb, dh, 2, BF16_ROWS), _halo_prev(tb, dh, 3, BF16_ROWS), _rows(tb, nstate), _rows(tb, nstate),
                  _full(c_re.shape), _full(c_im.shape), _full(v512.shape), _full(convw.shape), _full(glu_w.shape),
                  _full(h16.shape), _full(h64.shape), _full(w_out.shape), _full(vd.shape)],
        out_specs=(_rows(tb, dh), _rows(tb, d), _rows(tb, d)),
        compiler_params=_cparams(("parallel",), VMEM_BIG),
    )(x, proj, proj, proj, proj, proj, proj, s_re, s_im, c_re, c_im, v512, convw, glu_w, h16, h64, w_out, vd)


def _ffn_up(x2, vec, w_up_st):
    t, d = x2.shape
    ns, _, nc = w_up_st.shape
    tb = _blk(t, TB_FFN_UP)

    def body(x_ref, vec_ref, w_ref, up_ref, h2_ref):
        xv = x_ref[...]
        r = lax.rsqrt(_rowmean(xv * xv) + EPS)
        h = xv * r * vec_ref[0:1, :] * vec_ref[1:2, :] + vec_ref[2:3, :]
        hb = h.astype(BF16)
        h2_ref[...] = hb
        for j in range(ns):
            up_ref[:, j * nc:(j + 1) * nc] = jnp.dot(hb, w_ref[j], preferred_element_type=F32)

    return pl.pallas_call(
        body, name="ffn_up", grid=(t // tb,),
        out_shape=(jax.ShapeDtypeStruct((t, ns * nc), F32), jax.ShapeDtypeStruct((t, d), BF16)),
        in_specs=[_rows(tb, d), _full((SUBLANES, d)), _resident(w_up_st.shape)],
        out_specs=(_rows(tb, ns * nc), _rows(tb, d)),
        compiler_params=_cparams(("parallel",), VMEM_BIG),
    )(x2, vec, w_up_st)


def _ffn_down(up, fw, w_down, w_down_t, x2, tgt, vd):
    t, nh = up.shape
    dff, d = w_down.shape
    tb = _blk(t, TB_FFN)
    inv_d = 1.0 / d

    def body(up_ref, uph_ref, fw_ref, wd_ref, wdt_ref, x2_ref, tgt_ref, vd_ref,
             act_ref, ddn_ref, dout_ref, dhid_ref, vec_ref, loss_ref, a_s, vv_s, sg_s):
        i = pl.program_id(0)

        def conv_cols(sl):
            x = up_ref[:, sl]
            halo = jnp.where(i > 0, uph_ref[:, sl], 0.0)
            return (fw_ref[0:1, sl] * _shift_down(x, halo, 2) + fw_ref[1:2, sl] * _shift_down(x, halo, 1)
                    + fw_ref[2:3, sl] * x)

        dn = None
        for o in range(0, dff, CW_FFN):
            sl = slice(o, o + CW_FFN)
            a = conv_cols(sl)
            vv = conv_cols(slice(dff + o, dff + o + CW_FFN))
            sg = _sigmoid(a)
            si = a * sg
            a_s[:, sl] = si
            vv_s[:, sl] = vv
            sg_s[:, sl] = sg
            actb = (si * vv).astype(BF16)
            act_ref[:, sl] = actb
            pj = lax.dot_general(actb, wdt_ref[:, sl], (((1,), (1,)), ((), ())), preferred_element_type=F32)
            dn = pj if dn is None else dn + pj
        r3 = lax.rsqrt(_rowmean(dn * dn) + EPS)
        xn = dn * r3
        g = vd_ref[0:1, :]
        gt2 = vd_ref[1:2, :]
        dnn = xn * g
        diff = x2_ref[...] + gt2 * dnn - tgt_ref[...]
        part = 0.5 * inv_d * jnp.sum(diff * diff)

        @pl.when(i == 0)
        def _():
            loss_ref[...] = jnp.zeros(loss_ref.shape, F32)
        loss_ref[...] += part
        dout = diff * inv_d
        dout_ref[...] = dout
        ddnn = dout * gt2
        _acc_rows(vec_ref, i == 0, [_colsum(dout * dnn), _colsum(ddnn * xn)])
        dxn = ddnn * g
        ddn = r3 * (dxn - xn * _rowmean(dxn * xn))
        ddnb = ddn.astype(BF16)
        ddn_ref[...] = ddnb
        for o in range(0, dff, CW_FFN):
            sl = slice(o, o + CW_FFN)
            dact = lax.dot_general(ddnb, wd_ref[sl, :], (((1,), (1,)), ((), ())), preferred_element_type=F32)
            si, vv, sg = a_s[:, sl], vv_s[:, sl], sg_s[:, sl]
            dhid_ref[:, sl] = (dact * vv * (sg + si * (1.0 - sg))).astype(BF16)
            dhid_ref[:, dff + o:dff + o + CW_FFN] = (dact * si).astype(BF16)

    return pl.pallas_call(
        body, name="ffn_down", grid=(t // tb,),
        scratch_shapes=[pltpu.VMEM((tb, dff), F32)] * 3,
        out_shape=(jax.ShapeDtypeStruct((t, dff), BF16), jax.ShapeDtypeStruct((t, d), BF16),
                   jax.ShapeDtypeStruct((t, d), F32), jax.ShapeDtypeStruct((t, nh), BF16),
                   jax.ShapeDtypeStruct((SUBLANES, d), F32), jax.ShapeDtypeStruct((SUBLANES, 128), F32)),
        in_specs=[_rows(tb, nh), _halo_prev(tb, nh), _full(fw.shape), _resident(w_down.shape),
                  _resident(w_down_t.shape), _rows(tb, d),
                  _rows(tb, d), _full(vd.shape)],
        out_specs=(_rows(tb, dff), _rows(tb, d), _rows(tb, d), _rows(tb, nh), _full((SUBLANES, d)),
                   _full((SUBLANES, 128))),
        compiler_params=_cparams(("arbitrary",), VMEM_BIG),
    )(up, up, fw, w_down, w_down_t, x2, tgt, vd)


def _ffn_up_bwd(dhid, up, fw, x2, dout, vec, w_up_st):
    t, nh = dhid.shape
    d = x2.shape[1]
    ns, _, nc = w_up_st.shape
    tb = _blk(t, TB_FFN)
    nblk = t // tb
    cw = 128

    def body(dh_ref, dhn_ref, up_ref, fw_ref, x2_ref, dout_ref, vec_ref, w_ref,
             dx2_ref, dup_ref, vp_ref, df_ref):
        i = pl.program_id(0)

        @pl.when(i == 0)
        def _():
            df_ref[...] = jnp.zeros(df_ref.shape, F32)
        dh2 = None
        for j in range(ns):
            for o in range(j * nc, (j + 1) * nc, cw):
                sl = slice(o, o + cw)
                dh = dh_ref[:, sl].astype(F32)
                dhn = jnp.where(i < nblk - 1, dhn_ref[:, sl].astype(F32), 0.0)
                dh1 = _shift_up(dh, dhn, 1)
                dh2s = _shift_up(dh, dhn, 2)
                dup_ref[:, sl] = (fw_ref[2:3, sl] * dh + fw_ref[1:2, sl] * dh1 + fw_ref[0:1, sl] * dh2s).astype(BF16)
                up_v = up_ref[:, sl]
                df_ref[0:1, sl] += _colsum(dh2s * up_v)
                df_ref[1:2, sl] += _colsum(dh1 * up_v)
                df_ref[2:3, sl] += _colsum(dh * up_v)
            pj = lax.dot_general(dup_ref[:, j * nc:(j + 1) * nc], w_ref[j], (((1,), (1,)), ((), ())),
                                 preferred_element_type=F32)
            dh2 = pj if dh2 is None else dh2 + pj
        xv = x2_ref[...]
        r = lax.rsqrt(_rowmean(xv * xv) + EPS)
        xn = xv * r
        g = vec_ref[0:1, :]
        hg = xn * g
        dhg = dh2 * vec_ref[1:2, :]
        _acc_rows(vp_ref, i == 0, [_colsum(dh2), _colsum(dh2 * hg), _colsum(dhg * xn)])
        dxn = dhg * g
        dx2_ref[...] = dout_ref[...] + r * (dxn - xn * _rowmean(dxn * xn))

    return pl.pallas_call(
        body, name="ffn_up_bwd", grid=(nblk,),
        out_shape=(jax.ShapeDtypeStruct((t, d), F32), jax.ShapeDtypeStruct((t, nh), BF16),
                   jax.ShapeDtypeStruct((SUBLANES, d), F32), jax.ShapeDtypeStruct((SUBLANES, nh), F32)),
        in_specs=[_rows(tb, nh), _halo_next(tb, nh, t, rows=BF16_ROWS), _rows(tb, nh), _full(fw.shape),
                  _rows(tb, d), _rows(tb, d), _full(vec.shape), _resident(w_up_st.shape)],
        out_specs=(_rows(tb, d), _rows(tb, nh), _full((SUBLANES, d)), _full((SUBLANES, nh))),
        compiler_params=_cparams(("arbitrary",), VMEM_BIG),
    )(dhid, dhid, up, fw, x2, dout, vec, w_up_st)


def _mix_out_bwd(dx2, o, y1, proj, s_re, s_im, c_re, c_im, v512, convw, glu_w, h16, h64, w_out, vd):
    t, d = dx2.shape
    dh = y1.shape[1]
    nstate = c_re.shape[0]
    du, ds = dh // SSM_SPLIT, nstate // SSM_SPLIT
    tb = _blk(t, TB_MIX)

    def body(dx2_ref, o_ref, y1_ref, u_ref, bg_ref, cg_ref, v_ref, cgh_ref, vh_ref, cre_ref, cim_ref, p_ref,
             cw_ref, gw_ref, h16_ref, h64_ref, wo_ref, vd_ref, sr_ref, si_ref,
             do_ref, ycat_ref, z_ref, dq_ref, dy1_ref, gr_ref, gi_ref, dcc_ref, dbg_ref, vpd_ref, vp5_ref,
             dcr_ref, dci_ref):
        i = pl.program_id(0)
        first = i == 0

        @pl.when(first)
        def _():
            dcr_ref[...] = jnp.zeros(dcr_ref.shape, F32)
            dci_ref[...] = jnp.zeros(dci_ref.shape, F32)
        ov = o_ref[...]
        ro = lax.rsqrt(_rowmean(ov * ov) + EPS)
        on_ = ov * ro
        g = vd_ref[0:1, :]
        dx2v = dx2_ref[...]
        don = dx2v * vd_ref[1:2, :]
        _acc_rows(vpd_ref, first, [_colsum(dx2v * on_ * g), _colsum(don * on_)])
        dxn = don * g
        dob = (ro * (dxn - on_ * _rowmean(dxn * on_))).astype(BF16)
        do_ref[...] = dob
        dyc_a =lax.dot_general(dob, wo_ref[0:dh, :], (((1,), (1,)), ((), ())), preferred_element_type=F32)
        dyc_b = lax.dot_general(dob, wo_ref[dh:2 * dh, :], (((1,), (1,)), ((), ())), preferred_element_type=F32)
        y1v = y1_ref[...]
        u = u_ref[...].astype(F32)
        z, dz_dy1 = _gelu_and_grad(y1v)
        zb = z.astype(BF16)
        sg = _sigmoid(jnp.dot(zb, gw_ref[...], preferred_element_type=F32) + p_ref[1:2, :])
        ya = z * sg
        ra = lax.rsqrt(_head_ms(ya, h16_ref) + EPS)
        yan = ya * ra
        ga = p_ref[2:3, :]
        ycat_ref[:, 0:dh] = (yan * ga).astype(BF16)
        dyn = dyc_a * ga
        dya = ra * (dyn - yan * _split_dot(dyn * yan, h16_ref[...]))
        dq = dya * z * sg * (1.0 - sg)
        dqb = dq.astype(BF16)
        z_ref[...] = zb
        dq_ref[...] = dqb
        dz = dya * sg + lax.dot_general(dqb, gw_ref[...], (((1,), (1,)), ((), ())), preferred_element_type=F32)
        dy1 = dz * dz_dy1
        dy1_ref[...] = dy1
        dy1b = dy1.astype(BF16)
        for q in range(SSM_SPLIT):
            rq, cq = slice(q * ds, (q + 1) * ds), slice(q * du, (q + 1) * du)
            gr_ref[:, rq] = lax.dot_general(dy1b[:, cq], cre_ref[rq, cq], (((1,), (1,)), ((), ())),
                                            preferred_element_type=F32).astype(BF16)
            gi_ref[:, rq] = (-lax.dot_general(dy1b[:, cq], cim_ref[rq, cq], (((1,), (1,)), ((), ())),
                                              preferred_element_type=F32)).astype(BF16)
            dcr_ref[rq, :] += _dot_tn(sr_ref[:, rq], dy1b[:, cq])
            dci_ref[rq, :] += _dot_tn(si_ref[:, rq], dy1b[:, cq])
        bg = bg_ref[...].astype(F32)
        cv = cg_ref[...].astype(F32) * v_ref[...].astype(F32)
        cvh = jnp.where(i > 0, cgh_ref[...].astype(F32) * vh_ref[...].astype(F32), 0.0)
        cv1 = _shift_down(cv, cvh, 1)
        cv2 = _shift_down(cv, cvh, 2)
        cc = cw_ref[0:1, :] * cv2 + cw_ref[1:2, :] * cv1 + cw_ref[2:3, :] * cv
        yb = bg * cc
        rb = lax.rsqrt(_head_ms(yb, h64_ref) + EPS)
        ybn = yb * rb
        gb = p_ref[3:4, :]
        ycat_ref[:, dh:2 * dh] = (ybn * gb).astype(BF16)
        dynb = dyc_b * gb
        dyb = rb * (dynb - ybn * _split_dot(dynb * ybn, h64_ref[...]))
        dcc = dyb * bg
        dbg_ref[...] = dyb * cc
        dcc_ref[...] = dcc
        _acc_rows(vp5_ref, first, [_colsum(dyc_a * yan), _colsum(dyc_b * ybn), _colsum(dq), _colsum(dy1 * u),
                                   _colsum(dcc * cv2), _colsum(dcc * cv1), _colsum(dcc * cv)])

    return pl.pallas_call(
        body, name="mix_out_bwd", grid=(t // tb,),
        out_shape=(jax.ShapeDtypeStruct((t, d), BF16), jax.ShapeDtypeStruct((t, 2 * dh), BF16),
                   jax.ShapeDtypeStruct((t, dh), BF16), jax.ShapeDtypeStruct((t, dh), BF16),
                   jax.ShapeDtypeStruct((t, dh), F32), jax.ShapeDtypeStruct((t, nstate), BF16),
                   jax.ShapeDtypeStruct((t, nstate), BF16), jax.ShapeDtypeStruct((t, dh), F32),
                   jax.ShapeDtypeStruct((t, dh), F32), jax.ShapeDtypeStruct((SUBLANES, d), F32),
                   jax.ShapeDtypeStruct((SUBLANES, dh), F32), jax.ShapeDtypeStruct((nstate, du), F32),
                   jax.ShapeDtypeStruct((nstate, du), F32)),
        in_specs=[_rows(tb, d), _rows(tb, d), _rows(tb, dh), _rows(tb, dh, 0), _rows(tb, dh, 1), _rows(tb, dh, 2),
                  _rows(tb, dh, 3), _halo_prev(tb, dh, 2, BF16_ROWS), _halo_prev(tb, dh, 3, BF16_ROWS), _resident(c_re.shape),
                  _resident(c_im.shape), _full(v512.shape), _full(convw.shape), _resident(glu_w.shape),
                  _resident(h16.shape), _resident(h64.shape), _resident(w_out.shape), _full(vd.shape),
                  _rows(tb, nstate), _rows(tb, nstate)],
        out_specs=(_rows(tb, d), _rows(tb, 2 * dh), _rows(tb, dh), _rows(tb, dh), _rows(tb, dh), _rows(tb, nstate),
                   _rows(tb, nstate), _rows(tb, dh), _rows(tb, dh), _full((SUBLANES, d)), _full((SUBLANES, dh)),
                   _full((nstate, du)), _full((nstate, du))),
        compiler_params=_cparams(("arbitrary",), VMEM_BIG),
    )(dx2, o, y1, proj, proj, proj, proj, proj, proj, c_re, c_im, v512, convw, glu_w, h16, h64, w_out, vd,
      s_re, s_im)


def _mix_in_bwd(gt_re, gt_im, b_re, b_im, dy1, dcc, dbg, proj, x, dx2, vec, v512, convw, w_in_st):
    t, d = x.shape
    dh = dy1.shape[1]
    nstate = gt_re.shape[1]
    du_w, ds = dh // SSM_SPLIT, nstate // SSM_SPLIT
    ns, _, nc = w_in_st.shape
    tb = _blk(t, TB_MIX)
    nblk = t // tb

    def body(gr_ref, gi_ref, bre_ref, bim_ref, dy1_ref, dcc_ref, dccn_ref, dbg_ref, u_ref, cg_ref, v_ref, x_ref,
             dx2_ref, vec_ref, p_ref, cw_ref, w_ref, gx_ref, dproj_ref, vp_ref, dbr_ref, dbi_ref):
        i = pl.program_id(0)

        @pl.when(i == 0)
        def _():
            dbr_ref[...] = jnp.zeros(dbr_ref.shape, F32)
            dbi_ref[...] = jnp.zeros(dbi_ref.shape, F32)
        ub = u_ref[...].astype(BF16)
        du = []
        for q in range(SSM_SPLIT):
            rq, cq = slice(q * du_w, (q + 1) * du_w), slice(q * ds, (q + 1) * ds)
            du.append(lax.dot_general(gr_ref[:, cq].astype(BF16), bre_ref[rq, cq], (((1,), (1,)), ((), ())),
                                      preferred_element_type=F32)
                      + lax.dot_general(gi_ref[:, cq].astype(BF16), bim_ref[rq, cq], (((1,), (1,)), ((), ())),
                                        preferred_element_type=F32))
            dbr_ref[rq, :] += _dot_tn(ub[:, rq], gr_ref[:, cq])
            dbi_ref[rq, :] += _dot_tn(ub[:, rq], gi_ref[:, cq])
        du = dy1_ref[...] * p_ref[0:1, :] + jnp.concatenate(du, axis=1)
        dcc = dcc_ref[...]
        dccn = jnp.where(i < nblk - 1, dccn_ref[...], 0.0)
        dcv = (cw_ref[2:3, :] * dcc + cw_ref[1:2, :] * _shift_up(dcc, dccn, 1)
               + cw_ref[0:1, :] * _shift_up(dcc, dccn, 2))
        parts = [du, dbg_ref[...], dcv * v_ref[...].astype(F32), dcv * cg_ref[...].astype(F32)]
        xv = x_ref[...]
        r = lax.rsqrt(_rowmean(xv * xv) + EPS)
        xn = xv * r
        g = vec_ref[0:1, :]
        hg = xn * g
        dh1 = None
        for j in range(ns):
            pb = parts[j].astype(BF16)
            dproj_ref[:, j * nc:(j + 1) * nc] = pb
            pj =lax.dot_general(pb, w_ref[j], (((1,), (1,)), ((), ())), preferred_element_type=F32)
            dh1 = pj if dh1 is None else dh1 + pj
        dhg = dh1 * vec_ref[1:2, :]
        _acc_rows(vp_ref, i == 0, [_colsum(dh1), _colsum(dh1 * hg), _colsum(dhg * xn)])
        dxn = dhg * g
        gx_ref[...] = dx2_ref[...] + r * (dxn - xn * _rowmean(dxn * xn))

    assert nc == dh and ns == 4
    return pl.pallas_call(
        body, name="mix_in_bwd", grid=(nblk,),
        out_shape=(jax.ShapeDtypeStruct((t, d), F32), jax.ShapeDtypeStruct((t, ns * nc), BF16),
                   jax.ShapeDtypeStruct((SUBLANES, d), F32), jax.ShapeDtypeStruct((dh, ds), F32),
                   jax.ShapeDtypeStruct((dh, ds), F32)),
        in_specs=[_rows(tb, nstate), _rows(tb, nstate), _resident(b_re.shape), _resident(b_im.shape), _rows(tb, dh),
                  _rows(tb, dh), _halo_next(tb, dh, t), _rows(tb, dh), _rows(tb, dh, 0), _rows(tb, dh, 2),
                  _rows(tb, dh, 3), _rows(tb, d), _rows(tb, d), _full(vec.shape), _full(v512.shape),
                  _full(convw.shape), _resident(w_in_st.shape)],
        out_specs=(_rows(tb, d), _rows(tb, ns * nc), _full((SUBLANES, d)), _full((dh, ds)), _full((dh, ds))),
        compiler_params=_cparams(("arbitrary",), VMEM_BIG),
    )(gt_re, gt_im, b_re, b_im, dy1, dcc, dcc, dbg, proj, proj, proj, x, dx2, vec, v512, convw, w_in_st)


def _matmul_tn(a, b, m, bn, out_dtype, name, diag=False, bt=TB_TN, after=None):
    t = a.shape[0]
    n = b.shape[1]
    bt = _blk(t, bt)
    nk = t // bt
    extra = [] if after is None else [after]
    a_map = (lambda j, k: (k, j)) if diag else (lambda j, k: (k, 0))

    def body(a_ref, b_ref, *rest):
        o_ref, acc_ref = rest[-2:]
        k = pl.program_id(1)

        @pl.when(k == 0)
        def _():
            acc_ref[...] = jnp.zeros(acc_ref.shape, F32)
        acc_ref[...] += _dot_tn(a_ref[...], b_ref[...])

        @pl.when(k == nk - 1)
        def _():
            o_ref[...] = acc_ref[...].astype(out_dtype)

    return pl.pallas_call(
        body, name=name, grid=(n // bn, nk),
        out_shape=jax.ShapeDtypeStruct((n // bn, m, bn), out_dtype),
        in_specs=[pl.BlockSpec((bt, m), a_map), pl.BlockSpec((bt, bn), lambda j, k: (k, j))]
        + [pl.BlockSpec(memory_space=pl.ANY)] * len(extra),
        out_specs=pl.BlockSpec((None, m, bn), lambda j, k: (j, 0, 0)),
        scratch_shapes=[pltpu.VMEM((m, bn), F32)],
        compiler_params=_cparams(("parallel", "arbitrary"), VMEM_BIG),
    )(a, b, *extra)


def _ssm_bgrad(d_bre, d_bim, bt_re, bt_im, rows_in, fold, tile_b):
    gh, cb = d_bre.shape
    nb = SSM_SPLIT
    rb = gh // nb
    gp = nb * cb
    p = fold.shape[1]

    def body(dr_ref, di_ref, br_ref, bi_ref, rin_ref, f_ref, tb_ref, dbr_ref, dbi_ref, rout_ref):
        row = lax.broadcasted_iota(jnp.int32, (rb, cb), 0)
        col = lax.broadcasted_iota(jnp.int32, (rb, cb), 1)
        mask = (row >> 4) == (col >> 6)
        gr = jnp.where(mask, dr_ref[...], 0.0)
        gi = jnp.where(mask, di_ref[...], 0.0)
        cr, ci = rin_ref[0:1, :], rin_ref[1:2, :]
        dbr_ref[...] = _split3_dot(cr * gr + ci * gi, f_ref[...])
        dbi_ref[...] = _split3_dot(cr * gi - ci * gr, f_ref[...])
        br = _split3_dot(br_ref[...], tb_ref[...])
        bi = _split3_dot(bi_ref[...], tb_ref[...])
        rout_ref[...] = jnp.zeros(rout_ref.shape, F32)
        rout_ref[0:1, :] = _colsum(br * gr + bi * gi)
        rout_ref[1:2, :] = _colsum(br * gi - bi * gr)

    dspec = pl.BlockSpec((rb, cb), lambda j: (j, 0))
    rspec = pl.BlockSpec((SUBLANES, cb), lambda j: (0, j))
    ospec = pl.BlockSpec((rb, p), lambda j: (j, 0))
    return pl.pallas_call(
        body, name="ssm_bgrad", grid=(nb,),
        out_shape=(jax.ShapeDtypeStruct((gh, p), F32), jax.ShapeDtypeStruct((gh, p), F32),
                   jax.ShapeDtypeStruct((SUBLANES, gp), F32)),
        in_specs=[dspec, dspec, ospec, ospec, rspec, _full(fold.shape), _full(tile_b.shape)],
        out_specs=(ospec, ospec, rspec),
        compiler_params=_cparams(("parallel",)),
    )(d_bre, d_bim, bt_re, bt_im, rows_in, fold, tile_b)


def _ssm_cgrad(d_cre, d_cim, fold):
    gp, cb = d_cre.shape
    nb = SSM_SPLIT
    rb = gp // nb
    h = fold.shape[1]

    def body(dr_ref, di_ref, f_ref, cr_ref, ci_ref):
        row = lax.broadcasted_iota(jnp.int32, (rb, cb), 0)
        col = lax.broadcasted_iota(jnp.int32, (rb, cb), 1)
        mask = (row >> 6) == (col >> 4)
        cr_ref[...] = _split3_dot(jnp.where(mask, dr_ref[...], 0.0), f_ref[...])
        ci_ref[...] = -_split3_dot(jnp.where(mask, di_ref[...], 0.0), f_ref[...])

    cspec = pl.BlockSpec((rb, cb), lambda j: (j, 0))
    ospec = pl.BlockSpec((rb, h), lambda j: (j, 0))
    return pl.pallas_call(
        body, name="ssm_cgrad", grid=(nb,),
        out_shape=(jax.ShapeDtypeStruct((gp, h), F32),) * 2,
        in_specs=[cspec, cspec, _full(fold.shape)], out_specs=(ospec, ospec),
        compiler_params=_cparams(("parallel",)),
    )(d_cre, d_cim, fold)


def _ssm_lamgrad(lam_re, lam_im, log_step, abar_re, abar_im, coef_re, coef_im, gc_re, gc_im, ga_re, ga_im):
    g, p = lam_re.shape

    def body(lr_ref, li_ref, ls_ref, ar_ref, ai_ref, cr_ref, ci_ref, gcr_ref, gci_ref, gar_ref, gai_ref,
             dlr_ref, dli_ref, dls_ref):
        lam_raw = lr_ref[...]
        lr = jnp.minimum(lam_raw, LAMBDA_RE_MAX)
        li = li_ref[...]
        st = jnp.exp(ls_ref[...])
        den = lr * lr + li * li
        gcr, gci = gcr_ref[...], gci_ref[...]
        gab_r = gar_ref[...] + (lr * gcr - li * gci) / den
        gab_i = gai_ref[...] + (lr * gci + li * gcr) / den
        cr, ci = cr_ref[...], ci_ref[...]
        wr = -(cr * lr + ci * li) / den
        wi = -(ci * lr - cr * li) / den
        gl_r = wr * gcr + wi * gci
        gl_i = wr * gci - wi * gcr
        ar, ai = ar_ref[...], ai_ref[...]
        gw_r = ar * gab_r + ai * gab_i
        gw_i = ar * gab_i - ai * gab_r
        gl_r = gl_r + st * gw_r
        gl_i = gl_i + st * gw_i
        pass_through = jnp.where(lam_raw < LAMBDA_RE_MAX, 1.0, jnp.where(lam_raw == LAMBDA_RE_MAX, 0.5, 0.0))
        dlr_ref[...] = gl_r * pass_through
        dli_ref[...] = gl_i
        dls_ref[...] = st * jnp.sum(lr * gw_r + li * gw_i, axis=1, keepdims=True)

    sds = jax.ShapeDtypeStruct((g, p), F32)
    return pl.pallas_call(body, name="ssm_lamgrad", out_shape=(sds, sds, jax.ShapeDtypeStruct((g, 1), F32)))(
        lam_re, lam_im, log_step, abar_re, abar_im, coef_re, coef_im, gc_re, gc_im, ga_re, ga_im)


def _row_block(r, most=256):
    for rb in range(min(r, most), BF16_ROWS - 1, -1):
        if r % rb == 0 and rb % BF16_ROWS == 0:
            return rb
    return r


def _adamw_math(w, g, m, v):
    m = ADAM_B1 * m + (1.0 - ADAM_B1) * g
    v = ADAM_B2 * v + (1.0 - ADAM_B2) * (g * g)
    m_hat = m / (1.0 - ADAM_B1 ** ADAM_STEP)
    v_hat = v / (1.0 - ADAM_B2 ** ADAM_STEP)
    delta = -ADAM_LR * (m_hat / (jnp.sqrt(v_hat) + ADAM_EPS) + ADAM_WD * w)
    return delta, m, v


def _adamw_big(p_mine, p_sib, w, m, v, name):
    r, c = w.shape
    rb = _row_block(r)

    def body(a_ref, b_ref, w_ref, m_ref, v_ref, g_ref, d_ref, mo_ref, vo_ref):
        g = a_ref[...].astype(F32) + b_ref[...].astype(F32)
        g_ref[...] = g
        d_ref[...], mo_ref[...], vo_ref[...] = _adamw_math(w_ref[...], g, m_ref[...], v_ref[...])

    spec = pl.BlockSpec((rb, c), lambda i: (i, 0))
    sds = jax.ShapeDtypeStruct((r, c), F32)
    return pl.pallas_call(
        body, name=name, grid=(r // rb,), out_shape=(sds,) * 4, in_specs=[spec] * 5, out_specs=(spec,) * 4,
        compiler_params=_cparams(("parallel",), VMEM_KEEP_OPERANDS_IN_HBM),
    )(p_mine, p_sib, w, m, v)


def _sum_blocks(stack, name):
    n, r, c = stack.shape
    rb = _row_block(r)

    def body(s_ref, o_ref):
        acc = s_ref[0].astype(F32)
        for k in range(1, n):
            acc = acc + s_ref[k].astype(F32)
        o_ref[...] = acc

    return pl.pallas_call(
        body, name=name, grid=(r // rb,), out_shape=jax.ShapeDtypeStruct((r, c), F32),
        in_specs=[pl.BlockSpec((n, rb, c), lambda i: (0, i, 0))], out_specs=pl.BlockSpec((rb, c), lambda i: (i, 0)),
        compiler_params=_cparams(("parallel",), VMEM_KEEP_OPERANDS_IN_HBM),
    )(stack)


def _sum_landed(landed, own, chip, name):
    n, r, c = landed.shape
    rb = _row_block(r)

    def body(chip_ref, own_ref, l1_ref, l2_ref, l3_ref, o_ref):
        acc = own_ref[0].astype(F32)
        for ref in (l1_ref, l2_ref, l3_ref):
            acc = acc + ref[0].astype(F32)
        o_ref[...] = acc.astype(BF16)

    def slot(k):
        return pl.BlockSpec((1, rb, c), lambda i, ch: ((ch[0] + k) % n, i, 0))

    return pl.pallas_call(
        body, name=name, out_shape=jax.ShapeDtypeStruct((r, c), BF16),
        grid_spec=pltpu.PrefetchScalarGridSpec(
            num_scalar_prefetch=1, grid=(r // rb,), in_specs=[slot(0), slot(1), slot(2), slot(3)],
            out_specs=pl.BlockSpec((rb, c), lambda i, ch: (i, 0))),
        compiler_params=_cparams(("parallel",), VMEM_KEEP_OPERANDS_IN_HBM),
    )(jnp.reshape(chip, (1,)).astype(jnp.int32), own, landed, landed, landed)


def _add2(a, b):
    def body(a_ref, b_ref, o_ref):
        o_ref[...] = a_ref[...] + b_ref[...]

    return pl.pallas_call(body, name="add_small", out_shape=jax.ShapeDtypeStruct(a.shape, F32))(a, b)


def _adamw_ada(c_all, dmod_cols, w, m, v):
    d, n = w.shape
    bn = 512

    def body(c_ref, dm_ref, w_ref, m_ref, v_ref, g_ref, d_ref, mo_ref, vo_ref):
        cc = c_ref[...]
        g = _dot_tn(cc * _sigmoid(cc), dm_ref[...])
        g_ref[...] = g
        d_ref[...], mo_ref[...], vo_ref[...] = _adamw_math(w_ref[...], g, m_ref[...], v_ref[...])

    spec = pl.BlockSpec((d, bn), lambda j: (0, j))
    sds = jax.ShapeDtypeStruct((d, n), F32)
    return pl.pallas_call(
        body, name="adamw_ada", grid=(n // bn,), out_shape=(sds,) * 4,
        in_specs=[_full((N_DEV, d)), pl.BlockSpec((N_DEV, bn), lambda j: (0, j)), spec, spec, spec],
        out_specs=(spec,) * 4, compiler_params=_cparams(("parallel",), VMEM_KEEP_OPERANDS_IN_HBM),
    )(c_all, dmod_cols, w, m, v)


def _adamw_small(items):
    n = len(items)

    def body(*refs):
        ins, outs = refs[:4 * n], refs[4 * n:]
        for k in range(n):
            w_ref, g_ref, m_ref, v_ref = ins[4 * k:4 * k + 4]
            outs[3 * k][...], outs[3 * k + 1][...], outs[3 * k + 2][...] = _adamw_math(
                w_ref[...], g_ref[...], m_ref[...], v_ref[...])

    flat = [a for it in items for a in it]
    out_shape = tuple(jax.ShapeDtypeStruct(it[0].shape, F32) for it in items for _ in range(3))
    res = pl.pallas_call(body, name="adamw_small", out_shape=out_shape,
                         compiler_params=_cparams(vmem=VMEM_KEEP_OPERANDS_IN_HBM))(*flat)
    return [tuple(res[3 * k:3 * k + 3]) for k in range(n)]


def _group_mean_matrix(n, group):
    idx = np.arange(n) // group
    return (idx[:, None] == idx[None, :]).astype(np.float32) / group


def _fold_matrix(n, period):
    return (np.arange(n)[:, None] % period == np.arange(period)[None, :]).astype(np.float32)


def _rows8(*rows):
    c = rows[0].shape[-1]
    pad = jnp.zeros((SUBLANES - len(rows), c), F32)
    return jnp.concatenate([r.reshape(1, c) for r in rows] + [pad], axis=0)


def _to_rows(a, width):
    flat = a.reshape(-1)
    n = -(-flat.shape[0] // width)
    flat = jnp.pad(flat, (0, n * width - flat.shape[0]))
    return flat.reshape(n, width)


def kernel(x, c, w_ada, b_ada, g_pre_mix, g_post_mix, w_in, ssm_lam_re, ssm_lam_im, ssm_log_step, ssm_b_re, ssm_b_im, ssm_c_re, ssm_c_im, ssm_d, glu_w, glu_b, g_out_ssm, conv_w, g_out_conv, w_out, g_pre_ffn, g_post_ffn, w_up, ffn_conv_w, w_down, loss_target, m_w_ada, m_b_ada, m_g_pre_mix, m_g_post_mix, m_w_in, m_ssm_lam_re, m_ssm_lam_im, m_ssm_log_step, m_ssm_b_re, m_ssm_b_im, m_ssm_c_re, m_ssm_c_im, m_ssm_d, m_glu_w, m_glu_b, m_g_out_ssm, m_conv_w, m_g_out_conv, m_w_out, m_g_pre_ffn, m_g_post_ffn, m_w_up, m_ffn_conv_w, m_w_down, v_w_ada, v_b_ada, v_g_pre_mix, v_g_post_mix, v_w_in, v_ssm_lam_re, v_ssm_lam_im, v_ssm_log_step, v_ssm_b_re, v_ssm_b_im, v_ssm_c_re, v_ssm_c_im, v_ssm_d, v_glu_w, v_glu_b, v_g_out_ssm, v_conv_w, v_g_out_conv, v_w_out, v_g_pre_ffn, v_g_post_ffn, v_w_up, v_ffn_conv_w, v_w_down):
    xs = x[0]
    tgt = loss_target[0]
    t, d = xs.shape
    xi, yi, ci = lax.axis_index("x"), lax.axis_index("y"), lax.axis_index("c")
    chip = 2 * xi + yi
    dev = 2 * chip + ci

    n_groups, n_state = ssm_lam_re.shape[1:]
    n_gch = ssm_b_re.shape[3]
    d_ssm = n_groups * n_gch
    gp = n_groups * n_state
    n_ada = w_ada.shape[2]
    d_ff = w_down.shape[1] * N_CHIPS
    n_upc = w_up.shape[2]

    w_names = ("w_in", "glu_w", "w_out", "w_up", "w_down")
    c_gath = _allgather8(jnp.broadcast_to(c, (SUBLANES, d)), "gather_c")
    c_all = c_gath.reshape(N_DEV, SUBLANES, d)[:, 0, :]

    def pad8(a):
        return jnp.concatenate([a, jnp.zeros((SUBLANES - a.shape[0], a.shape[1]), a.dtype)], axis=0)

    def start(name, arrs, after):
        return _chips_start(name, True, [], [_landing(a, chip) for a in arrs], after)

    w_names = ("w_in", "mod", "conv_w", "ffn_conv_w", "glu_w", "w_out", "w_up", "w_down")
    first = start("weights_start_in", [w_in[0].astype(BF16)], c_gath)
    b_sh = lax.dynamic_slice(b_ada, (0, chip * n_ada), (1, n_ada))
    mod_sh = _mod_shard(c_all + first[4][0:1, 0:1], w_ada[0], b_sh)
    second = start("weights_start_rest", [mod_sh, pad8(conv_w[0]), pad8(ffn_conv_w[0])]
                   + [w[0].astype(BF16) for w in (glu_w, w_out, w_up, w_down)], None)
    w_send, w_recv, w_land = [list(first[k]) + list(second[k]) for k in (0, 1, 3)]
    w_token = second[4]

    def weights(names, after):
        ks = [w_names.index(nm) for nm in names]
        return _chips_wait("weights_wait_" + names[-1], True, [w_send[k] for k in ks], [w_recv[k] for k in ks],
                           [], [w_land[k] for k in ks], after)[1]

    lam_re, lam_im = ssm_lam_re[0], ssm_lam_im[0]
    log_step = ssm_log_step[0].reshape(n_groups, 1) + w_token[0:1, 0:1]
    abar_re, abar_im, coef_re, coef_im = _ssm_prep(lam_re, lam_im, log_step)
    a_rows = _rows8(abar_re.reshape(1, gp), abar_im.reshape(1, gp))
    coef_rows = _rows8(coef_re.reshape(1, gp), coef_im.reshape(1, gp))
    bt_re = ssm_b_re[0].transpose(0, 2, 1).reshape(d_ssm, n_state)
    bt_im = ssm_b_im[0].transpose(0, 2, 1).reshape(d_ssm, n_state)
    ct_re = ssm_c_re[0].transpose(0, 2, 1).reshape(gp, n_gch)
    ct_im = ssm_c_im[0].transpose(0, 2, 1).reshape(gp, n_gch)
    tile_b = jnp.asarray(np.tile(np.eye(n_state), (1, n_groups // SSM_SPLIT)), BF16)
    tile_c = jnp.asarray(np.tile(np.eye(n_gch), (1, n_groups)), BF16)
    bblk_re, bblk_im, cblk_re, cblk_im = _ssm_blocks(bt_re, bt_im, ct_re, ct_im, coef_rows, tile_b, tile_c)

    h16 = jnp.asarray(_group_mean_matrix(d_ssm, n_gch), BF16)
    h64 = jnp.asarray(_group_mean_matrix(d_ssm, CONV_HEAD_DIM), BF16)

    g_mod, g_cw, g_fw, w_in_st = weights(("mod", "conv_w", "ffn_conv_w", "w_in"), bblk_re)
    mod_all = g_mod.transpose(1, 0, 2).reshape(N_DEV, N_CHIPS * n_ada)
    mod = lax.dynamic_slice(mod_all, (dev, 0), (1, N_CHIPS * n_ada))
    sh1, sc1, gt1, sh2, sc2, gt2 = [mod[:, k * d:(k + 1) * d] for k in range(6)]
    convw_full = pad8(g_cw[:, :3, :].transpose(1, 0, 2).reshape(3, d_ssm))
    fw_full = pad8(g_fw[:, :3, :].transpose(1, 0, 2).reshape(3, N_CHIPS * n_upc))

    v512 = _rows8(ssm_d, glu_b, g_out_ssm, g_out_conv)
    vec1 =_rows8(g_pre_mix, 1.0 + sc1, sh1)
    vd1 = _rows8(g_post_mix, gt1)
    vec2 = _rows8(g_pre_ffn, 1.0 + sc2, sh2)
    vd2 = _rows8(g_post_ffn, gt2)

    proj, bu_re, bu_im, h1b = _mix_in(xs, vec1, w_in_st, bblk_re, bblk_im)
    s_re, s_im = _scan_fwd(a_rows, bu_re, bu_im)
    g_glu, g_wout = weights(("glu_w", "w_out"), s_re)
    glu_full = g_glu.reshape(d_ssm, d_ssm)
    w_out_full = g_wout.reshape(2 * d_ssm, d)
    y1, o_mix, x2 = _mix_out(xs, proj, s_re, s_im, cblk_re, cblk_im, v512, convw_full, glu_full, h16, h64,
                             w_out_full, vd1)
    (w_up_st,) = weights(("w_up",), x2)
    up, h2b = _ffn_up(x2, vec2, w_up_st)
    (g_wdown,) = weights(("w_down",), up)
    w_down_full = g_wdown.reshape(d_ff, d)
    actb, ddnb, dout, dhid, vp_dn, loss_blk = _ffn_down(up, fw_full, w_down_full, w_down_full.T, x2, tgt, vd2)

    gw_down = _matmul_tn(actb, ddnb, d_ff, d, BF16, "dw_down", bt=1024).reshape(N_CHIPS, d_ff // N_CHIPS, d)
    dx2, dupb, vp_up, df_rows = _ffn_up_bwd(dhid, up, fw_full, x2, dout, vec2, w_up_st)
    gw_up = _matmul_tn(h2b, dupb, d, n_upc, BF16, "dw_up", bt=2048)
    ga_send, ga_recv, ga_src, ga_land, ga_token = _chips_start(
        "grads_start_ffn", False, [gw_down, gw_up], [lax.empty(g.shape, g.dtype) for g in (gw_down, gw_up)])
    (dob, ycatb, zb, dqb, dy1, g_re, g_im, dcc, dbg, vp_mo, vp5, d_cre, d_cim) = _mix_out_bwd(
        dx2, o_mix, y1, proj, s_re, s_im, cblk_re, cblk_im, v512, convw_full, glu_full, h16, h64, w_out_full,
        vd1 + ga_token[0:1, 0:1])
    gw_out = _matmul_tn(ycatb, dob, 2 * d_ssm, d, BF16, "dw_out", bt=2048)
    gw_out = gw_out.reshape(N_CHIPS, 2 * d_ssm // N_CHIPS, d)
    gw_glu = _matmul_tn(zb, dqb, d_ssm, d_ssm, BF16, "dw_glu", bt=2048).reshape(N_CHIPS, d_ssm // N_CHIPS, d_ssm)
    gb_send, gb_recv, gb_src, gb_land, gb_token = _chips_start(
        "grads_start_mix", False, [gw_out, gw_glu], [lax.empty(g.shape, g.dtype) for g in (gw_out, gw_glu)])
    gt_re, gt_im, ga_re8, ga_im8 = _scan_bwd(a_rows + gb_token[0:1, 0:1], g_re, g_im, s_re, s_im)
    grad_x, dprojb, vp_mi, d_bre, d_bim = _mix_in_bwd(gt_re, gt_im, bblk_re, bblk_im, dy1, dcc, dbg, proj, xs, dx2,
                                                      vec1, v512, convw_full, w_in_st)
    ssm_u, ssm_s = d_ssm // SSM_SPLIT, gp // SSM_SPLIT

    fold_b = jnp.asarray(_fold_matrix(ssm_s, n_state), BF16)
    fold_c = jnp.asarray(_fold_matrix(ssm_u, n_gch), BF16)
    db_re_f, db_im_f, gc_rows = _ssm_bgrad(d_bre, d_bim, bt_re, bt_im, coef_rows, fold_b, tile_b)
    dc_re_f, dc_im_f = _ssm_cgrad(d_cre, d_cim, fold_c)
    ga_sum = _ga_rowsum(ga_re8, ga_im8)
    g_lam_re, g_lam_im, g_log_step = _ssm_lamgrad(
        lam_re, lam_im, log_step, abar_re, abar_im, coef_re, coef_im,
        gc_rows[0].reshape(n_groups, n_state), gc_rows[1].reshape(n_groups, n_state),
        ga_sum[0].reshape(n_groups, n_state), ga_sum[1].reshape(n_groups, n_state))
    g_b_re = db_re_f.reshape(n_groups, n_gch, n_state).transpose(0, 2, 1)
    g_b_im = db_im_f.reshape(n_groups, n_gch, n_state).transpose(0, 2, 1)
    g_c_re = dc_re_f.reshape(n_groups, n_state, n_gch).transpose(0, 2, 1)
    g_c_im = dc_im_f.reshape(n_groups, n_state, n_gch).transpose(0, 2, 1)

    dmod = jnp.concatenate([vp_mi[0:1], vp_mi[1:2], vp_mo[0:1], vp_up[0:1], vp_up[1:2], vp_dn[0:1]], axis=1)
    small = [
        ("g_pre_mix", vp_mi[2:3]), ("g_post_mix", vp_mo[1:2]), ("g_pre_ffn", vp_up[2:3]), ("g_post_ffn", vp_dn[1:2]),
        ("ssm_lam_re", g_lam_re), ("ssm_lam_im", g_lam_im), ("ssm_log_step", g_log_step),
        ("ssm_b_re", g_b_re), ("ssm_b_im", g_b_im), ("ssm_c_re", g_c_re), ("ssm_c_im", g_c_im),
        ("ssm_d", vp5[3:4]), ("glu_b", vp5[2:3]), ("g_out_ssm", vp5[0:1]), ("g_out_conv", vp5[1:2]),
        ("conv_w", vp5[4:7]), ("ffn_conv_w", df_rows[0:3]), ("loss", loss_blk[0:1, 0:1]),
    ]
    packed, offsets, row = [], {}, 0
    for name, a in small:
        r = _to_rows(a, d)
        offsets[name] = (row, a.shape)
        packed.append(r)
        row += r.shape[0]
    n_small = -(-row // SUBLANES) * SUBLANES
    packed.append(jnp.zeros((n_small - row, d), F32))
    packed.append(pad8(dmod.reshape(6, d)))
    pack = jnp.concatenate(packed, axis=0)
    sm_send, sm_recv, _, sm_land, sm_token = _chips_start("small_start", True, [], [_landing(pack, chip)])

    gw_in = _matmul_tn(h1b, dprojb, d, w_in.shape[2], BF16, "dw_in", bt=2048, after=sm_token)
    gc_send, gc_recv, gc_src, gc_land, gc_token = _chips_start(
        "grads_start_in", False, [gw_in], [lax.empty(gw_in.shape, gw_in.dtype)])

    def partials(names, own, landed):
        return [_sum_landed(l, o, chip, "sum_" + nm) for l, o, nm in zip(landed, own, names)]

    def update(names, mine, theirs):
        done = {}
        for nm, pm, ps in zip(names, mine, theirs):
            w_, m_, v_ = big_params[nm]
            done[nm] = _adamw_big(pm, ps, w_[0], m_[0], v_[0], "adamw_" + nm)
        return done

    big_params = {"w_down": (w_down, m_w_down, v_w_down), "w_up": (w_up, m_w_up, v_w_up),
                  "w_out": (w_out, m_w_out, v_w_out), "glu_w": (glu_w, m_glu_w, v_glu_w),
                  "w_in": (w_in, m_w_in, v_w_in)}
    ffn_names, mix_names = ("w_down", "w_up"), ("w_out", "glu_w", "w_in")
    p_ffn = partials(ffn_names, *_chips_wait("grads_wait_ffn", False, ga_send, ga_recv, ga_src, ga_land, gc_token))
    sa_send, sa_recv, sa_src, sa_land, sa_token = _sibling_start("swap_start_ffn", p_ffn)

    (sm_landed,) = _chips_wait("small_wait", True, sm_send, sm_recv, [], sm_land, sa_token)[1]
    sm_part = _sum_blocks(sm_landed, "sum_small")
    dmod_mine = sm_landed[:, n_small:n_small + SUBLANES, :]
    ss_send, ss_recv, ss_src, ss_land, ss_token = _sibling_start("swap_start_small", [sm_part, dmod_mine])
    p_ffn, t_ffn = _sibling_wait("swap_wait_ffn", sa_send, sa_recv, sa_src, sa_land, ss_token)
    big = update(ffn_names, p_ffn, t_ffn)
    (sm_part, dmod_mine), (sm_sib, dmod_sib) = _sibling_wait("swap_wait_small", ss_send, ss_recv, ss_src, ss_land,
                                                              big["w_up"][0])
    sums = _add2(sm_part, sm_sib)
    dmod_by_core = jnp.stack([dmod_mine, dmod_sib], axis=1)
    dmod_by_core = jnp.where(ci == 0, dmod_by_core, dmod_by_core[:, ::-1])
    dmod_all = dmod_by_core[:, :, :6, :].reshape(N_DEV, 6 * d)
    g_b_ada = sums[n_small:n_small + 6].reshape(1, 6 * d)

    def unpack(name):
        r0, shape = offsets[name]
        size = math.prod(shape)
        nrow = -(-size // d)
        return sums[r0:r0 + nrow].reshape(-1)[:size].reshape(shape)

    p_mix = partials(mix_names, *_chips_wait(
        "grads_wait_mix", False, list(gb_send) + list(gc_send), list(gb_recv) + list(gc_recv),
        list(gb_src) + list(gc_src), list(gb_land) + list(gc_land), sums))
    sb_send, sb_recv, sb_src, sb_land, sb_token = _sibling_start("swap_start_mix", p_mix)

    dmod_cols = lax.dynamic_slice(dmod_all, (0, chip * n_ada), (N_DEV, n_ada)) + sb_token[0:1, 0:1]
    ada = _adamw_ada(c_all, dmod_cols, w_ada[0], m_w_ada[0], v_w_ada[0])
    p_mix, t_mix = _sibling_wait("swap_wait_mix", sb_send, sb_recv, sb_src, sb_land, ada[0])
    big.update(update(mix_names, p_mix, t_mix))

    g_small = {name: unpack(name) for name, _ in small}
    g_small["b_ada"] = g_b_ada
    g_small["conv_w"] = lax.dynamic_slice(g_small["conv_w"], (0, chip * conv_w.shape[2]), (3, conv_w.shape[2]))
    g_small["ffn_conv_w"] = lax.dynamic_slice(g_small["ffn_conv_w"], (0, chip * n_upc), (3, n_upc))
    g_small["ssm_log_step"] = g_small["ssm_log_step"].reshape(1, n_groups)
    small_params = {
        "b_ada": (b_ada, m_b_ada, v_b_ada), "g_pre_mix": (g_pre_mix, m_g_pre_mix, v_g_pre_mix),
        "g_post_mix": (g_post_mix, m_g_post_mix, v_g_post_mix), "ssm_lam_re": (ssm_lam_re, m_ssm_lam_re, v_ssm_lam_re),
        "ssm_lam_im": (ssm_lam_im, m_ssm_lam_im, v_ssm_lam_im),
        "ssm_log_step": (ssm_log_step, m_ssm_log_step, v_ssm_log_step),
        "ssm_b_re": (ssm_b_re, m_ssm_b_re, v_ssm_b_re), "ssm_b_im": (ssm_b_im, m_ssm_b_im, v_ssm_b_im),
        "ssm_c_re": (ssm_c_re, m_ssm_c_re, v_ssm_c_re), "ssm_c_im": (ssm_c_im, m_ssm_c_im, v_ssm_c_im),
        "ssm_d": (ssm_d, m_ssm_d, v_ssm_d), "glu_b": (glu_b, m_glu_b, v_glu_b),
        "g_out_ssm": (g_out_ssm, m_g_out_ssm, v_g_out_ssm), "conv_w": (conv_w, m_conv_w, v_conv_w),
        "g_out_conv": (g_out_conv, m_g_out_conv, v_g_out_conv), "g_pre_ffn": (g_pre_ffn, m_g_pre_ffn, v_g_pre_ffn),
        "g_post_ffn": (g_post_ffn, m_g_post_ffn, v_g_post_ffn),
        "ffn_conv_w": (ffn_conv_w, m_ffn_conv_w, v_ffn_conv_w),
    }

    def natural(a):
        return a[0] if a.ndim > 2 else a

    names = list(small_params)
    items = []
    for nm in names:
        w_, m_, v_ = small_params[nm]
        items.append((natural(w_), g_small[nm].reshape(natural(w_).shape), natural(m_), natural(v_)))
    upd = _adamw_small(items)
    small_out = {}
    for nm, (dl, mo, vo) in zip(names, upd):
        shp = small_params[nm][0].shape
        small_out[nm] = (g_small[nm].reshape(shp), dl.reshape(shp), mo.reshape(shp), vo.reshape(shp))

    loss = g_small["loss"][0, 0]

    order = ["w_ada", "b_ada", "g_pre_mix", "g_post_mix", "w_in", "ssm_lam_re", "ssm_lam_im", "ssm_log_step",
             "ssm_b_re", "ssm_b_im", "ssm_c_re", "ssm_c_im", "ssm_d", "glu_w", "glu_b", "g_out_ssm", "conv_w",
             "g_out_conv", "w_out", "g_pre_ffn", "g_post_ffn", "w_up", "ffn_conv_w", "w_down"]
    results = {"w_ada": tuple(a[None] for a in ada)}
    for nm in big:
        results[nm] = tuple(a[None] for a in big[nm])
    results.update(small_out)
    outs = [loss, grad_x[None]]
    for k in range(4):
        outs += [results[nm][k] for nm in order]
    return tuple(outs)


def _ga_rowsum(ga_re8, ga_im8):
    n = ga_re8.shape[1]

    def body(r_ref, i_ref, o_ref):
        o_ref[...] = jnp.zeros(o_ref.shape, F32)
        o_ref[0:1, :] = _colsum(r_ref[...])
        o_ref[1:2, :] = _colsum(i_ref[...])

    return pl.pallas_call(body, name="ga_rowsum", out_shape=jax.ShapeDtypeStruct((SUBLANES, n), F32))(ga_re8, ga_im8)
```

```python
import functools
import math

import jax
import jax.numpy as jnp
import numpy as np
from jax import lax
from jax.experimental import pallas as pl
from jax.experimental.pallas import tpu as pltpu

F32 = jnp.float32
BF16 = jnp.bfloat16
MESH = pl.DeviceIdType.MESH

EPS = 1e-6
LAMBDA_RE_MAX = -1e-4
ADAM_LR = 0.001
ADAM_B1 = 0.9
ADAM_B2 = 0.999
ADAM_EPS = 1e-08
ADAM_WD = 0.01
ADAM_STEP = 10

SUBLANES = 8
BF16_ROWS = 16
N_CHIPS = 4
N_DEV = 8
CONV_HEAD_DIM = 64
VMEM_BIG = 56 * 1024 * 1024
VMEM_MID = 40 * 1024 * 1024
VMEM_KEEP_OPERANDS_IN_HBM = 62 * 1024 * 1024

TB_MIX = 256
TB_FFN = 256
TB_FFN_UP = 512
TB_SCAN = 2048
W_SCAN = 256
SSM_SPLIT = 4
CW_FFN = 256
SCAN_UNROLL = 4
TB_TN = 512


def _cparams(sem=None, vmem=None):
    kw = {}
    if sem is not None:
        kw["dimension_semantics"] = sem
    if vmem is not None:
        kw["vmem_limit_bytes"] = vmem
    return pltpu.CompilerParams(**kw)


def _blk(t, pref):
    return pref if t % pref == 0 else t


def _dot(a, b):
    return jnp.dot(a.astype(BF16), b.astype(BF16), preferred_element_type=F32)


def _dot_nt(a, b):
    return lax.dot_general(a.astype(BF16), b.astype(BF16), (((1,), (1,)), ((), ())),
                           preferred_element_type=F32)


def _dot_tn(a, b):
    return lax.dot_general(a.astype(BF16), b.astype(BF16), (((0,), (0,)), ((), ())),
                           preferred_element_type=F32)


def _sigmoid(x):
    return 0.5 * jnp.tanh(0.5 * x) + 0.5


_GELU_K = math.sqrt(2.0 / math.pi)
_GELU_C = 0.044715


def _gelu(x):
    th = jnp.tanh(_GELU_K * (x + _GELU_C * x * x * x))
    return x * (0.5 * (1.0 + th))


def _gelu_and_grad(x):
    x2 = x * x
    th = jnp.tanh(_GELU_K * (x + _GELU_C * x2 * x))
    half = 0.5 * (1.0 + th)
    return x * half, half + 0.5 * x * (1.0 - th * th) * _GELU_K * (1.0 + 3.0 * _GELU_C * x2)


def _rowmean(x):
    return jnp.mean(x, axis=-1, keepdims=True)


def _colsum(x):
    return jnp.sum(x, axis=0, keepdims=True)


def _split_dot(x, m):
    hi = x.astype(BF16)
    lo = (x - hi.astype(F32)).astype(BF16)
    return (jnp.dot(hi, m, preferred_element_type=F32) + jnp.dot(lo, m, preferred_element_type=F32))


def _split3_dot(x, m):
    hi = x.astype(BF16)
    r1 = x - hi.astype(F32)
    mid = r1.astype(BF16)
    lo = (r1 - mid.astype(F32)).astype(BF16)
    return (jnp.dot(hi, m, preferred_element_type=F32) + jnp.dot(mid, m, preferred_element_type=F32)
            + jnp.dot(lo, m, preferred_element_type=F32))


def _shift_down(x, halo, k):
    r = pltpu.roll(x, k, 0)
    row = lax.broadcasted_iota(jnp.int32, x.shape, 0)
    last = halo.shape[0]
    for j in range(k):
        r = jnp.where(row == j, halo[last - k + j:last - k + j + 1, :], r)
    return r


def _shift_up(x, halo, k):
    n = x.shape[0]
    r = pltpu.roll(x, n - k, 0)
    row = lax.broadcasted_iota(jnp.int32, x.shape, 0)
    for j in range(k):
        r = jnp.where(row == n - k + j, halo[j:j + 1, :], r)
    return r


def _acc_rows(ref, first, rows):
    @pl.when(first)
    def _():
        ref[...] = jnp.zeros(ref.shape, ref.dtype)
    for j, r in enumerate(rows):
        ref[j:j + 1, :] += r


def _rows(tb, c, col=0):
    return pl.BlockSpec((tb, c), lambda i, col=col: (i, col))


def _full(shape):
    nd = len(shape)
    return pl.BlockSpec(shape, lambda i, nd=nd: (0,) * nd)


def _resident(shape):
    nd = len(shape)
    return pl.BlockSpec(shape, lambda i, nd=nd: (0,) * nd, pipeline_mode=pl.Buffered(1))


def _halo_prev(tb, c, col=0, rows=SUBLANES):
    per = tb // rows
    return pl.BlockSpec((rows, c), lambda i, col=col: (jnp.maximum(i * per - 1, 0), col))


def _halo_next(tb, c, t, col=0, rows=SUBLANES):
    per = tb // rows
    last = t // rows - 1
    return pl.BlockSpec((rows, c), lambda i, col=col: (jnp.minimum((i + 1) * per, last), col))


def _mesh_pos():
    return lax.axis_index("x"), lax.axis_index("y"), lax.axis_index("c")


def _allgather8(x_pad, name):
    m_per, n = x_pad.shape

    def body(x_ref, out_ref, send_sems, recv_sems, local_sem):
        x, y, c = _mesh_pos()
        me, sibling = (x, y, c), (x, y, 1 - c)
        chips = [(1 - x, y), (x, 1 - y), (1 - x, 1 - y)]

        def rows(px, py, pc):
            return out_ref.at[pl.ds((4 * px + 2 * py + pc) * m_per, m_per), :]

        def copy(k, block, to, src=None):
            return pltpu.make_async_remote_copy(
                src_ref=rows(*block) if src is None else src, dst_ref=rows(*block),
                send_sem=send_sems.at[k], recv_sem=recv_sems.at[k], device_id=to, device_id_type=MESH)

        mine = pltpu.make_async_copy(x_ref, rows(*me), local_sem)
        mine.start()
        first = [copy(0, me, sibling, src=x_ref)]
        first += [copy(1 + j, me, (*chip, c), src=x_ref) for j, chip in enumerate(chips)]
        for cp in first:
            cp.start()
        passed = [copy(4 + j, (*chip, c), sibling) for j, chip in enumerate(chips)]
        for j, chip in enumerate(chips):
            copy(1 + j, (*chip, c), me).wait_recv()
            passed[j].start()
        copy(0, sibling, me).wait_recv()
        for j, chip in enumerate(chips):
            copy(4 + j, (*chip, 1 - c), me).wait_recv()
        for cp in first + passed:
            cp.wait_send()
        mine.wait()

    return pl.pallas_call(
        body, name=name,
        out_shape=jax.ShapeDtypeStruct((N_DEV * m_per, n), F32),
        in_specs=[pl.BlockSpec(memory_space=pltpu.VMEM)],
        out_specs=pl.BlockSpec(memory_space=pltpu.VMEM),
        scratch_shapes=[pltpu.SemaphoreType.DMA((7,)), pltpu.SemaphoreType.DMA((7,)), pltpu.SemaphoreType.DMA],
    )(x_pad)


_HBM = pl.BlockSpec(memory_space=pltpu.HBM)
_SEM = pl.BlockSpec(memory_space=pltpu.SEMAPHORE)
_EFFECT = pltpu.SideEffectType.DATAFLOW_SIDE_EFFECTING


def _chip_copy(gather, src_ref, land_ref, send, recv, j, arrival):
    x, y, c = _mesh_pos()
    peer = [(1 - x, y), (x, 1 - y), (1 - x, 1 - y)][j]
    peer_chip = 2 * peer[0] + peer[1]
    my_chip = 2 * x + y
    return pltpu.make_async_remote_copy(
        src_ref=land_ref.at[my_chip] if gather else src_ref.at[peer_chip],
        dst_ref=land_ref.at[peer_chip if arrival else my_chip],
        send_sem=send.at[j], recv_sem=recv.at[j], device_id=(*peer, c), device_id_type=MESH)


def _chips_start(name, gather, srcs, lands, after=None):
    n, ns = len(lands), len(srcs)
    extra = [] if after is None else [after]

    def body(*refs):
        src_refs, land_refs = refs[:ns], refs[ns:ns + n]
        outs = refs[ns + n + len(extra):]
        sends, recvs, token = outs[:n], outs[n:2 * n], outs[-1]
        for k in range(n):
            for j in range(3):
                _chip_copy(gather, src_refs[k] if ns else None, land_refs[k], sends[k], recvs[k], j, False).start()
        token[...] = jnp.zeros(token.shape, F32)

    sem = pltpu.SemaphoreType.DMA((3,))
    thru = tuple(pltpu.HBM(a.shape, a.dtype) for a in list(srcs) + list(lands))
    res = pl.pallas_call(
        body, name=name,
        out_shape=(sem,) * (2 * n) + thru + (jax.ShapeDtypeStruct((SUBLANES, 128), F32),),
        in_specs=[_HBM] * (ns + n) + [pl.BlockSpec(memory_space=pl.ANY)] * len(extra),
        out_specs=(_SEM,) * (2 * n) + (_HBM,) * (ns + n) + (pl.BlockSpec(memory_space=pltpu.VMEM),),
        input_output_aliases={k: 2 * n + k for k in range(ns + n)},
        compiler_params=pltpu.CompilerParams(has_side_effects=_EFFECT),
    )(*[pltpu.with_memory_space_constraint(a, pltpu.HBM) for a in list(srcs) + list(lands)], *extra)
    return res[:n], res[n:2 * n], res[2 * n:2 * n + ns], res[2 * n + ns:2 * n + ns + n], res[-1]


def _chips_wait(name, gather, sends, recvs, srcs, lands, after):
    n, ns = len(lands), len(srcs)

    def body(*refs):
        src_refs, land_refs = refs[:ns], refs[ns:ns + n]
        sends_, recvs_ = refs[ns + n:ns + 2 * n], refs[ns + 2 * n:ns + 3 * n]
        for k in range(n):
            for j in range(3):
                cp = _chip_copy(gather, src_refs[k] if ns else None, land_refs[k], sends_[k], recvs_[k], j, True)
                cp.wait_send()
                cp.wait_recv()

    thru = tuple(pltpu.HBM(a.shape, a.dtype) for a in list(srcs) + list(lands))
    res = pl.pallas_call(
        body, name=name, out_shape=thru,
        in_specs=[_HBM] * (ns + n) + [_SEM] * (2 * n) + [pl.BlockSpec(memory_space=pl.ANY)],
        out_specs=(_HBM,) * (ns + n),
        input_output_aliases={k: k for k in range(ns + n)},
        compiler_params=pltpu.CompilerParams(has_side_effects=_EFFECT),
    )(*srcs, *lands, *sends, *recvs, after)
    return res[:ns], res[ns:]


def _sibling_copy(src_ref, land_ref, send, recv):
    x, y, c = _mesh_pos()
    return pltpu.make_async_remote_copy(src_ref=src_ref, dst_ref=land_ref, send_sem=send.at[0], recv_sem=recv.at[0],
                                        device_id=(x, y, 1 - c), device_id_type=MESH)


def _sibling_start(name, arrs, after=None):
    n = len(arrs)
    extra = [] if after is None else [after]
    lands = [lax.empty(a.shape, a.dtype) for a in arrs]

    def body(*refs):
        src_refs, land_refs = refs[:n], refs[n:2 * n]
        outs = refs[2 * n + len(extra):]
        sends, recvs, token = outs[:n], outs[n:2 * n], outs[-1]
        for k in range(n):
            _sibling_copy(src_refs[k], land_refs[k], sends[k], recvs[k]).start()
        token[...] = jnp.zeros(token.shape, F32)

    sem = pltpu.SemaphoreType.DMA((1,))
    thru = tuple(pltpu.HBM(a.shape, a.dtype) for a in list(arrs) + lands)
    res = pl.pallas_call(
        body, name=name,
        out_shape=(sem,) * (2 * n) + thru + (jax.ShapeDtypeStruct((SUBLANES, 128), F32),),
        in_specs=[_HBM] * (2 * n) + [pl.BlockSpec(memory_space=pl.ANY)] * len(extra),
        out_specs=(_SEM,) * (2 * n) + (_HBM,) * (2 * n) + (pl.BlockSpec(memory_space=pltpu.VMEM),),
        input_output_aliases={k: 2 * n + k for k in range(2 * n)},
        compiler_params=pltpu.CompilerParams(has_side_effects=_EFFECT),
    )(*[pltpu.with_memory_space_constraint(a, pltpu.HBM) for a in list(arrs) + lands], *extra)
    return res[:n], res[n:2 * n], res[2 * n:3 * n], res[3 * n:4 * n], res[-1]


def _sibling_wait(name, sends, recvs, srcs, lands, after):
    n = len(srcs)

    def body(*refs):
        src_refs, land_refs = refs[:n], refs[n:2 * n]
        sends_, recvs_ = refs[2 * n:3 * n], refs[3 * n:4 * n]
        for k in range(n):
            cp = _sibling_copy(src_refs[k], land_refs[k], sends_[k], recvs_[k])
            cp.wait_send()
            cp.wait_recv()

    thru = tuple(pltpu.HBM(a.shape, a.dtype) for a in list(srcs) + list(lands))
    res = pl.pallas_call(
        body, name=name, out_shape=thru,
        in_specs=[_HBM] * (2 * n) + [_SEM] * (2 * n) + [pl.BlockSpec(memory_space=pl.ANY)],
        out_specs=(_HBM,) * (2 * n),
        input_output_aliases={k: k for k in range(2 * n)},
        compiler_params=pltpu.CompilerParams(has_side_effects=_EFFECT),
    )(*srcs, *lands, *sends, *recvs, after)
    return res[:n], res[n:]


def _landing(own, chip):
    zone = lax.empty((N_CHIPS,) + own.shape, own.dtype)
    return lax.dynamic_update_slice(zone, own[None], (chip,) + (0,) * own.ndim)


def _mod_shard(c_all, w_ada_sh, b_sh):
    d, n = w_ada_sh.shape
    bn = 512

    def body(c_ref, w_ref, b_ref, o_ref):
        cc = c_ref[...]
        ca = cc * _sigmoid(cc)
        o_ref[...] = _dot(ca, w_ref[...]) + b_ref[...]

    return pl.pallas_call(
        body, name="mod_shard", grid=(n // bn,),
        out_shape=jax.ShapeDtypeStruct((N_DEV, n), F32),
        in_specs=[_full((N_DEV, d)), pl.BlockSpec((d, bn), lambda j: (0, j)), pl.BlockSpec((1, bn), lambda j: (0, j))],
        out_specs=pl.BlockSpec((N_DEV, bn), lambda j: (0, j)),
        compiler_params=_cparams(("parallel",)),
    )(c_all, w_ada_sh, b_sh)


def _ssm_prep(lam_re, lam_im, log_step):
    g, p = lam_re.shape

    def body(lr_ref, li_ref, ls_ref, ar_ref, ai_ref, cr_ref, ci_ref):
        lr = jnp.minimum(lr_ref[...], LAMBDA_RE_MAX)
        li = li_ref[...]
        st = jnp.exp(ls_ref[...])
        mag = jnp.exp(lr * st)
        ar = mag * jnp.cos(li * st)
        ai = mag * jnp.sin(li * st)
        den = lr * lr + li * li
        nr = ar - 1.0
        ar_ref[...] = ar
        ai_ref[...] = ai
        cr_ref[...] = (nr * lr + ai * li) / den
        ci_ref[...] = (ai * lr - nr * li) / den

    sds = jax.ShapeDtypeStruct((g, p), F32)
    return pl.pallas_call(body, name="ssm_prep", out_shape=(sds,) * 4)(lam_re, lam_im, log_step)


def _ssm_blocks(bt_re, bt_im, ct_re, ct_im, coef_rows, tile_b, tile_c):
    gh, p = bt_re.shape
    gp, h = ct_re.shape
    nb = SSM_SPLIT
    cb, rb = gp // nb, gp // nb

    def body(btr, bti, ctr, cti, cf, tb_ref, tc_ref, bre_o, bim_o, cre_o, cim_o):
        j = pl.program_id(0)
        row = lax.broadcasted_iota(jnp.int32, (gh, cb), 0)
        col = lax.broadcasted_iota(jnp.int32, (gh, cb), 1) + j * cb
        mask = (row >> 4) == (col >> 6)
        cr, ci = cf[0:1, :], cf[1:2, :]
        br = _split3_dot(btr[...], tb_ref[...])
        bi = _split3_dot(bti[...], tb_ref[...])
        bre_o[...] = jnp.where(mask, br * cr - bi * ci, 0.0).astype(BF16)
        bim_o[...] = jnp.where(mask, br * ci + bi * cr, 0.0).astype(BF16)
        row2 = lax.broadcasted_iota(jnp.int32, (rb, gh), 0) + j * rb
        col2 = lax.broadcasted_iota(jnp.int32, (rb, gh), 1)
        mask2 = (row2 >> 6) == (col2 >> 4)
        cre_o[...] = jnp.where(mask2, _split3_dot(ctr[...], tc_ref[...]), 0.0).astype(BF16)
        cim_o[...] = jnp.where(mask2, _split3_dot(cti[...], tc_ref[...]), 0.0).astype(BF16)

    bspec = pl.BlockSpec((gh, cb), lambda j: (0, j))
    cspec = pl.BlockSpec((rb, gh), lambda j: (j, 0))
    cin = pl.BlockSpec((rb, h), lambda j: (j, 0))
    return pl.pallas_call(
        body, name="ssm_blocks", grid=(nb,),
        out_shape=(jax.ShapeDtypeStruct((gh, gp), BF16),) * 2 + (jax.ShapeDtypeStruct((gp, gh), BF16),) * 2,
        in_specs=[_full((gh, p)), _full((gh, p)), cin, cin, pl.BlockSpec((SUBLANES, cb), lambda j: (0, j)),
                  _full(tile_b.shape), _full(tile_c.shape)],
        out_specs=(bspec, bspec, cspec, cspec),
        compiler_params=_cparams(("parallel",)),
    )(bt_re, bt_im, ct_re, ct_im, coef_rows, tile_b, tile_c)


def _scan_consts(a_ref, reverse):
    w = a_ref.shape[1]
    ar1 = a_ref[0:1, :]
    ai1 = a_ref[1:2, :]
    if reverse:
        ai1 = -ai1
    pr, pi = [ar1], [ai1]
    for _ in range(1, SUBLANES):
        nr = pr[-1] * ar1 - pi[-1] * ai1
        ni = pr[-1] * ai1 + pi[-1] * ar1
        pr.append(nr)
        pi.append(ni)
    row = lax.broadcasted_iota(jnp.int32, (SUBLANES, w), 0)
    dist = (SUBLANES - 1 - row) if reverse else row

    def pick(vals):
        out = jnp.broadcast_to(vals[SUBLANES - 1], (SUBLANES, w))
        for r in range(SUBLANES - 1):
            out = jnp.where(dist == r, vals[r], out)
        return out

    p_r, p_i = pick(pr), pick(pi)
    steps = []
    for k in (1, 2, 4):
        steps.append((k, jnp.where(dist >= k, pr[k - 1], 0.0), jnp.where(dist >= k, pi[k - 1], 0.0)))
    a8 = (jnp.broadcast_to(pr[SUBLANES - 1], (SUBLANES, w)), jnp.broadcast_to(pi[SUBLANES - 1], (SUBLANES, w)))
    return row, p_r, p_i, steps, a8


def _scan_tile(xr, xi, cr, ci, consts, reverse):
    row, p_r, p_i, steps, (a8r, a8i) = consts
    for k, s_r, s_i in steps:
        sh = (SUBLANES - k) if reverse else k
        qr = pltpu.roll(xr, sh, 0)
        qi = pltpu.roll(xi, sh, 0)
        xr, xi = xr + s_r * qr - s_i * qi, xi + s_r * qi + s_i * qr
    outr = xr + p_r * cr - p_i * ci
    outi = xi + p_r * ci + p_i * cr
    e = 0 if reverse else SUBLANES - 1
    er = jnp.broadcast_to(xr[e:e + 1, :], xr.shape)
    ei = jnp.broadcast_to(xi[e:e + 1, :], xi.shape)
    return outr, outi, er + a8r * cr - a8i * ci, ei + a8r * ci + a8i * cr


def _scan_fwd(a_rows, bu_re, bu_im):
    t, n = bu_re.shape
    tb, w = _blk(t, TB_SCAN), W_SCAN
    ntile = tb // SUBLANES

    def body(a_ref, br_ref, bi_ref, sr_ref, si_ref, car, cai):
        @pl.when(pl.program_id(1) == 0)
        def _():
            car[...] = jnp.zeros(car.shape, F32)
            cai[...] = jnp.zeros(cai.shape, F32)
        consts = _scan_consts(a_ref, False)

        def pair(i, carry):
            o = pl.multiple_of(i * BF16_ROWS, BF16_ROWS)
            b_r = br_ref[pl.ds(o, BF16_ROWS), :].astype(F32)
            b_i = bi_ref[pl.ds(o, BF16_ROWS), :].astype(F32)
            outs = []
            for h in range(2):
                rows = slice(h * SUBLANES, (h + 1) * SUBLANES)
                outr, outi, ncr, nci = _scan_tile(b_r[rows, :], b_i[rows, :], carry[0], carry[1], consts, False)
                outs.append((outr, outi))
                carry = (ncr, nci)
            sr_ref[pl.ds(o, BF16_ROWS), :] = jnp.concatenate([outs[0][0], outs[1][0]], axis=0).astype(BF16)
            si_ref[pl.ds(o, BF16_ROWS), :] = jnp.concatenate([outs[0][1], outs[1][1]], axis=0).astype(BF16)
            return carry

        def pairs(i, carry):
            for s in range(SCAN_UNROLL // 2):
                carry = pair(i * (SCAN_UNROLL // 2) + s, carry)
            return carry

        cr, ci = lax.fori_loop(0, ntile // SCAN_UNROLL, pairs, (car[...], cai[...]))
        car[...] = cr
        cai[...] = ci

    spec = pl.BlockSpec((tb, w), lambda s, k: (k, s))
    sds = jax.ShapeDtypeStruct((t, n), BF16)
    return pl.pallas_call(
        body, name="scan_fwd", grid=(n // w, t // tb), out_shape=(sds, sds),
        in_specs=[pl.BlockSpec((SUBLANES, w), lambda s, k: (0, s)), spec, spec], out_specs=(spec, spec),
        scratch_shapes=[pltpu.VMEM((SUBLANES, w), F32), pltpu.VMEM((SUBLANES, w), F32)],
        compiler_params=_cparams(("parallel", "arbitrary"), VMEM_MID),
    )(a_rows, bu_re, bu_im)


def _scan_bwd(a_rows, g_re, g_im, s_re, s_im):
    t, n = g_re.shape
    tb, w = _blk(t, TB_SCAN), W_SCAN
    ntile = tb // SUBLANES
    npair = tb // BF16_ROWS
    nt = t // tb

    def body(a_ref, gr_ref, gi_ref, sr_ref, si_ref, or_ref, oi_ref, gar_ref, gai_ref, car, cai):
        @pl.when(pl.program_id(1) == 0)
        def _():
            car[...] = jnp.zeros(car.shape, F32)
            cai[...] = jnp.zeros(cai.shape, F32)
            gar_ref[...] = jnp.zeros(gar_ref.shape, F32)
            gai_ref[...] = jnp.zeros(gai_ref.shape, F32)
        consts = _scan_consts(a_ref, True)
        row = consts[0]

        def pair(i, carry):
            cr, ci, accr, acci = carry
            o = pl.multiple_of((npair - 1 - i) * BF16_ROWS, BF16_ROWS)
            s_r = sr_ref[pl.ds(o, BF16_ROWS), :].astype(F32)
            s_i = si_ref[pl.ds(o, BF16_ROWS), :].astype(F32)
            g_r = gr_ref[pl.ds(o, BF16_ROWS), :].astype(F32)
            g_i = gi_ref[pl.ds(o, BF16_ROWS), :].astype(F32)
            outs = [None, None]
            for h in (1, 0):
                rows = slice(h * SUBLANES, (h + 1) * SUBLANES)
                outr, outi, ncr, nci = _scan_tile(g_r[rows, :], g_i[rows, :], cr, ci, consts, True)
                outs[h] = (outr, outi)
                gnr = jnp.where(row == SUBLANES - 1, cr, pltpu.roll(outr, SUBLANES - 1, 0))
                gni = jnp.where(row == SUBLANES - 1, ci, pltpu.roll(outi, SUBLANES - 1, 0))
                sr = s_r[h * SUBLANES:(h + 1) * SUBLANES, :]
                si = s_i[h * SUBLANES:(h + 1) * SUBLANES, :]
                accr, acci = accr + sr * gnr + si * gni, acci + sr * gni - si * gnr
                cr, ci = ncr, nci
            or_ref[pl.ds(o, BF16_ROWS), :] = jnp.concatenate([outs[0][0], outs[1][0]], axis=0).astype(BF16)
            oi_ref[pl.ds(o, BF16_ROWS), :] = jnp.concatenate([outs[0][1], outs[1][1]], axis=0).astype(BF16)
            return cr, ci, accr, acci

        def pairs(i, carry):
            for s in range(SCAN_UNROLL // 2):
                carry = pair(i * (SCAN_UNROLL // 2) + s, carry)
            return carry

        cr, ci, accr, acci = lax.fori_loop(0, ntile // SCAN_UNROLL, pairs,
                                           (car[...], cai[...], gar_ref[...], gai_ref[...]))
        car[...] = cr
        cai[...] = ci
        gar_ref[...] = accr
        gai_ref[...] = acci

    spec = pl.BlockSpec((tb, w), lambda s, k: (nt - 1 - k, s))
    aspec = pl.BlockSpec((SUBLANES, w), lambda s, k: (0, s))
    sds = jax.ShapeDtypeStruct((t, n), BF16)
    asds = jax.ShapeDtypeStruct((SUBLANES, n), F32)
    return pl.pallas_call(
        body, name="scan_bwd", grid=(n // w, nt), out_shape=(sds, sds, asds, asds),
        in_specs=[aspec, spec, spec, spec, spec], out_specs=(spec, spec, aspec, aspec),
        scratch_shapes=[pltpu.VMEM((SUBLANES, w), F32), pltpu.VMEM((SUBLANES, w), F32)],
        compiler_params=_cparams(("parallel", "arbitrary"), VMEM_MID),
    )(a_rows, g_re, g_im, s_re, s_im)


def _mix_in(x, vec, w_in_st, b_re, b_im):
    t, d = x.shape
    ns, _, nc = w_in_st.shape
    dssm, nstate = b_re.shape
    du, ds = dssm // SSM_SPLIT, nstate // SSM_SPLIT
    tb = _blk(t, TB_MIX)

    def body(x_ref, vec_ref, w_ref, bre_ref, bim_ref, proj_ref, bur_ref, bui_ref, h1_ref):
        xv = x_ref[...]
        r = lax.rsqrt(_rowmean(xv * xv) + EPS)
        h = xv * r * vec_ref[0:1, :] * vec_ref[1:2, :] + vec_ref[2:3, :]
        hb = h.astype(BF16)
        h1_ref[...] = hb
        u = None
        for j in range(ns):
            pj = jnp.dot(hb, w_ref[j], preferred_element_type=F32)
            proj_ref[:, j * nc:(j + 1) * nc] = pj.astype(BF16)
            if j == 0:
                u = pj
        ub = u.astype(BF16)
        for q in range(SSM_SPLIT):
            rq, cq = slice(q * du, (q + 1) * du), slice(q * ds, (q + 1) * ds)
            bur_ref[:, cq] = jnp.dot(ub[:, rq], bre_ref[rq, cq], preferred_element_type=F32).astype(BF16)
            bui_ref[:, cq] = jnp.dot(ub[:, rq], bim_ref[rq, cq], preferred_element_type=F32).astype(BF16)

    return pl.pallas_call(
        body, name="mix_in", grid=(t // tb,),
        out_shape=(jax.ShapeDtypeStruct((t, ns * nc), BF16), jax.ShapeDtypeStruct((t, nstate), BF16),
                   jax.ShapeDtypeStruct((t, nstate), BF16), jax.ShapeDtypeStruct((t, d), BF16)),
        in_specs=[_rows(tb, d), _full((SUBLANES, d)), _resident(w_in_st.shape), _resident(b_re.shape),
                  _resident(b_im.shape)],
        out_specs=(_rows(tb, ns * nc), _rows(tb, nstate), _rows(tb, nstate), _rows(tb, d)),
        compiler_params=_cparams(("parallel",), VMEM_BIG),
    )(x, vec, w_in_st, b_re, b_im)


def _head_ms(y, h_ref):
    return _split_dot(y * y, h_ref[...])


def _conv3(x, halo, w_ref):
    return w_ref[0:1, :] * _shift_down(x, halo, 2) + w_ref[1:2, :] * _shift_down(x, halo, 1) + w_ref[2:3, :] * x


def _mix_out(x, proj, s_re, s_im, c_re, c_im, v512, convw, glu_w, h16, h64, w_out, vd):
    t, d = x.shape
    dh = c_re.shape[1]
    nstate = s_re.shape[1]
    du, ds = dh // SSM_SPLIT, nstate // SSM_SPLIT
    tb = _blk(t, TB_MIX)

    def body(x_ref, u_ref, bg_ref, cg_ref, v_ref, cgh_ref, vh_ref, sr_ref, si_ref, cre_ref, cim_ref, p_ref,
             cw_ref, gw_ref, h16_ref, h64_ref, wo_ref, vd_ref, y1_ref, o_ref, x2_ref):
        i = pl.program_id(0)
        u = u_ref[...].astype(F32)
        ys = []
        for q in range(SSM_SPLIT):
            rq, cq = slice(q * ds, (q + 1) * ds), slice(q * du, (q + 1) * du)
            ys.append(_dot(sr_ref[:, rq], cre_ref[rq, cq]) - _dot(si_ref[:, rq], cim_ref[rq, cq]))
        ys = jnp.concatenate(ys, axis=1)
        y1 = ys + p_ref[0:1, :] * u
        y1_ref[...] = y1
        z = _gelu(y1)
        q = _dot(z, gw_ref[...]) + p_ref[1:2, :]
        ya = z * _sigmoid(q)
        na = ya * lax.rsqrt(_head_ms(ya, h16_ref) + EPS) * p_ref[2:3, :]
        cv = cg_ref[...].astype(F32) * v_ref[...].astype(F32)
        cvh = jnp.where(i > 0, cgh_ref[...].astype(F32) * vh_ref[...].astype(F32), 0.0)
        yb = bg_ref[...].astype(F32) * _conv3(cv, cvh, cw_ref)
        nb = yb * lax.rsqrt(_head_ms(yb, h64_ref) + EPS) * p_ref[3:4, :]
        o = _dot(na, wo_ref[0:dh, :]) + _dot(nb, wo_ref[dh:2 * dh, :])
        o_ref[...] = o
        on = o * lax.rsqrt(_rowmean(o * o) + EPS) * vd_ref[0:1, :]
        x2_ref[...] = x_ref[...] + vd_ref[1:2, :] * on

    return pl.pallas_call(
        body, name="mix_out", grid=(t // tb,),
        out_shape=(jax.ShapeDtypeStruct((t, dh), F32), jax.ShapeDtypeStruct((t, d), F32),
                   jax.ShapeDtypeStruct((t, d), F32)),
        in_specs=[_rows(tb, d), _rows(tb, dh, 0), _rows(tb, dh, 1), _rows(tb, dh, 2), _rows(tb, dh, 3),
                  _halo_prev(tb, dh, 2, BF16_ROWS), _halo_prev(tb, dh, 3, BF16_ROWS), _rows(tb, nstate), _rows(tb, nstate),
                  _full(c_re.shape), _full(c_im.shape), _full(v512.shape), _full(convw.shape), _full(glu_w.shape),
                  _full(h16.shape), _full(h64.shape), _full(w_out.shape), _full(vd.shape)],
        out_specs=(_rows(tb, dh), _rows(tb, d), _rows(tb, d)),
        compiler_params=_cparams(("parallel",), VMEM_BIG),
    )(x, proj, proj, proj, proj, proj, proj, s_re, s_im, c_re, c_im, v512, convw, glu_w, h16, h64, w_out, vd)


def _ffn_up(x2, vec, w_up_st):
    t, d = x2.shape
    ns, _, nc = w_up_st.shape
    tb = _blk(t, TB_FFN_UP)

    def body(x_ref, vec_ref, w_ref, up_ref, h2_ref):
        xv = x_ref[...]
        r = lax.rsqrt(_rowmean(xv * xv) + EPS)
        h = xv * r * vec_ref[0:1, :] * vec_ref[1:2, :] + vec_ref[2:3, :]
        hb = h.astype(BF16)
        h2_ref[...] = hb
        for j in range(ns):
            up_ref[:, j * nc:(j + 1) * nc] = jnp.dot(hb, w_ref[j], preferred_element_type=F32)

    return pl.pallas_call(
        body, name="ffn_up", grid=(t // tb,),
        out_shape=(jax.ShapeDtypeStruct((t, ns * nc), F32), jax.ShapeDtypeStruct((t, d), BF16)),
        in_specs=[_rows(tb, d), _full((SUBLANES, d)), _resident(w_up_st.shape)],
        out_specs=(_rows(tb, ns * nc), _rows(tb, d)),
        compiler_params=_cparams(("parallel",), VMEM_BIG),
    )(x2, vec, w_up_st)


def _ffn_down(up, fw, w_down, w_down_t, x2, tgt, vd):
    t, nh = up.shape
    dff, d = w_down.shape
    tb = _blk(t, TB_FFN)
    inv_d = 1.0 / d

    def body(up_ref, uph_ref, fw_ref, wd_ref, wdt_ref, x2_ref, tgt_ref, vd_ref,
             act_ref, ddn_ref, dout_ref, dhid_ref, vec_ref, loss_ref, a_s, vv_s, sg_s):
        i = pl.program_id(0)

        def conv_cols(sl):
            x = up_ref[:, sl]
            halo = jnp.where(i > 0, uph_ref[:, sl], 0.0)
            return (fw_ref[0:1, sl] * _shift_down(x, halo, 2) + fw_ref[1:2, sl] * _shift_down(x, halo, 1)
                    + fw_ref[2:3, sl] * x)

        dn = None
        for o in range(0, dff, CW_FFN):
            sl = slice(o, o + CW_FFN)
            a = conv_cols(sl)
            vv = conv_cols(slice(dff + o, dff + o + CW_FFN))
            sg = _sigmoid(a)
            si = a * sg
            a_s[:, sl] = si
            vv_s[:, sl] = vv
            sg_s[:, sl] = sg
            actb = (si * vv).astype(BF16)
            act_ref[:, sl] = actb
            pj = lax.dot_general(actb, wdt_ref[:, sl], (((1,), (1,)), ((), ())), preferred_element_type=F32)
            dn = pj if dn is None else dn + pj
        r3 = lax.rsqrt(_rowmean(dn * dn) + EPS)
        xn = dn * r3
        g = vd_ref[0:1, :]
        gt2 = vd_ref[1:2, :]
        dnn = xn * g
        diff = x2_ref[...] + gt2 * dnn - tgt_ref[...]
        part = 0.5 * inv_d * jnp.sum(diff * diff)

        @pl.when(i == 0)
        def _():
            loss_ref[...] = jnp.zeros(loss_ref.shape, F32)
        loss_ref[...] += part
        dout = diff * inv_d
        dout_ref[...] = dout
        ddnn = dout * gt2
        _acc_rows(vec_ref, i == 0, [_colsum(dout * dnn), _colsum(ddnn * xn)])
        dxn = ddnn * g
        ddn = r3 * (dxn - xn * _rowmean(dxn * xn))
        ddnb = ddn.astype(BF16)
        ddn_ref[...] = ddnb
        for o in range(0, dff, CW_FFN):
            sl = slice(o, o + CW_FFN)
            dact = lax.dot_general(ddnb, wd_ref[sl, :], (((1,), (1,)), ((), ())), preferred_element_type=F32)
            si, vv, sg = a_s[:, sl], vv_s[:, sl], sg_s[:, sl]
            dhid_ref[:, sl] = (dact * vv * (sg + si * (1.0 - sg))).astype(BF16)
            dhid_ref[:, dff + o:dff + o + CW_FFN] = (dact * si).astype(BF16)

    return pl.pallas_call(
        body, name="ffn_down", grid=(t // tb,),
        scratch_shapes=[pltpu.VMEM((tb, dff), F32)] * 3,
        out_shape=(jax.ShapeDtypeStruct((t, dff), BF16), jax.ShapeDtypeStruct((t, d), BF16),
                   jax.ShapeDtypeStruct((t, d), F32), jax.ShapeDtypeStruct((t, nh), BF16),
                   jax.ShapeDtypeStruct((SUBLANES, d), F32), jax.ShapeDtypeStruct((SUBLANES, 128), F32)),
        in_specs=[_rows(tb, nh), _halo_prev(tb, nh), _full(fw.shape), _resident(w_down.shape),
                  _resident(w_down_t.shape), _rows(tb, d),
                  _rows(tb, d), _full(vd.shape)],
        out_specs=(_rows(tb, dff), _rows(tb, d), _rows(tb, d), _rows(tb, nh), _full((SUBLANES, d)),
                   _full((SUBLANES, 128))),
        compiler_params=_cparams(("arbitrary",), VMEM_BIG),
    )(up, up, fw, w_down, w_down_t, x2, tgt, vd)


def _ffn_up_bwd(dhid, up, fw, x2, dout, vec, w_up_st):
    t, nh = dhid.shape
    d = x2.shape[1]
    ns, _, nc = w_up_st.shape
    tb = _blk(t, TB_FFN)
    nblk = t // tb
    cw = 128

    def body(dh_ref, dhn_ref, up_ref, fw_ref, x2_ref, dout_ref, vec_ref, w_ref,
             dx2_ref, dup_ref, vp_ref, df_ref):
        i = pl.program_id(0)

        @pl.when(i == 0)
        def _():
            df_ref[...] = jnp.zeros(df_ref.shape, F32)
        dh2 = None
        for j in range(ns):
            for o in range(j * nc, (j + 1) * nc, cw):
                sl = slice(o, o + cw)
                dh = dh_ref[:, sl].astype(F32)
                dhn = jnp.where(i < nblk - 1, dhn_ref[:, sl].astype(F32), 0.0)
                dh1 = _shift_up(dh, dhn, 1)
                dh2s = _shift_up(dh, dhn, 2)
                dup_ref[:, sl] = (fw_ref[2:3, sl] * dh + fw_ref[1:2, sl] * dh1 + fw_ref[0:1, sl] * dh2s).astype(BF16)
                up_v = up_ref[:, sl]
                df_ref[0:1, sl] += _colsum(dh2s * up_v)
                df_ref[1:2, sl] += _colsum(dh1 * up_v)
                df_ref[2:3, sl] += _colsum(dh * up_v)
            pj = lax.dot_general(dup_ref[:, j * nc:(j + 1) * nc], w_ref[j], (((1,), (1,)), ((), ())),
                                 preferred_element_type=F32)
            dh2 = pj if dh2 is None else dh2 + pj
        xv = x2_ref[...]
        r = lax.rsqrt(_rowmean(xv * xv) + EPS)
        xn = xv * r
        g = vec_ref[0:1, :]
        hg = xn * g
        dhg = dh2 * vec_ref[1:2, :]
        _acc_rows(vp_ref, i == 0, [_colsum(dh2), _colsum(dh2 * hg), _colsum(dhg * xn)])
        dxn = dhg * g
        dx2_ref[...] = dout_ref[...] + r * (dxn - xn * _rowmean(dxn * xn))

    return pl.pallas_call(
        body, name="ffn_up_bwd", grid=(nblk,),
        out_shape=(jax.ShapeDtypeStruct((t, d), F32), jax.ShapeDtypeStruct((t, nh), BF16),
                   jax.ShapeDtypeStruct((SUBLANES, d), F32), jax.ShapeDtypeStruct((SUBLANES, nh), F32)),
        in_specs=[_rows(tb, nh), _halo_next(tb, nh, t, rows=BF16_ROWS), _rows(tb, nh), _full(fw.shape),
                  _rows(tb, d), _rows(tb, d), _full(vec.shape), _resident(w_up_st.shape)],
        out_specs=(_rows(tb, d), _rows(tb, nh), _full((SUBLANES, d)), _full((SUBLANES, nh))),
        compiler_params=_cparams(("arbitrary",), VMEM_BIG),
    )(dhid, dhid, up, fw, x2, dout, vec, w_up_st)


def _mix_out_bwd(dx2, o, y1, proj, s_re, s_im, c_re, c_im, v512, convw, glu_w, h16, h64, w_out, vd):
    t, d = dx2.shape
    dh = y1.shape[1]
    nstate = c_re.shape[0]
    du, ds = dh // SSM_SPLIT, nstate // SSM_SPLIT
    tb = _blk(t, TB_MIX)

    def body(dx2_ref, o_ref, y1_ref, u_ref, bg_ref, cg_ref, v_ref, cgh_ref, vh_ref, cre_ref, cim_ref, p_ref,
             cw_ref, gw_ref, h16_ref, h64_ref, wo_ref, vd_ref, sr_ref, si_ref,
             do_ref, ycat_ref, z_ref, dq_ref, dy1_ref, gr_ref, gi_ref, dcc_ref, dbg_ref, vpd_ref, vp5_ref,
             dcr_ref, dci_ref):
        i = pl.program_id(0)
        first = i == 0

        @pl.when(first)
        def _():
            dcr_ref[...] = jnp.zeros(dcr_ref.shape, F32)
            dci_ref[...] = jnp.zeros(dci_ref.shape, F32)
        ov = o_ref[...]
        ro = lax.rsqrt(_rowmean(ov * ov) + EPS)
        on_ = ov * ro
        g = vd_ref[0:1, :]
        dx2v = dx2_ref[...]
        don = dx2v * vd_ref[1:2, :]
        _acc_rows(vpd_ref, first, [_colsum(dx2v * on_ * g), _colsum(don * on_)])
        dxn = don * g
        dob = (ro * (dxn - on_ * _rowmean(dxn * on_))).astype(BF16)
        do_ref[...] = dob
        dyc_a =lax.dot_general(dob, wo_ref[0:dh, :], (((1,), (1,)), ((), ())), preferred_element_type=F32)
        dyc_b = lax.dot_general(dob, wo_ref[dh:2 * dh, :], (((1,), (1,)), ((), ())), preferred_element_type=F32)
        y1v = y1_ref[...]
        u = u_ref[...].astype(F32)
        z, dz_dy1 = _gelu_and_grad(y1v)
        zb = z.astype(BF16)
        sg = _sigmoid(jnp.dot(zb, gw_ref[...], preferred_element_type=F32) + p_ref[1:2, :])
        ya = z * sg
        ra = lax.rsqrt(_head_ms(ya, h16_ref) + EPS)
        yan = ya * ra
        ga = p_ref[2:3, :]
        ycat_ref[:, 0:dh] = (yan * ga).astype(BF16)
        dyn = dyc_a * ga
        dya = ra * (dyn - yan * _split_dot(dyn * yan, h16_ref[...]))
        dq = dya * z * sg * (1.0 - sg)
        dqb = dq.astype(BF16)
        z_ref[...] = zb
        dq_ref[...] = dqb
        dz = dya * sg + lax.dot_general(dqb, gw_ref[...], (((1,), (1,)), ((), ())), preferred_element_type=F32)
        dy1 = dz * dz_dy1
        dy1_ref[...] = dy1
        dy1b = dy1.astype(BF16)
        for q in range(SSM_SPLIT):
            rq, cq = slice(q * ds, (q + 1) * ds), slice(q * du, (q + 1) * du)
            gr_ref[:, rq] = lax.dot_general(dy1b[:, cq], cre_ref[rq, cq], (((1,), (1,)), ((), ())),
                                            preferred_element_type=F32).astype(BF16)
            gi_ref[:, rq] = (-lax.dot_general(dy1b[:, cq], cim_ref[rq, cq], (((1,), (1,)), ((), ())),
                                              preferred_element_type=F32)).astype(BF16)
            dcr_ref[rq, :] += _dot_tn(sr_ref[:, rq], dy1b[:, cq])
            dci_ref[rq, :] += _dot_tn(si_ref[:, rq], dy1b[:, cq])
        bg = bg_ref[...].astype(F32)
        cv = cg_ref[...].astype(F32) * v_ref[...].astype(F32)
        cvh = jnp.where(i > 0, cgh_ref[...].astype(F32) * vh_ref[...].astype(F32), 0.0)
        cv1 = _shift_down(cv, cvh, 1)
        cv2 = _shift_down(cv, cvh, 2)
        cc = cw_ref[0:1, :] * cv2 + cw_ref[1:2, :] * cv1 + cw_ref[2:3, :] * cv
        yb = bg * cc
        rb = lax.rsqrt(_head_ms(yb, h64_ref) + EPS)
        ybn = yb * rb
        gb = p_ref[3:4, :]
        ycat_ref[:, dh:2 * dh] = (ybn * gb).astype(BF16)
        dynb = dyc_b * gb
        dyb = rb * (dynb - ybn * _split_dot(dynb * ybn, h64_ref[...]))
        dcc = dyb * bg
        dbg_ref[...] = dyb * cc
        dcc_ref[...] = dcc
        _acc_rows(vp5_ref, first, [_colsum(dyc_a * yan), _colsum(dyc_b * ybn), _colsum(dq), _colsum(dy1 * u),
                                   _colsum(dcc * cv2), _colsum(dcc * cv1), _colsum(dcc * cv)])

    return pl.pallas_call(
        body, name="mix_out_bwd", grid=(t // tb,),
        out_shape=(jax.ShapeDtypeStruct((t, d), BF16), jax.ShapeDtypeStruct((t, 2 * dh), BF16),
                   jax.ShapeDtypeStruct((t, dh), BF16), jax.ShapeDtypeStruct((t, dh), BF16),
                   jax.ShapeDtypeStruct((t, dh), F32), jax.ShapeDtypeStruct((t, nstate), BF16),
                   jax.ShapeDtypeStruct((t, nstate), BF16), jax.ShapeDtypeStruct((t, dh), F32),
                   jax.ShapeDtypeStruct((t, dh), F32), jax.ShapeDtypeStruct((SUBLANES, d), F32),
                   jax.ShapeDtypeStruct((SUBLANES, dh), F32), jax.ShapeDtypeStruct((nstate, du), F32),
                   jax.ShapeDtypeStruct((nstate, du), F32)),
        in_specs=[_rows(tb, d), _rows(tb, d), _rows(tb, dh), _rows(tb, dh, 0), _rows(tb, dh, 1), _rows(tb, dh, 2),
                  _rows(tb, dh, 3), _halo_prev(tb, dh, 2, BF16_ROWS), _halo_prev(tb, dh, 3, BF16_ROWS), _resident(c_re.shape),
                  _resident(c_im.shape), _full(v512.shape), _full(convw.shape), _resident(glu_w.shape),
                  _resident(h16.shape), _resident(h64.shape), _resident(w_out.shape), _full(vd.shape),
                  _rows(tb, nstate), _rows(tb, nstate)],
        out_specs=(_rows(tb, d), _rows(tb, 2 * dh), _rows(tb, dh), _rows(tb, dh), _rows(tb, dh), _rows(tb, nstate),
                   _rows(tb, nstate), _rows(tb, dh), _rows(tb, dh), _full((SUBLANES, d)), _full((SUBLANES, dh)),
                   _full((nstate, du)), _full((nstate, du))),
        compiler_params=_cparams(("arbitrary",), VMEM_BIG),
    )(dx2, o, y1, proj, proj, proj, proj, proj, proj, c_re, c_im, v512, convw, glu_w, h16, h64, w_out, vd,
      s_re, s_im)


def _mix_in_bwd(gt_re, gt_im, b_re, b_im, dy1, dcc, dbg, proj, x, dx2, vec, v512, convw, w_in_st):
    t, d = x.shape
    dh = dy1.shape[1]
    nstate = gt_re.shape[1]
    du_w, ds = dh // SSM_SPLIT, nstate // SSM_SPLIT
    ns, _, nc = w_in_st.shape
    tb = _blk(t, TB_MIX)
    nblk = t // tb

    def body(gr_ref, gi_ref, bre_ref, bim_ref, dy1_ref, dcc_ref, dccn_ref, dbg_ref, u_ref, cg_ref, v_ref, x_ref,
             dx2_ref, vec_ref, p_ref, cw_ref, w_ref, gx_ref, dproj_ref, vp_ref, dbr_ref, dbi_ref):
        i = pl.program_id(0)

        @pl.when(i == 0)
        def _():
            dbr_ref[...] = jnp.zeros(dbr_ref.shape, F32)
            dbi_ref[...] = jnp.zeros(dbi_ref.shape, F32)
        ub = u_ref[...].astype(BF16)
        du = []
        for q in range(SSM_SPLIT):
            rq, cq = slice(q * du_w, (q + 1) * du_w), slice(q * ds, (q + 1) * ds)
            du.append(lax.dot_general(gr_ref[:, cq].astype(BF16), bre_ref[rq, cq], (((1,), (1,)), ((), ())),
                                      preferred_element_type=F32)
                      + lax.dot_general(gi_ref[:, cq].astype(BF16), bim_ref[rq, cq], (((1,), (1,)), ((), ())),
                                        preferred_element_type=F32))
            dbr_ref[rq, :] += _dot_tn(ub[:, rq], gr_ref[:, cq])
            dbi_ref[rq, :] += _dot_tn(ub[:, rq], gi_ref[:, cq])
        du = dy1_ref[...] * p_ref[0:1, :] + jnp.concatenate(du, axis=1)
        dcc = dcc_ref[...]
        dccn = jnp.where(i < nblk - 1, dccn_ref[...], 0.0)
        dcv = (cw_ref[2:3, :] * dcc + cw_ref[1:2, :] * _shift_up(dcc, dccn, 1)
               + cw_ref[0:1, :] * _shift_up(dcc, dccn, 2))
        parts = [du, dbg_ref[...], dcv * v_ref[...].astype(F32), dcv * cg_ref[...].astype(F32)]
        xv = x_ref[...]
        r = lax.rsqrt(_rowmean(xv * xv) + EPS)
        xn = xv * r
        g = vec_ref[0:1, :]
        hg = xn * g
        dh1 = None
        for j in range(ns):
            pb = parts[j].astype(BF16)
            dproj_ref[:, j * nc:(j + 1) * nc] = pb
            pj =lax.dot_general(pb, w_ref[j], (((1,), (1,)), ((), ())), preferred_element_type=F32)
            dh1 = pj if dh1 is None else dh1 + pj
        dhg = dh1 * vec_ref[1:2, :]
        _acc_rows(vp_ref, i == 0, [_colsum(dh1), _colsum(dh1 * hg), _colsum(dhg * xn)])
        dxn = dhg * g
        gx_ref[...] = dx2_ref[...] + r * (dxn - xn * _rowmean(dxn * xn))

    assert nc == dh and ns == 4
    return pl.pallas_call(
        body, name="mix_in_bwd", grid=(nblk,),
        out_shape=(jax.ShapeDtypeStruct((t, d), F32), jax.ShapeDtypeStruct((t, ns * nc), BF16),
                   jax.ShapeDtypeStruct((SUBLANES, d), F32), jax.ShapeDtypeStruct((dh, ds), F32),
                   jax.ShapeDtypeStruct((dh, ds), F32)),
        in_specs=[_rows(tb, nstate), _rows(tb, nstate), _resident(b_re.shape), _resident(b_im.shape), _rows(tb, dh),
                  _rows(tb, dh), _halo_next(tb, dh, t), _rows(tb, dh), _rows(tb, dh, 0), _rows(tb, dh, 2),
                  _rows(tb, dh, 3), _rows(tb, d), _rows(tb, d), _full(vec.shape), _full(v512.shape),
                  _full(convw.shape), _resident(w_in_st.shape)],
        out_specs=(_rows(tb, d), _rows(tb, ns * nc), _full((SUBLANES, d)), _full((dh, ds)), _full((dh, ds))),
        compiler_params=_cparams(("arbitrary",), VMEM_BIG),
    )(gt_re, gt_im, b_re, b_im, dy1, dcc, dcc, dbg, proj, proj, proj, x, dx2, vec, v512, convw, w_in_st)


def _matmul_tn(a, b, m, bn, out_dtype, name, diag=False, bt=TB_TN, after=None):
    t = a.shape[0]
    n = b.shape[1]
    bt = _blk(t, bt)
    nk = t // bt
    extra = [] if after is None else [after]
    a_map = (lambda j, k: (k, j)) if diag else (lambda j, k: (k, 0))

    def body(a_ref, b_ref, *rest):
        o_ref, acc_ref = rest[-2:]
        k = pl.program_id(1)

        @pl.when(k == 0)
        def _():
            acc_ref[...] = jnp.zeros(acc_ref.shape, F32)
        acc_ref[...] += _dot_tn(a_ref[...], b_ref[...])

        @pl.when(k == nk - 1)
        def _():
            o_ref[...] = acc_ref[...].astype(out_dtype)

    return pl.pallas_call(
        body, name=name, grid=(n // bn, nk),
        out_shape=jax.ShapeDtypeStruct((n // bn, m, bn), out_dtype),
        in_specs=[pl.BlockSpec((bt, m), a_map), pl.BlockSpec((bt, bn), lambda j, k: (k, j))]
        + [pl.BlockSpec(memory_space=pl.ANY)] * len(extra),
        out_specs=pl.BlockSpec((None, m, bn), lambda j, k: (j, 0, 0)),
        scratch_shapes=[pltpu.VMEM((m, bn), F32)],
        compiler_params=_cparams(("parallel", "arbitrary"), VMEM_BIG),
    )(a, b, *extra)


def _ssm_bgrad(d_bre, d_bim, bt_re, bt_im, rows_in, fold, tile_b):
    gh, cb = d_bre.shape
    nb = SSM_SPLIT
    rb = gh // nb
    gp = nb * cb
    p = fold.shape[1]

    def body(dr_ref, di_ref, br_ref, bi_ref, rin_ref, f_ref, tb_ref, dbr_ref, dbi_ref, rout_ref):
        row = lax.broadcasted_iota(jnp.int32, (rb, cb), 0)
        col = lax.broadcasted_iota(jnp.int32, (rb, cb), 1)
        mask = (row >> 4) == (col >> 6)
        gr = jnp.where(mask, dr_ref[...], 0.0)
        gi = jnp.where(mask, di_ref[...], 0.0)
        cr, ci = rin_ref[0:1, :], rin_ref[1:2, :]
        dbr_ref[...] = _split3_dot(cr * gr + ci * gi, f_ref[...])
        dbi_ref[...] = _split3_dot(cr * gi - ci * gr, f_ref[...])
        br = _split3_dot(br_ref[...], tb_ref[...])
        bi = _split3_dot(bi_ref[...], tb_ref[...])
        rout_ref[...] = jnp.zeros(rout_ref.shape, F32)
        rout_ref[0:1, :] = _colsum(br * gr + bi * gi)
        rout_ref[1:2, :] = _colsum(br * gi - bi * gr)

    dspec = pl.BlockSpec((rb, cb), lambda j: (j, 0))
    rspec = pl.BlockSpec((SUBLANES, cb), lambda j: (0, j))
    ospec = pl.BlockSpec((rb, p), lambda j: (j, 0))
    return pl.pallas_call(
        body, name="ssm_bgrad", grid=(nb,),
        out_shape=(jax.ShapeDtypeStruct((gh, p), F32), jax.ShapeDtypeStruct((gh, p), F32),
                   jax.ShapeDtypeStruct((SUBLANES, gp), F32)),
        in_specs=[dspec, dspec, ospec, ospec, rspec, _full(fold.shape), _full(tile_b.shape)],
        out_specs=(ospec, ospec, rspec),
        compiler_params=_cparams(("parallel",)),
    )(d_bre, d_bim, bt_re, bt_im, rows_in, fold, tile_b)


def _ssm_cgrad(d_cre, d_cim, fold):
    gp, cb = d_cre.shape
    nb = SSM_SPLIT
    rb = gp // nb
    h = fold.shape[1]

    def body(dr_ref, di_ref, f_ref, cr_ref, ci_ref):
        row = lax.broadcasted_iota(jnp.int32, (rb, cb), 0)
        col = lax.broadcasted_iota(jnp.int32, (rb, cb), 1)
        mask = (row >> 6) == (col >> 4)
        cr_ref[...] = _split3_dot(jnp.where(mask, dr_ref[...], 0.0), f_ref[...])
        ci_ref[...] = -_split3_dot(jnp.where(mask, di_ref[...], 0.0), f_ref[...])

    cspec = pl.BlockSpec((rb, cb), lambda j: (j, 0))
    ospec = pl.BlockSpec((rb, h), lambda j: (j, 0))
    return pl.pallas_call(
        body, name="ssm_cgrad", grid=(nb,),
        out_shape=(jax.ShapeDtypeStruct((gp, h), F32),) * 2,
        in_specs=[cspec, cspec, _full(fold.shape)], out_specs=(ospec, ospec),
        compiler_params=_cparams(("parallel",)),
    )(d_cre, d_cim, fold)


def _ssm_lamgrad(lam_re, lam_im, log_step, abar_re, abar_im, coef_re, coef_im, gc_re, gc_im, ga_re, ga_im):
    g, p = lam_re.shape

    def body(lr_ref, li_ref, ls_ref, ar_ref, ai_ref, cr_ref, ci_ref, gcr_ref, gci_ref, gar_ref, gai_ref,
             dlr_ref, dli_ref, dls_ref):
        lam_raw = lr_ref[...]
        lr = jnp.minimum(lam_raw, LAMBDA_RE_MAX)
        li = li_ref[...]
        st = jnp.exp(ls_ref[...])
        den = lr * lr + li * li
        gcr, gci = gcr_ref[...], gci_ref[...]
        gab_r = gar_ref[...] + (lr * gcr - li * gci) / den
        gab_i = gai_ref[...] + (lr * gci + li * gcr) / den
        cr, ci = cr_ref[...], ci_ref[...]
        wr = -(cr * lr + ci * li) / den
        wi = -(ci * lr - cr * li) / den
        gl_r = wr * gcr + wi * gci
        gl_i = wr * gci - wi * gcr
        ar, ai = ar_ref[...], ai_ref[...]
        gw_r = ar * gab_r + ai * gab_i
        gw_i = ar * gab_i - ai * gab_r
        gl_r = gl_r + st * gw_r
        gl_i = gl_i + st * gw_i
        pass_through = jnp.where(lam_raw < LAMBDA_RE_MAX, 1.0, jnp.where(lam_raw == LAMBDA_RE_MAX, 0.5, 0.0))
        dlr_ref[...] = gl_r * pass_through
        dli_ref[...] = gl_i
        dls_ref[...] = st * jnp.sum(lr * gw_r + li * gw_i, axis=1, keepdims=True)

    sds = jax.ShapeDtypeStruct((g, p), F32)
    return pl.pallas_call(body, name="ssm_lamgrad", out_shape=(sds, sds, jax.ShapeDtypeStruct((g, 1), F32)))(
        lam_re, lam_im, log_step, abar_re, abar_im, coef_re, coef_im, gc_re, gc_im, ga_re, ga_im)


def _row_block(r, most=512):
    for rb in range(min(r, most), BF16_ROWS - 1, -1):
        if r % rb == 0 and rb % BF16_ROWS == 0:
            return rb
    return r


def _adamw_math(w, g, m, v):
    m = ADAM_B1 * m + (1.0 - ADAM_B1) * g
    v = ADAM_B2 * v + (1.0 - ADAM_B2) * (g * g)
    m_hat = m / (1.0 - ADAM_B1 ** ADAM_STEP)
    v_hat = v / (1.0 - ADAM_B2 ** ADAM_STEP)
    delta = -ADAM_LR * (m_hat / (jnp.sqrt(v_hat) + ADAM_EPS) + ADAM_WD * w)
    return delta, m, v


def _adamw_big(p_mine, p_sib, w, m, v, name):
    r, c = w.shape
    rb = _row_block(r)

    def body(a_ref, b_ref, w_ref, m_ref, v_ref, g_ref, d_ref, mo_ref, vo_ref):
        g = a_ref[...].astype(F32) + b_ref[...].astype(F32)
        g_ref[...] = g
        d_ref[...], mo_ref[...], vo_ref[...] = _adamw_math(w_ref[...], g, m_ref[...], v_ref[...])

    spec = pl.BlockSpec((rb, c), lambda i: (i, 0))
    sds = jax.ShapeDtypeStruct((r, c), F32)
    return pl.pallas_call(
        body, name=name, grid=(r // rb,), out_shape=(sds,) * 4, in_specs=[spec] * 5, out_specs=(spec,) * 4,
        compiler_params=_cparams(("parallel",), VMEM_KEEP_OPERANDS_IN_HBM),
    )(p_mine, p_sib, w, m, v)


def _sum_blocks(stack, name):
    n, r, c = stack.shape
    rb = _row_block(r)

    def body(s_ref, o_ref):
        acc = s_ref[0].astype(F32)
        for k in range(1, n):
            acc = acc + s_ref[k].astype(F32)
        o_ref[...] = acc

    return pl.pallas_call(
        body, name=name, grid=(r // rb,), out_shape=jax.ShapeDtypeStruct((r, c), F32),
        in_specs=[pl.BlockSpec((n, rb, c), lambda i: (0, i, 0))], out_specs=pl.BlockSpec((rb, c), lambda i: (i, 0)),
        compiler_params=_cparams(("parallel",), VMEM_KEEP_OPERANDS_IN_HBM),
    )(stack)


def _sum_landed(landed, own, chip, name):
    n, r, c = landed.shape
    rb = _row_block(r)

    def body(chip_ref, own_ref, l1_ref, l2_ref, l3_ref, o_ref):
        acc = own_ref[0].astype(F32)
        for ref in (l1_ref, l2_ref, l3_ref):
            acc = acc + ref[0].astype(F32)
        o_ref[...] = acc.astype(BF16)

    def slot(k):
        return pl.BlockSpec((1, rb, c), lambda i, ch: ((ch[0] + k) % n, i, 0))

    return pl.pallas_call(
        body, name=name, out_shape=jax.ShapeDtypeStruct((r, c), BF16),
        grid_spec=pltpu.PrefetchScalarGridSpec(
            num_scalar_prefetch=1, grid=(r // rb,), in_specs=[slot(0), slot(1), slot(2), slot(3)],
            out_specs=pl.BlockSpec((rb, c), lambda i, ch: (i, 0))),
        compiler_params=_cparams(("parallel",), VMEM_KEEP_OPERANDS_IN_HBM),
    )(jnp.reshape(chip, (1,)).astype(jnp.int32), own, landed, landed, landed)


def _add2(a, b):
    def body(a_ref, b_ref, o_ref):
        o_ref[...] = a_ref[...] + b_ref[...]

    return pl.pallas_call(body, name="add_small", out_shape=jax.ShapeDtypeStruct(a.shape, F32))(a, b)


def _adamw_ada(c_all, dmod_cols, w, m, v):
    d, n = w.shape
    bn = 512

    def body(c_ref, dm_ref, w_ref, m_ref, v_ref, g_ref, d_ref, mo_ref, vo_ref):
        cc = c_ref[...]
        g = _dot_tn(cc * _sigmoid(cc), dm_ref[...])
        g_ref[...] = g
        d_ref[...], mo_ref[...], vo_ref[...] = _adamw_math(w_ref[...], g, m_ref[...], v_ref[...])

    spec = pl.BlockSpec((d, bn), lambda j: (0, j))
    sds = jax.ShapeDtypeStruct((d, n), F32)
    return pl.pallas_call(
        body, name="adamw_ada", grid=(n // bn,), out_shape=(sds,) * 4,
        in_specs=[_full((N_DEV, d)), pl.BlockSpec((N_DEV, bn), lambda j: (0, j)), spec, spec, spec],
        out_specs=(spec,) * 4, compiler_params=_cparams(("parallel",), VMEM_KEEP_OPERANDS_IN_HBM),
    )(c_all, dmod_cols, w, m, v)


def _adamw_small(items):
    n = len(items)

    def body(*refs):
        ins, outs = refs[:4 * n], refs[4 * n:]
        for k in range(n):
            w_ref, g_ref, m_ref, v_ref = ins[4 * k:4 * k + 4]
            outs[3 * k][...], outs[3 * k + 1][...], outs[3 * k + 2][...] = _adamw_math(
                w_ref[...], g_ref[...], m_ref[...], v_ref[...])

    flat = [a for it in items for a in it]
    out_shape = tuple(jax.ShapeDtypeStruct(it[0].shape, F32) for it in items for _ in range(3))
    res = pl.pallas_call(body, name="adamw_small", out_shape=out_shape,
                         compiler_params=_cparams(vmem=VMEM_KEEP_OPERANDS_IN_HBM))(*flat)
    return [tuple(res[3 * k:3 * k + 3]) for k in range(n)]


def _group_mean_matrix(n, group):
    idx = np.arange(n) // group
    return (idx[:, None] == idx[None, :]).astype(np.float32) / group


def _fold_matrix(n, period):
    return (np.arange(n)[:, None] % period == np.arange(period)[None, :]).astype(np.float32)


def _rows8(*rows):
    c = rows[0].shape[-1]
    pad = jnp.zeros((SUBLANES - len(rows), c), F32)
    return jnp.concatenate([r.reshape(1, c) for r in rows] + [pad], axis=0)


def _to_rows(a, width):
    flat = a.reshape(-1)
    n = -(-flat.shape[0] // width)
    flat = jnp.pad(flat, (0, n * width - flat.shape[0]))
    return flat.reshape(n, width)


def kernel(x, c, w_ada, b_ada, g_pre_mix, g_post_mix, w_in, ssm_lam_re, ssm_lam_im, ssm_log_step, ssm_b_re, ssm_b_im, ssm_c_re, ssm_c_im, ssm_d, glu_w, glu_b, g_out_ssm, conv_w, g_out_conv, w_out, g_pre_ffn, g_post_ffn, w_up, ffn_conv_w, w_down, loss_target, m_w_ada, m_b_ada, m_g_pre_mix, m_g_post_mix, m_w_in, m_ssm_lam_re, m_ssm_lam_im, m_ssm_log_step, m_ssm_b_re, m_ssm_b_im, m_ssm_c_re, m_ssm_c_im, m_ssm_d, m_glu_w, m_glu_b, m_g_out_ssm, m_conv_w, m_g_out_conv, m_w_out, m_g_pre_ffn, m_g_post_ffn, m_w_up, m_ffn_conv_w, m_w_down, v_w_ada, v_b_ada, v_g_pre_mix, v_g_post_mix, v_w_in, v_ssm_lam_re, v_ssm_lam_im, v_ssm_log_step, v_ssm_b_re, v_ssm_b_im, v_ssm_c_re, v_ssm_c_im, v_ssm_d, v_glu_w, v_glu_b, v_g_out_ssm, v_conv_w, v_g_out_conv, v_w_out, v_g_pre_ffn, v_g_post_ffn, v_w_up, v_ffn_conv_w, v_w_down):
    xs = x[0]
    tgt = loss_target[0]
    t, d = xs.shape
    xi, yi, ci = lax.axis_index("x"), lax.axis_index("y"), lax.axis_index("c")
    chip = 2 * xi + yi
    dev = 2 * chip + ci

    n_groups, n_state = ssm_lam_re.shape[1:]
    n_gch = ssm_b_re.shape[3]
    d_ssm = n_groups * n_gch
    gp = n_groups * n_state
    n_ada = w_ada.shape[2]
    d_ff = w_down.shape[1] * N_CHIPS
    n_upc = w_up.shape[2]

    w_names = ("w_in", "glu_w", "w_out", "w_up", "w_down")
    c_gath = _allgather8(jnp.broadcast_to(c, (SUBLANES, d)), "gather_c")
    c_all = c_gath.reshape(N_DEV, SUBLANES, d)[:, 0, :]

    def pad8(a):
        return jnp.concatenate([a, jnp.zeros((SUBLANES - a.shape[0], a.shape[1]), a.dtype)], axis=0)

    def start(name, arrs, after):
        return _chips_start(name, True, [], [_landing(a, chip) for a in arrs], after)

    w_names = ("w_in", "mod", "conv_w", "ffn_conv_w", "glu_w", "w_out", "w_up", "w_down")
    first = start("weights_start_in", [w_in[0].astype(BF16)], c_gath)
    b_sh = lax.dynamic_slice(b_ada, (0, chip * n_ada), (1, n_ada))
    mod_sh = _mod_shard(c_all + first[4][0:1, 0:1], w_ada[0], b_sh)
    second = start("weights_start_rest", [mod_sh, pad8(conv_w[0]), pad8(ffn_conv_w[0])]
                   + [w[0].astype(BF16) for w in (glu_w, w_out, w_up, w_down)], None)
    w_send, w_recv, w_land = [list(first[k]) + list(second[k]) for k in (0, 1, 3)]
    w_token = second[4]

    def weights(names, after):
        ks = [w_names.index(nm) for nm in names]
        return _chips_wait("weights_wait_" + names[-1], True, [w_send[k] for k in ks], [w_recv[k] for k in ks],
                           [], [w_land[k] for k in ks], after)[1]

    lam_re, lam_im = ssm_lam_re[0], ssm_lam_im[0]
    log_step = ssm_log_step[0].reshape(n_groups, 1) + w_token[0:1, 0:1]
    abar_re, abar_im, coef_re, coef_im = _ssm_prep(lam_re, lam_im, log_step)
    a_rows = _rows8(abar_re.reshape(1, gp), abar_im.reshape(1, gp))
    coef_rows = _rows8(coef_re.reshape(1, gp), coef_im.reshape(1, gp))
    bt_re = ssm_b_re[0].transpose(0, 2, 1).reshape(d_ssm, n_state)
    bt_im = ssm_b_im[0].transpose(0, 2, 1).reshape(d_ssm, n_state)
    ct_re = ssm_c_re[0].transpose(0, 2, 1).reshape(gp, n_gch)
    ct_im = ssm_c_im[0].transpose(0, 2, 1).reshape(gp, n_gch)
    tile_b = jnp.asarray(np.tile(np.eye(n_state), (1, n_groups // SSM_SPLIT)), BF16)
    tile_c = jnp.asarray(np.tile(np.eye(n_gch), (1, n_groups)), BF16)
    bblk_re, bblk_im, cblk_re, cblk_im = _ssm_blocks(bt_re, bt_im, ct_re, ct_im, coef_rows, tile_b, tile_c)

    h16 = jnp.asarray(_group_mean_matrix(d_ssm, n_gch), BF16)
    h64 = jnp.asarray(_group_mean_matrix(d_ssm, CONV_HEAD_DIM), BF16)

    g_mod, g_cw, g_fw, w_in_st = weights(("mod", "conv_w", "ffn_conv_w", "w_in"), bblk_re)
    mod_all = g_mod.transpose(1, 0, 2).reshape(N_DEV, N_CHIPS * n_ada)
    mod = lax.dynamic_slice(mod_all, (dev, 0), (1, N_CHIPS * n_ada))
    sh1, sc1, gt1, sh2, sc2, gt2 = [mod[:, k * d:(k + 1) * d] for k in range(6)]
    convw_full = pad8(g_cw[:, :3, :].transpose(1, 0, 2).reshape(3, d_ssm))
    fw_full = pad8(g_fw[:, :3, :].transpose(1, 0, 2).reshape(3, N_CHIPS * n_upc))

    v512 = _rows8(ssm_d, glu_b, g_out_ssm, g_out_conv)
    vec1 =_rows8(g_pre_mix, 1.0 + sc1, sh1)
    vd1 = _rows8(g_post_mix, gt1)
    vec2 = _rows8(g_pre_ffn, 1.0 + sc2, sh2)
    vd2 = _rows8(g_post_ffn, gt2)

    proj, bu_re, bu_im, h1b = _mix_in(xs, vec1, w_in_st, bblk_re, bblk_im)
    s_re, s_im = _scan_fwd(a_rows, bu_re, bu_im)
    g_glu, g_wout = weights(("glu_w", "w_out"), s_re)
    glu_full = g_glu.reshape(d_ssm, d_ssm)
    w_out_full = g_wout.reshape(2 * d_ssm, d)
    y1, o_mix, x2 = _mix_out(xs, proj, s_re, s_im, cblk_re, cblk_im, v512, convw_full, glu_full, h16, h64,
                             w_out_full, vd1)
    (w_up_st,) = weights(("w_up",), x2)
    up, h2b = _ffn_up(x2, vec2, w_up_st)
    (g_wdown,) = weights(("w_down",), up)
    w_down_full = g_wdown.reshape(d_ff, d)
    actb, ddnb, dout, dhid, vp_dn, loss_blk = _ffn_down(up, fw_full, w_down_full, w_down_full.T, x2, tgt, vd2)

    gw_down = _matmul_tn(actb, ddnb, d_ff, d, BF16, "dw_down", bt=1024).reshape(N_CHIPS, d_ff // N_CHIPS, d)
    dx2, dupb, vp_up, df_rows = _ffn_up_bwd(dhid, up, fw_full, x2, dout, vec2, w_up_st)
    gw_up = _matmul_tn(h2b, dupb, d, n_upc, BF16, "dw_up", bt=2048)
    ga_send, ga_recv, ga_src, ga_land, ga_token = _chips_start(
        "grads_start_ffn", False, [gw_down, gw_up], [lax.empty(g.shape, g.dtype) for g in (gw_down, gw_up)])
    (dob, ycatb, zb, dqb, dy1, g_re, g_im, dcc, dbg, vp_mo, vp5, d_cre, d_cim) = _mix_out_bwd(
        dx2, o_mix, y1, proj, s_re, s_im, cblk_re, cblk_im, v512, convw_full, glu_full, h16, h64, w_out_full,
        vd1 + ga_token[0:1, 0:1])
    gw_out = _matmul_tn(ycatb, dob, 2 * d_ssm, d, BF16, "dw_out", bt=2048)
    gw_out = gw_out.reshape(N_CHIPS, 2 * d_ssm // N_CHIPS, d)
    gw_glu = _matmul_tn(zb, dqb, d_ssm, d_ssm, BF16, "dw_glu", bt=2048).reshape(N_CHIPS, d_ssm // N_CHIPS, d_ssm)
    gb_send, gb_recv, gb_src, gb_land, gb_token = _chips_start(
        "grads_start_mix", False, [gw_out, gw_glu], [lax.empty(g.shape, g.dtype) for g in (gw_out, gw_glu)])
    gt_re, gt_im, ga_re8, ga_im8 = _scan_bwd(a_rows + gb_token[0:1, 0:1], g_re, g_im, s_re, s_im)
    grad_x, dprojb, vp_mi, d_bre, d_bim = _mix_in_bwd(gt_re, gt_im, bblk_re, bblk_im, dy1, dcc, dbg, proj, xs, dx2,
                                                      vec1, v512, convw_full, w_in_st)
    ssm_u, ssm_s = d_ssm // SSM_SPLIT, gp // SSM_SPLIT

    fold_b = jnp.asarray(_fold_matrix(ssm_s, n_state), BF16)
    fold_c = jnp.asarray(_fold_matrix(ssm_u, n_gch), BF16)
    db_re_f, db_im_f, gc_rows = _ssm_bgrad(d_bre, d_bim, bt_re, bt_im, coef_rows, fold_b, tile_b)
    dc_re_f, dc_im_f = _ssm_cgrad(d_cre, d_cim, fold_c)
    ga_sum = _ga_rowsum(ga_re8, ga_im8)
    g_lam_re, g_lam_im, g_log_step = _ssm_lamgrad(
        lam_re, lam_im, log_step, abar_re, abar_im, coef_re, coef_im,
        gc_rows[0].reshape(n_groups, n_state), gc_rows[1].reshape(n_groups, n_state),
        ga_sum[0].reshape(n_groups, n_state), ga_sum[1].reshape(n_groups, n_state))
    g_b_re = db_re_f.reshape(n_groups, n_gch, n_state).transpose(0, 2, 1)
    g_b_im = db_im_f.reshape(n_groups, n_gch, n_state).transpose(0, 2, 1)
    g_c_re = dc_re_f.reshape(n_groups, n_state, n_gch).transpose(0, 2, 1)
    g_c_im = dc_im_f.reshape(n_groups, n_state, n_gch).transpose(0, 2, 1)

    dmod = jnp.concatenate([vp_mi[0:1], vp_mi[1:2], vp_mo[0:1], vp_up[0:1], vp_up[1:2], vp_dn[0:1]], axis=1)
    small = [
        ("g_pre_mix", vp_mi[2:3]), ("g_post_mix", vp_mo[1:2]), ("g_pre_ffn", vp_up[2:3]), ("g_post_ffn", vp_dn[1:2]),
        ("ssm_lam_re", g_lam_re), ("ssm_lam_im", g_lam_im), ("ssm_log_step", g_log_step),
        ("ssm_b_re", g_b_re), ("ssm_b_im", g_b_im), ("ssm_c_re", g_c_re), ("ssm_c_im", g_c_im),
        ("ssm_d", vp5[3:4]), ("glu_b", vp5[2:3]), ("g_out_ssm", vp5[0:1]), ("g_out_conv", vp5[1:2]),
        ("conv_w", vp5[4:7]), ("ffn_conv_w", df_rows[0:3]), ("loss", loss_blk[0:1, 0:1]),
    ]
    packed, offsets, row = [], {}, 0
    for name, a in small:
        r = _to_rows(a, d)
        offsets[name] = (row, a.shape)
        packed.append(r)
        row += r.shape[0]
    n_small = -(-row // SUBLANES) * SUBLANES
    packed.append(jnp.zeros((n_small - row, d), F32))
    packed.append(pad8(dmod.reshape(6, d)))
    pack = jnp.concatenate(packed, axis=0)
    sm_send, sm_recv, _, sm_land, sm_token = _chips_start("small_start", True, [], [_landing(pack, chip)])

    gw_in = _matmul_tn(h1b, dprojb, d, w_in.shape[2], BF16, "dw_in", bt=2048, after=sm_token)
    gc_send, gc_recv, gc_src, gc_land, gc_token = _chips_start(
        "grads_start_in", False, [gw_in], [lax.empty(gw_in.shape, gw_in.dtype)])

    def partials(names, own, landed):
        return [_sum_landed(l, o, chip, "sum_" + nm) for l, o, nm in zip(landed, own, names)]

    def update(names, mine, theirs):
        done = {}
        for nm, pm, ps in zip(names, mine, theirs):
            w_, m_, v_ = big_params[nm]
            done[nm] = _adamw_big(pm, ps, w_[0], m_[0], v_[0], "adamw_" + nm)
        return done

    big_params = {"w_down": (w_down, m_w_down, v_w_down), "w_up": (w_up, m_w_up, v_w_up),
                  "w_out": (w_out, m_w_out, v_w_out), "glu_w": (glu_w, m_glu_w, v_glu_w),
                  "w_in": (w_in, m_w_in, v_w_in)}
    ffn_names, mix_names = ("w_down", "w_up"), ("w_out", "glu_w", "w_in")
    p_ffn = partials(ffn_names, *_chips_wait("grads_wait_ffn", False, ga_send, ga_recv, ga_src, ga_land, gc_token))
    sa_send, sa_recv, sa_src, sa_land, sa_token = _sibling_start("swap_start_ffn", p_ffn)

    (sm_landed,) = _chips_wait("small_wait", True, sm_send, sm_recv, [], sm_land, sa_token)[1]
    sm_part = _sum_blocks(sm_landed, "sum_small")
    dmod_mine = sm_landed[:, n_small:n_small + SUBLANES, :]
    ss_send, ss_recv, ss_src, ss_land, ss_token = _sibling_start("swap_start_small", [sm_part, dmod_mine])
    p_ffn, t_ffn = _sibling_wait("swap_wait_ffn", sa_send, sa_recv, sa_src, sa_land, ss_token)
    big = update(ffn_names, p_ffn, t_ffn)
    (sm_part, dmod_mine), (sm_sib, dmod_sib) = _sibling_wait("swap_wait_small", ss_send, ss_recv, ss_src, ss_land,
                                                              big["w_up"][0])
    sums = _add2(sm_part, sm_sib)
    dmod_by_core = jnp.stack([dmod_mine, dmod_sib], axis=1)
    dmod_by_core = jnp.where(ci == 0, dmod_by_core, dmod_by_core[:, ::-1])
    dmod_all = dmod_by_core[:, :, :6, :].reshape(N_DEV, 6 * d)
    g_b_ada = sums[n_small:n_small + 6].reshape(1, 6 * d)

    def unpack(name):
        r0, shape = offsets[name]
        size = math.prod(shape)
        nrow = -(-size // d)
        return sums[r0:r0 + nrow].reshape(-1)[:size].reshape(shape)

    p_mix = partials(mix_names, *_chips_wait(
        "grads_wait_mix", False, list(gb_send) + list(gc_send), list(gb_recv) + list(gc_recv),
        list(gb_src) + list(gc_src), list(gb_land) + list(gc_land), sums))
    sb_send, sb_recv, sb_src, sb_land, sb_token = _sibling_start("swap_start_mix", p_mix)

    dmod_cols = lax.dynamic_slice(dmod_all, (0, chip * n_ada), (N_DEV, n_ada)) + sb_token[0:1, 0:1]
    ada = _adamw_ada(c_all, dmod_cols, w_ada[0], m_w_ada[0], v_w_ada[0])
    p_mix, t_mix = _sibling_wait("swap_wait_mix", sb_send, sb_recv, sb_src, sb_land, ada[0])
    big.update(update(mix_names, p_mix, t_mix))

    g_small = {name: unpack(name) for name, _ in small}
    g_small["b_ada"] = g_b_ada
    g_small["conv_w"] = lax.dynamic_slice(g_small["conv_w"], (0, chip * conv_w.shape[2]), (3, conv_w.shape[2]))
    g_small["ffn_conv_w"] = lax.dynamic_slice(g_small["ffn_conv_w"], (0, chip * n_upc), (3, n_upc))
    g_small["ssm_log_step"] = g_small["ssm_log_step"].reshape(1, n_groups)
    small_params = {
        "b_ada": (b_ada, m_b_ada, v_b_ada), "g_pre_mix": (g_pre_mix, m_g_pre_mix, v_g_pre_mix),
        "g_post_mix": (g_post_mix, m_g_post_mix, v_g_post_mix), "ssm_lam_re": (ssm_lam_re, m_ssm_lam_re, v_ssm_lam_re),
        "ssm_lam_im": (ssm_lam_im, m_ssm_lam_im, v_ssm_lam_im),
        "ssm_log_step": (ssm_log_step, m_ssm_log_step, v_ssm_log_step),
        "ssm_b_re": (ssm_b_re, m_ssm_b_re, v_ssm_b_re), "ssm_b_im": (ssm_b_im, m_ssm_b_im, v_ssm_b_im),
        "ssm_c_re": (ssm_c_re, m_ssm_c_re, v_ssm_c_re), "ssm_c_im": (ssm_c_im, m_ssm_c_im, v_ssm_c_im),
        "ssm_d": (ssm_d, m_ssm_d, v_ssm_d), "glu_b": (glu_b, m_glu_b, v_glu_b),
        "g_out_ssm": (g_out_ssm, m_g_out_ssm, v_g_out_ssm), "conv_w": (conv_w, m_conv_w, v_conv_w),
        "g_out_conv": (g_out_conv, m_g_out_conv, v_g_out_conv), "g_pre_ffn": (g_pre_ffn, m_g_pre_ffn, v_g_pre_ffn),
        "g_post_ffn": (g_post_ffn, m_g_post_ffn, v_g_post_ffn),
        "ffn_conv_w": (ffn_conv_w, m_ffn_conv_w, v_ffn_conv_w),
    }

    def natural(a):
        return a[0] if a.ndim > 2 else a

    names = list(small_params)
    items = []
    for nm in names:
        w_, m_, v_ = small_params[nm]
        items.append((natural(w_), g_small[nm].reshape(natural(w_).shape), natural(m_), natural(v_)))
    upd = _adamw_small(items)
    small_out = {}
    for nm, (dl, mo, vo) in zip(names, upd):
        shp = small_params[nm][0].shape
        small_out[nm] = (g_small[nm].reshape(shp), dl.reshape(shp), mo.reshape(shp), vo.reshape(shp))

    loss = g_small["loss"][0, 0]

    order = ["w_ada", "b_ada", "g_pre_mix", "g_post_mix", "w_in", "ssm_lam_re", "ssm_lam_im", "ssm_log_step",
             "ssm_b_re", "ssm_b_im", "ssm_c_re", "ssm_c_im", "ssm_d", "glu_w", "glu_b", "g_out_ssm", "conv_w",
             "g_out_conv", "w_out", "g_pre_ffn", "g_post_ffn", "w_up", "ffn_conv_w", "w_down"]
    results = {"w_ada": tuple(a[None] for a in ada)}
    for nm in big:
        results[nm] = tuple(a[None] for a in big[nm])
    results.update(small_out)
    outs = [loss, grad_x[None]]
    for k in range(4):
        outs += [results[nm][k] for nm in order]
    return tuple(outs)


def _ga_rowsum(ga_re8, ga_im8):
    n = ga_re8.shape[1]

    def body(r_ref, i_ref, o_ref):
        o_ref[...] = jnp.zeros(o_ref.shape, F32)
        o_ref[0:1, :] = _colsum(r_ref[...])
        o_ref[1:2, :] = _colsum(i_ref[...])

    return pl.pallas_call(body, name="ga_rowsum", out_shape=jax.ShapeDtypeStruct((SUBLANES, n), F32))(ga_re8, ga_im8)
```

```python
import functools
import math

import jax
import jax.numpy as jnp
import numpy as np
from jax import lax
from jax.experimental import pallas as pl
from jax.experimental.pallas import tpu as pltpu

F32 = jnp.float32
BF16 = jnp.bfloat16
MESH = pl.DeviceIdType.MESH

EPS = 1e-6
LAMBDA_RE_MAX = -1e-4
ADAM_LR = 0.001
ADAM_B1 = 0.9
ADAM_B2 = 0.999
ADAM_EPS = 1e-08
ADAM_WD = 0.01
ADAM_STEP = 10

SUBLANES = 8
BF16_ROWS = 16
N_CHIPS = 4
N_DEV = 8
CONV_HEAD_DIM = 64
VMEM_BIG = 56 * 1024 * 1024
VMEM_MID = 40 * 1024 * 1024
VMEM_KEEP_OPERANDS_IN_HBM = 62 * 1024 * 1024

TB_MIX = 256
TB_FFN = 256
TB_FFN_UP = 512
TB_SCAN = 4096
W_SCAN = 256
SSM_SPLIT = 4
CW_FFN = 256
SCAN_UNROLL = 4
TB_TN = 512


def _cparams(sem=None, vmem=None):
    kw = {}
    if sem is not None:
        kw["dimension_semantics"] = sem
    if vmem is not None:
        kw["vmem_limit_bytes"] = vmem
    return pltpu.CompilerParams(**kw)


def _blk(t, pref):
    return pref if t % pref == 0 else t


def _dot(a, b):
    return jnp.dot(a.astype(BF16), b.astype(BF16), preferred_element_type=F32)


def _dot_nt(a, b):
    return lax.dot_general(a.astype(BF16), b.astype(BF16), (((1,), (1,)), ((), ())),
                           preferred_element_type=F32)


def _dot_tn(a, b):
    return lax.dot_general(a.astype(BF16), b.astype(BF16), (((0,), (0,)), ((), ())),
                           preferred_element_type=F32)


def _sigmoid(x):
    return 0.5 * jnp.tanh(0.5 * x) + 0.5


_GELU_K = math.sqrt(2.0 / math.pi)
_GELU_C = 0.044715


def _gelu(x):
    th = jnp.tanh(_GELU_K * (x + _GELU_C * x * x * x))
    return x * (0.5 * (1.0 + th))


def _gelu_and_grad(x):
    x2 = x * x
    th = jnp.tanh(_GELU_K * (x + _GELU_C * x2 * x))
    half = 0.5 * (1.0 + th)
    return x * half, half + 0.5 * x * (1.0 - th * th) * _GELU_K * (1.0 + 3.0 * _GELU_C * x2)


def _rowmean(x):
    return jnp.mean(x, axis=-1, keepdims=True)


def _colsum(x):
    return jnp.sum(x, axis=0, keepdims=True)


def _split_dot(x, m):
    hi = x.astype(BF16)
    lo = (x - hi.astype(F32)).astype(BF16)
    return (jnp.dot(hi, m, preferred_element_type=F32) + jnp.dot(lo, m, preferred_element_type=F32))


def _split3_dot(x, m):
    hi = x.astype(BF16)
    r1 = x - hi.astype(F32)
    mid = r1.astype(BF16)
    lo = (r1 - mid.astype(F32)).astype(BF16)
    return (jnp.dot(hi, m, preferred_element_type=F32) + jnp.dot(mid, m, preferred_element_type=F32)
            + jnp.dot(lo, m, preferred_element_type=F32))


def _shift_down(x, halo, k):
    r = pltpu.roll(x, k, 0)
    row = lax.broadcasted_iota(jnp.int32, x.shape, 0)
    last = halo.shape[0]
    for j in range(k):
        r = jnp.where(row == j, halo[last - k + j:last - k + j + 1, :], r)
    return r


def _shift_up(x, halo, k):
    n = x.shape[0]
    r = pltpu.roll(x, n - k, 0)
    row = lax.broadcasted_iota(jnp.int32, x.shape, 0)
    for j in range(k):
        r = jnp.where(row == n - k + j, halo[j:j + 1, :], r)
    return r


def _acc_rows(ref, first, rows):
    @pl.when(first)
    def _():
        ref[...] = jnp.zeros(ref.shape, ref.dtype)
    for j, r in enumerate(rows):
        ref[j:j + 1, :] += r


def _rows(tb, c, col=0):
    return pl.BlockSpec((tb, c), lambda i, col=col: (i, col))


def _full(shape):
    nd = len(shape)
    return pl.BlockSpec(shape, lambda i, nd=nd: (0,) * nd)


def _resident(shape):
    nd = len(shape)
    return pl.BlockSpec(shape, lambda i, nd=nd: (0,) * nd, pipeline_mode=pl.Buffered(1))


def _halo_prev(tb, c, col=0, rows=SUBLANES):
    per = tb // rows
    return pl.BlockSpec((rows, c), lambda i, col=col: (jnp.maximum(i * per - 1, 0), col))


def _halo_next(tb, c, t, col=0, rows=SUBLANES):
    per = tb // rows
    last = t // rows - 1
    return pl.BlockSpec((rows, c), lambda i, col=col: (jnp.minimum((i + 1) * per, last), col))


def _mesh_pos():
    return lax.axis_index("x"), lax.axis_index("y"), lax.axis_index("c")


def _allgather8(x_pad, name):
    m_per, n = x_pad.shape

    def body(x_ref, out_ref, send_sems, recv_sems, local_sem):
        x, y, c = _mesh_pos()
        me, sibling = (x, y, c), (x, y, 1 - c)
        chips = [(1 - x, y), (x, 1 - y), (1 - x, 1 - y)]

        def rows(px, py, pc):
            return out_ref.at[pl.ds((4 * px + 2 * py + pc) * m_per, m_per), :]

        def copy(k, block, to, src=None):
            return pltpu.make_async_remote_copy(
                src_ref=rows(*block) if src is None else src, dst_ref=rows(*block),
                send_sem=send_sems.at[k], recv_sem=recv_sems.at[k], device_id=to, device_id_type=MESH)

        mine = pltpu.make_async_copy(x_ref, rows(*me), local_sem)
        mine.start()
        first = [copy(0, me, sibling, src=x_ref)]
        first += [copy(1 + j, me, (*chip, c), src=x_ref) for j, chip in enumerate(chips)]
        for cp in first:
            cp.start()
        passed = [copy(4 + j, (*chip, c), sibling) for j, chip in enumerate(chips)]
        for j, chip in enumerate(chips):
            copy(1 + j, (*chip, c), me).wait_recv()
            passed[j].start()
        copy(0, sibling, me).wait_recv()
        for j, chip in enumerate(chips):
            copy(4 + j, (*chip, 1 - c), me).wait_recv()
        for cp in first + passed:
            cp.wait_send()
        mine.wait()

    return pl.pallas_call(
        body, name=name,
        out_shape=jax.ShapeDtypeStruct((N_DEV * m_per, n), F32),
        in_specs=[pl.BlockSpec(memory_space=pltpu.VMEM)],
        out_specs=pl.BlockSpec(memory_space=pltpu.VMEM),
        scratch_shapes=[pltpu.SemaphoreType.DMA((7,)), pltpu.SemaphoreType.DMA((7,)), pltpu.SemaphoreType.DMA],
    )(x_pad)


_HBM = pl.BlockSpec(memory_space=pltpu.HBM)
_SEM = pl.BlockSpec(memory_space=pltpu.SEMAPHORE)
_EFFECT = pltpu.SideEffectType.DATAFLOW_SIDE_EFFECTING


def _chip_copy(gather, src_ref, land_ref, send, recv, j, arrival):
    x, y, c = _mesh_pos()
    peer = [(1 - x, y), (x, 1 - y), (1 - x, 1 - y)][j]
    peer_chip = 2 * peer[0] + peer[1]
    my_chip = 2 * x + y
    return pltpu.make_async_remote_copy(
        src_ref=land_ref.at[my_chip] if gather else src_ref.at[peer_chip],
        dst_ref=land_ref.at[peer_chip if arrival else my_chip],
        send_sem=send.at[j], recv_sem=recv.at[j], device_id=(*peer, c), device_id_type=MESH)


def _chips_start(name, gather, srcs, lands, after=None):
    n, ns = len(lands), len(srcs)
    extra = [] if after is None else [after]

    def body(*refs):
        src_refs, land_refs = refs[:ns], refs[ns:ns + n]
        outs = refs[ns + n + len(extra):]
        sends, recvs, token = outs[:n], outs[n:2 * n], outs[-1]
        for k in range(n):
            for j in range(3):
                _chip_copy(gather, src_refs[k] if ns else None, land_refs[k], sends[k], recvs[k], j, False).start()
        token[...] = jnp.zeros(token.shape, F32)

    sem = pltpu.SemaphoreType.DMA((3,))
    thru = tuple(pltpu.HBM(a.shape, a.dtype) for a in list(srcs) + list(lands))
    res = pl.pallas_call(
        body, name=name,
        out_shape=(sem,) * (2 * n) + thru + (jax.ShapeDtypeStruct((SUBLANES, 128), F32),),
        in_specs=[_HBM] * (ns + n) + [pl.BlockSpec(memory_space=pl.ANY)] * len(extra),
        out_specs=(_SEM,) * (2 * n) + (_HBM,) * (ns + n) + (pl.BlockSpec(memory_space=pltpu.VMEM),),
        input_output_aliases={k: 2 * n + k for k in range(ns + n)},
        compiler_params=pltpu.CompilerParams(has_side_effects=_EFFECT),
    )(*[pltpu.with_memory_space_constraint(a, pltpu.HBM) for a in list(srcs) + list(lands)], *extra)
    return res[:n], res[n:2 * n], res[2 * n:2 * n + ns], res[2 * n + ns:2 * n + ns + n], res[-1]


def _chips_wait(name, gather, sends, recvs, srcs, lands, after):
    n, ns = len(lands), len(srcs)

    def body(*refs):
        src_refs, land_refs = refs[:ns], refs[ns:ns + n]
        sends_, recvs_ = refs[ns + n:ns + 2 * n], refs[ns + 2 * n:ns + 3 * n]
        for k in range(n):
            for j in range(3):
                cp = _chip_copy(gather, src_refs[k] if ns else None, land_refs[k], sends_[k], recvs_[k], j, True)
                cp.wait_send()
                cp.wait_recv()

    thru = tuple(pltpu.HBM(a.shape, a.dtype) for a in list(srcs) + list(lands))
    res = pl.pallas_call(
        body, name=name, out_shape=thru,
        in_specs=[_HBM] * (ns + n) + [_SEM] * (2 * n) + [pl.BlockSpec(memory_space=pl.ANY)],
        out_specs=(_HBM,) * (ns + n),
        input_output_aliases={k: k for k in range(ns + n)},
        compiler_params=pltpu.CompilerParams(has_side_effects=_EFFECT),
    )(*srcs, *lands, *sends, *recvs, after)
    return res[:ns], res[ns:]


def _sibling_copy(src_ref, land_ref, send, recv):
    x, y, c = _mesh_pos()
    return pltpu.make_async_remote_copy(src_ref=src_ref, dst_ref=land_ref, send_sem=send.at[0], recv_sem=recv.at[0],
                                        device_id=(x, y, 1 - c), device_id_type=MESH)


def _sibling_start(name, arrs, after=None):
    n = len(arrs)
    extra = [] if after is None else [after]
    lands = [lax.empty(a.shape, a.dtype) for a in arrs]

    def body(*refs):
        src_refs, land_refs = refs[:n], refs[n:2 * n]
        outs = refs[2 * n + len(extra):]
        sends, recvs, token = outs[:n], outs[n:2 * n], outs[-1]
        for k in range(n):
            _sibling_copy(src_refs[k], land_refs[k], sends[k], recvs[k]).start()
        token[...] = jnp.zeros(token.shape, F32)

    sem = pltpu.SemaphoreType.DMA((1,))
    thru = tuple(pltpu.HBM(a.shape, a.dtype) for a in list(arrs) + lands)
    res = pl.pallas_call(
        body, name=name,
        out_shape=(sem,) * (2 * n) + thru + (jax.ShapeDtypeStruct((SUBLANES, 128), F32),),
        in_specs=[_HBM] * (2 * n) + [pl.BlockSpec(memory_space=pl.ANY)] * len(extra),
        out_specs=(_SEM,) * (2 * n) + (_HBM,) * (2 * n) + (pl.BlockSpec(memory_space=pltpu.VMEM),),
        input_output_aliases={k: 2 * n + k for k in range(2 * n)},
        compiler_params=pltpu.CompilerParams(has_side_effects=_EFFECT),
    )(*[pltpu.with_memory_space_constraint(a, pltpu.HBM) for a in list(arrs) + lands], *extra)
    return res[:n], res[n:2 * n], res[2 * n:3 * n], res[3 * n:4 * n], res[-1]


def _sibling_wait(name, sends, recvs, srcs, lands, after):
    n = len(srcs)

    def body(*refs):
        src_refs, land_refs = refs[:n], refs[n:2 * n]
        sends_, recvs_ = refs[2 * n:3 * n], refs[3 * n:4 * n]
        for k in range(n):
            cp = _sibling_copy(src_refs[k], land_refs[k], sends_[k], recvs_[k])
            cp.wait_send()
            cp.wait_recv()

    thru = tuple(pltpu.HBM(a.shape, a.dtype) for a in list(srcs) + list(lands))
    res = pl.pallas_call(
        body, name=name, out_shape=thru,
        in_specs=[_HBM] * (2 * n) + [_SEM] * (2 * n) + [pl.BlockSpec(memory_space=pl.ANY)],
        out_specs=(_HBM,) * (2 * n),
        input_output_aliases={k: k for k in range(2 * n)},
        compiler_params=pltpu.CompilerParams(has_side_effects=_EFFECT),
    )(*srcs, *lands, *sends, *recvs, after)
    return res[:n], res[n:]


def _landing(own, chip):
    zone = lax.empty((N_CHIPS,) + own.shape, own.dtype)
    return lax.dynamic_update_slice(zone, own[None], (chip,) + (0,) * own.ndim)


def _mod_shard(c_all, w_ada_sh, b_sh):
    d, n = w_ada_sh.shape
    bn = 512

    def body(c_ref, w_ref, b_ref, o_ref):
        cc = c_ref[...]
        ca = cc * _sigmoid(cc)
        o_ref[...] = _dot(ca, w_ref[...]) + b_ref[...]

    return pl.pallas_call(
        body, name="mod_shard", grid=(n // bn,),
        out_shape=jax.ShapeDtypeStruct((N_DEV, n), F32),
        in_specs=[_full((N_DEV, d)), pl.BlockSpec((d, bn), lambda j: (0, j)), pl.BlockSpec((1, bn), lambda j: (0, j))],
        out_specs=pl.BlockSpec((N_DEV, bn), lambda j: (0, j)),
        compiler_params=_cparams(("parallel",)),
    )(c_all, w_ada_sh, b_sh)


def _ssm_prep(lam_re, lam_im, log_step):
    g, p = lam_re.shape

    def body(lr_ref, li_ref, ls_ref, ar_ref, ai_ref, cr_ref, ci_ref):
        lr = jnp.minimum(lr_ref[...], LAMBDA_RE_MAX)
        li = li_ref[...]
        st = jnp.exp(ls_ref[...])
        mag = jnp.exp(lr * st)
        ar = mag * jnp.cos(li * st)
        ai = mag * jnp.sin(li * st)
        den = lr * lr + li * li
        nr = ar - 1.0
        ar_ref[...] = ar
        ai_ref[...] = ai
        cr_ref[...] = (nr * lr + ai * li) / den
        ci_ref[...] = (ai * lr - nr * li) / den

    sds = jax.ShapeDtypeStruct((g, p), F32)
    return pl.pallas_call(body, name="ssm_prep", out_shape=(sds,) * 4)(lam_re, lam_im, log_step)


def _ssm_blocks(bt_re, bt_im, ct_re, ct_im, coef_rows, tile_b, tile_c):
    gh, p = bt_re.shape
    gp, h = ct_re.shape
    nb = SSM_SPLIT
    cb, rb = gp // nb, gp // nb

    def body(btr, bti, ctr, cti, cf, tb_ref, tc_ref, bre_o, bim_o, cre_o, cim_o):
        j = pl.program_id(0)
        row = lax.broadcasted_iota(jnp.int32, (gh, cb), 0)
        col = lax.broadcasted_iota(jnp.int32, (gh, cb), 1) + j * cb
        mask = (row >> 4) == (col >> 6)
        cr, ci = cf[0:1, :], cf[1:2, :]
        br = _split3_dot(btr[...], tb_ref[...])
        bi = _split3_dot(bti[...], tb_ref[...])
        bre_o[...] = jnp.where(mask, br * cr - bi * ci, 0.0).astype(BF16)
        bim_o[...] = jnp.where(mask, br * ci + bi * cr, 0.0).astype(BF16)
        row2 = lax.broadcasted_iota(jnp.int32, (rb, gh), 0) + j * rb
        col2 = lax.broadcasted_iota(jnp.int32, (rb, gh), 1)
        mask2 = (row2 >> 6) == (col2 >> 4)
        cre_o[...] = jnp.where(mask2, _split3_dot(ctr[...], tc_ref[...]), 0.0).astype(BF16)
        cim_o[...] = jnp.where(mask2, _split3_dot(cti[...], tc_ref[...]), 0.0).astype(BF16)

    bspec = pl.BlockSpec((gh, cb), lambda j: (0, j))
    cspec = pl.BlockSpec((rb, gh), lambda j: (j, 0))
    cin = pl.BlockSpec((rb, h), lambda j: (j, 0))
    return pl.pallas_call(
        body, name="ssm_blocks", grid=(nb,),
        out_shape=(jax.ShapeDtypeStruct((gh, gp), BF16),) * 2 + (jax.ShapeDtypeStruct((gp, gh), BF16),) * 2,
        in_specs=[_full((gh, p)), _full((gh, p)), cin, cin, pl.BlockSpec((SUBLANES, cb), lambda j: (0, j)),
                  _full(tile_b.shape), _full(tile_c.shape)],
        out_specs=(bspec, bspec, cspec, cspec),
        compiler_params=_cparams(("parallel",)),
    )(bt_re, bt_im, ct_re, ct_im, coef_rows, tile_b, tile_c)


def _scan_consts(a_ref, reverse):
    w = a_ref.shape[1]
    ar1 = a_ref[0:1, :]
    ai1 = a_ref[1:2, :]
    if reverse:
        ai1 = -ai1
    pr, pi = [ar1], [ai1]
    for _ in range(1, SUBLANES):
        nr = pr[-1] * ar1 - pi[-1] * ai1
        ni = pr[-1] * ai1 + pi[-1] * ar1
        pr.append(nr)
        pi.append(ni)
    row = lax.broadcasted_iota(jnp.int32, (SUBLANES, w), 0)
    dist = (SUBLANES - 1 - row) if reverse else row

    def pick(vals):
        out = jnp.broadcast_to(vals[SUBLANES - 1], (SUBLANES, w))
        for r in range(SUBLANES - 1):
            out = jnp.where(dist == r, vals[r], out)
        return out

    p_r, p_i = pick(pr), pick(pi)
    steps = []
    for k in (1, 2, 4):
        steps.append((k, jnp.where(dist >= k, pr[k - 1], 0.0), jnp.where(dist >= k, pi[k - 1], 0.0)))
    a8 = (jnp.broadcast_to(pr[SUBLANES - 1], (SUBLANES, w)), jnp.broadcast_to(pi[SUBLANES - 1], (SUBLANES, w)))
    return row, p_r, p_i, steps, a8


def _scan_tile(xr, xi, cr, ci, consts, reverse):
    row, p_r, p_i, steps, (a8r, a8i) = consts
    for k, s_r, s_i in steps:
        sh = (SUBLANES - k) if reverse else k
        qr = pltpu.roll(xr, sh, 0)
        qi = pltpu.roll(xi, sh, 0)
        xr, xi = xr + s_r * qr - s_i * qi, xi + s_r * qi + s_i * qr
    outr = xr + p_r * cr - p_i * ci
    outi = xi + p_r * ci + p_i * cr
    e = 0 if reverse else SUBLANES - 1
    er = jnp.broadcast_to(xr[e:e + 1, :], xr.shape)
    ei = jnp.broadcast_to(xi[e:e + 1, :], xi.shape)
    return outr, outi, er + a8r * cr - a8i * ci, ei + a8r * ci + a8i * cr


def _scan_fwd(a_rows, bu_re, bu_im):
    t, n = bu_re.shape
    tb, w = _blk(t, TB_SCAN), W_SCAN
    ntile = tb // SUBLANES

    def body(a_ref, br_ref, bi_ref, sr_ref, si_ref, car, cai):
        @pl.when(pl.program_id(1) == 0)
        def _():
            car[...] = jnp.zeros(car.shape, F32)
            cai[...] = jnp.zeros(cai.shape, F32)
        consts = _scan_consts(a_ref, False)

        def pair(i, carry):
            o = pl.multiple_of(i * BF16_ROWS, BF16_ROWS)
            b_r = br_ref[pl.ds(o, BF16_ROWS), :].astype(F32)
            b_i = bi_ref[pl.ds(o, BF16_ROWS), :].astype(F32)
            outs = []
            for h in range(2):
                rows = slice(h * SUBLANES, (h + 1) * SUBLANES)
                outr, outi, ncr, nci = _scan_tile(b_r[rows, :], b_i[rows, :], carry[0], carry[1], consts, False)
                outs.append((outr, outi))
                carry = (ncr, nci)
            sr_ref[pl.ds(o, BF16_ROWS), :] = jnp.concatenate([outs[0][0], outs[1][0]], axis=0).astype(BF16)
            si_ref[pl.ds(o, BF16_ROWS), :] = jnp.concatenate([outs[0][1], outs[1][1]], axis=0).astype(BF16)
            return carry

        def pairs(i, carry):
            for s in range(SCAN_UNROLL // 2):
                carry = pair(i * (SCAN_UNROLL // 2) + s, carry)
            return carry

        cr, ci = lax.fori_loop(0, ntile // SCAN_UNROLL, pairs, (car[...], cai[...]))
        car[...] = cr
        cai[...] = ci

    spec = pl.BlockSpec((tb, w), lambda s, k: (k, s))
    sds = jax.ShapeDtypeStruct((t, n), BF16)
    return pl.pallas_call(
        body, name="scan_fwd", grid=(n // w, t // tb), out_shape=(sds, sds),
        in_specs=[pl.BlockSpec((SUBLANES, w), lambda s, k: (0, s)), spec, spec], out_specs=(spec, spec),
        scratch_shapes=[pltpu.VMEM((SUBLANES, w), F32), pltpu.VMEM((SUBLANES, w), F32)],
        compiler_params=_cparams(("parallel", "arbitrary"), VMEM_MID),
    )(a_rows, bu_re, bu_im)


def _scan_bwd(a_rows, g_re, g_im, s_re, s_im):
    t, n = g_re.shape
    tb, w = _blk(t, TB_SCAN), W_SCAN
    ntile = tb // SUBLANES
    npair = tb // BF16_ROWS
    nt = t // tb

    def body(a_ref, gr_ref, gi_ref, sr_ref, si_ref, or_ref, oi_ref, gar_ref, gai_ref, car, cai):
        @pl.when(pl.program_id(1) == 0)
        def _():
            car[...] = jnp.zeros(car.shape, F32)
            cai[...] = jnp.zeros(cai.shape, F32)
            gar_ref[...] = jnp.zeros(gar_ref.shape, F32)
            gai_ref[...] = jnp.zeros(gai_ref.shape, F32)
        consts = _scan_consts(a_ref, True)
        row = consts[0]

        def pair(i, carry):
            cr, ci, accr, acci = carry
            o = pl.multiple_of((npair - 1 - i) * BF16_ROWS, BF16_ROWS)
            s_r = sr_ref[pl.ds(o, BF16_ROWS), :].astype(F32)
            s_i = si_ref[pl.ds(o, BF16_ROWS), :].astype(F32)
            g_r = gr_ref[pl.ds(o, BF16_ROWS), :].astype(F32)
            g_i = gi_ref[pl.ds(o, BF16_ROWS), :].astype(F32)
            outs = [None, None]
            for h in (1, 0):
                rows = slice(h * SUBLANES, (h + 1) * SUBLANES)
                outr, outi, ncr, nci = _scan_tile(g_r[rows, :], g_i[rows, :], cr, ci, consts, True)
                outs[h] = (outr, outi)
                gnr = jnp.where(row == SUBLANES - 1, cr, pltpu.roll(outr, SUBLANES - 1, 0))
                gni = jnp.where(row == SUBLANES - 1, ci, pltpu.roll(outi, SUBLANES - 1, 0))
                sr = s_r[h * SUBLANES:(h + 1) * SUBLANES, :]
                si = s_i[h * SUBLANES:(h + 1) * SUBLANES, :]
                accr, acci = accr + sr * gnr + si * gni, acci + sr * gni - si * gnr
                cr, ci = ncr, nci
            or_ref[pl.ds(o, BF16_ROWS), :] = jnp.concatenate([outs[0][0], outs[1][0]], axis=0).astype(BF16)
            oi_ref[pl.ds(o, BF16_ROWS), :] = jnp.concatenate([outs[0][1], outs[1][1]], axis=0).astype(BF16)
            return cr, ci, accr, acci

        def pairs(i, carry):
            for s in range(SCAN_UNROLL // 2):
                carry = pair(i * (SCAN_UNROLL // 2) + s, carry)
            return carry

        cr, ci, accr, acci = lax.fori_loop(0, ntile // SCAN_UNROLL, pairs,
                                           (car[...], cai[...], gar_ref[...], gai_ref[...]))
        car[...] = cr
        cai[...] = ci
        gar_ref[...] = accr
        gai_ref[...] = acci

    spec = pl.BlockSpec((tb, w), lambda s, k: (nt - 1 - k, s))
    aspec = pl.BlockSpec((SUBLANES, w), lambda s, k: (0, s))
    sds = jax.ShapeDtypeStruct((t, n), BF16)
    asds = jax.ShapeDtypeStruct((SUBLANES, n), F32)
    return pl.pallas_call(
        body, name="scan_bwd", grid=(n // w, nt), out_shape=(sds, sds, asds, asds),
        in_specs=[aspec, spec, spec, spec, spec], out_specs=(spec, spec, aspec, aspec),
        scratch_shapes=[pltpu.VMEM((SUBLANES, w), F32), pltpu.VMEM((SUBLANES, w), F32)],
        compiler_params=_cparams(("parallel", "arbitrary"), VMEM_MID),
    )(a_rows, g_re, g_im, s_re, s_im)


def _mix_in(x, vec, w_in_st, b_re, b_im):
    t, d = x.shape
    ns, _, nc = w_in_st.shape
    dssm, nstate = b_re.shape
    du, ds = dssm // SSM_SPLIT, nstate // SSM_SPLIT
    tb = _blk(t, TB_MIX)

    def body(x_ref, vec_ref, w_ref, bre_ref, bim_ref, proj_ref, bur_ref, bui_ref, h1_ref):
        xv = x_ref[...]
        r = lax.rsqrt(_rowmean(xv * xv) + EPS)
        h = xv * r * vec_ref[0:1, :] * vec_ref[1:2, :] + vec_ref[2:3, :]
        hb = h.astype(BF16)
        h1_ref[...] = hb
        u = None
        for j in range(ns):
            pj = jnp.dot(hb, w_ref[j], preferred_element_type=F32)
            proj_ref[:, j * nc:(j + 1) * nc] = pj.astype(BF16)
            if j == 0:
                u = pj
        ub = u.astype(BF16)
        for q in range(SSM_SPLIT):
            rq, cq = slice(q * du, (q + 1) * du), slice(q * ds, (q + 1) * ds)
            bur_ref[:, cq] = jnp.dot(ub[:, rq], bre_ref[rq, cq], preferred_element_type=F32).astype(BF16)
            bui_ref[:, cq] = jnp.dot(ub[:, rq], bim_ref[rq, cq], preferred_element_type=F32).astype(BF16)

    return pl.pallas_call(
        body, name="mix_in", grid=(t // tb,),
        out_shape=(jax.ShapeDtypeStruct((t, ns * nc), BF16), jax.ShapeDtypeStruct((t, nstate), BF16),
                   jax.ShapeDtypeStruct((t, nstate), BF16), jax.ShapeDtypeStruct((t, d), BF16)),
        in_specs=[_rows(tb, d), _full((SUBLANES, d)), _resident(w_in_st.shape), _resident(b_re.shape),
                  _resident(b_im.shape)],
        out_specs=(_rows(tb, ns * nc), _rows(tb, nstate), _rows(tb, nstate), _rows(tb, d)),
        compiler_params=_cparams(("parallel",), VMEM_BIG),
    )(x, vec, w_in_st, b_re, b_im)


def _head_ms(y, h_ref):
    return _split_dot(y * y, h_ref[...])


def _conv3(x, halo, w_ref):
    return w_ref[0:1, :] * _shift_down(x, halo, 2) + w_ref[1:2, :] * _shift_down(x, halo, 1) + w_ref[2:3, :] * x


def _mix_out(x, proj, s_re, s_im, c_re, c_im, v512, convw, glu_w, h16, h64, w_out, vd):
    t, d = x.shape
    dh = c_re.shape[1]
    nstate = s_re.shape[1]
    du, ds = dh // SSM_SPLIT, nstate // SSM_SPLIT
    tb = _blk(t, TB_MIX)

    def body(x_ref, u_ref, bg_ref, cg_ref, v_ref, cgh_ref, vh_ref, sr_ref, si_ref, cre_ref, cim_ref, p_ref,
             cw_ref, gw_ref, h16_ref, h64_ref, wo_ref, vd_ref, y1_ref, o_ref, x2_ref):
        i = pl.program_id(0)
        u = u_ref[...].astype(F32)
        ys = []
        for q in range(SSM_SPLIT):
            rq, cq = slice(q * ds, (q + 1) * ds), slice(q * du, (q + 1) * du)
            ys.append(_dot(sr_ref[:, rq], cre_ref[rq, cq]) - _dot(si_ref[:, rq], cim_ref[rq, cq]))
        ys = jnp.concatenate(ys, axis=1)
        y1 = ys + p_ref[0:1, :] * u
        y1_ref[...] = y1
        z = _gelu(y1)
        q = _dot(z, gw_ref[...]) + p_ref[1:2, :]
        ya = z * _sigmoid(q)
        na = ya * lax.rsqrt(_head_ms(ya, h16_ref) + EPS) * p_ref[2:3, :]
        cv = cg_ref[...].astype(F32) * v_ref[...].astype(F32)
        cvh = jnp.where(i > 0, cgh_ref[...].astype(F32) * vh_ref[...].astype(F32), 0.0)
        yb = bg_ref[...].astype(F32) * _conv3(cv, cvh, cw_ref)
        nb = yb * lax.rsqrt(_head_ms(yb, h64_ref) + EPS) * p_ref[3:4, :]
        o = _dot(na, wo_ref[0:dh, :]) + _dot(nb, wo_ref[dh:2 * dh, :])
        o_ref[...] = o
        on = o * lax.rsqrt(_rowmean(o * o) + EPS) * vd_ref[0:1, :]
        x2_ref[...] = x_ref[...] + vd_ref[1:2, :] * on

    return pl.pallas_call(
        body, name="mix_out", grid=(t // tb,),
        out_shape=(jax.ShapeDtypeStruct((t, dh), F32), jax.ShapeDtypeStruct((t, d), F32),
                   jax.ShapeDtypeStruct((t, d), F32)),
        in_specs=[_rows(tb, d), _rows(tb, dh, 0), _rows(tb, dh, 1), _rows(tb, dh, 2), _rows(tb, dh, 3),
                  _halo_prev(tb, dh, 2, BF16_ROWS), _halo_prev(tb, dh, 3, BF16_ROWS), _rows(tb, nstate), _rows(tb, nstate),
                  _full(c_re.shape), _full(c_im.shape), _full(v512.shape), _full(convw.shape), _full(glu_w.shape),
                  _full(h16.shape), _full(h64.shape), _full(w_out.shape), _full(vd.shape)],
        out_specs=(_rows(tb, dh), _rows(tb, d), _rows(tb, d)),
        compiler_params=_cparams(("parallel",), VMEM_BIG),
    )(x, proj, proj, proj, proj, proj, proj, s_re, s_im, c_re, c_im, v512, convw, glu_w, h16, h64, w_out, vd)


def _ffn_up(x2, vec, w_up_st):
    t, d = x2.shape
    ns, _, nc = w_up_st.shape
    tb = _blk(t, TB_FFN_UP)

    def body(x_ref, vec_ref, w_ref, up_ref, h2_ref):
        xv = x_ref[...]
        r = lax.rsqrt(_rowmean(xv * xv) + EPS)
        h = xv * r * vec_ref[0:1, :] * vec_ref[1:2, :] + vec_ref[2:3, :]
        hb = h.astype(BF16)
        h2_ref[...] = hb
        for j in range(ns):
            up_ref[:, j * nc:(j + 1) * nc] = jnp.dot(hb, w_ref[j], preferred_element_type=F32)

    return pl.pallas_call(
        body, name="ffn_up", grid=(t // tb,),
        out_shape=(jax.ShapeDtypeStruct((t, ns * nc), F32), jax.ShapeDtypeStruct((t, d), BF16)),
        in_specs=[_rows(tb, d), _full((SUBLANES, d)), _resident(w_up_st.shape)],
        out_specs=(_rows(tb, ns * nc), _rows(tb, d)),
        compiler_params=_cparams(("parallel",), VMEM_BIG),
    )(x2, vec, w_up_st)


def _ffn_down(up, fw, w_down, w_down_t, x2, tgt, vd):
    t, nh = up.shape
    dff, d = w_down.shape
    tb = _blk(t, TB_FFN)
    inv_d = 1.0 / d

    def body(up_ref, uph_ref, fw_ref, wd_ref, wdt_ref, x2_ref, tgt_ref, vd_ref,
             act_ref, ddn_ref, dout_ref, dhid_ref, vec_ref, loss_ref, a_s, vv_s, sg_s):
        i = pl.program_id(0)

        def conv_cols(sl):
            x = up_ref[:, sl]
            halo = jnp.where(i > 0, uph_ref[:, sl], 0.0)
            return (fw_ref[0:1, sl] * _shift_down(x, halo, 2) + fw_ref[1:2, sl] * _shift_down(x, halo, 1)
                    + fw_ref[2:3, sl] * x)

        dn = None
        for o in range(0, dff, CW_FFN):
            sl = slice(o, o + CW_FFN)
            a = conv_cols(sl)
            vv = conv_cols(slice(dff + o, dff + o + CW_FFN))
            sg = _sigmoid(a)
            si = a * sg
            a_s[:, sl] = si
            vv_s[:, sl] = vv
            sg_s[:, sl] = sg
            actb = (si * vv).astype(BF16)
            act_ref[:, sl] = actb
            pj = lax.dot_general(actb, wdt_ref[:, sl], (((1,), (1,)), ((), ())), preferred_element_type=F32)
            dn = pj if dn is None else dn + pj
        r3 = lax.rsqrt(_rowmean(dn * dn) + EPS)
        xn = dn * r3
        g = vd_ref[0:1, :]
        gt2 = vd_ref[1:2, :]
        dnn = xn * g
        diff = x2_ref[...] + gt2 * dnn - tgt_ref[...]
        part = 0.5 * inv_d * jnp.sum(diff * diff)

        @pl.when(i == 0)
        def _():
            loss_ref[...] = jnp.zeros(loss_ref.shape, F32)
        loss_ref[...] += part
        dout = diff * inv_d
        dout_ref[...] = dout
        ddnn = dout * gt2
        _acc_rows(vec_ref, i == 0, [_colsum(dout * dnn), _colsum(ddnn * xn)])
        dxn = ddnn * g
        ddn = r3 * (dxn - xn * _rowmean(dxn * xn))
        ddnb = ddn.astype(BF16)
        ddn_ref[...] = ddnb
        for o in range(0, dff, CW_FFN):
            sl = slice(o, o + CW_FFN)
            dact = lax.dot_general(ddnb, wd_ref[sl, :], (((1,), (1,)), ((), ())), preferred_element_type=F32)
            si, vv, sg = a_s[:, sl], vv_s[:, sl], sg_s[:, sl]
            dhid_ref[:, sl] = (dact * vv * (sg + si * (1.0 - sg))).astype(BF16)
            dhid_ref[:, dff + o:dff + o + CW_FFN] = (dact * si).astype(BF16)

    return pl.pallas_call(
        body, name="ffn_down", grid=(t // tb,),
        scratch_shapes=[pltpu.VMEM((tb, dff), F32)] * 3,
        out_shape=(jax.ShapeDtypeStruct((t, dff), BF16), jax.ShapeDtypeStruct((t, d), BF16),
                   jax.ShapeDtypeStruct((t, d), F32), jax.ShapeDtypeStruct((t, nh), BF16),
                   jax.ShapeDtypeStruct((SUBLANES, d), F32), jax.ShapeDtypeStruct((SUBLANES, 128), F32)),
        in_specs=[_rows(tb, nh), _halo_prev(tb, nh), _full(fw.shape), _resident(w_down.shape),
                  _resident(w_down_t.shape), _rows(tb, d),
                  _rows(tb, d), _full(vd.shape)],
        out_specs=(_rows(tb, dff), _rows(tb, d), _rows(tb, d), _rows(tb, nh), _full((SUBLANES, d)),
                   _full((SUBLANES, 128))),
        compiler_params=_cparams(("arbitrary",), VMEM_BIG),
    )(up, up, fw, w_down, w_down_t, x2, tgt, vd)


def _ffn_up_bwd(dhid, up, fw, x2, dout, vec, w_up_st):
    t, nh = dhid.shape
    d = x2.shape[1]
    ns, _, nc = w_up_st.shape
    tb = _blk(t, TB_FFN)
    nblk = t // tb
    cw = 128

    def body(dh_ref, dhn_ref, up_ref, fw_ref, x2_ref, dout_ref, vec_ref, w_ref,
             dx2_ref, dup_ref, vp_ref, df_ref):
        i = pl.program_id(0)

        @pl.when(i == 0)
        def _():
            df_ref[...] = jnp.zeros(df_ref.shape, F32)
        dh2 = None
        for j in range(ns):
            for o in range(j * nc, (j + 1) * nc, cw):
                sl = slice(o, o + cw)
                dh = dh_ref[:, sl].astype(F32)
                dhn = jnp.where(i < nblk - 1, dhn_ref[:, sl].astype(F32), 0.0)
                dh1 = _shift_up(dh, dhn, 1)
                dh2s = _shift_up(dh, dhn, 2)
                dup_ref[:, sl] = (fw_ref[2:3, sl] * dh + fw_ref[1:2, sl] * dh1 + fw_ref[0:1, sl] * dh2s).astype(BF16)
                up_v = up_ref[:, sl]
                df_ref[0:1, sl] += _colsum(dh2s * up_v)
                df_ref[1:2, sl] += _colsum(dh1 * up_v)
                df_ref[2:3, sl] += _colsum(dh * up_v)
            pj = lax.dot_general(dup_ref[:, j * nc:(j + 1) * nc], w_ref[j], (((1,), (1,)), ((), ())),
                                 preferred_element_type=F32)
            dh2 = pj if dh2 is None else dh2 + pj
        xv = x2_ref[...]
        r = lax.rsqrt(_rowmean(xv * xv) + EPS)
        xn = xv * r
        g = vec_ref[0:1, :]
        hg = xn * g
        dhg = dh2 * vec_ref[1:2, :]
        _acc_rows(vp_ref, i == 0, [_colsum(dh2), _colsum(dh2 * hg), _colsum(dhg * xn)])
        dxn = dhg * g
        dx2_ref[...] = dout_ref[...] + r * (dxn - xn * _rowmean(dxn * xn))

    return pl.pallas_call(
        body, name="ffn_up_bwd", grid=(nblk,),
        out_shape=(jax.ShapeDtypeStruct((t, d), F32), jax.ShapeDtypeStruct((t, nh), BF16),
                   jax.ShapeDtypeStruct((SUBLANES, d), F32), jax.ShapeDtypeStruct((SUBLANES, nh), F32)),
        in_specs=[_rows(tb, nh), _halo_next(tb, nh, t, rows=BF16_ROWS), _rows(tb, nh), _full(fw.shape),
                  _rows(tb, d), _rows(tb, d), _full(vec.shape), _resident(w_up_st.shape)],
        out_specs=(_rows(tb, d), _rows(tb, nh), _full((SUBLANES, d)), _full((SUBLANES, nh))),
        compiler_params=_cparams(("arbitrary",), VMEM_BIG),
    )(dhid, dhid, up, fw, x2, dout, vec, w_up_st)


def _mix_out_bwd(dx2, o, y1, proj, s_re, s_im, c_re, c_im, v512, convw, glu_w, h16, h64, w_out, vd):
    t, d = dx2.shape
    dh = y1.shape[1]
    nstate = c_re.shape[0]
    du, ds = dh // SSM_SPLIT, nstate // SSM_SPLIT
    tb = _blk(t, TB_MIX)

    def body(dx2_ref, o_ref, y1_ref, u_ref, bg_ref, cg_ref, v_ref, cgh_ref, vh_ref, cre_ref, cim_ref, p_ref,
             cw_ref, gw_ref, h16_ref, h64_ref, wo_ref, vd_ref, sr_ref, si_ref,
             do_ref, ycat_ref, z_ref, dq_ref, dy1_ref, gr_ref, gi_ref, dcc_ref, dbg_ref, vpd_ref, vp5_ref,
             dcr_ref, dci_ref):
        i = pl.program_id(0)
        first = i == 0

        @pl.when(first)
        def _():
            dcr_ref[...] = jnp.zeros(dcr_ref.shape, F32)
            dci_ref[...] = jnp.zeros(dci_ref.shape, F32)
        ov = o_ref[...]
        ro = lax.rsqrt(_rowmean(ov * ov) + EPS)
        on_ = ov * ro
        g = vd_ref[0:1, :]
        dx2v = dx2_ref[...]
        don = dx2v * vd_ref[1:2, :]
        _acc_rows(vpd_ref, first, [_colsum(dx2v * on_ * g), _colsum(don * on_)])
        dxn = don * g
        dob = (ro * (dxn - on_ * _rowmean(dxn * on_))).astype(BF16)
        do_ref[...] = dob
        dyc_a =lax.dot_general(dob, wo_ref[0:dh, :], (((1,), (1,)), ((), ())), preferred_element_type=F32)
        dyc_b = lax.dot_general(dob, wo_ref[dh:2 * dh, :], (((1,), (1,)), ((), ())), preferred_element_type=F32)
        y1v = y1_ref[...]
        u = u_ref[...].astype(F32)
        z, dz_dy1 = _gelu_and_grad(y1v)
        zb = z.astype(BF16)
        sg = _sigmoid(jnp.dot(zb, gw_ref[...], preferred_element_type=F32) + p_ref[1:2, :])
        ya = z * sg
        ra = lax.rsqrt(_head_ms(ya, h16_ref) + EPS)
        yan = ya * ra
        ga = p_ref[2:3, :]
        ycat_ref[:, 0:dh] = (yan * ga).astype(BF16)
        dyn = dyc_a * ga
        dya = ra * (dyn - yan * _split_dot(dyn * yan, h16_ref[...]))
        dq = dya * z * sg * (1.0 - sg)
        dqb = dq.astype(BF16)
        z_ref[...] = zb
        dq_ref[...] = dqb
        dz = dya * sg + lax.dot_general(dqb, gw_ref[...], (((1,), (1,)), ((), ())), preferred_element_type=F32)
        dy1 = dz * dz_dy1
        dy1_ref[...] = dy1
        dy1b = dy1.astype(BF16)
        for q in range(SSM_SPLIT):
            rq, cq = slice(q * ds, (q + 1) * ds), slice(q * du, (q + 1) * du)
            gr_ref[:, rq] = lax.dot_general(dy1b[:, cq], cre_ref[rq, cq], (((1,), (1,)), ((), ())),
                                            preferred_element_type=F32).astype(BF16)
            gi_ref[:, rq] = (-lax.dot_general(dy1b[:, cq], cim_ref[rq, cq], (((1,), (1,)), ((), ())),
                                              preferred_element_type=F32)).astype(BF16)
            dcr_ref[rq, :] += _dot_tn(sr_ref[:, rq], dy1b[:, cq])
            dci_ref[rq, :] += _dot_tn(si_ref[:, rq], dy1b[:, cq])
        bg = bg_ref[...].astype(F32)
        cv = cg_ref[...].astype(F32) * v_ref[...].astype(F32)
        cvh = jnp.where(i > 0, cgh_ref[...].astype(F32) * vh_ref[...].astype(F32), 0.0)
        cv1 = _shift_down(cv, cvh, 1)
        cv2 = _shift_down(cv, cvh, 2)
        cc = cw_ref[0:1, :] * cv2 + cw_ref[1:2, :] * cv1 + cw_ref[2:3, :] * cv
        yb = bg * cc
        rb = lax.rsqrt(_head_ms(yb, h64_ref) + EPS)
        ybn = yb * rb
        gb = p_ref[3:4, :]
        ycat_ref[:, dh:2 * dh] = (ybn * gb).astype(BF16)
        dynb = dyc_b * gb
        dyb = rb * (dynb - ybn * _split_dot(dynb * ybn, h64_ref[...]))
        dcc = dyb * bg
        dbg_ref[...] = dyb * cc
        dcc_ref[...] = dcc
        _acc_rows(vp5_ref, first, [_colsum(dyc_a * yan), _colsum(dyc_b * ybn), _colsum(dq), _colsum(dy1 * u),
                                   _colsum(dcc * cv2), _colsum(dcc * cv1), _colsum(dcc * cv)])

    return pl.pallas_call(
        body, name="mix_out_bwd", grid=(t // tb,),
        out_shape=(jax.ShapeDtypeStruct((t, d), BF16), jax.ShapeDtypeStruct((t, 2 * dh), BF16),
                   jax.ShapeDtypeStruct((t, dh), BF16), jax.ShapeDtypeStruct((t, dh), BF16),
                   jax.ShapeDtypeStruct((t, dh), F32), jax.ShapeDtypeStruct((t, nstate), BF16),
                   jax.ShapeDtypeStruct((t, nstate), BF16), jax.ShapeDtypeStruct((t, dh), F32),
                   jax.ShapeDtypeStruct((t, dh), F32), jax.ShapeDtypeStruct((SUBLANES, d), F32),
                   jax.ShapeDtypeStruct((SUBLANES, dh), F32), jax.ShapeDtypeStruct((nstate, du), F32),
                   jax.ShapeDtypeStruct((nstate, du), F32)),
        in_specs=[_rows(tb, d), _rows(tb, d), _rows(tb, dh), _rows(tb, dh, 0), _rows(tb, dh, 1), _rows(tb, dh, 2),
                  _rows(tb, dh, 3), _halo_prev(tb, dh, 2, BF16_ROWS), _halo_prev(tb, dh, 3, BF16_ROWS), _resident(c_re.shape),
                  _resident(c_im.shape), _full(v512.shape), _full(convw.shape), _resident(glu_w.shape),
                  _resident(h16.shape), _resident(h64.shape), _resident(w_out.shape), _full(vd.shape),
                  _rows(tb, nstate), _rows(tb, nstate)],
        out_specs=(_rows(tb, d), _rows(tb, 2 * dh), _rows(tb, dh), _rows(tb, dh), _rows(tb, dh), _rows(tb, nstate),
                   _rows(tb, nstate), _rows(tb, dh), _rows(tb, dh), _full((SUBLANES, d)), _full((SUBLANES, dh)),
                   _full((nstate, du)), _full((nstate, du))),
        compiler_params=_cparams(("arbitrary",), VMEM_BIG),
    )(dx2, o, y1, proj, proj, proj, proj, proj, proj, c_re, c_im, v512, convw, glu_w, h16, h64, w_out, vd,
      s_re, s_im)


def _mix_in_bwd(gt_re, gt_im, b_re, b_im, dy1, dcc, dbg, proj, x, dx2, vec, v512, convw, w_in_st):
    t, d = x.shape
    dh = dy1.shape[1]
    nstate = gt_re.shape[1]
    du_w, ds = dh // SSM_SPLIT, nstate // SSM_SPLIT
    ns, _, nc = w_in_st.shape
    tb = _blk(t, TB_MIX)
    nblk = t // tb

    def body(gr_ref, gi_ref, bre_ref, bim_ref, dy1_ref, dcc_ref, dccn_ref, dbg_ref, u_ref, cg_ref, v_ref, x_ref,
             dx2_ref, vec_ref, p_ref, cw_ref, w_ref, gx_ref, dproj_ref, vp_ref, dbr_ref, dbi_ref):
        i = pl.program_id(0)

        @pl.when(i == 0)
        def _():
            dbr_ref[...] = jnp.zeros(dbr_ref.shape, F32)
            dbi_ref[...] = jnp.zeros(dbi_ref.shape, F32)
        ub = u_ref[...].astype(BF16)
        du = []
        for q in range(SSM_SPLIT):
            rq, cq = slice(q * du_w, (q + 1) * du_w), slice(q * ds, (q + 1) * ds)
            du.append(lax.dot_general(gr_ref[:, cq].astype(BF16), bre_ref[rq, cq], (((1,), (1,)), ((), ())),
                                      preferred_element_type=F32)
                      + lax.dot_general(gi_ref[:, cq].astype(BF16), bim_ref[rq, cq], (((1,), (1,)), ((), ())),
                                        preferred_element_type=F32))
            dbr_ref[rq, :] += _dot_tn(ub[:, rq], gr_ref[:, cq])
            dbi_ref[rq, :] += _dot_tn(ub[:, rq], gi_ref[:, cq])
        du = dy1_ref[...] * p_ref[0:1, :] + jnp.concatenate(du, axis=1)
        dcc = dcc_ref[...]
        dccn = jnp.where(i < nblk - 1, dccn_ref[...], 0.0)
        dcv = (cw_ref[2:3, :] * dcc + cw_ref[1:2, :] * _shift_up(dcc, dccn, 1)
               + cw_ref[0:1, :] * _shift_up(dcc, dccn, 2))
        parts = [du, dbg_ref[...], dcv * v_ref[...].astype(F32), dcv * cg_ref[...].astype(F32)]
        xv = x_ref[...]
        r = lax.rsqrt(_rowmean(xv * xv) + EPS)
        xn = xv * r
        g = vec_ref[0:1, :]
        hg = xn * g
        dh1 = None
        for j in range(ns):
            pb = parts[j].astype(BF16)
            dproj_ref[:, j * nc:(j + 1) * nc] = pb
            pj =lax.dot_general(pb, w_ref[j], (((1,), (1,)), ((), ())), preferred_element_type=F32)
            dh1 = pj if dh1 is None else dh1 + pj
        dhg = dh1 * vec_ref[1:2, :]
        _acc_rows(vp_ref, i == 0, [_colsum(dh1), _colsum(dh1 * hg), _colsum(dhg * xn)])
        dxn = dhg * g
        gx_ref[...] = dx2_ref[...] + r * (dxn - xn * _rowmean(dxn * xn))

    assert nc == dh and ns == 4
    return pl.pallas_call(
        body, name="mix_in_bwd", grid=(nblk,),
        out_shape=(jax.ShapeDtypeStruct((t, d), F32), jax.ShapeDtypeStruct((t, ns * nc), BF16),
                   jax.ShapeDtypeStruct((SUBLANES, d), F32), jax.ShapeDtypeStruct((dh, ds), F32),
                   jax.ShapeDtypeStruct((dh, ds), F32)),
        in_specs=[_rows(tb, nstate), _rows(tb, nstate), _resident(b_re.shape), _resident(b_im.shape), _rows(tb, dh),
                  _rows(tb, dh), _halo_next(tb, dh, t), _rows(tb, dh), _rows(tb, dh, 0), _rows(tb, dh, 2),
                  _rows(tb, dh, 3), _rows(tb, d), _rows(tb, d), _full(vec.shape), _full(v512.shape),
                  _full(convw.shape), _resident(w_in_st.shape)],
        out_specs=(_rows(tb, d), _rows(tb, ns * nc), _full((SUBLANES, d)), _full((dh, ds)), _full((dh, ds))),
        compiler_params=_cparams(("arbitrary",), VMEM_BIG),
    )(gt_re, gt_im, b_re, b_im, dy1, dcc, dcc, dbg, proj, proj, proj, x, dx2, vec, v512, convw, w_in_st)


def _matmul_tn(a, b, m, bn, out_dtype, name, diag=False, bt=TB_TN, after=None):
    t = a.shape[0]
    n = b.shape[1]
    bt = _blk(t, bt)
    nk = t // bt
    extra = [] if after is None else [after]
    a_map = (lambda j, k: (k, j)) if diag else (lambda j, k: (k, 0))

    def body(a_ref, b_ref, *rest):
        o_ref, acc_ref = rest[-2:]
        k = pl.program_id(1)

        @pl.when(k == 0)
        def _():
            acc_ref[...] = jnp.zeros(acc_ref.shape, F32)
        acc_ref[...] += _dot_tn(a_ref[...], b_ref[...])

        @pl.when(k == nk - 1)
        def _():
            o_ref[...] = acc_ref[...].astype(out_dtype)

    return pl.pallas_call(
        body, name=name, grid=(n // bn, nk),
        out_shape=jax.ShapeDtypeStruct((n // bn, m, bn), out_dtype),
        in_specs=[pl.BlockSpec((bt, m), a_map), pl.BlockSpec((bt, bn), lambda j, k: (k, j))]
        + [pl.BlockSpec(memory_space=pl.ANY)] * len(extra),
        out_specs=pl.BlockSpec((None, m, bn), lambda j, k: (j, 0, 0)),
        scratch_shapes=[pltpu.VMEM((m, bn), F32)],
        compiler_params=_cparams(("parallel", "arbitrary"), VMEM_BIG),
    )(a, b, *extra)


def _ssm_bgrad(d_bre, d_bim, bt_re, bt_im, rows_in, fold, tile_b):
    gh, cb = d_bre.shape
    nb = SSM_SPLIT
    rb = gh // nb
    gp = nb * cb
    p = fold.shape[1]

    def body(dr_ref, di_ref, br_ref, bi_ref, rin_ref, f_ref, tb_ref, dbr_ref, dbi_ref, rout_ref):
        row = lax.broadcasted_iota(jnp.int32, (rb, cb), 0)
        col = lax.broadcasted_iota(jnp.int32, (rb, cb), 1)
        mask = (row >> 4) == (col >> 6)
        gr = jnp.where(mask, dr_ref[...], 0.0)
        gi = jnp.where(mask, di_ref[...], 0.0)
        cr, ci = rin_ref[0:1, :], rin_ref[1:2, :]
        dbr_ref[...] = _split3_dot(cr * gr + ci * gi, f_ref[...])
        dbi_ref[...] = _split3_dot(cr * gi - ci * gr, f_ref[...])
        br = _split3_dot(br_ref[...], tb_ref[...])
        bi = _split3_dot(bi_ref[...], tb_ref[...])
        rout_ref[...] = jnp.zeros(rout_ref.shape, F32)
        rout_ref[0:1, :] = _colsum(br * gr + bi * gi)
        rout_ref[1:2, :] = _colsum(br * gi - bi * gr)

    dspec = pl.BlockSpec((rb, cb), lambda j: (j, 0))
    rspec = pl.BlockSpec((SUBLANES, cb), lambda j: (0, j))
    ospec = pl.BlockSpec((rb, p), lambda j: (j, 0))
    return pl.pallas_call(
        body, name="ssm_bgrad", grid=(nb,),
        out_shape=(jax.ShapeDtypeStruct((gh, p), F32), jax.ShapeDtypeStruct((gh, p), F32),
                   jax.ShapeDtypeStruct((SUBLANES, gp), F32)),
        in_specs=[dspec, dspec, ospec, ospec, rspec, _full(fold.shape), _full(tile_b.shape)],
        out_specs=(ospec, ospec, rspec),
        compiler_params=_cparams(("parallel",)),
    )(d_bre, d_bim, bt_re, bt_im, rows_in, fold, tile_b)


def _ssm_cgrad(d_cre, d_cim, fold):
    gp, cb = d_cre.shape
    nb = SSM_SPLIT
    rb = gp // nb
    h = fold.shape[1]

    def body(dr_ref, di_ref, f_ref, cr_ref, ci_ref):
        row = lax.broadcasted_iota(jnp.int32, (rb, cb), 0)
        col = lax.broadcasted_iota(jnp.int32, (rb, cb), 1)
        mask = (row >> 6) == (col >> 4)
        cr_ref[...] = _split3_dot(jnp.where(mask, dr_ref[...], 0.0), f_ref[...])
        ci_ref[...] = -_split3_dot(jnp.where(mask, di_ref[...], 0.0), f_ref[...])

    cspec = pl.BlockSpec((rb, cb), lambda j: (j, 0))
    ospec = pl.BlockSpec((rb, h), lambda j: (j, 0))
    return pl.pallas_call(
        body, name="ssm_cgrad", grid=(nb,),
        out_shape=(jax.ShapeDtypeStruct((gp, h), F32),) * 2,
        in_specs=[cspec, cspec, _full(fold.shape)], out_specs=(ospec, ospec),
        compiler_params=_cparams(("parallel",)),
    )(d_cre, d_cim, fold)


def _ssm_lamgrad(lam_re, lam_im, log_step, abar_re, abar_im, coef_re, coef_im, gc_re, gc_im, ga_re, ga_im):
    g, p = lam_re.shape

    def body(lr_ref, li_ref, ls_ref, ar_ref, ai_ref, cr_ref, ci_ref, gcr_ref, gci_ref, gar_ref, gai_ref,
             dlr_ref, dli_ref, dls_ref):
        lam_raw = lr_ref[...]
        lr = jnp.minimum(lam_raw, LAMBDA_RE_MAX)
        li = li_ref[...]
        st = jnp.exp(ls_ref[...])
        den = lr * lr + li * li
        gcr, gci = gcr_ref[...], gci_ref[...]
        gab_r = gar_ref[...] + (lr * gcr - li * gci) / den
        gab_i = gai_ref[...] + (lr * gci + li * gcr) / den
        cr, ci = cr_ref[...], ci_ref[...]
        wr = -(cr * lr + ci * li) / den
        wi = -(ci * lr - cr * li) / den
        gl_r = wr * gcr + wi * gci
        gl_i = wr * gci - wi * gcr
        ar, ai = ar_ref[...], ai_ref[...]
        gw_r = ar * gab_r + ai * gab_i
        gw_i = ar * gab_i - ai * gab_r
        gl_r = gl_r + st * gw_r
        gl_i = gl_i + st * gw_i
        pass_through = jnp.where(lam_raw < LAMBDA_RE_MAX, 1.0, jnp.where(lam_raw == LAMBDA_RE_MAX, 0.5, 0.0))
        dlr_ref[...] = gl_r * pass_through
        dli_ref[...] = gl_i
        dls_ref[...] = st * jnp.sum(lr * gw_r + li * gw_i, axis=1, keepdims=True)

    sds = jax.ShapeDtypeStruct((g, p), F32)
    return pl.pallas_call(body, name="ssm_lamgrad", out_shape=(sds, sds, jax.ShapeDtypeStruct((g, 1), F32)))(
        lam_re, lam_im, log_step, abar_re, abar_im, coef_re, coef_im, gc_re, gc_im, ga_re, ga_im)


def _row_block(r, most=512):
    for rb in range(min(r, most), BF16_ROWS - 1, -1):
        if r % rb == 0 and rb % BF16_ROWS == 0:
            return rb
    return r


def _adamw_math(w, g, m, v):
    m = ADAM_B1 * m + (1.0 - ADAM_B1) * g
    v = ADAM_B2 * v + (1.0 - ADAM_B2) * (g * g)
    m_hat = m / (1.0 - ADAM_B1 ** ADAM_STEP)
    v_hat = v / (1.0 - ADAM_B2 ** ADAM_STEP)
    delta = -ADAM_LR * (m_hat / (jnp.sqrt(v_hat) + ADAM_EPS) + ADAM_WD * w)
    return delta, m, v


def _adamw_big(p_mine, p_sib, w, m, v, name):
    r, c = w.shape
    rb = _row_block(r)

    def body(a_ref, b_ref, w_ref, m_ref, v_ref, g_ref, d_ref, mo_ref, vo_ref):
        g = a_ref[...].astype(F32) + b_ref[...].astype(F32)
        g_ref[...] = g
        d_ref[...], mo_ref[...], vo_ref[...] = _adamw_math(w_ref[...], g, m_ref[...], v_ref[...])

    spec = pl.BlockSpec((rb, c), lambda i: (i, 0))
    sds = jax.ShapeDtypeStruct((r, c), F32)
    return pl.pallas_call(
        body, name=name, grid=(r // rb,), out_shape=(sds,) * 4, in_specs=[spec] * 5, out_specs=(spec,) * 4,
        compiler_params=_cparams(("parallel",), VMEM_KEEP_OPERANDS_IN_HBM),
    )(p_mine, p_sib, w, m, v)


def _sum_blocks(stack, name):
    n, r, c = stack.shape
    rb = _row_block(r)

    def body(s_ref, o_ref):
        acc = s_ref[0].astype(F32)
        for k in range(1, n):
            acc = acc + s_ref[k].astype(F32)
        o_ref[...] = acc

    return pl.pallas_call(
        body, name=name, grid=(r // rb,), out_shape=jax.ShapeDtypeStruct((r, c), F32),
        in_specs=[pl.BlockSpec((n, rb, c), lambda i: (0, i, 0))], out_specs=pl.BlockSpec((rb, c), lambda i: (i, 0)),
        compiler_params=_cparams(("parallel",), VMEM_KEEP_OPERANDS_IN_HBM),
    )(stack)


def _sum_landed(landed, own, chip, name):
    n, r, c = landed.shape
    rb = _row_block(r)

    def body(chip_ref, own_ref, l1_ref, l2_ref, l3_ref, o_ref):
        acc = own_ref[0].astype(F32)
        for ref in (l1_ref, l2_ref, l3_ref):
            acc = acc + ref[0].astype(F32)
        o_ref[...] = acc.astype(BF16)

    def slot(k):
        return pl.BlockSpec((1, rb, c), lambda i, ch: ((ch[0] + k) % n, i, 0))

    return pl.pallas_call(
        body, name=name, out_shape=jax.ShapeDtypeStruct((r, c), BF16),
        grid_spec=pltpu.PrefetchScalarGridSpec(
            num_scalar_prefetch=1, grid=(r // rb,), in_specs=[slot(0), slot(1), slot(2), slot(3)],
            out_specs=pl.BlockSpec((rb, c), lambda i, ch: (i, 0))),
        compiler_params=_cparams(("parallel",), VMEM_KEEP_OPERANDS_IN_HBM),
    )(jnp.reshape(chip, (1,)).astype(jnp.int32), own, landed, landed, landed)


def _add2(a, b):
    def body(a_ref, b_ref, o_ref):
        o_ref[...] = a_ref[...] + b_ref[...]

    return pl.pallas_call(body, name="add_small", out_shape=jax.ShapeDtypeStruct(a.shape, F32))(a, b)


def _adamw_ada(c_all, dmod_cols, w, m, v):
    d, n = w.shape
    bn = n // 2 if (n // 2) % 128 == 0 else n

    def body(c_ref, dm_ref, w_ref, m_ref, v_ref, g_ref, d_ref, mo_ref, vo_ref):
        cc = c_ref[...]
        g = _dot_tn(cc * _sigmoid(cc), dm_ref[...])
        g_ref[...] = g
        d_ref[...], mo_ref[...], vo_ref[...] = _adamw_math(w_ref[...], g, m_ref[...], v_ref[...])

    spec = pl.BlockSpec((d, bn), lambda j: (0, j))
    sds = jax.ShapeDtypeStruct((d, n), F32)
    return pl.pallas_call(
        body, name="adamw_ada", grid=(n // bn,), out_shape=(sds,) * 4,
        in_specs=[_full((N_DEV, d)), pl.BlockSpec((N_DEV, bn), lambda j: (0, j)), spec, spec, spec],
        out_specs=(spec,) * 4, compiler_params=_cparams(("parallel",), VMEM_KEEP_OPERANDS_IN_HBM),
    )(c_all, dmod_cols, w, m, v)


def _adamw_small(items):
    n = len(items)

    def body(*refs):
        ins, outs = refs[:4 * n], refs[4 * n:]
        for k in range(n):
            w_ref, g_ref, m_ref, v_ref = ins[4 * k:4 * k + 4]
            outs[3 * k][...], outs[3 * k + 1][...], outs[3 * k + 2][...] = _adamw_math(
                w_ref[...], g_ref[...], m_ref[...], v_ref[...])

    flat = [a for it in items for a in it]
    out_shape = tuple(jax.ShapeDtypeStruct(it[0].shape, F32) for it in items for _ in range(3))
    res = pl.pallas_call(body, name="adamw_small", out_shape=out_shape,
                         compiler_params=_cparams(vmem=VMEM_KEEP_OPERANDS_IN_HBM))(*flat)
    return [tuple(res[3 * k:3 * k + 3]) for k in range(n)]


def _group_mean_matrix(n, group):
    idx = np.arange(n) // group
    return (idx[:, None] == idx[None, :]).astype(np.float32) / group


def _fold_matrix(n, period):
    return (np.arange(n)[:, None] % period == np.arange(period)[None, :]).astype(np.float32)


def _rows8(*rows):
    c = rows[0].shape[-1]
    pad = jnp.zeros((SUBLANES - len(rows), c), F32)
    return jnp.concatenate([r.reshape(1, c) for r in rows] + [pad], axis=0)


def _to_rows(a, width):
    flat = a.reshape(-1)
    n = -(-flat.shape[0] // width)
    flat = jnp.pad(flat, (0, n * width - flat.shape[0]))
    return flat.reshape(n, width)


def kernel(x, c, w_ada, b_ada, g_pre_mix, g_post_mix, w_in, ssm_lam_re, ssm_lam_im, ssm_log_step, ssm_b_re, ssm_b_im, ssm_c_re, ssm_c_im, ssm_d, glu_w, glu_b, g_out_ssm, conv_w, g_out_conv, w_out, g_pre_ffn, g_post_ffn, w_up, ffn_conv_w, w_down, loss_target, m_w_ada, m_b_ada, m_g_pre_mix, m_g_post_mix, m_w_in, m_ssm_lam_re, m_ssm_lam_im, m_ssm_log_step, m_ssm_b_re, m_ssm_b_im, m_ssm_c_re, m_ssm_c_im, m_ssm_d, m_glu_w, m_glu_b, m_g_out_ssm, m_conv_w, m_g_out_conv, m_w_out, m_g_pre_ffn, m_g_post_ffn, m_w_up, m_ffn_conv_w, m_w_down, v_w_ada, v_b_ada, v_g_pre_mix, v_g_post_mix, v_w_in, v_ssm_lam_re, v_ssm_lam_im, v_ssm_log_step, v_ssm_b_re, v_ssm_b_im, v_ssm_c_re, v_ssm_c_im, v_ssm_d, v_glu_w, v_glu_b, v_g_out_ssm, v_conv_w, v_g_out_conv, v_w_out, v_g_pre_ffn, v_g_post_ffn, v_w_up, v_ffn_conv_w, v_w_down):
    xs = x[0]
    tgt = loss_target[0]
    t, d = xs.shape
    xi, yi, ci = lax.axis_index("x"), lax.axis_index("y"), lax.axis_index("c")
    chip = 2 * xi + yi
    dev = 2 * chip + ci

    n_groups, n_state = ssm_lam_re.shape[1:]
    n_gch = ssm_b_re.shape[3]
    d_ssm = n_groups * n_gch
    gp = n_groups * n_state
    n_ada = w_ada.shape[2]
    d_ff = w_down.shape[1] * N_CHIPS
    n_upc = w_up.shape[2]

    w_names = ("w_in", "glu_w", "w_out", "w_up", "w_down")
    c_gath = _allgather8(jnp.broadcast_to(c, (SUBLANES, d)), "gather_c")
    c_all = c_gath.reshape(N_DEV, SUBLANES, d)[:, 0, :]

    def pad8(a):
        return jnp.concatenate([a, jnp.zeros((SUBLANES - a.shape[0], a.shape[1]), a.dtype)], axis=0)

    def start(name, arrs, after):
        return _chips_start(name, True, [], [_landing(a, chip) for a in arrs], after)

    w_names = ("w_in", "mod", "conv_w", "ffn_conv_w", "glu_w", "w_out", "w_up", "w_down")
    first = start("weights_start_in", [w_in[0].astype(BF16)], c_gath)
    b_sh = lax.dynamic_slice(b_ada, (0, chip * n_ada), (1, n_ada))
    mod_sh = _mod_shard(c_all + first[4][0:1, 0:1], w_ada[0], b_sh)
    second = start("weights_start_rest", [mod_sh, pad8(conv_w[0]), pad8(ffn_conv_w[0])]
                   + [w[0].astype(BF16) for w in (glu_w, w_out, w_up, w_down)], None)
    w_send, w_recv, w_land = [list(first[k]) + list(second[k]) for k in (0, 1, 3)]
    w_token = second[4]

    def weights(names, after):
        ks = [w_names.index(nm) for nm in names]
        return _chips_wait("weights_wait_" + names[-1], True, [w_send[k] for k in ks], [w_recv[k] for k in ks],
                           [], [w_land[k] for k in ks], after)[1]

    lam_re, lam_im = ssm_lam_re[0], ssm_lam_im[0]
    log_step = ssm_log_step[0].reshape(n_groups, 1) + w_token[0:1, 0:1]
    abar_re, abar_im, coef_re, coef_im = _ssm_prep(lam_re, lam_im, log_step)
    a_rows = _rows8(abar_re.reshape(1, gp), abar_im.reshape(1, gp))
    coef_rows = _rows8(coef_re.reshape(1, gp), coef_im.reshape(1, gp))
    bt_re = ssm_b_re[0].transpose(0, 2, 1).reshape(d_ssm, n_state)
    bt_im = ssm_b_im[0].transpose(0, 2, 1).reshape(d_ssm, n_state)
    ct_re = ssm_c_re[0].transpose(0, 2, 1).reshape(gp, n_gch)
    ct_im = ssm_c_im[0].transpose(0, 2, 1).reshape(gp, n_gch)
    tile_b = jnp.asarray(np.tile(np.eye(n_state), (1, n_groups // SSM_SPLIT)), BF16)
    tile_c = jnp.asarray(np.tile(np.eye(n_gch), (1, n_groups)), BF16)
    bblk_re, bblk_im, cblk_re, cblk_im = _ssm_blocks(bt_re, bt_im, ct_re, ct_im, coef_rows, tile_b, tile_c)

    h16 = jnp.asarray(_group_mean_matrix(d_ssm, n_gch), BF16)
    h64 = jnp.asarray(_group_mean_matrix(d_ssm, CONV_HEAD_DIM), BF16)

    g_mod, g_cw, g_fw, w_in_st = weights(("mod", "conv_w", "ffn_conv_w", "w_in"), bblk_re)
    mod_all = g_mod.transpose(1, 0, 2).reshape(N_DEV, N_CHIPS * n_ada)
    mod = lax.dynamic_slice(mod_all, (dev, 0), (1, N_CHIPS * n_ada))
    sh1, sc1, gt1, sh2, sc2, gt2 = [mod[:, k * d:(k + 1) * d] for k in range(6)]
    convw_full = pad8(g_cw[:, :3, :].transpose(1, 0, 2).reshape(3, d_ssm))
    fw_full = pad8(g_fw[:, :3, :].transpose(1, 0, 2).reshape(3, N_CHIPS * n_upc))

    v512 = _rows8(ssm_d, glu_b, g_out_ssm, g_out_conv)
    vec1 =_rows8(g_pre_mix, 1.0 + sc1, sh1)
    vd1 = _rows8(g_post_mix, gt1)
    vec2 = _rows8(g_pre_ffn, 1.0 + sc2, sh2)
    vd2 = _rows8(g_post_ffn, gt2)

    proj, bu_re, bu_im, h1b = _mix_in(xs, vec1, w_in_st, bblk_re, bblk_im)
    s_re, s_im = _scan_fwd(a_rows, bu_re, bu_im)
    g_glu, g_wout = weights(("glu_w", "w_out"), s_re)
    glu_full = g_glu.reshape(d_ssm, d_ssm)
    w_out_full = g_wout.reshape(2 * d_ssm, d)
    y1, o_mix, x2 = _mix_out(xs, proj, s_re, s_im, cblk_re, cblk_im, v512, convw_full, glu_full, h16, h64,
                             w_out_full, vd1)
    (w_up_st,) = weights(("w_up",), x2)
    up, h2b = _ffn_up(x2, vec2, w_up_st)
    (g_wdown,) = weights(("w_down",), up)
    w_down_full = g_wdown.reshape(d_ff, d)
    actb, ddnb, dout, dhid, vp_dn, loss_blk = _ffn_down(up, fw_full, w_down_full, w_down_full.T, x2, tgt, vd2)

    gw_down = _matmul_tn(actb, ddnb, d_ff, d, BF16, "dw_down", bt=1024).reshape(N_CHIPS, d_ff // N_CHIPS, d)
    dx2, dupb, vp_up, df_rows = _ffn_up_bwd(dhid, up, fw_full, x2, dout, vec2, w_up_st)
    gw_up = _matmul_tn(h2b, dupb, d, n_upc, BF16, "dw_up", bt=2048)
    ga_send, ga_recv, ga_src, ga_land, ga_token = _chips_start(
        "grads_start_ffn", False, [gw_down, gw_up], [lax.empty(g.shape, g.dtype) for g in (gw_down, gw_up)])
    (dob, ycatb, zb, dqb, dy1, g_re, g_im, dcc, dbg, vp_mo, vp5, d_cre, d_cim) = _mix_out_bwd(
        dx2, o_mix, y1, proj, s_re, s_im, cblk_re, cblk_im, v512, convw_full, glu_full, h16, h64, w_out_full,
        vd1 + ga_token[0:1, 0:1])
    gw_out = _matmul_tn(ycatb, dob, 2 * d_ssm, d, BF16, "dw_out", bt=2048)
    gw_out = gw_out.reshape(N_CHIPS, 2 * d_ssm // N_CHIPS, d)
    gw_glu = _matmul_tn(zb, dqb, d_ssm, d_ssm, BF16, "dw_glu", bt=2048).reshape(N_CHIPS, d_ssm // N_CHIPS, d_ssm)
    gb_send, gb_recv, gb_src, gb_land, gb_token = _chips_start(
        "grads_start_mix", False, [gw_out, gw_glu], [lax.empty(g.shape, g.dtype) for g in (gw_out, gw_glu)])
    gt_re, gt_im, ga_re8, ga_im8 = _scan_bwd(a_rows + gb_token[0:1, 0:1], g_re, g_im, s_re, s_im)
    grad_x, dprojb, vp_mi, d_bre, d_bim = _mix_in_bwd(gt_re, gt_im, bblk_re, bblk_im, dy1, dcc, dbg, proj, xs, dx2,
                                                      vec1, v512, convw_full, w_in_st)
    ssm_u, ssm_s = d_ssm // SSM_SPLIT, gp // SSM_SPLIT

    fold_b = jnp.asarray(_fold_matrix(ssm_s, n_state), BF16)
    fold_c = jnp.asarray(_fold_matrix(ssm_u, n_gch), BF16)
    db_re_f, db_im_f, gc_rows = _ssm_bgrad(d_bre, d_bim, bt_re, bt_im, coef_rows, fold_b, tile_b)
    dc_re_f, dc_im_f = _ssm_cgrad(d_cre, d_cim, fold_c)
    ga_sum = _ga_rowsum(ga_re8, ga_im8)
    g_lam_re, g_lam_im, g_log_step = _ssm_lamgrad(
        lam_re, lam_im, log_step, abar_re, abar_im, coef_re, coef_im,
        gc_rows[0].reshape(n_groups, n_state), gc_rows[1].reshape(n_groups, n_state),
        ga_sum[0].reshape(n_groups, n_state), ga_sum[1].reshape(n_groups, n_state))
    g_b_re = db_re_f.reshape(n_groups, n_gch, n_state).transpose(0, 2, 1)
    g_b_im = db_im_f.reshape(n_groups, n_gch, n_state).transpose(0, 2, 1)
    g_c_re = dc_re_f.reshape(n_groups, n_state, n_gch).transpose(0, 2, 1)
    g_c_im = dc_im_f.reshape(n_groups, n_state, n_gch).transpose(0, 2, 1)

    dmod = jnp.concatenate([vp_mi[0:1], vp_mi[1:2], vp_mo[0:1], vp_up[0:1], vp_up[1:2], vp_dn[0:1]], axis=1)
    small = [
        ("g_pre_mix", vp_mi[2:3]), ("g_post_mix", vp_mo[1:2]), ("g_pre_ffn", vp_up[2:3]), ("g_post_ffn", vp_dn[1:2]),
        ("ssm_lam_re", g_lam_re), ("ssm_lam_im", g_lam_im), ("ssm_log_step", g_log_step),
        ("ssm_b_re", g_b_re), ("ssm_b_im", g_b_im), ("ssm_c_re", g_c_re), ("ssm_c_im", g_c_im),
        ("ssm_d", vp5[3:4]), ("glu_b", vp5[2:3]), ("g_out_ssm", vp5[0:1]), ("g_out_conv", vp5[1:2]),
        ("conv_w", vp5[4:7]), ("ffn_conv_w", df_rows[0:3]), ("loss", loss_blk[0:1, 0:1]),
    ]
    packed, offsets, row = [], {}, 0
    for name, a in small:
        r = _to_rows(a, d)
        offsets[name] = (row, a.shape)
        packed.append(r)
        row += r.shape[0]
    n_small = -(-row // SUBLANES) * SUBLANES
    packed.append(jnp.zeros((n_small - row, d), F32))
    packed.append(pad8(dmod.reshape(6, d)))
    pack = jnp.concatenate(packed, axis=0)
    sm_send, sm_recv, _, sm_land, sm_token = _chips_start("small_start", True, [], [_landing(pack, chip)])

    gw_in = _matmul_tn(h1b, dprojb, d, w_in.shape[2], BF16, "dw_in", bt=2048, after=sm_token)
    gc_send, gc_recv, gc_src, gc_land, gc_token = _chips_start(
        "grads_start_in", False, [gw_in], [lax.empty(gw_in.shape, gw_in.dtype)])

    def partials(names, own, landed):
        return [_sum_landed(l, o, chip, "sum_" + nm) for l, o, nm in zip(landed, own, names)]

    def update(names, mine, theirs):
        done = {}
        for nm, pm, ps in zip(names, mine, theirs):
            w_, m_, v_ = big_params[nm]
            done[nm] = _adamw_big(pm, ps, w_[0], m_[0], v_[0], "adamw_" + nm)
        return done

    big_params = {"w_down": (w_down, m_w_down, v_w_down), "w_up": (w_up, m_w_up, v_w_up),
                  "w_out": (w_out, m_w_out, v_w_out), "glu_w": (glu_w, m_glu_w, v_glu_w),
                  "w_in": (w_in, m_w_in, v_w_in)}
    ffn_names, mix_names = ("w_down", "w_up"), ("w_out", "glu_w", "w_in")
    p_ffn = partials(ffn_names, *_chips_wait("grads_wait_ffn", False, ga_send, ga_recv, ga_src, ga_land, gc_token))
    sa_send, sa_recv, sa_src, sa_land, sa_token = _sibling_start("swap_start_ffn", p_ffn)

    (sm_landed,) = _chips_wait("small_wait", True, sm_send, sm_recv, [], sm_land, sa_token)[1]
    sm_part = _sum_blocks(sm_landed, "sum_small")
    dmod_mine = sm_landed[:, n_small:n_small + SUBLANES, :]
    ss_send, ss_recv, ss_src, ss_land, ss_token = _sibling_start("swap_start_small", [sm_part, dmod_mine])
    p_ffn, t_ffn = _sibling_wait("swap_wait_ffn", sa_send, sa_recv, sa_src, sa_land, ss_token)
    big = update(ffn_names, p_ffn, t_ffn)
    (sm_part, dmod_mine), (sm_sib, dmod_sib) = _sibling_wait("swap_wait_small", ss_send, ss_recv, ss_src, ss_land,
                                                              big["w_up"][0])
    sums = _add2(sm_part, sm_sib)
    dmod_by_core = jnp.stack([dmod_mine, dmod_sib], axis=1)
    dmod_by_core = jnp.where(ci == 0, dmod_by_core, dmod_by_core[:, ::-1])
    dmod_all = dmod_by_core[:, :, :6, :].reshape(N_DEV, 6 * d)
    g_b_ada = sums[n_small:n_small + 6].reshape(1, 6 * d)

    def unpack(name):
        r0, shape = offsets[name]
        size = math.prod(shape)
        nrow = -(-size // d)
        return sums[r0:r0 + nrow].reshape(-1)[:size].reshape(shape)

    p_mix = partials(mix_names, *_chips_wait(
        "grads_wait_mix", False, list(gb_send) + list(gc_send), list(gb_recv) + list(gc_recv),
        list(gb_src) + list(gc_src), list(gb_land) + list(gc_land), sums))
    sb_send, sb_recv, sb_src, sb_land, sb_token = _sibling_start("swap_start_mix", p_mix)

    dmod_cols = lax.dynamic_slice(dmod_all, (0, chip * n_ada), (N_DEV, n_ada)) + sb_token[0:1, 0:1]
    ada = _adamw_ada(c_all, dmod_cols, w_ada[0], m_w_ada[0], v_w_ada[0])
    p_mix, t_mix = _sibling_wait("swap_wait_mix", sb_send, sb_recv, sb_src, sb_land, ada[0])
    big.update(update(mix_names, p_mix, t_mix))

    g_small = {name: unpack(name) for name, _ in small}
    g_small["b_ada"] = g_b_ada
    g_small["conv_w"] = lax.dynamic_slice(g_small["conv_w"], (0, chip * conv_w.shape[2]), (3, conv_w.shape[2]))
    g_small["ffn_conv_w"] = lax.dynamic_slice(g_small["ffn_conv_w"], (0, chip * n_upc), (3, n_upc))
    g_small["ssm_log_step"] = g_small["ssm_log_step"].reshape(1, n_groups)
    small_params = {
        "b_ada": (b_ada, m_b_ada, v_b_ada), "g_pre_mix": (g_pre_mix, m_g_pre_mix, v_g_pre_mix),
        "g_post_mix": (g_post_mix, m_g_post_mix, v_g_post_mix), "ssm_lam_re": (ssm_lam_re, m_ssm_lam_re, v_ssm_lam_re),
        "ssm_lam_im": (ssm_lam_im, m_ssm_lam_im, v_ssm_lam_im),
        "ssm_log_step": (ssm_log_step, m_ssm_log_step, v_ssm_log_step),
        "ssm_b_re": (ssm_b_re, m_ssm_b_re, v_ssm_b_re), "ssm_b_im": (ssm_b_im, m_ssm_b_im, v_ssm_b_im),
        "ssm_c_re": (ssm_c_re, m_ssm_c_re, v_ssm_c_re), "ssm_c_im": (ssm_c_im, m_ssm_c_im, v_ssm_c_im),
        "ssm_d": (ssm_d, m_ssm_d, v_ssm_d), "glu_b": (glu_b, m_glu_b, v_glu_b),
        "g_out_ssm": (g_out_ssm, m_g_out_ssm, v_g_out_ssm), "conv_w": (conv_w, m_conv_w, v_conv_w),
        "g_out_conv": (g_out_conv, m_g_out_conv, v_g_out_conv), "g_pre_ffn": (g_pre_ffn, m_g_pre_ffn, v_g_pre_ffn),
        "g_post_ffn": (g_post_ffn, m_g_post_ffn, v_g_post_ffn),
        "ffn_conv_w": (ffn_conv_w, m_ffn_conv_w, v_ffn_conv_w),
    }

    def natural(a):
        return a[0] if a.ndim > 2 else a

    names = list(small_params)
    items = []
    for nm in names:
        w_, m_, v_ = small_params[nm]
        items.append((natural(w_), g_small[nm].reshape(natural(w_).shape), natural(m_), natural(v_)))
    upd = _adamw_small(items)
    small_out = {}
    for nm, (dl, mo, vo) in zip(names, upd):
        shp = small_params[nm][0].shape
        small_out[nm] = (g_small[nm].reshape(shp), dl.reshape(shp), mo.reshape(shp), vo.reshape(shp))

    loss = g_small["loss"][0, 0]

    order = ["w_ada", "b_ada", "g_pre_mix", "g_post_mix", "w_in", "ssm_lam_re", "ssm_lam_im", "ssm_log_step",
             "ssm_b_re", "ssm_b_im", "ssm_c_re", "ssm_c_im", "ssm_d", "glu_w", "glu_b", "g_out_ssm", "conv_w",
             "g_out_conv", "w_out", "g_pre_ffn", "g_post_ffn", "w_up", "ffn_conv_w", "w_down"]
    results = {"w_ada": tuple(a[None] for a in ada)}
    for nm in big:
        results[nm] = tuple(a[None] for a in big[nm])
    results.update(small_out)
    outs = [loss, grad_x[None]]
    for k in range(4):
        outs += [results[nm][k] for nm in order]
    return tuple(outs)


def _ga_rowsum(ga_re8, ga_im8):
    n = ga_re8.shape[1]

    def body(r_ref, i_ref, o_ref):
        o_ref[...] = jnp.zeros(o_ref.shape, F32)
        o_ref[0:1, :] = _colsum(r_ref[...])
        o_ref[1:2, :] = _colsum(i_ref[...])

    return pl.pallas_call(body, name="ga_rowsum", out_shape=jax.ShapeDtypeStruct((SUBLANES, n), F32))(ga_re8, ga_im8)
```

```python
import functools
import math

import jax
import jax.numpy as jnp
import numpy as np
from jax import lax
from jax.experimental import pallas as pl
from jax.experimental.pallas import tpu as pltpu

F32 = jnp.float32
BF16 = jnp.bfloat16
MESH = pl.DeviceIdType.MESH

EPS = 1e-6
LAMBDA_RE_MAX = -1e-4
ADAM_LR = 0.001
ADAM_B1 = 0.9
ADAM_B2 = 0.999
ADAM_EPS = 1e-08
ADAM_WD = 0.01
ADAM_STEP = 10

SUBLANES = 8
BF16_ROWS = 16
N_CHIPS = 4
N_DEV = 8
CONV_HEAD_DIM = 64
VMEM_BIG = 56 * 1024 * 1024
VMEM_MID = 40 * 1024 * 1024
VMEM_KEEP_OPERANDS_IN_HBM = 62 * 1024 * 1024

TB_MIX = 256
TB_MIX_FWD = 512
TB_FFN = 256
TB_FFN_UP = 512
TB_SCAN = 2048
W_SCAN = 256
SSM_SPLIT = 4
CW_FFN = 256
SCAN_UNROLL = 4
TB_TN = 512


def _cparams(sem=None, vmem=None):
    kw = {}
    if sem is not None:
        kw["dimension_semantics"] = sem
    if vmem is not None:
        kw["vmem_limit_bytes"] = vmem
    return pltpu.CompilerParams(**kw)


def _blk(t, pref):
    return pref if t % pref == 0 else t


def _dot(a, b):
    return jnp.dot(a.astype(BF16), b.astype(BF16), preferred_element_type=F32)


def _dot_nt(a, b):
    return lax.dot_general(a.astype(BF16), b.astype(BF16), (((1,), (1,)), ((), ())),
                           preferred_element_type=F32)


def _dot_tn(a, b):
    return lax.dot_general(a.astype(BF16), b.astype(BF16), (((0,), (0,)), ((), ())),
                           preferred_element_type=F32)


def _sigmoid(x):
    return 0.5 * jnp.tanh(0.5 * x) + 0.5


_GELU_K = math.sqrt(2.0 / math.pi)
_GELU_C = 0.044715


def _gelu(x):
    th = jnp.tanh(_GELU_K * (x + _GELU_C * x * x * x))
    return x * (0.5 * (1.0 + th))


def _gelu_and_grad(x):
    x2 = x * x
    th = jnp.tanh(_GELU_K * (x + _GELU_C * x2 * x))
    half = 0.5 * (1.0 + th)
    return x * half, half + 0.5 * x * (1.0 - th * th) * _GELU_K * (1.0 + 3.0 * _GELU_C * x2)


def _rowmean(x):
    return jnp.mean(x, axis=-1, keepdims=True)


def _colsum(x):
    return jnp.sum(x, axis=0, keepdims=True)


def _split_dot(x, m):
    hi = x.astype(BF16)
    lo = (x - hi.astype(F32)).astype(BF16)
    return (jnp.dot(hi, m, preferred_element_type=F32) + jnp.dot(lo, m, preferred_element_type=F32))


def _split3_dot(x, m):
    hi = x.astype(BF16)
    r1 = x - hi.astype(F32)
    mid = r1.astype(BF16)
    lo = (r1 - mid.astype(F32)).astype(BF16)
    return (jnp.dot(hi, m, preferred_element_type=F32) + jnp.dot(mid, m, preferred_element_type=F32)
            + jnp.dot(lo, m, preferred_element_type=F32))


def _shift_down(x, halo, k):
    r = pltpu.roll(x, k, 0)
    row = lax.broadcasted_iota(jnp.int32, x.shape, 0)
    last = halo.shape[0]
    for j in range(k):
        r = jnp.where(row == j, halo[last - k + j:last - k + j + 1, :], r)
    return r


def _shift_up(x, halo, k):
    n = x.shape[0]
    r = pltpu.roll(x, n - k, 0)
    row = lax.broadcasted_iota(jnp.int32, x.shape, 0)
    for j in range(k):
        r = jnp.where(row == n - k + j, halo[j:j + 1, :], r)
    return r


def _acc_rows(ref, first, rows):
    @pl.when(first)
    def _():
        ref[...] = jnp.zeros(ref.shape, ref.dtype)
    for j, r in enumerate(rows):
        ref[j:j + 1, :] += r


def _rows(tb, c, col=0):
    return pl.BlockSpec((tb, c), lambda i, col=col: (i, col))


def _full(shape):
    nd = len(shape)
    return pl.BlockSpec(shape, lambda i, nd=nd: (0,) * nd)


def _resident(shape):
    nd = len(shape)
    return pl.BlockSpec(shape, lambda i, nd=nd: (0,) * nd, pipeline_mode=pl.Buffered(1))


def _halo_prev(tb, c, col=0, rows=SUBLANES):
    per = tb // rows
    return pl.BlockSpec((rows, c), lambda i, col=col: (jnp.maximum(i * per - 1, 0), col))


def _halo_next(tb, c, t, col=0, rows=SUBLANES):
    per = tb // rows
    last = t // rows - 1
    return pl.BlockSpec((rows, c), lambda i, col=col: (jnp.minimum((i + 1) * per, last), col))


def _mesh_pos():
    return lax.axis_index("x"), lax.axis_index("y"), lax.axis_index("c")


def _allgather8(x_pad, name):
    m_per, n = x_pad.shape

    def body(x_ref, out_ref, send_sems, recv_sems, local_sem):
        x, y, c = _mesh_pos()
        me, sibling = (x, y, c), (x, y, 1 - c)
        chips = [(1 - x, y), (x, 1 - y), (1 - x, 1 - y)]

        def rows(px, py, pc):
            return out_ref.at[pl.ds((4 * px + 2 * py + pc) * m_per, m_per), :]

        def copy(k, block, to, src=None):
            return pltpu.make_async_remote_copy(
                src_ref=rows(*block) if src is None else src, dst_ref=rows(*block),
                send_sem=send_sems.at[k], recv_sem=recv_sems.at[k], device_id=to, device_id_type=MESH)

        mine = pltpu.make_async_copy(x_ref, rows(*me), local_sem)
        mine.start()
        first = [copy(0, me, sibling, src=x_ref)]
        first += [copy(1 + j, me, (*chip, c), src=x_ref) for j, chip in enumerate(chips)]
        for cp in first:
            cp.start()
        passed = [copy(4 + j, (*chip, c), sibling) for j, chip in enumerate(chips)]
        for j, chip in enumerate(chips):
            copy(1 + j, (*chip, c), me).wait_recv()
            passed[j].start()
        copy(0, sibling, me).wait_recv()
        for j, chip in enumerate(chips):
            copy(4 + j, (*chip, 1 - c), me).wait_recv()
        for cp in first + passed:
            cp.wait_send()
        mine.wait()

    return pl.pallas_call(
        body, name=name,
        out_shape=jax.ShapeDtypeStruct((N_DEV * m_per, n), F32),
        in_specs=[pl.BlockSpec(memory_space=pltpu.VMEM)],
        out_specs=pl.BlockSpec(memory_space=pltpu.VMEM),
        scratch_shapes=[pltpu.SemaphoreType.DMA((7,)), pltpu.SemaphoreType.DMA((7,)), pltpu.SemaphoreType.DMA],
    )(x_pad)


_HBM = pl.BlockSpec(memory_space=pltpu.HBM)
_SEM = pl.BlockSpec(memory_space=pltpu.SEMAPHORE)
_EFFECT = pltpu.SideEffectType.DATAFLOW_SIDE_EFFECTING


def _chip_copy(gather, src_ref, land_ref, send, recv, j, arrival):
    x, y, c = _mesh_pos()
    peer = [(1 - x, y), (x, 1 - y), (1 - x, 1 - y)][j]
    peer_chip = 2 * peer[0] + peer[1]
    my_chip = 2 * x + y
    return pltpu.make_async_remote_copy(
        src_ref=land_ref.at[my_chip] if gather else src_ref.at[peer_chip],
        dst_ref=land_ref.at[peer_chip if arrival else my_chip],
        send_sem=send.at[j], recv_sem=recv.at[j], device_id=(*peer, c), device_id_type=MESH)


def _chips_start(name, gather, srcs, lands, after=None):
    n, ns = len(lands), len(srcs)
    extra = [] if after is None else [after]

    def body(*refs):
        src_refs, land_refs = refs[:ns], refs[ns:ns + n]
        outs = refs[ns + n + len(extra):]
        sends, recvs, token = outs[:n], outs[n:2 * n], outs[-1]
        for k in range(n):
            for j in range(3):
                _chip_copy(gather, src_refs[k] if ns else None, land_refs[k], sends[k], recvs[k], j, False).start()
        token[...] = jnp.zeros(token.shape, F32)

    sem = pltpu.SemaphoreType.DMA((3,))
    thru = tuple(pltpu.HBM(a.shape, a.dtype) for a in list(srcs) + list(lands))
    res = pl.pallas_call(
        body, name=name,
        out_shape=(sem,) * (2 * n) + thru + (jax.ShapeDtypeStruct((SUBLANES, 128), F32),),
        in_specs=[_HBM] * (ns + n) + [pl.BlockSpec(memory_space=pl.ANY)] * len(extra),
        out_specs=(_SEM,) * (2 * n) + (_HBM,) * (ns + n) + (pl.BlockSpec(memory_space=pltpu.VMEM),),
        input_output_aliases={k: 2 * n + k for k in range(ns + n)},
        compiler_params=pltpu.CompilerParams(has_side_effects=_EFFECT),
    )(*[pltpu.with_memory_space_constraint(a, pltpu.HBM) for a in list(srcs) + list(lands)], *extra)
    return res[:n], res[n:2 * n], res[2 * n:2 * n + ns], res[2 * n + ns:2 * n + ns + n], res[-1]


def _chips_wait(name, gather, sends, recvs, srcs, lands, after):
    n, ns = len(lands), len(srcs)

    def body(*refs):
        src_refs, land_refs = refs[:ns], refs[ns:ns + n]
        sends_, recvs_ = refs[ns + n:ns + 2 * n], refs[ns + 2 * n:ns + 3 * n]
        for k in range(n):
            for j in range(3):
                cp = _chip_copy(gather, src_refs[k] if ns else None, land_refs[k], sends_[k], recvs_[k], j, True)
                cp.wait_send()
                cp.wait_recv()

    thru = tuple(pltpu.HBM(a.shape, a.dtype) for a in list(srcs) + list(lands))
    res = pl.pallas_call(
        body, name=name, out_shape=thru,
        in_specs=[_HBM] * (ns + n) + [_SEM] * (2 * n) + [pl.BlockSpec(memory_space=pl.ANY)],
        out_specs=(_HBM,) * (ns + n),
        input_output_aliases={k: k for k in range(ns + n)},
        compiler_params=pltpu.CompilerParams(has_side_effects=_EFFECT),
    )(*srcs, *lands, *sends, *recvs, after)
    return res[:ns], res[ns:]


def _sibling_copy(src_ref, land_ref, send, recv):
    x, y, c = _mesh_pos()
    return pltpu.make_async_remote_copy(src_ref=src_ref, dst_ref=land_ref, send_sem=send.at[0], recv_sem=recv.at[0],
                                        device_id=(x, y, 1 - c), device_id_type=MESH)


def _sibling_start(name, arrs, after=None):
    n = len(arrs)
    extra = [] if after is None else [after]
    lands = [lax.empty(a.shape, a.dtype) for a in arrs]

    def body(*refs):
        src_refs, land_refs = refs[:n], refs[n:2 * n]
        outs = refs[2 * n + len(extra):]
        sends, recvs, token = outs[:n], outs[n:2 * n], outs[-1]
        for k in range(n):
            _sibling_copy(src_refs[k], land_refs[k], sends[k], recvs[k]).start()
        token[...] = jnp.zeros(token.shape, F32)

    sem = pltpu.SemaphoreType.DMA((1,))
    thru = tuple(pltpu.HBM(a.shape, a.dtype) for a in list(arrs) + lands)
    res = pl.pallas_call(
        body, name=name,
        out_shape=(sem,) * (2 * n) + thru + (jax.ShapeDtypeStruct((SUBLANES, 128), F32),),
        in_specs=[_HBM] * (2 * n) + [pl.BlockSpec(memory_space=pl.ANY)] * len(extra),
        out_specs=(_SEM,) * (2 * n) + (_HBM,) * (2 * n) + (pl.BlockSpec(memory_space=pltpu.VMEM),),
        input_output_aliases={k: 2 * n + k for k in range(2 * n)},
        compiler_params=pltpu.CompilerParams(has_side_effects=_EFFECT),
    )(*[pltpu.with_memory_space_constraint(a, pltpu.HBM) for a in list(arrs) + lands], *extra)
    return res[:n], res[n:2 * n], res[2 * n:3 * n], res[3 * n:4 * n], res[-1]


def _sibling_wait(name, sends, recvs, srcs, lands, after):
    n = len(srcs)

    def body(*refs):
        src_refs, land_refs = refs[:n], refs[n:2 * n]
        sends_, recvs_ = refs[2 * n:3 * n], refs[3 * n:4 * n]
        for k in range(n):
            cp = _sibling_copy(src_refs[k], land_refs[k], sends_[k], recvs_[k])
            cp.wait_send()
            cp.wait_recv()

    thru = tuple(pltpu.HBM(a.shape, a.dtype) for a in list(srcs) + list(lands))
    res = pl.pallas_call(
        body, name=name, out_shape=thru,
        in_specs=[_HBM] * (2 * n) + [_SEM] * (2 * n) + [pl.BlockSpec(memory_space=pl.ANY)],
        out_specs=(_HBM,) * (2 * n),
        input_output_aliases={k: k for k in range(2 * n)},
        compiler_params=pltpu.CompilerParams(has_side_effects=_EFFECT),
    )(*srcs, *lands, *sends, *recvs, after)
    return res[:n], res[n:]


def _landing(own, chip):
    zone = lax.empty((N_CHIPS,) + own.shape, own.dtype)
    return lax.dynamic_update_slice(zone, own[None], (chip,) + (0,) * own.ndim)


def _mod_shard(c_all, w_ada_sh, b_sh):
    d, n = w_ada_sh.shape
    bn = 512

    def body(c_ref, w_ref, b_ref, o_ref):
        cc = c_ref[...]
        ca = cc * _sigmoid(cc)
        o_ref[...] = _dot(ca, w_ref[...]) + b_ref[...]

    return pl.pallas_call(
        body, name="mod_shard", grid=(n // bn,),
        out_shape=jax.ShapeDtypeStruct((N_DEV, n), F32),
        in_specs=[_full((N_DEV, d)), pl.BlockSpec((d, bn), lambda j: (0, j)), pl.BlockSpec((1, bn), lambda j: (0, j))],
        out_specs=pl.BlockSpec((N_DEV, bn), lambda j: (0, j)),
        compiler_params=_cparams(("parallel",)),
    )(c_all, w_ada_sh, b_sh)


def _ssm_prep(lam_re, lam_im, log_step):
    g, p = lam_re.shape

    def body(lr_ref, li_ref, ls_ref, ar_ref, ai_ref, cr_ref, ci_ref):
        lr = jnp.minimum(lr_ref[...], LAMBDA_RE_MAX)
        li = li_ref[...]
        st = jnp.exp(ls_ref[...])
        mag = jnp.exp(lr * st)
        ar = mag * jnp.cos(li * st)
        ai = mag * jnp.sin(li * st)
        den = lr * lr + li * li
        nr = ar - 1.0
        ar_ref[...] = ar
        ai_ref[...] = ai
        cr_ref[...] = (nr * lr + ai * li) / den
        ci_ref[...] = (ai * lr - nr * li) / den

    sds = jax.ShapeDtypeStruct((g, p), F32)
    return pl.pallas_call(body, name="ssm_prep", out_shape=(sds,) * 4)(lam_re, lam_im, log_step)


def _ssm_blocks(bt_re, bt_im, ct_re, ct_im, coef_rows, tile_b, tile_c):
    gh, p = bt_re.shape
    gp, h = ct_re.shape
    nb = SSM_SPLIT
    cb, rb = gp // nb, gp // nb

    def body(btr, bti, ctr, cti, cf, tb_ref, tc_ref, bre_o, bim_o, cre_o, cim_o):
        j = pl.program_id(0)
        row = lax.broadcasted_iota(jnp.int32, (gh, cb), 0)
        col = lax.broadcasted_iota(jnp.int32, (gh, cb), 1) + j * cb
        mask = (row >> 4) == (col >> 6)
        cr, ci = cf[0:1, :], cf[1:2, :]
        br = _split3_dot(btr[...], tb_ref[...])
        bi = _split3_dot(bti[...], tb_ref[...])
        bre_o[...] = jnp.where(mask, br * cr - bi * ci, 0.0).astype(BF16)
        bim_o[...] = jnp.where(mask, br * ci + bi * cr, 0.0).astype(BF16)
        row2 = lax.broadcasted_iota(jnp.int32, (rb, gh), 0) + j * rb
        col2 = lax.broadcasted_iota(jnp.int32, (rb, gh), 1)
        mask2 = (row2 >> 6) == (col2 >> 4)
        cre_o[...] = jnp.where(mask2, _split3_dot(ctr[...], tc_ref[...]), 0.0).astype(BF16)
        cim_o[...] = jnp.where(mask2, _split3_dot(cti[...], tc_ref[...]), 0.0).astype(BF16)

    bspec = pl.BlockSpec((gh, cb), lambda j: (0, j))
    cspec = pl.BlockSpec((rb, gh), lambda j: (j, 0))
    cin = pl.BlockSpec((rb, h), lambda j: (j, 0))
    return pl.pallas_call(
        body, name="ssm_blocks", grid=(nb,),
        out_shape=(jax.ShapeDtypeStruct((gh, gp), BF16),) * 2 + (jax.ShapeDtypeStruct((gp, gh), BF16),) * 2,
        in_specs=[_full((gh, p)), _full((gh, p)), cin, cin, pl.BlockSpec((SUBLANES, cb), lambda j: (0, j)),
                  _full(tile_b.shape), _full(tile_c.shape)],
        out_specs=(bspec, bspec, cspec, cspec),
        compiler_params=_cparams(("parallel",)),
    )(bt_re, bt_im, ct_re, ct_im, coef_rows, tile_b, tile_c)


def _scan_consts(a_ref, reverse):
    w = a_ref.shape[1]
    ar1 = a_ref[0:1, :]
    ai1 = a_ref[1:2, :]
    if reverse:
        ai1 = -ai1
    pr, pi = [ar1], [ai1]
    for _ in range(1, SUBLANES):
        nr = pr[-1] * ar1 - pi[-1] * ai1
        ni = pr[-1] * ai1 + pi[-1] * ar1
        pr.append(nr)
        pi.append(ni)
    row = lax.broadcasted_iota(jnp.int32, (SUBLANES, w), 0)
    dist = (SUBLANES - 1 - row) if reverse else row

    def pick(vals):
        out = jnp.broadcast_to(vals[SUBLANES - 1], (SUBLANES, w))
        for r in range(SUBLANES - 1):
            out = jnp.where(dist == r, vals[r], out)
        return out

    p_r, p_i = pick(pr), pick(pi)
    steps = []
    for k in (1, 2, 4):
        steps.append((k, jnp.where(dist >= k, pr[k - 1], 0.0), jnp.where(dist >= k, pi[k - 1], 0.0)))
    a8 = (jnp.broadcast_to(pr[SUBLANES - 1], (SUBLANES, w)), jnp.broadcast_to(pi[SUBLANES - 1], (SUBLANES, w)))
    return row, p_r, p_i, steps, a8


def _scan_tile(xr, xi, cr, ci, consts, reverse):
    row, p_r, p_i, steps, (a8r, a8i) = consts
    for k, s_r, s_i in steps:
        sh = (SUBLANES - k) if reverse else k
        qr = pltpu.roll(xr, sh, 0)
        qi = pltpu.roll(xi, sh, 0)
        xr, xi = xr + s_r * qr - s_i * qi, xi + s_r * qi + s_i * qr
    outr = xr + p_r * cr - p_i * ci
    outi = xi + p_r * ci + p_i * cr
    e = 0 if reverse else SUBLANES - 1
    er = jnp.broadcast_to(xr[e:e + 1, :], xr.shape)
    ei = jnp.broadcast_to(xi[e:e + 1, :], xi.shape)
    return outr, outi, er + a8r * cr - a8i * ci, ei + a8r * ci + a8i * cr


def _scan_fwd(a_rows, bu_re, bu_im):
    t, n = bu_re.shape
    tb, w = _blk(t, TB_SCAN), W_SCAN
    ntile = tb // SUBLANES

    def body(a_ref, br_ref, bi_ref, sr_ref, si_ref, car, cai):
        @pl.when(pl.program_id(1) == 0)
        def _():
            car[...] = jnp.zeros(car.shape, F32)
            cai[...] = jnp.zeros(cai.shape, F32)
        consts = _scan_consts(a_ref, False)

        def pair(i, carry):
            o = pl.multiple_of(i * BF16_ROWS, BF16_ROWS)
            b_r = br_ref[pl.ds(o, BF16_ROWS), :].astype(F32)
            b_i = bi_ref[pl.ds(o, BF16_ROWS), :].astype(F32)
            outs = []
            for h in range(2):
                rows = slice(h * SUBLANES, (h + 1) * SUBLANES)
                outr, outi, ncr, nci = _scan_tile(b_r[rows, :], b_i[rows, :], carry[0], carry[1], consts, False)
                outs.append((outr, outi))
                carry = (ncr, nci)
            sr_ref[pl.ds(o, BF16_ROWS), :] = jnp.concatenate([outs[0][0], outs[1][0]], axis=0).astype(BF16)
            si_ref[pl.ds(o, BF16_ROWS), :] = jnp.concatenate([outs[0][1], outs[1][1]], axis=0).astype(BF16)
            return carry

        def pairs(i, carry):
            for s in range(SCAN_UNROLL // 2):
                carry = pair(i * (SCAN_UNROLL // 2) + s, carry)
            return carry

        cr, ci = lax.fori_loop(0, ntile // SCAN_UNROLL, pairs, (car[...], cai[...]))
        car[...] = cr
        cai[...] = ci

    spec = pl.BlockSpec((tb, w), lambda s, k: (k, s))
    sds = jax.ShapeDtypeStruct((t, n), BF16)
    return pl.pallas_call(
        body, name="scan_fwd", grid=(n // w, t // tb), out_shape=(sds, sds),
        in_specs=[pl.BlockSpec((SUBLANES, w), lambda s, k: (0, s)), spec, spec], out_specs=(spec, spec),
        scratch_shapes=[pltpu.VMEM((SUBLANES, w), F32), pltpu.VMEM((SUBLANES, w), F32)],
        compiler_params=_cparams(("parallel", "arbitrary"), VMEM_MID),
    )(a_rows, bu_re, bu_im)


def _scan_bwd(a_rows, g_re, g_im, s_re, s_im):
    t, n = g_re.shape
    tb, w = _blk(t, TB_SCAN), W_SCAN
    ntile = tb // SUBLANES
    npair = tb // BF16_ROWS
    nt = t // tb

    def body(a_ref, gr_ref, gi_ref, sr_ref, si_ref, or_ref, oi_ref, gar_ref, gai_ref, car, cai):
        @pl.when(pl.program_id(1) == 0)
        def _():
            car[...] = jnp.zeros(car.shape, F32)
            cai[...] = jnp.zeros(cai.shape, F32)
            gar_ref[...] = jnp.zeros(gar_ref.shape, F32)
            gai_ref[...] = jnp.zeros(gai_ref.shape, F32)
        consts = _scan_consts(a_ref, True)
        row = consts[0]

        def pair(i, carry):
            cr, ci, accr, acci = carry
            o = pl.multiple_of((npair - 1 - i) * BF16_ROWS, BF16_ROWS)
            s_r = sr_ref[pl.ds(o, BF16_ROWS), :].astype(F32)
            s_i = si_ref[pl.ds(o, BF16_ROWS), :].astype(F32)
            g_r = gr_ref[pl.ds(o, BF16_ROWS), :].astype(F32)
            g_i = gi_ref[pl.ds(o, BF16_ROWS), :].astype(F32)
            outs = [None, None]
            for h in (1, 0):
                rows = slice(h * SUBLANES, (h + 1) * SUBLANES)
                outr, outi, ncr, nci = _scan_tile(g_r[rows, :], g_i[rows, :], cr, ci, consts, True)
                outs[h] = (outr, outi)
                gnr = jnp.where(row == SUBLANES - 1, cr, pltpu.roll(outr, SUBLANES - 1, 0))
                gni = jnp.where(row == SUBLANES - 1, ci, pltpu.roll(outi, SUBLANES - 1, 0))
                sr = s_r[h * SUBLANES:(h + 1) * SUBLANES, :]
                si = s_i[h * SUBLANES:(h + 1) * SUBLANES, :]
                accr, acci = accr + sr * gnr + si * gni, acci + sr * gni - si * gnr
                cr, ci = ncr, nci
            or_ref[pl.ds(o, BF16_ROWS), :] = jnp.concatenate([outs[0][0], outs[1][0]], axis=0).astype(BF16)
            oi_ref[pl.ds(o, BF16_ROWS), :] = jnp.concatenate([outs[0][1], outs[1][1]], axis=0).astype(BF16)
            return cr, ci, accr, acci

        def pairs(i, carry):
            for s in range(SCAN_UNROLL // 2):
                carry = pair(i * (SCAN_UNROLL // 2) + s, carry)
            return carry

        cr, ci, accr, acci = lax.fori_loop(0, ntile // SCAN_UNROLL, pairs,
                                           (car[...], cai[...], gar_ref[...], gai_ref[...]))
        car[...] = cr
        cai[...] = ci
        gar_ref[...] = accr
        gai_ref[...] = acci

    spec = pl.BlockSpec((tb, w), lambda s, k: (nt - 1 - k, s))
    aspec = pl.BlockSpec((SUBLANES, w), lambda s, k: (0, s))
    sds = jax.ShapeDtypeStruct((t, n), BF16)
    asds = jax.ShapeDtypeStruct((SUBLANES, n), F32)
    return pl.pallas_call(
        body, name="scan_bwd", grid=(n // w, nt), out_shape=(sds, sds, asds, asds),
        in_specs=[aspec, spec, spec, spec, spec], out_specs=(spec, spec, aspec, aspec),
        scratch_shapes=[pltpu.VMEM((SUBLANES, w), F32), pltpu.VMEM((SUBLANES, w), F32)],
        compiler_params=_cparams(("parallel", "arbitrary"), VMEM_MID),
    )(a_rows, g_re, g_im, s_re, s_im)


def _mix_in(x, vec, w_in_st, b_re, b_im):
    t, d = x.shape
    ns, _, nc = w_in_st.shape
    dssm, nstate = b_re.shape
    du, ds = dssm // SSM_SPLIT, nstate // SSM_SPLIT
    tb = _blk(t, TB_MIX_FWD)

    def body(x_ref, vec_ref, w_ref, bre_ref, bim_ref, proj_ref, bur_ref, bui_ref, h1_ref):
        xv = x_ref[...]
        r = lax.rsqrt(_rowmean(xv * xv) + EPS)
        h = xv * r * vec_ref[0:1, :] * vec_ref[1:2, :] + vec_ref[2:3, :]
        hb = h.astype(BF16)
        h1_ref[...] = hb
        u = None
        for j in range(ns):
            pj = jnp.dot(hb, w_ref[j], preferred_element_type=F32)
            proj_ref[:, j * nc:(j + 1) * nc] = pj.astype(BF16)
            if j == 0:
                u = pj
        ub = u.astype(BF16)
        for q in range(SSM_SPLIT):
            rq, cq = slice(q * du, (q + 1) * du), slice(q * ds, (q + 1) * ds)
            bur_ref[:, cq] = jnp.dot(ub[:, rq], bre_ref[rq, cq], preferred_element_type=F32).astype(BF16)
            bui_ref[:, cq] = jnp.dot(ub[:, rq], bim_ref[rq, cq], preferred_element_type=F32).astype(BF16)

    return pl.pallas_call(
        body, name="mix_in", grid=(t // tb,),
        out_shape=(jax.ShapeDtypeStruct((t, ns * nc), BF16), jax.ShapeDtypeStruct((t, nstate), BF16),
                   jax.ShapeDtypeStruct((t, nstate), BF16), jax.ShapeDtypeStruct((t, d), BF16)),
        in_specs=[_rows(tb, d), _full((SUBLANES, d)), _resident(w_in_st.shape), _resident(b_re.shape),
                  _resident(b_im.shape)],
        out_specs=(_rows(tb, ns * nc), _rows(tb, nstate), _rows(tb, nstate), _rows(tb, d)),
        compiler_params=_cparams(("parallel",), VMEM_BIG),
    )(x, vec, w_in_st, b_re, b_im)


def _head_ms(y, h_ref):
    return _split_dot(y * y, h_ref[...])


def _conv3(x, halo, w_ref):
    return w_ref[0:1, :] * _shift_down(x, halo, 2) + w_ref[1:2, :] * _shift_down(x, halo, 1) + w_ref[2:3, :] * x


def _mix_out(x, proj, s_re, s_im, c_re, c_im, v512, convw, glu_w, h16, h64, w_out, vd):
    t, d = x.shape
    dh = c_re.shape[1]
    nstate = s_re.shape[1]
    du, ds = dh // SSM_SPLIT, nstate // SSM_SPLIT
    tb = _blk(t, TB_MIX_FWD)

    def body(x_ref, u_ref, bg_ref, cg_ref, v_ref, cgh_ref, vh_ref, sr_ref, si_ref, cre_ref, cim_ref, p_ref,
             cw_ref, gw_ref, h16_ref, h64_ref, wo_ref, vd_ref, y1_ref, o_ref, x2_ref):
        i = pl.program_id(0)
        u = u_ref[...].astype(F32)
        ys = []
        for q in range(SSM_SPLIT):
            rq, cq = slice(q * ds, (q + 1) * ds), slice(q * du, (q + 1) * du)
            ys.append(_dot(sr_ref[:, rq], cre_ref[rq, cq]) - _dot(si_ref[:, rq], cim_ref[rq, cq]))
        ys = jnp.concatenate(ys, axis=1)
        y1 = ys + p_ref[0:1, :] * u
        y1_ref[...] = y1
        z = _gelu(y1)
        q = _dot(z, gw_ref[...]) + p_ref[1:2, :]
        ya = z * _sigmoid(q)
        na = ya * lax.rsqrt(_head_ms(ya, h16_ref) + EPS) * p_ref[2:3, :]
        cv = cg_ref[...].astype(F32) * v_ref[...].astype(F32)
        cvh = jnp.where(i > 0, cgh_ref[...].astype(F32) * vh_ref[...].astype(F32), 0.0)
        yb = bg_ref[...].astype(F32) * _conv3(cv, cvh, cw_ref)
        nb = yb * lax.rsqrt(_head_ms(yb, h64_ref) + EPS) * p_ref[3:4, :]
        o = _dot(na, wo_ref[0:dh, :]) + _dot(nb, wo_ref[dh:2 * dh, :])
        o_ref[...] = o
        on = o * lax.rsqrt(_rowmean(o * o) + EPS) * vd_ref[0:1, :]
        x2_ref[...] = x_ref[...] + vd_ref[1:2, :] * on

    return pl.pallas_call(
        body, name="mix_out", grid=(t // tb,),
        out_shape=(jax.ShapeDtypeStruct((t, dh), F32), jax.ShapeDtypeStruct((t, d), F32),
                   jax.ShapeDtypeStruct((t, d), F32)),
        in_specs=[_rows(tb, d), _rows(tb, dh, 0), _rows(tb, dh, 1), _rows(tb, dh, 2), _rows(tb, dh, 3),
                  _halo_prev(tb, dh, 2, BF16_ROWS), _halo_prev(tb, dh, 3, BF16_ROWS), _rows(tb, nstate), _rows(tb, nstate),
                  _full(c_re.shape), _full(c_im.shape), _full(v512.shape), _full(convw.shape), _full(glu_w.shape),
                  _full(h16.shape), _full(h64.shape), _full(w_out.shape), _full(vd.shape)],
        out_specs=(_rows(tb, dh), _rows(tb, d), _rows(tb, d)),
        compiler_params=_cparams(("parallel",), VMEM_BIG),
    )(x, proj, proj, proj, proj, proj, proj, s_re, s_im, c_re, c_im, v512, convw, glu_w, h16, h64, w_out, vd)


def _ffn_up(x2, vec, w_up_st):
    t, d = x2.shape
    ns, _, nc = w_up_st.shape
    tb = _blk(t, TB_FFN_UP)

    def body(x_ref, vec_ref, w_ref, up_ref, h2_ref):
        xv = x_ref[...]
        r = lax.rsqrt(_rowmean(xv * xv) + EPS)
        h = xv * r * vec_ref[0:1, :] * vec_ref[1:2, :] + vec_ref[2:3, :]
        hb = h.astype(BF16)
        h2_ref[...] = hb
        for j in range(ns):
            up_ref[:, j * nc:(j + 1) * nc] = jnp.dot(hb, w_ref[j], preferred_element_type=F32)

    return pl.pallas_call(
        body, name="ffn_up", grid=(t // tb,),
        out_shape=(jax.ShapeDtypeStruct((t, ns * nc), F32), jax.ShapeDtypeStruct((t, d), BF16)),
        in_specs=[_rows(tb, d), _full((SUBLANES, d)), _resident(w_up_st.shape)],
        out_specs=(_rows(tb, ns * nc), _rows(tb, d)),
        compiler_params=_cparams(("parallel",), VMEM_BIG),
    )(x2, vec, w_up_st)


def _ffn_down(up, fw, w_down, w_down_t, x2, tgt, vd):
    t, nh = up.shape
    dff, d = w_down.shape
    tb = _blk(t, TB_FFN)
    inv_d = 1.0 / d

    def body(up_ref, uph_ref, fw_ref, wd_ref, wdt_ref, x2_ref, tgt_ref, vd_ref,
             act_ref, ddn_ref, dout_ref, dhid_ref, vec_ref, loss_ref, a_s, vv_s, sg_s):
        i = pl.program_id(0)

        def conv_cols(sl):
            x = up_ref[:, sl]
            halo = jnp.where(i > 0, uph_ref[:, sl], 0.0)
            return (fw_ref[0:1, sl] * _shift_down(x, halo, 2) + fw_ref[1:2, sl] * _shift_down(x, halo, 1)
                    + fw_ref[2:3, sl] * x)

        dn = None
        for o in range(0, dff, CW_FFN):
            sl = slice(o, o + CW_FFN)
            a = conv_cols(sl)
            vv = conv_cols(slice(dff + o, dff + o + CW_FFN))
            sg = _sigmoid(a)
            si = a * sg
            a_s[:, sl] = si
            vv_s[:, sl] = vv
            sg_s[:, sl] = sg
            actb = (si * vv).astype(BF16)
            act_ref[:, sl] = actb
            pj = lax.dot_general(actb, wdt_ref[:, sl], (((1,), (1,)), ((), ())), preferred_element_type=F32)
            dn = pj if dn is None else dn + pj
        r3 = lax.rsqrt(_rowmean(dn * dn) + EPS)
        xn = dn * r3
        g = vd_ref[0:1, :]
        gt2 = vd_ref[1:2, :]
        dnn = xn * g
        diff = x2_ref[...] + gt2 * dnn - tgt_ref[...]
        part = 0.5 * inv_d * jnp.sum(diff * diff)

        @pl.when(i == 0)
        def _():
            loss_ref[...] = jnp.zeros(loss_ref.shape, F32)
        loss_ref[...] += part
        dout = diff * inv_d
        dout_ref[...] = dout
        ddnn = dout * gt2
        _acc_rows(vec_ref, i == 0, [_colsum(dout * dnn), _colsum(ddnn * xn)])
        dxn = ddnn * g
        ddn = r3 * (dxn - xn * _rowmean(dxn * xn))
        ddnb = ddn.astype(BF16)
        ddn_ref[...] = ddnb
        for o in range(0, dff, CW_FFN):
            sl = slice(o, o + CW_FFN)
            dact = lax.dot_general(ddnb, wd_ref[sl, :], (((1,), (1,)), ((), ())), preferred_element_type=F32)
            si, vv, sg = a_s[:, sl], vv_s[:, sl], sg_s[:, sl]
            dhid_ref[:, sl] = (dact * vv * (sg + si * (1.0 - sg))).astype(BF16)
            dhid_ref[:, dff + o:dff + o + CW_FFN] = (dact * si).astype(BF16)

    return pl.pallas_call(
        body, name="ffn_down", grid=(t // tb,),
        scratch_shapes=[pltpu.VMEM((tb, dff), F32)] * 3,
        out_shape=(jax.ShapeDtypeStruct((t, dff), BF16), jax.ShapeDtypeStruct((t, d), BF16),
                   jax.ShapeDtypeStruct((t, d), F32), jax.ShapeDtypeStruct((t, nh), BF16),
                   jax.ShapeDtypeStruct((SUBLANES, d), F32), jax.ShapeDtypeStruct((SUBLANES, 128), F32)),
        in_specs=[_rows(tb, nh), _halo_prev(tb, nh), _full(fw.shape), _resident(w_down.shape),
                  _resident(w_down_t.shape), _rows(tb, d),
                  _rows(tb, d), _full(vd.shape)],
        out_specs=(_rows(tb, dff), _rows(tb, d), _rows(tb, d), _rows(tb, nh), _full((SUBLANES, d)),
                   _full((SUBLANES, 128))),
        compiler_params=_cparams(("arbitrary",), VMEM_BIG),
    )(up, up, fw, w_down, w_down_t, x2, tgt, vd)


def _ffn_up_bwd(dhid, up, fw, x2, dout, vec, w_up_st):
    t, nh = dhid.shape
    d = x2.shape[1]
    ns, _, nc = w_up_st.shape
    tb = _blk(t, TB_FFN)
    nblk = t // tb
    cw = 128

    def body(dh_ref, dhn_ref, up_ref, fw_ref, x2_ref, dout_ref, vec_ref, w_ref,
             dx2_ref, dup_ref, vp_ref, df_ref):
        i = pl.program_id(0)

        @pl.when(i == 0)
        def _():
            df_ref[...] = jnp.zeros(df_ref.shape, F32)
        dh2 = None
        for j in range(ns):
            for o in range(j * nc, (j + 1) * nc, cw):
                sl = slice(o, o + cw)
                dh = dh_ref[:, sl].astype(F32)
                dhn = jnp.where(i < nblk - 1, dhn_ref[:, sl].astype(F32), 0.0)
                dh1 = _shift_up(dh, dhn, 1)
                dh2s = _shift_up(dh, dhn, 2)
                dup_ref[:, sl] = (fw_ref[2:3, sl] * dh + fw_ref[1:2, sl] * dh1 + fw_ref[0:1, sl] * dh2s).astype(BF16)
                up_v = up_ref[:, sl]
                df_ref[0:1, sl] += _colsum(dh2s * up_v)
                df_ref[1:2, sl] += _colsum(dh1 * up_v)
                df_ref[2:3, sl] += _colsum(dh * up_v)
            pj = lax.dot_general(dup_ref[:, j * nc:(j + 1) * nc], w_ref[j], (((1,), (1,)), ((), ())),
                                 preferred_element_type=F32)
            dh2 = pj if dh2 is None else dh2 + pj
        xv = x2_ref[...]
        r = lax.rsqrt(_rowmean(xv * xv) + EPS)
        xn = xv * r
        g = vec_ref[0:1, :]
        hg = xn * g
        dhg = dh2 * vec_ref[1:2, :]
        _acc_rows(vp_ref, i == 0, [_colsum(dh2), _colsum(dh2 * hg), _colsum(dhg * xn)])
        dxn = dhg * g
        dx2_ref[...] = dout_ref[...] + r * (dxn - xn * _rowmean(dxn * xn))

    return pl.pallas_call(
        body, name="ffn_up_bwd", grid=(nblk,),
        out_shape=(jax.ShapeDtypeStruct((t, d), F32), jax.ShapeDtypeStruct((t, nh), BF16),
                   jax.ShapeDtypeStruct((SUBLANES, d), F32), jax.ShapeDtypeStruct((SUBLANES, nh), F32)),
        in_specs=[_rows(tb, nh), _halo_next(tb, nh, t, rows=BF16_ROWS), _rows(tb, nh), _full(fw.shape),
                  _rows(tb, d), _rows(tb, d), _full(vec.shape), _resident(w_up_st.shape)],
        out_specs=(_rows(tb, d), _rows(tb, nh), _full((SUBLANES, d)), _full((SUBLANES, nh))),
        compiler_params=_cparams(("arbitrary",), VMEM_BIG),
    )(dhid, dhid, up, fw, x2, dout, vec, w_up_st)


def _mix_out_bwd(dx2, o, y1, proj, s_re, s_im, c_re, c_im, v512, convw, glu_w, h16, h64, w_out, vd):
    t, d = dx2.shape
    dh = y1.shape[1]
    nstate = c_re.shape[0]
    du, ds = dh // SSM_SPLIT, nstate // SSM_SPLIT
    tb = _blk(t, TB_MIX)

    def body(dx2_ref, o_ref, y1_ref, u_ref, bg_ref, cg_ref, v_ref, cgh_ref, vh_ref, cre_ref, cim_ref, p_ref,
             cw_ref, gw_ref, h16_ref, h64_ref, wo_ref, vd_ref, sr_ref, si_ref,
             do_ref, ycat_ref, z_ref, dq_ref, dy1_ref, gr_ref, gi_ref, dcc_ref, dbg_ref, vpd_ref, vp5_ref,
             dcr_ref, dci_ref):
        i = pl.program_id(0)
        first = i == 0

        @pl.when(first)
        def _():
            dcr_ref[...] = jnp.zeros(dcr_ref.shape, F32)
            dci_ref[...] = jnp.zeros(dci_ref.shape, F32)
        ov = o_ref[...]
        ro = lax.rsqrt(_rowmean(ov * ov) + EPS)
        on_ = ov * ro
        g = vd_ref[0:1, :]
        dx2v = dx2_ref[...]
        don = dx2v * vd_ref[1:2, :]
        _acc_rows(vpd_ref, first, [_colsum(dx2v * on_ * g), _colsum(don * on_)])
        dxn = don * g
        dob = (ro * (dxn - on_ * _rowmean(dxn * on_))).astype(BF16)
        do_ref[...] = dob
        dyc_a =lax.dot_general(dob, wo_ref[0:dh, :], (((1,), (1,)), ((), ())), preferred_element_type=F32)
        dyc_b = lax.dot_general(dob, wo_ref[dh:2 * dh, :], (((1,), (1,)), ((), ())), preferred_element_type=F32)
        y1v = y1_ref[...]
        u = u_ref[...].astype(F32)
        z, dz_dy1 = _gelu_and_grad(y1v)
        zb = z.astype(BF16)
        sg = _sigmoid(jnp.dot(zb, gw_ref[...], preferred_element_type=F32) + p_ref[1:2, :])
        ya = z * sg
        ra = lax.rsqrt(_head_ms(ya, h16_ref) + EPS)
        yan = ya * ra
        ga = p_ref[2:3, :]
        ycat_ref[:, 0:dh] = (yan * ga).astype(BF16)
        dyn = dyc_a * ga
        dya = ra * (dyn - yan * _split_dot(dyn * yan, h16_ref[...]))
        dq = dya * z * sg * (1.0 - sg)
        dqb = dq.astype(BF16)
        z_ref[...] = zb
        dq_ref[...] = dqb
        dz = dya * sg + lax.dot_general(dqb, gw_ref[...], (((1,), (1,)), ((), ())), preferred_element_type=F32)
        dy1 = dz * dz_dy1
        dy1_ref[...] = dy1
        dy1b = dy1.astype(BF16)
        for q in range(SSM_SPLIT):
            rq, cq = slice(q * ds, (q + 1) * ds), slice(q * du, (q + 1) * du)
            gr_ref[:, rq] = lax.dot_general(dy1b[:, cq], cre_ref[rq, cq], (((1,), (1,)), ((), ())),
                                            preferred_element_type=F32).astype(BF16)
            gi_ref[:, rq] = (-lax.dot_general(dy1b[:, cq], cim_ref[rq, cq], (((1,), (1,)), ((), ())),
                                              preferred_element_type=F32)).astype(BF16)
            dcr_ref[rq, :] += _dot_tn(sr_ref[:, rq], dy1b[:, cq])
            dci_ref[rq, :] += _dot_tn(si_ref[:, rq], dy1b[:, cq])
        bg = bg_ref[...].astype(F32)
        cv = cg_ref[...].astype(F32) * v_ref[...].astype(F32)
        cvh = jnp.where(i > 0, cgh_ref[...].astype(F32) * vh_ref[...].astype(F32), 0.0)
        cv1 = _shift_down(cv, cvh, 1)
        cv2 = _shift_down(cv, cvh, 2)
        cc = cw_ref[0:1, :] * cv2 + cw_ref[1:2, :] * cv1 + cw_ref[2:3, :] * cv
        yb = bg * cc
        rb = lax.rsqrt(_head_ms(yb, h64_ref) + EPS)
        ybn = yb * rb
        gb = p_ref[3:4, :]
        ycat_ref[:, dh:2 * dh] = (ybn * gb).astype(BF16)
        dynb = dyc_b * gb
        dyb = rb * (dynb - ybn * _split_dot(dynb * ybn, h64_ref[...]))
        dcc = dyb * bg
        dbg_ref[...] = dyb * cc
        dcc_ref[...] = dcc
        _acc_rows(vp5_ref, first, [_colsum(dyc_a * yan), _colsum(dyc_b * ybn), _colsum(dq), _colsum(dy1 * u),
                                   _colsum(dcc * cv2), _colsum(dcc * cv1), _colsum(dcc * cv)])

    return pl.pallas_call(
        body, name="mix_out_bwd", grid=(t // tb,),
        out_shape=(jax.ShapeDtypeStruct((t, d), BF16), jax.ShapeDtypeStruct((t, 2 * dh), BF16),
                   jax.ShapeDtypeStruct((t, dh), BF16), jax.ShapeDtypeStruct((t, dh), BF16),
                   jax.ShapeDtypeStruct((t, dh), F32), jax.ShapeDtypeStruct((t, nstate), BF16),
                   jax.ShapeDtypeStruct((t, nstate), BF16), jax.ShapeDtypeStruct((t, dh), F32),
                   jax.ShapeDtypeStruct((t, dh), F32), jax.ShapeDtypeStruct((SUBLANES, d), F32),
                   jax.ShapeDtypeStruct((SUBLANES, dh), F32), jax.ShapeDtypeStruct((nstate, du), F32),
                   jax.ShapeDtypeStruct((nstate, du), F32)),
        in_specs=[_rows(tb, d), _rows(tb, d), _rows(tb, dh), _rows(tb, dh, 0), _rows(tb, dh, 1), _rows(tb, dh, 2),
                  _rows(tb, dh, 3), _halo_prev(tb, dh, 2, BF16_ROWS), _halo_prev(tb, dh, 3, BF16_ROWS), _resident(c_re.shape),
                  _resident(c_im.shape), _full(v512.shape), _full(convw.shape), _resident(glu_w.shape),
                  _resident(h16.shape), _resident(h64.shape), _resident(w_out.shape), _full(vd.shape),
                  _rows(tb, nstate), _rows(tb, nstate)],
        out_specs=(_rows(tb, d), _rows(tb, 2 * dh), _rows(tb, dh), _rows(tb, dh), _rows(tb, dh), _rows(tb, nstate),
                   _rows(tb, nstate), _rows(tb, dh), _rows(tb, dh), _full((SUBLANES, d)), _full((SUBLANES, dh)),
                   _full((nstate, du)), _full((nstate, du))),
        compiler_params=_cparams(("arbitrary",), VMEM_BIG),
    )(dx2, o, y1, proj, proj, proj, proj, proj, proj, c_re, c_im, v512, convw, glu_w, h16, h64, w_out, vd,
      s_re, s_im)


def _mix_in_bwd(gt_re, gt_im, b_re, b_im, dy1, dcc, dbg, proj, x, dx2, vec, v512, convw, w_in_st):
    t, d = x.shape
    dh = dy1.shape[1]
    nstate = gt_re.shape[1]
    du_w, ds = dh // SSM_SPLIT, nstate // SSM_SPLIT
    ns, _, nc = w_in_st.shape
    tb = _blk(t, TB_MIX)
    nblk = t // tb

    def body(gr_ref, gi_ref, bre_ref, bim_ref, dy1_ref, dcc_ref, dccn_ref, dbg_ref, u_ref, cg_ref, v_ref, x_ref,
             dx2_ref, vec_ref, p_ref, cw_ref, w_ref, gx_ref, dproj_ref, vp_ref, dbr_ref, dbi_ref):
        i = pl.program_id(0)

        @pl.when(i == 0)
        def _():
            dbr_ref[...] = jnp.zeros(dbr_ref.shape, F32)
            dbi_ref[...] = jnp.zeros(dbi_ref.shape, F32)
        ub = u_ref[...].astype(BF16)
        du = []
        for q in range(SSM_SPLIT):
            rq, cq = slice(q * du_w, (q + 1) * du_w), slice(q * ds, (q + 1) * ds)
            du.append(lax.dot_general(gr_ref[:, cq].astype(BF16), bre_ref[rq, cq], (((1,), (1,)), ((), ())),
                                      preferred_element_type=F32)
                      + lax.dot_general(gi_ref[:, cq].astype(BF16), bim_ref[rq, cq], (((1,), (1,)), ((), ())),
                                        preferred_element_type=F32))
            dbr_ref[rq, :] += _dot_tn(ub[:, rq], gr_ref[:, cq])
            dbi_ref[rq, :] += _dot_tn(ub[:, rq], gi_ref[:, cq])
        du = dy1_ref[...] * p_ref[0:1, :] + jnp.concatenate(du, axis=1)
        dcc = dcc_ref[...]
        dccn = jnp.where(i < nblk - 1, dccn_ref[...], 0.0)
        dcv = (cw_ref[2:3, :] * dcc + cw_ref[1:2, :] * _shift_up(dcc, dccn, 1)
               + cw_ref[0:1, :] * _shift_up(dcc, dccn, 2))
        parts = [du, dbg_ref[...], dcv * v_ref[...].astype(F32), dcv * cg_ref[...].astype(F32)]
        xv = x_ref[...]
        r = lax.rsqrt(_rowmean(xv * xv) + EPS)
        xn = xv * r
        g = vec_ref[0:1, :]
        hg = xn * g
        dh1 = None
        for j in range(ns):
            pb = parts[j].astype(BF16)
            dproj_ref[:, j * nc:(j + 1) * nc] = pb
            pj =lax.dot_general(pb, w_ref[j], (((1,), (1,)), ((), ())), preferred_element_type=F32)
            dh1 = pj if dh1 is None else dh1 + pj
        dhg = dh1 * vec_ref[1:2, :]
        _acc_rows(vp_ref, i == 0, [_colsum(dh1), _colsum(dh1 * hg), _colsum(dhg * xn)])
        dxn = dhg * g
        gx_ref[...] = dx2_ref[...] + r * (dxn - xn * _rowmean(dxn * xn))

    assert nc == dh and ns == 4
    return pl.pallas_call(
        body, name="mix_in_bwd", grid=(nblk,),
        out_shape=(jax.ShapeDtypeStruct((t, d), F32), jax.ShapeDtypeStruct((t, ns * nc), BF16),
                   jax.ShapeDtypeStruct((SUBLANES, d), F32), jax.ShapeDtypeStruct((dh, ds), F32),
                   jax.ShapeDtypeStruct((dh, ds), F32)),
        in_specs=[_rows(tb, nstate), _rows(tb, nstate), _resident(b_re.shape), _resident(b_im.shape), _rows(tb, dh),
                  _rows(tb, dh), _halo_next(tb, dh, t), _rows(tb, dh), _rows(tb, dh, 0), _rows(tb, dh, 2),
                  _rows(tb, dh, 3), _rows(tb, d), _rows(tb, d), _full(vec.shape), _full(v512.shape),
                  _full(convw.shape), _resident(w_in_st.shape)],
        out_specs=(_rows(tb, d), _rows(tb, ns * nc), _full((SUBLANES, d)), _full((dh, ds)), _full((dh, ds))),
        compiler_params=_cparams(("arbitrary",), VMEM_BIG),
    )(gt_re, gt_im, b_re, b_im, dy1, dcc, dcc, dbg, proj, proj, proj, x, dx2, vec, v512, convw, w_in_st)


def _matmul_tn(a, b, m, bn, out_dtype, name, diag=False, bt=TB_TN, after=None):
    t = a.shape[0]
    n = b.shape[1]
    bt = _blk(t, bt)
    nk = t // bt
    extra = [] if after is None else [after]
    a_map = (lambda j, k: (k, j)) if diag else (lambda j, k: (k, 0))

    def body(a_ref, b_ref, *rest):
        o_ref, acc_ref = rest[-2:]
        k = pl.program_id(1)

        @pl.when(k == 0)
        def _():
            acc_ref[...] = jnp.zeros(acc_ref.shape, F32)
        acc_ref[...] += _dot_tn(a_ref[...], b_ref[...])

        @pl.when(k == nk - 1)
        def _():
            o_ref[...] = acc_ref[...].astype(out_dtype)

    return pl.pallas_call(
        body, name=name, grid=(n // bn, nk),
        out_shape=jax.ShapeDtypeStruct((n // bn, m, bn), out_dtype),
        in_specs=[pl.BlockSpec((bt, m), a_map), pl.BlockSpec((bt, bn), lambda j, k: (k, j))]
        + [pl.BlockSpec(memory_space=pl.ANY)] * len(extra),
        out_specs=pl.BlockSpec((None, m, bn), lambda j, k: (j, 0, 0)),
        scratch_shapes=[pltpu.VMEM((m, bn), F32)],
        compiler_params=_cparams(("parallel", "arbitrary"), VMEM_BIG),
    )(a, b, *extra)


def _ssm_bgrad(d_bre, d_bim, bt_re, bt_im, rows_in, fold, tile_b):
    gh, cb = d_bre.shape
    nb = SSM_SPLIT
    rb = gh // nb
    gp = nb * cb
    p = fold.shape[1]

    def body(dr_ref, di_ref, br_ref, bi_ref, rin_ref, f_ref, tb_ref, dbr_ref, dbi_ref, rout_ref):
        row = lax.broadcasted_iota(jnp.int32, (rb, cb), 0)
        col = lax.broadcasted_iota(jnp.int32, (rb, cb), 1)
        mask = (row >> 4) == (col >> 6)
        gr = jnp.where(mask, dr_ref[...], 0.0)
        gi = jnp.where(mask, di_ref[...], 0.0)
        cr, ci = rin_ref[0:1, :], rin_ref[1:2, :]
        dbr_ref[...] = _split3_dot(cr * gr + ci * gi, f_ref[...])
        dbi_ref[...] = _split3_dot(cr * gi - ci * gr, f_ref[...])
        br = _split3_dot(br_ref[...], tb_ref[...])
        bi = _split3_dot(bi_ref[...], tb_ref[...])
        rout_ref[...] = jnp.zeros(rout_ref.shape, F32)
        rout_ref[0:1, :] = _colsum(br * gr + bi * gi)
        rout_ref[1:2, :] = _colsum(br * gi - bi * gr)

    dspec = pl.BlockSpec((rb, cb), lambda j: (j, 0))
    rspec = pl.BlockSpec((SUBLANES, cb), lambda j: (0, j))
    ospec = pl.BlockSpec((rb, p), lambda j: (j, 0))
    return pl.pallas_call(
        body, name="ssm_bgrad", grid=(nb,),
        out_shape=(jax.ShapeDtypeStruct((gh, p), F32), jax.ShapeDtypeStruct((gh, p), F32),
                   jax.ShapeDtypeStruct((SUBLANES, gp), F32)),
        in_specs=[dspec, dspec, ospec, ospec, rspec, _full(fold.shape), _full(tile_b.shape)],
        out_specs=(ospec, ospec, rspec),
        compiler_params=_cparams(("parallel",)),
    )(d_bre, d_bim, bt_re, bt_im, rows_in, fold, tile_b)


def _ssm_cgrad(d_cre, d_cim, fold):
    gp, cb = d_cre.shape
    nb = SSM_SPLIT
    rb = gp // nb
    h = fold.shape[1]

    def body(dr_ref, di_ref, f_ref, cr_ref, ci_ref):
        row = lax.broadcasted_iota(jnp.int32, (rb, cb), 0)
        col = lax.broadcasted_iota(jnp.int32, (rb, cb), 1)
        mask = (row >> 6) == (col >> 4)
        cr_ref[...] = _split3_dot(jnp.where(mask, dr_ref[...], 0.0), f_ref[...])
        ci_ref[...] = -_split3_dot(jnp.where(mask, di_ref[...], 0.0), f_ref[...])

    cspec = pl.BlockSpec((rb, cb), lambda j: (j, 0))
    ospec = pl.BlockSpec((rb, h), lambda j: (j, 0))
    return pl.pallas_call(
        body, name="ssm_cgrad", grid=(nb,),
        out_shape=(jax.ShapeDtypeStruct((gp, h), F32),) * 2,
        in_specs=[cspec, cspec, _full(fold.shape)], out_specs=(ospec, ospec),
        compiler_params=_cparams(("parallel",)),
    )(d_cre, d_cim, fold)


def _ssm_lamgrad(lam_re, lam_im, log_step, abar_re, abar_im, coef_re, coef_im, gc_re, gc_im, ga_re, ga_im):
    g, p = lam_re.shape

    def body(lr_ref, li_ref, ls_ref, ar_ref, ai_ref, cr_ref, ci_ref, gcr_ref, gci_ref, gar_ref, gai_ref,
             dlr_ref, dli_ref, dls_ref):
        lam_raw = lr_ref[...]
        lr = jnp.minimum(lam_raw, LAMBDA_RE_MAX)
        li = li_ref[...]
        st = jnp.exp(ls_ref[...])
        den = lr * lr + li * li
        gcr, gci = gcr_ref[...], gci_ref[...]
        gab_r = gar_ref[...] + (lr * gcr - li * gci) / den
        gab_i = gai_ref[...] + (lr * gci + li * gcr) / den
        cr, ci = cr_ref[...], ci_ref[...]
        wr = -(cr * lr + ci * li) / den
        wi = -(ci * lr - cr * li) / den
        gl_r = wr * gcr + wi * gci
        gl_i = wr * gci - wi * gcr
        ar, ai = ar_ref[...], ai_ref[...]
        gw_r = ar * gab_r + ai * gab_i
        gw_i = ar * gab_i - ai * gab_r
        gl_r = gl_r + st * gw_r
        gl_i = gl_i + st * gw_i
        pass_through = jnp.where(lam_raw < LAMBDA_RE_MAX, 1.0, jnp.where(lam_raw == LAMBDA_RE_MAX, 0.5, 0.0))
        dlr_ref[...] = gl_r * pass_through
        dli_ref[...] = gl_i
        dls_ref[...] = st * jnp.sum(lr * gw_r + li * gw_i, axis=1, keepdims=True)

    sds = jax.ShapeDtypeStruct((g, p), F32)
    return pl.pallas_call(body, name="ssm_lamgrad", out_shape=(sds, sds, jax.ShapeDtypeStruct((g, 1), F32)))(
        lam_re, lam_im, log_step, abar_re, abar_im, coef_re, coef_im, gc_re, gc_im, ga_re, ga_im)


def _row_block(r, most=512):
    for rb in range(min(r, most), BF16_ROWS - 1, -1):
        if r % rb == 0 and rb % BF16_ROWS == 0:
            return rb
    return r


def _adamw_math(w, g, m, v):
    m = ADAM_B1 * m + (1.0 - ADAM_B1) * g
    v = ADAM_B2 * v + (1.0 - ADAM_B2) * (g * g)
    m_hat = m / (1.0 - ADAM_B1 ** ADAM_STEP)
    v_hat = v / (1.0 - ADAM_B2 ** ADAM_STEP)
    delta = -ADAM_LR * (m_hat / (jnp.sqrt(v_hat) + ADAM_EPS) + ADAM_WD * w)
    return delta, m, v


def _adamw_big(p_mine, p_sib, w, m, v, name):
    r, c = w.shape
    rb = _row_block(r)

    def body(a_ref, b_ref, w_ref, m_ref, v_ref, g_ref, d_ref, mo_ref, vo_ref):
        g = a_ref[...].astype(F32) + b_ref[...].astype(F32)
        g_ref[...] = g
        d_ref[...], mo_ref[...], vo_ref[...] = _adamw_math(w_ref[...], g, m_ref[...], v_ref[...])

    spec = pl.BlockSpec((rb, c), lambda i: (i, 0))
    sds = jax.ShapeDtypeStruct((r, c), F32)
    return pl.pallas_call(
        body, name=name, grid=(r // rb,), out_shape=(sds,) * 4, in_specs=[spec] * 5, out_specs=(spec,) * 4,
        compiler_params=_cparams(("parallel",), VMEM_KEEP_OPERANDS_IN_HBM),
    )(p_mine, p_sib, w, m, v)


def _sum_blocks(stack, name):
    n, r, c = stack.shape
    rb = _row_block(r)

    def body(s_ref, o_ref):
        acc = s_ref[0].astype(F32)
        for k in range(1, n):
            acc = acc + s_ref[k].astype(F32)
        o_ref[...] = acc

    return pl.pallas_call(
        body, name=name, grid=(r // rb,), out_shape=jax.ShapeDtypeStruct((r, c), F32),
        in_specs=[pl.BlockSpec((n, rb, c), lambda i: (0, i, 0))], out_specs=pl.BlockSpec((rb, c), lambda i: (i, 0)),
        compiler_params=_cparams(("parallel",), VMEM_KEEP_OPERANDS_IN_HBM),
    )(stack)


def _sum_landed(landed, own, chip, name):
    n, r, c = landed.shape
    rb = _row_block(r)

    def body(chip_ref, own_ref, l1_ref, l2_ref, l3_ref, o_ref):
        acc = own_ref[0].astype(F32)
        for ref in (l1_ref, l2_ref, l3_ref):
            acc = acc + ref[0].astype(F32)
        o_ref[...] = acc.astype(BF16)

    def slot(k):
        return pl.BlockSpec((1, rb, c), lambda i, ch: ((ch[0] + k) % n, i, 0))

    return pl.pallas_call(
        body, name=name, out_shape=jax.ShapeDtypeStruct((r, c), BF16),
        grid_spec=pltpu.PrefetchScalarGridSpec(
            num_scalar_prefetch=1, grid=(r // rb,), in_specs=[slot(0), slot(1), slot(2), slot(3)],
            out_specs=pl.BlockSpec((rb, c), lambda i, ch: (i, 0))),
        compiler_params=_cparams(("parallel",), VMEM_KEEP_OPERANDS_IN_HBM),
    )(jnp.reshape(chip, (1,)).astype(jnp.int32), own, landed, landed, landed)


def _add2(a, b):
    def body(a_ref, b_ref, o_ref):
        o_ref[...] = a_ref[...] + b_ref[...]

    return pl.pallas_call(body, name="add_small", out_shape=jax.ShapeDtypeStruct(a.shape, F32))(a, b)


def _adamw_ada(c_all, dmod_cols, w, m, v):
    d, n = w.shape
    bn = 512

    def body(c_ref, dm_ref, w_ref, m_ref, v_ref, g_ref, d_ref, mo_ref, vo_ref):
        cc = c_ref[...]
        g = _dot_tn(cc * _sigmoid(cc), dm_ref[...])
        g_ref[...] = g
        d_ref[...], mo_ref[...], vo_ref[...] = _adamw_math(w_ref[...], g, m_ref[...], v_ref[...])

    spec = pl.BlockSpec((d, bn), lambda j: (0, j))
    sds = jax.ShapeDtypeStruct((d, n), F32)
    return pl.pallas_call(
        body, name="adamw_ada", grid=(n // bn,), out_shape=(sds,) * 4,
        in_specs=[_full((N_DEV, d)), pl.BlockSpec((N_DEV, bn), lambda j: (0, j)), spec, spec, spec],
        out_specs=(spec,) * 4, compiler_params=_cparams(("parallel",), VMEM_KEEP_OPERANDS_IN_HBM),
    )(c_all, dmod_cols, w, m, v)


def _adamw_small(items):
    n = len(items)

    def body(*refs):
        ins, outs = refs[:4 * n], refs[4 * n:]
        for k in range(n):
            w_ref, g_ref, m_ref, v_ref = ins[4 * k:4 * k + 4]
            outs[3 * k][...], outs[3 * k + 1][...], outs[3 * k + 2][...] = _adamw_math(
                w_ref[...], g_ref[...], m_ref[...], v_ref[...])

    flat = [a for it in items for a in it]
    out_shape = tuple(jax.ShapeDtypeStruct(it[0].shape, F32) for it in items for _ in range(3))
    res = pl.pallas_call(body, name="adamw_small", out_shape=out_shape,
                         compiler_params=_cparams(vmem=VMEM_KEEP_OPERANDS_IN_HBM))(*flat)
    return [tuple(res[3 * k:3 * k + 3]) for k in range(n)]


def _group_mean_matrix(n, group):
    idx = np.arange(n) // group
    return (idx[:, None] == idx[None, :]).astype(np.float32) / group


def _fold_matrix(n, period):
    return (np.arange(n)[:, None] % period == np.arange(period)[None, :]).astype(np.float32)


def _rows8(*rows):
    c = rows[0].shape[-1]
    pad = jnp.zeros((SUBLANES - len(rows), c), F32)
    return jnp.concatenate([r.reshape(1, c) for r in rows] + [pad], axis=0)


def _to_rows(a, width):
    flat = a.reshape(-1)
    n = -(-flat.shape[0] // width)
    flat = jnp.pad(flat, (0, n * width - flat.shape[0]))
    return flat.reshape(n, width)


def kernel(x, c, w_ada, b_ada, g_pre_mix, g_post_mix, w_in, ssm_lam_re, ssm_lam_im, ssm_log_step, ssm_b_re, ssm_b_im, ssm_c_re, ssm_c_im, ssm_d, glu_w, glu_b, g_out_ssm, conv_w, g_out_conv, w_out, g_pre_ffn, g_post_ffn, w_up, ffn_conv_w, w_down, loss_target, m_w_ada, m_b_ada, m_g_pre_mix, m_g_post_mix, m_w_in, m_ssm_lam_re, m_ssm_lam_im, m_ssm_log_step, m_ssm_b_re, m_ssm_b_im, m_ssm_c_re, m_ssm_c_im, m_ssm_d, m_glu_w, m_glu_b, m_g_out_ssm, m_conv_w, m_g_out_conv, m_w_out, m_g_pre_ffn, m_g_post_ffn, m_w_up, m_ffn_conv_w, m_w_down, v_w_ada, v_b_ada, v_g_pre_mix, v_g_post_mix, v_w_in, v_ssm_lam_re, v_ssm_lam_im, v_ssm_log_step, v_ssm_b_re, v_ssm_b_im, v_ssm_c_re, v_ssm_c_im, v_ssm_d, v_glu_w, v_glu_b, v_g_out_ssm, v_conv_w, v_g_out_conv, v_w_out, v_g_pre_ffn, v_g_post_ffn, v_w_up, v_ffn_conv_w, v_w_down):
    xs = x[0]
    tgt = loss_target[0]
    t, d = xs.shape
    xi, yi, ci = lax.axis_index("x"), lax.axis_index("y"), lax.axis_index("c")
    chip = 2 * xi + yi
    dev = 2 * chip + ci

    n_groups, n_state = ssm_lam_re.shape[1:]
    n_gch = ssm_b_re.shape[3]
    d_ssm = n_groups * n_gch
    gp = n_groups * n_state
    n_ada = w_ada.shape[2]
    d_ff = w_down.shape[1] * N_CHIPS
    n_upc = w_up.shape[2]

    w_names = ("w_in", "glu_w", "w_out", "w_up", "w_down")
    c_gath = _allgather8(jnp.broadcast_to(c, (SUBLANES, d)), "gather_c")
    c_all = c_gath.reshape(N_DEV, SUBLANES, d)[:, 0, :]

    def pad8(a):
        return jnp.concatenate([a, jnp.zeros((SUBLANES - a.shape[0], a.shape[1]), a.dtype)], axis=0)

    def start(name, arrs, after):
        return _chips_start(name, True, [], [_landing(a, chip) for a in arrs], after)

    w_names = ("w_in", "mod", "conv_w", "ffn_conv_w", "glu_w", "w_out", "w_up", "w_down")
    first = start("weights_start_in", [w_in[0].astype(BF16)], c_gath)
    b_sh = lax.dynamic_slice(b_ada, (0, chip * n_ada), (1, n_ada))
    mod_sh = _mod_shard(c_all + first[4][0:1, 0:1], w_ada[0], b_sh)
    second = start("weights_start_rest", [mod_sh, pad8(conv_w[0]), pad8(ffn_conv_w[0])]
                   + [w[0].astype(BF16) for w in (glu_w, w_out, w_up, w_down)], None)
    w_send, w_recv, w_land = [list(first[k]) + list(second[k]) for k in (0, 1, 3)]
    w_token = second[4]

    def weights(names, after):
        ks = [w_names.index(nm) for nm in names]
        return _chips_wait("weights_wait_" + names[-1], True, [w_send[k] for k in ks], [w_recv[k] for k in ks],
                           [], [w_land[k] for k in ks], after)[1]

    lam_re, lam_im = ssm_lam_re[0], ssm_lam_im[0]
    log_step = ssm_log_step[0].reshape(n_groups, 1) + w_token[0:1, 0:1]
    abar_re, abar_im, coef_re, coef_im = _ssm_prep(lam_re, lam_im, log_step)
    a_rows = _rows8(abar_re.reshape(1, gp), abar_im.reshape(1, gp))
    coef_rows = _rows8(coef_re.reshape(1, gp), coef_im.reshape(1, gp))
    bt_re = ssm_b_re[0].transpose(0, 2, 1).reshape(d_ssm, n_state)
    bt_im = ssm_b_im[0].transpose(0, 2, 1).reshape(d_ssm, n_state)
    ct_re = ssm_c_re[0].transpose(0, 2, 1).reshape(gp, n_gch)
    ct_im = ssm_c_im[0].transpose(0, 2, 1).reshape(gp, n_gch)
    tile_b = jnp.asarray(np.tile(np.eye(n_state), (1, n_groups // SSM_SPLIT)), BF16)
    tile_c = jnp.asarray(np.tile(np.eye(n_gch), (1, n_groups)), BF16)
    bblk_re, bblk_im, cblk_re, cblk_im = _ssm_blocks(bt_re, bt_im, ct_re, ct_im, coef_rows, tile_b, tile_c)

    h16 = jnp.asarray(_group_mean_matrix(d_ssm, n_gch), BF16)
    h64 = jnp.asarray(_group_mean_matrix(d_ssm, CONV_HEAD_DIM), BF16)

    g_mod, g_cw, g_fw, w_in_st = weights(("mod", "conv_w", "ffn_conv_w", "w_in"), bblk_re)
    mod_all = g_mod.transpose(1, 0, 2).reshape(N_DEV, N_CHIPS * n_ada)
    mod = lax.dynamic_slice(mod_all, (dev, 0), (1, N_CHIPS * n_ada))
    sh1, sc1, gt1, sh2, sc2, gt2 = [mod[:, k * d:(k + 1) * d] for k in range(6)]
    convw_full = pad8(g_cw[:, :3, :].transpose(1, 0, 2).reshape(3, d_ssm))
    fw_full = pad8(g_fw[:, :3, :].transpose(1, 0, 2).reshape(3, N_CHIPS * n_upc))

    v512 = _rows8(ssm_d, glu_b, g_out_ssm, g_out_conv)
    vec1 =_rows8(g_pre_mix, 1.0 + sc1, sh1)
    vd1 = _rows8(g_post_mix, gt1)
    vec2 = _rows8(g_pre_ffn, 1.0 + sc2, sh2)
    vd2 = _rows8(g_post_ffn, gt2)

    proj, bu_re, bu_im, h1b = _mix_in(xs, vec1, w_in_st, bblk_re, bblk_im)
    s_re, s_im = _scan_fwd(a_rows, bu_re, bu_im)
    g_glu, g_wout = weights(("glu_w", "w_out"), s_re)
    glu_full = g_glu.reshape(d_ssm, d_ssm)
    w_out_full = g_wout.reshape(2 * d_ssm, d)
    y1, o_mix, x2 = _mix_out(xs, proj, s_re, s_im, cblk_re, cblk_im, v512, convw_full, glu_full, h16, h64,
                             w_out_full, vd1)
    (w_up_st,) = weights(("w_up",), x2)
    up, h2b = _ffn_up(x2, vec2, w_up_st)
    (g_wdown,) = weights(("w_down",), up)
    w_down_full = g_wdown.reshape(d_ff, d)
    actb, ddnb, dout, dhid, vp_dn, loss_blk = _ffn_down(up, fw_full, w_down_full, w_down_full.T, x2, tgt, vd2)

    gw_down = _matmul_tn(actb, ddnb, d_ff, d, BF16, "dw_down", bt=1024).reshape(N_CHIPS, d_ff // N_CHIPS, d)
    dx2, dupb, vp_up, df_rows = _ffn_up_bwd(dhid, up, fw_full, x2, dout, vec2, w_up_st)
    gw_up = _matmul_tn(h2b, dupb, d, n_upc, BF16, "dw_up", bt=2048)
    ga_send, ga_recv, ga_src, ga_land, ga_token = _chips_start(
        "grads_start_ffn", False, [gw_down, gw_up], [lax.empty(g.shape, g.dtype) for g in (gw_down, gw_up)])
    (dob, ycatb, zb, dqb, dy1, g_re, g_im, dcc, dbg, vp_mo, vp5, d_cre, d_cim) = _mix_out_bwd(
        dx2, o_mix, y1, proj, s_re, s_im, cblk_re, cblk_im, v512, convw_full, glu_full, h16, h64, w_out_full,
        vd1 + ga_token[0:1, 0:1])
    gw_out = _matmul_tn(ycatb, dob, 2 * d_ssm, d, BF16, "dw_out", bt=2048)
    gw_out = gw_out.reshape(N_CHIPS, 2 * d_ssm // N_CHIPS, d)
    gw_glu = _matmul_tn(zb, dqb, d_ssm, d_ssm, BF16, "dw_glu", bt=2048).reshape(N_CHIPS, d_ssm // N_CHIPS, d_ssm)
    gb_send, gb_recv, gb_src, gb_land, gb_token = _chips_start(
        "grads_start_mix", False, [gw_out, gw_glu], [lax.empty(g.shape, g.dtype) for g in (gw_out, gw_glu)])
    gt_re, gt_im, ga_re8, ga_im8 = _scan_bwd(a_rows + gb_token[0:1, 0:1], g_re, g_im, s_re, s_im)
    grad_x, dprojb, vp_mi, d_bre, d_bim = _mix_in_bwd(gt_re, gt_im, bblk_re, bblk_im, dy1, dcc, dbg, proj, xs, dx2,
                                                      vec1, v512, convw_full, w_in_st)
    ssm_u, ssm_s = d_ssm // SSM_SPLIT, gp // SSM_SPLIT

    fold_b = jnp.asarray(_fold_matrix(ssm_s, n_state), BF16)
    fold_c = jnp.asarray(_fold_matrix(ssm_u, n_gch), BF16)
    db_re_f, db_im_f, gc_rows = _ssm_bgrad(d_bre, d_bim, bt_re, bt_im, coef_rows, fold_b, tile_b)
    dc_re_f, dc_im_f = _ssm_cgrad(d_cre, d_cim, fold_c)
    ga_sum = _ga_rowsum(ga_re8, ga_im8)
    g_lam_re, g_lam_im, g_log_step = _ssm_lamgrad(
        lam_re, lam_im, log_step, abar_re, abar_im, coef_re, coef_im,
        gc_rows[0].reshape(n_groups, n_state), gc_rows[1].reshape(n_groups, n_state),
        ga_sum[0].reshape(n_groups, n_state), ga_sum[1].reshape(n_groups, n_state))
    g_b_re = db_re_f.reshape(n_groups, n_gch, n_state).transpose(0, 2, 1)
    g_b_im = db_im_f.reshape(n_groups, n_gch, n_state).transpose(0, 2, 1)
    g_c_re = dc_re_f.reshape(n_groups, n_state, n_gch).transpose(0, 2, 1)
    g_c_im = dc_im_f.reshape(n_groups, n_state, n_gch).transpose(0, 2, 1)

    dmod = jnp.concatenate([vp_mi[0:1], vp_mi[1:2], vp_mo[0:1], vp_up[0:1], vp_up[1:2], vp_dn[0:1]], axis=1)
    small = [
        ("g_pre_mix", vp_mi[2:3]), ("g_post_mix", vp_mo[1:2]), ("g_pre_ffn", vp_up[2:3]), ("g_post_ffn", vp_dn[1:2]),
        ("ssm_lam_re", g_lam_re), ("ssm_lam_im", g_lam_im), ("ssm_log_step", g_log_step),
        ("ssm_b_re", g_b_re), ("ssm_b_im", g_b_im), ("ssm_c_re", g_c_re), ("ssm_c_im", g_c_im),
        ("ssm_d", vp5[3:4]), ("glu_b", vp5[2:3]), ("g_out_ssm", vp5[0:1]), ("g_out_conv", vp5[1:2]),
        ("conv_w", vp5[4:7]), ("ffn_conv_w", df_rows[0:3]), ("loss", loss_blk[0:1, 0:1]),
    ]
    packed, offsets, row = [], {}, 0
    for name, a in small:
        r = _to_rows(a, d)
        offsets[name] = (row, a.shape)
        packed.append(r)
        row += r.shape[0]
    n_small = -(-row // SUBLANES) * SUBLANES
    packed.append(jnp.zeros((n_small - row, d), F32))
    packed.append(pad8(dmod.reshape(6, d)))
    pack = jnp.concatenate(packed, axis=0)
    sm_send, sm_recv, _, sm_land, sm_token = _chips_start("small_start", True, [], [_landing(pack, chip)])

    gw_in = _matmul_tn(h1b, dprojb, d, w_in.shape[2], BF16, "dw_in", bt=2048, after=sm_token)
    gc_send, gc_recv, gc_src, gc_land, gc_token = _chips_start(
        "grads_start_in", False, [gw_in], [lax.empty(gw_in.shape, gw_in.dtype)])

    def partials(names, own, landed):
        return [_sum_landed(l, o, chip, "sum_" + nm) for l, o, nm in zip(landed, own, names)]

    def update(names, mine, theirs):
        done = {}
        for nm, pm, ps in zip(names, mine, theirs):
            w_, m_, v_ = big_params[nm]
            done[nm] = _adamw_big(pm, ps, w_[0], m_[0], v_[0], "adamw_" + nm)
        return done

    big_params = {"w_down": (w_down, m_w_down, v_w_down), "w_up": (w_up, m_w_up, v_w_up),
                  "w_out": (w_out, m_w_out, v_w_out), "glu_w": (glu_w, m_glu_w, v_glu_w),
                  "w_in": (w_in, m_w_in, v_w_in)}
    ffn_names, mix_names = ("w_down", "w_up"), ("w_out", "glu_w", "w_in")
    p_ffn = partials(ffn_names, *_chips_wait("grads_wait_ffn", False, ga_send, ga_recv, ga_src, ga_land, gc_token))
    sa_send, sa_recv, sa_src, sa_land, sa_token = _sibling_start("swap_start_ffn", p_ffn)

    (sm_landed,) = _chips_wait("small_wait", True, sm_send, sm_recv, [], sm_land, sa_token)[1]
    sm_part = _sum_blocks(sm_landed, "sum_small")
    dmod_mine = sm_landed[:, n_small:n_small + SUBLANES, :]
    ss_send, ss_recv, ss_src, ss_land, ss_token = _sibling_start("swap_start_small", [sm_part, dmod_mine])
    p_ffn, t_ffn = _sibling_wait("swap_wait_ffn", sa_send, sa_recv, sa_src, sa_land, ss_token)
    big = update(ffn_names, p_ffn, t_ffn)
    (sm_part, dmod_mine), (sm_sib, dmod_sib) = _sibling_wait("swap_wait_small", ss_send, ss_recv, ss_src, ss_land,
                                                              big["w_up"][0])
    sums = _add2(sm_part, sm_sib)
    dmod_by_core = jnp.stack([dmod_mine, dmod_sib], axis=1)
    dmod_by_core = jnp.where(ci == 0, dmod_by_core, dmod_by_core[:, ::-1])
    dmod_all = dmod_by_core[:, :, :6, :].reshape(N_DEV, 6 * d)
    g_b_ada = sums[n_small:n_small + 6].reshape(1, 6 * d)

    def unpack(name):
        r0, shape = offsets[name]
        size = math.prod(shape)
        nrow = -(-size // d)
        return sums[r0:r0 + nrow].reshape(-1)[:size].reshape(shape)

    p_mix = partials(mix_names, *_chips_wait(
        "grads_wait_mix", False, list(gb_send) + list(gc_send), list(gb_recv) + list(gc_recv),
        list(gb_src) + list(gc_src), list(gb_land) + list(gc_land), sums))
    sb_send, sb_recv, sb_src, sb_land, sb_token = _sibling_start("swap_start_mix", p_mix)

    dmod_cols = lax.dynamic_slice(dmod_all, (0, chip * n_ada), (N_DEV, n_ada)) + sb_token[0:1, 0:1]
    ada = _adamw_ada(c_all, dmod_cols, w_ada[0], m_w_ada[0], v_w_ada[0])
    p_mix, t_mix = _sibling_wait("swap_wait_mix", sb_send, sb_recv, sb_src, sb_land, ada[0])
    big.update(update(mix_names, p_mix, t_mix))

    g_small = {name: unpack(name) for name, _ in small}
    g_small["b_ada"] = g_b_ada
    g_small["conv_w"] = lax.dynamic_slice(g_small["conv_w"], (0, chip * conv_w.shape[2]), (3, conv_w.shape[2]))
    g_small["ffn_conv_w"] = lax.dynamic_slice(g_small["ffn_conv_w"], (0, chip * n_upc), (3, n_upc))
    g_small["ssm_log_step"] = g_small["ssm_log_step"].reshape(1, n_groups)
    small_params = {
        "b_ada": (b_ada, m_b_ada, v_b_ada), "g_pre_mix": (g_pre_mix, m_g_pre_mix, v_g_pre_mix),
        "g_post_mix": (g_post_mix, m_g_post_mix, v_g_post_mix), "ssm_lam_re": (ssm_lam_re, m_ssm_lam_re, v_ssm_lam_re),
        "ssm_lam_im": (ssm_lam_im, m_ssm_lam_im, v_ssm_lam_im),
        "ssm_log_step": (ssm_log_step, m_ssm_log_step, v_ssm_log_step),
        "ssm_b_re": (ssm_b_re, m_ssm_b_re, v_ssm_b_re), "ssm_b_im": (ssm_b_im, m_ssm_b_im, v_ssm_b_im),
        "ssm_c_re": (ssm_c_re, m_ssm_c_re, v_ssm_c_re), "ssm_c_im": (ssm_c_im, m_ssm_c_im, v_ssm_c_im),
        "ssm_d": (ssm_d, m_ssm_d, v_ssm_d), "glu_b": (glu_b, m_glu_b, v_glu_b),
        "g_out_ssm": (g_out_ssm, m_g_out_ssm, v_g_out_ssm), "conv_w": (conv_w, m_conv_w, v_conv_w),
        "g_out_conv": (g_out_conv, m_g_out_conv, v_g_out_conv), "g_pre_ffn": (g_pre_ffn, m_g_pre_ffn, v_g_pre_ffn),
        "g_post_ffn": (g_post_ffn, m_g_post_ffn, v_g_post_ffn),
        "ffn_conv_w": (ffn_conv_w, m_ffn_conv_w, v_ffn_conv_w),
    }

    def natural(a):
        return a[0] if a.ndim > 2 else a

    names = list(small_params)
    items = []
    for nm in names:
        w_, m_, v_ = small_params[nm]
        items.append((natural(w_), g_small[nm].reshape(natural(w_).shape), natural(m_), natural(v_)))
    upd = _adamw_small(items)
    small_out = {}
    for nm, (dl, mo, vo) in zip(names, upd):
        shp = small_params[nm][0].shape
        small_out[nm] = (g_small[nm].reshape(shp), dl.reshape(shp), mo.reshape(shp), vo.reshape(shp))

    loss = g_small["loss"][0, 0]

    order = ["w_ada", "b_ada", "g_pre_mix", "g_post_mix", "w_in", "ssm_lam_re", "ssm_lam_im", "ssm_log_step",
             "ssm_b_re", "ssm_b_im", "ssm_c_re", "ssm_c_im", "ssm_d", "glu_w", "glu_b", "g_out_ssm", "conv_w",
             "g_out_conv", "w_out", "g_pre_ffn", "g_post_ffn", "w_up", "ffn_conv_w", "w_down"]
    results = {"w_ada": tuple(a[None] for a in ada)}
    for nm in big:
        results[nm] = tuple(a[None] for a in big[nm])
    results.update(small_out)
    outs = [loss, grad_x[None]]
    for k in range(4):
        outs += [results[nm][k] for nm in order]
    return tuple(outs)


def _ga_rowsum(ga_re8, ga_im8):
    n = ga_re8.shape[1]

    def body(r_ref, i_ref, o_ref):
        o_ref[...] = jnp.zeros(o_ref.shape, F32)
        o_ref[0:1, :] = _colsum(r_ref[...])
        o_ref[1:2, :] = _colsum(i_ref[...])

    return pl.pallas_call(body, name="ga_rowsum", out_shape=jax.ShapeDtypeStruct((SUBLANES, n), F32))(ga_re8, ga_im8)
```

```python
import functools
import math

import jax
import jax.numpy as jnp
import numpy as np
from jax import lax
from jax.experimental import pallas as pl
from jax.experimental.pallas import tpu as pltpu

F32 = jnp.float32
BF16 = jnp.bfloat16
MESH = pl.DeviceIdType.MESH

EPS = 1e-6
LAMBDA_RE_MAX = -1e-4
ADAM_LR = 0.001
ADAM_B1 = 0.9
ADAM_B2 = 0.999
ADAM_EPS = 1e-08
ADAM_WD = 0.01
ADAM_STEP = 10

SUBLANES = 8
BF16_ROWS = 16
N_CHIPS = 4
N_DEV = 8
CONV_HEAD_DIM = 64
VMEM_BIG = 56 * 1024 * 1024
VMEM_MID = 40 * 1024 * 1024
VMEM_KEEP_OPERANDS_IN_HBM = 62 * 1024 * 1024

TB_MIX = 256
TB_MIX_FWD = 512
TB_FFN = 256
TB_FFN_UP = 512
TB_SCAN = 2048
W_SCAN = 256
SSM_SPLIT = 4
CW_FFN = 256
SCAN_UNROLL = 4
TB_TN = 512


def _cparams(sem=None, vmem=None):
    kw = {}
    if sem is not None:
        kw["dimension_semantics"] = sem
    if vmem is not None:
        kw["vmem_limit_bytes"] = vmem
    return pltpu.CompilerParams(**kw)


def _blk(t, pref):
    return pref if t % pref == 0 else t


def _dot(a, b):
    return jnp.dot(a.astype(BF16), b.astype(BF16), preferred_element_type=F32)


def _dot_nt(a, b):
    return lax.dot_general(a.astype(BF16), b.astype(BF16), (((1,), (1,)), ((), ())),
                           preferred_element_type=F32)


def _dot_tn(a, b):
    return lax.dot_general(a.astype(BF16), b.astype(BF16), (((0,), (0,)), ((), ())),
                           preferred_element_type=F32)


def _sigmoid(x):
    return 0.5 * jnp.tanh(0.5 * x) + 0.5


_GELU_K = math.sqrt(2.0 / math.pi)
_GELU_C = 0.044715


def _gelu(x):
    th = jnp.tanh(_GELU_K * (x + _GELU_C * x * x * x))
    return x * (0.5 * (1.0 + th))


def _gelu_and_grad(x):
    x2 = x * x
    th = jnp.tanh(_GELU_K * (x + _GELU_C * x2 * x))
    half = 0.5 * (1.0 + th)
    return x * half, half + 0.5 * x * (1.0 - th * th) * _GELU_K * (1.0 + 3.0 * _GELU_C * x2)


def _rowmean(x):
    return jnp.mean(x, axis=-1, keepdims=True)


def _colsum(x):
    return jnp.sum(x, axis=0, keepdims=True)


def _split_dot(x, m):
    hi = x.astype(BF16)
    lo = (x - hi.astype(F32)).astype(BF16)
    return (jnp.dot(hi, m, preferred_element_type=F32) + jnp.dot(lo, m, preferred_element_type=F32))


def _split3_dot(x, m):
    hi = x.astype(BF16)
    r1 = x - hi.astype(F32)
    mid = r1.astype(BF16)
    lo = (r1 - mid.astype(F32)).astype(BF16)
    return (jnp.dot(hi, m, preferred_element_type=F32) + jnp.dot(mid, m, preferred_element_type=F32)
            + jnp.dot(lo, m, preferred_element_type=F32))


def _shift_down(x, halo, k):
    r = pltpu.roll(x, k, 0)
    row = lax.broadcasted_iota(jnp.int32, x.shape, 0)
    last = halo.shape[0]
    for j in range(k):
        r = jnp.where(row == j, halo[last - k + j:last - k + j + 1, :], r)
    return r


def _shift_up(x, halo, k):
    n = x.shape[0]
    r = pltpu.roll(x, n - k, 0)
    row = lax.broadcasted_iota(jnp.int32, x.shape, 0)
    for j in range(k):
        r = jnp.where(row == n - k + j, halo[j:j + 1, :], r)
    return r


def _acc_rows(ref, first, rows):
    @pl.when(first)
    def _():
        ref[...] = jnp.zeros(ref.shape, ref.dtype)
    for j, r in enumerate(rows):
        ref[j:j + 1, :] += r


def _rows(tb, c, col=0):
    return pl.BlockSpec((tb, c), lambda i, col=col: (i, col))


def _full(shape):
    nd = len(shape)
    return pl.BlockSpec(shape, lambda i, nd=nd: (0,) * nd)


def _resident(shape):
    nd = len(shape)
    return pl.BlockSpec(shape, lambda i, nd=nd: (0,) * nd, pipeline_mode=pl.Buffered(1))


def _halo_prev(tb, c, col=0, rows=SUBLANES):
    per = tb // rows
    return pl.BlockSpec((rows, c), lambda i, col=col: (jnp.maximum(i * per - 1, 0), col))


def _halo_next(tb, c, t, col=0, rows=SUBLANES):
    per = tb // rows
    last = t // rows - 1
    return pl.BlockSpec((rows, c), lambda i, col=col: (jnp.minimum((i + 1) * per, last), col))


def _mesh_pos():
    return lax.axis_index("x"), lax.axis_index("y"), lax.axis_index("c")


def _allgather8(x_pad, name):
    m_per, n = x_pad.shape

    def body(x_ref, out_ref, send_sems, recv_sems, local_sem):
        x, y, c = _mesh_pos()
        me, sibling = (x, y, c), (x, y, 1 - c)
        chips = [(1 - x, y), (x, 1 - y), (1 - x, 1 - y)]

        def rows(px, py, pc):
            return out_ref.at[pl.ds((4 * px + 2 * py + pc) * m_per, m_per), :]

        def copy(k, block, to, src=None):
            return pltpu.make_async_remote_copy(
                src_ref=rows(*block) if src is None else src, dst_ref=rows(*block),
                send_sem=send_sems.at[k], recv_sem=recv_sems.at[k], device_id=to, device_id_type=MESH)

        mine = pltpu.make_async_copy(x_ref, rows(*me), local_sem)
        mine.start()
        first = [copy(0, me, sibling, src=x_ref)]
        first += [copy(1 + j, me, (*chip, c), src=x_ref) for j, chip in enumerate(chips)]
        for cp in first:
            cp.start()
        passed = [copy(4 + j, (*chip, c), sibling) for j, chip in enumerate(chips)]
        for j, chip in enumerate(chips):
            copy(1 + j, (*chip, c), me).wait_recv()
            passed[j].start()
        copy(0, sibling, me).wait_recv()
        for j, chip in enumerate(chips):
            copy(4 + j, (*chip, 1 - c), me).wait_recv()
        for cp in first + passed:
            cp.wait_send()
        mine.wait()

    return pl.pallas_call(
        body, name=name,
        out_shape=jax.ShapeDtypeStruct((N_DEV * m_per, n), F32),
        in_specs=[pl.BlockSpec(memory_space=pltpu.VMEM)],
        out_specs=pl.BlockSpec(memory_space=pltpu.VMEM),
        scratch_shapes=[pltpu.SemaphoreType.DMA((7,)), pltpu.SemaphoreType.DMA((7,)), pltpu.SemaphoreType.DMA],
    )(x_pad)


_HBM = pl.BlockSpec(memory_space=pltpu.HBM)
_SEM = pl.BlockSpec(memory_space=pltpu.SEMAPHORE)
_EFFECT = pltpu.SideEffectType.DATAFLOW_SIDE_EFFECTING


def _chip_copy(gather, src_ref, land_ref, send, recv, j, arrival):
    x, y, c = _mesh_pos()
    peer = [(1 - x, y), (x, 1 - y), (1 - x, 1 - y)][j]
    peer_chip = 2 * peer[0] + peer[1]
    my_chip = 2 * x + y
    return pltpu.make_async_remote_copy(
        src_ref=land_ref.at[my_chip] if gather else src_ref.at[peer_chip],
        dst_ref=land_ref.at[peer_chip if arrival else my_chip],
        send_sem=send.at[j], recv_sem=recv.at[j], device_id=(*peer, c), device_id_type=MESH)


def _chips_start(name, gather, srcs, lands, after=None):
    n, ns = len(lands), len(srcs)
    extra = [] if after is None else [after]

    def body(*refs):
        src_refs, land_refs = refs[:ns], refs[ns:ns + n]
        outs = refs[ns + n + len(extra):]
        sends, recvs, token = outs[:n], outs[n:2 * n], outs[-1]
        for k in range(n):
            for j in range(3):
                _chip_copy(gather, src_refs[k] if ns else None, land_refs[k], sends[k], recvs[k], j, False).start()
        token[...] = jnp.zeros(token.shape, F32)

    sem = pltpu.SemaphoreType.DMA((3,))
    thru = tuple(pltpu.HBM(a.shape, a.dtype) for a in list(srcs) + list(lands))
    res = pl.pallas_call(
        body, name=name,
        out_shape=(sem,) * (2 * n) + thru + (jax.ShapeDtypeStruct((SUBLANES, 128), F32),),
        in_specs=[_HBM] * (ns + n) + [pl.BlockSpec(memory_space=pl.ANY)] * len(extra),
        out_specs=(_SEM,) * (2 * n) + (_HBM,) * (ns + n) + (pl.BlockSpec(memory_space=pltpu.VMEM),),
        input_output_aliases={k: 2 * n + k for k in range(ns + n)},
        compiler_params=pltpu.CompilerParams(has_side_effects=_EFFECT),
    )(*[pltpu.with_memory_space_constraint(a, pltpu.HBM) for a in list(srcs) + list(lands)], *extra)
    return res[:n], res[n:2 * n], res[2 * n:2 * n + ns], res[2 * n + ns:2 * n + ns + n], res[-1]


def _chips_wait(name, gather, sends, recvs, srcs, lands, after):
    n, ns = len(lands), len(srcs)

    def body(*refs):
        src_refs, land_refs = refs[:ns], refs[ns:ns + n]
        sends_, recvs_ = refs[ns + n:ns + 2 * n], refs[ns + 2 * n:ns + 3 * n]
        for k in range(n):
            for j in range(3):
                cp = _chip_copy(gather, src_refs[k] if ns else None, land_refs[k], sends_[k], recvs_[k], j, True)
                cp.wait_send()
                cp.wait_recv()

    thru = tuple(pltpu.HBM(a.shape, a.dtype) for a in list(srcs) + list(lands))
    res = pl.pallas_call(
        body, name=name, out_shape=thru,
        in_specs=[_HBM] * (ns + n) + [_SEM] * (2 * n) + [pl.BlockSpec(memory_space=pl.ANY)],
        out_specs=(_HBM,) * (ns + n),
        input_output_aliases={k: k for k in range(ns + n)},
        compiler_params=pltpu.CompilerParams(has_side_effects=_EFFECT),
    )(*srcs, *lands, *sends, *recvs, after)
    return res[:ns], res[ns:]


def _sibling_copy(src_ref, land_ref, send, recv):
    x, y, c = _mesh_pos()
    return pltpu.make_async_remote_copy(src_ref=src_ref, dst_ref=land_ref, send_sem=send.at[0], recv_sem=recv.at[0],
                                        device_id=(x, y, 1 - c), device_id_type=MESH)


def _sibling_start(name, arrs, after=None):
    n = len(arrs)
    extra = [] if after is None else [after]
    lands = [lax.empty(a.shape, a.dtype) for a in arrs]

    def body(*refs):
        src_refs, land_refs = refs[:n], refs[n:2 * n]
        outs = refs[2 * n + len(extra):]
        sends, recvs, token = outs[:n], outs[n:2 * n], outs[-1]
        for k in range(n):
            _sibling_copy(src_refs[k], land_refs[k], sends[k], recvs[k]).start()
        token[...] = jnp.zeros(token.shape, F32)

    sem = pltpu.SemaphoreType.DMA((1,))
    thru = tuple(pltpu.HBM(a.shape, a.dtype) for a in list(arrs) + lands)
    res = pl.pallas_call(
        body, name=name,
        out_shape=(sem,) * (2 * n) + thru + (jax.ShapeDtypeStruct((SUBLANES, 128), F32),),
        in_specs=[_HBM] * (2 * n) + [pl.BlockSpec(memory_space=pl.ANY)] * len(extra),
        out_specs=(_SEM,) * (2 * n) + (_HBM,) * (2 * n) + (pl.BlockSpec(memory_space=pltpu.VMEM),),
        input_output_aliases={k: 2 * n + k for k in range(2 * n)},
        compiler_params=pltpu.CompilerParams(has_side_effects=_EFFECT),
    )(*[pltpu.with_memory_space_constraint(a, pltpu.HBM) for a in list(arrs) + lands], *extra)
    return res[:n], res[n:2 * n], res[2 * n:3 * n], res[3 * n:4 * n], res[-1]


def _sibling_wait(name, sends, recvs, srcs, lands, after):
    n = len(srcs)

    def body(*refs):
        src_refs, land_refs = refs[:n], refs[n:2 * n]
        sends_, recvs_ = refs[2 * n:3 * n], refs[3 * n:4 * n]
        for k in range(n):
            cp = _sibling_copy(src_refs[k], land_refs[k], sends_[k], recvs_[k])
            cp.wait_send()
            cp.wait_recv()

    thru = tuple(pltpu.HBM(a.shape, a.dtype) for a in list(srcs) + list(lands))
    res = pl.pallas_call(
        body, name=name, out_shape=thru,
        in_specs=[_HBM] * (2 * n) + [_SEM] * (2 * n) + [pl.BlockSpec(memory_space=pl.ANY)],
        out_specs=(_HBM,) * (2 * n),
        input_output_aliases={k: k for k in range(2 * n)},
        compiler_params=pltpu.CompilerParams(has_side_effects=_EFFECT),
    )(*srcs, *lands, *sends, *recvs, after)
    return res[:n], res[n:]


def _landing(own, chip):
    zone = lax.empty((N_CHIPS,) + own.shape, own.dtype)
    return lax.dynamic_update_slice(zone, own[None], (chip,) + (0,) * own.ndim)


def _mod_shard(c_all, w_ada_sh, b_sh):
    d, n = w_ada_sh.shape
    bn = 512

    def body(c_ref, w_ref, b_ref, o_ref):
        cc = c_ref[...]
        ca = cc * _sigmoid(cc)
        o_ref[...] = _dot(ca, w_ref[...]) + b_ref[...]

    return pl.pallas_call(
        body, name="mod_shard", grid=(n // bn,),
        out_shape=jax.ShapeDtypeStruct((N_DEV, n), F32),
        in_specs=[_full((N_DEV, d)), pl.BlockSpec((d, bn), lambda j: (0, j)), pl.BlockSpec((1, bn), lambda j: (0, j))],
        out_specs=pl.BlockSpec((N_DEV, bn), lambda j: (0, j)),
        compiler_params=_cparams(("parallel",)),
    )(c_all, w_ada_sh, b_sh)


def _ssm_prep(lam_re, lam_im, log_step):
    g, p = lam_re.shape

    def body(lr_ref, li_ref, ls_ref, ar_ref, ai_ref, cr_ref, ci_ref):
        lr = jnp.minimum(lr_ref[...], LAMBDA_RE_MAX)
        li = li_ref[...]
        st = jnp.exp(ls_ref[...])
        mag = jnp.exp(lr * st)
        ar = mag * jnp.cos(li * st)
        ai = mag * jnp.sin(li * st)
        den = lr * lr + li * li
        nr = ar - 1.0
        ar_ref[...] = ar
        ai_ref[...] = ai
        cr_ref[...] = (nr * lr + ai * li) / den
        ci_ref[...] = (ai * lr - nr * li) / den

    sds = jax.ShapeDtypeStruct((g, p), F32)
    return pl.pallas_call(body, name="ssm_prep", out_shape=(sds,) * 4)(lam_re, lam_im, log_step)


def _ssm_blocks(bt_re, bt_im, ct_re, ct_im, coef_rows, tile_b, tile_c):
    gh, p = bt_re.shape
    gp, h = ct_re.shape
    nb = SSM_SPLIT
    cb, rb = gp // nb, gp // nb

    def body(btr, bti, ctr, cti, cf, tb_ref, tc_ref, bre_o, bim_o, cre_o, cim_o):
        j = pl.program_id(0)
        row = lax.broadcasted_iota(jnp.int32, (gh, cb), 0)
        col = lax.broadcasted_iota(jnp.int32, (gh, cb), 1) + j * cb
        mask = (row >> 4) == (col >> 6)
        cr, ci = cf[0:1, :], cf[1:2, :]
        br = _split3_dot(btr[...], tb_ref[...])
        bi = _split3_dot(bti[...], tb_ref[...])
        bre_o[...] = jnp.where(mask, br * cr - bi * ci, 0.0).astype(BF16)
        bim_o[...] = jnp.where(mask, br * ci + bi * cr, 0.0).astype(BF16)
        row2 = lax.broadcasted_iota(jnp.int32, (rb, gh), 0) + j * rb
        col2 = lax.broadcasted_iota(jnp.int32, (rb, gh), 1)
        mask2 = (row2 >> 6) == (col2 >> 4)
        cre_o[...] = jnp.where(mask2, _split3_dot(ctr[...], tc_ref[...]), 0.0).astype(BF16)
        cim_o[...] = jnp.where(mask2, _split3_dot(cti[...], tc_ref[...]), 0.0).astype(BF16)

    bspec = pl.BlockSpec((gh, cb), lambda j: (0, j))
    cspec = pl.BlockSpec((rb, gh), lambda j: (j, 0))
    cin = pl.BlockSpec((rb, h), lambda j: (j, 0))
    return pl.pallas_call(
        body, name="ssm_blocks", grid=(nb,),
        out_shape=(jax.ShapeDtypeStruct((gh, gp), BF16),) * 2 + (jax.ShapeDtypeStruct((gp, gh), BF16),) * 2,
        in_specs=[_full((gh, p)), _full((gh, p)), cin, cin, pl.BlockSpec((SUBLANES, cb), lambda j: (0, j)),
                  _full(tile_b.shape), _full(tile_c.shape)],
        out_specs=(bspec, bspec, cspec, cspec),
        compiler_params=_cparams(("parallel",)),
    )(bt_re, bt_im, ct_re, ct_im, coef_rows, tile_b, tile_c)


def _scan_consts(a_ref, reverse):
    w = a_ref.shape[1]
    ar1 = a_ref[0:1, :]
    ai1 = a_ref[1:2, :]
    if reverse:
        ai1 = -ai1
    pr, pi = [ar1], [ai1]
    for _ in range(1, SUBLANES):
        nr = pr[-1] * ar1 - pi[-1] * ai1
        ni = pr[-1] * ai1 + pi[-1] * ar1
        pr.append(nr)
        pi.append(ni)
    row = lax.broadcasted_iota(jnp.int32, (SUBLANES, w), 0)
    dist = (SUBLANES - 1 - row) if reverse else row

    def pick(vals):
        out = jnp.broadcast_to(vals[SUBLANES - 1], (SUBLANES, w))
        for r in range(SUBLANES - 1):
            out = jnp.where(dist == r, vals[r], out)
        return out

    p_r, p_i = pick(pr), pick(pi)
    steps = []
    for k in (1, 2, 4):
        steps.append((k, jnp.where(dist >= k, pr[k - 1], 0.0), jnp.where(dist >= k, pi[k - 1], 0.0)))
    a8 = (jnp.broadcast_to(pr[SUBLANES - 1], (SUBLANES, w)), jnp.broadcast_to(pi[SUBLANES - 1], (SUBLANES, w)))
    return row, p_r, p_i, steps, a8


def _scan_tile(xr, xi, cr, ci, consts, reverse):
    row, p_r, p_i, steps, (a8r, a8i) = consts
    for k, s_r, s_i in steps:
        sh = (SUBLANES - k) if reverse else k
        qr = pltpu.roll(xr, sh, 0)
        qi = pltpu.roll(xi, sh, 0)
        xr, xi = xr + s_r * qr - s_i * qi, xi + s_r * qi + s_i * qr
    outr = xr + p_r * cr - p_i * ci
    outi = xi + p_r * ci + p_i * cr
    e = 0 if reverse else SUBLANES - 1
    er = jnp.broadcast_to(xr[e:e + 1, :], xr.shape)
    ei = jnp.broadcast_to(xi[e:e + 1, :], xi.shape)
    return outr, outi, er + a8r * cr - a8i * ci, ei + a8r * ci + a8i * cr


def _scan_fwd(a_rows, bu_re, bu_im):
    t, n = bu_re.shape
    tb, w = _blk(t, TB_SCAN), W_SCAN
    ntile = tb // SUBLANES

    def body(a_ref, br_ref, bi_ref, sr_ref, si_ref, car, cai):
        @pl.when(pl.program_id(1) == 0)
        def _():
            car[...] = jnp.zeros(car.shape, F32)
            cai[...] = jnp.zeros(cai.shape, F32)
        consts = _scan_consts(a_ref, False)

        def pair(i, carry):
            o = pl.multiple_of(i * BF16_ROWS, BF16_ROWS)
            b_r = br_ref[pl.ds(o, BF16_ROWS), :].astype(F32)
            b_i = bi_ref[pl.ds(o, BF16_ROWS), :].astype(F32)
            outs = []
            for h in range(2):
                rows = slice(h * SUBLANES, (h + 1) * SUBLANES)
                outr, outi, ncr, nci = _scan_tile(b_r[rows, :], b_i[rows, :], carry[0], carry[1], consts, False)
                outs.append((outr, outi))
                carry = (ncr, nci)
            sr_ref[pl.ds(o, BF16_ROWS), :] = jnp.concatenate([outs[0][0], outs[1][0]], axis=0).astype(BF16)
            si_ref[pl.ds(o, BF16_ROWS), :] = jnp.concatenate([outs[0][1], outs[1][1]], axis=0).astype(BF16)
            return carry

        def pairs(i, carry):
            for s in range(SCAN_UNROLL // 2):
                carry = pair(i * (SCAN_UNROLL // 2) + s, carry)
            return carry

        cr, ci = lax.fori_loop(0, ntile // SCAN_UNROLL, pairs, (car[...], cai[...]))
        car[...] = cr
        cai[...] = ci

    spec = pl.BlockSpec((tb, w), lambda s, k: (k, s))
    sds = jax.ShapeDtypeStruct((t, n), BF16)
    return pl.pallas_call(
        body, name="scan_fwd", grid=(n // w, t // tb), out_shape=(sds, sds),
        in_specs=[pl.BlockSpec((SUBLANES, w), lambda s, k: (0, s)), spec, spec], out_specs=(spec, spec),
        scratch_shapes=[pltpu.VMEM((SUBLANES, w), F32), pltpu.VMEM((SUBLANES, w), F32)],
        compiler_params=_cparams(("parallel", "arbitrary"), VMEM_MID),
    )(a_rows, bu_re, bu_im)


def _scan_bwd(a_rows, g_re, g_im, s_re, s_im):
    t, n = g_re.shape
    tb, w = _blk(t, TB_SCAN), W_SCAN
    ntile = tb // SUBLANES
    npair = tb // BF16_ROWS
    nt = t // tb

    def body(a_ref, gr_ref, gi_ref, sr_ref, si_ref, or_ref, oi_ref, gar_ref, gai_ref, car, cai):
        @pl.when(pl.program_id(1) == 0)
        def _():
            car[...] = jnp.zeros(car.shape, F32)
            cai[...] = jnp.zeros(cai.shape, F32)
            gar_ref[...] = jnp.zeros(gar_ref.shape, F32)
            gai_ref[...] = jnp.zeros(gai_ref.shape, F32)
        consts = _scan_consts(a_ref, True)
        row = consts[0]

        def pair(i, carry):
            cr, ci, accr, acci = carry
            o = pl.multiple_of((npair - 1 - i) * BF16_ROWS, BF16_ROWS)
            s_r = sr_ref[pl.ds(o, BF16_ROWS), :].astype(F32)
            s_i = si_ref[pl.ds(o, BF16_ROWS), :].astype(F32)
            g_r = gr_ref[pl.ds(o, BF16_ROWS), :].astype(F32)
            g_i = gi_ref[pl.ds(o, BF16_ROWS), :].astype(F32)
            outs = [None, None]
            for h in (1, 0):
                rows = slice(h * SUBLANES, (h + 1) * SUBLANES)
                outr, outi, ncr, nci = _scan_tile(g_r[rows, :], g_i[rows, :], cr, ci, consts, True)
                outs[h] = (outr, outi)
                gnr = jnp.where(row == SUBLANES - 1, cr, pltpu.roll(outr, SUBLANES - 1, 0))
                gni = jnp.where(row == SUBLANES - 1, ci, pltpu.roll(outi, SUBLANES - 1, 0))
                sr = s_r[h * SUBLANES:(h + 1) * SUBLANES, :]
                si = s_i[h * SUBLANES:(h + 1) * SUBLANES, :]
                accr, acci = accr + sr * gnr + si * gni, acci + sr * gni - si * gnr
                cr, ci = ncr, nci
            or_ref[pl.ds(o, BF16_ROWS), :] = jnp.concatenate([outs[0][0], outs[1][0]], axis=0).astype(BF16)
            oi_ref[pl.ds(o, BF16_ROWS), :] = jnp.concatenate([outs[0][1], outs[1][1]], axis=0).astype(BF16)
            return cr, ci, accr, acci

        def pairs(i, carry):
            for s in range(SCAN_UNROLL // 2):
                carry = pair(i * (SCAN_UNROLL // 2) + s, carry)
            return carry

        cr, ci, accr, acci = lax.fori_loop(0, ntile // SCAN_UNROLL, pairs,
                                           (car[...], cai[...], gar_ref[...], gai_ref[...]))
        car[...] = cr
        cai[...] = ci
        gar_ref[...] = accr
        gai_ref[...] = acci

    spec = pl.BlockSpec((tb, w), lambda s, k: (nt - 1 - k, s))
    aspec = pl.BlockSpec((SUBLANES, w), lambda s, k: (0, s))
    sds = jax.ShapeDtypeStruct((t, n), BF16)
    asds = jax.ShapeDtypeStruct((SUBLANES, n), F32)
    return pl.pallas_call(
        body, name="scan_bwd", grid=(n // w, nt), out_shape=(sds, sds, asds, asds),
        in_specs=[aspec, spec, spec, spec, spec], out_specs=(spec, spec, aspec, aspec),
        scratch_shapes=[pltpu.VMEM((SUBLANES, w), F32), pltpu.VMEM((SUBLANES, w), F32)],
        compiler_params=_cparams(("parallel", "arbitrary"), VMEM_MID),
    )(a_rows, g_re, g_im, s_re, s_im)


def _mix_in(x, vec, w_in_st, b_re, b_im):
    t, d = x.shape
    ns, _, nc = w_in_st.shape
    dssm, nstate = b_re.shape
    du, ds = dssm // SSM_SPLIT, nstate // SSM_SPLIT
    tb = _blk(t, TB_MIX_FWD)

    def body(x_ref, vec_ref, w_ref, bre_ref, bim_ref, proj_ref, bur_ref, bui_ref, h1_ref):
        xv = x_ref[...]
        r = lax.rsqrt(_rowmean(xv * xv) + EPS)
        h = xv * r * vec_ref[0:1, :] * vec_ref[1:2, :] + vec_ref[2:3, :]
        hb = h.astype(BF16)
        h1_ref[...] = hb
        u = None
        for j in range(ns):
            pj = jnp.dot(hb, w_ref[j], preferred_element_type=F32)
            proj_ref[:, j * nc:(j + 1) * nc] = pj.astype(BF16)
            if j == 0:
                u = pj
        ub = u.astype(BF16)
        for q in range(SSM_SPLIT):
            rq, cq = slice(q * du, (q + 1) * du), slice(q * ds, (q + 1) * ds)
            bur_ref[:, cq] = jnp.dot(ub[:, rq], bre_ref[rq, cq], preferred_element_type=F32).astype(BF16)
            bui_ref[:, cq] = jnp.dot(ub[:, rq], bim_ref[rq, cq], preferred_element_type=F32).astype(BF16)

    return pl.pallas_call(
        body, name="mix_in", grid=(t // tb,),
        out_shape=(jax.ShapeDtypeStruct((t, ns * nc), BF16), jax.ShapeDtypeStruct((t, nstate), BF16),
                   jax.ShapeDtypeStruct((t, nstate), BF16), jax.ShapeDtypeStruct((t, d), BF16)),
        in_specs=[_rows(tb, d), _full((SUBLANES, d)), _resident(w_in_st.shape), _resident(b_re.shape),
                  _resident(b_im.shape)],
        out_specs=(_rows(tb, ns * nc), _rows(tb, nstate), _rows(tb, nstate), _rows(tb, d)),
        compiler_params=_cparams(("parallel",), VMEM_BIG),
    )(x, vec, w_in_st, b_re, b_im)


def _head_ms(y, h_ref):
    return _split_dot(y * y, h_ref[...])


def _conv3(x, halo, w_ref):
    return w_ref[0:1, :] * _shift_down(x, halo, 2) + w_ref[1:2, :] * _shift_down(x, halo, 1) + w_ref[2:3, :] * x


def _mix_out(x, proj, s_re, s_im, c_re, c_im, v512, convw, glu_w, h16, h64, w_out, vd):
    t, d = x.shape
    dh = c_re.shape[1]
    nstate = s_re.shape[1]
    du, ds = dh // SSM_SPLIT, nstate // SSM_SPLIT
    tb = _blk(t, TB_MIX_FWD)

    def body(x_ref, u_ref, bg_ref, cg_ref, v_ref, cgh_ref, vh_ref, sr_ref, si_ref, cre_ref, cim_ref, p_ref,
             cw_ref, gw_ref, h16_ref, h64_ref, wo_ref, vd_ref, y1_ref, o_ref, x2_ref):
        i = pl.program_id(0)
        u = u_ref[...].astype(F32)
        ys = []
        for q in range(SSM_SPLIT):
            rq, cq = slice(q * ds, (q + 1) * ds), slice(q * du, (q + 1) * du)
            ys.append(_dot(sr_ref[:, rq], cre_ref[rq, cq]) - _dot(si_ref[:, rq], cim_ref[rq, cq]))
        ys = jnp.concatenate(ys, axis=1)
        y1 = ys + p_ref[0:1, :] * u
        y1_ref[...] = y1
        z = _gelu(y1)
        q = _dot(z, gw_ref[...]) + p_ref[1:2, :]
        ya = z * _sigmoid(q)
        na = ya * lax.rsqrt(_head_ms(ya, h16_ref) + EPS) * p_ref[2:3, :]
        cv = cg_ref[...].astype(F32) * v_ref[...].astype(F32)
        cvh = jnp.where(i > 0, cgh_ref[...].astype(F32) * vh_ref[...].astype(F32), 0.0)
        yb = bg_ref[...].astype(F32) * _conv3(cv, cvh, cw_ref)
        nb = yb * lax.rsqrt(_head_ms(yb, h64_ref) + EPS) * p_ref[3:4, :]
        o = _dot(na, wo_ref[0:dh, :]) + _dot(nb, wo_ref[dh:2 * dh, :])
        o_ref[...] = o
        on = o * lax.rsqrt(_rowmean(o * o) + EPS) * vd_ref[0:1, :]
        x2_ref[...] = x_ref[...] + vd_ref[1:2, :] * on

    return pl.pallas_call(
        body, name="mix_out", grid=(t // tb,),
        out_shape=(jax.ShapeDtypeStruct((t, dh), F32), jax.ShapeDtypeStruct((t, d), F32),
                   jax.ShapeDtypeStruct((t, d), F32)),
        in_specs=[_rows(tb, d), _rows(tb, dh, 0), _rows(tb, dh, 1), _rows(tb, dh, 2), _rows(tb, dh, 3),
                  _halo_prev(tb, dh, 2, BF16_ROWS), _halo_prev(tb, dh, 3, BF16_ROWS), _rows(tb, nstate), _rows(tb, nstate),
                  _full(c_re.shape), _full(c_im.shape), _full(v512.shape), _full(convw.shape), _full(glu_w.shape),
                  _full(h16.shape), _full(h64.shape), _full(w_out.shape), _full(vd.shape)],
        out_specs=(_rows(tb, dh), _rows(tb, d), _rows(tb, d)),
        compiler_params=_cparams(("parallel",), VMEM_BIG),
    )(x, proj, proj, proj, proj, proj, proj, s_re, s_im, c_re, c_im, v512, convw, glu_w, h16, h64, w_out, vd)


def _ffn_up(x2, vec, w_up_st):
    t, d = x2.shape
    ns, _, nc = w_up_st.shape
    tb = _blk(t, TB_FFN_UP)

    def body(x_ref, vec_ref, w_ref, up_ref, h2_ref):
        xv = x_ref[...]
        r = lax.rsqrt(_rowmean(xv * xv) + EPS)
        h = xv * r * vec_ref[0:1, :] * vec_ref[1:2, :] + vec_ref[2:3, :]
        hb = h.astype(BF16)
        h2_ref[...] = hb
        for j in range(ns):
            up_ref[:, j * nc:(j + 1) * nc] = jnp.dot(hb, w_ref[j], preferred_element_type=F32)

    return pl.pallas_call(
        body, name="ffn_up", grid=(t // tb,),
        out_shape=(jax.ShapeDtypeStruct((t, ns * nc), F32), jax.ShapeDtypeStruct((t, d), BF16)),
        in_specs=[_rows(tb, d), _full((SUBLANES, d)), _resident(w_up_st.shape)],
        out_specs=(_rows(tb, ns * nc), _rows(tb, d)),
        compiler_params=_cparams(("parallel",), VMEM_BIG),
    )(x2, vec, w_up_st)


def _ffn_down(up, fw, w_down, w_down_t, x2, tgt, vd):
    t, nh = up.shape
    dff, d = w_down.shape
    tb = _blk(t, TB_FFN)
    inv_d = 1.0 / d

    def body(up_ref, uph_ref, fw_ref, wd_ref, wdt_ref, x2_ref, tgt_ref, vd_ref,
             act_ref, ddn_ref, dout_ref, dhid_ref, vec_ref, loss_ref, a_s, vv_s, sg_s):
        i = pl.program_id(0)

        def conv_cols(sl):
            x = up_ref[:, sl]
            halo = jnp.where(i > 0, uph_ref[:, sl], 0.0)
            return (fw_ref[0:1, sl] * _shift_down(x, halo, 2) + fw_ref[1:2, sl] * _shift_down(x, halo, 1)
                    + fw_ref[2:3, sl] * x)

        dn = None
        for o in range(0, dff, CW_FFN):
            sl = slice(o, o + CW_FFN)
            a = conv_cols(sl)
            vv = conv_cols(slice(dff + o, dff + o + CW_FFN))
            sg = _sigmoid(a)
            si = a * sg
            a_s[:, sl] = si
            vv_s[:, sl] = vv
            sg_s[:, sl] = sg
            actb = (si * vv).astype(BF16)
            act_ref[:, sl] = actb
            pj = lax.dot_general(actb, wdt_ref[:, sl], (((1,), (1,)), ((), ())), preferred_element_type=F32)
            dn = pj if dn is None else dn + pj
        r3 = lax.rsqrt(_rowmean(dn * dn) + EPS)
        xn = dn * r3
        g = vd_ref[0:1, :]
        gt2 = vd_ref[1:2, :]
        dnn = xn * g
        diff = x2_ref[...] + gt2 * dnn - tgt_ref[...]
        part = 0.5 * inv_d * jnp.sum(diff * diff)

        @pl.when(i == 0)
        def _():
            loss_ref[...] = jnp.zeros(loss_ref.shape, F32)
        loss_ref[...] += part
        dout = diff * inv_d
        dout_ref[...] = dout
        ddnn = dout * gt2
        _acc_rows(vec_ref, i == 0, [_colsum(dout * dnn), _colsum(ddnn * xn)])
        dxn = ddnn * g
        ddn = r3 * (dxn - xn * _rowmean(dxn * xn))
        ddnb = ddn.astype(BF16)
        ddn_ref[...] = ddnb
        for o in range(0, dff, CW_FFN):
            sl = slice(o, o + CW_FFN)
            dact = lax.dot_general(ddnb, wd_ref[sl, :], (((1,), (1,)), ((), ())), preferred_element_type=F32)
            si, vv, sg = a_s[:, sl], vv_s[:, sl], sg_s[:, sl]
            dhid_ref[:, sl] = (dact * vv * (sg + si * (1.0 - sg))).astype(BF16)
            dhid_ref[:, dff + o:dff + o + CW_FFN] = (dact * si).astype(BF16)

    return pl.pallas_call(
        body, name="ffn_down", grid=(t // tb,),
        scratch_shapes=[pltpu.VMEM((tb, dff), F32)] * 3,
        out_shape=(jax.ShapeDtypeStruct((t, dff), BF16), jax.ShapeDtypeStruct((t, d), BF16),
                   jax.ShapeDtypeStruct((t, d), F32), jax.ShapeDtypeStruct((t, nh), BF16),
                   jax.ShapeDtypeStruct((SUBLANES, d), F32), jax.ShapeDtypeStruct((SUBLANES, 128), F32)),
        in_specs=[_rows(tb, nh), _halo_prev(tb, nh), _full(fw.shape), _resident(w_down.shape),
                  _resident(w_down_t.shape), _rows(tb, d),
                  _rows(tb, d), _full(vd.shape)],
        out_specs=(_rows(tb, dff), _rows(tb, d), _rows(tb, d), _rows(tb, nh), _full((SUBLANES, d)),
                   _full((SUBLANES, 128))),
        compiler_params=_cparams(("arbitrary",), VMEM_BIG),
    )(up, up, fw, w_down, w_down_t, x2, tgt, vd)


def _ffn_up_bwd(dhid, up, fw, x2, dout, vec, w_up_st):
    t, nh = dhid.shape
    d = x2.shape[1]
    ns, _, nc = w_up_st.shape
    tb = _blk(t, TB_FFN)
    nblk = t // tb
    cw = 128

    def body(dh_ref, dhn_ref, up_ref, fw_ref, x2_ref, dout_ref, vec_ref, w_ref,
             dx2_ref, dup_ref, vp_ref, df_ref):
        i = pl.program_id(0)

        @pl.when(i == 0)
        def _():
            df_ref[...] = jnp.zeros(df_ref.shape, F32)
        dh2 = None
        for j in range(ns):
            for o in range(j * nc, (j + 1) * nc, cw):
                sl = slice(o, o + cw)
                dh = dh_ref[:, sl].astype(F32)
                dhn = jnp.where(i < nblk - 1, dhn_ref[:, sl].astype(F32), 0.0)
                dh1 = _shift_up(dh, dhn, 1)
                dh2s = _shift_up(dh, dhn, 2)
                dup_ref[:, sl] = (fw_ref[2:3, sl] * dh + fw_ref[1:2, sl] * dh1 + fw_ref[0:1, sl] * dh2s).astype(BF16)
                up_v = up_ref[:, sl]
                df_ref[0:1, sl] += _colsum(dh2s * up_v)
                df_ref[1:2, sl] += _colsum(dh1 * up_v)
                df_ref[2:3, sl] += _colsum(dh * up_v)
            pj = lax.dot_general(dup_ref[:, j * nc:(j + 1) * nc], w_ref[j], (((1,), (1,)), ((), ())),
                                 preferred_element_type=F32)
            dh2 = pj if dh2 is None else dh2 + pj
        xv = x2_ref[...]
        r = lax.rsqrt(_rowmean(xv * xv) + EPS)
        xn = xv * r
        g = vec_ref[0:1, :]
        hg = xn * g
        dhg = dh2 * vec_ref[1:2, :]
        _acc_rows(vp_ref, i == 0, [_colsum(dh2), _colsum(dh2 * hg), _colsum(dhg * xn)])
        dxn = dhg * g
        dx2_ref[...] = dout_ref[...] + r * (dxn - xn * _rowmean(dxn * xn))

    return pl.pallas_call(
        body, name="ffn_up_bwd", grid=(nblk,),
        out_shape=(jax.ShapeDtypeStruct((t, d), F32), jax.ShapeDtypeStruct((t, nh), BF16),
                   jax.ShapeDtypeStruct((SUBLANES, d), F32), jax.ShapeDtypeStruct((SUBLANES, nh), F32)),
        in_specs=[_rows(tb, nh), _halo_next(tb, nh, t, rows=BF16_ROWS), _rows(tb, nh), _full(fw.shape),
                  _rows(tb, d), _rows(tb, d), _full(vec.shape), _resident(w_up_st.shape)],
        out_specs=(_rows(tb, d), _rows(tb, nh), _full((SUBLANES, d)), _full((SUBLANES, nh))),
        compiler_params=_cparams(("arbitrary",), VMEM_BIG),
    )(dhid, dhid, up, fw, x2, dout, vec, w_up_st)


def _mix_out_bwd(dx2, o, y1, proj, s_re, s_im, c_re, c_im, v512, convw, glu_w, h16, h64, w_out, vd):
    t, d = dx2.shape
    dh = y1.shape[1]
    nstate = c_re.shape[0]
    du, ds = dh // SSM_SPLIT, nstate // SSM_SPLIT
    tb = _blk(t, TB_MIX)

    def body(dx2_ref, o_ref, y1_ref, u_ref, bg_ref, cg_ref, v_ref, cgh_ref, vh_ref, cre_ref, cim_ref, p_ref,
             cw_ref, gw_ref, h16_ref, h64_ref, wo_ref, vd_ref, sr_ref, si_ref,
             do_ref, ycat_ref, z_ref, dq_ref, dy1_ref, gr_ref, gi_ref, dcc_ref, dbg_ref, vpd_ref, vp5_ref,
             dcr_ref, dci_ref):
        i = pl.program_id(0)
        first = i == 0

        @pl.when(first)
        def _():
            dcr_ref[...] = jnp.zeros(dcr_ref.shape, F32)
            dci_ref[...] = jnp.zeros(dci_ref.shape, F32)
        ov = o_ref[...]
        ro = lax.rsqrt(_rowmean(ov * ov) + EPS)
        on_ = ov * ro
        g = vd_ref[0:1, :]
        dx2v = dx2_ref[...]
        don = dx2v * vd_ref[1:2, :]
        _acc_rows(vpd_ref, first, [_colsum(dx2v * on_ * g), _colsum(don * on_)])
        dxn = don * g
        dob = (ro * (dxn - on_ * _rowmean(dxn * on_))).astype(BF16)
        do_ref[...] = dob
        dyc_a =lax.dot_general(dob, wo_ref[0:dh, :], (((1,), (1,)), ((), ())), preferred_element_type=F32)
        dyc_b = lax.dot_general(dob, wo_ref[dh:2 * dh, :], (((1,), (1,)), ((), ())), preferred_element_type=F32)
        y1v = y1_ref[...]
        u = u_ref[...].astype(F32)
        z, dz_dy1 = _gelu_and_grad(y1v)
        zb = z.astype(BF16)
        sg = _sigmoid(jnp.dot(zb, gw_ref[...], preferred_element_type=F32) + p_ref[1:2, :])
        ya = z * sg
        ra = lax.rsqrt(_head_ms(ya, h16_ref) + EPS)
        yan = ya * ra
        ga = p_ref[2:3, :]
        ycat_ref[:, 0:dh] = (yan * ga).astype(BF16)
        dyn = dyc_a * ga
        dya = ra * (dyn - yan * _split_dot(dyn * yan, h16_ref[...]))
        dq = dya * z * sg * (1.0 - sg)
        dqb = dq.astype(BF16)
        z_ref[...] = zb
        dq_ref[...] = dqb
        dz = dya * sg + lax.dot_general(dqb, gw_ref[...], (((1,), (1,)), ((), ())), preferred_element_type=F32)
        dy1 = dz * dz_dy1
        dy1_ref[...] = dy1
        dy1b = dy1.astype(BF16)
        for q in range(SSM_SPLIT):
            rq, cq = slice(q * ds, (q + 1) * ds), slice(q * du, (q + 1) * du)
            gr_ref[:, rq] = lax.dot_general(dy1b[:, cq], cre_ref[rq, cq], (((1,), (1,)), ((), ())),
                                            preferred_element_type=F32).astype(BF16)
            gi_ref[:, rq] = (-lax.dot_general(dy1b[:, cq], cim_ref[rq, cq], (((1,), (1,)), ((), ())),
                                              preferred_element_type=F32)).astype(BF16)
            dcr_ref[rq, :] += _dot_tn(sr_ref[:, rq], dy1b[:, cq])
            dci_ref[rq, :] += _dot_tn(si_ref[:, rq], dy1b[:, cq])
        bg = bg_ref[...].astype(F32)
        cv = cg_ref[...].astype(F32) * v_ref[...].astype(F32)
        cvh = jnp.where(i > 0, cgh_ref[...].astype(F32) * vh_ref[...].astype(F32), 0.0)
        cv1 = _shift_down(cv, cvh, 1)
        cv2 = _shift_down(cv, cvh, 2)
        cc = cw_ref[0:1, :] * cv2 + cw_ref[1:2, :] * cv1 + cw_ref[2:3, :] * cv
        yb = bg * cc
        rb = lax.rsqrt(_head_ms(yb, h64_ref) + EPS)
        ybn = yb * rb
        gb = p_ref[3:4, :]
        ycat_ref[:, dh:2 * dh] = (ybn * gb).astype(BF16)
        dynb = dyc_b * gb
        dyb = rb * (dynb - ybn * _split_dot(dynb * ybn, h64_ref[...]))
        dcc = dyb * bg
        dbg_ref[...] = dyb * cc
        dcc_ref[...] = dcc
        _acc_rows(vp5_ref, first, [_colsum(dyc_a * yan), _colsum(dyc_b * ybn), _colsum(dq), _colsum(dy1 * u),
                                   _colsum(dcc * cv2), _colsum(dcc * cv1), _colsum(dcc * cv)])

    return pl.pallas_call(
        body, name="mix_out_bwd", grid=(t // tb,),
        out_shape=(jax.ShapeDtypeStruct((t, d), BF16), jax.ShapeDtypeStruct((t, 2 * dh), BF16),
                   jax.ShapeDtypeStruct((t, dh), BF16), jax.ShapeDtypeStruct((t, dh), BF16),
                   jax.ShapeDtypeStruct((t, dh), F32), jax.ShapeDtypeStruct((t, nstate), BF16),
                   jax.ShapeDtypeStruct((t, nstate), BF16), jax.ShapeDtypeStruct((t, dh), F32),
                   jax.ShapeDtypeStruct((t, dh), F32), jax.ShapeDtypeStruct((SUBLANES, d), F32),
                   jax.ShapeDtypeStruct((SUBLANES, dh), F32), jax.ShapeDtypeStruct((nstate, du), F32),
                   jax.ShapeDtypeStruct((nstate, du), F32)),
        in_specs=[_rows(tb, d), _rows(tb, d), _rows(tb, dh), _rows(tb, dh, 0), _rows(tb, dh, 1), _rows(tb, dh, 2),
                  _rows(tb, dh, 3), _halo_prev(tb, dh, 2, BF16_ROWS), _halo_prev(tb, dh, 3, BF16_ROWS), _resident(c_re.shape),
                  _resident(c_im.shape), _full(v512.shape), _full(convw.shape), _resident(glu_w.shape),
                  _resident(h16.shape), _resident(h64.shape), _resident(w_out.shape), _full(vd.shape),
                  _rows(tb, nstate), _rows(tb, nstate)],
        out_specs=(_rows(tb, d), _rows(tb, 2 * dh), _rows(tb, dh), _rows(tb, dh), _rows(tb, dh), _rows(tb, nstate),
                   _rows(tb, nstate), _rows(tb, dh), _rows(tb, dh), _full((SUBLANES, d)), _full((SUBLANES, dh)),
                   _full((nstate, du)), _full((nstate, du))),
        compiler_params=_cparams(("arbitrary",), VMEM_BIG),
    )(dx2, o, y1, proj, proj, proj, proj, proj, proj, c_re, c_im, v512, convw, glu_w, h16, h64, w_out, vd,
      s_re, s_im)


def _mix_in_bwd(gt_re, gt_im, b_re, b_im, dy1, dcc, dbg, proj, x, dx2, vec, v512, convw, w_in_st):
    t, d = x.shape
    dh = dy1.shape[1]
    nstate = gt_re.shape[1]
    du_w, ds = dh // SSM_SPLIT, nstate // SSM_SPLIT
    ns, _, nc = w_in_st.shape
    tb = _blk(t, TB_MIX_FWD)
    nblk = t // tb

    def body(gr_ref, gi_ref, bre_ref, bim_ref, dy1_ref, dcc_ref, dccn_ref, dbg_ref, u_ref, cg_ref, v_ref, x_ref,
             dx2_ref, vec_ref, p_ref, cw_ref, w_ref, gx_ref, dproj_ref, vp_ref, dbr_ref, dbi_ref):
        i = pl.program_id(0)

        @pl.when(i == 0)
        def _():
            dbr_ref[...] = jnp.zeros(dbr_ref.shape, F32)
            dbi_ref[...] = jnp.zeros(dbi_ref.shape, F32)
        ub = u_ref[...].astype(BF16)
        du = []
        for q in range(SSM_SPLIT):
            rq, cq = slice(q * du_w, (q + 1) * du_w), slice(q * ds, (q + 1) * ds)
            du.append(lax.dot_general(gr_ref[:, cq].astype(BF16), bre_ref[rq, cq], (((1,), (1,)), ((), ())),
                                      preferred_element_type=F32)
                      + lax.dot_general(gi_ref[:, cq].astype(BF16), bim_ref[rq, cq], (((1,), (1,)), ((), ())),
                                        preferred_element_type=F32))
            dbr_ref[rq, :] += _dot_tn(ub[:, rq], gr_ref[:, cq])
            dbi_ref[rq, :] += _dot_tn(ub[:, rq], gi_ref[:, cq])
        du = dy1_ref[...] * p_ref[0:1, :] + jnp.concatenate(du, axis=1)
        dcc = dcc_ref[...]
        dccn = jnp.where(i < nblk - 1, dccn_ref[...], 0.0)
        dcv = (cw_ref[2:3, :] * dcc + cw_ref[1:2, :] * _shift_up(dcc, dccn, 1)
               + cw_ref[0:1, :] * _shift_up(dcc, dccn, 2))
        parts = [du, dbg_ref[...], dcv * v_ref[...].astype(F32), dcv * cg_ref[...].astype(F32)]
        xv = x_ref[...]
        r = lax.rsqrt(_rowmean(xv * xv) + EPS)
        xn = xv * r
        g = vec_ref[0:1, :]
        hg = xn * g
        dh1 = None
        for j in range(ns):
            pb = parts[j].astype(BF16)
            dproj_ref[:, j * nc:(j + 1) * nc] = pb
            pj =lax.dot_general(pb, w_ref[j], (((1,), (1,)), ((), ())), preferred_element_type=F32)
            dh1 = pj if dh1 is None else dh1 + pj
        dhg = dh1 * vec_ref[1:2, :]
        _acc_rows(vp_ref, i == 0, [_colsum(dh1), _colsum(dh1 * hg), _colsum(dhg * xn)])
        dxn = dhg * g
        gx_ref[...] = dx2_ref[...] + r * (dxn - xn * _rowmean(dxn * xn))

    assert nc == dh and ns == 4
    return pl.pallas_call(
        body, name="mix_in_bwd", grid=(nblk,),
        out_shape=(jax.ShapeDtypeStruct((t, d), F32), jax.ShapeDtypeStruct((t, ns * nc), BF16),
                   jax.ShapeDtypeStruct((SUBLANES, d), F32), jax.ShapeDtypeStruct((dh, ds), F32),
                   jax.ShapeDtypeStruct((dh, ds), F32)),
        in_specs=[_rows(tb, nstate), _rows(tb, nstate), _resident(b_re.shape), _resident(b_im.shape), _rows(tb, dh),
                  _rows(tb, dh), _halo_next(tb, dh, t), _rows(tb, dh), _rows(tb, dh, 0), _rows(tb, dh, 2),
                  _rows(tb, dh, 3), _rows(tb, d), _rows(tb, d), _full(vec.shape), _full(v512.shape),
                  _full(convw.shape), _resident(w_in_st.shape)],
        out_specs=(_rows(tb, d), _rows(tb, ns * nc), _full((SUBLANES, d)), _full((dh, ds)), _full((dh, ds))),
        compiler_params=_cparams(("arbitrary",), VMEM_BIG),
    )(gt_re, gt_im, b_re, b_im, dy1, dcc, dcc, dbg, proj, proj, proj, x, dx2, vec, v512, convw, w_in_st)


def _matmul_tn(a, b, m, bn, out_dtype, name, diag=False, bt=TB_TN, after=None):
    t = a.shape[0]
    n = b.shape[1]
    bt = _blk(t, bt)
    nk = t // bt
    extra = [] if after is None else [after]
    a_map = (lambda j, k: (k, j)) if diag else (lambda j, k: (k, 0))

    def body(a_ref, b_ref, *rest):
        o_ref, acc_ref = rest[-2:]
        k = pl.program_id(1)

        @pl.when(k == 0)
        def _():
            acc_ref[...] = jnp.zeros(acc_ref.shape, F32)
        acc_ref[...] += _dot_tn(a_ref[...], b_ref[...])

        @pl.when(k == nk - 1)
        def _():
            o_ref[...] = acc_ref[...].astype(out_dtype)

    return pl.pallas_call(
        body, name=name, grid=(n // bn, nk),
        out_shape=jax.ShapeDtypeStruct((n // bn, m, bn), out_dtype),
        in_specs=[pl.BlockSpec((bt, m), a_map), pl.BlockSpec((bt, bn), lambda j, k: (k, j))]
        + [pl.BlockSpec(memory_space=pl.ANY)] * len(extra),
        out_specs=pl.BlockSpec((None, m, bn), lambda j, k: (j, 0, 0)),
        scratch_shapes=[pltpu.VMEM((m, bn), F32)],
        compiler_params=_cparams(("parallel", "arbitrary"), VMEM_BIG),
    )(a, b, *extra)


def _ssm_bgrad(d_bre, d_bim, bt_re, bt_im, rows_in, fold, tile_b):
    gh, cb = d_bre.shape
    nb = SSM_SPLIT
    rb = gh // nb
    gp = nb * cb
    p = fold.shape[1]

    def body(dr_ref, di_ref, br_ref, bi_ref, rin_ref, f_ref, tb_ref, dbr_ref, dbi_ref, rout_ref):
        row = lax.broadcasted_iota(jnp.int32, (rb, cb), 0)
        col = lax.broadcasted_iota(jnp.int32, (rb, cb), 1)
        mask = (row >> 4) == (col >> 6)
        gr = jnp.where(mask, dr_ref[...], 0.0)
        gi = jnp.where(mask, di_ref[...], 0.0)
        cr, ci = rin_ref[0:1, :], rin_ref[1:2, :]
        dbr_ref[...] = _split3_dot(cr * gr + ci * gi, f_ref[...])
        dbi_ref[...] = _split3_dot(cr * gi - ci * gr, f_ref[...])
        br = _split3_dot(br_ref[...], tb_ref[...])
        bi = _split3_dot(bi_ref[...], tb_ref[...])
        rout_ref[...] = jnp.zeros(rout_ref.shape, F32)
        rout_ref[0:1, :] = _colsum(br * gr + bi * gi)
        rout_ref[1:2, :] = _colsum(br * gi - bi * gr)

    dspec = pl.BlockSpec((rb, cb), lambda j: (j, 0))
    rspec = pl.BlockSpec((SUBLANES, cb), lambda j: (0, j))
    ospec = pl.BlockSpec((rb, p), lambda j: (j, 0))
    return pl.pallas_call(
        body, name="ssm_bgrad", grid=(nb,),
        out_shape=(jax.ShapeDtypeStruct((gh, p), F32), jax.ShapeDtypeStruct((gh, p), F32),
                   jax.ShapeDtypeStruct((SUBLANES, gp), F32)),
        in_specs=[dspec, dspec, ospec, ospec, rspec, _full(fold.shape), _full(tile_b.shape)],
        out_specs=(ospec, ospec, rspec),
        compiler_params=_cparams(("parallel",)),
    )(d_bre, d_bim, bt_re, bt_im, rows_in, fold, tile_b)


def _ssm_cgrad(d_cre, d_cim, fold):
    gp, cb = d_cre.shape
    nb = SSM_SPLIT
    rb = gp // nb
    h = fold.shape[1]

    def body(dr_ref, di_ref, f_ref, cr_ref, ci_ref):
        row = lax.broadcasted_iota(jnp.int32, (rb, cb), 0)
        col = lax.broadcasted_iota(jnp.int32, (rb, cb), 1)
        mask = (row >> 6) == (col >> 4)
        cr_ref[...] = _split3_dot(jnp.where(mask, dr_ref[...], 0.0), f_ref[...])
        ci_ref[...] = -_split3_dot(jnp.where(mask, di_ref[...], 0.0), f_ref[...])

    cspec = pl.BlockSpec((rb, cb), lambda j: (j, 0))
    ospec = pl.BlockSpec((rb, h), lambda j: (j, 0))
    return pl.pallas_call(
        body, name="ssm_cgrad", grid=(nb,),
        out_shape=(jax.ShapeDtypeStruct((gp, h), F32),) * 2,
        in_specs=[cspec, cspec, _full(fold.shape)], out_specs=(ospec, ospec),
        compiler_params=_cparams(("parallel",)),
    )(d_cre, d_cim, fold)


def _ssm_lamgrad(lam_re, lam_im, log_step, abar_re, abar_im, coef_re, coef_im, gc_re, gc_im, ga_re, ga_im):
    g, p = lam_re.shape

    def body(lr_ref, li_ref, ls_ref, ar_ref, ai_ref, cr_ref, ci_ref, gcr_ref, gci_ref, gar_ref, gai_ref,
             dlr_ref, dli_ref, dls_ref):
        lam_raw = lr_ref[...]
        lr = jnp.minimum(lam_raw, LAMBDA_RE_MAX)
        li = li_ref[...]
        st = jnp.exp(ls_ref[...])
        den = lr * lr + li * li
        gcr, gci = gcr_ref[...], gci_ref[...]
        gab_r = gar_ref[...] + (lr * gcr - li * gci) / den
        gab_i = gai_ref[...] + (lr * gci + li * gcr) / den
        cr, ci = cr_ref[...], ci_ref[...]
        wr = -(cr * lr + ci * li) / den
        wi = -(ci * lr - cr * li) / den
        gl_r = wr * gcr + wi * gci
        gl_i = wr * gci - wi * gcr
        ar, ai = ar_ref[...], ai_ref[...]
        gw_r = ar * gab_r + ai * gab_i
        gw_i = ar * gab_i - ai * gab_r
        gl_r = gl_r + st * gw_r
        gl_i = gl_i + st * gw_i
        pass_through = jnp.where(lam_raw < LAMBDA_RE_MAX, 1.0, jnp.where(lam_raw == LAMBDA_RE_MAX, 0.5, 0.0))
        dlr_ref[...] = gl_r * pass_through
        dli_ref[...] = gl_i
        dls_ref[...] = st * jnp.sum(lr * gw_r + li * gw_i, axis=1, keepdims=True)

    sds = jax.ShapeDtypeStruct((g, p), F32)
    return pl.pallas_call(body, name="ssm_lamgrad", out_shape=(sds, sds, jax.ShapeDtypeStruct((g, 1), F32)))(
        lam_re, lam_im, log_step, abar_re, abar_im, coef_re, coef_im, gc_re, gc_im, ga_re, ga_im)


def _row_block(r, most=512):
    for rb in range(min(r, most), BF16_ROWS - 1, -1):
        if r % rb == 0 and rb % BF16_ROWS == 0:
            return rb
    return r


def _adamw_math(w, g, m, v):
    m = ADAM_B1 * m + (1.0 - ADAM_B1) * g
    v = ADAM_B2 * v + (1.0 - ADAM_B2) * (g * g)
    m_hat = m / (1.0 - ADAM_B1 ** ADAM_STEP)
    v_hat = v / (1.0 - ADAM_B2 ** ADAM_STEP)
    delta = -ADAM_LR * (m_hat / (jnp.sqrt(v_hat) + ADAM_EPS) + ADAM_WD * w)
    return delta, m, v


def _adamw_big(p_mine, p_sib, w, m, v, name):
    r, c = w.shape
    rb = _row_block(r)

    def body(a_ref, b_ref, w_ref, m_ref, v_ref, g_ref, d_ref, mo_ref, vo_ref):
        g = a_ref[...].astype(F32) + b_ref[...].astype(F32)
        g_ref[...] = g
        d_ref[...], mo_ref[...], vo_ref[...] = _adamw_math(w_ref[...], g, m_ref[...], v_ref[...])

    spec = pl.BlockSpec((rb, c), lambda i: (i, 0))
    sds = jax.ShapeDtypeStruct((r, c), F32)
    return pl.pallas_call(
        body, name=name, grid=(r // rb,), out_shape=(sds,) * 4, in_specs=[spec] * 5, out_specs=(spec,) * 4,
        compiler_params=_cparams(("parallel",), VMEM_KEEP_OPERANDS_IN_HBM),
    )(p_mine, p_sib, w, m, v)


def _sum_blocks(stack, name):
    n, r, c = stack.shape
    rb = _row_block(r)

    def body(s_ref, o_ref):
        acc = s_ref[0].astype(F32)
        for k in range(1, n):
            acc = acc + s_ref[k].astype(F32)
        o_ref[...] = acc

    return pl.pallas_call(
        body, name=name, grid=(r // rb,), out_shape=jax.ShapeDtypeStruct((r, c), F32),
        in_specs=[pl.BlockSpec((n, rb, c), lambda i: (0, i, 0))], out_specs=pl.BlockSpec((rb, c), lambda i: (i, 0)),
        compiler_params=_cparams(("parallel",), VMEM_KEEP_OPERANDS_IN_HBM),
    )(stack)


def _sum_landed(landed, own, chip, name):
    n, r, c = landed.shape
    rb = _row_block(r)

    def body(chip_ref, own_ref, l1_ref, l2_ref, l3_ref, o_ref):
        acc = own_ref[0].astype(F32)
        for ref in (l1_ref, l2_ref, l3_ref):
            acc = acc + ref[0].astype(F32)
        o_ref[...] = acc.astype(BF16)

    def slot(k):
        return pl.BlockSpec((1, rb, c), lambda i, ch: ((ch[0] + k) % n, i, 0))

    return pl.pallas_call(
        body, name=name, out_shape=jax.ShapeDtypeStruct((r, c), BF16),
        grid_spec=pltpu.PrefetchScalarGridSpec(
            num_scalar_prefetch=1, grid=(r // rb,), in_specs=[slot(0), slot(1), slot(2), slot(3)],
            out_specs=pl.BlockSpec((rb, c), lambda i, ch: (i, 0))),
        compiler_params=_cparams(("parallel",), VMEM_KEEP_OPERANDS_IN_HBM),
    )(jnp.reshape(chip, (1,)).astype(jnp.int32), own, landed, landed, landed)


def _add2(a, b):
    def body(a_ref, b_ref, o_ref):
        o_ref[...] = a_ref[...] + b_ref[...]

    return pl.pallas_call(body, name="add_small", out_shape=jax.ShapeDtypeStruct(a.shape, F32))(a, b)


def _adamw_ada(c_all, dmod_cols, w, m, v):
    d, n = w.shape
    bn = 512

    def body(c_ref, dm_ref, w_ref, m_ref, v_ref, g_ref, d_ref, mo_ref, vo_ref):
        cc = c_ref[...]
        g = _dot_tn(cc * _sigmoid(cc), dm_ref[...])
        g_ref[...] = g
        d_ref[...], mo_ref[...], vo_ref[...] = _adamw_math(w_ref[...], g, m_ref[...], v_ref[...])

    spec = pl.BlockSpec((d, bn), lambda j: (0, j))
    sds = jax.ShapeDtypeStruct((d, n), F32)
    return pl.pallas_call(
        body, name="adamw_ada", grid=(n // bn,), out_shape=(sds,) * 4,
        in_specs=[_full((N_DEV, d)), pl.BlockSpec((N_DEV, bn), lambda j: (0, j)), spec, spec, spec],
        out_specs=(spec,) * 4, compiler_params=_cparams(("parallel",), VMEM_KEEP_OPERANDS_IN_HBM),
    )(c_all, dmod_cols, w, m, v)


def _adamw_small(items):
    n = len(items)

    def body(*refs):
        ins, outs = refs[:4 * n], refs[4 * n:]
        for k in range(n):
            w_ref, g_ref, m_ref, v_ref = ins[4 * k:4 * k + 4]
            outs[3 * k][...], outs[3 * k + 1][...], outs[3 * k + 2][...] = _adamw_math(
                w_ref[...], g_ref[...], m_ref[...], v_ref[...])

    flat = [a for it in items for a in it]
    out_shape = tuple(jax.ShapeDtypeStruct(it[0].shape, F32) for it in items for _ in range(3))
    res = pl.pallas_call(body, name="adamw_small", out_shape=out_shape,
                         compiler_params=_cparams(vmem=VMEM_KEEP_OPERANDS_IN_HBM))(*flat)
    return [tuple(res[3 * k:3 * k + 3]) for k in range(n)]


def _group_mean_matrix(n, group):
    idx = np.arange(n) // group
    return (idx[:, None] == idx[None, :]).astype(np.float32) / group


def _fold_matrix(n, period):
    return (np.arange(n)[:, None] % period == np.arange(period)[None, :]).astype(np.float32)


def _rows8(*rows):
    c = rows[0].shape[-1]
    pad = jnp.zeros((SUBLANES - len(rows), c), F32)
    return jnp.concatenate([r.reshape(1, c) for r in rows] + [pad], axis=0)


def _to_rows(a, width):
    flat = a.reshape(-1)
    n = -(-flat.shape[0] // width)
    flat = jnp.pad(flat, (0, n * width - flat.shape[0]))
    return flat.reshape(n, width)


def kernel(x, c, w_ada, b_ada, g_pre_mix, g_post_mix, w_in, ssm_lam_re, ssm_lam_im, ssm_log_step, ssm_b_re, ssm_b_im, ssm_c_re, ssm_c_im, ssm_d, glu_w, glu_b, g_out_ssm, conv_w, g_out_conv, w_out, g_pre_ffn, g_post_ffn, w_up, ffn_conv_w, w_down, loss_target, m_w_ada, m_b_ada, m_g_pre_mix, m_g_post_mix, m_w_in, m_ssm_lam_re, m_ssm_lam_im, m_ssm_log_step, m_ssm_b_re, m_ssm_b_im, m_ssm_c_re, m_ssm_c_im, m_ssm_d, m_glu_w, m_glu_b, m_g_out_ssm, m_conv_w, m_g_out_conv, m_w_out, m_g_pre_ffn, m_g_post_ffn, m_w_up, m_ffn_conv_w, m_w_down, v_w_ada, v_b_ada, v_g_pre_mix, v_g_post_mix, v_w_in, v_ssm_lam_re, v_ssm_lam_im, v_ssm_log_step, v_ssm_b_re, v_ssm_b_im, v_ssm_c_re, v_ssm_c_im, v_ssm_d, v_glu_w, v_glu_b, v_g_out_ssm, v_conv_w, v_g_out_conv, v_w_out, v_g_pre_ffn, v_g_post_ffn, v_w_up, v_ffn_conv_w, v_w_down):
    xs = x[0]
    tgt = loss_target[0]
    t, d = xs.shape
    xi, yi, ci = lax.axis_index("x"), lax.axis_index("y"), lax.axis_index("c")
    chip = 2 * xi + yi
    dev = 2 * chip + ci

    n_groups, n_state = ssm_lam_re.shape[1:]
    n_gch = ssm_b_re.shape[3]
    d_ssm = n_groups * n_gch
    gp = n_groups * n_state
    n_ada = w_ada.shape[2]
    d_ff = w_down.shape[1] * N_CHIPS
    n_upc = w_up.shape[2]

    w_names = ("w_in", "glu_w", "w_out", "w_up", "w_down")
    c_gath = _allgather8(jnp.broadcast_to(c, (SUBLANES, d)), "gather_c")
    c_all = c_gath.reshape(N_DEV, SUBLANES, d)[:, 0, :]

    def pad8(a):
        return jnp.concatenate([a, jnp.zeros((SUBLANES - a.shape[0], a.shape[1]), a.dtype)], axis=0)

    def start(name, arrs, after):
        return _chips_start(name, True, [], [_landing(a, chip) for a in arrs], after)

    w_names = ("w_in", "mod", "conv_w", "ffn_conv_w", "glu_w", "w_out", "w_up", "w_down")
    first = start("weights_start_in", [w_in[0].astype(BF16)], c_gath)
    b_sh = lax.dynamic_slice(b_ada, (0, chip * n_ada), (1, n_ada))
    mod_sh = _mod_shard(c_all + first[4][0:1, 0:1], w_ada[0], b_sh)
    second = start("weights_start_rest", [mod_sh, pad8(conv_w[0]), pad8(ffn_conv_w[0])]
                   + [w[0].astype(BF16) for w in (glu_w, w_out, w_up, w_down)], None)
    w_send, w_recv, w_land = [list(first[k]) + list(second[k]) for k in (0, 1, 3)]
    w_token = second[4]

    def weights(names, after):
        ks = [w_names.index(nm) for nm in names]
        return _chips_wait("weights_wait_" + names[-1], True, [w_send[k] for k in ks], [w_recv[k] for k in ks],
                           [], [w_land[k] for k in ks], after)[1]

    lam_re, lam_im = ssm_lam_re[0], ssm_lam_im[0]
    log_step = ssm_log_step[0].reshape(n_groups, 1) + w_token[0:1, 0:1]
    abar_re, abar_im, coef_re, coef_im = _ssm_prep(lam_re, lam_im, log_step)
    a_rows = _rows8(abar_re.reshape(1, gp), abar_im.reshape(1, gp))
    coef_rows = _rows8(coef_re.reshape(1, gp), coef_im.reshape(1, gp))
    bt_re = ssm_b_re[0].transpose(0, 2, 1).reshape(d_ssm, n_state)
    bt_im = ssm_b_im[0].transpose(0, 2, 1).reshape(d_ssm, n_state)
    ct_re = ssm_c_re[0].transpose(0, 2, 1).reshape(gp, n_gch)
    ct_im = ssm_c_im[0].transpose(0, 2, 1).reshape(gp, n_gch)
    tile_b = jnp.asarray(np.tile(np.eye(n_state), (1, n_groups // SSM_SPLIT)), BF16)
    tile_c = jnp.asarray(np.tile(np.eye(n_gch), (1, n_groups)), BF16)
    bblk_re, bblk_im, cblk_re, cblk_im = _ssm_blocks(bt_re, bt_im, ct_re, ct_im, coef_rows, tile_b, tile_c)

    h16 = jnp.asarray(_group_mean_matrix(d_ssm, n_gch), BF16)
    h64 = jnp.asarray(_group_mean_matrix(d_ssm, CONV_HEAD_DIM), BF16)

    g_mod, g_cw, g_fw, w_in_st = weights(("mod", "conv_w", "ffn_conv_w", "w_in"), bblk_re)
    mod_all = g_mod.transpose(1, 0, 2).reshape(N_DEV, N_CHIPS * n_ada)
    mod = lax.dynamic_slice(mod_all, (dev, 0), (1, N_CHIPS * n_ada))
    sh1, sc1, gt1, sh2, sc2, gt2 = [mod[:, k * d:(k + 1) * d] for k in range(6)]
    convw_full = pad8(g_cw[:, :3, :].transpose(1, 0, 2).reshape(3, d_ssm))
    fw_full = pad8(g_fw[:, :3, :].transpose(1, 0, 2).reshape(3, N_CHIPS * n_upc))

    v512 = _rows8(ssm_d, glu_b, g_out_ssm, g_out_conv)
    vec1 =_rows8(g_pre_mix, 1.0 + sc1, sh1)
    vd1 = _rows8(g_post_mix, gt1)
    vec2 = _rows8(g_pre_ffn, 1.0 + sc2, sh2)
    vd2 = _rows8(g_post_ffn, gt2)

    proj, bu_re, bu_im, h1b = _mix_in(xs, vec1, w_in_st, bblk_re, bblk_im)
    s_re, s_im = _scan_fwd(a_rows, bu_re, bu_im)
    g_glu, g_wout = weights(("glu_w", "w_out"), s_re)
    glu_full = g_glu.reshape(d_ssm, d_ssm)
    w_out_full = g_wout.reshape(2 * d_ssm, d)
    y1, o_mix, x2 = _mix_out(xs, proj, s_re, s_im, cblk_re, cblk_im, v512, convw_full, glu_full, h16, h64,
                             w_out_full, vd1)
    (w_up_st,) = weights(("w_up",), x2)
    up, h2b = _ffn_up(x2, vec2, w_up_st)
    (g_wdown,) = weights(("w_down",), up)
    w_down_full = g_wdown.reshape(d_ff, d)
    actb, ddnb, dout, dhid, vp_dn, loss_blk = _ffn_down(up, fw_full, w_down_full, w_down_full.T, x2, tgt, vd2)

    gw_down = _matmul_tn(actb, ddnb, d_ff, d, BF16, "dw_down", bt=1024).reshape(N_CHIPS, d_ff // N_CHIPS, d)
    dx2, dupb, vp_up, df_rows = _ffn_up_bwd(dhid, up, fw_full, x2, dout, vec2, w_up_st)
    gw_up = _matmul_tn(h2b, dupb, d, n_upc, BF16, "dw_up", bt=2048)
    ga_send, ga_recv, ga_src, ga_land, ga_token = _chips_start(
        "grads_start_ffn", False, [gw_down, gw_up], [lax.empty(g.shape, g.dtype) for g in (gw_down, gw_up)])
    (dob, ycatb, zb, dqb, dy1, g_re, g_im, dcc, dbg, vp_mo, vp5, d_cre, d_cim) = _mix_out_bwd(
        dx2, o_mix, y1, proj, s_re, s_im, cblk_re, cblk_im, v512, convw_full, glu_full, h16, h64, w_out_full,
        vd1 + ga_token[0:1, 0:1])
    gw_out = _matmul_tn(ycatb, dob, 2 * d_ssm, d, BF16, "dw_out", bt=2048)
    gw_out = gw_out.reshape(N_CHIPS, 2 * d_ssm // N_CHIPS, d)
    gw_glu = _matmul_tn(zb, dqb, d_ssm, d_ssm, BF16, "dw_glu", bt=2048).reshape(N_CHIPS, d_ssm // N_CHIPS, d_ssm)
    gb_send, gb_recv, gb_src, gb_land, gb_token = _chips_start(
        "grads_start_mix", False, [gw_out, gw_glu], [lax.empty(g.shape, g.dtype) for g in (gw_out, gw_glu)])
    gt_re, gt_im, ga_re8, ga_im8 = _scan_bwd(a_rows + gb_token[0:1, 0:1], g_re, g_im, s_re, s_im)
    grad_x, dprojb, vp_mi, d_bre, d_bim = _mix_in_bwd(gt_re, gt_im, bblk_re, bblk_im, dy1, dcc, dbg, proj, xs, dx2,
                                                      vec1, v512, convw_full, w_in_st)
    ssm_u, ssm_s = d_ssm // SSM_SPLIT, gp // SSM_SPLIT

    fold_b = jnp.asarray(_fold_matrix(ssm_s, n_state), BF16)
    fold_c = jnp.asarray(_fold_matrix(ssm_u, n_gch), BF16)
    db_re_f, db_im_f, gc_rows = _ssm_bgrad(d_bre, d_bim, bt_re, bt_im, coef_rows, fold_b, tile_b)
    dc_re_f, dc_im_f = _ssm_cgrad(d_cre, d_cim, fold_c)
    ga_sum = _ga_rowsum(ga_re8, ga_im8)
    g_lam_re, g_lam_im, g_log_step = _ssm_lamgrad(
        lam_re, lam_im, log_step, abar_re, abar_im, coef_re, coef_im,
        gc_rows[0].reshape(n_groups, n_state), gc_rows[1].reshape(n_groups, n_state),
        ga_sum[0].reshape(n_groups, n_state), ga_sum[1].reshape(n_groups, n_state))
    g_b_re = db_re_f.reshape(n_groups, n_gch, n_state).transpose(0, 2, 1)
    g_b_im = db_im_f.reshape(n_groups, n_gch, n_state).transpose(0, 2, 1)
    g_c_re = dc_re_f.reshape(n_groups, n_state, n_gch).transpose(0, 2, 1)
    g_c_im = dc_im_f.reshape(n_groups, n_state, n_gch).transpose(0, 2, 1)

    dmod = jnp.concatenate([vp_mi[0:1], vp_mi[1:2], vp_mo[0:1], vp_up[0:1], vp_up[1:2], vp_dn[0:1]], axis=1)
    small = [
        ("g_pre_mix", vp_mi[2:3]), ("g_post_mix", vp_mo[1:2]), ("g_pre_ffn", vp_up[2:3]), ("g_post_ffn", vp_dn[1:2]),
        ("ssm_lam_re", g_lam_re), ("ssm_lam_im", g_lam_im), ("ssm_log_step", g_log_step),
        ("ssm_b_re", g_b_re), ("ssm_b_im", g_b_im), ("ssm_c_re", g_c_re), ("ssm_c_im", g_c_im),
        ("ssm_d", vp5[3:4]), ("glu_b", vp5[2:3]), ("g_out_ssm", vp5[0:1]), ("g_out_conv", vp5[1:2]),
        ("conv_w", vp5[4:7]), ("ffn_conv_w", df_rows[0:3]), ("loss", loss_blk[0:1, 0:1]),
    ]
    packed, offsets, row = [], {}, 0
    for name, a in small:
        r = _to_rows(a, d)
        offsets[name] = (row, a.shape)
        packed.append(r)
        row += r.shape[0]
    n_small = -(-row // SUBLANES) * SUBLANES
    packed.append(jnp.zeros((n_small - row, d), F32))
    packed.append(pad8(dmod.reshape(6, d)))
    pack = jnp.concatenate(packed, axis=0)
    sm_send, sm_recv, _, sm_land, sm_token = _chips_start("small_start", True, [], [_landing(pack, chip)])

    gw_in = _matmul_tn(h1b, dprojb, d, w_in.shape[2], BF16, "dw_in", bt=2048, after=sm_token)
    gc_send, gc_recv, gc_src, gc_land, gc_token = _chips_start(
        "grads_start_in", False, [gw_in], [lax.empty(gw_in.shape, gw_in.dtype)])

    def partials(names, own, landed):
        return [_sum_landed(l, o, chip, "sum_" + nm) for l, o, nm in zip(landed, own, names)]

    def update(names, mine, theirs):
        done = {}
        for nm, pm, ps in zip(names, mine, theirs):
            w_, m_, v_ = big_params[nm]
            done[nm] = _adamw_big(pm, ps, w_[0], m_[0], v_[0], "adamw_" + nm)
        return done

    big_params = {"w_down": (w_down, m_w_down, v_w_down), "w_up": (w_up, m_w_up, v_w_up),
                  "w_out": (w_out, m_w_out, v_w_out), "glu_w": (glu_w, m_glu_w, v_glu_w),
                  "w_in": (w_in, m_w_in, v_w_in)}
    ffn_names, mix_names = ("w_down", "w_up"), ("w_out", "glu_w", "w_in")
    p_ffn = partials(ffn_names, *_chips_wait("grads_wait_ffn", False, ga_send, ga_recv, ga_src, ga_land, gc_token))
    sa_send, sa_recv, sa_src, sa_land, sa_token = _sibling_start("swap_start_ffn", p_ffn)

    (sm_landed,) = _chips_wait("small_wait", True, sm_send, sm_recv, [], sm_land, sa_token)[1]
    sm_part = _sum_blocks(sm_landed, "sum_small")
    dmod_mine = sm_landed[:, n_small:n_small + SUBLANES, :]
    ss_send, ss_recv, ss_src, ss_land, ss_token = _sibling_start("swap_start_small", [sm_part, dmod_mine])
    p_ffn, t_ffn = _sibling_wait("swap_wait_ffn", sa_send, sa_recv, sa_src, sa_land, ss_token)
    big = update(ffn_names, p_ffn, t_ffn)
    (sm_part, dmod_mine), (sm_sib, dmod_sib) = _sibling_wait("swap_wait_small", ss_send, ss_recv, ss_src, ss_land,
                                                              big["w_up"][0])
    sums = _add2(sm_part, sm_sib)
    dmod_by_core = jnp.stack([dmod_mine, dmod_sib], axis=1)
    dmod_by_core = jnp.where(ci == 0, dmod_by_core, dmod_by_core[:, ::-1])
    dmod_all = dmod_by_core[:, :, :6, :].reshape(N_DEV, 6 * d)
    g_b_ada = sums[n_small:n_small + 6].reshape(1, 6 * d)

    def unpack(name):
        r0, shape = offsets[name]
        size = math.prod(shape)
        nrow = -(-size // d)
        return sums[r0:r0 + nrow].reshape(-1)[:size].reshape(shape)

    p_mix = partials(mix_names, *_chips_wait(
        "grads_wait_mix", False, list(gb_send) + list(gc_send), list(gb_recv) + list(gc_recv),
        list(gb_src) + list(gc_src), list(gb_land) + list(gc_land), sums))
    sb_send, sb_recv, sb_src, sb_land, sb_token = _sibling_start("swap_start_mix", p_mix)

    dmod_cols = lax.dynamic_slice(dmod_all, (0, chip * n_ada), (N_DEV, n_ada)) + sb_token[0:1, 0:1]
    ada = _adamw_ada(c_all, dmod_cols, w_ada[0], m_w_ada[0], v_w_ada[0])
    p_mix, t_mix = _sibling_wait("swap_wait_mix", sb_send, sb_recv, sb_src, sb_land, ada[0])
    big.update(update(mix_names, p_mix, t_mix))

    g_small = {name: unpack(name) for name, _ in small}
    g_small["b_ada"] = g_b_ada
    g_small["conv_w"] = lax.dynamic_slice(g_small["conv_w"], (0, chip * conv_w.shape[2]), (3, conv_w.shape[2]))
    g_small["ffn_conv_w"] = lax.dynamic_slice(g_small["ffn_conv_w"], (0, chip * n_upc), (3, n_upc))
    g_small["ssm_log_step"] = g_small["ssm_log_step"].reshape(1, n_groups)
    small_params = {
        "b_ada": (b_ada, m_b_ada, v_b_ada), "g_pre_mix": (g_pre_mix, m_g_pre_mix, v_g_pre_mix),
        "g_post_mix": (g_post_mix, m_g_post_mix, v_g_post_mix), "ssm_lam_re": (ssm_lam_re, m_ssm_lam_re, v_ssm_lam_re),
        "ssm_lam_im": (ssm_lam_im, m_ssm_lam_im, v_ssm_lam_im),
        "ssm_log_step": (ssm_log_step, m_ssm_log_step, v_ssm_log_step),
        "ssm_b_re": (ssm_b_re, m_ssm_b_re, v_ssm_b_re), "ssm_b_im": (ssm_b_im, m_ssm_b_im, v_ssm_b_im),
        "ssm_c_re": (ssm_c_re, m_ssm_c_re, v_ssm_c_re), "ssm_c_im": (ssm_c_im, m_ssm_c_im, v_ssm_c_im),
        "ssm_d": (ssm_d, m_ssm_d, v_ssm_d), "glu_b": (glu_b, m_glu_b, v_glu_b),
        "g_out_ssm": (g_out_ssm, m_g_out_ssm, v_g_out_ssm), "conv_w": (conv_w, m_conv_w, v_conv_w),
        "g_out_conv": (g_out_conv, m_g_out_conv, v_g_out_conv), "g_pre_ffn": (g_pre_ffn, m_g_pre_ffn, v_g_pre_ffn),
        "g_post_ffn": (g_post_ffn, m_g_post_ffn, v_g_post_ffn),
        "ffn_conv_w": (ffn_conv_w, m_ffn_conv_w, v_ffn_conv_w),
    }

    def natural(a):
        return a[0] if a.ndim > 2 else a

    names = list(small_params)
    items = []
    for nm in names:
        w_, m_, v_ = small_params[nm]
        items.append((natural(w_), g_small[nm].reshape(natural(w_).shape), natural(m_), natural(v_)))
    upd = _adamw_small(items)
    small_out = {}
    for nm, (dl, mo, vo) in zip(names, upd):
        shp = small_params[nm][0].shape
        small_out[nm] = (g_small[nm].reshape(shp), dl.reshape(shp), mo.reshape(shp), vo.reshape(shp))

    loss = g_small["loss"][0, 0]

    order = ["w_ada", "b_ada", "g_pre_mix", "g_post_mix", "w_in", "ssm_lam_re", "ssm_lam_im", "ssm_log_step",
             "ssm_b_re", "ssm_b_im", "ssm_c_re", "ssm_c_im", "ssm_d", "glu_w", "glu_b", "g_out_ssm", "conv_w",
             "g_out_conv", "w_out", "g_pre_ffn", "g_post_ffn", "w_up", "ffn_conv_w", "w_down"]
    results = {"w_ada": tuple(a[None] for a in ada)}
    for nm in big:
        results[nm] = tuple(a[None] for a in big[nm])
    results.update(small_out)
    outs = [loss, grad_x[None]]
    for k in range(4):
        outs += [results[nm][k] for nm in order]
    return tuple(outs)


def _ga_rowsum(ga_re8, ga_im8):
    n = ga_re8.shape[1]

    def body(r_ref, i_ref, o_ref):
        o_ref[...] = jnp.zeros(o_ref.shape, F32)
        o_ref[0:1, :] = _colsum(r_ref[...])
        o_ref[1:2, :] = _colsum(i_ref[...])

    return pl.pallas_call(body, name="ga_rowsum", out_shape=jax.ShapeDtypeStruct((SUBLANES, n), F32))(ga_re8, ga_im8)
```

```python
import functools
import math

import jax
import jax.numpy as jnp
import numpy as np
from jax import lax
from jax.experimental import pallas as pl
from jax.experimental.pallas import tpu as pltpu

F32 = jnp.float32
BF16 = jnp.bfloat16
MESH = pl.DeviceIdType.MESH

EPS = 1e-6
LAMBDA_RE_MAX = -1e-4
ADAM_LR = 0.001
ADAM_B1 = 0.9
ADAM_B2 = 0.999
ADAM_EPS = 1e-08
ADAM_WD = 0.01
ADAM_STEP = 10

SUBLANES = 8
BF16_ROWS = 16
N_CHIPS = 4
N_DEV = 8
CONV_HEAD_DIM = 64
VMEM_BIG = 56 * 1024 * 1024
VMEM_MID = 40 * 1024 * 1024
VMEM_KEEP_OPERANDS_IN_HBM = 62 * 1024 * 1024

TB_MIX = 256
TB_MIX_FWD = 512
TB_FFN = 256
TB_FFN_UP = 512
TB_SCAN = 2048
W_SCAN = 256
SSM_SPLIT = 4
CW_FFN = 256
SCAN_UNROLL = 4
TB_TN = 512


def _cparams(sem=None, vmem=None):
    kw = {}
    if sem is not None:
        kw["dimension_semantics"] = sem
    if vmem is not None:
        kw["vmem_limit_bytes"] = vmem
    return pltpu.CompilerParams(**kw)


def _blk(t, pref):
    return pref if t % pref == 0 else t


def _dot(a, b):
    return jnp.dot(a.astype(BF16), b.astype(BF16), preferred_element_type=F32)


def _dot_nt(a, b):
    return lax.dot_general(a.astype(BF16), b.astype(BF16), (((1,), (1,)), ((), ())),
                           preferred_element_type=F32)


def _dot_tn(a, b):
    return lax.dot_general(a.astype(BF16), b.astype(BF16), (((0,), (0,)), ((), ())),
                           preferred_element_type=F32)


def _sigmoid(x):
    return 0.5 * jnp.tanh(0.5 * x) + 0.5


_GELU_K = math.sqrt(2.0 / math.pi)
_GELU_C = 0.044715


def _gelu(x):
    th = jnp.tanh(_GELU_K * (x + _GELU_C * x * x * x))
    return x * (0.5 * (1.0 + th))


def _gelu_and_grad(x):
    x2 = x * x
    th = jnp.tanh(_GELU_K * (x + _GELU_C * x2 * x))
    half = 0.5 * (1.0 + th)
    return x * half, half + 0.5 * x * (1.0 - th * th) * _GELU_K * (1.0 + 3.0 * _GELU_C * x2)


def _rowmean(x):
    return jnp.mean(x, axis=-1, keepdims=True)


def _colsum(x):
    return jnp.sum(x, axis=0, keepdims=True)


def _split_dot(x, m):
    hi = x.astype(BF16)
    lo = (x - hi.astype(F32)).astype(BF16)
    return (jnp.dot(hi, m, preferred_element_type=F32) + jnp.dot(lo, m, preferred_element_type=F32))


def _split3_dot(x, m):
    hi = x.astype(BF16)
    r1 = x - hi.astype(F32)
    mid = r1.astype(BF16)
    lo = (r1 - mid.astype(F32)).astype(BF16)
    return (jnp.dot(hi, m, preferred_element_type=F32) + jnp.dot(mid, m, preferred_element_type=F32)
            + jnp.dot(lo, m, preferred_element_type=F32))


def _shift_down(x, halo, k):
    r = pltpu.roll(x, k, 0)
    row = lax.broadcasted_iota(jnp.int32, x.shape, 0)
    last = halo.shape[0]
    for j in range(k):
        r = jnp.where(row == j, halo[last - k + j:last - k + j + 1, :], r)
    return r


def _shift_up(x, halo, k):
    n = x.shape[0]
    r = pltpu.roll(x, n - k, 0)
    row = lax.broadcasted_iota(jnp.int32, x.shape, 0)
    for j in range(k):
        r = jnp.where(row == n - k + j, halo[j:j + 1, :], r)
    return r


def _acc_rows(ref, first, rows):
    @pl.when(first)
    def _():
        ref[...] = jnp.zeros(ref.shape, ref.dtype)
    for j, r in enumerate(rows):
        ref[j:j + 1, :] += r


def _rows(tb, c, col=0):
    return pl.BlockSpec((tb, c), lambda i, col=col: (i, col))


def _full(shape):
    nd = len(shape)
    return pl.BlockSpec(shape, lambda i, nd=nd: (0,) * nd)


def _resident(shape):
    nd = len(shape)
    return pl.BlockSpec(shape, lambda i, nd=nd: (0,) * nd, pipeline_mode=pl.Buffered(1))


def _halo_prev(tb, c, col=0, rows=SUBLANES):
    per = tb // rows
    return pl.BlockSpec((rows, c), lambda i, col=col: (jnp.maximum(i * per - 1, 0), col))


def _halo_next(tb, c, t, col=0, rows=SUBLANES):
    per = tb // rows
    last = t // rows - 1
    return pl.BlockSpec((rows, c), lambda i, col=col: (jnp.minimum((i + 1) * per, last), col))


def _mesh_pos():
    return lax.axis_index("x"), lax.axis_index("y"), lax.axis_index("c")


def _allgather8(x_pad, name):
    m_per, n = x_pad.shape

    def body(x_ref, out_ref, send_sems, recv_sems, local_sem):
        x, y, c = _mesh_pos()
        me, sibling = (x, y, c), (x, y, 1 - c)
        chips = [(1 - x, y), (x, 1 - y), (1 - x, 1 - y)]

        def rows(px, py, pc):
            return out_ref.at[pl.ds((4 * px + 2 * py + pc) * m_per, m_per), :]

        def copy(k, block, to, src=None):
            return pltpu.make_async_remote_copy(
                src_ref=rows(*block) if src is None else src, dst_ref=rows(*block),
                send_sem=send_sems.at[k], recv_sem=recv_sems.at[k], device_id=to, device_id_type=MESH)

        mine = pltpu.make_async_copy(x_ref, rows(*me), local_sem)
        mine.start()
        first = [copy(0, me, sibling, src=x_ref)]
        first += [copy(1 + j, me, (*chip, c), src=x_ref) for j, chip in enumerate(chips)]
        for cp in first:
            cp.start()
        passed = [copy(4 + j, (*chip, c), sibling) for j, chip in enumerate(chips)]
        for j, chip in enumerate(chips):
            copy(1 + j, (*chip, c), me).wait_recv()
            passed[j].start()
        copy(0, sibling, me).wait_recv()
        for j, chip in enumerate(chips):
            copy(4 + j, (*chip, 1 - c), me).wait_recv()
        for cp in first + passed:
            cp.wait_send()
        mine.wait()

    return pl.pallas_call(
        body, name=name,
        out_shape=jax.ShapeDtypeStruct((N_DEV * m_per, n), F32),
        in_specs=[pl.BlockSpec(memory_space=pltpu.VMEM)],
        out_specs=pl.BlockSpec(memory_space=pltpu.VMEM),
        scratch_shapes=[pltpu.SemaphoreType.DMA((7,)), pltpu.SemaphoreType.DMA((7,)), pltpu.SemaphoreType.DMA],
    )(x_pad)


_HBM = pl.BlockSpec(memory_space=pltpu.HBM)
_SEM = pl.BlockSpec(memory_space=pltpu.SEMAPHORE)
_EFFECT = pltpu.SideEffectType.DATAFLOW_SIDE_EFFECTING


def _chip_copy(gather, src_ref, land_ref, send, recv, j, arrival):
    x, y, c = _mesh_pos()
    peer = [(1 - x, y), (x, 1 - y), (1 - x, 1 - y)][j]
    peer_chip = 2 * peer[0] + peer[1]
    my_chip = 2 * x + y
    return pltpu.make_async_remote_copy(
        src_ref=land_ref.at[my_chip] if gather else src_ref.at[peer_chip],
        dst_ref=land_ref.at[peer_chip if arrival else my_chip],
        send_sem=send.at[j], recv_sem=recv.at[j], device_id=(*peer, c), device_id_type=MESH)


def _chips_start(name, gather, srcs, lands, after=None):
    n, ns = len(lands), len(srcs)
    extra = [] if after is None else [after]

    def body(*refs):
        src_refs, land_refs = refs[:ns], refs[ns:ns + n]
        outs = refs[ns + n + len(extra):]
        sends, recvs, token = outs[:n], outs[n:2 * n], outs[-1]
        for k in range(n):
            for j in range(3):
                _chip_copy(gather, src_refs[k] if ns else None, land_refs[k], sends[k], recvs[k], j, False).start()
        token[...] = jnp.zeros(token.shape, F32)

    sem = pltpu.SemaphoreType.DMA((3,))
    thru = tuple(pltpu.HBM(a.shape, a.dtype) for a in list(srcs) + list(lands))
    res = pl.pallas_call(
        body, name=name,
        out_shape=(sem,) * (2 * n) + thru + (jax.ShapeDtypeStruct((SUBLANES, 128), F32),),
        in_specs=[_HBM] * (ns + n) + [pl.BlockSpec(memory_space=pl.ANY)] * len(extra),
        out_specs=(_SEM,) * (2 * n) + (_HBM,) * (ns + n) + (pl.BlockSpec(memory_space=pltpu.VMEM),),
        input_output_aliases={k: 2 * n + k for k in range(ns + n)},
        compiler_params=pltpu.CompilerParams(has_side_effects=_EFFECT),
    )(*[pltpu.with_memory_space_constraint(a, pltpu.HBM) for a in list(srcs) + list(lands)], *extra)
    return res[:n], res[n:2 * n], res[2 * n:2 * n + ns], res[2 * n + ns:2 * n + ns + n], res[-1]


def _chips_wait(name, gather, sends, recvs, srcs, lands, after):
    n, ns = len(lands), len(srcs)

    def body(*refs):
        src_refs, land_refs = refs[:ns], refs[ns:ns + n]
        sends_, recvs_ = refs[ns + n:ns + 2 * n], refs[ns + 2 * n:ns + 3 * n]
        for k in range(n):
            for j in range(3):
                cp = _chip_copy(gather, src_refs[k] if ns else None, land_refs[k], sends_[k], recvs_[k], j, True)
                cp.wait_send()
                cp.wait_recv()

    thru = tuple(pltpu.HBM(a.shape, a.dtype) for a in list(srcs) + list(lands))
    res = pl.pallas_call(
        body, name=name, out_shape=thru,
        in_specs=[_HBM] * (ns + n) + [_SEM] * (2 * n) + [pl.BlockSpec(memory_space=pl.ANY)],
        out_specs=(_HBM,) * (ns + n),
        input_output_aliases={k: k for k in range(ns + n)},
        compiler_params=pltpu.CompilerParams(has_side_effects=_EFFECT),
    )(*srcs, *lands, *sends, *recvs, after)
    return res[:ns], res[ns:]


def _sibling_copy(src_ref, land_ref, send, recv):
    x, y, c = _mesh_pos()
    return pltpu.make_async_remote_copy(src_ref=src_ref, dst_ref=land_ref, send_sem=send.at[0], recv_sem=recv.at[0],
                                        device_id=(x, y, 1 - c), device_id_type=MESH)


def _sibling_start(name, arrs, after=None):
    n = len(arrs)
    extra = [] if after is None else [after]
    lands = [lax.empty(a.shape, a.dtype) for a in arrs]

    def body(*refs):
        src_refs, land_refs = refs[:n], refs[n:2 * n]
        outs = refs[2 * n + len(extra):]
        sends, recvs, token = outs[:n], outs[n:2 * n], outs[-1]
        for k in range(n):
            _sibling_copy(src_refs[k], land_refs[k], sends[k], recvs[k]).start()
        token[...] = jnp.zeros(token.shape, F32)

    sem = pltpu.SemaphoreType.DMA((1,))
    thru = tuple(pltpu.HBM(a.shape, a.dtype) for a in list(arrs) + lands)
    res = pl.pallas_call(
        body, name=name,
        out_shape=(sem,) * (2 * n) + thru + (jax.ShapeDtypeStruct((SUBLANES, 128), F32),),
        in_specs=[_HBM] * (2 * n) + [pl.BlockSpec(memory_space=pl.ANY)] * len(extra),
        out_specs=(_SEM,) * (2 * n) + (_HBM,) * (2 * n) + (pl.BlockSpec(memory_space=pltpu.VMEM),),
        input_output_aliases={k: 2 * n + k for k in range(2 * n)},
        compiler_params=pltpu.CompilerParams(has_side_effects=_EFFECT),
    )(*[pltpu.with_memory_space_constraint(a, pltpu.HBM) for a in list(arrs) + lands], *extra)
    return res[:n], res[n:2 * n], res[2 * n:3 * n], res[3 * n:4 * n], res[-1]


def _sibling_wait(name, sends, recvs, srcs, lands, after):
    n = len(srcs)

    def body(*refs):
        src_refs, land_refs = refs[:n], refs[n:2 * n]
        sends_, recvs_ = refs[2 * n:3 * n], refs[3 * n:4 * n]
        for k in range(n):
            cp = _sibling_copy(src_refs[k], land_refs[k], sends_[k], recvs_[k])
            cp.wait_send()
            cp.wait_recv()

    thru = tuple(pltpu.HBM(a.shape, a.dtype) for a in list(srcs) + list(lands))
    res = pl.pallas_call(
        body, name=name, out_shape=thru,
        in_specs=[_HBM] * (2 * n) + [_SEM] * (2 * n) + [pl.BlockSpec(memory_space=pl.ANY)],
        out_specs=(_HBM,) * (2 * n),
        input_output_aliases={k: k for k in range(2 * n)},
        compiler_params=pltpu.CompilerParams(has_side_effects=_EFFECT),
    )(*srcs, *lands, *sends, *recvs, after)
    return res[:n], res[n:]


def _landing(own, chip):
    zone = lax.empty((N_CHIPS,) + own.shape, own.dtype)
    return lax.dynamic_update_slice(zone, own[None], (chip,) + (0,) * own.ndim)


def _mod_shard(c_all, w_ada_sh, b_sh):
    d, n = w_ada_sh.shape
    bn = 512

    def body(c_ref, w_ref, b_ref, o_ref):
        cc = c_ref[...]
        ca = cc * _sigmoid(cc)
        o_ref[...] = _dot(ca, w_ref[...]) + b_ref[...]

    return pl.pallas_call(
        body, name="mod_shard", grid=(n // bn,),
        out_shape=jax.ShapeDtypeStruct((N_DEV, n), F32),
        in_specs=[_full((N_DEV, d)), pl.BlockSpec((d, bn), lambda j: (0, j)), pl.BlockSpec((1, bn), lambda j: (0, j))],
        out_specs=pl.BlockSpec((N_DEV, bn), lambda j: (0, j)),
        compiler_params=_cparams(("parallel",)),
    )(c_all, w_ada_sh, b_sh)


def _ssm_prep(lam_re, lam_im, log_step):
    g, p = lam_re.shape

    def body(lr_ref, li_ref, ls_ref, ar_ref, ai_ref, cr_ref, ci_ref):
        lr = jnp.minimum(lr_ref[...], LAMBDA_RE_MAX)
        li = li_ref[...]
        st = jnp.exp(ls_ref[...])
        mag = jnp.exp(lr * st)
        ar = mag * jnp.cos(li * st)
        ai = mag * jnp.sin(li * st)
        den = lr * lr + li * li
        nr = ar - 1.0
        ar_ref[...] = ar
        ai_ref[...] = ai
        cr_ref[...] = (nr * lr + ai * li) / den
        ci_ref[...] = (ai * lr - nr * li) / den

    sds = jax.ShapeDtypeStruct((g, p), F32)
    return pl.pallas_call(body, name="ssm_prep", out_shape=(sds,) * 4)(lam_re, lam_im, log_step)


def _ssm_blocks(bt_re, bt_im, ct_re, ct_im, coef_rows, tile_b, tile_c):
    gh, p = bt_re.shape
    gp, h = ct_re.shape
    nb = SSM_SPLIT
    cb, rb = gp // nb, gp // nb

    def body(btr, bti, ctr, cti, cf, tb_ref, tc_ref, bre_o, bim_o, cre_o, cim_o):
        j = pl.program_id(0)
        row = lax.broadcasted_iota(jnp.int32, (gh, cb), 0)
        col = lax.broadcasted_iota(jnp.int32, (gh, cb), 1) + j * cb
        mask = (row >> 4) == (col >> 6)
        cr, ci = cf[0:1, :], cf[1:2, :]
        br = _split3_dot(btr[...], tb_ref[...])
        bi = _split3_dot(bti[...], tb_ref[...])
        bre_o[...] = jnp.where(mask, br * cr - bi * ci, 0.0).astype(BF16)
        bim_o[...] = jnp.where(mask, br * ci + bi * cr, 0.0).astype(BF16)
        row2 = lax.broadcasted_iota(jnp.int32, (rb, gh), 0) + j * rb
        col2 = lax.broadcasted_iota(jnp.int32, (rb, gh), 1)
        mask2 = (row2 >> 6) == (col2 >> 4)
        cre_o[...] = jnp.where(mask2, _split3_dot(ctr[...], tc_ref[...]), 0.0).astype(BF16)
        cim_o[...] = jnp.where(mask2, _split3_dot(cti[...], tc_ref[...]), 0.0).astype(BF16)

    bspec = pl.BlockSpec((gh, cb), lambda j: (0, j))
    cspec = pl.BlockSpec((rb, gh), lambda j: (j, 0))
    cin = pl.BlockSpec((rb, h), lambda j: (j, 0))
    return pl.pallas_call(
        body, name="ssm_blocks", grid=(nb,),
        out_shape=(jax.ShapeDtypeStruct((gh, gp), BF16),) * 2 + (jax.ShapeDtypeStruct((gp, gh), BF16),) * 2,
        in_specs=[_full((gh, p)), _full((gh, p)), cin, cin, pl.BlockSpec((SUBLANES, cb), lambda j: (0, j)),
                  _full(tile_b.shape), _full(tile_c.shape)],
        out_specs=(bspec, bspec, cspec, cspec),
        compiler_params=_cparams(("parallel",)),
    )(bt_re, bt_im, ct_re, ct_im, coef_rows, tile_b, tile_c)


def _scan_consts(a_ref, reverse):
    w = a_ref.shape[1]
    ar1 = a_ref[0:1, :]
    ai1 = a_ref[1:2, :]
    if reverse:
        ai1 = -ai1
    pr, pi = [ar1], [ai1]
    for _ in range(1, SUBLANES):
        nr = pr[-1] * ar1 - pi[-1] * ai1
        ni = pr[-1] * ai1 + pi[-1] * ar1
        pr.append(nr)
        pi.append(ni)
    row = lax.broadcasted_iota(jnp.int32, (SUBLANES, w), 0)
    dist = (SUBLANES - 1 - row) if reverse else row

    def pick(vals):
        out = jnp.broadcast_to(vals[SUBLANES - 1], (SUBLANES, w))
        for r in range(SUBLANES - 1):
            out = jnp.where(dist == r, vals[r], out)
        return out

    p_r, p_i = pick(pr), pick(pi)
    steps = []
    for k in (1, 2, 4):
        steps.append((k, jnp.where(dist >= k, pr[k - 1], 0.0), jnp.where(dist >= k, pi[k - 1], 0.0)))
    a8 = (jnp.broadcast_to(pr[SUBLANES - 1], (SUBLANES, w)), jnp.broadcast_to(pi[SUBLANES - 1], (SUBLANES, w)))
    return row, p_r, p_i, steps, a8


def _scan_tile(xr, xi, cr, ci, consts, reverse):
    row, p_r, p_i, steps, (a8r, a8i) = consts
    for k, s_r, s_i in steps:
        sh = (SUBLANES - k) if reverse else k
        qr = pltpu.roll(xr, sh, 0)
        qi = pltpu.roll(xi, sh, 0)
        xr, xi = xr + s_r * qr - s_i * qi, xi + s_r * qi + s_i * qr
    outr = xr + p_r * cr - p_i * ci
    outi = xi + p_r * ci + p_i * cr
    e = 0 if reverse else SUBLANES - 1
    er = jnp.broadcast_to(xr[e:e + 1, :], xr.shape)
    ei = jnp.broadcast_to(xi[e:e + 1, :], xi.shape)
    return outr, outi, er + a8r * cr - a8i * ci, ei + a8r * ci + a8i * cr


def _scan_fwd(a_rows, bu_re, bu_im):
    t, n = bu_re.shape
    tb, w = _blk(t, TB_SCAN), W_SCAN
    ntile = tb // SUBLANES

    def body(a_ref, br_ref, bi_ref, sr_ref, si_ref, car, cai):
        @pl.when(pl.program_id(1) == 0)
        def _():
            car[...] = jnp.zeros(car.shape, F32)
            cai[...] = jnp.zeros(cai.shape, F32)
        consts = _scan_consts(a_ref, False)

        def pair(i, carry):
            o = pl.multiple_of(i * BF16_ROWS, BF16_ROWS)
            b_r = br_ref[pl.ds(o, BF16_ROWS), :].astype(F32)
            b_i = bi_ref[pl.ds(o, BF16_ROWS), :].astype(F32)
            outs = []
            for h in range(2):
                rows = slice(h * SUBLANES, (h + 1) * SUBLANES)
                outr, outi, ncr, nci = _scan_tile(b_r[rows, :], b_i[rows, :], carry[0], carry[1], consts, False)
                outs.append((outr, outi))
                carry = (ncr, nci)
            sr_ref[pl.ds(o, BF16_ROWS), :] = jnp.concatenate([outs[0][0], outs[1][0]], axis=0).astype(BF16)
            si_ref[pl.ds(o, BF16_ROWS), :] = jnp.concatenate([outs[0][1], outs[1][1]], axis=0).astype(BF16)
            return carry

        def pairs(i, carry):
            for s in range(SCAN_UNROLL // 2):
                carry = pair(i * (SCAN_UNROLL // 2) + s, carry)
            return carry

        cr, ci = lax.fori_loop(0, ntile // SCAN_UNROLL, pairs, (car[...], cai[...]))
        car[...] = cr
        cai[...] = ci

    spec = pl.BlockSpec((tb, w), lambda s, k: (k, s))
    sds = jax.ShapeDtypeStruct((t, n), BF16)
    return pl.pallas_call(
        body, name="scan_fwd", grid=(n // w, t // tb), out_shape=(sds, sds),
        in_specs=[pl.BlockSpec((SUBLANES, w), lambda s, k: (0, s)), spec, spec], out_specs=(spec, spec),
        scratch_shapes=[pltpu.VMEM((SUBLANES, w), F32), pltpu.VMEM((SUBLANES, w), F32)],
        compiler_params=_cparams(("parallel", "arbitrary"), VMEM_MID),
    )(a_rows, bu_re, bu_im)


def _scan_bwd(a_rows, g_re, g_im, s_re, s_im):
    t, n = g_re.shape
    tb, w = _blk(t, TB_SCAN), W_SCAN
    ntile = tb // SUBLANES
    npair = tb // BF16_ROWS
    nt = t // tb

    def body(a_ref, gr_ref, gi_ref, sr_ref, si_ref, or_ref, oi_ref, gar_ref, gai_ref, car, cai):
        @pl.when(pl.program_id(1) == 0)
        def _():
            car[...] = jnp.zeros(car.shape, F32)
            cai[...] = jnp.zeros(cai.shape, F32)
            gar_ref[...] = jnp.zeros(gar_ref.shape, F32)
            gai_ref[...] = jnp.zeros(gai_ref.shape, F32)
        consts = _scan_consts(a_ref, True)
        row = consts[0]

        def pair(i, carry):
            cr, ci, accr, acci = carry
            o = pl.multiple_of((npair - 1 - i) * BF16_ROWS, BF16_ROWS)
            s_r = sr_ref[pl.ds(o, BF16_ROWS), :].astype(F32)
            s_i = si_ref[pl.ds(o, BF16_ROWS), :].astype(F32)
            g_r = gr_ref[pl.ds(o, BF16_ROWS), :].astype(F32)
            g_i = gi_ref[pl.ds(o, BF16_ROWS), :].astype(F32)
            outs = [None, None]
            for h in (1, 0):
                rows = slice(h * SUBLANES, (h + 1) * SUBLANES)
                outr, outi, ncr, nci = _scan_tile(g_r[rows, :], g_i[rows, :], cr, ci, consts, True)
                outs[h] = (outr, outi)
                gnr = jnp.where(row == SUBLANES - 1, cr, pltpu.roll(outr, SUBLANES - 1, 0))
                gni = jnp.where(row == SUBLANES - 1, ci, pltpu.roll(outi, SUBLANES - 1, 0))
                sr = s_r[h * SUBLANES:(h + 1) * SUBLANES, :]
                si = s_i[h * SUBLANES:(h + 1) * SUBLANES, :]
                accr, acci = accr + sr * gnr + si * gni, acci + sr * gni - si * gnr
                cr, ci = ncr, nci
            or_ref[pl.ds(o, BF16_ROWS), :] = jnp.concatenate([outs[0][0], outs[1][0]], axis=0).astype(BF16)
            oi_ref[pl.ds(o, BF16_ROWS), :] = jnp.concatenate([outs[0][1], outs[1][1]], axis=0).astype(BF16)
            return cr, ci, accr, acci

        def pairs(i, carry):
            for s in range(SCAN_UNROLL // 2):
                carry = pair(i * (SCAN_UNROLL // 2) + s, carry)
            return carry

        cr, ci, accr, acci = lax.fori_loop(0, ntile // SCAN_UNROLL, pairs,
                                           (car[...], cai[...], gar_ref[...], gai_ref[...]))
        car[...] = cr
        cai[...] = ci
        gar_ref[...] = accr
        gai_ref[...] = acci

    spec = pl.BlockSpec((tb, w), lambda s, k: (nt - 1 - k, s))
    aspec = pl.BlockSpec((SUBLANES, w), lambda s, k: (0, s))
    sds = jax.ShapeDtypeStruct((t, n), BF16)
    asds = jax.ShapeDtypeStruct((SUBLANES, n), F32)
    return pl.pallas_call(
        body, name="scan_bwd", grid=(n // w, nt), out_shape=(sds, sds, asds, asds),
        in_specs=[aspec, spec, spec, spec, spec], out_specs=(spec, spec, aspec, aspec),
        scratch_shapes=[pltpu.VMEM((SUBLANES, w), F32), pltpu.VMEM((SUBLANES, w), F32)],
        compiler_params=_cparams(("parallel", "arbitrary"), VMEM_MID),
    )(a_rows, g_re, g_im, s_re, s_im)


def _mix_in(x, vec, w_in_st, b_re, b_im):
    t, d = x.shape
    ns, _, nc = w_in_st.shape
    dssm, nstate = b_re.shape
    du, ds = dssm // SSM_SPLIT, nstate // SSM_SPLIT
    tb = _blk(t, TB_MIX_FWD)

    def body(x_ref, vec_ref, w_ref, bre_ref, bim_ref, proj_ref, bur_ref, bui_ref, h1_ref):
        xv = x_ref[...]
        r = lax.rsqrt(_rowmean(xv * xv) + EPS)
        h = xv * r * vec_ref[0:1, :] * vec_ref[1:2, :] + vec_ref[2:3, :]
        hb = h.astype(BF16)
        h1_ref[...] = hb
        u = None
        for j in range(ns):
            pj = jnp.dot(hb, w_ref[j], preferred_element_type=F32)
            proj_ref[:, j * nc:(j + 1) * nc] = pj.astype(BF16)
            if j == 0:
                u = pj
        ub = u.astype(BF16)
        for q in range(SSM_SPLIT):
            rq, cq = slice(q * du, (q + 1) * du), slice(q * ds, (q + 1) * ds)
            bur_ref[:, cq] = jnp.dot(ub[:, rq], bre_ref[rq, cq], preferred_element_type=F32).astype(BF16)
            bui_ref[:, cq] = jnp.dot(ub[:, rq], bim_ref[rq, cq], preferred_element_type=F32).astype(BF16)

    return pl.pallas_call(
        body, name="mix_in", grid=(t // tb,),
        out_shape=(jax.ShapeDtypeStruct((t, ns * nc), BF16), jax.ShapeDtypeStruct((t, nstate), BF16),
                   jax.ShapeDtypeStruct((t, nstate), BF16), jax.ShapeDtypeStruct((t, d), BF16)),
        in_specs=[_rows(tb, d), _full((SUBLANES, d)), _resident(w_in_st.shape), _resident(b_re.shape),
                  _resident(b_im.shape)],
        out_specs=(_rows(tb, ns * nc), _rows(tb, nstate), _rows(tb, nstate), _rows(tb, d)),
        compiler_params=_cparams(("parallel",), VMEM_BIG),
    )(x, vec, w_in_st, b_re, b_im)


def _head_ms(y, h_ref):
    return _split_dot(y * y, h_ref[...])


def _conv3(x, halo, w_ref):
    return w_ref[0:1, :] * _shift_down(x, halo, 2) + w_ref[1:2, :] * _shift_down(x, halo, 1) + w_ref[2:3, :] * x


def _mix_out(x, proj, s_re, s_im, c_re, c_im, v512, convw, glu_w, h16, h64, w_out, vd):
    t, d = x.shape
    dh = c_re.shape[1]
    nstate = s_re.shape[1]
    du, ds = dh // SSM_SPLIT, nstate // SSM_SPLIT
    tb = _blk(t, TB_MIX_FWD)

    def body(x_ref, u_ref, bg_ref, cg_ref, v_ref, cgh_ref, vh_ref, sr_ref, si_ref, cre_ref, cim_ref, p_ref,
             cw_ref, gw_ref, h16_ref, h64_ref, wo_ref, vd_ref, y1_ref, o_ref, x2_ref):
        i = pl.program_id(0)
        u = u_ref[...].astype(F32)
        ys = []
        for q in range(SSM_SPLIT):
            rq, cq = slice(q * ds, (q + 1) * ds), slice(q * du, (q + 1) * du)
            ys.append(_dot(sr_ref[:, rq], cre_ref[rq, cq]) - _dot(si_ref[:, rq], cim_ref[rq, cq]))
        ys = jnp.concatenate(ys, axis=1)
        y1 = ys + p_ref[0:1, :] * u
        y1_ref[...] = y1
        z = _gelu(y1)
        q = _dot(z, gw_ref[...]) + p_ref[1:2, :]
        ya = z * _sigmoid(q)
        na = ya * lax.rsqrt(_head_ms(ya, h16_ref) + EPS) * p_ref[2:3, :]
        cv = cg_ref[...].astype(F32) * v_ref[...].astype(F32)
        cvh = jnp.where(i > 0, cgh_ref[...].astype(F32) * vh_ref[...].astype(F32), 0.0)
        yb = bg_ref[...].astype(F32) * _conv3(cv, cvh, cw_ref)
        nb = yb * lax.rsqrt(_head_ms(yb, h64_ref) + EPS) * p_ref[3:4, :]
        o = _dot(na, wo_ref[0:dh, :]) + _dot(nb, wo_ref[dh:2 * dh, :])
        o_ref[...] = o
        on = o * lax.rsqrt(_rowmean(o * o) + EPS) * vd_ref[0:1, :]
        x2_ref[...] = x_ref[...] + vd_ref[1:2, :] * on

    return pl.pallas_call(
        body, name="mix_out", grid=(t // tb,),
        out_shape=(jax.ShapeDtypeStruct((t, dh), F32), jax.ShapeDtypeStruct((t, d), F32),
                   jax.ShapeDtypeStruct((t, d), F32)),
        in_specs=[_rows(tb, d), _rows(tb, dh, 0), _rows(tb, dh, 1), _rows(tb, dh, 2), _rows(tb, dh, 3),
                  _halo_prev(tb, dh, 2, BF16_ROWS), _halo_prev(tb, dh, 3, BF16_ROWS), _rows(tb, nstate), _rows(tb, nstate),
                  _full(c_re.shape), _full(c_im.shape), _full(v512.shape), _full(convw.shape), _full(glu_w.shape),
                  _full(h16.shape), _full(h64.shape), _full(w_out.shape), _full(vd.shape)],
        out_specs=(_rows(tb, dh), _rows(tb, d), _rows(tb, d)),
        compiler_params=_cparams(("parallel",), VMEM_BIG),
    )(x, proj, proj, proj, proj, proj, proj, s_re, s_im, c_re, c_im, v512, convw, glu_w, h16, h64, w_out, vd)


def _ffn_up(x2, vec, w_up_st):
    t, d = x2.shape
    ns, _, nc = w_up_st.shape
    tb = _blk(t, TB_FFN_UP)

    def body(x_ref, vec_ref, w_ref, up_ref, h2_ref):
        xv = x_ref[...]
        r = lax.rsqrt(_rowmean(xv * xv) + EPS)
        h = xv * r * vec_ref[0:1, :] * vec_ref[1:2, :] + vec_ref[2:3, :]
        hb = h.astype(BF16)
        h2_ref[...] = hb
        for j in range(ns):
            up_ref[:, j * nc:(j + 1) * nc] = jnp.dot(hb, w_ref[j], preferred_element_type=F32)

    return pl.pallas_call(
        body, name="ffn_up", grid=(t // tb,),
        out_shape=(jax.ShapeDtypeStruct((t, ns * nc), F32), jax.ShapeDtypeStruct((t, d), BF16)),
        in_specs=[_rows(tb, d), _full((SUBLANES, d)), _resident(w_up_st.shape)],
        out_specs=(_rows(tb, ns * nc), _rows(tb, d)),
        compiler_params=_cparams(("parallel",), VMEM_BIG),
    )(x2, vec, w_up_st)


def _ffn_down(up, fw, w_down, w_down_t, x2, tgt, vd):
    t, nh = up.shape
    dff, d = w_down.shape
    tb = _blk(t, TB_FFN)
    inv_d = 1.0 / d

    def body(up_ref, uph_ref, fw_ref, wd_ref, wdt_ref, x2_ref, tgt_ref, vd_ref,
             act_ref, ddn_ref, dout_ref, dhid_ref, vec_ref, loss_ref, a_s, vv_s, sg_s):
        i = pl.program_id(0)

        def conv_cols(sl):
            x = up_ref[:, sl]
            halo = jnp.where(i > 0, uph_ref[:, sl], 0.0)
            return (fw_ref[0:1, sl] * _shift_down(x, halo, 2) + fw_ref[1:2, sl] * _shift_down(x, halo, 1)
                    + fw_ref[2:3, sl] * x)

        dn = None
        for o in range(0, dff, CW_FFN):
            sl = slice(o, o + CW_FFN)
            a = conv_cols(sl)
            vv = conv_cols(slice(dff + o, dff + o + CW_FFN))
            sg = _sigmoid(a)
            si = a * sg
            a_s[:, sl] = si
            vv_s[:, sl] = vv
            sg_s[:, sl] = sg
            actb = (si * vv).astype(BF16)
            act_ref[:, sl] = actb
            pj = lax.dot_general(actb, wdt_ref[:, sl], (((1,), (1,)), ((), ())), preferred_element_type=F32)
            dn = pj if dn is None else dn + pj
        r3 = lax.rsqrt(_rowmean(dn * dn) + EPS)
        xn = dn * r3
        g = vd_ref[0:1, :]
        gt2 = vd_ref[1:2, :]
        dnn = xn * g
        diff = x2_ref[...] + gt2 * dnn - tgt_ref[...]
        part = 0.5 * inv_d * jnp.sum(diff * diff)

        @pl.when(i == 0)
        def _():
            loss_ref[...] = jnp.zeros(loss_ref.shape, F32)
        loss_ref[...] += part
        dout = diff * inv_d
        dout_ref[...] = dout
        ddnn = dout * gt2
        _acc_rows(vec_ref, i == 0, [_colsum(dout * dnn), _colsum(ddnn * xn)])
        dxn = ddnn * g
        ddn = r3 * (dxn - xn * _rowmean(dxn * xn))
        ddnb = ddn.astype(BF16)
        ddn_ref[...] = ddnb
        for o in range(0, dff, CW_FFN):
            sl = slice(o, o + CW_FFN)
            dact = lax.dot_general(ddnb, wd_ref[sl, :], (((1,), (1,)), ((), ())), preferred_element_type=F32)
            si, vv, sg = a_s[:, sl], vv_s[:, sl], sg_s[:, sl]
            dhid_ref[:, sl] = (dact * vv * (sg + si * (1.0 - sg))).astype(BF16)
            dhid_ref[:, dff + o:dff + o + CW_FFN] = (dact * si).astype(BF16)

    return pl.pallas_call(
        body, name="ffn_down", grid=(t // tb,),
        scratch_shapes=[pltpu.VMEM((tb, dff), F32)] * 3,
        out_shape=(jax.ShapeDtypeStruct((t, dff), BF16), jax.ShapeDtypeStruct((t, d), BF16),
                   jax.ShapeDtypeStruct((t, d), F32), jax.ShapeDtypeStruct((t, nh), BF16),
                   jax.ShapeDtypeStruct((SUBLANES, d), F32), jax.ShapeDtypeStruct((SUBLANES, 128), F32)),
        in_specs=[_rows(tb, nh), _halo_prev(tb, nh), _full(fw.shape), _resident(w_down.shape),
                  _resident(w_down_t.shape), _rows(tb, d),
                  _rows(tb, d), _full(vd.shape)],
        out_specs=(_rows(tb, dff), _rows(tb, d), _rows(tb, d), _rows(tb, nh), _full((SUBLANES, d)),
                   _full((SUBLANES, 128))),
        compiler_params=_cparams(("arbitrary",), VMEM_BIG),
    )(up, up, fw, w_down, w_down_t, x2, tgt, vd)


def _ffn_up_bwd(dhid, up, fw, x2, dout, vec, w_up_st):
    t, nh = dhid.shape
    d = x2.shape[1]
    ns, _, nc = w_up_st.shape
    tb = _blk(t, TB_FFN)
    nblk = t // tb
    cw = 128

    def body(dh_ref, dhn_ref, up_ref, fw_ref, x2_ref, dout_ref, vec_ref, w_ref,
             dx2_ref, dup_ref, vp_ref, df_ref):
        i = pl.program_id(0)

        @pl.when(i == 0)
        def _():
            df_ref[...] = jnp.zeros(df_ref.shape, F32)
        dh2 = None
        for j in range(ns):
            for o in range(j * nc, (j + 1) * nc, cw):
                sl = slice(o, o + cw)
                dh = dh_ref[:, sl].astype(F32)
                dhn = jnp.where(i < nblk - 1, dhn_ref[:, sl].astype(F32), 0.0)
                dh1 = _shift_up(dh, dhn, 1)
                dh2s = _shift_up(dh, dhn, 2)
                dup_ref[:, sl] = (fw_ref[2:3, sl] * dh + fw_ref[1:2, sl] * dh1 + fw_ref[0:1, sl] * dh2s).astype(BF16)
                up_v = up_ref[:, sl]
                df_ref[0:1, sl] += _colsum(dh2s * up_v)
                df_ref[1:2, sl] += _colsum(dh1 * up_v)
                df_ref[2:3, sl] += _colsum(dh * up_v)
            pj = lax.dot_general(dup_ref[:, j * nc:(j + 1) * nc], w_ref[j], (((1,), (1,)), ((), ())),
                                 preferred_element_type=F32)
            dh2 = pj if dh2 is None else dh2 + pj
        xv = x2_ref[...]
        r = lax.rsqrt(_rowmean(xv * xv) + EPS)
        xn = xv * r
        g = vec_ref[0:1, :]
        hg = xn * g
        dhg = dh2 * vec_ref[1:2, :]
        _acc_rows(vp_ref, i == 0, [_colsum(dh2), _colsum(dh2 * hg), _colsum(dhg * xn)])
        dxn = dhg * g
        dx2_ref[...] = dout_ref[...] + r * (dxn - xn * _rowmean(dxn * xn))

    return pl.pallas_call(
        body, name="ffn_up_bwd", grid=(nblk,),
        out_shape=(jax.ShapeDtypeStruct((t, d), F32), jax.ShapeDtypeStruct((t, nh), BF16),
                   jax.ShapeDtypeStruct((SUBLANES, d), F32), jax.ShapeDtypeStruct((SUBLANES, nh), F32)),
        in_specs=[_rows(tb, nh), _halo_next(tb, nh, t, rows=BF16_ROWS), _rows(tb, nh), _full(fw.shape),
                  _rows(tb, d), _rows(tb, d), _full(vec.shape), _resident(w_up_st.shape)],
        out_specs=(_rows(tb, d), _rows(tb, nh), _full((SUBLANES, d)), _full((SUBLANES, nh))),
        compiler_params=_cparams(("arbitrary",), VMEM_BIG),
    )(dhid, dhid, up, fw, x2, dout, vec, w_up_st)


def _mix_out_bwd(dx2, o, y1, proj, s_re, s_im, c_re, c_im, v512, convw, glu_w, h16, h64, w_out, vd):
    t, d = dx2.shape
    dh = y1.shape[1]
    nstate = c_re.shape[0]
    du, ds = dh // SSM_SPLIT, nstate // SSM_SPLIT
    tb = _blk(t, TB_MIX)

    def body(dx2_ref, o_ref, y1_ref, u_ref, bg_ref, cg_ref, v_ref, cgh_ref, vh_ref, cre_ref, cim_ref, p_ref,
             cw_ref, gw_ref, h16_ref, h64_ref, wo_ref, vd_ref, sr_ref, si_ref,
             do_ref, ycat_ref, z_ref, dq_ref, dy1_ref, gr_ref, gi_ref, dcc_ref, dbg_ref, vpd_ref, vp5_ref,
             dcr_ref, dci_ref):
        i = pl.program_id(0)
        first = i == 0

        @pl.when(first)
        def _():
            dcr_ref[...] = jnp.zeros(dcr_ref.shape, F32)
            dci_ref[...] = jnp.zeros(dci_ref.shape, F32)
        ov = o_ref[...]
        ro = lax.rsqrt(_rowmean(ov * ov) + EPS)
        on_ = ov * ro
        g = vd_ref[0:1, :]
        dx2v = dx2_ref[...]
        don = dx2v * vd_ref[1:2, :]
        _acc_rows(vpd_ref, first, [_colsum(dx2v * on_ * g), _colsum(don * on_)])
        dxn = don * g
        dob = (ro * (dxn - on_ * _rowmean(dxn * on_))).astype(BF16)
        do_ref[...] = dob
        dyc_a =lax.dot_general(dob, wo_ref[0:dh, :], (((1,), (1,)), ((), ())), preferred_element_type=F32)
        dyc_b = lax.dot_general(dob, wo_ref[dh:2 * dh, :], (((1,), (1,)), ((), ())), preferred_element_type=F32)
        y1v = y1_ref[...]
        u = u_ref[...].astype(F32)
        z, dz_dy1 = _gelu_and_grad(y1v)
        zb = z.astype(BF16)
        sg = _sigmoid(jnp.dot(zb, gw_ref[...], preferred_element_type=F32) + p_ref[1:2, :])
        ya = z * sg
        ra = lax.rsqrt(_head_ms(ya, h16_ref) + EPS)
        yan = ya * ra
        ga = p_ref[2:3, :]
        ycat_ref[:, 0:dh] = (yan * ga).astype(BF16)
        dyn = dyc_a * ga
        dya = ra * (dyn - yan * _split_dot(dyn * yan, h16_ref[...]))
        dq = dya * z * sg * (1.0 - sg)
        dqb = dq.astype(BF16)
        z_ref[...] = zb
        dq_ref[...] = dqb
        dz = dya * sg + lax.dot_general(dqb, gw_ref[...], (((1,), (1,)), ((), ())), preferred_element_type=F32)
        dy1 = dz * dz_dy1
        dy1_ref[...] = dy1
        dy1b = dy1.astype(BF16)
        for q in range(SSM_SPLIT):
            rq, cq = slice(q * ds, (q + 1) * ds), slice(q * du, (q + 1) * du)
            gr_ref[:, rq] = lax.dot_general(dy1b[:, cq], cre_ref[rq, cq], (((1,), (1,)), ((), ())),
                                            preferred_element_type=F32).astype(BF16)
            gi_ref[:, rq] = (-lax.dot_general(dy1b[:, cq], cim_ref[rq, cq], (((1,), (1,)), ((), ())),
                                              preferred_element_type=F32)).astype(BF16)
            dcr_ref[rq, :] += _dot_tn(sr_ref[:, rq], dy1b[:, cq])
            dci_ref[rq, :] += _dot_tn(si_ref[:, rq], dy1b[:, cq])
        bg = bg_ref[...].astype(F32)
        cv = cg_ref[...].astype(F32) * v_ref[...].astype(F32)
        cvh = jnp.where(i > 0, cgh_ref[...].astype(F32) * vh_ref[...].astype(F32), 0.0)
        cv1 = _shift_down(cv, cvh, 1)
        cv2 = _shift_down(cv, cvh, 2)
        cc = cw_ref[0:1, :] * cv2 + cw_ref[1:2, :] * cv1 + cw_ref[2:3, :] * cv
        yb = bg * cc
        rb = lax.rsqrt(_head_ms(yb, h64_ref) + EPS)
        ybn = yb * rb
        gb = p_ref[3:4, :]
        ycat_ref[:, dh:2 * dh] = (ybn * gb).astype(BF16)
        dynb = dyc_b * gb
        dyb = rb * (dynb - ybn * _split_dot(dynb * ybn, h64_ref[...]))
        dcc = dyb * bg
        dbg_ref[...] = dyb * cc
        dcc_ref[...] = dcc
        _acc_rows(vp5_ref, first, [_colsum(dyc_a * yan), _colsum(dyc_b * ybn), _colsum(dq), _colsum(dy1 * u),
                                   _colsum(dcc * cv2), _colsum(dcc * cv1), _colsum(dcc * cv)])

    return pl.pallas_call(
        body, name="mix_out_bwd", grid=(t // tb,),
        out_shape=(jax.ShapeDtypeStruct((t, d), BF16), jax.ShapeDtypeStruct((t, 2 * dh), BF16),
                   jax.ShapeDtypeStruct((t, dh), BF16), jax.ShapeDtypeStruct((t, dh), BF16),
                   jax.ShapeDtypeStruct((t, dh), F32), jax.ShapeDtypeStruct((t, nstate), BF16),
                   jax.ShapeDtypeStruct((t, nstate), BF16), jax.ShapeDtypeStruct((t, dh), F32),
                   jax.ShapeDtypeStruct((t, dh), F32), jax.ShapeDtypeStruct((SUBLANES, d), F32),
                   jax.ShapeDtypeStruct((SUBLANES, dh), F32), jax.ShapeDtypeStruct((nstate, du), F32),
                   jax.ShapeDtypeStruct((nstate, du), F32)),
        in_specs=[_rows(tb, d), _rows(tb, d), _rows(tb, dh), _rows(tb, dh, 0), _rows(tb, dh, 1), _rows(tb, dh, 2),
                  _rows(tb, dh, 3), _halo_prev(tb, dh, 2, BF16_ROWS), _halo_prev(tb, dh, 3, BF16_ROWS), _resident(c_re.shape),
                  _resident(c_im.shape), _full(v512.shape), _full(convw.shape), _resident(glu_w.shape),
                  _resident(h16.shape), _resident(h64.shape), _resident(w_out.shape), _full(vd.shape),
                  _rows(tb, nstate), _rows(tb, nstate)],
        out_specs=(_rows(tb, d), _rows(tb, 2 * dh), _rows(tb, dh), _rows(tb, dh), _rows(tb, dh), _rows(tb, nstate),
                   _rows(tb, nstate), _rows(tb, dh), _rows(tb, dh), _full((SUBLANES, d)), _full((SUBLANES, dh)),
                   _full((nstate, du)), _full((nstate, du))),
        compiler_params=_cparams(("arbitrary",), VMEM_BIG),
    )(dx2, o, y1, proj, proj, proj, proj, proj, proj, c_re, c_im, v512, convw, glu_w, h16, h64, w_out, vd,
      s_re, s_im)


def _mix_in_bwd(gt_re, gt_im, b_re, b_im, dy1, dcc, dbg, proj, x, dx2, vec, v512, convw, w_in_st):
    t, d = x.shape
    dh = dy1.shape[1]
    nstate = gt_re.shape[1]
    du_w, ds = dh // SSM_SPLIT, nstate // SSM_SPLIT
    ns, _, nc = w_in_st.shape
    tb = _blk(t, TB_MIX)
    nblk = t // tb

    def body(gr_ref, gi_ref, bre_ref, bim_ref, dy1_ref, dcc_ref, dccn_ref, dbg_ref, u_ref, cg_ref, v_ref, x_ref,
             dx2_ref, vec_ref, p_ref, cw_ref, w_ref, gx_ref, dproj_ref, vp_ref, dbr_ref, dbi_ref):
        i = pl.program_id(0)

        @pl.when(i == 0)
        def _():
            dbr_ref[...] = jnp.zeros(dbr_ref.shape, F32)
            dbi_ref[...] = jnp.zeros(dbi_ref.shape, F32)
        ub = u_ref[...].astype(BF16)
        du = []
        for q in range(SSM_SPLIT):
            rq, cq = slice(q * du_w, (q + 1) * du_w), slice(q * ds, (q + 1) * ds)
            du.append(lax.dot_general(gr_ref[:, cq].astype(BF16), bre_ref[rq, cq], (((1,), (1,)), ((), ())),
                                      preferred_element_type=F32)
                      + lax.dot_general(gi_ref[:, cq].astype(BF16), bim_ref[rq, cq], (((1,), (1,)), ((), ())),
                                        preferred_element_type=F32))
            dbr_ref[rq, :] += _dot_tn(ub[:, rq], gr_ref[:, cq])
            dbi_ref[rq, :] += _dot_tn(ub[:, rq], gi_ref[:, cq])
        du = dy1_ref[...] * p_ref[0:1, :] + jnp.concatenate(du, axis=1)
        dcc = dcc_ref[...]
        dccn = jnp.where(i < nblk - 1, dccn_ref[...], 0.0)
        dcv = (cw_ref[2:3, :] * dcc + cw_ref[1:2, :] * _shift_up(dcc, dccn, 1)
               + cw_ref[0:1, :] * _shift_up(dcc, dccn, 2))
        parts = [du, dbg_ref[...], dcv * v_ref[...].astype(F32), dcv * cg_ref[...].astype(F32)]
        xv = x_ref[...]
        r = lax.rsqrt(_rowmean(xv * xv) + EPS)
        xn = xv * r
        g = vec_ref[0:1, :]
        hg = xn * g
        dh1 = None
        for j in range(ns):
            pb = parts[j].astype(BF16)
            dproj_ref[:, j * nc:(j + 1) * nc] = pb
            pj =lax.dot_general(pb, w_ref[j], (((1,), (1,)), ((), ())), preferred_element_type=F32)
            dh1 = pj if dh1 is None else dh1 + pj
        dhg = dh1 * vec_ref[1:2, :]
        _acc_rows(vp_ref, i == 0, [_colsum(dh1), _colsum(dh1 * hg), _colsum(dhg * xn)])
        dxn = dhg * g
        gx_ref[...] = dx2_ref[...] + r * (dxn - xn * _rowmean(dxn * xn))

    assert nc == dh and ns == 4
    return pl.pallas_call(
        body, name="mix_in_bwd", grid=(nblk,),
        out_shape=(jax.ShapeDtypeStruct((t, d), F32), jax.ShapeDtypeStruct((t, ns * nc), BF16),
                   jax.ShapeDtypeStruct((SUBLANES, d), F32), jax.ShapeDtypeStruct((dh, ds), F32),
                   jax.ShapeDtypeStruct((dh, ds), F32)),
        in_specs=[_rows(tb, nstate), _rows(tb, nstate), _resident(b_re.shape), _resident(b_im.shape), _rows(tb, dh),
                  _rows(tb, dh), _halo_next(tb, dh, t), _rows(tb, dh), _rows(tb, dh, 0), _rows(tb, dh, 2),
                  _rows(tb, dh, 3), _rows(tb, d), _rows(tb, d), _full(vec.shape), _full(v512.shape),
                  _full(convw.shape), _resident(w_in_st.shape)],
        out_specs=(_rows(tb, d), _rows(tb, ns * nc), _full((SUBLANES, d)), _full((dh, ds)), _full((dh, ds))),
        compiler_params=_cparams(("arbitrary",), VMEM_BIG),
    )(gt_re, gt_im, b_re, b_im, dy1, dcc, dcc, dbg, proj, proj, proj, x, dx2, vec, v512, convw, w_in_st)


def _matmul_tn(a, b, m, bn, out_dtype, name, diag=False, bt=TB_TN, after=None):
    t = a.shape[0]
    n = b.shape[1]
    bt = _blk(t, bt)
    nk = t // bt
    extra = [] if after is None else [after]
    a_map = (lambda j, k: (k, j)) if diag else (lambda j, k: (k, 0))

    def body(a_ref, b_ref, *rest):
        o_ref, acc_ref = rest[-2:]
        k = pl.program_id(1)

        @pl.when(k == 0)
        def _():
            acc_ref[...] = jnp.zeros(acc_ref.shape, F32)
        acc_ref[...] += _dot_tn(a_ref[...], b_ref[...])

        @pl.when(k == nk - 1)
        def _():
            o_ref[...] = acc_ref[...].astype(out_dtype)

    return pl.pallas_call(
        body, name=name, grid=(n // bn, nk),
        out_shape=jax.ShapeDtypeStruct((n // bn, m, bn), out_dtype),
        in_specs=[pl.BlockSpec((bt, m), a_map), pl.BlockSpec((bt, bn), lambda j, k: (k, j))]
        + [pl.BlockSpec(memory_space=pl.ANY)] * len(extra),
        out_specs=pl.BlockSpec((None, m, bn), lambda j, k: (j, 0, 0)),
        scratch_shapes=[pltpu.VMEM((m, bn), F32)],
        compiler_params=_cparams(("parallel", "arbitrary"), VMEM_BIG),
    )(a, b, *extra)


def _ssm_bgrad(d_bre, d_bim, bt_re, bt_im, rows_in, fold, tile_b):
    gh, cb = d_bre.shape
    nb = SSM_SPLIT
    rb = gh // nb
    gp = nb * cb
    p = fold.shape[1]

    def body(dr_ref, di_ref, br_ref, bi_ref, rin_ref, f_ref, tb_ref, dbr_ref, dbi_ref, rout_ref):
        row = lax.broadcasted_iota(jnp.int32, (rb, cb), 0)
        col = lax.broadcasted_iota(jnp.int32, (rb, cb), 1)
        mask = (row >> 4) == (col >> 6)
        gr = jnp.where(mask, dr_ref[...], 0.0)
        gi = jnp.where(mask, di_ref[...], 0.0)
        cr, ci = rin_ref[0:1, :], rin_ref[1:2, :]
        dbr_ref[...] = _split3_dot(cr * gr + ci * gi, f_ref[...])
        dbi_ref[...] = _split3_dot(cr * gi - ci * gr, f_ref[...])
        br = _split3_dot(br_ref[...], tb_ref[...])
        bi = _split3_dot(bi_ref[...], tb_ref[...])
        rout_ref[...] = jnp.zeros(rout_ref.shape, F32)
        rout_ref[0:1, :] = _colsum(br * gr + bi * gi)
        rout_ref[1:2, :] = _colsum(br * gi - bi * gr)

    dspec = pl.BlockSpec((rb, cb), lambda j: (j, 0))
    rspec = pl.BlockSpec((SUBLANES, cb), lambda j: (0, j))
    ospec = pl.BlockSpec((rb, p), lambda j: (j, 0))
    return pl.pallas_call(
        body, name="ssm_bgrad", grid=(nb,),
        out_shape=(jax.ShapeDtypeStruct((gh, p), F32), jax.ShapeDtypeStruct((gh, p), F32),
                   jax.ShapeDtypeStruct((SUBLANES, gp), F32)),
        in_specs=[dspec, dspec, ospec, ospec, rspec, _full(fold.shape), _full(tile_b.shape)],
        out_specs=(ospec, ospec, rspec),
        compiler_params=_cparams(("parallel",)),
    )(d_bre, d_bim, bt_re, bt_im, rows_in, fold, tile_b)


def _ssm_cgrad(d_cre, d_cim, fold):
    gp, cb = d_cre.shape
    nb = SSM_SPLIT
    rb = gp // nb
    h = fold.shape[1]

    def body(dr_ref, di_ref, f_ref, cr_ref, ci_ref):
        row = lax.broadcasted_iota(jnp.int32, (rb, cb), 0)
        col = lax.broadcasted_iota(jnp.int32, (rb, cb), 1)
        mask = (row >> 6) == (col >> 4)
        cr_ref[...] = _split3_dot(jnp.where(mask, dr_ref[...], 0.0), f_ref[...])
        ci_ref[...] = -_split3_dot(jnp.where(mask, di_ref[...], 0.0), f_ref[...])

    cspec = pl.BlockSpec((rb, cb), lambda j: (j, 0))
    ospec = pl.BlockSpec((rb, h), lambda j: (j, 0))
    return pl.pallas_call(
        body, name="ssm_cgrad", grid=(nb,),
        out_shape=(jax.ShapeDtypeStruct((gp, h), F32),) * 2,
        in_specs=[cspec, cspec, _full(fold.shape)], out_specs=(ospec, ospec),
        compiler_params=_cparams(("parallel",)),
    )(d_cre, d_cim, fold)


def _ssm_lamgrad(lam_re, lam_im, log_step, abar_re, abar_im, coef_re, coef_im, gc_re, gc_im, ga_re, ga_im):
    g, p = lam_re.shape

    def body(lr_ref, li_ref, ls_ref, ar_ref, ai_ref, cr_ref, ci_ref, gcr_ref, gci_ref, gar_ref, gai_ref,
             dlr_ref, dli_ref, dls_ref):
        lam_raw = lr_ref[...]
        lr = jnp.minimum(lam_raw, LAMBDA_RE_MAX)
        li = li_ref[...]
        st = jnp.exp(ls_ref[...])
        den = lr * lr + li * li
        gcr, gci = gcr_ref[...], gci_ref[...]
        gab_r = gar_ref[...] + (lr * gcr - li * gci) / den
        gab_i = gai_ref[...] + (lr * gci + li * gcr) / den
        cr, ci = cr_ref[...], ci_ref[...]
        wr = -(cr * lr + ci * li) / den
        wi = -(ci * lr - cr * li) / den
        gl_r = wr * gcr + wi * gci
        gl_i = wr * gci - wi * gcr
        ar, ai = ar_ref[...], ai_ref[...]
        gw_r = ar * gab_r + ai * gab_i
        gw_i = ar * gab_i - ai * gab_r
        gl_r = gl_r + st * gw_r
        gl_i = gl_i + st * gw_i
        pass_through = jnp.where(lam_raw < LAMBDA_RE_MAX, 1.0, jnp.where(lam_raw == LAMBDA_RE_MAX, 0.5, 0.0))
        dlr_ref[...] = gl_r * pass_through
        dli_ref[...] = gl_i
        dls_ref[...] = st * jnp.sum(lr * gw_r + li * gw_i, axis=1, keepdims=True)

    sds = jax.ShapeDtypeStruct((g, p), F32)
    return pl.pallas_call(body, name="ssm_lamgrad", out_shape=(sds, sds, jax.ShapeDtypeStruct((g, 1), F32)))(
        lam_re, lam_im, log_step, abar_re, abar_im, coef_re, coef_im, gc_re, gc_im, ga_re, ga_im)


def _row_block(r, most=512):
    for rb in range(min(r, most), BF16_ROWS - 1, -1):
        if r % rb == 0 and rb % BF16_ROWS == 0:
            return rb
    return r


def _adamw_math(w, g, m, v):
    m = ADAM_B1 * m + (1.0 - ADAM_B1) * g
    v = ADAM_B2 * v + (1.0 - ADAM_B2) * (g * g)
    m_hat = m / (1.0 - ADAM_B1 ** ADAM_STEP)
    v_hat = v / (1.0 - ADAM_B2 ** ADAM_STEP)
    delta = -ADAM_LR * (m_hat / (jnp.sqrt(v_hat) + ADAM_EPS) + ADAM_WD * w)
    return delta, m, v


def _adamw_big(p_mine, p_sib, w, m, v, name):
    r, c = w.shape
    rb = _row_block(r)

    def body(a_ref, b_ref, w_ref, m_ref, v_ref, g_ref, d_ref, mo_ref, vo_ref):
        g = a_ref[...].astype(F32) + b_ref[...].astype(F32)
        g_ref[...] = g
        d_ref[...], mo_ref[...], vo_ref[...] = _adamw_math(w_ref[...], g, m_ref[...], v_ref[...])

    spec = pl.BlockSpec((rb, c), lambda i: (i, 0))
    sds = jax.ShapeDtypeStruct((r, c), F32)
    return pl.pallas_call(
        body, name=name, grid=(r // rb,), out_shape=(sds,) * 4, in_specs=[spec] * 5, out_specs=(spec,) * 4,
        compiler_params=_cparams(("parallel",), VMEM_KEEP_OPERANDS_IN_HBM),
    )(p_mine, p_sib, w, m, v)


def _sum_blocks(stack, name):
    n, r, c = stack.shape
    rb = _row_block(r)

    def body(s_ref, o_ref):
        acc = s_ref[0].astype(F32)
        for k in range(1, n):
            acc = acc + s_ref[k].astype(F32)
        o_ref[...] = acc

    return pl.pallas_call(
        body, name=name, grid=(r // rb,), out_shape=jax.ShapeDtypeStruct((r, c), F32),
        in_specs=[pl.BlockSpec((n, rb, c), lambda i: (0, i, 0))], out_specs=pl.BlockSpec((rb, c), lambda i: (i, 0)),
        compiler_params=_cparams(("parallel",), VMEM_KEEP_OPERANDS_IN_HBM),
    )(stack)


def _sum_landed(landed, own, chip, name):
    n, r, c = landed.shape
    rb = _row_block(r)

    def body(chip_ref, own_ref, l1_ref, l2_ref, l3_ref, o_ref):
        acc = own_ref[0].astype(F32)
        for ref in (l1_ref, l2_ref, l3_ref):
            acc = acc + ref[0].astype(F32)
        o_ref[...] = acc.astype(BF16)

    def slot(k):
        return pl.BlockSpec((1, rb, c), lambda i, ch: ((ch[0] + k) % n, i, 0))

    return pl.pallas_call(
        body, name=name, out_shape=jax.ShapeDtypeStruct((r, c), BF16),
        grid_spec=pltpu.PrefetchScalarGridSpec(
            num_scalar_prefetch=1, grid=(r // rb,), in_specs=[slot(0), slot(1), slot(2), slot(3)],
            out_specs=pl.BlockSpec((rb, c), lambda i, ch: (i, 0))),
        compiler_params=_cparams(("parallel",), VMEM_KEEP_OPERANDS_IN_HBM),
    )(jnp.reshape(chip, (1,)).astype(jnp.int32), own, landed, landed, landed)


def _add2(a, b):
    def body(a_ref, b_ref, o_ref):
        o_ref[...] = a_ref[...] + b_ref[...]

    return pl.pallas_call(body, name="add_small", out_shape=jax.ShapeDtypeStruct(a.shape, F32))(a, b)


def _adamw_ada(c_all, dmod_cols, w, m, v):
    d, n = w.shape
    bn = 512

    def body(c_ref, dm_ref, w_ref, m_ref, v_ref, g_ref, d_ref, mo_ref, vo_ref):
        cc = c_ref[...]
        g = _dot_tn(cc * _sigmoid(cc), dm_ref[...])
        g_ref[...] = g
        d_ref[...], mo_ref[...], vo_ref[...] = _adamw_math(w_ref[...], g, m_ref[...], v_ref[...])

    spec = pl.BlockSpec((d, bn), lambda j: (0, j))
    sds = jax.ShapeDtypeStruct((d, n), F32)
    return pl.pallas_call(
        body, name="adamw_ada", grid=(n // bn,), out_shape=(sds,) * 4,
        in_specs=[_full((N_DEV, d)), pl.BlockSpec((N_DEV, bn), lambda j: (0, j)), spec, spec, spec],
        out_specs=(spec,) * 4, compiler_params=_cparams(("parallel",), VMEM_KEEP_OPERANDS_IN_HBM),
    )(c_all, dmod_cols, w, m, v)


def _adamw_small(items):
    n = len(items)

    def body(*refs):
        ins, outs = refs[:4 * n], refs[4 * n:]
        for k in range(n):
            w_ref, g_ref, m_ref, v_ref = ins[4 * k:4 * k + 4]
            outs[3 * k][...], outs[3 * k + 1][...], outs[3 * k + 2][...] = _adamw_math(
                w_ref[...], g_ref[...], m_ref[...], v_ref[...])

    flat = [a for it in items for a in it]
    out_shape = tuple(jax.ShapeDtypeStruct(it[0].shape, F32) for it in items for _ in range(3))
    res = pl.pallas_call(body, name="adamw_small", out_shape=out_shape,
                         compiler_params=_cparams(vmem=VMEM_KEEP_OPERANDS_IN_HBM))(*flat)
    return [tuple(res[3 * k:3 * k + 3]) for k in range(n)]


def _group_mean_matrix(n, group):
    idx = np.arange(n) // group
    return (idx[:, None] == idx[None, :]).astype(np.float32) / group


def _fold_matrix(n, period):
    return (np.arange(n)[:, None] % period == np.arange(period)[None, :]).astype(np.float32)


def _rows8(*rows):
    c = rows[0].shape[-1]
    pad = jnp.zeros((SUBLANES - len(rows), c), F32)
    return jnp.concatenate([r.reshape(1, c) for r in rows] + [pad], axis=0)


def _to_rows(a, width):
    flat = a.reshape(-1)
    n = -(-flat.shape[0] // width)
    flat = jnp.pad(flat, (0, n * width - flat.shape[0]))
    return flat.reshape(n, width)


def kernel(x, c, w_ada, b_ada, g_pre_mix, g_post_mix, w_in, ssm_lam_re, ssm_lam_im, ssm_log_step, ssm_b_re, ssm_b_im, ssm_c_re, ssm_c_im, ssm_d, glu_w, glu_b, g_out_ssm, conv_w, g_out_conv, w_out, g_pre_ffn, g_post_ffn, w_up, ffn_conv_w, w_down, loss_target, m_w_ada, m_b_ada, m_g_pre_mix, m_g_post_mix, m_w_in, m_ssm_lam_re, m_ssm_lam_im, m_ssm_log_step, m_ssm_b_re, m_ssm_b_im, m_ssm_c_re, m_ssm_c_im, m_ssm_d, m_glu_w, m_glu_b, m_g_out_ssm, m_conv_w, m_g_out_conv, m_w_out, m_g_pre_ffn, m_g_post_ffn, m_w_up, m_ffn_conv_w, m_w_down, v_w_ada, v_b_ada, v_g_pre_mix, v_g_post_mix, v_w_in, v_ssm_lam_re, v_ssm_lam_im, v_ssm_log_step, v_ssm_b_re, v_ssm_b_im, v_ssm_c_re, v_ssm_c_im, v_ssm_d, v_glu_w, v_glu_b, v_g_out_ssm, v_conv_w, v_g_out_conv, v_w_out, v_g_pre_ffn, v_g_post_ffn, v_w_up, v_ffn_conv_w, v_w_down):
    xs = x[0]
    tgt = loss_target[0]
    t, d = xs.shape
    xi, yi, ci = lax.axis_index("x"), lax.axis_index("y"), lax.axis_index("c")
    chip = 2 * xi + yi
    dev = 2 * chip + ci

    n_groups, n_state = ssm_lam_re.shape[1:]
    n_gch = ssm_b_re.shape[3]
    d_ssm = n_groups * n_gch
    gp = n_groups * n_state
    n_ada = w_ada.shape[2]
    d_ff = w_down.shape[1] * N_CHIPS
    n_upc = w_up.shape[2]

    w_names = ("w_in", "glu_w", "w_out", "w_up", "w_down")
    c_gath = _allgather8(jnp.broadcast_to(c, (SUBLANES, d)), "gather_c")
    c_all = c_gath.reshape(N_DEV, SUBLANES, d)[:, 0, :]

    def pad8(a):
        return jnp.concatenate([a, jnp.zeros((SUBLANES - a.shape[0], a.shape[1]), a.dtype)], axis=0)

    def start(name, arrs, after):
        return _chips_start(name, True, [], [_landing(a, chip) for a in arrs], after)

    w_names = ("w_in", "mod", "conv_w", "ffn_conv_w", "glu_w", "w_out", "w_up", "w_down")
    first = start("weights_start_in", [w_in[0].astype(BF16)], c_gath)
    b_sh = lax.dynamic_slice(b_ada, (0, chip * n_ada), (1, n_ada))
    mod_sh = _mod_shard(c_all + first[4][0:1, 0:1], w_ada[0], b_sh)
    second = start("weights_start_rest", [mod_sh, pad8(conv_w[0]), pad8(ffn_conv_w[0])]
                   + [w[0].astype(BF16) for w in (glu_w, w_out, w_up, w_down)], None)
    w_send, w_recv, w_land = [list(first[k]) + list(second[k]) for k in (0, 1, 3)]
    w_token = second[4]

    def weights(names, after):
        ks = [w_names.index(nm) for nm in names]
        return _chips_wait("weights_wait_" + names[-1], True, [w_send[k] for k in ks], [w_recv[k] for k in ks],
                           [], [w_land[k] for k in ks], after)[1]

    lam_re, lam_im = ssm_lam_re[0], ssm_lam_im[0]
    log_step = ssm_log_step[0].reshape(n_groups, 1) + w_token[0:1, 0:1]
    abar_re, abar_im, coef_re, coef_im = _ssm_prep(lam_re, lam_im, log_step)
    a_rows = _rows8(abar_re.reshape(1, gp), abar_im.reshape(1, gp))
    coef_rows = _rows8(coef_re.reshape(1, gp), coef_im.reshape(1, gp))
    bt_re = ssm_b_re[0].transpose(0, 2, 1).reshape(d_ssm, n_state)
    bt_im = ssm_b_im[0].transpose(0, 2, 1).reshape(d_ssm, n_state)
    ct_re = ssm_c_re[0].transpose(0, 2, 1).reshape(gp, n_gch)
    ct_im = ssm_c_im[0].transpose(0, 2, 1).reshape(gp, n_gch)
    tile_b = jnp.asarray(np.tile(np.eye(n_state), (1, n_groups // SSM_SPLIT)), BF16)
    tile_c = jnp.asarray(np.tile(np.eye(n_gch), (1, n_groups)), BF16)
    bblk_re, bblk_im, cblk_re, cblk_im = _ssm_blocks(bt_re, bt_im, ct_re, ct_im, coef_rows, tile_b, tile_c)

    h16 = jnp.asarray(_group_mean_matrix(d_ssm, n_gch), BF16)
    h64 = jnp.asarray(_group_mean_matrix(d_ssm, CONV_HEAD_DIM), BF16)

    g_mod, g_cw, g_fw, w_in_st = weights(("mod", "conv_w", "ffn_conv_w", "w_in"), bblk_re)
    mod_all = g_mod.transpose(1, 0, 2).reshape(N_DEV, N_CHIPS * n_ada)
    mod = lax.dynamic_slice(mod_all, (dev, 0), (1, N_CHIPS * n_ada))
    sh1, sc1, gt1, sh2, sc2, gt2 = [mod[:, k * d:(k + 1) * d] for k in range(6)]
    convw_full = pad8(g_cw[:, :3, :].transpose(1, 0, 2).reshape(3, d_ssm))
    fw_full = pad8(g_fw[:, :3, :].transpose(1, 0, 2).reshape(3, N_CHIPS * n_upc))

    v512 = _rows8(ssm_d, glu_b, g_out_ssm, g_out_conv)
    vec1 =_rows8(g_pre_mix, 1.0 + sc1, sh1)
    vd1 = _rows8(g_post_mix, gt1)
    vec2 = _rows8(g_pre_ffn, 1.0 + sc2, sh2)
    vd2 = _rows8(g_post_ffn, gt2)

    proj, bu_re, bu_im, h1b = _mix_in(xs, vec1, w_in_st, bblk_re, bblk_im)
    s_re, s_im = _scan_fwd(a_rows, bu_re, bu_im)
    g_glu, g_wout = weights(("glu_w", "w_out"), s_re)
    glu_full = g_glu.reshape(d_ssm, d_ssm)
    w_out_full = g_wout.reshape(2 * d_ssm, d)
    y1, o_mix, x2 = _mix_out(xs, proj, s_re, s_im, cblk_re, cblk_im, v512, convw_full, glu_full, h16, h64,
                             w_out_full, vd1)
    (w_up_st,) = weights(("w_up",), x2)
    up, h2b = _ffn_up(x2, vec2, w_up_st)
    (g_wdown,) = weights(("w_down",), up)
    w_down_full = g_wdown.reshape(d_ff, d)
    actb, ddnb, dout, dhid, vp_dn, loss_blk = _ffn_down(up, fw_full, w_down_full, w_down_full.T, x2, tgt, vd2)

    gw_down = _matmul_tn(actb, ddnb, d_ff, d, BF16, "dw_down", bt=1024).reshape(N_CHIPS, d_ff // N_CHIPS, d)
    dx2, dupb, vp_up, df_rows = _ffn_up_bwd(dhid, up, fw_full, x2, dout, vec2, w_up_st)
    gw_up = _matmul_tn(h2b, dupb, d, n_upc, BF16, "dw_up", bt=4096)
    ga_send, ga_recv, ga_src, ga_land, ga_token = _chips_start(
        "grads_start_ffn", False, [gw_down, gw_up], [lax.empty(g.shape, g.dtype) for g in (gw_down, gw_up)])
    (dob, ycatb, zb, dqb, dy1, g_re, g_im, dcc, dbg, vp_mo, vp5, d_cre, d_cim) = _mix_out_bwd(
        dx2, o_mix, y1, proj, s_re, s_im, cblk_re, cblk_im, v512, convw_full, glu_full, h16, h64, w_out_full,
        vd1 + ga_token[0:1, 0:1])
    gw_out = _matmul_tn(ycatb, dob, 2 * d_ssm, d, BF16, "dw_out", bt=4096)
    gw_out = gw_out.reshape(N_CHIPS, 2 * d_ssm // N_CHIPS, d)
    gw_glu = _matmul_tn(zb, dqb, d_ssm, d_ssm, BF16, "dw_glu", bt=4096).reshape(N_CHIPS, d_ssm // N_CHIPS, d_ssm)
    gb_send, gb_recv, gb_src, gb_land, gb_token = _chips_start(
        "grads_start_mix", False, [gw_out, gw_glu], [lax.empty(g.shape, g.dtype) for g in (gw_out, gw_glu)])
    gt_re, gt_im, ga_re8, ga_im8 = _scan_bwd(a_rows + gb_token[0:1, 0:1], g_re, g_im, s_re, s_im)
    grad_x, dprojb, vp_mi, d_bre, d_bim = _mix_in_bwd(gt_re, gt_im, bblk_re, bblk_im, dy1, dcc, dbg, proj, xs, dx2,
                                                      vec1, v512, convw_full, w_in_st)
    ssm_u, ssm_s = d_ssm // SSM_SPLIT, gp // SSM_SPLIT

    fold_b = jnp.asarray(_fold_matrix(ssm_s, n_state), BF16)
    fold_c = jnp.asarray(_fold_matrix(ssm_u, n_gch), BF16)
    db_re_f, db_im_f, gc_rows = _ssm_bgrad(d_bre, d_bim, bt_re, bt_im, coef_rows, fold_b, tile_b)
    dc_re_f, dc_im_f = _ssm_cgrad(d_cre, d_cim, fold_c)
    ga_sum = _ga_rowsum(ga_re8, ga_im8)
    g_lam_re, g_lam_im, g_log_step = _ssm_lamgrad(
        lam_re, lam_im, log_step, abar_re, abar_im, coef_re, coef_im,
        gc_rows[0].reshape(n_groups, n_state), gc_rows[1].reshape(n_groups, n_state),
        ga_sum[0].reshape(n_groups, n_state), ga_sum[1].reshape(n_groups, n_state))
    g_b_re = db_re_f.reshape(n_groups, n_gch, n_state).transpose(0, 2, 1)
    g_b_im = db_im_f.reshape(n_groups, n_gch, n_state).transpose(0, 2, 1)
    g_c_re = dc_re_f.reshape(n_groups, n_state, n_gch).transpose(0, 2, 1)
    g_c_im = dc_im_f.reshape(n_groups, n_state, n_gch).transpose(0, 2, 1)

    dmod = jnp.concatenate([vp_mi[0:1], vp_mi[1:2], vp_mo[0:1], vp_up[0:1], vp_up[1:2], vp_dn[0:1]], axis=1)
    small = [
        ("g_pre_mix", vp_mi[2:3]), ("g_post_mix", vp_mo[1:2]), ("g_pre_ffn", vp_up[2:3]), ("g_post_ffn", vp_dn[1:2]),
        ("ssm_lam_re", g_lam_re), ("ssm_lam_im", g_lam_im), ("ssm_log_step", g_log_step),
        ("ssm_b_re", g_b_re), ("ssm_b_im", g_b_im), ("ssm_c_re", g_c_re), ("ssm_c_im", g_c_im),
        ("ssm_d", vp5[3:4]), ("glu_b", vp5[2:3]), ("g_out_ssm", vp5[0:1]), ("g_out_conv", vp5[1:2]),
        ("conv_w", vp5[4:7]), ("ffn_conv_w", df_rows[0:3]), ("loss", loss_blk[0:1, 0:1]),
    ]
    packed, offsets, row = [], {}, 0
    for name, a in small:
        r = _to_rows(a, d)
        offsets[name] = (row, a.shape)
        packed.append(r)
        row += r.shape[0]
    n_small = -(-row // SUBLANES) * SUBLANES
    packed.append(jnp.zeros((n_small - row, d), F32))
    packed.append(pad8(dmod.reshape(6, d)))
    pack = jnp.concatenate(packed, axis=0)
    sm_send, sm_recv, _, sm_land, sm_token = _chips_start("small_start", True, [], [_landing(pack, chip)])

    gw_in = _matmul_tn(h1b, dprojb, d, w_in.shape[2], BF16, "dw_in", bt=4096, after=sm_token)
    gc_send, gc_recv, gc_src, gc_land, gc_token = _chips_start(
        "grads_start_in", False, [gw_in], [lax.empty(gw_in.shape, gw_in.dtype)])

    def partials(names, own, landed):
        return [_sum_landed(l, o, chip, "sum_" + nm) for l, o, nm in zip(landed, own, names)]

    def update(names, mine, theirs):
        done = {}
        for nm, pm, ps in zip(names, mine, theirs):
            w_, m_, v_ = big_params[nm]
            done[nm] = _adamw_big(pm, ps, w_[0], m_[0], v_[0], "adamw_" + nm)
        return done

    big_params = {"w_down": (w_down, m_w_down, v_w_down), "w_up": (w_up, m_w_up, v_w_up),
                  "w_out": (w_out, m_w_out, v_w_out), "glu_w": (glu_w, m_glu_w, v_glu_w),
                  "w_in": (w_in, m_w_in, v_w_in)}
    ffn_names, mix_names = ("w_down", "w_up"), ("w_out", "glu_w", "w_in")
    p_ffn = partials(ffn_names, *_chips_wait("grads_wait_ffn", False, ga_send, ga_recv, ga_src, ga_land, gc_token))
    sa_send, sa_recv, sa_src, sa_land, sa_token = _sibling_start("swap_start_ffn", p_ffn)

    (sm_landed,) = _chips_wait("small_wait", True, sm_send, sm_recv, [], sm_land, sa_token)[1]
    sm_part = _sum_blocks(sm_landed, "sum_small")
    dmod_mine = sm_landed[:, n_small:n_small + SUBLANES, :]
    ss_send, ss_recv, ss_src, ss_land, ss_token = _sibling_start("swap_start_small", [sm_part, dmod_mine])
    p_ffn, t_ffn = _sibling_wait("swap_wait_ffn", sa_send, sa_recv, sa_src, sa_land, ss_token)
    big = update(ffn_names, p_ffn, t_ffn)
    (sm_part, dmod_mine), (sm_sib, dmod_sib) = _sibling_wait("swap_wait_small", ss_send, ss_recv, ss_src, ss_land,
                                                              big["w_up"][0])
    sums = _add2(sm_part, sm_sib)
    dmod_by_core = jnp.stack([dmod_mine, dmod_sib], axis=1)
    dmod_by_core = jnp.where(ci == 0, dmod_by_core, dmod_by_core[:, ::-1])
    dmod_all = dmod_by_core[:, :, :6, :].reshape(N_DEV, 6 * d)
    g_b_ada = sums[n_small:n_small + 6].reshape(1, 6 * d)

    def unpack(name):
        r0, shape = offsets[name]
        size = math.prod(shape)
        nrow = -(-size // d)
        return sums[r0:r0 + nrow].reshape(-1)[:size].reshape(shape)

    p_mix = partials(mix_names, *_chips_wait(
        "grads_wait_mix", False, list(gb_send) + list(gc_send), list(gb_recv) + list(gc_recv),
        list(gb_src) + list(gc_src), list(gb_land) + list(gc_land), sums))
    sb_send, sb_recv, sb_src, sb_land, sb_token = _sibling_start("swap_start_mix", p_mix)

    dmod_cols = lax.dynamic_slice(dmod_all, (0, chip * n_ada), (N_DEV, n_ada)) + sb_token[0:1, 0:1]
    ada = _adamw_ada(c_all, dmod_cols, w_ada[0], m_w_ada[0], v_w_ada[0])
    p_mix, t_mix = _sibling_wait("swap_wait_mix", sb_send, sb_recv, sb_src, sb_land, ada[0])
    big.update(update(mix_names, p_mix, t_mix))

    g_small = {name: unpack(name) for name, _ in small}
    g_small["b_ada"] = g_b_ada
    g_small["conv_w"] = lax.dynamic_slice(g_small["conv_w"], (0, chip * conv_w.shape[2]), (3, conv_w.shape[2]))
    g_small["ffn_conv_w"] = lax.dynamic_slice(g_small["ffn_conv_w"], (0, chip * n_upc), (3, n_upc))
    g_small["ssm_log_step"] = g_small["ssm_log_step"].reshape(1, n_groups)
    small_params = {
        "b_ada": (b_ada, m_b_ada, v_b_ada), "g_pre_mix": (g_pre_mix, m_g_pre_mix, v_g_pre_mix),
        "g_post_mix": (g_post_mix, m_g_post_mix, v_g_post_mix), "ssm_lam_re": (ssm_lam_re, m_ssm_lam_re, v_ssm_lam_re),
        "ssm_lam_im": (ssm_lam_im, m_ssm_lam_im, v_ssm_lam_im),
        "ssm_log_step": (ssm_log_step, m_ssm_log_step, v_ssm_log_step),
        "ssm_b_re": (ssm_b_re, m_ssm_b_re, v_ssm_b_re), "ssm_b_im": (ssm_b_im, m_ssm_b_im, v_ssm_b_im),
        "ssm_c_re": (ssm_c_re, m_ssm_c_re, v_ssm_c_re), "ssm_c_im": (ssm_c_im, m_ssm_c_im, v_ssm_c_im),
        "ssm_d": (ssm_d, m_ssm_d, v_ssm_d), "glu_b": (glu_b, m_glu_b, v_glu_b),
        "g_out_ssm": (g_out_ssm, m_g_out_ssm, v_g_out_ssm), "conv_w": (conv_w, m_conv_w, v_conv_w),
        "g_out_conv": (g_out_conv, m_g_out_conv, v_g_out_conv), "g_pre_ffn": (g_pre_ffn, m_g_pre_ffn, v_g_pre_ffn),
        "g_post_ffn": (g_post_ffn, m_g_post_ffn, v_g_post_ffn),
        "ffn_conv_w": (ffn_conv_w, m_ffn_conv_w, v_ffn_conv_w),
    }

    def natural(a):
        return a[0] if a.ndim > 2 else a

    names = list(small_params)
    items = []
    for nm in names:
        w_, m_, v_ = small_params[nm]
        items.append((natural(w_), g_small[nm].reshape(natural(w_).shape), natural(m_), natural(v_)))
    upd = _adamw_small(items)
    small_out = {}
    for nm, (dl, mo, vo) in zip(names, upd):
        shp = small_params[nm][0].shape
        small_out[nm] = (g_small[nm].reshape(shp), dl.reshape(shp), mo.reshape(shp), vo.reshape(shp))

    loss = g_small["loss"][0, 0]

    order = ["w_ada", "b_ada", "g_pre_mix", "g_post_mix", "w_in", "ssm_lam_re", "ssm_lam_im", "ssm_log_step",
             "ssm_b_re", "ssm_b_im", "ssm_c_re", "ssm_c_im", "ssm_d", "glu_w", "glu_b", "g_out_ssm", "conv_w",
             "g_out_conv", "w_out", "g_pre_ffn", "g_post_ffn", "w_up", "ffn_conv_w", "w_down"]
    results = {"w_ada": tuple(a[None] for a in ada)}
    for nm in big:
        results[nm] = tuple(a[None] for a in big[nm])
    results.update(small_out)
    outs = [loss, grad_x[None]]
    for k in range(4):
        outs += [results[nm][k] for nm in order]
    return tuple(outs)


def _ga_rowsum(ga_re8, ga_im8):
    n = ga_re8.shape[1]

    def body(r_ref, i_ref, o_ref):
        o_ref[...] = jnp.zeros(o_ref.shape, F32)
        o_ref[0:1, :] = _colsum(r_ref[...])
        o_ref[1:2, :] = _colsum(i_ref[...])

    return pl.pallas_call(body, name="ga_rowsum", out_shape=jax.ShapeDtypeStruct((SUBLANES, n), F32))(ga_re8, ga_im8)
```

```python
import functools
import math

import jax
import jax.numpy as jnp
import numpy as np
from jax import lax
from jax.experimental import pallas as pl
from jax.experimental.pallas import tpu as pltpu

F32 = jnp.float32
BF16 = jnp.bfloat16
MESH = pl.DeviceIdType.MESH

EPS = 1e-6
LAMBDA_RE_MAX = -1e-4
ADAM_LR = 0.001
ADAM_B1 = 0.9
ADAM_B2 = 0.999
ADAM_EPS = 1e-08
ADAM_WD = 0.01
ADAM_STEP = 10

SUBLANES = 8
BF16_ROWS = 16
N_CHIPS = 4
N_DEV = 8
CONV_HEAD_DIM = 64
VMEM_BIG = 56 * 1024 * 1024
VMEM_MID = 40 * 1024 * 1024
VMEM_KEEP_OPERANDS_IN_HBM = 62 * 1024 * 1024

TB_MIX = 256
TB_MIX_FWD = 512
TB_FFN = 256
TB_FFN_UP = 512
TB_SCAN = 2048
W_SCAN = 256
SSM_SPLIT = 4
CW_FFN = 256
SCAN_UNROLL = 4
TB_TN = 512


def _cparams(sem=None, vmem=None):
    kw = {}
    if sem is not None:
        kw["dimension_semantics"] = sem
    if vmem is not None:
        kw["vmem_limit_bytes"] = vmem
    return pltpu.CompilerParams(**kw)


def _blk(t, pref):
    return pref if t % pref == 0 else t


def _dot(a, b):
    return jnp.dot(a.astype(BF16), b.astype(BF16), preferred_element_type=F32)


def _dot_nt(a, b):
    return lax.dot_general(a.astype(BF16), b.astype(BF16), (((1,), (1,)), ((), ())),
                           preferred_element_type=F32)


def _dot_tn(a, b):
    return lax.dot_general(a.astype(BF16), b.astype(BF16), (((0,), (0,)), ((), ())),
                           preferred_element_type=F32)


def _sigmoid(x):
    return 0.5 * jnp.tanh(0.5 * x) + 0.5


_GELU_K = math.sqrt(2.0 / math.pi)
_GELU_C = 0.044715


def _gelu(x):
    th = jnp.tanh(_GELU_K * (x + _GELU_C * x * x * x))
    return x * (0.5 * (1.0 + th))


def _gelu_and_grad(x):
    x2 = x * x
    th = jnp.tanh(_GELU_K * (x + _GELU_C * x2 * x))
    half = 0.5 * (1.0 + th)
    return x * half, half + 0.5 * x * (1.0 - th * th) * _GELU_K * (1.0 + 3.0 * _GELU_C * x2)


def _rowmean(x):
    return jnp.mean(x, axis=-1, keepdims=True)


def _colsum(x):
    return jnp.sum(x, axis=0, keepdims=True)


def _split_dot(x, m):
    hi = x.astype(BF16)
    lo = (x - hi.astype(F32)).astype(BF16)
    return (jnp.dot(hi, m, preferred_element_type=F32) + jnp.dot(lo, m, preferred_element_type=F32))


def _split3_dot(x, m):
    hi = x.astype(BF16)
    r1 = x - hi.astype(F32)
    mid = r1.astype(BF16)
    lo = (r1 - mid.astype(F32)).astype(BF16)
    return (jnp.dot(hi, m, preferred_element_type=F32) + jnp.dot(mid, m, preferred_element_type=F32)
            + jnp.dot(lo, m, preferred_element_type=F32))


def _shift_down(x, halo, k):
    r = pltpu.roll(x, k, 0)
    row = lax.broadcasted_iota(jnp.int32, x.shape, 0)
    last = halo.shape[0]
    for j in range(k):
        r = jnp.where(row == j, halo[last - k + j:last - k + j + 1, :], r)
    return r


def _shift_up(x, halo, k):
    n = x.shape[0]
    r = pltpu.roll(x, n - k, 0)
    row = lax.broadcasted_iota(jnp.int32, x.shape, 0)
    for j in range(k):
        r = jnp.where(row == n - k + j, halo[j:j + 1, :], r)
    return r


def _acc_rows(ref, first, rows):
    @pl.when(first)
    def _():
        ref[...] = jnp.zeros(ref.shape, ref.dtype)
    for j, r in enumerate(rows):
        ref[j:j + 1, :] += r


def _rows(tb, c, col=0):
    return pl.BlockSpec((tb, c), lambda i, col=col: (i, col))


def _full(shape):
    nd = len(shape)
    return pl.BlockSpec(shape, lambda i, nd=nd: (0,) * nd)


def _resident(shape):
    nd = len(shape)
    return pl.BlockSpec(shape, lambda i, nd=nd: (0,) * nd, pipeline_mode=pl.Buffered(1))


def _halo_prev(tb, c, col=0, rows=SUBLANES):
    per = tb // rows
    return pl.BlockSpec((rows, c), lambda i, col=col: (jnp.maximum(i * per - 1, 0), col))


def _halo_next(tb, c, t, col=0, rows=SUBLANES):
    per = tb // rows
    last = t // rows - 1
    return pl.BlockSpec((rows, c), lambda i, col=col: (jnp.minimum((i + 1) * per, last), col))


def _mesh_pos():
    return lax.axis_index("x"), lax.axis_index("y"), lax.axis_index("c")


def _allgather8(x_pad, name):
    m_per, n = x_pad.shape

    def body(x_ref, out_ref, send_sems, recv_sems, local_sem):
        x, y, c = _mesh_pos()
        me, sibling = (x, y, c), (x, y, 1 - c)
        chips = [(1 - x, y), (x, 1 - y), (1 - x, 1 - y)]

        def rows(px, py, pc):
            return out_ref.at[pl.ds((4 * px + 2 * py + pc) * m_per, m_per), :]

        def copy(k, block, to, src=None):
            return pltpu.make_async_remote_copy(
                src_ref=rows(*block) if src is None else src, dst_ref=rows(*block),
                send_sem=send_sems.at[k], recv_sem=recv_sems.at[k], device_id=to, device_id_type=MESH)

        mine = pltpu.make_async_copy(x_ref, rows(*me), local_sem)
        mine.start()
        first = [copy(0, me, sibling, src=x_ref)]
        first += [copy(1 + j, me, (*chip, c), src=x_ref) for j, chip in enumerate(chips)]
        for cp in first:
            cp.start()
        passed = [copy(4 + j, (*chip, c), sibling) for j, chip in enumerate(chips)]
        for j, chip in enumerate(chips):
            copy(1 + j, (*chip, c), me).wait_recv()
            passed[j].start()
        copy(0, sibling, me).wait_recv()
        for j, chip in enumerate(chips):
            copy(4 + j, (*chip, 1 - c), me).wait_recv()
        for cp in first + passed:
            cp.wait_send()
        mine.wait()

    return pl.pallas_call(
        body, name=name,
        out_shape=jax.ShapeDtypeStruct((N_DEV * m_per, n), F32),
        in_specs=[pl.BlockSpec(memory_space=pltpu.VMEM)],
        out_specs=pl.BlockSpec(memory_space=pltpu.VMEM),
        scratch_shapes=[pltpu.SemaphoreType.DMA((7,)), pltpu.SemaphoreType.DMA((7,)), pltpu.SemaphoreType.DMA],
    )(x_pad)


_START_IDS = {name: k for k, name in enumerate((
    "weights_start_in", "weights_start_rest", "grads_start_ffn", "grads_start_mix", "grads_start_in", "small_start",
    "swap_start_ffn", "swap_start_small", "swap_start_mix"))}
_HBM = pl.BlockSpec(memory_space=pltpu.HBM)
_SEM = pl.BlockSpec(memory_space=pltpu.SEMAPHORE)
_EFFECT = pltpu.SideEffectType.DATAFLOW_SIDE_EFFECTING


def _chip_copy(gather, src_ref, land_ref, send, recv, j, arrival):
    x, y, c = _mesh_pos()
    peer = [(1 - x, y), (x, 1 - y), (1 - x, 1 - y)][j]
    peer_chip = 2 * peer[0] + peer[1]
    my_chip = 2 * x + y
    return pltpu.make_async_remote_copy(
        src_ref=land_ref.at[my_chip] if gather else src_ref.at[peer_chip],
        dst_ref=land_ref.at[peer_chip if arrival else my_chip],
        send_sem=send.at[j], recv_sem=recv.at[j], device_id=(*peer, c), device_id_type=MESH)


def _chips_start(name, gather, srcs, lands, after=None):
    n, ns = len(lands), len(srcs)
    extra = [] if after is None else [after]

    def body(*refs):
        src_refs, land_refs = refs[:ns], refs[ns:ns + n]
        outs = refs[ns + n + len(extra):]
        sends, recvs, token = outs[:n], outs[n:2 * n], outs[-1]
        x, y, c = _mesh_pos()
        barrier = pltpu.get_barrier_semaphore()
        for peer in [(1 - x, y), (x, 1 - y), (1 - x, 1 - y)]:
            pl.semaphore_signal(barrier, inc=1, device_id=(*peer, c), device_id_type=MESH)
        pl.semaphore_wait(barrier, N_CHIPS - 1)
        for k in range(n):
            for j in range(3):
                _chip_copy(gather, src_refs[k] if ns else None, land_refs[k], sends[k], recvs[k], j, False).start()
        token[...] = jnp.zeros(token.shape, F32)

    sem = pltpu.SemaphoreType.DMA((3,))
    thru = tuple(pltpu.HBM(a.shape, a.dtype) for a in list(srcs) + list(lands))
    res = pl.pallas_call(
        body, name=name,
        out_shape=(sem,) * (2 * n) + thru + (jax.ShapeDtypeStruct((SUBLANES, 128), F32),),
        in_specs=[_HBM] * (ns + n) + [pl.BlockSpec(memory_space=pl.ANY)] * len(extra),
        out_specs=(_SEM,) * (2 * n) + (_HBM,) * (ns + n) + (pl.BlockSpec(memory_space=pltpu.VMEM),),
        input_output_aliases={k: 2 * n + k for k in range(ns + n)},
        compiler_params=pltpu.CompilerParams(has_side_effects=_EFFECT, collective_id=_START_IDS[name]),
    )(*[pltpu.with_memory_space_constraint(a, pltpu.HBM) for a in list(srcs) + list(lands)], *extra)
    return res[:n], res[n:2 * n], res[2 * n:2 * n + ns], res[2 * n + ns:2 * n + ns + n], res[-1]


def _chips_wait(name, gather, sends, recvs, srcs, lands, after):
    n, ns = len(lands), len(srcs)

    def body(*refs):
        src_refs, land_refs = refs[:ns], refs[ns:ns + n]
        sends_, recvs_ = refs[ns + n:ns + 2 * n], refs[ns + 2 * n:ns + 3 * n]
        for k in range(n):
            for j in range(3):
                cp = _chip_copy(gather, src_refs[k] if ns else None, land_refs[k], sends_[k], recvs_[k], j, True)
                cp.wait_send()
                cp.wait_recv()

    thru = tuple(pltpu.HBM(a.shape, a.dtype) for a in list(srcs) + list(lands))
    res = pl.pallas_call(
        body, name=name, out_shape=thru,
        in_specs=[_HBM] * (ns + n) + [_SEM] * (2 * n) + [pl.BlockSpec(memory_space=pl.ANY)],
        out_specs=(_HBM,) * (ns + n),
        input_output_aliases={k: k for k in range(ns + n)},
        compiler_params=pltpu.CompilerParams(has_side_effects=_EFFECT),
    )(*srcs, *lands, *sends, *recvs, after)
    return res[:ns], res[ns:]


def _sibling_copy(src_ref, land_ref, send, recv):
    x, y, c = _mesh_pos()
    return pltpu.make_async_remote_copy(src_ref=src_ref, dst_ref=land_ref, send_sem=send.at[0], recv_sem=recv.at[0],
                                        device_id=(x, y, 1 - c), device_id_type=MESH)


def _sibling_start(name, arrs, after=None):
    n = len(arrs)
    extra = [] if after is None else [after]
    lands = [lax.empty(a.shape, a.dtype) for a in arrs]

    def body(*refs):
        src_refs, land_refs = refs[:n], refs[n:2 * n]
        outs = refs[2 * n + len(extra):]
        sends, recvs, token = outs[:n], outs[n:2 * n], outs[-1]
        x, y, c = _mesh_pos()
        barrier = pltpu.get_barrier_semaphore()
        pl.semaphore_signal(barrier, inc=1, device_id=(x, y, 1 - c), device_id_type=MESH)
        pl.semaphore_wait(barrier, 1)
        for k in range(n):
            _sibling_copy(src_refs[k], land_refs[k], sends[k], recvs[k]).start()
        token[...] = jnp.zeros(token.shape, F32)

    sem = pltpu.SemaphoreType.DMA((1,))
    thru = tuple(pltpu.HBM(a.shape, a.dtype) for a in list(arrs) + lands)
    res = pl.pallas_call(
        body, name=name,
        out_shape=(sem,) * (2 * n) + thru + (jax.ShapeDtypeStruct((SUBLANES, 128), F32),),
        in_specs=[_HBM] * (2 * n) + [pl.BlockSpec(memory_space=pl.ANY)] * len(extra),
        out_specs=(_SEM,) * (2 * n) + (_HBM,) * (2 * n) + (pl.BlockSpec(memory_space=pltpu.VMEM),),
        input_output_aliases={k: 2 * n + k for k in range(2 * n)},
        compiler_params=pltpu.CompilerParams(has_side_effects=_EFFECT, collective_id=_START_IDS[name]),
    )(*[pltpu.with_memory_space_constraint(a, pltpu.HBM) for a in list(arrs) + lands], *extra)
    return res[:n], res[n:2 * n], res[2 * n:3 * n], res[3 * n:4 * n], res[-1]


def _sibling_wait(name, sends, recvs, srcs, lands, after):
    n = len(srcs)

    def body(*refs):
        src_refs, land_refs = refs[:n], refs[n:2 * n]
        sends_, recvs_ = refs[2 * n:3 * n], refs[3 * n:4 * n]
        for k in range(n):
            cp = _sibling_copy(src_refs[k], land_refs[k], sends_[k], recvs_[k])
            cp.wait_send()
            cp.wait_recv()

    thru = tuple(pltpu.HBM(a.shape, a.dtype) for a in list(srcs) + list(lands))
    res = pl.pallas_call(
        body, name=name, out_shape=thru,
        in_specs=[_HBM] * (2 * n) + [_SEM] * (2 * n) + [pl.BlockSpec(memory_space=pl.ANY)],
        out_specs=(_HBM,) * (2 * n),
        input_output_aliases={k: k for k in range(2 * n)},
        compiler_params=pltpu.CompilerParams(has_side_effects=_EFFECT),
    )(*srcs, *lands, *sends, *recvs, after)
    return res[:n], res[n:]


def _landing(own, chip):
    zone = lax.empty((N_CHIPS,) + own.shape, own.dtype)
    return lax.dynamic_update_slice(zone, own[None], (chip,) + (0,) * own.ndim)


def _mod_shard(c_all, w_ada_sh, b_sh):
    d, n = w_ada_sh.shape
    bn = 512

    def body(c_ref, w_ref, b_ref, o_ref):
        cc = c_ref[...]
        ca = cc * _sigmoid(cc)
        o_ref[...] = _dot(ca, w_ref[...]) + b_ref[...]

    return pl.pallas_call(
        body, name="mod_shard", grid=(n // bn,),
        out_shape=jax.ShapeDtypeStruct((N_DEV, n), F32),
        in_specs=[_full((N_DEV, d)), pl.BlockSpec((d, bn), lambda j: (0, j)), pl.BlockSpec((1, bn), lambda j: (0, j))],
        out_specs=pl.BlockSpec((N_DEV, bn), lambda j: (0, j)),
        compiler_params=_cparams(("parallel",)),
    )(c_all, w_ada_sh, b_sh)


def _ssm_prep(lam_re, lam_im, log_step):
    g, p = lam_re.shape

    def body(lr_ref, li_ref, ls_ref, ar_ref, ai_ref, cr_ref, ci_ref):
        lr = jnp.minimum(lr_ref[...], LAMBDA_RE_MAX)
        li = li_ref[...]
        st = jnp.exp(ls_ref[...])
        mag = jnp.exp(lr * st)
        ar = mag * jnp.cos(li * st)
        ai = mag * jnp.sin(li * st)
        den = lr * lr + li * li
        nr = ar - 1.0
        ar_ref[...] = ar
        ai_ref[...] = ai
        cr_ref[...] = (nr * lr + ai * li) / den
        ci_ref[...] = (ai * lr - nr * li) / den

    sds = jax.ShapeDtypeStruct((g, p), F32)
    return pl.pallas_call(body, name="ssm_prep", out_shape=(sds,) * 4)(lam_re, lam_im, log_step)


def _ssm_blocks(bt_re, bt_im, ct_re, ct_im, coef_rows, tile_b, tile_c):
    gh, p = bt_re.shape
    gp, h = ct_re.shape
    nb = SSM_SPLIT
    cb, rb = gp // nb, gp // nb

    def body(btr, bti, ctr, cti, cf, tb_ref, tc_ref, bre_o, bim_o, cre_o, cim_o):
        j = pl.program_id(0)
        row = lax.broadcasted_iota(jnp.int32, (gh, cb), 0)
        col = lax.broadcasted_iota(jnp.int32, (gh, cb), 1) + j * cb
        mask = (row >> 4) == (col >> 6)
        cr, ci = cf[0:1, :], cf[1:2, :]
        br = _split3_dot(btr[...], tb_ref[...])
        bi = _split3_dot(bti[...], tb_ref[...])
        bre_o[...] = jnp.where(mask, br * cr - bi * ci, 0.0).astype(BF16)
        bim_o[...] = jnp.where(mask, br * ci + bi * cr, 0.0).astype(BF16)
        row2 = lax.broadcasted_iota(jnp.int32, (rb, gh), 0) + j * rb
        col2 = lax.broadcasted_iota(jnp.int32, (rb, gh), 1)
        mask2 = (row2 >> 6) == (col2 >> 4)
        cre_o[...] = jnp.where(mask2, _split3_dot(ctr[...], tc_ref[...]), 0.0).astype(BF16)
        cim_o[...] = jnp.where(mask2, _split3_dot(cti[...], tc_ref[...]), 0.0).astype(BF16)

    bspec = pl.BlockSpec((gh, cb), lambda j: (0, j))
    cspec = pl.BlockSpec((rb, gh), lambda j: (j, 0))
    cin = pl.BlockSpec((rb, h), lambda j: (j, 0))
    return pl.pallas_call(
        body, name="ssm_blocks", grid=(nb,),
        out_shape=(jax.ShapeDtypeStruct((gh, gp), BF16),) * 2 + (jax.ShapeDtypeStruct((gp, gh), BF16),) * 2,
        in_specs=[_full((gh, p)), _full((gh, p)), cin, cin, pl.BlockSpec((SUBLANES, cb), lambda j: (0, j)),
                  _full(tile_b.shape), _full(tile_c.shape)],
        out_specs=(bspec, bspec, cspec, cspec),
        compiler_params=_cparams(("parallel",)),
    )(bt_re, bt_im, ct_re, ct_im, coef_rows, tile_b, tile_c)


def _scan_consts(a_ref, reverse):
    w = a_ref.shape[1]
    ar1 = a_ref[0:1, :]
    ai1 = a_ref[1:2, :]
    if reverse:
        ai1 = -ai1
    pr, pi = [ar1], [ai1]
    for _ in range(1, SUBLANES):
        nr = pr[-1] * ar1 - pi[-1] * ai1
        ni = pr[-1] * ai1 + pi[-1] * ar1
        pr.append(nr)
        pi.append(ni)
    row = lax.broadcasted_iota(jnp.int32, (SUBLANES, w), 0)
    dist = (SUBLANES - 1 - row) if reverse else row

    def pick(vals):
        out = jnp.broadcast_to(vals[SUBLANES - 1], (SUBLANES, w))
        for r in range(SUBLANES - 1):
            out = jnp.where(dist == r, vals[r], out)
        return out

    p_r, p_i = pick(pr), pick(pi)
    steps = []
    for k in (1, 2, 4):
        steps.append((k, jnp.where(dist >= k, pr[k - 1], 0.0), jnp.where(dist >= k, pi[k - 1], 0.0)))
    a8 = (jnp.broadcast_to(pr[SUBLANES - 1], (SUBLANES, w)), jnp.broadcast_to(pi[SUBLANES - 1], (SUBLANES, w)))
    return row, p_r, p_i, steps, a8


def _scan_tile(xr, xi, cr, ci, consts, reverse):
    row, p_r, p_i, steps, (a8r, a8i) = consts
    for k, s_r, s_i in steps:
        sh = (SUBLANES - k) if reverse else k
        qr = pltpu.roll(xr, sh, 0)
        qi = pltpu.roll(xi, sh, 0)
        xr, xi = xr + s_r * qr - s_i * qi, xi + s_r * qi + s_i * qr
    outr = xr + p_r * cr - p_i * ci
    outi = xi + p_r * ci + p_i * cr
    e = 0 if reverse else SUBLANES - 1
    er = jnp.broadcast_to(xr[e:e + 1, :], xr.shape)
    ei = jnp.broadcast_to(xi[e:e + 1, :], xi.shape)
    return outr, outi, er + a8r * cr - a8i * ci, ei + a8r * ci + a8i * cr


def _scan_fwd(a_rows, bu_re, bu_im):
    t, n = bu_re.shape
    tb, w = _blk(t, TB_SCAN), W_SCAN
    ntile = tb // SUBLANES

    def body(a_ref, br_ref, bi_ref, sr_ref, si_ref, car, cai):
        @pl.when(pl.program_id(1) == 0)
        def _():
            car[...] = jnp.zeros(car.shape, F32)
            cai[...] = jnp.zeros(cai.shape, F32)
        consts = _scan_consts(a_ref, False)

        def pair(i, carry):
            o = pl.multiple_of(i * BF16_ROWS, BF16_ROWS)
            b_r = br_ref[pl.ds(o, BF16_ROWS), :].astype(F32)
            b_i = bi_ref[pl.ds(o, BF16_ROWS), :].astype(F32)
            outs = []
            for h in range(2):
                rows = slice(h * SUBLANES, (h + 1) * SUBLANES)
                outr, outi, ncr, nci = _scan_tile(b_r[rows, :], b_i[rows, :], carry[0], carry[1], consts, False)
                outs.append((outr, outi))
                carry = (ncr, nci)
            sr_ref[pl.ds(o, BF16_ROWS), :] = jnp.concatenate([outs[0][0], outs[1][0]], axis=0).astype(BF16)
            si_ref[pl.ds(o, BF16_ROWS), :] = jnp.concatenate([outs[0][1], outs[1][1]], axis=0).astype(BF16)
            return carry

        def pairs(i, carry):
            for s in range(SCAN_UNROLL // 2):
                carry = pair(i * (SCAN_UNROLL // 2) + s, carry)
            return carry

        cr, ci = lax.fori_loop(0, ntile // SCAN_UNROLL, pairs, (car[...], cai[...]))
        car[...] = cr
        cai[...] = ci

    spec = pl.BlockSpec((tb, w), lambda s, k: (k, s))
    sds = jax.ShapeDtypeStruct((t, n), BF16)
    return pl.pallas_call(
        body, name="scan_fwd", grid=(n // w, t // tb), out_shape=(sds, sds),
        in_specs=[pl.BlockSpec((SUBLANES, w), lambda s, k: (0, s)), spec, spec], out_specs=(spec, spec),
        scratch_shapes=[pltpu.VMEM((SUBLANES, w), F32), pltpu.VMEM((SUBLANES, w), F32)],
        compiler_params=_cparams(("parallel", "arbitrary"), VMEM_MID),
    )(a_rows, bu_re, bu_im)


def _scan_bwd(a_rows, g_re, g_im, s_re, s_im):
    t, n = g_re.shape
    tb, w = _blk(t, TB_SCAN), W_SCAN
    ntile = tb // SUBLANES
    npair = tb // BF16_ROWS
    nt = t // tb

    def body(a_ref, gr_ref, gi_ref, sr_ref, si_ref, or_ref, oi_ref, gar_ref, gai_ref, car, cai):
        @pl.when(pl.program_id(1) == 0)
        def _():
            car[...] = jnp.zeros(car.shape, F32)
            cai[...] = jnp.zeros(cai.shape, F32)
            gar_ref[...] = jnp.zeros(gar_ref.shape, F32)
            gai_ref[...] = jnp.zeros(gai_ref.shape, F32)
        consts = _scan_consts(a_ref, True)
        row = consts[0]

        def pair(i, carry):
            cr, ci, accr, acci = carry
            o = pl.multiple_of((npair - 1 - i) * BF16_ROWS, BF16_ROWS)
            s_r = sr_ref[pl.ds(o, BF16_ROWS), :].astype(F32)
            s_i = si_ref[pl.ds(o, BF16_ROWS), :].astype(F32)
            g_r = gr_ref[pl.ds(o, BF16_ROWS), :].astype(F32)
            g_i = gi_ref[pl.ds(o, BF16_ROWS), :].astype(F32)
            outs = [None, None]
            for h in (1, 0):
                rows = slice(h * SUBLANES, (h + 1) * SUBLANES)
                outr, outi, ncr, nci = _scan_tile(g_r[rows, :], g_i[rows, :], cr, ci, consts, True)
                outs[h] = (outr, outi)
                gnr = jnp.where(row == SUBLANES - 1, cr, pltpu.roll(outr, SUBLANES - 1, 0))
                gni = jnp.where(row == SUBLANES - 1, ci, pltpu.roll(outi, SUBLANES - 1, 0))
                sr = s_r[h * SUBLANES:(h + 1) * SUBLANES, :]
                si = s_i[h * SUBLANES:(h + 1) * SUBLANES, :]
                accr, acci = accr + sr * gnr + si * gni, acci + sr * gni - si * gnr
                cr, ci = ncr, nci
            or_ref[pl.ds(o, BF16_ROWS), :] = jnp.concatenate([outs[0][0], outs[1][0]], axis=0).astype(BF16)
            oi_ref[pl.ds(o, BF16_ROWS), :] = jnp.concatenate([outs[0][1], outs[1][1]], axis=0).astype(BF16)
            return cr, ci, accr, acci

        def pairs(i, carry):
            for s in range(SCAN_UNROLL // 2):
                carry = pair(i * (SCAN_UNROLL // 2) + s, carry)
            return carry

        cr, ci, accr, acci = lax.fori_loop(0, ntile // SCAN_UNROLL, pairs,
                                           (car[...], cai[...], gar_ref[...], gai_ref[...]))
        car[...] = cr
        cai[...] = ci
        gar_ref[...] = accr
        gai_ref[...] = acci

    spec = pl.BlockSpec((tb, w), lambda s, k: (nt - 1 - k, s))
    aspec = pl.BlockSpec((SUBLANES, w), lambda s, k: (0, s))
    sds = jax.ShapeDtypeStruct((t, n), BF16)
    asds = jax.ShapeDtypeStruct((SUBLANES, n), F32)
    return pl.pallas_call(
        body, name="scan_bwd", grid=(n // w, nt), out_shape=(sds, sds, asds, asds),
        in_specs=[aspec, spec, spec, spec, spec], out_specs=(spec, spec, aspec, aspec),
        scratch_shapes=[pltpu.VMEM((SUBLANES, w), F32), pltpu.VMEM((SUBLANES, w), F32)],
        compiler_params=_cparams(("parallel", "arbitrary"), VMEM_MID),
    )(a_rows, g_re, g_im, s_re, s_im)


def _mix_in(x, vec, w_in_st, b_re, b_im):
    t, d = x.shape
    ns, _, nc = w_in_st.shape
    dssm, nstate = b_re.shape
    du, ds = dssm // SSM_SPLIT, nstate // SSM_SPLIT
    tb = _blk(t, TB_MIX_FWD)

    def body(x_ref, vec_ref, w_ref, bre_ref, bim_ref, proj_ref, bur_ref, bui_ref, h1_ref):
        xv = x_ref[...]
        r = lax.rsqrt(_rowmean(xv * xv) + EPS)
        h = xv * r * vec_ref[0:1, :] * vec_ref[1:2, :] + vec_ref[2:3, :]
        hb = h.astype(BF16)
        h1_ref[...] = hb
        u = None
        for j in range(ns):
            pj = jnp.dot(hb, w_ref[j], preferred_element_type=F32)
            proj_ref[:, j * nc:(j + 1) * nc] = pj.astype(BF16)
            if j == 0:
                u = pj
        ub = u.astype(BF16)
        for q in range(SSM_SPLIT):
            rq, cq = slice(q * du, (q + 1) * du), slice(q * ds, (q + 1) * ds)
            bur_ref[:, cq] = jnp.dot(ub[:, rq], bre_ref[rq, cq], preferred_element_type=F32).astype(BF16)
            bui_ref[:, cq] = jnp.dot(ub[:, rq], bim_ref[rq, cq], preferred_element_type=F32).astype(BF16)

    return pl.pallas_call(
        body, name="mix_in", grid=(t // tb,),
        out_shape=(jax.ShapeDtypeStruct((t, ns * nc), BF16), jax.ShapeDtypeStruct((t, nstate), BF16),
                   jax.ShapeDtypeStruct((t, nstate), BF16), jax.ShapeDtypeStruct((t, d), BF16)),
        in_specs=[_rows(tb, d), _full((SUBLANES, d)), _resident(w_in_st.shape), _resident(b_re.shape),
                  _resident(b_im.shape)],
        out_specs=(_rows(tb, ns * nc), _rows(tb, nstate), _rows(tb, nstate), _rows(tb, d)),
        compiler_params=_cparams(("parallel",), VMEM_BIG),
    )(x, vec, w_in_st, b_re, b_im)


def _head_ms(y, h_ref):
    return _split_dot(y * y, h_ref[...])


def _conv3(x, halo, w_ref):
    return w_ref[0:1, :] * _shift_down(x, halo, 2) + w_ref[1:2, :] * _shift_down(x, halo, 1) + w_ref[2:3, :] * x


def _mix_out(x, proj, s_re, s_im, c_re, c_im, v512, convw, glu_w, h16, h64, w_out, vd):
    t, d = x.shape
    dh = c_re.shape[1]
    nstate = s_re.shape[1]
    du, ds = dh // SSM_SPLIT, nstate // SSM_SPLIT
    tb = _blk(t, TB_MIX_FWD)

    def body(x_ref, u_ref, bg_ref, cg_ref, v_ref, cgh_ref, vh_ref, sr_ref, si_ref, cre_ref, cim_ref, p_ref,
             cw_ref, gw_ref, h16_ref, h64_ref, wo_ref, vd_ref, y1_ref, o_ref, x2_ref):
        i = pl.program_id(0)
        u = u_ref[...].astype(F32)
        ys = []
        for q in range(SSM_SPLIT):
            rq, cq = slice(q * ds, (q + 1) * ds), slice(q * du, (q + 1) * du)
            ys.append(_dot(sr_ref[:, rq], cre_ref[rq, cq]) - _dot(si_ref[:, rq], cim_ref[rq, cq]))
        ys = jnp.concatenate(ys, axis=1)
        y1 = ys + p_ref[0:1, :] * u
        y1_ref[...] = y1
        z = _gelu(y1)
        q = _dot(z, gw_ref[...]) + p_ref[1:2, :]
        ya = z * _sigmoid(q)
        na = ya * lax.rsqrt(_head_ms(ya, h16_ref) + EPS) * p_ref[2:3, :]
        cv = cg_ref[...].astype(F32) * v_ref[...].astype(F32)
        cvh = jnp.where(i > 0, cgh_ref[...].astype(F32) * vh_ref[...].astype(F32), 0.0)
        yb = bg_ref[...].astype(F32) * _conv3(cv, cvh, cw_ref)
        nb = yb * lax.rsqrt(_head_ms(yb, h64_ref) + EPS) * p_ref[3:4, :]
        o = _dot(na, wo_ref[0:dh, :]) + _dot(nb, wo_ref[dh:2 * dh, :])
        o_ref[...] = o
        on = o * lax.rsqrt(_rowmean(o * o) + EPS) * vd_ref[0:1, :]
        x2_ref[...] = x_ref[...] + vd_ref[1:2, :] * on

    return pl.pallas_call(
        body, name="mix_out", grid=(t // tb,),
        out_shape=(jax.ShapeDtypeStruct((t, dh), F32), jax.ShapeDtypeStruct((t, d), F32),
                   jax.ShapeDtypeStruct((t, d), F32)),
        in_specs=[_rows(tb, d), _rows(tb, dh, 0), _rows(tb, dh, 1), _rows(tb, dh, 2), _rows(tb, dh, 3),
                  _halo_prev(tb, dh, 2, BF16_ROWS), _halo_prev(tb, dh, 3, BF16_ROWS), _rows(tb, nstate), _rows(tb, nstate),
                  _full(c_re.shape), _full(c_im.shape), _full(v512.shape), _full(convw.shape), _full(glu_w.shape),
                  _full(h16.shape), _full(h64.shape), _full(w_out.shape), _full(vd.shape)],
        out_specs=(_rows(tb, dh), _rows(tb, d), _rows(tb, d)),
        compiler_params=_cparams(("parallel",), VMEM_BIG),
    )(x, proj, proj, proj, proj, proj, proj, s_re, s_im, c_re, c_im, v512, convw, glu_w, h16, h64, w_out, vd)


def _ffn_up(x2, vec, w_up_st):
    t, d = x2.shape
    ns, _, nc = w_up_st.shape
    tb = _blk(t, TB_FFN_UP)

    def body(x_ref, vec_ref, w_ref, up_ref, h2_ref):
        xv = x_ref[...]
        r = lax.rsqrt(_rowmean(xv * xv) + EPS)
        h = xv * r * vec_ref[0:1, :] * vec_ref[1:2, :] + vec_ref[2:3, :]
        hb = h.astype(BF16)
        h2_ref[...] = hb
        for j in range(ns):
            up_ref[:, j * nc:(j + 1) * nc] = jnp.dot(hb, w_ref[j], preferred_element_type=F32)

    return pl.pallas_call(
        body, name="ffn_up", grid=(t // tb,),
        out_shape=(jax.ShapeDtypeStruct((t, ns * nc), F32), jax.ShapeDtypeStruct((t, d), BF16)),
        in_specs=[_rows(tb, d), _full((SUBLANES, d)), _resident(w_up_st.shape)],
        out_specs=(_rows(tb, ns * nc), _rows(tb, d)),
        compiler_params=_cparams(("parallel",), VMEM_BIG),
    )(x2, vec, w_up_st)


def _ffn_down(up, fw, w_down, w_down_t, x2, tgt, vd):
    t, nh = up.shape
    dff, d = w_down.shape
    tb = _blk(t, TB_FFN)
    inv_d = 1.0 / d

    def body(up_ref, uph_ref, fw_ref, wd_ref, wdt_ref, x2_ref, tgt_ref, vd_ref,
             act_ref, ddn_ref, dout_ref, dhid_ref, vec_ref, loss_ref, a_s, vv_s, sg_s):
        i = pl.program_id(0)

        def conv_cols(sl):
            x = up_ref[:, sl]
            halo = jnp.where(i > 0, uph_ref[:, sl], 0.0)
            return (fw_ref[0:1, sl] * _shift_down(x, halo, 2) + fw_ref[1:2, sl] * _shift_down(x, halo, 1)
                    + fw_ref[2:3, sl] * x)

        dn = None
        for o in range(0, dff, CW_FFN):
            sl = slice(o, o + CW_FFN)
            a = conv_cols(sl)
            vv = conv_cols(slice(dff + o, dff + o + CW_FFN))
            sg = _sigmoid(a)
            si = a * sg
            a_s[:, sl] = si
            vv_s[:, sl] = vv
            sg_s[:, sl] = sg
            actb = (si * vv).astype(BF16)
            act_ref[:, sl] = actb
            pj = lax.dot_general(actb, wdt_ref[:, sl], (((1,), (1,)), ((), ())), preferred_element_type=F32)
            dn = pj if dn is None else dn + pj
        r3 = lax.rsqrt(_rowmean(dn * dn) + EPS)
        xn = dn * r3
        g = vd_ref[0:1, :]
        gt2 = vd_ref[1:2, :]
        dnn = xn * g
        diff = x2_ref[...] + gt2 * dnn - tgt_ref[...]
        part = 0.5 * inv_d * jnp.sum(diff * diff)

        @pl.when(i == 0)
        def _():
            loss_ref[...] = jnp.zeros(loss_ref.shape, F32)
        loss_ref[...] += part
        dout = diff * inv_d
        dout_ref[...] = dout
        ddnn = dout * gt2
        _acc_rows(vec_ref, i == 0, [_colsum(dout * dnn), _colsum(ddnn * xn)])
        dxn = ddnn * g
        ddn = r3 * (dxn - xn * _rowmean(dxn * xn))
        ddnb = ddn.astype(BF16)
        ddn_ref[...] = ddnb
        for o in range(0, dff, CW_FFN):
            sl = slice(o, o + CW_FFN)
            dact = lax.dot_general(ddnb, wd_ref[sl, :], (((1,), (1,)), ((), ())), preferred_element_type=F32)
            si, vv, sg = a_s[:, sl], vv_s[:, sl], sg_s[:, sl]
            dhid_ref[:, sl] = (dact * vv * (sg + si * (1.0 - sg))).astype(BF16)
            dhid_ref[:, dff + o:dff + o + CW_FFN] = (dact * si).astype(BF16)

    return pl.pallas_call(
        body, name="ffn_down", grid=(t // tb,),
        scratch_shapes=[pltpu.VMEM((tb, dff), F32)] * 3,
        out_shape=(jax.ShapeDtypeStruct((t, dff), BF16), jax.ShapeDtypeStruct((t, d), BF16),
                   jax.ShapeDtypeStruct((t, d), F32), jax.ShapeDtypeStruct((t, nh), BF16),
                   jax.ShapeDtypeStruct((SUBLANES, d), F32), jax.ShapeDtypeStruct((SUBLANES, 128), F32)),
        in_specs=[_rows(tb, nh), _halo_prev(tb, nh), _full(fw.shape), _resident(w_down.shape),
                  _resident(w_down_t.shape), _rows(tb, d),
                  _rows(tb, d), _full(vd.shape)],
        out_specs=(_rows(tb, dff), _rows(tb, d), _rows(tb, d), _rows(tb, nh), _full((SUBLANES, d)),
                   _full((SUBLANES, 128))),
        compiler_params=_cparams(("arbitrary",), VMEM_BIG),
    )(up, up, fw, w_down, w_down_t, x2, tgt, vd)


def _ffn_up_bwd(dhid, up, fw, x2, dout, vec, w_up_st):
    t, nh = dhid.shape
    d = x2.shape[1]
    ns, _, nc = w_up_st.shape
    tb = _blk(t, TB_FFN)
    nblk = t // tb
    cw = 128

    def body(dh_ref, dhn_ref, up_ref, fw_ref, x2_ref, dout_ref, vec_ref, w_ref,
             dx2_ref, dup_ref, vp_ref, df_ref):
        i = pl.program_id(0)

        @pl.when(i == 0)
        def _():
            df_ref[...] = jnp.zeros(df_ref.shape, F32)
        dh2 = None
        for j in range(ns):
            for o in range(j * nc, (j + 1) * nc, cw):
                sl = slice(o, o + cw)
                dh = dh_ref[:, sl].astype(F32)
                dhn = jnp.where(i < nblk - 1, dhn_ref[:, sl].astype(F32), 0.0)
                dh1 = _shift_up(dh, dhn, 1)
                dh2s = _shift_up(dh, dhn, 2)
                dup_ref[:, sl] = (fw_ref[2:3, sl] * dh + fw_ref[1:2, sl] * dh1 + fw_ref[0:1, sl] * dh2s).astype(BF16)
                up_v = up_ref[:, sl]
                df_ref[0:1, sl] += _colsum(dh2s * up_v)
                df_ref[1:2, sl] += _colsum(dh1 * up_v)
                df_ref[2:3, sl] += _colsum(dh * up_v)
            pj = lax.dot_general(dup_ref[:, j * nc:(j + 1) * nc], w_ref[j], (((1,), (1,)), ((), ())),
                                 preferred_element_type=F32)
            dh2 = pj if dh2 is None else dh2 + pj
        xv = x2_ref[...]
        r = lax.rsqrt(_rowmean(xv * xv) + EPS)
        xn = xv * r
        g = vec_ref[0:1, :]
        hg = xn * g
        dhg = dh2 * vec_ref[1:2, :]
        _acc_rows(vp_ref, i == 0, [_colsum(dh2), _colsum(dh2 * hg), _colsum(dhg * xn)])
        dxn = dhg * g
        dx2_ref[...] = dout_ref[...] + r * (dxn - xn * _rowmean(dxn * xn))

    return pl.pallas_call(
        body, name="ffn_up_bwd", grid=(nblk,),
        out_shape=(jax.ShapeDtypeStruct((t, d), F32), jax.ShapeDtypeStruct((t, nh), BF16),
                   jax.ShapeDtypeStruct((SUBLANES, d), F32), jax.ShapeDtypeStruct((SUBLANES, nh), F32)),
        in_specs=[_rows(tb, nh), _halo_next(tb, nh, t, rows=BF16_ROWS), _rows(tb, nh), _full(fw.shape),
                  _rows(tb, d), _rows(tb, d), _full(vec.shape), _resident(w_up_st.shape)],
        out_specs=(_rows(tb, d), _rows(tb, nh), _full((SUBLANES, d)), _full((SUBLANES, nh))),
        compiler_params=_cparams(("arbitrary",), VMEM_BIG),
    )(dhid, dhid, up, fw, x2, dout, vec, w_up_st)


def _mix_out_bwd(dx2, o, y1, proj, s_re, s_im, c_re, c_im, v512, convw, glu_w, h16, h64, w_out, vd):
    t, d = dx2.shape
    dh = y1.shape[1]
    nstate = c_re.shape[0]
    du, ds = dh // SSM_SPLIT, nstate // SSM_SPLIT
    tb = _blk(t, TB_MIX)

    def body(dx2_ref, o_ref, y1_ref, u_ref, bg_ref, cg_ref, v_ref, cgh_ref, vh_ref, cre_ref, cim_ref, p_ref,
             cw_ref, gw_ref, h16_ref, h64_ref, wo_ref, vd_ref, sr_ref, si_ref,
             do_ref, ycat_ref, z_ref, dq_ref, dy1_ref, gr_ref, gi_ref, dcc_ref, dbg_ref, vpd_ref, vp5_ref,
             dcr_ref, dci_ref):
        i = pl.program_id(0)
        first = i == 0

        @pl.when(first)
        def _():
            dcr_ref[...] = jnp.zeros(dcr_ref.shape, F32)
            dci_ref[...] = jnp.zeros(dci_ref.shape, F32)
        ov = o_ref[...]
        ro = lax.rsqrt(_rowmean(ov * ov) + EPS)
        on_ = ov * ro
        g = vd_ref[0:1, :]
        dx2v = dx2_ref[...]
        don = dx2v * vd_ref[1:2, :]
        _acc_rows(vpd_ref, first, [_colsum(dx2v * on_ * g), _colsum(don * on_)])
        dxn = don * g
        dob = (ro * (dxn - on_ * _rowmean(dxn * on_))).astype(BF16)
        do_ref[...] = dob
        dyc_a =lax.dot_general(dob, wo_ref[0:dh, :], (((1,), (1,)), ((), ())), preferred_element_type=F32)
        dyc_b = lax.dot_general(dob, wo_ref[dh:2 * dh, :], (((1,), (1,)), ((), ())), preferred_element_type=F32)
        y1v = y1_ref[...]
        u = u_ref[...].astype(F32)
        z, dz_dy1 = _gelu_and_grad(y1v)
        zb = z.astype(BF16)
        sg = _sigmoid(jnp.dot(zb, gw_ref[...], preferred_element_type=F32) + p_ref[1:2, :])
        ya = z * sg
        ra = lax.rsqrt(_head_ms(ya, h16_ref) + EPS)
        yan = ya * ra
        ga = p_ref[2:3, :]
        ycat_ref[:, 0:dh] = (yan * ga).astype(BF16)
        dyn = dyc_a * ga
        dya = ra * (dyn - yan * _split_dot(dyn * yan, h16_ref[...]))
        dq = dya * z * sg * (1.0 - sg)
        dqb = dq.astype(BF16)
        z_ref[...] = zb
        dq_ref[...] = dqb
        dz = dya * sg + lax.dot_general(dqb, gw_ref[...], (((1,), (1,)), ((), ())), preferred_element_type=F32)
        dy1 = dz * dz_dy1
        dy1_ref[...] = dy1
        dy1b = dy1.astype(BF16)
        for q in range(SSM_SPLIT):
            rq, cq = slice(q * ds, (q + 1) * ds), slice(q * du, (q + 1) * du)
            gr_ref[:, rq] = lax.dot_general(dy1b[:, cq], cre_ref[rq, cq], (((1,), (1,)), ((), ())),
                                            preferred_element_type=F32).astype(BF16)
            gi_ref[:, rq] = (-lax.dot_general(dy1b[:, cq], cim_ref[rq, cq], (((1,), (1,)), ((), ())),
                                              preferred_element_type=F32)).astype(BF16)
            dcr_ref[rq, :] += _dot_tn(sr_ref[:, rq], dy1b[:, cq])
            dci_ref[rq, :] += _dot_tn(si_ref[:, rq], dy1b[:, cq])
        bg = bg_ref[...].astype(F32)
        cv = cg_ref[...].astype(F32) * v_ref[...].astype(F32)
        cvh = jnp.where(i > 0, cgh_ref[...].astype(F32) * vh_ref[...].astype(F32), 0.0)
        cv1 = _shift_down(cv, cvh, 1)
        cv2 = _shift_down(cv, cvh, 2)
        cc = cw_ref[0:1, :] * cv2 + cw_ref[1:2, :] * cv1 + cw_ref[2:3, :] * cv
        yb = bg * cc
        rb = lax.rsqrt(_head_ms(yb, h64_ref) + EPS)
        ybn = yb * rb
        gb = p_ref[3:4, :]
        ycat_ref[:, dh:2 * dh] = (ybn * gb).astype(BF16)
        dynb = dyc_b * gb
        dyb = rb * (dynb - ybn * _split_dot(dynb * ybn, h64_ref[...]))
        dcc = dyb * bg
        dbg_ref[...] = dyb * cc
        dcc_ref[...] = dcc
        _acc_rows(vp5_ref, first, [_colsum(dyc_a * yan), _colsum(dyc_b * ybn), _colsum(dq), _colsum(dy1 * u),
                                   _colsum(dcc * cv2), _colsum(dcc * cv1), _colsum(dcc * cv)])

    return pl.pallas_call(
        body, name="mix_out_bwd", grid=(t // tb,),
        out_shape=(jax.ShapeDtypeStruct((t, d), BF16), jax.ShapeDtypeStruct((t, 2 * dh), BF16),
                   jax.ShapeDtypeStruct((t, dh), BF16), jax.ShapeDtypeStruct((t, dh), BF16),
                   jax.ShapeDtypeStruct((t, dh), F32), jax.ShapeDtypeStruct((t, nstate), BF16),
                   jax.ShapeDtypeStruct((t, nstate), BF16), jax.ShapeDtypeStruct((t, dh), F32),
                   jax.ShapeDtypeStruct((t, dh), F32), jax.ShapeDtypeStruct((SUBLANES, d), F32),
                   jax.ShapeDtypeStruct((SUBLANES, dh), F32), jax.ShapeDtypeStruct((nstate, du), F32),
                   jax.ShapeDtypeStruct((nstate, du), F32)),
        in_specs=[_rows(tb, d), _rows(tb, d), _rows(tb, dh), _rows(tb, dh, 0), _rows(tb, dh, 1), _rows(tb, dh, 2),
                  _rows(tb, dh, 3), _halo_prev(tb, dh, 2, BF16_ROWS), _halo_prev(tb, dh, 3, BF16_ROWS), _resident(c_re.shape),
                  _resident(c_im.shape), _full(v512.shape), _full(convw.shape), _resident(glu_w.shape),
                  _resident(h16.shape), _resident(h64.shape), _resident(w_out.shape), _full(vd.shape),
                  _rows(tb, nstate), _rows(tb, nstate)],
        out_specs=(_rows(tb, d), _rows(tb, 2 * dh), _rows(tb, dh), _rows(tb, dh), _rows(tb, dh), _rows(tb, nstate),
                   _rows(tb, nstate), _rows(tb, dh), _rows(tb, dh), _full((SUBLANES, d)), _full((SUBLANES, dh)),
                   _full((nstate, du)), _full((nstate, du))),
        compiler_params=_cparams(("arbitrary",), VMEM_BIG),
    )(dx2, o, y1, proj, proj, proj, proj, proj, proj, c_re, c_im, v512, convw, glu_w, h16, h64, w_out, vd,
      s_re, s_im)


def _mix_in_bwd(gt_re, gt_im, b_re, b_im, dy1, dcc, dbg, proj, x, dx2, vec, v512, convw, w_in_st):
    t, d = x.shape
    dh = dy1.shape[1]
    nstate = gt_re.shape[1]
    du_w, ds = dh // SSM_SPLIT, nstate // SSM_SPLIT
    ns, _, nc = w_in_st.shape
    tb = _blk(t, TB_MIX)
    nblk = t // tb

    def body(gr_ref, gi_ref, bre_ref, bim_ref, dy1_ref, dcc_ref, dccn_ref, dbg_ref, u_ref, cg_ref, v_ref, x_ref,
             dx2_ref, vec_ref, p_ref, cw_ref, w_ref, gx_ref, dproj_ref, vp_ref, dbr_ref, dbi_ref):
        i = pl.program_id(0)

        @pl.when(i == 0)
        def _():
            dbr_ref[...] = jnp.zeros(dbr_ref.shape, F32)
            dbi_ref[...] = jnp.zeros(dbi_ref.shape, F32)
        ub = u_ref[...].astype(BF16)
        du = []
        for q in range(SSM_SPLIT):
            rq, cq = slice(q * du_w, (q + 1) * du_w), slice(q * ds, (q + 1) * ds)
            du.append(lax.dot_general(gr_ref[:, cq].astype(BF16), bre_ref[rq, cq], (((1,), (1,)), ((), ())),
                                      preferred_element_type=F32)
                      + lax.dot_general(gi_ref[:, cq].astype(BF16), bim_ref[rq, cq], (((1,), (1,)), ((), ())),
                                        preferred_element_type=F32))
            dbr_ref[rq, :] += _dot_tn(ub[:, rq], gr_ref[:, cq])
            dbi_ref[rq, :] += _dot_tn(ub[:, rq], gi_ref[:, cq])
        du = dy1_ref[...] * p_ref[0:1, :] + jnp.concatenate(du, axis=1)
        dcc = dcc_ref[...]
        dccn = jnp.where(i < nblk - 1, dccn_ref[...], 0.0)
        dcv = (cw_ref[2:3, :] * dcc + cw_ref[1:2, :] * _shift_up(dcc, dccn, 1)
               + cw_ref[0:1, :] * _shift_up(dcc, dccn, 2))
        parts = [du, dbg_ref[...], dcv * v_ref[...].astype(F32), dcv * cg_ref[...].astype(F32)]
        xv = x_ref[...]
        r = lax.rsqrt(_rowmean(xv * xv) + EPS)
        xn = xv * r
        g = vec_ref[0:1, :]
        hg = xn * g
        dh1 = None
        for j in range(ns):
            pb = parts[j].astype(BF16)
            dproj_ref[:, j * nc:(j + 1) * nc] = pb
            pj =lax.dot_general(pb, w_ref[j], (((1,), (1,)), ((), ())), preferred_element_type=F32)
            dh1 = pj if dh1 is None else dh1 + pj
        dhg = dh1 * vec_ref[1:2, :]
        _acc_rows(vp_ref, i == 0, [_colsum(dh1), _colsum(dh1 * hg), _colsum(dhg * xn)])
        dxn = dhg * g
        gx_ref[...] = dx2_ref[...] + r * (dxn - xn * _rowmean(dxn * xn))

    assert nc == dh and ns == 4
    return pl.pallas_call(
        body, name="mix_in_bwd", grid=(nblk,),
        out_shape=(jax.ShapeDtypeStruct((t, d), F32), jax.ShapeDtypeStruct((t, ns * nc), BF16),
                   jax.ShapeDtypeStruct((SUBLANES, d), F32), jax.ShapeDtypeStruct((dh, ds), F32),
                   jax.ShapeDtypeStruct((dh, ds), F32)),
        in_specs=[_rows(tb, nstate), _rows(tb, nstate), _resident(b_re.shape), _resident(b_im.shape), _rows(tb, dh),
                  _rows(tb, dh), _halo_next(tb, dh, t), _rows(tb, dh), _rows(tb, dh, 0), _rows(tb, dh, 2),
                  _rows(tb, dh, 3), _rows(tb, d), _rows(tb, d), _full(vec.shape), _full(v512.shape),
                  _full(convw.shape), _resident(w_in_st.shape)],
        out_specs=(_rows(tb, d), _rows(tb, ns * nc), _full((SUBLANES, d)), _full((dh, ds)), _full((dh, ds))),
        compiler_params=_cparams(("arbitrary",), VMEM_BIG),
    )(gt_re, gt_im, b_re, b_im, dy1, dcc, dcc, dbg, proj, proj, proj, x, dx2, vec, v512, convw, w_in_st)


def _matmul_tn(a, b, m, bn, out_dtype, name, diag=False, bt=TB_TN, after=None):
    t = a.shape[0]
    n = b.shape[1]
    bt = _blk(t, bt)
    nk = t // bt
    extra = [] if after is None else [after]
    a_map = (lambda j, k: (k, j)) if diag else (lambda j, k: (k, 0))

    def body(a_ref, b_ref, *rest):
        o_ref, acc_ref = rest[-2:]
        k = pl.program_id(1)

        @pl.when(k == 0)
        def _():
            acc_ref[...] = jnp.zeros(acc_ref.shape, F32)
        acc_ref[...] += _dot_tn(a_ref[...], b_ref[...])

        @pl.when(k == nk - 1)
        def _():
            o_ref[...] = acc_ref[...].astype(out_dtype)

    return pl.pallas_call(
        body, name=name, grid=(n // bn, nk),
        out_shape=jax.ShapeDtypeStruct((n // bn, m, bn), out_dtype),
        in_specs=[pl.BlockSpec((bt, m), a_map), pl.BlockSpec((bt, bn), lambda j, k: (k, j))]
        + [pl.BlockSpec(memory_space=pl.ANY)] * len(extra),
        out_specs=pl.BlockSpec((None, m, bn), lambda j, k: (j, 0, 0)),
        scratch_shapes=[pltpu.VMEM((m, bn), F32)],
        compiler_params=_cparams(("parallel", "arbitrary"), VMEM_BIG),
    )(a, b, *extra)


def _ssm_bgrad(d_bre, d_bim, bt_re, bt_im, rows_in, fold, tile_b):
    gh, cb = d_bre.shape
    nb = SSM_SPLIT
    rb = gh // nb
    gp = nb * cb
    p = fold.shape[1]

    def body(dr_ref, di_ref, br_ref, bi_ref, rin_ref, f_ref, tb_ref, dbr_ref, dbi_ref, rout_ref):
        row = lax.broadcasted_iota(jnp.int32, (rb, cb), 0)
        col = lax.broadcasted_iota(jnp.int32, (rb, cb), 1)
        mask = (row >> 4) == (col >> 6)
        gr = jnp.where(mask, dr_ref[...], 0.0)
        gi = jnp.where(mask, di_ref[...], 0.0)
        cr, ci = rin_ref[0:1, :], rin_ref[1:2, :]
        dbr_ref[...] = _split3_dot(cr * gr + ci * gi, f_ref[...])
        dbi_ref[...] = _split3_dot(cr * gi - ci * gr, f_ref[...])
        br = _split3_dot(br_ref[...], tb_ref[...])
        bi = _split3_dot(bi_ref[...], tb_ref[...])
        rout_ref[...] = jnp.zeros(rout_ref.shape, F32)
        rout_ref[0:1, :] = _colsum(br * gr + bi * gi)
        rout_ref[1:2, :] = _colsum(br * gi - bi * gr)

    dspec = pl.BlockSpec((rb, cb), lambda j: (j, 0))
    rspec = pl.BlockSpec((SUBLANES, cb), lambda j: (0, j))
    ospec = pl.BlockSpec((rb, p), lambda j: (j, 0))
    return pl.pallas_call(
        body, name="ssm_bgrad", grid=(nb,),
        out_shape=(jax.ShapeDtypeStruct((gh, p), F32), jax.ShapeDtypeStruct((gh, p), F32),
                   jax.ShapeDtypeStruct((SUBLANES, gp), F32)),
        in_specs=[dspec, dspec, ospec, ospec, rspec, _full(fold.shape), _full(tile_b.shape)],
        out_specs=(ospec, ospec, rspec),
        compiler_params=_cparams(("parallel",)),
    )(d_bre, d_bim, bt_re, bt_im, rows_in, fold, tile_b)


def _ssm_cgrad(d_cre, d_cim, fold):
    gp, cb = d_cre.shape
    nb = SSM_SPLIT
    rb = gp // nb
    h = fold.shape[1]

    def body(dr_ref, di_ref, f_ref, cr_ref, ci_ref):
        row = lax.broadcasted_iota(jnp.int32, (rb, cb), 0)
        col = lax.broadcasted_iota(jnp.int32, (rb, cb), 1)
        mask = (row >> 6) == (col >> 4)
        cr_ref[...] = _split3_dot(jnp.where(mask, dr_ref[...], 0.0), f_ref[...])
        ci_ref[...] = -_split3_dot(jnp.where(mask, di_ref[...], 0.0), f_ref[...])

    cspec = pl.BlockSpec((rb, cb), lambda j: (j, 0))
    ospec = pl.BlockSpec((rb, h), lambda j: (j, 0))
    return pl.pallas_call(
        body, name="ssm_cgrad", grid=(nb,),
        out_shape=(jax.ShapeDtypeStruct((gp, h), F32),) * 2,
        in_specs=[cspec, cspec, _full(fold.shape)], out_specs=(ospec, ospec),
        compiler_params=_cparams(("parallel",)),
    )(d_cre, d_cim, fold)


def _ssm_lamgrad(lam_re, lam_im, log_step, abar_re, abar_im, coef_re, coef_im, gc_re, gc_im, ga_re, ga_im):
    g, p = lam_re.shape

    def body(lr_ref, li_ref, ls_ref, ar_ref, ai_ref, cr_ref, ci_ref, gcr_ref, gci_ref, gar_ref, gai_ref,
             dlr_ref, dli_ref, dls_ref):
        lam_raw = lr_ref[...]
        lr = jnp.minimum(lam_raw, LAMBDA_RE_MAX)
        li = li_ref[...]
        st = jnp.exp(ls_ref[...])
        den = lr * lr + li * li
        gcr, gci = gcr_ref[...], gci_ref[...]
        gab_r = gar_ref[...] + (lr * gcr - li * gci) / den
        gab_i = gai_ref[...] + (lr * gci + li * gcr) / den
        cr, ci = cr_ref[...], ci_ref[...]
        wr = -(cr * lr + ci * li) / den
        wi = -(ci * lr - cr * li) / den
        gl_r = wr * gcr + wi * gci
        gl_i = wr * gci - wi * gcr
        ar, ai = ar_ref[...], ai_ref[...]
        gw_r = ar * gab_r + ai * gab_i
        gw_i = ar * gab_i - ai * gab_r
        gl_r = gl_r + st * gw_r
        gl_i = gl_i + st * gw_i
        pass_through = jnp.where(lam_raw < LAMBDA_RE_MAX, 1.0, jnp.where(lam_raw == LAMBDA_RE_MAX, 0.5, 0.0))
        dlr_ref[...] = gl_r * pass_through
        dli_ref[...] = gl_i
        dls_ref[...] = st * jnp.sum(lr * gw_r + li * gw_i, axis=1, keepdims=True)

    sds = jax.ShapeDtypeStruct((g, p), F32)
    return pl.pallas_call(body, name="ssm_lamgrad", out_shape=(sds, sds, jax.ShapeDtypeStruct((g, 1), F32)))(
        lam_re, lam_im, log_step, abar_re, abar_im, coef_re, coef_im, gc_re, gc_im, ga_re, ga_im)


def _row_block(r, most=512):
    for rb in range(min(r, most), BF16_ROWS - 1, -1):
        if r % rb == 0 and rb % BF16_ROWS == 0:
            return rb
    return r


def _adamw_math(w, g, m, v):
    m = ADAM_B1 * m + (1.0 - ADAM_B1) * g
    v = ADAM_B2 * v + (1.0 - ADAM_B2) * (g * g)
    m_hat = m / (1.0 - ADAM_B1 ** ADAM_STEP)
    v_hat = v / (1.0 - ADAM_B2 ** ADAM_STEP)
    delta = -ADAM_LR * (m_hat / (jnp.sqrt(v_hat) + ADAM_EPS) + ADAM_WD * w)
    return delta, m, v


def _adamw_big(p_mine, p_sib, w, m, v, name):
    r, c = w.shape
    rb = _row_block(r)

    def body(a_ref, b_ref, w_ref, m_ref, v_ref, g_ref, d_ref, mo_ref, vo_ref):
        g = a_ref[...].astype(F32) + b_ref[...].astype(F32)
        g_ref[...] = g
        d_ref[...], mo_ref[...], vo_ref[...] = _adamw_math(w_ref[...], g, m_ref[...], v_ref[...])

    spec = pl.BlockSpec((rb, c), lambda i: (i, 0))
    sds = jax.ShapeDtypeStruct((r, c), F32)
    return pl.pallas_call(
        body, name=name, grid=(r // rb,), out_shape=(sds,) * 4, in_specs=[spec] * 5, out_specs=(spec,) * 4,
        compiler_params=_cparams(("parallel",), VMEM_KEEP_OPERANDS_IN_HBM),
    )(p_mine, p_sib, w, m, v)


def _sum_blocks(stack, name):
    n, r, c = stack.shape
    rb = _row_block(r)

    def body(s_ref, o_ref):
        acc = s_ref[0].astype(F32)
        for k in range(1, n):
            acc = acc + s_ref[k].astype(F32)
        o_ref[...] = acc

    return pl.pallas_call(
        body, name=name, grid=(r // rb,), out_shape=jax.ShapeDtypeStruct((r, c), F32),
        in_specs=[pl.BlockSpec((n, rb, c), lambda i: (0, i, 0))], out_specs=pl.BlockSpec((rb, c), lambda i: (i, 0)),
        compiler_params=_cparams(("parallel",), VMEM_KEEP_OPERANDS_IN_HBM),
    )(stack)


def _sum_landed(landed, own, chip, name):
    n, r, c = landed.shape
    rb = _row_block(r)

    def body(chip_ref, own_ref, l1_ref, l2_ref, l3_ref, o_ref):
        acc = own_ref[0].astype(F32)
        for ref in (l1_ref, l2_ref, l3_ref):
            acc = acc + ref[0].astype(F32)
        o_ref[...] = acc.astype(BF16)

    def slot(k):
        return pl.BlockSpec((1, rb, c), lambda i, ch: ((ch[0] + k) % n, i, 0))

    return pl.pallas_call(
        body, name=name, out_shape=jax.ShapeDtypeStruct((r, c), BF16),
        grid_spec=pltpu.PrefetchScalarGridSpec(
            num_scalar_prefetch=1, grid=(r // rb,), in_specs=[slot(0), slot(1), slot(2), slot(3)],
            out_specs=pl.BlockSpec((rb, c), lambda i, ch: (i, 0))),
        compiler_params=_cparams(("parallel",), VMEM_KEEP_OPERANDS_IN_HBM),
    )(jnp.reshape(chip, (1,)).astype(jnp.int32), own, landed, landed, landed)


def _add2(a, b):
    def body(a_ref, b_ref, o_ref):
        o_ref[...] = a_ref[...] + b_ref[...]

    return pl.pallas_call(body, name="add_small", out_shape=jax.ShapeDtypeStruct(a.shape, F32))(a, b)


def _adamw_ada(c_all, dmod_cols, w, m, v):
    d, n = w.shape
    bn = 512

    def body(c_ref, dm_ref, w_ref, m_ref, v_ref, g_ref, d_ref, mo_ref, vo_ref):
        cc = c_ref[...]
        g = _dot_tn(cc * _sigmoid(cc), dm_ref[...])
        g_ref[...] = g
        d_ref[...], mo_ref[...], vo_ref[...] = _adamw_math(w_ref[...], g, m_ref[...], v_ref[...])

    spec = pl.BlockSpec((d, bn), lambda j: (0, j))
    sds = jax.ShapeDtypeStruct((d, n), F32)
    return pl.pallas_call(
        body, name="adamw_ada", grid=(n // bn,), out_shape=(sds,) * 4,
        in_specs=[_full((N_DEV, d)), pl.BlockSpec((N_DEV, bn), lambda j: (0, j)), spec, spec, spec],
        out_specs=(spec,) * 4, compiler_params=_cparams(("parallel",), VMEM_KEEP_OPERANDS_IN_HBM),
    )(c_all, dmod_cols, w, m, v)


def _adamw_small(items):
    n = len(items)

    def body(*refs):
        ins, outs = refs[:4 * n], refs[4 * n:]
        for k in range(n):
            w_ref, g_ref, m_ref, v_ref = ins[4 * k:4 * k + 4]
            outs[3 * k][...], outs[3 * k + 1][...], outs[3 * k + 2][...] = _adamw_math(
                w_ref[...], g_ref[...], m_ref[...], v_ref[...])

    flat = [a for it in items for a in it]
    out_shape = tuple(jax.ShapeDtypeStruct(it[0].shape, F32) for it in items for _ in range(3))
    res = pl.pallas_call(body, name="adamw_small", out_shape=out_shape,
                         compiler_params=_cparams(vmem=VMEM_KEEP_OPERANDS_IN_HBM))(*flat)
    return [tuple(res[3 * k:3 * k + 3]) for k in range(n)]


def _group_mean_matrix(n, group):
    idx = np.arange(n) // group
    return (idx[:, None] == idx[None, :]).astype(np.float32) / group


def _fold_matrix(n, period):
    return (np.arange(n)[:, None] % period == np.arange(period)[None, :]).astype(np.float32)


def _rows8(*rows):
    c = rows[0].shape[-1]
    pad = jnp.zeros((SUBLANES - len(rows), c), F32)
    return jnp.concatenate([r.reshape(1, c) for r in rows] + [pad], axis=0)


def _to_rows(a, width):
    flat = a.reshape(-1)
    n = -(-flat.shape[0] // width)
    flat = jnp.pad(flat, (0, n * width - flat.shape[0]))
    return flat.reshape(n, width)


def kernel(x, c, w_ada, b_ada, g_pre_mix, g_post_mix, w_in, ssm_lam_re, ssm_lam_im, ssm_log_step, ssm_b_re, ssm_b_im, ssm_c_re, ssm_c_im, ssm_d, glu_w, glu_b, g_out_ssm, conv_w, g_out_conv, w_out, g_pre_ffn, g_post_ffn, w_up, ffn_conv_w, w_down, loss_target, m_w_ada, m_b_ada, m_g_pre_mix, m_g_post_mix, m_w_in, m_ssm_lam_re, m_ssm_lam_im, m_ssm_log_step, m_ssm_b_re, m_ssm_b_im, m_ssm_c_re, m_ssm_c_im, m_ssm_d, m_glu_w, m_glu_b, m_g_out_ssm, m_conv_w, m_g_out_conv, m_w_out, m_g_pre_ffn, m_g_post_ffn, m_w_up, m_ffn_conv_w, m_w_down, v_w_ada, v_b_ada, v_g_pre_mix, v_g_post_mix, v_w_in, v_ssm_lam_re, v_ssm_lam_im, v_ssm_log_step, v_ssm_b_re, v_ssm_b_im, v_ssm_c_re, v_ssm_c_im, v_ssm_d, v_glu_w, v_glu_b, v_g_out_ssm, v_conv_w, v_g_out_conv, v_w_out, v_g_pre_ffn, v_g_post_ffn, v_w_up, v_ffn_conv_w, v_w_down):
    xs = x[0]
    tgt = loss_target[0]
    t, d = xs.shape
    xi, yi, ci = lax.axis_index("x"), lax.axis_index("y"), lax.axis_index("c")
    chip = 2 * xi + yi
    dev = 2 * chip + ci

    n_groups, n_state = ssm_lam_re.shape[1:]
    n_gch = ssm_b_re.shape[3]
    d_ssm = n_groups * n_gch
    gp = n_groups * n_state
    n_ada = w_ada.shape[2]
    d_ff = w_down.shape[1] * N_CHIPS
    n_upc = w_up.shape[2]

    w_names = ("w_in", "glu_w", "w_out", "w_up", "w_down")
    c_gath = _allgather8(jnp.broadcast_to(c, (SUBLANES, d)), "gather_c")
    c_all = c_gath.reshape(N_DEV, SUBLANES, d)[:, 0, :]

    def pad8(a):
        return jnp.concatenate([a, jnp.zeros((SUBLANES - a.shape[0], a.shape[1]), a.dtype)], axis=0)

    def start(name, arrs, after):
        return _chips_start(name, True, [], [_landing(a, chip) for a in arrs], after)

    w_names = ("w_in", "mod", "conv_w", "ffn_conv_w", "glu_w", "w_out", "w_up", "w_down")
    first = start("weights_start_in", [w_in[0].astype(BF16)], c_gath)
    b_sh = lax.dynamic_slice(b_ada, (0, chip * n_ada), (1, n_ada))
    mod_sh = _mod_shard(c_all + first[4][0:1, 0:1], w_ada[0], b_sh)
    second = start("weights_start_rest", [mod_sh, pad8(conv_w[0]), pad8(ffn_conv_w[0])]
                   + [w[0].astype(BF16) for w in (glu_w, w_out, w_up, w_down)], None)
    w_send, w_recv, w_land = [list(first[k]) + list(second[k]) for k in (0, 1, 3)]
    w_token = second[4]

    def weights(names, after):
        ks = [w_names.index(nm) for nm in names]
        return _chips_wait("weights_wait_" + names[-1], True, [w_send[k] for k in ks], [w_recv[k] for k in ks],
                           [], [w_land[k] for k in ks], after)[1]

    lam_re, lam_im = ssm_lam_re[0], ssm_lam_im[0]
    log_step = ssm_log_step[0].reshape(n_groups, 1) + w_token[0:1, 0:1]
    abar_re, abar_im, coef_re, coef_im = _ssm_prep(lam_re, lam_im, log_step)
    a_rows = _rows8(abar_re.reshape(1, gp), abar_im.reshape(1, gp))
    coef_rows = _rows8(coef_re.reshape(1, gp), coef_im.reshape(1, gp))
    bt_re = ssm_b_re[0].transpose(0, 2, 1).reshape(d_ssm, n_state)
    bt_im = ssm_b_im[0].transpose(0, 2, 1).reshape(d_ssm, n_state)
    ct_re = ssm_c_re[0].transpose(0, 2, 1).reshape(gp, n_gch)
    ct_im = ssm_c_im[0].transpose(0, 2, 1).reshape(gp, n_gch)
    tile_b = jnp.asarray(np.tile(np.eye(n_state), (1, n_groups // SSM_SPLIT)), BF16)
    tile_c = jnp.asarray(np.tile(np.eye(n_gch), (1, n_groups)), BF16)
    bblk_re, bblk_im, cblk_re, cblk_im = _ssm_blocks(bt_re, bt_im, ct_re, ct_im, coef_rows, tile_b, tile_c)

    h16 = jnp.asarray(_group_mean_matrix(d_ssm, n_gch), BF16)
    h64 = jnp.asarray(_group_mean_matrix(d_ssm, CONV_HEAD_DIM), BF16)

    g_mod, g_cw, g_fw, w_in_st = weights(("mod", "conv_w", "ffn_conv_w", "w_in"), bblk_re)
    mod_all = g_mod.transpose(1, 0, 2).reshape(N_DEV, N_CHIPS * n_ada)
    mod = lax.dynamic_slice(mod_all, (dev, 0), (1, N_CHIPS * n_ada))
    sh1, sc1, gt1, sh2, sc2, gt2 = [mod[:, k * d:(k + 1) * d] for k in range(6)]
    convw_full = pad8(g_cw[:, :3, :].transpose(1, 0, 2).reshape(3, d_ssm))
    fw_full = pad8(g_fw[:, :3, :].transpose(1, 0, 2).reshape(3, N_CHIPS * n_upc))

    v512 = _rows8(ssm_d, glu_b, g_out_ssm, g_out_conv)
    vec1 =_rows8(g_pre_mix, 1.0 + sc1, sh1)
    vd1 = _rows8(g_post_mix, gt1)
    vec2 = _rows8(g_pre_ffn, 1.0 + sc2, sh2)
    vd2 = _rows8(g_post_ffn, gt2)

    proj, bu_re, bu_im, h1b = _mix_in(xs, vec1, w_in_st, bblk_re, bblk_im)
    s_re, s_im = _scan_fwd(a_rows, bu_re, bu_im)
    g_glu, g_wout = weights(("glu_w", "w_out"), s_re)
    glu_full = g_glu.reshape(d_ssm, d_ssm)
    w_out_full = g_wout.reshape(2 * d_ssm, d)
    y1, o_mix, x2 = _mix_out(xs, proj, s_re, s_im, cblk_re, cblk_im, v512, convw_full, glu_full, h16, h64,
                             w_out_full, vd1)
    (w_up_st,) = weights(("w_up",), x2)
    up, h2b = _ffn_up(x2, vec2, w_up_st)
    (g_wdown,) = weights(("w_down",), up)
    w_down_full = g_wdown.reshape(d_ff, d)
    actb, ddnb, dout, dhid, vp_dn, loss_blk = _ffn_down(up, fw_full, w_down_full, w_down_full.T, x2, tgt, vd2)

    gw_down = _matmul_tn(actb, ddnb, d_ff, d, BF16, "dw_down", bt=1024).reshape(N_CHIPS, d_ff // N_CHIPS, d)
    dx2, dupb, vp_up, df_rows = _ffn_up_bwd(dhid, up, fw_full, x2, dout, vec2, w_up_st)
    gw_up = _matmul_tn(h2b, dupb, d, n_upc, BF16, "dw_up", bt=4096)
    ga_send, ga_recv, ga_src, ga_land, ga_token = _chips_start(
        "grads_start_ffn", False, [gw_down, gw_up], [lax.empty(g.shape, g.dtype) for g in (gw_down, gw_up)])
    (dob, ycatb, zb, dqb, dy1, g_re, g_im, dcc, dbg, vp_mo, vp5, d_cre, d_cim) = _mix_out_bwd(
        dx2, o_mix, y1, proj, s_re, s_im, cblk_re, cblk_im, v512, convw_full, glu_full, h16, h64, w_out_full,
        vd1 + ga_token[0:1, 0:1])
    gw_out = _matmul_tn(ycatb, dob, 2 * d_ssm, d, BF16, "dw_out", bt=4096)
    gw_out = gw_out.reshape(N_CHIPS, 2 * d_ssm // N_CHIPS, d)
    gw_glu = _matmul_tn(zb, dqb, d_ssm, d_ssm, BF16, "dw_glu", bt=4096).reshape(N_CHIPS, d_ssm // N_CHIPS, d_ssm)
    gb_send, gb_recv, gb_src, gb_land, gb_token = _chips_start(
        "grads_start_mix", False, [gw_out, gw_glu], [lax.empty(g.shape, g.dtype) for g in (gw_out, gw_glu)])
    gt_re, gt_im, ga_re8, ga_im8 = _scan_bwd(a_rows + gb_token[0:1, 0:1], g_re, g_im, s_re, s_im)
    grad_x, dprojb, vp_mi, d_bre, d_bim = _mix_in_bwd(gt_re, gt_im, bblk_re, bblk_im, dy1, dcc, dbg, proj, xs, dx2,
                                                      vec1, v512, convw_full, w_in_st)
    ssm_u, ssm_s = d_ssm // SSM_SPLIT, gp // SSM_SPLIT

    fold_b = jnp.asarray(_fold_matrix(ssm_s, n_state), BF16)
    fold_c = jnp.asarray(_fold_matrix(ssm_u, n_gch), BF16)
    db_re_f, db_im_f, gc_rows = _ssm_bgrad(d_bre, d_bim, bt_re, bt_im, coef_rows, fold_b, tile_b)
    dc_re_f, dc_im_f = _ssm_cgrad(d_cre, d_cim, fold_c)
    ga_sum = _ga_rowsum(ga_re8, ga_im8)
    g_lam_re, g_lam_im, g_log_step = _ssm_lamgrad(
        lam_re, lam_im, log_step, abar_re, abar_im, coef_re, coef_im,
        gc_rows[0].reshape(n_groups, n_state), gc_rows[1].reshape(n_groups, n_state),
        ga_sum[0].reshape(n_groups, n_state), ga_sum[1].reshape(n_groups, n_state))
    g_b_re = db_re_f.reshape(n_groups, n_gch, n_state).transpose(0, 2, 1)
    g_b_im = db_im_f.reshape(n_groups, n_gch, n_state).transpose(0, 2, 1)
    g_c_re = dc_re_f.reshape(n_groups, n_state, n_gch).transpose(0, 2, 1)
    g_c_im = dc_im_f.reshape(n_groups, n_state, n_gch).transpose(0, 2, 1)

    dmod = jnp.concatenate([vp_mi[0:1], vp_mi[1:2], vp_mo[0:1], vp_up[0:1], vp_up[1:2], vp_dn[0:1]], axis=1)
    small = [
        ("g_pre_mix", vp_mi[2:3]), ("g_post_mix", vp_mo[1:2]), ("g_pre_ffn", vp_up[2:3]), ("g_post_ffn", vp_dn[1:2]),
        ("ssm_lam_re", g_lam_re), ("ssm_lam_im", g_lam_im), ("ssm_log_step", g_log_step),
        ("ssm_b_re", g_b_re), ("ssm_b_im", g_b_im), ("ssm_c_re", g_c_re), ("ssm_c_im", g_c_im),
        ("ssm_d", vp5[3:4]), ("glu_b", vp5[2:3]), ("g_out_ssm", vp5[0:1]), ("g_out_conv", vp5[1:2]),
        ("conv_w", vp5[4:7]), ("ffn_conv_w", df_rows[0:3]), ("loss", loss_blk[0:1, 0:1]),
    ]
    packed, offsets, row = [], {}, 0
    for name, a in small:
        r = _to_rows(a, d)
        offsets[name] = (row, a.shape)
        packed.append(r)
        row += r.shape[0]
    n_small = -(-row // SUBLANES) * SUBLANES
    packed.append(jnp.zeros((n_small - row, d), F32))
    packed.append(pad8(dmod.reshape(6, d)))
    pack = jnp.concatenate(packed, axis=0)
    sm_send, sm_recv, _, sm_land, sm_token = _chips_start("small_start", True, [], [_landing(pack, chip)])

    gw_in = _matmul_tn(h1b, dprojb, d, w_in.shape[2], BF16, "dw_in", bt=4096, after=sm_token)
    gc_send, gc_recv, gc_src, gc_land, gc_token = _chips_start(
        "grads_start_in", False, [gw_in], [lax.empty(gw_in.shape, gw_in.dtype)])

    def partials(names, own, landed):
        return [_sum_landed(l, o, chip, "sum_" + nm) for l, o, nm in zip(landed, own, names)]

    def update(names, mine, theirs):
        done = {}
        for nm, pm, ps in zip(names, mine, theirs):
            w_, m_, v_ = big_params[nm]
            done[nm] = _adamw_big(pm, ps, w_[0], m_[0], v_[0], "adamw_" + nm)
        return done

    big_params = {"w_down": (w_down, m_w_down, v_w_down), "w_up": (w_up, m_w_up, v_w_up),
                  "w_out": (w_out, m_w_out, v_w_out), "glu_w": (glu_w, m_glu_w, v_glu_w),
                  "w_in": (w_in, m_w_in, v_w_in)}
    ffn_names, mix_names = ("w_down", "w_up"), ("w_out", "glu_w", "w_in")
    p_ffn = partials(ffn_names, *_chips_wait("grads_wait_ffn", False, ga_send, ga_recv, ga_src, ga_land, gc_token))
    sa_send, sa_recv, sa_src, sa_land, sa_token = _sibling_start("swap_start_ffn", p_ffn)

    (sm_landed,) = _chips_wait("small_wait", True, sm_send, sm_recv, [], sm_land, sa_token)[1]
    sm_part = _sum_blocks(sm_landed, "sum_small")
    dmod_mine = sm_landed[:, n_small:n_small + SUBLANES, :]
    ss_send, ss_recv, ss_src, ss_land, ss_token = _sibling_start("swap_start_small", [sm_part, dmod_mine])
    p_ffn, t_ffn = _sibling_wait("swap_wait_ffn", sa_send, sa_recv, sa_src, sa_land, ss_token)
    big = update(ffn_names, p_ffn, t_ffn)
    (sm_part, dmod_mine), (sm_sib, dmod_sib) = _sibling_wait("swap_wait_small", ss_send, ss_recv, ss_src, ss_land,
                                                              big["w_up"][0])
    sums = _add2(sm_part, sm_sib)
    dmod_by_core = jnp.stack([dmod_mine, dmod_sib], axis=1)
    dmod_by_core = jnp.where(ci == 0, dmod_by_core, dmod_by_core[:, ::-1])
    dmod_all = dmod_by_core[:, :, :6, :].reshape(N_DEV, 6 * d)
    g_b_ada = sums[n_small:n_small + 6].reshape(1, 6 * d)

    def unpack(name):
        r0, shape = offsets[name]
        size = math.prod(shape)
        nrow = -(-size // d)
        return sums[r0:r0 + nrow].reshape(-1)[:size].reshape(shape)

    p_mix = partials(mix_names, *_chips_wait(
        "grads_wait_mix", False, list(gb_send) + list(gc_send), list(gb_recv) + list(gc_recv),
        list(gb_src) + list(gc_src), list(gb_land) + list(gc_land), sums))
    sb_send, sb_recv, sb_src, sb_land, sb_token = _sibling_start("swap_start_mix", p_mix)

    dmod_cols = lax.dynamic_slice(dmod_all, (0, chip * n_ada), (N_DEV, n_ada)) + sb_token[0:1, 0:1]
    ada = _adamw_ada(c_all, dmod_cols, w_ada[0], m_w_ada[0], v_w_ada[0])
    p_mix, t_mix = _sibling_wait("swap_wait_mix", sb_send, sb_recv, sb_src, sb_land, ada[0])
    big.update(update(mix_names, p_mix, t_mix))

    g_small = {name: unpack(name) for name, _ in small}
    g_small["b_ada"] = g_b_ada
    g_small["conv_w"] = lax.dynamic_slice(g_small["conv_w"], (0, chip * conv_w.shape[2]), (3, conv_w.shape[2]))
    g_small["ffn_conv_w"] = lax.dynamic_slice(g_small["ffn_conv_w"], (0, chip * n_upc), (3, n_upc))
    g_small["ssm_log_step"] = g_small["ssm_log_step"].reshape(1, n_groups)
    small_params = {
        "b_ada": (b_ada, m_b_ada, v_b_ada), "g_pre_mix": (g_pre_mix, m_g_pre_mix, v_g_pre_mix),
        "g_post_mix": (g_post_mix, m_g_post_mix, v_g_post_mix), "ssm_lam_re": (ssm_lam_re, m_ssm_lam_re, v_ssm_lam_re),
        "ssm_lam_im": (ssm_lam_im, m_ssm_lam_im, v_ssm_lam_im),
        "ssm_log_step": (ssm_log_step, m_ssm_log_step, v_ssm_log_step),
        "ssm_b_re": (ssm_b_re, m_ssm_b_re, v_ssm_b_re), "ssm_b_im": (ssm_b_im, m_ssm_b_im, v_ssm_b_im),
        "ssm_c_re": (ssm_c_re, m_ssm_c_re, v_ssm_c_re), "ssm_c_im": (ssm_c_im, m_ssm_c_im, v_ssm_c_im),
        "ssm_d": (ssm_d, m_ssm_d, v_ssm_d), "glu_b": (glu_b, m_glu_b, v_glu_b),
        "g_out_ssm": (g_out_ssm, m_g_out_ssm, v_g_out_ssm), "conv_w": (conv_w, m_conv_w, v_conv_w),
        "g_out_conv": (g_out_conv, m_g_out_conv, v_g_out_conv), "g_pre_ffn": (g_pre_ffn, m_g_pre_ffn, v_g_pre_ffn),
        "g_post_ffn": (g_post_ffn, m_g_post_ffn, v_g_post_ffn),
        "ffn_conv_w": (ffn_conv_w, m_ffn_conv_w, v_ffn_conv_w),
    }

    def natural(a):
        return a[0] if a.ndim > 2 else a

    names = list(small_params)
    items = []
    for nm in names:
        w_, m_, v_ = small_params[nm]
        items.append((natural(w_), g_small[nm].reshape(natural(w_).shape), natural(m_), natural(v_)))
    upd = _adamw_small(items)
    small_out = {}
    for nm, (dl, mo, vo) in zip(names, upd):
        shp = small_params[nm][0].shape
        small_out[nm] = (g_small[nm].reshape(shp), dl.reshape(shp), mo.reshape(shp), vo.reshape(shp))

    loss = g_small["loss"][0, 0]

    order = ["w_ada", "b_ada", "g_pre_mix", "g_post_mix", "w_in", "ssm_lam_re", "ssm_lam_im", "ssm_log_step",
             "ssm_b_re", "ssm_b_im", "ssm_c_re", "ssm_c_im", "ssm_d", "glu_w", "glu_b", "g_out_ssm", "conv_w",
             "g_out_conv", "w_out", "g_pre_ffn", "g_post_ffn", "w_up", "ffn_conv_w", "w_down"]
    results = {"w_ada": tuple(a[None] for a in ada)}
    for nm in big:
        results[nm] = tuple(a[None] for a in big[nm])
    results.update(small_out)
    outs = [loss, grad_x[None]]
    for k in range(4):
        outs += [results[nm][k] for nm in order]
    return tuple(outs)


def _ga_rowsum(ga_re8, ga_im8):
    n = ga_re8.shape[1]

    def body(r_ref, i_ref, o_ref):
        o_ref[...] = jnp.zeros(o_ref.shape, F32)
        o_ref[0:1, :] = _colsum(r_ref[...])
        o_ref[1:2, :] = _colsum(i_ref[...])

    return pl.pallas_call(body, name="ga_rowsum", out_shape=jax.ShapeDtypeStruct((SUBLANES, n), F32))(ga_re8, ga_im8)
```

```python
import functools
import math

import jax
import jax.numpy as jnp
import numpy as np
from jax import lax
from jax.experimental import pallas as pl
from jax.experimental.pallas import tpu as pltpu

F32 = jnp.float32
BF16 = jnp.bfloat16
MESH = pl.DeviceIdType.MESH

EPS = 1e-6
LAMBDA_RE_MAX = -1e-4
ADAM_LR = 0.001
ADAM_B1 = 0.9
ADAM_B2 = 0.999
ADAM_EPS = 1e-08
ADAM_WD = 0.01
ADAM_STEP = 10

SUBLANES = 8
BF16_ROWS = 16
N_CHIPS = 4
N_DEV = 8
CONV_HEAD_DIM = 64
VMEM_BIG = 56 * 1024 * 1024
VMEM_MID = 40 * 1024 * 1024
VMEM_KEEP_OPERANDS_IN_HBM = 62 * 1024 * 1024

TB_MIX = 256
TB_MIX_FWD = 512
TB_FFN = 256
TB_FFN_UP = 512
TB_SCAN = 2048
W_SCAN = 256
SSM_SPLIT = 4
CW_FFN = 256
SCAN_UNROLL = 4
TB_TN = 512


def _cparams(sem=None, vmem=None):
    kw = {}
    if sem is not None:
        kw["dimension_semantics"] = sem
    if vmem is not None:
        kw["vmem_limit_bytes"] = vmem
    return pltpu.CompilerParams(**kw)


def _blk(t, pref):
    return pref if t % pref == 0 else t


def _dot(a, b):
    return jnp.dot(a.astype(BF16), b.astype(BF16), preferred_element_type=F32)


def _dot_nt(a, b):
    return lax.dot_general(a.astype(BF16), b.astype(BF16), (((1,), (1,)), ((), ())),
                           preferred_element_type=F32)


def _dot_tn(a, b):
    return lax.dot_general(a.astype(BF16), b.astype(BF16), (((0,), (0,)), ((), ())),
                           preferred_element_type=F32)


def _sigmoid(x):
    return 0.5 * jnp.tanh(0.5 * x) + 0.5


_GELU_K = math.sqrt(2.0 / math.pi)
_GELU_C = 0.044715


def _gelu(x):
    th = jnp.tanh(_GELU_K * (x + _GELU_C * x * x * x))
    return x * (0.5 * (1.0 + th))


def _gelu_and_grad(x):
    x2 = x * x
    th = jnp.tanh(_GELU_K * (x + _GELU_C * x2 * x))
    half = 0.5 * (1.0 + th)
    return x * half, half + 0.5 * x * (1.0 - th * th) * _GELU_K * (1.0 + 3.0 * _GELU_C * x2)


def _rowmean(x):
    return jnp.mean(x, axis=-1, keepdims=True)


def _colsum(x):
    return jnp.sum(x, axis=0, keepdims=True)


def _split_dot(x, m):
    hi = x.astype(BF16)
    lo = (x - hi.astype(F32)).astype(BF16)
    return (jnp.dot(hi, m, preferred_element_type=F32) + jnp.dot(lo, m, preferred_element_type=F32))


def _split3_dot(x, m):
    hi = x.astype(BF16)
    r1 = x - hi.astype(F32)
    mid = r1.astype(BF16)
    lo = (r1 - mid.astype(F32)).astype(BF16)
    return (jnp.dot(hi, m, preferred_element_type=F32) + jnp.dot(mid, m, preferred_element_type=F32)
            + jnp.dot(lo, m, preferred_element_type=F32))


def _shift_down(x, halo, k):
    r = pltpu.roll(x, k, 0)
    row = lax.broadcasted_iota(jnp.int32, x.shape, 0)
    last = halo.shape[0]
    for j in range(k):
        r = jnp.where(row == j, halo[last - k + j:last - k + j + 1, :], r)
    return r


def _shift_up(x, halo, k):
    n = x.shape[0]
    r = pltpu.roll(x, n - k, 0)
    row = lax.broadcasted_iota(jnp.int32, x.shape, 0)
    for j in range(k):
        r = jnp.where(row == n - k + j, halo[j:j + 1, :], r)
    return r


def _acc_rows(ref, first, rows):
    @pl.when(first)
    def _():
        ref[...] = jnp.zeros(ref.shape, ref.dtype)
    for j, r in enumerate(rows):
        ref[j:j + 1, :] += r


def _rows(tb, c, col=0):
    return pl.BlockSpec((tb, c), lambda i, col=col: (i, col))


def _full(shape):
    nd = len(shape)
    return pl.BlockSpec(shape, lambda i, nd=nd: (0,) * nd)


def _resident(shape):
    nd = len(shape)
    return pl.BlockSpec(shape, lambda i, nd=nd: (0,) * nd, pipeline_mode=pl.Buffered(1))


def _halo_prev(tb, c, col=0, rows=SUBLANES):
    per = tb // rows
    return pl.BlockSpec((rows, c), lambda i, col=col: (jnp.maximum(i * per - 1, 0), col))


def _halo_next(tb, c, t, col=0, rows=SUBLANES):
    per = tb // rows
    last = t // rows - 1
    return pl.BlockSpec((rows, c), lambda i, col=col: (jnp.minimum((i + 1) * per, last), col))


def _mesh_pos():
    return lax.axis_index("x"), lax.axis_index("y"), lax.axis_index("c")


def _allgather8(x_pad, name):
    m_per, n = x_pad.shape

    def body(x_ref, out_ref, send_sems, recv_sems, local_sem):
        x, y, c = _mesh_pos()
        me, sibling = (x, y, c), (x, y, 1 - c)
        chips = [(1 - x, y), (x, 1 - y), (1 - x, 1 - y)]

        def rows(px, py, pc):
            return out_ref.at[pl.ds((4 * px + 2 * py + pc) * m_per, m_per), :]

        def copy(k, block, to, src=None):
            return pltpu.make_async_remote_copy(
                src_ref=rows(*block) if src is None else src, dst_ref=rows(*block),
                send_sem=send_sems.at[k], recv_sem=recv_sems.at[k], device_id=to, device_id_type=MESH)

        mine = pltpu.make_async_copy(x_ref, rows(*me), local_sem)
        mine.start()
        first = [copy(0, me, sibling, src=x_ref)]
        first += [copy(1 + j, me, (*chip, c), src=x_ref) for j, chip in enumerate(chips)]
        for cp in first:
            cp.start()
        passed = [copy(4 + j, (*chip, c), sibling) for j, chip in enumerate(chips)]
        for j, chip in enumerate(chips):
            copy(1 + j, (*chip, c), me).wait_recv()
            passed[j].start()
        copy(0, sibling, me).wait_recv()
        for j, chip in enumerate(chips):
            copy(4 + j, (*chip, 1 - c), me).wait_recv()
        for cp in first + passed:
            cp.wait_send()
        mine.wait()

    return pl.pallas_call(
        body, name=name,
        out_shape=jax.ShapeDtypeStruct((N_DEV * m_per, n), F32),
        in_specs=[pl.BlockSpec(memory_space=pltpu.VMEM)],
        out_specs=pl.BlockSpec(memory_space=pltpu.VMEM),
        scratch_shapes=[pltpu.SemaphoreType.DMA((7,)), pltpu.SemaphoreType.DMA((7,)), pltpu.SemaphoreType.DMA],
    )(x_pad)


_START_IDS = {name: k for k, name in enumerate((
    "weights_start_in", "weights_start_rest", "grads_start_ffn", "grads_start_mix", "grads_start_in", "small_start",
    "swap_start_ffn", "swap_start_small", "swap_start_mix"))}
_HBM = pl.BlockSpec(memory_space=pltpu.HBM)
_SEM = pl.BlockSpec(memory_space=pltpu.SEMAPHORE)
_EFFECT = pltpu.SideEffectType.DATAFLOW_SIDE_EFFECTING


def _chip_copy(gather, src_ref, land_ref, send, recv, j, arrival):
    x, y, c = _mesh_pos()
    peer = [(1 - x, y), (x, 1 - y), (1 - x, 1 - y)][j]
    peer_chip = 2 * peer[0] + peer[1]
    my_chip = 2 * x + y
    return pltpu.make_async_remote_copy(
        src_ref=land_ref.at[my_chip] if gather else src_ref.at[peer_chip],
        dst_ref=land_ref.at[peer_chip if arrival else my_chip],
        send_sem=send.at[j], recv_sem=recv.at[j], device_id=(*peer, c), device_id_type=MESH)


def _chips_start(name, gather, srcs, lands, after=None):
    n, ns = len(lands), len(srcs)
    extra = [] if after is None else [after]

    def body(*refs):
        src_refs, land_refs = refs[:ns], refs[ns:ns + n]
        outs = refs[ns + n + len(extra):]
        sends, recvs, token = outs[:n], outs[n:2 * n], outs[-1]
        x, y, c = _mesh_pos()
        barrier = pltpu.get_barrier_semaphore()
        for peer in [(1 - x, y), (x, 1 - y), (1 - x, 1 - y)]:
            pl.semaphore_signal(barrier, inc=1, device_id=(*peer, c), device_id_type=MESH)
        pl.semaphore_wait(barrier, N_CHIPS - 1)
        for k in range(n):
            for j in range(3):
                _chip_copy(gather, src_refs[k] if ns else None, land_refs[k], sends[k], recvs[k], j, False).start()
        token[...] = jnp.zeros(token.shape, F32)

    sem = pltpu.SemaphoreType.DMA((3,))
    thru = tuple(pltpu.HBM(a.shape, a.dtype) for a in list(srcs) + list(lands))
    res = pl.pallas_call(
        body, name=name,
        out_shape=(sem,) * (2 * n) + thru + (jax.ShapeDtypeStruct((SUBLANES, 128), F32),),
        in_specs=[_HBM] * (ns + n) + [pl.BlockSpec(memory_space=pl.ANY)] * len(extra),
        out_specs=(_SEM,) * (2 * n) + (_HBM,) * (ns + n) + (pl.BlockSpec(memory_space=pltpu.VMEM),),
        input_output_aliases={k: 2 * n + k for k in range(ns + n)},
        compiler_params=pltpu.CompilerParams(has_side_effects=_EFFECT, collective_id=_START_IDS[name]),
    )(*[pltpu.with_memory_space_constraint(a, pltpu.HBM) for a in list(srcs) + list(lands)], *extra)
    return res[:n], res[n:2 * n], res[2 * n:2 * n + ns], res[2 * n + ns:2 * n + ns + n], res[-1]


def _chips_wait(name, gather, sends, recvs, srcs, lands, after):
    n, ns = len(lands), len(srcs)

    def body(*refs):
        src_refs, land_refs = refs[:ns], refs[ns:ns + n]
        sends_, recvs_ = refs[ns + n:ns + 2 * n], refs[ns + 2 * n:ns + 3 * n]
        for k in range(n):
            for j in range(3):
                cp = _chip_copy(gather, src_refs[k] if ns else None, land_refs[k], sends_[k], recvs_[k], j, True)
                cp.wait_send()
                cp.wait_recv()

    thru = tuple(pltpu.HBM(a.shape, a.dtype) for a in list(srcs) + list(lands))
    res = pl.pallas_call(
        body, name=name, out_shape=thru,
        in_specs=[_HBM] * (ns + n) + [_SEM] * (2 * n) + [pl.BlockSpec(memory_space=pl.ANY)],
        out_specs=(_HBM,) * (ns + n),
        input_output_aliases={k: k for k in range(ns + n)},
        compiler_params=pltpu.CompilerParams(has_side_effects=_EFFECT),
    )(*srcs, *lands, *sends, *recvs, after)
    return res[:ns], res[ns:]


def _sibling_copy(src_ref, land_ref, send, recv):
    x, y, c = _mesh_pos()
    return pltpu.make_async_remote_copy(src_ref=src_ref, dst_ref=land_ref, send_sem=send.at[0], recv_sem=recv.at[0],
                                        device_id=(x, y, 1 - c), device_id_type=MESH)


def _sibling_start(name, arrs, after=None):
    n = len(arrs)
    extra = [] if after is None else [after]
    lands = [lax.empty(a.shape, a.dtype) for a in arrs]

    def body(*refs):
        src_refs, land_refs = refs[:n], refs[n:2 * n]
        outs = refs[2 * n + len(extra):]
        sends, recvs, token = outs[:n], outs[n:2 * n], outs[-1]
        x, y, c = _mesh_pos()
        barrier = pltpu.get_barrier_semaphore()
        pl.semaphore_signal(barrier, inc=1, device_id=(x, y, 1 - c), device_id_type=MESH)
        pl.semaphore_wait(barrier, 1)
        for k in range(n):
            _sibling_copy(src_refs[k], land_refs[k], sends[k], recvs[k]).start()
        token[...] = jnp.zeros(token.shape, F32)

    sem = pltpu.SemaphoreType.DMA((1,))
    thru = tuple(pltpu.HBM(a.shape, a.dtype) for a in list(arrs) + lands)
    res = pl.pallas_call(
        body, name=name,
        out_shape=(sem,) * (2 * n) + thru + (jax.ShapeDtypeStruct((SUBLANES, 128), F32),),
        in_specs=[_HBM] * (2 * n) + [pl.BlockSpec(memory_space=pl.ANY)] * len(extra),
        out_specs=(_SEM,) * (2 * n) + (_HBM,) * (2 * n) + (pl.BlockSpec(memory_space=pltpu.VMEM),),
        input_output_aliases={k: 2 * n + k for k in range(2 * n)},
        compiler_params=pltpu.CompilerParams(has_side_effects=_EFFECT, collective_id=_START_IDS[name]),
    )(*[pltpu.with_memory_space_constraint(a, pltpu.HBM) for a in list(arrs) + lands], *extra)
    return res[:n], res[n:2 * n], res[2 * n:3 * n], res[3 * n:4 * n], res[-1]


def _sibling_wait(name, sends, recvs, srcs, lands, after):
    n = len(srcs)

    def body(*refs):
        src_refs, land_refs = refs[:n], refs[n:2 * n]
        sends_, recvs_ = refs[2 * n:3 * n], refs[3 * n:4 * n]
        for k in range(n):
            cp = _sibling_copy(src_refs[k], land_refs[k], sends_[k], recvs_[k])
            cp.wait_send()
            cp.wait_recv()

    thru = tuple(pltpu.HBM(a.shape, a.dtype) for a in list(srcs) + list(lands))
    res = pl.pallas_call(
        body, name=name, out_shape=thru,
        in_specs=[_HBM] * (2 * n) + [_SEM] * (2 * n) + [pl.BlockSpec(memory_space=pl.ANY)],
        out_specs=(_HBM,) * (2 * n),
        input_output_aliases={k: k for k in range(2 * n)},
        compiler_params=pltpu.CompilerParams(has_side_effects=_EFFECT),
    )(*srcs, *lands, *sends, *recvs, after)
    return res[:n], res[n:]


def _landing(own, chip):
    zone = lax.empty((N_CHIPS,) + own.shape, own.dtype)
    return lax.dynamic_update_slice(zone, own[None], (chip,) + (0,) * own.ndim)


def _mod_shard(c_all, w_ada_sh, b_sh):
    d, n = w_ada_sh.shape
    bn = 512

    def body(c_ref, w_ref, b_ref, o_ref):
        cc = c_ref[...]
        ca = cc * _sigmoid(cc)
        o_ref[...] = _dot(ca, w_ref[...]) + b_ref[...]

    return pl.pallas_call(
        body, name="mod_shard", grid=(n // bn,),
        out_shape=jax.ShapeDtypeStruct((N_DEV, n), F32),
        in_specs=[_full((N_DEV, d)), pl.BlockSpec((d, bn), lambda j: (0, j)), pl.BlockSpec((1, bn), lambda j: (0, j))],
        out_specs=pl.BlockSpec((N_DEV, bn), lambda j: (0, j)),
        compiler_params=_cparams(("parallel",)),
    )(c_all, w_ada_sh, b_sh)


def _ssm_prep(lam_re, lam_im, log_step):
    g, p = lam_re.shape

    def body(lr_ref, li_ref, ls_ref, ar_ref, ai_ref, cr_ref, ci_ref):
        lr = jnp.minimum(lr_ref[...], LAMBDA_RE_MAX)
        li = li_ref[...]
        st = jnp.exp(ls_ref[...])
        mag = jnp.exp(lr * st)
        ar = mag * jnp.cos(li * st)
        ai = mag * jnp.sin(li * st)
        den = lr * lr + li * li
        nr = ar - 1.0
        ar_ref[...] = ar
        ai_ref[...] = ai
        cr_ref[...] = (nr * lr + ai * li) / den
        ci_ref[...] = (ai * lr - nr * li) / den

    sds = jax.ShapeDtypeStruct((g, p), F32)
    return pl.pallas_call(body, name="ssm_prep", out_shape=(sds,) * 4)(lam_re, lam_im, log_step)


def _ssm_blocks(bt_re, bt_im, ct_re, ct_im, coef_rows, tile_b, tile_c):
    gh, p = bt_re.shape
    gp, h = ct_re.shape
    nb = SSM_SPLIT
    cb, rb = gp // nb, gp // nb

    def body(btr, bti, ctr, cti, cf, tb_ref, tc_ref, bre_o, bim_o, cre_o, cim_o):
        j = pl.program_id(0)
        row = lax.broadcasted_iota(jnp.int32, (gh, cb), 0)
        col = lax.broadcasted_iota(jnp.int32, (gh, cb), 1) + j * cb
        mask = (row >> 4) == (col >> 6)
        cr, ci = cf[0:1, :], cf[1:2, :]
        br = _split3_dot(btr[...], tb_ref[...])
        bi = _split3_dot(bti[...], tb_ref[...])
        bre_o[...] = jnp.where(mask, br * cr - bi * ci, 0.0).astype(BF16)
        bim_o[...] = jnp.where(mask, br * ci + bi * cr, 0.0).astype(BF16)
        row2 = lax.broadcasted_iota(jnp.int32, (rb, gh), 0) + j * rb
        col2 = lax.broadcasted_iota(jnp.int32, (rb, gh), 1)
        mask2 = (row2 >> 6) == (col2 >> 4)
        cre_o[...] = jnp.where(mask2, _split3_dot(ctr[...], tc_ref[...]), 0.0).astype(BF16)
        cim_o[...] = jnp.where(mask2, _split3_dot(cti[...], tc_ref[...]), 0.0).astype(BF16)

    bspec = pl.BlockSpec((gh, cb), lambda j: (0, j))
    cspec = pl.BlockSpec((rb, gh), lambda j: (j, 0))
    cin = pl.BlockSpec((rb, h), lambda j: (j, 0))
    return pl.pallas_call(
        body, name="ssm_blocks", grid=(nb,),
        out_shape=(jax.ShapeDtypeStruct((gh, gp), BF16),) * 2 + (jax.ShapeDtypeStruct((gp, gh), BF16),) * 2,
        in_specs=[_full((gh, p)), _full((gh, p)), cin, cin, pl.BlockSpec((SUBLANES, cb), lambda j: (0, j)),
                  _full(tile_b.shape), _full(tile_c.shape)],
        out_specs=(bspec, bspec, cspec, cspec),
        compiler_params=_cparams(("parallel",)),
    )(bt_re, bt_im, ct_re, ct_im, coef_rows, tile_b, tile_c)


def _scan_consts(a_ref, reverse):
    w = a_ref.shape[1]
    ar1 = a_ref[0:1, :]
    ai1 = a_ref[1:2, :]
    if reverse:
        ai1 = -ai1
    pr, pi = [ar1], [ai1]
    for _ in range(1, SUBLANES):
        nr = pr[-1] * ar1 - pi[-1] * ai1
        ni = pr[-1] * ai1 + pi[-1] * ar1
        pr.append(nr)
        pi.append(ni)
    row = lax.broadcasted_iota(jnp.int32, (SUBLANES, w), 0)
    dist = (SUBLANES - 1 - row) if reverse else row

    def pick(vals):
        out = jnp.broadcast_to(vals[SUBLANES - 1], (SUBLANES, w))
        for r in range(SUBLANES - 1):
            out = jnp.where(dist == r, vals[r], out)
        return out

    p_r, p_i = pick(pr), pick(pi)
    steps = []
    for k in (1, 2, 4):
        steps.append((k, jnp.where(dist >= k, pr[k - 1], 0.0), jnp.where(dist >= k, pi[k - 1], 0.0)))
    a8 = (jnp.broadcast_to(pr[SUBLANES - 1], (SUBLANES, w)), jnp.broadcast_to(pi[SUBLANES - 1], (SUBLANES, w)))
    return row, p_r, p_i, steps, a8


def _scan_tile(xr, xi, cr, ci, consts, reverse):
    row, p_r, p_i, steps, (a8r, a8i) = consts
    for k, s_r, s_i in steps:
        sh = (SUBLANES - k) if reverse else k
        qr = pltpu.roll(xr, sh, 0)
        qi = pltpu.roll(xi, sh, 0)
        xr, xi = xr + s_r * qr - s_i * qi, xi + s_r * qi + s_i * qr
    outr = xr + p_r * cr - p_i * ci
    outi = xi + p_r * ci + p_i * cr
    e = 0 if reverse else SUBLANES - 1
    er = jnp.broadcast_to(xr[e:e + 1, :], xr.shape)
    ei = jnp.broadcast_to(xi[e:e + 1, :], xi.shape)
    return outr, outi, er + a8r * cr - a8i * ci, ei + a8r * ci + a8i * cr


def _scan_fwd(a_rows, bu_re, bu_im):
    t, n = bu_re.shape
    tb, w = _blk(t, TB_SCAN), W_SCAN
    ntile = tb // SUBLANES

    def body(a_ref, br_ref, bi_ref, sr_ref, si_ref, car, cai):
        @pl.when(pl.program_id(1) == 0)
        def _():
            car[...] = jnp.zeros(car.shape, F32)
            cai[...] = jnp.zeros(cai.shape, F32)
        consts = _scan_consts(a_ref, False)

        def pair(i, carry):
            o = pl.multiple_of(i * BF16_ROWS, BF16_ROWS)
            b_r = br_ref[pl.ds(o, BF16_ROWS), :].astype(F32)
            b_i = bi_ref[pl.ds(o, BF16_ROWS), :].astype(F32)
            outs = []
            for h in range(2):
                rows = slice(h * SUBLANES, (h + 1) * SUBLANES)
                outr, outi, ncr, nci = _scan_tile(b_r[rows, :], b_i[rows, :], carry[0], carry[1], consts, False)
                outs.append((outr, outi))
                carry = (ncr, nci)
            sr_ref[pl.ds(o, BF16_ROWS), :] = jnp.concatenate([outs[0][0], outs[1][0]], axis=0).astype(BF16)
            si_ref[pl.ds(o, BF16_ROWS), :] = jnp.concatenate([outs[0][1], outs[1][1]], axis=0).astype(BF16)
            return carry

        def pairs(i, carry):
            for s in range(SCAN_UNROLL // 2):
                carry = pair(i * (SCAN_UNROLL // 2) + s, carry)
            return carry

        cr, ci = lax.fori_loop(0, ntile // SCAN_UNROLL, pairs, (car[...], cai[...]))
        car[...] = cr
        cai[...] = ci

    spec = pl.BlockSpec((tb, w), lambda s, k: (k, s))
    sds = jax.ShapeDtypeStruct((t, n), BF16)
    return pl.pallas_call(
        body, name="scan_fwd", grid=(n // w, t // tb), out_shape=(sds, sds),
        in_specs=[pl.BlockSpec((SUBLANES, w), lambda s, k: (0, s)), spec, spec], out_specs=(spec, spec),
        scratch_shapes=[pltpu.VMEM((SUBLANES, w), F32), pltpu.VMEM((SUBLANES, w), F32)],
        compiler_params=_cparams(("parallel", "arbitrary"), VMEM_MID),
    )(a_rows, bu_re, bu_im)


def _scan_bwd(a_rows, g_re, g_im, s_re, s_im):
    t, n = g_re.shape
    tb, w = _blk(t, TB_SCAN), W_SCAN
    ntile = tb // SUBLANES
    npair = tb // BF16_ROWS
    nt = t // tb

    def body(a_ref, gr_ref, gi_ref, sr_ref, si_ref, or_ref, oi_ref, gar_ref, gai_ref, car, cai):
        @pl.when(pl.program_id(1) == 0)
        def _():
            car[...] = jnp.zeros(car.shape, F32)
            cai[...] = jnp.zeros(cai.shape, F32)
            gar_ref[...] = jnp.zeros(gar_ref.shape, F32)
            gai_ref[...] = jnp.zeros(gai_ref.shape, F32)
        consts = _scan_consts(a_ref, True)
        row = consts[0]

        def pair(i, carry):
            cr, ci, accr, acci = carry
            o = pl.multiple_of((npair - 1 - i) * BF16_ROWS, BF16_ROWS)
            s_r = sr_ref[pl.ds(o, BF16_ROWS), :].astype(F32)
            s_i = si_ref[pl.ds(o, BF16_ROWS), :].astype(F32)
            g_r = gr_ref[pl.ds(o, BF16_ROWS), :].astype(F32)
            g_i = gi_ref[pl.ds(o, BF16_ROWS), :].astype(F32)
            outs = [None, None]
            for h in (1, 0):
                rows = slice(h * SUBLANES, (h + 1) * SUBLANES)
                outr, outi, ncr, nci = _scan_tile(g_r[rows, :], g_i[rows, :], cr, ci, consts, True)
                outs[h] = (outr, outi)
                gnr = jnp.where(row == SUBLANES - 1, cr, pltpu.roll(outr, SUBLANES - 1, 0))
                gni = jnp.where(row == SUBLANES - 1, ci, pltpu.roll(outi, SUBLANES - 1, 0))
                sr = s_r[h * SUBLANES:(h + 1) * SUBLANES, :]
                si = s_i[h * SUBLANES:(h + 1) * SUBLANES, :]
                accr, acci = accr + sr * gnr + si * gni, acci + sr * gni - si * gnr
                cr, ci = ncr, nci
            or_ref[pl.ds(o, BF16_ROWS), :] = jnp.concatenate([outs[0][0], outs[1][0]], axis=0).astype(BF16)
            oi_ref[pl.ds(o, BF16_ROWS), :] = jnp.concatenate([outs[0][1], outs[1][1]], axis=0).astype(BF16)
            return cr, ci, accr, acci

        def pairs(i, carry):
            for s in range(SCAN_UNROLL // 2):
                carry = pair(i * (SCAN_UNROLL // 2) + s, carry)
            return carry

        cr, ci, accr, acci = lax.fori_loop(0, ntile // SCAN_UNROLL, pairs,
                                           (car[...], cai[...], gar_ref[...], gai_ref[...]))
        car[...] = cr
        cai[...] = ci
        gar_ref[...] = accr
        gai_ref[...] = acci

    spec = pl.BlockSpec((tb, w), lambda s, k: (nt - 1 - k, s))
    aspec = pl.BlockSpec((SUBLANES, w), lambda s, k: (0, s))
    sds = jax.ShapeDtypeStruct((t, n), BF16)
    asds = jax.ShapeDtypeStruct((SUBLANES, n), F32)
    return pl.pallas_call(
        body, name="scan_bwd", grid=(n // w, nt), out_shape=(sds, sds, asds, asds),
        in_specs=[aspec, spec, spec, spec, spec], out_specs=(spec, spec, aspec, aspec),
        scratch_shapes=[pltpu.VMEM((SUBLANES, w), F32), pltpu.VMEM((SUBLANES, w), F32)],
        compiler_params=_cparams(("parallel", "arbitrary"), VMEM_MID),
    )(a_rows, g_re, g_im, s_re, s_im)


def _mix_in(x, vec, w_in_st, b_re, b_im):
    t, d = x.shape
    ns, _, nc = w_in_st.shape
    dssm, nstate = b_re.shape
    du, ds = dssm // SSM_SPLIT, nstate // SSM_SPLIT
    tb = _blk(t, TB_MIX_FWD)

    def body(x_ref, vec_ref, w_ref, bre_ref, bim_ref, proj_ref, bur_ref, bui_ref, h1_ref):
        xv = x_ref[...]
        r = lax.rsqrt(_rowmean(xv * xv) + EPS)
        h = xv * r * vec_ref[0:1, :] * vec_ref[1:2, :] + vec_ref[2:3, :]
        hb = h.astype(BF16)
        h1_ref[...] = hb
        u = None
        for j in range(ns):
            pj = jnp.dot(hb, w_ref[j], preferred_element_type=F32)
            proj_ref[:, j * nc:(j + 1) * nc] = pj.astype(BF16)
            if j == 0:
                u = pj
        ub = u.astype(BF16)
        for q in range(SSM_SPLIT):
            rq, cq = slice(q * du, (q + 1) * du), slice(q * ds, (q + 1) * ds)
            bur_ref[:, cq] = jnp.dot(ub[:, rq], bre_ref[rq, cq], preferred_element_type=F32).astype(BF16)
            bui_ref[:, cq] = jnp.dot(ub[:, rq], bim_ref[rq, cq], preferred_element_type=F32).astype(BF16)

    return pl.pallas_call(
        body, name="mix_in", grid=(t // tb,),
        out_shape=(jax.ShapeDtypeStruct((t, ns * nc), BF16), jax.ShapeDtypeStruct((t, nstate), BF16),
                   jax.ShapeDtypeStruct((t, nstate), BF16), jax.ShapeDtypeStruct((t, d), BF16)),
        in_specs=[_rows(tb, d), _full((SUBLANES, d)), _resident(w_in_st.shape), _resident(b_re.shape),
                  _resident(b_im.shape)],
        out_specs=(_rows(tb, ns * nc), _rows(tb, nstate), _rows(tb, nstate), _rows(tb, d)),
        compiler_params=_cparams(("parallel",), VMEM_BIG),
    )(x, vec, w_in_st, b_re, b_im)


def _head_ms(y, h_ref):
    return _split_dot(y * y, h_ref[...])


def _conv3(x, halo, w_ref):
    return w_ref[0:1, :] * _shift_down(x, halo, 2) + w_ref[1:2, :] * _shift_down(x, halo, 1) + w_ref[2:3, :] * x


def _mix_out(x, proj, s_re, s_im, c_re, c_im, v512, convw, glu_w, h16, h64, w_out, vd):
    t, d = x.shape
    dh = c_re.shape[1]
    nstate = s_re.shape[1]
    du, ds = dh // SSM_SPLIT, nstate // SSM_SPLIT
    tb = _blk(t, TB_MIX_FWD)

    def body(x_ref, u_ref, bg_ref, cg_ref, v_ref, cgh_ref, vh_ref, sr_ref, si_ref, cre_ref, cim_ref, p_ref,
             cw_ref, gw_ref, h16_ref, h64_ref, wo_ref, vd_ref, y1_ref, o_ref, x2_ref):
        i = pl.program_id(0)
        u = u_ref[...].astype(F32)
        ys = []
        for q in range(SSM_SPLIT):
            rq, cq = slice(q * ds, (q + 1) * ds), slice(q * du, (q + 1) * du)
            ys.append(_dot(sr_ref[:, rq], cre_ref[rq, cq]) - _dot(si_ref[:, rq], cim_ref[rq, cq]))
        ys = jnp.concatenate(ys, axis=1)
        y1 = ys + p_ref[0:1, :] * u
        y1_ref[...] = y1
        z = _gelu(y1)
        q = _dot(z, gw_ref[...]) + p_ref[1:2, :]
        ya = z * _sigmoid(q)
        na = ya * lax.rsqrt(_head_ms(ya, h16_ref) + EPS) * p_ref[2:3, :]
        cv = cg_ref[...].astype(F32) * v_ref[...].astype(F32)
        cvh = jnp.where(i > 0, cgh_ref[...].astype(F32) * vh_ref[...].astype(F32), 0.0)
        yb = bg_ref[...].astype(F32) * _conv3(cv, cvh, cw_ref)
        nb = yb * lax.rsqrt(_head_ms(yb, h64_ref) + EPS) * p_ref[3:4, :]
        o = _dot(na, wo_ref[0:dh, :]) + _dot(nb, wo_ref[dh:2 * dh, :])
        o_ref[...] = o
        on = o * lax.rsqrt(_rowmean(o * o) + EPS) * vd_ref[0:1, :]
        x2_ref[...] = x_ref[...] + vd_ref[1:2, :] * on

    return pl.pallas_call(
        body, name="mix_out", grid=(t // tb,),
        out_shape=(jax.ShapeDtypeStruct((t, dh), F32), jax.ShapeDtypeStruct((t, d), F32),
                   jax.ShapeDtypeStruct((t, d), F32)),
        in_specs=[_rows(tb, d), _rows(tb, dh, 0), _rows(tb, dh, 1), _rows(tb, dh, 2), _rows(tb, dh, 3),
                  _halo_prev(tb, dh, 2, BF16_ROWS), _halo_prev(tb, dh, 3, BF16_ROWS), _rows(tb, nstate), _rows(tb, nstate),
                  _full(c_re.shape), _full(c_im.shape), _full(v512.shape), _full(convw.shape), _full(glu_w.shape),
                  _full(h16.shape), _full(h64.shape), _full(w_out.shape), _full(vd.shape)],
        out_specs=(_rows(tb, dh), _rows(tb, d), _rows(tb, d)),
        compiler_params=_cparams(("parallel",), VMEM_BIG),
    )(x, proj, proj, proj, proj, proj, proj, s_re, s_im, c_re, c_im, v512, convw, glu_w, h16, h64, w_out, vd)


def _ffn_up(x2, vec, w_up_st):
    t, d = x2.shape
    ns, _, nc = w_up_st.shape
    tb = _blk(t, TB_FFN_UP)

    def body(x_ref, vec_ref, w_ref, up_ref, h2_ref):
        xv = x_ref[...]
        r = lax.rsqrt(_rowmean(xv * xv) + EPS)
        h = xv * r * vec_ref[0:1, :] * vec_ref[1:2, :] + vec_ref[2:3, :]
        hb = h.astype(BF16)
        h2_ref[...] = hb
        for j in range(ns):
            up_ref[:, j * nc:(j + 1) * nc] = jnp.dot(hb, w_ref[j], preferred_element_type=F32)

    return pl.pallas_call(
        body, name="ffn_up", grid=(t // tb,),
        out_shape=(jax.ShapeDtypeStruct((t, ns * nc), F32), jax.ShapeDtypeStruct((t, d), BF16)),
        in_specs=[_rows(tb, d), _full((SUBLANES, d)), _resident(w_up_st.shape)],
        out_specs=(_rows(tb, ns * nc), _rows(tb, d)),
        compiler_params=_cparams(("parallel",), VMEM_BIG),
    )(x2, vec, w_up_st)


def _ffn_down(up, fw, w_down, w_down_t, x2, tgt, vd):
    t, nh = up.shape
    dff, d = w_down.shape
    tb = _blk(t, TB_FFN)
    inv_d = 1.0 / d

    def body(up_ref, uph_ref, fw_ref, wd_ref, wdt_ref, x2_ref, tgt_ref, vd_ref,
             act_ref, ddn_ref, dout_ref, dhid_ref, vec_ref, loss_ref, a_s, vv_s, sg_s):
        i = pl.program_id(0)

        def conv_cols(sl):
            x = up_ref[:, sl]
            halo = jnp.where(i > 0, uph_ref[:, sl], 0.0)
            return (fw_ref[0:1, sl] * _shift_down(x, halo, 2) + fw_ref[1:2, sl] * _shift_down(x, halo, 1)
                    + fw_ref[2:3, sl] * x)

        dn = None
        for o in range(0, dff, CW_FFN):
            sl = slice(o, o + CW_FFN)
            a = conv_cols(sl)
            vv = conv_cols(slice(dff + o, dff + o + CW_FFN))
            sg = _sigmoid(a)
            si = a * sg
            a_s[:, sl] = si
            vv_s[:, sl] = vv
            sg_s[:, sl] = sg
            actb = (si * vv).astype(BF16)
            act_ref[:, sl] = actb
            pj = lax.dot_general(actb, wdt_ref[:, sl], (((1,), (1,)), ((), ())), preferred_element_type=F32)
            dn = pj if dn is None else dn + pj
        r3 = lax.rsqrt(_rowmean(dn * dn) + EPS)
        xn = dn * r3
        g = vd_ref[0:1, :]
        gt2 = vd_ref[1:2, :]
        dnn = xn * g
        diff = x2_ref[...] + gt2 * dnn - tgt_ref[...]
        part = 0.5 * inv_d * jnp.sum(diff * diff)

        @pl.when(i == 0)
        def _():
            loss_ref[...] = jnp.zeros(loss_ref.shape, F32)
        loss_ref[...] += part
        dout = diff * inv_d
        dout_ref[...] = dout
        ddnn = dout * gt2
        _acc_rows(vec_ref, i == 0, [_colsum(dout * dnn), _colsum(ddnn * xn)])
        dxn = ddnn * g
        ddn = r3 * (dxn - xn * _rowmean(dxn * xn))
        ddnb = ddn.astype(BF16)
        ddn_ref[...] = ddnb
        for o in range(0, dff, CW_FFN):
            sl = slice(o, o + CW_FFN)
            dact = lax.dot_general(ddnb, wd_ref[sl, :], (((1,), (1,)), ((), ())), preferred_element_type=F32)
            si, vv, sg = a_s[:, sl], vv_s[:, sl], sg_s[:, sl]
            dhid_ref[:, sl] = (dact * vv * (sg + si * (1.0 - sg))).astype(BF16)
            dhid_ref[:, dff + o:dff + o + CW_FFN] = (dact * si).astype(BF16)

    return pl.pallas_call(
        body, name="ffn_down", grid=(t // tb,),
        scratch_shapes=[pltpu.VMEM((tb, dff), F32)] * 3,
        out_shape=(jax.ShapeDtypeStruct((t, dff), BF16), jax.ShapeDtypeStruct((t, d), BF16),
                   jax.ShapeDtypeStruct((t, d), F32), jax.ShapeDtypeStruct((t, nh), BF16),
                   jax.ShapeDtypeStruct((SUBLANES, d), F32), jax.ShapeDtypeStruct((SUBLANES, 128), F32)),
        in_specs=[_rows(tb, nh), _halo_prev(tb, nh), _full(fw.shape), _resident(w_down.shape),
                  _resident(w_down_t.shape), _rows(tb, d),
                  _rows(tb, d), _full(vd.shape)],
        out_specs=(_rows(tb, dff), _rows(tb, d), _rows(tb, d), _rows(tb, nh), _full((SUBLANES, d)),
                   _full((SUBLANES, 128))),
        compiler_params=_cparams(("arbitrary",), VMEM_BIG),
    )(up, up, fw, w_down, w_down_t, x2, tgt, vd)


def _ffn_up_bwd(dhid, up, fw, x2, dout, vec, w_up_st):
    t, nh = dhid.shape
    d = x2.shape[1]
    ns, _, nc = w_up_st.shape
    tb = _blk(t, TB_FFN)
    nblk = t // tb
    cw = 128

    def body(dh_ref, dhn_ref, up_ref, fw_ref, x2_ref, dout_ref, vec_ref, w_ref,
             dx2_ref, dup_ref, vp_ref, df_ref):
        i = pl.program_id(0)

        @pl.when(i == 0)
        def _():
            df_ref[...] = jnp.zeros(df_ref.shape, F32)
        dh2 = None
        for j in range(ns):
            for o in range(j * nc, (j + 1) * nc, cw):
                sl = slice(o, o + cw)
                dh = dh_ref[:, sl].astype(F32)
                dhn = jnp.where(i < nblk - 1, dhn_ref[:, sl].astype(F32), 0.0)
                dh1 = _shift_up(dh, dhn, 1)
                dh2s = _shift_up(dh, dhn, 2)
                dup_ref[:, sl] = (fw_ref[2:3, sl] * dh + fw_ref[1:2, sl] * dh1 + fw_ref[0:1, sl] * dh2s).astype(BF16)
                up_v = up_ref[:, sl]
                df_ref[0:1, sl] += _colsum(dh2s * up_v)
                df_ref[1:2, sl] += _colsum(dh1 * up_v)
                df_ref[2:3, sl] += _colsum(dh * up_v)
            pj = lax.dot_general(dup_ref[:, j * nc:(j + 1) * nc], w_ref[j], (((1,), (1,)), ((), ())),
                                 preferred_element_type=F32)
            dh2 = pj if dh2 is None else dh2 + pj
        xv = x2_ref[...]
        r = lax.rsqrt(_rowmean(xv * xv) + EPS)
        xn = xv * r
        g = vec_ref[0:1, :]
        hg = xn * g
        dhg = dh2 * vec_ref[1:2, :]
        _acc_rows(vp_ref, i == 0, [_colsum(dh2), _colsum(dh2 * hg), _colsum(dhg * xn)])
        dxn = dhg * g
        dx2_ref[...] = dout_ref[...] + r * (dxn - xn * _rowmean(dxn * xn))

    return pl.pallas_call(
        body, name="ffn_up_bwd", grid=(nblk,),
        out_shape=(jax.ShapeDtypeStruct((t, d), F32), jax.ShapeDtypeStruct((t, nh), BF16),
                   jax.ShapeDtypeStruct((SUBLANES, d), F32), jax.ShapeDtypeStruct((SUBLANES, nh), F32)),
        in_specs=[_rows(tb, nh), _halo_next(tb, nh, t, rows=BF16_ROWS), _rows(tb, nh), _full(fw.shape),
                  _rows(tb, d), _rows(tb, d), _full(vec.shape), _resident(w_up_st.shape)],
        out_specs=(_rows(tb, d), _rows(tb, nh), _full((SUBLANES, d)), _full((SUBLANES, nh))),
        compiler_params=_cparams(("arbitrary",), VMEM_BIG),
    )(dhid, dhid, up, fw, x2, dout, vec, w_up_st)


def _mix_out_bwd(dx2, o, y1, proj, s_re, s_im, c_re, c_im, v512, convw, glu_w, h16, h64, w_out, vd):
    t, d = dx2.shape
    dh = y1.shape[1]
    nstate = c_re.shape[0]
    du, ds = dh // SSM_SPLIT, nstate // SSM_SPLIT
    tb = _blk(t, TB_MIX)

    def body(dx2_ref, o_ref, y1_ref, u_ref, bg_ref, cg_ref, v_ref, cgh_ref, vh_ref, cre_ref, cim_ref, p_ref,
             cw_ref, gw_ref, h16_ref, h64_ref, wo_ref, vd_ref, sr_ref, si_ref,
             do_ref, ycat_ref, z_ref, dq_ref, dy1_ref, gr_ref, gi_ref, dcc_ref, dbg_ref, vpd_ref, vp5_ref,
             dcr_ref, dci_ref):
        i = pl.program_id(0)
        first = i == 0

        @pl.when(first)
        def _():
            dcr_ref[...] = jnp.zeros(dcr_ref.shape, F32)
            dci_ref[...] = jnp.zeros(dci_ref.shape, F32)
        ov = o_ref[...]
        ro = lax.rsqrt(_rowmean(ov * ov) + EPS)
        on_ = ov * ro
        g = vd_ref[0:1, :]
        dx2v = dx2_ref[...]
        don = dx2v * vd_ref[1:2, :]
        _acc_rows(vpd_ref, first, [_colsum(dx2v * on_ * g), _colsum(don * on_)])
        dxn = don * g
        dob = (ro * (dxn - on_ * _rowmean(dxn * on_))).astype(BF16)
        do_ref[...] = dob
        dyc_a =lax.dot_general(dob, wo_ref[0:dh, :], (((1,), (1,)), ((), ())), preferred_element_type=F32)
        dyc_b = lax.dot_general(dob, wo_ref[dh:2 * dh, :], (((1,), (1,)), ((), ())), preferred_element_type=F32)
        y1v = y1_ref[...]
        u = u_ref[...].astype(F32)
        z, dz_dy1 = _gelu_and_grad(y1v)
        zb = z.astype(BF16)
        sg = _sigmoid(jnp.dot(zb, gw_ref[...], preferred_element_type=F32) + p_ref[1:2, :])
        ya = z * sg
        ra = lax.rsqrt(_head_ms(ya, h16_ref) + EPS)
        yan = ya * ra
        ga = p_ref[2:3, :]
        ycat_ref[:, 0:dh] = (yan * ga).astype(BF16)
        dyn = dyc_a * ga
        dya = ra * (dyn - yan * _split_dot(dyn * yan, h16_ref[...]))
        dq = dya * z * sg * (1.0 - sg)
        dqb = dq.astype(BF16)
        z_ref[...] = zb
        dq_ref[...] = dqb
        dz = dya * sg + lax.dot_general(dqb, gw_ref[...], (((1,), (1,)), ((), ())), preferred_element_type=F32)
        dy1 = dz * dz_dy1
        dy1_ref[...] = dy1
        dy1b = dy1.astype(BF16)
        for q in range(SSM_SPLIT):
            rq, cq = slice(q * ds, (q + 1) * ds), slice(q * du, (q + 1) * du)
            gr_ref[:, rq] = lax.dot_general(dy1b[:, cq], cre_ref[rq, cq], (((1,), (1,)), ((), ())),
                                            preferred_element_type=F32).astype(BF16)
            gi_ref[:, rq] = (-lax.dot_general(dy1b[:, cq], cim_ref[rq, cq], (((1,), (1,)), ((), ())),
                                              preferred_element_type=F32)).astype(BF16)
            dcr_ref[rq, :] += _dot_tn(sr_ref[:, rq], dy1b[:, cq])
            dci_ref[rq, :] += _dot_tn(si_ref[:, rq], dy1b[:, cq])
        bg = bg_ref[...].astype(F32)
        cv = cg_ref[...].astype(F32) * v_ref[...].astype(F32)
        cvh = jnp.where(i > 0, cgh_ref[...].astype(F32) * vh_ref[...].astype(F32), 0.0)
        cv1 = _shift_down(cv, cvh, 1)
        cv2 = _shift_down(cv, cvh, 2)
        cc = cw_ref[0:1, :] * cv2 + cw_ref[1:2, :] * cv1 + cw_ref[2:3, :] * cv
        yb = bg * cc
        rb = lax.rsqrt(_head_ms(yb, h64_ref) + EPS)
        ybn = yb * rb
        gb = p_ref[3:4, :]
        ycat_ref[:, dh:2 * dh] = (ybn * gb).astype(BF16)
        dynb = dyc_b * gb
        dyb = rb * (dynb - ybn * _split_dot(dynb * ybn, h64_ref[...]))
        dcc = dyb * bg
        dbg_ref[...] = dyb * cc
        dcc_ref[...] = dcc
        _acc_rows(vp5_ref, first, [_colsum(dyc_a * yan), _colsum(dyc_b * ybn), _colsum(dq), _colsum(dy1 * u),
                                   _colsum(dcc * cv2), _colsum(dcc * cv1), _colsum(dcc * cv)])

    return pl.pallas_call(
        body, name="mix_out_bwd", grid=(t // tb,),
        out_shape=(jax.ShapeDtypeStruct((t, d), BF16), jax.ShapeDtypeStruct((t, 2 * dh), BF16),
                   jax.ShapeDtypeStruct((t, dh), BF16), jax.ShapeDtypeStruct((t, dh), BF16),
                   jax.ShapeDtypeStruct((t, dh), F32), jax.ShapeDtypeStruct((t, nstate), BF16),
                   jax.ShapeDtypeStruct((t, nstate), BF16), jax.ShapeDtypeStruct((t, dh), F32),
                   jax.ShapeDtypeStruct((t, dh), F32), jax.ShapeDtypeStruct((SUBLANES, d), F32),
                   jax.ShapeDtypeStruct((SUBLANES, dh), F32), jax.ShapeDtypeStruct((nstate, du), F32),
                   jax.ShapeDtypeStruct((nstate, du), F32)),
        in_specs=[_rows(tb, d), _rows(tb, d), _rows(tb, dh), _rows(tb, dh, 0), _rows(tb, dh, 1), _rows(tb, dh, 2),
                  _rows(tb, dh, 3), _halo_prev(tb, dh, 2, BF16_ROWS), _halo_prev(tb, dh, 3, BF16_ROWS), _resident(c_re.shape),
                  _resident(c_im.shape), _full(v512.shape), _full(convw.shape), _resident(glu_w.shape),
                  _resident(h16.shape), _resident(h64.shape), _resident(w_out.shape), _full(vd.shape),
                  _rows(tb, nstate), _rows(tb, nstate)],
        out_specs=(_rows(tb, d), _rows(tb, 2 * dh), _rows(tb, dh), _rows(tb, dh), _rows(tb, dh), _rows(tb, nstate),
                   _rows(tb, nstate), _rows(tb, dh), _rows(tb, dh), _full((SUBLANES, d)), _full((SUBLANES, dh)),
                   _full((nstate, du)), _full((nstate, du))),
        compiler_params=_cparams(("arbitrary",), VMEM_BIG),
    )(dx2, o, y1, proj, proj, proj, proj, proj, proj, c_re, c_im, v512, convw, glu_w, h16, h64, w_out, vd,
      s_re, s_im)


def _mix_in_bwd(gt_re, gt_im, b_re, b_im, dy1, dcc, dbg, proj, x, dx2, vec, v512, convw, w_in_st):
    t, d = x.shape
    dh = dy1.shape[1]
    nstate = gt_re.shape[1]
    du_w, ds = dh // SSM_SPLIT, nstate // SSM_SPLIT
    ns, _, nc = w_in_st.shape
    tb = _blk(t, TB_MIX)
    nblk = t // tb

    def body(gr_ref, gi_ref, bre_ref, bim_ref, dy1_ref, dcc_ref, dccn_ref, dbg_ref, u_ref, cg_ref, v_ref, x_ref,
             dx2_ref, vec_ref, p_ref, cw_ref, w_ref, gx_ref, dproj_ref, vp_ref, dbr_ref, dbi_ref):
        i = pl.program_id(0)

        @pl.when(i == 0)
        def _():
            dbr_ref[...] = jnp.zeros(dbr_ref.shape, F32)
            dbi_ref[...] = jnp.zeros(dbi_ref.shape, F32)
        ub = u_ref[...].astype(BF16)
        du = []
        for q in range(SSM_SPLIT):
            rq, cq = slice(q * du_w, (q + 1) * du_w), slice(q * ds, (q + 1) * ds)
            du.append(lax.dot_general(gr_ref[:, cq].astype(BF16), bre_ref[rq, cq], (((1,), (1,)), ((), ())),
                                      preferred_element_type=F32)
                      + lax.dot_general(gi_ref[:, cq].astype(BF16), bim_ref[rq, cq], (((1,), (1,)), ((), ())),
                                        preferred_element_type=F32))
            dbr_ref[rq, :] += _dot_tn(ub[:, rq], gr_ref[:, cq])
            dbi_ref[rq, :] += _dot_tn(ub[:, rq], gi_ref[:, cq])
        du = dy1_ref[...] * p_ref[0:1, :] + jnp.concatenate(du, axis=1)
        dcc = dcc_ref[...]
        dccn = jnp.where(i < nblk - 1, dccn_ref[...], 0.0)
        dcv = (cw_ref[2:3, :] * dcc + cw_ref[1:2, :] * _shift_up(dcc, dccn, 1)
               + cw_ref[0:1, :] * _shift_up(dcc, dccn, 2))
        parts = [du, dbg_ref[...], dcv * v_ref[...].astype(F32), dcv * cg_ref[...].astype(F32)]
        xv = x_ref[...]
        r = lax.rsqrt(_rowmean(xv * xv) + EPS)
        xn = xv * r
        g = vec_ref[0:1, :]
        hg = xn * g
        dh1 = None
        for j in range(ns):
            pb = parts[j].astype(BF16)
            dproj_ref[:, j * nc:(j + 1) * nc] = pb
            pj =lax.dot_general(pb, w_ref[j], (((1,), (1,)), ((), ())), preferred_element_type=F32)
            dh1 = pj if dh1 is None else dh1 + pj
        dhg = dh1 * vec_ref[1:2, :]
        _acc_rows(vp_ref, i == 0, [_colsum(dh1), _colsum(dh1 * hg), _colsum(dhg * xn)])
        dxn = dhg * g
        gx_ref[...] = dx2_ref[...] + r * (dxn - xn * _rowmean(dxn * xn))

    assert nc == dh and ns == 4
    return pl.pallas_call(
        body, name="mix_in_bwd", grid=(nblk,),
        out_shape=(jax.ShapeDtypeStruct((t, d), F32), jax.ShapeDtypeStruct((t, ns * nc), BF16),
                   jax.ShapeDtypeStruct((SUBLANES, d), F32), jax.ShapeDtypeStruct((dh, ds), F32),
                   jax.ShapeDtypeStruct((dh, ds), F32)),
        in_specs=[_rows(tb, nstate), _rows(tb, nstate), _resident(b_re.shape), _resident(b_im.shape), _rows(tb, dh),
                  _rows(tb, dh), _halo_next(tb, dh, t), _rows(tb, dh), _rows(tb, dh, 0), _rows(tb, dh, 2),
                  _rows(tb, dh, 3), _rows(tb, d), _rows(tb, d), _full(vec.shape), _full(v512.shape),
                  _full(convw.shape), _resident(w_in_st.shape)],
        out_specs=(_rows(tb, d), _rows(tb, ns * nc), _full((SUBLANES, d)), _full((dh, ds)), _full((dh, ds))),
        compiler_params=_cparams(("arbitrary",), VMEM_BIG),
    )(gt_re, gt_im, b_re, b_im, dy1, dcc, dcc, dbg, proj, proj, proj, x, dx2, vec, v512, convw, w_in_st)


def _matmul_tn(a, b, m, bn, out_dtype, name, diag=False, bt=TB_TN, after=None):
    t = a.shape[0]
    n = b.shape[1]
    bt = _blk(t, bt)
    nk = t // bt
    extra = [] if after is None else [after]
    a_map = (lambda j, k: (k, j)) if diag else (lambda j, k: (k, 0))

    def body(a_ref, b_ref, *rest):
        o_ref, acc_ref = rest[-2:]
        k = pl.program_id(1)

        @pl.when(k == 0)
        def _():
            acc_ref[...] = jnp.zeros(acc_ref.shape, F32)
        acc_ref[...] += _dot_tn(a_ref[...], b_ref[...])

        @pl.when(k == nk - 1)
        def _():
            o_ref[...] = acc_ref[...].astype(out_dtype)

    def body_one_pass(a_ref, b_ref, *rest):
        rest[-1][...] = _dot_tn(a_ref[...], b_ref[...]).astype(out_dtype)

    return pl.pallas_call(
        body_one_pass if nk == 1 else body, name=name, grid=(n // bn, nk),
        out_shape=jax.ShapeDtypeStruct((n // bn, m, bn), out_dtype),
        in_specs=[pl.BlockSpec((bt, m), a_map), pl.BlockSpec((bt, bn), lambda j, k: (k, j))]
        + [pl.BlockSpec(memory_space=pl.ANY)] * len(extra),
        out_specs=pl.BlockSpec((None, m, bn), lambda j, k: (j, 0, 0)),
        scratch_shapes=[] if nk == 1 else [pltpu.VMEM((m, bn), F32)],
        compiler_params=_cparams(("parallel", "arbitrary"), VMEM_BIG),
    )(a, b, *extra)


def _ssm_bgrad(d_bre, d_bim, bt_re, bt_im, rows_in, fold, tile_b):
    gh, cb = d_bre.shape
    nb = SSM_SPLIT
    rb = gh // nb
    gp = nb * cb
    p = fold.shape[1]

    def body(dr_ref, di_ref, br_ref, bi_ref, rin_ref, f_ref, tb_ref, dbr_ref, dbi_ref, rout_ref):
        row = lax.broadcasted_iota(jnp.int32, (rb, cb), 0)
        col = lax.broadcasted_iota(jnp.int32, (rb, cb), 1)
        mask = (row >> 4) == (col >> 6)
        gr = jnp.where(mask, dr_ref[...], 0.0)
        gi = jnp.where(mask, di_ref[...], 0.0)
        cr, ci = rin_ref[0:1, :], rin_ref[1:2, :]
        dbr_ref[...] = _split3_dot(cr * gr + ci * gi, f_ref[...])
        dbi_ref[...] = _split3_dot(cr * gi - ci * gr, f_ref[...])
        br = _split3_dot(br_ref[...], tb_ref[...])
        bi = _split3_dot(bi_ref[...], tb_ref[...])
        rout_ref[...] = jnp.zeros(rout_ref.shape, F32)
        rout_ref[0:1, :] = _colsum(br * gr + bi * gi)
        rout_ref[1:2, :] = _colsum(br * gi - bi * gr)

    dspec = pl.BlockSpec((rb, cb), lambda j: (j, 0))
    rspec = pl.BlockSpec((SUBLANES, cb), lambda j: (0, j))
    ospec = pl.BlockSpec((rb, p), lambda j: (j, 0))
    return pl.pallas_call(
        body, name="ssm_bgrad", grid=(nb,),
        out_shape=(jax.ShapeDtypeStruct((gh, p), F32), jax.ShapeDtypeStruct((gh, p), F32),
                   jax.ShapeDtypeStruct((SUBLANES, gp), F32)),
        in_specs=[dspec, dspec, ospec, ospec, rspec, _full(fold.shape), _full(tile_b.shape)],
        out_specs=(ospec, ospec, rspec),
        compiler_params=_cparams(("parallel",)),
    )(d_bre, d_bim, bt_re, bt_im, rows_in, fold, tile_b)


def _ssm_cgrad(d_cre, d_cim, fold):
    gp, cb = d_cre.shape
    nb = SSM_SPLIT
    rb = gp // nb
    h = fold.shape[1]

    def body(dr_ref, di_ref, f_ref, cr_ref, ci_ref):
        row = lax.broadcasted_iota(jnp.int32, (rb, cb), 0)
        col = lax.broadcasted_iota(jnp.int32, (rb, cb), 1)
        mask = (row >> 6) == (col >> 4)
        cr_ref[...] = _split3_dot(jnp.where(mask, dr_ref[...], 0.0), f_ref[...])
        ci_ref[...] = -_split3_dot(jnp.where(mask, di_ref[...], 0.0), f_ref[...])

    cspec = pl.BlockSpec((rb, cb), lambda j: (j, 0))
    ospec = pl.BlockSpec((rb, h), lambda j: (j, 0))
    return pl.pallas_call(
        body, name="ssm_cgrad", grid=(nb,),
        out_shape=(jax.ShapeDtypeStruct((gp, h), F32),) * 2,
        in_specs=[cspec, cspec, _full(fold.shape)], out_specs=(ospec, ospec),
        compiler_params=_cparams(("parallel",)),
    )(d_cre, d_cim, fold)


def _ssm_lamgrad(lam_re, lam_im, log_step, abar_re, abar_im, coef_re, coef_im, gc_re, gc_im, ga_re, ga_im):
    g, p = lam_re.shape

    def body(lr_ref, li_ref, ls_ref, ar_ref, ai_ref, cr_ref, ci_ref, gcr_ref, gci_ref, gar_ref, gai_ref,
             dlr_ref, dli_ref, dls_ref):
        lam_raw = lr_ref[...]
        lr = jnp.minimum(lam_raw, LAMBDA_RE_MAX)
        li = li_ref[...]
        st = jnp.exp(ls_ref[...])
        den = lr * lr + li * li
        gcr, gci = gcr_ref[...], gci_ref[...]
        gab_r = gar_ref[...] + (lr * gcr - li * gci) / den
        gab_i = gai_ref[...] + (lr * gci + li * gcr) / den
        cr, ci = cr_ref[...], ci_ref[...]
        wr = -(cr * lr + ci * li) / den
        wi = -(ci * lr - cr * li) / den
        gl_r = wr * gcr + wi * gci
        gl_i = wr * gci - wi * gcr
        ar, ai = ar_ref[...], ai_ref[...]
        gw_r = ar * gab_r + ai * gab_i
        gw_i = ar * gab_i - ai * gab_r
        gl_r = gl_r + st * gw_r
        gl_i = gl_i + st * gw_i
        pass_through = jnp.where(lam_raw < LAMBDA_RE_MAX, 1.0, jnp.where(lam_raw == LAMBDA_RE_MAX, 0.5, 0.0))
        dlr_ref[...] = gl_r * pass_through
        dli_ref[...] = gl_i
        dls_ref[...] = st * jnp.sum(lr * gw_r + li * gw_i, axis=1, keepdims=True)

    sds = jax.ShapeDtypeStruct((g, p), F32)
    return pl.pallas_call(body, name="ssm_lamgrad", out_shape=(sds, sds, jax.ShapeDtypeStruct((g, 1), F32)))(
        lam_re, lam_im, log_step, abar_re, abar_im, coef_re, coef_im, gc_re, gc_im, ga_re, ga_im)


def _row_block(r, most=512):
    for rb in range(min(r, most), BF16_ROWS - 1, -1):
        if r % rb == 0 and rb % BF16_ROWS == 0:
            return rb
    return r


def _adamw_math(w, g, m, v):
    m = ADAM_B1 * m + (1.0 - ADAM_B1) * g
    v = ADAM_B2 * v + (1.0 - ADAM_B2) * (g * g)
    m_hat = m / (1.0 - ADAM_B1 ** ADAM_STEP)
    v_hat = v / (1.0 - ADAM_B2 ** ADAM_STEP)
    delta = -ADAM_LR * (m_hat / (jnp.sqrt(v_hat) + ADAM_EPS) + ADAM_WD * w)
    return delta, m, v


def _adamw_big(p_mine, p_sib, w, m, v, name):
    r, c = w.shape
    rb = _row_block(r)

    def body(a_ref, b_ref, w_ref, m_ref, v_ref, g_ref, d_ref, mo_ref, vo_ref):
        g = a_ref[...].astype(F32) + b_ref[...].astype(F32)
        g_ref[...] = g
        d_ref[...], mo_ref[...], vo_ref[...] = _adamw_math(w_ref[...], g, m_ref[...], v_ref[...])

    spec = pl.BlockSpec((rb, c), lambda i: (i, 0))
    sds = jax.ShapeDtypeStruct((r, c), F32)
    return pl.pallas_call(
        body, name=name, grid=(r // rb,), out_shape=(sds,) * 4, in_specs=[spec] * 5, out_specs=(spec,) * 4,
        compiler_params=_cparams(("parallel",), VMEM_KEEP_OPERANDS_IN_HBM),
    )(p_mine, p_sib, w, m, v)


def _sum_blocks(stack, name):
    n, r, c = stack.shape
    rb = _row_block(r)

    def body(s_ref, o_ref):
        acc = s_ref[0].astype(F32)
        for k in range(1, n):
            acc = acc + s_ref[k].astype(F32)
        o_ref[...] = acc

    return pl.pallas_call(
        body, name=name, grid=(r // rb,), out_shape=jax.ShapeDtypeStruct((r, c), F32),
        in_specs=[pl.BlockSpec((n, rb, c), lambda i: (0, i, 0))], out_specs=pl.BlockSpec((rb, c), lambda i: (i, 0)),
        compiler_params=_cparams(("parallel",), VMEM_KEEP_OPERANDS_IN_HBM),
    )(stack)


def _sum_landed(landed, own, chip, name):
    n, r, c = landed.shape
    rb = _row_block(r)

    def body(chip_ref, own_ref, l1_ref, l2_ref, l3_ref, o_ref):
        acc = own_ref[0].astype(F32)
        for ref in (l1_ref, l2_ref, l3_ref):
            acc = acc + ref[0].astype(F32)
        o_ref[...] = acc.astype(BF16)

    def slot(k):
        return pl.BlockSpec((1, rb, c), lambda i, ch: ((ch[0] + k) % n, i, 0))

    return pl.pallas_call(
        body, name=name, out_shape=jax.ShapeDtypeStruct((r, c), BF16),
        grid_spec=pltpu.PrefetchScalarGridSpec(
            num_scalar_prefetch=1, grid=(r // rb,), in_specs=[slot(0), slot(1), slot(2), slot(3)],
            out_specs=pl.BlockSpec((rb, c), lambda i, ch: (i, 0))),
        compiler_params=_cparams(("parallel",), VMEM_KEEP_OPERANDS_IN_HBM),
    )(jnp.reshape(chip, (1,)).astype(jnp.int32), own, landed, landed, landed)


def _add2(a, b):
    def body(a_ref, b_ref, o_ref):
        o_ref[...] = a_ref[...] + b_ref[...]

    return pl.pallas_call(body, name="add_small", out_shape=jax.ShapeDtypeStruct(a.shape, F32))(a, b)


def _adamw_ada(c_all, dmod_cols, w, m, v):
    d, n = w.shape
    bn = 512

    def body(c_ref, dm_ref, w_ref, m_ref, v_ref, g_ref, d_ref, mo_ref, vo_ref):
        cc = c_ref[...]
        g = _dot_tn(cc * _sigmoid(cc), dm_ref[...])
        g_ref[...] = g
        d_ref[...], mo_ref[...], vo_ref[...] = _adamw_math(w_ref[...], g, m_ref[...], v_ref[...])

    spec = pl.BlockSpec((d, bn), lambda j: (0, j))
    sds = jax.ShapeDtypeStruct((d, n), F32)
    return pl.pallas_call(
        body, name="adamw_ada", grid=(n // bn,), out_shape=(sds,) * 4,
        in_specs=[_full((N_DEV, d)), pl.BlockSpec((N_DEV, bn), lambda j: (0, j)), spec, spec, spec],
        out_specs=(spec,) * 4, compiler_params=_cparams(("parallel",), VMEM_KEEP_OPERANDS_IN_HBM),
    )(c_all, dmod_cols, w, m, v)


def _adamw_small(items):
    n = len(items)

    def body(*refs):
        ins, outs = refs[:4 * n], refs[4 * n:]
        for k in range(n):
            w_ref, g_ref, m_ref, v_ref = ins[4 * k:4 * k + 4]
            outs[3 * k][...], outs[3 * k + 1][...], outs[3 * k + 2][...] = _adamw_math(
                w_ref[...], g_ref[...], m_ref[...], v_ref[...])

    flat = [a for it in items for a in it]
    out_shape = tuple(jax.ShapeDtypeStruct(it[0].shape, F32) for it in items for _ in range(3))
    res = pl.pallas_call(body, name="adamw_small", out_shape=out_shape,
                         compiler_params=_cparams(vmem=VMEM_KEEP_OPERANDS_IN_HBM))(*flat)
    return [tuple(res[3 * k:3 * k + 3]) for k in range(n)]


def _group_mean_matrix(n, group):
    idx = np.arange(n) // group
    return (idx[:, None] == idx[None, :]).astype(np.float32) / group


def _fold_matrix(n, period):
    return (np.arange(n)[:, None] % period == np.arange(period)[None, :]).astype(np.float32)


def _rows8(*rows):
    c = rows[0].shape[-1]
    pad = jnp.zeros((SUBLANES - len(rows), c), F32)
    return jnp.concatenate([r.reshape(1, c) for r in rows] + [pad], axis=0)


def _to_rows(a, width):
    flat = a.reshape(-1)
    n = -(-flat.shape[0] // width)
    flat = jnp.pad(flat, (0, n * width - flat.shape[0]))
    return flat.reshape(n, width)


def kernel(x, c, w_ada, b_ada, g_pre_mix, g_post_mix, w_in, ssm_lam_re, ssm_lam_im, ssm_log_step, ssm_b_re, ssm_b_im, ssm_c_re, ssm_c_im, ssm_d, glu_w, glu_b, g_out_ssm, conv_w, g_out_conv, w_out, g_pre_ffn, g_post_ffn, w_up, ffn_conv_w, w_down, loss_target, m_w_ada, m_b_ada, m_g_pre_mix, m_g_post_mix, m_w_in, m_ssm_lam_re, m_ssm_lam_im, m_ssm_log_step, m_ssm_b_re, m_ssm_b_im, m_ssm_c_re, m_ssm_c_im, m_ssm_d, m_glu_w, m_glu_b, m_g_out_ssm, m_conv_w, m_g_out_conv, m_w_out, m_g_pre_ffn, m_g_post_ffn, m_w_up, m_ffn_conv_w, m_w_down, v_w_ada, v_b_ada, v_g_pre_mix, v_g_post_mix, v_w_in, v_ssm_lam_re, v_ssm_lam_im, v_ssm_log_step, v_ssm_b_re, v_ssm_b_im, v_ssm_c_re, v_ssm_c_im, v_ssm_d, v_glu_w, v_glu_b, v_g_out_ssm, v_conv_w, v_g_out_conv, v_w_out, v_g_pre_ffn, v_g_post_ffn, v_w_up, v_ffn_conv_w, v_w_down):
    xs = x[0]
    tgt = loss_target[0]
    t, d = xs.shape
    xi, yi, ci = lax.axis_index("x"), lax.axis_index("y"), lax.axis_index("c")
    chip = 2 * xi + yi
    dev = 2 * chip + ci

    n_groups, n_state = ssm_lam_re.shape[1:]
    n_gch = ssm_b_re.shape[3]
    d_ssm = n_groups * n_gch
    gp = n_groups * n_state
    n_ada = w_ada.shape[2]
    d_ff = w_down.shape[1] * N_CHIPS
    n_upc = w_up.shape[2]

    w_names = ("w_in", "glu_w", "w_out", "w_up", "w_down")
    c_gath = _allgather8(jnp.broadcast_to(c, (SUBLANES, d)), "gather_c")
    c_all = c_gath.reshape(N_DEV, SUBLANES, d)[:, 0, :]

    def pad8(a):
        return jnp.concatenate([a, jnp.zeros((SUBLANES - a.shape[0], a.shape[1]), a.dtype)], axis=0)

    def start(name, arrs, after):
        return _chips_start(name, True, [], [_landing(a, chip) for a in arrs], after)

    w_names = ("w_in", "mod", "conv_w", "ffn_conv_w", "glu_w", "w_out", "w_up", "w_down")
    first = start("weights_start_in", [w_in[0].astype(BF16)], c_gath)
    b_sh = lax.dynamic_slice(b_ada, (0, chip * n_ada), (1, n_ada))
    mod_sh = _mod_shard(c_all + first[4][0:1, 0:1], w_ada[0], b_sh)
    second = start("weights_start_rest", [mod_sh, pad8(conv_w[0]), pad8(ffn_conv_w[0])]
                   + [w[0].astype(BF16) for w in (glu_w, w_out, w_up, w_down)], None)
    w_send, w_recv, w_land = [list(first[k]) + list(second[k]) for k in (0, 1, 3)]
    w_token = second[4]

    def weights(names, after):
        ks = [w_names.index(nm) for nm in names]
        return _chips_wait("weights_wait_" + names[-1], True, [w_send[k] for k in ks], [w_recv[k] for k in ks],
                           [], [w_land[k] for k in ks], after)[1]

    lam_re, lam_im = ssm_lam_re[0], ssm_lam_im[0]
    log_step = ssm_log_step[0].reshape(n_groups, 1) + w_token[0:1, 0:1]
    abar_re, abar_im, coef_re, coef_im = _ssm_prep(lam_re, lam_im, log_step)
    a_rows = _rows8(abar_re.reshape(1, gp), abar_im.reshape(1, gp))
    coef_rows = _rows8(coef_re.reshape(1, gp), coef_im.reshape(1, gp))
    bt_re = ssm_b_re[0].transpose(0, 2, 1).reshape(d_ssm, n_state)
    bt_im = ssm_b_im[0].transpose(0, 2, 1).reshape(d_ssm, n_state)
    ct_re = ssm_c_re[0].transpose(0, 2, 1).reshape(gp, n_gch)
    ct_im = ssm_c_im[0].transpose(0, 2, 1).reshape(gp, n_gch)
    tile_b = jnp.asarray(np.tile(np.eye(n_state), (1, n_groups // SSM_SPLIT)), BF16)
    tile_c = jnp.asarray(np.tile(np.eye(n_gch), (1, n_groups)), BF16)
    bblk_re, bblk_im, cblk_re, cblk_im = _ssm_blocks(bt_re, bt_im, ct_re, ct_im, coef_rows, tile_b, tile_c)

    h16 = jnp.asarray(_group_mean_matrix(d_ssm, n_gch), BF16)
    h64 = jnp.asarray(_group_mean_matrix(d_ssm, CONV_HEAD_DIM), BF16)

    g_mod, g_cw, g_fw, w_in_st = weights(("mod", "conv_w", "ffn_conv_w", "w_in"), bblk_re)
    mod_all = g_mod.transpose(1, 0, 2).reshape(N_DEV, N_CHIPS * n_ada)
    mod = lax.dynamic_slice(mod_all, (dev, 0), (1, N_CHIPS * n_ada))
    sh1, sc1, gt1, sh2, sc2, gt2 = [mod[:, k * d:(k + 1) * d] for k in range(6)]
    convw_full = pad8(g_cw[:, :3, :].transpose(1, 0, 2).reshape(3, d_ssm))
    fw_full = pad8(g_fw[:, :3, :].transpose(1, 0, 2).reshape(3, N_CHIPS * n_upc))

    v512 = _rows8(ssm_d, glu_b, g_out_ssm, g_out_conv)
    vec1 =_rows8(g_pre_mix, 1.0 + sc1, sh1)
    vd1 = _rows8(g_post_mix, gt1)
    vec2 = _rows8(g_pre_ffn, 1.0 + sc2, sh2)
    vd2 = _rows8(g_post_ffn, gt2)

    proj, bu_re, bu_im, h1b = _mix_in(xs, vec1, w_in_st, bblk_re, bblk_im)
    s_re, s_im = _scan_fwd(a_rows, bu_re, bu_im)
    g_glu, g_wout = weights(("glu_w", "w_out"), s_re)
    glu_full = g_glu.reshape(d_ssm, d_ssm)
    w_out_full = g_wout.reshape(2 * d_ssm, d)
    y1, o_mix, x2 = _mix_out(xs, proj, s_re, s_im, cblk_re, cblk_im, v512, convw_full, glu_full, h16, h64,
                             w_out_full, vd1)
    (w_up_st,) = weights(("w_up",), x2)
    up, h2b = _ffn_up(x2, vec2, w_up_st)
    (g_wdown,) = weights(("w_down",), up)
    w_down_full = g_wdown.reshape(d_ff, d)
    actb, ddnb, dout, dhid, vp_dn, loss_blk = _ffn_down(up, fw_full, w_down_full, w_down_full.T, x2, tgt, vd2)

    gw_down = _matmul_tn(actb, ddnb, d_ff, d, BF16, "dw_down", bt=1024).reshape(N_CHIPS, d_ff // N_CHIPS, d)
    dx2, dupb, vp_up, df_rows = _ffn_up_bwd(dhid, up, fw_full, x2, dout, vec2, w_up_st)
    gw_up = _matmul_tn(h2b, dupb, d, n_upc, BF16, "dw_up", bt=4096)
    ga_send, ga_recv, ga_src, ga_land, ga_token = _chips_start(
        "grads_start_ffn", False, [gw_down, gw_up], [lax.empty(g.shape, g.dtype) for g in (gw_down, gw_up)])
    (dob, ycatb, zb, dqb, dy1, g_re, g_im, dcc, dbg, vp_mo, vp5, d_cre, d_cim) = _mix_out_bwd(
        dx2, o_mix, y1, proj, s_re, s_im, cblk_re, cblk_im, v512, convw_full, glu_full, h16, h64, w_out_full,
        vd1 + ga_token[0:1, 0:1])
    gw_out = _matmul_tn(ycatb, dob, 2 * d_ssm, d, BF16, "dw_out", bt=4096)
    gw_out = gw_out.reshape(N_CHIPS, 2 * d_ssm // N_CHIPS, d)
    gw_glu = _matmul_tn(zb, dqb, d_ssm, d_ssm, BF16, "dw_glu", bt=4096).reshape(N_CHIPS, d_ssm // N_CHIPS, d_ssm)
    gb_send, gb_recv, gb_src, gb_land, gb_token = _chips_start(
        "grads_start_mix", False, [gw_out, gw_glu], [lax.empty(g.shape, g.dtype) for g in (gw_out, gw_glu)])
    gt_re, gt_im, ga_re8, ga_im8 = _scan_bwd(a_rows + gb_token[0:1, 0:1], g_re, g_im, s_re, s_im)
    grad_x, dprojb, vp_mi, d_bre, d_bim = _mix_in_bwd(gt_re, gt_im, bblk_re, bblk_im, dy1, dcc, dbg, proj, xs, dx2,
                                                      vec1, v512, convw_full, w_in_st)
    ssm_u, ssm_s = d_ssm // SSM_SPLIT, gp // SSM_SPLIT

    fold_b = jnp.asarray(_fold_matrix(ssm_s, n_state), BF16)
    fold_c = jnp.asarray(_fold_matrix(ssm_u, n_gch), BF16)
    db_re_f, db_im_f, gc_rows = _ssm_bgrad(d_bre, d_bim, bt_re, bt_im, coef_rows, fold_b, tile_b)
    dc_re_f, dc_im_f = _ssm_cgrad(d_cre, d_cim, fold_c)
    ga_sum = _ga_rowsum(ga_re8, ga_im8)
    g_lam_re, g_lam_im, g_log_step = _ssm_lamgrad(
        lam_re, lam_im, log_step, abar_re, abar_im, coef_re, coef_im,
        gc_rows[0].reshape(n_groups, n_state), gc_rows[1].reshape(n_groups, n_state),
        ga_sum[0].reshape(n_groups, n_state), ga_sum[1].reshape(n_groups, n_state))
    g_b_re = db_re_f.reshape(n_groups, n_gch, n_state).transpose(0, 2, 1)
    g_b_im = db_im_f.reshape(n_groups, n_gch, n_state).transpose(0, 2, 1)
    g_c_re = dc_re_f.reshape(n_groups, n_state, n_gch).transpose(0, 2, 1)
    g_c_im = dc_im_f.reshape(n_groups, n_state, n_gch).transpose(0, 2, 1)

    dmod = jnp.concatenate([vp_mi[0:1], vp_mi[1:2], vp_mo[0:1], vp_up[0:1], vp_up[1:2], vp_dn[0:1]], axis=1)
    small = [
        ("g_pre_mix", vp_mi[2:3]), ("g_post_mix", vp_mo[1:2]), ("g_pre_ffn", vp_up[2:3]), ("g_post_ffn", vp_dn[1:2]),
        ("ssm_lam_re", g_lam_re), ("ssm_lam_im", g_lam_im), ("ssm_log_step", g_log_step),
        ("ssm_b_re", g_b_re), ("ssm_b_im", g_b_im), ("ssm_c_re", g_c_re), ("ssm_c_im", g_c_im),
        ("ssm_d", vp5[3:4]), ("glu_b", vp5[2:3]), ("g_out_ssm", vp5[0:1]), ("g_out_conv", vp5[1:2]),
        ("conv_w", vp5[4:7]), ("ffn_conv_w", df_rows[0:3]), ("loss", loss_blk[0:1, 0:1]),
    ]
    packed, offsets, row = [], {}, 0
    for name, a in small:
        r = _to_rows(a, d)
        offsets[name] = (row, a.shape)
        packed.append(r)
        row += r.shape[0]
    n_small = -(-row // SUBLANES) * SUBLANES
    packed.append(jnp.zeros((n_small - row, d), F32))
    packed.append(pad8(dmod.reshape(6, d)))
    pack = jnp.concatenate(packed, axis=0)
    sm_send, sm_recv, _, sm_land, sm_token = _chips_start("small_start", True, [], [_landing(pack, chip)])

    gw_in = _matmul_tn(h1b, dprojb, d, w_in.shape[2], BF16, "dw_in", bt=4096, after=sm_token)
    gc_send, gc_recv, gc_src, gc_land, gc_token = _chips_start(
        "grads_start_in", False, [gw_in], [lax.empty(gw_in.shape, gw_in.dtype)])

    def partials(names, own, landed):
        return [_sum_landed(l, o, chip, "sum_" + nm) for l, o, nm in zip(landed, own, names)]

    def update(names, mine, theirs):
        done = {}
        for nm, pm, ps in zip(names, mine, theirs):
            w_, m_, v_ = big_params[nm]
            done[nm] = _adamw_big(pm, ps, w_[0], m_[0], v_[0], "adamw_" + nm)
        return done

    big_params = {"w_down": (w_down, m_w_down, v_w_down), "w_up": (w_up, m_w_up, v_w_up),
                  "w_out": (w_out, m_w_out, v_w_out), "glu_w": (glu_w, m_glu_w, v_glu_w),
                  "w_in": (w_in, m_w_in, v_w_in)}
    ffn_names, mix_names = ("w_down", "w_up"), ("w_out", "glu_w", "w_in")
    p_ffn = partials(ffn_names, *_chips_wait("grads_wait_ffn", False, ga_send, ga_recv, ga_src, ga_land, gc_token))
    sa_send, sa_recv, sa_src, sa_land, sa_token = _sibling_start("swap_start_ffn", p_ffn)

    (sm_landed,) = _chips_wait("small_wait", True, sm_send, sm_recv, [], sm_land, sa_token)[1]
    sm_part = _sum_blocks(sm_landed, "sum_small")
    dmod_mine = sm_landed[:, n_small:n_small + SUBLANES, :]
    ss_send, ss_recv, ss_src, ss_land, ss_token = _sibling_start("swap_start_small", [sm_part, dmod_mine])
    p_ffn, t_ffn = _sibling_wait("swap_wait_ffn", sa_send, sa_recv, sa_src, sa_land, ss_token)
    big = update(ffn_names, p_ffn, t_ffn)
    (sm_part, dmod_mine), (sm_sib, dmod_sib) = _sibling_wait("swap_wait_small", ss_send, ss_recv, ss_src, ss_land,
                                                              big["w_up"][0])
    sums = _add2(sm_part, sm_sib)
    dmod_by_core = jnp.stack([dmod_mine, dmod_sib], axis=1)
    dmod_by_core = jnp.where(ci == 0, dmod_by_core, dmod_by_core[:, ::-1])
    dmod_all = dmod_by_core[:, :, :6, :].reshape(N_DEV, 6 * d)
    g_b_ada = sums[n_small:n_small + 6].reshape(1, 6 * d)

    def unpack(name):
        r0, shape = offsets[name]
        size = math.prod(shape)
        nrow = -(-size // d)
        return sums[r0:r0 + nrow].reshape(-1)[:size].reshape(shape)

    p_mix = partials(mix_names, *_chips_wait(
        "grads_wait_mix", False, list(gb_send) + list(gc_send), list(gb_recv) + list(gc_recv),
        list(gb_src) + list(gc_src), list(gb_land) + list(gc_land), sums))
    sb_send, sb_recv, sb_src, sb_land, sb_token = _sibling_start("swap_start_mix", p_mix)

    dmod_cols = lax.dynamic_slice(dmod_all, (0, chip * n_ada), (N_DEV, n_ada)) + sb_token[0:1, 0:1]
    ada = _adamw_ada(c_all, dmod_cols, w_ada[0], m_w_ada[0], v_w_ada[0])
    p_mix, t_mix = _sibling_wait("swap_wait_mix", sb_send, sb_recv, sb_src, sb_land, ada[0])
    big.update(update(mix_names, p_mix, t_mix))

    g_small = {name: unpack(name) for name, _ in small}
    g_small["b_ada"] = g_b_ada
    g_small["conv_w"] = lax.dynamic_slice(g_small["conv_w"], (0, chip * conv_w.shape[2]), (3, conv_w.shape[2]))
    g_small["ffn_conv_w"] = lax.dynamic_slice(g_small["ffn_conv_w"], (0, chip * n_upc), (3, n_upc))
    g_small["ssm_log_step"] = g_small["ssm_log_step"].reshape(1, n_groups)
    small_params = {
        "b_ada": (b_ada, m_b_ada, v_b_ada), "g_pre_mix": (g_pre_mix, m_g_pre_mix, v_g_pre_mix),
        "g_post_mix": (g_post_mix, m_g_post_mix, v_g_post_mix), "ssm_lam_re": (ssm_lam_re, m_ssm_lam_re, v_ssm_lam_re),
        "ssm_lam_im": (ssm_lam_im, m_ssm_lam_im, v_ssm_lam_im),
        "ssm_log_step": (ssm_log_step, m_ssm_log_step, v_ssm_log_step),
        "ssm_b_re": (ssm_b_re, m_ssm_b_re, v_ssm_b_re), "ssm_b_im": (ssm_b_im, m_ssm_b_im, v_ssm_b_im),
        "ssm_c_re": (ssm_c_re, m_ssm_c_re, v_ssm_c_re), "ssm_c_im": (ssm_c_im, m_ssm_c_im, v_ssm_c_im),
        "ssm_d": (ssm_d, m_ssm_d, v_ssm_d), "glu_b": (glu_b, m_glu_b, v_glu_b),
        "g_out_ssm": (g_out_ssm, m_g_out_ssm, v_g_out_ssm), "conv_w": (conv_w, m_conv_w, v_conv_w),
        "g_out_conv": (g_out_conv, m_g_out_conv, v_g_out_conv), "g_pre_ffn": (g_pre_ffn, m_g_pre_ffn, v_g_pre_ffn),
        "g_post_ffn": (g_post_ffn, m_g_post_ffn, v_g_post_ffn),
        "ffn_conv_w": (ffn_conv_w, m_ffn_conv_w, v_ffn_conv_w),
    }

    def natural(a):
        return a[0] if a.ndim > 2 else a

    names = list(small_params)
    items = []
    for nm in names:
        w_, m_, v_ = small_params[nm]
        items.append((natural(w_), g_small[nm].reshape(natural(w_).shape), natural(m_), natural(v_)))
    upd = _adamw_small(items)
    small_out = {}
    for nm, (dl, mo, vo) in zip(names, upd):
        shp = small_params[nm][0].shape
        small_out[nm] = (g_small[nm].reshape(shp), dl.reshape(shp), mo.reshape(shp), vo.reshape(shp))

    loss = g_small["loss"][0, 0]

    order = ["w_ada", "b_ada", "g_pre_mix", "g_post_mix", "w_in", "ssm_lam_re", "ssm_lam_im", "ssm_log_step",
             "ssm_b_re", "ssm_b_im", "ssm_c_re", "ssm_c_im", "ssm_d", "glu_w", "glu_b", "g_out_ssm", "conv_w",
             "g_out_conv", "w_out", "g_pre_ffn", "g_post_ffn", "w_up", "ffn_conv_w", "w_down"]
    results = {"w_ada": tuple(a[None] for a in ada)}
    for nm in big:
        results[nm] = tuple(a[None] for a in big[nm])
    results.update(small_out)
    outs = [loss, grad_x[None]]
    for k in range(4):
        outs += [results[nm][k] for nm in order]
    return tuple(outs)


def _ga_rowsum(ga_re8, ga_im8):
    n = ga_re8.shape[1]

    def body(r_ref, i_ref, o_ref):
        o_ref[...] = jnp.zeros(o_ref.shape, F32)
        o_ref[0:1, :] = _colsum(r_ref[...])
        o_ref[1:2, :] = _colsum(i_ref[...])

    return pl.pallas_call(body, name="ga_rowsum", out_shape=jax.ShapeDtypeStruct((SUBLANES, n), F32))(ga_re8, ga_im8)
```

```python
import functools
import math

import jax
import jax.numpy as jnp
import numpy as np
from jax import lax
from jax.experimental import pallas as pl
from jax.experimental.pallas import tpu as pltpu

F32 = jnp.float32
BF16 = jnp.bfloat16
MESH = pl.DeviceIdType.MESH

EPS = 1e-6
LAMBDA_RE_MAX = -1e-4
ADAM_LR = 0.001
ADAM_B1 = 0.9
ADAM_B2 = 0.999
ADAM_EPS = 1e-08
ADAM_WD = 0.01
ADAM_STEP = 10

SUBLANES = 8
BF16_ROWS = 16
N_CHIPS = 4
N_DEV = 8
CONV_HEAD_DIM = 64
VMEM_BIG = 56 * 1024 * 1024
VMEM_MID = 40 * 1024 * 1024
VMEM_KEEP_OPERANDS_IN_HBM = 62 * 1024 * 1024

TB_MIX = 256
TB_MIX_FWD = 512
TB_FFN = 256
TB_FFN_UP = 512
TB_SCAN = 2048
W_SCAN = 256
SSM_SPLIT = 4
CW_FFN = 256
SCAN_UNROLL = 4
TB_TN = 512


def _cparams(sem=None, vmem=None):
    kw = {}
    if sem is not None:
        kw["dimension_semantics"] = sem
    if vmem is not None:
        kw["vmem_limit_bytes"] = vmem
    return pltpu.CompilerParams(**kw)


def _blk(t, pref):
    return pref if t % pref == 0 else t


def _dot(a, b):
    return jnp.dot(a.astype(BF16), b.astype(BF16), preferred_element_type=F32)


def _dot_nt(a, b):
    return lax.dot_general(a.astype(BF16), b.astype(BF16), (((1,), (1,)), ((), ())),
                           preferred_element_type=F32)


def _dot_tn(a, b):
    return lax.dot_general(a.astype(BF16), b.astype(BF16), (((0,), (0,)), ((), ())),
                           preferred_element_type=F32)


def _sigmoid(x):
    return 0.5 * jnp.tanh(0.5 * x) + 0.5


_GELU_K = math.sqrt(2.0 / math.pi)
_GELU_C = 0.044715


def _gelu(x):
    th = jnp.tanh(_GELU_K * (x + _GELU_C * x * x * x))
    return x * (0.5 * (1.0 + th))


def _gelu_and_grad(x):
    x2 = x * x
    th = jnp.tanh(_GELU_K * (x + _GELU_C * x2 * x))
    half = 0.5 * (1.0 + th)
    return x * half, half + 0.5 * x * (1.0 - th * th) * _GELU_K * (1.0 + 3.0 * _GELU_C * x2)


def _rowmean(x):
    return jnp.mean(x, axis=-1, keepdims=True)


def _colsum(x):
    return jnp.sum(x, axis=0, keepdims=True)


def _split_dot(x, m):
    hi = x.astype(BF16)
    lo = (x - hi.astype(F32)).astype(BF16)
    return (jnp.dot(hi, m, preferred_element_type=F32) + jnp.dot(lo, m, preferred_element_type=F32))


def _split3_dot(x, m):
    hi = x.astype(BF16)
    r1 = x - hi.astype(F32)
    mid = r1.astype(BF16)
    lo = (r1 - mid.astype(F32)).astype(BF16)
    return (jnp.dot(hi, m, preferred_element_type=F32) + jnp.dot(mid, m, preferred_element_type=F32)
            + jnp.dot(lo, m, preferred_element_type=F32))


def _shift_down(x, halo, k):
    r = pltpu.roll(x, k, 0)
    row = lax.broadcasted_iota(jnp.int32, x.shape, 0)
    last = halo.shape[0]
    for j in range(k):
        r = jnp.where(row == j, halo[last - k + j:last - k + j + 1, :], r)
    return r


def _shift_up(x, halo, k):
    n = x.shape[0]
    r = pltpu.roll(x, n - k, 0)
    row = lax.broadcasted_iota(jnp.int32, x.shape, 0)
    for j in range(k):
        r = jnp.where(row == n - k + j, halo[j:j + 1, :], r)
    return r


def _acc_rows(ref, first, rows):
    @pl.when(first)
    def _():
        ref[...] = jnp.zeros(ref.shape, ref.dtype)
    for j, r in enumerate(rows):
        ref[j:j + 1, :] += r


def _rows(tb, c, col=0):
    return pl.BlockSpec((tb, c), lambda i, col=col: (i, col))


def _full(shape):
    nd = len(shape)
    return pl.BlockSpec(shape, lambda i, nd=nd: (0,) * nd)


def _resident(shape):
    nd = len(shape)
    return pl.BlockSpec(shape, lambda i, nd=nd: (0,) * nd, pipeline_mode=pl.Buffered(1))


def _halo_prev(tb, c, col=0, rows=SUBLANES):
    per = tb // rows
    return pl.BlockSpec((rows, c), lambda i, col=col: (jnp.maximum(i * per - 1, 0), col))


def _halo_next(tb, c, t, col=0, rows=SUBLANES):
    per = tb // rows
    last = t // rows - 1
    return pl.BlockSpec((rows, c), lambda i, col=col: (jnp.minimum((i + 1) * per, last), col))


def _mesh_pos():
    return lax.axis_index("x"), lax.axis_index("y"), lax.axis_index("c")


def _allgather8(x_pad, name):
    m_per, n = x_pad.shape

    def body(x_ref, out_ref, send_sems, recv_sems, local_sem):
        x, y, c = _mesh_pos()
        me, sibling = (x, y, c), (x, y, 1 - c)
        chips = [(1 - x, y), (x, 1 - y), (1 - x, 1 - y)]

        def rows(px, py, pc):
            return out_ref.at[pl.ds((4 * px + 2 * py + pc) * m_per, m_per), :]

        def copy(k, block, to, src=None):
            return pltpu.make_async_remote_copy(
                src_ref=rows(*block) if src is None else src, dst_ref=rows(*block),
                send_sem=send_sems.at[k], recv_sem=recv_sems.at[k], device_id=to, device_id_type=MESH)

        mine = pltpu.make_async_copy(x_ref, rows(*me), local_sem)
        mine.start()
        first = [copy(0, me, sibling, src=x_ref)]
        first += [copy(1 + j, me, (*chip, c), src=x_ref) for j, chip in enumerate(chips)]
        for cp in first:
            cp.start()
        passed = [copy(4 + j, (*chip, c), sibling) for j, chip in enumerate(chips)]
        for j, chip in enumerate(chips):
            copy(1 + j, (*chip, c), me).wait_recv()
            passed[j].start()
        copy(0, sibling, me).wait_recv()
        for j, chip in enumerate(chips):
            copy(4 + j, (*chip, 1 - c), me).wait_recv()
        for cp in first + passed:
            cp.wait_send()
        mine.wait()

    return pl.pallas_call(
        body, name=name,
        out_shape=jax.ShapeDtypeStruct((N_DEV * m_per, n), F32),
        in_specs=[pl.BlockSpec(memory_space=pltpu.VMEM)],
        out_specs=pl.BlockSpec(memory_space=pltpu.VMEM),
        scratch_shapes=[pltpu.SemaphoreType.DMA((7,)), pltpu.SemaphoreType.DMA((7,)), pltpu.SemaphoreType.DMA],
    )(x_pad)


_START_IDS = {name: k for k, name in enumerate((
    "weights_start_in", "weights_start_rest", "grads_start_ffn", "grads_start_mix", "grads_start_in", "small_start",
    "swap_start_ffn", "swap_start_small", "swap_start_mix"))}
_HBM = pl.BlockSpec(memory_space=pltpu.HBM)
_SEM = pl.BlockSpec(memory_space=pltpu.SEMAPHORE)
_EFFECT = pltpu.SideEffectType.DATAFLOW_SIDE_EFFECTING


def _chip_copy(gather, src_ref, land_ref, send, recv, j, arrival):
    x, y, c = _mesh_pos()
    peer = [(1 - x, y), (x, 1 - y), (1 - x, 1 - y)][j]
    peer_chip = 2 * peer[0] + peer[1]
    my_chip = 2 * x + y
    return pltpu.make_async_remote_copy(
        src_ref=land_ref.at[my_chip] if gather else src_ref.at[peer_chip],
        dst_ref=land_ref.at[peer_chip if arrival else my_chip],
        send_sem=send.at[j], recv_sem=recv.at[j], device_id=(*peer, c), device_id_type=MESH)


def _chips_start(name, gather, srcs, lands, after=None):
    n, ns = len(lands), len(srcs)
    extra = [] if after is None else [after]

    def body(*refs):
        src_refs, land_refs = refs[:ns], refs[ns:ns + n]
        outs = refs[ns + n + len(extra):]
        sends, recvs, token = outs[:n], outs[n:2 * n], outs[-1]
        x, y, c = _mesh_pos()
        barrier = pltpu.get_barrier_semaphore()
        for peer in [(1 - x, y), (x, 1 - y), (1 - x, 1 - y)]:
            pl.semaphore_signal(barrier, inc=1, device_id=(*peer, c), device_id_type=MESH)
        pl.semaphore_wait(barrier, N_CHIPS - 1)
        for k in range(n):
            for j in range(3):
                _chip_copy(gather, src_refs[k] if ns else None, land_refs[k], sends[k], recvs[k], j, False).start()
        token[...] = jnp.zeros(token.shape, F32)

    sem = pltpu.SemaphoreType.DMA((3,))
    thru = tuple(pltpu.HBM(a.shape, a.dtype) for a in list(srcs) + list(lands))
    res = pl.pallas_call(
        body, name=name,
        out_shape=(sem,) * (2 * n) + thru + (jax.ShapeDtypeStruct((SUBLANES, 128), F32),),
        in_specs=[_HBM] * (ns + n) + [pl.BlockSpec(memory_space=pl.ANY)] * len(extra),
        out_specs=(_SEM,) * (2 * n) + (_HBM,) * (ns + n) + (pl.BlockSpec(memory_space=pltpu.VMEM),),
        input_output_aliases={k: 2 * n + k for k in range(ns + n)},
        compiler_params=pltpu.CompilerParams(has_side_effects=_EFFECT, collective_id=_START_IDS[name]),
    )(*[pltpu.with_memory_space_constraint(a, pltpu.HBM) for a in list(srcs) + list(lands)], *extra)
    return res[:n], res[n:2 * n], res[2 * n:2 * n + ns], res[2 * n + ns:2 * n + ns + n], res[-1]


def _chips_wait(name, gather, sends, recvs, srcs, lands, after):
    n, ns = len(lands), len(srcs)

    def body(*refs):
        src_refs, land_refs = refs[:ns], refs[ns:ns + n]
        sends_, recvs_ = refs[ns + n:ns + 2 * n], refs[ns + 2 * n:ns + 3 * n]
        for k in range(n):
            for j in range(3):
                cp = _chip_copy(gather, src_refs[k] if ns else None, land_refs[k], sends_[k], recvs_[k], j, True)
                cp.wait_send()
                cp.wait_recv()

    thru = tuple(pltpu.HBM(a.shape, a.dtype) for a in list(srcs) + list(lands))
    res = pl.pallas_call(
        body, name=name, out_shape=thru,
        in_specs=[_HBM] * (ns + n) + [_SEM] * (2 * n) + [pl.BlockSpec(memory_space=pl.ANY)],
        out_specs=(_HBM,) * (ns + n),
        input_output_aliases={k: k for k in range(ns + n)},
        compiler_params=pltpu.CompilerParams(has_side_effects=_EFFECT),
    )(*srcs, *lands, *sends, *recvs, after)
    return res[:ns], res[ns:]


def _sibling_copy(src_ref, land_ref, send, recv):
    x, y, c = _mesh_pos()
    return pltpu.make_async_remote_copy(src_ref=src_ref, dst_ref=land_ref, send_sem=send.at[0], recv_sem=recv.at[0],
                                        device_id=(x, y, 1 - c), device_id_type=MESH)


def _sibling_start(name, arrs, after=None):
    n = len(arrs)
    extra = [] if after is None else [after]
    lands = [lax.empty(a.shape, a.dtype) for a in arrs]

    def body(*refs):
        src_refs, land_refs = refs[:n], refs[n:2 * n]
        outs = refs[2 * n + len(extra):]
        sends, recvs, token = outs[:n], outs[n:2 * n], outs[-1]
        x, y, c = _mesh_pos()
        barrier = pltpu.get_barrier_semaphore()
        pl.semaphore_signal(barrier, inc=1, device_id=(x, y, 1 - c), device_id_type=MESH)
        pl.semaphore_wait(barrier, 1)
        for k in range(n):
            _sibling_copy(src_refs[k], land_refs[k], sends[k], recvs[k]).start()
        token[...] = jnp.zeros(token.shape, F32)

    sem = pltpu.SemaphoreType.DMA((1,))
    thru = tuple(pltpu.HBM(a.shape, a.dtype) for a in list(arrs) + lands)
    res = pl.pallas_call(
        body, name=name,
        out_shape=(sem,) * (2 * n) + thru + (jax.ShapeDtypeStruct((SUBLANES, 128), F32),),
        in_specs=[_HBM] * (2 * n) + [pl.BlockSpec(memory_space=pl.ANY)] * len(extra),
        out_specs=(_SEM,) * (2 * n) + (_HBM,) * (2 * n) + (pl.BlockSpec(memory_space=pltpu.VMEM),),
        input_output_aliases={k: 2 * n + k for k in range(2 * n)},
        compiler_params=pltpu.CompilerParams(has_side_effects=_EFFECT, collective_id=_START_IDS[name]),
    )(*[pltpu.with_memory_space_constraint(a, pltpu.HBM) for a in list(arrs) + lands], *extra)
    return res[:n], res[n:2 * n], res[2 * n:3 * n], res[3 * n:4 * n], res[-1]


def _sibling_wait(name, sends, recvs, srcs, lands, after):
    n = len(srcs)

    def body(*refs):
        src_refs, land_refs = refs[:n], refs[n:2 * n]
        sends_, recvs_ = refs[2 * n:3 * n], refs[3 * n:4 * n]
        for k in range(n):
            cp = _sibling_copy(src_refs[k], land_refs[k], sends_[k], recvs_[k])
            cp.wait_send()
            cp.wait_recv()

    thru = tuple(pltpu.HBM(a.shape, a.dtype) for a in list(srcs) + list(lands))
    res = pl.pallas_call(
        body, name=name, out_shape=thru,
        in_specs=[_HBM] * (2 * n) + [_SEM] * (2 * n) + [pl.BlockSpec(memory_space=pl.ANY)],
        out_specs=(_HBM,) * (2 * n),
        input_output_aliases={k: k for k in range(2 * n)},
        compiler_params=pltpu.CompilerParams(has_side_effects=_EFFECT),
    )(*srcs, *lands, *sends, *recvs, after)
    return res[:n], res[n:]


def _landing(own, chip):
    zone = lax.empty((N_CHIPS,) + own.shape, own.dtype)
    return lax.dynamic_update_slice(zone, own[None], (chip,) + (0,) * own.ndim)


def _mod_shard(c_all, w_ada_sh, b_sh):
    d, n = w_ada_sh.shape
    bn = 512

    def body(c_ref, w_ref, b_ref, o_ref):
        cc = c_ref[...]
        ca = cc * _sigmoid(cc)
        o_ref[...] = _dot(ca, w_ref[...]) + b_ref[...]

    return pl.pallas_call(
        body, name="mod_shard", grid=(n // bn,),
        out_shape=jax.ShapeDtypeStruct((N_DEV, n), F32),
        in_specs=[_full((N_DEV, d)), pl.BlockSpec((d, bn), lambda j: (0, j)), pl.BlockSpec((1, bn), lambda j: (0, j))],
        out_specs=pl.BlockSpec((N_DEV, bn), lambda j: (0, j)),
        compiler_params=_cparams(("parallel",)),
    )(c_all, w_ada_sh, b_sh)


def _ssm_prep(lam_re, lam_im, log_step):
    g, p = lam_re.shape

    def body(lr_ref, li_ref, ls_ref, ar_ref, ai_ref, cr_ref, ci_ref):
        lr = jnp.minimum(lr_ref[...], LAMBDA_RE_MAX)
        li = li_ref[...]
        st = jnp.exp(ls_ref[...])
        mag = jnp.exp(lr * st)
        ar = mag * jnp.cos(li * st)
        ai = mag * jnp.sin(li * st)
        den = lr * lr + li * li
        nr = ar - 1.0
        ar_ref[...] = ar
        ai_ref[...] = ai
        cr_ref[...] = (nr * lr + ai * li) / den
        ci_ref[...] = (ai * lr - nr * li) / den

    sds = jax.ShapeDtypeStruct((g, p), F32)
    return pl.pallas_call(body, name="ssm_prep", out_shape=(sds,) * 4)(lam_re, lam_im, log_step)


def _ssm_blocks(bt_re, bt_im, ct_re, ct_im, coef_rows, tile_b, tile_c):
    gh, p = bt_re.shape
    gp, h = ct_re.shape
    nb = SSM_SPLIT
    cb, rb = gp // nb, gp // nb

    def body(btr, bti, ctr, cti, cf, tb_ref, tc_ref, bre_o, bim_o, cre_o, cim_o):
        j = pl.program_id(0)
        row = lax.broadcasted_iota(jnp.int32, (gh, cb), 0)
        col = lax.broadcasted_iota(jnp.int32, (gh, cb), 1) + j * cb
        mask = (row >> 4) == (col >> 6)
        cr, ci = cf[0:1, :], cf[1:2, :]
        br = _split3_dot(btr[...], tb_ref[...])
        bi = _split3_dot(bti[...], tb_ref[...])
        bre_o[...] = jnp.where(mask, br * cr - bi * ci, 0.0).astype(BF16)
        bim_o[...] = jnp.where(mask, br * ci + bi * cr, 0.0).astype(BF16)
        row2 = lax.broadcasted_iota(jnp.int32, (rb, gh), 0) + j * rb
        col2 = lax.broadcasted_iota(jnp.int32, (rb, gh), 1)
        mask2 = (row2 >> 6) == (col2 >> 4)
        cre_o[...] = jnp.where(mask2, _split3_dot(ctr[...], tc_ref[...]), 0.0).astype(BF16)
        cim_o[...] = jnp.where(mask2, _split3_dot(cti[...], tc_ref[...]), 0.0).astype(BF16)

    bspec = pl.BlockSpec((gh, cb), lambda j: (0, j))
    cspec = pl.BlockSpec((rb, gh), lambda j: (j, 0))
    cin = pl.BlockSpec((rb, h), lambda j: (j, 0))
    return pl.pallas_call(
        body, name="ssm_blocks", grid=(nb,),
        out_shape=(jax.ShapeDtypeStruct((gh, gp), BF16),) * 2 + (jax.ShapeDtypeStruct((gp, gh), BF16),) * 2,
        in_specs=[_full((gh, p)), _full((gh, p)), cin, cin, pl.BlockSpec((SUBLANES, cb), lambda j: (0, j)),
                  _full(tile_b.shape), _full(tile_c.shape)],
        out_specs=(bspec, bspec, cspec, cspec),
        compiler_params=_cparams(("parallel",)),
    )(bt_re, bt_im, ct_re, ct_im, coef_rows, tile_b, tile_c)


def _scan_consts(a_ref, reverse):
    w = a_ref.shape[1]
    ar1 = a_ref[0:1, :]
    ai1 = a_ref[1:2, :]
    if reverse:
        ai1 = -ai1
    pr, pi = [ar1], [ai1]
    for _ in range(1, SUBLANES):
        nr = pr[-1] * ar1 - pi[-1] * ai1
        ni = pr[-1] * ai1 + pi[-1] * ar1
        pr.append(nr)
        pi.append(ni)
    row = lax.broadcasted_iota(jnp.int32, (SUBLANES, w), 0)
    dist = (SUBLANES - 1 - row) if reverse else row

    def pick(vals):
        out = jnp.broadcast_to(vals[SUBLANES - 1], (SUBLANES, w))
        for r in range(SUBLANES - 1):
            out = jnp.where(dist == r, vals[r], out)
        return out

    p_r, p_i = pick(pr), pick(pi)
    steps = []
    for k in (1, 2, 4):
        steps.append((k, jnp.where(dist >= k, pr[k - 1], 0.0), jnp.where(dist >= k, pi[k - 1], 0.0)))
    a8 = (jnp.broadcast_to(pr[SUBLANES - 1], (SUBLANES, w)), jnp.broadcast_to(pi[SUBLANES - 1], (SUBLANES, w)))
    return row, p_r, p_i, steps, a8


def _scan_tile(xr, xi, cr, ci, consts, reverse):
    row, p_r, p_i, steps, (a8r, a8i) = consts
    for k, s_r, s_i in steps:
        sh = (SUBLANES - k) if reverse else k
        qr = pltpu.roll(xr, sh, 0)
        qi = pltpu.roll(xi, sh, 0)
        xr, xi = xr + s_r * qr - s_i * qi, xi + s_r * qi + s_i * qr
    outr = xr + p_r * cr - p_i * ci
    outi = xi + p_r * ci + p_i * cr
    e = 0 if reverse else SUBLANES - 1
    er = jnp.broadcast_to(xr[e:e + 1, :], xr.shape)
    ei = jnp.broadcast_to(xi[e:e + 1, :], xi.shape)
    return outr, outi, er + a8r * cr - a8i * ci, ei + a8r * ci + a8i * cr


def _scan_fwd(a_rows, bu_re, bu_im):
    t, n = bu_re.shape
    tb, w = _blk(t, TB_SCAN), W_SCAN
    ntile = tb // SUBLANES

    def body(a_ref, br_ref, bi_ref, sr_ref, si_ref, car, cai):
        @pl.when(pl.program_id(1) == 0)
        def _():
            car[...] = jnp.zeros(car.shape, F32)
            cai[...] = jnp.zeros(cai.shape, F32)
        consts = _scan_consts(a_ref, False)

        def pair(i, carry):
            o = pl.multiple_of(i * BF16_ROWS, BF16_ROWS)
            b_r = br_ref[pl.ds(o, BF16_ROWS), :].astype(F32)
            b_i = bi_ref[pl.ds(o, BF16_ROWS), :].astype(F32)
            outs = []
            for h in range(2):
                rows = slice(h * SUBLANES, (h + 1) * SUBLANES)
                outr, outi, ncr, nci = _scan_tile(b_r[rows, :], b_i[rows, :], carry[0], carry[1], consts, False)
                outs.append((outr, outi))
                carry = (ncr, nci)
            sr_ref[pl.ds(o, BF16_ROWS), :] = jnp.concatenate([outs[0][0], outs[1][0]], axis=0).astype(BF16)
            si_ref[pl.ds(o, BF16_ROWS), :] = jnp.concatenate([outs[0][1], outs[1][1]], axis=0).astype(BF16)
            return carry

        def pairs(i, carry):
            for s in range(SCAN_UNROLL // 2):
                carry = pair(i * (SCAN_UNROLL // 2) + s, carry)
            return carry

        cr, ci = lax.fori_loop(0, ntile // SCAN_UNROLL, pairs, (car[...], cai[...]))
        car[...] = cr
        cai[...] = ci

    spec = pl.BlockSpec((tb, w), lambda s, k: (k, s))
    sds = jax.ShapeDtypeStruct((t, n), BF16)
    return pl.pallas_call(
        body, name="scan_fwd", grid=(n // w, t // tb), out_shape=(sds, sds),
        in_specs=[pl.BlockSpec((SUBLANES, w), lambda s, k: (0, s)), spec, spec], out_specs=(spec, spec),
        scratch_shapes=[pltpu.VMEM((SUBLANES, w), F32), pltpu.VMEM((SUBLANES, w), F32)],
        compiler_params=_cparams(("parallel", "arbitrary"), VMEM_MID),
    )(a_rows, bu_re, bu_im)


def _scan_bwd(a_rows, g_re, g_im, s_re, s_im):
    t, n = g_re.shape
    tb, w = _blk(t, TB_SCAN), W_SCAN
    ntile = tb // SUBLANES
    npair = tb // BF16_ROWS
    nt = t // tb

    def body(a_ref, gr_ref, gi_ref, sr_ref, si_ref, or_ref, oi_ref, gar_ref, gai_ref, car, cai):
        @pl.when(pl.program_id(1) == 0)
        def _():
            car[...] = jnp.zeros(car.shape, F32)
            cai[...] = jnp.zeros(cai.shape, F32)
            gar_ref[...] = jnp.zeros(gar_ref.shape, F32)
            gai_ref[...] = jnp.zeros(gai_ref.shape, F32)
        consts = _scan_consts(a_ref, True)
        row = consts[0]

        def pair(i, carry):
            cr, ci, accr, acci = carry
            o = pl.multiple_of((npair - 1 - i) * BF16_ROWS, BF16_ROWS)
            s_r = sr_ref[pl.ds(o, BF16_ROWS), :].astype(F32)
            s_i = si_ref[pl.ds(o, BF16_ROWS), :].astype(F32)
            g_r = gr_ref[pl.ds(o, BF16_ROWS), :].astype(F32)
            g_i = gi_ref[pl.ds(o, BF16_ROWS), :].astype(F32)
            outs = [None, None]
            for h in (1, 0):
                rows = slice(h * SUBLANES, (h + 1) * SUBLANES)
                outr, outi, ncr, nci = _scan_tile(g_r[rows, :], g_i[rows, :], cr, ci, consts, True)
                outs[h] = (outr, outi)
                gnr = jnp.where(row == SUBLANES - 1, cr, pltpu.roll(outr, SUBLANES - 1, 0))
                gni = jnp.where(row == SUBLANES - 1, ci, pltpu.roll(outi, SUBLANES - 1, 0))
                sr = s_r[h * SUBLANES:(h + 1) * SUBLANES, :]
                si = s_i[h * SUBLANES:(h + 1) * SUBLANES, :]
                accr, acci = accr + sr * gnr + si * gni, acci + sr * gni - si * gnr
                cr, ci = ncr, nci
            or_ref[pl.ds(o, BF16_ROWS), :] = jnp.concatenate([outs[0][0], outs[1][0]], axis=0).astype(BF16)
            oi_ref[pl.ds(o, BF16_ROWS), :] = jnp.concatenate([outs[0][1], outs[1][1]], axis=0).astype(BF16)
            return cr, ci, accr, acci

        def pairs(i, carry):
            for s in range(SCAN_UNROLL // 2):
                carry = pair(i * (SCAN_UNROLL // 2) + s, carry)
            return carry

        cr, ci, accr, acci = lax.fori_loop(0, ntile // SCAN_UNROLL, pairs,
                                           (car[...], cai[...], gar_ref[...], gai_ref[...]))
        car[...] = cr
        cai[...] = ci
        gar_ref[...] = accr
        gai_ref[...] = acci

    spec = pl.BlockSpec((tb, w), lambda s, k: (nt - 1 - k, s))
    aspec = pl.BlockSpec((SUBLANES, w), lambda s, k: (0, s))
    sds = jax.ShapeDtypeStruct((t, n), BF16)
    asds = jax.ShapeDtypeStruct((SUBLANES, n), F32)
    return pl.pallas_call(
        body, name="scan_bwd", grid=(n // w, nt), out_shape=(sds, sds, asds, asds),
        in_specs=[aspec, spec, spec, spec, spec], out_specs=(spec, spec, aspec, aspec),
        scratch_shapes=[pltpu.VMEM((SUBLANES, w), F32), pltpu.VMEM((SUBLANES, w), F32)],
        compiler_params=_cparams(("parallel", "arbitrary"), VMEM_MID),
    )(a_rows, g_re, g_im, s_re, s_im)


def _mix_in(x, vec, w_in_st, b_re, b_im):
    t, d = x.shape
    ns, _, nc = w_in_st.shape
    dssm, nstate = b_re.shape
    du, ds = dssm // SSM_SPLIT, nstate // SSM_SPLIT
    tb = _blk(t, TB_MIX_FWD)

    def body(x_ref, vec_ref, w_ref, bre_ref, bim_ref, proj_ref, bur_ref, bui_ref, h1_ref):
        xv = x_ref[...]
        r = lax.rsqrt(_rowmean(xv * xv) + EPS)
        h = xv * r * vec_ref[0:1, :] * vec_ref[1:2, :] + vec_ref[2:3, :]
        hb = h.astype(BF16)
        h1_ref[...] = hb
        u = None
        for j in range(ns):
            pj = jnp.dot(hb, w_ref[j], preferred_element_type=F32)
            proj_ref[:, j * nc:(j + 1) * nc] = pj.astype(BF16)
            if j == 0:
                u = pj
        ub = u.astype(BF16)
        for q in range(SSM_SPLIT):
            rq, cq = slice(q * du, (q + 1) * du), slice(q * ds, (q + 1) * ds)
            bur_ref[:, cq] = jnp.dot(ub[:, rq], bre_ref[rq, cq], preferred_element_type=F32).astype(BF16)
            bui_ref[:, cq] = jnp.dot(ub[:, rq], bim_ref[rq, cq], preferred_element_type=F32).astype(BF16)

    return pl.pallas_call(
        body, name="mix_in", grid=(t // tb,),
        out_shape=(jax.ShapeDtypeStruct((t, ns * nc), BF16), jax.ShapeDtypeStruct((t, nstate), BF16),
                   jax.ShapeDtypeStruct((t, nstate), BF16), jax.ShapeDtypeStruct((t, d), BF16)),
        in_specs=[_rows(tb, d), _full((SUBLANES, d)), _resident(w_in_st.shape), _resident(b_re.shape),
                  _resident(b_im.shape)],
        out_specs=(_rows(tb, ns * nc), _rows(tb, nstate), _rows(tb, nstate), _rows(tb, d)),
        compiler_params=_cparams(("parallel",), VMEM_BIG),
    )(x, vec, w_in_st, b_re, b_im)


def _head_ms(y, h_ref):
    return _split_dot(y * y, h_ref[...])


def _conv3(x, halo, w_ref):
    return w_ref[0:1, :] * _shift_down(x, halo, 2) + w_ref[1:2, :] * _shift_down(x, halo, 1) + w_ref[2:3, :] * x


def _mix_out(x, proj, s_re, s_im, c_re, c_im, v512, convw, glu_w, h16, h64, w_out, vd):
    t, d = x.shape
    dh = c_re.shape[1]
    nstate = s_re.shape[1]
    du, ds = dh // SSM_SPLIT, nstate // SSM_SPLIT
    tb = _blk(t, TB_MIX_FWD)

    def body(x_ref, u_ref, bg_ref, cg_ref, v_ref, cgh_ref, vh_ref, sr_ref, si_ref, cre_ref, cim_ref, p_ref,
             cw_ref, gw_ref, h16_ref, h64_ref, wo_ref, vd_ref, y1_ref, o_ref, x2_ref):
        i = pl.program_id(0)
        u = u_ref[...].astype(F32)
        ys = []
        for q in range(SSM_SPLIT):
            rq, cq = slice(q * ds, (q + 1) * ds), slice(q * du, (q + 1) * du)
            ys.append(_dot(sr_ref[:, rq], cre_ref[rq, cq]) - _dot(si_ref[:, rq], cim_ref[rq, cq]))
        ys = jnp.concatenate(ys, axis=1)
        y1 = ys + p_ref[0:1, :] * u
        y1_ref[...] = y1
        z = _gelu(y1)
        q = _dot(z, gw_ref[...]) + p_ref[1:2, :]
        ya = z * _sigmoid(q)
        na = ya * lax.rsqrt(_head_ms(ya, h16_ref) + EPS) * p_ref[2:3, :]
        cv = cg_ref[...].astype(F32) * v_ref[...].astype(F32)
        cvh = jnp.where(i > 0, cgh_ref[...].astype(F32) * vh_ref[...].astype(F32), 0.0)
        yb = bg_ref[...].astype(F32) * _conv3(cv, cvh, cw_ref)
        nb = yb * lax.rsqrt(_head_ms(yb, h64_ref) + EPS) * p_ref[3:4, :]
        o = _dot(na, wo_ref[0:dh, :]) + _dot(nb, wo_ref[dh:2 * dh, :])
        o_ref[...] = o
        on = o * lax.rsqrt(_rowmean(o * o) + EPS) * vd_ref[0:1, :]
        x2_ref[...] = x_ref[...] + vd_ref[1:2, :] * on

    return pl.pallas_call(
        body, name="mix_out", grid=(t // tb,),
        out_shape=(jax.ShapeDtypeStruct((t, dh), F32), jax.ShapeDtypeStruct((t, d), F32),
                   jax.ShapeDtypeStruct((t, d), F32)),
        in_specs=[_rows(tb, d), _rows(tb, dh, 0), _rows(tb, dh, 1), _rows(tb, dh, 2), _rows(tb, dh, 3),
                  _halo_prev(tb, dh, 2, BF16_ROWS), _halo_prev(tb, dh, 3, BF16_ROWS), _rows(tb, nstate), _rows(tb, nstate),
                  _full(c_re.shape), _full(c_im.shape), _full(v512.shape), _full(convw.shape), _full(glu_w.shape),
                  _full(h16.shape), _full(h64.shape), _full(w_out.shape), _full(vd.shape)],
        out_specs=(_rows(tb, dh), _rows(tb, d), _rows(tb, d)),
        compiler_params=_cparams(("parallel",), VMEM_BIG),
    )(x, proj, proj, proj, proj, proj, proj, s_re, s_im, c_re, c_im, v512, convw, glu_w, h16, h64, w_out, vd)


def _ffn_up(x2, vec, w_up_st):
    t, d = x2.shape
    ns, _, nc = w_up_st.shape
    tb = _blk(t, TB_FFN_UP)

    def body(x_ref, vec_ref, w_ref, up_ref, h2_ref):
        xv = x_ref[...]
        r = lax.rsqrt(_rowmean(xv * xv) + EPS)
        h = xv * r * vec_ref[0:1, :] * vec_ref[1:2, :] + vec_ref[2:3, :]
        hb = h.astype(BF16)
        h2_ref[...] = hb
        for j in range(ns):
            up_ref[:, j * nc:(j + 1) * nc] = jnp.dot(hb, w_ref[j], preferred_element_type=F32)

    return pl.pallas_call(
        body, name="ffn_up", grid=(t // tb,),
        out_shape=(jax.ShapeDtypeStruct((t, ns * nc), F32), jax.ShapeDtypeStruct((t, d), BF16)),
        in_specs=[_rows(tb, d), _full((SUBLANES, d)), _resident(w_up_st.shape)],
        out_specs=(_rows(tb, ns * nc), _rows(tb, d)),
        compiler_params=_cparams(("parallel",), VMEM_BIG),
    )(x2, vec, w_up_st)


def _ffn_down(up, fw, w_down, w_down_t, x2, tgt, vd):
    t, nh = up.shape
    dff, d = w_down.shape
    tb = _blk(t, TB_FFN)
    inv_d = 1.0 / d

    def body(up_ref, uph_ref, fw_ref, wd_ref, wdt_ref, x2_ref, tgt_ref, vd_ref,
             act_ref, ddn_ref, dout_ref, dhid_ref, vec_ref, loss_ref, a_s, vv_s, sg_s):
        i = pl.program_id(0)

        def conv_cols(sl):
            x = up_ref[:, sl]
            halo = jnp.where(i > 0, uph_ref[:, sl], 0.0)
            return (fw_ref[0:1, sl] * _shift_down(x, halo, 2) + fw_ref[1:2, sl] * _shift_down(x, halo, 1)
                    + fw_ref[2:3, sl] * x)

        dn = None
        for o in range(0, dff, CW_FFN):
            sl = slice(o, o + CW_FFN)
            a = conv_cols(sl)
            vv = conv_cols(slice(dff + o, dff + o + CW_FFN))
            sg = _sigmoid(a)
            si = a * sg
            a_s[:, sl] = si
            vv_s[:, sl] = vv
            sg_s[:, sl] = sg
            actb = (si * vv).astype(BF16)
            act_ref[:, sl] = actb
            pj = lax.dot_general(actb, wdt_ref[:, sl], (((1,), (1,)), ((), ())), preferred_element_type=F32)
            dn = pj if dn is None else dn + pj
        r3 = lax.rsqrt(_rowmean(dn * dn) + EPS)
        xn = dn * r3
        g = vd_ref[0:1, :]
        gt2 = vd_ref[1:2, :]
        dnn = xn * g
        diff = x2_ref[...] + gt2 * dnn - tgt_ref[...]
        part = 0.5 * inv_d * jnp.sum(diff * diff)

        @pl.when(i == 0)
        def _():
            loss_ref[...] = jnp.zeros(loss_ref.shape, F32)
        loss_ref[...] += part
        dout = diff * inv_d
        dout_ref[...] = dout
        ddnn = dout * gt2
        _acc_rows(vec_ref, i == 0, [_colsum(dout * dnn), _colsum(ddnn * xn)])
        dxn = ddnn * g
        ddn = r3 * (dxn - xn * _rowmean(dxn * xn))
        ddnb = ddn.astype(BF16)
        ddn_ref[...] = ddnb
        for o in range(0, dff, CW_FFN):
            sl = slice(o, o + CW_FFN)
            dact = lax.dot_general(ddnb, wd_ref[sl, :], (((1,), (1,)), ((), ())), preferred_element_type=F32)
            si, vv, sg = a_s[:, sl], vv_s[:, sl], sg_s[:, sl]
            dhid_ref[:, sl] = (dact * vv * (sg + si * (1.0 - sg))).astype(BF16)
            dhid_ref[:, dff + o:dff + o + CW_FFN] = (dact * si).astype(BF16)

    return pl.pallas_call(
        body, name="ffn_down", grid=(t // tb,),
        scratch_shapes=[pltpu.VMEM((tb, dff), F32)] * 3,
        out_shape=(jax.ShapeDtypeStruct((t, dff), BF16), jax.ShapeDtypeStruct((t, d), BF16),
                   jax.ShapeDtypeStruct((t, d), F32), jax.ShapeDtypeStruct((t, nh), BF16),
                   jax.ShapeDtypeStruct((SUBLANES, d), F32), jax.ShapeDtypeStruct((SUBLANES, 128), F32)),
        in_specs=[_rows(tb, nh), _halo_prev(tb, nh), _full(fw.shape), _resident(w_down.shape),
                  _resident(w_down_t.shape), _rows(tb, d),
                  _rows(tb, d), _full(vd.shape)],
        out_specs=(_rows(tb, dff), _rows(tb, d), _rows(tb, d), _rows(tb, nh), _full((SUBLANES, d)),
                   _full((SUBLANES, 128))),
        compiler_params=_cparams(("arbitrary",), VMEM_BIG),
    )(up, up, fw, w_down, w_down_t, x2, tgt, vd)


def _ffn_up_bwd(dhid, up, fw, x2, dout, vec, w_up_st):
    t, nh = dhid.shape
    d = x2.shape[1]
    ns, _, nc = w_up_st.shape
    tb = _blk(t, TB_FFN)
    nblk = t // tb
    cw = 128

    def body(dh_ref, dhn_ref, up_ref, fw_ref, x2_ref, dout_ref, vec_ref, w_ref,
             dx2_ref, dup_ref, vp_ref, df_ref):
        i = pl.program_id(0)

        @pl.when(i == 0)
        def _():
            df_ref[...] = jnp.zeros(df_ref.shape, F32)
        dh2 = None
        for j in range(ns):
            for o in range(j * nc, (j + 1) * nc, cw):
                sl = slice(o, o + cw)
                dh = dh_ref[:, sl].astype(F32)
                dhn = jnp.where(i < nblk - 1, dhn_ref[:, sl].astype(F32), 0.0)
                dh1 = _shift_up(dh, dhn, 1)
                dh2s = _shift_up(dh, dhn, 2)
                dup_ref[:, sl] = (fw_ref[2:3, sl] * dh + fw_ref[1:2, sl] * dh1 + fw_ref[0:1, sl] * dh2s).astype(BF16)
                up_v = up_ref[:, sl]
                df_ref[0:1, sl] += _colsum(dh2s * up_v)
                df_ref[1:2, sl] += _colsum(dh1 * up_v)
                df_ref[2:3, sl] += _colsum(dh * up_v)
            pj = lax.dot_general(dup_ref[:, j * nc:(j + 1) * nc], w_ref[j], (((1,), (1,)), ((), ())),
                                 preferred_element_type=F32)
            dh2 = pj if dh2 is None else dh2 + pj
        xv = x2_ref[...]
        r = lax.rsqrt(_rowmean(xv * xv) + EPS)
        xn = xv * r
        g = vec_ref[0:1, :]
        hg = xn * g
        dhg = dh2 * vec_ref[1:2, :]
        _acc_rows(vp_ref, i == 0, [_colsum(dh2), _colsum(dh2 * hg), _colsum(dhg * xn)])
        dxn = dhg * g
        dx2_ref[...] = dout_ref[...] + r * (dxn - xn * _rowmean(dxn * xn))

    return pl.pallas_call(
        body, name="ffn_up_bwd", grid=(nblk,),
        out_shape=(jax.ShapeDtypeStruct((t, d), F32), jax.ShapeDtypeStruct((t, nh), BF16),
                   jax.ShapeDtypeStruct((SUBLANES, d), F32), jax.ShapeDtypeStruct((SUBLANES, nh), F32)),
        in_specs=[_rows(tb, nh), _halo_next(tb, nh, t, rows=BF16_ROWS), _rows(tb, nh), _full(fw.shape),
                  _rows(tb, d), _rows(tb, d), _full(vec.shape), _resident(w_up_st.shape)],
        out_specs=(_rows(tb, d), _rows(tb, nh), _full((SUBLANES, d)), _full((SUBLANES, nh))),
        compiler_params=_cparams(("arbitrary",), VMEM_BIG),
    )(dhid, dhid, up, fw, x2, dout, vec, w_up_st)


def _mix_out_bwd(dx2, o, y1, proj, s_re, s_im, c_re, c_im, v512, convw, glu_w, h16, h64, w_out, vd):
    t, d = dx2.shape
    dh = y1.shape[1]
    nstate = c_re.shape[0]
    du, ds = dh // SSM_SPLIT, nstate // SSM_SPLIT
    tb = _blk(t, TB_MIX)

    def body(dx2_ref, o_ref, y1_ref, u_ref, bg_ref, cg_ref, v_ref, cgh_ref, vh_ref, cre_ref, cim_ref, p_ref,
             cw_ref, gw_ref, h16_ref, h64_ref, wo_ref, vd_ref, sr_ref, si_ref,
             do_ref, ycat_ref, z_ref, dq_ref, dy1_ref, gr_ref, gi_ref, dcc_ref, dbg_ref, vpd_ref, vp5_ref,
             dcr_ref, dci_ref):
        i = pl.program_id(0)
        first = i == 0

        @pl.when(first)
        def _():
            dcr_ref[...] = jnp.zeros(dcr_ref.shape, F32)
            dci_ref[...] = jnp.zeros(dci_ref.shape, F32)
        ov = o_ref[...]
        ro = lax.rsqrt(_rowmean(ov * ov) + EPS)
        on_ = ov * ro
        g = vd_ref[0:1, :]
        dx2v = dx2_ref[...]
        don = dx2v * vd_ref[1:2, :]
        _acc_rows(vpd_ref, first, [_colsum(dx2v * on_ * g), _colsum(don * on_)])
        dxn = don * g
        dob = (ro * (dxn - on_ * _rowmean(dxn * on_))).astype(BF16)
        do_ref[...] = dob
        dyc_a =lax.dot_general(dob, wo_ref[0:dh, :], (((1,), (1,)), ((), ())), preferred_element_type=F32)
        dyc_b = lax.dot_general(dob, wo_ref[dh:2 * dh, :], (((1,), (1,)), ((), ())), preferred_element_type=F32)
        y1v = y1_ref[...]
        u = u_ref[...].astype(F32)
        z, dz_dy1 = _gelu_and_grad(y1v)
        zb = z.astype(BF16)
        sg = _sigmoid(jnp.dot(zb, gw_ref[...], preferred_element_type=F32) + p_ref[1:2, :])
        ya = z * sg
        ra = lax.rsqrt(_head_ms(ya, h16_ref) + EPS)
        yan = ya * ra
        ga = p_ref[2:3, :]
        ycat_ref[:, 0:dh] = (yan * ga).astype(BF16)
        dyn = dyc_a * ga
        dya = ra * (dyn - yan * _split_dot(dyn * yan, h16_ref[...]))
        dq = dya * z * sg * (1.0 - sg)
        dqb = dq.astype(BF16)
        z_ref[...] = zb
        dq_ref[...] = dqb
        dz = dya * sg + lax.dot_general(dqb, gw_ref[...], (((1,), (1,)), ((), ())), preferred_element_type=F32)
        dy1 = dz * dz_dy1
        dy1_ref[...] = dy1
        dy1b = dy1.astype(BF16)
        for q in range(SSM_SPLIT):
            rq, cq = slice(q * ds, (q + 1) * ds), slice(q * du, (q + 1) * du)
            gr_ref[:, rq] = lax.dot_general(dy1b[:, cq], cre_ref[rq, cq], (((1,), (1,)), ((), ())),
                                            preferred_element_type=F32).astype(BF16)
            gi_ref[:, rq] = (-lax.dot_general(dy1b[:, cq], cim_ref[rq, cq], (((1,), (1,)), ((), ())),
                                              preferred_element_type=F32)).astype(BF16)
            dcr_ref[rq, :] += _dot_tn(sr_ref[:, rq], dy1b[:, cq])
            dci_ref[rq, :] += _dot_tn(si_ref[:, rq], dy1b[:, cq])
        bg = bg_ref[...].astype(F32)
        cv = cg_ref[...].astype(F32) * v_ref[...].astype(F32)
        cvh = jnp.where(i > 0, cgh_ref[...].astype(F32) * vh_ref[...].astype(F32), 0.0)
        cv1 = _shift_down(cv, cvh, 1)
        cv2 = _shift_down(cv, cvh, 2)
        cc = cw_ref[0:1, :] * cv2 + cw_ref[1:2, :] * cv1 + cw_ref[2:3, :] * cv
        yb = bg * cc
        rb = lax.rsqrt(_head_ms(yb, h64_ref) + EPS)
        ybn = yb * rb
        gb = p_ref[3:4, :]
        ycat_ref[:, dh:2 * dh] = (ybn * gb).astype(BF16)
        dynb = dyc_b * gb
        dyb = rb * (dynb - ybn * _split_dot(dynb * ybn, h64_ref[...]))
        dcc = dyb * bg
        dbg_ref[...] = dyb * cc
        dcc_ref[...] = dcc
        _acc_rows(vp5_ref, first, [_colsum(dyc_a * yan), _colsum(dyc_b * ybn), _colsum(dq), _colsum(dy1 * u),
                                   _colsum(dcc * cv2), _colsum(dcc * cv1), _colsum(dcc * cv)])

    return pl.pallas_call(
        body, name="mix_out_bwd", grid=(t // tb,),
        out_shape=(jax.ShapeDtypeStruct((t, d), BF16), jax.ShapeDtypeStruct((t, 2 * dh), BF16),
                   jax.ShapeDtypeStruct((t, dh), BF16), jax.ShapeDtypeStruct((t, dh), BF16),
                   jax.ShapeDtypeStruct((t, dh), F32), jax.ShapeDtypeStruct((t, nstate), BF16),
                   jax.ShapeDtypeStruct((t, nstate), BF16), jax.ShapeDtypeStruct((t, dh), F32),
                   jax.ShapeDtypeStruct((t, dh), F32), jax.ShapeDtypeStruct((SUBLANES, d), F32),
                   jax.ShapeDtypeStruct((SUBLANES, dh), F32), jax.ShapeDtypeStruct((nstate, du), F32),
                   jax.ShapeDtypeStruct((nstate, du), F32)),
        in_specs=[_rows(tb, d), _rows(tb, d), _rows(tb, dh), _rows(tb, dh, 0), _rows(tb, dh, 1), _rows(tb, dh, 2),
                  _rows(tb, dh, 3), _halo_prev(tb, dh, 2, BF16_ROWS), _halo_prev(tb, dh, 3, BF16_ROWS), _resident(c_re.shape),
                  _resident(c_im.shape), _full(v512.shape), _full(convw.shape), _resident(glu_w.shape),
                  _resident(h16.shape), _resident(h64.shape), _resident(w_out.shape), _full(vd.shape),
                  _rows(tb, nstate), _rows(tb, nstate)],
        out_specs=(_rows(tb, d), _rows(tb, 2 * dh), _rows(tb, dh), _rows(tb, dh), _rows(tb, dh), _rows(tb, nstate),
                   _rows(tb, nstate), _rows(tb, dh), _rows(tb, dh), _full((SUBLANES, d)), _full((SUBLANES, dh)),
                   _full((nstate, du)), _full((nstate, du))),
        compiler_params=_cparams(("arbitrary",), VMEM_BIG),
    )(dx2, o, y1, proj, proj, proj, proj, proj, proj, c_re, c_im, v512, convw, glu_w, h16, h64, w_out, vd,
      s_re, s_im)


def _mix_in_bwd(gt_re, gt_im, b_re, b_im, dy1, dcc, dbg, proj, x, dx2, vec, v512, convw, w_in_st):
    t, d = x.shape
    dh = dy1.shape[1]
    nstate = gt_re.shape[1]
    du_w, ds = dh // SSM_SPLIT, nstate // SSM_SPLIT
    ns, _, nc = w_in_st.shape
    tb = _blk(t, TB_MIX)
    nblk = t // tb

    def body(gr_ref, gi_ref, bre_ref, bim_ref, dy1_ref, dcc_ref, dccn_ref, dbg_ref, u_ref, cg_ref, v_ref, x_ref,
             dx2_ref, vec_ref, p_ref, cw_ref, w_ref, gx_ref, dproj_ref, vp_ref, dbr_ref, dbi_ref):
        i = pl.program_id(0)

        @pl.when(i == 0)
        def _():
            dbr_ref[...] = jnp.zeros(dbr_ref.shape, F32)
            dbi_ref[...] = jnp.zeros(dbi_ref.shape, F32)
        ub = u_ref[...].astype(BF16)
        du = []
        for q in range(SSM_SPLIT):
            rq, cq = slice(q * du_w, (q + 1) * du_w), slice(q * ds, (q + 1) * ds)
            du.append(lax.dot_general(gr_ref[:, cq].astype(BF16), bre_ref[rq, cq], (((1,), (1,)), ((), ())),
                                      preferred_element_type=F32)
                      + lax.dot_general(gi_ref[:, cq].astype(BF16), bim_ref[rq, cq], (((1,), (1,)), ((), ())),
                                        preferred_element_type=F32))
            dbr_ref[rq, :] += _dot_tn(ub[:, rq], gr_ref[:, cq])
            dbi_ref[rq, :] += _dot_tn(ub[:, rq], gi_ref[:, cq])
        du = dy1_ref[...] * p_ref[0:1, :] + jnp.concatenate(du, axis=1)
        dcc = dcc_ref[...]
        dccn = jnp.where(i < nblk - 1, dccn_ref[...], 0.0)
        dcv = (cw_ref[2:3, :] * dcc + cw_ref[1:2, :] * _shift_up(dcc, dccn, 1)
               + cw_ref[0:1, :] * _shift_up(dcc, dccn, 2))
        parts = [du, dbg_ref[...], dcv * v_ref[...].astype(F32), dcv * cg_ref[...].astype(F32)]
        xv = x_ref[...]
        r = lax.rsqrt(_rowmean(xv * xv) + EPS)
        xn = xv * r
        g = vec_ref[0:1, :]
        hg = xn * g
        dh1 = None
        for j in range(ns):
            pb = parts[j].astype(BF16)
            dproj_ref[:, j * nc:(j + 1) * nc] = pb
            pj =lax.dot_general(pb, w_ref[j], (((1,), (1,)), ((), ())), preferred_element_type=F32)
            dh1 = pj if dh1 is None else dh1 + pj
        dhg = dh1 * vec_ref[1:2, :]
        _acc_rows(vp_ref, i == 0, [_colsum(dh1), _colsum(dh1 * hg), _colsum(dhg * xn)])
        dxn = dhg * g
        gx_ref[...] = dx2_ref[...] + r * (dxn - xn * _rowmean(dxn * xn))

    assert nc == dh and ns == 4
    return pl.pallas_call(
        body, name="mix_in_bwd", grid=(nblk,),
        out_shape=(jax.ShapeDtypeStruct((t, d), F32), jax.ShapeDtypeStruct((t, ns * nc), BF16),
                   jax.ShapeDtypeStruct((SUBLANES, d), F32), jax.ShapeDtypeStruct((dh, ds), F32),
                   jax.ShapeDtypeStruct((dh, ds), F32)),
        in_specs=[_rows(tb, nstate), _rows(tb, nstate), _resident(b_re.shape), _resident(b_im.shape), _rows(tb, dh),
                  _rows(tb, dh), _halo_next(tb, dh, t), _rows(tb, dh), _rows(tb, dh, 0), _rows(tb, dh, 2),
                  _rows(tb, dh, 3), _rows(tb, d), _rows(tb, d), _full(vec.shape), _full(v512.shape),
                  _full(convw.shape), _resident(w_in_st.shape)],
        out_specs=(_rows(tb, d), _rows(tb, ns * nc), _full((SUBLANES, d)), _full((dh, ds)), _full((dh, ds))),
        compiler_params=_cparams(("arbitrary",), VMEM_BIG),
    )(gt_re, gt_im, b_re, b_im, dy1, dcc, dcc, dbg, proj, proj, proj, x, dx2, vec, v512, convw, w_in_st)


def _matmul_tn(a, b, m, bn, out_dtype, name, diag=False, bt=TB_TN, after=None):
    t = a.shape[0]
    n = b.shape[1]
    bt = _blk(t, bt)
    nk = t // bt
    extra = [] if after is None else [after]
    a_map = (lambda j, k: (k, j)) if diag else (lambda j, k: (k, 0))

    def body(a_ref, b_ref, *rest):
        o_ref, acc_ref = rest[-2:]
        k = pl.program_id(1)

        @pl.when(k == 0)
        def _():
            acc_ref[...] = jnp.zeros(acc_ref.shape, F32)
        acc_ref[...] += _dot_tn(a_ref[...], b_ref[...])

        @pl.when(k == nk - 1)
        def _():
            o_ref[...] = acc_ref[...].astype(out_dtype)

    def body_one_pass(a_ref, b_ref, *rest):
        rest[-1][...] = _dot_tn(a_ref[...], b_ref[...]).astype(out_dtype)

    return pl.pallas_call(
        body_one_pass if nk == 1 else body, name=name, grid=(n // bn, nk),
        out_shape=jax.ShapeDtypeStruct((n // bn, m, bn), out_dtype),
        in_specs=[pl.BlockSpec((bt, m), a_map), pl.BlockSpec((bt, bn), lambda j, k: (k, j))]
        + [pl.BlockSpec(memory_space=pl.ANY)] * len(extra),
        out_specs=pl.BlockSpec((None, m, bn), lambda j, k: (j, 0, 0)),
        scratch_shapes=[] if nk == 1 else [pltpu.VMEM((m, bn), F32)],
        compiler_params=_cparams(("parallel", "arbitrary"), VMEM_BIG),
    )(a, b, *extra)


def _matmul_tn_rows(a, b, mb, out_dtype, name):
    t, m = a.shape
    n = b.shape[1]

    def body(a_ref, b_ref, o_ref):
        o_ref[...] = _dot_tn(a_ref[...], b_ref[...]).astype(out_dtype)

    return pl.pallas_call(
        body, name=name, grid=(m // mb,),
        out_shape=jax.ShapeDtypeStruct((1, m, n), out_dtype),
        in_specs=[pl.BlockSpec((t, mb), lambda i: (0, i)), _resident((t, n))],
        out_specs=pl.BlockSpec((None, mb, n), lambda i: (0, i, 0)),
        compiler_params=_cparams(("parallel",), VMEM_BIG),
    )(a, b)


def _ssm_bgrad(d_bre, d_bim, bt_re, bt_im, rows_in, fold, tile_b):
    gh, cb = d_bre.shape
    nb = SSM_SPLIT
    rb = gh // nb
    gp = nb * cb
    p = fold.shape[1]

    def body(dr_ref, di_ref, br_ref, bi_ref, rin_ref, f_ref, tb_ref, dbr_ref, dbi_ref, rout_ref):
        row = lax.broadcasted_iota(jnp.int32, (rb, cb), 0)
        col = lax.broadcasted_iota(jnp.int32, (rb, cb), 1)
        mask = (row >> 4) == (col >> 6)
        gr = jnp.where(mask, dr_ref[...], 0.0)
        gi = jnp.where(mask, di_ref[...], 0.0)
        cr, ci = rin_ref[0:1, :], rin_ref[1:2, :]
        dbr_ref[...] = _split3_dot(cr * gr + ci * gi, f_ref[...])
        dbi_ref[...] = _split3_dot(cr * gi - ci * gr, f_ref[...])
        br = _split3_dot(br_ref[...], tb_ref[...])
        bi = _split3_dot(bi_ref[...], tb_ref[...])
        rout_ref[...] = jnp.zeros(rout_ref.shape, F32)
        rout_ref[0:1, :] = _colsum(br * gr + bi * gi)
        rout_ref[1:2, :] = _colsum(br * gi - bi * gr)

    dspec = pl.BlockSpec((rb, cb), lambda j: (j, 0))
    rspec = pl.BlockSpec((SUBLANES, cb), lambda j: (0, j))
    ospec = pl.BlockSpec((rb, p), lambda j: (j, 0))
    return pl.pallas_call(
        body, name="ssm_bgrad", grid=(nb,),
        out_shape=(jax.ShapeDtypeStruct((gh, p), F32), jax.ShapeDtypeStruct((gh, p), F32),
                   jax.ShapeDtypeStruct((SUBLANES, gp), F32)),
        in_specs=[dspec, dspec, ospec, ospec, rspec, _full(fold.shape), _full(tile_b.shape)],
        out_specs=(ospec, ospec, rspec),
        compiler_params=_cparams(("parallel",)),
    )(d_bre, d_bim, bt_re, bt_im, rows_in, fold, tile_b)


def _ssm_cgrad(d_cre, d_cim, fold):
    gp, cb = d_cre.shape
    nb = SSM_SPLIT
    rb = gp // nb
    h = fold.shape[1]

    def body(dr_ref, di_ref, f_ref, cr_ref, ci_ref):
        row = lax.broadcasted_iota(jnp.int32, (rb, cb), 0)
        col = lax.broadcasted_iota(jnp.int32, (rb, cb), 1)
        mask = (row >> 6) == (col >> 4)
        cr_ref[...] = _split3_dot(jnp.where(mask, dr_ref[...], 0.0), f_ref[...])
        ci_ref[...] = -_split3_dot(jnp.where(mask, di_ref[...], 0.0), f_ref[...])

    cspec = pl.BlockSpec((rb, cb), lambda j: (j, 0))
    ospec = pl.BlockSpec((rb, h), lambda j: (j, 0))
    return pl.pallas_call(
        body, name="ssm_cgrad", grid=(nb,),
        out_shape=(jax.ShapeDtypeStruct((gp, h), F32),) * 2,
        in_specs=[cspec, cspec, _full(fold.shape)], out_specs=(ospec, ospec),
        compiler_params=_cparams(("parallel",)),
    )(d_cre, d_cim, fold)


def _ssm_lamgrad(lam_re, lam_im, log_step, abar_re, abar_im, coef_re, coef_im, gc_re, gc_im, ga_re, ga_im):
    g, p = lam_re.shape

    def body(lr_ref, li_ref, ls_ref, ar_ref, ai_ref, cr_ref, ci_ref, gcr_ref, gci_ref, gar_ref, gai_ref,
             dlr_ref, dli_ref, dls_ref):
        lam_raw = lr_ref[...]
        lr = jnp.minimum(lam_raw, LAMBDA_RE_MAX)
        li = li_ref[...]
        st = jnp.exp(ls_ref[...])
        den = lr * lr + li * li
        gcr, gci = gcr_ref[...], gci_ref[...]
        gab_r = gar_ref[...] + (lr * gcr - li * gci) / den
        gab_i = gai_ref[...] + (lr * gci + li * gcr) / den
        cr, ci = cr_ref[...], ci_ref[...]
        wr = -(cr * lr + ci * li) / den
        wi = -(ci * lr - cr * li) / den
        gl_r = wr * gcr + wi * gci
        gl_i = wr * gci - wi * gcr
        ar, ai = ar_ref[...], ai_ref[...]
        gw_r = ar * gab_r + ai * gab_i
        gw_i = ar * gab_i - ai * gab_r
        gl_r = gl_r + st * gw_r
        gl_i = gl_i + st * gw_i
        pass_through = jnp.where(lam_raw < LAMBDA_RE_MAX, 1.0, jnp.where(lam_raw == LAMBDA_RE_MAX, 0.5, 0.0))
        dlr_ref[...] = gl_r * pass_through
        dli_ref[...] = gl_i
        dls_ref[...] = st * jnp.sum(lr * gw_r + li * gw_i, axis=1, keepdims=True)

    sds = jax.ShapeDtypeStruct((g, p), F32)
    return pl.pallas_call(body, name="ssm_lamgrad", out_shape=(sds, sds, jax.ShapeDtypeStruct((g, 1), F32)))(
        lam_re, lam_im, log_step, abar_re, abar_im, coef_re, coef_im, gc_re, gc_im, ga_re, ga_im)


def _row_block(r, most=512):
    for rb in range(min(r, most), BF16_ROWS - 1, -1):
        if r % rb == 0 and rb % BF16_ROWS == 0:
            return rb
    return r


def _adamw_math(w, g, m, v):
    m = ADAM_B1 * m + (1.0 - ADAM_B1) * g
    v = ADAM_B2 * v + (1.0 - ADAM_B2) * (g * g)
    m_hat = m / (1.0 - ADAM_B1 ** ADAM_STEP)
    v_hat = v / (1.0 - ADAM_B2 ** ADAM_STEP)
    delta = -ADAM_LR * (m_hat / (jnp.sqrt(v_hat) + ADAM_EPS) + ADAM_WD * w)
    return delta, m, v


def _adamw_big(p_mine, p_sib, w, m, v, name):
    r, c = w.shape
    rb = _row_block(r)

    def body(a_ref, b_ref, w_ref, m_ref, v_ref, g_ref, d_ref, mo_ref, vo_ref):
        g = a_ref[...].astype(F32) + b_ref[...].astype(F32)
        g_ref[...] = g
        d_ref[...], mo_ref[...], vo_ref[...] = _adamw_math(w_ref[...], g, m_ref[...], v_ref[...])

    spec = pl.BlockSpec((rb, c), lambda i: (i, 0))
    sds = jax.ShapeDtypeStruct((r, c), F32)
    return pl.pallas_call(
        body, name=name, grid=(r // rb,), out_shape=(sds,) * 4, in_specs=[spec] * 5, out_specs=(spec,) * 4,
        compiler_params=_cparams(("parallel",), VMEM_KEEP_OPERANDS_IN_HBM),
    )(p_mine, p_sib, w, m, v)


def _sum_blocks(stack, name):
    n, r, c = stack.shape
    rb = _row_block(r)

    def body(s_ref, o_ref):
        acc = s_ref[0].astype(F32)
        for k in range(1, n):
            acc = acc + s_ref[k].astype(F32)
        o_ref[...] = acc

    return pl.pallas_call(
        body, name=name, grid=(r // rb,), out_shape=jax.ShapeDtypeStruct((r, c), F32),
        in_specs=[pl.BlockSpec((n, rb, c), lambda i: (0, i, 0))], out_specs=pl.BlockSpec((rb, c), lambda i: (i, 0)),
        compiler_params=_cparams(("parallel",), VMEM_KEEP_OPERANDS_IN_HBM),
    )(stack)


def _sum_landed(landed, own, chip, name):
    n, r, c = landed.shape
    rb = _row_block(r)

    def body(chip_ref, own_ref, l1_ref, l2_ref, l3_ref, o_ref):
        acc = own_ref[0].astype(F32)
        for ref in (l1_ref, l2_ref, l3_ref):
            acc = acc + ref[0].astype(F32)
        o_ref[...] = acc.astype(BF16)

    def slot(k):
        return pl.BlockSpec((1, rb, c), lambda i, ch: ((ch[0] + k) % n, i, 0))

    return pl.pallas_call(
        body, name=name, out_shape=jax.ShapeDtypeStruct((r, c), BF16),
        grid_spec=pltpu.PrefetchScalarGridSpec(
            num_scalar_prefetch=1, grid=(r // rb,), in_specs=[slot(0), slot(1), slot(2), slot(3)],
            out_specs=pl.BlockSpec((rb, c), lambda i, ch: (i, 0))),
        compiler_params=_cparams(("parallel",), VMEM_KEEP_OPERANDS_IN_HBM),
    )(jnp.reshape(chip, (1,)).astype(jnp.int32), own, landed, landed, landed)


def _add2(a, b):
    def body(a_ref, b_ref, o_ref):
        o_ref[...] = a_ref[...] + b_ref[...]

    return pl.pallas_call(body, name="add_small", out_shape=jax.ShapeDtypeStruct(a.shape, F32))(a, b)


def _adamw_ada(c_all, dmod_cols, w, m, v):
    d, n = w.shape
    bn = 512

    def body(c_ref, dm_ref, w_ref, m_ref, v_ref, g_ref, d_ref, mo_ref, vo_ref):
        cc = c_ref[...]
        g = _dot_tn(cc * _sigmoid(cc), dm_ref[...])
        g_ref[...] = g
        d_ref[...], mo_ref[...], vo_ref[...] = _adamw_math(w_ref[...], g, m_ref[...], v_ref[...])

    spec = pl.BlockSpec((d, bn), lambda j: (0, j))
    sds = jax.ShapeDtypeStruct((d, n), F32)
    return pl.pallas_call(
        body, name="adamw_ada", grid=(n // bn,), out_shape=(sds,) * 4,
        in_specs=[_full((N_DEV, d)), pl.BlockSpec((N_DEV, bn), lambda j: (0, j)), spec, spec, spec],
        out_specs=(spec,) * 4, compiler_params=_cparams(("parallel",), VMEM_KEEP_OPERANDS_IN_HBM),
    )(c_all, dmod_cols, w, m, v)


def _adamw_small(items):
    n = len(items)

    def body(*refs):
        ins, outs = refs[:4 * n], refs[4 * n:]
        for k in range(n):
            w_ref, g_ref, m_ref, v_ref = ins[4 * k:4 * k + 4]
            outs[3 * k][...], outs[3 * k + 1][...], outs[3 * k + 2][...] = _adamw_math(
                w_ref[...], g_ref[...], m_ref[...], v_ref[...])

    flat = [a for it in items for a in it]
    out_shape = tuple(jax.ShapeDtypeStruct(it[0].shape, F32) for it in items for _ in range(3))
    res = pl.pallas_call(body, name="adamw_small", out_shape=out_shape,
                         compiler_params=_cparams(vmem=VMEM_KEEP_OPERANDS_IN_HBM))(*flat)
    return [tuple(res[3 * k:3 * k + 3]) for k in range(n)]


def _group_mean_matrix(n, group):
    idx = np.arange(n) // group
    return (idx[:, None] == idx[None, :]).astype(np.float32) / group


def _fold_matrix(n, period):
    return (np.arange(n)[:, None] % period == np.arange(period)[None, :]).astype(np.float32)


def _rows8(*rows):
    c = rows[0].shape[-1]
    pad = jnp.zeros((SUBLANES - len(rows), c), F32)
    return jnp.concatenate([r.reshape(1, c) for r in rows] + [pad], axis=0)


def _to_rows(a, width):
    flat = a.reshape(-1)
    n = -(-flat.shape[0] // width)
    flat = jnp.pad(flat, (0, n * width - flat.shape[0]))
    return flat.reshape(n, width)


def kernel(x, c, w_ada, b_ada, g_pre_mix, g_post_mix, w_in, ssm_lam_re, ssm_lam_im, ssm_log_step, ssm_b_re, ssm_b_im, ssm_c_re, ssm_c_im, ssm_d, glu_w, glu_b, g_out_ssm, conv_w, g_out_conv, w_out, g_pre_ffn, g_post_ffn, w_up, ffn_conv_w, w_down, loss_target, m_w_ada, m_b_ada, m_g_pre_mix, m_g_post_mix, m_w_in, m_ssm_lam_re, m_ssm_lam_im, m_ssm_log_step, m_ssm_b_re, m_ssm_b_im, m_ssm_c_re, m_ssm_c_im, m_ssm_d, m_glu_w, m_glu_b, m_g_out_ssm, m_conv_w, m_g_out_conv, m_w_out, m_g_pre_ffn, m_g_post_ffn, m_w_up, m_ffn_conv_w, m_w_down, v_w_ada, v_b_ada, v_g_pre_mix, v_g_post_mix, v_w_in, v_ssm_lam_re, v_ssm_lam_im, v_ssm_log_step, v_ssm_b_re, v_ssm_b_im, v_ssm_c_re, v_ssm_c_im, v_ssm_d, v_glu_w, v_glu_b, v_g_out_ssm, v_conv_w, v_g_out_conv, v_w_out, v_g_pre_ffn, v_g_post_ffn, v_w_up, v_ffn_conv_w, v_w_down):
    xs = x[0]
    tgt = loss_target[0]
    t, d = xs.shape
    xi, yi, ci = lax.axis_index("x"), lax.axis_index("y"), lax.axis_index("c")
    chip = 2 * xi + yi
    dev = 2 * chip + ci

    n_groups, n_state = ssm_lam_re.shape[1:]
    n_gch = ssm_b_re.shape[3]
    d_ssm = n_groups * n_gch
    gp = n_groups * n_state
    n_ada = w_ada.shape[2]
    d_ff = w_down.shape[1] * N_CHIPS
    n_upc = w_up.shape[2]

    w_names = ("w_in", "glu_w", "w_out", "w_up", "w_down")
    c_gath = _allgather8(jnp.broadcast_to(c, (SUBLANES, d)), "gather_c")
    c_all = c_gath.reshape(N_DEV, SUBLANES, d)[:, 0, :]

    def pad8(a):
        return jnp.concatenate([a, jnp.zeros((SUBLANES - a.shape[0], a.shape[1]), a.dtype)], axis=0)

    def start(name, arrs, after):
        return _chips_start(name, True, [], [_landing(a, chip) for a in arrs], after)

    w_names = ("w_in", "mod", "conv_w", "ffn_conv_w", "glu_w", "w_out", "w_up", "w_down")
    first = start("weights_start_in", [w_in[0].astype(BF16)], c_gath)
    b_sh = lax.dynamic_slice(b_ada, (0, chip * n_ada), (1, n_ada))
    mod_sh = _mod_shard(c_all + first[4][0:1, 0:1], w_ada[0], b_sh)
    second = start("weights_start_rest", [mod_sh, pad8(conv_w[0]), pad8(ffn_conv_w[0])]
                   + [w[0].astype(BF16) for w in (glu_w, w_out, w_up, w_down)], None)
    w_send, w_recv, w_land = [list(first[k]) + list(second[k]) for k in (0, 1, 3)]
    w_token = second[4]

    def weights(names, after):
        ks = [w_names.index(nm) for nm in names]
        return _chips_wait("weights_wait_" + names[-1], True, [w_send[k] for k in ks], [w_recv[k] for k in ks],
                           [], [w_land[k] for k in ks], after)[1]

    lam_re, lam_im = ssm_lam_re[0], ssm_lam_im[0]
    log_step = ssm_log_step[0].reshape(n_groups, 1) + w_token[0:1, 0:1]
    abar_re, abar_im, coef_re, coef_im = _ssm_prep(lam_re, lam_im, log_step)
    a_rows = _rows8(abar_re.reshape(1, gp), abar_im.reshape(1, gp))
    coef_rows = _rows8(coef_re.reshape(1, gp), coef_im.reshape(1, gp))
    bt_re = ssm_b_re[0].transpose(0, 2, 1).reshape(d_ssm, n_state)
    bt_im = ssm_b_im[0].transpose(0, 2, 1).reshape(d_ssm, n_state)
    ct_re = ssm_c_re[0].transpose(0, 2, 1).reshape(gp, n_gch)
    ct_im = ssm_c_im[0].transpose(0, 2, 1).reshape(gp, n_gch)
    tile_b = jnp.asarray(np.tile(np.eye(n_state), (1, n_groups // SSM_SPLIT)), BF16)
    tile_c = jnp.asarray(np.tile(np.eye(n_gch), (1, n_groups)), BF16)
    bblk_re, bblk_im, cblk_re, cblk_im = _ssm_blocks(bt_re, bt_im, ct_re, ct_im, coef_rows, tile_b, tile_c)

    h16 = jnp.asarray(_group_mean_matrix(d_ssm, n_gch), BF16)
    h64 = jnp.asarray(_group_mean_matrix(d_ssm, CONV_HEAD_DIM), BF16)

    g_mod, g_cw, g_fw, w_in_st = weights(("mod", "conv_w", "ffn_conv_w", "w_in"), bblk_re)
    mod_all = g_mod.transpose(1, 0, 2).reshape(N_DEV, N_CHIPS * n_ada)
    mod = lax.dynamic_slice(mod_all, (dev, 0), (1, N_CHIPS * n_ada))
    sh1, sc1, gt1, sh2, sc2, gt2 = [mod[:, k * d:(k + 1) * d] for k in range(6)]
    convw_full = pad8(g_cw[:, :3, :].transpose(1, 0, 2).reshape(3, d_ssm))
    fw_full = pad8(g_fw[:, :3, :].transpose(1, 0, 2).reshape(3, N_CHIPS * n_upc))

    v512 = _rows8(ssm_d, glu_b, g_out_ssm, g_out_conv)
    vec1 =_rows8(g_pre_mix, 1.0 + sc1, sh1)
    vd1 = _rows8(g_post_mix, gt1)
    vec2 = _rows8(g_pre_ffn, 1.0 + sc2, sh2)
    vd2 = _rows8(g_post_ffn, gt2)

    proj, bu_re, bu_im, h1b = _mix_in(xs, vec1, w_in_st, bblk_re, bblk_im)
    s_re, s_im = _scan_fwd(a_rows, bu_re, bu_im)
    g_glu, g_wout = weights(("glu_w", "w_out"), s_re)
    glu_full = g_glu.reshape(d_ssm, d_ssm)
    w_out_full = g_wout.reshape(2 * d_ssm, d)
    y1, o_mix, x2 = _mix_out(xs, proj, s_re, s_im, cblk_re, cblk_im, v512, convw_full, glu_full, h16, h64,
                             w_out_full, vd1)
    (w_up_st,) = weights(("w_up",), x2)
    up, h2b = _ffn_up(x2, vec2, w_up_st)
    (g_wdown,) = weights(("w_down",), up)
    w_down_full = g_wdown.reshape(d_ff, d)
    actb, ddnb, dout, dhid, vp_dn, loss_blk = _ffn_down(up, fw_full, w_down_full, w_down_full.T, x2, tgt, vd2)

    gw_down = _matmul_tn_rows(actb, ddnb, 256, BF16, "dw_down").reshape(N_CHIPS, d_ff // N_CHIPS, d)
    dx2, dupb, vp_up, df_rows = _ffn_up_bwd(dhid, up, fw_full, x2, dout, vec2, w_up_st)
    gw_up = _matmul_tn(h2b, dupb, d, n_upc, BF16, "dw_up", bt=4096)
    ga_send, ga_recv, ga_src, ga_land, ga_token = _chips_start(
        "grads_start_ffn", False, [gw_down, gw_up], [lax.empty(g.shape, g.dtype) for g in (gw_down, gw_up)])
    (dob, ycatb, zb, dqb, dy1, g_re, g_im, dcc, dbg, vp_mo, vp5, d_cre, d_cim) = _mix_out_bwd(
        dx2, o_mix, y1, proj, s_re, s_im, cblk_re, cblk_im, v512, convw_full, glu_full, h16, h64, w_out_full,
        vd1 + ga_token[0:1, 0:1])
    gw_out = _matmul_tn(ycatb, dob, 2 * d_ssm, d, BF16, "dw_out", bt=4096)
    gw_out = gw_out.reshape(N_CHIPS, 2 * d_ssm // N_CHIPS, d)
    gw_glu = _matmul_tn(zb, dqb, d_ssm, d_ssm, BF16, "dw_glu", bt=4096).reshape(N_CHIPS, d_ssm // N_CHIPS, d_ssm)
    gb_send, gb_recv, gb_src, gb_land, gb_token = _chips_start(
        "grads_start_mix", False, [gw_out, gw_glu], [lax.empty(g.shape, g.dtype) for g in (gw_out, gw_glu)])
    gt_re, gt_im, ga_re8, ga_im8 = _scan_bwd(a_rows + gb_token[0:1, 0:1], g_re, g_im, s_re, s_im)
    grad_x, dprojb, vp_mi, d_bre, d_bim = _mix_in_bwd(gt_re, gt_im, bblk_re, bblk_im, dy1, dcc, dbg, proj, xs, dx2,
                                                      vec1, v512, convw_full, w_in_st)
    ssm_u, ssm_s = d_ssm // SSM_SPLIT, gp // SSM_SPLIT

    fold_b = jnp.asarray(_fold_matrix(ssm_s, n_state), BF16)
    fold_c = jnp.asarray(_fold_matrix(ssm_u, n_gch), BF16)
    db_re_f, db_im_f, gc_rows = _ssm_bgrad(d_bre, d_bim, bt_re, bt_im, coef_rows, fold_b, tile_b)
    dc_re_f, dc_im_f = _ssm_cgrad(d_cre, d_cim, fold_c)
    ga_sum = _ga_rowsum(ga_re8, ga_im8)
    g_lam_re, g_lam_im, g_log_step = _ssm_lamgrad(
        lam_re, lam_im, log_step, abar_re, abar_im, coef_re, coef_im,
        gc_rows[0].reshape(n_groups, n_state), gc_rows[1].reshape(n_groups, n_state),
        ga_sum[0].reshape(n_groups, n_state), ga_sum[1].reshape(n_groups, n_state))
    g_b_re = db_re_f.reshape(n_groups, n_gch, n_state).transpose(0, 2, 1)
    g_b_im = db_im_f.reshape(n_groups, n_gch, n_state).transpose(0, 2, 1)
    g_c_re = dc_re_f.reshape(n_groups, n_state, n_gch).transpose(0, 2, 1)
    g_c_im = dc_im_f.reshape(n_groups, n_state, n_gch).transpose(0, 2, 1)

    dmod = jnp.concatenate([vp_mi[0:1], vp_mi[1:2], vp_mo[0:1], vp_up[0:1], vp_up[1:2], vp_dn[0:1]], axis=1)
    small = [
        ("g_pre_mix", vp_mi[2:3]), ("g_post_mix", vp_mo[1:2]), ("g_pre_ffn", vp_up[2:3]), ("g_post_ffn", vp_dn[1:2]),
        ("ssm_lam_re", g_lam_re), ("ssm_lam_im", g_lam_im), ("ssm_log_step", g_log_step),
        ("ssm_b_re", g_b_re), ("ssm_b_im", g_b_im), ("ssm_c_re", g_c_re), ("ssm_c_im", g_c_im),
        ("ssm_d", vp5[3:4]), ("glu_b", vp5[2:3]), ("g_out_ssm", vp5[0:1]), ("g_out_conv", vp5[1:2]),
        ("conv_w", vp5[4:7]), ("ffn_conv_w", df_rows[0:3]), ("loss", loss_blk[0:1, 0:1]),
    ]
    packed, offsets, row = [], {}, 0
    for name, a in small:
        r = _to_rows(a, d)
        offsets[name] = (row, a.shape)
        packed.append(r)
        row += r.shape[0]
    n_small = -(-row // SUBLANES) * SUBLANES
    packed.append(jnp.zeros((n_small - row, d), F32))
    packed.append(pad8(dmod.reshape(6, d)))
    pack = jnp.concatenate(packed, axis=0)
    sm_send, sm_recv, _, sm_land, sm_token = _chips_start("small_start", True, [], [_landing(pack, chip)])

    gw_in = _matmul_tn(h1b, dprojb, d, w_in.shape[2], BF16, "dw_in", bt=4096, after=sm_token)
    gc_send, gc_recv, gc_src, gc_land, gc_token = _chips_start(
        "grads_start_in", False, [gw_in], [lax.empty(gw_in.shape, gw_in.dtype)])

    def partials(names, own, landed):
        return [_sum_landed(l, o, chip, "sum_" + nm) for l, o, nm in zip(landed, own, names)]

    def update(names, mine, theirs):
        done = {}
        for nm, pm, ps in zip(names, mine, theirs):
            w_, m_, v_ = big_params[nm]
            done[nm] = _adamw_big(pm, ps, w_[0], m_[0], v_[0], "adamw_" + nm)
        return done

    big_params = {"w_down": (w_down, m_w_down, v_w_down), "w_up": (w_up, m_w_up, v_w_up),
                  "w_out": (w_out, m_w_out, v_w_out), "glu_w": (glu_w, m_glu_w, v_glu_w),
                  "w_in": (w_in, m_w_in, v_w_in)}
    ffn_names, mix_names = ("w_down", "w_up"), ("w_out", "glu_w", "w_in")
    p_ffn = partials(ffn_names, *_chips_wait("grads_wait_ffn", False, ga_send, ga_recv, ga_src, ga_land, gc_token))
    sa_send, sa_recv, sa_src, sa_land, sa_token = _sibling_start("swap_start_ffn", p_ffn)

    (sm_landed,) = _chips_wait("small_wait", True, sm_send, sm_recv, [], sm_land, sa_token)[1]
    sm_part = _sum_blocks(sm_landed, "sum_small")
    dmod_mine = sm_landed[:, n_small:n_small + SUBLANES, :]
    ss_send, ss_recv, ss_src, ss_land, ss_token = _sibling_start("swap_start_small", [sm_part, dmod_mine])
    p_ffn, t_ffn = _sibling_wait("swap_wait_ffn", sa_send, sa_recv, sa_src, sa_land, ss_token)
    big = update(ffn_names, p_ffn, t_ffn)
    (sm_part, dmod_mine), (sm_sib, dmod_sib) = _sibling_wait("swap_wait_small", ss_send, ss_recv, ss_src, ss_land,
                                                              big["w_up"][0])
    sums = _add2(sm_part, sm_sib)
    dmod_by_core = jnp.stack([dmod_mine, dmod_sib], axis=1)
    dmod_by_core = jnp.where(ci == 0, dmod_by_core, dmod_by_core[:, ::-1])
    dmod_all = dmod_by_core[:, :, :6, :].reshape(N_DEV, 6 * d)
    g_b_ada = sums[n_small:n_small + 6].reshape(1, 6 * d)

    def unpack(name):
        r0, shape = offsets[name]
        size = math.prod(shape)
        nrow = -(-size // d)
        return sums[r0:r0 + nrow].reshape(-1)[:size].reshape(shape)

    p_mix = partials(mix_names, *_chips_wait(
        "grads_wait_mix", False, list(gb_send) + list(gc_send), list(gb_recv) + list(gc_recv),
        list(gb_src) + list(gc_src), list(gb_land) + list(gc_land), sums))
    sb_send, sb_recv, sb_src, sb_land, sb_token = _sibling_start("swap_start_mix", p_mix)

    dmod_cols = lax.dynamic_slice(dmod_all, (0, chip * n_ada), (N_DEV, n_ada)) + sb_token[0:1, 0:1]
    ada = _adamw_ada(c_all, dmod_cols, w_ada[0], m_w_ada[0], v_w_ada[0])
    p_mix, t_mix = _sibling_wait("swap_wait_mix", sb_send, sb_recv, sb_src, sb_land, ada[0])
    big.update(update(mix_names, p_mix, t_mix))

    g_small = {name: unpack(name) for name, _ in small}
    g_small["b_ada"] = g_b_ada
    g_small["conv_w"] = lax.dynamic_slice(g_small["conv_w"], (0, chip * conv_w.shape[2]), (3, conv_w.shape[2]))
    g_small["ffn_conv_w"] = lax.dynamic_slice(g_small["ffn_conv_w"], (0, chip * n_upc), (3, n_upc))
    g_small["ssm_log_step"] = g_small["ssm_log_step"].reshape(1, n_groups)
    small_params = {
        "b_ada": (b_ada, m_b_ada, v_b_ada), "g_pre_mix": (g_pre_mix, m_g_pre_mix, v_g_pre_mix),
        "g_post_mix": (g_post_mix, m_g_post_mix, v_g_post_mix), "ssm_lam_re": (ssm_lam_re, m_ssm_lam_re, v_ssm_lam_re),
        "ssm_lam_im": (ssm_lam_im, m_ssm_lam_im, v_ssm_lam_im),
        "ssm_log_step": (ssm_log_step, m_ssm_log_step, v_ssm_log_step),
        "ssm_b_re": (ssm_b_re, m_ssm_b_re, v_ssm_b_re), "ssm_b_im": (ssm_b_im, m_ssm_b_im, v_ssm_b_im),
        "ssm_c_re": (ssm_c_re, m_ssm_c_re, v_ssm_c_re), "ssm_c_im": (ssm_c_im, m_ssm_c_im, v_ssm_c_im),
        "ssm_d": (ssm_d, m_ssm_d, v_ssm_d), "glu_b": (glu_b, m_glu_b, v_glu_b),
        "g_out_ssm": (g_out_ssm, m_g_out_ssm, v_g_out_ssm), "conv_w": (conv_w, m_conv_w, v_conv_w),
        "g_out_conv": (g_out_conv, m_g_out_conv, v_g_out_conv), "g_pre_ffn": (g_pre_ffn, m_g_pre_ffn, v_g_pre_ffn),
        "g_post_ffn": (g_post_ffn, m_g_post_ffn, v_g_post_ffn),
        "ffn_conv_w": (ffn_conv_w, m_ffn_conv_w, v_ffn_conv_w),
    }

    def natural(a):
        return a[0] if a.ndim > 2 else a

    names = list(small_params)
    items = []
    for nm in names:
        w_, m_, v_ = small_params[nm]
        items.append((natural(w_), g_small[nm].reshape(natural(w_).shape), natural(m_), natural(v_)))
    upd = _adamw_small(items)
    small_out = {}
    for nm, (dl, mo, vo) in zip(names, upd):
        shp = small_params[nm][0].shape
        small_out[nm] = (g_small[nm].reshape(shp), dl.reshape(shp), mo.reshape(shp), vo.reshape(shp))

    loss = g_small["loss"][0, 0]

    order = ["w_ada", "b_ada", "g_pre_mix", "g_post_mix", "w_in", "ssm_lam_re", "ssm_lam_im", "ssm_log_step",
             "ssm_b_re", "ssm_b_im", "ssm_c_re", "ssm_c_im", "ssm_d", "glu_w", "glu_b", "g_out_ssm", "conv_w",
             "g_out_conv", "w_out", "g_pre_ffn", "g_post_ffn", "w_up", "ffn_conv_w", "w_down"]
    results = {"w_ada": tuple(a[None] for a in ada)}
    for nm in big:
        results[nm] = tuple(a[None] for a in big[nm])
    results.update(small_out)
    outs = [loss, grad_x[None]]
    for k in range(4):
        outs += [results[nm][k] for nm in order]
    return tuple(outs)


def _ga_rowsum(ga_re8, ga_im8):
    n = ga_re8.shape[1]

    def body(r_ref, i_ref, o_ref):
        o_ref[...] = jnp.zeros(o_ref.shape, F32)
        o_ref[0:1, :] = _colsum(r_ref[...])
        o_ref[1:2, :] = _colsum(i_ref[...])

    return pl.pallas_call(body, name="ga_rowsum", out_shape=jax.ShapeDtypeStruct((SUBLANES, n), F32))(ga_re8, ga_im8)
```
